```python
import math
import jax, jax.numpy as jnp
from jax import lax
import numpy as np

D_MODEL = 1024
BATCH = 8
SEQ = 4096
DEPTH = 2

PL_DIM = 256
N_EVEN = (DEPTH + 1) // 2
N_ODD = DEPTH // 2
CONV_A_WIDTH = D_MODEL
CONV_A_KERNEL = 3
GDN_HEADS = 8
GDN_HEAD_DIM = 128
GDN_WIDTH = GDN_HEADS * GDN_HEAD_DIM
GDN_CONV_KERNEL = 4
GDN_CHUNK = 64
HGRN_HEAD_DIM = 128
HGRN_WIDTH = 2 * D_MODEL
HGRN_HEADS = HGRN_WIDTH // HGRN_HEAD_DIM
HGRN_CHUNK = 32
EVEN_IN = 4 * CONV_A_WIDTH + 4 * GDN_WIDTH + 2 * GDN_HEADS
EVEN_MIX = CONV_A_WIDTH + GDN_WIDTH
ODD_IN = 4 * HGRN_WIDTH
ODD_MIX = HGRN_WIDTH
DEEPNORM_ALPHA = (2.0 * DEPTH) ** 0.25
DEEPNORM_BETA = (8.0 * DEPTH) ** -0.25
NORM_EPS = 1e-5

kernel_name = "hybrid_shortconv_gdn_hgrn2_deepnorm"


def layer_norm(x, g, b):
    xf = x.astype(jnp.float32)
    mu = jnp.mean(xf, axis=-1, keepdims=True)
    var = jnp.mean(jnp.square(xf - mu), axis=-1, keepdims=True)
    return ((xf - mu) * lax.rsqrt(var + NORM_EPS) * g.astype(jnp.float32) + b.astype(jnp.float32)).astype(x.dtype)


def rms_norm(x, g):
    xf = x.astype(jnp.float32)
    y = xf * lax.rsqrt(jnp.mean(jnp.square(xf), axis=-1, keepdims=True) + NORM_EPS)
    return (y * g.astype(jnp.float32)).astype(x.dtype)


def l2_normalize(x):
    xf = x.astype(jnp.float32)
    return (xf * lax.rsqrt(jnp.sum(jnp.square(xf), axis=-1, keepdims=True) + 1e-6)).astype(x.dtype)


def causal_depthwise_conv(x, w):
    k, c = w.shape
    return lax.conv_general_dilated(
        x, w[:, None, :].astype(x.dtype), window_strides=(1,), padding=[(k - 1, 0)],
        dimension_numbers=("NWC", "WIO", "NWC"), feature_group_count=c)


def _to_chunks(t, c):
    b, s, h, d = t.shape
    return t.reshape(b, s // c, c, h, d).transpose(0, 3, 1, 2, 4)


def _from_chunks(t):
    b, h, n, c, d = t.shape
    return t.transpose(0, 2, 3, 1, 4).reshape(b, n * c, h, d)


def gated_delta_rule(q, k, v, g, beta):
    dtype = v.dtype
    c = GDN_CHUNK
    dk, dv = q.shape[-1], v.shape[-1]
    q = _to_chunks(q.astype(jnp.float32) * (dk ** -0.5), c)
    k = _to_chunks(k.astype(jnp.float32), c)
    v = _to_chunks(v.astype(jnp.float32), c)
    g = _to_chunks(g.astype(jnp.float32)[..., None], c)[..., 0]
    beta = _to_chunks(beta.astype(jnp.float32)[..., None], c)
    gc = jnp.cumsum(g, axis=-1)
    incl = jnp.tril(jnp.ones((c, c), bool))
    strict = jnp.tril(jnp.ones((c, c), bool), -1)
    decay = jnp.exp(jnp.where(incl, gc[..., :, None] - gc[..., None, :], -jnp.inf))
    kb = k * beta
    low = jnp.where(strict, jnp.einsum("bhnid,bhnjd->bhnij", kb, k) * decay, 0.0)
    rhs = jnp.concatenate([v * beta, kb * jnp.exp(gc)[..., None]], axis=-1)
    sol = lax.linalg.triangular_solve(low + jnp.eye(c, dtype=jnp.float32), rhs,
                                      left_side=True, lower=True, unit_diagonal=True)
    u, w = sol[..., :dv], sol[..., dv:]
    attn = jnp.einsum("bhnid,bhnjd->bhnij", q, k) * decay
    q_dec = q * jnp.exp(gc)[..., None]
    g_last = gc[..., -1]
    k_dec = k * jnp.exp(g_last[..., None] - gc)[..., None]

    def step(state, xs):
        q_i, k_i, u_i, w_i, a_i, gl_i = xs
        v_new = u_i - jnp.einsum("bhcd,bhde->bhce", w_i, state)
        o_i = jnp.einsum("bhcd,bhde->bhce", q_i, state) + jnp.einsum("bhij,bhje->bhie", a_i, v_new)
        state = state * jnp.exp(gl_i)[..., None, None] + jnp.einsum("bhcd,bhce->bhde", k_i, v_new)
        return state, o_i

    mv = lambda t: jnp.moveaxis(t, 2, 0)
    s0 = jnp.zeros(q.shape[:2] + (dk, dv), jnp.float32)
    _, o = lax.scan(step, s0, (mv(q_dec), mv(k_dec), mv(u), mv(w), mv(attn), mv(g_last)))
    return _from_chunks(jnp.moveaxis(o, 0, 2)).astype(dtype)


def hgrn2_recurrence(q, k, v, logf):
    dtype = v.dtype
    c = HGRN_CHUNK
    q, k, v, logf = (jnp.moveaxis(_to_chunks(t.astype(jnp.float32), c), 2, 0) for t in (q, k, v, logf))
    b = jnp.cumsum(logf, axis=-2)
    incl = jnp.tril(jnp.ones((c, c), bool))[:, :, None]

    def step(state, xs):
        q_i, k_i, v_i, b_i = xs
        b_last = b_i[..., -1:, :]
        decay = jnp.exp(jnp.where(incl, b_i[..., :, None, :] - b_i[..., None, :, :], -jnp.inf))
        attn = jnp.einsum("bhtd,bhsd,bhtsd->bhts", q_i, k_i, decay)
        o_i = (jnp.einsum("bhtd,bhde->bhte", q_i * jnp.exp(b_i), state)
               + jnp.einsum("bhts,bhse->bhte", attn, v_i))
        state = (state * jnp.exp(b_last)[..., 0, :, None]
                 + jnp.einsum("bhsd,bhse->bhde", k_i * jnp.exp(b_last - b_i), v_i))
        return state, o_i

    s0 = jnp.zeros(q.shape[1:3] + (q.shape[-1], v.shape[-1]), jnp.float32)
    _, o = lax.scan(step, s0, (q, k, v, b))
    return _from_chunks(jnp.moveaxis(o, 0, 2)).astype(dtype)


def conv_gdn_mixer(x, w_in, conv_a_w, conv_b_w, a_log, dt_bias, gdn_norm_g, w_out):
    bsz, s, _ = x.shape
    wa, wg, h = CONV_A_WIDTH, GDN_WIDTH, GDN_HEADS
    proj = x @ w_in
    cuts = [wa, 2 * wa, 3 * wa, 4 * wa, 4 * wa + 3 * wg, 4 * wa + 4 * wg, 4 * wa + 4 * wg + h]
    h_a, c_a, b_a, z_a, qkv, z_b, beta_raw, a_raw = jnp.split(proj, cuts, axis=-1)
    y_a = b_a * causal_depthwise_conv(c_a * h_a, conv_a_w) * jax.nn.silu(z_a)
    qkv = jax.nn.silu(causal_depthwise_conv(qkv, conv_b_w))
    q, k, v = (t.reshape(bsz, s, h, GDN_HEAD_DIM) for t in jnp.split(qkv, 3, axis=-1))
    q, k = l2_normalize(q), l2_normalize(k)
    beta = jax.nn.sigmoid(beta_raw.astype(jnp.float32))
    g = -jnp.exp(a_log.astype(jnp.float32)) * jax.nn.softplus(a_raw.astype(jnp.float32) + dt_bias.astype(jnp.float32))
    o = gated_delta_rule(q, k, v, g, beta)
    o = rms_norm(o, gdn_norm_g) * jax.nn.silu(z_b.reshape(bsz, s, h, GDN_HEAD_DIM))
    y = jnp.concatenate([y_a, o.reshape(bsz, s, wg)], axis=-1)
    return y @ w_out


def hgrn2_mixer(x, w_in, lower_bound, hgrn_norm_g, w_out):
    bsz, s, _ = x.shape
    q_raw, f_raw, i_in, z = jnp.split(x @ w_in, 4, axis=-1)
    f = lower_bound + (1.0 - lower_bound) * jax.nn.sigmoid(f_raw.astype(jnp.float32))
    q = jax.nn.silu(q_raw)
    k = (1.0 - f).astype(x.dtype)
    logf = jnp.log(f)
    heads = lambda t: t.reshape(bsz, s, HGRN_HEADS, HGRN_HEAD_DIM)
    o = hgrn2_recurrence(heads(q), heads(k), heads(i_in), heads(logf))
    o = rms_norm(o, hgrn_norm_g) * jax.nn.silu(heads(z))
    return o.reshape(bsz, s, HGRN_WIDTH) @ w_out


def _fwd_setup_inputs(seed: int = 0) -> dict:
    key = jax.random.key(seed)
    ks = jax.random.split(key, 20)
    nrm = lambda k, shape, scale: jax.random.normal(k, shape, jnp.float32) * scale
    dt = jnp.exp(jax.random.uniform(ks[6], (N_EVEN, GDN_HEADS), jnp.float32)
                 * (math.log(0.1) - math.log(0.001)) + math.log(0.001))
    return {
        "x": nrm(ks[0], (BATCH, SEQ, D_MODEL), 1.0),
        "p": nrm(ks[1], (DEPTH, BATCH, SEQ, PL_DIM), 1.0),
        "w_in_even": nrm(ks[2], (N_EVEN, D_MODEL, EVEN_IN), D_MODEL ** -0.5),
        "conv_a_w": nrm(ks[3], (N_EVEN, CONV_A_KERNEL, CONV_A_WIDTH), CONV_A_KERNEL ** -0.5),
        "conv_b_w": nrm(ks[4], (N_EVEN, GDN_CONV_KERNEL, 3 * GDN_WIDTH), GDN_CONV_KERNEL ** -0.5),
        "a_log": jnp.log(jax.random.uniform(ks[5], (N_EVEN, GDN_HEADS), jnp.float32, 1.0, 16.0)),
        "dt_bias": dt + jnp.log(-jnp.expm1(-dt)),
        "gdn_norm_g": 1.0 + nrm(ks[7], (N_EVEN, GDN_HEAD_DIM), 0.02),
        "w_out_even": nrm(ks[8], (N_EVEN, EVEN_MIX, D_MODEL), EVEN_MIX ** -0.5 * DEEPNORM_BETA),
        "w_in_odd": nrm(ks[9], (N_ODD, D_MODEL, ODD_IN), D_MODEL ** -0.5),
        "lower_bounds": nrm(ks[10], (DEPTH, HGRN_WIDTH), 0.1),
        "hgrn_norm_g": 1.0 + nrm(ks[11], (N_ODD, HGRN_HEAD_DIM), 0.02),
        "w_out_odd": nrm(ks[12], (N_ODD, ODD_MIX, D_MODEL), ODD_MIX ** -0.5 * DEEPNORM_BETA),
        "ln_g": 1.0 + nrm(ks[13], (DEPTH, D_MODEL), 0.02),
        "ln_b": nrm(ks[14], (DEPTH, D_MODEL), 0.02),
        "w_pl": nrm(ks[15], (DEPTH, PL_DIM, D_MODEL), PL_DIM ** -0.5),
        "w_pl_gate": nrm(ks[16], (DEPTH, D_MODEL, D_MODEL), D_MODEL ** -0.5),
    }


def _fwd_reference(x, p, w_in_even, conv_a_w, conv_b_w, a_log, dt_bias, gdn_norm_g, w_out_even,
              w_in_odd, lower_bounds, hgrn_norm_g, w_out_odd, ln_g, ln_b, w_pl, w_pl_gate):
    lbs = jnp.cumsum(jax.nn.softmax(lower_bounds.astype(jnp.float32), axis=0), axis=0)
    lbs = lbs - lbs[0]
    for i in range(DEPTH):
        j = i // 2
        if i % 2 == 0:
            s = conv_gdn_mixer(x, w_in_even[j], conv_a_w[j], conv_b_w[j], a_log[j], dt_bias[j],
                               gdn_norm_g[j], w_out_even[j])
        else:
            s = hgrn2_mixer(x, w_in_odd[j], lbs[i], hgrn_norm_g[j], w_out_odd[j])
        x = layer_norm(DEEPNORM_ALPHA * x + s, ln_g[i], ln_b[i])
        gate = jax.nn.sigmoid((x @ w_pl_gate[i]).astype(jnp.float32))
        x = x + ((p[i] @ w_pl[i]).astype(jnp.float32) * gate).astype(x.dtype)
    return x


import jax as _jax
import jax.numpy as _jnp

TWIN_FORMAT = 'train_step'
FWD_PARAMS = ['x', 'p', 'w_in_even', 'conv_a_w', 'conv_b_w', 'a_log', 'dt_bias', 'gdn_norm_g', 'w_out_even', 'w_in_odd', 'lower_bounds', 'hgrn_norm_g', 'w_out_odd', 'ln_g', 'ln_b', 'w_pl', 'w_pl_gate']
TWIN_WEIGHTS = ['w_in_even', 'conv_a_w', 'conv_b_w', 'a_log', 'dt_bias', 'gdn_norm_g', 'w_out_even', 'w_in_odd', 'lower_bounds', 'hgrn_norm_g', 'w_out_odd', 'ln_g', 'ln_b', 'w_pl', 'w_pl_gate']
TWIN_DIFF_INPUT = 'x'
TWIN_INPUTS = ['x', 'p', 'w_in_even', 'conv_a_w', 'conv_b_w', 'a_log', 'dt_bias', 'gdn_norm_g', 'w_out_even', 'w_in_odd', 'lower_bounds', 'hgrn_norm_g', 'w_out_odd', 'ln_g', 'ln_b', 'w_pl', 'w_pl_gate', 'loss_target', 'm_w_in_even', 'm_conv_a_w', 'm_conv_b_w', 'm_a_log', 'm_dt_bias', 'm_gdn_norm_g', 'm_w_out_even', 'm_w_in_odd', 'm_lower_bounds', 'm_hgrn_norm_g', 'm_w_out_odd', 'm_ln_g', 'm_ln_b', 'm_w_pl', 'm_w_pl_gate', 'v_w_in_even', 'v_conv_a_w', 'v_conv_b_w', 'v_a_log', 'v_dt_bias', 'v_gdn_norm_g', 'v_w_out_even', 'v_w_in_odd', 'v_lower_bounds', 'v_hgrn_norm_g', 'v_w_out_odd', 'v_ln_g', 'v_ln_b', 'v_w_pl', 'v_w_pl_gate']
TWIN_OUTPUTS = ['loss', 'grad_x', 'grad_w_in_even', 'grad_conv_a_w', 'grad_conv_b_w', 'grad_a_log', 'grad_dt_bias', 'grad_gdn_norm_g', 'grad_w_out_even', 'grad_w_in_odd', 'grad_lower_bounds', 'grad_hgrn_norm_g', 'grad_w_out_odd', 'grad_ln_g', 'grad_ln_b', 'grad_w_pl', 'grad_w_pl_gate', 'delta_w_in_even', 'delta_conv_a_w', 'delta_conv_b_w', 'delta_a_log', 'delta_dt_bias', 'delta_gdn_norm_g', 'delta_w_out_even', 'delta_w_in_odd', 'delta_lower_bounds', 'delta_hgrn_norm_g', 'delta_w_out_odd', 'delta_ln_g', 'delta_ln_b', 'delta_w_pl', 'delta_w_pl_gate', 'new_m_w_in_even', 'new_m_conv_a_w', 'new_m_conv_b_w', 'new_m_a_log', 'new_m_dt_bias', 'new_m_gdn_norm_g', 'new_m_w_out_even', 'new_m_w_in_odd', 'new_m_lower_bounds', 'new_m_hgrn_norm_g', 'new_m_w_out_odd', 'new_m_ln_g', 'new_m_ln_b', 'new_m_w_pl', 'new_m_w_pl_gate', 'new_v_w_in_even', 'new_v_conv_a_w', 'new_v_conv_b_w', 'new_v_a_log', 'new_v_dt_bias', 'new_v_gdn_norm_g', 'new_v_w_out_even', 'new_v_w_in_odd', 'new_v_lower_bounds', 'new_v_hgrn_norm_g', 'new_v_w_out_odd', 'new_v_ln_g', 'new_v_ln_b', 'new_v_w_pl', 'new_v_w_pl_gate']
TWIN_LEAF_KINDS = {'loss': 'loss', 'grad_x': 'grad_x', 'grad_w_in_even': 'grad_w', 'grad_conv_a_w': 'grad_w', 'grad_conv_b_w': 'grad_w', 'grad_a_log': 'grad_w', 'grad_dt_bias': 'grad_w', 'grad_gdn_norm_g': 'grad_w', 'grad_w_out_even': 'grad_w', 'grad_w_in_odd': 'grad_w', 'grad_lower_bounds': 'grad_w', 'grad_hgrn_norm_g': 'grad_w', 'grad_w_out_odd': 'grad_w', 'grad_ln_g': 'grad_w', 'grad_ln_b': 'grad_w', 'grad_w_pl': 'grad_w', 'grad_w_pl_gate': 'grad_w', 'delta_w_in_even': 'delta_w', 'delta_conv_a_w': 'delta_w', 'delta_conv_b_w': 'delta_w', 'delta_a_log': 'delta_w', 'delta_dt_bias': 'delta_w', 'delta_gdn_norm_g': 'delta_w', 'delta_w_out_even': 'delta_w', 'delta_w_in_odd': 'delta_w', 'delta_lower_bounds': 'delta_w', 'delta_hgrn_norm_g': 'delta_w', 'delta_w_out_odd': 'delta_w', 'delta_ln_g': 'delta_w', 'delta_ln_b': 'delta_w', 'delta_w_pl': 'delta_w', 'delta_w_pl_gate': 'delta_w', 'new_m_w_in_even': 'new_m', 'new_m_conv_a_w': 'new_m', 'new_m_conv_b_w': 'new_m', 'new_m_a_log': 'new_m', 'new_m_dt_bias': 'new_m', 'new_m_gdn_norm_g': 'new_m', 'new_m_w_out_even': 'new_m', 'new_m_w_in_odd': 'new_m', 'new_m_lower_bounds': 'new_m', 'new_m_hgrn_norm_g': 'new_m', 'new_m_w_out_odd': 'new_m', 'new_m_ln_g': 'new_m', 'new_m_ln_b': 'new_m', 'new_m_w_pl': 'new_m', 'new_m_w_pl_gate': 'new_m', 'new_v_w_in_even': 'new_v', 'new_v_conv_a_w': 'new_v', 'new_v_conv_b_w': 'new_v', 'new_v_a_log': 'new_v', 'new_v_dt_bias': 'new_v', 'new_v_gdn_norm_g': 'new_v', 'new_v_w_out_even': 'new_v', 'new_v_w_in_odd': 'new_v', 'new_v_lower_bounds': 'new_v', 'new_v_hgrn_norm_g': 'new_v', 'new_v_w_out_odd': 'new_v', 'new_v_ln_g': 'new_v', 'new_v_ln_b': 'new_v', 'new_v_w_pl': 'new_v', 'new_v_w_pl_gate': 'new_v'}


def _forward(args):
    return _fwd_reference(*[args[k] for k in FWD_PARAMS])


def _output_shape():
    out = _jax.eval_shape(lambda: _forward(_fwd_setup_inputs(0)))
    return out.shape, out.dtype

N_MICROBATCH = 1
ADAM_LR = 0.001
ADAM_B1 = 0.9
ADAM_B2 = 0.999
ADAM_EPS = 1e-08
ADAM_WD = 0.01
ADAM_STEP = 10
PER_EXAMPLE_BATCH_AXIS = {'x': 0, 'p': 1, 'loss_target': 0}
SHARED_INPUTS = []
_WEIGHT_DTYPES = {'w_in_even': _jnp.float32, 'conv_a_w': _jnp.float32, 'conv_b_w': _jnp.float32, 'a_log': _jnp.float32, 'dt_bias': _jnp.float32, 'gdn_norm_g': _jnp.float32, 'w_out_even': _jnp.float32, 'w_in_odd': _jnp.float32, 'lower_bounds': _jnp.float32, 'hgrn_norm_g': _jnp.float32, 'w_out_odd': _jnp.float32, 'ln_g': _jnp.float32, 'ln_b': _jnp.float32, 'w_pl': _jnp.float32, 'w_pl_gate': _jnp.float32}
MOMENT_SCALE = {'w_in_even': 2.485256e-02, 'conv_a_w': 2.892229e-02, 'conv_b_w': 2.440351e-02, 'a_log': 2.061323e-01, 'dt_bias': 2.016308e-01, 'gdn_norm_g': 1.598400e-01, 'w_out_even': 1.162345e-01, 'w_in_odd': 2.344863e-02, 'lower_bounds': 2.798635e-03, 'hgrn_norm_g': 1.393013e-01, 'w_out_odd': 9.242606e-02, 'ln_g': 2.334437e+01, 'ln_b': 2.986443e+00, 'w_pl': 2.957229e-01, 'w_pl_gate': 7.171813e-02}


def _to_microbatches(a, axis):
    t = _jnp.moveaxis(a, axis, 0)
    t = t.reshape((N_MICROBATCH, t.shape[0] // N_MICROBATCH) + t.shape[1:])
    return _jnp.moveaxis(t, 1, axis + 1)


def setup_inputs(seed: int = 0) -> dict:
    inp = _fwd_setup_inputs(seed)
    key = _jax.random.fold_in(_jax.random.key(seed), 7919)
    shape, _ = _output_shape()
    out = dict(inp)
    out["loss_target"] = _jax.random.normal(_jax.random.fold_in(key, 0), shape, _jnp.float32)
    for i, name in enumerate(TWIN_WEIGHTS):
        w = inp[name].astype(_jnp.float32)
        if MOMENT_SCALE is None:
            s = _jnp.sqrt(_jnp.mean(_jnp.square(w)) + 1e-30)
        else:
            s = MOMENT_SCALE[name]
        km, kv = _jax.random.split(_jax.random.fold_in(key, i + 1))
        out[name] = w
        out["m_" + name] = s * _jax.random.normal(km, w.shape, _jnp.float32)
        out["v_" + name] = (s * s) * _jax.random.uniform(kv, w.shape, _jnp.float32, 0.5, 1.5)
    if N_MICROBATCH > 1:
        for name, axis in PER_EXAMPLE_BATCH_AXIS.items():
            out[name] = _to_microbatches(out[name], axis)
    return {'x': out['x'], 'p': out['p'], 'w_in_even': out['w_in_even'], 'conv_a_w': out['conv_a_w'], 'conv_b_w': out['conv_b_w'], 'a_log': out['a_log'], 'dt_bias': out['dt_bias'], 'gdn_norm_g': out['gdn_norm_g'], 'w_out_even': out['w_out_even'], 'w_in_odd': out['w_in_odd'], 'lower_bounds': out['lower_bounds'], 'hgrn_norm_g': out['hgrn_norm_g'], 'w_out_odd': out['w_out_odd'], 'ln_g': out['ln_g'], 'ln_b': out['ln_b'], 'w_pl': out['w_pl'], 'w_pl_gate': out['w_pl_gate'], 'loss_target': out['loss_target'], 'm_w_in_even': out['m_w_in_even'], 'm_conv_a_w': out['m_conv_a_w'], 'm_conv_b_w': out['m_conv_b_w'], 'm_a_log': out['m_a_log'], 'm_dt_bias': out['m_dt_bias'], 'm_gdn_norm_g': out['m_gdn_norm_g'], 'm_w_out_even': out['m_w_out_even'], 'm_w_in_odd': out['m_w_in_odd'], 'm_lower_bounds': out['m_lower_bounds'], 'm_hgrn_norm_g': out['m_hgrn_norm_g'], 'm_w_out_odd': out['m_w_out_odd'], 'm_ln_g': out['m_ln_g'], 'm_ln_b': out['m_ln_b'], 'm_w_pl': out['m_w_pl'], 'm_w_pl_gate': out['m_w_pl_gate'], 'v_w_in_even': out['v_w_in_even'], 'v_conv_a_w': out['v_conv_a_w'], 'v_conv_b_w': out['v_conv_b_w'], 'v_a_log': out['v_a_log'], 'v_dt_bias': out['v_dt_bias'], 'v_gdn_norm_g': out['v_gdn_norm_g'], 'v_w_out_even': out['v_w_out_even'], 'v_w_in_odd': out['v_w_in_odd'], 'v_lower_bounds': out['v_lower_bounds'], 'v_hgrn_norm_g': out['v_hgrn_norm_g'], 'v_w_out_odd': out['v_w_out_odd'], 'v_ln_g': out['v_ln_g'], 'v_ln_b': out['v_ln_b'], 'v_w_pl': out['v_w_pl'], 'v_w_pl_gate': out['v_w_pl_gate']}


def _loss(weights, diff, rest, loss_target):
    with _jax.named_scope("forward"):
        args = {**rest, TWIN_DIFF_INPUT: diff, **{k: w.astype(_WEIGHT_DTYPES[k]) for k, w in weights.items()}}
        y = _forward(args)
    with _jax.named_scope("loss_head"):
        err = _jnp.square(y.astype(_jnp.float32) - loss_target)
        return 0.5 * _jnp.sum(_jnp.mean(err, axis=-1)) if err.ndim else 0.5 * err


def _adamw(w, g, m, v):
    m = ADAM_B1 * m + (1.0 - ADAM_B1) * g
    v = ADAM_B2 * v + (1.0 - ADAM_B2) * _jnp.square(g)
    m_hat = m / (1.0 - ADAM_B1 ** ADAM_STEP)
    v_hat = v / (1.0 - ADAM_B2 ** ADAM_STEP)
    delta = -ADAM_LR * (m_hat / (_jnp.sqrt(v_hat) + ADAM_EPS) + ADAM_WD * w)
    return delta, m, v


def reference(x, p, w_in_even, conv_a_w, conv_b_w, a_log, dt_bias, gdn_norm_g, w_out_even, w_in_odd, lower_bounds, hgrn_norm_g, w_out_odd, ln_g, ln_b, w_pl, w_pl_gate, loss_target, m_w_in_even, m_conv_a_w, m_conv_b_w, m_a_log, m_dt_bias, m_gdn_norm_g, m_w_out_even, m_w_in_odd, m_lower_bounds, m_hgrn_norm_g, m_w_out_odd, m_ln_g, m_ln_b, m_w_pl, m_w_pl_gate, v_w_in_even, v_conv_a_w, v_conv_b_w, v_a_log, v_dt_bias, v_gdn_norm_g, v_w_out_even, v_w_in_odd, v_lower_bounds, v_hgrn_norm_g, v_w_out_odd, v_ln_g, v_ln_b, v_w_pl, v_w_pl_gate):
    given = dict(x=x, p=p, w_in_even=w_in_even, conv_a_w=conv_a_w, conv_b_w=conv_b_w, a_log=a_log, dt_bias=dt_bias, gdn_norm_g=gdn_norm_g, w_out_even=w_out_even, w_in_odd=w_in_odd, lower_bounds=lower_bounds, hgrn_norm_g=hgrn_norm_g, w_out_odd=w_out_odd, ln_g=ln_g, ln_b=ln_b, w_pl=w_pl, w_pl_gate=w_pl_gate, loss_target=loss_target, m_w_in_even=m_w_in_even, m_conv_a_w=m_conv_a_w, m_conv_b_w=m_conv_b_w, m_a_log=m_a_log, m_dt_bias=m_dt_bias, m_gdn_norm_g=m_gdn_norm_g, m_w_out_even=m_w_out_even, m_w_in_odd=m_w_in_odd, m_lower_bounds=m_lower_bounds, m_hgrn_norm_g=m_hgrn_norm_g, m_w_out_odd=m_w_out_odd, m_ln_g=m_ln_g, m_ln_b=m_ln_b, m_w_pl=m_w_pl, m_w_pl_gate=m_w_pl_gate, v_w_in_even=v_w_in_even, v_conv_a_w=v_conv_a_w, v_conv_b_w=v_conv_b_w, v_a_log=v_a_log, v_dt_bias=v_dt_bias, v_gdn_norm_g=v_gdn_norm_g, v_w_out_even=v_w_out_even, v_w_in_odd=v_w_in_odd, v_lower_bounds=v_lower_bounds, v_hgrn_norm_g=v_hgrn_norm_g, v_w_out_odd=v_w_out_odd, v_ln_g=v_ln_g, v_ln_b=v_ln_b, v_w_pl=v_w_pl, v_w_pl_gate=v_w_pl_gate)
    weights = {n: given[n] for n in TWIN_WEIGHTS}
    shared = {n: given[n] for n in SHARED_INPUTS}
    per_example = {n: given[n] for n in ['x', 'p']}
    grad_fn = _jax.value_and_grad(_loss, argnums=(0, 1))

    def one_microbatch(ex, loss_target):
        ex = dict(ex)
        diff = ex.pop(TWIN_DIFF_INPUT)
        return grad_fn(weights, diff, {**shared, **ex}, loss_target)

    if N_MICROBATCH == 1:
        loss, (grad_w, grad_x) = one_microbatch(per_example, given["loss_target"])
    else:
        def body(carry, xs):
            loss_sum, grad_sum = carry
            l_k, (gw_k, gx_k) = one_microbatch(xs[0], xs[1])
            with _jax.named_scope("update"):
                return (loss_sum + l_k, _jax.tree.map(_jnp.add, grad_sum, gw_k)), gx_k

        init = (_jnp.zeros((), _jnp.float32), _jax.tree.map(_jnp.zeros_like, weights))
        (loss, grad_w), grad_x = _jax.lax.scan(body, init, (per_example, given["loss_target"]))
    with _jax.named_scope("update"):
        delta_w, new_m, new_v = {}, {}, {}
        for n in TWIN_WEIGHTS:
            delta_w[n], new_m[n], new_v[n] = _adamw(weights[n], grad_w[n], given["m_" + n], given["v_" + n])
    return (loss, grad_x, *[grad_w[n] for n in TWIN_WEIGHTS], *[delta_w[n] for n in TWIN_WEIGHTS],
            *[new_m[n] for n in TWIN_WEIGHTS], *[new_v[n] for n in TWIN_WEIGHTS])
```

```python
import functools

import jax
import jax.numpy as jnp
from jax import lax
from jax.experimental import pallas as pl
from jax.experimental.pallas import tpu as pltpu

F32 = jnp.float32
BF16 = jnp.bfloat16
HI = lax.Precision.HIGHEST

D_MODEL = 1024
PL_DIM = 256
GDN_HEADS = 8
HEAD_DIM = 128
GDN_CHUNK = 64
HGRN_HEADS = 16
HGRN_CHUNK = 32
HGRN_WIDTH = 2048
DEEPNORM_ALPHA = 4.0 ** 0.25
NORM_EPS = 1e-5
ADAM_LR, ADAM_B1, ADAM_B2, ADAM_EPS, ADAM_WD, ADAM_STEP = 0.001, 0.9, 0.999, 1e-08, 0.01, 10

VMEM_LIMIT = 56 * 1024 * 1024
SUBLANES = 8
LANES = 128

_DIMS = {"nn": (((1,), (0,)), ((), ())), "nt": (((1,), (1,)), ((), ())), "tn": (((0,), (0,)), ((), ()))}


def _params(*sem):
    return pltpu.CompilerParams(dimension_semantics=sem, vmem_limit_bytes=VMEM_LIMIT)


def _mm_raw(a, b, kind, hi):
    if hi:
        return lax.dot_general(a, b, _DIMS[kind], precision=HI, preferred_element_type=F32)
    return lax.dot_general(a.astype(BF16), b.astype(BF16), _DIMS[kind], preferred_element_type=F32)


@functools.partial(jax.custom_vjp, nondiff_argnums=(2, 3))
def _mm_vjp(a, b, kind, hi):
    return _mm_raw(a, b, kind, hi)


def _mm_vjp_fwd(a, b, kind, hi):
    return _mm_raw(a, b, kind, hi), (a, b)


def _mm_vjp_bwd(kind, hi, res, dc):
    a, b = res
    if kind == "nn":
        return _mm_raw(dc, b, "nt", hi), _mm_raw(a, dc, "tn", hi)
    if kind == "nt":
        return _mm_raw(dc, b, "nn", hi), _mm_raw(dc, a, "tn", hi)
    return _mm_raw(b, dc, "nt", hi), _mm_raw(a, dc, "nn", hi)


_mm_vjp.defvjp(_mm_vjp_fwd, _mm_vjp_bwd)


def _lane_total(v):
    return jnp.broadcast_to(jnp.sum(v, axis=-1, keepdims=True), v.shape)


def _matmul(a, b, *, name, ta=False, tb=False, out_dtype=F32, add=None, add_scale=1.0, tm=512, tn=1024, tk=1024):
    m, k = (a.shape[1], a.shape[0]) if ta else a.shape
    n = b.shape[0] if tb else b.shape[1]
    tm, tn, tk = min(tm, m), min(tn, n), min(tk, k)
    assert m % tm == 0 and n % tn == 0 and k % tk == 0, (name, m, n, k)
    nk = k // tk
    kind = ("t" if ta else "n") + ("t" if tb else "n")
    dims = (((0 if ta else 1,), (1 if tb else 0,)), ((), ()))

    def body(*refs):
        if add is None:
            a_ref, b_ref, o_ref, acc_ref = refs
        else:
            a_ref, b_ref, add_ref, o_ref, acc_ref = refs
        kk = pl.program_id(2)

        @pl.when(kk == 0)
        def _():
            acc_ref[...] = jnp.zeros_like(acc_ref)

        acc_ref[...] += lax.dot_general(a_ref[...].astype(BF16), b_ref[...].astype(BF16), dims,
                                        preferred_element_type=F32)

        @pl.when(kk == nk - 1)
        def _():
            r = acc_ref[...]
            if add is not None:
                r = r + add_scale * add_ref[...].astype(F32)
            o_ref[...] = r.astype(out_dtype)

    a_spec = pl.BlockSpec((tk, tm), lambda i, j, kk: (kk, i)) if ta else pl.BlockSpec((tm, tk), lambda i, j, kk: (i, kk))
    b_spec = pl.BlockSpec((tn, tk), lambda i, j, kk: (j, kk)) if tb else pl.BlockSpec((tk, tn), lambda i, j, kk: (kk, j))
    o_spec = pl.BlockSpec((tm, tn), lambda i, j, kk: (i, j))
    in_specs = [a_spec, b_spec] + ([o_spec] if add is not None else [])
    args = (a, b) + ((add,) if add is not None else ())
    del kind
    return pl.pallas_call(
        body, name=name, grid=(m // tm, n // tn, nk), in_specs=in_specs, out_specs=o_spec,
        out_shape=jax.ShapeDtypeStruct((m, n), out_dtype), scratch_shapes=[pltpu.VMEM((tm, tn), F32)],
        compiler_params=_params("parallel", "parallel", "arbitrary"))(*args)


HALO = SUBLANES


def _halo_specs(tt, width, col, nt):
    r = tt // HALO
    prev = pl.BlockSpec((HALO, width), lambda i: (jnp.maximum(i * r - 1, 0), col))
    nxt = pl.BlockSpec((HALO, width), lambda i: (jnp.minimum((i + 1) * r, nt * r - 1), col))
    return prev, nxt


def _shift_down(ext, k):
    return ext if k == 0 else pltpu.roll(ext, k, 0)


def _shift_up(ext, k):
    return ext if k == 0 else pltpu.roll(ext, ext.shape[0] - k, 0)


def _causal_conv(ext, w, taps):
    acc = None
    for j in range(taps):
        term = w[j:j + 1, :] * _shift_down(ext, taps - 1 - j)
        acc = term if acc is None else acc + term
    return acc[HALO:, :]


def _conv_a_fwd(proj_a, conv_w):
    t = proj_a.shape[0]
    tt = min(t, 256)
    nt = t // tt
    wdt = 1024

    def body(cur_ref, prev_ref, w_ref, y_ref):
        i = pl.program_id(0)
        cur = cur_ref[...]
        h, c, b, z = (cur[:, k * wdt:(k + 1) * wdt] for k in range(4))
        prev = prev_ref[...]
        u_prev = jnp.where(i > 0, prev[:, wdt:2 * wdt] * prev[:, 0:wdt], 0.0)
        ext = jnp.concatenate([u_prev, c * h], axis=0)
        conv = _causal_conv(ext, w_ref[...], 3)
        y_ref[...] = (b * conv * jax.nn.silu(z)).astype(BF16)

    prev_spec, _ = _halo_specs(tt, 4 * wdt, 0, nt)
    return pl.pallas_call(
        body, name="conv_a_fwd", grid=(nt,),
        in_specs=[pl.BlockSpec((tt, 4 * wdt), lambda i: (i, 0)), prev_spec, pl.BlockSpec((3, wdt), lambda i: (0, 0))],
        out_specs=pl.BlockSpec((tt, wdt), lambda i: (i, 0)),
        out_shape=jax.ShapeDtypeStruct((t, wdt), BF16), compiler_params=_params("parallel"))(proj_a, proj_a, conv_w)


def _conv_a_bwd(proj_a, conv_w, dy):
    t = proj_a.shape[0]
    tt = min(t, 256)
    nt = t // tt
    wdt = 1024

    def body(cur_ref, prev_ref, nxt_ref, w_ref, dy_ref, dyn_ref, d_ref, dw_ref):
        i = pl.program_id(0)
        w = w_ref[...]
        cur, prev, nxt = cur_ref[...], prev_ref[...], nxt_ref[...]
        split = lambda a: tuple(a[:, k * wdt:(k + 1) * wdt] for k in range(4))
        h, c, b, z = split(cur)
        hp, cp, _, _ = split(prev)
        hn, cn, bn, zn = split(nxt)
        u_prev = jnp.where(i > 0, cp * hp, 0.0)
        u_ext = jnp.concatenate([u_prev, c * h, cn * hn], axis=0)
        taps = [_shift_down(u_ext, 2 - j)[HALO:, :] for j in range(3)]
        conv = w[0:1, :] * taps[0] + w[1:2, :] * taps[1] + w[2:3, :] * taps[2]
        b_cn = jnp.concatenate([b, bn], axis=0)
        z_cn = jnp.concatenate([z, zn], axis=0)
        dy_cn = jnp.concatenate([dy_ref[...], jnp.where(i < nt - 1, dyn_ref[...], 0.0)], axis=0)
        sg = jax.nn.sigmoid(z_cn)
        silu = z_cn * sg
        d_conv = dy_cn * b_cn * silu
        db = (dy_cn * conv * silu)[:tt, :]
        dz = (dy_cn * b_cn * conv * (sg * (1.0 + z_cn * (1.0 - sg))))[:tt, :]
        du = None
        for j in range(3):
            term = w[j:j + 1, :] * _shift_up(d_conv, 2 - j)
            du = term if du is None else du + term
        du = du[:tt, :]
        d_ref[...] = jnp.concatenate([du * c, du * h, db, dz], axis=1).astype(BF16)

        @pl.when(i == 0)
        def _():
            dw_ref[...] = jnp.zeros_like(dw_ref)

        d_cur = d_conv[:tt, :]
        rows = [jnp.sum(d_cur * taps[j][:tt, :], axis=0, keepdims=True) for j in range(3)]
        dw_ref[0:3, :] += jnp.concatenate(rows, axis=0)

    prev_spec, nxt_spec = _halo_specs(tt, 4 * wdt, 0, nt)
    _, dyn_spec = _halo_specs(tt, wdt, 0, nt)
    return pl.pallas_call(
        body, name="conv_a_bwd", grid=(nt,),
        in_specs=[pl.BlockSpec((tt, 4 * wdt), lambda i: (i, 0)), prev_spec, nxt_spec,
                  pl.BlockSpec((3, wdt), lambda i: (0, 0)), pl.BlockSpec((tt, wdt), lambda i: (i, 0)), dyn_spec],
        out_specs=[pl.BlockSpec((tt, 4 * wdt), lambda i: (i, 0)), pl.BlockSpec((SUBLANES, wdt), lambda i: (0, 0))],
        out_shape=[jax.ShapeDtypeStruct((t, 4 * wdt), BF16), jax.ShapeDtypeStruct((SUBLANES, wdt), F32)],
        compiler_params=_params("arbitrary"))(proj_a, proj_a, proj_a, conv_w, dy, dy)


def _conv_b_fwd(proj_qkv, conv_w):
    t, width = proj_qkv.shape
    tt = min(t, 256)
    nt = t // tt
    wdt = 1024

    def body(cur_ref, prev_ref, w_ref, y_ref):
        i = pl.program_id(1)
        ext = jnp.concatenate([jnp.where(i > 0, prev_ref[...], 0.0), cur_ref[...]], axis=0)
        y_ref[...] = jax.nn.silu(_causal_conv(ext, w_ref[...], 4))

    r = tt // HALO
    return pl.pallas_call(
        body, name="conv_b_fwd", grid=(width // wdt, nt),
        in_specs=[pl.BlockSpec((tt, wdt), lambda j, i: (i, j)),
                  pl.BlockSpec((HALO, wdt), lambda j, i: (jnp.maximum(i * r - 1, 0), j)),
                  pl.BlockSpec((4, wdt), lambda j, i: (0, j))],
        out_specs=pl.BlockSpec((tt, wdt), lambda j, i: (i, j)),
        out_shape=jax.ShapeDtypeStruct((t, width), F32), compiler_params=_params("parallel", "parallel"))(
            proj_qkv, proj_qkv, conv_w)


def _conv_b_bwd(proj_qkv, conv_w, d_act, col, name):
    t = proj_qkv.shape[0]
    tt = min(t, 256)
    nt = t // tt
    wdt = 1024

    def body(cur_ref, prev_ref, nxt_ref, w_ref, da_ref, dan_ref, d_ref, dw_ref):
        i = pl.program_id(0)
        w = w_ref[...]
        u_ext = jnp.concatenate([jnp.where(i > 0, prev_ref[...], 0.0), cur_ref[...], nxt_ref[...]], axis=0)
        taps = [_shift_down(u_ext, 3 - j)[HALO:, :] for j in range(4)]
        conv = w[0:1, :] * taps[0] + w[1:2, :] * taps[1] + w[2:3, :] * taps[2] + w[3:4, :] * taps[3]
        da_cn = jnp.concatenate([da_ref[...], jnp.where(i < nt - 1, dan_ref[...], 0.0)], axis=0)
        sg = jax.nn.sigmoid(conv)
        d_conv = da_cn * (sg * (1.0 + conv * (1.0 - sg)))
        du = None
        for j in range(4):
            term = w[j:j + 1, :] * _shift_up(d_conv, 3 - j)
            du = term if du is None else du + term
        d_ref[...] = du[:tt, :].astype(BF16)

        @pl.when(i == 0)
        def _():
            dw_ref[...] = jnp.zeros_like(dw_ref)

        d_cur = d_conv[:tt, :]
        rows = [jnp.sum(d_cur * taps[j][:tt, :], axis=0, keepdims=True) for j in range(4)]
        dw_ref[0:4, :] += jnp.concatenate(rows, axis=0)

    prev_spec, nxt_spec = _halo_specs(tt, wdt, col, nt)
    _, dan_spec = _halo_specs(tt, wdt, 0, nt)
    return pl.pallas_call(
        body, name=name, grid=(nt,),
        in_specs=[pl.BlockSpec((tt, wdt), lambda i: (i, col)), prev_spec, nxt_spec,
                  pl.BlockSpec((4, wdt), lambda i: (0, col)), pl.BlockSpec((tt, wdt), lambda i: (i, 0)), dan_spec],
        out_specs=[pl.BlockSpec((tt, wdt), lambda i: (i, 0)), pl.BlockSpec((SUBLANES, wdt), lambda i: (0, 0))],
        out_shape=[jax.ShapeDtypeStruct((t, wdt), BF16), jax.ShapeDtypeStruct((SUBLANES, wdt), F32)],
        compiler_params=_params("arbitrary"))(proj_qkv, proj_qkv, proj_qkv, conv_w, d_act, d_act)


def _rms_gate(o, gn, z):
    on = o * lax.rsqrt(jnp.mean(o * o, axis=-1, keepdims=True) + NORM_EPS) * gn
    return on * jax.nn.silu(z)


def _gdn_chunk(mm, qa, ka, va, z, braw, araw, alog, dtb, gn, state):
    c = GDN_CHUNK
    q = qa * lax.rsqrt(jnp.sum(qa * qa, axis=-1, keepdims=True) + 1e-6) * (HEAD_DIM ** -0.5)
    k = ka * lax.rsqrt(jnp.sum(ka * ka, axis=-1, keepdims=True) + 1e-6)
    beta = jax.nn.sigmoid(braw)
    g = -jnp.exp(alog) * jax.nn.softplus(araw + dtb)
    ri = lax.broadcasted_iota(jnp.int32, (c, c), 0)
    ci = lax.broadcasted_iota(jnp.int32, (c, c), 1)
    incl, strict, eye = ri >= ci, ri > ci, ri == ci
    gc = mm(incl.astype(F32), g, "nn", True)
    gc_i = gc[:, :c]
    gc_j = mm(jnp.ones((c, c), F32), jnp.where(eye, gc_i, 0.0), "nn", True)
    decay = jnp.where(incl, jnp.exp(jnp.where(incl, gc_i - gc_j, 0.0)), 0.0)
    kb = k * beta
    low = jnp.where(strict, mm(kb, k, "nt", False) * decay, 0.0)
    x = -low
    inv = eye.astype(F32) + x
    for _ in range(5):
        x = mm(x, x, "nn", True)
        inv = inv + mm(inv, x, "nn", True)
    egc = jnp.exp(gc)
    u = mm(inv, va * beta, "nn", True)
    w = mm(inv, kb * egc, "nn", True)
    attn = jnp.where(incl, mm(q, k, "nt", False) * decay, 0.0)
    g_last = jnp.sum(g, axis=0, keepdims=True)
    v_new = u - mm(w, state, "nn", False)
    o = mm(q * egc, state, "nn", False) + mm(attn, v_new, "nn", False)
    new_state = state * jnp.exp(g_last) + mm(k * jnp.exp(g_last - gc), v_new, "tn", False)
    return _rms_gate(o, gn, z), new_state


def _gdn_specs(tb, rev, nt):
    ti = (lambda i: nt - 1 - i) if rev else (lambda i: i)
    col = lambda off: pl.BlockSpec((tb, HEAD_DIM), lambda h, i: (ti(i), off + h))
    rep = pl.BlockSpec((1, tb, LANES), lambda h, i: (h, ti(i), 0))
    par = pl.BlockSpec((1, SUBLANES, LANES), lambda h, i: (h, 0, 0))
    gn = pl.BlockSpec((SUBLANES, LANES), lambda h, i: (0, 0))
    hist = pl.BlockSpec((1, tb // GDN_CHUNK, HEAD_DIM, HEAD_DIM), lambda h, i: (h, ti(i), 0, 0))
    return col, rep, par, gn, hist


def _gdn_fwd(qkv_act, zb, braw, araw, alog, dtb, gn):
    t = qkv_act.shape[0]
    tb = min(t, 256)
    nt, nc = t // tb, tb // GDN_CHUNK

    def body(q_ref, k_ref, v_ref, z_ref, br_ref, ar_ref, al_ref, dt_ref, gn_ref, y_ref, hist_ref, s_ref):
        @pl.when(pl.program_id(1) == 0)
        def _():
            s_ref[...] = jnp.zeros_like(s_ref)

        al, dt, g = al_ref[0, 0:1, :], dt_ref[0, 0:1, :], gn_ref[0:1, :]
        state = s_ref[...]
        for c in range(nc):
            sl = pl.ds(c * GDN_CHUNK, GDN_CHUNK)
            hist_ref[0, c] = state
            y, state = _gdn_chunk(_mm_raw, q_ref[sl, :], k_ref[sl, :], v_ref[sl, :], z_ref[sl, :],
                                  br_ref[0, sl, :], ar_ref[0, sl, :], al, dt, g, state)
            y_ref[sl, :] = y.astype(BF16)
        s_ref[...] = state

    col, rep, par, gns, hist = _gdn_specs(tb, False, nt)
    return pl.pallas_call(
        body, name="gdn_fwd", grid=(GDN_HEADS, nt),
        in_specs=[col(0), col(GDN_HEADS), col(2 * GDN_HEADS), col(0), rep, rep, par, par, gns],
        out_specs=[col(0), hist],
        out_shape=[jax.ShapeDtypeStruct((t, GDN_HEADS * HEAD_DIM), BF16),
                   jax.ShapeDtypeStruct((GDN_HEADS, t // GDN_CHUNK, HEAD_DIM, HEAD_DIM), F32)],
        scratch_shapes=[pltpu.VMEM((HEAD_DIM, HEAD_DIM), F32)],
        compiler_params=_params("parallel", "arbitrary"))(qkv_act, qkv_act, qkv_act, zb, braw, araw, alog, dtb, gn)


def _gdn_bwd(qkv_act, zb, braw, araw, alog, dtb, gn, hist, dy):
    t = qkv_act.shape[0]
    tb = min(t, 256)
    nt, nc = t // tb, tb // GDN_CHUNK

    def body(q_ref, k_ref, v_ref, z_ref, br_ref, ar_ref, al_ref, dt_ref, gn_ref, hist_ref, dy_ref,
             dq_ref, dk_ref, dv_ref, dz_ref, dbr_ref, dar_ref, dal_ref, ddt_ref, dgn_ref, ds_ref):
        @pl.when(pl.program_id(1) == 0)
        def _():
            ds_ref[...] = jnp.zeros_like(ds_ref)
            dal_ref[...] = jnp.zeros_like(dal_ref)
            ddt_ref[...] = jnp.zeros_like(ddt_ref)
            dgn_ref[...] = jnp.zeros_like(dgn_ref)

        al, dt, g = al_ref[0, 0:1, :], dt_ref[0, 0:1, :], gn_ref[0:1, :]
        d_state = ds_ref[...]
        for c in reversed(range(nc)):
            sl = pl.ds(c * GDN_CHUNK, GDN_CHUNK)
            _, vjp = jax.vjp(functools.partial(_gdn_chunk, _mm_vjp), q_ref[sl, :], k_ref[sl, :], v_ref[sl, :],
                             z_ref[sl, :], br_ref[0, sl, :], ar_ref[0, sl, :], al, dt, g, hist_ref[0, c])
            dq, dk, dv, dz, dbr, dar, dal, ddt, dgn, d_state = vjp((dy_ref[sl, :], d_state))
            dq_ref[sl, :] = dq
            dk_ref[sl, :] = dk
            dv_ref[sl, :] = dv
            dz_ref[sl, :] = dz.astype(BF16)
            dbr_ref[0, sl, :] = _lane_total(dbr)
            dar_ref[0, sl, :] = _lane_total(dar)
            dal_ref[0, 0:1, :] += _lane_total(dal)
            ddt_ref[0, 0:1, :] += _lane_total(ddt)
            dgn_ref[0, 0:1, :] += dgn
        ds_ref[...] = d_state

    col, rep, par, gns, hists = _gdn_specs(tb, True, nt)
    width = GDN_HEADS * HEAD_DIM
    return pl.pallas_call(
        body, name="gdn_bwd", grid=(GDN_HEADS, nt),
        in_specs=[col(0), col(GDN_HEADS), col(2 * GDN_HEADS), col(0), rep, rep, par, par, gns, hists, col(0)],
        out_specs=[col(0), col(0), col(0), col(0), rep, rep, par, par, par],
        out_shape=[jax.ShapeDtypeStruct((t, width), F32)] * 3 + [jax.ShapeDtypeStruct((t, width), BF16)]
        + [jax.ShapeDtypeStruct((GDN_HEADS, t, LANES), F32)] * 2
        + [jax.ShapeDtypeStruct((GDN_HEADS, SUBLANES, LANES), F32)] * 3,
        scratch_shapes=[pltpu.VMEM((HEAD_DIM, HEAD_DIM), F32)],
        compiler_params=_params("parallel", "arbitrary"))(qkv_act, qkv_act, qkv_act, zb, braw, araw, alog, dtb, gn, hist, dy)


def _hgrn_chunk(mm, qr, fr, iv, z, lbl, gn, state):
    c = HGRN_CHUNK
    lb = jax.nn.sigmoid(lbl[1:2, :] - lbl[0:1, :])
    f = lb + (1.0 - lb) * jax.nn.sigmoid(fr)
    q = jax.nn.silu(qr)
    k = 1.0 - f
    logf = jnp.log(f)
    ri = lax.broadcasted_iota(jnp.int32, (c, c), 0)
    ci = lax.broadcasted_iota(jnp.int32, (c, c), 1)
    b = mm((ri >= ci).astype(F32), logf, "nn", True)
    ti = lax.broadcasted_iota(jnp.int32, (c, c, HEAD_DIM), 0)
    si = lax.broadcasted_iota(jnp.int32, (c, c, HEAD_DIM), 1)
    m3 = ti >= si
    decay = jnp.where(m3, jnp.exp(jnp.where(m3, b[:, None, :] - b[None, :, :], 0.0)), 0.0)
    attn = jnp.sum(q[:, None, :] * k[None, :, :] * decay, axis=-1)
    o = mm(q * jnp.exp(b), state, "nt", False) + mm(attn, iv, "nn", False)
    b_last = jnp.sum(logf, axis=0, keepdims=True)
    new_state = state * jnp.exp(b_last) + mm(iv, k * jnp.exp(b_last - b), "tn", False)
    return _rms_gate(o, gn, z), new_state


def _hgrn_specs(tb, rev, nt):
    ti = (lambda i: nt - 1 - i) if rev else (lambda i: i)
    col = lambda off: pl.BlockSpec((tb, HEAD_DIM), lambda h, i: (ti(i), off + h))
    lbs = pl.BlockSpec((2, HEAD_DIM), lambda h, i: (0, h))
    gn = pl.BlockSpec((SUBLANES, LANES), lambda h, i: (0, 0))
    hist = pl.BlockSpec((1, tb // HGRN_CHUNK, HEAD_DIM, HEAD_DIM), lambda h, i: (h, ti(i), 0, 0))
    return col, lbs, gn, hist


def _hgrn_fwd(proj, lower_bounds, gn):
    t = proj.shape[0]
    tb = min(t, 256)
    nt, nc = t // tb, tb // HGRN_CHUNK

    def body(q_ref, f_ref, i_ref, z_ref, lb_ref, gn_ref, y_ref, hist_ref, s_ref):
        @pl.when(pl.program_id(1) == 0)
        def _():
            s_ref[...] = jnp.zeros_like(s_ref)

        lbl, g = lb_ref[...], gn_ref[0:1, :]
        state = s_ref[...]
        for c in range(nc):
            sl = pl.ds(c * HGRN_CHUNK, HGRN_CHUNK)
            hist_ref[0, c] = state
            y, state = _hgrn_chunk(_mm_raw, q_ref[sl, :], f_ref[sl, :], i_ref[sl, :], z_ref[sl, :], lbl, g, state)
            y_ref[sl, :] = y.astype(BF16)
        s_ref[...] = state

    col, lbs, gns, hist = _hgrn_specs(tb, False, nt)
    hh = HGRN_HEADS
    return pl.pallas_call(
        body, name="hgrn_fwd", grid=(hh, nt),
        in_specs=[col(0), col(hh), col(2 * hh), col(3 * hh), lbs, gns],
        out_specs=[col(0), hist],
        out_shape=[jax.ShapeDtypeStruct((t, HGRN_WIDTH), BF16),
                   jax.ShapeDtypeStruct((hh, t // HGRN_CHUNK, HEAD_DIM, HEAD_DIM), F32)],
        scratch_shapes=[pltpu.VMEM((HEAD_DIM, HEAD_DIM), F32)],
        compiler_params=_params("parallel", "arbitrary"))(proj, proj, proj, proj, lower_bounds, gn)


def _hgrn_bwd(proj, lower_bounds, gn, hist, dy):
    t = proj.shape[0]
    tb = min(t, 256)
    nt, nc = t // tb, tb // HGRN_CHUNK

    def body(q_ref, f_ref, i_ref, z_ref, lb_ref, gn_ref, hist_ref, dy_ref,
             dq_ref, df_ref, di_ref, dz_ref, dlb_ref, dgn_ref, ds_ref):
        @pl.when(pl.program_id(1) == 0)
        def _():
            ds_ref[...] = jnp.zeros_like(ds_ref)
            dlb_ref[...] = jnp.zeros_like(dlb_ref)
            dgn_ref[...] = jnp.zeros_like(dgn_ref)

        lbl, g = lb_ref[...], gn_ref[0:1, :]
        d_state = ds_ref[...]
        for c in reversed(range(nc)):
            sl = pl.ds(c * HGRN_CHUNK, HGRN_CHUNK)
            _, vjp = jax.vjp(functools.partial(_hgrn_chunk, _mm_vjp), q_ref[sl, :], f_ref[sl, :], i_ref[sl, :],
                             z_ref[sl, :], lbl, g, hist_ref[0, c])
            dq, df, di, dz, dlb, dgn, d_state = vjp((dy_ref[sl, :], d_state))
            dq_ref[sl, :] = dq.astype(BF16)
            df_ref[sl, :] = df.astype(BF16)
            di_ref[sl, :] = di.astype(BF16)
            dz_ref[sl, :] = dz.astype(BF16)
            dlb_ref[...] += dlb
            dgn_ref[0, 0:1, :] += dgn
        ds_ref[...] = d_state

    col, lbs, gns, hists = _hgrn_specs(tb, True, nt)
    hh = HGRN_HEADS
    par = pl.BlockSpec((1, SUBLANES, LANES), lambda h, i: (h, 0, 0))
    return pl.pallas_call(
        body, name="hgrn_bwd", grid=(hh, nt),
        in_specs=[col(0), col(hh), col(2 * hh), col(3 * hh), lbs, gns, hists, col(0)],
        out_specs=[col(0), col(0), col(0), col(0), lbs, par],
        out_shape=[jax.ShapeDtypeStruct((t, HGRN_WIDTH), BF16)] * 4
        + [jax.ShapeDtypeStruct((2, HGRN_WIDTH), F32), jax.ShapeDtypeStruct((hh, SUBLANES, LANES), F32)],
        scratch_shapes=[pltpu.VMEM((HEAD_DIM, HEAD_DIM), F32)],
        compiler_params=_params("parallel", "arbitrary"))(proj, proj, proj, proj, lower_bounds, gn, hist, dy)


def _layer_norm(pre, g, b):
    mu = jnp.mean(pre, axis=-1, keepdims=True)
    d = pre - mu
    var = jnp.mean(d * d, axis=-1, keepdims=True)
    return d * lax.rsqrt(var + NORM_EPS) * g + b


def _lnpl_fwd(xin, s, p, wg, wpl, ln_g, ln_b):
    t = xin.shape[0]
    tt = min(t, 256)

    def body(x_ref, s_ref, p_ref, wg_ref, wpl_ref, g_ref, b_ref, o_ref, ob_ref):
        xn = _layer_norm(DEEPNORM_ALPHA * x_ref[...] + s_ref[...], g_ref[...], b_ref[...])
        gate = jax.nn.sigmoid(_mm_raw(xn, wg_ref[...], "nn", False))
        out = xn + _mm_raw(p_ref[...], wpl_ref[...], "nn", False) * gate
        o_ref[...] = out
        ob_ref[...] = out.astype(BF16)

    row = lambda w: pl.BlockSpec((tt, w), lambda i: (i, 0))
    full = lambda a: pl.BlockSpec(a.shape, lambda i: (0, 0))
    return pl.pallas_call(
        body, name="lnpl_fwd", grid=(t // tt,),
        in_specs=[row(D_MODEL), row(D_MODEL), row(PL_DIM), full(wg), full(wpl), full(ln_g), full(ln_b)],
        out_specs=[row(D_MODEL), row(D_MODEL)],
        out_shape=[jax.ShapeDtypeStruct((t, D_MODEL), F32), jax.ShapeDtypeStruct((t, D_MODEL), BF16)],
        compiler_params=_params("parallel"))(xin, s, p, wg, wpl, ln_g, ln_b)


def _lnpl_bwd(xin, s, p, wg, wpl, ln_g, ln_b, upstream, last, name):
    t = xin.shape[0]
    tt = min(t, 256)

    def body(x_ref, s_ref, p_ref, wg_ref, wpl_ref, g_ref, b_ref, up_ref,
             dpre_ref, dwg_ref, dwpl_ref, dg_ref, db_ref, loss_ref):
        @pl.when(pl.program_id(0) == 0)
        def _():
            for r in (dwg_ref, dwpl_ref, dg_ref, db_ref, loss_ref):
                r[...] = jnp.zeros_like(r)

        pre = DEEPNORM_ALPHA * x_ref[...] + s_ref[...]
        xn, ln_vjp = jax.vjp(_layer_norm, pre, g_ref[...], b_ref[...])
        gate = jax.nn.sigmoid(_mm_raw(xn, wg_ref[...], "nn", False))
        plv = _mm_raw(p_ref[...], wpl_ref[...], "nn", False)
        if last:
            err = xn + plv * gate - up_ref[...]
            dout = err * (1.0 / D_MODEL)
            tot = jnp.sum(jnp.sum(err * err, axis=1, keepdims=True), axis=0, keepdims=True) * (0.5 / D_MODEL)
            loss_ref[...] += jnp.broadcast_to(tot, loss_ref.shape)
        else:
            dout = up_ref[...]
        dplv = dout * gate
        dlogits = dout * plv * gate * (1.0 - gate)
        dwg_ref[...] += _mm_raw(xn, dlogits, "tn", False)
        dwpl_ref[...] += _mm_raw(p_ref[...], dplv, "tn", False)
        dxn = dout + _mm_raw(dlogits, wg_ref[...], "nt", False)
        dpre, dg, db = ln_vjp(dxn)
        dpre_ref[...] = dpre
        dg_ref[...] += dg
        db_ref[...] += db

    row = lambda w: pl.BlockSpec((tt, w), lambda i: (i, 0))
    full = lambda shape: pl.BlockSpec(shape, lambda i: (0, 0))
    return pl.pallas_call(
        body, name=name, grid=(t // tt,),
        in_specs=[row(D_MODEL), row(D_MODEL), row(PL_DIM), full(wg.shape), full(wpl.shape), full(ln_g.shape),
                  full(ln_b.shape), row(D_MODEL)],
        out_specs=[row(D_MODEL), full(wg.shape), full(wpl.shape), full(ln_g.shape), full(ln_b.shape),
                   full((SUBLANES, LANES))],
        out_shape=[jax.ShapeDtypeStruct((t, D_MODEL), F32), jax.ShapeDtypeStruct(wg.shape, F32),
                   jax.ShapeDtypeStruct(wpl.shape, F32), jax.ShapeDtypeStruct(ln_g.shape, F32),
                   jax.ShapeDtypeStruct(ln_b.shape, F32), jax.ShapeDtypeStruct((SUBLANES, LANES), F32)],
        compiler_params=_params("arbitrary"))(xin, s, p, wg, wpl, ln_g, ln_b, upstream)


def _rep_rows(v):
    return jnp.broadcast_to(v.reshape(1, LANES), (SUBLANES, LANES))


def _rep_heads(v):
    return jnp.broadcast_to(v.reshape(-1, 1, 1), (v.shape[0], SUBLANES, LANES))


def _local_step(x, p, target, w):
    a = DEEPNORM_ALPHA
    nh = GDN_HEADS
    xb = x.astype(BF16)
    wie = w["w_in_even"]
    w_a, w_qkv, w_zb = wie[:, :4096], wie[:, 4096:7168], wie[:, 7168:8192]
    w_tail = jnp.pad(wie[:, 8192:], ((0, 0), (0, LANES - 2 * nh)))
    woe, wio, woo = w["w_out_even"], w["w_in_odd"], w["w_out_odd"]
    conv_a_w, conv_b_w = w["conv_a_w"], w["conv_b_w"]
    ln_g0, ln_b0, ln_g1, ln_b1 = (v.reshape(1, D_MODEL) for v in (w["ln_g"][0], w["ln_b"][0], w["ln_g"][1], w["ln_b"][1]))
    alog, dtb = _rep_heads(w["a_log"].reshape(nh)), _rep_heads(w["dt_bias"].reshape(nh))
    gdn_g, hgrn_g = _rep_rows(w["gdn_norm_g"]), _rep_rows(w["hgrn_norm_g"])

    proj_a = _matmul(xb, w_a, name="fwd_proj_a")
    proj_qkv = _matmul(xb, w_qkv, name="fwd_proj_qkv")
    proj_zb = _matmul(xb, w_zb, name="fwd_proj_zb")
    proj_tail = _matmul(xb, w_tail, name="fwd_proj_tail")
    rep = lambda cols: jnp.broadcast_to(cols.T[:, :, None], (nh, cols.shape[0], LANES))
    braw, araw = rep(proj_tail[:, :nh]), rep(proj_tail[:, nh:2 * nh])
    y_a = _conv_a_fwd(proj_a, conv_a_w)
    qkv_act = _conv_b_fwd(proj_qkv, conv_b_w)
    y_b, gdn_hist = _gdn_fwd(qkv_act, proj_zb, braw, araw, alog, dtb, gdn_g)
    s0 = _matmul(y_b, woe[1024:], name="fwd_out_even_b", add=_matmul(y_a, woe[:1024], name="fwd_out_even_a"))
    x1, x1b = _lnpl_fwd(x, s0, p[0], w["w_pl_gate"][0], w["w_pl"][0], ln_g0, ln_b0)
    proj_o = _matmul(x1b, wio, name="fwd_proj_odd")
    y_o, hgrn_hist = _hgrn_fwd(proj_o, w["lower_bounds"], hgrn_g)
    s1 = _matmul(y_o, woo, name="fwd_out_odd")

    g = {}
    dpre1, dwg1, dwpl1, dlng1, dlnb1, loss = _lnpl_bwd(x1, s1, p[1], w["w_pl_gate"][1], w["w_pl"][1], ln_g1, ln_b1,
                                                     target, True, "lnpl_bwd_odd")
    dy_o = _matmul(dpre1, woo, tb=True, name="bwd_out_odd_dx")
    g["w_out_odd"] = _matmul(y_o, dpre1, ta=True, name="bwd_out_odd_dw")
    dq, df, di, dz, dlb, dhg = _hgrn_bwd(proj_o, w["lower_bounds"], hgrn_g, hgrn_hist, dy_o)
    dx1 = dpre1
    scale = a
    dws = []
    for j, dj in enumerate((dq, df, di, dz)):
        dx1 = _matmul(dj, wio[:, j * HGRN_WIDTH:(j + 1) * HGRN_WIDTH], tb=True, add=dx1, add_scale=scale,
                      name=f"bwd_proj_odd_dx{j}")
        scale = 1.0
        dws.append(_matmul(x1b, dj, ta=True, name=f"bwd_proj_odd_dw{j}"))
    g["w_in_odd"] = jnp.concatenate(dws, axis=1)
    g["hgrn_norm_g"] = jnp.sum(dhg[:, 0, :], axis=0, keepdims=True)
    g["lower_bounds"] = dlb

    dpre0, dwg0, dwpl0, dlng0, dlnb0, _ = _lnpl_bwd(x, s0, p[0], w["w_pl_gate"][0], w["w_pl"][0], ln_g0, ln_b0,
                                                  dx1, False, "lnpl_bwd_even")
    g["w_pl_gate"] = jnp.stack([dwg0, dwg1])
    g["w_pl"] = jnp.stack([dwpl0, dwpl1])
    g["ln_g"] = jnp.concatenate([dlng0, dlng1], axis=0)
    g["ln_b"] = jnp.concatenate([dlnb0, dlnb1], axis=0)
    dy_a = _matmul(dpre0, woe[:1024], tb=True, name="bwd_out_even_dxa")
    dy_b = _matmul(dpre0, woe[1024:], tb=True, name="bwd_out_even_dxb")
    g["w_out_even"] = jnp.concatenate([_matmul(y_a, dpre0, ta=True, name="bwd_out_even_dwa"),
                                       _matmul(y_b, dpre0, ta=True, name="bwd_out_even_dwb")], axis=0)
    dqa, dka, dva, dzb, dbr, dar, dal, ddt, dgn = _gdn_bwd(qkv_act, proj_zb, braw, araw, alog, dtb, gdn_g, gdn_hist, dy_b)
    g["a_log"] = dal[:, 0, 0].reshape(1, nh)
    g["dt_bias"] = ddt[:, 0, 0].reshape(1, nh)
    g["gdn_norm_g"] = jnp.sum(dgn[:, 0, :], axis=0, keepdims=True)
    d_pre_qkv, dwb = [], []
    for j, dj in enumerate((dqa, dka, dva)):
        dpj, dwj = _conv_b_bwd(proj_qkv, conv_b_w, dj, j, f"conv_b_bwd{j}")
        d_pre_qkv.append(dpj)
        dwb.append(dwj[:4])
    g["conv_b_w"] = jnp.concatenate(dwb, axis=1)
    d_a, dwa = _conv_a_bwd(proj_a, conv_a_w, dy_a)
    g["conv_a_w"] = dwa[:3]
    d_tail = jnp.concatenate([dbr[:, :, 0].T, dar[:, :, 0].T, jnp.zeros((x.shape[0], LANES - 2 * nh), F32)],
                             axis=1).astype(BF16)
    pieces = [(d_a, w_a), (d_pre_qkv[0], w_qkv[:, :1024]), (d_pre_qkv[1], w_qkv[:, 1024:2048]),
              (d_pre_qkv[2], w_qkv[:, 2048:]), (dzb, w_zb), (d_tail, w_tail)]
    dx = dpre0
    scale = a
    dws = []
    for j, (dj, wj) in enumerate(pieces):
        dx = _matmul(dj, wj, tb=True, add=dx, add_scale=scale, name=f"bwd_proj_even_dx{j}")
        scale = 1.0
        dws.append(_matmul(xb, dj, ta=True, name=f"bwd_proj_even_dw{j}"))
    dws[-1] = dws[-1][:, :2 * nh]
    g["w_in_even"] = jnp.concatenate(dws, axis=1)
    return loss, dx, g


def _adamw(w, g, m, v, name):
    rows, cols = w.shape
    tr = rows if rows <= 256 else 256
    assert rows % tr == 0, (name, rows)

    def body(w_ref, g_ref, m_ref, v_ref, d_ref, nm_ref, nv_ref):
        gg = g_ref[...]
        nm = ADAM_B1 * m_ref[...] + (1.0 - ADAM_B1) * gg
        nv = ADAM_B2 * v_ref[...] + (1.0 - ADAM_B2) * jnp.square(gg)
        m_hat = nm / (1.0 - ADAM_B1 ** ADAM_STEP)
        v_hat = nv / (1.0 - ADAM_B2 ** ADAM_STEP)
        d_ref[...] = -ADAM_LR * (m_hat / (jnp.sqrt(v_hat) + ADAM_EPS) + ADAM_WD * w_ref[...])
        nm_ref[...] = nm
        nv_ref[...] = nv

    spec = pl.BlockSpec((tr, cols), lambda i: (i, 0))
    return pl.pallas_call(
        body, name=name, grid=(rows // tr,), in_specs=[spec] * 4, out_specs=[spec] * 3,
        out_shape=[jax.ShapeDtypeStruct(w.shape, F32)] * 3, compiler_params=_params("parallel"))(w, g, m, v)


def _row_tile(rows, cap):
    assert rows % SUBLANES == 0, rows
    return SUBLANES * max(d for d in range(1, cap // SUBLANES + 1) if (rows // SUBLANES) % d == 0)


def _add_n(terms, name):
    rows, cols = terms[0].shape
    tr = _row_tile(rows, 4096)

    def body(*refs):
        acc = refs[0][...]
        for r in refs[1:-1]:
            acc = acc + r[...]
        refs[-1][...] = acc

    spec = pl.BlockSpec((tr, cols), lambda i: (i, 0))
    return pl.pallas_call(
        body, name=name, grid=(rows // tr,), in_specs=[spec] * len(terms), out_specs=spec,
        out_shape=jax.ShapeDtypeStruct((rows, cols), F32), compiler_params=_params("parallel"))(*terms)


MESH = pl.DeviceIdType.MESH
N_DEV = 8
HBM_SPEC = pl.BlockSpec(memory_space=pltpu.HBM)
VMEM_SPEC = pl.BlockSpec(memory_space=pltpu.VMEM)


def _coords():
    return lax.axis_index("x"), lax.axis_index("y"), lax.axis_index("c")


def _flip(v, bit):
    return 1 - v if bit else v


def _remote(src, dst, send_sem, recv_sem, dev):
    return pltpu.make_async_remote_copy(src_ref=src, dst_ref=dst, send_sem=send_sem, recv_sem=recv_sem,
                                        device_id=dev, device_id_type=MESH)


def _exchange_small(buf, reduce, name):
    rows = buf.shape[0]

    def body(in_ref, out_ref, slots, send_sems, recv_sems):
        x, y, c = _coords()
        me = 4 * x + 2 * y + c
        slots[me] = in_ref[...]
        peer = lambda k: (_flip(x, (k >> 2) & 1), _flip(y, (k >> 1) & 1), _flip(c, k & 1))
        sends = []
        for k in range(1, N_DEV):
            cp = _remote(in_ref, slots.at[me], send_sems.at[k - 1], recv_sems.at[k - 1], peer(k))
            cp.start()
            sends.append(cp)
        for k in range(1, N_DEV):
            px, py, pc = peer(k)
            _remote(in_ref, slots.at[4 * px + 2 * py + pc], send_sems.at[k - 1], recv_sems.at[k - 1], peer(k)).wait_recv()
        for cp in sends:
            cp.wait_send()
        if reduce:
            acc = slots[0]
            for d in range(1, N_DEV):
                acc = acc + slots[d]
            out_ref[...] = acc
        else:
            out_ref[...] = slots[...]

    out_shape = (rows, LANES) if reduce else (N_DEV, rows, LANES)
    return pl.pallas_call(
        body, name=name, in_specs=[VMEM_SPEC], out_specs=VMEM_SPEC, out_shape=jax.ShapeDtypeStruct(out_shape, F32),
        scratch_shapes=[pltpu.VMEM((N_DEV, rows, LANES), F32), pltpu.SemaphoreType.DMA((N_DEV - 1,)),
                        pltpu.SemaphoreType.DMA((N_DEV - 1,))])(buf)


def _half_rows(half, which):
    return pl.ds(pl.multiple_of(which * half, 16), half)


def _gather_shards(shard, name):
    rows = shard.shape[0]
    half = rows // 2

    def body(in_ref, out_ref, loc_sem, ici_s, ici_r, d2d_s, d2d_r):
        x, y, c = _coords()
        chip = 2 * x + y
        mine = pltpu.make_async_copy(in_ref, out_ref.at[chip], loc_sem)
        mine.start()
        other = lambda k: (_flip(x, (k >> 1) & 1), _flip(y, k & 1))
        sends = []
        for k in (1, 2, 3):
            ox, oy = other(k)
            cp = _remote(in_ref.at[_half_rows(half, c)], out_ref.at[chip, _half_rows(half, c)],
                         ici_s.at[k - 1], ici_r.at[k - 1], (ox, oy, c))
            cp.start()
            sends.append(cp)
        for k in (1, 2, 3):
            ox, oy = other(k)
            piece = out_ref.at[2 * ox + oy, _half_rows(half, c)]
            _remote(piece, piece, ici_s.at[k - 1], ici_r.at[k - 1], (ox, oy, c)).wait_recv()
            cp = _remote(piece, piece, d2d_s.at[k - 1], d2d_r.at[k - 1], (x, y, 1 - c))
            cp.start()
            sends.append(cp)
        for k in (1, 2, 3):
            ox, oy = other(k)
            piece = out_ref.at[2 * ox + oy, _half_rows(half, 1 - c)]
            _remote(piece, piece, d2d_s.at[k - 1], d2d_r.at[k - 1], (x, y, 1 - c)).wait_recv()
        for cp in sends:
            cp.wait_send()
        mine.wait()

    return pl.pallas_call(
        body, name=name, in_specs=[HBM_SPEC], out_specs=HBM_SPEC,
        out_shape=jax.ShapeDtypeStruct((4, rows, LANES), shard.dtype),
        scratch_shapes=[pltpu.SemaphoreType.DMA] + [pltpu.SemaphoreType.DMA((3,))] * 4)(shard)


def _rs_sibling_swap(g4):
    rows = g4.shape[1]
    half = rows // 2

    def body(in_ref, mine_ref, got_ref, loc_sems, send_sems, recv_sems):
        x, y, c = _coords()
        local, sends = [], []
        for s in range(4):
            cp = pltpu.make_async_copy(in_ref.at[s, _half_rows(half, c)], mine_ref.at[s], loc_sems.at[s])
            cp.start()
            local.append(cp)
            cp = _remote(in_ref.at[s, _half_rows(half, 1 - c)], got_ref.at[s], send_sems.at[s], recv_sems.at[s],
                         (x, y, 1 - c))
            cp.start()
            sends.append(cp)
        for cp in sends:
            cp.wait_recv()
        for cp in sends:
            cp.wait_send()
        for cp in local:
            cp.wait()

    shape = jax.ShapeDtypeStruct((4, half, LANES), g4.dtype)
    return pl.pallas_call(
        body, name="rs_sibling_swap", in_specs=[HBM_SPEC], out_specs=[HBM_SPEC, HBM_SPEC], out_shape=[shape, shape],
        scratch_shapes=[pltpu.SemaphoreType.DMA((4,))] * 3)(g4)


def _rs_chip_scatter(p4):
    half = p4.shape[1]

    def body(in_ref, mine_ref, got_ref, loc_sem, send_sems, recv_sems):
        x, y, c = _coords()
        mine = pltpu.make_async_copy(in_ref.at[2 * x + y], mine_ref, loc_sem)
        mine.start()
        sends = []
        for k in (1, 2, 3):
            ox, oy = _flip(x, (k >> 1) & 1), _flip(y, k & 1)
            cp = _remote(in_ref.at[2 * ox + oy], got_ref.at[k - 1], send_sems.at[k - 1], recv_sems.at[k - 1], (ox, oy, c))
            cp.start()
            sends.append(cp)
        for cp in sends:
            cp.wait_recv()
        for cp in sends:
            cp.wait_send()
        mine.wait()

    return pl.pallas_call(
        body, name="rs_chip_scatter", in_specs=[HBM_SPEC], out_specs=[HBM_SPEC, HBM_SPEC],
        out_shape=[jax.ShapeDtypeStruct((half, LANES), p4.dtype), jax.ShapeDtypeStruct((3, half, LANES), p4.dtype)],
        scratch_shapes=[pltpu.SemaphoreType.DMA] + [pltpu.SemaphoreType.DMA((3,))] * 2)(p4)


def _share_halves(mine):
    half = mine.shape[0]

    def body(in_ref, out_ref, loc_sem, send_sem, recv_sem):
        x, y, c = _coords()
        keep = pltpu.make_async_copy(in_ref, out_ref.at[_half_rows(half, c)], loc_sem)
        keep.start()
        cp = _remote(in_ref, out_ref.at[_half_rows(half, c)], send_sem, recv_sem, (x, y, 1 - c))
        cp.start()
        _remote(in_ref, out_ref.at[_half_rows(half, 1 - c)], send_sem, recv_sem, (x, y, 1 - c)).wait_recv()
        cp.wait_send()
        keep.wait()

    return pl.pallas_call(
        body, name="rs_share_halves", in_specs=[HBM_SPEC], out_specs=HBM_SPEC,
        out_shape=jax.ShapeDtypeStruct((2 * half, LANES), mine.dtype),
        scratch_shapes=[pltpu.SemaphoreType.DMA] * 3)(mine)


def _reduce_scatter(g4):
    mine, sib = _rs_sibling_swap(g4)
    half = mine.shape[1]
    p4 = _add_n([mine.reshape(4 * half, LANES), sib.reshape(4 * half, LANES)], "rs_add_sibling").reshape(4, half, LANES)
    own, got = _rs_chip_scatter(p4)
    return _share_halves(_add_n([own, got[0], got[1], got[2]], "rs_add_chips"))


def _cols_split(full):
    r, c4 = full.shape
    return full.reshape(r, 4, c4 // 4).transpose(1, 0, 2)


def _cols_join(s4):
    return s4.transpose(1, 0, 2).reshape(s4.shape[1], 4 * s4.shape[2])


_BIG = {
    "w_in_even": ((1024, 2052), _cols_split, _cols_join),
    "w_out_even": ((512, 1024), lambda f: f.reshape(4, 512, 1024), lambda s: s.reshape(2048, 1024)),
    "w_in_odd": ((1024, 2048), _cols_split, _cols_join),
    "w_out_odd": ((512, 1024), lambda f: f.reshape(4, 512, 1024), lambda s: s.reshape(2048, 1024)),
    "w_pl": ((2, 256, 256), lambda f: f.reshape(2, 256, 4, 256).transpose(2, 0, 1, 3),
             lambda s: s.transpose(1, 2, 0, 3).reshape(2, 256, 1024)),
    "w_pl_gate": ((2, 256, 1024), lambda f: f.reshape(2, 4, 256, 1024).transpose(1, 0, 2, 3),
                  lambda s: s.transpose(1, 0, 2, 3).reshape(2, 1024, 1024)),
}


def _size(shape):
    n = 1
    for d in shape:
        n *= d
    return n


def _pack_big(parts, lead):
    flat = [parts[n].reshape(lead + (_size(_BIG[n][0]),)) for n in _BIG]
    return jnp.concatenate(flat, axis=-1).reshape(lead + (-1, LANES))


def _unpack_big(buf, lead):
    flat = buf.reshape(lead + (-1,))
    out, off = {}, 0
    for n, (shape, _, _) in _BIG.items():
        out[n] = flat[..., off:off + _size(shape)].reshape(lead + shape)
        off += _size(shape)
    return out


_SMALL = {"a_log": (1, 8), "dt_bias": (1, 8), "gdn_norm_g": (1, 128), "hgrn_norm_g": (1, 128),
          "lower_bounds": (2, 2048), "ln_g": (2, 1024), "ln_b": (2, 1024), "conv_a_w": (3, 1024), "conv_b_w": (4, 3072)}
_CONV_SHARD = {"conv_a_w": (3, 256), "conv_b_w": (4, 768)}


def _pack_small(parts, shapes, head_rows=0):
    rows = []
    for n, shape in shapes.items():
        v = parts[n].reshape(-1)
        rows.append(jnp.pad(v, (0, -v.shape[0] % LANES)).reshape(-1, LANES))
    buf = jnp.concatenate(rows, axis=0)
    return jnp.pad(buf, ((head_rows, -(buf.shape[0] + head_rows) % SUBLANES), (0, 0)))


def _unpack_small(buf, shapes, head_rows=0):
    out, off = {}, head_rows
    for n, shape in shapes.items():
        nrow = -(-_size(shape) // LANES)
        out[n] = buf[off:off + nrow].reshape(-1)[:_size(shape)].reshape(shape)
        off += nrow
    return out


_WEIGHTS = ["w_in_even", "conv_a_w", "conv_b_w", "a_log", "dt_bias", "gdn_norm_g", "w_out_even", "w_in_odd",
            "lower_bounds", "hgrn_norm_g", "w_out_odd", "ln_g", "ln_b", "w_pl", "w_pl_gate"]


def kernel(x, p, w_in_even, conv_a_w, conv_b_w, a_log, dt_bias, gdn_norm_g, w_out_even, w_in_odd, lower_bounds, hgrn_norm_g, w_out_odd, ln_g, ln_b, w_pl, w_pl_gate, loss_target, m_w_in_even, m_conv_a_w, m_conv_b_w, m_a_log, m_dt_bias, m_gdn_norm_g, m_w_out_even, m_w_in_odd, m_lower_bounds, m_hgrn_norm_g, m_w_out_odd, m_ln_g, m_ln_b, m_w_pl, m_w_pl_gate, v_w_in_even, v_conv_a_w, v_conv_b_w, v_a_log, v_dt_bias, v_gdn_norm_g, v_w_out_even, v_w_in_odd, v_lower_bounds, v_hgrn_norm_g, v_w_out_odd, v_ln_g, v_ln_b, v_w_pl, v_w_pl_gate):
    w = dict(zip(_WEIGHTS, (w_in_even, conv_a_w, conv_b_w, a_log, dt_bias, gdn_norm_g, w_out_even, w_in_odd,
                            lower_bounds, hgrn_norm_g, w_out_odd, ln_g, ln_b, w_pl, w_pl_gate)))
    m = dict(zip(_WEIGHTS, (m_w_in_even, m_conv_a_w, m_conv_b_w, m_a_log, m_dt_bias, m_gdn_norm_g, m_w_out_even,
                            m_w_in_odd, m_lower_bounds, m_hgrn_norm_g, m_w_out_odd, m_ln_g, m_ln_b, m_w_pl, m_w_pl_gate)))
    v = dict(zip(_WEIGHTS, (v_w_in_even, v_conv_a_w, v_conv_b_w, v_a_log, v_dt_bias, v_gdn_norm_g, v_w_out_even,
                            v_w_in_odd, v_lower_bounds, v_hgrn_norm_g, v_w_out_odd, v_ln_g, v_ln_b, v_w_pl, v_w_pl_gate)))
    chip = 2 * lax.axis_index("x") + lax.axis_index("y")

    shard_shapes = {n: _BIG[n][0] for n in _BIG}
    big_shard = {n: w[n].reshape(shard_shapes[n]) for n in _BIG}
    gathered = _unpack_big(_gather_shards(_pack_big({n: big_shard[n].astype(BF16) for n in _BIG}, ()), "gather_weights"), (4,))
    full = {n: _BIG[n][2](gathered[n]) for n in _BIG}
    conv_mine = _pack_small({n: w[n] for n in _CONV_SHARD}, _CONV_SHARD)
    conv_all = _exchange_small(conv_mine, False, "gather_conv")
    conv_by_chip = [_unpack_small(conv_all[2 * s], _CONV_SHARD) for s in range(4)]
    for n in _CONV_SHARD:
        full[n] = jnp.concatenate([conv_by_chip[s][n] for s in range(4)], axis=1)
    for n in _SMALL:
        if n not in _CONV_SHARD:
            full[n] = w[n]

    loss_part, dx, g = _local_step(x[0], p[:, 0], loss_target[0], full)

    g4 = _pack_big({n: _BIG[n][1](g[n]) for n in _BIG}, (4,))
    g_big = _unpack_big(_reduce_scatter(g4), ())
    small_sum = _exchange_small(jnp.concatenate([loss_part, _pack_small(g, _SMALL)], axis=0), True, "reduce_small")
    loss = small_sum[0, 0]
    g_small = _unpack_small(small_sum, _SMALL, head_rows=SUBLANES)
    for n, (rows, cols) in _CONV_SHARD.items():
        g_small[n] = lax.dynamic_slice_in_dim(g_small[n], chip * cols, cols, axis=1)

    grads, delta, new_m, new_v = {}, {}, {}, {}
    for n in _BIG:
        shape2 = (-1, shard_shapes[n][-1])
        grads[n] = g_big[n].reshape(w[n].shape)
        d_, m_, v_ = _adamw(w[n].reshape(shape2), g_big[n].reshape(shape2), m[n].reshape(shape2), v[n].reshape(shape2),
                            f"adamw_{n}")
        delta[n], new_m[n], new_v[n] = (t.reshape(w[n].shape) for t in (d_, m_, v_))
    own = {n: (_CONV_SHARD[n] if n in _CONV_SHARD else _SMALL[n]) for n in _SMALL}
    packs = [_pack_small({n: src[n] for n in _SMALL}, own) for src in (w, g_small, m, v)]
    outs = [_unpack_small(t, own) for t in _adamw(*packs, "adamw_small")]
    for n in _SMALL:
        grads[n] = g_small[n].reshape(w[n].shape)
        delta[n], new_m[n], new_v[n] = (t[n].reshape(w[n].shape) for t in outs)
    return (loss, dx[None], *[grads[n] for n in _WEIGHTS], *[delta[n] for n in _WEIGHTS],
            *[new_m[n] for n in _WEIGHTS], *[new_v[n] for n in _WEIGHTS])
```

```python
import functools

import jax
import jax.numpy as jnp
from jax import lax
from jax.experimental import pallas as pl
from jax.experimental.pallas import tpu as pltpu

F32 = jnp.float32
BF16 = jnp.bfloat16
HI = lax.Precision.HIGHEST

D_MODEL = 1024
PL_DIM = 256
GDN_HEADS = 8
HEAD_DIM = 128
GDN_CHUNK = 64
HGRN_HEADS = 16
HGRN_CHUNK = 32
HGRN_WIDTH = 2048
DEEPNORM_ALPHA = 4.0 ** 0.25
NORM_EPS = 1e-5
ADAM_LR, ADAM_B1, ADAM_B2, ADAM_EPS, ADAM_WD, ADAM_STEP = 0.001, 0.9, 0.999, 1e-08, 0.01, 10

VMEM_LIMIT = 56 * 1024 * 1024
SUBLANES = 8
LANES = 128

_DIMS = {"nn": (((1,), (0,)), ((), ())), "nt": (((1,), (1,)), ((), ())), "tn": (((0,), (0,)), ((), ()))}


def _params(*sem):
    return pltpu.CompilerParams(dimension_semantics=sem, vmem_limit_bytes=VMEM_LIMIT)


def _mm_raw(a, b, kind, hi):
    if hi:
        return lax.dot_general(a, b, _DIMS[kind], precision=HI, preferred_element_type=F32)
    return lax.dot_general(a.astype(BF16), b.astype(BF16), _DIMS[kind], preferred_element_type=F32)


@functools.partial(jax.custom_vjp, nondiff_argnums=(2, 3))
def _mm_vjp(a, b, kind, hi):
    return _mm_raw(a, b, kind, hi)


def _mm_vjp_fwd(a, b, kind, hi):
    return _mm_raw(a, b, kind, hi), (a, b)


def _mm_vjp_bwd(kind, hi, res, dc):
    a, b = res
    if kind == "nn":
        return _mm_raw(dc, b, "nt", hi), _mm_raw(a, dc, "tn", hi)
    if kind == "nt":
        return _mm_raw(dc, b, "nn", hi), _mm_raw(dc, a, "tn", hi)
    return _mm_raw(b, dc, "nt", hi), _mm_raw(a, dc, "nn", hi)


_mm_vjp.defvjp(_mm_vjp_fwd, _mm_vjp_bwd)


def _lane_total(v):
    return jnp.broadcast_to(jnp.sum(v, axis=-1, keepdims=True), v.shape)


def _matmul(a, b, *, name, ta=False, tb=False, out_dtype=F32, add=None, add_scale=1.0, tm=512, tn=1024, tk=1024):
    m, k = (a.shape[1], a.shape[0]) if ta else a.shape
    n = b.shape[0] if tb else b.shape[1]
    tm, tn, tk = min(tm, m), min(tn, n), min(tk, k)
    assert m % tm == 0 and n % tn == 0 and k % tk == 0, (name, m, n, k)
    nk = k // tk
    kind = ("t" if ta else "n") + ("t" if tb else "n")
    dims = (((0 if ta else 1,), (1 if tb else 0,)), ((), ()))

    def body(*refs):
        if add is None:
            a_ref, b_ref, o_ref, acc_ref = refs
        else:
            a_ref, b_ref, add_ref, o_ref, acc_ref = refs
        kk = pl.program_id(2)

        @pl.when(kk == 0)
        def _():
            acc_ref[...] = jnp.zeros_like(acc_ref)

        acc_ref[...] += lax.dot_general(a_ref[...].astype(BF16), b_ref[...].astype(BF16), dims,
                                        preferred_element_type=F32)

        @pl.when(kk == nk - 1)
        def _():
            r = acc_ref[...]
            if add is not None:
                r = r + add_scale * add_ref[...].astype(F32)
            o_ref[...] = r.astype(out_dtype)

    a_spec = pl.BlockSpec((tk, tm), lambda i, j, kk: (kk, i)) if ta else pl.BlockSpec((tm, tk), lambda i, j, kk: (i, kk))
    b_spec = pl.BlockSpec((tn, tk), lambda i, j, kk: (j, kk)) if tb else pl.BlockSpec((tk, tn), lambda i, j, kk: (kk, j))
    o_spec = pl.BlockSpec((tm, tn), lambda i, j, kk: (i, j))
    in_specs = [a_spec, b_spec] + ([o_spec] if add is not None else [])
    args = (a, b) + ((add,) if add is not None else ())
    del kind
    return pl.pallas_call(
        body, name=name, grid=(m // tm, n // tn, nk), in_specs=in_specs, out_specs=o_spec,
        out_shape=jax.ShapeDtypeStruct((m, n), out_dtype), scratch_shapes=[pltpu.VMEM((tm, tn), F32)],
        compiler_params=_params("parallel", "parallel", "arbitrary"))(*args)


HALO = SUBLANES


def _halo_specs(tt, width, col, nt):
    r = tt // HALO
    prev = pl.BlockSpec((HALO, width), lambda i: (jnp.maximum(i * r - 1, 0), col))
    nxt = pl.BlockSpec((HALO, width), lambda i: (jnp.minimum((i + 1) * r, nt * r - 1), col))
    return prev, nxt


def _shift_down(ext, k):
    return ext if k == 0 else pltpu.roll(ext, k, 0)


def _shift_up(ext, k):
    return ext if k == 0 else pltpu.roll(ext, ext.shape[0] - k, 0)


def _causal_conv(ext, w, taps):
    acc = None
    for j in range(taps):
        term = w[j:j + 1, :] * _shift_down(ext, taps - 1 - j)
        acc = term if acc is None else acc + term
    return acc[HALO:, :]


def _conv_a_fwd(proj_a, conv_w):
    t = proj_a.shape[0]
    tt = min(t, 256)
    nt = t // tt
    wdt = 1024

    def body(cur_ref, prev_ref, w_ref, y_ref):
        i = pl.program_id(0)
        cur = cur_ref[...]
        h, c, b, z = (cur[:, k * wdt:(k + 1) * wdt] for k in range(4))
        prev = prev_ref[...]
        u_prev = jnp.where(i > 0, prev[:, wdt:2 * wdt] * prev[:, 0:wdt], 0.0)
        ext = jnp.concatenate([u_prev, c * h], axis=0)
        conv = _causal_conv(ext, w_ref[...], 3)
        y_ref[...] = (b * conv * jax.nn.silu(z)).astype(BF16)

    prev_spec, _ = _halo_specs(tt, 4 * wdt, 0, nt)
    return pl.pallas_call(
        body, name="conv_a_fwd", grid=(nt,),
        in_specs=[pl.BlockSpec((tt, 4 * wdt), lambda i: (i, 0)), prev_spec, pl.BlockSpec((3, wdt), lambda i: (0, 0))],
        out_specs=pl.BlockSpec((tt, wdt), lambda i: (i, 0)),
        out_shape=jax.ShapeDtypeStruct((t, wdt), BF16), compiler_params=_params("parallel"))(proj_a, proj_a, conv_w)


def _conv_a_bwd(proj_a, conv_w, dy):
    t = proj_a.shape[0]
    tt = min(t, 256)
    nt = t // tt
    wdt = 1024

    def body(cur_ref, prev_ref, nxt_ref, w_ref, dy_ref, dyn_ref, d_ref, dw_ref):
        i = pl.program_id(0)
        w = w_ref[...]
        cur, prev, nxt = cur_ref[...], prev_ref[...], nxt_ref[...]
        split = lambda a: tuple(a[:, k * wdt:(k + 1) * wdt] for k in range(4))
        h, c, b, z = split(cur)
        hp, cp, _, _ = split(prev)
        hn, cn, bn, zn = split(nxt)
        u_prev = jnp.where(i > 0, cp * hp, 0.0)
        u_ext = jnp.concatenate([u_prev, c * h, cn * hn], axis=0)
        taps = [_shift_down(u_ext, 2 - j)[HALO:, :] for j in range(3)]
        conv = w[0:1, :] * taps[0] + w[1:2, :] * taps[1] + w[2:3, :] * taps[2]
        b_cn = jnp.concatenate([b, bn], axis=0)
        z_cn = jnp.concatenate([z, zn], axis=0)
        dy_cn = jnp.concatenate([dy_ref[...], jnp.where(i < nt - 1, dyn_ref[...], 0.0)], axis=0)
        sg = jax.nn.sigmoid(z_cn)
        silu = z_cn * sg
        d_conv = dy_cn * b_cn * silu
        db = (dy_cn * conv * silu)[:tt, :]
        dz = (dy_cn * b_cn * conv * (sg * (1.0 + z_cn * (1.0 - sg))))[:tt, :]
        du = None
        for j in range(3):
            term = w[j:j + 1, :] * _shift_up(d_conv, 2 - j)
            du = term if du is None else du + term
        du = du[:tt, :]
        d_ref[...] = jnp.concatenate([du * c, du * h, db, dz], axis=1).astype(BF16)

        @pl.when(i == 0)
        def _():
            dw_ref[...] = jnp.zeros_like(dw_ref)

        d_cur = d_conv[:tt, :]
        rows = [jnp.sum(d_cur * taps[j][:tt, :], axis=0, keepdims=True) for j in range(3)]
        dw_ref[0:3, :] += jnp.concatenate(rows, axis=0)

    prev_spec, nxt_spec = _halo_specs(tt, 4 * wdt, 0, nt)
    _, dyn_spec = _halo_specs(tt, wdt, 0, nt)
    return pl.pallas_call(
        body, name="conv_a_bwd", grid=(nt,),
        in_specs=[pl.BlockSpec((tt, 4 * wdt), lambda i: (i, 0)), prev_spec, nxt_spec,
                  pl.BlockSpec((3, wdt), lambda i: (0, 0)), pl.BlockSpec((tt, wdt), lambda i: (i, 0)), dyn_spec],
        out_specs=[pl.BlockSpec((tt, 4 * wdt), lambda i: (i, 0)), pl.BlockSpec((SUBLANES, wdt), lambda i: (0, 0))],
        out_shape=[jax.ShapeDtypeStruct((t, 4 * wdt), BF16), jax.ShapeDtypeStruct((SUBLANES, wdt), F32)],
        compiler_params=_params("arbitrary"))(proj_a, proj_a, proj_a, conv_w, dy, dy)


def _conv_b_fwd(proj_qkv, conv_w):
    t, width = proj_qkv.shape
    tt = min(t, 256)
    nt = t // tt
    wdt = 1024

    def body(cur_ref, prev_ref, w_ref, y_ref):
        i = pl.program_id(1)
        ext = jnp.concatenate([jnp.where(i > 0, prev_ref[...], 0.0), cur_ref[...]], axis=0)
        y_ref[...] = jax.nn.silu(_causal_conv(ext, w_ref[...], 4))

    r = tt // HALO
    return pl.pallas_call(
        body, name="conv_b_fwd", grid=(width // wdt, nt),
        in_specs=[pl.BlockSpec((tt, wdt), lambda j, i: (i, j)),
                  pl.BlockSpec((HALO, wdt), lambda j, i: (jnp.maximum(i * r - 1, 0), j)),
                  pl.BlockSpec((4, wdt), lambda j, i: (0, j))],
        out_specs=pl.BlockSpec((tt, wdt), lambda j, i: (i, j)),
        out_shape=jax.ShapeDtypeStruct((t, width), F32), compiler_params=_params("parallel", "parallel"))(
            proj_qkv, proj_qkv, conv_w)


def _conv_b_bwd(proj_qkv, conv_w, d_act, col, name):
    t = proj_qkv.shape[0]
    tt = min(t, 256)
    nt = t // tt
    wdt = 1024

    def body(cur_ref, prev_ref, nxt_ref, w_ref, da_ref, dan_ref, d_ref, dw_ref):
        i = pl.program_id(0)
        w = w_ref[...]
        u_ext = jnp.concatenate([jnp.where(i > 0, prev_ref[...], 0.0), cur_ref[...], nxt_ref[...]], axis=0)
        taps = [_shift_down(u_ext, 3 - j)[HALO:, :] for j in range(4)]
        conv = w[0:1, :] * taps[0] + w[1:2, :] * taps[1] + w[2:3, :] * taps[2] + w[3:4, :] * taps[3]
        da_cn = jnp.concatenate([da_ref[...], jnp.where(i < nt - 1, dan_ref[...], 0.0)], axis=0)
        sg = jax.nn.sigmoid(conv)
        d_conv = da_cn * (sg * (1.0 + conv * (1.0 - sg)))
        du = None
        for j in range(4):
            term = w[j:j + 1, :] * _shift_up(d_conv, 3 - j)
            du = term if du is None else du + term
        d_ref[...] = du[:tt, :].astype(BF16)

        @pl.when(i == 0)
        def _():
            dw_ref[...] = jnp.zeros_like(dw_ref)

        d_cur = d_conv[:tt, :]
        rows = [jnp.sum(d_cur * taps[j][:tt, :], axis=0, keepdims=True) for j in range(4)]
        dw_ref[0:4, :] += jnp.concatenate(rows, axis=0)

    prev_spec, nxt_spec = _halo_specs(tt, wdt, col, nt)
    _, dan_spec = _halo_specs(tt, wdt, 0, nt)
    return pl.pallas_call(
        body, name=name, grid=(nt,),
        in_specs=[pl.BlockSpec((tt, wdt), lambda i: (i, col)), prev_spec, nxt_spec,
                  pl.BlockSpec((4, wdt), lambda i: (0, col)), pl.BlockSpec((tt, wdt), lambda i: (i, 0)), dan_spec],
        out_specs=[pl.BlockSpec((tt, wdt), lambda i: (i, 0)), pl.BlockSpec((SUBLANES, wdt), lambda i: (0, 0))],
        out_shape=[jax.ShapeDtypeStruct((t, wdt), BF16), jax.ShapeDtypeStruct((SUBLANES, wdt), F32)],
        compiler_params=_params("arbitrary"))(proj_qkv, proj_qkv, proj_qkv, conv_w, d_act, d_act)


def _rms_gate(o, gn, z):
    on = o * lax.rsqrt(jnp.mean(o * o, axis=-1, keepdims=True) + NORM_EPS) * gn
    return on * jax.nn.silu(z)


def _gdn_chunk(mm, qa, ka, va, z, braw, araw, alog, dtb, gn, state):
    c = GDN_CHUNK
    q = qa * lax.rsqrt(jnp.sum(qa * qa, axis=-1, keepdims=True) + 1e-6) * (HEAD_DIM ** -0.5)
    k = ka * lax.rsqrt(jnp.sum(ka * ka, axis=-1, keepdims=True) + 1e-6)
    beta = jax.nn.sigmoid(braw)
    g = -jnp.exp(alog) * jax.nn.softplus(araw + dtb)
    ri = lax.broadcasted_iota(jnp.int32, (c, c), 0)
    ci = lax.broadcasted_iota(jnp.int32, (c, c), 1)
    incl, strict, eye = ri >= ci, ri > ci, ri == ci
    gc = mm(incl.astype(F32), g, "nn", True)
    gc_i = gc[:, :c]
    gc_j = mm(jnp.ones((c, c), F32), jnp.where(eye, gc_i, 0.0), "nn", True)
    decay = jnp.where(incl, jnp.exp(jnp.where(incl, gc_i - gc_j, 0.0)), 0.0)
    kb = k * beta
    low = jnp.where(strict, mm(kb, k, "nt", False) * decay, 0.0)
    x = -low
    inv = eye.astype(F32) + x
    for _ in range(5):
        x = mm(x, x, "nn", True)
        inv = inv + mm(inv, x, "nn", True)
    egc = jnp.exp(gc)
    u = mm(inv, va * beta, "nn", True)
    w = mm(inv, kb * egc, "nn", True)
    attn = jnp.where(incl, mm(q, k, "nt", False) * decay, 0.0)
    g_last = jnp.sum(g, axis=0, keepdims=True)
    v_new = u - mm(w, state, "nn", False)
    o = mm(q * egc, state, "nn", False) + mm(attn, v_new, "nn", False)
    new_state = state * jnp.exp(g_last) + mm(k * jnp.exp(g_last - gc), v_new, "tn", False)
    return _rms_gate(o, gn, z), new_state


def _gdn_specs(tb, rev, nt):
    ti = (lambda i: nt - 1 - i) if rev else (lambda i: i)
    col = lambda off: pl.BlockSpec((tb, HEAD_DIM), lambda h, i: (ti(i), off + h))
    rep = pl.BlockSpec((1, tb, LANES), lambda h, i: (h, ti(i), 0))
    par = pl.BlockSpec((1, SUBLANES, LANES), lambda h, i: (h, 0, 0))
    gn = pl.BlockSpec((SUBLANES, LANES), lambda h, i: (0, 0))
    hist = pl.BlockSpec((1, tb // GDN_CHUNK, HEAD_DIM, HEAD_DIM), lambda h, i: (h, ti(i), 0, 0))
    return col, rep, par, gn, hist


def _gdn_fwd(qkv_act, zb, braw, araw, alog, dtb, gn):
    t = qkv_act.shape[0]
    tb = min(t, 256)
    nt, nc = t // tb, tb // GDN_CHUNK

    def body(q_ref, k_ref, v_ref, z_ref, br_ref, ar_ref, al_ref, dt_ref, gn_ref, y_ref, hist_ref, s_ref):
        @pl.when(pl.program_id(1) == 0)
        def _():
            s_ref[...] = jnp.zeros_like(s_ref)

        al, dt, g = al_ref[0, 0:1, :], dt_ref[0, 0:1, :], gn_ref[0:1, :]
        state = s_ref[...]
        for c in range(nc):
            sl = pl.ds(c * GDN_CHUNK, GDN_CHUNK)
            hist_ref[0, c] = state
            y, state = _gdn_chunk(_mm_raw, q_ref[sl, :], k_ref[sl, :], v_ref[sl, :], z_ref[sl, :],
                                  br_ref[0, sl, :], ar_ref[0, sl, :], al, dt, g, state)
            y_ref[sl, :] = y.astype(BF16)
        s_ref[...] = state

    col, rep, par, gns, hist = _gdn_specs(tb, False, nt)
    return pl.pallas_call(
        body, name="gdn_fwd", grid=(GDN_HEADS, nt),
        in_specs=[col(0), col(GDN_HEADS), col(2 * GDN_HEADS), col(0), rep, rep, par, par, gns],
        out_specs=[col(0), hist],
        out_shape=[jax.ShapeDtypeStruct((t, GDN_HEADS * HEAD_DIM), BF16),
                   jax.ShapeDtypeStruct((GDN_HEADS, t // GDN_CHUNK, HEAD_DIM, HEAD_DIM), F32)],
        scratch_shapes=[pltpu.VMEM((HEAD_DIM, HEAD_DIM), F32)],
        compiler_params=_params("parallel", "arbitrary"))(qkv_act, qkv_act, qkv_act, zb, braw, araw, alog, dtb, gn)


def _gdn_bwd(qkv_act, zb, braw, araw, alog, dtb, gn, hist, dy):
    t = qkv_act.shape[0]
    tb = min(t, 256)
    nt, nc = t // tb, tb // GDN_CHUNK

    def body(q_ref, k_ref, v_ref, z_ref, br_ref, ar_ref, al_ref, dt_ref, gn_ref, hist_ref, dy_ref,
             dq_ref, dk_ref, dv_ref, dz_ref, dbr_ref, dar_ref, dal_ref, ddt_ref, dgn_ref, ds_ref):
        @pl.when(pl.program_id(1) == 0)
        def _():
            ds_ref[...] = jnp.zeros_like(ds_ref)
            dal_ref[...] = jnp.zeros_like(dal_ref)
            ddt_ref[...] = jnp.zeros_like(ddt_ref)
            dgn_ref[...] = jnp.zeros_like(dgn_ref)

        al, dt, g = al_ref[0, 0:1, :], dt_ref[0, 0:1, :], gn_ref[0:1, :]
        d_state = ds_ref[...]
        for c in reversed(range(nc)):
            sl = pl.ds(c * GDN_CHUNK, GDN_CHUNK)
            _, vjp = jax.vjp(functools.partial(_gdn_chunk, _mm_vjp), q_ref[sl, :], k_ref[sl, :], v_ref[sl, :],
                             z_ref[sl, :], br_ref[0, sl, :], ar_ref[0, sl, :], al, dt, g, hist_ref[0, c])
            dq, dk, dv, dz, dbr, dar, dal, ddt, dgn, d_state = vjp((dy_ref[sl, :], d_state))
            dq_ref[sl, :] = dq
            dk_ref[sl, :] = dk
            dv_ref[sl, :] = dv
            dz_ref[sl, :] = dz.astype(BF16)
            dbr_ref[0, sl, :] = _lane_total(dbr)
            dar_ref[0, sl, :] = _lane_total(dar)
            dal_ref[0, 0:1, :] += _lane_total(dal)
            ddt_ref[0, 0:1, :] += _lane_total(ddt)
            dgn_ref[0, 0:1, :] += dgn
        ds_ref[...] = d_state

    col, rep, par, gns, hists = _gdn_specs(tb, True, nt)
    width = GDN_HEADS * HEAD_DIM
    return pl.pallas_call(
        body, name="gdn_bwd", grid=(GDN_HEADS, nt),
        in_specs=[col(0), col(GDN_HEADS), col(2 * GDN_HEADS), col(0), rep, rep, par, par, gns, hists, col(0)],
        out_specs=[col(0), col(0), col(0), col(0), rep, rep, par, par, par],
        out_shape=[jax.ShapeDtypeStruct((t, width), F32)] * 3 + [jax.ShapeDtypeStruct((t, width), BF16)]
        + [jax.ShapeDtypeStruct((GDN_HEADS, t, LANES), F32)] * 2
        + [jax.ShapeDtypeStruct((GDN_HEADS, SUBLANES, LANES), F32)] * 3,
        scratch_shapes=[pltpu.VMEM((HEAD_DIM, HEAD_DIM), F32)],
        compiler_params=_params("parallel", "arbitrary"))(qkv_act, qkv_act, qkv_act, zb, braw, araw, alog, dtb, gn, hist, dy)


def _hgrn_chunk(mm, qr, fr, iv, z, lbl, gn, state):
    c = HGRN_CHUNK
    lb = jax.nn.sigmoid(lbl[1:2, :] - lbl[0:1, :])
    f = lb + (1.0 - lb) * jax.nn.sigmoid(fr)
    q = jax.nn.silu(qr)
    k = 1.0 - f
    logf = jnp.log(f)
    ri = lax.broadcasted_iota(jnp.int32, (c, c), 0)
    ci = lax.broadcasted_iota(jnp.int32, (c, c), 1)
    b = mm((ri >= ci).astype(F32), logf, "nn", True)
    ti = lax.broadcasted_iota(jnp.int32, (c, c, HEAD_DIM), 0)
    si = lax.broadcasted_iota(jnp.int32, (c, c, HEAD_DIM), 1)
    m3 = ti >= si
    decay = jnp.where(m3, jnp.exp(jnp.where(m3, b[:, None, :] - b[None, :, :], 0.0)), 0.0)
    attn = jnp.sum(q[:, None, :] * k[None, :, :] * decay, axis=-1)
    o = mm(q * jnp.exp(b), state, "nt", False) + mm(attn, iv, "nn", False)
    b_last = jnp.sum(logf, axis=0, keepdims=True)
    new_state = state * jnp.exp(b_last) + mm(iv, k * jnp.exp(b_last - b), "tn", False)
    return _rms_gate(o, gn, z), new_state


def _hgrn_specs(tb, rev, nt):
    ti = (lambda i: nt - 1 - i) if rev else (lambda i: i)
    col = lambda off: pl.BlockSpec((tb, HEAD_DIM), lambda h, i: (ti(i), off + h))
    lbs = pl.BlockSpec((2, HEAD_DIM), lambda h, i: (0, h))
    gn = pl.BlockSpec((SUBLANES, LANES), lambda h, i: (0, 0))
    hist = pl.BlockSpec((1, tb // HGRN_CHUNK, HEAD_DIM, HEAD_DIM), lambda h, i: (h, ti(i), 0, 0))
    return col, lbs, gn, hist


def _hgrn_fwd(proj4, lower_bounds, gn):
    t = proj4[0].shape[0]
    tb = min(t, 256)
    nt, nc = t // tb, tb // HGRN_CHUNK

    def body(q_ref, f_ref, i_ref, z_ref, lb_ref, gn_ref, y_ref, hist_ref, s_ref):
        @pl.when(pl.program_id(1) == 0)
        def _():
            s_ref[...] = jnp.zeros_like(s_ref)

        lbl, g = lb_ref[...], gn_ref[0:1, :]
        state = s_ref[...]
        for c in range(nc):
            sl = pl.ds(c * HGRN_CHUNK, HGRN_CHUNK)
            hist_ref[0, c] = state
            y, state = _hgrn_chunk(_mm_raw, q_ref[sl, :], f_ref[sl, :], i_ref[sl, :], z_ref[sl, :], lbl, g, state)
            y_ref[sl, :] = y.astype(BF16)
        s_ref[...] = state

    col, lbs, gns, hist = _hgrn_specs(tb, False, nt)
    hh = HGRN_HEADS
    return pl.pallas_call(
        body, name="hgrn_fwd", grid=(hh, nt),
        in_specs=[col(0), col(0), col(0), col(0), lbs, gns],
        out_specs=[col(0), hist],
        out_shape=[jax.ShapeDtypeStruct((t, HGRN_WIDTH), BF16),
                   jax.ShapeDtypeStruct((hh, t // HGRN_CHUNK, HEAD_DIM, HEAD_DIM), F32)],
        scratch_shapes=[pltpu.VMEM((HEAD_DIM, HEAD_DIM), F32)],
        compiler_params=_params("parallel", "arbitrary"))(*proj4, lower_bounds, gn)


def _hgrn_bwd(proj4, lower_bounds, gn, hist, dy):
    t = proj4[0].shape[0]
    tb = min(t, 256)
    nt, nc = t // tb, tb // HGRN_CHUNK

    def body(q_ref, f_ref, i_ref, z_ref, lb_ref, gn_ref, hist_ref, dy_ref,
             dq_ref, df_ref, di_ref, dz_ref, dlb_ref, dgn_ref, ds_ref):
        @pl.when(pl.program_id(1) == 0)
        def _():
            ds_ref[...] = jnp.zeros_like(ds_ref)
            dlb_ref[...] = jnp.zeros_like(dlb_ref)
            dgn_ref[...] = jnp.zeros_like(dgn_ref)

        lbl, g = lb_ref[...], gn_ref[0:1, :]
        d_state = ds_ref[...]
        for c in reversed(range(nc)):
            sl = pl.ds(c * HGRN_CHUNK, HGRN_CHUNK)
            _, vjp = jax.vjp(functools.partial(_hgrn_chunk, _mm_vjp), q_ref[sl, :], f_ref[sl, :], i_ref[sl, :],
                             z_ref[sl, :], lbl, g, hist_ref[0, c])
            dq, df, di, dz, dlb, dgn, d_state = vjp((dy_ref[sl, :], d_state))
            dq_ref[sl, :] = dq.astype(BF16)
            df_ref[sl, :] = df.astype(BF16)
            di_ref[sl, :] = di.astype(BF16)
            dz_ref[sl, :] = dz.astype(BF16)
            dlb_ref[...] += dlb
            dgn_ref[0, 0:1, :] += dgn
        ds_ref[...] = d_state

    col, lbs, gns, hists = _hgrn_specs(tb, True, nt)
    hh = HGRN_HEADS
    par = pl.BlockSpec((1, SUBLANES, LANES), lambda h, i: (h, 0, 0))
    return pl.pallas_call(
        body, name="hgrn_bwd", grid=(hh, nt),
        in_specs=[col(0), col(0), col(0), col(0), lbs, gns, hists, col(0)],
        out_specs=[col(0), col(0), col(0), col(0), lbs, par],
        out_shape=[jax.ShapeDtypeStruct((t, HGRN_WIDTH), BF16)] * 4
        + [jax.ShapeDtypeStruct((2, HGRN_WIDTH), F32), jax.ShapeDtypeStruct((hh, SUBLANES, LANES), F32)],
        scratch_shapes=[pltpu.VMEM((HEAD_DIM, HEAD_DIM), F32)],
        compiler_params=_params("parallel", "arbitrary"))(*proj4, lower_bounds, gn, hist, dy)


def _layer_norm(pre, g, b):
    mu = jnp.mean(pre, axis=-1, keepdims=True)
    d = pre - mu
    var = jnp.mean(d * d, axis=-1, keepdims=True)
    return d * lax.rsqrt(var + NORM_EPS) * g + b


def _lnpl_fwd(xin, s, p, wg, wpl, ln_g, ln_b):
    t = xin.shape[0]
    tt = min(t, 256)

    def body(x_ref, s_ref, p_ref, wg_ref, wpl_ref, g_ref, b_ref, o_ref, ob_ref):
        xn = _layer_norm(DEEPNORM_ALPHA * x_ref[...] + s_ref[...], g_ref[...], b_ref[...])
        gate = jax.nn.sigmoid(_mm_raw(xn, wg_ref[...], "nn", False))
        out = xn + _mm_raw(p_ref[...], wpl_ref[...], "nn", False) * gate
        o_ref[...] = out
        ob_ref[...] = out.astype(BF16)

    row = lambda w: pl.BlockSpec((tt, w), lambda i: (i, 0))
    full = lambda a: pl.BlockSpec(a.shape, lambda i: (0, 0))
    return pl.pallas_call(
        body, name="lnpl_fwd", grid=(t // tt,),
        in_specs=[row(D_MODEL), row(D_MODEL), row(PL_DIM), full(wg), full(wpl), full(ln_g), full(ln_b)],
        out_specs=[row(D_MODEL), row(D_MODEL)],
        out_shape=[jax.ShapeDtypeStruct((t, D_MODEL), F32), jax.ShapeDtypeStruct((t, D_MODEL), BF16)],
        compiler_params=_params("parallel"))(xin, s, p, wg, wpl, ln_g, ln_b)


def _lnpl_bwd(xin, s, p, wg, wpl, ln_g, ln_b, upstream, last, name):
    t = xin.shape[0]
    tt = min(t, 256)

    def body(x_ref, s_ref, p_ref, wg_ref, wpl_ref, g_ref, b_ref, up_ref,
             dpre_ref, dwg_ref, dwpl_ref, dg_ref, db_ref, loss_ref):
        @pl.when(pl.program_id(0) == 0)
        def _():
            for r in (dwg_ref, dwpl_ref, dg_ref, db_ref, loss_ref):
                r[...] = jnp.zeros_like(r)

        pre = DEEPNORM_ALPHA * x_ref[...] + s_ref[...]
        xn, ln_vjp = jax.vjp(_layer_norm, pre, g_ref[...], b_ref[...])
        gate = jax.nn.sigmoid(_mm_raw(xn, wg_ref[...], "nn", False))
        plv = _mm_raw(p_ref[...], wpl_ref[...], "nn", False)
        if last:
            err = xn + plv * gate - up_ref[...]
            dout = err * (1.0 / D_MODEL)
            tot = jnp.sum(jnp.sum(err * err, axis=1, keepdims=True), axis=0, keepdims=True) * (0.5 / D_MODEL)
            loss_ref[...] += jnp.broadcast_to(tot, loss_ref.shape)
        else:
            dout = up_ref[...]
        dplv = dout * gate
        dlogits = dout * plv * gate * (1.0 - gate)
        dwg_ref[...] += _mm_raw(xn, dlogits, "tn", False)
        dwpl_ref[...] += _mm_raw(p_ref[...], dplv, "tn", False)
        dxn = dout + _mm_raw(dlogits, wg_ref[...], "nt", False)
        dpre, dg, db = ln_vjp(dxn)
        dpre_ref[...] = dpre
        dg_ref[...] += dg
        db_ref[...] += db

    row = lambda w: pl.BlockSpec((tt, w), lambda i: (i, 0))
    full = lambda shape: pl.BlockSpec(shape, lambda i: (0, 0))
    return pl.pallas_call(
        body, name=name, grid=(t // tt,),
        in_specs=[row(D_MODEL), row(D_MODEL), row(PL_DIM), full(wg.shape), full(wpl.shape), full(ln_g.shape),
                  full(ln_b.shape), row(D_MODEL)],
        out_specs=[row(D_MODEL), full(wg.shape), full(wpl.shape), full(ln_g.shape), full(ln_b.shape),
                   full((SUBLANES, LANES))],
        out_shape=[jax.ShapeDtypeStruct((t, D_MODEL), F32), jax.ShapeDtypeStruct(wg.shape, F32),
                   jax.ShapeDtypeStruct(wpl.shape, F32), jax.ShapeDtypeStruct(ln_g.shape, F32),
                   jax.ShapeDtypeStruct(ln_b.shape, F32), jax.ShapeDtypeStruct((SUBLANES, LANES), F32)],
        compiler_params=_params("arbitrary"))(xin, s, p, wg, wpl, ln_g, ln_b, upstream)


def _rep_rows(v):
    return jnp.broadcast_to(v.reshape(1, LANES), (SUBLANES, LANES))


def _rep_heads(v):
    return jnp.broadcast_to(v.reshape(-1, 1, 1), (v.shape[0], SUBLANES, LANES))


def _local_step(x, p, target, w):
    a = DEEPNORM_ALPHA
    nh = GDN_HEADS
    xb = x.astype(BF16)
    wie = w["w_in_even"]
    w_a, w_qkv, w_zb = wie[:, :4096], wie[:, 4096:7168], wie[:, 7168:8192]
    w_tail = jnp.pad(wie[:, 8192:], ((0, 0), (0, LANES - 2 * nh)))
    woe, wio, woo = w["w_out_even"], w["w_in_odd"], w["w_out_odd"]
    conv_a_w, conv_b_w = w["conv_a_w"], w["conv_b_w"]
    ln_g0, ln_b0, ln_g1, ln_b1 = (v.reshape(1, D_MODEL) for v in (w["ln_g"][0], w["ln_b"][0], w["ln_g"][1], w["ln_b"][1]))
    alog, dtb = _rep_heads(w["a_log"].reshape(nh)), _rep_heads(w["dt_bias"].reshape(nh))
    gdn_g, hgrn_g = _rep_rows(w["gdn_norm_g"]), _rep_rows(w["hgrn_norm_g"])

    proj_a = _matmul(xb, w_a, name="fwd_proj_a")
    proj_qkv = _matmul(xb, w_qkv, name="fwd_proj_qkv")
    proj_zb = _matmul(xb, w_zb, name="fwd_proj_zb")
    proj_tail = _matmul(xb, w_tail, name="fwd_proj_tail")
    rep = lambda cols: jnp.broadcast_to(cols.T[:, :, None], (nh, cols.shape[0], LANES))
    braw, araw = rep(proj_tail[:, :nh]), rep(proj_tail[:, nh:2 * nh])
    y_a = _conv_a_fwd(proj_a, conv_a_w)
    qkv_act = _conv_b_fwd(proj_qkv, conv_b_w)
    y_b, gdn_hist = _gdn_fwd(qkv_act, proj_zb, braw, araw, alog, dtb, gdn_g)
    s0 = _matmul(y_b, woe[1024:], name="fwd_out_even_b", add=_matmul(y_a, woe[:1024], name="fwd_out_even_a"))
    x1, x1b = _lnpl_fwd(x, s0, p[0], w["w_pl_gate"][0], w["w_pl"][0], ln_g0, ln_b0)
    proj_o = [_matmul(x1b, wio[j], name=f"fwd_proj_odd{j}") for j in range(4)]
    y_o, hgrn_hist = _hgrn_fwd(proj_o, w["lower_bounds"], hgrn_g)
    s1 = _matmul(y_o, woo, name="fwd_out_odd")

    g = {}
    dpre1, dwg1, dwpl1, dlng1, dlnb1, loss = _lnpl_bwd(x1, s1, p[1], w["w_pl_gate"][1], w["w_pl"][1], ln_g1, ln_b1,
                                                     target, True, "lnpl_bwd_odd")
    dy_o = _matmul(dpre1, woo, tb=True, name="bwd_out_odd_dx")
    g["w_out_odd"] = _matmul(y_o, dpre1, ta=True, name="bwd_out_odd_dw")
    dq, df, di, dz, dlb, dhg = _hgrn_bwd(proj_o, w["lower_bounds"], hgrn_g, hgrn_hist, dy_o)
    dx1 = dpre1
    scale = a
    dws = []
    for j, dj in enumerate((dq, df, di, dz)):
        dx1 = _matmul(dj, wio[j], tb=True, add=dx1, add_scale=scale, name=f"bwd_proj_odd_dx{j}")
        scale = 1.0
        dws.append(_matmul(x1b, dj, ta=True, name=f"bwd_proj_odd_dw{j}"))
    g["w_in_odd"] = jnp.stack(dws)
    g["hgrn_norm_g"] = jnp.sum(dhg[:, 0, :], axis=0, keepdims=True)
    g["lower_bounds"] = dlb

    dpre0, dwg0, dwpl0, dlng0, dlnb0, _ = _lnpl_bwd(x, s0, p[0], w["w_pl_gate"][0], w["w_pl"][0], ln_g0, ln_b0,
                                                  dx1, False, "lnpl_bwd_even")
    g["w_pl_gate"] = jnp.stack([dwg0, dwg1])
    g["w_pl"] = jnp.stack([dwpl0, dwpl1])
    g["ln_g"] = jnp.concatenate([dlng0, dlng1], axis=0)
    g["ln_b"] = jnp.concatenate([dlnb0, dlnb1], axis=0)
    dy_a = _matmul(dpre0, woe[:1024], tb=True, name="bwd_out_even_dxa")
    dy_b = _matmul(dpre0, woe[1024:], tb=True, name="bwd_out_even_dxb")
    g["w_out_even"] = jnp.concatenate([_matmul(y_a, dpre0, ta=True, name="bwd_out_even_dwa"),
                                       _matmul(y_b, dpre0, ta=True, name="bwd_out_even_dwb")], axis=0)
    dqa, dka, dva, dzb, dbr, dar, dal, ddt, dgn = _gdn_bwd(qkv_act, proj_zb, braw, araw, alog, dtb, gdn_g, gdn_hist, dy_b)
    g["a_log"] = dal[:, 0, 0].reshape(1, nh)
    g["dt_bias"] = ddt[:, 0, 0].reshape(1, nh)
    g["gdn_norm_g"] = jnp.sum(dgn[:, 0, :], axis=0, keepdims=True)
    d_pre_qkv, dwb = [], []
    for j, dj in enumerate((dqa, dka, dva)):
        dpj, dwj = _conv_b_bwd(proj_qkv, conv_b_w, dj, j, f"conv_b_bwd{j}")
        d_pre_qkv.append(dpj)
        dwb.append(dwj[:4])
    g["conv_b_w"] = jnp.concatenate(dwb, axis=1)
    d_a, dwa = _conv_a_bwd(proj_a, conv_a_w, dy_a)
    g["conv_a_w"] = dwa[:3]
    d_tail = jnp.concatenate([dbr[:, :, 0].T, dar[:, :, 0].T, jnp.zeros((x.shape[0], LANES - 2 * nh), F32)],
                             axis=1).astype(BF16)
    pieces = [(d_a, w_a), (d_pre_qkv[0], w_qkv[:, :1024]), (d_pre_qkv[1], w_qkv[:, 1024:2048]),
              (d_pre_qkv[2], w_qkv[:, 2048:]), (dzb, w_zb), (d_tail, w_tail)]
    dx = dpre0
    scale = a
    dws = []
    for j, (dj, wj) in enumerate(pieces):
        dx = _matmul(dj, wj, tb=True, add=dx, add_scale=scale, name=f"bwd_proj_even_dx{j}")
        scale = 1.0
        dws.append(_matmul(xb, dj, ta=True, name=f"bwd_proj_even_dw{j}"))
    dws[-1] = dws[-1][:, :2 * nh]
    g["w_in_even"] = jnp.concatenate(dws, axis=1)
    return loss, dx, g


def _adamw(w, g, m, v, name):
    rows, cols = w.shape
    tr = rows if rows <= 256 else 256
    assert rows % tr == 0, (name, rows)

    def body(w_ref, g_ref, m_ref, v_ref, d_ref, nm_ref, nv_ref):
        gg = g_ref[...]
        nm = ADAM_B1 * m_ref[...] + (1.0 - ADAM_B1) * gg
        nv = ADAM_B2 * v_ref[...] + (1.0 - ADAM_B2) * jnp.square(gg)
        m_hat = nm / (1.0 - ADAM_B1 ** ADAM_STEP)
        v_hat = nv / (1.0 - ADAM_B2 ** ADAM_STEP)
        d_ref[...] = -ADAM_LR * (m_hat / (jnp.sqrt(v_hat) + ADAM_EPS) + ADAM_WD * w_ref[...])
        nm_ref[...] = nm
        nv_ref[...] = nv

    spec = pl.BlockSpec((tr, cols), lambda i: (i, 0))
    return pl.pallas_call(
        body, name=name, grid=(rows // tr,), in_specs=[spec] * 4, out_specs=[spec] * 3,
        out_shape=[jax.ShapeDtypeStruct(w.shape, F32)] * 3, compiler_params=_params("parallel"))(w, g, m, v)


MESH = pl.DeviceIdType.MESH
N_DEV = 8
HBM_SPEC = pl.BlockSpec(memory_space=pltpu.HBM)
VMEM_SPEC = pl.BlockSpec(memory_space=pltpu.VMEM)


def _coords():
    return lax.axis_index("x"), lax.axis_index("y"), lax.axis_index("c")


def _flip(v, bit):
    return 1 - v if bit else v


def _remote(src, dst, send_sem, recv_sem, dev):
    return pltpu.make_async_remote_copy(src_ref=src, dst_ref=dst, send_sem=send_sem, recv_sem=recv_sem,
                                        device_id=dev, device_id_type=MESH)


def _exchange_small(buf, reduce, name):
    rows = buf.shape[0]

    def body(in_ref, out_ref, slots, send_sems, recv_sems):
        x, y, c = _coords()
        me = 4 * x + 2 * y + c
        slots[me] = in_ref[...]
        peer = lambda k: (_flip(x, (k >> 2) & 1), _flip(y, (k >> 1) & 1), _flip(c, k & 1))
        sends = []
        for k in range(1, N_DEV):
            cp = _remote(in_ref, slots.at[me], send_sems.at[k - 1], recv_sems.at[k - 1], peer(k))
            cp.start()
            sends.append(cp)
        for k in range(1, N_DEV):
            px, py, pc = peer(k)
            _remote(in_ref, slots.at[4 * px + 2 * py + pc], send_sems.at[k - 1], recv_sems.at[k - 1], peer(k)).wait_recv()
        for cp in sends:
            cp.wait_send()
        if reduce:
            acc = slots[0]
            for d in range(1, N_DEV):
                acc = acc + slots[d]
            out_ref[...] = acc
        else:
            out_ref[...] = slots[...]

    out_shape = (rows, LANES) if reduce else (N_DEV, rows, LANES)
    return pl.pallas_call(
        body, name=name, in_specs=[VMEM_SPEC], out_specs=VMEM_SPEC, out_shape=jax.ShapeDtypeStruct(out_shape, F32),
        scratch_shapes=[pltpu.VMEM((N_DEV, rows, LANES), F32), pltpu.SemaphoreType.DMA((N_DEV - 1,)),
                        pltpu.SemaphoreType.DMA((N_DEV - 1,))])(buf)


def _half_rows(half, which):
    return pl.ds(pl.multiple_of(which * half, 16), half)


def _other_chip(x, y, k):
    return _flip(x, (k >> 1) & 1), _flip(y, k & 1)


def _gather_shards(shards):
    n = len(shards)

    def body(*refs):
        ins, outs = refs[:n], refs[n:2 * n]
        ici_s, ici_r, d2d_s, d2d_r = refs[2 * n:]
        x, y, c = _coords()
        chip = 2 * x + y
        sends = []
        for i in range(n):
            half = ins[i].shape[0] // 2
            for k in (1, 2, 3):
                ox, oy = _other_chip(x, y, k)
                cp = _remote(ins[i].at[_half_rows(half, c)], outs[i].at[chip, _half_rows(half, c)],
                             ici_s.at[3 * i + k - 1], ici_r.at[3 * i + k - 1], (ox, oy, c))
                cp.start()
                sends.append(cp)
        for k in (1, 2, 3):
            ox, oy = _other_chip(x, y, k)
            for i in range(n):
                half = ins[i].shape[0] // 2
                piece = outs[i].at[2 * ox + oy, _half_rows(half, c)]
                _remote(piece, piece, ici_s.at[3 * i + k - 1], ici_r.at[3 * i + k - 1], (ox, oy, c)).wait_recv()
                cp = _remote(piece, piece, d2d_s.at[3 * i + k - 1], d2d_r.at[3 * i + k - 1], (x, y, 1 - c))
                cp.start()
                sends.append(cp)
        for k in (1, 2, 3):
            ox, oy = _other_chip(x, y, k)
            for i in range(n):
                half = ins[i].shape[0] // 2
                piece = outs[i].at[2 * ox + oy, _half_rows(half, 1 - c)]
                _remote(piece, piece, d2d_s.at[3 * i + k - 1], d2d_r.at[3 * i + k - 1], (x, y, 1 - c)).wait_recv()
        for cp in sends:
            cp.wait_send()

    return pl.pallas_call(
        body, name="gather_weights", in_specs=[HBM_SPEC] * n, out_specs=[HBM_SPEC] * n,
        out_shape=[jax.ShapeDtypeStruct((4,) + s.shape, s.dtype) for s in shards],
        scratch_shapes=[pltpu.SemaphoreType.DMA((3 * n,))] * 4)(*shards)


def _rs_sibling_swap(g4s):
    n = len(g4s)

    def body(*refs):
        ins, outs = refs[:n], refs[n:2 * n]
        send_sems, recv_sems = refs[2 * n:]
        x, y, c = _coords()
        sends = []
        for i in range(n):
            half = ins[i].shape[1] // 2
            for s in range(4):
                cp = _remote(ins[i].at[s, _half_rows(half, 1 - c)], outs[i].at[s], send_sems.at[4 * i + s],
                             recv_sems.at[4 * i + s], (x, y, 1 - c))
                cp.start()
                sends.append(cp)
        for cp in sends:
            cp.wait_recv()
        for cp in sends:
            cp.wait_send()

    return pl.pallas_call(
        body, name="rs_sibling_swap", in_specs=[HBM_SPEC] * n, out_specs=[HBM_SPEC] * n,
        out_shape=[jax.ShapeDtypeStruct((4, g.shape[1] // 2, g.shape[2]), g.dtype) for g in g4s],
        scratch_shapes=[pltpu.SemaphoreType.DMA((4 * n,))] * 2)(*g4s)


def _rs_add_sibling(g4, got, c_idx, name):
    _, rows, cols = g4.shape
    half = rows // 2
    tr = min(half, 256)
    nb = half // tr

    def body(c_ref, a_ref, b_ref, o_ref):
        o_ref[...] = a_ref[...] + b_ref[...]

    blk = (1, tr, cols)
    grid_spec = pltpu.PrefetchScalarGridSpec(
        num_scalar_prefetch=1, grid=(4, nb),
        in_specs=[pl.BlockSpec(blk, lambda s, i, c_ref: (s, c_ref[0] * nb + i, 0)),
                  pl.BlockSpec(blk, lambda s, i, c_ref: (s, i, 0))],
        out_specs=pl.BlockSpec(blk, lambda s, i, c_ref: (s, i, 0)))
    return pl.pallas_call(body, name=name, grid_spec=grid_spec, out_shape=jax.ShapeDtypeStruct(got.shape, F32),
                          compiler_params=_params("parallel", "parallel"))(c_idx, g4, got)


def _rs_chip_scatter(p4s):
    n = len(p4s)

    def body(*refs):
        ins, outs = refs[:n], refs[n:2 * n]
        send_sems, recv_sems = refs[2 * n:]
        x, y, c = _coords()
        sends = []
        for i in range(n):
            for k in (1, 2, 3):
                ox, oy = _other_chip(x, y, k)
                cp = _remote(ins[i].at[2 * ox + oy], outs[i].at[k - 1], send_sems.at[3 * i + k - 1],
                             recv_sems.at[3 * i + k - 1], (ox, oy, c))
                cp.start()
                sends.append(cp)
        for cp in sends:
            cp.wait_recv()
        for cp in sends:
            cp.wait_send()

    return pl.pallas_call(
        body, name="rs_chip_scatter", in_specs=[HBM_SPEC] * n, out_specs=[HBM_SPEC] * n,
        out_shape=[jax.ShapeDtypeStruct((3,) + p.shape[1:], p.dtype) for p in p4s],
        scratch_shapes=[pltpu.SemaphoreType.DMA((3 * n,))] * 2)(*p4s)


def _rs_add_chips(p4, got3, idx, name):
    _, half, cols = p4.shape
    tr = min(half, 256)
    nb = half // tr

    def body(idx_ref, p_ref, a_ref, b_ref, c_ref, o_ref):
        o_ref[...] = ((p_ref[0] + a_ref[0]) + b_ref[0]) + c_ref[0]

    blk = (1, tr, cols)
    grid_spec = pltpu.PrefetchScalarGridSpec(
        num_scalar_prefetch=1, grid=(nb,),
        in_specs=[pl.BlockSpec(blk, lambda i, idx_ref: (idx_ref[0], i, 0))]
        + [pl.BlockSpec(blk, functools.partial(lambda k, i, idx_ref: (k, i, 0), k)) for k in range(3)],
        out_specs=pl.BlockSpec((tr, cols), lambda i, idx_ref: (idx_ref[1] * nb + i, 0)))
    return pl.pallas_call(body, name=name, grid_spec=grid_spec, out_shape=jax.ShapeDtypeStruct((2 * half, cols), F32),
                          compiler_params=_params("parallel"))(idx, p4, got3, got3, got3)


def _rs_share_halves(bufs):
    n = len(bufs)

    def body(*refs):
        ins, outs = refs[:n], refs[n:2 * n]
        send_sems, recv_sems = refs[2 * n:]
        x, y, c = _coords()
        sends = []
        for i in range(n):
            half = ins[i].shape[0] // 2
            cp = _remote(ins[i].at[_half_rows(half, c)], outs[i].at[_half_rows(half, c)], send_sems.at[i],
                         recv_sems.at[i], (x, y, 1 - c))
            cp.start()
            sends.append(cp)
        for i in range(n):
            half = ins[i].shape[0] // 2
            _remote(ins[i].at[_half_rows(half, c)], outs[i].at[_half_rows(half, 1 - c)], send_sems.at[i],
                    recv_sems.at[i], (x, y, 1 - c)).wait_recv()
        for cp in sends:
            cp.wait_send()

    return pl.pallas_call(
        body, name="rs_share_halves", in_specs=[HBM_SPEC] * n, out_specs=[HBM_SPEC] * n,
        out_shape=[jax.ShapeDtypeStruct(b.shape, b.dtype) for b in bufs],
        input_output_aliases={i: i for i in range(n)},
        scratch_shapes=[pltpu.SemaphoreType.DMA((n,))] * 2)(*bufs)


def _reduce_scatter(g4s, names):
    x, y, c = _coords()
    c_idx = jnp.stack([c]).astype(jnp.int32)
    idx = jnp.stack([2 * x + y, c]).astype(jnp.int32)
    got = _rs_sibling_swap(g4s)
    p4s = [_rs_add_sibling(g, s, c_idx, f"rs_add_sibling_{nm}") for g, s, nm in zip(g4s, got, names)]
    got3 = _rs_chip_scatter(p4s)
    bufs = [_rs_add_chips(p, t, idx, f"rs_add_chips_{nm}") for p, t, nm in zip(p4s, got3, names)]
    return _rs_share_halves(bufs)


def _cols_join(s4):
    return s4.transpose(1, 0, 2).reshape(s4.shape[1], 4 * s4.shape[2])


def _cols_split(full):
    r, c4 = full.shape
    return full.reshape(r, 4, c4 // 4).transpose(1, 0, 2)


_BIG = {
    "w_in_even": ((1024, 2052), _cols_join, _cols_split),
    "w_out_even": ((512, 1024), lambda s: s.reshape(2048, 1024), lambda f: f.reshape(4, 512, 1024)),
    "w_in_odd": ((1024, 2048), lambda s: s, lambda f: f),
    "w_out_odd": ((512, 1024), lambda s: s.reshape(2048, 1024), lambda f: f.reshape(4, 512, 1024)),
    "w_pl": ((512, 256), lambda s: s.reshape(4, 2, 256, 256).transpose(1, 2, 0, 3).reshape(2, 256, 1024),
             lambda f: f.reshape(2, 256, 4, 256).transpose(2, 0, 1, 3).reshape(4, 512, 256)),
    "w_pl_gate": ((512, 1024), lambda s: s.reshape(4, 2, 256, 1024).transpose(1, 0, 2, 3).reshape(2, 1024, 1024),
                  lambda f: f.reshape(2, 4, 256, 1024).transpose(1, 0, 2, 3).reshape(4, 512, 1024)),
}


def _size(shape):
    n = 1
    for d in shape:
        n *= d
    return n


_SMALL = {"a_log": (1, 8), "dt_bias": (1, 8), "gdn_norm_g": (1, 128), "hgrn_norm_g": (1, 128),
          "lower_bounds": (2, 2048), "ln_g": (2, 1024), "ln_b": (2, 1024), "conv_a_w": (3, 1024), "conv_b_w": (4, 3072)}
_CONV_SHARD = {"conv_a_w": (3, 256), "conv_b_w": (4, 768)}


def _pack_small(parts, shapes, head_rows=0):
    rows = []
    for n, shape in shapes.items():
        v = parts[n].reshape(-1)
        rows.append(jnp.pad(v, (0, -v.shape[0] % LANES)).reshape(-1, LANES))
    buf = jnp.concatenate(rows, axis=0)
    return jnp.pad(buf, ((head_rows, -(buf.shape[0] + head_rows) % SUBLANES), (0, 0)))


def _unpack_small(buf, shapes, head_rows=0):
    out, off = {}, head_rows
    for n, shape in shapes.items():
        nrow = -(-_size(shape) // LANES)
        out[n] = buf[off:off + nrow].reshape(-1)[:_size(shape)].reshape(shape)
        off += nrow
    return out


_WEIGHTS = ["w_in_even", "conv_a_w", "conv_b_w", "a_log", "dt_bias", "gdn_norm_g", "w_out_even", "w_in_odd",
            "lower_bounds", "hgrn_norm_g", "w_out_odd", "ln_g", "ln_b", "w_pl", "w_pl_gate"]


def kernel(x, p, w_in_even, conv_a_w, conv_b_w, a_log, dt_bias, gdn_norm_g, w_out_even, w_in_odd, lower_bounds, hgrn_norm_g, w_out_odd, ln_g, ln_b, w_pl, w_pl_gate, loss_target, m_w_in_even, m_conv_a_w, m_conv_b_w, m_a_log, m_dt_bias, m_gdn_norm_g, m_w_out_even, m_w_in_odd, m_lower_bounds, m_hgrn_norm_g, m_w_out_odd, m_ln_g, m_ln_b, m_w_pl, m_w_pl_gate, v_w_in_even, v_conv_a_w, v_conv_b_w, v_a_log, v_dt_bias, v_gdn_norm_g, v_w_out_even, v_w_in_odd, v_lower_bounds, v_hgrn_norm_g, v_w_out_odd, v_ln_g, v_ln_b, v_w_pl, v_w_pl_gate):
    w = dict(zip(_WEIGHTS, (w_in_even, conv_a_w, conv_b_w, a_log, dt_bias, gdn_norm_g, w_out_even, w_in_odd,
                            lower_bounds, hgrn_norm_g, w_out_odd, ln_g, ln_b, w_pl, w_pl_gate)))
    m = dict(zip(_WEIGHTS, (m_w_in_even, m_conv_a_w, m_conv_b_w, m_a_log, m_dt_bias, m_gdn_norm_g, m_w_out_even,
                            m_w_in_odd, m_lower_bounds, m_hgrn_norm_g, m_w_out_odd, m_ln_g, m_ln_b, m_w_pl, m_w_pl_gate)))
    v = dict(zip(_WEIGHTS, (v_w_in_even, v_conv_a_w, v_conv_b_w, v_a_log, v_dt_bias, v_gdn_norm_g, v_w_out_even,
                            v_w_in_odd, v_lower_bounds, v_hgrn_norm_g, v_w_out_odd, v_ln_g, v_ln_b, v_w_pl, v_w_pl_gate)))
    chip = 2 * lax.axis_index("x") + lax.axis_index("y")

    names = list(_BIG)
    shard_shapes = {n: _BIG[n][0] for n in names}
    shards = [w[n].reshape(shard_shapes[n]).astype(BF16) for n in names]
    gathered = _gather_shards(shards)
    full = {n: _BIG[n][1](lax.dynamic_update_slice(ga, sh[None], (chip, 0, 0)))
            for n, ga, sh in zip(names, gathered, shards)}
    conv_mine = _pack_small({n: w[n] for n in _CONV_SHARD}, _CONV_SHARD)
    conv_all = _exchange_small(conv_mine, False, "gather_conv")
    conv_by_chip = [_unpack_small(conv_all[2 * s], _CONV_SHARD) for s in range(4)]
    for n in _CONV_SHARD:
        full[n] = jnp.concatenate([conv_by_chip[s][n] for s in range(4)], axis=1)
    for n in _SMALL:
        if n not in _CONV_SHARD:
            full[n] = w[n]

    loss_part, dx, g = _local_step(x[0], p[:, 0], loss_target[0], full)

    g_big = dict(zip(names, _reduce_scatter([_BIG[n][2](g[n]) for n in names], names)))
    small_sum = _exchange_small(jnp.concatenate([loss_part, _pack_small(g, _SMALL)], axis=0), True, "reduce_small")
    loss = small_sum[0, 0]
    g_small = _unpack_small(small_sum, _SMALL, head_rows=SUBLANES)
    for n, (rows, cols) in _CONV_SHARD.items():
        g_small[n] = lax.dynamic_slice_in_dim(g_small[n], chip * cols, cols, axis=1)

    grads, delta, new_m, new_v = {}, {}, {}, {}
    for n in _BIG:
        shape2 = (-1, shard_shapes[n][-1])
        grads[n] = g_big[n].reshape(w[n].shape)
        d_, m_, v_ = _adamw(w[n].reshape(shape2), g_big[n].reshape(shape2), m[n].reshape(shape2), v[n].reshape(shape2),
                            f"adamw_{n}")
        delta[n], new_m[n], new_v[n] = (t.reshape(w[n].shape) for t in (d_, m_, v_))
    own = {n: (_CONV_SHARD[n] if n in _CONV_SHARD else _SMALL[n]) for n in _SMALL}
    packs = [_pack_small({n: src[n] for n in _SMALL}, own) for src in (w, g_small, m, v)]
    outs = [_unpack_small(t, own) for t in _adamw(*packs, "adamw_small")]
    for n in _SMALL:
        grads[n] = g_small[n].reshape(w[n].shape)
        delta[n], new_m[n], new_v[n] = (t[n].reshape(w[n].shape) for t in outs)
    return (loss, dx[None], *[grads[n] for n in _WEIGHTS], *[delta[n] for n in _WEIGHTS],
            *[new_m[n] for n in _WEIGHTS], *[new_v[n] for n in _WEIGHTS])
```

```python
import functools

import jax
import jax.numpy as jnp
from jax import lax
from jax.experimental import pallas as pl
from jax.experimental.pallas import tpu as pltpu

F32 = jnp.float32
BF16 = jnp.bfloat16
HI = lax.Precision.HIGHEST

D_MODEL = 1024
PL_DIM = 256
GDN_HEADS = 8
HEAD_DIM = 128
GDN_CHUNK = 64
HGRN_HEADS = 16
HGRN_CHUNK = 32
HGRN_WIDTH = 2048
DEEPNORM_ALPHA = 4.0 ** 0.25
NORM_EPS = 1e-5
ADAM_LR, ADAM_B1, ADAM_B2, ADAM_EPS, ADAM_WD, ADAM_STEP = 0.001, 0.9, 0.999, 1e-08, 0.01, 10

VMEM_LIMIT = 56 * 1024 * 1024
SUBLANES = 8
LANES = 128

_DIMS = {"nn": (((1,), (0,)), ((), ())), "nt": (((1,), (1,)), ((), ())), "tn": (((0,), (0,)), ((), ()))}


def _params(*sem):
    return pltpu.CompilerParams(dimension_semantics=sem, vmem_limit_bytes=VMEM_LIMIT)


ONE_PASS, THREE_PASS, FULL_F32 = 0, 1, 2


def _mm_raw(a, b, kind, prec):
    nb = a.ndim - 2
    ca = a.ndim - 1 if kind[0] == "n" else a.ndim - 2
    cb = b.ndim - 2 if kind[1] == "n" else b.ndim - 1
    dims = (((ca,), (cb,)), (tuple(range(nb)),) * 2)
    if prec == FULL_F32:
        return lax.dot_general(a, b, dims, precision=HI, preferred_element_type=F32)
    dot = lambda p, q: lax.dot_general(p, q, dims, preferred_element_type=F32)
    ah, bh = a.astype(BF16), b.astype(BF16)
    if prec == ONE_PASS:
        return dot(ah, bh)
    al = (a - ah.astype(F32)).astype(BF16)
    bl = (b - bh.astype(F32)).astype(BF16)
    return dot(ah, bh) + (dot(ah, bl) + dot(al, bh))


@functools.partial(jax.custom_vjp, nondiff_argnums=(2, 3))
def _mm_vjp(a, b, kind, hi):
    return _mm_raw(a, b, kind, hi)


def _mm_vjp_fwd(a, b, kind, hi):
    return _mm_raw(a, b, kind, hi), (a, b)


def _mm_vjp_bwd(kind, hi, res, dc):
    a, b = res
    if kind == "nn":
        return _mm_raw(dc, b, "nt", hi), _mm_raw(a, dc, "tn", hi)
    if kind == "nt":
        return _mm_raw(dc, b, "nn", hi), _mm_raw(dc, a, "tn", hi)
    return _mm_raw(b, dc, "nt", hi), _mm_raw(a, dc, "nn", hi)


_mm_vjp.defvjp(_mm_vjp_fwd, _mm_vjp_bwd)


def _lane_total(v):
    return jnp.broadcast_to(jnp.sum(v, axis=-1, keepdims=True), v.shape)


def _matmul(a, b, *, name, ta=False, tb=False, out_dtype=F32, add=None, add_scale=1.0, tm=512, tn=1024, tk=1024):
    m, k = (a.shape[1], a.shape[0]) if ta else a.shape
    n = b.shape[0] if tb else b.shape[1]
    tm, tn, tk = min(tm, m), min(tn, n), min(tk, k)
    assert m % tm == 0 and n % tn == 0 and k % tk == 0, (name, m, n, k)
    nk = k // tk
    kind = ("t" if ta else "n") + ("t" if tb else "n")
    dims = (((0 if ta else 1,), (1 if tb else 0,)), ((), ()))

    def body(*refs):
        if add is None:
            a_ref, b_ref, o_ref, acc_ref = refs
        else:
            a_ref, b_ref, add_ref, o_ref, acc_ref = refs
        kk = pl.program_id(2)

        @pl.when(kk == 0)
        def _():
            acc_ref[...] = jnp.zeros_like(acc_ref)

        acc_ref[...] += lax.dot_general(a_ref[...].astype(BF16), b_ref[...].astype(BF16), dims,
                                        preferred_element_type=F32)

        @pl.when(kk == nk - 1)
        def _():
            r = acc_ref[...]
            if add is not None:
                r = r + add_scale * add_ref[...].astype(F32)
            o_ref[...] = r.astype(out_dtype)

    a_spec = pl.BlockSpec((tk, tm), lambda i, j, kk: (kk, i)) if ta else pl.BlockSpec((tm, tk), lambda i, j, kk: (i, kk))
    b_spec = pl.BlockSpec((tn, tk), lambda i, j, kk: (j, kk)) if tb else pl.BlockSpec((tk, tn), lambda i, j, kk: (kk, j))
    o_spec = pl.BlockSpec((tm, tn), lambda i, j, kk: (i, j))
    in_specs = [a_spec, b_spec] + ([o_spec] if add is not None else [])
    args = (a, b) + ((add,) if add is not None else ())
    del kind
    return pl.pallas_call(
        body, name=name, grid=(m // tm, n // tn, nk), in_specs=in_specs, out_specs=o_spec,
        out_shape=jax.ShapeDtypeStruct((m, n), out_dtype), scratch_shapes=[pltpu.VMEM((tm, tn), F32)],
        compiler_params=_params("parallel", "parallel", "arbitrary"))(*args)


HALO = SUBLANES


def _halo_specs(tt, width, col, nt):
    r = tt // HALO
    prev = pl.BlockSpec((HALO, width), lambda i: (jnp.maximum(i * r - 1, 0), col))
    nxt = pl.BlockSpec((HALO, width), lambda i: (jnp.minimum((i + 1) * r, nt * r - 1), col))
    return prev, nxt


def _shift_down(ext, k):
    return ext if k == 0 else pltpu.roll(ext, k, 0)


def _shift_up(ext, k):
    return ext if k == 0 else pltpu.roll(ext, ext.shape[0] - k, 0)


def _causal_conv(ext, w, taps):
    acc = None
    for j in range(taps):
        term = w[j:j + 1, :] * _shift_down(ext, taps - 1 - j)
        acc = term if acc is None else acc + term
    return acc[HALO:, :]


def _conv_a_fwd(proj_a, conv_w):
    t = proj_a.shape[0]
    tt = min(t, 256)
    nt = t // tt
    wdt = 1024

    def body(cur_ref, prev_ref, w_ref, y_ref):
        i = pl.program_id(0)
        cur = cur_ref[...]
        h, c, b, z = (cur[:, k * wdt:(k + 1) * wdt] for k in range(4))
        prev = prev_ref[...]
        u_prev = jnp.where(i > 0, prev[:, wdt:2 * wdt] * prev[:, 0:wdt], 0.0)
        ext = jnp.concatenate([u_prev, c * h], axis=0)
        conv = _causal_conv(ext, w_ref[...], 3)
        y_ref[...] = (b * conv * jax.nn.silu(z)).astype(BF16)

    prev_spec, _ = _halo_specs(tt, 4 * wdt, 0, nt)
    return pl.pallas_call(
        body, name="conv_a_fwd", grid=(nt,),
        in_specs=[pl.BlockSpec((tt, 4 * wdt), lambda i: (i, 0)), prev_spec, pl.BlockSpec((3, wdt), lambda i: (0, 0))],
        out_specs=pl.BlockSpec((tt, wdt), lambda i: (i, 0)),
        out_shape=jax.ShapeDtypeStruct((t, wdt), BF16), compiler_params=_params("parallel"))(proj_a, proj_a, conv_w)


def _conv_a_bwd(proj_a, conv_w, dy):
    t = proj_a.shape[0]
    tt = min(t, 256)
    nt = t // tt
    wdt = 1024

    def body(cur_ref, prev_ref, nxt_ref, w_ref, dy_ref, dyn_ref, d_ref, dw_ref):
        i = pl.program_id(0)
        w = w_ref[...]
        cur, prev, nxt = cur_ref[...], prev_ref[...], nxt_ref[...]
        split = lambda a: tuple(a[:, k * wdt:(k + 1) * wdt] for k in range(4))
        h, c, b, z = split(cur)
        hp, cp, _, _ = split(prev)
        hn, cn, bn, zn = split(nxt)
        u_prev = jnp.where(i > 0, cp * hp, 0.0)
        u_ext = jnp.concatenate([u_prev, c * h, cn * hn], axis=0)
        taps = [_shift_down(u_ext, 2 - j)[HALO:, :] for j in range(3)]
        conv = w[0:1, :] * taps[0] + w[1:2, :] * taps[1] + w[2:3, :] * taps[2]
        b_cn = jnp.concatenate([b, bn], axis=0)
        z_cn = jnp.concatenate([z, zn], axis=0)
        dy_cn = jnp.concatenate([dy_ref[...], jnp.where(i < nt - 1, dyn_ref[...], 0.0)], axis=0)
        sg = jax.nn.sigmoid(z_cn)
        silu = z_cn * sg
        d_conv = dy_cn * b_cn * silu
        db = (dy_cn * conv * silu)[:tt, :]
        dz = (dy_cn * b_cn * conv * (sg * (1.0 + z_cn * (1.0 - sg))))[:tt, :]
        du = None
        for j in range(3):
            term = w[j:j + 1, :] * _shift_up(d_conv, 2 - j)
            du = term if du is None else du + term
        du = du[:tt, :]
        d_ref[...] = jnp.concatenate([du * c, du * h, db, dz], axis=1).astype(BF16)

        @pl.when(i == 0)
        def _():
            dw_ref[...] = jnp.zeros_like(dw_ref)

        d_cur = d_conv[:tt, :]
        rows = [jnp.sum(d_cur * taps[j][:tt, :], axis=0, keepdims=True) for j in range(3)]
        dw_ref[0:3, :] += jnp.concatenate(rows, axis=0)

    prev_spec, nxt_spec = _halo_specs(tt, 4 * wdt, 0, nt)
    _, dyn_spec = _halo_specs(tt, wdt, 0, nt)
    return pl.pallas_call(
        body, name="conv_a_bwd", grid=(nt,),
        in_specs=[pl.BlockSpec((tt, 4 * wdt), lambda i: (i, 0)), prev_spec, nxt_spec,
                  pl.BlockSpec((3, wdt), lambda i: (0, 0)), pl.BlockSpec((tt, wdt), lambda i: (i, 0)), dyn_spec],
        out_specs=[pl.BlockSpec((tt, 4 * wdt), lambda i: (i, 0)), pl.BlockSpec((SUBLANES, wdt), lambda i: (0, 0))],
        out_shape=[jax.ShapeDtypeStruct((t, 4 * wdt), BF16), jax.ShapeDtypeStruct((SUBLANES, wdt), F32)],
        compiler_params=_params("arbitrary"))(proj_a, proj_a, proj_a, conv_w, dy, dy)


def _conv_b_fwd(proj_qkv, conv_w):
    t, width = proj_qkv.shape
    tt = min(t, 256)
    nt = t // tt
    wdt = 1024

    def body(cur_ref, prev_ref, w_ref, y_ref):
        i = pl.program_id(1)
        ext = jnp.concatenate([jnp.where(i > 0, prev_ref[...], 0.0), cur_ref[...]], axis=0)
        y_ref[...] = jax.nn.silu(_causal_conv(ext, w_ref[...], 4))

    r = tt // HALO
    return pl.pallas_call(
        body, name="conv_b_fwd", grid=(width // wdt, nt),
        in_specs=[pl.BlockSpec((tt, wdt), lambda j, i: (i, j)),
                  pl.BlockSpec((HALO, wdt), lambda j, i: (jnp.maximum(i * r - 1, 0), j)),
                  pl.BlockSpec((4, wdt), lambda j, i: (0, j))],
        out_specs=pl.BlockSpec((tt, wdt), lambda j, i: (i, j)),
        out_shape=jax.ShapeDtypeStruct((t, width), F32), compiler_params=_params("parallel", "parallel"))(
            proj_qkv, proj_qkv, conv_w)


def _conv_b_bwd(proj_qkv, conv_w, d_act, col, name):
    t = proj_qkv.shape[0]
    tt = min(t, 256)
    nt = t // tt
    wdt = 1024

    def body(cur_ref, prev_ref, nxt_ref, w_ref, da_ref, dan_ref, d_ref, dw_ref):
        i = pl.program_id(0)
        w = w_ref[...]
        u_ext = jnp.concatenate([jnp.where(i > 0, prev_ref[...], 0.0), cur_ref[...], nxt_ref[...]], axis=0)
        taps = [_shift_down(u_ext, 3 - j)[HALO:, :] for j in range(4)]
        conv = w[0:1, :] * taps[0] + w[1:2, :] * taps[1] + w[2:3, :] * taps[2] + w[3:4, :] * taps[3]
        da_cn = jnp.concatenate([da_ref[...], jnp.where(i < nt - 1, dan_ref[...], 0.0)], axis=0)
        sg = jax.nn.sigmoid(conv)
        d_conv = da_cn * (sg * (1.0 + conv * (1.0 - sg)))
        du = None
        for j in range(4):
            term = w[j:j + 1, :] * _shift_up(d_conv, 3 - j)
            du = term if du is None else du + term
        d_ref[...] = du[:tt, :].astype(BF16)

        @pl.when(i == 0)
        def _():
            dw_ref[...] = jnp.zeros_like(dw_ref)

        d_cur = d_conv[:tt, :]
        rows = [jnp.sum(d_cur * taps[j][:tt, :], axis=0, keepdims=True) for j in range(4)]
        dw_ref[0:4, :] += jnp.concatenate(rows, axis=0)

    prev_spec, nxt_spec = _halo_specs(tt, wdt, col, nt)
    _, dan_spec = _halo_specs(tt, wdt, 0, nt)
    return pl.pallas_call(
        body, name=name, grid=(nt,),
        in_specs=[pl.BlockSpec((tt, wdt), lambda i: (i, col)), prev_spec, nxt_spec,
                  pl.BlockSpec((4, wdt), lambda i: (0, col)), pl.BlockSpec((tt, wdt), lambda i: (i, 0)), dan_spec],
        out_specs=[pl.BlockSpec((tt, wdt), lambda i: (i, 0)), pl.BlockSpec((SUBLANES, wdt), lambda i: (0, 0))],
        out_shape=[jax.ShapeDtypeStruct((t, wdt), BF16), jax.ShapeDtypeStruct((SUBLANES, wdt), F32)],
        compiler_params=_params("arbitrary"))(proj_qkv, proj_qkv, proj_qkv, conv_w, d_act, d_act)


def _rms_gate(o, gn, z):
    on = o * lax.rsqrt(jnp.mean(o * o, axis=-1, keepdims=True) + NORM_EPS) * gn
    return on * jax.nn.silu(z)


def _gdn_chunk(mm, qa, ka, va, z, braw, araw, alog, dtb, gn, state):
    c = GDN_CHUNK
    q = qa * lax.rsqrt(jnp.sum(qa * qa, axis=-1, keepdims=True) + 1e-6) * (HEAD_DIM ** -0.5)
    k = ka * lax.rsqrt(jnp.sum(ka * ka, axis=-1, keepdims=True) + 1e-6)
    beta = jax.nn.sigmoid(braw)
    g = -jnp.exp(alog) * jax.nn.softplus(araw + dtb)
    ri = lax.broadcasted_iota(jnp.int32, (c, c), 0)
    ci = lax.broadcasted_iota(jnp.int32, (c, c), 1)
    incl, strict, eye = ri >= ci, ri > ci, ri == ci
    gc = mm(incl.astype(F32), g, "nn", True)
    gc_i = gc[:, :c]
    gc_j = mm(jnp.ones((c, c), F32), jnp.where(eye, gc_i, 0.0), "nn", True)
    decay = jnp.where(incl, jnp.exp(jnp.where(incl, gc_i - gc_j, 0.0)), 0.0)
    kb = k * beta
    low = jnp.where(strict, mm(kb, k, "nt", False) * decay, 0.0)
    x = -low
    inv = eye.astype(F32) + x
    for _ in range(5):
        x = mm(x, x, "nn", True)
        inv = inv + mm(inv, x, "nn", True)
    egc = jnp.exp(gc)
    u = mm(inv, va * beta, "nn", True)
    w = mm(inv, kb * egc, "nn", True)
    attn = jnp.where(incl, mm(q, k, "nt", False) * decay, 0.0)
    g_last = jnp.sum(g, axis=0, keepdims=True)
    v_new = u - mm(w, state, "nn", False)
    o = mm(q * egc, state, "nn", False) + mm(attn, v_new, "nn", False)
    new_state = state * jnp.exp(g_last) + mm(k * jnp.exp(g_last - gc), v_new, "tn", False)
    return _rms_gate(o, gn, z), new_state


def _gdn_specs(tb, rev, nt):
    ti = (lambda i: nt - 1 - i) if rev else (lambda i: i)
    col = lambda off: pl.BlockSpec((tb, HEAD_DIM), lambda h, i: (ti(i), off + h))
    rep = pl.BlockSpec((1, tb, LANES), lambda h, i: (h, ti(i), 0))
    par = pl.BlockSpec((1, SUBLANES, LANES), lambda h, i: (h, 0, 0))
    gn = pl.BlockSpec((SUBLANES, LANES), lambda h, i: (0, 0))
    hist = pl.BlockSpec((1, tb // GDN_CHUNK, HEAD_DIM, HEAD_DIM), lambda h, i: (h, ti(i), 0, 0))
    return col, rep, par, gn, hist


def _gdn_fwd(qkv_act, zb, braw, araw, alog, dtb, gn):
    t = qkv_act.shape[0]
    tb = min(t, 256)
    nt, nc = t // tb, tb // GDN_CHUNK

    def body(q_ref, k_ref, v_ref, z_ref, br_ref, ar_ref, al_ref, dt_ref, gn_ref, y_ref, hist_ref, s_ref):
        @pl.when(pl.program_id(1) == 0)
        def _():
            s_ref[...] = jnp.zeros_like(s_ref)

        al, dt, g = al_ref[0, 0:1, :], dt_ref[0, 0:1, :], gn_ref[0:1, :]
        state = s_ref[...]
        for c in range(nc):
            sl = pl.ds(c * GDN_CHUNK, GDN_CHUNK)
            hist_ref[0, c] = state
            y, state = _gdn_chunk(_mm_raw, q_ref[sl, :], k_ref[sl, :], v_ref[sl, :], z_ref[sl, :],
                                  br_ref[0, sl, :], ar_ref[0, sl, :], al, dt, g, state)
            y_ref[sl, :] = y.astype(BF16)
        s_ref[...] = state

    col, rep, par, gns, hist = _gdn_specs(tb, False, nt)
    return pl.pallas_call(
        body, name="gdn_fwd", grid=(GDN_HEADS, nt),
        in_specs=[col(0), col(GDN_HEADS), col(2 * GDN_HEADS), col(0), rep, rep, par, par, gns],
        out_specs=[col(0), hist],
        out_shape=[jax.ShapeDtypeStruct((t, GDN_HEADS * HEAD_DIM), BF16),
                   jax.ShapeDtypeStruct((GDN_HEADS, t // GDN_CHUNK, HEAD_DIM, HEAD_DIM), F32)],
        scratch_shapes=[pltpu.VMEM((HEAD_DIM, HEAD_DIM), F32)],
        compiler_params=_params("parallel", "arbitrary"))(qkv_act, qkv_act, qkv_act, zb, braw, araw, alog, dtb, gn)


def _gdn_bwd(qkv_act, zb, braw, araw, alog, dtb, gn, hist, dy):
    t = qkv_act.shape[0]
    tb = min(t, 256)
    nt, nc = t // tb, tb // GDN_CHUNK

    def body(q_ref, k_ref, v_ref, z_ref, br_ref, ar_ref, al_ref, dt_ref, gn_ref, hist_ref, dy_ref,
             dq_ref, dk_ref, dv_ref, dz_ref, dbr_ref, dar_ref, dal_ref, ddt_ref, dgn_ref, ds_ref):
        @pl.when(pl.program_id(1) == 0)
        def _():
            ds_ref[...] = jnp.zeros_like(ds_ref)
            dal_ref[...] = jnp.zeros_like(dal_ref)
            ddt_ref[...] = jnp.zeros_like(ddt_ref)
            dgn_ref[...] = jnp.zeros_like(dgn_ref)

        al, dt, g = al_ref[0, 0:1, :], dt_ref[0, 0:1, :], gn_ref[0:1, :]
        d_state = ds_ref[...]
        for c in reversed(range(nc)):
            sl = pl.ds(c * GDN_CHUNK, GDN_CHUNK)
            _, vjp = jax.vjp(functools.partial(_gdn_chunk, _mm_vjp), q_ref[sl, :], k_ref[sl, :], v_ref[sl, :],
                             z_ref[sl, :], br_ref[0, sl, :], ar_ref[0, sl, :], al, dt, g, hist_ref[0, c])
            dq, dk, dv, dz, dbr, dar, dal, ddt, dgn, d_state = vjp((dy_ref[sl, :], d_state))
            dq_ref[sl, :] = dq
            dk_ref[sl, :] = dk
            dv_ref[sl, :] = dv
            dz_ref[sl, :] = dz.astype(BF16)
            dbr_ref[0, sl, :] = _lane_total(dbr)
            dar_ref[0, sl, :] = _lane_total(dar)
            dal_ref[0, 0:1, :] += _lane_total(dal)
            ddt_ref[0, 0:1, :] += _lane_total(ddt)
            dgn_ref[0, 0:1, :] += dgn
        ds_ref[...] = d_state

    col, rep, par, gns, hists = _gdn_specs(tb, True, nt)
    width = GDN_HEADS * HEAD_DIM
    return pl.pallas_call(
        body, name="gdn_bwd", grid=(GDN_HEADS, nt),
        in_specs=[col(0), col(GDN_HEADS), col(2 * GDN_HEADS), col(0), rep, rep, par, par, gns, hists, col(0)],
        out_specs=[col(0), col(0), col(0), col(0), rep, rep, par, par, par],
        out_shape=[jax.ShapeDtypeStruct((t, width), F32)] * 3 + [jax.ShapeDtypeStruct((t, width), BF16)]
        + [jax.ShapeDtypeStruct((GDN_HEADS, t, LANES), F32)] * 2
        + [jax.ShapeDtypeStruct((GDN_HEADS, SUBLANES, LANES), F32)] * 3,
        scratch_shapes=[pltpu.VMEM((HEAD_DIM, HEAD_DIM), F32)],
        compiler_params=_params("parallel", "arbitrary"))(qkv_act, qkv_act, qkv_act, zb, braw, araw, alog, dtb, gn, hist, dy)


def _hgrn_chunk(mm, qr, fr, iv, z, lbl, gn, state):
    c = HGRN_CHUNK
    lb = jax.nn.sigmoid(lbl[1:2, :] - lbl[0:1, :])
    f = lb + (1.0 - lb) * jax.nn.sigmoid(fr)
    q = jax.nn.silu(qr)
    k = 1.0 - f
    logf = jnp.log(f)
    ri = lax.broadcasted_iota(jnp.int32, (c, c), 0)
    ci = lax.broadcasted_iota(jnp.int32, (c, c), 1)
    b = mm((ri >= ci).astype(F32), logf, "nn", True)
    ti = lax.broadcasted_iota(jnp.int32, (c, c, HEAD_DIM), 0)
    si = lax.broadcasted_iota(jnp.int32, (c, c, HEAD_DIM), 1)
    m3 = ti >= si
    decay = jnp.where(m3, jnp.exp(jnp.where(m3, b[:, None, :] - b[None, :, :], 0.0)), 0.0)
    attn = jnp.sum(q[:, None, :] * k[None, :, :] * decay, axis=-1)
    o = mm(q * jnp.exp(b), state, "nt", False) + mm(attn, iv, "nn", False)
    b_last = jnp.sum(logf, axis=0, keepdims=True)
    new_state = state * jnp.exp(b_last) + mm(iv, k * jnp.exp(b_last - b), "tn", False)
    return _rms_gate(o, gn, z), new_state


def _hgrn_specs(tb, rev, nt):
    ti = (lambda i: nt - 1 - i) if rev else (lambda i: i)
    col = lambda off: pl.BlockSpec((tb, HEAD_DIM), lambda h, i: (ti(i), off + h))
    lbs = pl.BlockSpec((2, HEAD_DIM), lambda h, i: (0, h))
    gn = pl.BlockSpec((SUBLANES, LANES), lambda h, i: (0, 0))
    hist = pl.BlockSpec((1, tb // HGRN_CHUNK, HEAD_DIM, HEAD_DIM), lambda h, i: (h, ti(i), 0, 0))
    return col, lbs, gn, hist


def _hgrn_fwd(proj4, lower_bounds, gn):
    t = proj4[0].shape[0]
    tb = min(t, 256)
    nt, nc = t // tb, tb // HGRN_CHUNK

    def body(q_ref, f_ref, i_ref, z_ref, lb_ref, gn_ref, y_ref, hist_ref, s_ref):
        @pl.when(pl.program_id(1) == 0)
        def _():
            s_ref[...] = jnp.zeros_like(s_ref)

        lbl, g = lb_ref[...], gn_ref[0:1, :]
        state = s_ref[...]
        for c in range(nc):
            sl = pl.ds(c * HGRN_CHUNK, HGRN_CHUNK)
            hist_ref[0, c] = state
            y, state = _hgrn_chunk(_mm_raw, q_ref[sl, :], f_ref[sl, :], i_ref[sl, :], z_ref[sl, :], lbl, g, state)
            y_ref[sl, :] = y.astype(BF16)
        s_ref[...] = state

    col, lbs, gns, hist = _hgrn_specs(tb, False, nt)
    hh = HGRN_HEADS
    return pl.pallas_call(
        body, name="hgrn_fwd", grid=(hh, nt),
        in_specs=[col(0), col(0), col(0), col(0), lbs, gns],
        out_specs=[col(0), hist],
        out_shape=[jax.ShapeDtypeStruct((t, HGRN_WIDTH), BF16),
                   jax.ShapeDtypeStruct((hh, t // HGRN_CHUNK, HEAD_DIM, HEAD_DIM), F32)],
        scratch_shapes=[pltpu.VMEM((HEAD_DIM, HEAD_DIM), F32)],
        compiler_params=_params("parallel", "arbitrary"))(*proj4, lower_bounds, gn)


def _hgrn_bwd(proj4, lower_bounds, gn, hist, dy):
    t = proj4[0].shape[0]
    tb = min(t, 256)
    nt, nc = t // tb, tb // HGRN_CHUNK

    def body(q_ref, f_ref, i_ref, z_ref, lb_ref, gn_ref, hist_ref, dy_ref,
             dq_ref, df_ref, di_ref, dz_ref, dlb_ref, dgn_ref, ds_ref):
        @pl.when(pl.program_id(1) == 0)
        def _():
            ds_ref[...] = jnp.zeros_like(ds_ref)
            dlb_ref[...] = jnp.zeros_like(dlb_ref)
            dgn_ref[...] = jnp.zeros_like(dgn_ref)

        lbl, g = lb_ref[...], gn_ref[0:1, :]
        d_state = ds_ref[...]
        for c in reversed(range(nc)):
            sl = pl.ds(c * HGRN_CHUNK, HGRN_CHUNK)
            _, vjp = jax.vjp(functools.partial(_hgrn_chunk, _mm_vjp), q_ref[sl, :], f_ref[sl, :], i_ref[sl, :],
                             z_ref[sl, :], lbl, g, hist_ref[0, c])
            dq, df, di, dz, dlb, dgn, d_state = vjp((dy_ref[sl, :], d_state))
            dq_ref[sl, :] = dq.astype(BF16)
            df_ref[sl, :] = df.astype(BF16)
            di_ref[sl, :] = di.astype(BF16)
            dz_ref[sl, :] = dz.astype(BF16)
            dlb_ref[...] += dlb
            dgn_ref[0, 0:1, :] += dgn
        ds_ref[...] = d_state

    col, lbs, gns, hists = _hgrn_specs(tb, True, nt)
    hh = HGRN_HEADS
    par = pl.BlockSpec((1, SUBLANES, LANES), lambda h, i: (h, 0, 0))
    return pl.pallas_call(
        body, name="hgrn_bwd", grid=(hh, nt),
        in_specs=[col(0), col(0), col(0), col(0), lbs, gns, hists, col(0)],
        out_specs=[col(0), col(0), col(0), col(0), lbs, par],
        out_shape=[jax.ShapeDtypeStruct((t, HGRN_WIDTH), BF16)] * 4
        + [jax.ShapeDtypeStruct((2, HGRN_WIDTH), F32), jax.ShapeDtypeStruct((hh, SUBLANES, LANES), F32)],
        scratch_shapes=[pltpu.VMEM((HEAD_DIM, HEAD_DIM), F32)],
        compiler_params=_params("parallel", "arbitrary"))(*proj4, lower_bounds, gn, hist, dy)


def _gdn_prep(mm, qa, ka, va, braw, araw, alog, dtb):
    n, c, _ = qa.shape
    q = qa * lax.rsqrt(jnp.sum(qa * qa, axis=-1, keepdims=True) + 1e-6) * (HEAD_DIM ** -0.5)
    k = ka * lax.rsqrt(jnp.sum(ka * ka, axis=-1, keepdims=True) + 1e-6)
    beta = jax.nn.sigmoid(braw)
    g = -jnp.exp(alog) * jax.nn.softplus(araw + dtb)
    ri = lax.broadcasted_iota(jnp.int32, (n, c, c), 1)
    ci = lax.broadcasted_iota(jnp.int32, (n, c, c), 2)
    incl, strict, eye = ri >= ci, ri > ci, ri == ci
    gc = mm(incl.astype(F32), g, "nn", FULL_F32)
    gc_i = gc[:, :, :c]
    gc_j = mm(jnp.ones((n, c, c), F32), jnp.where(eye, gc_i, 0.0), "nn", FULL_F32)
    decay = jnp.where(incl, jnp.exp(jnp.where(incl, gc_i - gc_j, 0.0)), 0.0)
    kb = k * beta
    low = jnp.where(strict, mm(kb, k, "nt", ONE_PASS) * decay, 0.0)
    x = -low
    inv = eye.astype(F32) + x
    for _ in range(5):
        x = mm(x, x, "nn", THREE_PASS)
        inv = inv + mm(inv, x, "nn", THREE_PASS)
    egc = jnp.exp(gc)
    u = mm(inv, va * beta, "nn", THREE_PASS)
    w = mm(inv, kb * egc, "nn", THREE_PASS)
    attn = jnp.where(incl, mm(q, k, "nt", ONE_PASS) * decay, 0.0)
    g_last = jnp.sum(g, axis=1, keepdims=True)
    return u, w, q * egc, k * jnp.exp(g_last - gc), attn, jnp.exp(g_last)


def _gdn_scan(mm, u, w, qd, kd, attn, egl, z, gn, state):
    v_new = u - mm(w, state, "nn", ONE_PASS)
    o = mm(qd, state, "nn", ONE_PASS) + mm(attn, v_new, "nn", ONE_PASS)
    new_state = state * egl + mm(kd, v_new, "tn", ONE_PASS)
    return _rms_gate(o, gn, z), new_state


def _chunks(ref_value, n, c):
    return ref_value.reshape(n, c, ref_value.shape[-1])


def _by_head(ref, rows, heads):
    return jnp.stack([ref[rows, pl.ds(h * HEAD_DIM, HEAD_DIM)] for h in range(heads)])


def _store_heads(ref, rows, value):
    for h in range(value.shape[0]):
        ref[rows, pl.ds(h * HEAD_DIM, HEAD_DIM)] = value[h]


def _gdn_prep_specs(tb, nt_unused=None):
    col = lambda off: pl.BlockSpec((tb, HEAD_DIM), lambda h, i: (i, off + h))
    rep = pl.BlockSpec((1, tb, LANES), lambda h, i: (h, i, 0))
    par = pl.BlockSpec((1, SUBLANES, LANES), lambda h, i: (h, 0, 0))
    att = pl.BlockSpec((1, tb, GDN_CHUNK), lambda h, i: (h, i, 0))
    egl = pl.BlockSpec((1, tb // GDN_CHUNK, SUBLANES, LANES), lambda h, i: (h, i, 0, 0))
    return col, rep, par, att, egl


def _gdn_prep_fwd(qkv_act, braw, araw, alog, dtb):
    t = qkv_act.shape[0]
    tb = min(t, 256)
    nt, nc = t // tb, tb // GDN_CHUNK
    width = GDN_HEADS * HEAD_DIM

    def body(q_ref, k_ref, v_ref, br_ref, ar_ref, al_ref, dt_ref, u_ref, w_ref, qd_ref, kd_ref, at_ref, eg_ref):
        ch = lambda r: _chunks(r, nc, GDN_CHUNK)
        u, w, qd, kd, attn, egl = _gdn_prep(_mm_raw, ch(q_ref[...]), ch(k_ref[...]), ch(v_ref[...]), ch(br_ref[0]),
                                            ch(ar_ref[0]), al_ref[0, 0:1, :], dt_ref[0, 0:1, :])
        u_ref[...] = u.reshape(tb, HEAD_DIM)
        w_ref[...] = w.reshape(tb, HEAD_DIM).astype(BF16)
        qd_ref[...] = qd.reshape(tb, HEAD_DIM).astype(BF16)
        kd_ref[...] = kd.reshape(tb, HEAD_DIM).astype(BF16)
        at_ref[0] = attn.reshape(tb, GDN_CHUNK).astype(BF16)
        eg_ref[0] = jnp.broadcast_to(egl, (nc, SUBLANES, LANES))

    col, rep, par, att, egl = _gdn_prep_specs(tb)
    h = GDN_HEADS
    return pl.pallas_call(
        body, name="gdn_prep_fwd", grid=(h, nt),
        in_specs=[col(0), col(h), col(2 * h), rep, rep, par, par],
        out_specs=[col(0), col(0), col(0), col(0), att, egl],
        out_shape=[jax.ShapeDtypeStruct((t, width), F32)] + [jax.ShapeDtypeStruct((t, width), BF16)] * 3
        + [jax.ShapeDtypeStruct((h, t, GDN_CHUNK), BF16), jax.ShapeDtypeStruct((h, t // GDN_CHUNK, SUBLANES, LANES), F32)],
        compiler_params=_params("parallel", "parallel"))(qkv_act, qkv_act, qkv_act, braw, araw, alog, dtb)


def _gdn_prep_bwd(qkv_act, braw, araw, alog, dtb, du, dw, dqd, dkd, dattn, degl):
    t = qkv_act.shape[0]
    tb = min(t, 256)
    nt, nc = t // tb, tb // GDN_CHUNK
    width = GDN_HEADS * HEAD_DIM

    def body(q_ref, k_ref, v_ref, br_ref, ar_ref, al_ref, dt_ref, du_ref, dw_ref, dqd_ref, dkd_ref, dat_ref, deg_ref,
             dq_ref, dk_ref, dv_ref, dbr_ref, dar_ref, dal_ref, ddt_ref):
        @pl.when(pl.program_id(1) == 0)
        def _():
            dal_ref[...] = jnp.zeros_like(dal_ref)
            ddt_ref[...] = jnp.zeros_like(ddt_ref)

        ch = lambda r: _chunks(r, nc, GDN_CHUNK)
        _, vjp = jax.vjp(functools.partial(_gdn_prep, _mm_vjp), ch(q_ref[...]), ch(k_ref[...]), ch(v_ref[...]),
                         ch(br_ref[0]), ch(ar_ref[0]), al_ref[0, 0:1, :], dt_ref[0, 0:1, :])
        dq, dk, dv, dbr, dar, dal, ddt = vjp((ch(du_ref[...]), ch(dw_ref[...]), ch(dqd_ref[...]), ch(dkd_ref[...]),
                                              ch(dat_ref[0]), deg_ref[0][:, 0:1, :]))
        dq_ref[...] = dq.reshape(tb, HEAD_DIM)
        dk_ref[...] = dk.reshape(tb, HEAD_DIM)
        dv_ref[...] = dv.reshape(tb, HEAD_DIM)
        dbr_ref[0] = _lane_total(dbr.reshape(tb, LANES))
        dar_ref[0] = _lane_total(dar.reshape(tb, LANES))
        dal_ref[0, 0:1, :] += _lane_total(dal)
        ddt_ref[0, 0:1, :] += _lane_total(ddt)

    col, rep, par, att, egl = _gdn_prep_specs(tb)
    h = GDN_HEADS
    return pl.pallas_call(
        body, name="gdn_prep_bwd", grid=(h, nt),
        in_specs=[col(0), col(h), col(2 * h), rep, rep, par, par, col(0), col(0), col(0), col(0), att, egl],
        out_specs=[col(0), col(0), col(0), rep, rep, par, par],
        out_shape=[jax.ShapeDtypeStruct((t, width), F32)] * 3 + [jax.ShapeDtypeStruct((h, t, LANES), F32)] * 2
        + [jax.ShapeDtypeStruct((h, SUBLANES, LANES), F32)] * 2,
        compiler_params=_params("parallel", "arbitrary"))(qkv_act, qkv_act, qkv_act, braw, araw, alog, dtb,
                                                         du, dw, dqd, dkd, dattn, degl)


def _scan_specs(tb, heads, chunk, rev, nt):
    ti = (lambda i: nt - 1 - i) if rev else (lambda i: i)
    row = pl.BlockSpec((tb, heads * HEAD_DIM), lambda i: (ti(i), 0))
    att = pl.BlockSpec((heads, tb, chunk), lambda i: (0, ti(i), 0))
    egl = pl.BlockSpec((heads, tb // chunk, SUBLANES, LANES), lambda i: (0, ti(i), 0, 0))
    hist = pl.BlockSpec((heads, tb // chunk, HEAD_DIM, HEAD_DIM), lambda i: (0, ti(i), 0, 0))
    gn = pl.BlockSpec((SUBLANES, LANES), lambda i: (0, 0))
    return row, att, egl, hist, gn


def _gdn_scan_fwd(u, w, qd, kd, attn, egl, zb, gn):
    t = u.shape[0]
    tb = min(t, 256)
    nt, nc = t // tb, tb // GDN_CHUNK
    nh = GDN_HEADS

    def body(u_ref, w_ref, qd_ref, kd_ref, at_ref, eg_ref, z_ref, gn_ref, y_ref, hist_ref, s_ref):
        @pl.when(pl.program_id(0) == 0)
        def _():
            s_ref[...] = jnp.zeros_like(s_ref)

        g = gn_ref[0:1, :]
        state = s_ref[...]
        for c in range(nc):
            rows = pl.ds(c * GDN_CHUNK, GDN_CHUNK)
            heads = lambda r: _by_head(r, rows, nh)
            hist_ref[:, c] = state
            y, state = _gdn_scan(_mm_raw, heads(u_ref), heads(w_ref), heads(qd_ref), heads(kd_ref), at_ref[:, rows, :],
                                 eg_ref[:, c, 0:1, :], heads(z_ref), g, state)
            _store_heads(y_ref, rows, y.astype(BF16))
        s_ref[...] = state

    row, att, egs, hist, gns = _scan_specs(tb, nh, GDN_CHUNK, False, nt)
    return pl.pallas_call(
        body, name="gdn_scan_fwd", grid=(nt,), in_specs=[row, row, row, row, att, egs, row, gns], out_specs=[row, hist],
        out_shape=[jax.ShapeDtypeStruct((t, nh * HEAD_DIM), BF16),
                   jax.ShapeDtypeStruct((nh, t // GDN_CHUNK, HEAD_DIM, HEAD_DIM), F32)],
        scratch_shapes=[pltpu.VMEM((nh, HEAD_DIM, HEAD_DIM), F32)],
        compiler_params=_params("arbitrary"))(u, w, qd, kd, attn, egl, zb, gn)


def _gdn_scan_bwd(u, w, qd, kd, attn, egl, zb, gn, hist, dy):
    t = u.shape[0]
    tb = min(t, 256)
    nt, nc = t // tb, tb // GDN_CHUNK
    nh = GDN_HEADS

    def body(u_ref, w_ref, qd_ref, kd_ref, at_ref, eg_ref, z_ref, gn_ref, hist_ref, dy_ref,
             du_ref, dw_ref, dqd_ref, dkd_ref, dat_ref, deg_ref, dz_ref, dgn_ref, ds_ref):
        @pl.when(pl.program_id(0) == 0)
        def _():
            ds_ref[...] = jnp.zeros_like(ds_ref)
            dgn_ref[...] = jnp.zeros_like(dgn_ref)

        g = gn_ref[0:1, :]
        d_state = ds_ref[...]
        for c in reversed(range(nc)):
            rows = pl.ds(c * GDN_CHUNK, GDN_CHUNK)
            heads = lambda r: _by_head(r, rows, nh).astype(F32)
            _, vjp = jax.vjp(functools.partial(_gdn_scan, _mm_vjp), heads(u_ref), heads(w_ref), heads(qd_ref),
                             heads(kd_ref), at_ref[:, rows, :].astype(F32), eg_ref[:, c, 0:1, :], heads(z_ref), g,
                             hist_ref[:, c])
            du, dw, dqd, dkd, dat, deg, dz, dgn, d_state = vjp((heads(dy_ref), d_state))
            _store_heads(du_ref, rows, du)
            _store_heads(dw_ref, rows, dw)
            _store_heads(dqd_ref, rows, dqd)
            _store_heads(dkd_ref, rows, dkd)
            dat_ref[:, rows, :] = dat
            deg_ref[:, c] = jnp.broadcast_to(deg, (nh, SUBLANES, LANES))
            _store_heads(dz_ref, rows, dz.astype(BF16))
            dgn_ref[0:1, :] += dgn
        ds_ref[...] = d_state

    row, att, egs, hists, gns = _scan_specs(tb, nh, GDN_CHUNK, True, nt)
    wide = jax.ShapeDtypeStruct((t, nh * HEAD_DIM), F32)
    return pl.pallas_call(
        body, name="gdn_scan_bwd", grid=(nt,),
        in_specs=[row, row, row, row, att, egs, row, gns, hists, row],
        out_specs=[row, row, row, row, att, egs, row, gns],
        out_shape=[wide] * 4 + [jax.ShapeDtypeStruct((nh, t, GDN_CHUNK), F32),
                                jax.ShapeDtypeStruct((nh, t // GDN_CHUNK, SUBLANES, LANES), F32),
                                jax.ShapeDtypeStruct((t, nh * HEAD_DIM), BF16),
                                jax.ShapeDtypeStruct((SUBLANES, LANES), F32)],
        scratch_shapes=[pltpu.VMEM((nh, HEAD_DIM, HEAD_DIM), F32)],
        compiler_params=_params("arbitrary"))(u, w, qd, kd, attn, egl, zb, gn, hist, dy)


def _hgrn_prep(mm, qr, fr, lbl):
    c = HGRN_CHUNK
    lb = jax.nn.sigmoid(lbl[1:2, :] - lbl[0:1, :])
    f = lb + (1.0 - lb) * jax.nn.sigmoid(fr)
    q = jax.nn.silu(qr)
    k = 1.0 - f
    logf = jnp.log(f)
    ri = lax.broadcasted_iota(jnp.int32, (c, c), 0)
    ci = lax.broadcasted_iota(jnp.int32, (c, c), 1)
    b = mm((ri >= ci).astype(F32), logf, "nn", FULL_F32)
    ti = lax.broadcasted_iota(jnp.int32, (c, c, HEAD_DIM), 0)
    si = lax.broadcasted_iota(jnp.int32, (c, c, HEAD_DIM), 1)
    m3 = ti >= si
    decay = jnp.where(m3, jnp.exp(jnp.where(m3, b[:, None, :] - b[None, :, :], 0.0)), 0.0)
    attn = jnp.sum(q[:, None, :] * k[None, :, :] * decay, axis=-1)
    b_last = jnp.sum(logf, axis=0, keepdims=True)
    return q * jnp.exp(b), k * jnp.exp(b_last - b), attn, jnp.exp(b_last)


def _hgrn_scan(mm, qe, kd, attn, ebl, iv, z, gn, state):
    o = mm(qe, state, "nt", ONE_PASS) + mm(attn, iv, "nn", ONE_PASS)
    new_state = state * ebl + mm(iv, kd, "tn", ONE_PASS)
    return _rms_gate(o, gn, z), new_state


def _hgrn_prep_specs(tb):
    col = pl.BlockSpec((tb, HEAD_DIM), lambda h, i: (i, h))
    lbs = pl.BlockSpec((2, HEAD_DIM), lambda h, i: (0, h))
    att = pl.BlockSpec((1, tb, HGRN_CHUNK), lambda h, i: (h, i, 0))
    ebl = pl.BlockSpec((1, tb // HGRN_CHUNK, SUBLANES, LANES), lambda h, i: (h, i, 0, 0))
    return col, lbs, att, ebl


def _hgrn_prep_fwd(qr, fr, lower_bounds):
    t = qr.shape[0]
    tb = min(t, 256)
    nt, nc = t // tb, tb // HGRN_CHUNK
    hh = HGRN_HEADS

    def body(q_ref, f_ref, lb_ref, qe_ref, kd_ref, at_ref, eb_ref):
        lbl = lb_ref[...]
        for c in range(nc):
            rows = pl.ds(c * HGRN_CHUNK, HGRN_CHUNK)
            qe, kd, attn, ebl = _hgrn_prep(_mm_raw, q_ref[rows, :], f_ref[rows, :], lbl)
            qe_ref[rows, :] = qe.astype(BF16)
            kd_ref[rows, :] = kd.astype(BF16)
            at_ref[0, rows, :] = attn.astype(BF16)
            eb_ref[0, c] = jnp.broadcast_to(ebl, (SUBLANES, LANES))

    col, lbs, att, ebs = _hgrn_prep_specs(tb)
    return pl.pallas_call(
        body, name="hgrn_prep_fwd", grid=(hh, nt), in_specs=[col, col, lbs], out_specs=[col, col, att, ebs],
        out_shape=[jax.ShapeDtypeStruct((t, HGRN_WIDTH), BF16)] * 2
        + [jax.ShapeDtypeStruct((hh, t, HGRN_CHUNK), BF16), jax.ShapeDtypeStruct((hh, t // HGRN_CHUNK, SUBLANES, LANES), F32)],
        compiler_params=_params("parallel", "parallel"))(qr, fr, lower_bounds)


def _hgrn_prep_bwd(qr, fr, lower_bounds, dqe, dkd, dattn, debl):
    t = qr.shape[0]
    tb = min(t, 256)
    nt, nc = t // tb, tb // HGRN_CHUNK
    hh = HGRN_HEADS

    def body(q_ref, f_ref, lb_ref, dqe_ref, dkd_ref, dat_ref, deb_ref, dq_ref, df_ref, dlb_ref):
        @pl.when(pl.program_id(1) == 0)
        def _():
            dlb_ref[...] = jnp.zeros_like(dlb_ref)

        lbl = lb_ref[...]
        for c in range(nc):
            rows = pl.ds(c * HGRN_CHUNK, HGRN_CHUNK)
            _, vjp = jax.vjp(functools.partial(_hgrn_prep, _mm_vjp), q_ref[rows, :], f_ref[rows, :], lbl)
            dq, df, dlb = vjp((dqe_ref[rows, :], dkd_ref[rows, :], dat_ref[0, rows, :], deb_ref[0, c, 0:1, :]))
            dq_ref[rows, :] = dq.astype(BF16)
            df_ref[rows, :] = df.astype(BF16)
            dlb_ref[...] += dlb

    col, lbs, att, ebs = _hgrn_prep_specs(tb)
    return pl.pallas_call(
        body, name="hgrn_prep_bwd", grid=(hh, nt), in_specs=[col, col, lbs, col, col, att, ebs],
        out_specs=[col, col, lbs],
        out_shape=[jax.ShapeDtypeStruct((t, HGRN_WIDTH), BF16)] * 2 + [jax.ShapeDtypeStruct((2, HGRN_WIDTH), F32)],
        compiler_params=_params("parallel", "arbitrary"))(qr, fr, lower_bounds, dqe, dkd, dattn, debl)


def _hgrn_scan_fwd(qe, kd, attn, ebl, iv, z, gn):
    t = qe.shape[0]
    tb = min(t, 128)
    nt, nc = t // tb, tb // HGRN_CHUNK
    hh = HGRN_HEADS

    def body(qe_ref, kd_ref, at_ref, eb_ref, i_ref, z_ref, gn_ref, y_ref, hist_ref, s_ref):
        @pl.when(pl.program_id(0) == 0)
        def _():
            s_ref[...] = jnp.zeros_like(s_ref)

        g = gn_ref[0:1, :]
        state = s_ref[...]
        for c in range(nc):
            rows = pl.ds(c * HGRN_CHUNK, HGRN_CHUNK)
            heads = lambda r: _by_head(r, rows, hh)
            hist_ref[:, c] = state
            y, state = _hgrn_scan(_mm_raw, heads(qe_ref), heads(kd_ref), at_ref[:, rows, :], eb_ref[:, c, 0:1, :],
                                  heads(i_ref), heads(z_ref), g, state)
            _store_heads(y_ref, rows, y.astype(BF16))
        s_ref[...] = state

    row, att, ebs, hist, gns = _scan_specs(tb, hh, HGRN_CHUNK, False, nt)
    return pl.pallas_call(
        body, name="hgrn_scan_fwd", grid=(nt,), in_specs=[row, row, att, ebs, row, row, gns], out_specs=[row, hist],
        out_shape=[jax.ShapeDtypeStruct((t, HGRN_WIDTH), BF16),
                   jax.ShapeDtypeStruct((hh, t // HGRN_CHUNK, HEAD_DIM, HEAD_DIM), F32)],
        scratch_shapes=[pltpu.VMEM((hh, HEAD_DIM, HEAD_DIM), F32)],
        compiler_params=_params("arbitrary"))(qe, kd, attn, ebl, iv, z, gn)


def _hgrn_scan_bwd(qe, kd, attn, ebl, iv, z, gn, hist, dy):
    t = qe.shape[0]
    tb = min(t, 128)
    nt, nc = t // tb, tb // HGRN_CHUNK
    hh = HGRN_HEADS

    def body(qe_ref, kd_ref, at_ref, eb_ref, i_ref, z_ref, gn_ref, hist_ref, dy_ref,
             dqe_ref, dkd_ref, dat_ref, deb_ref, di_ref, dz_ref, dgn_ref, ds_ref):
        @pl.when(pl.program_id(0) == 0)
        def _():
            ds_ref[...] = jnp.zeros_like(ds_ref)
            dgn_ref[...] = jnp.zeros_like(dgn_ref)

        g = gn_ref[0:1, :]
        d_state = ds_ref[...]
        for c in reversed(range(nc)):
            rows = pl.ds(c * HGRN_CHUNK, HGRN_CHUNK)
            heads = lambda r: _by_head(r, rows, hh).astype(F32)
            _, vjp = jax.vjp(functools.partial(_hgrn_scan, _mm_vjp), heads(qe_ref), heads(kd_ref),
                             at_ref[:, rows, :].astype(F32), eb_ref[:, c, 0:1, :], heads(i_ref), heads(z_ref), g,
                             hist_ref[:, c])
            dqe, dkd, dat, deb, di, dz, dgn, d_state = vjp((heads(dy_ref), d_state))
            _store_heads(dqe_ref, rows, dqe)
            _store_heads(dkd_ref, rows, dkd)
            dat_ref[:, rows, :] = dat
            deb_ref[:, c] = jnp.broadcast_to(deb, (hh, SUBLANES, LANES))
            _store_heads(di_ref, rows, di.astype(BF16))
            _store_heads(dz_ref, rows, dz.astype(BF16))
            dgn_ref[0:1, :] += dgn
        ds_ref[...] = d_state

    row, att, ebs, hists, gns = _scan_specs(tb, hh, HGRN_CHUNK, True, nt)
    wide = lambda dt: jax.ShapeDtypeStruct((t, HGRN_WIDTH), dt)
    return pl.pallas_call(
        body, name="hgrn_scan_bwd", grid=(nt,),
        in_specs=[row, row, att, ebs, row, row, gns, hists, row],
        out_specs=[row, row, att, ebs, row, row, gns],
        out_shape=[wide(F32), wide(F32), jax.ShapeDtypeStruct((hh, t, HGRN_CHUNK), F32),
                   jax.ShapeDtypeStruct((hh, t // HGRN_CHUNK, SUBLANES, LANES), F32), wide(BF16), wide(BF16),
                   jax.ShapeDtypeStruct((SUBLANES, LANES), F32)],
        scratch_shapes=[pltpu.VMEM((hh, HEAD_DIM, HEAD_DIM), F32)],
        compiler_params=_params("arbitrary"))(qe, kd, attn, ebl, iv, z, gn, hist, dy)


def _layer_norm(pre, g, b):
    mu = jnp.mean(pre, axis=-1, keepdims=True)
    d = pre - mu
    var = jnp.mean(d * d, axis=-1, keepdims=True)
    return d * lax.rsqrt(var + NORM_EPS) * g + b


def _lnpl_fwd(xin, s, p, wg, wpl, ln_g, ln_b):
    t = xin.shape[0]
    tt = min(t, 256)

    def body(x_ref, s_ref, p_ref, wg_ref, wpl_ref, g_ref, b_ref, o_ref, ob_ref):
        xn = _layer_norm(DEEPNORM_ALPHA * x_ref[...] + s_ref[...], g_ref[...], b_ref[...])
        gate = jax.nn.sigmoid(_mm_raw(xn, wg_ref[...], "nn", False))
        out = xn + _mm_raw(p_ref[...], wpl_ref[...], "nn", False) * gate
        o_ref[...] = out
        ob_ref[...] = out.astype(BF16)

    row = lambda w: pl.BlockSpec((tt, w), lambda i: (i, 0))
    full = lambda a: pl.BlockSpec(a.shape, lambda i: (0, 0))
    return pl.pallas_call(
        body, name="lnpl_fwd", grid=(t // tt,),
        in_specs=[row(D_MODEL), row(D_MODEL), row(PL_DIM), full(wg), full(wpl), full(ln_g), full(ln_b)],
        out_specs=[row(D_MODEL), row(D_MODEL)],
        out_shape=[jax.ShapeDtypeStruct((t, D_MODEL), F32), jax.ShapeDtypeStruct((t, D_MODEL), BF16)],
        compiler_params=_params("parallel"))(xin, s, p, wg, wpl, ln_g, ln_b)


def _lnpl_bwd(xin, s, p, wg, wpl, ln_g, ln_b, upstream, last, name):
    t = xin.shape[0]
    tt = min(t, 256)

    def body(x_ref, s_ref, p_ref, wg_ref, wpl_ref, g_ref, b_ref, up_ref,
             dpre_ref, dwg_ref, dwpl_ref, dg_ref, db_ref, loss_ref):
        @pl.when(pl.program_id(0) == 0)
        def _():
            for r in (dwg_ref, dwpl_ref, dg_ref, db_ref, loss_ref):
                r[...] = jnp.zeros_like(r)

        pre = DEEPNORM_ALPHA * x_ref[...] + s_ref[...]
        xn, ln_vjp = jax.vjp(_layer_norm, pre, g_ref[...], b_ref[...])
        gate = jax.nn.sigmoid(_mm_raw(xn, wg_ref[...], "nn", False))
        plv = _mm_raw(p_ref[...], wpl_ref[...], "nn", False)
        if last:
            err = xn + plv * gate - up_ref[...]
            dout = err * (1.0 / D_MODEL)
            tot = jnp.sum(jnp.sum(err * err, axis=1, keepdims=True), axis=0, keepdims=True) * (0.5 / D_MODEL)
            loss_ref[...] += jnp.broadcast_to(tot, loss_ref.shape)
        else:
            dout = up_ref[...]
        dplv = dout * gate
        dlogits = dout * plv * gate * (1.0 - gate)
        dwg_ref[...] += _mm_raw(xn, dlogits, "tn", False)
        dwpl_ref[...] += _mm_raw(p_ref[...], dplv, "tn", False)
        dxn = dout + _mm_raw(dlogits, wg_ref[...], "nt", False)
        dpre, dg, db = ln_vjp(dxn)
        dpre_ref[...] = dpre
        dg_ref[...] += dg
        db_ref[...] += db

    row = lambda w: pl.BlockSpec((tt, w), lambda i: (i, 0))
    full = lambda shape: pl.BlockSpec(shape, lambda i: (0, 0))
    return pl.pallas_call(
        body, name=name, grid=(t // tt,),
        in_specs=[row(D_MODEL), row(D_MODEL), row(PL_DIM), full(wg.shape), full(wpl.shape), full(ln_g.shape),
                  full(ln_b.shape), row(D_MODEL)],
        out_specs=[row(D_MODEL), full(wg.shape), full(wpl.shape), full(ln_g.shape), full(ln_b.shape),
                   full((SUBLANES, LANES))],
        out_shape=[jax.ShapeDtypeStruct((t, D_MODEL), F32), jax.ShapeDtypeStruct(wg.shape, F32),
                   jax.ShapeDtypeStruct(wpl.shape, F32), jax.ShapeDtypeStruct(ln_g.shape, F32),
                   jax.ShapeDtypeStruct(ln_b.shape, F32), jax.ShapeDtypeStruct((SUBLANES, LANES), F32)],
        compiler_params=_params("arbitrary"))(xin, s, p, wg, wpl, ln_g, ln_b, upstream)


def _rep_rows(v):
    return jnp.broadcast_to(v.reshape(1, LANES), (SUBLANES, LANES))


def _rep_heads(v):
    return jnp.broadcast_to(v.reshape(-1, 1, 1), (v.shape[0], SUBLANES, LANES))


def _local_step(x, p, target, w):
    a = DEEPNORM_ALPHA
    nh = GDN_HEADS
    xb = x.astype(BF16)
    wie = w["w_in_even"]
    w_a, w_qkv, w_zb = wie[:, :4096], wie[:, 4096:7168], wie[:, 7168:8192]
    w_tail = jnp.pad(wie[:, 8192:], ((0, 0), (0, LANES - 2 * nh)))
    woe, wio, woo = w["w_out_even"], w["w_in_odd"], w["w_out_odd"]
    conv_a_w, conv_b_w = w["conv_a_w"], w["conv_b_w"]
    ln_g0, ln_b0, ln_g1, ln_b1 = (v.reshape(1, D_MODEL) for v in (w["ln_g"][0], w["ln_b"][0], w["ln_g"][1], w["ln_b"][1]))
    alog, dtb = _rep_heads(w["a_log"].reshape(nh)), _rep_heads(w["dt_bias"].reshape(nh))
    gdn_g, hgrn_g = _rep_rows(w["gdn_norm_g"]), _rep_rows(w["hgrn_norm_g"])

    proj_a = _matmul(xb, w_a, name="fwd_proj_a")
    proj_qkv = _matmul(xb, w_qkv, name="fwd_proj_qkv")
    proj_zb = _matmul(xb, w_zb, name="fwd_proj_zb")
    proj_tail = _matmul(xb, w_tail, name="fwd_proj_tail")
    rep = lambda cols: jnp.broadcast_to(cols.T[:, :, None], (nh, cols.shape[0], LANES))
    braw, araw = rep(proj_tail[:, :nh]), rep(proj_tail[:, nh:2 * nh])
    y_a = _conv_a_fwd(proj_a, conv_a_w)
    qkv_act = _conv_b_fwd(proj_qkv, conv_b_w)
    gdn_pre = _gdn_prep_fwd(qkv_act, braw, araw, alog, dtb)
    y_b, gdn_hist = _gdn_scan_fwd(*gdn_pre, proj_zb, gdn_g)
    s0 = _matmul(y_b, woe[1024:], name="fwd_out_even_b", add=_matmul(y_a, woe[:1024], name="fwd_out_even_a"))
    x1, x1b = _lnpl_fwd(x, s0, p[0], w["w_pl_gate"][0], w["w_pl"][0], ln_g0, ln_b0)
    proj_o = [_matmul(x1b, wio[j], name=f"fwd_proj_odd{j}") for j in range(4)]
    hgrn_pre = _hgrn_prep_fwd(proj_o[0], proj_o[1], w["lower_bounds"])
    y_o, hgrn_hist = _hgrn_scan_fwd(*hgrn_pre, proj_o[2], proj_o[3], hgrn_g)
    s1 = _matmul(y_o, woo, name="fwd_out_odd")

    g = {}
    dpre1, dwg1, dwpl1, dlng1, dlnb1, loss = _lnpl_bwd(x1, s1, p[1], w["w_pl_gate"][1], w["w_pl"][1], ln_g1, ln_b1,
                                                     target, True, "lnpl_bwd_odd")
    dy_o = _matmul(dpre1, woo, tb=True, name="bwd_out_odd_dx")
    g["w_out_odd"] = _matmul(y_o, dpre1, ta=True, name="bwd_out_odd_dw")
    dqe, dkd, dat, deb, di, dz, dhg = _hgrn_scan_bwd(*hgrn_pre, proj_o[2], proj_o[3], hgrn_g, hgrn_hist, dy_o)
    dq, df, dlb = _hgrn_prep_bwd(proj_o[0], proj_o[1], w["lower_bounds"], dqe, dkd, dat, deb)
    dx1 = dpre1
    scale = a
    dws = []
    for j, dj in enumerate((dq, df, di, dz)):
        dx1 = _matmul(dj, wio[j], tb=True, add=dx1, add_scale=scale, name=f"bwd_proj_odd_dx{j}")
        scale = 1.0
        dws.append(_matmul(x1b, dj, ta=True, name=f"bwd_proj_odd_dw{j}"))
    g["w_in_odd"] = jnp.stack(dws)
    g["hgrn_norm_g"] = dhg[0:1]
    g["lower_bounds"] = dlb

    dpre0, dwg0, dwpl0, dlng0, dlnb0, _ = _lnpl_bwd(x, s0, p[0], w["w_pl_gate"][0], w["w_pl"][0], ln_g0, ln_b0,
                                                  dx1, False, "lnpl_bwd_even")
    g["w_pl_gate"] = jnp.stack([dwg0, dwg1])
    g["w_pl"] = jnp.stack([dwpl0, dwpl1])
    g["ln_g"] = jnp.concatenate([dlng0, dlng1], axis=0)
    g["ln_b"] = jnp.concatenate([dlnb0, dlnb1], axis=0)
    dy_a = _matmul(dpre0, woe[:1024], tb=True, name="bwd_out_even_dxa")
    dy_b = _matmul(dpre0, woe[1024:], tb=True, name="bwd_out_even_dxb")
    g["w_out_even"] = jnp.concatenate([_matmul(y_a, dpre0, ta=True, name="bwd_out_even_dwa"),
                                       _matmul(y_b, dpre0, ta=True, name="bwd_out_even_dwb")], axis=0)
    du, dw, dqd, dkd, dat, deg, dzb, dgn = _gdn_scan_bwd(*gdn_pre, proj_zb, gdn_g, gdn_hist, dy_b)
    dqa, dka, dva, dbr, dar, dal, ddt = _gdn_prep_bwd(qkv_act, braw, araw, alog, dtb, du, dw, dqd, dkd, dat, deg)
    g["a_log"] = dal[:, 0, 0].reshape(1, nh)
    g["dt_bias"] = ddt[:, 0, 0].reshape(1, nh)
    g["gdn_norm_g"] = dgn[0:1]
    d_pre_qkv, dwb = [], []
    for j, dj in enumerate((dqa, dka, dva)):
        dpj, dwj = _conv_b_bwd(proj_qkv, conv_b_w, dj, j, f"conv_b_bwd{j}")
        d_pre_qkv.append(dpj)
        dwb.append(dwj[:4])
    g["conv_b_w"] = jnp.concatenate(dwb, axis=1)
    d_a, dwa = _conv_a_bwd(proj_a, conv_a_w, dy_a)
    g["conv_a_w"] = dwa[:3]
    d_tail = jnp.concatenate([dbr[:, :, 0].T, dar[:, :, 0].T, jnp.zeros((x.shape[0], LANES - 2 * nh), F32)],
                             axis=1).astype(BF16)
    pieces = [(d_a, w_a), (d_pre_qkv[0], w_qkv[:, :1024]), (d_pre_qkv[1], w_qkv[:, 1024:2048]),
              (d_pre_qkv[2], w_qkv[:, 2048:]), (dzb, w_zb), (d_tail, w_tail)]
    dx = dpre0
    scale = a
    dws = []
    for j, (dj, wj) in enumerate(pieces):
        dx = _matmul(dj, wj, tb=True, add=dx, add_scale=scale, name=f"bwd_proj_even_dx{j}")
        scale = 1.0
        dws.append(_matmul(xb, dj, ta=True, name=f"bwd_proj_even_dw{j}"))
    dws[-1] = dws[-1][:, :2 * nh]
    g["w_in_even"] = jnp.concatenate(dws, axis=1)
    return loss, dx, g


def _adamw(w, g, m, v, name):
    rows, cols = w.shape
    tr = rows if rows <= 256 else 256
    assert rows % tr == 0, (name, rows)

    def body(w_ref, g_ref, m_ref, v_ref, d_ref, nm_ref, nv_ref):
        gg = g_ref[...]
        nm = ADAM_B1 * m_ref[...] + (1.0 - ADAM_B1) * gg
        nv = ADAM_B2 * v_ref[...] + (1.0 - ADAM_B2) * jnp.square(gg)
        m_hat = nm / (1.0 - ADAM_B1 ** ADAM_STEP)
        v_hat = nv / (1.0 - ADAM_B2 ** ADAM_STEP)
        d_ref[...] = -ADAM_LR * (m_hat / (jnp.sqrt(v_hat) + ADAM_EPS) + ADAM_WD * w_ref[...])
        nm_ref[...] = nm
        nv_ref[...] = nv

    spec = pl.BlockSpec((tr, cols), lambda i: (i, 0))
    return pl.pallas_call(
        body, name=name, grid=(rows // tr,), in_specs=[spec] * 4, out_specs=[spec] * 3,
        out_shape=[jax.ShapeDtypeStruct(w.shape, F32)] * 3, compiler_params=_params("parallel"))(w, g, m, v)


MESH = pl.DeviceIdType.MESH
N_DEV = 8
HBM_SPEC = pl.BlockSpec(memory_space=pltpu.HBM)
VMEM_SPEC = pl.BlockSpec(memory_space=pltpu.VMEM)


def _coords():
    return lax.axis_index("x"), lax.axis_index("y"), lax.axis_index("c")


def _flip(v, bit):
    return 1 - v if bit else v


def _remote(src, dst, send_sem, recv_sem, dev):
    return pltpu.make_async_remote_copy(src_ref=src, dst_ref=dst, send_sem=send_sem, recv_sem=recv_sem,
                                        device_id=dev, device_id_type=MESH)


def _exchange_small(buf, reduce, name):
    rows = buf.shape[0]

    def body(in_ref, out_ref, slots, send_sems, recv_sems):
        x, y, c = _coords()
        me = 4 * x + 2 * y + c
        slots[me] = in_ref[...]
        peer = lambda k: (_flip(x, (k >> 2) & 1), _flip(y, (k >> 1) & 1), _flip(c, k & 1))
        sends = []
        for k in range(1, N_DEV):
            cp = _remote(in_ref, slots.at[me], send_sems.at[k - 1], recv_sems.at[k - 1], peer(k))
            cp.start()
            sends.append(cp)
        for k in range(1, N_DEV):
            px, py, pc = peer(k)
            _remote(in_ref, slots.at[4 * px + 2 * py + pc], send_sems.at[k - 1], recv_sems.at[k - 1], peer(k)).wait_recv()
        for cp in sends:
            cp.wait_send()
        if reduce:
            acc = slots[0]
            for d in range(1, N_DEV):
                acc = acc + slots[d]
            out_ref[...] = acc
        else:
            out_ref[...] = slots[...]

    out_shape = (rows, LANES) if reduce else (N_DEV, rows, LANES)
    return pl.pallas_call(
        body, name=name, in_specs=[VMEM_SPEC], out_specs=VMEM_SPEC, out_shape=jax.ShapeDtypeStruct(out_shape, F32),
        scratch_shapes=[pltpu.VMEM((N_DEV, rows, LANES), F32), pltpu.SemaphoreType.DMA((N_DEV - 1,)),
                        pltpu.SemaphoreType.DMA((N_DEV - 1,))])(buf)


def _half_rows(half, which):
    return pl.ds(pl.multiple_of(which * half, 16), half)


def _other_chip(x, y, k):
    return _flip(x, (k >> 1) & 1), _flip(y, k & 1)


def _gather_shards(shards):
    n = len(shards)

    def body(*refs):
        ins, outs = refs[:n], refs[n:2 * n]
        ici_s, ici_r, d2d_s, d2d_r = refs[2 * n:]
        x, y, c = _coords()
        chip = 2 * x + y
        sends = []
        for i in range(n):
            half = ins[i].shape[0] // 2
            for k in (1, 2, 3):
                ox, oy = _other_chip(x, y, k)
                cp = _remote(ins[i].at[_half_rows(half, c)], outs[i].at[chip, _half_rows(half, c)],
                             ici_s.at[3 * i + k - 1], ici_r.at[3 * i + k - 1], (ox, oy, c))
                cp.start()
                sends.append(cp)
        for k in (1, 2, 3):
            ox, oy = _other_chip(x, y, k)
            for i in range(n):
                half = ins[i].shape[0] // 2
                piece = outs[i].at[2 * ox + oy, _half_rows(half, c)]
                _remote(piece, piece, ici_s.at[3 * i + k - 1], ici_r.at[3 * i + k - 1], (ox, oy, c)).wait_recv()
                cp = _remote(piece, piece, d2d_s.at[3 * i + k - 1], d2d_r.at[3 * i + k - 1], (x, y, 1 - c))
                cp.start()
                sends.append(cp)
        for k in (1, 2, 3):
            ox, oy = _other_chip(x, y, k)
            for i in range(n):
                half = ins[i].shape[0] // 2
                piece = outs[i].at[2 * ox + oy, _half_rows(half, 1 - c)]
                _remote(piece, piece, d2d_s.at[3 * i + k - 1], d2d_r.at[3 * i + k - 1], (x, y, 1 - c)).wait_recv()
        for cp in sends:
            cp.wait_send()

    return pl.pallas_call(
        body, name="gather_weights", in_specs=[HBM_SPEC] * n, out_specs=[HBM_SPEC] * n,
        out_shape=[jax.ShapeDtypeStruct((4,) + s.shape, s.dtype) for s in shards],
        scratch_shapes=[pltpu.SemaphoreType.DMA((3 * n,))] * 4)(*shards)


def _rs_sibling_swap(g4s):
    n = len(g4s)

    def body(*refs):
        ins, outs = refs[:n], refs[n:2 * n]
        send_sems, recv_sems = refs[2 * n:]
        x, y, c = _coords()
        sends = []
        for i in range(n):
            half = ins[i].shape[1] // 2
            for s in range(4):
                cp = _remote(ins[i].at[s, _half_rows(half, 1 - c)], outs[i].at[s], send_sems.at[4 * i + s],
                             recv_sems.at[4 * i + s], (x, y, 1 - c))
                cp.start()
                sends.append(cp)
        for cp in sends:
            cp.wait_recv()
        for cp in sends:
            cp.wait_send()

    return pl.pallas_call(
        body, name="rs_sibling_swap", in_specs=[HBM_SPEC] * n, out_specs=[HBM_SPEC] * n,
        out_shape=[jax.ShapeDtypeStruct((4, g.shape[1] // 2, g.shape[2]), g.dtype) for g in g4s],
        scratch_shapes=[pltpu.SemaphoreType.DMA((4 * n,))] * 2)(*g4s)


def _rs_add_sibling(g4, got, c_idx, name):
    _, rows, cols = g4.shape
    half = rows // 2
    tr = min(half, 256)
    nb = half // tr

    def body(c_ref, a_ref, b_ref, o_ref):
        o_ref[...] = a_ref[...] + b_ref[...]

    blk = (1, tr, cols)
    grid_spec = pltpu.PrefetchScalarGridSpec(
        num_scalar_prefetch=1, grid=(4, nb),
        in_specs=[pl.BlockSpec(blk, lambda s, i, c_ref: (s, c_ref[0] * nb + i, 0)),
                  pl.BlockSpec(blk, lambda s, i, c_ref: (s, i, 0))],
        out_specs=pl.BlockSpec(blk, lambda s, i, c_ref: (s, i, 0)))
    return pl.pallas_call(body, name=name, grid_spec=grid_spec, out_shape=jax.ShapeDtypeStruct(got.shape, F32),
                          compiler_params=_params("parallel", "parallel"))(c_idx, g4, got)


def _rs_chip_scatter(p4s):
    n = len(p4s)

    def body(*refs):
        ins, outs = refs[:n], refs[n:2 * n]
        send_sems, recv_sems = refs[2 * n:]
        x, y, c = _coords()
        sends = []
        for i in range(n):
            for k in (1, 2, 3):
                ox, oy = _other_chip(x, y, k)
                cp = _remote(ins[i].at[2 * ox + oy], outs[i].at[k - 1], send_sems.at[3 * i + k - 1],
                             recv_sems.at[3 * i + k - 1], (ox, oy, c))
                cp.start()
                sends.append(cp)
        for cp in sends:
            cp.wait_recv()
        for cp in sends:
            cp.wait_send()

    return pl.pallas_call(
        body, name="rs_chip_scatter", in_specs=[HBM_SPEC] * n, out_specs=[HBM_SPEC] * n,
        out_shape=[jax.ShapeDtypeStruct((3,) + p.shape[1:], p.dtype) for p in p4s],
        scratch_shapes=[pltpu.SemaphoreType.DMA((3 * n,))] * 2)(*p4s)


def _rs_add_chips(p4, got3, idx, name):
    _, half, cols = p4.shape
    tr = min(half, 256)
    nb = half // tr

    def body(idx_ref, p_ref, a_ref, b_ref, c_ref, o_ref):
        o_ref[...] = ((p_ref[0] + a_ref[0]) + b_ref[0]) + c_ref[0]

    blk = (1, tr, cols)
    grid_spec = pltpu.PrefetchScalarGridSpec(
        num_scalar_prefetch=1, grid=(nb,),
        in_specs=[pl.BlockSpec(blk, lambda i, idx_ref: (idx_ref[0], i, 0))]
        + [pl.BlockSpec(blk, functools.partial(lambda k, i, idx_ref: (k, i, 0), k)) for k in range(3)],
        out_specs=pl.BlockSpec((tr, cols), lambda i, idx_ref: (idx_ref[1] * nb + i, 0)))
    return pl.pallas_call(body, name=name, grid_spec=grid_spec, out_shape=jax.ShapeDtypeStruct((2 * half, cols), F32),
                          compiler_params=_params("parallel"))(idx, p4, got3, got3, got3)


def _rs_share_halves(bufs):
    n = len(bufs)

    def body(*refs):
        ins, outs = refs[:n], refs[n:2 * n]
        send_sems, recv_sems = refs[2 * n:]
        x, y, c = _coords()
        sends = []
        for i in range(n):
            half = ins[i].shape[0] // 2
            cp = _remote(ins[i].at[_half_rows(half, c)], outs[i].at[_half_rows(half, c)], send_sems.at[i],
                         recv_sems.at[i], (x, y, 1 - c))
            cp.start()
            sends.append(cp)
        for i in range(n):
            half = ins[i].shape[0] // 2
            _remote(ins[i].at[_half_rows(half, c)], outs[i].at[_half_rows(half, 1 - c)], send_sems.at[i],
                    recv_sems.at[i], (x, y, 1 - c)).wait_recv()
        for cp in sends:
            cp.wait_send()

    return pl.pallas_call(
        body, name="rs_share_halves", in_specs=[HBM_SPEC] * n, out_specs=[HBM_SPEC] * n,
        out_shape=[jax.ShapeDtypeStruct(b.shape, b.dtype) for b in bufs],
        input_output_aliases={i: i for i in range(n)},
        scratch_shapes=[pltpu.SemaphoreType.DMA((n,))] * 2)(*bufs)


def _reduce_scatter(g4s, names):
    x, y, c = _coords()
    c_idx = jnp.stack([c]).astype(jnp.int32)
    idx = jnp.stack([2 * x + y, c]).astype(jnp.int32)
    got = _rs_sibling_swap(g4s)
    p4s = [_rs_add_sibling(g, s, c_idx, f"rs_add_sibling_{nm}") for g, s, nm in zip(g4s, got, names)]
    got3 = _rs_chip_scatter(p4s)
    bufs = [_rs_add_chips(p, t, idx, f"rs_add_chips_{nm}") for p, t, nm in zip(p4s, got3, names)]
    return _rs_share_halves(bufs)


def _cols_join(s4):
    return s4.transpose(1, 0, 2).reshape(s4.shape[1], 4 * s4.shape[2])


def _cols_split(full):
    r, c4 = full.shape
    return full.reshape(r, 4, c4 // 4).transpose(1, 0, 2)


_BIG = {
    "w_in_even": ((1024, 2052), _cols_join, _cols_split),
    "w_out_even": ((512, 1024), lambda s: s.reshape(2048, 1024), lambda f: f.reshape(4, 512, 1024)),
    "w_in_odd": ((1024, 2048), lambda s: s, lambda f: f),
    "w_out_odd": ((512, 1024), lambda s: s.reshape(2048, 1024), lambda f: f.reshape(4, 512, 1024)),
    "w_pl": ((512, 256), lambda s: s.reshape(4, 2, 256, 256).transpose(1, 2, 0, 3).reshape(2, 256, 1024),
             lambda f: f.reshape(2, 256, 4, 256).transpose(2, 0, 1, 3).reshape(4, 512, 256)),
    "w_pl_gate": ((512, 1024), lambda s: s.reshape(4, 2, 256, 1024).transpose(1, 0, 2, 3).reshape(2, 1024, 1024),
                  lambda f: f.reshape(2, 4, 256, 1024).transpose(1, 0, 2, 3).reshape(4, 512, 1024)),
}


def _size(shape):
    n = 1
    for d in shape:
        n *= d
    return n


_SMALL = {"a_log": (1, 8), "dt_bias": (1, 8), "gdn_norm_g": (1, 128), "hgrn_norm_g": (1, 128),
          "lower_bounds": (2, 2048), "ln_g": (2, 1024), "ln_b": (2, 1024), "conv_a_w": (3, 1024), "conv_b_w": (4, 3072)}
_CONV_SHARD = {"conv_a_w": (3, 256), "conv_b_w": (4, 768)}


def _pack_small(parts, shapes, head_rows=0):
    rows = []
    for n, shape in shapes.items():
        v = parts[n].reshape(-1)
        rows.append(jnp.pad(v, (0, -v.shape[0] % LANES)).reshape(-1, LANES))
    buf = jnp.concatenate(rows, axis=0)
    return jnp.pad(buf, ((head_rows, -(buf.shape[0] + head_rows) % SUBLANES), (0, 0)))


def _unpack_small(buf, shapes, head_rows=0):
    out, off = {}, head_rows
    for n, shape in shapes.items():
        nrow = -(-_size(shape) // LANES)
        out[n] = buf[off:off + nrow].reshape(-1)[:_size(shape)].reshape(shape)
        off += nrow
    return out


_WEIGHTS = ["w_in_even", "conv_a_w", "conv_b_w", "a_log", "dt_bias", "gdn_norm_g", "w_out_even", "w_in_odd",
            "lower_bounds", "hgrn_norm_g", "w_out_odd", "ln_g", "ln_b", "w_pl", "w_pl_gate"]


def kernel(x, p, w_in_even, conv_a_w, conv_b_w, a_log, dt_bias, gdn_norm_g, w_out_even, w_in_odd, lower_bounds, hgrn_norm_g, w_out_odd, ln_g, ln_b, w_pl, w_pl_gate, loss_target, m_w_in_even, m_conv_a_w, m_conv_b_w, m_a_log, m_dt_bias, m_gdn_norm_g, m_w_out_even, m_w_in_odd, m_lower_bounds, m_hgrn_norm_g, m_w_out_odd, m_ln_g, m_ln_b, m_w_pl, m_w_pl_gate, v_w_in_even, v_conv_a_w, v_conv_b_w, v_a_log, v_dt_bias, v_gdn_norm_g, v_w_out_even, v_w_in_odd, v_lower_bounds, v_hgrn_norm_g, v_w_out_odd, v_ln_g, v_ln_b, v_w_pl, v_w_pl_gate):
    w = dict(zip(_WEIGHTS, (w_in_even, conv_a_w, conv_b_w, a_log, dt_bias, gdn_norm_g, w_out_even, w_in_odd,
                            lower_bounds, hgrn_norm_g, w_out_odd, ln_g, ln_b, w_pl, w_pl_gate)))
    m = dict(zip(_WEIGHTS, (m_w_in_even, m_conv_a_w, m_conv_b_w, m_a_log, m_dt_bias, m_gdn_norm_g, m_w_out_even,
                            m_w_in_odd, m_lower_bounds, m_hgrn_norm_g, m_w_out_odd, m_ln_g, m_ln_b, m_w_pl, m_w_pl_gate)))
    v = dict(zip(_WEIGHTS, (v_w_in_even, v_conv_a_w, v_conv_b_w, v_a_log, v_dt_bias, v_gdn_norm_g, v_w_out_even,
                            v_w_in_odd, v_lower_bounds, v_hgrn_norm_g, v_w_out_odd, v_ln_g, v_ln_b, v_w_pl, v_w_pl_gate)))
    chip = 2 * lax.axis_index("x") + lax.axis_index("y")

    names = list(_BIG)
    shard_shapes = {n: _BIG[n][0] for n in names}
    shards = [w[n].reshape(shard_shapes[n]).astype(BF16) for n in names]
    gathered = _gather_shards(shards)
    full = {n: _BIG[n][1](lax.dynamic_update_slice(ga, sh[None], (chip, 0, 0)))
            for n, ga, sh in zip(names, gathered, shards)}
    conv_mine = _pack_small({n: w[n] for n in _CONV_SHARD}, _CONV_SHARD)
    conv_all = _exchange_small(conv_mine, False, "gather_conv")
    conv_by_chip = [_unpack_small(conv_all[2 * s], _CONV_SHARD) for s in range(4)]
    for n in _CONV_SHARD:
        full[n] = jnp.concatenate([conv_by_chip[s][n] for s in range(4)], axis=1)
    for n in _SMALL:
        if n not in _CONV_SHARD:
            full[n] = w[n]

    loss_part, dx, g = _local_step(x[0], p[:, 0], loss_target[0], full)

    g_big = dict(zip(names, _reduce_scatter([_BIG[n][2](g[n]) for n in names], names)))
    small_sum = _exchange_small(jnp.concatenate([loss_part, _pack_small(g, _SMALL)], axis=0), True, "reduce_small")
    loss = small_sum[0, 0]
    g_small = _unpack_small(small_sum, _SMALL, head_rows=SUBLANES)
    for n, (rows, cols) in _CONV_SHARD.items():
        g_small[n] = lax.dynamic_slice_in_dim(g_small[n], chip * cols, cols, axis=1)

    grads, delta, new_m, new_v = {}, {}, {}, {}
    for n in _BIG:
        shape2 = (-1, shard_shapes[n][-1])
        grads[n] = g_big[n].reshape(w[n].shape)
        d_, m_, v_ = _adamw(w[n].reshape(shape2), g_big[n].reshape(shape2), m[n].reshape(shape2), v[n].reshape(shape2),
                            f"adamw_{n}")
        delta[n], new_m[n], new_v[n] = (t.reshape(w[n].shape) for t in (d_, m_, v_))
    own = {n: (_CONV_SHARD[n] if n in _CONV_SHARD else _SMALL[n]) for n in _SMALL}
    packs = [_pack_small({n: src[n] for n in _SMALL}, own) for src in (w, g_small, m, v)]
    outs = [_unpack_small(t, own) for t in _adamw(*packs, "adamw_small")]
    for n in _SMALL:
        grads[n] = g_small[n].reshape(w[n].shape)
        delta[n], new_m[n], new_v[n] = (t[n].reshape(w[n].shape) for t in outs)
    return (loss, dx[None], *[grads[n] for n in _WEIGHTS], *[delta[n] for n in _WEIGHTS],
            *[new_m[n] for n in _WEIGHTS], *[new_v[n] for n in _WEIGHTS])
```

```python
import functools

import jax
import jax.numpy as jnp
from jax import lax
from jax.experimental import pallas as pl
from jax.experimental.pallas import tpu as pltpu

F32 = jnp.float32
BF16 = jnp.bfloat16
HI = lax.Precision.HIGHEST

D_MODEL = 1024
PL_DIM = 256
GDN_HEADS = 8
HEAD_DIM = 128
GDN_CHUNK = 64
HGRN_HEADS = 16
HGRN_CHUNK = 32
HGRN_WIDTH = 2048
DEEPNORM_ALPHA = 4.0 ** 0.25
NORM_EPS = 1e-5
ADAM_LR, ADAM_B1, ADAM_B2, ADAM_EPS, ADAM_WD, ADAM_STEP = 0.001, 0.9, 0.999, 1e-08, 0.01, 10

VMEM_LIMIT = 56 * 1024 * 1024
SUBLANES = 8
LANES = 128

_DIMS = {"nn": (((1,), (0,)), ((), ())), "nt": (((1,), (1,)), ((), ())), "tn": (((0,), (0,)), ((), ()))}


def _params(*sem):
    return pltpu.CompilerParams(dimension_semantics=sem, vmem_limit_bytes=VMEM_LIMIT)


ONE_PASS, THREE_PASS, FULL_F32 = 0, 1, 2


def _mm_raw(a, b, kind, prec):
    nb = a.ndim - 2
    ca = a.ndim - 1 if kind[0] == "n" else a.ndim - 2
    cb = b.ndim - 2 if kind[1] == "n" else b.ndim - 1
    dims = (((ca,), (cb,)), (tuple(range(nb)),) * 2)
    if prec == FULL_F32:
        return lax.dot_general(a, b, dims, precision=HI, preferred_element_type=F32)
    dot = lambda p, q: lax.dot_general(p, q, dims, preferred_element_type=F32)
    ah, bh = a.astype(BF16), b.astype(BF16)
    if prec == ONE_PASS:
        return dot(ah, bh)
    al = (a - ah.astype(F32)).astype(BF16)
    bl = (b - bh.astype(F32)).astype(BF16)
    return dot(ah, bh) + (dot(ah, bl) + dot(al, bh))


@functools.partial(jax.custom_vjp, nondiff_argnums=(2, 3))
def _mm_vjp(a, b, kind, hi):
    return _mm_raw(a, b, kind, hi)


def _mm_vjp_fwd(a, b, kind, hi):
    return _mm_raw(a, b, kind, hi), (a, b)


def _mm_vjp_bwd(kind, hi, res, dc):
    a, b = res
    if kind == "nn":
        return _mm_raw(dc, b, "nt", hi), _mm_raw(a, dc, "tn", hi)
    if kind == "nt":
        return _mm_raw(dc, b, "nn", hi), _mm_raw(dc, a, "tn", hi)
    return _mm_raw(b, dc, "nt", hi), _mm_raw(a, dc, "nn", hi)


_mm_vjp.defvjp(_mm_vjp_fwd, _mm_vjp_bwd)


def _lane_total(v):
    return jnp.broadcast_to(jnp.sum(v, axis=-1, keepdims=True), v.shape)


def _matmul(a, b, *, name, ta=False, tb=False, out_dtype=F32, add=None, add_scale=1.0, tm=1024, tn=1024, tk=1024):
    m, k = (a.shape[1], a.shape[0]) if ta else a.shape
    n = b.shape[0] if tb else b.shape[1]
    tm, tn, tk = min(tm, m), min(tn, n), min(tk, k)
    assert m % tm == 0 and n % tn == 0 and k % tk == 0, (name, m, n, k)
    nk = k // tk
    kind = ("t" if ta else "n") + ("t" if tb else "n")
    dims = (((0 if ta else 1,), (1 if tb else 0,)), ((), ()))

    def body(*refs):
        if add is None:
            a_ref, b_ref, o_ref, acc_ref = refs
        else:
            a_ref, b_ref, add_ref, o_ref, acc_ref = refs
        kk = pl.program_id(2)

        @pl.when(kk == 0)
        def _():
            acc_ref[...] = jnp.zeros_like(acc_ref)

        acc_ref[...] += lax.dot_general(a_ref[...].astype(BF16), b_ref[...].astype(BF16), dims,
                                        preferred_element_type=F32)

        @pl.when(kk == nk - 1)
        def _():
            r = acc_ref[...]
            if add is not None:
                r = r + add_scale * add_ref[...].astype(F32)
            o_ref[...] = r.astype(out_dtype)

    a_spec = pl.BlockSpec((tk, tm), lambda i, j, kk: (kk, i)) if ta else pl.BlockSpec((tm, tk), lambda i, j, kk: (i, kk))
    b_spec = pl.BlockSpec((tn, tk), lambda i, j, kk: (j, kk)) if tb else pl.BlockSpec((tk, tn), lambda i, j, kk: (kk, j))
    o_spec = pl.BlockSpec((tm, tn), lambda i, j, kk: (i, j))
    in_specs = [a_spec, b_spec] + ([o_spec] if add is not None else [])
    args = (a, b) + ((add,) if add is not None else ())
    del kind
    return pl.pallas_call(
        body, name=name, grid=(m // tm, n // tn, nk), in_specs=in_specs, out_specs=o_spec,
        out_shape=jax.ShapeDtypeStruct((m, n), out_dtype), scratch_shapes=[pltpu.VMEM((tm, tn), F32)],
        compiler_params=_params("parallel", "parallel", "arbitrary"))(*args)


HALO = SUBLANES


def _halo_specs(tt, width, col, nt):
    r = tt // HALO
    prev = pl.BlockSpec((HALO, width), lambda i: (jnp.maximum(i * r - 1, 0), col))
    nxt = pl.BlockSpec((HALO, width), lambda i: (jnp.minimum((i + 1) * r, nt * r - 1), col))
    return prev, nxt


def _shift_down(ext, k):
    return ext if k == 0 else pltpu.roll(ext, k, 0)


def _shift_up(ext, k):
    return ext if k == 0 else pltpu.roll(ext, ext.shape[0] - k, 0)


def _causal_conv(ext, w, taps):
    acc = None
    for j in range(taps):
        term = w[j:j + 1, :] * _shift_down(ext, taps - 1 - j)
        acc = term if acc is None else acc + term
    return acc[HALO:, :]


def _conv_a_fwd(proj_a, conv_w):
    t = proj_a.shape[0]
    tt = min(t, 256)
    nt = t // tt
    wdt = 1024

    def body(cur_ref, prev_ref, w_ref, y_ref):
        i = pl.program_id(0)
        cur = cur_ref[...]
        h, c, b, z = (cur[:, k * wdt:(k + 1) * wdt] for k in range(4))
        prev = prev_ref[...]
        u_prev = jnp.where(i > 0, prev[:, wdt:2 * wdt] * prev[:, 0:wdt], 0.0)
        ext = jnp.concatenate([u_prev, c * h], axis=0)
        conv = _causal_conv(ext, w_ref[...], 3)
        y_ref[...] = (b * conv * jax.nn.silu(z)).astype(BF16)

    prev_spec, _ = _halo_specs(tt, 4 * wdt, 0, nt)
    return pl.pallas_call(
        body, name="conv_a_fwd", grid=(nt,),
        in_specs=[pl.BlockSpec((tt, 4 * wdt), lambda i: (i, 0)), prev_spec, pl.BlockSpec((3, wdt), lambda i: (0, 0))],
        out_specs=pl.BlockSpec((tt, wdt), lambda i: (i, 0)),
        out_shape=jax.ShapeDtypeStruct((t, wdt), BF16), compiler_params=_params("parallel"))(proj_a, proj_a, conv_w)


def _conv_a_bwd(proj_a, conv_w, dy):
    t = proj_a.shape[0]
    tt = min(t, 256)
    nt = t // tt
    wdt = 1024

    def body(cur_ref, prev_ref, nxt_ref, w_ref, dy_ref, dyn_ref, d_ref, dw_ref):
        i = pl.program_id(0)
        w = w_ref[...]
        cur, prev, nxt = cur_ref[...], prev_ref[...], nxt_ref[...]
        split = lambda a: tuple(a[:, k * wdt:(k + 1) * wdt] for k in range(4))
        h, c, b, z = split(cur)
        hp, cp, _, _ = split(prev)
        hn, cn, bn, zn = split(nxt)
        u_prev = jnp.where(i > 0, cp * hp, 0.0)
        u_ext = jnp.concatenate([u_prev, c * h, cn * hn], axis=0)
        taps = [_shift_down(u_ext, 2 - j)[HALO:, :] for j in range(3)]
        conv = w[0:1, :] * taps[0] + w[1:2, :] * taps[1] + w[2:3, :] * taps[2]
        b_cn = jnp.concatenate([b, bn], axis=0)
        z_cn = jnp.concatenate([z, zn], axis=0)
        dy_cn = jnp.concatenate([dy_ref[...], jnp.where(i < nt - 1, dyn_ref[...], 0.0)], axis=0)
        sg = jax.nn.sigmoid(z_cn)
        silu = z_cn * sg
        d_conv = dy_cn * b_cn * silu
        db = (dy_cn * conv * silu)[:tt, :]
        dz = (dy_cn * b_cn * conv * (sg * (1.0 + z_cn * (1.0 - sg))))[:tt, :]
        du = None
        for j in range(3):
            term = w[j:j + 1, :] * _shift_up(d_conv, 2 - j)
            du = term if du is None else du + term
        du = du[:tt, :]
        d_ref[...] = jnp.concatenate([du * c, du * h, db, dz], axis=1).astype(BF16)

        @pl.when(i == 0)
        def _():
            dw_ref[...] = jnp.zeros_like(dw_ref)

        d_cur = d_conv[:tt, :]
        rows = [jnp.sum(d_cur * taps[j][:tt, :], axis=0, keepdims=True) for j in range(3)]
        dw_ref[0:3, :] += jnp.concatenate(rows, axis=0)

    prev_spec, nxt_spec = _halo_specs(tt, 4 * wdt, 0, nt)
    _, dyn_spec = _halo_specs(tt, wdt, 0, nt)
    return pl.pallas_call(
        body, name="conv_a_bwd", grid=(nt,),
        in_specs=[pl.BlockSpec((tt, 4 * wdt), lambda i: (i, 0)), prev_spec, nxt_spec,
                  pl.BlockSpec((3, wdt), lambda i: (0, 0)), pl.BlockSpec((tt, wdt), lambda i: (i, 0)), dyn_spec],
        out_specs=[pl.BlockSpec((tt, 4 * wdt), lambda i: (i, 0)), pl.BlockSpec((SUBLANES, wdt), lambda i: (0, 0))],
        out_shape=[jax.ShapeDtypeStruct((t, 4 * wdt), BF16), jax.ShapeDtypeStruct((SUBLANES, wdt), F32)],
        compiler_params=_params("arbitrary"))(proj_a, proj_a, proj_a, conv_w, dy, dy)


def _conv_b_fwd(proj_qkv, conv_w):
    t, width = proj_qkv.shape
    tt = min(t, 256)
    nt = t // tt
    wdt = 1024

    def body(cur_ref, prev_ref, w_ref, y_ref):
        i = pl.program_id(1)
        ext = jnp.concatenate([jnp.where(i > 0, prev_ref[...], 0.0), cur_ref[...]], axis=0)
        y_ref[...] = jax.nn.silu(_causal_conv(ext, w_ref[...], 4))

    r = tt // HALO
    return pl.pallas_call(
        body, name="conv_b_fwd", grid=(width // wdt, nt),
        in_specs=[pl.BlockSpec((tt, wdt), lambda j, i: (i, j)),
                  pl.BlockSpec((HALO, wdt), lambda j, i: (jnp.maximum(i * r - 1, 0), j)),
                  pl.BlockSpec((4, wdt), lambda j, i: (0, j))],
        out_specs=pl.BlockSpec((tt, wdt), lambda j, i: (i, j)),
        out_shape=jax.ShapeDtypeStruct((t, width), F32), compiler_params=_params("parallel", "parallel"))(
            proj_qkv, proj_qkv, conv_w)


def _conv_b_bwd(proj_qkv, conv_w, d_act, col, name):
    t = proj_qkv.shape[0]
    tt = min(t, 256)
    nt = t // tt
    wdt = 1024

    def body(cur_ref, prev_ref, nxt_ref, w_ref, da_ref, dan_ref, d_ref, dw_ref):
        i = pl.program_id(0)
        w = w_ref[...]
        u_ext = jnp.concatenate([jnp.where(i > 0, prev_ref[...], 0.0), cur_ref[...], nxt_ref[...]], axis=0)
        taps = [_shift_down(u_ext, 3 - j)[HALO:, :] for j in range(4)]
        conv = w[0:1, :] * taps[0] + w[1:2, :] * taps[1] + w[2:3, :] * taps[2] + w[3:4, :] * taps[3]
        da_cn = jnp.concatenate([da_ref[...], jnp.where(i < nt - 1, dan_ref[...], 0.0)], axis=0)
        sg = jax.nn.sigmoid(conv)
        d_conv = da_cn * (sg * (1.0 + conv * (1.0 - sg)))
        du = None
        for j in range(4):
            term = w[j:j + 1, :] * _shift_up(d_conv, 3 - j)
            du = term if du is None else du + term
        d_ref[...] = du[:tt, :].astype(BF16)

        @pl.when(i == 0)
        def _():
            dw_ref[...] = jnp.zeros_like(dw_ref)

        d_cur = d_conv[:tt, :]
        rows = [jnp.sum(d_cur * taps[j][:tt, :], axis=0, keepdims=True) for j in range(4)]
        dw_ref[0:4, :] += jnp.concatenate(rows, axis=0)

    prev_spec, nxt_spec = _halo_specs(tt, wdt, col, nt)
    _, dan_spec = _halo_specs(tt, wdt, 0, nt)
    return pl.pallas_call(
        body, name=name, grid=(nt,),
        in_specs=[pl.BlockSpec((tt, wdt), lambda i: (i, col)), prev_spec, nxt_spec,
                  pl.BlockSpec((4, wdt), lambda i: (0, col)), pl.BlockSpec((tt, wdt), lambda i: (i, 0)), dan_spec],
        out_specs=[pl.BlockSpec((tt, wdt), lambda i: (i, 0)), pl.BlockSpec((SUBLANES, wdt), lambda i: (0, 0))],
        out_shape=[jax.ShapeDtypeStruct((t, wdt), BF16), jax.ShapeDtypeStruct((SUBLANES, wdt), F32)],
        compiler_params=_params("arbitrary"))(proj_qkv, proj_qkv, proj_qkv, conv_w, d_act, d_act)


def _rms_gate(o, gn, z):
    on = o * lax.rsqrt(jnp.mean(o * o, axis=-1, keepdims=True) + NORM_EPS) * gn
    return on * jax.nn.silu(z)


def _gdn_chunk(mm, qa, ka, va, z, braw, araw, alog, dtb, gn, state):
    c = GDN_CHUNK
    q = qa * lax.rsqrt(jnp.sum(qa * qa, axis=-1, keepdims=True) + 1e-6) * (HEAD_DIM ** -0.5)
    k = ka * lax.rsqrt(jnp.sum(ka * ka, axis=-1, keepdims=True) + 1e-6)
    beta = jax.nn.sigmoid(braw)
    g = -jnp.exp(alog) * jax.nn.softplus(araw + dtb)
    ri = lax.broadcasted_iota(jnp.int32, (c, c), 0)
    ci = lax.broadcasted_iota(jnp.int32, (c, c), 1)
    incl, strict, eye = ri >= ci, ri > ci, ri == ci
    gc = mm(incl.astype(F32), g, "nn", True)
    gc_i = gc[:, :c]
    gc_j = mm(jnp.ones((c, c), F32), jnp.where(eye, gc_i, 0.0), "nn", True)
    decay = jnp.where(incl, jnp.exp(jnp.where(incl, gc_i - gc_j, 0.0)), 0.0)
    kb = k * beta
    low = jnp.where(strict, mm(kb, k, "nt", False) * decay, 0.0)
    x = -low
    inv = eye.astype(F32) + x
    for _ in range(5):
        x = mm(x, x, "nn", True)
        inv = inv + mm(inv, x, "nn", True)
    egc = jnp.exp(gc)
    u = mm(inv, va * beta, "nn", True)
    w = mm(inv, kb * egc, "nn", True)
    attn = jnp.where(incl, mm(q, k, "nt", False) * decay, 0.0)
    g_last = jnp.sum(g, axis=0, keepdims=True)
    v_new = u - mm(w, state, "nn", False)
    o = mm(q * egc, state, "nn", False) + mm(attn, v_new, "nn", False)
    new_state = state * jnp.exp(g_last) + mm(k * jnp.exp(g_last - gc), v_new, "tn", False)
    return _rms_gate(o, gn, z), new_state


def _gdn_specs(tb, rev, nt):
    ti = (lambda i: nt - 1 - i) if rev else (lambda i: i)
    col = lambda off: pl.BlockSpec((tb, HEAD_DIM), lambda h, i: (ti(i), off + h))
    rep = pl.BlockSpec((1, tb, LANES), lambda h, i: (h, ti(i), 0))
    par = pl.BlockSpec((1, SUBLANES, LANES), lambda h, i: (h, 0, 0))
    gn = pl.BlockSpec((SUBLANES, LANES), lambda h, i: (0, 0))
    hist = pl.BlockSpec((1, tb // GDN_CHUNK, HEAD_DIM, HEAD_DIM), lambda h, i: (h, ti(i), 0, 0))
    return col, rep, par, gn, hist


def _gdn_fwd(qkv_act, zb, braw, araw, alog, dtb, gn):
    t = qkv_act.shape[0]
    tb = min(t, 256)
    nt, nc = t // tb, tb // GDN_CHUNK

    def body(q_ref, k_ref, v_ref, z_ref, br_ref, ar_ref, al_ref, dt_ref, gn_ref, y_ref, hist_ref, s_ref):
        @pl.when(pl.program_id(1) == 0)
        def _():
            s_ref[...] = jnp.zeros_like(s_ref)

        al, dt, g = al_ref[0, 0:1, :], dt_ref[0, 0:1, :], gn_ref[0:1, :]
        state = s_ref[...]
        for c in range(nc):
            sl = pl.ds(c * GDN_CHUNK, GDN_CHUNK)
            hist_ref[0, c] = state
            y, state = _gdn_chunk(_mm_raw, q_ref[sl, :], k_ref[sl, :], v_ref[sl, :], z_ref[sl, :],
                                  br_ref[0, sl, :], ar_ref[0, sl, :], al, dt, g, state)
            y_ref[sl, :] = y.astype(BF16)
        s_ref[...] = state

    col, rep, par, gns, hist = _gdn_specs(tb, False, nt)
    return pl.pallas_call(
        body, name="gdn_fwd", grid=(GDN_HEADS, nt),
        in_specs=[col(0), col(GDN_HEADS), col(2 * GDN_HEADS), col(0), rep, rep, par, par, gns],
        out_specs=[col(0), hist],
        out_shape=[jax.ShapeDtypeStruct((t, GDN_HEADS * HEAD_DIM), BF16),
                   jax.ShapeDtypeStruct((GDN_HEADS, t // GDN_CHUNK, HEAD_DIM, HEAD_DIM), F32)],
        scratch_shapes=[pltpu.VMEM((HEAD_DIM, HEAD_DIM), F32)],
        compiler_params=_params("parallel", "arbitrary"))(qkv_act, qkv_act, qkv_act, zb, braw, araw, alog, dtb, gn)


def _gdn_bwd(qkv_act, zb, braw, araw, alog, dtb, gn, hist, dy):
    t = qkv_act.shape[0]
    tb = min(t, 256)
    nt, nc = t // tb, tb // GDN_CHUNK

    def body(q_ref, k_ref, v_ref, z_ref, br_ref, ar_ref, al_ref, dt_ref, gn_ref, hist_ref, dy_ref,
             dq_ref, dk_ref, dv_ref, dz_ref, dbr_ref, dar_ref, dal_ref, ddt_ref, dgn_ref, ds_ref):
        @pl.when(pl.program_id(1) == 0)
        def _():
            ds_ref[...] = jnp.zeros_like(ds_ref)
            dal_ref[...] = jnp.zeros_like(dal_ref)
            ddt_ref[...] = jnp.zeros_like(ddt_ref)
            dgn_ref[...] = jnp.zeros_like(dgn_ref)

        al, dt, g = al_ref[0, 0:1, :], dt_ref[0, 0:1, :], gn_ref[0:1, :]
        d_state = ds_ref[...]
        for c in reversed(range(nc)):
            sl = pl.ds(c * GDN_CHUNK, GDN_CHUNK)
            _, vjp = jax.vjp(functools.partial(_gdn_chunk, _mm_vjp), q_ref[sl, :], k_ref[sl, :], v_ref[sl, :],
                             z_ref[sl, :], br_ref[0, sl, :], ar_ref[0, sl, :], al, dt, g, hist_ref[0, c])
            dq, dk, dv, dz, dbr, dar, dal, ddt, dgn, d_state = vjp((dy_ref[sl, :], d_state))
            dq_ref[sl, :] = dq
            dk_ref[sl, :] = dk
            dv_ref[sl, :] = dv
            dz_ref[sl, :] = dz.astype(BF16)
            dbr_ref[0, sl, :] = _lane_total(dbr)
            dar_ref[0, sl, :] = _lane_total(dar)
            dal_ref[0, 0:1, :] += _lane_total(dal)
            ddt_ref[0, 0:1, :] += _lane_total(ddt)
            dgn_ref[0, 0:1, :] += dgn
        ds_ref[...] = d_state

    col, rep, par, gns, hists = _gdn_specs(tb, True, nt)
    width = GDN_HEADS * HEAD_DIM
    return pl.pallas_call(
        body, name="gdn_bwd", grid=(GDN_HEADS, nt),
        in_specs=[col(0), col(GDN_HEADS), col(2 * GDN_HEADS), col(0), rep, rep, par, par, gns, hists, col(0)],
        out_specs=[col(0), col(0), col(0), col(0), rep, rep, par, par, par],
        out_shape=[jax.ShapeDtypeStruct((t, width), F32)] * 3 + [jax.ShapeDtypeStruct((t, width), BF16)]
        + [jax.ShapeDtypeStruct((GDN_HEADS, t, LANES), F32)] * 2
        + [jax.ShapeDtypeStruct((GDN_HEADS, SUBLANES, LANES), F32)] * 3,
        scratch_shapes=[pltpu.VMEM((HEAD_DIM, HEAD_DIM), F32)],
        compiler_params=_params("parallel", "arbitrary"))(qkv_act, qkv_act, qkv_act, zb, braw, araw, alog, dtb, gn, hist, dy)


def _hgrn_chunk(mm, qr, fr, iv, z, lbl, gn, state):
    c = HGRN_CHUNK
    lb = jax.nn.sigmoid(lbl[1:2, :] - lbl[0:1, :])
    f = lb + (1.0 - lb) * jax.nn.sigmoid(fr)
    q = jax.nn.silu(qr)
    k = 1.0 - f
    logf = jnp.log(f)
    ri = lax.broadcasted_iota(jnp.int32, (c, c), 0)
    ci = lax.broadcasted_iota(jnp.int32, (c, c), 1)
    b = mm((ri >= ci).astype(F32), logf, "nn", True)
    ti = lax.broadcasted_iota(jnp.int32, (c, c, HEAD_DIM), 0)
    si = lax.broadcasted_iota(jnp.int32, (c, c, HEAD_DIM), 1)
    m3 = ti >= si
    decay = jnp.where(m3, jnp.exp(jnp.where(m3, b[:, None, :] - b[None, :, :], 0.0)), 0.0)
    attn = jnp.sum(q[:, None, :] * k[None, :, :] * decay, axis=-1)
    o = mm(q * jnp.exp(b), state, "nt", False) + mm(attn, iv, "nn", False)
    b_last = jnp.sum(logf, axis=0, keepdims=True)
    new_state = state * jnp.exp(b_last) + mm(iv, k * jnp.exp(b_last - b), "tn", False)
    return _rms_gate(o, gn, z), new_state


def _hgrn_specs(tb, rev, nt):
    ti = (lambda i: nt - 1 - i) if rev else (lambda i: i)
    col = lambda off: pl.BlockSpec((tb, HEAD_DIM), lambda h, i: (ti(i), off + h))
    lbs = pl.BlockSpec((2, HEAD_DIM), lambda h, i: (0, h))
    gn = pl.BlockSpec((SUBLANES, LANES), lambda h, i: (0, 0))
    hist = pl.BlockSpec((1, tb // HGRN_CHUNK, HEAD_DIM, HEAD_DIM), lambda h, i: (h, ti(i), 0, 0))
    return col, lbs, gn, hist


def _hgrn_fwd(proj4, lower_bounds, gn):
    t = proj4[0].shape[0]
    tb = min(t, 256)
    nt, nc = t // tb, tb // HGRN_CHUNK

    def body(q_ref, f_ref, i_ref, z_ref, lb_ref, gn_ref, y_ref, hist_ref, s_ref):
        @pl.when(pl.program_id(1) == 0)
        def _():
            s_ref[...] = jnp.zeros_like(s_ref)

        lbl, g = lb_ref[...], gn_ref[0:1, :]
        state = s_ref[...]
        for c in range(nc):
            sl = pl.ds(c * HGRN_CHUNK, HGRN_CHUNK)
            hist_ref[0, c] = state
            y, state = _hgrn_chunk(_mm_raw, q_ref[sl, :], f_ref[sl, :], i_ref[sl, :], z_ref[sl, :], lbl, g, state)
            y_ref[sl, :] = y.astype(BF16)
        s_ref[...] = state

    col, lbs, gns, hist = _hgrn_specs(tb, False, nt)
    hh = HGRN_HEADS
    return pl.pallas_call(
        body, name="hgrn_fwd", grid=(hh, nt),
        in_specs=[col(0), col(0), col(0), col(0), lbs, gns],
        out_specs=[col(0), hist],
        out_shape=[jax.ShapeDtypeStruct((t, HGRN_WIDTH), BF16),
                   jax.ShapeDtypeStruct((hh, t // HGRN_CHUNK, HEAD_DIM, HEAD_DIM), F32)],
        scratch_shapes=[pltpu.VMEM((HEAD_DIM, HEAD_DIM), F32)],
        compiler_params=_params("parallel", "arbitrary"))(*proj4, lower_bounds, gn)


def _hgrn_bwd(proj4, lower_bounds, gn, hist, dy):
    t = proj4[0].shape[0]
    tb = min(t, 256)
    nt, nc = t // tb, tb // HGRN_CHUNK

    def body(q_ref, f_ref, i_ref, z_ref, lb_ref, gn_ref, hist_ref, dy_ref,
             dq_ref, df_ref, di_ref, dz_ref, dlb_ref, dgn_ref, ds_ref):
        @pl.when(pl.program_id(1) == 0)
        def _():
            ds_ref[...] = jnp.zeros_like(ds_ref)
            dlb_ref[...] = jnp.zeros_like(dlb_ref)
            dgn_ref[...] = jnp.zeros_like(dgn_ref)

        lbl, g = lb_ref[...], gn_ref[0:1, :]
        d_state = ds_ref[...]
        for c in reversed(range(nc)):
            sl = pl.ds(c * HGRN_CHUNK, HGRN_CHUNK)
            _, vjp = jax.vjp(functools.partial(_hgrn_chunk, _mm_vjp), q_ref[sl, :], f_ref[sl, :], i_ref[sl, :],
                             z_ref[sl, :], lbl, g, hist_ref[0, c])
            dq, df, di, dz, dlb, dgn, d_state = vjp((dy_ref[sl, :], d_state))
            dq_ref[sl, :] = dq.astype(BF16)
            df_ref[sl, :] = df.astype(BF16)
            di_ref[sl, :] = di.astype(BF16)
            dz_ref[sl, :] = dz.astype(BF16)
            dlb_ref[...] += dlb
            dgn_ref[0, 0:1, :] += dgn
        ds_ref[...] = d_state

    col, lbs, gns, hists = _hgrn_specs(tb, True, nt)
    hh = HGRN_HEADS
    par = pl.BlockSpec((1, SUBLANES, LANES), lambda h, i: (h, 0, 0))
    return pl.pallas_call(
        body, name="hgrn_bwd", grid=(hh, nt),
        in_specs=[col(0), col(0), col(0), col(0), lbs, gns, hists, col(0)],
        out_specs=[col(0), col(0), col(0), col(0), lbs, par],
        out_shape=[jax.ShapeDtypeStruct((t, HGRN_WIDTH), BF16)] * 4
        + [jax.ShapeDtypeStruct((2, HGRN_WIDTH), F32), jax.ShapeDtypeStruct((hh, SUBLANES, LANES), F32)],
        scratch_shapes=[pltpu.VMEM((HEAD_DIM, HEAD_DIM), F32)],
        compiler_params=_params("parallel", "arbitrary"))(*proj4, lower_bounds, gn, hist, dy)


GDN_PREP_ROWS = 512


def _unit_lower_inverse(low):
    c = low.shape[-1]
    eye = lax.broadcasted_iota(jnp.int32, low.shape, low.ndim - 2) == lax.broadcasted_iota(jnp.int32, low.shape, low.ndim - 1)
    x = -low
    inv = eye.astype(F32) + x
    for _ in range(c.bit_length() - 2):
        x = _mm_raw(x, x, "nn", THREE_PASS)
        inv = inv + _mm_raw(inv, x, "nn", THREE_PASS)
    return inv


@jax.custom_vjp
def _unit_lower_inverse_vjp(low):
    return _unit_lower_inverse(low)


def _unit_lower_inverse_fwd(low):
    inv = _unit_lower_inverse(low)
    return inv, inv


def _unit_lower_inverse_bwd(inv, d_inv):
    return (-_mm_raw(_mm_raw(inv, d_inv, "tn", THREE_PASS), inv, "nt", THREE_PASS),)


_unit_lower_inverse_vjp.defvjp(_unit_lower_inverse_fwd, _unit_lower_inverse_bwd)


def _gdn_prep(mm, qa, ka, va, braw, araw, alog, dtb):
    n, c, _ = qa.shape
    q = qa * lax.rsqrt(jnp.sum(qa * qa, axis=-1, keepdims=True) + 1e-6) * (HEAD_DIM ** -0.5)
    k = ka * lax.rsqrt(jnp.sum(ka * ka, axis=-1, keepdims=True) + 1e-6)
    beta = jax.nn.sigmoid(braw)
    g = -jnp.exp(alog) * jax.nn.softplus(araw + dtb)
    ri = lax.broadcasted_iota(jnp.int32, (n, c, c), 1)
    ci = lax.broadcasted_iota(jnp.int32, (n, c, c), 2)
    incl, strict, eye = ri >= ci, ri > ci, ri == ci
    gc = mm(incl.astype(F32), g, "nn", FULL_F32)
    gc_i = gc[:, :, :c]
    gc_j = mm(jnp.ones((n, c, c), F32), jnp.where(eye, gc_i, 0.0), "nn", FULL_F32)
    decay = jnp.where(incl, jnp.exp(jnp.where(incl, gc_i - gc_j, 0.0)), 0.0)
    kb = k * beta
    low = jnp.where(strict, mm(kb, k, "nt", ONE_PASS) * decay, 0.0)
    inv = _unit_lower_inverse(low) if mm is _mm_raw else _unit_lower_inverse_vjp(low)
    egc = jnp.exp(gc)
    u = mm(inv, va * beta, "nn", THREE_PASS)
    w = mm(inv, kb * egc, "nn", THREE_PASS)
    attn = jnp.where(incl, mm(q, k, "nt", ONE_PASS) * decay, 0.0)
    g_last = jnp.sum(g, axis=1, keepdims=True)
    return u, w, q * egc, k * jnp.exp(g_last - gc), attn, jnp.exp(g_last)


def _gdn_scan(mm, u, w, qd, kd, attn, egl, z, gn, state):
    v_new = u - mm(w, state, "nn", ONE_PASS)
    o = mm(qd, state, "nn", ONE_PASS) + mm(attn, v_new, "nn", ONE_PASS)
    new_state = state * egl + mm(kd, v_new, "tn", ONE_PASS)
    return _rms_gate(o, gn, z), new_state


def _chunks(ref_value, n, c):
    return ref_value.reshape(n, c, ref_value.shape[-1])


def _by_head(ref, rows, heads):
    return jnp.stack([ref[rows, pl.ds(h * HEAD_DIM, HEAD_DIM)] for h in range(heads)])


def _store_heads(ref, rows, value):
    for h in range(value.shape[0]):
        ref[rows, pl.ds(h * HEAD_DIM, HEAD_DIM)] = value[h]


def _gdn_prep_specs(tb, nt_unused=None):
    col = lambda off: pl.BlockSpec((tb, HEAD_DIM), lambda h, i: (i, off + h))
    rep = pl.BlockSpec((1, tb, LANES), lambda h, i: (h, i, 0))
    par = pl.BlockSpec((1, SUBLANES, LANES), lambda h, i: (h, 0, 0))
    att = pl.BlockSpec((1, tb, GDN_CHUNK), lambda h, i: (h, i, 0))
    egl = pl.BlockSpec((1, tb // GDN_CHUNK, SUBLANES, LANES), lambda h, i: (h, i, 0, 0))
    return col, rep, par, att, egl


def _gdn_prep_fwd(qkv_act, braw, araw, alog, dtb):
    t = qkv_act.shape[0]
    tb = min(t, GDN_PREP_ROWS)
    nt, nc = t // tb, tb // GDN_CHUNK
    width = GDN_HEADS * HEAD_DIM

    def body(q_ref, k_ref, v_ref, br_ref, ar_ref, al_ref, dt_ref, u_ref, w_ref, qd_ref, kd_ref, at_ref, eg_ref):
        ch = lambda r: _chunks(r, nc, GDN_CHUNK)
        u, w, qd, kd, attn, egl = _gdn_prep(_mm_raw, ch(q_ref[...]), ch(k_ref[...]), ch(v_ref[...]), ch(br_ref[0]),
                                            ch(ar_ref[0]), al_ref[0, 0:1, :], dt_ref[0, 0:1, :])
        u_ref[...] = u.reshape(tb, HEAD_DIM)
        w_ref[...] = w.reshape(tb, HEAD_DIM).astype(BF16)
        qd_ref[...] = qd.reshape(tb, HEAD_DIM).astype(BF16)
        kd_ref[...] = kd.reshape(tb, HEAD_DIM).astype(BF16)
        at_ref[0] = attn.reshape(tb, GDN_CHUNK).astype(BF16)
        eg_ref[0] = jnp.broadcast_to(egl, (nc, SUBLANES, LANES))

    col, rep, par, att, egl = _gdn_prep_specs(tb)
    h = GDN_HEADS
    return pl.pallas_call(
        body, name="gdn_prep_fwd", grid=(h, nt),
        in_specs=[col(0), col(h), col(2 * h), rep, rep, par, par],
        out_specs=[col(0), col(0), col(0), col(0), att, egl],
        out_shape=[jax.ShapeDtypeStruct((t, width), F32)] + [jax.ShapeDtypeStruct((t, width), BF16)] * 3
        + [jax.ShapeDtypeStruct((h, t, GDN_CHUNK), BF16), jax.ShapeDtypeStruct((h, t // GDN_CHUNK, SUBLANES, LANES), F32)],
        compiler_params=_params("parallel", "parallel"))(qkv_act, qkv_act, qkv_act, braw, araw, alog, dtb)


def _gdn_prep_bwd(qkv_act, braw, araw, alog, dtb, du, dw, dqd, dkd, dattn, degl):
    t = qkv_act.shape[0]
    tb = min(t, GDN_PREP_ROWS)
    nt, nc = t // tb, tb // GDN_CHUNK
    width = GDN_HEADS * HEAD_DIM

    def body(q_ref, k_ref, v_ref, br_ref, ar_ref, al_ref, dt_ref, du_ref, dw_ref, dqd_ref, dkd_ref, dat_ref, deg_ref,
             dq_ref, dk_ref, dv_ref, dbr_ref, dar_ref, dal_ref, ddt_ref):
        @pl.when(pl.program_id(1) == 0)
        def _():
            dal_ref[...] = jnp.zeros_like(dal_ref)
            ddt_ref[...] = jnp.zeros_like(ddt_ref)

        ch = lambda r: _chunks(r, nc, GDN_CHUNK)
        _, vjp = jax.vjp(functools.partial(_gdn_prep, _mm_vjp), ch(q_ref[...]), ch(k_ref[...]), ch(v_ref[...]),
                         ch(br_ref[0]), ch(ar_ref[0]), al_ref[0, 0:1, :], dt_ref[0, 0:1, :])
        dq, dk, dv, dbr, dar, dal, ddt = vjp((ch(du_ref[...]), ch(dw_ref[...]), ch(dqd_ref[...]), ch(dkd_ref[...]),
                                              ch(dat_ref[0]), deg_ref[0][:, 0:1, :]))
        dq_ref[...] = dq.reshape(tb, HEAD_DIM)
        dk_ref[...] = dk.reshape(tb, HEAD_DIM)
        dv_ref[...] = dv.reshape(tb, HEAD_DIM)
        dbr_ref[0] = _lane_total(dbr.reshape(tb, LANES))
        dar_ref[0] = _lane_total(dar.reshape(tb, LANES))
        dal_ref[0, 0:1, :] += _lane_total(dal)
        ddt_ref[0, 0:1, :] += _lane_total(ddt)

    col, rep, par, att, egl = _gdn_prep_specs(tb)
    h = GDN_HEADS
    return pl.pallas_call(
        body, name="gdn_prep_bwd", grid=(h, nt),
        in_specs=[col(0), col(h), col(2 * h), rep, rep, par, par, col(0), col(0), col(0), col(0), att, egl],
        out_specs=[col(0), col(0), col(0), rep, rep, par, par],
        out_shape=[jax.ShapeDtypeStruct((t, width), F32)] * 3 + [jax.ShapeDtypeStruct((h, t, LANES), F32)] * 2
        + [jax.ShapeDtypeStruct((h, SUBLANES, LANES), F32)] * 2,
        compiler_params=_params("parallel", "arbitrary"))(qkv_act, qkv_act, qkv_act, braw, araw, alog, dtb,
                                                         du, dw, dqd, dkd, dattn, degl)


def _scan_specs(tb, heads, chunk, rev, nt):
    ti = (lambda i: nt - 1 - i) if rev else (lambda i: i)
    row = pl.BlockSpec((tb, heads * HEAD_DIM), lambda i: (ti(i), 0))
    att = pl.BlockSpec((heads, tb, chunk), lambda i: (0, ti(i), 0))
    egl = pl.BlockSpec((heads, tb // chunk, SUBLANES, LANES), lambda i: (0, ti(i), 0, 0))
    hist = pl.BlockSpec((heads, tb // chunk, HEAD_DIM, HEAD_DIM), lambda i: (0, ti(i), 0, 0))
    gn = pl.BlockSpec((SUBLANES, LANES), lambda i: (0, 0))
    return row, att, egl, hist, gn


def _gdn_scan_fwd(u, w, qd, kd, attn, egl, zb, gn):
    t = u.shape[0]
    tb = min(t, 256)
    nt, nc = t // tb, tb // GDN_CHUNK
    nh = GDN_HEADS

    def body(u_ref, w_ref, qd_ref, kd_ref, at_ref, eg_ref, z_ref, gn_ref, y_ref, hist_ref, s_ref):
        @pl.when(pl.program_id(0) == 0)
        def _():
            s_ref[...] = jnp.zeros_like(s_ref)

        g = gn_ref[0:1, :]
        state = s_ref[...]
        for c in range(nc):
            rows = pl.ds(c * GDN_CHUNK, GDN_CHUNK)
            heads = lambda r: _by_head(r, rows, nh)
            hist_ref[:, c] = state
            y, state = _gdn_scan(_mm_raw, heads(u_ref), heads(w_ref), heads(qd_ref), heads(kd_ref), at_ref[:, rows, :],
                                 eg_ref[:, c, 0:1, :], heads(z_ref), g, state)
            _store_heads(y_ref, rows, y.astype(BF16))
        s_ref[...] = state

    row, att, egs, hist, gns = _scan_specs(tb, nh, GDN_CHUNK, False, nt)
    return pl.pallas_call(
        body, name="gdn_scan_fwd", grid=(nt,), in_specs=[row, row, row, row, att, egs, row, gns], out_specs=[row, hist],
        out_shape=[jax.ShapeDtypeStruct((t, nh * HEAD_DIM), BF16),
                   jax.ShapeDtypeStruct((nh, t // GDN_CHUNK, HEAD_DIM, HEAD_DIM), F32)],
        scratch_shapes=[pltpu.VMEM((nh, HEAD_DIM, HEAD_DIM), F32)],
        compiler_params=_params("arbitrary"))(u, w, qd, kd, attn, egl, zb, gn)


def _gdn_scan_bwd(u, w, qd, kd, attn, egl, zb, gn, hist, dy):
    t = u.shape[0]
    tb = min(t, 256)
    nt, nc = t // tb, tb // GDN_CHUNK
    nh = GDN_HEADS

    def body(u_ref, w_ref, qd_ref, kd_ref, at_ref, eg_ref, z_ref, gn_ref, hist_ref, dy_ref,
             du_ref, dw_ref, dqd_ref, dkd_ref, dat_ref, deg_ref, dz_ref, dgn_ref, ds_ref):
        @pl.when(pl.program_id(0) == 0)
        def _():
            ds_ref[...] = jnp.zeros_like(ds_ref)
            dgn_ref[...] = jnp.zeros_like(dgn_ref)

        g = gn_ref[0:1, :]
        d_state = ds_ref[...]
        for c in reversed(range(nc)):
            rows = pl.ds(c * GDN_CHUNK, GDN_CHUNK)
            heads = lambda r: _by_head(r, rows, nh).astype(F32)
            _, vjp = jax.vjp(functools.partial(_gdn_scan, _mm_vjp), heads(u_ref), heads(w_ref), heads(qd_ref),
                             heads(kd_ref), at_ref[:, rows, :].astype(F32), eg_ref[:, c, 0:1, :], heads(z_ref), g,
                             hist_ref[:, c])
            du, dw, dqd, dkd, dat, deg, dz, dgn, d_state = vjp((heads(dy_ref), d_state))
            _store_heads(du_ref, rows, du)
            _store_heads(dw_ref, rows, dw)
            _store_heads(dqd_ref, rows, dqd)
            _store_heads(dkd_ref, rows, dkd)
            dat_ref[:, rows, :] = dat
            deg_ref[:, c] = jnp.broadcast_to(deg, (nh, SUBLANES, LANES))
            _store_heads(dz_ref, rows, dz.astype(BF16))
            dgn_ref[0:1, :] += dgn
        ds_ref[...] = d_state

    row, att, egs, hists, gns = _scan_specs(tb, nh, GDN_CHUNK, True, nt)
    wide = jax.ShapeDtypeStruct((t, nh * HEAD_DIM), F32)
    return pl.pallas_call(
        body, name="gdn_scan_bwd", grid=(nt,),
        in_specs=[row, row, row, row, att, egs, row, gns, hists, row],
        out_specs=[row, row, row, row, att, egs, row, gns],
        out_shape=[wide] * 4 + [jax.ShapeDtypeStruct((nh, t, GDN_CHUNK), F32),
                                jax.ShapeDtypeStruct((nh, t // GDN_CHUNK, SUBLANES, LANES), F32),
                                jax.ShapeDtypeStruct((t, nh * HEAD_DIM), BF16),
                                jax.ShapeDtypeStruct((SUBLANES, LANES), F32)],
        scratch_shapes=[pltpu.VMEM((nh, HEAD_DIM, HEAD_DIM), F32)],
        compiler_params=_params("arbitrary"))(u, w, qd, kd, attn, egl, zb, gn, hist, dy)


def _hgrn_prep(mm, qr, fr, lbl):
    n, c, _ = qr.shape
    lb = jax.nn.sigmoid(lbl[1:2, :] - lbl[0:1, :])
    f = lb + (1.0 - lb) * jax.nn.sigmoid(fr)
    q = jax.nn.silu(qr)
    k = 1.0 - f
    logf = jnp.log(f)
    ri = lax.broadcasted_iota(jnp.int32, (n, c, c), 1)
    ci = lax.broadcasted_iota(jnp.int32, (n, c, c), 2)
    b = mm((ri >= ci).astype(F32), logf, "nn", FULL_F32)
    attn = _hgrn_attn(mm, q, k, b)
    b_last = jnp.sum(logf, axis=1, keepdims=True)
    return q * jnp.exp(b), k * jnp.exp(b_last - b), attn, jnp.exp(b_last)


HGRN_SUB = 8


def _hgrn_attn(mm, q, k, b):
    n, c, d = q.shape
    sb = HGRN_SUB
    nsb = c // sb
    sub = lambda a: a.reshape(n * nsb, sb, d)
    qs, ks, bs = sub(q), sub(k), sub(b)
    ti = lax.broadcasted_iota(jnp.int32, (n * nsb, sb, sb, d), 1)
    si = lax.broadcasted_iota(jnp.int32, (n * nsb, sb, sb, d), 2)
    m4 = ti >= si
    decay = jnp.where(m4, jnp.exp(jnp.where(m4, bs[:, :, None, :] - bs[:, None, :, :], 0.0)), 0.0)
    diag = jnp.sum(qs[:, :, None, :] * ks[:, None, :, :] * decay, axis=-1).reshape(n, c, sb)
    row = lax.broadcasted_iota(jnp.int32, (n, c, c), 1)
    col = lax.broadcasted_iota(jnp.int32, (n, c, c), 2)
    spread = (lax.broadcasted_iota(jnp.int32, (n, sb, c), 2) & (sb - 1)) == lax.broadcasted_iota(jnp.int32, (n, sb, c), 1)
    same_block = (row & -sb) == (col & -sb)
    attn = jnp.where(same_block, mm(diag, spread.astype(F32), "nn", THREE_PASS), 0.0)
    before = [jnp.zeros((n, sb, c), F32)]
    col8 = lax.broadcasted_iota(jnp.int32, (n, sb, c), 2)
    for i in range(1, nsb):
        r0 = i * sb
        bi = b[:, r0:r0 + sb, :]
        ref = bi[:, 0:1, :]
        part = mm(q[:, r0:r0 + sb, :] * jnp.exp(bi - ref), k * jnp.exp(jnp.minimum(ref - b, 0.0)), "nt", THREE_PASS)
        before.append(jnp.where(col8 < r0, part, 0.0))
    return attn + jnp.concatenate(before, axis=1)


def _hgrn_scan(mm, qe, kd, attn, ebl, iv, z, gn, state):
    o = mm(qe, state, "nt", ONE_PASS) + mm(attn, iv, "nn", ONE_PASS)
    new_state = state * ebl + mm(iv, kd, "tn", ONE_PASS)
    return _rms_gate(o, gn, z), new_state


def _hgrn_prep_specs(tb):
    col = pl.BlockSpec((tb, HEAD_DIM), lambda h, i: (i, h))
    lbs = pl.BlockSpec((2, HEAD_DIM), lambda h, i: (0, h))
    att = pl.BlockSpec((1, tb, HGRN_CHUNK), lambda h, i: (h, i, 0))
    ebl = pl.BlockSpec((1, tb // HGRN_CHUNK, SUBLANES, LANES), lambda h, i: (h, i, 0, 0))
    return col, lbs, att, ebl


def _hgrn_prep_fwd(qr, fr, lower_bounds):
    t = qr.shape[0]
    tb = min(t, 256)
    nt, nc = t // tb, tb // HGRN_CHUNK
    hh = HGRN_HEADS

    def body(q_ref, f_ref, lb_ref, qe_ref, kd_ref, at_ref, eb_ref):
        ch = lambda r: _chunks(r, nc, HGRN_CHUNK)
        qe, kd, attn, ebl = _hgrn_prep(_mm_raw, ch(q_ref[...]), ch(f_ref[...]), lb_ref[...])
        qe_ref[...] = qe.reshape(tb, HEAD_DIM).astype(BF16)
        kd_ref[...] = kd.reshape(tb, HEAD_DIM).astype(BF16)
        at_ref[0] = attn.reshape(tb, HGRN_CHUNK).astype(BF16)
        eb_ref[0] = jnp.broadcast_to(ebl, (nc, SUBLANES, LANES))

    col, lbs, att, ebs = _hgrn_prep_specs(tb)
    return pl.pallas_call(
        body, name="hgrn_prep_fwd", grid=(hh, nt), in_specs=[col, col, lbs], out_specs=[col, col, att, ebs],
        out_shape=[jax.ShapeDtypeStruct((t, HGRN_WIDTH), BF16)] * 2
        + [jax.ShapeDtypeStruct((hh, t, HGRN_CHUNK), BF16), jax.ShapeDtypeStruct((hh, t // HGRN_CHUNK, SUBLANES, LANES), F32)],
        compiler_params=_params("parallel", "parallel"))(qr, fr, lower_bounds)


def _hgrn_prep_bwd(qr, fr, lower_bounds, dqe, dkd, dattn, debl):
    t = qr.shape[0]
    tb = min(t, 256)
    nt, nc = t // tb, tb // HGRN_CHUNK
    hh = HGRN_HEADS

    def body(q_ref, f_ref, lb_ref, dqe_ref, dkd_ref, dat_ref, deb_ref, dq_ref, df_ref, dlb_ref):
        @pl.when(pl.program_id(1) == 0)
        def _():
            dlb_ref[...] = jnp.zeros_like(dlb_ref)

        ch = lambda r: _chunks(r, nc, HGRN_CHUNK)
        _, vjp = jax.vjp(functools.partial(_hgrn_prep, _mm_vjp), ch(q_ref[...]), ch(f_ref[...]), lb_ref[...])
        dq, df, dlb = vjp((ch(dqe_ref[...]), ch(dkd_ref[...]), ch(dat_ref[0]), deb_ref[0][:, 0:1, :]))
        dq_ref[...] = dq.reshape(tb, HEAD_DIM).astype(BF16)
        df_ref[...] = df.reshape(tb, HEAD_DIM).astype(BF16)
        dlb_ref[...] += dlb

    col, lbs, att, ebs = _hgrn_prep_specs(tb)
    return pl.pallas_call(
        body, name="hgrn_prep_bwd", grid=(hh, nt), in_specs=[col, col, lbs, col, col, att, ebs],
        out_specs=[col, col, lbs],
        out_shape=[jax.ShapeDtypeStruct((t, HGRN_WIDTH), BF16)] * 2 + [jax.ShapeDtypeStruct((2, HGRN_WIDTH), F32)],
        compiler_params=_params("parallel", "arbitrary"))(qr, fr, lower_bounds, dqe, dkd, dattn, debl)


def _hgrn_scan_fwd(qe, kd, attn, ebl, iv, z, gn):
    t = qe.shape[0]
    tb = min(t, 128)
    nt, nc = t // tb, tb // HGRN_CHUNK
    hh = HGRN_HEADS

    def body(qe_ref, kd_ref, at_ref, eb_ref, i_ref, z_ref, gn_ref, y_ref, hist_ref, s_ref):
        @pl.when(pl.program_id(0) == 0)
        def _():
            s_ref[...] = jnp.zeros_like(s_ref)

        g = gn_ref[0:1, :]
        state = s_ref[...]
        for c in range(nc):
            rows = pl.ds(c * HGRN_CHUNK, HGRN_CHUNK)
            heads = lambda r: _by_head(r, rows, hh)
            hist_ref[:, c] = state
            y, state = _hgrn_scan(_mm_raw, heads(qe_ref), heads(kd_ref), at_ref[:, rows, :], eb_ref[:, c, 0:1, :],
                                  heads(i_ref), heads(z_ref), g, state)
            _store_heads(y_ref, rows, y.astype(BF16))
        s_ref[...] = state

    row, att, ebs, hist, gns = _scan_specs(tb, hh, HGRN_CHUNK, False, nt)
    return pl.pallas_call(
        body, name="hgrn_scan_fwd", grid=(nt,), in_specs=[row, row, att, ebs, row, row, gns], out_specs=[row, hist],
        out_shape=[jax.ShapeDtypeStruct((t, HGRN_WIDTH), BF16),
                   jax.ShapeDtypeStruct((hh, t // HGRN_CHUNK, HEAD_DIM, HEAD_DIM), F32)],
        scratch_shapes=[pltpu.VMEM((hh, HEAD_DIM, HEAD_DIM), F32)],
        compiler_params=_params("arbitrary"))(qe, kd, attn, ebl, iv, z, gn)


def _hgrn_scan_bwd(qe, kd, attn, ebl, iv, z, gn, hist, dy):
    t = qe.shape[0]
    tb = min(t, 128)
    nt, nc = t // tb, tb // HGRN_CHUNK
    hh = HGRN_HEADS

    def body(qe_ref, kd_ref, at_ref, eb_ref, i_ref, z_ref, gn_ref, hist_ref, dy_ref,
             dqe_ref, dkd_ref, dat_ref, deb_ref, di_ref, dz_ref, dgn_ref, ds_ref):
        @pl.when(pl.program_id(0) == 0)
        def _():
            ds_ref[...] = jnp.zeros_like(ds_ref)
            dgn_ref[...] = jnp.zeros_like(dgn_ref)

        g = gn_ref[0:1, :]
        d_state = ds_ref[...]
        for c in reversed(range(nc)):
            rows = pl.ds(c * HGRN_CHUNK, HGRN_CHUNK)
            heads = lambda r: _by_head(r, rows, hh).astype(F32)
            _, vjp = jax.vjp(functools.partial(_hgrn_scan, _mm_vjp), heads(qe_ref), heads(kd_ref),
                             at_ref[:, rows, :].astype(F32), eb_ref[:, c, 0:1, :], heads(i_ref), heads(z_ref), g,
                             hist_ref[:, c])
            dqe, dkd, dat, deb, di, dz, dgn, d_state = vjp((heads(dy_ref), d_state))
            _store_heads(dqe_ref, rows, dqe)
            _store_heads(dkd_ref, rows, dkd)
            dat_ref[:, rows, :] = dat
            deb_ref[:, c] = jnp.broadcast_to(deb, (hh, SUBLANES, LANES))
            _store_heads(di_ref, rows, di.astype(BF16))
            _store_heads(dz_ref, rows, dz.astype(BF16))
            dgn_ref[0:1, :] += dgn
        ds_ref[...] = d_state

    row, att, ebs, hists, gns = _scan_specs(tb, hh, HGRN_CHUNK, True, nt)
    wide = lambda dt: jax.ShapeDtypeStruct((t, HGRN_WIDTH), dt)
    return pl.pallas_call(
        body, name="hgrn_scan_bwd", grid=(nt,),
        in_specs=[row, row, att, ebs, row, row, gns, hists, row],
        out_specs=[row, row, att, ebs, row, row, gns],
        out_shape=[wide(F32), wide(F32), jax.ShapeDtypeStruct((hh, t, HGRN_CHUNK), F32),
                   jax.ShapeDtypeStruct((hh, t // HGRN_CHUNK, SUBLANES, LANES), F32), wide(BF16), wide(BF16),
                   jax.ShapeDtypeStruct((SUBLANES, LANES), F32)],
        scratch_shapes=[pltpu.VMEM((hh, HEAD_DIM, HEAD_DIM), F32)],
        compiler_params=_params("arbitrary"))(qe, kd, attn, ebl, iv, z, gn, hist, dy)


def _layer_norm(pre, g, b):
    mu = jnp.mean(pre, axis=-1, keepdims=True)
    d = pre - mu
    var = jnp.mean(d * d, axis=-1, keepdims=True)
    return d * lax.rsqrt(var + NORM_EPS) * g + b


def _lnpl_fwd(xin, s, p, wg, wpl, ln_g, ln_b):
    t = xin.shape[0]
    tt = min(t, 256)

    def body(x_ref, s_ref, p_ref, wg_ref, wpl_ref, g_ref, b_ref, o_ref, ob_ref):
        xn = _layer_norm(DEEPNORM_ALPHA * x_ref[...] + s_ref[...], g_ref[...], b_ref[...])
        gate = jax.nn.sigmoid(_mm_raw(xn, wg_ref[...], "nn", False))
        out = xn + _mm_raw(p_ref[...], wpl_ref[...], "nn", False) * gate
        o_ref[...] = out
        ob_ref[...] = out.astype(BF16)

    row = lambda w: pl.BlockSpec((tt, w), lambda i: (i, 0))
    full = lambda a: pl.BlockSpec(a.shape, lambda i: (0, 0))
    return pl.pallas_call(
        body, name="lnpl_fwd", grid=(t // tt,),
        in_specs=[row(D_MODEL), row(D_MODEL), row(PL_DIM), full(wg), full(wpl), full(ln_g), full(ln_b)],
        out_specs=[row(D_MODEL), row(D_MODEL)],
        out_shape=[jax.ShapeDtypeStruct((t, D_MODEL), F32), jax.ShapeDtypeStruct((t, D_MODEL), BF16)],
        compiler_params=_params("parallel"))(xin, s, p, wg, wpl, ln_g, ln_b)


def _lnpl_bwd(xin, s, p, wg, wpl, ln_g, ln_b, upstream, last, name):
    t = xin.shape[0]
    tt = min(t, 256)

    def body(x_ref, s_ref, p_ref, wg_ref, wpl_ref, g_ref, b_ref, up_ref,
             dpre_ref, dwg_ref, dwpl_ref, dg_ref, db_ref, loss_ref):
        @pl.when(pl.program_id(0) == 0)
        def _():
            for r in (dwg_ref, dwpl_ref, dg_ref, db_ref, loss_ref):
                r[...] = jnp.zeros_like(r)

        pre = DEEPNORM_ALPHA * x_ref[...] + s_ref[...]
        xn, ln_vjp = jax.vjp(_layer_norm, pre, g_ref[...], b_ref[...])
        gate = jax.nn.sigmoid(_mm_raw(xn, wg_ref[...], "nn", False))
        plv = _mm_raw(p_ref[...], wpl_ref[...], "nn", False)
        if last:
            err = xn + plv * gate - up_ref[...]
            dout = err * (1.0 / D_MODEL)
            tot = jnp.sum(jnp.sum(err * err, axis=1, keepdims=True), axis=0, keepdims=True) * (0.5 / D_MODEL)
            loss_ref[...] += jnp.broadcast_to(tot, loss_ref.shape)
        else:
            dout = up_ref[...]
        dplv = dout * gate
        dlogits = dout * plv * gate * (1.0 - gate)
        dwg_ref[...] += _mm_raw(xn, dlogits, "tn", False)
        dwpl_ref[...] += _mm_raw(p_ref[...], dplv, "tn", False)
        dxn = dout + _mm_raw(dlogits, wg_ref[...], "nt", False)
        dpre, dg, db = ln_vjp(dxn)
        dpre_ref[...] = dpre
        dg_ref[...] += dg
        db_ref[...] += db

    row = lambda w: pl.BlockSpec((tt, w), lambda i: (i, 0))
    full = lambda shape: pl.BlockSpec(shape, lambda i: (0, 0))
    return pl.pallas_call(
        body, name=name, grid=(t // tt,),
        in_specs=[row(D_MODEL), row(D_MODEL), row(PL_DIM), full(wg.shape), full(wpl.shape), full(ln_g.shape),
                  full(ln_b.shape), row(D_MODEL)],
        out_specs=[row(D_MODEL), full(wg.shape), full(wpl.shape), full(ln_g.shape), full(ln_b.shape),
                   full((SUBLANES, LANES))],
        out_shape=[jax.ShapeDtypeStruct((t, D_MODEL), F32), jax.ShapeDtypeStruct(wg.shape, F32),
                   jax.ShapeDtypeStruct(wpl.shape, F32), jax.ShapeDtypeStruct(ln_g.shape, F32),
                   jax.ShapeDtypeStruct(ln_b.shape, F32), jax.ShapeDtypeStruct((SUBLANES, LANES), F32)],
        compiler_params=_params("arbitrary"))(xin, s, p, wg, wpl, ln_g, ln_b, upstream)


def _rep_rows(v):
    return jnp.broadcast_to(v.reshape(1, LANES), (SUBLANES, LANES))


def _rep_heads(v):
    return jnp.broadcast_to(v.reshape(-1, 1, 1), (v.shape[0], SUBLANES, LANES))


def _local_step(x, p, target, w):
    a = DEEPNORM_ALPHA
    nh = GDN_HEADS
    xb = x.astype(BF16)
    wie = w["w_in_even"]
    w_a, w_qkv, w_zb = wie[:, :4096], wie[:, 4096:7168], wie[:, 7168:8192]
    w_tail = jnp.pad(wie[:, 8192:], ((0, 0), (0, LANES - 2 * nh)))
    woe, wio, woo = w["w_out_even"], w["w_in_odd"], w["w_out_odd"]
    conv_a_w, conv_b_w = w["conv_a_w"], w["conv_b_w"]
    ln_g0, ln_b0, ln_g1, ln_b1 = (v.reshape(1, D_MODEL) for v in (w["ln_g"][0], w["ln_b"][0], w["ln_g"][1], w["ln_b"][1]))
    alog, dtb = _rep_heads(w["a_log"].reshape(nh)), _rep_heads(w["dt_bias"].reshape(nh))
    gdn_g, hgrn_g = _rep_rows(w["gdn_norm_g"]), _rep_rows(w["hgrn_norm_g"])

    proj_a = _matmul(xb, w_a, name="fwd_proj_a")
    proj_qkv = _matmul(xb, w_qkv, name="fwd_proj_qkv")
    proj_zb = _matmul(xb, w_zb, name="fwd_proj_zb")
    proj_tail = _matmul(xb, w_tail, name="fwd_proj_tail")
    rep = lambda cols: jnp.broadcast_to(cols.T[:, :, None], (nh, cols.shape[0], LANES))
    braw, araw = rep(proj_tail[:, :nh]), rep(proj_tail[:, nh:2 * nh])
    y_a = _conv_a_fwd(proj_a, conv_a_w)
    qkv_act = _conv_b_fwd(proj_qkv, conv_b_w)
    gdn_pre = _gdn_prep_fwd(qkv_act, braw, araw, alog, dtb)
    y_b, gdn_hist = _gdn_scan_fwd(*gdn_pre, proj_zb, gdn_g)
    s0 = _matmul(y_b, woe[1024:], name="fwd_out_even_b", add=_matmul(y_a, woe[:1024], name="fwd_out_even_a"))
    x1, x1b = _lnpl_fwd(x, s0, p[0], w["w_pl_gate"][0], w["w_pl"][0], ln_g0, ln_b0)
    proj_o = [_matmul(x1b, wio[j], name=f"fwd_proj_odd{j}") for j in range(4)]
    hgrn_pre = _hgrn_prep_fwd(proj_o[0], proj_o[1], w["lower_bounds"])
    y_o, hgrn_hist = _hgrn_scan_fwd(*hgrn_pre, proj_o[2], proj_o[3], hgrn_g)
    s1 = _matmul(y_o, woo, name="fwd_out_odd")

    g = {}
    dpre1, dwg1, dwpl1, dlng1, dlnb1, loss = _lnpl_bwd(x1, s1, p[1], w["w_pl_gate"][1], w["w_pl"][1], ln_g1, ln_b1,
                                                     target, True, "lnpl_bwd_odd")
    dy_o = _matmul(dpre1, woo, tb=True, name="bwd_out_odd_dx")
    g["w_out_odd"] = _matmul(y_o, dpre1, ta=True, name="bwd_out_odd_dw")
    dqe, dkd, dat, deb, di, dz, dhg = _hgrn_scan_bwd(*hgrn_pre, proj_o[2], proj_o[3], hgrn_g, hgrn_hist, dy_o)
    dq, df, dlb = _hgrn_prep_bwd(proj_o[0], proj_o[1], w["lower_bounds"], dqe, dkd, dat, deb)
    dx1 = dpre1
    scale = a
    dws = []
    for j, dj in enumerate((dq, df, di, dz)):
        dx1 = _matmul(dj, wio[j], tb=True, add=dx1, add_scale=scale, name=f"bwd_proj_odd_dx{j}")
        scale = 1.0
        dws.append(_matmul(x1b, dj, ta=True, name=f"bwd_proj_odd_dw{j}"))
    g["w_in_odd"] = jnp.stack(dws)
    g["hgrn_norm_g"] = dhg[0:1]
    g["lower_bounds"] = dlb

    dpre0, dwg0, dwpl0, dlng0, dlnb0, _ = _lnpl_bwd(x, s0, p[0], w["w_pl_gate"][0], w["w_pl"][0], ln_g0, ln_b0,
                                                  dx1, False, "lnpl_bwd_even")
    g["w_pl_gate"] = jnp.stack([dwg0, dwg1])
    g["w_pl"] = jnp.stack([dwpl0, dwpl1])
    g["ln_g"] = jnp.concatenate([dlng0, dlng1], axis=0)
    g["ln_b"] = jnp.concatenate([dlnb0, dlnb1], axis=0)
    dy_a = _matmul(dpre0, woe[:1024], tb=True, name="bwd_out_even_dxa")
    dy_b = _matmul(dpre0, woe[1024:], tb=True, name="bwd_out_even_dxb")
    g["w_out_even"] = jnp.concatenate([_matmul(y_a, dpre0, ta=True, name="bwd_out_even_dwa"),
                                       _matmul(y_b, dpre0, ta=True, name="bwd_out_even_dwb")], axis=0)
    du, dw, dqd, dkd, dat, deg, dzb, dgn = _gdn_scan_bwd(*gdn_pre, proj_zb, gdn_g, gdn_hist, dy_b)
    dqa, dka, dva, dbr, dar, dal, ddt = _gdn_prep_bwd(qkv_act, braw, araw, alog, dtb, du, dw, dqd, dkd, dat, deg)
    g["a_log"] = dal[:, 0, 0].reshape(1, nh)
    g["dt_bias"] = ddt[:, 0, 0].reshape(1, nh)
    g["gdn_norm_g"] = dgn[0:1]
    d_pre_qkv, dwb = [], []
    for j, dj in enumerate((dqa, dka, dva)):
        dpj, dwj = _conv_b_bwd(proj_qkv, conv_b_w, dj, j, f"conv_b_bwd{j}")
        d_pre_qkv.append(dpj)
        dwb.append(dwj[:4])
    g["conv_b_w"] = jnp.concatenate(dwb, axis=1)
    d_a, dwa = _conv_a_bwd(proj_a, conv_a_w, dy_a)
    g["conv_a_w"] = dwa[:3]
    d_tail = jnp.concatenate([dbr[:, :, 0].T, dar[:, :, 0].T, jnp.zeros((x.shape[0], LANES - 2 * nh), F32)],
                             axis=1).astype(BF16)
    pieces = [(d_a, w_a), (d_pre_qkv[0], w_qkv[:, :1024]), (d_pre_qkv[1], w_qkv[:, 1024:2048]),
              (d_pre_qkv[2], w_qkv[:, 2048:]), (dzb, w_zb), (d_tail, w_tail)]
    dx = dpre0
    scale = a
    dws = []
    for j, (dj, wj) in enumerate(pieces):
        dx = _matmul(dj, wj, tb=True, add=dx, add_scale=scale, name=f"bwd_proj_even_dx{j}")
        scale = 1.0
        dws.append(_matmul(xb, dj, ta=True, name=f"bwd_proj_even_dw{j}"))
    dws[-1] = dws[-1][:, :2 * nh]
    g["w_in_even"] = jnp.concatenate(dws, axis=1)
    return loss, dx, g


def _adamw(w, g, m, v, name):
    rows, cols = w.shape
    tr = rows if rows <= 256 else 256
    assert rows % tr == 0, (name, rows)

    def body(w_ref, g_ref, m_ref, v_ref, d_ref, nm_ref, nv_ref):
        gg = g_ref[...]
        nm = ADAM_B1 * m_ref[...] + (1.0 - ADAM_B1) * gg
        nv = ADAM_B2 * v_ref[...] + (1.0 - ADAM_B2) * jnp.square(gg)
        m_hat = nm / (1.0 - ADAM_B1 ** ADAM_STEP)
        v_hat = nv / (1.0 - ADAM_B2 ** ADAM_STEP)
        d_ref[...] = -ADAM_LR * (m_hat / (jnp.sqrt(v_hat) + ADAM_EPS) + ADAM_WD * w_ref[...])
        nm_ref[...] = nm
        nv_ref[...] = nv

    spec = pl.BlockSpec((tr, cols), lambda i: (i, 0))
    return pl.pallas_call(
        body, name=name, grid=(rows // tr,), in_specs=[spec] * 4, out_specs=[spec] * 3,
        out_shape=[jax.ShapeDtypeStruct(w.shape, F32)] * 3, compiler_params=_params("parallel"))(w, g, m, v)


MESH = pl.DeviceIdType.MESH
N_DEV = 8
HBM_SPEC = pl.BlockSpec(memory_space=pltpu.HBM)
VMEM_SPEC = pl.BlockSpec(memory_space=pltpu.VMEM)


def _coords():
    return lax.axis_index("x"), lax.axis_index("y"), lax.axis_index("c")


def _flip(v, bit):
    return 1 - v if bit else v


def _remote(src, dst, send_sem, recv_sem, dev):
    return pltpu.make_async_remote_copy(src_ref=src, dst_ref=dst, send_sem=send_sem, recv_sem=recv_sem,
                                        device_id=dev, device_id_type=MESH)


def _exchange_small(buf, reduce, name):
    rows = buf.shape[0]

    def body(in_ref, out_ref, slots, send_sems, recv_sems):
        x, y, c = _coords()
        me = 4 * x + 2 * y + c
        slots[me] = in_ref[...]
        peer = lambda k: (_flip(x, (k >> 2) & 1), _flip(y, (k >> 1) & 1), _flip(c, k & 1))
        sends = []
        for k in range(1, N_DEV):
            cp = _remote(in_ref, slots.at[me], send_sems.at[k - 1], recv_sems.at[k - 1], peer(k))
            cp.start()
            sends.append(cp)
        for k in range(1, N_DEV):
            px, py, pc = peer(k)
            _remote(in_ref, slots.at[4 * px + 2 * py + pc], send_sems.at[k - 1], recv_sems.at[k - 1], peer(k)).wait_recv()
        for cp in sends:
            cp.wait_send()
        if reduce:
            acc = slots[0]
            for d in range(1, N_DEV):
                acc = acc + slots[d]
            out_ref[...] = acc
        else:
            out_ref[...] = slots[...]

    out_shape = (rows, LANES) if reduce else (N_DEV, rows, LANES)
    return pl.pallas_call(
        body, name=name, in_specs=[VMEM_SPEC], out_specs=VMEM_SPEC, out_shape=jax.ShapeDtypeStruct(out_shape, F32),
        scratch_shapes=[pltpu.VMEM((N_DEV, rows, LANES), F32), pltpu.SemaphoreType.DMA((N_DEV - 1,)),
                        pltpu.SemaphoreType.DMA((N_DEV - 1,))])(buf)


def _half_rows(half, which):
    return pl.ds(pl.multiple_of(which * half, 16), half)


def _other_chip(x, y, k):
    return _flip(x, (k >> 1) & 1), _flip(y, k & 1)


def _gather_shards(shards):
    n = len(shards)

    def body(*refs):
        ins, outs = refs[:n], refs[n:2 * n]
        ici_s, ici_r, d2d_s, d2d_r = refs[2 * n:]
        x, y, c = _coords()
        chip = 2 * x + y
        sends = []
        for i in range(n):
            half = ins[i].shape[0] // 2
            for k in (1, 2, 3):
                ox, oy = _other_chip(x, y, k)
                cp = _remote(ins[i].at[_half_rows(half, c)], outs[i].at[chip, _half_rows(half, c)],
                             ici_s.at[3 * i + k - 1], ici_r.at[3 * i + k - 1], (ox, oy, c))
                cp.start()
                sends.append(cp)
        for k in (1, 2, 3):
            ox, oy = _other_chip(x, y, k)
            for i in range(n):
                half = ins[i].shape[0] // 2
                piece = outs[i].at[2 * ox + oy, _half_rows(half, c)]
                _remote(piece, piece, ici_s.at[3 * i + k - 1], ici_r.at[3 * i + k - 1], (ox, oy, c)).wait_recv()
                cp = _remote(piece, piece, d2d_s.at[3 * i + k - 1], d2d_r.at[3 * i + k - 1], (x, y, 1 - c))
                cp.start()
                sends.append(cp)
        for k in (1, 2, 3):
            ox, oy = _other_chip(x, y, k)
            for i in range(n):
                half = ins[i].shape[0] // 2
                piece = outs[i].at[2 * ox + oy, _half_rows(half, 1 - c)]
                _remote(piece, piece, d2d_s.at[3 * i + k - 1], d2d_r.at[3 * i + k - 1], (x, y, 1 - c)).wait_recv()
        for cp in sends:
            cp.wait_send()

    return pl.pallas_call(
        body, name="gather_weights", in_specs=[HBM_SPEC] * n, out_specs=[HBM_SPEC] * n,
        out_shape=[jax.ShapeDtypeStruct((4,) + s.shape, s.dtype) for s in shards],
        scratch_shapes=[pltpu.SemaphoreType.DMA((3 * n,))] * 4)(*shards)


def _rs_sibling_swap(g4s):
    n = len(g4s)

    def body(*refs):
        ins, outs = refs[:n], refs[n:2 * n]
        send_sems, recv_sems = refs[2 * n:]
        x, y, c = _coords()
        sends = []
        for i in range(n):
            half = ins[i].shape[1] // 2
            for s in range(4):
                cp = _remote(ins[i].at[s, _half_rows(half, 1 - c)], outs[i].at[s], send_sems.at[4 * i + s],
                             recv_sems.at[4 * i + s], (x, y, 1 - c))
                cp.start()
                sends.append(cp)
        for cp in sends:
            cp.wait_recv()
        for cp in sends:
            cp.wait_send()

    return pl.pallas_call(
        body, name="rs_sibling_swap", in_specs=[HBM_SPEC] * n, out_specs=[HBM_SPEC] * n,
        out_shape=[jax.ShapeDtypeStruct((4, g.shape[1] // 2, g.shape[2]), g.dtype) for g in g4s],
        scratch_shapes=[pltpu.SemaphoreType.DMA((4 * n,))] * 2)(*g4s)


def _rs_add_sibling(g4, got, c_idx, name):
    _, rows, cols = g4.shape
    half = rows // 2
    tr = min(half, 256)
    nb = half // tr

    def body(c_ref, a_ref, b_ref, o_ref, ob_ref):
        total = a_ref[...] + b_ref[...]
        o_ref[...] = total
        ob_ref[...] = total.astype(BF16)

    blk = (1, tr, cols)
    out = pl.BlockSpec(blk, lambda s, i, c_ref: (s, i, 0))
    grid_spec = pltpu.PrefetchScalarGridSpec(
        num_scalar_prefetch=1, grid=(4, nb),
        in_specs=[pl.BlockSpec(blk, lambda s, i, c_ref: (s, c_ref[0] * nb + i, 0)), out],
        out_specs=[out, out])
    return pl.pallas_call(
        body, name=name, grid_spec=grid_spec,
        out_shape=[jax.ShapeDtypeStruct(got.shape, F32), jax.ShapeDtypeStruct(got.shape, BF16)],
        compiler_params=_params("parallel", "parallel"))(c_idx, g4, got)


def _rs_chip_scatter(p4s):
    n = len(p4s)

    def body(*refs):
        ins, outs = refs[:n], refs[n:2 * n]
        send_sems, recv_sems = refs[2 * n:]
        x, y, c = _coords()
        sends = []
        for i in range(n):
            for k in (1, 2, 3):
                ox, oy = _other_chip(x, y, k)
                cp = _remote(ins[i].at[2 * ox + oy], outs[i].at[k - 1], send_sems.at[3 * i + k - 1],
                             recv_sems.at[3 * i + k - 1], (ox, oy, c))
                cp.start()
                sends.append(cp)
        for cp in sends:
            cp.wait_recv()
        for cp in sends:
            cp.wait_send()

    return pl.pallas_call(
        body, name="rs_chip_scatter", in_specs=[HBM_SPEC] * n, out_specs=[HBM_SPEC] * n,
        out_shape=[jax.ShapeDtypeStruct((3,) + p.shape[1:], p.dtype) for p in p4s],
        scratch_shapes=[pltpu.SemaphoreType.DMA((3 * n,))] * 2)(*p4s)


def _rs_add_chips(p4, got3, idx, name):
    _, half, cols = p4.shape
    tr = min(half, 256)
    nb = half // tr

    def body(idx_ref, p_ref, a_ref, b_ref, c_ref, o_ref):
        o_ref[...] = ((p_ref[0] + a_ref[0].astype(F32)) + b_ref[0].astype(F32)) + c_ref[0].astype(F32)

    blk = (1, tr, cols)
    grid_spec = pltpu.PrefetchScalarGridSpec(
        num_scalar_prefetch=1, grid=(nb,),
        in_specs=[pl.BlockSpec(blk, lambda i, idx_ref: (idx_ref[0], i, 0))]
        + [pl.BlockSpec(blk, functools.partial(lambda k, i, idx_ref: (k, i, 0), k)) for k in range(3)],
        out_specs=pl.BlockSpec((tr, cols), lambda i, idx_ref: (idx_ref[1] * nb + i, 0)))
    return pl.pallas_call(body, name=name, grid_spec=grid_spec, out_shape=jax.ShapeDtypeStruct((2 * half, cols), F32),
                          compiler_params=_params("parallel"))(idx, p4, got3, got3, got3)


def _rs_share_halves(bufs):
    n = len(bufs)

    def body(*refs):
        ins, outs = refs[:n], refs[n:2 * n]
        send_sems, recv_sems = refs[2 * n:]
        x, y, c = _coords()
        sends = []
        for i in range(n):
            half = ins[i].shape[0] // 2
            cp = _remote(ins[i].at[_half_rows(half, c)], outs[i].at[_half_rows(half, c)], send_sems.at[i],
                         recv_sems.at[i], (x, y, 1 - c))
            cp.start()
            sends.append(cp)
        for i in range(n):
            half = ins[i].shape[0] // 2
            _remote(ins[i].at[_half_rows(half, c)], outs[i].at[_half_rows(half, 1 - c)], send_sems.at[i],
                    recv_sems.at[i], (x, y, 1 - c)).wait_recv()
        for cp in sends:
            cp.wait_send()

    return pl.pallas_call(
        body, name="rs_share_halves", in_specs=[HBM_SPEC] * n, out_specs=[HBM_SPEC] * n,
        out_shape=[jax.ShapeDtypeStruct(b.shape, b.dtype) for b in bufs],
        input_output_aliases={i: i for i in range(n)},
        scratch_shapes=[pltpu.SemaphoreType.DMA((n,))] * 2)(*bufs)


def _reduce_scatter(g4s, names):
    x, y, c = _coords()
    c_idx = jnp.stack([c]).astype(jnp.int32)
    idx = jnp.stack([2 * x + y, c]).astype(jnp.int32)
    got = _rs_sibling_swap(g4s)
    p4s = [_rs_add_sibling(g, s, c_idx, f"rs_add_sibling_{nm}") for g, s, nm in zip(g4s, got, names)]
    got3 = _rs_chip_scatter([pb for _, pb in p4s])
    bufs = [_rs_add_chips(p, t, idx, f"rs_add_chips_{nm}") for (p, _), t, nm in zip(p4s, got3, names)]
    return _rs_share_halves(bufs)


def _cols_join(s4):
    return s4.transpose(1, 0, 2).reshape(s4.shape[1], 4 * s4.shape[2])


def _cols_split(full):
    r, c4 = full.shape
    return full.reshape(r, 4, c4 // 4).transpose(1, 0, 2)


_BIG = {
    "w_in_even": ((1024, 2052), _cols_join, _cols_split),
    "w_out_even": ((512, 1024), lambda s: s.reshape(2048, 1024), lambda f: f.reshape(4, 512, 1024)),
    "w_in_odd": ((1024, 2048), lambda s: s, lambda f: f),
    "w_out_odd": ((512, 1024), lambda s: s.reshape(2048, 1024), lambda f: f.reshape(4, 512, 1024)),
    "w_pl": ((512, 256), lambda s: s.reshape(4, 2, 256, 256).transpose(1, 2, 0, 3).reshape(2, 256, 1024),
             lambda f: f.reshape(2, 256, 4, 256).transpose(2, 0, 1, 3).reshape(4, 512, 256)),
    "w_pl_gate": ((512, 1024), lambda s: s.reshape(4, 2, 256, 1024).transpose(1, 0, 2, 3).reshape(2, 1024, 1024),
                  lambda f: f.reshape(2, 4, 256, 1024).transpose(1, 0, 2, 3).reshape(4, 512, 1024)),
}


def _size(shape):
    n = 1
    for d in shape:
        n *= d
    return n


_SMALL = {"a_log": (1, 8), "dt_bias": (1, 8), "gdn_norm_g": (1, 128), "hgrn_norm_g": (1, 128),
          "lower_bounds": (2, 2048), "ln_g": (2, 1024), "ln_b": (2, 1024), "conv_a_w": (3, 1024), "conv_b_w": (4, 3072)}
_CONV_SHARD = {"conv_a_w": (3, 256), "conv_b_w": (4, 768)}


def _pack_small(parts, shapes, head_rows=0):
    rows = []
    for n, shape in shapes.items():
        v = parts[n].reshape(-1)
        rows.append(jnp.pad(v, (0, -v.shape[0] % LANES)).reshape(-1, LANES))
    buf = jnp.concatenate(rows, axis=0)
    return jnp.pad(buf, ((head_rows, -(buf.shape[0] + head_rows) % SUBLANES), (0, 0)))


def _unpack_small(buf, shapes, head_rows=0):
    out, off = {}, head_rows
    for n, shape in shapes.items():
        nrow = -(-_size(shape) // LANES)
        out[n] = buf[off:off + nrow].reshape(-1)[:_size(shape)].reshape(shape)
        off += nrow
    return out


_WEIGHTS = ["w_in_even", "conv_a_w", "conv_b_w", "a_log", "dt_bias", "gdn_norm_g", "w_out_even", "w_in_odd",
            "lower_bounds", "hgrn_norm_g", "w_out_odd", "ln_g", "ln_b", "w_pl", "w_pl_gate"]


def kernel(x, p, w_in_even, conv_a_w, conv_b_w, a_log, dt_bias, gdn_norm_g, w_out_even, w_in_odd, lower_bounds, hgrn_norm_g, w_out_odd, ln_g, ln_b, w_pl, w_pl_gate, loss_target, m_w_in_even, m_conv_a_w, m_conv_b_w, m_a_log, m_dt_bias, m_gdn_norm_g, m_w_out_even, m_w_in_odd, m_lower_bounds, m_hgrn_norm_g, m_w_out_odd, m_ln_g, m_ln_b, m_w_pl, m_w_pl_gate, v_w_in_even, v_conv_a_w, v_conv_b_w, v_a_log, v_dt_bias, v_gdn_norm_g, v_w_out_even, v_w_in_odd, v_lower_bounds, v_hgrn_norm_g, v_w_out_odd, v_ln_g, v_ln_b, v_w_pl, v_w_pl_gate):
    w = dict(zip(_WEIGHTS, (w_in_even, conv_a_w, conv_b_w, a_log, dt_bias, gdn_norm_g, w_out_even, w_in_odd,
                            lower_bounds, hgrn_norm_g, w_out_odd, ln_g, ln_b, w_pl, w_pl_gate)))
    m = dict(zip(_WEIGHTS, (m_w_in_even, m_conv_a_w, m_conv_b_w, m_a_log, m_dt_bias, m_gdn_norm_g, m_w_out_even,
                            m_w_in_odd, m_lower_bounds, m_hgrn_norm_g, m_w_out_odd, m_ln_g, m_ln_b, m_w_pl, m_w_pl_gate)))
    v = dict(zip(_WEIGHTS, (v_w_in_even, v_conv_a_w, v_conv_b_w, v_a_log, v_dt_bias, v_gdn_norm_g, v_w_out_even,
                            v_w_in_odd, v_lower_bounds, v_hgrn_norm_g, v_w_out_odd, v_ln_g, v_ln_b, v_w_pl, v_w_pl_gate)))
    chip = 2 * lax.axis_index("x") + lax.axis_index("y")

    names = list(_BIG)
    shard_shapes = {n: _BIG[n][0] for n in names}
    shards = [w[n].reshape(shard_shapes[n]).astype(BF16) for n in names]
    gathered = _gather_shards(shards)
    full = {n: _BIG[n][1](lax.dynamic_update_slice(ga, sh[None], (chip, 0, 0)))
            for n, ga, sh in zip(names, gathered, shards)}
    conv_mine = _pack_small({n: w[n] for n in _CONV_SHARD}, _CONV_SHARD)
    conv_all = _exchange_small(conv_mine, False, "gather_conv")
    conv_by_chip = [_unpack_small(conv_all[2 * s], _CONV_SHARD) for s in range(4)]
    for n in _CONV_SHARD:
        full[n] = jnp.concatenate([conv_by_chip[s][n] for s in range(4)], axis=1)
    for n in _SMALL:
        if n not in _CONV_SHARD:
            full[n] = w[n]

    loss_part, dx, g = _local_step(x[0], p[:, 0], loss_target[0], full)

    g_big = dict(zip(names, _reduce_scatter([_BIG[n][2](g[n]) for n in names], names)))
    small_sum = _exchange_small(jnp.concatenate([loss_part, _pack_small(g, _SMALL)], axis=0), True, "reduce_small")
    loss = small_sum[0, 0]
    g_small = _unpack_small(small_sum, _SMALL, head_rows=SUBLANES)
    for n, (rows, cols) in _CONV_SHARD.items():
        g_small[n] = lax.dynamic_slice_in_dim(g_small[n], chip * cols, cols, axis=1)

    grads, delta, new_m, new_v = {}, {}, {}, {}
    for n in _BIG:
        shape2 = (-1, shard_shapes[n][-1])
        grads[n] = g_big[n].reshape(w[n].shape)
        d_, m_, v_ = _adamw(w[n].reshape(shape2), g_big[n].reshape(shape2), m[n].reshape(shape2), v[n].reshape(shape2),
                            f"adamw_{n}")
        delta[n], new_m[n], new_v[n] = (t.reshape(w[n].shape) for t in (d_, m_, v_))
    own = {n: (_CONV_SHARD[n] if n in _CONV_SHARD else _SMALL[n]) for n in _SMALL}
    packs = [_pack_small({n: src[n] for n in _SMALL}, own) for src in (w, g_small, m, v)]
    outs = [_unpack_small(t, own) for t in _adamw(*packs, "adamw_small")]
    for n in _SMALL:
        grads[n] = g_small[n].reshape(w[n].shape)
        delta[n], new_m[n], new_v[n] = (t[n].reshape(w[n].shape) for t in outs)
    return (loss, dx[None], *[grads[n] for n in _WEIGHTS], *[delta[n] for n in _WEIGHTS],
            *[new_m[n] for n in _WEIGHTS], *[new_v[n] for n in _WEIGHTS])
```

```python
import functools

import jax
import jax.numpy as jnp
from jax import lax
from jax.experimental import pallas as pl
from jax.experimental.pallas import tpu as pltpu

F32 = jnp.float32
BF16 = jnp.bfloat16
HI = lax.Precision.HIGHEST

D_MODEL = 1024
PL_DIM = 256
GDN_HEADS = 8
HEAD_DIM = 128
GDN_CHUNK = 64
HGRN_HEADS = 16
HGRN_CHUNK = 32
HGRN_WIDTH = 2048
DEEPNORM_ALPHA = 4.0 ** 0.25
NORM_EPS = 1e-5
ADAM_LR, ADAM_B1, ADAM_B2, ADAM_EPS, ADAM_WD, ADAM_STEP = 0.001, 0.9, 0.999, 1e-08, 0.01, 10

VMEM_LIMIT = 56 * 1024 * 1024
SUBLANES = 8
LANES = 128


def _params(*sem):
    return pltpu.CompilerParams(dimension_semantics=sem, vmem_limit_bytes=VMEM_LIMIT)


ONE_PASS, THREE_PASS, FULL_F32, EXACT_LHS, EXACT_RHS = 0, 1, 2, 3, 4


def _split3(v):
    hi = v.astype(BF16)
    r1 = v - hi.astype(F32)
    mid = r1.astype(BF16)
    return hi, mid, (r1 - mid.astype(F32)).astype(BF16)


def _mm_raw(a, b, kind, prec):
    nb = a.ndim - 2
    ca = a.ndim - 1 if kind[0] == "n" else a.ndim - 2
    cb = b.ndim - 2 if kind[1] == "n" else b.ndim - 1
    dims = (((ca,), (cb,)), (tuple(range(nb)),) * 2)
    if prec == FULL_F32:
        return lax.dot_general(a, b, dims, precision=HI, preferred_element_type=F32)
    dot = lambda p, q: lax.dot_general(p, q, dims, preferred_element_type=F32)
    ah, bh = a.astype(BF16), b.astype(BF16)
    if prec == ONE_PASS:
        return dot(ah, bh)
    if prec == EXACT_LHS:
        b1, b2, b3 = _split3(b)
        return dot(ah, b1) + (dot(ah, b2) + dot(ah, b3))
    if prec == EXACT_RHS:
        a1, a2, a3 = _split3(a)
        return dot(a1, bh) + (dot(a2, bh) + dot(a3, bh))
    al = (a - ah.astype(F32)).astype(BF16)
    bl = (b - bh.astype(F32)).astype(BF16)
    return dot(ah, bh) + (dot(ah, bl) + dot(al, bh))


@functools.partial(jax.custom_vjp, nondiff_argnums=(2, 3))
def _mm_vjp(a, b, kind, hi):
    return _mm_raw(a, b, kind, hi)


def _mm_vjp_fwd(a, b, kind, hi):
    return _mm_raw(a, b, kind, hi), (a, b)


def _mm_vjp_bwd(kind, hi, res, dc):
    a, b = res
    if hi in (EXACT_LHS, EXACT_RHS):
        assert kind == "nn"
        if hi == EXACT_LHS:
            return jnp.zeros_like(a), _mm_raw(a, dc, "tn", EXACT_LHS)
        return _mm_raw(dc, b, "nt", EXACT_RHS), jnp.zeros_like(b)
    if kind == "nn":
        return _mm_raw(dc, b, "nt", hi), _mm_raw(a, dc, "tn", hi)
    if kind == "nt":
        return _mm_raw(dc, b, "nn", hi), _mm_raw(dc, a, "tn", hi)
    return _mm_raw(b, dc, "nt", hi), _mm_raw(a, dc, "nn", hi)


_mm_vjp.defvjp(_mm_vjp_fwd, _mm_vjp_bwd)


def _lane_total(v):
    return jnp.broadcast_to(jnp.sum(v, axis=-1, keepdims=True), v.shape)


def _matmul(a, b, *, name, ta=False, tb=False, out_dtype=F32, add=None, add_scale=1.0, tm=1024, tn=1024, tk=1024):
    m, k = (a.shape[1], a.shape[0]) if ta else a.shape
    n = b.shape[0] if tb else b.shape[1]
    tm, tn, tk = min(tm, m), min(tn, n), min(tk, k)
    assert m % tm == 0 and n % tn == 0 and k % tk == 0, (name, m, n, k)
    nk = k // tk
    dims = (((0 if ta else 1,), (1 if tb else 0,)), ((), ()))

    def body(*refs):
        if add is None:
            a_ref, b_ref, o_ref, acc_ref = refs
        else:
            a_ref, b_ref, add_ref, o_ref, acc_ref = refs
        kk = pl.program_id(2)

        @pl.when(kk == 0)
        def _():
            acc_ref[...] = jnp.zeros_like(acc_ref)

        acc_ref[...] += lax.dot_general(a_ref[...].astype(BF16), b_ref[...].astype(BF16), dims,
                                        preferred_element_type=F32)

        @pl.when(kk == nk - 1)
        def _():
            r = acc_ref[...]
            if add is not None:
                r = r + add_scale * add_ref[...].astype(F32)
            o_ref[...] = r.astype(out_dtype)

    a_spec = pl.BlockSpec((tk, tm), lambda i, j, kk: (kk, i)) if ta else pl.BlockSpec((tm, tk), lambda i, j, kk: (i, kk))
    b_spec = pl.BlockSpec((tn, tk), lambda i, j, kk: (j, kk)) if tb else pl.BlockSpec((tk, tn), lambda i, j, kk: (kk, j))
    o_spec = pl.BlockSpec((tm, tn), lambda i, j, kk: (i, j))
    in_specs = [a_spec, b_spec] + ([o_spec] if add is not None else [])
    args = (a, b) + ((add,) if add is not None else ())
    return pl.pallas_call(
        body, name=name, grid=(m // tm, n // tn, nk), in_specs=in_specs, out_specs=o_spec,
        out_shape=jax.ShapeDtypeStruct((m, n), out_dtype), scratch_shapes=[pltpu.VMEM((tm, tn), F32)],
        compiler_params=_params("parallel", "parallel", "arbitrary"))(*args)


HALO = SUBLANES


def _halo_specs(tt, width, col, nt):
    r = tt // HALO
    prev = pl.BlockSpec((HALO, width), lambda i: (jnp.maximum(i * r - 1, 0), col))
    nxt = pl.BlockSpec((HALO, width), lambda i: (jnp.minimum((i + 1) * r, nt * r - 1), col))
    return prev, nxt


def _shift_down(ext, k):
    return ext if k == 0 else pltpu.roll(ext, k, 0)


def _shift_up(ext, k):
    return ext if k == 0 else pltpu.roll(ext, ext.shape[0] - k, 0)


def _causal_conv(ext, w, taps):
    acc = None
    for j in range(taps):
        term = w[j:j + 1, :] * _shift_down(ext, taps - 1 - j)
        acc = term if acc is None else acc + term
    return acc[HALO:, :]


def _conv_a_fwd(proj_a, conv_w):
    t = proj_a.shape[0]
    tt = min(t, 256)
    nt = t // tt
    wdt = 1024

    def body(cur_ref, prev_ref, w_ref, y_ref):
        i = pl.program_id(0)
        cur = cur_ref[...]
        h, c, b, z = (cur[:, k * wdt:(k + 1) * wdt] for k in range(4))
        prev = prev_ref[...]
        u_prev = jnp.where(i > 0, prev[:, wdt:2 * wdt] * prev[:, 0:wdt], 0.0)
        ext = jnp.concatenate([u_prev, c * h], axis=0)
        conv = _causal_conv(ext, w_ref[...], 3)
        y_ref[...] = (b * conv * jax.nn.silu(z)).astype(BF16)

    prev_spec, _ = _halo_specs(tt, 4 * wdt, 0, nt)
    return pl.pallas_call(
        body, name="conv_a_fwd", grid=(nt,),
        in_specs=[pl.BlockSpec((tt, 4 * wdt), lambda i: (i, 0)), prev_spec, pl.BlockSpec((3, wdt), lambda i: (0, 0))],
        out_specs=pl.BlockSpec((tt, wdt), lambda i: (i, 0)),
        out_shape=jax.ShapeDtypeStruct((t, wdt), BF16), compiler_params=_params("parallel"))(proj_a, proj_a, conv_w)


def _conv_a_bwd(proj_a, conv_w, dy):
    t = proj_a.shape[0]
    tt = min(t, 256)
    nt = t // tt
    wdt = 1024

    def body(cur_ref, prev_ref, nxt_ref, w_ref, dy_ref, dyn_ref, d_ref, dw_ref):
        i = pl.program_id(0)
        w = w_ref[...]
        cur, prev, nxt = cur_ref[...], prev_ref[...], nxt_ref[...]
        split = lambda a: tuple(a[:, k * wdt:(k + 1) * wdt] for k in range(4))
        h, c, b, z = split(cur)
        hp, cp, _, _ = split(prev)
        hn, cn, bn, zn = split(nxt)
        u_prev = jnp.where(i > 0, cp * hp, 0.0)
        u_ext = jnp.concatenate([u_prev, c * h, cn * hn], axis=0)
        taps = [_shift_down(u_ext, 2 - j)[HALO:, :] for j in range(3)]
        conv = w[0:1, :] * taps[0] + w[1:2, :] * taps[1] + w[2:3, :] * taps[2]
        b_cn = jnp.concatenate([b, bn], axis=0)
        z_cn = jnp.concatenate([z, zn], axis=0)
        dy_cn = jnp.concatenate([dy_ref[...], jnp.where(i < nt - 1, dyn_ref[...], 0.0)], axis=0)
        sg = jax.nn.sigmoid(z_cn)
        silu = z_cn * sg
        d_conv = dy_cn * b_cn * silu
        db = (dy_cn * conv * silu)[:tt, :]
        dz = (dy_cn * b_cn * conv * (sg * (1.0 + z_cn * (1.0 - sg))))[:tt, :]
        du = None
        for j in range(3):
            term = w[j:j + 1, :] * _shift_up(d_conv, 2 - j)
            du = term if du is None else du + term
        du = du[:tt, :]
        d_ref[...] = jnp.concatenate([du * c, du * h, db, dz], axis=1).astype(BF16)

        @pl.when(i == 0)
        def _():
            dw_ref[...] = jnp.zeros_like(dw_ref)

        d_cur = d_conv[:tt, :]
        rows = [jnp.sum(d_cur * taps[j][:tt, :], axis=0, keepdims=True) for j in range(3)]
        dw_ref[0:3, :] += jnp.concatenate(rows, axis=0)

    prev_spec, nxt_spec = _halo_specs(tt, 4 * wdt, 0, nt)
    _, dyn_spec = _halo_specs(tt, wdt, 0, nt)
    return pl.pallas_call(
        body, name="conv_a_bwd", grid=(nt,),
        in_specs=[pl.BlockSpec((tt, 4 * wdt), lambda i: (i, 0)), prev_spec, nxt_spec,
                  pl.BlockSpec((3, wdt), lambda i: (0, 0)), pl.BlockSpec((tt, wdt), lambda i: (i, 0)), dyn_spec],
        out_specs=[pl.BlockSpec((tt, 4 * wdt), lambda i: (i, 0)), pl.BlockSpec((SUBLANES, wdt), lambda i: (0, 0))],
        out_shape=[jax.ShapeDtypeStruct((t, 4 * wdt), BF16), jax.ShapeDtypeStruct((SUBLANES, wdt), F32)],
        compiler_params=_params("arbitrary"))(proj_a, proj_a, proj_a, conv_w, dy, dy)


def _conv_b_fwd(proj_qkv, conv_w):
    t, width = proj_qkv.shape
    tt = min(t, 256)
    nt = t // tt
    wdt = 1024

    def body(cur_ref, prev_ref, w_ref, y_ref):
        i = pl.program_id(1)
        ext = jnp.concatenate([jnp.where(i > 0, prev_ref[...], 0.0), cur_ref[...]], axis=0)
        y_ref[...] = jax.nn.silu(_causal_conv(ext, w_ref[...], 4))

    r = tt // HALO
    return pl.pallas_call(
        body, name="conv_b_fwd", grid=(width // wdt, nt),
        in_specs=[pl.BlockSpec((tt, wdt), lambda j, i: (i, j)),
                  pl.BlockSpec((HALO, wdt), lambda j, i: (jnp.maximum(i * r - 1, 0), j)),
                  pl.BlockSpec((4, wdt), lambda j, i: (0, j))],
        out_specs=pl.BlockSpec((tt, wdt), lambda j, i: (i, j)),
        out_shape=jax.ShapeDtypeStruct((t, width), F32), compiler_params=_params("parallel", "parallel"))(
            proj_qkv, proj_qkv, conv_w)


def _conv_b_bwd(proj_qkv, conv_w, d_act, col, name):
    t = proj_qkv.shape[0]
    tt = min(t, 256)
    nt = t // tt
    wdt = 1024

    def body(cur_ref, prev_ref, nxt_ref, w_ref, da_ref, dan_ref, d_ref, dw_ref):
        i = pl.program_id(0)
        w = w_ref[...]
        u_ext = jnp.concatenate([jnp.where(i > 0, prev_ref[...], 0.0), cur_ref[...], nxt_ref[...]], axis=0)
        taps = [_shift_down(u_ext, 3 - j)[HALO:, :] for j in range(4)]
        conv = w[0:1, :] * taps[0] + w[1:2, :] * taps[1] + w[2:3, :] * taps[2] + w[3:4, :] * taps[3]
        da_cn = jnp.concatenate([da_ref[...], jnp.where(i < nt - 1, dan_ref[...], 0.0)], axis=0)
        sg = jax.nn.sigmoid(conv)
        d_conv = da_cn * (sg * (1.0 + conv * (1.0 - sg)))
        du = None
        for j in range(4):
            term = w[j:j + 1, :] * _shift_up(d_conv, 3 - j)
            du = term if du is None else du + term
        d_ref[...] = du[:tt, :].astype(BF16)

        @pl.when(i == 0)
        def _():
            dw_ref[...] = jnp.zeros_like(dw_ref)

        d_cur = d_conv[:tt, :]
        rows = [jnp.sum(d_cur * taps[j][:tt, :], axis=0, keepdims=True) for j in range(4)]
        dw_ref[0:4, :] += jnp.concatenate(rows, axis=0)

    prev_spec, nxt_spec = _halo_specs(tt, wdt, col, nt)
    _, dan_spec = _halo_specs(tt, wdt, 0, nt)
    return pl.pallas_call(
        body, name=name, grid=(nt,),
        in_specs=[pl.BlockSpec((tt, wdt), lambda i: (i, col)), prev_spec, nxt_spec,
                  pl.BlockSpec((4, wdt), lambda i: (0, col)), pl.BlockSpec((tt, wdt), lambda i: (i, 0)), dan_spec],
        out_specs=[pl.BlockSpec((tt, wdt), lambda i: (i, 0)), pl.BlockSpec((SUBLANES, wdt), lambda i: (0, 0))],
        out_shape=[jax.ShapeDtypeStruct((t, wdt), BF16), jax.ShapeDtypeStruct((SUBLANES, wdt), F32)],
        compiler_params=_params("arbitrary"))(proj_qkv, proj_qkv, proj_qkv, conv_w, d_act, d_act)


def _rms_gate(o, gn, z):
    on = o * lax.rsqrt(jnp.mean(o * o, axis=-1, keepdims=True) + NORM_EPS) * gn
    return on * jax.nn.silu(z)


GDN_PREP_ROWS = 512


def _unit_lower_inverse(low):
    c = low.shape[-1]
    eye = lax.broadcasted_iota(jnp.int32, low.shape, low.ndim - 2) == lax.broadcasted_iota(jnp.int32, low.shape, low.ndim - 1)
    x = -low
    inv = eye.astype(F32) + x
    for _ in range(c.bit_length() - 2):
        x = _mm_raw(x, x, "nn", THREE_PASS)
        inv = inv + _mm_raw(inv, x, "nn", THREE_PASS)
    return inv


@jax.custom_vjp
def _unit_lower_inverse_vjp(low):
    return _unit_lower_inverse(low)


def _unit_lower_inverse_fwd(low):
    inv = _unit_lower_inverse(low)
    return inv, inv


def _unit_lower_inverse_bwd(inv, d_inv):
    return (-_mm_raw(_mm_raw(inv, d_inv, "tn", THREE_PASS), inv, "nt", THREE_PASS),)


_unit_lower_inverse_vjp.defvjp(_unit_lower_inverse_fwd, _unit_lower_inverse_bwd)


def _gdn_prep(mm, qa, ka, va, braw, araw, alog, dtb):
    n, c, _ = qa.shape
    q = qa * lax.rsqrt(jnp.sum(qa * qa, axis=-1, keepdims=True) + 1e-6) * (HEAD_DIM ** -0.5)
    k = ka * lax.rsqrt(jnp.sum(ka * ka, axis=-1, keepdims=True) + 1e-6)
    beta = jax.nn.sigmoid(braw)
    g = -jnp.exp(alog) * jax.nn.softplus(araw + dtb)
    ri = lax.broadcasted_iota(jnp.int32, (n, c, c), 1)
    ci = lax.broadcasted_iota(jnp.int32, (n, c, c), 2)
    incl, strict, eye = ri >= ci, ri > ci, ri == ci
    gc = mm(incl.astype(F32), g, "nn", EXACT_LHS)
    gc_i = gc[:, :, :c]
    gc_j = mm(jnp.ones((n, c, c), F32), jnp.where(eye, gc_i, 0.0), "nn", EXACT_LHS)
    decay = jnp.where(incl, jnp.exp(jnp.where(incl, gc_i - gc_j, 0.0)), 0.0)
    kb = k * beta
    low = jnp.where(strict, mm(kb, k, "nt", ONE_PASS) * decay, 0.0)
    inv = _unit_lower_inverse(low) if mm is _mm_raw else _unit_lower_inverse_vjp(low)
    egc = jnp.exp(gc)
    u = mm(inv, va * beta, "nn", THREE_PASS)
    w = mm(inv, kb * egc, "nn", THREE_PASS)
    attn = jnp.where(incl, mm(q, k, "nt", ONE_PASS) * decay, 0.0)
    g_last = jnp.sum(g, axis=1, keepdims=True)
    return u, w, q * egc, k * jnp.exp(g_last - gc), attn, jnp.exp(g_last)


def _gdn_scan(mm, u, w, qd, kd, attn, egl, z, gn, state):
    v_new = u - mm(w, state, "nn", ONE_PASS)
    o = mm(qd, state, "nn", ONE_PASS) + mm(attn, v_new, "nn", ONE_PASS)
    new_state = state * egl + mm(kd, v_new, "tn", ONE_PASS)
    return _rms_gate(o, gn, z), new_state


def _chunks(ref_value, n, c):
    return ref_value.reshape(n, c, ref_value.shape[-1])


def _by_head(ref, rows, heads):
    return jnp.stack([ref[rows, pl.ds(h * HEAD_DIM, HEAD_DIM)] for h in range(heads)])


def _store_heads(ref, rows, value):
    for h in range(value.shape[0]):
        ref[rows, pl.ds(h * HEAD_DIM, HEAD_DIM)] = value[h]


def _gdn_prep_specs(tb, nt_unused=None):
    col = lambda off: pl.BlockSpec((tb, HEAD_DIM), lambda h, i: (i, off + h))
    rep = pl.BlockSpec((1, tb, LANES), lambda h, i: (h, i, 0))
    par = pl.BlockSpec((1, SUBLANES, LANES), lambda h, i: (h, 0, 0))
    att = pl.BlockSpec((1, tb, GDN_CHUNK), lambda h, i: (h, i, 0))
    egl = pl.BlockSpec((1, tb // GDN_CHUNK, SUBLANES, LANES), lambda h, i: (h, i, 0, 0))
    return col, rep, par, att, egl


def _gdn_prep_fwd(qkv_act, braw, araw, alog, dtb):
    t = qkv_act.shape[0]
    tb = min(t, GDN_PREP_ROWS)
    nt, nc = t // tb, tb // GDN_CHUNK
    width = GDN_HEADS * HEAD_DIM

    def body(q_ref, k_ref, v_ref, br_ref, ar_ref, al_ref, dt_ref, u_ref, w_ref, qd_ref, kd_ref, at_ref, eg_ref):
        ch = lambda r: _chunks(r, nc, GDN_CHUNK)
        u, w, qd, kd, attn, egl = _gdn_prep(_mm_raw, ch(q_ref[...]), ch(k_ref[...]), ch(v_ref[...]), ch(br_ref[0]),
                                            ch(ar_ref[0]), al_ref[0, 0:1, :], dt_ref[0, 0:1, :])
        u_ref[...] = u.reshape(tb, HEAD_DIM)
        w_ref[...] = w.reshape(tb, HEAD_DIM).astype(BF16)
        qd_ref[...] = qd.reshape(tb, HEAD_DIM).astype(BF16)
        kd_ref[...] = kd.reshape(tb, HEAD_DIM).astype(BF16)
        at_ref[0] = attn.reshape(tb, GDN_CHUNK).astype(BF16)
        eg_ref[0] = jnp.broadcast_to(egl, (nc, SUBLANES, LANES))

    col, rep, par, att, egl = _gdn_prep_specs(tb)
    h = GDN_HEADS
    return pl.pallas_call(
        body, name="gdn_prep_fwd", grid=(h, nt),
        in_specs=[col(0), col(h), col(2 * h), rep, rep, par, par],
        out_specs=[col(0), col(0), col(0), col(0), att, egl],
        out_shape=[jax.ShapeDtypeStruct((t, width), F32)] + [jax.ShapeDtypeStruct((t, width), BF16)] * 3
        + [jax.ShapeDtypeStruct((h, t, GDN_CHUNK), BF16), jax.ShapeDtypeStruct((h, t // GDN_CHUNK, SUBLANES, LANES), F32)],
        compiler_params=_params("parallel", "parallel"))(qkv_act, qkv_act, qkv_act, braw, araw, alog, dtb)


def _gdn_prep_bwd(qkv_act, braw, araw, alog, dtb, du, dw, dqd, dkd, dattn, degl):
    t = qkv_act.shape[0]
    tb = min(t, GDN_PREP_ROWS)
    nt, nc = t // tb, tb // GDN_CHUNK
    width = GDN_HEADS * HEAD_DIM

    def body(q_ref, k_ref, v_ref, br_ref, ar_ref, al_ref, dt_ref, du_ref, dw_ref, dqd_ref, dkd_ref, dat_ref, deg_ref,
             dq_ref, dk_ref, dv_ref, dbr_ref, dar_ref, dal_ref, ddt_ref):
        @pl.when(pl.program_id(1) == 0)
        def _():
            dal_ref[...] = jnp.zeros_like(dal_ref)
            ddt_ref[...] = jnp.zeros_like(ddt_ref)

        ch = lambda r: _chunks(r, nc, GDN_CHUNK)
        _, vjp = jax.vjp(functools.partial(_gdn_prep, _mm_vjp), ch(q_ref[...]), ch(k_ref[...]), ch(v_ref[...]),
                         ch(br_ref[0]), ch(ar_ref[0]), al_ref[0, 0:1, :], dt_ref[0, 0:1, :])
        dq, dk, dv, dbr, dar, dal, ddt = vjp((ch(du_ref[...]), ch(dw_ref[...]), ch(dqd_ref[...]), ch(dkd_ref[...]),
                                              ch(dat_ref[0]), deg_ref[0][:, 0:1, :]))
        dq_ref[...] = dq.reshape(tb, HEAD_DIM)
        dk_ref[...] = dk.reshape(tb, HEAD_DIM)
        dv_ref[...] = dv.reshape(tb, HEAD_DIM)
        dbr_ref[0] = _lane_total(dbr.reshape(tb, LANES))
        dar_ref[0] = _lane_total(dar.reshape(tb, LANES))
        dal_ref[0, 0:1, :] += _lane_total(dal)
        ddt_ref[0, 0:1, :] += _lane_total(ddt)

    col, rep, par, att, egl = _gdn_prep_specs(tb)
    h = GDN_HEADS
    return pl.pallas_call(
        body, name="gdn_prep_bwd", grid=(h, nt),
        in_specs=[col(0), col(h), col(2 * h), rep, rep, par, par, col(0), col(0), col(0), col(0), att, egl],
        out_specs=[col(0), col(0), col(0), rep, rep, par, par],
        out_shape=[jax.ShapeDtypeStruct((t, width), F32)] * 3 + [jax.ShapeDtypeStruct((h, t, LANES), F32)] * 2
        + [jax.ShapeDtypeStruct((h, SUBLANES, LANES), F32)] * 2,
        compiler_params=_params("parallel", "arbitrary"))(qkv_act, qkv_act, qkv_act, braw, araw, alog, dtb,
                                                         du, dw, dqd, dkd, dattn, degl)


def _scan_specs(tb, heads, chunk, rev, nt):
    ti = (lambda i: nt - 1 - i) if rev else (lambda i: i)
    row = pl.BlockSpec((tb, heads * HEAD_DIM), lambda i: (ti(i), 0))
    att = pl.BlockSpec((heads, tb, chunk), lambda i: (0, ti(i), 0))
    egl = pl.BlockSpec((heads, tb // chunk, SUBLANES, LANES), lambda i: (0, ti(i), 0, 0))
    hist = pl.BlockSpec((heads, tb // chunk, HEAD_DIM, HEAD_DIM), lambda i: (0, ti(i), 0, 0))
    gn = pl.BlockSpec((SUBLANES, LANES), lambda i: (0, 0))
    return row, att, egl, hist, gn


def _gdn_scan_fwd(u, w, qd, kd, attn, egl, zb, gn):
    t = u.shape[0]
    tb = min(t, 256)
    nt, nc = t // tb, tb // GDN_CHUNK
    nh = GDN_HEADS

    def body(u_ref, w_ref, qd_ref, kd_ref, at_ref, eg_ref, z_ref, gn_ref, y_ref, hist_ref, s_ref):
        @pl.when(pl.program_id(0) == 0)
        def _():
            s_ref[...] = jnp.zeros_like(s_ref)

        g = gn_ref[0:1, :]
        state = s_ref[...]
        for c in range(nc):
            rows = pl.ds(c * GDN_CHUNK, GDN_CHUNK)
            heads = lambda r: _by_head(r, rows, nh)
            hist_ref[:, c] = state
            y, state = _gdn_scan(_mm_raw, heads(u_ref), heads(w_ref), heads(qd_ref), heads(kd_ref), at_ref[:, rows, :],
                                 eg_ref[:, c, 0:1, :], heads(z_ref), g, state)
            _store_heads(y_ref, rows, y.astype(BF16))
        s_ref[...] = state

    row, att, egs, hist, gns = _scan_specs(tb, nh, GDN_CHUNK, False, nt)
    return pl.pallas_call(
        body, name="gdn_scan_fwd", grid=(nt,), in_specs=[row, row, row, row, att, egs, row, gns], out_specs=[row, hist],
        out_shape=[jax.ShapeDtypeStruct((t, nh * HEAD_DIM), BF16),
                   jax.ShapeDtypeStruct((nh, t // GDN_CHUNK, HEAD_DIM, HEAD_DIM), F32)],
        scratch_shapes=[pltpu.VMEM((nh, HEAD_DIM, HEAD_DIM), F32)],
        compiler_params=_params("arbitrary"))(u, w, qd, kd, attn, egl, zb, gn)


def _gdn_scan_bwd(u, w, qd, kd, attn, egl, zb, gn, hist, dy):
    t = u.shape[0]
    tb = min(t, 256)
    nt, nc = t // tb, tb // GDN_CHUNK
    nh = GDN_HEADS

    def body(u_ref, w_ref, qd_ref, kd_ref, at_ref, eg_ref, z_ref, gn_ref, hist_ref, dy_ref,
             du_ref, dw_ref, dqd_ref, dkd_ref, dat_ref, deg_ref, dz_ref, dgn_ref, ds_ref):
        @pl.when(pl.program_id(0) == 0)
        def _():
            ds_ref[...] = jnp.zeros_like(ds_ref)
            dgn_ref[...] = jnp.zeros_like(dgn_ref)

        g = gn_ref[0:1, :]
        d_state = ds_ref[...]
        for c in reversed(range(nc)):
            rows = pl.ds(c * GDN_CHUNK, GDN_CHUNK)
            heads = lambda r: _by_head(r, rows, nh).astype(F32)
            _, vjp = jax.vjp(functools.partial(_gdn_scan, _mm_vjp), heads(u_ref), heads(w_ref), heads(qd_ref),
                             heads(kd_ref), at_ref[:, rows, :].astype(F32), eg_ref[:, c, 0:1, :], heads(z_ref), g,
                             hist_ref[:, c])
            du, dw, dqd, dkd, dat, deg, dz, dgn, d_state = vjp((heads(dy_ref), d_state))
            _store_heads(du_ref, rows, du)
            _store_heads(dw_ref, rows, dw)
            _store_heads(dqd_ref, rows, dqd)
            _store_heads(dkd_ref, rows, dkd)
            dat_ref[:, rows, :] = dat
            deg_ref[:, c] = jnp.broadcast_to(deg, (nh, SUBLANES, LANES))
            _store_heads(dz_ref, rows, dz.astype(BF16))
            dgn_ref[0:1, :] += dgn
        ds_ref[...] = d_state

    row, att, egs, hists, gns = _scan_specs(tb, nh, GDN_CHUNK, True, nt)
    wide = jax.ShapeDtypeStruct((t, nh * HEAD_DIM), F32)
    return pl.pallas_call(
        body, name="gdn_scan_bwd", grid=(nt,),
        in_specs=[row, row, row, row, att, egs, row, gns, hists, row],
        out_specs=[row, row, row, row, att, egs, row, gns],
        out_shape=[wide] * 4 + [jax.ShapeDtypeStruct((nh, t, GDN_CHUNK), F32),
                                jax.ShapeDtypeStruct((nh, t // GDN_CHUNK, SUBLANES, LANES), F32),
                                jax.ShapeDtypeStruct((t, nh * HEAD_DIM), BF16),
                                jax.ShapeDtypeStruct((SUBLANES, LANES), F32)],
        scratch_shapes=[pltpu.VMEM((nh, HEAD_DIM, HEAD_DIM), F32)],
        compiler_params=_params("arbitrary"))(u, w, qd, kd, attn, egl, zb, gn, hist, dy)


def _hgrn_prep(mm, qr, fr, lbl):
    n, c, _ = qr.shape
    lb = jax.nn.sigmoid(lbl[1:2, :] - lbl[0:1, :])
    f = lb + (1.0 - lb) * jax.nn.sigmoid(fr)
    q = jax.nn.silu(qr)
    k = 1.0 - f
    logf = jnp.log(f)
    ri = lax.broadcasted_iota(jnp.int32, (n, c, c), 1)
    ci = lax.broadcasted_iota(jnp.int32, (n, c, c), 2)
    b = mm((ri >= ci).astype(F32), logf, "nn", EXACT_LHS)
    attn = _hgrn_attn(mm, q, k, b)
    b_last = jnp.sum(logf, axis=1, keepdims=True)
    return q * jnp.exp(b), k * jnp.exp(b_last - b), attn, jnp.exp(b_last)


HGRN_SUB = 8


@functools.partial(jax.custom_vjp, nondiff_argnums=(1,))
def _roll_rows(x, shift):
    return pltpu.roll(x, shift, x.ndim - 2)


def _roll_rows_fwd(x, shift):
    return _roll_rows(x, shift), None


def _roll_rows_bwd(shift, _, d):
    return (pltpu.roll(d, d.shape[-2] - shift, d.ndim - 2),)


_roll_rows.defvjp(_roll_rows_fwd, _roll_rows_bwd)


def _hgrn_attn(mm, q, k, b):
    n, c, d = q.shape
    sb = HGRN_SUB
    row = lax.broadcasted_iota(jnp.int32, (n, c, c), 1)
    col = lax.broadcasted_iota(jnp.int32, (n, c, c), 2)
    attn = None
    for delta in range(sb):
        if delta == 0:
            prod = q * k
        else:
            prod = q * _roll_rows(k, delta) * jnp.exp(jnp.minimum(b - _roll_rows(b, delta), 0.0))
        term = jnp.where(row - col == delta, jnp.sum(prod, axis=-1, keepdims=True), 0.0)
        attn = term if attn is None else attn + term
    far = [jnp.zeros((n, sb, c), F32)]
    row8 = lax.broadcasted_iota(jnp.int32, (n, sb, c), 1)
    col8 = lax.broadcasted_iota(jnp.int32, (n, sb, c), 2)
    for i in range(1, c // sb):
        r0 = i * sb
        bi = b[:, r0:r0 + sb, :]
        ref = bi[:, 0:1, :]
        part = mm(q[:, r0:r0 + sb, :] * jnp.exp(bi - ref), k * jnp.exp(jnp.minimum(ref - b, 0.0)), "nt", ONE_PASS)
        far.append(jnp.where(row8 + r0 - col8 >= sb, part, 0.0))
    return attn + jnp.concatenate(far, axis=1)


def _hgrn_scan(mm, qe, kd, attn, ebl, iv, z, gn, state):
    o = mm(qe, state, "nt", ONE_PASS) + mm(attn, iv, "nn", ONE_PASS)
    new_state = state * ebl + mm(iv, kd, "tn", ONE_PASS)
    return _rms_gate(o, gn, z), new_state


def _hgrn_prep_specs(tb):
    col = pl.BlockSpec((tb, HEAD_DIM), lambda h, i: (i, h))
    lbs = pl.BlockSpec((2, HEAD_DIM), lambda h, i: (0, h))
    att = pl.BlockSpec((1, tb, HGRN_CHUNK), lambda h, i: (h, i, 0))
    ebl = pl.BlockSpec((1, tb // HGRN_CHUNK, SUBLANES, LANES), lambda h, i: (h, i, 0, 0))
    return col, lbs, att, ebl


def _hgrn_prep_fwd(qr, fr, lower_bounds):
    t = qr.shape[0]
    tb = min(t, 256)
    nt, nc = t // tb, tb // HGRN_CHUNK
    hh = HGRN_HEADS

    def body(q_ref, f_ref, lb_ref, qe_ref, kd_ref, at_ref, eb_ref):
        ch = lambda r: _chunks(r, nc, HGRN_CHUNK)
        qe, kd, attn, ebl = _hgrn_prep(_mm_raw, ch(q_ref[...]), ch(f_ref[...]), lb_ref[...])
        qe_ref[...] = qe.reshape(tb, HEAD_DIM).astype(BF16)
        kd_ref[...] = kd.reshape(tb, HEAD_DIM).astype(BF16)
        at_ref[0] = attn.reshape(tb, HGRN_CHUNK).astype(BF16)
        eb_ref[0] = jnp.broadcast_to(ebl, (nc, SUBLANES, LANES))

    col, lbs, att, ebs = _hgrn_prep_specs(tb)
    return pl.pallas_call(
        body, name="hgrn_prep_fwd", grid=(hh, nt), in_specs=[col, col, lbs], out_specs=[col, col, att, ebs],
        out_shape=[jax.ShapeDtypeStruct((t, HGRN_WIDTH), BF16)] * 2
        + [jax.ShapeDtypeStruct((hh, t, HGRN_CHUNK), BF16), jax.ShapeDtypeStruct((hh, t // HGRN_CHUNK, SUBLANES, LANES), F32)],
        compiler_params=_params("parallel", "parallel"))(qr, fr, lower_bounds)


def _hgrn_prep_bwd(qr, fr, lower_bounds, dqe, dkd, dattn, debl):
    t = qr.shape[0]
    tb = min(t, 256)
    nt, nc = t // tb, tb // HGRN_CHUNK
    hh = HGRN_HEADS

    def body(q_ref, f_ref, lb_ref, dqe_ref, dkd_ref, dat_ref, deb_ref, dq_ref, df_ref, dlb_ref):
        @pl.when(pl.program_id(1) == 0)
        def _():
            dlb_ref[...] = jnp.zeros_like(dlb_ref)

        ch = lambda r: _chunks(r, nc, HGRN_CHUNK)
        _, vjp = jax.vjp(functools.partial(_hgrn_prep, _mm_vjp), ch(q_ref[...]), ch(f_ref[...]), lb_ref[...])
        dq, df, dlb = vjp((ch(dqe_ref[...]), ch(dkd_ref[...]), ch(dat_ref[0]), deb_ref[0][:, 0:1, :]))
        dq_ref[...] = dq.reshape(tb, HEAD_DIM).astype(BF16)
        df_ref[...] = df.reshape(tb, HEAD_DIM).astype(BF16)
        dlb_ref[...] += dlb

    col, lbs, att, ebs = _hgrn_prep_specs(tb)
    return pl.pallas_call(
        body, name="hgrn_prep_bwd", grid=(hh, nt), in_specs=[col, col, lbs, col, col, att, ebs],
        out_specs=[col, col, lbs],
        out_shape=[jax.ShapeDtypeStruct((t, HGRN_WIDTH), BF16)] * 2 + [jax.ShapeDtypeStruct((2, HGRN_WIDTH), F32)],
        compiler_params=_params("parallel", "arbitrary"))(qr, fr, lower_bounds, dqe, dkd, dattn, debl)


def _hgrn_scan_fwd(qe, kd, attn, ebl, iv, z, gn):
    t = qe.shape[0]
    tb = min(t, 128)
    nt, nc = t // tb, tb // HGRN_CHUNK
    hh = HGRN_HEADS

    def body(qe_ref, kd_ref, at_ref, eb_ref, i_ref, z_ref, gn_ref, y_ref, hist_ref, s_ref):
        @pl.when(pl.program_id(0) == 0)
        def _():
            s_ref[...] = jnp.zeros_like(s_ref)

        g = gn_ref[0:1, :]
        state = s_ref[...]
        for c in range(nc):
            rows = pl.ds(c * HGRN_CHUNK, HGRN_CHUNK)
            heads = lambda r: _by_head(r, rows, hh)
            hist_ref[:, c] = state
            y, state = _hgrn_scan(_mm_raw, heads(qe_ref), heads(kd_ref), at_ref[:, rows, :], eb_ref[:, c, 0:1, :],
                                  heads(i_ref), heads(z_ref), g, state)
            _store_heads(y_ref, rows, y.astype(BF16))
        s_ref[...] = state

    row, att, ebs, hist, gns = _scan_specs(tb, hh, HGRN_CHUNK, False, nt)
    return pl.pallas_call(
        body, name="hgrn_scan_fwd", grid=(nt,), in_specs=[row, row, att, ebs, row, row, gns], out_specs=[row, hist],
        out_shape=[jax.ShapeDtypeStruct((t, HGRN_WIDTH), BF16),
                   jax.ShapeDtypeStruct((hh, t // HGRN_CHUNK, HEAD_DIM, HEAD_DIM), F32)],
        scratch_shapes=[pltpu.VMEM((hh, HEAD_DIM, HEAD_DIM), F32)],
        compiler_params=_params("arbitrary"))(qe, kd, attn, ebl, iv, z, gn)


def _hgrn_scan_bwd(qe, kd, attn, ebl, iv, z, gn, hist, dy):
    t = qe.shape[0]
    tb = min(t, 128)
    nt, nc = t // tb, tb // HGRN_CHUNK
    hh = HGRN_HEADS

    def body(qe_ref, kd_ref, at_ref, eb_ref, i_ref, z_ref, gn_ref, hist_ref, dy_ref,
             dqe_ref, dkd_ref, dat_ref, deb_ref, di_ref, dz_ref, dgn_ref, ds_ref):
        @pl.when(pl.program_id(0) == 0)
        def _():
            ds_ref[...] = jnp.zeros_like(ds_ref)
            dgn_ref[...] = jnp.zeros_like(dgn_ref)

        g = gn_ref[0:1, :]
        d_state = ds_ref[...]
        for c in reversed(range(nc)):
            rows = pl.ds(c * HGRN_CHUNK, HGRN_CHUNK)
            heads = lambda r: _by_head(r, rows, hh).astype(F32)
            _, vjp = jax.vjp(functools.partial(_hgrn_scan, _mm_vjp), heads(qe_ref), heads(kd_ref),
                             at_ref[:, rows, :].astype(F32), eb_ref[:, c, 0:1, :], heads(i_ref), heads(z_ref), g,
                             hist_ref[:, c])
            dqe, dkd, dat, deb, di, dz, dgn, d_state = vjp((heads(dy_ref), d_state))
            _store_heads(dqe_ref, rows, dqe)
            _store_heads(dkd_ref, rows, dkd)
            dat_ref[:, rows, :] = dat
            deb_ref[:, c] = jnp.broadcast_to(deb, (hh, SUBLANES, LANES))
            _store_heads(di_ref, rows, di.astype(BF16))
            _store_heads(dz_ref, rows, dz.astype(BF16))
            dgn_ref[0:1, :] += dgn
        ds_ref[...] = d_state

    row, att, ebs, hists, gns = _scan_specs(tb, hh, HGRN_CHUNK, True, nt)
    wide = lambda dt: jax.ShapeDtypeStruct((t, HGRN_WIDTH), dt)
    return pl.pallas_call(
        body, name="hgrn_scan_bwd", grid=(nt,),
        in_specs=[row, row, att, ebs, row, row, gns, hists, row],
        out_specs=[row, row, att, ebs, row, row, gns],
        out_shape=[wide(F32), wide(F32), jax.ShapeDtypeStruct((hh, t, HGRN_CHUNK), F32),
                   jax.ShapeDtypeStruct((hh, t // HGRN_CHUNK, SUBLANES, LANES), F32), wide(BF16), wide(BF16),
                   jax.ShapeDtypeStruct((SUBLANES, LANES), F32)],
        scratch_shapes=[pltpu.VMEM((hh, HEAD_DIM, HEAD_DIM), F32)],
        compiler_params=_params("arbitrary"))(qe, kd, attn, ebl, iv, z, gn, hist, dy)


def _layer_norm(pre, g, b):
    mu = jnp.mean(pre, axis=-1, keepdims=True)
    d = pre - mu
    var = jnp.mean(d * d, axis=-1, keepdims=True)
    return d * lax.rsqrt(var + NORM_EPS) * g + b


def _lnpl_fwd(xin, s, p, wg, wpl, ln_g, ln_b):
    t = xin.shape[0]
    tt = min(t, 256)

    def body(x_ref, s_ref, p_ref, wg_ref, wpl_ref, g_ref, b_ref, o_ref, ob_ref):
        xn = _layer_norm(DEEPNORM_ALPHA * x_ref[...] + s_ref[...], g_ref[...], b_ref[...])
        gate = jax.nn.sigmoid(_mm_raw(xn, wg_ref[...], "nn", False))
        out = xn + _mm_raw(p_ref[...], wpl_ref[...], "nn", False) * gate
        o_ref[...] = out
        ob_ref[...] = out.astype(BF16)

    row = lambda w: pl.BlockSpec((tt, w), lambda i: (i, 0))
    full = lambda a: pl.BlockSpec(a.shape, lambda i: (0, 0))
    return pl.pallas_call(
        body, name="lnpl_fwd", grid=(t // tt,),
        in_specs=[row(D_MODEL), row(D_MODEL), row(PL_DIM), full(wg), full(wpl), full(ln_g), full(ln_b)],
        out_specs=[row(D_MODEL), row(D_MODEL)],
        out_shape=[jax.ShapeDtypeStruct((t, D_MODEL), F32), jax.ShapeDtypeStruct((t, D_MODEL), BF16)],
        compiler_params=_params("parallel"))(xin, s, p, wg, wpl, ln_g, ln_b)


def _lnpl_bwd(xin, s, p, wg, wpl, ln_g, ln_b, upstream, last, name):
    t = xin.shape[0]
    tt = min(t, 256)

    def body(x_ref, s_ref, p_ref, wg_ref, wpl_ref, g_ref, b_ref, up_ref,
             dpre_ref, dwg_ref, dwpl_ref, dg_ref, db_ref, loss_ref):
        @pl.when(pl.program_id(0) == 0)
        def _():
            for r in (dwg_ref, dwpl_ref, dg_ref, db_ref, loss_ref):
                r[...] = jnp.zeros_like(r)

        pre = DEEPNORM_ALPHA * x_ref[...] + s_ref[...]
        xn, ln_vjp = jax.vjp(_layer_norm, pre, g_ref[...], b_ref[...])
        gate = jax.nn.sigmoid(_mm_raw(xn, wg_ref[...], "nn", False))
        plv = _mm_raw(p_ref[...], wpl_ref[...], "nn", False)
        if last:
            err = xn + plv * gate - up_ref[...]
            dout = err * (1.0 / D_MODEL)
            tot = jnp.sum(jnp.sum(err * err, axis=1, keepdims=True), axis=0, keepdims=True) * (0.5 / D_MODEL)
            loss_ref[...] += jnp.broadcast_to(tot, loss_ref.shape)
        else:
            dout = up_ref[...]
        dplv = dout * gate
        dlogits = dout * plv * gate * (1.0 - gate)
        dwg_ref[...] += _mm_raw(xn, dlogits, "tn", False)
        dwpl_ref[...] += _mm_raw(p_ref[...], dplv, "tn", False)
        dxn = dout + _mm_raw(dlogits, wg_ref[...], "nt", False)
        dpre, dg, db = ln_vjp(dxn)
        dpre_ref[...] = dpre
        dg_ref[...] += dg
        db_ref[...] += db

    row = lambda w: pl.BlockSpec((tt, w), lambda i: (i, 0))
    full = lambda shape: pl.BlockSpec(shape, lambda i: (0, 0))
    return pl.pallas_call(
        body, name=name, grid=(t // tt,),
        in_specs=[row(D_MODEL), row(D_MODEL), row(PL_DIM), full(wg.shape), full(wpl.shape), full(ln_g.shape),
                  full(ln_b.shape), row(D_MODEL)],
        out_specs=[row(D_MODEL), full(wg.shape), full(wpl.shape), full(ln_g.shape), full(ln_b.shape),
                   full((SUBLANES, LANES))],
        out_shape=[jax.ShapeDtypeStruct((t, D_MODEL), F32), jax.ShapeDtypeStruct(wg.shape, F32),
                   jax.ShapeDtypeStruct(wpl.shape, F32), jax.ShapeDtypeStruct(ln_g.shape, F32),
                   jax.ShapeDtypeStruct(ln_b.shape, F32), jax.ShapeDtypeStruct((SUBLANES, LANES), F32)],
        compiler_params=_params("arbitrary"))(xin, s, p, wg, wpl, ln_g, ln_b, upstream)


def _rep_rows(v):
    return jnp.broadcast_to(v.reshape(1, LANES), (SUBLANES, LANES))


def _rep_heads(v):
    return jnp.broadcast_to(v.reshape(-1, 1, 1), (v.shape[0], SUBLANES, LANES))


def _local_step(x, p, target, w):
    a = DEEPNORM_ALPHA
    nh = GDN_HEADS
    xb = x.astype(BF16)
    wie = w["w_in_even"]
    w_a, w_qkv, w_zb = wie[:, :4096], wie[:, 4096:7168], wie[:, 7168:8192]
    w_tail = jnp.pad(wie[:, 8192:], ((0, 0), (0, LANES - 2 * nh)))
    woe, wio, woo = w["w_out_even"], w["w_in_odd"], w["w_out_odd"]
    conv_a_w, conv_b_w = w["conv_a_w"], w["conv_b_w"]
    ln_g0, ln_b0, ln_g1, ln_b1 = (v.reshape(1, D_MODEL) for v in (w["ln_g"][0], w["ln_b"][0], w["ln_g"][1], w["ln_b"][1]))
    alog, dtb = _rep_heads(w["a_log"].reshape(nh)), _rep_heads(w["dt_bias"].reshape(nh))
    gdn_g, hgrn_g = _rep_rows(w["gdn_norm_g"]), _rep_rows(w["hgrn_norm_g"])

    proj_a = _matmul(xb, w_a, name="fwd_proj_a")
    proj_qkv = _matmul(xb, w_qkv, name="fwd_proj_qkv")
    proj_zb = _matmul(xb, w_zb, name="fwd_proj_zb")
    proj_tail = _matmul(xb, w_tail, name="fwd_proj_tail")
    rep = lambda cols: jnp.broadcast_to(cols.T[:, :, None], (nh, cols.shape[0], LANES))
    braw, araw = rep(proj_tail[:, :nh]), rep(proj_tail[:, nh:2 * nh])
    y_a = _conv_a_fwd(proj_a, conv_a_w)
    qkv_act = _conv_b_fwd(proj_qkv, conv_b_w)
    gdn_pre = _gdn_prep_fwd(qkv_act, braw, araw, alog, dtb)
    y_b, gdn_hist = _gdn_scan_fwd(*gdn_pre, proj_zb, gdn_g)
    s0 = _matmul(y_b, woe[1024:], name="fwd_out_even_b", add=_matmul(y_a, woe[:1024], name="fwd_out_even_a"))
    x1, x1b = _lnpl_fwd(x, s0, p[0], w["w_pl_gate"][0], w["w_pl"][0], ln_g0, ln_b0)
    proj_o = [_matmul(x1b, wio[j], name=f"fwd_proj_odd{j}") for j in range(4)]
    hgrn_pre = _hgrn_prep_fwd(proj_o[0], proj_o[1], w["lower_bounds"])
    y_o, hgrn_hist = _hgrn_scan_fwd(*hgrn_pre, proj_o[2], proj_o[3], hgrn_g)
    s1 = _matmul(y_o, woo, name="fwd_out_odd")

    g = {}
    dpre1, dwg1, dwpl1, dlng1, dlnb1, loss = _lnpl_bwd(x1, s1, p[1], w["w_pl_gate"][1], w["w_pl"][1], ln_g1, ln_b1,
                                                     target, True, "lnpl_bwd_odd")
    dy_o = _matmul(dpre1, woo, tb=True, name="bwd_out_odd_dx")
    g["w_out_odd"] = _matmul(y_o, dpre1, ta=True, name="bwd_out_odd_dw")
    dqe, dkd, dat, deb, di, dz, dhg = _hgrn_scan_bwd(*hgrn_pre, proj_o[2], proj_o[3], hgrn_g, hgrn_hist, dy_o)
    dq, df, dlb = _hgrn_prep_bwd(proj_o[0], proj_o[1], w["lower_bounds"], dqe, dkd, dat, deb)
    dx1 = dpre1
    scale = a
    dws = []
    for j, dj in enumerate((dq, df, di, dz)):
        dx1 = _matmul(dj, wio[j], tb=True, add=dx1, add_scale=scale, name=f"bwd_proj_odd_dx{j}")
        scale = 1.0
        dws.append(_matmul(x1b, dj, ta=True, name=f"bwd_proj_odd_dw{j}"))
    g["w_in_odd"] = jnp.stack(dws)
    g["hgrn_norm_g"] = dhg[0:1]
    g["lower_bounds"] = dlb

    dpre0, dwg0, dwpl0, dlng0, dlnb0, _ = _lnpl_bwd(x, s0, p[0], w["w_pl_gate"][0], w["w_pl"][0], ln_g0, ln_b0,
                                                  dx1, False, "lnpl_bwd_even")
    g["w_pl_gate"] = jnp.stack([dwg0, dwg1])
    g["w_pl"] = jnp.stack([dwpl0, dwpl1])
    g["ln_g"] = jnp.concatenate([dlng0, dlng1], axis=0)
    g["ln_b"] = jnp.concatenate([dlnb0, dlnb1], axis=0)
    dy_a = _matmul(dpre0, woe[:1024], tb=True, name="bwd_out_even_dxa")
    dy_b = _matmul(dpre0, woe[1024:], tb=True, name="bwd_out_even_dxb")
    g["w_out_even"] = jnp.concatenate([_matmul(y_a, dpre0, ta=True, name="bwd_out_even_dwa"),
                                       _matmul(y_b, dpre0, ta=True, name="bwd_out_even_dwb")], axis=0)
    du, dw, dqd, dkd, dat, deg, dzb, dgn = _gdn_scan_bwd(*gdn_pre, proj_zb, gdn_g, gdn_hist, dy_b)
    dqa, dka, dva, dbr, dar, dal, ddt = _gdn_prep_bwd(qkv_act, braw, araw, alog, dtb, du, dw, dqd, dkd, dat, deg)
    g["a_log"] = dal[:, 0, 0].reshape(1, nh)
    g["dt_bias"] = ddt[:, 0, 0].reshape(1, nh)
    g["gdn_norm_g"] = dgn[0:1]
    d_pre_qkv, dwb = [], []
    for j, dj in enumerate((dqa, dka, dva)):
        dpj, dwj = _conv_b_bwd(proj_qkv, conv_b_w, dj, j, f"conv_b_bwd{j}")
        d_pre_qkv.append(dpj)
        dwb.append(dwj[:4])
    g["conv_b_w"] = jnp.concatenate(dwb, axis=1)
    d_a, dwa = _conv_a_bwd(proj_a, conv_a_w, dy_a)
    g["conv_a_w"] = dwa[:3]
    d_tail = jnp.concatenate([dbr[:, :, 0].T, dar[:, :, 0].T, jnp.zeros((x.shape[0], LANES - 2 * nh), F32)],
                             axis=1).astype(BF16)
    pieces = [(d_a, w_a), (d_pre_qkv[0], w_qkv[:, :1024]), (d_pre_qkv[1], w_qkv[:, 1024:2048]),
              (d_pre_qkv[2], w_qkv[:, 2048:]), (dzb, w_zb), (d_tail, w_tail)]
    dx = dpre0
    scale = a
    dws = []
    for j, (dj, wj) in enumerate(pieces):
        dx = _matmul(dj, wj, tb=True, add=dx, add_scale=scale, name=f"bwd_proj_even_dx{j}")
        scale = 1.0
        dws.append(_matmul(xb, dj, ta=True, name=f"bwd_proj_even_dw{j}"))
    dws[-1] = dws[-1][:, :2 * nh]
    g["w_in_even"] = jnp.concatenate(dws, axis=1)
    return loss, dx, g


def _adamw(w, g, m, v, name):
    rows, cols = w.shape
    tr = rows if rows <= 256 else 256
    assert rows % tr == 0, (name, rows)

    def body(w_ref, g_ref, m_ref, v_ref, d_ref, nm_ref, nv_ref):
        gg = g_ref[...]
        nm = ADAM_B1 * m_ref[...] + (1.0 - ADAM_B1) * gg
        nv = ADAM_B2 * v_ref[...] + (1.0 - ADAM_B2) * jnp.square(gg)
        m_hat = nm / (1.0 - ADAM_B1 ** ADAM_STEP)
        v_hat = nv / (1.0 - ADAM_B2 ** ADAM_STEP)
        d_ref[...] = -ADAM_LR * (m_hat / (jnp.sqrt(v_hat) + ADAM_EPS) + ADAM_WD * w_ref[...])
        nm_ref[...] = nm
        nv_ref[...] = nv

    spec = pl.BlockSpec((tr, cols), lambda i: (i, 0))
    return pl.pallas_call(
        body, name=name, grid=(rows // tr,), in_specs=[spec] * 4, out_specs=[spec] * 3,
        out_shape=[jax.ShapeDtypeStruct(w.shape, F32)] * 3, compiler_params=_params("parallel"))(w, g, m, v)


MESH = pl.DeviceIdType.MESH
N_DEV = 8
HBM_SPEC = pl.BlockSpec(memory_space=pltpu.HBM)
VMEM_SPEC = pl.BlockSpec(memory_space=pltpu.VMEM)


def _coords():
    return lax.axis_index("x"), lax.axis_index("y"), lax.axis_index("c")


def _flip(v, bit):
    return 1 - v if bit else v


def _remote(src, dst, send_sem, recv_sem, dev):
    return pltpu.make_async_remote_copy(src_ref=src, dst_ref=dst, send_sem=send_sem, recv_sem=recv_sem,
                                        device_id=dev, device_id_type=MESH)


def _exchange_small(buf, reduce, name):
    rows = buf.shape[0]

    def body(in_ref, out_ref, slots, send_sems, recv_sems):
        x, y, c = _coords()
        me = 4 * x + 2 * y + c
        slots[me] = in_ref[...]
        peer = lambda k: (_flip(x, (k >> 2) & 1), _flip(y, (k >> 1) & 1), _flip(c, k & 1))
        sends = []
        for k in range(1, N_DEV):
            cp = _remote(in_ref, slots.at[me], send_sems.at[k - 1], recv_sems.at[k - 1], peer(k))
            cp.start()
            sends.append(cp)
        for k in range(1, N_DEV):
            px, py, pc = peer(k)
            _remote(in_ref, slots.at[4 * px + 2 * py + pc], send_sems.at[k - 1], recv_sems.at[k - 1], peer(k)).wait_recv()
        for cp in sends:
            cp.wait_send()
        if reduce:
            acc = slots[0]
            for d in range(1, N_DEV):
                acc = acc + slots[d]
            out_ref[...] = acc
        else:
            out_ref[...] = slots[...]

    out_shape = (rows, LANES) if reduce else (N_DEV, rows, LANES)
    return pl.pallas_call(
        body, name=name, in_specs=[VMEM_SPEC], out_specs=VMEM_SPEC, out_shape=jax.ShapeDtypeStruct(out_shape, F32),
        scratch_shapes=[pltpu.VMEM((N_DEV, rows, LANES), F32), pltpu.SemaphoreType.DMA((N_DEV - 1,)),
                        pltpu.SemaphoreType.DMA((N_DEV - 1,))])(buf)


def _half_rows(half, which):
    return pl.ds(pl.multiple_of(which * half, 16), half)


def _other_chip(x, y, k):
    return _flip(x, (k >> 1) & 1), _flip(y, k & 1)


def _gather_shards(shards):
    n = len(shards)

    def body(*refs):
        ins, outs = refs[:n], refs[n:2 * n]
        ici_s, ici_r, d2d_s, d2d_r = refs[2 * n:]
        x, y, c = _coords()
        chip = 2 * x + y
        sends = []
        for i in range(n):
            half = ins[i].shape[0] // 2
            for k in (1, 2, 3):
                ox, oy = _other_chip(x, y, k)
                cp = _remote(ins[i].at[_half_rows(half, c)], outs[i].at[chip, _half_rows(half, c)],
                             ici_s.at[3 * i + k - 1], ici_r.at[3 * i + k - 1], (ox, oy, c))
                cp.start()
                sends.append(cp)
        for k in (1, 2, 3):
            ox, oy = _other_chip(x, y, k)
            for i in range(n):
                half = ins[i].shape[0] // 2
                piece = outs[i].at[2 * ox + oy, _half_rows(half, c)]
                _remote(piece, piece, ici_s.at[3 * i + k - 1], ici_r.at[3 * i + k - 1], (ox, oy, c)).wait_recv()
                cp = _remote(piece, piece, d2d_s.at[3 * i + k - 1], d2d_r.at[3 * i + k - 1], (x, y, 1 - c))
                cp.start()
                sends.append(cp)
        for k in (1, 2, 3):
            ox, oy = _other_chip(x, y, k)
            for i in range(n):
                half = ins[i].shape[0] // 2
                piece = outs[i].at[2 * ox + oy, _half_rows(half, 1 - c)]
                _remote(piece, piece, d2d_s.at[3 * i + k - 1], d2d_r.at[3 * i + k - 1], (x, y, 1 - c)).wait_recv()
        for cp in sends:
            cp.wait_send()

    return pl.pallas_call(
        body, name="gather_weights", in_specs=[HBM_SPEC] * n, out_specs=[HBM_SPEC] * n,
        out_shape=[jax.ShapeDtypeStruct((4,) + s.shape, s.dtype) for s in shards],
        scratch_shapes=[pltpu.SemaphoreType.DMA((3 * n,))] * 4)(*shards)


def _rs_sibling_swap(g4s):
    n = len(g4s)

    def body(*refs):
        ins, outs = refs[:n], refs[n:2 * n]
        send_sems, recv_sems = refs[2 * n:]
        x, y, c = _coords()
        sends = []
        for i in range(n):
            half = ins[i].shape[1] // 2
            for s in range(4):
                cp = _remote(ins[i].at[s, _half_rows(half, 1 - c)], outs[i].at[s], send_sems.at[4 * i + s],
                             recv_sems.at[4 * i + s], (x, y, 1 - c))
                cp.start()
                sends.append(cp)
        for cp in sends:
            cp.wait_recv()
        for cp in sends:
            cp.wait_send()

    return pl.pallas_call(
        body, name="rs_sibling_swap", in_specs=[HBM_SPEC] * n, out_specs=[HBM_SPEC] * n,
        out_shape=[jax.ShapeDtypeStruct((4, g.shape[1] // 2, g.shape[2]), g.dtype) for g in g4s],
        scratch_shapes=[pltpu.SemaphoreType.DMA((4 * n,))] * 2)(*g4s)


def _rs_add_sibling(g4, got, c_idx, name):
    _, rows, cols = g4.shape
    half = rows // 2
    tr = min(half, 256)
    nb = half // tr

    def body(c_ref, a_ref, b_ref, o_ref, ob_ref):
        total = a_ref[...] + b_ref[...]
        o_ref[...] = total
        ob_ref[...] = total.astype(BF16)

    blk = (1, tr, cols)
    out = pl.BlockSpec(blk, lambda s, i, c_ref: (s, i, 0))
    grid_spec = pltpu.PrefetchScalarGridSpec(
        num_scalar_prefetch=1, grid=(4, nb),
        in_specs=[pl.BlockSpec(blk, lambda s, i, c_ref: (s, c_ref[0] * nb + i, 0)), out],
        out_specs=[out, out])
    return pl.pallas_call(
        body, name=name, grid_spec=grid_spec,
        out_shape=[jax.ShapeDtypeStruct(got.shape, F32), jax.ShapeDtypeStruct(got.shape, BF16)],
        compiler_params=_params("parallel", "parallel"))(c_idx, g4, got)


def _rs_chip_scatter(p4s):
    n = len(p4s)

    def body(*refs):
        ins, outs = refs[:n], refs[n:2 * n]
        send_sems, recv_sems = refs[2 * n:]
        x, y, c = _coords()
        sends = []
        for i in range(n):
            for k in (1, 2, 3):
                ox, oy = _other_chip(x, y, k)
                cp = _remote(ins[i].at[2 * ox + oy], outs[i].at[k - 1], send_sems.at[3 * i + k - 1],
                             recv_sems.at[3 * i + k - 1], (ox, oy, c))
                cp.start()
                sends.append(cp)
        for cp in sends:
            cp.wait_recv()
        for cp in sends:
            cp.wait_send()

    return pl.pallas_call(
        body, name="rs_chip_scatter", in_specs=[HBM_SPEC] * n, out_specs=[HBM_SPEC] * n,
        out_shape=[jax.ShapeDtypeStruct((3,) + p.shape[1:], p.dtype) for p in p4s],
        scratch_shapes=[pltpu.SemaphoreType.DMA((3 * n,))] * 2)(*p4s)


def _rs_add_chips(p4, got3, idx, name):
    _, half, cols = p4.shape
    tr = min(half, 256)
    nb = half // tr

    def body(idx_ref, p_ref, a_ref, b_ref, c_ref, o_ref):
        o_ref[...] = ((p_ref[0] + a_ref[0].astype(F32)) + b_ref[0].astype(F32)) + c_ref[0].astype(F32)

    blk = (1, tr, cols)
    grid_spec = pltpu.PrefetchScalarGridSpec(
        num_scalar_prefetch=1, grid=(nb,),
        in_specs=[pl.BlockSpec(blk, lambda i, idx_ref: (idx_ref[0], i, 0))]
        + [pl.BlockSpec(blk, functools.partial(lambda k, i, idx_ref: (k, i, 0), k)) for k in range(3)],
        out_specs=pl.BlockSpec((tr, cols), lambda i, idx_ref: (idx_ref[1] * nb + i, 0)))
    return pl.pallas_call(body, name=name, grid_spec=grid_spec, out_shape=jax.ShapeDtypeStruct((2 * half, cols), F32),
                          compiler_params=_params("parallel"))(idx, p4, got3, got3, got3)


def _rs_share_halves(bufs):
    n = len(bufs)

    def body(*refs):
        ins, outs = refs[:n], refs[n:2 * n]
        send_sems, recv_sems = refs[2 * n:]
        x, y, c = _coords()
        sends = []
        for i in range(n):
            half = ins[i].shape[0] // 2
            cp = _remote(ins[i].at[_half_rows(half, c)], outs[i].at[_half_rows(half, c)], send_sems.at[i],
                         recv_sems.at[i], (x, y, 1 - c))
            cp.start()
            sends.append(cp)
        for i in range(n):
            half = ins[i].shape[0] // 2
            _remote(ins[i].at[_half_rows(half, c)], outs[i].at[_half_rows(half, 1 - c)], send_sems.at[i],
                    recv_sems.at[i], (x, y, 1 - c)).wait_recv()
        for cp in sends:
            cp.wait_send()

    return pl.pallas_call(
        body, name="rs_share_halves", in_specs=[HBM_SPEC] * n, out_specs=[HBM_SPEC] * n,
        out_shape=[jax.ShapeDtypeStruct(b.shape, b.dtype) for b in bufs],
        input_output_aliases={i: i for i in range(n)},
        scratch_shapes=[pltpu.SemaphoreType.DMA((n,))] * 2)(*bufs)


def _reduce_scatter(g4s, names):
    x, y, c = _coords()
    c_idx = jnp.stack([c]).astype(jnp.int32)
    idx = jnp.stack([2 * x + y, c]).astype(jnp.int32)
    got = _rs_sibling_swap(g4s)
    p4s = [_rs_add_sibling(g, s, c_idx, f"rs_add_sibling_{nm}") for g, s, nm in zip(g4s, got, names)]
    got3 = _rs_chip_scatter([pb for _, pb in p4s])
    bufs = [_rs_add_chips(p, t, idx, f"rs_add_chips_{nm}") for (p, _), t, nm in zip(p4s, got3, names)]
    return _rs_share_halves(bufs)


def _cols_join(s4):
    return s4.transpose(1, 0, 2).reshape(s4.shape[1], 4 * s4.shape[2])


def _cols_split(full):
    r, c4 = full.shape
    return full.reshape(r, 4, c4 // 4).transpose(1, 0, 2)


_BIG = {
    "w_in_even": ((1024, 2052), _cols_join, _cols_split),
    "w_out_even": ((512, 1024), lambda s: s.reshape(2048, 1024), lambda f: f.reshape(4, 512, 1024)),
    "w_in_odd": ((1024, 2048), lambda s: s, lambda f: f),
    "w_out_odd": ((512, 1024), lambda s: s.reshape(2048, 1024), lambda f: f.reshape(4, 512, 1024)),
    "w_pl": ((512, 256), lambda s: s.reshape(4, 2, 256, 256).transpose(1, 2, 0, 3).reshape(2, 256, 1024),
             lambda f: f.reshape(2, 256, 4, 256).transpose(2, 0, 1, 3).reshape(4, 512, 256)),
    "w_pl_gate": ((512, 1024), lambda s: s.reshape(4, 2, 256, 1024).transpose(1, 0, 2, 3).reshape(2, 1024, 1024),
                  lambda f: f.reshape(2, 4, 256, 1024).transpose(1, 0, 2, 3).reshape(4, 512, 1024)),
}


def _size(shape):
    n = 1
    for d in shape:
        n *= d
    return n


_SMALL = {"a_log": (1, 8), "dt_bias": (1, 8), "gdn_norm_g": (1, 128), "hgrn_norm_g": (1, 128),
          "lower_bounds": (2, 2048), "ln_g": (2, 1024), "ln_b": (2, 1024), "conv_a_w": (3, 1024), "conv_b_w": (4, 3072)}
_CONV_SHARD = {"conv_a_w": (3, 256), "conv_b_w": (4, 768)}


def _pack_small(parts, shapes, head_rows=0):
    rows = []
    for n, shape in shapes.items():
        v = parts[n].reshape(-1)
        rows.append(jnp.pad(v, (0, -v.shape[0] % LANES)).reshape(-1, LANES))
    buf = jnp.concatenate(rows, axis=0)
    return jnp.pad(buf, ((head_rows, -(buf.shape[0] + head_rows) % SUBLANES), (0, 0)))


def _unpack_small(buf, shapes, head_rows=0):
    out, off = {}, head_rows
    for n, shape in shapes.items():
        nrow = -(-_size(shape) // LANES)
        out[n] = buf[off:off + nrow].reshape(-1)[:_size(shape)].reshape(shape)
        off += nrow
    return out


_WEIGHTS = ["w_in_even", "conv_a_w", "conv_b_w", "a_log", "dt_bias", "gdn_norm_g", "w_out_even", "w_in_odd",
            "lower_bounds", "hgrn_norm_g", "w_out_odd", "ln_g", "ln_b", "w_pl", "w_pl_gate"]


def kernel(x, p, w_in_even, conv_a_w, conv_b_w, a_log, dt_bias, gdn_norm_g, w_out_even, w_in_odd, lower_bounds, hgrn_norm_g, w_out_odd, ln_g, ln_b, w_pl, w_pl_gate, loss_target, m_w_in_even, m_conv_a_w, m_conv_b_w, m_a_log, m_dt_bias, m_gdn_norm_g, m_w_out_even, m_w_in_odd, m_lower_bounds, m_hgrn_norm_g, m_w_out_odd, m_ln_g, m_ln_b, m_w_pl, m_w_pl_gate, v_w_in_even, v_conv_a_w, v_conv_b_w, v_a_log, v_dt_bias, v_gdn_norm_g, v_w_out_even, v_w_in_odd, v_lower_bounds, v_hgrn_norm_g, v_w_out_odd, v_ln_g, v_ln_b, v_w_pl, v_w_pl_gate):
    w = dict(zip(_WEIGHTS, (w_in_even, conv_a_w, conv_b_w, a_log, dt_bias, gdn_norm_g, w_out_even, w_in_odd,
                            lower_bounds, hgrn_norm_g, w_out_odd, ln_g, ln_b, w_pl, w_pl_gate)))
    m = dict(zip(_WEIGHTS, (m_w_in_even, m_conv_a_w, m_conv_b_w, m_a_log, m_dt_bias, m_gdn_norm_g, m_w_out_even,
                            m_w_in_odd, m_lower_bounds, m_hgrn_norm_g, m_w_out_odd, m_ln_g, m_ln_b, m_w_pl, m_w_pl_gate)))
    v = dict(zip(_WEIGHTS, (v_w_in_even, v_conv_a_w, v_conv_b_w, v_a_log, v_dt_bias, v_gdn_norm_g, v_w_out_even,
                            v_w_in_odd, v_lower_bounds, v_hgrn_norm_g, v_w_out_odd, v_ln_g, v_ln_b, v_w_pl, v_w_pl_gate)))
    chip = 2 * lax.axis_index("x") + lax.axis_index("y")

    names = list(_BIG)
    shard_shapes = {n: _BIG[n][0] for n in names}
    shards = [w[n].reshape(shard_shapes[n]).astype(BF16) for n in names]
    gathered = _gather_shards(shards)
    full = {n: _BIG[n][1](lax.dynamic_update_slice(ga, sh[None], (chip, 0, 0)))
            for n, ga, sh in zip(names, gathered, shards)}
    conv_mine = _pack_small({n: w[n] for n in _CONV_SHARD}, _CONV_SHARD)
    conv_all = _exchange_small(conv_mine, False, "gather_conv")
    conv_by_chip = [_unpack_small(conv_all[2 * s], _CONV_SHARD) for s in range(4)]
    for n in _CONV_SHARD:
        full[n] = jnp.concatenate([conv_by_chip[s][n] for s in range(4)], axis=1)
    for n in _SMALL:
        if n not in _CONV_SHARD:
            full[n] = w[n]

    loss_part, dx, g = _local_step(x[0], p[:, 0], loss_target[0], full)

    g_big = dict(zip(names, _reduce_scatter([_BIG[n][2](g[n]) for n in names], names)))
    small_sum = _exchange_small(jnp.concatenate([loss_part, _pack_small(g, _SMALL)], axis=0), True, "reduce_small")
    loss = small_sum[0, 0]
    g_small = _unpack_small(small_sum, _SMALL, head_rows=SUBLANES)
    for n, (rows, cols) in _CONV_SHARD.items():
        g_small[n] = lax.dynamic_slice_in_dim(g_small[n], chip * cols, cols, axis=1)

    grads, delta, new_m, new_v = {}, {}, {}, {}
    for n in _BIG:
        shape2 = (-1, shard_shapes[n][-1])
        grads[n] = g_big[n].reshape(w[n].shape)
        d_, m_, v_ = _adamw(w[n].reshape(shape2), g_big[n].reshape(shape2), m[n].reshape(shape2), v[n].reshape(shape2),
                            f"adamw_{n}")
        delta[n], new_m[n], new_v[n] = (t.reshape(w[n].shape) for t in (d_, m_, v_))
    own = {n: (_CONV_SHARD[n] if n in _CONV_SHARD else _SMALL[n]) for n in _SMALL}
    packs = [_pack_small({n: src[n] for n in _SMALL}, own) for src in (w, g_small, m, v)]
    outs = [_unpack_small(t, own) for t in _adamw(*packs, "adamw_small")]
    for n in _SMALL:
        grads[n] = g_small[n].reshape(w[n].shape)
        delta[n], new_m[n], new_v[n] = (t[n].reshape(w[n].shape) for t in outs)
    return (loss, dx[None], *[grads[n] for n in _WEIGHTS], *[delta[n] for n in _WEIGHTS],
            *[new_m[n] for n in _WEIGHTS], *[new_v[n] for n in _WEIGHTS])
```

```python
import functools

import jax
import jax.numpy as jnp
from jax import lax
from jax.experimental import pallas as pl
from jax.experimental.pallas import tpu as pltpu

F32 = jnp.float32
BF16 = jnp.bfloat16
HI = lax.Precision.HIGHEST

D_MODEL = 1024
PL_DIM = 256
GDN_HEADS = 8
HEAD_DIM = 128
GDN_CHUNK = 64
HGRN_HEADS = 16
HGRN_CHUNK = 32
HGRN_WIDTH = 2048
DEEPNORM_ALPHA = 4.0 ** 0.25
NORM_EPS = 1e-5
ADAM_LR, ADAM_B1, ADAM_B2, ADAM_EPS, ADAM_WD, ADAM_STEP = 0.001, 0.9, 0.999, 1e-08, 0.01, 10

VMEM_LIMIT = 56 * 1024 * 1024
SUBLANES = 8
LANES = 128


def _params(*sem):
    return pltpu.CompilerParams(dimension_semantics=sem, vmem_limit_bytes=VMEM_LIMIT)


ONE_PASS, THREE_PASS, FULL_F32, EXACT_LHS, EXACT_RHS = 0, 1, 2, 3, 4


def _split3(v):
    hi = v.astype(BF16)
    r1 = v - hi.astype(F32)
    mid = r1.astype(BF16)
    return hi, mid, (r1 - mid.astype(F32)).astype(BF16)


def _mm_raw(a, b, kind, prec):
    nb = a.ndim - 2
    ca = a.ndim - 1 if kind[0] == "n" else a.ndim - 2
    cb = b.ndim - 2 if kind[1] == "n" else b.ndim - 1
    dims = (((ca,), (cb,)), (tuple(range(nb)),) * 2)
    if prec == FULL_F32:
        return lax.dot_general(a, b, dims, precision=HI, preferred_element_type=F32)
    dot = lambda p, q: lax.dot_general(p, q, dims, preferred_element_type=F32)
    ah, bh = a.astype(BF16), b.astype(BF16)
    if prec == ONE_PASS:
        return dot(ah, bh)
    if prec == EXACT_LHS:
        b1, b2, b3 = _split3(b)
        return dot(ah, b1) + (dot(ah, b2) + dot(ah, b3))
    if prec == EXACT_RHS:
        a1, a2, a3 = _split3(a)
        return dot(a1, bh) + (dot(a2, bh) + dot(a3, bh))
    al = (a - ah.astype(F32)).astype(BF16)
    bl = (b - bh.astype(F32)).astype(BF16)
    return dot(ah, bh) + (dot(ah, bl) + dot(al, bh))


@functools.partial(jax.custom_vjp, nondiff_argnums=(2, 3))
def _mm_vjp(a, b, kind, hi):
    return _mm_raw(a, b, kind, hi)


def _mm_vjp_fwd(a, b, kind, hi):
    return _mm_raw(a, b, kind, hi), (a, b)


def _mm_vjp_bwd(kind, hi, res, dc):
    a, b = res
    if hi in (EXACT_LHS, EXACT_RHS):
        assert kind == "nn"
        if hi == EXACT_LHS:
            return jnp.zeros_like(a), _mm_raw(a, dc, "tn", EXACT_LHS)
        return _mm_raw(dc, b, "nt", EXACT_RHS), jnp.zeros_like(b)
    if kind == "nn":
        return _mm_raw(dc, b, "nt", hi), _mm_raw(a, dc, "tn", hi)
    if kind == "nt":
        return _mm_raw(dc, b, "nn", hi), _mm_raw(dc, a, "tn", hi)
    return _mm_raw(b, dc, "nt", hi), _mm_raw(a, dc, "nn", hi)


_mm_vjp.defvjp(_mm_vjp_fwd, _mm_vjp_bwd)


def _lane_total(v):
    return jnp.broadcast_to(jnp.sum(v, axis=-1, keepdims=True), v.shape)


def _matmul(a, b, *, name, ta=False, tb=False, out_dtype=F32, add=None, add_scale=1.0, tm=1024, tn=1024, tk=1024):
    m, k = (a.shape[1], a.shape[0]) if ta else a.shape
    n = b.shape[0] if tb else b.shape[1]
    tm, tn, tk = min(tm, m), min(tn, n), min(tk, k)
    assert m % tm == 0 and n % tn == 0 and k % tk == 0, (name, m, n, k)
    nk = k // tk
    dims = (((0 if ta else 1,), (1 if tb else 0,)), ((), ()))

    def body(*refs):
        if add is None:
            a_ref, b_ref, o_ref, acc_ref = refs
        else:
            a_ref, b_ref, add_ref, o_ref, acc_ref = refs
        kk = pl.program_id(2)

        @pl.when(kk == 0)
        def _():
            acc_ref[...] = jnp.zeros_like(acc_ref)

        acc_ref[...] += lax.dot_general(a_ref[...].astype(BF16), b_ref[...].astype(BF16), dims,
                                        preferred_element_type=F32)

        @pl.when(kk == nk - 1)
        def _():
            r = acc_ref[...]
            if add is not None:
                r = r + add_scale * add_ref[...].astype(F32)
            o_ref[...] = r.astype(out_dtype)

    a_spec = pl.BlockSpec((tk, tm), lambda i, j, kk: (kk, i)) if ta else pl.BlockSpec((tm, tk), lambda i, j, kk: (i, kk))
    b_spec = pl.BlockSpec((tn, tk), lambda i, j, kk: (j, kk)) if tb else pl.BlockSpec((tk, tn), lambda i, j, kk: (kk, j))
    o_spec = pl.BlockSpec((tm, tn), lambda i, j, kk: (i, j))
    in_specs = [a_spec, b_spec] + ([o_spec] if add is not None else [])
    args = (a, b) + ((add,) if add is not None else ())
    return pl.pallas_call(
        body, name=name, grid=(m // tm, n // tn, nk), in_specs=in_specs, out_specs=o_spec,
        out_shape=jax.ShapeDtypeStruct((m, n), out_dtype), scratch_shapes=[pltpu.VMEM((tm, tn), F32)],
        compiler_params=_params("parallel", "parallel", "arbitrary"))(*args)


HALO = SUBLANES


def _halo_specs(tt, width, col, nt):
    r = tt // HALO
    prev = pl.BlockSpec((HALO, width), lambda i: (jnp.maximum(i * r - 1, 0), col))
    nxt = pl.BlockSpec((HALO, width), lambda i: (jnp.minimum((i + 1) * r, nt * r - 1), col))
    return prev, nxt


def _shift_down(ext, k):
    return ext if k == 0 else pltpu.roll(ext, k, 0)


def _shift_up(ext, k):
    return ext if k == 0 else pltpu.roll(ext, ext.shape[0] - k, 0)


def _causal_conv(ext, w, taps):
    acc = None
    for j in range(taps):
        term = w[j:j + 1, :] * _shift_down(ext, taps - 1 - j)
        acc = term if acc is None else acc + term
    return acc[HALO:, :]


def _conv_a_fwd(proj_a, conv_w):
    t = proj_a.shape[0]
    tt = min(t, 256)
    nt = t // tt
    wdt = 1024

    def body(cur_ref, prev_ref, w_ref, y_ref):
        i = pl.program_id(0)
        cur = cur_ref[...]
        h, c, b, z = (cur[:, k * wdt:(k + 1) * wdt] for k in range(4))
        prev = prev_ref[...]
        u_prev = jnp.where(i > 0, prev[:, wdt:2 * wdt] * prev[:, 0:wdt], 0.0)
        ext = jnp.concatenate([u_prev, c * h], axis=0)
        conv = _causal_conv(ext, w_ref[...], 3)
        y_ref[...] = (b * conv * jax.nn.silu(z)).astype(BF16)

    prev_spec, _ = _halo_specs(tt, 4 * wdt, 0, nt)
    return pl.pallas_call(
        body, name="conv_a_fwd", grid=(nt,),
        in_specs=[pl.BlockSpec((tt, 4 * wdt), lambda i: (i, 0)), prev_spec, pl.BlockSpec((3, wdt), lambda i: (0, 0))],
        out_specs=pl.BlockSpec((tt, wdt), lambda i: (i, 0)),
        out_shape=jax.ShapeDtypeStruct((t, wdt), BF16), compiler_params=_params("parallel"))(proj_a, proj_a, conv_w)


def _conv_a_bwd(proj_a, conv_w, dy):
    t = proj_a.shape[0]
    tt = min(t, 256)
    nt = t // tt
    wdt = 1024

    def body(cur_ref, prev_ref, nxt_ref, w_ref, dy_ref, dyn_ref, d_ref, dw_ref):
        i = pl.program_id(0)
        w = w_ref[...]
        cur, prev, nxt = cur_ref[...], prev_ref[...], nxt_ref[...]
        split = lambda a: tuple(a[:, k * wdt:(k + 1) * wdt] for k in range(4))
        h, c, b, z = split(cur)
        hp, cp, _, _ = split(prev)
        hn, cn, bn, zn = split(nxt)
        u_prev = jnp.where(i > 0, cp * hp, 0.0)
        u_ext = jnp.concatenate([u_prev, c * h, cn * hn], axis=0)
        taps = [_shift_down(u_ext, 2 - j)[HALO:, :] for j in range(3)]
        conv = w[0:1, :] * taps[0] + w[1:2, :] * taps[1] + w[2:3, :] * taps[2]
        b_cn = jnp.concatenate([b, bn], axis=0)
        z_cn = jnp.concatenate([z, zn], axis=0)
        dy_cn = jnp.concatenate([dy_ref[...], jnp.where(i < nt - 1, dyn_ref[...], 0.0)], axis=0)
        sg = jax.nn.sigmoid(z_cn)
        silu = z_cn * sg
        d_conv = dy_cn * b_cn * silu
        db = (dy_cn * conv * silu)[:tt, :]
        dz = (dy_cn * b_cn * conv * (sg * (1.0 + z_cn * (1.0 - sg))))[:tt, :]
        du = None
        for j in range(3):
            term = w[j:j + 1, :] * _shift_up(d_conv, 2 - j)
            du = term if du is None else du + term
        du = du[:tt, :]
        d_ref[...] = jnp.concatenate([du * c, du * h, db, dz], axis=1).astype(BF16)

        @pl.when(i == 0)
        def _():
            dw_ref[...] = jnp.zeros_like(dw_ref)

        d_cur = d_conv[:tt, :]
        rows = [jnp.sum(d_cur * taps[j][:tt, :], axis=0, keepdims=True) for j in range(3)]
        dw_ref[0:3, :] += jnp.concatenate(rows, axis=0)

    prev_spec, nxt_spec = _halo_specs(tt, 4 * wdt, 0, nt)
    _, dyn_spec = _halo_specs(tt, wdt, 0, nt)
    return pl.pallas_call(
        body, name="conv_a_bwd", grid=(nt,),
        in_specs=[pl.BlockSpec((tt, 4 * wdt), lambda i: (i, 0)), prev_spec, nxt_spec,
                  pl.BlockSpec((3, wdt), lambda i: (0, 0)), pl.BlockSpec((tt, wdt), lambda i: (i, 0)), dyn_spec],
        out_specs=[pl.BlockSpec((tt, 4 * wdt), lambda i: (i, 0)), pl.BlockSpec((SUBLANES, wdt), lambda i: (0, 0))],
        out_shape=[jax.ShapeDtypeStruct((t, 4 * wdt), BF16), jax.ShapeDtypeStruct((SUBLANES, wdt), F32)],
        compiler_params=_params("arbitrary"))(proj_a, proj_a, proj_a, conv_w, dy, dy)


def _conv_b_fwd(proj_qkv, conv_w):
    t, width = proj_qkv.shape
    tt = min(t, 256)
    nt = t // tt
    wdt = 1024

    def body(cur_ref, prev_ref, w_ref, y_ref):
        i = pl.program_id(1)
        ext = jnp.concatenate([jnp.where(i > 0, prev_ref[...], 0.0), cur_ref[...]], axis=0)
        y_ref[...] = jax.nn.silu(_causal_conv(ext, w_ref[...], 4))

    r = tt // HALO
    return pl.pallas_call(
        body, name="conv_b_fwd", grid=(width // wdt, nt),
        in_specs=[pl.BlockSpec((tt, wdt), lambda j, i: (i, j)),
                  pl.BlockSpec((HALO, wdt), lambda j, i: (jnp.maximum(i * r - 1, 0), j)),
                  pl.BlockSpec((4, wdt), lambda j, i: (0, j))],
        out_specs=pl.BlockSpec((tt, wdt), lambda j, i: (i, j)),
        out_shape=jax.ShapeDtypeStruct((t, width), F32), compiler_params=_params("parallel", "parallel"))(
            proj_qkv, proj_qkv, conv_w)


def _conv_b_bwd(proj_qkv, conv_w, d_act, col, name):
    t = proj_qkv.shape[0]
    tt = min(t, 256)
    nt = t // tt
    wdt = 1024

    def body(cur_ref, prev_ref, nxt_ref, w_ref, da_ref, dan_ref, d_ref, dw_ref):
        i = pl.program_id(0)
        w = w_ref[...]
        u_ext = jnp.concatenate([jnp.where(i > 0, prev_ref[...], 0.0), cur_ref[...], nxt_ref[...]], axis=0)
        taps = [_shift_down(u_ext, 3 - j)[HALO:, :] for j in range(4)]
        conv = w[0:1, :] * taps[0] + w[1:2, :] * taps[1] + w[2:3, :] * taps[2] + w[3:4, :] * taps[3]
        da_cn = jnp.concatenate([da_ref[...], jnp.where(i < nt - 1, dan_ref[...], 0.0)], axis=0)
        sg = jax.nn.sigmoid(conv)
        d_conv = da_cn * (sg * (1.0 + conv * (1.0 - sg)))
        du = None
        for j in range(4):
            term = w[j:j + 1, :] * _shift_up(d_conv, 3 - j)
            du = term if du is None else du + term
        d_ref[...] = du[:tt, :].astype(BF16)

        @pl.when(i == 0)
        def _():
            dw_ref[...] = jnp.zeros_like(dw_ref)

        d_cur = d_conv[:tt, :]
        rows = [jnp.sum(d_cur * taps[j][:tt, :], axis=0, keepdims=True) for j in range(4)]
        dw_ref[0:4, :] += jnp.concatenate(rows, axis=0)

    prev_spec, nxt_spec = _halo_specs(tt, wdt, col, nt)
    _, dan_spec = _halo_specs(tt, wdt, 0, nt)
    return pl.pallas_call(
        body, name=name, grid=(nt,),
        in_specs=[pl.BlockSpec((tt, wdt), lambda i: (i, col)), prev_spec, nxt_spec,
                  pl.BlockSpec((4, wdt), lambda i: (0, col)), pl.BlockSpec((tt, wdt), lambda i: (i, 0)), dan_spec],
        out_specs=[pl.BlockSpec((tt, wdt), lambda i: (i, 0)), pl.BlockSpec((SUBLANES, wdt), lambda i: (0, 0))],
        out_shape=[jax.ShapeDtypeStruct((t, wdt), BF16), jax.ShapeDtypeStruct((SUBLANES, wdt), F32)],
        compiler_params=_params("arbitrary"))(proj_qkv, proj_qkv, proj_qkv, conv_w, d_act, d_act)


def _rms_gate(o, gn, z):
    on = o * lax.rsqrt(jnp.mean(o * o, axis=-1, keepdims=True) + NORM_EPS) * gn
    return on * jax.nn.silu(z)


GDN_PREP_ROWS = 512


def _unit_lower_inverse(low):
    c = low.shape[-1]
    eye = lax.broadcasted_iota(jnp.int32, low.shape, low.ndim - 2) == lax.broadcasted_iota(jnp.int32, low.shape, low.ndim - 1)
    x = -low
    inv = eye.astype(F32) + x
    for _ in range(c.bit_length() - 2):
        x = _mm_raw(x, x, "nn", THREE_PASS)
        inv = inv + _mm_raw(inv, x, "nn", THREE_PASS)
    return inv


@jax.custom_vjp
def _unit_lower_inverse_vjp(low):
    return _unit_lower_inverse(low)


def _unit_lower_inverse_fwd(low):
    inv = _unit_lower_inverse(low)
    return inv, inv


def _unit_lower_inverse_bwd(inv, d_inv):
    return (-_mm_raw(_mm_raw(inv, d_inv, "tn", THREE_PASS), inv, "nt", THREE_PASS),)


_unit_lower_inverse_vjp.defvjp(_unit_lower_inverse_fwd, _unit_lower_inverse_bwd)


def _gdn_prep(mm, qa, ka, va, braw, araw, alog, dtb):
    n, c, _ = qa.shape
    q = qa * lax.rsqrt(jnp.sum(qa * qa, axis=-1, keepdims=True) + 1e-6) * (HEAD_DIM ** -0.5)
    k = ka * lax.rsqrt(jnp.sum(ka * ka, axis=-1, keepdims=True) + 1e-6)
    beta = jax.nn.sigmoid(braw)
    g = -jnp.exp(alog) * jax.nn.softplus(araw + dtb)
    ri = lax.broadcasted_iota(jnp.int32, (n, c, c), 1)
    ci = lax.broadcasted_iota(jnp.int32, (n, c, c), 2)
    incl, strict, eye = ri >= ci, ri > ci, ri == ci
    gc = mm(incl.astype(F32), g, "nn", EXACT_LHS)
    gc_i = gc[:, :, :c]
    gc_j = mm(jnp.ones((n, c, c), F32), jnp.where(eye, gc_i, 0.0), "nn", EXACT_LHS)
    decay = jnp.where(incl, jnp.exp(jnp.where(incl, gc_i - gc_j, 0.0)), 0.0)
    kb = k * beta
    low = jnp.where(strict, mm(kb, k, "nt", ONE_PASS) * decay, 0.0)
    inv = _unit_lower_inverse(low) if mm is _mm_raw else _unit_lower_inverse_vjp(low)
    egc = jnp.exp(gc)
    u = mm(inv, va * beta, "nn", THREE_PASS)
    w = mm(inv, kb * egc, "nn", THREE_PASS)
    attn = jnp.where(incl, mm(q, k, "nt", ONE_PASS) * decay, 0.0)
    g_last = jnp.sum(g, axis=1, keepdims=True)
    return u, w, q * egc, k * jnp.exp(g_last - gc), attn, jnp.exp(g_last)


def _gdn_scan(mm, u, w, qd, kd, attn, egl, z, gn, state):
    v_new = u - mm(w, state, "nn", ONE_PASS)
    o = mm(qd, state, "nn", ONE_PASS) + mm(attn, v_new, "nn", ONE_PASS)
    new_state = state * egl + mm(kd, v_new, "tn", ONE_PASS)
    return _rms_gate(o, gn, z), new_state


def _chunks(ref_value, n, c):
    return ref_value.reshape(n, c, ref_value.shape[-1])


def _by_head(ref, rows, heads):
    return jnp.stack([ref[rows, pl.ds(h * HEAD_DIM, HEAD_DIM)] for h in range(heads)])


def _store_heads(ref, rows, value):
    for h in range(value.shape[0]):
        ref[rows, pl.ds(h * HEAD_DIM, HEAD_DIM)] = value[h]


def _gdn_prep_specs(tb, nt_unused=None):
    col = lambda off: pl.BlockSpec((tb, HEAD_DIM), lambda h, i: (i, off + h))
    rep = pl.BlockSpec((1, tb, LANES), lambda h, i: (h, i, 0))
    par = pl.BlockSpec((1, SUBLANES, LANES), lambda h, i: (h, 0, 0))
    att = pl.BlockSpec((1, tb, GDN_CHUNK), lambda h, i: (h, i, 0))
    egl = pl.BlockSpec((1, tb // GDN_CHUNK, SUBLANES, LANES), lambda h, i: (h, i, 0, 0))
    return col, rep, par, att, egl


def _gdn_prep_fwd(qkv_act, braw, araw, alog, dtb):
    t = qkv_act.shape[0]
    tb = min(t, GDN_PREP_ROWS)
    nt, nc = t // tb, tb // GDN_CHUNK
    width = GDN_HEADS * HEAD_DIM

    def body(q_ref, k_ref, v_ref, br_ref, ar_ref, al_ref, dt_ref, u_ref, w_ref, qd_ref, kd_ref, at_ref, eg_ref):
        ch = lambda r: _chunks(r, nc, GDN_CHUNK)
        u, w, qd, kd, attn, egl = _gdn_prep(_mm_raw, ch(q_ref[...]), ch(k_ref[...]), ch(v_ref[...]), ch(br_ref[0]),
                                            ch(ar_ref[0]), al_ref[0, 0:1, :], dt_ref[0, 0:1, :])
        u_ref[...] = u.reshape(tb, HEAD_DIM)
        w_ref[...] = w.reshape(tb, HEAD_DIM).astype(BF16)
        qd_ref[...] = qd.reshape(tb, HEAD_DIM).astype(BF16)
        kd_ref[...] = kd.reshape(tb, HEAD_DIM).astype(BF16)
        at_ref[0] = attn.reshape(tb, GDN_CHUNK).astype(BF16)
        eg_ref[0] = jnp.broadcast_to(egl, (nc, SUBLANES, LANES))

    col, rep, par, att, egl = _gdn_prep_specs(tb)
    h = GDN_HEADS
    return pl.pallas_call(
        body, name="gdn_prep_fwd", grid=(h, nt),
        in_specs=[col(0), col(h), col(2 * h), rep, rep, par, par],
        out_specs=[col(0), col(0), col(0), col(0), att, egl],
        out_shape=[jax.ShapeDtypeStruct((t, width), F32)] + [jax.ShapeDtypeStruct((t, width), BF16)] * 3
        + [jax.ShapeDtypeStruct((h, t, GDN_CHUNK), BF16), jax.ShapeDtypeStruct((h, t // GDN_CHUNK, SUBLANES, LANES), F32)],
        compiler_params=_params("parallel", "parallel"))(qkv_act, qkv_act, qkv_act, braw, araw, alog, dtb)


def _gdn_prep_bwd(qkv_act, braw, araw, alog, dtb, du, dw, dqd, dkd, dattn, degl):
    t = qkv_act.shape[0]
    tb = min(t, GDN_PREP_ROWS)
    nt, nc = t // tb, tb // GDN_CHUNK
    width = GDN_HEADS * HEAD_DIM

    def body(q_ref, k_ref, v_ref, br_ref, ar_ref, al_ref, dt_ref, du_ref, dw_ref, dqd_ref, dkd_ref, dat_ref, deg_ref,
             dq_ref, dk_ref, dv_ref, dbr_ref, dar_ref, dal_ref, ddt_ref):
        @pl.when(pl.program_id(1) == 0)
        def _():
            dal_ref[...] = jnp.zeros_like(dal_ref)
            ddt_ref[...] = jnp.zeros_like(ddt_ref)

        ch = lambda r: _chunks(r, nc, GDN_CHUNK)
        _, vjp = jax.vjp(functools.partial(_gdn_prep, _mm_vjp), ch(q_ref[...]), ch(k_ref[...]), ch(v_ref[...]),
                         ch(br_ref[0]), ch(ar_ref[0]), al_ref[0, 0:1, :], dt_ref[0, 0:1, :])
        dq, dk, dv, dbr, dar, dal, ddt = vjp((ch(du_ref[...]), ch(dw_ref[...]), ch(dqd_ref[...]), ch(dkd_ref[...]),
                                              ch(dat_ref[0]), deg_ref[0][:, 0:1, :]))
        dq_ref[...] = dq.reshape(tb, HEAD_DIM)
        dk_ref[...] = dk.reshape(tb, HEAD_DIM)
        dv_ref[...] = dv.reshape(tb, HEAD_DIM)
        dbr_ref[0] = _lane_total(dbr.reshape(tb, LANES))
        dar_ref[0] = _lane_total(dar.reshape(tb, LANES))
        dal_ref[0, 0:1, :] += _lane_total(dal)
        ddt_ref[0, 0:1, :] += _lane_total(ddt)

    col, rep, par, att, egl = _gdn_prep_specs(tb)
    h = GDN_HEADS
    return pl.pallas_call(
        body, name="gdn_prep_bwd", grid=(h, nt),
        in_specs=[col(0), col(h), col(2 * h), rep, rep, par, par, col(0), col(0), col(0), col(0), att, egl],
        out_specs=[col(0), col(0), col(0), rep, rep, par, par],
        out_shape=[jax.ShapeDtypeStruct((t, width), F32)] * 3 + [jax.ShapeDtypeStruct((h, t, LANES), F32)] * 2
        + [jax.ShapeDtypeStruct((h, SUBLANES, LANES), F32)] * 2,
        compiler_params=_params("parallel", "arbitrary"))(qkv_act, qkv_act, qkv_act, braw, araw, alog, dtb,
                                                         du, dw, dqd, dkd, dattn, degl)


def _scan_specs(tb, heads, chunk, rev, nt):
    ti = (lambda i: nt - 1 - i) if rev else (lambda i: i)
    row = pl.BlockSpec((tb, heads * HEAD_DIM), lambda i: (ti(i), 0))
    att = pl.BlockSpec((heads, tb, chunk), lambda i: (0, ti(i), 0))
    egl = pl.BlockSpec((heads, tb // chunk, SUBLANES, LANES), lambda i: (0, ti(i), 0, 0))
    hist = pl.BlockSpec((heads, tb // chunk, HEAD_DIM, HEAD_DIM), lambda i: (0, ti(i), 0, 0))
    gn = pl.BlockSpec((SUBLANES, LANES), lambda i: (0, 0))
    return row, att, egl, hist, gn


def _gdn_scan_fwd(u, w, qd, kd, attn, egl, zb, gn):
    t = u.shape[0]
    tb = min(t, 256)
    nt, nc = t // tb, tb // GDN_CHUNK
    nh = GDN_HEADS

    def body(u_ref, w_ref, qd_ref, kd_ref, at_ref, eg_ref, z_ref, gn_ref, y_ref, hist_ref, s_ref):
        @pl.when(pl.program_id(0) == 0)
        def _():
            s_ref[...] = jnp.zeros_like(s_ref)

        g = gn_ref[0:1, :]
        state = s_ref[...]
        for c in range(nc):
            rows = pl.ds(c * GDN_CHUNK, GDN_CHUNK)
            heads = lambda r: _by_head(r, rows, nh)
            hist_ref[:, c] = state
            y, state = _gdn_scan(_mm_raw, heads(u_ref), heads(w_ref), heads(qd_ref), heads(kd_ref), at_ref[:, rows, :],
                                 eg_ref[:, c, 0:1, :], heads(z_ref), g, state)
            _store_heads(y_ref, rows, y.astype(BF16))
        s_ref[...] = state

    row, att, egs, hist, gns = _scan_specs(tb, nh, GDN_CHUNK, False, nt)
    return pl.pallas_call(
        body, name="gdn_scan_fwd", grid=(nt,), in_specs=[row, row, row, row, att, egs, row, gns], out_specs=[row, hist],
        out_shape=[jax.ShapeDtypeStruct((t, nh * HEAD_DIM), BF16),
                   jax.ShapeDtypeStruct((nh, t // GDN_CHUNK, HEAD_DIM, HEAD_DIM), F32)],
        scratch_shapes=[pltpu.VMEM((nh, HEAD_DIM, HEAD_DIM), F32)],
        compiler_params=_params("arbitrary"))(u, w, qd, kd, attn, egl, zb, gn)


def _gdn_scan_bwd(u, w, qd, kd, attn, egl, zb, gn, hist, dy):
    t = u.shape[0]
    tb = min(t, 256)
    nt, nc = t // tb, tb // GDN_CHUNK
    nh = GDN_HEADS

    def body(u_ref, w_ref, qd_ref, kd_ref, at_ref, eg_ref, z_ref, gn_ref, hist_ref, dy_ref,
             du_ref, dw_ref, dqd_ref, dkd_ref, dat_ref, deg_ref, dz_ref, dgn_ref, ds_ref):
        @pl.when(pl.program_id(0) == 0)
        def _():
            ds_ref[...] = jnp.zeros_like(ds_ref)
            dgn_ref[...] = jnp.zeros_like(dgn_ref)

        g = gn_ref[0:1, :]
        d_state = ds_ref[...]
        for c in reversed(range(nc)):
            rows = pl.ds(c * GDN_CHUNK, GDN_CHUNK)
            heads = lambda r: _by_head(r, rows, nh).astype(F32)
            _, vjp = jax.vjp(functools.partial(_gdn_scan, _mm_vjp), heads(u_ref), heads(w_ref), heads(qd_ref),
                             heads(kd_ref), at_ref[:, rows, :].astype(F32), eg_ref[:, c, 0:1, :], heads(z_ref), g,
                             hist_ref[:, c])
            du, dw, dqd, dkd, dat, deg, dz, dgn, d_state = vjp((heads(dy_ref), d_state))
            _store_heads(du_ref, rows, du)
            _store_heads(dw_ref, rows, dw)
            _store_heads(dqd_ref, rows, dqd)
            _store_heads(dkd_ref, rows, dkd)
            dat_ref[:, rows, :] = dat
            deg_ref[:, c] = jnp.broadcast_to(deg, (nh, SUBLANES, LANES))
            _store_heads(dz_ref, rows, dz.astype(BF16))
            dgn_ref[0:1, :] += dgn
        ds_ref[...] = d_state

    row, att, egs, hists, gns = _scan_specs(tb, nh, GDN_CHUNK, True, nt)
    wide = jax.ShapeDtypeStruct((t, nh * HEAD_DIM), F32)
    return pl.pallas_call(
        body, name="gdn_scan_bwd", grid=(nt,),
        in_specs=[row, row, row, row, att, egs, row, gns, hists, row],
        out_specs=[row, row, row, row, att, egs, row, gns],
        out_shape=[wide] * 4 + [jax.ShapeDtypeStruct((nh, t, GDN_CHUNK), F32),
                                jax.ShapeDtypeStruct((nh, t // GDN_CHUNK, SUBLANES, LANES), F32),
                                jax.ShapeDtypeStruct((t, nh * HEAD_DIM), BF16),
                                jax.ShapeDtypeStruct((SUBLANES, LANES), F32)],
        scratch_shapes=[pltpu.VMEM((nh, HEAD_DIM, HEAD_DIM), F32)],
        compiler_params=_params("arbitrary"))(u, w, qd, kd, attn, egl, zb, gn, hist, dy)


def _hgrn_prep(mm, qr, fr, lbl):
    n, c, _ = qr.shape
    lb = jax.nn.sigmoid(lbl[1:2, :] - lbl[0:1, :])
    f = lb + (1.0 - lb) * jax.nn.sigmoid(fr)
    q = jax.nn.silu(qr)
    k = 1.0 - f
    logf = jnp.log(f)
    ri = lax.broadcasted_iota(jnp.int32, (n, c, c), 1)
    ci = lax.broadcasted_iota(jnp.int32, (n, c, c), 2)
    b = mm((ri >= ci).astype(F32), logf, "nn", EXACT_LHS)
    attn = _hgrn_attn(mm, q, k, b)
    b_last = jnp.sum(logf, axis=1, keepdims=True)
    return q * jnp.exp(b), k * jnp.exp(b_last - b), attn, jnp.exp(b_last)


HGRN_SUB = 8


@functools.partial(jax.custom_vjp, nondiff_argnums=(1,))
def _roll_rows(x, shift):
    return pltpu.roll(x, shift, x.ndim - 2)


def _roll_rows_fwd(x, shift):
    return _roll_rows(x, shift), None


def _roll_rows_bwd(shift, _, d):
    return (pltpu.roll(d, d.shape[-2] - shift, d.ndim - 2),)


_roll_rows.defvjp(_roll_rows_fwd, _roll_rows_bwd)


def _hgrn_attn(mm, q, k, b):
    n, c, d = q.shape
    sb = HGRN_SUB
    row = lax.broadcasted_iota(jnp.int32, (n, c, c), 1)
    col = lax.broadcasted_iota(jnp.int32, (n, c, c), 2)
    attn = None
    for delta in range(sb):
        if delta == 0:
            prod = q * k
        else:
            prod = q * _roll_rows(k, delta) * jnp.exp(jnp.minimum(b - _roll_rows(b, delta), 0.0))
        term = jnp.where(row - col == delta, jnp.sum(prod, axis=-1, keepdims=True), 0.0)
        attn = term if attn is None else attn + term
    far = [jnp.zeros((n, sb, c), F32)]
    row8 = lax.broadcasted_iota(jnp.int32, (n, sb, c), 1)
    col8 = lax.broadcasted_iota(jnp.int32, (n, sb, c), 2)
    for i in range(1, c // sb):
        r0 = i * sb
        bi = b[:, r0:r0 + sb, :]
        ref = bi[:, 0:1, :]
        part = mm(q[:, r0:r0 + sb, :] * jnp.exp(bi - ref), k * jnp.exp(jnp.minimum(ref - b, 0.0)), "nt", ONE_PASS)
        far.append(jnp.where(row8 + r0 - col8 >= sb, part, 0.0))
    return attn + jnp.concatenate(far, axis=1)


def _hgrn_scan(mm, qe, kd, attn, ebl, iv, z, gn, state):
    o = mm(qe, state, "nt", ONE_PASS) + mm(attn, iv, "nn", ONE_PASS)
    new_state = state * ebl + mm(iv, kd, "tn", ONE_PASS)
    return _rms_gate(o, gn, z), new_state


def _hgrn_prep_specs(tb):
    col = pl.BlockSpec((tb, HEAD_DIM), lambda h, i: (i, h))
    lbs = pl.BlockSpec((2, HEAD_DIM), lambda h, i: (0, h))
    att = pl.BlockSpec((1, tb, HGRN_CHUNK), lambda h, i: (h, i, 0))
    ebl = pl.BlockSpec((1, tb // HGRN_CHUNK, SUBLANES, LANES), lambda h, i: (h, i, 0, 0))
    return col, lbs, att, ebl


def _hgrn_prep_fwd(qr, fr, lower_bounds):
    t = qr.shape[0]
    tb = min(t, 256)
    nt, nc = t // tb, tb // HGRN_CHUNK
    hh = HGRN_HEADS

    def body(q_ref, f_ref, lb_ref, qe_ref, kd_ref, at_ref, eb_ref):
        ch = lambda r: _chunks(r, nc, HGRN_CHUNK)
        qe, kd, attn, ebl = _hgrn_prep(_mm_raw, ch(q_ref[...]), ch(f_ref[...]), lb_ref[...])
        qe_ref[...] = qe.reshape(tb, HEAD_DIM).astype(BF16)
        kd_ref[...] = kd.reshape(tb, HEAD_DIM).astype(BF16)
        at_ref[0] = attn.reshape(tb, HGRN_CHUNK).astype(BF16)
        eb_ref[0] = jnp.broadcast_to(ebl, (nc, SUBLANES, LANES))

    col, lbs, att, ebs = _hgrn_prep_specs(tb)
    return pl.pallas_call(
        body, name="hgrn_prep_fwd", grid=(hh, nt), in_specs=[col, col, lbs], out_specs=[col, col, att, ebs],
        out_shape=[jax.ShapeDtypeStruct((t, HGRN_WIDTH), BF16)] * 2
        + [jax.ShapeDtypeStruct((hh, t, HGRN_CHUNK), BF16), jax.ShapeDtypeStruct((hh, t // HGRN_CHUNK, SUBLANES, LANES), F32)],
        compiler_params=_params("parallel", "parallel"))(qr, fr, lower_bounds)


def _hgrn_prep_bwd(qr, fr, lower_bounds, dqe, dkd, dattn, debl):
    t = qr.shape[0]
    tb = min(t, 256)
    nt, nc = t // tb, tb // HGRN_CHUNK
    hh = HGRN_HEADS

    def body(q_ref, f_ref, lb_ref, dqe_ref, dkd_ref, dat_ref, deb_ref, dq_ref, df_ref, dlb_ref):
        @pl.when(pl.program_id(1) == 0)
        def _():
            dlb_ref[...] = jnp.zeros_like(dlb_ref)

        ch = lambda r: _chunks(r, nc, HGRN_CHUNK)
        _, vjp = jax.vjp(functools.partial(_hgrn_prep, _mm_vjp), ch(q_ref[...]), ch(f_ref[...]), lb_ref[...])
        dq, df, dlb = vjp((ch(dqe_ref[...]), ch(dkd_ref[...]), ch(dat_ref[0]), deb_ref[0][:, 0:1, :]))
        dq_ref[...] = dq.reshape(tb, HEAD_DIM).astype(BF16)
        df_ref[...] = df.reshape(tb, HEAD_DIM).astype(BF16)
        dlb_ref[...] += dlb

    col, lbs, att, ebs = _hgrn_prep_specs(tb)
    return pl.pallas_call(
        body, name="hgrn_prep_bwd", grid=(hh, nt), in_specs=[col, col, lbs, col, col, att, ebs],
        out_specs=[col, col, lbs],
        out_shape=[jax.ShapeDtypeStruct((t, HGRN_WIDTH), BF16)] * 2 + [jax.ShapeDtypeStruct((2, HGRN_WIDTH), F32)],
        compiler_params=_params("parallel", "arbitrary"))(qr, fr, lower_bounds, dqe, dkd, dattn, debl)


def _hgrn_scan_fwd(qe, kd, attn, ebl, iv, z, gn):
    t = qe.shape[0]
    tb = min(t, 128)
    nt, nc = t // tb, tb // HGRN_CHUNK
    hh = HGRN_HEADS

    def body(qe_ref, kd_ref, at_ref, eb_ref, i_ref, z_ref, gn_ref, y_ref, hist_ref, s_ref):
        @pl.when(pl.program_id(0) == 0)
        def _():
            s_ref[...] = jnp.zeros_like(s_ref)

        g = gn_ref[0:1, :]
        state = s_ref[...]
        for c in range(nc):
            rows = pl.ds(c * HGRN_CHUNK, HGRN_CHUNK)
            heads = lambda r: _by_head(r, rows, hh)
            hist_ref[:, c] = state
            y, state = _hgrn_scan(_mm_raw, heads(qe_ref), heads(kd_ref), at_ref[:, rows, :], eb_ref[:, c, 0:1, :],
                                  heads(i_ref), heads(z_ref), g, state)
            _store_heads(y_ref, rows, y.astype(BF16))
        s_ref[...] = state

    row, att, ebs, hist, gns = _scan_specs(tb, hh, HGRN_CHUNK, False, nt)
    return pl.pallas_call(
        body, name="hgrn_scan_fwd", grid=(nt,), in_specs=[row, row, att, ebs, row, row, gns], out_specs=[row, hist],
        out_shape=[jax.ShapeDtypeStruct((t, HGRN_WIDTH), BF16),
                   jax.ShapeDtypeStruct((hh, t // HGRN_CHUNK, HEAD_DIM, HEAD_DIM), F32)],
        scratch_shapes=[pltpu.VMEM((hh, HEAD_DIM, HEAD_DIM), F32)],
        compiler_params=_params("arbitrary"))(qe, kd, attn, ebl, iv, z, gn)


def _hgrn_scan_bwd(qe, kd, attn, ebl, iv, z, gn, hist, dy):
    t = qe.shape[0]
    tb = min(t, 128)
    nt, nc = t // tb, tb // HGRN_CHUNK
    hh = HGRN_HEADS

    def body(qe_ref, kd_ref, at_ref, eb_ref, i_ref, z_ref, gn_ref, hist_ref, dy_ref,
             dqe_ref, dkd_ref, dat_ref, deb_ref, di_ref, dz_ref, dgn_ref, ds_ref):
        @pl.when(pl.program_id(0) == 0)
        def _():
            ds_ref[...] = jnp.zeros_like(ds_ref)
            dgn_ref[...] = jnp.zeros_like(dgn_ref)

        g = gn_ref[0:1, :]
        d_state = ds_ref[...]
        for c in reversed(range(nc)):
            rows = pl.ds(c * HGRN_CHUNK, HGRN_CHUNK)
            heads = lambda r: _by_head(r, rows, hh).astype(F32)
            _, vjp = jax.vjp(functools.partial(_hgrn_scan, _mm_vjp), heads(qe_ref), heads(kd_ref),
                             at_ref[:, rows, :].astype(F32), eb_ref[:, c, 0:1, :], heads(i_ref), heads(z_ref), g,
                             hist_ref[:, c])
            dqe, dkd, dat, deb, di, dz, dgn, d_state = vjp((heads(dy_ref), d_state))
            _store_heads(dqe_ref, rows, dqe)
            _store_heads(dkd_ref, rows, dkd)
            dat_ref[:, rows, :] = dat
            deb_ref[:, c] = jnp.broadcast_to(deb, (hh, SUBLANES, LANES))
            _store_heads(di_ref, rows, di.astype(BF16))
            _store_heads(dz_ref, rows, dz.astype(BF16))
            dgn_ref[0:1, :] += dgn
        ds_ref[...] = d_state

    row, att, ebs, hists, gns = _scan_specs(tb, hh, HGRN_CHUNK, True, nt)
    wide = lambda dt: jax.ShapeDtypeStruct((t, HGRN_WIDTH), dt)
    return pl.pallas_call(
        body, name="hgrn_scan_bwd", grid=(nt,),
        in_specs=[row, row, att, ebs, row, row, gns, hists, row],
        out_specs=[row, row, att, ebs, row, row, gns],
        out_shape=[wide(F32), wide(F32), jax.ShapeDtypeStruct((hh, t, HGRN_CHUNK), F32),
                   jax.ShapeDtypeStruct((hh, t // HGRN_CHUNK, SUBLANES, LANES), F32), wide(BF16), wide(BF16),
                   jax.ShapeDtypeStruct((SUBLANES, LANES), F32)],
        scratch_shapes=[pltpu.VMEM((hh, HEAD_DIM, HEAD_DIM), F32)],
        compiler_params=_params("arbitrary"))(qe, kd, attn, ebl, iv, z, gn, hist, dy)


def _layer_norm(pre, g, b):
    mu = jnp.mean(pre, axis=-1, keepdims=True)
    d = pre - mu
    var = jnp.mean(d * d, axis=-1, keepdims=True)
    return d * lax.rsqrt(var + NORM_EPS) * g + b


def _lnpl_fwd(xin, s, p, wg, wpl, ln_g, ln_b):
    t = xin.shape[0]
    tt = min(t, 256)

    def body(x_ref, s_ref, p_ref, wg_ref, wpl_ref, g_ref, b_ref, o_ref, ob_ref):
        xn = _layer_norm(DEEPNORM_ALPHA * x_ref[...] + s_ref[...], g_ref[...], b_ref[...])
        gate = jax.nn.sigmoid(_mm_raw(xn, wg_ref[...], "nn", False))
        out = xn + _mm_raw(p_ref[...], wpl_ref[...], "nn", False) * gate
        o_ref[...] = out
        ob_ref[...] = out.astype(BF16)

    row = lambda w: pl.BlockSpec((tt, w), lambda i: (i, 0))
    full = lambda a: pl.BlockSpec(a.shape, lambda i: (0, 0))
    return pl.pallas_call(
        body, name="lnpl_fwd", grid=(t // tt,),
        in_specs=[row(D_MODEL), row(D_MODEL), row(PL_DIM), full(wg), full(wpl), full(ln_g), full(ln_b)],
        out_specs=[row(D_MODEL), row(D_MODEL)],
        out_shape=[jax.ShapeDtypeStruct((t, D_MODEL), F32), jax.ShapeDtypeStruct((t, D_MODEL), BF16)],
        compiler_params=_params("parallel"))(xin, s, p, wg, wpl, ln_g, ln_b)


def _lnpl_bwd(xin, s, p, wg, wpl, ln_g, ln_b, upstream, last, name):
    t = xin.shape[0]
    tt = min(t, 256)

    def body(x_ref, s_ref, p_ref, wg_ref, wpl_ref, g_ref, b_ref, up_ref,
             dpre_ref, dwg_ref, dwpl_ref, dg_ref, db_ref, loss_ref):
        @pl.when(pl.program_id(0) == 0)
        def _():
            for r in (dwg_ref, dwpl_ref, dg_ref, db_ref, loss_ref):
                r[...] = jnp.zeros_like(r)

        pre = DEEPNORM_ALPHA * x_ref[...] + s_ref[...]
        xn, ln_vjp = jax.vjp(_layer_norm, pre, g_ref[...], b_ref[...])
        gate = jax.nn.sigmoid(_mm_raw(xn, wg_ref[...], "nn", False))
        plv = _mm_raw(p_ref[...], wpl_ref[...], "nn", False)
        if last:
            err = xn + plv * gate - up_ref[...]
            dout = err * (1.0 / D_MODEL)
            tot = jnp.sum(jnp.sum(err * err, axis=1, keepdims=True), axis=0, keepdims=True) * (0.5 / D_MODEL)
            loss_ref[...] += jnp.broadcast_to(tot, loss_ref.shape)
        else:
            dout = up_ref[...]
        dplv = dout * gate
        dlogits = dout * plv * gate * (1.0 - gate)
        dwg_ref[...] += _mm_raw(xn, dlogits, "tn", False)
        dwpl_ref[...] += _mm_raw(p_ref[...], dplv, "tn", False)
        dxn = dout + _mm_raw(dlogits, wg_ref[...], "nt", False)
        dpre, dg, db = ln_vjp(dxn)
        dpre_ref[...] = dpre
        dg_ref[...] += dg
        db_ref[...] += db

    row = lambda w: pl.BlockSpec((tt, w), lambda i: (i, 0))
    full = lambda shape: pl.BlockSpec(shape, lambda i: (0, 0))
    return pl.pallas_call(
        body, name=name, grid=(t // tt,),
        in_specs=[row(D_MODEL), row(D_MODEL), row(PL_DIM), full(wg.shape), full(wpl.shape), full(ln_g.shape),
                  full(ln_b.shape), row(D_MODEL)],
        out_specs=[row(D_MODEL), full(wg.shape), full(wpl.shape), full(ln_g.shape), full(ln_b.shape),
                   full((SUBLANES, LANES))],
        out_shape=[jax.ShapeDtypeStruct((t, D_MODEL), F32), jax.ShapeDtypeStruct(wg.shape, F32),
                   jax.ShapeDtypeStruct(wpl.shape, F32), jax.ShapeDtypeStruct(ln_g.shape, F32),
                   jax.ShapeDtypeStruct(ln_b.shape, F32), jax.ShapeDtypeStruct((SUBLANES, LANES), F32)],
        compiler_params=_params("arbitrary"))(xin, s, p, wg, wpl, ln_g, ln_b, upstream)


def _rep_rows(v):
    return jnp.broadcast_to(v.reshape(1, LANES), (SUBLANES, LANES))


def _rep_heads(v):
    return jnp.broadcast_to(v.reshape(-1, 1, 1), (v.shape[0], SUBLANES, LANES))


def _local_step(x, p, target, w, late_weights):
    a = DEEPNORM_ALPHA
    nh = GDN_HEADS
    xb = x.astype(BF16)
    wie = w["w_in_even"]
    w_a, w_qkv, w_zb = wie[:, :4096], wie[:, 4096:7168], wie[:, 7168:8192]
    w_tail = jnp.pad(wie[:, 8192:], ((0, 0), (0, LANES - 2 * nh)))
    conv_a_w, conv_b_w = w["conv_a_w"], w["conv_b_w"]
    ln_g0, ln_b0, ln_g1, ln_b1 = (v.reshape(1, D_MODEL) for v in (w["ln_g"][0], w["ln_b"][0], w["ln_g"][1], w["ln_b"][1]))
    alog, dtb = _rep_heads(w["a_log"].reshape(nh)), _rep_heads(w["dt_bias"].reshape(nh))
    gdn_g, hgrn_g = _rep_rows(w["gdn_norm_g"]), _rep_rows(w["hgrn_norm_g"])

    proj_a = _matmul(xb, w_a, name="fwd_proj_a")
    proj_qkv = _matmul(xb, w_qkv, name="fwd_proj_qkv")
    proj_zb = _matmul(xb, w_zb, name="fwd_proj_zb")
    proj_tail = _matmul(xb, w_tail, name="fwd_proj_tail")
    rep = lambda cols: jnp.broadcast_to(cols.T[:, :, None], (nh, cols.shape[0], LANES))
    braw, araw = rep(proj_tail[:, :nh]), rep(proj_tail[:, nh:2 * nh])
    y_a = _conv_a_fwd(proj_a, conv_a_w)
    qkv_act = _conv_b_fwd(proj_qkv, conv_b_w)
    gdn_pre = _gdn_prep_fwd(qkv_act, braw, araw, alog, dtb)
    y_b, gdn_hist = _gdn_scan_fwd(*gdn_pre, proj_zb, gdn_g)
    w = {**w, **late_weights(y_b)}
    woe, wio, woo = w["w_out_even"], w["w_in_odd"], w["w_out_odd"]
    s0 = _matmul(y_b, woe[1024:], name="fwd_out_even_b", add=_matmul(y_a, woe[:1024], name="fwd_out_even_a"))
    x1, x1b = _lnpl_fwd(x, s0, p[0], w["w_pl_gate"][0], w["w_pl"][0], ln_g0, ln_b0)
    proj_o = [_matmul(x1b, wio[j], name=f"fwd_proj_odd{j}") for j in range(4)]
    hgrn_pre = _hgrn_prep_fwd(proj_o[0], proj_o[1], w["lower_bounds"])
    y_o, hgrn_hist = _hgrn_scan_fwd(*hgrn_pre, proj_o[2], proj_o[3], hgrn_g)
    s1 = _matmul(y_o, woo, name="fwd_out_odd")

    g = {}
    dpre1, dwg1, dwpl1, dlng1, dlnb1, loss = _lnpl_bwd(x1, s1, p[1], w["w_pl_gate"][1], w["w_pl"][1], ln_g1, ln_b1,
                                                     target, True, "lnpl_bwd_odd")
    dy_o = _matmul(dpre1, woo, tb=True, name="bwd_out_odd_dx")
    g["w_out_odd"] = _matmul(y_o, dpre1, ta=True, name="bwd_out_odd_dw")
    dqe, dkd, dat, deb, di, dz, dhg = _hgrn_scan_bwd(*hgrn_pre, proj_o[2], proj_o[3], hgrn_g, hgrn_hist, dy_o)
    dq, df, dlb = _hgrn_prep_bwd(proj_o[0], proj_o[1], w["lower_bounds"], dqe, dkd, dat, deb)
    dx1 = dpre1
    scale = a
    dws = []
    for j, dj in enumerate((dq, df, di, dz)):
        dx1 = _matmul(dj, wio[j], tb=True, add=dx1, add_scale=scale, name=f"bwd_proj_odd_dx{j}")
        scale = 1.0
        dws.append(_matmul(x1b, dj, ta=True, name=f"bwd_proj_odd_dw{j}"))
    g["w_in_odd"] = jnp.stack(dws)
    g["hgrn_norm_g"] = dhg[0:1]
    g["lower_bounds"] = dlb

    dpre0, dwg0, dwpl0, dlng0, dlnb0, _ = _lnpl_bwd(x, s0, p[0], w["w_pl_gate"][0], w["w_pl"][0], ln_g0, ln_b0,
                                                  dx1, False, "lnpl_bwd_even")
    g["w_pl_gate"] = jnp.stack([dwg0, dwg1])
    g["w_pl"] = jnp.stack([dwpl0, dwpl1])
    g["ln_g"] = jnp.concatenate([dlng0, dlng1], axis=0)
    g["ln_b"] = jnp.concatenate([dlnb0, dlnb1], axis=0)
    dy_a = _matmul(dpre0, woe[:1024], tb=True, name="bwd_out_even_dxa")
    dy_b = _matmul(dpre0, woe[1024:], tb=True, name="bwd_out_even_dxb")
    g["w_out_even"] = jnp.concatenate([_matmul(y_a, dpre0, ta=True, name="bwd_out_even_dwa"),
                                       _matmul(y_b, dpre0, ta=True, name="bwd_out_even_dwb")], axis=0)
    du, dw, dqd, dkd, dat, deg, dzb, dgn = _gdn_scan_bwd(*gdn_pre, proj_zb, gdn_g, gdn_hist, dy_b)
    dqa, dka, dva, dbr, dar, dal, ddt = _gdn_prep_bwd(qkv_act, braw, araw, alog, dtb, du, dw, dqd, dkd, dat, deg)
    g["a_log"] = dal[:, 0, 0].reshape(1, nh)
    g["dt_bias"] = ddt[:, 0, 0].reshape(1, nh)
    g["gdn_norm_g"] = dgn[0:1]
    d_pre_qkv, dwb = [], []
    for j, dj in enumerate((dqa, dka, dva)):
        dpj, dwj = _conv_b_bwd(proj_qkv, conv_b_w, dj, j, f"conv_b_bwd{j}")
        d_pre_qkv.append(dpj)
        dwb.append(dwj[:4])
    g["conv_b_w"] = jnp.concatenate(dwb, axis=1)
    d_a, dwa = _conv_a_bwd(proj_a, conv_a_w, dy_a)
    g["conv_a_w"] = dwa[:3]
    d_tail = jnp.concatenate([dbr[:, :, 0].T, dar[:, :, 0].T, jnp.zeros((x.shape[0], LANES - 2 * nh), F32)],
                             axis=1).astype(BF16)
    pieces = [(d_a, w_a), (d_pre_qkv[0], w_qkv[:, :1024]), (d_pre_qkv[1], w_qkv[:, 1024:2048]),
              (d_pre_qkv[2], w_qkv[:, 2048:]), (dzb, w_zb), (d_tail, w_tail)]
    dx = dpre0
    scale = a
    dws = []
    for j, (dj, wj) in enumerate(pieces):
        dx = _matmul(dj, wj, tb=True, add=dx, add_scale=scale, name=f"bwd_proj_even_dx{j}")
        scale = 1.0
        dws.append(_matmul(xb, dj, ta=True, name=f"bwd_proj_even_dw{j}"))
    dws[-1] = dws[-1][:, :2 * nh]
    g["w_in_even"] = jnp.concatenate(dws, axis=1)
    return loss, dx, g


def _adamw(w, g, m, v, name):
    rows, cols = w.shape
    tr = rows if rows <= 256 else 256
    assert rows % tr == 0, (name, rows)

    def body(w_ref, g_ref, m_ref, v_ref, d_ref, nm_ref, nv_ref):
        gg = g_ref[...]
        nm = ADAM_B1 * m_ref[...] + (1.0 - ADAM_B1) * gg
        nv = ADAM_B2 * v_ref[...] + (1.0 - ADAM_B2) * jnp.square(gg)
        m_hat = nm / (1.0 - ADAM_B1 ** ADAM_STEP)
        v_hat = nv / (1.0 - ADAM_B2 ** ADAM_STEP)
        d_ref[...] = -ADAM_LR * (m_hat / (jnp.sqrt(v_hat) + ADAM_EPS) + ADAM_WD * w_ref[...])
        nm_ref[...] = nm
        nv_ref[...] = nv

    spec = pl.BlockSpec((tr, cols), lambda i: (i, 0))
    return pl.pallas_call(
        body, name=name, grid=(rows // tr,), in_specs=[spec] * 4, out_specs=[spec] * 3,
        out_shape=[jax.ShapeDtypeStruct(w.shape, F32)] * 3, compiler_params=_params("parallel"))(w, g, m, v)


MESH = pl.DeviceIdType.MESH
N_DEV = 8
HBM_SPEC = pl.BlockSpec(memory_space=pltpu.HBM)
VMEM_SPEC = pl.BlockSpec(memory_space=pltpu.VMEM)


def _coords():
    return lax.axis_index("x"), lax.axis_index("y"), lax.axis_index("c")


def _flip(v, bit):
    return 1 - v if bit else v


def _remote(src, dst, send_sem, recv_sem, dev):
    return pltpu.make_async_remote_copy(src_ref=src, dst_ref=dst, send_sem=send_sem, recv_sem=recv_sem,
                                        device_id=dev, device_id_type=MESH)


def _exchange_small(buf, reduce, name):
    rows = buf.shape[0]

    def body(in_ref, out_ref, slots, send_sems, recv_sems):
        x, y, c = _coords()
        me = 4 * x + 2 * y + c
        slots[me] = in_ref[...]
        peer = lambda k: (_flip(x, (k >> 2) & 1), _flip(y, (k >> 1) & 1), _flip(c, k & 1))
        sends = []
        for k in range(1, N_DEV):
            cp = _remote(in_ref, slots.at[me], send_sems.at[k - 1], recv_sems.at[k - 1], peer(k))
            cp.start()
            sends.append(cp)
        for k in range(1, N_DEV):
            px, py, pc = peer(k)
            _remote(in_ref, slots.at[4 * px + 2 * py + pc], send_sems.at[k - 1], recv_sems.at[k - 1], peer(k)).wait_recv()
        for cp in sends:
            cp.wait_send()
        if reduce:
            acc = slots[0]
            for d in range(1, N_DEV):
                acc = acc + slots[d]
            out_ref[...] = acc
        else:
            out_ref[...] = slots[...]

    out_shape = (rows, LANES) if reduce else (N_DEV, rows, LANES)
    return pl.pallas_call(
        body, name=name, in_specs=[VMEM_SPEC], out_specs=VMEM_SPEC, out_shape=jax.ShapeDtypeStruct(out_shape, F32),
        scratch_shapes=[pltpu.VMEM((N_DEV, rows, LANES), F32), pltpu.SemaphoreType.DMA((N_DEV - 1,)),
                        pltpu.SemaphoreType.DMA((N_DEV - 1,))])(buf)


def _half_rows(half, which):
    return pl.ds(pl.multiple_of(which * half, 16), half)


def _other_chip(x, y, k):
    return _flip(x, (k >> 1) & 1), _flip(y, k & 1)


def _gather_shards(shards):
    n = len(shards)

    def body(*refs):
        ins, outs = refs[:n], refs[n:2 * n]
        ici_s, ici_r, d2d_s, d2d_r = refs[2 * n:]
        x, y, c = _coords()
        chip = 2 * x + y
        sends = []
        for i in range(n):
            half = ins[i].shape[0] // 2
            for k in (1, 2, 3):
                ox, oy = _other_chip(x, y, k)
                cp = _remote(ins[i].at[_half_rows(half, c)], outs[i].at[chip, _half_rows(half, c)],
                             ici_s.at[3 * i + k - 1], ici_r.at[3 * i + k - 1], (ox, oy, c))
                cp.start()
                sends.append(cp)
        for k in (1, 2, 3):
            ox, oy = _other_chip(x, y, k)
            for i in range(n):
                half = ins[i].shape[0] // 2
                piece = outs[i].at[2 * ox + oy, _half_rows(half, c)]
                _remote(piece, piece, ici_s.at[3 * i + k - 1], ici_r.at[3 * i + k - 1], (ox, oy, c)).wait_recv()
                cp = _remote(piece, piece, d2d_s.at[3 * i + k - 1], d2d_r.at[3 * i + k - 1], (x, y, 1 - c))
                cp.start()
                sends.append(cp)
        for k in (1, 2, 3):
            ox, oy = _other_chip(x, y, k)
            for i in range(n):
                half = ins[i].shape[0] // 2
                piece = outs[i].at[2 * ox + oy, _half_rows(half, 1 - c)]
                _remote(piece, piece, d2d_s.at[3 * i + k - 1], d2d_r.at[3 * i + k - 1], (x, y, 1 - c)).wait_recv()
        for cp in sends:
            cp.wait_send()

    return pl.pallas_call(
        body, name="gather_weights", in_specs=[HBM_SPEC] * n, out_specs=[HBM_SPEC] * n,
        out_shape=[jax.ShapeDtypeStruct((4,) + s.shape, s.dtype) for s in shards],
        scratch_shapes=[pltpu.SemaphoreType.DMA((3 * n,))] * 4)(*shards)


SEM_SPEC = pl.BlockSpec(memory_space=pltpu.SEMAPHORE)
DATAFLOW = pltpu.SideEffectType.DATAFLOW_SIDE_EFFECTING


def _ici_piece(srcs, lands, send_sems, recv_sems, i, k, x, y, c):
    half = srcs[i].shape[0] // 2
    ox, oy = _other_chip(x, y, k)
    return _remote(srcs[i].at[_half_rows(half, c)], lands[i].at[2 * x + y, _half_rows(half, c)],
                   send_sems.at[3 * i + k - 1], recv_sems.at[3 * i + k - 1], (ox, oy, c)), (ox, oy)


def _gather_start(shards):
    n = len(shards)

    def body(*refs):
        srcs, lands = refs[:n], refs[n:2 * n]
        send_sems, recv_sems = refs[2 * n], refs[2 * n + 1]
        token = refs[-1]
        x, y, c = _coords()
        for i in range(n):
            for k in (1, 2, 3):
                _ici_piece(srcs, lands, send_sems, recv_sems, i, k, x, y, c)[0].start()
        token[...] = jnp.zeros_like(token)

    hbm = lambda a: pltpu.with_memory_space_constraint(a, pltpu.HBM)
    lands = [lax.empty((4,) + s.shape, s.dtype) for s in shards]
    out = pl.pallas_call(
        body, name="gather_rest_start",
        out_shape=(pltpu.SemaphoreType.DMA((3 * n,)), pltpu.SemaphoreType.DMA((3 * n,)),
                   *[pltpu.HBM(s.shape, s.dtype) for s in shards], *[pltpu.HBM(a.shape, a.dtype) for a in lands],
                   jax.ShapeDtypeStruct((SUBLANES, LANES), F32)),
        in_specs=[HBM_SPEC] * (2 * n), out_specs=(SEM_SPEC, SEM_SPEC, *[HBM_SPEC] * (2 * n), VMEM_SPEC),
        input_output_aliases={i: 2 + i for i in range(2 * n)},
        compiler_params=pltpu.CompilerParams(has_side_effects=DATAFLOW))(*[hbm(s) for s in shards], *[hbm(a) for a in lands])
    return out[0], out[1], out[2:2 + n], out[2 + n:2 + 2 * n], out[-1]


def _gather_wait(send_sems, recv_sems, srcs, lands, after):
    n = len(srcs)

    def body(*refs):
        src_refs, land_refs = refs[:n], refs[n:2 * n]
        send_sems, recv_sems = refs[2 * n], refs[2 * n + 1]
        x, y, c = _coords()
        for i in range(n):
            half = src_refs[i].shape[0] // 2
            for k in (1, 2, 3):
                cp, (ox, oy) = _ici_piece(src_refs, land_refs, send_sems, recv_sems, i, k, x, y, c)
                cp.wait_send()
                piece = land_refs[i].at[2 * ox + oy, _half_rows(half, c)]
                _remote(piece, piece, send_sems.at[3 * i + k - 1], recv_sems.at[3 * i + k - 1], (ox, oy, c)).wait_recv()

    out = pl.pallas_call(
        body, name="gather_rest_wait",
        out_shape=(*[pltpu.HBM(s.shape, s.dtype) for s in srcs], *[pltpu.HBM(a.shape, a.dtype) for a in lands]),
        in_specs=[HBM_SPEC] * (2 * n) + [SEM_SPEC, SEM_SPEC, pl.BlockSpec(memory_space=pl.ANY)],
        out_specs=tuple([HBM_SPEC] * (2 * n)), input_output_aliases={i: i for i in range(2 * n)},
        compiler_params=pltpu.CompilerParams(has_side_effects=DATAFLOW))(*srcs, *lands, send_sems, recv_sems, after)
    return out[n:]


def _gather_forward(lands):
    n = len(lands)

    def body(*refs):
        ins, outs = refs[:n], refs[n:2 * n]
        send_sems, recv_sems = refs[2 * n:]
        x, y, c = _coords()
        sends = []
        for i in range(n):
            half = ins[i].shape[1] // 2
            for k in (1, 2, 3):
                ox, oy = _other_chip(x, y, k)
                cp = _remote(ins[i].at[2 * ox + oy, _half_rows(half, c)], outs[i].at[2 * ox + oy, _half_rows(half, c)],
                             send_sems.at[3 * i + k - 1], recv_sems.at[3 * i + k - 1], (x, y, 1 - c))
                cp.start()
                sends.append(cp)
        for i in range(n):
            half = ins[i].shape[1] // 2
            for k in (1, 2, 3):
                ox, oy = _other_chip(x, y, k)
                piece = outs[i].at[2 * ox + oy, _half_rows(half, 1 - c)]
                _remote(piece, piece, send_sems.at[3 * i + k - 1], recv_sems.at[3 * i + k - 1], (x, y, 1 - c)).wait_recv()
        for cp in sends:
            cp.wait_send()

    return pl.pallas_call(
        body, name="gather_rest_forward", in_specs=[HBM_SPEC] * n, out_specs=[HBM_SPEC] * n,
        out_shape=[jax.ShapeDtypeStruct(a.shape, a.dtype) for a in lands],
        input_output_aliases={i: i for i in range(n)},
        scratch_shapes=[pltpu.SemaphoreType.DMA((3 * n,))] * 2)(*lands)


def _rs_sibling_swap(g4s):
    n = len(g4s)

    def body(*refs):
        ins, outs = refs[:n], refs[n:2 * n]
        send_sems, recv_sems = refs[2 * n:]
        x, y, c = _coords()
        sends = []
        for i in range(n):
            half = ins[i].shape[1] // 2
            for s in range(4):
                cp = _remote(ins[i].at[s, _half_rows(half, 1 - c)], outs[i].at[s], send_sems.at[4 * i + s],
                             recv_sems.at[4 * i + s], (x, y, 1 - c))
                cp.start()
                sends.append(cp)
        for cp in sends:
            cp.wait_recv()
        for cp in sends:
            cp.wait_send()

    return pl.pallas_call(
        body, name="rs_sibling_swap", in_specs=[HBM_SPEC] * n, out_specs=[HBM_SPEC] * n,
        out_shape=[jax.ShapeDtypeStruct((4, g.shape[1] // 2, g.shape[2]), g.dtype) for g in g4s],
        scratch_shapes=[pltpu.SemaphoreType.DMA((4 * n,))] * 2)(*g4s)


def _rs_add_sibling(g4, got, c_idx, name):
    _, rows, cols = g4.shape
    half = rows // 2
    tr = min(half, 256)
    nb = half // tr

    def body(c_ref, a_ref, b_ref, o_ref, ob_ref):
        total = a_ref[...] + b_ref[...]
        o_ref[...] = total
        ob_ref[...] = total.astype(BF16)

    blk = (1, tr, cols)
    out = pl.BlockSpec(blk, lambda s, i, c_ref: (s, i, 0))
    grid_spec = pltpu.PrefetchScalarGridSpec(
        num_scalar_prefetch=1, grid=(4, nb),
        in_specs=[pl.BlockSpec(blk, lambda s, i, c_ref: (s, c_ref[0] * nb + i, 0)), out],
        out_specs=[out, out])
    return pl.pallas_call(
        body, name=name, grid_spec=grid_spec,
        out_shape=[jax.ShapeDtypeStruct(got.shape, F32), jax.ShapeDtypeStruct(got.shape, BF16)],
        compiler_params=_params("parallel", "parallel"))(c_idx, g4, got)


def _rs_chip_scatter(p4s):
    n = len(p4s)

    def body(*refs):
        ins, outs = refs[:n], refs[n:2 * n]
        send_sems, recv_sems = refs[2 * n:]
        x, y, c = _coords()
        sends = []
        for i in range(n):
            for k in (1, 2, 3):
                ox, oy = _other_chip(x, y, k)
                cp = _remote(ins[i].at[2 * ox + oy], outs[i].at[k - 1], send_sems.at[3 * i + k - 1],
                             recv_sems.at[3 * i + k - 1], (ox, oy, c))
                cp.start()
                sends.append(cp)
        for cp in sends:
            cp.wait_recv()
        for cp in sends:
            cp.wait_send()

    return pl.pallas_call(
        body, name="rs_chip_scatter", in_specs=[HBM_SPEC] * n, out_specs=[HBM_SPEC] * n,
        out_shape=[jax.ShapeDtypeStruct((3,) + p.shape[1:], p.dtype) for p in p4s],
        scratch_shapes=[pltpu.SemaphoreType.DMA((3 * n,))] * 2)(*p4s)


def _rs_add_chips(p4, got3, idx, name):
    _, half, cols = p4.shape
    tr = min(half, 256)
    nb = half // tr

    def body(idx_ref, p_ref, a_ref, b_ref, c_ref, o_ref):
        o_ref[...] = ((p_ref[0] + a_ref[0].astype(F32)) + b_ref[0].astype(F32)) + c_ref[0].astype(F32)

    blk = (1, tr, cols)
    grid_spec = pltpu.PrefetchScalarGridSpec(
        num_scalar_prefetch=1, grid=(nb,),
        in_specs=[pl.BlockSpec(blk, lambda i, idx_ref: (idx_ref[0], i, 0))]
        + [pl.BlockSpec(blk, functools.partial(lambda k, i, idx_ref: (k, i, 0), k)) for k in range(3)],
        out_specs=pl.BlockSpec((tr, cols), lambda i, idx_ref: (idx_ref[1] * nb + i, 0)))
    return pl.pallas_call(body, name=name, grid_spec=grid_spec, out_shape=jax.ShapeDtypeStruct((2 * half, cols), F32),
                          compiler_params=_params("parallel"))(idx, p4, got3, got3, got3)


def _rs_share_halves(bufs):
    n = len(bufs)

    def body(*refs):
        ins, outs = refs[:n], refs[n:2 * n]
        send_sems, recv_sems = refs[2 * n:]
        x, y, c = _coords()
        sends = []
        for i in range(n):
            half = ins[i].shape[0] // 2
            cp = _remote(ins[i].at[_half_rows(half, c)], outs[i].at[_half_rows(half, c)], send_sems.at[i],
                         recv_sems.at[i], (x, y, 1 - c))
            cp.start()
            sends.append(cp)
        for i in range(n):
            half = ins[i].shape[0] // 2
            _remote(ins[i].at[_half_rows(half, c)], outs[i].at[_half_rows(half, 1 - c)], send_sems.at[i],
                    recv_sems.at[i], (x, y, 1 - c)).wait_recv()
        for cp in sends:
            cp.wait_send()

    return pl.pallas_call(
        body, name="rs_share_halves", in_specs=[HBM_SPEC] * n, out_specs=[HBM_SPEC] * n,
        out_shape=[jax.ShapeDtypeStruct(b.shape, b.dtype) for b in bufs],
        input_output_aliases={i: i for i in range(n)},
        scratch_shapes=[pltpu.SemaphoreType.DMA((n,))] * 2)(*bufs)


def _reduce_scatter(g4s, names):
    x, y, c = _coords()
    c_idx = jnp.stack([c]).astype(jnp.int32)
    idx = jnp.stack([2 * x + y, c]).astype(jnp.int32)
    got = _rs_sibling_swap(g4s)
    p4s = [_rs_add_sibling(g, s, c_idx, f"rs_add_sibling_{nm}") for g, s, nm in zip(g4s, got, names)]
    got3 = _rs_chip_scatter([pb for _, pb in p4s])
    bufs = [_rs_add_chips(p, t, idx, f"rs_add_chips_{nm}") for (p, _), t, nm in zip(p4s, got3, names)]
    return _rs_share_halves(bufs)


def _cols_join(s4):
    return s4.transpose(1, 0, 2).reshape(s4.shape[1], 4 * s4.shape[2])


def _cols_split(full):
    r, c4 = full.shape
    return full.reshape(r, 4, c4 // 4).transpose(1, 0, 2)


_BIG = {
    "w_in_even": ((1024, 2052), _cols_join, _cols_split),
    "w_out_even": ((512, 1024), lambda s: s.reshape(2048, 1024), lambda f: f.reshape(4, 512, 1024)),
    "w_in_odd": ((1024, 2048), lambda s: s, lambda f: f),
    "w_out_odd": ((512, 1024), lambda s: s.reshape(2048, 1024), lambda f: f.reshape(4, 512, 1024)),
    "w_pl": ((512, 256), lambda s: s.reshape(4, 2, 256, 256).transpose(1, 2, 0, 3).reshape(2, 256, 1024),
             lambda f: f.reshape(2, 256, 4, 256).transpose(2, 0, 1, 3).reshape(4, 512, 256)),
    "w_pl_gate": ((512, 1024), lambda s: s.reshape(4, 2, 256, 1024).transpose(1, 0, 2, 3).reshape(2, 1024, 1024),
                  lambda f: f.reshape(2, 4, 256, 1024).transpose(1, 0, 2, 3).reshape(4, 512, 1024)),
}


def _size(shape):
    n = 1
    for d in shape:
        n *= d
    return n


_SMALL = {"a_log": (1, 8), "dt_bias": (1, 8), "gdn_norm_g": (1, 128), "hgrn_norm_g": (1, 128),
          "lower_bounds": (2, 2048), "ln_g": (2, 1024), "ln_b": (2, 1024), "conv_a_w": (3, 1024), "conv_b_w": (4, 3072)}
_CONV_SHARD = {"conv_a_w": (3, 256), "conv_b_w": (4, 768)}


def _pack_small(parts, shapes, head_rows=0):
    rows = []
    for n, shape in shapes.items():
        v = parts[n].reshape(-1)
        rows.append(jnp.pad(v, (0, -v.shape[0] % LANES)).reshape(-1, LANES))
    buf = jnp.concatenate(rows, axis=0)
    return jnp.pad(buf, ((head_rows, -(buf.shape[0] + head_rows) % SUBLANES), (0, 0)))


def _unpack_small(buf, shapes, head_rows=0):
    out, off = {}, head_rows
    for n, shape in shapes.items():
        nrow = -(-_size(shape) // LANES)
        out[n] = buf[off:off + nrow].reshape(-1)[:_size(shape)].reshape(shape)
        off += nrow
    return out


_WEIGHTS = ["w_in_even", "conv_a_w", "conv_b_w", "a_log", "dt_bias", "gdn_norm_g", "w_out_even", "w_in_odd",
            "lower_bounds", "hgrn_norm_g", "w_out_odd", "ln_g", "ln_b", "w_pl", "w_pl_gate"]


def kernel(x, p, w_in_even, conv_a_w, conv_b_w, a_log, dt_bias, gdn_norm_g, w_out_even, w_in_odd, lower_bounds, hgrn_norm_g, w_out_odd, ln_g, ln_b, w_pl, w_pl_gate, loss_target, m_w_in_even, m_conv_a_w, m_conv_b_w, m_a_log, m_dt_bias, m_gdn_norm_g, m_w_out_even, m_w_in_odd, m_lower_bounds, m_hgrn_norm_g, m_w_out_odd, m_ln_g, m_ln_b, m_w_pl, m_w_pl_gate, v_w_in_even, v_conv_a_w, v_conv_b_w, v_a_log, v_dt_bias, v_gdn_norm_g, v_w_out_even, v_w_in_odd, v_lower_bounds, v_hgrn_norm_g, v_w_out_odd, v_ln_g, v_ln_b, v_w_pl, v_w_pl_gate):
    w = dict(zip(_WEIGHTS, (w_in_even, conv_a_w, conv_b_w, a_log, dt_bias, gdn_norm_g, w_out_even, w_in_odd,
                            lower_bounds, hgrn_norm_g, w_out_odd, ln_g, ln_b, w_pl, w_pl_gate)))
    m = dict(zip(_WEIGHTS, (m_w_in_even, m_conv_a_w, m_conv_b_w, m_a_log, m_dt_bias, m_gdn_norm_g, m_w_out_even,
                            m_w_in_odd, m_lower_bounds, m_hgrn_norm_g, m_w_out_odd, m_ln_g, m_ln_b, m_w_pl, m_w_pl_gate)))
    v = dict(zip(_WEIGHTS, (v_w_in_even, v_conv_a_w, v_conv_b_w, v_a_log, v_dt_bias, v_gdn_norm_g, v_w_out_even,
                            v_w_in_odd, v_lower_bounds, v_hgrn_norm_g, v_w_out_odd, v_ln_g, v_ln_b, v_w_pl, v_w_pl_gate)))
    chip = 2 * lax.axis_index("x") + lax.axis_index("y")

    names = list(_BIG)
    shard_shapes = {n: _BIG[n][0] for n in names}
    shards = {n: w[n].reshape(shard_shapes[n]).astype(BF16) for n in names}
    whole = lambda n, stacked: _BIG[n][1](lax.dynamic_update_slice(stacked, shards[n][None], (chip, 0, 0)))
    early, late = names[:1], names[1:]
    full = {n: whole(n, ga) for n, ga in zip(early, _gather_shards([shards[n] for n in early]))}
    send_sems, recv_sems, srcs, lands, token = _gather_start([shards[n] for n in late])

    def late_weights(after):
        landed = _gather_forward(_gather_wait(send_sems, recv_sems, srcs, lands, after))
        return {n: whole(n, ga) for n, ga in zip(late, landed)}

    conv_mine = _pack_small({n: w[n] for n in _CONV_SHARD}, _CONV_SHARD)
    conv_all = _exchange_small(conv_mine, False, "gather_conv")
    conv_by_chip = [_unpack_small(conv_all[2 * s], _CONV_SHARD) for s in range(4)]
    for n in _CONV_SHARD:
        full[n] = jnp.concatenate([conv_by_chip[s][n] for s in range(4)], axis=1)
    for n in _SMALL:
        if n not in _CONV_SHARD:
            full[n] = w[n]

    loss_part, dx, g = _local_step(x[0] + token[0, 0], p[:, 0], loss_target[0], full, late_weights)

    g_big = dict(zip(names, _reduce_scatter([_BIG[n][2](g[n]) for n in names], names)))
    small_sum = _exchange_small(jnp.concatenate([loss_part, _pack_small(g, _SMALL)], axis=0), True, "reduce_small")
    loss = small_sum[0, 0]
    g_small = _unpack_small(small_sum, _SMALL, head_rows=SUBLANES)
    for n, (rows, cols) in _CONV_SHARD.items():
        g_small[n] = lax.dynamic_slice_in_dim(g_small[n], chip * cols, cols, axis=1)

    grads, delta, new_m, new_v = {}, {}, {}, {}
    for n in _BIG:
        shape2 = (-1, shard_shapes[n][-1])
        grads[n] = g_big[n].reshape(w[n].shape)
        d_, m_, v_ = _adamw(w[n].reshape(shape2), g_big[n].reshape(shape2), m[n].reshape(shape2), v[n].reshape(shape2),
                            f"adamw_{n}")
        delta[n], new_m[n], new_v[n] = (t.reshape(w[n].shape) for t in (d_, m_, v_))
    own = {n: (_CONV_SHARD[n] if n in _CONV_SHARD else _SMALL[n]) for n in _SMALL}
    packs = [_pack_small({n: src[n] for n in _SMALL}, own) for src in (w, g_small, m, v)]
    outs = [_unpack_small(t, own) for t in _adamw(*packs, "adamw_small")]
    for n in _SMALL:
        grads[n] = g_small[n].reshape(w[n].shape)
        delta[n], new_m[n], new_v[n] = (t[n].reshape(w[n].shape) for t in outs)
    return (loss, dx[None], *[grads[n] for n in _WEIGHTS], *[delta[n] for n in _WEIGHTS],
            *[new_m[n] for n in _WEIGHTS], *[new_v[n] for n in _WEIGHTS])
```

```python
import functools

import jax
import jax.numpy as jnp
from jax import lax
from jax.experimental import pallas as pl
from jax.experimental.pallas import tpu as pltpu

F32 = jnp.float32
BF16 = jnp.bfloat16
HI = lax.Precision.HIGHEST

D_MODEL = 1024
PL_DIM = 256
GDN_HEADS = 8
HEAD_DIM = 128
GDN_CHUNK = 64
HGRN_HEADS = 16
HGRN_CHUNK = 32
HGRN_WIDTH = 2048
DEEPNORM_ALPHA = 4.0 ** 0.25
NORM_EPS = 1e-5
ADAM_LR, ADAM_B1, ADAM_B2, ADAM_EPS, ADAM_WD, ADAM_STEP = 0.001, 0.9, 0.999, 1e-08, 0.01, 10

VMEM_LIMIT = 56 * 1024 * 1024
SUBLANES = 8
LANES = 128


def _params(*sem):
    return pltpu.CompilerParams(dimension_semantics=sem, vmem_limit_bytes=VMEM_LIMIT)


ONE_PASS, THREE_PASS, FULL_F32, EXACT_LHS, EXACT_RHS = 0, 1, 2, 3, 4


def _split3(v):
    hi = v.astype(BF16)
    r1 = v - hi.astype(F32)
    mid = r1.astype(BF16)
    return hi, mid, (r1 - mid.astype(F32)).astype(BF16)


def _mm_raw(a, b, kind, prec):
    nb = a.ndim - 2
    ca = a.ndim - 1 if kind[0] == "n" else a.ndim - 2
    cb = b.ndim - 2 if kind[1] == "n" else b.ndim - 1
    dims = (((ca,), (cb,)), (tuple(range(nb)),) * 2)
    if prec == FULL_F32:
        return lax.dot_general(a, b, dims, precision=HI, preferred_element_type=F32)
    dot = lambda p, q: lax.dot_general(p, q, dims, preferred_element_type=F32)
    ah, bh = a.astype(BF16), b.astype(BF16)
    if prec == ONE_PASS:
        return dot(ah, bh)
    if prec == EXACT_LHS:
        b1, b2, b3 = _split3(b)
        return dot(ah, b1) + (dot(ah, b2) + dot(ah, b3))
    if prec == EXACT_RHS:
        a1, a2, a3 = _split3(a)
        return dot(a1, bh) + (dot(a2, bh) + dot(a3, bh))
    al = (a - ah.astype(F32)).astype(BF16)
    bl = (b - bh.astype(F32)).astype(BF16)
    return dot(ah, bh) + (dot(ah, bl) + dot(al, bh))


@functools.partial(jax.custom_vjp, nondiff_argnums=(2, 3))
def _mm_vjp(a, b, kind, hi):
    return _mm_raw(a, b, kind, hi)


def _mm_vjp_fwd(a, b, kind, hi):
    return _mm_raw(a, b, kind, hi), (a, b)


def _mm_vjp_bwd(kind, hi, res, dc):
    a, b = res
    if hi in (EXACT_LHS, EXACT_RHS):
        assert kind == "nn"
        if hi == EXACT_LHS:
            return jnp.zeros_like(a), _mm_raw(a, dc, "tn", EXACT_LHS)
        return _mm_raw(dc, b, "nt", EXACT_RHS), jnp.zeros_like(b)
    if kind == "nn":
        return _mm_raw(dc, b, "nt", hi), _mm_raw(a, dc, "tn", hi)
    if kind == "nt":
        return _mm_raw(dc, b, "nn", hi), _mm_raw(dc, a, "tn", hi)
    return _mm_raw(b, dc, "nt", hi), _mm_raw(a, dc, "nn", hi)


_mm_vjp.defvjp(_mm_vjp_fwd, _mm_vjp_bwd)


def _lane_total(v):
    return jnp.broadcast_to(jnp.sum(v, axis=-1, keepdims=True), v.shape)


def _matmul(a, b, *, name, ta=False, tb=False, out_dtype=F32, add=None, add_scale=1.0, tm=1024, tn=1024, tk=1024):
    m, k = (a.shape[1], a.shape[0]) if ta else a.shape
    n = b.shape[0] if tb else b.shape[1]
    tm, tn, tk = min(tm, m), min(tn, n), min(tk, k)
    assert m % tm == 0 and n % tn == 0 and k % tk == 0, (name, m, n, k)
    nk = k // tk
    dims = (((0 if ta else 1,), (1 if tb else 0,)), ((), ()))

    def body(*refs):
        if add is None:
            a_ref, b_ref, o_ref, acc_ref = refs
        else:
            a_ref, b_ref, add_ref, o_ref, acc_ref = refs
        kk = pl.program_id(2)

        @pl.when(kk == 0)
        def _():
            acc_ref[...] = jnp.zeros_like(acc_ref)

        acc_ref[...] += lax.dot_general(a_ref[...].astype(BF16), b_ref[...].astype(BF16), dims,
                                        preferred_element_type=F32)

        @pl.when(kk == nk - 1)
        def _():
            r = acc_ref[...]
            if add is not None:
                r = r + add_scale * add_ref[...].astype(F32)
            o_ref[...] = r.astype(out_dtype)

    a_spec = pl.BlockSpec((tk, tm), lambda i, j, kk: (kk, i)) if ta else pl.BlockSpec((tm, tk), lambda i, j, kk: (i, kk))
    b_spec = pl.BlockSpec((tn, tk), lambda i, j, kk: (j, kk)) if tb else pl.BlockSpec((tk, tn), lambda i, j, kk: (kk, j))
    o_spec = pl.BlockSpec((tm, tn), lambda i, j, kk: (i, j))
    in_specs = [a_spec, b_spec] + ([o_spec] if add is not None else [])
    args = (a, b) + ((add,) if add is not None else ())
    return pl.pallas_call(
        body, name=name, grid=(m // tm, n // tn, nk), in_specs=in_specs, out_specs=o_spec,
        out_shape=jax.ShapeDtypeStruct((m, n), out_dtype), scratch_shapes=[pltpu.VMEM((tm, tn), F32)],
        compiler_params=_params("parallel", "parallel", "arbitrary"))(*args)


HALO = SUBLANES


def _halo_specs(tt, width, col, nt):
    r = tt // HALO
    prev = pl.BlockSpec((HALO, width), lambda i: (jnp.maximum(i * r - 1, 0), col))
    nxt = pl.BlockSpec((HALO, width), lambda i: (jnp.minimum((i + 1) * r, nt * r - 1), col))
    return prev, nxt


def _shift_down(ext, k):
    return ext if k == 0 else pltpu.roll(ext, k, 0)


def _shift_up(ext, k):
    return ext if k == 0 else pltpu.roll(ext, ext.shape[0] - k, 0)


def _causal_conv(ext, w, taps):
    acc = None
    for j in range(taps):
        term = w[j:j + 1, :] * _shift_down(ext, taps - 1 - j)
        acc = term if acc is None else acc + term
    return acc[HALO:, :]


def _conv_a_fwd(proj_a, conv_w):
    t = proj_a.shape[0]
    tt = min(t, 256)
    nt = t // tt
    wdt = 1024

    def body(cur_ref, prev_ref, w_ref, y_ref):
        i = pl.program_id(0)
        cur = cur_ref[...]
        h, c, b, z = (cur[:, k * wdt:(k + 1) * wdt] for k in range(4))
        prev = prev_ref[...]
        u_prev = jnp.where(i > 0, prev[:, wdt:2 * wdt] * prev[:, 0:wdt], 0.0)
        ext = jnp.concatenate([u_prev, c * h], axis=0)
        conv = _causal_conv(ext, w_ref[...], 3)
        y_ref[...] = (b * conv * jax.nn.silu(z)).astype(BF16)

    prev_spec, _ = _halo_specs(tt, 4 * wdt, 0, nt)
    return pl.pallas_call(
        body, name="conv_a_fwd", grid=(nt,),
        in_specs=[pl.BlockSpec((tt, 4 * wdt), lambda i: (i, 0)), prev_spec, pl.BlockSpec((3, wdt), lambda i: (0, 0))],
        out_specs=pl.BlockSpec((tt, wdt), lambda i: (i, 0)),
        out_shape=jax.ShapeDtypeStruct((t, wdt), BF16), compiler_params=_params("parallel"))(proj_a, proj_a, conv_w)


def _conv_a_bwd(proj_a, conv_w, dy):
    t = proj_a.shape[0]
    tt = min(t, 256)
    nt = t // tt
    wdt = 1024

    def body(cur_ref, prev_ref, nxt_ref, w_ref, dy_ref, dyn_ref, d_ref, dw_ref):
        i = pl.program_id(0)
        w = w_ref[...]
        cur, prev, nxt = cur_ref[...], prev_ref[...], nxt_ref[...]
        split = lambda a: tuple(a[:, k * wdt:(k + 1) * wdt] for k in range(4))
        h, c, b, z = split(cur)
        hp, cp, _, _ = split(prev)
        hn, cn, bn, zn = split(nxt)
        u_prev = jnp.where(i > 0, cp * hp, 0.0)
        u_ext = jnp.concatenate([u_prev, c * h, cn * hn], axis=0)
        taps = [_shift_down(u_ext, 2 - j)[HALO:, :] for j in range(3)]
        conv = w[0:1, :] * taps[0] + w[1:2, :] * taps[1] + w[2:3, :] * taps[2]
        b_cn = jnp.concatenate([b, bn], axis=0)
        z_cn = jnp.concatenate([z, zn], axis=0)
        dy_cn = jnp.concatenate([dy_ref[...], jnp.where(i < nt - 1, dyn_ref[...], 0.0)], axis=0)
        sg = jax.nn.sigmoid(z_cn)
        silu = z_cn * sg
        d_conv = dy_cn * b_cn * silu
        db = (dy_cn * conv * silu)[:tt, :]
        dz = (dy_cn * b_cn * conv * (sg * (1.0 + z_cn * (1.0 - sg))))[:tt, :]
        du = None
        for j in range(3):
            term = w[j:j + 1, :] * _shift_up(d_conv, 2 - j)
            du = term if du is None else du + term
        du = du[:tt, :]
        d_ref[...] = jnp.concatenate([du * c, du * h, db, dz], axis=1).astype(BF16)

        @pl.when(i == 0)
        def _():
            dw_ref[...] = jnp.zeros_like(dw_ref)

        d_cur = d_conv[:tt, :]
        rows = [jnp.sum(d_cur * taps[j][:tt, :], axis=0, keepdims=True) for j in range(3)]
        dw_ref[0:3, :] += jnp.concatenate(rows, axis=0)

    prev_spec, nxt_spec = _halo_specs(tt, 4 * wdt, 0, nt)
    _, dyn_spec = _halo_specs(tt, wdt, 0, nt)
    return pl.pallas_call(
        body, name="conv_a_bwd", grid=(nt,),
        in_specs=[pl.BlockSpec((tt, 4 * wdt), lambda i: (i, 0)), prev_spec, nxt_spec,
                  pl.BlockSpec((3, wdt), lambda i: (0, 0)), pl.BlockSpec((tt, wdt), lambda i: (i, 0)), dyn_spec],
        out_specs=[pl.BlockSpec((tt, 4 * wdt), lambda i: (i, 0)), pl.BlockSpec((SUBLANES, wdt), lambda i: (0, 0))],
        out_shape=[jax.ShapeDtypeStruct((t, 4 * wdt), BF16), jax.ShapeDtypeStruct((SUBLANES, wdt), F32)],
        compiler_params=_params("arbitrary"))(proj_a, proj_a, proj_a, conv_w, dy, dy)


def _conv_b_fwd(proj_qkv, conv_w):
    t, width = proj_qkv.shape
    tt = min(t, 256)
    nt = t // tt
    wdt = 1024

    def body(cur_ref, prev_ref, w_ref, y_ref):
        i = pl.program_id(1)
        ext = jnp.concatenate([jnp.where(i > 0, prev_ref[...], 0.0), cur_ref[...]], axis=0)
        y_ref[...] = jax.nn.silu(_causal_conv(ext, w_ref[...], 4))

    r = tt // HALO
    return pl.pallas_call(
        body, name="conv_b_fwd", grid=(width // wdt, nt),
        in_specs=[pl.BlockSpec((tt, wdt), lambda j, i: (i, j)),
                  pl.BlockSpec((HALO, wdt), lambda j, i: (jnp.maximum(i * r - 1, 0), j)),
                  pl.BlockSpec((4, wdt), lambda j, i: (0, j))],
        out_specs=pl.BlockSpec((tt, wdt), lambda j, i: (i, j)),
        out_shape=jax.ShapeDtypeStruct((t, width), F32), compiler_params=_params("parallel", "parallel"))(
            proj_qkv, proj_qkv, conv_w)


def _conv_b_bwd(proj_qkv, conv_w, d_act, col, name):
    t = proj_qkv.shape[0]
    tt = min(t, 256)
    nt = t // tt
    wdt = 1024

    def body(cur_ref, prev_ref, nxt_ref, w_ref, da_ref, dan_ref, d_ref, dw_ref):
        i = pl.program_id(0)
        w = w_ref[...]
        u_ext = jnp.concatenate([jnp.where(i > 0, prev_ref[...], 0.0), cur_ref[...], nxt_ref[...]], axis=0)
        taps = [_shift_down(u_ext, 3 - j)[HALO:, :] for j in range(4)]
        conv = w[0:1, :] * taps[0] + w[1:2, :] * taps[1] + w[2:3, :] * taps[2] + w[3:4, :] * taps[3]
        da_cn = jnp.concatenate([da_ref[...], jnp.where(i < nt - 1, dan_ref[...], 0.0)], axis=0)
        sg = jax.nn.sigmoid(conv)
        d_conv = da_cn * (sg * (1.0 + conv * (1.0 - sg)))
        du = None
        for j in range(4):
            term = w[j:j + 1, :] * _shift_up(d_conv, 3 - j)
            du = term if du is None else du + term
        d_ref[...] = du[:tt, :].astype(BF16)

        @pl.when(i == 0)
        def _():
            dw_ref[...] = jnp.zeros_like(dw_ref)

        d_cur = d_conv[:tt, :]
        rows = [jnp.sum(d_cur * taps[j][:tt, :], axis=0, keepdims=True) for j in range(4)]
        dw_ref[0:4, :] += jnp.concatenate(rows, axis=0)

    prev_spec, nxt_spec = _halo_specs(tt, wdt, col, nt)
    _, dan_spec = _halo_specs(tt, wdt, 0, nt)
    return pl.pallas_call(
        body, name=name, grid=(nt,),
        in_specs=[pl.BlockSpec((tt, wdt), lambda i: (i, col)), prev_spec, nxt_spec,
                  pl.BlockSpec((4, wdt), lambda i: (0, col)), pl.BlockSpec((tt, wdt), lambda i: (i, 0)), dan_spec],
        out_specs=[pl.BlockSpec((tt, wdt), lambda i: (i, 0)), pl.BlockSpec((SUBLANES, wdt), lambda i: (0, 0))],
        out_shape=[jax.ShapeDtypeStruct((t, wdt), BF16), jax.ShapeDtypeStruct((SUBLANES, wdt), F32)],
        compiler_params=_params("arbitrary"))(proj_qkv, proj_qkv, proj_qkv, conv_w, d_act, d_act)


def _rms_gate(o, gn, z):
    on = o * lax.rsqrt(jnp.mean(o * o, axis=-1, keepdims=True) + NORM_EPS) * gn
    return on * jax.nn.silu(z)


GDN_PREP_ROWS = 512


def _unit_lower_inverse(low):
    c = low.shape[-1]
    eye = lax.broadcasted_iota(jnp.int32, low.shape, low.ndim - 2) == lax.broadcasted_iota(jnp.int32, low.shape, low.ndim - 1)
    x = -low
    inv = eye.astype(F32) + x
    for _ in range(c.bit_length() - 2):
        x = _mm_raw(x, x, "nn", THREE_PASS)
        inv = inv + _mm_raw(inv, x, "nn", THREE_PASS)
    return inv


@jax.custom_vjp
def _unit_lower_inverse_vjp(low):
    return _unit_lower_inverse(low)


def _unit_lower_inverse_fwd(low):
    inv = _unit_lower_inverse(low)
    return inv, inv


def _unit_lower_inverse_bwd(inv, d_inv):
    return (-_mm_raw(_mm_raw(inv, d_inv, "tn", THREE_PASS), inv, "nt", THREE_PASS),)


_unit_lower_inverse_vjp.defvjp(_unit_lower_inverse_fwd, _unit_lower_inverse_bwd)


def _gdn_prep(mm, qa, ka, va, braw, araw, alog, dtb):
    n, c, _ = qa.shape
    q = qa * lax.rsqrt(jnp.sum(qa * qa, axis=-1, keepdims=True) + 1e-6) * (HEAD_DIM ** -0.5)
    k = ka * lax.rsqrt(jnp.sum(ka * ka, axis=-1, keepdims=True) + 1e-6)
    beta = jax.nn.sigmoid(braw)
    g = -jnp.exp(alog) * jax.nn.softplus(araw + dtb)
    ri = lax.broadcasted_iota(jnp.int32, (n, c, c), 1)
    ci = lax.broadcasted_iota(jnp.int32, (n, c, c), 2)
    incl, strict, eye = ri >= ci, ri > ci, ri == ci
    gc = mm(incl.astype(F32), g, "nn", EXACT_LHS)
    gc_i = gc[:, :, :c]
    gc_j = mm(jnp.ones((n, c, c), F32), jnp.where(eye, gc_i, 0.0), "nn", EXACT_LHS)
    decay = jnp.where(incl, jnp.exp(jnp.where(incl, gc_i - gc_j, 0.0)), 0.0)
    kb = k * beta
    low = jnp.where(strict, mm(kb, k, "nt", ONE_PASS) * decay, 0.0)
    inv = _unit_lower_inverse(low) if mm is _mm_raw else _unit_lower_inverse_vjp(low)
    egc = jnp.exp(gc)
    u = mm(inv, va * beta, "nn", THREE_PASS)
    w = mm(inv, kb * egc, "nn", THREE_PASS)
    attn = jnp.where(incl, mm(q, k, "nt", ONE_PASS) * decay, 0.0)
    g_last = jnp.sum(g, axis=1, keepdims=True)
    return u, w, q * egc, k * jnp.exp(g_last - gc), attn, jnp.exp(g_last)


def _gdn_scan(mm, u, w, qd, kd, attn, egl, z, gn, state):
    v_new = u - mm(w, state, "nn", ONE_PASS)
    o = mm(qd, state, "nn", ONE_PASS) + mm(attn, v_new, "nn", ONE_PASS)
    new_state = state * egl + mm(kd, v_new, "tn", ONE_PASS)
    return _rms_gate(o, gn, z), new_state


def _chunks(ref_value, n, c):
    return ref_value.reshape(n, c, ref_value.shape[-1])


def _by_head(ref, rows, heads):
    return jnp.stack([ref[rows, pl.ds(h * HEAD_DIM, HEAD_DIM)] for h in range(heads)])


def _store_heads(ref, rows, value):
    for h in range(value.shape[0]):
        ref[rows, pl.ds(h * HEAD_DIM, HEAD_DIM)] = value[h]


def _gdn_prep_specs(tb, nt_unused=None):
    col = lambda off: pl.BlockSpec((tb, HEAD_DIM), lambda h, i: (i, off + h))
    rep = pl.BlockSpec((1, tb, LANES), lambda h, i: (h, i, 0))
    par = pl.BlockSpec((1, SUBLANES, LANES), lambda h, i: (h, 0, 0))
    att = pl.BlockSpec((1, tb, GDN_CHUNK), lambda h, i: (h, i, 0))
    egl = pl.BlockSpec((1, tb // GDN_CHUNK, SUBLANES, LANES), lambda h, i: (h, i, 0, 0))
    return col, rep, par, att, egl


def _gdn_prep_fwd(qkv_act, braw, araw, alog, dtb):
    t = qkv_act.shape[0]
    tb = min(t, GDN_PREP_ROWS)
    nt, nc = t // tb, tb // GDN_CHUNK
    width = GDN_HEADS * HEAD_DIM

    def body(q_ref, k_ref, v_ref, br_ref, ar_ref, al_ref, dt_ref, u_ref, w_ref, qd_ref, kd_ref, at_ref, eg_ref):
        ch = lambda r: _chunks(r, nc, GDN_CHUNK)
        u, w, qd, kd, attn, egl = _gdn_prep(_mm_raw, ch(q_ref[...]), ch(k_ref[...]), ch(v_ref[...]), ch(br_ref[0]),
                                            ch(ar_ref[0]), al_ref[0, 0:1, :], dt_ref[0, 0:1, :])
        u_ref[...] = u.reshape(tb, HEAD_DIM)
        w_ref[...] = w.reshape(tb, HEAD_DIM).astype(BF16)
        qd_ref[...] = qd.reshape(tb, HEAD_DIM).astype(BF16)
        kd_ref[...] = kd.reshape(tb, HEAD_DIM).astype(BF16)
        at_ref[0] = attn.reshape(tb, GDN_CHUNK).astype(BF16)
        eg_ref[0] = jnp.broadcast_to(egl, (nc, SUBLANES, LANES))

    col, rep, par, att, egl = _gdn_prep_specs(tb)
    h = GDN_HEADS
    return pl.pallas_call(
        body, name="gdn_prep_fwd", grid=(h, nt),
        in_specs=[col(0), col(h), col(2 * h), rep, rep, par, par],
        out_specs=[col(0), col(0), col(0), col(0), att, egl],
        out_shape=[jax.ShapeDtypeStruct((t, width), F32)] + [jax.ShapeDtypeStruct((t, width), BF16)] * 3
        + [jax.ShapeDtypeStruct((h, t, GDN_CHUNK), BF16), jax.ShapeDtypeStruct((h, t // GDN_CHUNK, SUBLANES, LANES), F32)],
        compiler_params=_params("parallel", "parallel"))(qkv_act, qkv_act, qkv_act, braw, araw, alog, dtb)


def _gdn_prep_bwd(qkv_act, braw, araw, alog, dtb, du, dw, dqd, dkd, dattn, degl):
    t = qkv_act.shape[0]
    tb = min(t, GDN_PREP_ROWS)
    nt, nc = t // tb, tb // GDN_CHUNK
    width = GDN_HEADS * HEAD_DIM

    def body(q_ref, k_ref, v_ref, br_ref, ar_ref, al_ref, dt_ref, du_ref, dw_ref, dqd_ref, dkd_ref, dat_ref, deg_ref,
             dq_ref, dk_ref, dv_ref, dbr_ref, dar_ref, dal_ref, ddt_ref):
        @pl.when(pl.program_id(1) == 0)
        def _():
            dal_ref[...] = jnp.zeros_like(dal_ref)
            ddt_ref[...] = jnp.zeros_like(ddt_ref)

        ch = lambda r: _chunks(r, nc, GDN_CHUNK)
        _, vjp = jax.vjp(functools.partial(_gdn_prep, _mm_vjp), ch(q_ref[...]), ch(k_ref[...]), ch(v_ref[...]),
                         ch(br_ref[0]), ch(ar_ref[0]), al_ref[0, 0:1, :], dt_ref[0, 0:1, :])
        dq, dk, dv, dbr, dar, dal, ddt = vjp((ch(du_ref[...]), ch(dw_ref[...]), ch(dqd_ref[...]), ch(dkd_ref[...]),
                                              ch(dat_ref[0]), deg_ref[0][:, 0:1, :]))
        dq_ref[...] = dq.reshape(tb, HEAD_DIM)
        dk_ref[...] = dk.reshape(tb, HEAD_DIM)
        dv_ref[...] = dv.reshape(tb, HEAD_DIM)
        dbr_ref[0] = _lane_total(dbr.reshape(tb, LANES))
        dar_ref[0] = _lane_total(dar.reshape(tb, LANES))
        dal_ref[0, 0:1, :] += _lane_total(dal)
        ddt_ref[0, 0:1, :] += _lane_total(ddt)

    col, rep, par, att, egl = _gdn_prep_specs(tb)
    h = GDN_HEADS
    return pl.pallas_call(
        body, name="gdn_prep_bwd", grid=(h, nt),
        in_specs=[col(0), col(h), col(2 * h), rep, rep, par, par, col(0), col(0), col(0), col(0), att, egl],
        out_specs=[col(0), col(0), col(0), rep, rep, par, par],
        out_shape=[jax.ShapeDtypeStruct((t, width), F32)] * 3 + [jax.ShapeDtypeStruct((h, t, LANES), F32)] * 2
        + [jax.ShapeDtypeStruct((h, SUBLANES, LANES), F32)] * 2,
        compiler_params=_params("parallel", "arbitrary"))(qkv_act, qkv_act, qkv_act, braw, araw, alog, dtb,
                                                         du, dw, dqd, dkd, dattn, degl)


def _scan_specs(tb, heads, chunk, rev, nt):
    ti = (lambda i: nt - 1 - i) if rev else (lambda i: i)
    row = pl.BlockSpec((tb, heads * HEAD_DIM), lambda i: (ti(i), 0))
    att = pl.BlockSpec((heads, tb, chunk), lambda i: (0, ti(i), 0))
    egl = pl.BlockSpec((heads, tb // chunk, SUBLANES, LANES), lambda i: (0, ti(i), 0, 0))
    hist = pl.BlockSpec((heads, tb // chunk, HEAD_DIM, HEAD_DIM), lambda i: (0, ti(i), 0, 0))
    gn = pl.BlockSpec((SUBLANES, LANES), lambda i: (0, 0))
    return row, att, egl, hist, gn


def _gdn_scan_fwd(u, w, qd, kd, attn, egl, zb, gn):
    t = u.shape[0]
    tb = min(t, 256)
    nt, nc = t // tb, tb // GDN_CHUNK
    nh = GDN_HEADS

    def body(u_ref, w_ref, qd_ref, kd_ref, at_ref, eg_ref, z_ref, gn_ref, y_ref, hist_ref, s_ref):
        @pl.when(pl.program_id(0) == 0)
        def _():
            s_ref[...] = jnp.zeros_like(s_ref)

        g = gn_ref[0:1, :]
        state = s_ref[...]
        for c in range(nc):
            rows = pl.ds(c * GDN_CHUNK, GDN_CHUNK)
            heads = lambda r: _by_head(r, rows, nh)
            hist_ref[:, c] = state
            y, state = _gdn_scan(_mm_raw, heads(u_ref), heads(w_ref), heads(qd_ref), heads(kd_ref), at_ref[:, rows, :],
                                 eg_ref[:, c, 0:1, :], heads(z_ref), g, state)
            _store_heads(y_ref, rows, y.astype(BF16))
        s_ref[...] = state

    row, att, egs, hist, gns = _scan_specs(tb, nh, GDN_CHUNK, False, nt)
    return pl.pallas_call(
        body, name="gdn_scan_fwd", grid=(nt,), in_specs=[row, row, row, row, att, egs, row, gns], out_specs=[row, hist],
        out_shape=[jax.ShapeDtypeStruct((t, nh * HEAD_DIM), BF16),
                   jax.ShapeDtypeStruct((nh, t // GDN_CHUNK, HEAD_DIM, HEAD_DIM), F32)],
        scratch_shapes=[pltpu.VMEM((nh, HEAD_DIM, HEAD_DIM), F32)],
        compiler_params=_params("arbitrary"))(u, w, qd, kd, attn, egl, zb, gn)


def _gdn_scan_bwd(u, w, qd, kd, attn, egl, zb, gn, hist, dy):
    t = u.shape[0]
    tb = min(t, 256)
    nt, nc = t // tb, tb // GDN_CHUNK
    nh = GDN_HEADS

    def body(u_ref, w_ref, qd_ref, kd_ref, at_ref, eg_ref, z_ref, gn_ref, hist_ref, dy_ref,
             du_ref, dw_ref, dqd_ref, dkd_ref, dat_ref, deg_ref, dz_ref, dgn_ref, ds_ref):
        @pl.when(pl.program_id(0) == 0)
        def _():
            ds_ref[...] = jnp.zeros_like(ds_ref)
            dgn_ref[...] = jnp.zeros_like(dgn_ref)

        g = gn_ref[0:1, :]
        d_state = ds_ref[...]
        for c in reversed(range(nc)):
            rows = pl.ds(c * GDN_CHUNK, GDN_CHUNK)
            heads = lambda r: _by_head(r, rows, nh).astype(F32)
            _, vjp = jax.vjp(functools.partial(_gdn_scan, _mm_vjp), heads(u_ref), heads(w_ref), heads(qd_ref),
                             heads(kd_ref), at_ref[:, rows, :].astype(F32), eg_ref[:, c, 0:1, :], heads(z_ref), g,
                             hist_ref[:, c])
            du, dw, dqd, dkd, dat, deg, dz, dgn, d_state = vjp((heads(dy_ref), d_state))
            _store_heads(du_ref, rows, du)
            _store_heads(dw_ref, rows, dw)
            _store_heads(dqd_ref, rows, dqd)
            _store_heads(dkd_ref, rows, dkd)
            dat_ref[:, rows, :] = dat
            deg_ref[:, c] = jnp.broadcast_to(deg, (nh, SUBLANES, LANES))
            _store_heads(dz_ref, rows, dz.astype(BF16))
            dgn_ref[0:1, :] += dgn
        ds_ref[...] = d_state

    row, att, egs, hists, gns = _scan_specs(tb, nh, GDN_CHUNK, True, nt)
    wide = jax.ShapeDtypeStruct((t, nh * HEAD_DIM), F32)
    return pl.pallas_call(
        body, name="gdn_scan_bwd", grid=(nt,),
        in_specs=[row, row, row, row, att, egs, row, gns, hists, row],
        out_specs=[row, row, row, row, att, egs, row, gns],
        out_shape=[wide] * 4 + [jax.ShapeDtypeStruct((nh, t, GDN_CHUNK), F32),
                                jax.ShapeDtypeStruct((nh, t // GDN_CHUNK, SUBLANES, LANES), F32),
                                jax.ShapeDtypeStruct((t, nh * HEAD_DIM), BF16),
                                jax.ShapeDtypeStruct((SUBLANES, LANES), F32)],
        scratch_shapes=[pltpu.VMEM((nh, HEAD_DIM, HEAD_DIM), F32)],
        compiler_params=_params("arbitrary"))(u, w, qd, kd, attn, egl, zb, gn, hist, dy)


def _hgrn_prep(mm, qr, fr, lbl):
    n, c, _ = qr.shape
    lb = jax.nn.sigmoid(lbl[1:2, :] - lbl[0:1, :])
    f = lb + (1.0 - lb) * jax.nn.sigmoid(fr)
    q = jax.nn.silu(qr)
    k = 1.0 - f
    logf = jnp.log(f)
    ri = lax.broadcasted_iota(jnp.int32, (n, c, c), 1)
    ci = lax.broadcasted_iota(jnp.int32, (n, c, c), 2)
    b = mm((ri >= ci).astype(F32), logf, "nn", EXACT_LHS)
    attn = _hgrn_attn(mm, q, k, b)
    b_last = jnp.sum(logf, axis=1, keepdims=True)
    return q * jnp.exp(b), k * jnp.exp(b_last - b), attn, jnp.exp(b_last)


HGRN_SUB = 8


@functools.partial(jax.custom_vjp, nondiff_argnums=(1,))
def _roll_rows(x, shift):
    return pltpu.roll(x, shift, x.ndim - 2)


def _roll_rows_fwd(x, shift):
    return _roll_rows(x, shift), None


def _roll_rows_bwd(shift, _, d):
    return (pltpu.roll(d, d.shape[-2] - shift, d.ndim - 2),)


_roll_rows.defvjp(_roll_rows_fwd, _roll_rows_bwd)


def _hgrn_attn(mm, q, k, b):
    n, c, d = q.shape
    sb = HGRN_SUB
    row = lax.broadcasted_iota(jnp.int32, (n, c, c), 1)
    col = lax.broadcasted_iota(jnp.int32, (n, c, c), 2)
    attn = None
    for delta in range(sb):
        if delta == 0:
            prod = q * k
        else:
            prod = q * _roll_rows(k, delta) * jnp.exp(jnp.minimum(b - _roll_rows(b, delta), 0.0))
        term = jnp.where(row - col == delta, jnp.sum(prod, axis=-1, keepdims=True), 0.0)
        attn = term if attn is None else attn + term
    far = [jnp.zeros((n, sb, c), F32)]
    row8 = lax.broadcasted_iota(jnp.int32, (n, sb, c), 1)
    col8 = lax.broadcasted_iota(jnp.int32, (n, sb, c), 2)
    for i in range(1, c // sb):
        r0 = i * sb
        bi = b[:, r0:r0 + sb, :]
        ref = bi[:, 0:1, :]
        part = mm(q[:, r0:r0 + sb, :] * jnp.exp(bi - ref), k * jnp.exp(jnp.minimum(ref - b, 0.0)), "nt", ONE_PASS)
        far.append(jnp.where(row8 + r0 - col8 >= sb, part, 0.0))
    return attn + jnp.concatenate(far, axis=1)


def _hgrn_scan(mm, qe, kd, attn, ebl, iv, z, gn, state):
    o = mm(qe, state, "nt", ONE_PASS) + mm(attn, iv, "nn", ONE_PASS)
    new_state = state * ebl + mm(iv, kd, "tn", ONE_PASS)
    return _rms_gate(o, gn, z), new_state


def _hgrn_prep_specs(tb):
    col = pl.BlockSpec((tb, HEAD_DIM), lambda h, i: (i, h))
    lbs = pl.BlockSpec((2, HEAD_DIM), lambda h, i: (0, h))
    att = pl.BlockSpec((1, tb, HGRN_CHUNK), lambda h, i: (h, i, 0))
    ebl = pl.BlockSpec((1, tb // HGRN_CHUNK, SUBLANES, LANES), lambda h, i: (h, i, 0, 0))
    return col, lbs, att, ebl


def _hgrn_prep_fwd(qr, fr, lower_bounds):
    t = qr.shape[0]
    tb = min(t, 256)
    nt, nc = t // tb, tb // HGRN_CHUNK
    hh = HGRN_HEADS

    def body(q_ref, f_ref, lb_ref, qe_ref, kd_ref, at_ref, eb_ref):
        ch = lambda r: _chunks(r, nc, HGRN_CHUNK)
        qe, kd, attn, ebl = _hgrn_prep(_mm_raw, ch(q_ref[...]), ch(f_ref[...]), lb_ref[...])
        qe_ref[...] = qe.reshape(tb, HEAD_DIM).astype(BF16)
        kd_ref[...] = kd.reshape(tb, HEAD_DIM).astype(BF16)
        at_ref[0] = attn.reshape(tb, HGRN_CHUNK).astype(BF16)
        eb_ref[0] = jnp.broadcast_to(ebl, (nc, SUBLANES, LANES))

    col, lbs, att, ebs = _hgrn_prep_specs(tb)
    return pl.pallas_call(
        body, name="hgrn_prep_fwd", grid=(hh, nt), in_specs=[col, col, lbs], out_specs=[col, col, att, ebs],
        out_shape=[jax.ShapeDtypeStruct((t, HGRN_WIDTH), BF16)] * 2
        + [jax.ShapeDtypeStruct((hh, t, HGRN_CHUNK), BF16), jax.ShapeDtypeStruct((hh, t // HGRN_CHUNK, SUBLANES, LANES), F32)],
        compiler_params=_params("parallel", "parallel"))(qr, fr, lower_bounds)


def _hgrn_prep_bwd(qr, fr, lower_bounds, dqe, dkd, dattn, debl):
    t = qr.shape[0]
    tb = min(t, 256)
    nt, nc = t // tb, tb // HGRN_CHUNK
    hh = HGRN_HEADS

    def body(q_ref, f_ref, lb_ref, dqe_ref, dkd_ref, dat_ref, deb_ref, dq_ref, df_ref, dlb_ref):
        @pl.when(pl.program_id(1) == 0)
        def _():
            dlb_ref[...] = jnp.zeros_like(dlb_ref)

        ch = lambda r: _chunks(r, nc, HGRN_CHUNK)
        _, vjp = jax.vjp(functools.partial(_hgrn_prep, _mm_vjp), ch(q_ref[...]), ch(f_ref[...]), lb_ref[...])
        dq, df, dlb = vjp((ch(dqe_ref[...]), ch(dkd_ref[...]), ch(dat_ref[0]), deb_ref[0][:, 0:1, :]))
        dq_ref[...] = dq.reshape(tb, HEAD_DIM).astype(BF16)
        df_ref[...] = df.reshape(tb, HEAD_DIM).astype(BF16)
        dlb_ref[...] += dlb

    col, lbs, att, ebs = _hgrn_prep_specs(tb)
    return pl.pallas_call(
        body, name="hgrn_prep_bwd", grid=(hh, nt), in_specs=[col, col, lbs, col, col, att, ebs],
        out_specs=[col, col, lbs],
        out_shape=[jax.ShapeDtypeStruct((t, HGRN_WIDTH), BF16)] * 2 + [jax.ShapeDtypeStruct((2, HGRN_WIDTH), F32)],
        compiler_params=_params("parallel", "arbitrary"))(qr, fr, lower_bounds, dqe, dkd, dattn, debl)


def _hgrn_scan_fwd(qe, kd, attn, ebl, iv, z, gn):
    t = qe.shape[0]
    tb = min(t, 128)
    nt, nc = t // tb, tb // HGRN_CHUNK
    hh = HGRN_HEADS

    def body(qe_ref, kd_ref, at_ref, eb_ref, i_ref, z_ref, gn_ref, y_ref, hist_ref, s_ref):
        @pl.when(pl.program_id(0) == 0)
        def _():
            s_ref[...] = jnp.zeros_like(s_ref)

        g = gn_ref[0:1, :]
        state = s_ref[...]
        for c in range(nc):
            rows = pl.ds(c * HGRN_CHUNK, HGRN_CHUNK)
            heads = lambda r: _by_head(r, rows, hh)
            hist_ref[:, c] = state
            y, state = _hgrn_scan(_mm_raw, heads(qe_ref), heads(kd_ref), at_ref[:, rows, :], eb_ref[:, c, 0:1, :],
                                  heads(i_ref), heads(z_ref), g, state)
            _store_heads(y_ref, rows, y.astype(BF16))
        s_ref[...] = state

    row, att, ebs, hist, gns = _scan_specs(tb, hh, HGRN_CHUNK, False, nt)
    return pl.pallas_call(
        body, name="hgrn_scan_fwd", grid=(nt,), in_specs=[row, row, att, ebs, row, row, gns], out_specs=[row, hist],
        out_shape=[jax.ShapeDtypeStruct((t, HGRN_WIDTH), BF16),
                   jax.ShapeDtypeStruct((hh, t // HGRN_CHUNK, HEAD_DIM, HEAD_DIM), F32)],
        scratch_shapes=[pltpu.VMEM((hh, HEAD_DIM, HEAD_DIM), F32)],
        compiler_params=_params("arbitrary"))(qe, kd, attn, ebl, iv, z, gn)


def _hgrn_scan_bwd(qe, kd, attn, ebl, iv, z, gn, hist, dy):
    t = qe.shape[0]
    tb = min(t, 128)
    nt, nc = t // tb, tb // HGRN_CHUNK
    hh = HGRN_HEADS

    def body(qe_ref, kd_ref, at_ref, eb_ref, i_ref, z_ref, gn_ref, hist_ref, dy_ref,
             dqe_ref, dkd_ref, dat_ref, deb_ref, di_ref, dz_ref, dgn_ref, ds_ref):
        @pl.when(pl.program_id(0) == 0)
        def _():
            ds_ref[...] = jnp.zeros_like(ds_ref)
            dgn_ref[...] = jnp.zeros_like(dgn_ref)

        g = gn_ref[0:1, :]
        d_state = ds_ref[...]
        for c in reversed(range(nc)):
            rows = pl.ds(c * HGRN_CHUNK, HGRN_CHUNK)
            heads = lambda r: _by_head(r, rows, hh).astype(F32)
            _, vjp = jax.vjp(functools.partial(_hgrn_scan, _mm_vjp), heads(qe_ref), heads(kd_ref),
                             at_ref[:, rows, :].astype(F32), eb_ref[:, c, 0:1, :], heads(i_ref), heads(z_ref), g,
                             hist_ref[:, c])
            dqe, dkd, dat, deb, di, dz, dgn, d_state = vjp((heads(dy_ref), d_state))
            _store_heads(dqe_ref, rows, dqe)
            _store_heads(dkd_ref, rows, dkd)
            dat_ref[:, rows, :] = dat
            deb_ref[:, c] = jnp.broadcast_to(deb, (hh, SUBLANES, LANES))
            _store_heads(di_ref, rows, di.astype(BF16))
            _store_heads(dz_ref, rows, dz.astype(BF16))
            dgn_ref[0:1, :] += dgn
        ds_ref[...] = d_state

    row, att, ebs, hists, gns = _scan_specs(tb, hh, HGRN_CHUNK, True, nt)
    wide = lambda dt: jax.ShapeDtypeStruct((t, HGRN_WIDTH), dt)
    return pl.pallas_call(
        body, name="hgrn_scan_bwd", grid=(nt,),
        in_specs=[row, row, att, ebs, row, row, gns, hists, row],
        out_specs=[row, row, att, ebs, row, row, gns],
        out_shape=[wide(F32), wide(F32), jax.ShapeDtypeStruct((hh, t, HGRN_CHUNK), F32),
                   jax.ShapeDtypeStruct((hh, t // HGRN_CHUNK, SUBLANES, LANES), F32), wide(BF16), wide(BF16),
                   jax.ShapeDtypeStruct((SUBLANES, LANES), F32)],
        scratch_shapes=[pltpu.VMEM((hh, HEAD_DIM, HEAD_DIM), F32)],
        compiler_params=_params("arbitrary"))(qe, kd, attn, ebl, iv, z, gn, hist, dy)


def _layer_norm(pre, g, b):
    mu = jnp.mean(pre, axis=-1, keepdims=True)
    d = pre - mu
    var = jnp.mean(d * d, axis=-1, keepdims=True)
    return d * lax.rsqrt(var + NORM_EPS) * g + b


def _lnpl_fwd(xin, s, p, wg, wpl, ln_g, ln_b):
    t = xin.shape[0]
    tt = min(t, 256)

    def body(x_ref, s_ref, p_ref, wg_ref, wpl_ref, g_ref, b_ref, o_ref, ob_ref):
        xn = _layer_norm(DEEPNORM_ALPHA * x_ref[...] + s_ref[...], g_ref[...], b_ref[...])
        gate = jax.nn.sigmoid(_mm_raw(xn, wg_ref[...], "nn", False))
        out = xn + _mm_raw(p_ref[...], wpl_ref[...], "nn", False) * gate
        o_ref[...] = out
        ob_ref[...] = out.astype(BF16)

    row = lambda w: pl.BlockSpec((tt, w), lambda i: (i, 0))
    full = lambda a: pl.BlockSpec(a.shape, lambda i: (0, 0))
    return pl.pallas_call(
        body, name="lnpl_fwd", grid=(t // tt,),
        in_specs=[row(D_MODEL), row(D_MODEL), row(PL_DIM), full(wg), full(wpl), full(ln_g), full(ln_b)],
        out_specs=[row(D_MODEL), row(D_MODEL)],
        out_shape=[jax.ShapeDtypeStruct((t, D_MODEL), F32), jax.ShapeDtypeStruct((t, D_MODEL), BF16)],
        compiler_params=_params("parallel"))(xin, s, p, wg, wpl, ln_g, ln_b)


def _lnpl_bwd(xin, s, p, wg, wpl, ln_g, ln_b, upstream, last, name):
    t = xin.shape[0]
    tt = min(t, 256)

    def body(x_ref, s_ref, p_ref, wg_ref, wpl_ref, g_ref, b_ref, up_ref,
             dpre_ref, dwg_ref, dwpl_ref, dg_ref, db_ref, loss_ref):
        @pl.when(pl.program_id(0) == 0)
        def _():
            for r in (dwg_ref, dwpl_ref, dg_ref, db_ref, loss_ref):
                r[...] = jnp.zeros_like(r)

        pre = DEEPNORM_ALPHA * x_ref[...] + s_ref[...]
        xn, ln_vjp = jax.vjp(_layer_norm, pre, g_ref[...], b_ref[...])
        gate = jax.nn.sigmoid(_mm_raw(xn, wg_ref[...], "nn", False))
        plv = _mm_raw(p_ref[...], wpl_ref[...], "nn", False)
        if last:
            err = xn + plv * gate - up_ref[...]
            dout = err * (1.0 / D_MODEL)
            tot = jnp.sum(jnp.sum(err * err, axis=1, keepdims=True), axis=0, keepdims=True) * (0.5 / D_MODEL)
            loss_ref[...] += jnp.broadcast_to(tot, loss_ref.shape)
        else:
            dout = up_ref[...]
        dplv = dout * gate
        dlogits = dout * plv * gate * (1.0 - gate)
        dwg_ref[...] += _mm_raw(xn, dlogits, "tn", False)
        dwpl_ref[...] += _mm_raw(p_ref[...], dplv, "tn", False)
        dxn = dout + _mm_raw(dlogits, wg_ref[...], "nt", False)
        dpre, dg, db = ln_vjp(dxn)
        dpre_ref[...] = dpre
        dg_ref[...] += dg
        db_ref[...] += db

    row = lambda w: pl.BlockSpec((tt, w), lambda i: (i, 0))
    full = lambda shape: pl.BlockSpec(shape, lambda i: (0, 0))
    return pl.pallas_call(
        body, name=name, grid=(t // tt,),
        in_specs=[row(D_MODEL), row(D_MODEL), row(PL_DIM), full(wg.shape), full(wpl.shape), full(ln_g.shape),
                  full(ln_b.shape), row(D_MODEL)],
        out_specs=[row(D_MODEL), full(wg.shape), full(wpl.shape), full(ln_g.shape), full(ln_b.shape),
                   full((SUBLANES, LANES))],
        out_shape=[jax.ShapeDtypeStruct((t, D_MODEL), F32), jax.ShapeDtypeStruct(wg.shape, F32),
                   jax.ShapeDtypeStruct(wpl.shape, F32), jax.ShapeDtypeStruct(ln_g.shape, F32),
                   jax.ShapeDtypeStruct(ln_b.shape, F32), jax.ShapeDtypeStruct((SUBLANES, LANES), F32)],
        compiler_params=_params("arbitrary"))(xin, s, p, wg, wpl, ln_g, ln_b, upstream)


def _pack_tail(dbr, dar):
    nh, t, _ = dbr.shape
    tt = min(t, 512)

    def body(b_ref, a_ref, o_ref):
        lane = lax.broadcasted_iota(jnp.int32, (tt, LANES), 1)
        acc = jnp.zeros((tt, LANES), F32)
        for h in range(nh):
            acc = jnp.where(lane == h, b_ref[h], acc)
            acc = jnp.where(lane == nh + h, a_ref[h], acc)
        o_ref[...] = acc.astype(BF16)

    spec = pl.BlockSpec((nh, tt, LANES), lambda i: (0, i, 0))
    return pl.pallas_call(
        body, name="pack_tail", grid=(t // tt,), in_specs=[spec, spec], out_specs=pl.BlockSpec((tt, LANES), lambda i: (i, 0)),
        out_shape=jax.ShapeDtypeStruct((t, LANES), BF16), compiler_params=_params("parallel"))(dbr, dar)


def _rep_rows(v):
    return jnp.broadcast_to(v.reshape(1, LANES), (SUBLANES, LANES))


def _rep_heads(v):
    return jnp.broadcast_to(v.reshape(-1, 1, 1), (v.shape[0], SUBLANES, LANES))


def _local_step(x, p, target, w, late_weights, odd_grads_ready):
    a = DEEPNORM_ALPHA
    nh = GDN_HEADS
    xb = x.astype(BF16)
    wie = w["w_in_even"]
    w_a, w_qkv, w_zb = wie[:, :4096], wie[:, 4096:7168], wie[:, 7168:8192]
    w_tail = jnp.pad(wie[:, 8192:], ((0, 0), (0, LANES - 2 * nh)))
    conv_a_w, conv_b_w = w["conv_a_w"], w["conv_b_w"]
    ln_g0, ln_b0, ln_g1, ln_b1 = (v.reshape(1, D_MODEL) for v in (w["ln_g"][0], w["ln_b"][0], w["ln_g"][1], w["ln_b"][1]))
    alog, dtb = _rep_heads(w["a_log"].reshape(nh)), _rep_heads(w["dt_bias"].reshape(nh))
    gdn_g, hgrn_g = _rep_rows(w["gdn_norm_g"]), _rep_rows(w["hgrn_norm_g"])

    proj_a = _matmul(xb, w_a, name="fwd_proj_a")
    proj_qkv = _matmul(xb, w_qkv, name="fwd_proj_qkv")
    proj_zb = _matmul(xb, w_zb, name="fwd_proj_zb")
    proj_tail = _matmul(xb, w_tail, name="fwd_proj_tail")
    rep = lambda cols: jnp.broadcast_to(cols.T[:, :, None], (nh, cols.shape[0], LANES))
    braw, araw = rep(proj_tail[:, :nh]), rep(proj_tail[:, nh:2 * nh])
    y_a = _conv_a_fwd(proj_a, conv_a_w)
    qkv_act = _conv_b_fwd(proj_qkv, conv_b_w)
    gdn_pre = _gdn_prep_fwd(qkv_act, braw, araw, alog, dtb)
    y_b, gdn_hist = _gdn_scan_fwd(*gdn_pre, proj_zb, gdn_g)
    w = {**w, **late_weights(y_b)}
    woe, wio, woo = w["w_out_even"], w["w_in_odd"], w["w_out_odd"]
    s0 = _matmul(y_b, woe[1024:], name="fwd_out_even_b", add=_matmul(y_a, woe[:1024], name="fwd_out_even_a"))
    x1, x1b = _lnpl_fwd(x, s0, p[0], w["w_pl_gate"][0], w["w_pl"][0], ln_g0, ln_b0)
    proj_o = [_matmul(x1b, wio[j], name=f"fwd_proj_odd{j}") for j in range(4)]
    hgrn_pre = _hgrn_prep_fwd(proj_o[0], proj_o[1], w["lower_bounds"])
    y_o, hgrn_hist = _hgrn_scan_fwd(*hgrn_pre, proj_o[2], proj_o[3], hgrn_g)
    s1 = _matmul(y_o, woo, name="fwd_out_odd")

    g = {}
    dpre1, dwg1, dwpl1, dlng1, dlnb1, loss = _lnpl_bwd(x1, s1, p[1], w["w_pl_gate"][1], w["w_pl"][1], ln_g1, ln_b1,
                                                     target, True, "lnpl_bwd_odd")
    dy_o = _matmul(dpre1, woo, tb=True, name="bwd_out_odd_dx")
    g["w_out_odd"] = _matmul(y_o, dpre1, ta=True, name="bwd_out_odd_dw")
    dqe, dkd, dat, deb, di, dz, dhg = _hgrn_scan_bwd(*hgrn_pre, proj_o[2], proj_o[3], hgrn_g, hgrn_hist, dy_o)
    dq, df, dlb = _hgrn_prep_bwd(proj_o[0], proj_o[1], w["lower_bounds"], dqe, dkd, dat, deb)
    dx1 = dpre1
    scale = a
    dws = []
    for j, dj in enumerate((dq, df, di, dz)):
        dx1 = _matmul(dj, wio[j], tb=True, add=dx1, add_scale=scale, name=f"bwd_proj_odd_dx{j}")
        scale = 1.0
        dws.append(_matmul(x1b, dj, ta=True, name=f"bwd_proj_odd_dw{j}"))
    g["w_in_odd"] = jnp.stack(dws)
    g["hgrn_norm_g"] = dhg[0:1]
    g["lower_bounds"] = dlb
    g["w_pl_gate1"], g["w_pl1"] = dwg1, dwpl1
    token = odd_grads_ready({n: g[n] for n in ("w_in_odd", "w_out_odd", "w_pl_gate1", "w_pl1")})

    dpre0, dwg0, dwpl0, dlng0, dlnb0, _ = _lnpl_bwd(x, s0, p[0], w["w_pl_gate"][0], w["w_pl"][0], ln_g0 + token,
                                                  ln_b0, dx1, False, "lnpl_bwd_even")
    g["w_pl_gate0"], g["w_pl0"] = dwg0, dwpl0
    g["ln_g"] = jnp.concatenate([dlng0, dlng1], axis=0)
    g["ln_b"] = jnp.concatenate([dlnb0, dlnb1], axis=0)
    dy_a = _matmul(dpre0, woe[:1024], tb=True, name="bwd_out_even_dxa")
    dy_b = _matmul(dpre0, woe[1024:], tb=True, name="bwd_out_even_dxb")
    g["w_out_even"] = jnp.concatenate([_matmul(y_a, dpre0, ta=True, name="bwd_out_even_dwa"),
                                       _matmul(y_b, dpre0, ta=True, name="bwd_out_even_dwb")], axis=0)
    du, dw, dqd, dkd, dat, deg, dzb, dgn = _gdn_scan_bwd(*gdn_pre, proj_zb, gdn_g, gdn_hist, dy_b)
    dqa, dka, dva, dbr, dar, dal, ddt = _gdn_prep_bwd(qkv_act, braw, araw, alog, dtb, du, dw, dqd, dkd, dat, deg)
    g["a_log"] = dal[:, 0, 0].reshape(1, nh)
    g["dt_bias"] = ddt[:, 0, 0].reshape(1, nh)
    g["gdn_norm_g"] = dgn[0:1]
    d_pre_qkv, dwb = [], []
    for j, dj in enumerate((dqa, dka, dva)):
        dpj, dwj = _conv_b_bwd(proj_qkv, conv_b_w, dj, j, f"conv_b_bwd{j}")
        d_pre_qkv.append(dpj)
        dwb.append(dwj[:4])
    g["conv_b_w"] = jnp.concatenate(dwb, axis=1)
    d_a, dwa = _conv_a_bwd(proj_a, conv_a_w, dy_a)
    g["conv_a_w"] = dwa[:3]
    d_tail = _pack_tail(dbr, dar)
    pieces = [(d_a, w_a), (d_pre_qkv[0], w_qkv[:, :1024]), (d_pre_qkv[1], w_qkv[:, 1024:2048]),
              (d_pre_qkv[2], w_qkv[:, 2048:]), (dzb, w_zb), (d_tail, w_tail)]
    dx = dpre0
    scale = a
    dws = []
    for j, (dj, wj) in enumerate(pieces):
        dx = _matmul(dj, wj, tb=True, add=dx, add_scale=scale, name=f"bwd_proj_even_dx{j}")
        scale = 1.0
        dws.append(_matmul(xb, dj, ta=True, name=f"bwd_proj_even_dw{j}"))
    dws[-1] = dws[-1][:, :2 * nh]
    g["w_in_even"] = jnp.concatenate(dws, axis=1)
    return loss, dx, g


def _adamw(w, g, m, v, name):
    rows, cols = w.shape
    tr = rows if rows <= 256 else 256
    assert rows % tr == 0, (name, rows)

    def body(w_ref, g_ref, m_ref, v_ref, d_ref, nm_ref, nv_ref):
        gg = g_ref[...]
        nm = ADAM_B1 * m_ref[...] + (1.0 - ADAM_B1) * gg
        nv = ADAM_B2 * v_ref[...] + (1.0 - ADAM_B2) * jnp.square(gg)
        m_hat = nm / (1.0 - ADAM_B1 ** ADAM_STEP)
        v_hat = nv / (1.0 - ADAM_B2 ** ADAM_STEP)
        d_ref[...] = -ADAM_LR * (m_hat / (jnp.sqrt(v_hat) + ADAM_EPS) + ADAM_WD * w_ref[...])
        nm_ref[...] = nm
        nv_ref[...] = nv

    spec = pl.BlockSpec((tr, cols), lambda i: (i, 0))
    return pl.pallas_call(
        body, name=name, grid=(rows // tr,), in_specs=[spec] * 4, out_specs=[spec] * 3,
        out_shape=[jax.ShapeDtypeStruct(w.shape, F32)] * 3, compiler_params=_params("parallel"))(w, g, m, v)


MESH = pl.DeviceIdType.MESH
N_DEV = 8
HBM_SPEC = pl.BlockSpec(memory_space=pltpu.HBM)
VMEM_SPEC = pl.BlockSpec(memory_space=pltpu.VMEM)


def _coords():
    return lax.axis_index("x"), lax.axis_index("y"), lax.axis_index("c")


def _flip(v, bit):
    return 1 - v if bit else v


def _remote(src, dst, send_sem, recv_sem, dev):
    return pltpu.make_async_remote_copy(src_ref=src, dst_ref=dst, send_sem=send_sem, recv_sem=recv_sem,
                                        device_id=dev, device_id_type=MESH)


def _exchange_small(buf, reduce, name):
    rows = buf.shape[0]

    def body(in_ref, out_ref, slots, send_sems, recv_sems):
        x, y, c = _coords()
        me = 4 * x + 2 * y + c
        slots[me] = in_ref[...]
        peer = lambda k: (_flip(x, (k >> 2) & 1), _flip(y, (k >> 1) & 1), _flip(c, k & 1))
        sends = []
        for k in range(1, N_DEV):
            cp = _remote(in_ref, slots.at[me], send_sems.at[k - 1], recv_sems.at[k - 1], peer(k))
            cp.start()
            sends.append(cp)
        for k in range(1, N_DEV):
            px, py, pc = peer(k)
            _remote(in_ref, slots.at[4 * px + 2 * py + pc], send_sems.at[k - 1], recv_sems.at[k - 1], peer(k)).wait_recv()
        for cp in sends:
            cp.wait_send()
        if reduce:
            acc = slots[0]
            for d in range(1, N_DEV):
                acc = acc + slots[d]
            out_ref[...] = acc
        else:
            out_ref[...] = slots[...]

    out_shape = (rows, LANES) if reduce else (N_DEV, rows, LANES)
    return pl.pallas_call(
        body, name=name, in_specs=[VMEM_SPEC], out_specs=VMEM_SPEC, out_shape=jax.ShapeDtypeStruct(out_shape, F32),
        scratch_shapes=[pltpu.VMEM((N_DEV, rows, LANES), F32), pltpu.SemaphoreType.DMA((N_DEV - 1,)),
                        pltpu.SemaphoreType.DMA((N_DEV - 1,))])(buf)


def _half_rows(half, which):
    return pl.ds(pl.multiple_of(which * half, 16), half)


def _other_chip(x, y, k):
    return _flip(x, (k >> 1) & 1), _flip(y, k & 1)


def _gather_shards(shards):
    n = len(shards)

    def body(*refs):
        ins, outs = refs[:n], refs[n:2 * n]
        ici_s, ici_r, d2d_s, d2d_r = refs[2 * n:]
        x, y, c = _coords()
        chip = 2 * x + y
        sends = []
        for i in range(n):
            half = ins[i].shape[0] // 2
            for k in (1, 2, 3):
                ox, oy = _other_chip(x, y, k)
                cp = _remote(ins[i].at[_half_rows(half, c)], outs[i].at[chip, _half_rows(half, c)],
                             ici_s.at[3 * i + k - 1], ici_r.at[3 * i + k - 1], (ox, oy, c))
                cp.start()
                sends.append(cp)
        for k in (1, 2, 3):
            ox, oy = _other_chip(x, y, k)
            for i in range(n):
                half = ins[i].shape[0] // 2
                piece = outs[i].at[2 * ox + oy, _half_rows(half, c)]
                _remote(piece, piece, ici_s.at[3 * i + k - 1], ici_r.at[3 * i + k - 1], (ox, oy, c)).wait_recv()
                cp = _remote(piece, piece, d2d_s.at[3 * i + k - 1], d2d_r.at[3 * i + k - 1], (x, y, 1 - c))
                cp.start()
                sends.append(cp)
        for k in (1, 2, 3):
            ox, oy = _other_chip(x, y, k)
            for i in range(n):
                half = ins[i].shape[0] // 2
                piece = outs[i].at[2 * ox + oy, _half_rows(half, 1 - c)]
                _remote(piece, piece, d2d_s.at[3 * i + k - 1], d2d_r.at[3 * i + k - 1], (x, y, 1 - c)).wait_recv()
        for cp in sends:
            cp.wait_send()

    return pl.pallas_call(
        body, name="gather_weights", in_specs=[HBM_SPEC] * n, out_specs=[HBM_SPEC] * n,
        out_shape=[jax.ShapeDtypeStruct((4,) + s.shape, s.dtype) for s in shards],
        scratch_shapes=[pltpu.SemaphoreType.DMA((3 * n,))] * 4)(*shards)


SEM_SPEC = pl.BlockSpec(memory_space=pltpu.SEMAPHORE)
DATAFLOW = pltpu.SideEffectType.DATAFLOW_SIDE_EFFECTING


def _ici_piece(srcs, lands, send_sems, recv_sems, i, k, x, y, c):
    half = srcs[i].shape[0] // 2
    ox, oy = _other_chip(x, y, k)
    return _remote(srcs[i].at[_half_rows(half, c)], lands[i].at[2 * x + y, _half_rows(half, c)],
                   send_sems.at[3 * i + k - 1], recv_sems.at[3 * i + k - 1], (ox, oy, c)), (ox, oy)


def _gather_start(shards):
    n = len(shards)

    def body(*refs):
        srcs, lands = refs[:n], refs[n:2 * n]
        send_sems, recv_sems = refs[2 * n], refs[2 * n + 1]
        token = refs[-1]
        x, y, c = _coords()
        for i in range(n):
            for k in (1, 2, 3):
                _ici_piece(srcs, lands, send_sems, recv_sems, i, k, x, y, c)[0].start()
        token[...] = jnp.zeros_like(token)

    hbm = lambda a: pltpu.with_memory_space_constraint(a, pltpu.HBM)
    lands = [lax.empty((4,) + s.shape, s.dtype) for s in shards]
    out = pl.pallas_call(
        body, name="gather_rest_start",
        out_shape=(pltpu.SemaphoreType.DMA((3 * n,)), pltpu.SemaphoreType.DMA((3 * n,)),
                   *[pltpu.HBM(s.shape, s.dtype) for s in shards], *[pltpu.HBM(a.shape, a.dtype) for a in lands],
                   jax.ShapeDtypeStruct((SUBLANES, LANES), F32)),
        in_specs=[HBM_SPEC] * (2 * n), out_specs=(SEM_SPEC, SEM_SPEC, *[HBM_SPEC] * (2 * n), VMEM_SPEC),
        input_output_aliases={i: 2 + i for i in range(2 * n)},
        compiler_params=pltpu.CompilerParams(has_side_effects=DATAFLOW))(*[hbm(s) for s in shards], *[hbm(a) for a in lands])
    return out[0], out[1], out[2:2 + n], out[2 + n:2 + 2 * n], out[-1]


def _gather_wait(send_sems, recv_sems, srcs, lands, after):
    n = len(srcs)

    def body(*refs):
        src_refs, land_refs = refs[:n], refs[n:2 * n]
        send_sems, recv_sems = refs[2 * n], refs[2 * n + 1]
        x, y, c = _coords()
        for i in range(n):
            half = src_refs[i].shape[0] // 2
            for k in (1, 2, 3):
                cp, (ox, oy) = _ici_piece(src_refs, land_refs, send_sems, recv_sems, i, k, x, y, c)
                cp.wait_send()
                piece = land_refs[i].at[2 * ox + oy, _half_rows(half, c)]
                _remote(piece, piece, send_sems.at[3 * i + k - 1], recv_sems.at[3 * i + k - 1], (ox, oy, c)).wait_recv()

    out = pl.pallas_call(
        body, name="gather_rest_wait",
        out_shape=(*[pltpu.HBM(s.shape, s.dtype) for s in srcs], *[pltpu.HBM(a.shape, a.dtype) for a in lands]),
        in_specs=[HBM_SPEC] * (2 * n) + [SEM_SPEC, SEM_SPEC, pl.BlockSpec(memory_space=pl.ANY)],
        out_specs=tuple([HBM_SPEC] * (2 * n)), input_output_aliases={i: i for i in range(2 * n)},
        compiler_params=pltpu.CompilerParams(has_side_effects=DATAFLOW))(*srcs, *lands, send_sems, recv_sems, after)
    return out[n:]


def _gather_forward(lands):
    n = len(lands)

    def body(*refs):
        ins, outs = refs[:n], refs[n:2 * n]
        send_sems, recv_sems = refs[2 * n:]
        x, y, c = _coords()
        sends = []
        for i in range(n):
            half = ins[i].shape[1] // 2
            for k in (1, 2, 3):
                ox, oy = _other_chip(x, y, k)
                cp = _remote(ins[i].at[2 * ox + oy, _half_rows(half, c)], outs[i].at[2 * ox + oy, _half_rows(half, c)],
                             send_sems.at[3 * i + k - 1], recv_sems.at[3 * i + k - 1], (x, y, 1 - c))
                cp.start()
                sends.append(cp)
        for i in range(n):
            half = ins[i].shape[1] // 2
            for k in (1, 2, 3):
                ox, oy = _other_chip(x, y, k)
                piece = outs[i].at[2 * ox + oy, _half_rows(half, 1 - c)]
                _remote(piece, piece, send_sems.at[3 * i + k - 1], recv_sems.at[3 * i + k - 1], (x, y, 1 - c)).wait_recv()
        for cp in sends:
            cp.wait_send()

    return pl.pallas_call(
        body, name="gather_rest_forward", in_specs=[HBM_SPEC] * n, out_specs=[HBM_SPEC] * n,
        out_shape=[jax.ShapeDtypeStruct(a.shape, a.dtype) for a in lands],
        input_output_aliases={i: i for i in range(n)},
        scratch_shapes=[pltpu.SemaphoreType.DMA((3 * n,))] * 2)(*lands)


def _rs_sibling_swap(g4s, name):
    n = len(g4s)

    def body(*refs):
        ins, outs = refs[:n], refs[n:2 * n]
        send_sems, recv_sems = refs[2 * n:]
        x, y, c = _coords()
        sends = []
        for i in range(n):
            half = ins[i].shape[1] // 2
            for s in range(4):
                cp = _remote(ins[i].at[s, _half_rows(half, 1 - c)], outs[i].at[s], send_sems.at[4 * i + s],
                             recv_sems.at[4 * i + s], (x, y, 1 - c))
                cp.start()
                sends.append(cp)
        for cp in sends:
            cp.wait_recv()
        for cp in sends:
            cp.wait_send()

    return pl.pallas_call(
        body, name=name, in_specs=[HBM_SPEC] * n, out_specs=[HBM_SPEC] * n,
        out_shape=[jax.ShapeDtypeStruct((4, g.shape[1] // 2, g.shape[2]), g.dtype) for g in g4s],
        scratch_shapes=[pltpu.SemaphoreType.DMA((4 * n,))] * 2)(*g4s)


def _rs_add_sibling(g4, got, c_idx, name):
    _, rows, cols = g4.shape
    half = rows // 2
    tr = min(half, 256)
    nb = half // tr

    def body(c_ref, a_ref, b_ref, o_ref, ob_ref):
        total = a_ref[...] + b_ref[...]
        o_ref[...] = total
        ob_ref[...] = total.astype(BF16)

    blk = (1, tr, cols)
    out = pl.BlockSpec(blk, lambda s, i, c_ref: (s, i, 0))
    grid_spec = pltpu.PrefetchScalarGridSpec(
        num_scalar_prefetch=1, grid=(4, nb),
        in_specs=[pl.BlockSpec(blk, lambda s, i, c_ref: (s, c_ref[0] * nb + i, 0)), out],
        out_specs=[out, out])
    return pl.pallas_call(
        body, name=name, grid_spec=grid_spec,
        out_shape=[jax.ShapeDtypeStruct(got.shape, F32), jax.ShapeDtypeStruct(got.shape, BF16)],
        compiler_params=_params("parallel", "parallel"))(c_idx, g4, got)


def _rs_chip_scatter(p4s):
    n = len(p4s)

    def body(*refs):
        ins, outs = refs[:n], refs[n:2 * n]
        send_sems, recv_sems = refs[2 * n:]
        x, y, c = _coords()
        sends = []
        for i in range(n):
            for k in (1, 2, 3):
                ox, oy = _other_chip(x, y, k)
                cp = _remote(ins[i].at[2 * ox + oy], outs[i].at[k - 1], send_sems.at[3 * i + k - 1],
                             recv_sems.at[3 * i + k - 1], (ox, oy, c))
                cp.start()
                sends.append(cp)
        for cp in sends:
            cp.wait_recv()
        for cp in sends:
            cp.wait_send()

    return pl.pallas_call(
        body, name="rs_chip_scatter", in_specs=[HBM_SPEC] * n, out_specs=[HBM_SPEC] * n,
        out_shape=[jax.ShapeDtypeStruct((3,) + p.shape[1:], p.dtype) for p in p4s],
        scratch_shapes=[pltpu.SemaphoreType.DMA((3 * n,))] * 2)(*p4s)


def _rs_add_chips(p4, got3, idx, name):
    _, half, cols = p4.shape
    tr = min(half, 256)
    nb = half // tr

    def body(idx_ref, p_ref, a_ref, b_ref, c_ref, o_ref):
        o_ref[...] = ((p_ref[0] + a_ref[0].astype(F32)) + b_ref[0].astype(F32)) + c_ref[0].astype(F32)

    blk = (1, tr, cols)
    grid_spec = pltpu.PrefetchScalarGridSpec(
        num_scalar_prefetch=1, grid=(nb,),
        in_specs=[pl.BlockSpec(blk, lambda i, idx_ref: (idx_ref[0], i, 0))]
        + [pl.BlockSpec(blk, functools.partial(lambda k, i, idx_ref: (k, i, 0), k)) for k in range(3)],
        out_specs=pl.BlockSpec((tr, cols), lambda i, idx_ref: (idx_ref[1] * nb + i, 0)))
    return pl.pallas_call(body, name=name, grid_spec=grid_spec, out_shape=jax.ShapeDtypeStruct((2 * half, cols), F32),
                          compiler_params=_params("parallel"))(idx, p4, got3, got3, got3)


def _rs_share_halves(bufs):
    n = len(bufs)

    def body(*refs):
        ins, outs = refs[:n], refs[n:2 * n]
        send_sems, recv_sems = refs[2 * n:]
        x, y, c = _coords()
        sends = []
        for i in range(n):
            half = ins[i].shape[0] // 2
            cp = _remote(ins[i].at[_half_rows(half, c)], outs[i].at[_half_rows(half, c)], send_sems.at[i],
                         recv_sems.at[i], (x, y, 1 - c))
            cp.start()
            sends.append(cp)
        for i in range(n):
            half = ins[i].shape[0] // 2
            _remote(ins[i].at[_half_rows(half, c)], outs[i].at[_half_rows(half, 1 - c)], send_sems.at[i],
                    recv_sems.at[i], (x, y, 1 - c)).wait_recv()
        for cp in sends:
            cp.wait_send()

    return pl.pallas_call(
        body, name="rs_share_halves", in_specs=[HBM_SPEC] * n, out_specs=[HBM_SPEC] * n,
        out_shape=[jax.ShapeDtypeStruct(b.shape, b.dtype) for b in bufs],
        input_output_aliases={i: i for i in range(n)},
        scratch_shapes=[pltpu.SemaphoreType.DMA((n,))] * 2)(*bufs)


def _scatter_piece(srcs, lands, send_sems, recv_sems, i, k, x, y, c):
    ox, oy = _other_chip(x, y, k)
    return _remote(srcs[i].at[2 * ox + oy], lands[i].at[k - 1], send_sems.at[3 * i + k - 1],
                   recv_sems.at[3 * i + k - 1], (ox, oy, c))


def _rs_scatter_start(p4s):
    n = len(p4s)

    def body(*refs):
        srcs, lands = refs[:n], refs[n:2 * n]
        send_sems, recv_sems = refs[2 * n], refs[2 * n + 1]
        token = refs[-1]
        x, y, c = _coords()
        for i in range(n):
            for k in (1, 2, 3):
                _scatter_piece(srcs, lands, send_sems, recv_sems, i, k, x, y, c).start()
        token[...] = jnp.zeros_like(token)

    hbm = lambda a: pltpu.with_memory_space_constraint(a, pltpu.HBM)
    lands = [lax.empty((3,) + p.shape[1:], p.dtype) for p in p4s]
    out = pl.pallas_call(
        body, name="rs_scatter_start",
        out_shape=(pltpu.SemaphoreType.DMA((3 * n,)), pltpu.SemaphoreType.DMA((3 * n,)),
                   *[pltpu.HBM(p.shape, p.dtype) for p in p4s], *[pltpu.HBM(a.shape, a.dtype) for a in lands],
                   jax.ShapeDtypeStruct((SUBLANES, LANES), F32)),
        in_specs=[HBM_SPEC] * (2 * n), out_specs=(SEM_SPEC, SEM_SPEC, *[HBM_SPEC] * (2 * n), VMEM_SPEC),
        input_output_aliases={i: 2 + i for i in range(2 * n)},
        compiler_params=pltpu.CompilerParams(has_side_effects=DATAFLOW))(*[hbm(p) for p in p4s], *[hbm(a) for a in lands])
    return out[0], out[1], out[2:2 + n], out[2 + n:2 + 2 * n], out[-1]


def _rs_scatter_wait(send_sems, recv_sems, srcs, lands, after):
    n = len(srcs)

    def body(*refs):
        src_refs, land_refs = refs[:n], refs[n:2 * n]
        send_sems, recv_sems = refs[2 * n], refs[2 * n + 1]
        x, y, c = _coords()
        for i in range(n):
            for k in (1, 2, 3):
                cp = _scatter_piece(src_refs, land_refs, send_sems, recv_sems, i, k, x, y, c)
                cp.wait_send()
                cp.wait_recv()

    out = pl.pallas_call(
        body, name="rs_scatter_wait",
        out_shape=(*[pltpu.HBM(s.shape, s.dtype) for s in srcs], *[pltpu.HBM(a.shape, a.dtype) for a in lands]),
        in_specs=[HBM_SPEC] * (2 * n) + [SEM_SPEC, SEM_SPEC, pl.BlockSpec(memory_space=pl.ANY)],
        out_specs=tuple([HBM_SPEC] * (2 * n)), input_output_aliases={i: i for i in range(2 * n)},
        compiler_params=pltpu.CompilerParams(has_side_effects=DATAFLOW))(*srcs, *lands, send_sems, recv_sems, after)
    return out[n:]


def _rs_front(g4s, names, tag):
    c_idx = jnp.stack([lax.axis_index("c")]).astype(jnp.int32)
    got = _rs_sibling_swap(g4s, f"rs_sibling_swap_{tag}")
    return [_rs_add_sibling(g, s, c_idx, f"rs_add_sibling_{nm}") for g, s, nm in zip(g4s, got, names)]


def _rs_back(p4s, got3, names):
    x, y, c = _coords()
    idx = jnp.stack([2 * x + y, c]).astype(jnp.int32)
    return [_rs_add_chips(p, t, idx, f"rs_add_chips_{nm}") for (p, _), t, nm in zip(p4s, got3, names)]


def _cols_join(s4):
    return s4.transpose(1, 0, 2).reshape(s4.shape[1], 4 * s4.shape[2])


def _cols_split(full):
    r, c4 = full.shape
    return full.reshape(r, 4, c4 // 4).transpose(1, 0, 2)


_BIG = {
    "w_in_even": ((1024, 2052), _cols_join, _cols_split),
    "w_out_even": ((512, 1024), lambda s: s.reshape(2048, 1024), lambda f: f.reshape(4, 512, 1024)),
    "w_in_odd": ((1024, 2048), lambda s: s, lambda f: f),
    "w_out_odd": ((512, 1024), lambda s: s.reshape(2048, 1024), lambda f: f.reshape(4, 512, 1024)),
    "w_pl": ((512, 256), lambda s: s.reshape(4, 2, 256, 256).transpose(1, 2, 0, 3).reshape(2, 256, 1024),
             lambda f: f.reshape(2, 256, 4, 256).transpose(2, 0, 1, 3).reshape(4, 512, 256)),
    "w_pl_gate": ((512, 1024), lambda s: s.reshape(4, 2, 256, 1024).transpose(1, 0, 2, 3).reshape(2, 1024, 1024),
                  lambda f: f.reshape(2, 4, 256, 1024).transpose(1, 0, 2, 3).reshape(4, 512, 1024)),
}


_RS_ODD = {
    "w_in_odd": lambda f: f,
    "w_out_odd": lambda f: f.reshape(4, 512, 1024),
    "w_pl_gate1": lambda f: f.reshape(4, 256, 1024),
    "w_pl1": _cols_split,
}
_RS_EVEN = {
    "w_in_even": _cols_split,
    "w_out_even": lambda f: f.reshape(4, 512, 1024),
    "w_pl_gate0": lambda f: f.reshape(4, 256, 1024),
    "w_pl0": _cols_split,
}


def _size(shape):
    n = 1
    for d in shape:
        n *= d
    return n


_SMALL = {"a_log": (1, 8), "dt_bias": (1, 8), "gdn_norm_g": (1, 128), "hgrn_norm_g": (1, 128),
          "lower_bounds": (2, 2048), "ln_g": (2, 1024), "ln_b": (2, 1024), "conv_a_w": (3, 1024), "conv_b_w": (4, 3072)}
_CONV_SHARD = {"conv_a_w": (3, 256), "conv_b_w": (4, 768)}


def _pack_small(parts, shapes, head_rows=0):
    rows = []
    for n, shape in shapes.items():
        v = parts[n].reshape(-1)
        rows.append(jnp.pad(v, (0, -v.shape[0] % LANES)).reshape(-1, LANES))
    buf = jnp.concatenate(rows, axis=0)
    return jnp.pad(buf, ((head_rows, -(buf.shape[0] + head_rows) % SUBLANES), (0, 0)))


def _unpack_small(buf, shapes, head_rows=0):
    out, off = {}, head_rows
    for n, shape in shapes.items():
        nrow = -(-_size(shape) // LANES)
        out[n] = buf[off:off + nrow].reshape(-1)[:_size(shape)].reshape(shape)
        off += nrow
    return out


_WEIGHTS = ["w_in_even", "conv_a_w", "conv_b_w", "a_log", "dt_bias", "gdn_norm_g", "w_out_even", "w_in_odd",
            "lower_bounds", "hgrn_norm_g", "w_out_odd", "ln_g", "ln_b", "w_pl", "w_pl_gate"]


def kernel(x, p, w_in_even, conv_a_w, conv_b_w, a_log, dt_bias, gdn_norm_g, w_out_even, w_in_odd, lower_bounds, hgrn_norm_g, w_out_odd, ln_g, ln_b, w_pl, w_pl_gate, loss_target, m_w_in_even, m_conv_a_w, m_conv_b_w, m_a_log, m_dt_bias, m_gdn_norm_g, m_w_out_even, m_w_in_odd, m_lower_bounds, m_hgrn_norm_g, m_w_out_odd, m_ln_g, m_ln_b, m_w_pl, m_w_pl_gate, v_w_in_even, v_conv_a_w, v_conv_b_w, v_a_log, v_dt_bias, v_gdn_norm_g, v_w_out_even, v_w_in_odd, v_lower_bounds, v_hgrn_norm_g, v_w_out_odd, v_ln_g, v_ln_b, v_w_pl, v_w_pl_gate):
    w = dict(zip(_WEIGHTS, (w_in_even, conv_a_w, conv_b_w, a_log, dt_bias, gdn_norm_g, w_out_even, w_in_odd,
                            lower_bounds, hgrn_norm_g, w_out_odd, ln_g, ln_b, w_pl, w_pl_gate)))
    m = dict(zip(_WEIGHTS, (m_w_in_even, m_conv_a_w, m_conv_b_w, m_a_log, m_dt_bias, m_gdn_norm_g, m_w_out_even,
                            m_w_in_odd, m_lower_bounds, m_hgrn_norm_g, m_w_out_odd, m_ln_g, m_ln_b, m_w_pl, m_w_pl_gate)))
    v = dict(zip(_WEIGHTS, (v_w_in_even, v_conv_a_w, v_conv_b_w, v_a_log, v_dt_bias, v_gdn_norm_g, v_w_out_even,
                            v_w_in_odd, v_lower_bounds, v_hgrn_norm_g, v_w_out_odd, v_ln_g, v_ln_b, v_w_pl, v_w_pl_gate)))
    chip = 2 * lax.axis_index("x") + lax.axis_index("y")

    names = list(_BIG)
    shard_shapes = {n: _BIG[n][0] for n in names}
    shards = {n: w[n].reshape(shard_shapes[n]).astype(BF16) for n in names}
    whole = lambda n, stacked: _BIG[n][1](lax.dynamic_update_slice(stacked, shards[n][None], (chip, 0, 0)))
    early, late = names[:1], names[1:]
    full = {n: whole(n, ga) for n, ga in zip(early, _gather_shards([shards[n] for n in early]))}
    send_sems, recv_sems, srcs, lands, token = _gather_start([shards[n] for n in late])

    def late_weights(after):
        landed = _gather_forward(_gather_wait(send_sems, recv_sems, srcs, lands, after))
        return {n: whole(n, ga) for n, ga in zip(late, landed)}

    conv_mine = _pack_small({n: w[n] for n in _CONV_SHARD}, _CONV_SHARD)
    conv_all = _exchange_small(conv_mine, False, "gather_conv")
    conv_by_chip = [_unpack_small(conv_all[2 * s], _CONV_SHARD) for s in range(4)]
    for n in _CONV_SHARD:
        full[n] = jnp.concatenate([conv_by_chip[s][n] for s in range(4)], axis=1)
    for n in _SMALL:
        if n not in _CONV_SHARD:
            full[n] = w[n]

    odd = {}

    def odd_grads_ready(grads):
        odd["p4s"] = _rs_front([_RS_ODD[n](grads[n]) for n in _RS_ODD], list(_RS_ODD), "odd")
        odd["sems"] = _rs_scatter_start([pb for _, pb in odd["p4s"]])
        return odd["sems"][4][0, 0]

    loss_part, dx, g = _local_step(x[0] + token[0, 0], p[:, 0], loss_target[0], full, late_weights, odd_grads_ready)

    bufs = _rs_back(odd["p4s"], _rs_scatter_wait(*odd["sems"][:4], dx), list(_RS_ODD))
    p4s = _rs_front([_RS_EVEN[n](g[n]) for n in _RS_EVEN], list(_RS_EVEN), "even")
    bufs += _rs_back(p4s, _rs_chip_scatter([pb for _, pb in p4s]), list(_RS_EVEN))
    summed = dict(zip(list(_RS_ODD) + list(_RS_EVEN), _rs_share_halves(bufs)))
    g_big = {n: summed[n] for n in names if n in summed}
    g_big["w_pl"] = jnp.stack([summed["w_pl0"], summed["w_pl1"]])
    g_big["w_pl_gate"] = jnp.stack([summed["w_pl_gate0"], summed["w_pl_gate1"]])
    small_sum = _exchange_small(jnp.concatenate([loss_part, _pack_small(g, _SMALL)], axis=0), True, "reduce_small")
    loss = small_sum[0, 0]
    g_small = _unpack_small(small_sum, _SMALL, head_rows=SUBLANES)
    for n, (rows, cols) in _CONV_SHARD.items():
        g_small[n] = lax.dynamic_slice_in_dim(g_small[n], chip * cols, cols, axis=1)

    grads, delta, new_m, new_v = {}, {}, {}, {}
    for n in _BIG:
        shape2 = (-1, shard_shapes[n][-1])
        grads[n] = g_big[n].reshape(w[n].shape)
        d_, m_, v_ = _adamw(w[n].reshape(shape2), g_big[n].reshape(shape2), m[n].reshape(shape2), v[n].reshape(shape2),
                            f"adamw_{n}")
        delta[n], new_m[n], new_v[n] = (t.reshape(w[n].shape) for t in (d_, m_, v_))
    own = {n: (_CONV_SHARD[n] if n in _CONV_SHARD else _SMALL[n]) for n in _SMALL}
    packs = [_pack_small({n: src[n] for n in _SMALL}, own) for src in (w, g_small, m, v)]
    outs = [_unpack_small(t, own) for t in _adamw(*packs, "adamw_small")]
    for n in _SMALL:
        grads[n] = g_small[n].reshape(w[n].shape)
        delta[n], new_m[n], new_v[n] = (t[n].reshape(w[n].shape) for t in outs)
    return (loss, dx[None], *[grads[n] for n in _WEIGHTS], *[delta[n] for n in _WEIGHTS],
            *[new_m[n] for n in _WEIGHTS], *[new_v[n] for n in _WEIGHTS])
```

```python
import functools

import jax
import jax.numpy as jnp
from jax import lax
from jax.experimental import pallas as pl
from jax.experimental.pallas import tpu as pltpu

F32 = jnp.float32
BF16 = jnp.bfloat16
HI = lax.Precision.HIGHEST

D_MODEL = 1024
PL_DIM = 256
GDN_HEADS = 8
HEAD_DIM = 128
GDN_CHUNK = 64
HGRN_HEADS = 16
HGRN_CHUNK = 32
HGRN_WIDTH = 2048
DEEPNORM_ALPHA = 4.0 ** 0.25
NORM_EPS = 1e-5
ADAM_LR, ADAM_B1, ADAM_B2, ADAM_EPS, ADAM_WD, ADAM_STEP = 0.001, 0.9, 0.999, 1e-08, 0.01, 10

VMEM_LIMIT = 56 * 1024 * 1024
SUBLANES = 8
LANES = 128


def _params(*sem):
    return pltpu.CompilerParams(dimension_semantics=sem, vmem_limit_bytes=VMEM_LIMIT)


ONE_PASS, THREE_PASS, FULL_F32, EXACT_LHS, EXACT_RHS = 0, 1, 2, 3, 4


def _split3(v):
    hi = v.astype(BF16)
    r1 = v - hi.astype(F32)
    mid = r1.astype(BF16)
    return hi, mid, (r1 - mid.astype(F32)).astype(BF16)


def _mm_raw(a, b, kind, prec):
    nb = a.ndim - 2
    ca = a.ndim - 1 if kind[0] == "n" else a.ndim - 2
    cb = b.ndim - 2 if kind[1] == "n" else b.ndim - 1
    dims = (((ca,), (cb,)), (tuple(range(nb)),) * 2)
    if prec == FULL_F32:
        return lax.dot_general(a, b, dims, precision=HI, preferred_element_type=F32)
    dot = lambda p, q: lax.dot_general(p, q, dims, preferred_element_type=F32)
    ah, bh = a.astype(BF16), b.astype(BF16)
    if prec == ONE_PASS:
        return dot(ah, bh)
    if prec == EXACT_LHS:
        b1, b2, b3 = _split3(b)
        return dot(ah, b1) + (dot(ah, b2) + dot(ah, b3))
    if prec == EXACT_RHS:
        a1, a2, a3 = _split3(a)
        return dot(a1, bh) + (dot(a2, bh) + dot(a3, bh))
    al = (a - ah.astype(F32)).astype(BF16)
    bl = (b - bh.astype(F32)).astype(BF16)
    return dot(ah, bh) + (dot(ah, bl) + dot(al, bh))


@functools.partial(jax.custom_vjp, nondiff_argnums=(2, 3))
def _mm_vjp(a, b, kind, hi):
    return _mm_raw(a, b, kind, hi)


def _mm_vjp_fwd(a, b, kind, hi):
    return _mm_raw(a, b, kind, hi), (a, b)


def _mm_vjp_bwd(kind, hi, res, dc):
    a, b = res
    if hi in (EXACT_LHS, EXACT_RHS):
        assert kind == "nn"
        if hi == EXACT_LHS:
            return jnp.zeros_like(a), _mm_raw(a, dc, "tn", EXACT_LHS)
        return _mm_raw(dc, b, "nt", EXACT_RHS), jnp.zeros_like(b)
    if kind == "nn":
        return _mm_raw(dc, b, "nt", hi), _mm_raw(a, dc, "tn", hi)
    if kind == "nt":
        return _mm_raw(dc, b, "nn", hi), _mm_raw(dc, a, "tn", hi)
    return _mm_raw(b, dc, "nt", hi), _mm_raw(a, dc, "nn", hi)


_mm_vjp.defvjp(_mm_vjp_fwd, _mm_vjp_bwd)


def _lane_total(v):
    return jnp.broadcast_to(jnp.sum(v, axis=-1, keepdims=True), v.shape)


def _matmul(a, b, *, name, ta=False, tb=False, add=None, add_scale=1.0, tm=1024, tn=1024, tk=2048):
    m, k = (a.shape[1], a.shape[0]) if ta else a.shape
    n = b.shape[0] if tb else b.shape[1]
    tm, tn, tk = min(tm, m), min(tn, n), min(tk, k)
    assert m % tm == 0 and n % tn == 0 and k % tk == 0, (name, m, n, k)
    nk = k // tk
    dims = (((0 if ta else 1,), (1 if tb else 0,)), ((), ()))

    def body(*refs):
        a_ref, b_ref = refs[:2]
        o_ref = refs[-1]
        part = lax.dot_general(a_ref[...].astype(BF16), b_ref[...].astype(BF16), dims, preferred_element_type=F32)
        first = (lambda: part) if add is None else (lambda: part + add_scale * refs[2][...])
        if nk == 1:
            o_ref[...] = first()
        else:
            kk = pl.program_id(2)

            @pl.when(kk == 0)
            def _():
                o_ref[...] = first()

            @pl.when(kk > 0)
            def _():
                o_ref[...] += part

    a_spec = pl.BlockSpec((tk, tm), lambda i, j, kk: (kk, i)) if ta else pl.BlockSpec((tm, tk), lambda i, j, kk: (i, kk))
    b_spec = pl.BlockSpec((tn, tk), lambda i, j, kk: (j, kk)) if tb else pl.BlockSpec((tk, tn), lambda i, j, kk: (kk, j))
    o_spec = pl.BlockSpec((tm, tn), lambda i, j, kk: (i, j))
    in_specs = [a_spec, b_spec] + ([o_spec] if add is not None else [])
    args = (a, b) + ((add,) if add is not None else ())
    return pl.pallas_call(
        body, name=name, grid=(m // tm, n // tn, nk), in_specs=in_specs, out_specs=o_spec,
        out_shape=jax.ShapeDtypeStruct((m, n), F32),
        compiler_params=_params("parallel", "parallel", "arbitrary"))(*args)


HALO = SUBLANES


def _halo_specs(tt, width, col, nt):
    r = tt // HALO
    prev = pl.BlockSpec((HALO, width), lambda i: (jnp.maximum(i * r - 1, 0), col))
    nxt = pl.BlockSpec((HALO, width), lambda i: (jnp.minimum((i + 1) * r, nt * r - 1), col))
    return prev, nxt


def _shift_down(ext, k):
    return ext if k == 0 else pltpu.roll(ext, k, 0)


def _shift_up(ext, k):
    return ext if k == 0 else pltpu.roll(ext, ext.shape[0] - k, 0)


def _causal_conv(ext, w, taps):
    acc = None
    for j in range(taps):
        term = w[j:j + 1, :] * _shift_down(ext, taps - 1 - j)
        acc = term if acc is None else acc + term
    return acc[HALO:, :]


def _conv_a_fwd(proj_a, conv_w):
    t = proj_a.shape[0]
    tt = min(t, 256)
    nt = t // tt
    wdt = 1024

    def body(cur_ref, prev_ref, w_ref, y_ref):
        i = pl.program_id(0)
        cur = cur_ref[...]
        h, c, b, z = (cur[:, k * wdt:(k + 1) * wdt] for k in range(4))
        prev = prev_ref[...]
        u_prev = jnp.where(i > 0, prev[:, wdt:2 * wdt] * prev[:, 0:wdt], 0.0)
        ext = jnp.concatenate([u_prev, c * h], axis=0)
        conv = _causal_conv(ext, w_ref[...], 3)
        y_ref[...] = (b * conv * jax.nn.silu(z)).astype(BF16)

    prev_spec, _ = _halo_specs(tt, 4 * wdt, 0, nt)
    return pl.pallas_call(
        body, name="conv_a_fwd", grid=(nt,),
        in_specs=[pl.BlockSpec((tt, 4 * wdt), lambda i: (i, 0)), prev_spec, pl.BlockSpec((3, wdt), lambda i: (0, 0))],
        out_specs=pl.BlockSpec((tt, wdt), lambda i: (i, 0)),
        out_shape=jax.ShapeDtypeStruct((t, wdt), BF16), compiler_params=_params("parallel"))(proj_a, proj_a, conv_w)


def _conv_a_bwd(proj_a, conv_w, dy):
    t = proj_a.shape[0]
    tt = min(t, 256)
    nt = t // tt
    wdt = 1024

    def body(cur_ref, prev_ref, nxt_ref, w_ref, dy_ref, dyn_ref, d_ref, dw_ref):
        i = pl.program_id(0)
        w = w_ref[...]
        cur, prev, nxt = cur_ref[...], prev_ref[...], nxt_ref[...]
        split = lambda a: tuple(a[:, k * wdt:(k + 1) * wdt] for k in range(4))
        h, c, b, z = split(cur)
        hp, cp, _, _ = split(prev)
        hn, cn, bn, zn = split(nxt)
        u_prev = jnp.where(i > 0, cp * hp, 0.0)
        u_ext = jnp.concatenate([u_prev, c * h, cn * hn], axis=0)
        taps = [_shift_down(u_ext, 2 - j)[HALO:, :] for j in range(3)]
        conv = w[0:1, :] * taps[0] + w[1:2, :] * taps[1] + w[2:3, :] * taps[2]
        b_cn = jnp.concatenate([b, bn], axis=0)
        z_cn = jnp.concatenate([z, zn], axis=0)
        dy_cn = jnp.concatenate([dy_ref[...], jnp.where(i < nt - 1, dyn_ref[...], 0.0)], axis=0)
        sg = jax.nn.sigmoid(z_cn)
        silu = z_cn * sg
        d_conv = dy_cn * b_cn * silu
        db = (dy_cn * conv * silu)[:tt, :]
        dz = (dy_cn * b_cn * conv * (sg * (1.0 + z_cn * (1.0 - sg))))[:tt, :]
        du = None
        for j in range(3):
            term = w[j:j + 1, :] * _shift_up(d_conv, 2 - j)
            du = term if du is None else du + term
        du = du[:tt, :]
        d_ref[...] = jnp.concatenate([du * c, du * h, db, dz], axis=1).astype(BF16)

        @pl.when(i == 0)
        def _():
            dw_ref[...] = jnp.zeros_like(dw_ref)

        d_cur = d_conv[:tt, :]
        rows = [jnp.sum(d_cur * taps[j][:tt, :], axis=0, keepdims=True) for j in range(3)]
        dw_ref[0:3, :] += jnp.concatenate(rows, axis=0)

    prev_spec, nxt_spec = _halo_specs(tt, 4 * wdt, 0, nt)
    _, dyn_spec = _halo_specs(tt, wdt, 0, nt)
    return pl.pallas_call(
        body, name="conv_a_bwd", grid=(nt,),
        in_specs=[pl.BlockSpec((tt, 4 * wdt), lambda i: (i, 0)), prev_spec, nxt_spec,
                  pl.BlockSpec((3, wdt), lambda i: (0, 0)), pl.BlockSpec((tt, wdt), lambda i: (i, 0)), dyn_spec],
        out_specs=[pl.BlockSpec((tt, 4 * wdt), lambda i: (i, 0)), pl.BlockSpec((SUBLANES, wdt), lambda i: (0, 0))],
        out_shape=[jax.ShapeDtypeStruct((t, 4 * wdt), BF16), jax.ShapeDtypeStruct((SUBLANES, wdt), F32)],
        compiler_params=_params("arbitrary"))(proj_a, proj_a, proj_a, conv_w, dy, dy)


def _conv_b_fwd(proj_qkv, conv_w):
    t, width = proj_qkv.shape
    tt = min(t, 256)
    nt = t // tt
    wdt = 1024

    def body(cur_ref, prev_ref, w_ref, y_ref):
        i = pl.program_id(1)
        ext = jnp.concatenate([jnp.where(i > 0, prev_ref[...], 0.0), cur_ref[...]], axis=0)
        y_ref[...] = jax.nn.silu(_causal_conv(ext, w_ref[...], 4))

    r = tt // HALO
    return pl.pallas_call(
        body, name="conv_b_fwd", grid=(width // wdt, nt),
        in_specs=[pl.BlockSpec((tt, wdt), lambda j, i: (i, j)),
                  pl.BlockSpec((HALO, wdt), lambda j, i: (jnp.maximum(i * r - 1, 0), j)),
                  pl.BlockSpec((4, wdt), lambda j, i: (0, j))],
        out_specs=pl.BlockSpec((tt, wdt), lambda j, i: (i, j)),
        out_shape=jax.ShapeDtypeStruct((t, width), F32), compiler_params=_params("parallel", "parallel"))(
            proj_qkv, proj_qkv, conv_w)


def _conv_b_bwd(proj_qkv, conv_w, d_act, col, name):
    t = proj_qkv.shape[0]
    tt = min(t, 256)
    nt = t // tt
    wdt = 1024

    def body(cur_ref, prev_ref, nxt_ref, w_ref, da_ref, dan_ref, d_ref, dw_ref):
        i = pl.program_id(0)
        w = w_ref[...]
        u_ext = jnp.concatenate([jnp.where(i > 0, prev_ref[...], 0.0), cur_ref[...], nxt_ref[...]], axis=0)
        taps = [_shift_down(u_ext, 3 - j)[HALO:, :] for j in range(4)]
        conv = w[0:1, :] * taps[0] + w[1:2, :] * taps[1] + w[2:3, :] * taps[2] + w[3:4, :] * taps[3]
        da_cn = jnp.concatenate([da_ref[...], jnp.where(i < nt - 1, dan_ref[...], 0.0)], axis=0)
        sg = jax.nn.sigmoid(conv)
        d_conv = da_cn * (sg * (1.0 + conv * (1.0 - sg)))
        du = None
        for j in range(4):
            term = w[j:j + 1, :] * _shift_up(d_conv, 3 - j)
            du = term if du is None else du + term
        d_ref[...] = du[:tt, :].astype(BF16)

        @pl.when(i == 0)
        def _():
            dw_ref[...] = jnp.zeros_like(dw_ref)

        d_cur = d_conv[:tt, :]
        rows = [jnp.sum(d_cur * taps[j][:tt, :], axis=0, keepdims=True) for j in range(4)]
        dw_ref[0:4, :] += jnp.concatenate(rows, axis=0)

    prev_spec, nxt_spec = _halo_specs(tt, wdt, col, nt)
    _, dan_spec = _halo_specs(tt, wdt, 0, nt)
    return pl.pallas_call(
        body, name=name, grid=(nt,),
        in_specs=[pl.BlockSpec((tt, wdt), lambda i: (i, col)), prev_spec, nxt_spec,
                  pl.BlockSpec((4, wdt), lambda i: (0, col)), pl.BlockSpec((tt, wdt), lambda i: (i, 0)), dan_spec],
        out_specs=[pl.BlockSpec((tt, wdt), lambda i: (i, 0)), pl.BlockSpec((SUBLANES, wdt), lambda i: (0, 0))],
        out_shape=[jax.ShapeDtypeStruct((t, wdt), BF16), jax.ShapeDtypeStruct((SUBLANES, wdt), F32)],
        compiler_params=_params("arbitrary"))(proj_qkv, proj_qkv, proj_qkv, conv_w, d_act, d_act)


def _rms_gate(o, gn, z):
    on = o * lax.rsqrt(jnp.mean(o * o, axis=-1, keepdims=True) + NORM_EPS) * gn
    return on * jax.nn.silu(z)


GDN_PREP_ROWS = 512


def _unit_lower_inverse(low):
    c = low.shape[-1]
    eye = lax.broadcasted_iota(jnp.int32, low.shape, low.ndim - 2) == lax.broadcasted_iota(jnp.int32, low.shape, low.ndim - 1)
    x = -low
    inv = eye.astype(F32) + x
    for _ in range(c.bit_length() - 2):
        x = _mm_raw(x, x, "nn", THREE_PASS)
        inv = inv + _mm_raw(inv, x, "nn", THREE_PASS)
    return inv


@jax.custom_vjp
def _known_inverse(low, inv):
    return inv


def _known_inverse_fwd(low, inv):
    return inv, inv


def _known_inverse_bwd(inv, d_inv):
    return -_mm_raw(_mm_raw(inv, d_inv, "tn", THREE_PASS), inv, "nt", THREE_PASS), jnp.zeros_like(inv)


_known_inverse.defvjp(_known_inverse_fwd, _known_inverse_bwd)


def _gdn_prep(mm, qa, ka, va, braw, araw, alog, dtb, inv_kept=None):
    n, c, _ = qa.shape
    q = qa * lax.rsqrt(jnp.sum(qa * qa, axis=-1, keepdims=True) + 1e-6) * (HEAD_DIM ** -0.5)
    k = ka * lax.rsqrt(jnp.sum(ka * ka, axis=-1, keepdims=True) + 1e-6)
    beta = jax.nn.sigmoid(braw)
    g = -jnp.exp(alog) * jax.nn.softplus(araw + dtb)
    ri = lax.broadcasted_iota(jnp.int32, (n, c, c), 1)
    ci = lax.broadcasted_iota(jnp.int32, (n, c, c), 2)
    incl, strict, eye = ri >= ci, ri > ci, ri == ci
    gc = mm(incl.astype(F32), g, "nn", EXACT_LHS)
    gc_i = gc[:, :, :c]
    gc_j = mm(jnp.ones((n, c, c), F32), jnp.where(eye, gc_i, 0.0), "nn", EXACT_LHS)
    decay = jnp.where(incl, jnp.exp(jnp.where(incl, gc_i - gc_j, 0.0)), 0.0)
    kb = k * beta
    low = jnp.where(strict, mm(kb, k, "nt", ONE_PASS) * decay, 0.0)
    inv = _unit_lower_inverse(low) if inv_kept is None else _known_inverse(low, inv_kept)
    egc = jnp.exp(gc)
    u = mm(inv, va * beta, "nn", THREE_PASS)
    w = mm(inv, kb * egc, "nn", THREE_PASS)
    attn = jnp.where(incl, mm(q, k, "nt", ONE_PASS) * decay, 0.0)
    g_last = jnp.sum(g, axis=1, keepdims=True)
    outs = (u, w, q * egc, k * jnp.exp(g_last - gc), attn, jnp.exp(g_last))
    return outs + (inv,) if inv_kept is None else outs


def _gdn_scan(mm, u, w, qd, kd, attn, egl, z, gn, state):
    v_new = u - mm(w, state, "nn", ONE_PASS)
    o = mm(qd, state, "nn", ONE_PASS) + mm(attn, v_new, "nn", ONE_PASS)
    new_state = state * egl + mm(kd, v_new, "tn", ONE_PASS)
    return _rms_gate(o, gn, z), new_state


def _chunks(ref_value, n, c):
    return ref_value.reshape(n, c, ref_value.shape[-1])


def _by_head(ref, rows, heads):
    return jnp.stack([ref[rows, pl.ds(h * HEAD_DIM, HEAD_DIM)] for h in range(heads)])


def _store_heads(ref, rows, value):
    for h in range(value.shape[0]):
        ref[rows, pl.ds(h * HEAD_DIM, HEAD_DIM)] = value[h]


def _gdn_prep_specs(tb, nt_unused=None):
    col = lambda off: pl.BlockSpec((tb, HEAD_DIM), lambda h, i: (i, off + h))
    rep = pl.BlockSpec((1, tb, LANES), lambda h, i: (h, i, 0))
    par = pl.BlockSpec((1, SUBLANES, LANES), lambda h, i: (h, 0, 0))
    att = pl.BlockSpec((1, tb, GDN_CHUNK), lambda h, i: (h, i, 0))
    egl = pl.BlockSpec((1, tb // GDN_CHUNK, SUBLANES, LANES), lambda h, i: (h, i, 0, 0))
    return col, rep, par, att, egl


def _gdn_prep_fwd(qkv_act, braw, araw, alog, dtb):
    t = qkv_act.shape[0]
    tb = min(t, GDN_PREP_ROWS)
    nt, nc = t // tb, tb // GDN_CHUNK
    width = GDN_HEADS * HEAD_DIM

    def body(q_ref, k_ref, v_ref, br_ref, ar_ref, al_ref, dt_ref, u_ref, w_ref, qd_ref, kd_ref, at_ref, eg_ref, inv_ref):
        ch = lambda r: _chunks(r, nc, GDN_CHUNK)
        u, w, qd, kd, attn, egl, inv = _gdn_prep(_mm_raw, ch(q_ref[...]), ch(k_ref[...]), ch(v_ref[...]), ch(br_ref[0]),
                                                 ch(ar_ref[0]), al_ref[0, 0:1, :], dt_ref[0, 0:1, :])
        u_ref[...] = u.reshape(tb, HEAD_DIM)
        w_ref[...] = w.reshape(tb, HEAD_DIM).astype(BF16)
        qd_ref[...] = qd.reshape(tb, HEAD_DIM).astype(BF16)
        kd_ref[...] = kd.reshape(tb, HEAD_DIM).astype(BF16)
        at_ref[0] = attn.reshape(tb, GDN_CHUNK).astype(BF16)
        eg_ref[0] = jnp.broadcast_to(egl, (nc, SUBLANES, LANES))
        inv_ref[0] = inv.reshape(tb, GDN_CHUNK)

    col, rep, par, att, egl = _gdn_prep_specs(tb)
    h = GDN_HEADS
    return pl.pallas_call(
        body, name="gdn_prep_fwd", grid=(h, nt),
        in_specs=[col(0), col(h), col(2 * h), rep, rep, par, par],
        out_specs=[col(0), col(0), col(0), col(0), att, egl, att],
        out_shape=[jax.ShapeDtypeStruct((t, width), F32)] + [jax.ShapeDtypeStruct((t, width), BF16)] * 3
        + [jax.ShapeDtypeStruct((h, t, GDN_CHUNK), BF16), jax.ShapeDtypeStruct((h, t // GDN_CHUNK, SUBLANES, LANES), F32),
           jax.ShapeDtypeStruct((h, t, GDN_CHUNK), F32)],
        compiler_params=_params("parallel", "parallel"))(qkv_act, qkv_act, qkv_act, braw, araw, alog, dtb)


def _gdn_prep_bwd(qkv_act, braw, araw, alog, dtb, inv, du, dw, dqd, dkd, dattn, degl):
    t = qkv_act.shape[0]
    tb = min(t, GDN_PREP_ROWS)
    nt, nc = t // tb, tb // GDN_CHUNK
    width = GDN_HEADS * HEAD_DIM

    def body(q_ref, k_ref, v_ref, br_ref, ar_ref, al_ref, dt_ref, inv_ref, du_ref, dw_ref, dqd_ref, dkd_ref, dat_ref,
             deg_ref, dq_ref, dk_ref, dv_ref, dbr_ref, dar_ref, dal_ref, ddt_ref):
        @pl.when(pl.program_id(1) == 0)
        def _():
            dal_ref[...] = jnp.zeros_like(dal_ref)
            ddt_ref[...] = jnp.zeros_like(ddt_ref)

        ch = lambda r: _chunks(r, nc, GDN_CHUNK)
        _, vjp = jax.vjp(functools.partial(_gdn_prep, _mm_vjp, inv_kept=ch(inv_ref[0])), ch(q_ref[...]), ch(k_ref[...]),
                         ch(v_ref[...]), ch(br_ref[0]), ch(ar_ref[0]), al_ref[0, 0:1, :], dt_ref[0, 0:1, :])
        dq, dk, dv, dbr, dar, dal, ddt = vjp((ch(du_ref[...]), ch(dw_ref[...]), ch(dqd_ref[...]), ch(dkd_ref[...]),
                                              ch(dat_ref[0]), deg_ref[0][:, 0:1, :]))
        dq_ref[...] = dq.reshape(tb, HEAD_DIM)
        dk_ref[...] = dk.reshape(tb, HEAD_DIM)
        dv_ref[...] = dv.reshape(tb, HEAD_DIM)
        dbr_ref[0] = _lane_total(dbr.reshape(tb, LANES))
        dar_ref[0] = _lane_total(dar.reshape(tb, LANES))
        dal_ref[0, 0:1, :] += _lane_total(dal)
        ddt_ref[0, 0:1, :] += _lane_total(ddt)

    col, rep, par, att, egl = _gdn_prep_specs(tb)
    h = GDN_HEADS
    return pl.pallas_call(
        body, name="gdn_prep_bwd", grid=(h, nt),
        in_specs=[col(0), col(h), col(2 * h), rep, rep, par, par, att, col(0), col(0), col(0), col(0), att, egl],
        out_specs=[col(0), col(0), col(0), rep, rep, par, par],
        out_shape=[jax.ShapeDtypeStruct((t, width), F32)] * 3 + [jax.ShapeDtypeStruct((h, t, LANES), F32)] * 2
        + [jax.ShapeDtypeStruct((h, SUBLANES, LANES), F32)] * 2,
        compiler_params=_params("parallel", "arbitrary"))(qkv_act, qkv_act, qkv_act, braw, araw, alog, dtb, inv,
                                                         du, dw, dqd, dkd, dattn, degl)


def _scan_specs(tb, heads, chunk, rev, nt):
    ti = (lambda i: nt - 1 - i) if rev else (lambda i: i)
    row = pl.BlockSpec((tb, heads * HEAD_DIM), lambda i: (ti(i), 0))
    att = pl.BlockSpec((heads, tb, chunk), lambda i: (0, ti(i), 0))
    egl = pl.BlockSpec((heads, tb // chunk, SUBLANES, LANES), lambda i: (0, ti(i), 0, 0))
    hist = pl.BlockSpec((heads, tb // chunk, HEAD_DIM, HEAD_DIM), lambda i: (0, ti(i), 0, 0))
    gn = pl.BlockSpec((SUBLANES, LANES), lambda i: (0, 0))
    return row, att, egl, hist, gn


def _gdn_scan_fwd(u, w, qd, kd, attn, egl, zb, gn):
    t = u.shape[0]
    tb = min(t, 256)
    nt, nc = t // tb, tb // GDN_CHUNK
    nh = GDN_HEADS

    def body(u_ref, w_ref, qd_ref, kd_ref, at_ref, eg_ref, z_ref, gn_ref, y_ref, hist_ref, s_ref):
        @pl.when(pl.program_id(0) == 0)
        def _():
            s_ref[...] = jnp.zeros_like(s_ref)

        g = gn_ref[0:1, :]
        state = s_ref[...]
        for c in range(nc):
            rows = pl.ds(c * GDN_CHUNK, GDN_CHUNK)
            heads = lambda r: _by_head(r, rows, nh)
            hist_ref[:, c] = state
            y, state = _gdn_scan(_mm_raw, heads(u_ref), heads(w_ref), heads(qd_ref), heads(kd_ref), at_ref[:, rows, :],
                                 eg_ref[:, c, 0:1, :], heads(z_ref), g, state)
            _store_heads(y_ref, rows, y.astype(BF16))
        s_ref[...] = state

    row, att, egs, hist, gns = _scan_specs(tb, nh, GDN_CHUNK, False, nt)
    return pl.pallas_call(
        body, name="gdn_scan_fwd", grid=(nt,), in_specs=[row, row, row, row, att, egs, row, gns], out_specs=[row, hist],
        out_shape=[jax.ShapeDtypeStruct((t, nh * HEAD_DIM), BF16),
                   jax.ShapeDtypeStruct((nh, t // GDN_CHUNK, HEAD_DIM, HEAD_DIM), F32)],
        scratch_shapes=[pltpu.VMEM((nh, HEAD_DIM, HEAD_DIM), F32)],
        compiler_params=_params("arbitrary"))(u, w, qd, kd, attn, egl, zb, gn)


def _gdn_scan_bwd(u, w, qd, kd, attn, egl, zb, gn, hist, dy):
    t = u.shape[0]
    tb = min(t, 256)
    nt, nc = t // tb, tb // GDN_CHUNK
    nh = GDN_HEADS

    def body(u_ref, w_ref, qd_ref, kd_ref, at_ref, eg_ref, z_ref, gn_ref, hist_ref, dy_ref,
             du_ref, dw_ref, dqd_ref, dkd_ref, dat_ref, deg_ref, dz_ref, dgn_ref, ds_ref):
        @pl.when(pl.program_id(0) == 0)
        def _():
            ds_ref[...] = jnp.zeros_like(ds_ref)
            dgn_ref[...] = jnp.zeros_like(dgn_ref)

        g = gn_ref[0:1, :]
        d_state = ds_ref[...]
        for c in reversed(range(nc)):
            rows = pl.ds(c * GDN_CHUNK, GDN_CHUNK)
            heads = lambda r: _by_head(r, rows, nh).astype(F32)
            _, vjp = jax.vjp(functools.partial(_gdn_scan, _mm_vjp), heads(u_ref), heads(w_ref), heads(qd_ref),
                             heads(kd_ref), at_ref[:, rows, :].astype(F32), eg_ref[:, c, 0:1, :], heads(z_ref), g,
                             hist_ref[:, c])
            du, dw, dqd, dkd, dat, deg, dz, dgn, d_state = vjp((heads(dy_ref), d_state))
            _store_heads(du_ref, rows, du)
            _store_heads(dw_ref, rows, dw)
            _store_heads(dqd_ref, rows, dqd)
            _store_heads(dkd_ref, rows, dkd)
            dat_ref[:, rows, :] = dat
            deg_ref[:, c] = jnp.broadcast_to(deg, (nh, SUBLANES, LANES))
            _store_heads(dz_ref, rows, dz.astype(BF16))
            dgn_ref[0:1, :] += dgn
        ds_ref[...] = d_state

    row, att, egs, hists, gns = _scan_specs(tb, nh, GDN_CHUNK, True, nt)
    wide = jax.ShapeDtypeStruct((t, nh * HEAD_DIM), F32)
    return pl.pallas_call(
        body, name="gdn_scan_bwd", grid=(nt,),
        in_specs=[row, row, row, row, att, egs, row, gns, hists, row],
        out_specs=[row, row, row, row, att, egs, row, gns],
        out_shape=[wide] * 4 + [jax.ShapeDtypeStruct((nh, t, GDN_CHUNK), F32),
                                jax.ShapeDtypeStruct((nh, t // GDN_CHUNK, SUBLANES, LANES), F32),
                                jax.ShapeDtypeStruct((t, nh * HEAD_DIM), BF16),
                                jax.ShapeDtypeStruct((SUBLANES, LANES), F32)],
        scratch_shapes=[pltpu.VMEM((nh, HEAD_DIM, HEAD_DIM), F32)],
        compiler_params=_params("arbitrary"))(u, w, qd, kd, attn, egl, zb, gn, hist, dy)


def _hgrn_prep(mm, qr, fr, lbl):
    n, c, _ = qr.shape
    lb = jax.nn.sigmoid(lbl[1:2, :] - lbl[0:1, :])
    f = lb + (1.0 - lb) * jax.nn.sigmoid(fr)
    q = jax.nn.silu(qr)
    k = 1.0 - f
    logf = jnp.log(f)
    ri = lax.broadcasted_iota(jnp.int32, (n, c, c), 1)
    ci = lax.broadcasted_iota(jnp.int32, (n, c, c), 2)
    b = mm((ri >= ci).astype(F32), logf, "nn", EXACT_LHS)
    attn = _hgrn_attn(mm, q, k, b)
    b_last = jnp.sum(logf, axis=1, keepdims=True)
    return q * jnp.exp(b), k * jnp.exp(b_last - b), attn, jnp.exp(b_last)


HGRN_SUB = 8


@functools.partial(jax.custom_vjp, nondiff_argnums=(1,))
def _roll_rows(x, shift):
    return pltpu.roll(x, shift, x.ndim - 2)


def _roll_rows_fwd(x, shift):
    return _roll_rows(x, shift), None


def _roll_rows_bwd(shift, _, d):
    return (pltpu.roll(d, d.shape[-2] - shift, d.ndim - 2),)


_roll_rows.defvjp(_roll_rows_fwd, _roll_rows_bwd)


def _hgrn_attn(mm, q, k, b):
    n, c, d = q.shape
    sb = HGRN_SUB
    row = lax.broadcasted_iota(jnp.int32, (n, c, c), 1)
    col = lax.broadcasted_iota(jnp.int32, (n, c, c), 2)
    attn = None
    for delta in range(sb):
        if delta == 0:
            prod = q * k
        else:
            prod = q * _roll_rows(k, delta) * jnp.exp(jnp.minimum(b - _roll_rows(b, delta), 0.0))
        term = jnp.where(row - col == delta, jnp.sum(prod, axis=-1, keepdims=True), 0.0)
        attn = term if attn is None else attn + term
    far = [jnp.zeros((n, sb, c), F32)]
    row8 = lax.broadcasted_iota(jnp.int32, (n, sb, c), 1)
    col8 = lax.broadcasted_iota(jnp.int32, (n, sb, c), 2)
    for i in range(1, c // sb):
        r0 = i * sb
        bi = b[:, r0:r0 + sb, :]
        ref = bi[:, 0:1, :]
        part = mm(q[:, r0:r0 + sb, :] * jnp.exp(bi - ref), k * jnp.exp(jnp.minimum(ref - b, 0.0)), "nt", ONE_PASS)
        far.append(jnp.where(row8 + r0 - col8 >= sb, part, 0.0))
    return attn + jnp.concatenate(far, axis=1)


def _hgrn_scan(mm, qe, kd, attn, ebl, iv, z, gn, state):
    o = mm(qe, state, "nt", ONE_PASS) + mm(attn, iv, "nn", ONE_PASS)
    new_state = state * ebl + mm(iv, kd, "tn", ONE_PASS)
    return _rms_gate(o, gn, z), new_state


def _hgrn_prep_specs(tb):
    col = pl.BlockSpec((tb, HEAD_DIM), lambda h, i: (i, h))
    lbs = pl.BlockSpec((2, HEAD_DIM), lambda h, i: (0, h))
    att = pl.BlockSpec((1, tb, HGRN_CHUNK), lambda h, i: (h, i, 0))
    ebl = pl.BlockSpec((1, tb // HGRN_CHUNK, SUBLANES, LANES), lambda h, i: (h, i, 0, 0))
    return col, lbs, att, ebl


def _hgrn_prep_fwd(qr, fr, lower_bounds):
    t = qr.shape[0]
    tb = min(t, 256)
    nt, nc = t // tb, tb // HGRN_CHUNK
    hh = HGRN_HEADS

    def body(q_ref, f_ref, lb_ref, qe_ref, kd_ref, at_ref, eb_ref):
        ch = lambda r: _chunks(r, nc, HGRN_CHUNK)
        qe, kd, attn, ebl = _hgrn_prep(_mm_raw, ch(q_ref[...]), ch(f_ref[...]), lb_ref[...])
        qe_ref[...] = qe.reshape(tb, HEAD_DIM).astype(BF16)
        kd_ref[...] = kd.reshape(tb, HEAD_DIM).astype(BF16)
        at_ref[0] = attn.reshape(tb, HGRN_CHUNK).astype(BF16)
        eb_ref[0] = jnp.broadcast_to(ebl, (nc, SUBLANES, LANES))

    col, lbs, att, ebs = _hgrn_prep_specs(tb)
    return pl.pallas_call(
        body, name="hgrn_prep_fwd", grid=(hh, nt), in_specs=[col, col, lbs], out_specs=[col, col, att, ebs],
        out_shape=[jax.ShapeDtypeStruct((t, HGRN_WIDTH), BF16)] * 2
        + [jax.ShapeDtypeStruct((hh, t, HGRN_CHUNK), BF16), jax.ShapeDtypeStruct((hh, t // HGRN_CHUNK, SUBLANES, LANES), F32)],
        compiler_params=_params("parallel", "parallel"))(qr, fr, lower_bounds)


def _hgrn_prep_bwd(qr, fr, lower_bounds, dqe, dkd, dattn, debl):
    t = qr.shape[0]
    tb = min(t, 256)
    nt, nc = t // tb, tb // HGRN_CHUNK
    hh = HGRN_HEADS

    def body(q_ref, f_ref, lb_ref, dqe_ref, dkd_ref, dat_ref, deb_ref, dq_ref, df_ref, dlb_ref):
        @pl.when(pl.program_id(1) == 0)
        def _():
            dlb_ref[...] = jnp.zeros_like(dlb_ref)

        ch = lambda r: _chunks(r, nc, HGRN_CHUNK)
        _, vjp = jax.vjp(functools.partial(_hgrn_prep, _mm_vjp), ch(q_ref[...]), ch(f_ref[...]), lb_ref[...])
        dq, df, dlb = vjp((ch(dqe_ref[...]), ch(dkd_ref[...]), ch(dat_ref[0]), deb_ref[0][:, 0:1, :]))
        dq_ref[...] = dq.reshape(tb, HEAD_DIM).astype(BF16)
        df_ref[...] = df.reshape(tb, HEAD_DIM).astype(BF16)
        dlb_ref[...] += dlb

    col, lbs, att, ebs = _hgrn_prep_specs(tb)
    return pl.pallas_call(
        body, name="hgrn_prep_bwd", grid=(hh, nt), in_specs=[col, col, lbs, col, col, att, ebs],
        out_specs=[col, col, lbs],
        out_shape=[jax.ShapeDtypeStruct((t, HGRN_WIDTH), BF16)] * 2 + [jax.ShapeDtypeStruct((2, HGRN_WIDTH), F32)],
        compiler_params=_params("parallel", "arbitrary"))(qr, fr, lower_bounds, dqe, dkd, dattn, debl)


def _hgrn_scan_fwd(qe, kd, attn, ebl, iv, z, gn):
    t = qe.shape[0]
    tb = min(t, 128)
    nt, nc = t // tb, tb // HGRN_CHUNK
    hh = HGRN_HEADS

    def body(qe_ref, kd_ref, at_ref, eb_ref, i_ref, z_ref, gn_ref, y_ref, hist_ref, s_ref):
        @pl.when(pl.program_id(0) == 0)
        def _():
            s_ref[...] = jnp.zeros_like(s_ref)

        g = gn_ref[0:1, :]
        state = s_ref[...]
        for c in range(nc):
            rows = pl.ds(c * HGRN_CHUNK, HGRN_CHUNK)
            heads = lambda r: _by_head(r, rows, hh)
            hist_ref[:, c] = state
            y, state = _hgrn_scan(_mm_raw, heads(qe_ref), heads(kd_ref), at_ref[:, rows, :], eb_ref[:, c, 0:1, :],
                                  heads(i_ref), heads(z_ref), g, state)
            _store_heads(y_ref, rows, y.astype(BF16))
        s_ref[...] = state

    row, att, ebs, hist, gns = _scan_specs(tb, hh, HGRN_CHUNK, False, nt)
    return pl.pallas_call(
        body, name="hgrn_scan_fwd", grid=(nt,), in_specs=[row, row, att, ebs, row, row, gns], out_specs=[row, hist],
        out_shape=[jax.ShapeDtypeStruct((t, HGRN_WIDTH), BF16),
                   jax.ShapeDtypeStruct((hh, t // HGRN_CHUNK, HEAD_DIM, HEAD_DIM), F32)],
        scratch_shapes=[pltpu.VMEM((hh, HEAD_DIM, HEAD_DIM), F32)],
        compiler_params=_params("arbitrary"))(qe, kd, attn, ebl, iv, z, gn)


def _hgrn_scan_bwd(qe, kd, attn, ebl, iv, z, gn, hist, dy):
    t = qe.shape[0]
    tb = min(t, 128)
    nt, nc = t // tb, tb // HGRN_CHUNK
    hh = HGRN_HEADS

    def body(qe_ref, kd_ref, at_ref, eb_ref, i_ref, z_ref, gn_ref, hist_ref, dy_ref,
             dqe_ref, dkd_ref, dat_ref, deb_ref, di_ref, dz_ref, dgn_ref, ds_ref):
        @pl.when(pl.program_id(0) == 0)
        def _():
            ds_ref[...] = jnp.zeros_like(ds_ref)
            dgn_ref[...] = jnp.zeros_like(dgn_ref)

        g = gn_ref[0:1, :]
        d_state = ds_ref[...]
        for c in reversed(range(nc)):
            rows = pl.ds(c * HGRN_CHUNK, HGRN_CHUNK)
            heads = lambda r: _by_head(r, rows, hh).astype(F32)
            _, vjp = jax.vjp(functools.partial(_hgrn_scan, _mm_vjp), heads(qe_ref), heads(kd_ref),
                             at_ref[:, rows, :].astype(F32), eb_ref[:, c, 0:1, :], heads(i_ref), heads(z_ref), g,
                             hist_ref[:, c])
            dqe, dkd, dat, deb, di, dz, dgn, d_state = vjp((heads(dy_ref), d_state))
            _store_heads(dqe_ref, rows, dqe)
            _store_heads(dkd_ref, rows, dkd)
            dat_ref[:, rows, :] = dat
            deb_ref[:, c] = jnp.broadcast_to(deb, (hh, SUBLANES, LANES))
            _store_heads(di_ref, rows, di.astype(BF16))
            _store_heads(dz_ref, rows, dz.astype(BF16))
            dgn_ref[0:1, :] += dgn
        ds_ref[...] = d_state

    row, att, ebs, hists, gns = _scan_specs(tb, hh, HGRN_CHUNK, True, nt)
    wide = lambda dt: jax.ShapeDtypeStruct((t, HGRN_WIDTH), dt)
    return pl.pallas_call(
        body, name="hgrn_scan_bwd", grid=(nt,),
        in_specs=[row, row, att, ebs, row, row, gns, hists, row],
        out_specs=[row, row, att, ebs, row, row, gns],
        out_shape=[wide(F32), wide(F32), jax.ShapeDtypeStruct((hh, t, HGRN_CHUNK), F32),
                   jax.ShapeDtypeStruct((hh, t // HGRN_CHUNK, SUBLANES, LANES), F32), wide(BF16), wide(BF16),
                   jax.ShapeDtypeStruct((SUBLANES, LANES), F32)],
        scratch_shapes=[pltpu.VMEM((hh, HEAD_DIM, HEAD_DIM), F32)],
        compiler_params=_params("arbitrary"))(qe, kd, attn, ebl, iv, z, gn, hist, dy)


def _layer_norm(pre, g, b):
    mu = jnp.mean(pre, axis=-1, keepdims=True)
    d = pre - mu
    var = jnp.mean(d * d, axis=-1, keepdims=True)
    return d * lax.rsqrt(var + NORM_EPS) * g + b


def _lnpl_fwd(xin, s, p, wg, wpl, ln_g, ln_b):
    t = xin.shape[0]
    tt = min(t, 256)

    def body(x_ref, s_ref, p_ref, wg_ref, wpl_ref, g_ref, b_ref, o_ref, ob_ref):
        xn = _layer_norm(DEEPNORM_ALPHA * x_ref[...] + s_ref[...], g_ref[...], b_ref[...])
        gate = jax.nn.sigmoid(_mm_raw(xn, wg_ref[...], "nn", False))
        out = xn + _mm_raw(p_ref[...], wpl_ref[...], "nn", False) * gate
        o_ref[...] = out
        ob_ref[...] = out.astype(BF16)

    row = lambda w: pl.BlockSpec((tt, w), lambda i: (i, 0))
    full = lambda a: pl.BlockSpec(a.shape, lambda i: (0, 0))
    return pl.pallas_call(
        body, name="lnpl_fwd", grid=(t // tt,),
        in_specs=[row(D_MODEL), row(D_MODEL), row(PL_DIM), full(wg), full(wpl), full(ln_g), full(ln_b)],
        out_specs=[row(D_MODEL), row(D_MODEL)],
        out_shape=[jax.ShapeDtypeStruct((t, D_MODEL), F32), jax.ShapeDtypeStruct((t, D_MODEL), BF16)],
        compiler_params=_params("parallel"))(xin, s, p, wg, wpl, ln_g, ln_b)


def _lnpl_bwd(xin, s, p, wg, wpl, ln_g, ln_b, upstream, last, name):
    t = xin.shape[0]
    tt = min(t, 256)

    def body(x_ref, s_ref, p_ref, wg_ref, wpl_ref, g_ref, b_ref, up_ref,
             dpre_ref, dwg_ref, dwpl_ref, dg_ref, db_ref, loss_ref):
        @pl.when(pl.program_id(0) == 0)
        def _():
            for r in (dwg_ref, dwpl_ref, dg_ref, db_ref, loss_ref):
                r[...] = jnp.zeros_like(r)

        pre = DEEPNORM_ALPHA * x_ref[...] + s_ref[...]
        xn, ln_vjp = jax.vjp(_layer_norm, pre, g_ref[...], b_ref[...])
        gate = jax.nn.sigmoid(_mm_raw(xn, wg_ref[...], "nn", False))
        plv = _mm_raw(p_ref[...], wpl_ref[...], "nn", False)
        if last:
            err = xn + plv * gate - up_ref[...]
            dout = err * (1.0 / D_MODEL)
            tot = jnp.sum(jnp.sum(err * err, axis=1, keepdims=True), axis=0, keepdims=True) * (0.5 / D_MODEL)
            loss_ref[...] += jnp.broadcast_to(tot, loss_ref.shape)
        else:
            dout = up_ref[...]
        dplv = dout * gate
        dlogits = dout * plv * gate * (1.0 - gate)
        dwg_ref[...] += _mm_raw(xn, dlogits, "tn", False)
        dwpl_ref[...] += _mm_raw(p_ref[...], dplv, "tn", False)
        dxn = dout + _mm_raw(dlogits, wg_ref[...], "nt", False)
        dpre, dg, db = ln_vjp(dxn)
        dpre_ref[...] = dpre
        dg_ref[...] += dg
        db_ref[...] += db

    row = lambda w: pl.BlockSpec((tt, w), lambda i: (i, 0))
    full = lambda shape: pl.BlockSpec(shape, lambda i: (0, 0))
    return pl.pallas_call(
        body, name=name, grid=(t // tt,),
        in_specs=[row(D_MODEL), row(D_MODEL), row(PL_DIM), full(wg.shape), full(wpl.shape), full(ln_g.shape),
                  full(ln_b.shape), row(D_MODEL)],
        out_specs=[row(D_MODEL), full(wg.shape), full(wpl.shape), full(ln_g.shape), full(ln_b.shape),
                   full((SUBLANES, LANES))],
        out_shape=[jax.ShapeDtypeStruct((t, D_MODEL), F32), jax.ShapeDtypeStruct(wg.shape, F32),
                   jax.ShapeDtypeStruct(wpl.shape, F32), jax.ShapeDtypeStruct(ln_g.shape, F32),
                   jax.ShapeDtypeStruct(ln_b.shape, F32), jax.ShapeDtypeStruct((SUBLANES, LANES), F32)],
        compiler_params=_params("arbitrary"))(xin, s, p, wg, wpl, ln_g, ln_b, upstream)


def _pack_tail(dbr, dar):
    nh, t, _ = dbr.shape
    tt = min(t, 512)

    def body(b_ref, a_ref, o_ref):
        lane = lax.broadcasted_iota(jnp.int32, (tt, LANES), 1)
        acc = jnp.zeros((tt, LANES), F32)
        for h in range(nh):
            acc = jnp.where(lane == h, b_ref[h], acc)
            acc = jnp.where(lane == nh + h, a_ref[h], acc)
        o_ref[...] = acc.astype(BF16)

    spec = pl.BlockSpec((nh, tt, LANES), lambda i: (0, i, 0))
    return pl.pallas_call(
        body, name="pack_tail", grid=(t // tt,), in_specs=[spec, spec], out_specs=pl.BlockSpec((tt, LANES), lambda i: (i, 0)),
        out_shape=jax.ShapeDtypeStruct((t, LANES), BF16), compiler_params=_params("parallel"))(dbr, dar)


def _rep_rows(v):
    return jnp.broadcast_to(v.reshape(1, LANES), (SUBLANES, LANES))


def _rep_heads(v):
    return jnp.broadcast_to(v.reshape(-1, 1, 1), (v.shape[0], SUBLANES, LANES))


def _col_range(stacked, lo, hi):
    c = stacked.shape[2]
    parts = [stacked[s, :, max(lo, s * c) - s * c:min(hi, (s + 1) * c) - s * c]
             for s in range(4) if max(lo, s * c) < min(hi, (s + 1) * c)]
    return parts[0] if len(parts) == 1 else jnp.concatenate(parts, axis=1)


def _col_shards(pieces, c):
    shards, offs, o = [], [], 0
    for pc in pieces:
        offs.append(o)
        o += pc.shape[1]
    for s in range(4):
        lo, hi = s * c, (s + 1) * c
        parts = [pc[:, max(lo, o) - o:min(hi, o + pc.shape[1]) - o] for pc, o in zip(pieces, offs)
                 if max(lo, o) < min(hi, o + pc.shape[1])]
        shards.append(parts[0] if len(parts) == 1 else jnp.concatenate(parts, axis=1))
    return jnp.stack(shards)


def _local_step(x, p, target, w, late_weights, odd_grads_ready, start_token):
    a = DEEPNORM_ALPHA
    nh = GDN_HEADS
    xb = (x + start_token).astype(BF16)
    wie = w["w_in_even"]
    w_a, w_qkv, w_zb = _col_range(wie, 0, 4096), _col_range(wie, 4096, 7168), _col_range(wie, 7168, 8192)
    w_tail = jnp.pad(_col_range(wie, 8192, 8192 + 2 * nh), ((0, 0), (0, LANES - 2 * nh)))
    conv_a_w, conv_b_w = w["conv_a_w"], w["conv_b_w"]
    ln_g0, ln_b0, ln_g1, ln_b1 = (v.reshape(1, D_MODEL) for v in (w["ln_g"][0], w["ln_b"][0], w["ln_g"][1], w["ln_b"][1]))
    alog, dtb = _rep_heads(w["a_log"].reshape(nh)), _rep_heads(w["dt_bias"].reshape(nh))
    gdn_g, hgrn_g = _rep_rows(w["gdn_norm_g"]), _rep_rows(w["hgrn_norm_g"])

    proj_a = _matmul(xb, w_a, name="fwd_proj_a")
    proj_qkv = _matmul(xb, w_qkv, name="fwd_proj_qkv")
    proj_zb = _matmul(xb, w_zb, name="fwd_proj_zb")
    proj_tail = _matmul(xb, w_tail, name="fwd_proj_tail")
    rep = lambda cols: jnp.broadcast_to(cols.T[:, :, None], (nh, cols.shape[0], LANES))
    braw, araw = rep(proj_tail[:, :nh]), rep(proj_tail[:, nh:2 * nh])
    y_a = _conv_a_fwd(proj_a, conv_a_w)
    qkv_act = _conv_b_fwd(proj_qkv, conv_b_w)
    *gdn_pre, gdn_inv = _gdn_prep_fwd(qkv_act, braw, araw, alog, dtb)
    y_b, gdn_hist = _gdn_scan_fwd(*gdn_pre, proj_zb, gdn_g)
    w = {**w, **late_weights(y_b)}
    woe, wio, woo = w["w_out_even"], w["w_in_odd"], w["w_out_odd"]
    s0 = _matmul(y_b, woe[1024:], name="fwd_out_even_b", add=_matmul(y_a, woe[:1024], name="fwd_out_even_a"))
    x1, x1b = _lnpl_fwd(x, s0, p[0], w["w_pl_gate"][0], w["w_pl"][0], ln_g0, ln_b0)
    proj_o = [_matmul(x1b, wio[j], name=f"fwd_proj_odd{j}") for j in range(4)]
    hgrn_pre = _hgrn_prep_fwd(proj_o[0], proj_o[1], w["lower_bounds"])
    y_o, hgrn_hist = _hgrn_scan_fwd(*hgrn_pre, proj_o[2], proj_o[3], hgrn_g)
    s1 = _matmul(y_o, woo, name="fwd_out_odd")

    g = {}
    dpre1, dwg1, dwpl1, dlng1, dlnb1, loss = _lnpl_bwd(x1, s1, p[1], w["w_pl_gate"][1], w["w_pl"][1], ln_g1, ln_b1,
                                                     target, True, "lnpl_bwd_odd")
    dy_o = _matmul(dpre1, woo, tb=True, name="bwd_out_odd_dx")
    g["w_out_odd"] = _matmul(y_o, dpre1, ta=True, name="bwd_out_odd_dw")
    dqe, dkd, dat, deb, di, dz, dhg = _hgrn_scan_bwd(*hgrn_pre, proj_o[2], proj_o[3], hgrn_g, hgrn_hist, dy_o)
    dq, df, dlb = _hgrn_prep_bwd(proj_o[0], proj_o[1], w["lower_bounds"], dqe, dkd, dat, deb)
    dx1 = dpre1
    scale = a
    dws = []
    for j, dj in enumerate((dq, df, di, dz)):
        dx1 = _matmul(dj, wio[j], tb=True, add=dx1, add_scale=scale, name=f"bwd_proj_odd_dx{j}")
        scale = 1.0
        dws.append(_matmul(x1b, dj, ta=True, name=f"bwd_proj_odd_dw{j}"))
    g["w_in_odd"] = jnp.stack(dws)
    g["hgrn_norm_g"] = dhg[0:1]
    g["lower_bounds"] = dlb
    g["w_pl_gate1"], g["w_pl1"] = dwg1, dwpl1
    token = odd_grads_ready({n: g[n] for n in ("w_in_odd", "w_out_odd", "w_pl_gate1", "w_pl1")})

    dpre0, dwg0, dwpl0, dlng0, dlnb0, _ = _lnpl_bwd(x, s0, p[0], w["w_pl_gate"][0], w["w_pl"][0], ln_g0 + token,
                                                  ln_b0, dx1, False, "lnpl_bwd_even")
    g["w_pl_gate0"], g["w_pl0"] = dwg0, dwpl0
    g["ln_g"] = jnp.concatenate([dlng0, dlng1], axis=0)
    g["ln_b"] = jnp.concatenate([dlnb0, dlnb1], axis=0)
    dy_a = _matmul(dpre0, woe[:1024], tb=True, name="bwd_out_even_dxa")
    dy_b = _matmul(dpre0, woe[1024:], tb=True, name="bwd_out_even_dxb")
    g["w_out_even"] = jnp.concatenate([_matmul(y_a, dpre0, ta=True, name="bwd_out_even_dwa"),
                                       _matmul(y_b, dpre0, ta=True, name="bwd_out_even_dwb")], axis=0)
    du, dw, dqd, dkd, dat, deg, dzb, dgn = _gdn_scan_bwd(*gdn_pre, proj_zb, gdn_g, gdn_hist, dy_b)
    dqa, dka, dva, dbr, dar, dal, ddt = _gdn_prep_bwd(qkv_act, braw, araw, alog, dtb, gdn_inv, du, dw, dqd, dkd, dat, deg)
    g["a_log"] = dal[:, 0, 0].reshape(1, nh)
    g["dt_bias"] = ddt[:, 0, 0].reshape(1, nh)
    g["gdn_norm_g"] = dgn[0:1]
    d_pre_qkv, dwb = [], []
    for j, dj in enumerate((dqa, dka, dva)):
        dpj, dwj = _conv_b_bwd(proj_qkv, conv_b_w, dj, j, f"conv_b_bwd{j}")
        d_pre_qkv.append(dpj)
        dwb.append(dwj[:4])
    g["conv_b_w"] = jnp.concatenate(dwb, axis=1)
    d_a, dwa = _conv_a_bwd(proj_a, conv_a_w, dy_a)
    g["conv_a_w"] = dwa[:3]
    d_tail = _pack_tail(dbr, dar)
    pieces = [(d_a, w_a), (d_pre_qkv[0], w_qkv[:, :1024]), (d_pre_qkv[1], w_qkv[:, 1024:2048]),
              (d_pre_qkv[2], w_qkv[:, 2048:]), (dzb, w_zb), (d_tail, w_tail)]
    dx = dpre0
    scale = a
    dws = []
    for j, (dj, wj) in enumerate(pieces):
        dx = _matmul(dj, wj, tb=True, add=dx, add_scale=scale, name=f"bwd_proj_even_dx{j}")
        scale = 1.0
        dws.append(_matmul(xb, dj, ta=True, name=f"bwd_proj_even_dw{j}"))
    dws[-1] = dws[-1][:, :2 * nh]
    g["w_in_even"] = _col_shards(dws, wie.shape[2])
    return loss, dx, g


def _adamw(w, g, m, v, name):
    lead, rows, cols = w.shape
    tr = rows if rows <= 256 else 256
    assert rows % tr == 0, (name, rows)

    def body(w_ref, g_ref, m_ref, v_ref, d_ref, nm_ref, nv_ref):
        gg = g_ref[...]
        nm = ADAM_B1 * m_ref[...] + (1.0 - ADAM_B1) * gg
        nv = ADAM_B2 * v_ref[...] + (1.0 - ADAM_B2) * jnp.square(gg)
        m_hat = nm / (1.0 - ADAM_B1 ** ADAM_STEP)
        v_hat = nv / (1.0 - ADAM_B2 ** ADAM_STEP)
        d_ref[...] = -ADAM_LR * (m_hat / (jnp.sqrt(v_hat) + ADAM_EPS) + ADAM_WD * w_ref[...])
        nm_ref[...] = nm
        nv_ref[...] = nv

    spec = pl.BlockSpec((1, tr, cols), lambda l, i: (l, i, 0))
    return pl.pallas_call(
        body, name=name, grid=(lead, rows // tr), in_specs=[spec] * 4, out_specs=[spec] * 3,
        out_shape=[jax.ShapeDtypeStruct(w.shape, F32)] * 3, compiler_params=_params("parallel", "parallel"))(w, g, m, v)


MESH = pl.DeviceIdType.MESH
N_DEV = 8
HBM_SPEC = pl.BlockSpec(memory_space=pltpu.HBM)
VMEM_SPEC = pl.BlockSpec(memory_space=pltpu.VMEM)


def _coords():
    return lax.axis_index("x"), lax.axis_index("y"), lax.axis_index("c")


def _flip(v, bit):
    return 1 - v if bit else v


def _remote(src, dst, send_sem, recv_sem, dev):
    return pltpu.make_async_remote_copy(src_ref=src, dst_ref=dst, send_sem=send_sem, recv_sem=recv_sem,
                                        device_id=dev, device_id_type=MESH)


def _exchange_small(buf, reduce, name):
    rows = buf.shape[0]

    def body(in_ref, out_ref, slots, send_sems, recv_sems):
        x, y, c = _coords()
        me = 4 * x + 2 * y + c
        slots[me] = in_ref[...]
        peer = lambda k: (_flip(x, (k >> 2) & 1), _flip(y, (k >> 1) & 1), _flip(c, k & 1))
        sends = []
        for k in range(1, N_DEV):
            cp = _remote(in_ref, slots.at[me], send_sems.at[k - 1], recv_sems.at[k - 1], peer(k))
            cp.start()
            sends.append(cp)
        for k in range(1, N_DEV):
            px, py, pc = peer(k)
            _remote(in_ref, slots.at[4 * px + 2 * py + pc], send_sems.at[k - 1], recv_sems.at[k - 1], peer(k)).wait_recv()
        for cp in sends:
            cp.wait_send()
        if reduce:
            acc = slots[0]
            for d in range(1, N_DEV):
                acc = acc + slots[d]
            out_ref[...] = acc
        else:
            out_ref[...] = slots[...]

    out_shape = (rows, LANES) if reduce else (N_DEV, rows, LANES)
    return pl.pallas_call(
        body, name=name, in_specs=[VMEM_SPEC], out_specs=VMEM_SPEC, out_shape=jax.ShapeDtypeStruct(out_shape, F32),
        scratch_shapes=[pltpu.VMEM((N_DEV, rows, LANES), F32), pltpu.SemaphoreType.DMA((N_DEV - 1,)),
                        pltpu.SemaphoreType.DMA((N_DEV - 1,))])(buf)


def _half_rows(half, which):
    return pl.ds(pl.multiple_of(which * half, 16), half)


def _other_chip(x, y, k):
    return _flip(x, (k >> 1) & 1), _flip(y, k & 1)


def _gather_shards(shards):
    n = len(shards)

    def body(*refs):
        ins, outs = refs[:n], refs[n:2 * n]
        ici_s, ici_r, d2d_s, d2d_r = refs[2 * n:]
        x, y, c = _coords()
        chip = 2 * x + y
        sends = []
        for i in range(n):
            half = ins[i].shape[0] // 2
            for k in (1, 2, 3):
                ox, oy = _other_chip(x, y, k)
                cp = _remote(ins[i].at[_half_rows(half, c)], outs[i].at[chip, _half_rows(half, c)],
                             ici_s.at[3 * i + k - 1], ici_r.at[3 * i + k - 1], (ox, oy, c))
                cp.start()
                sends.append(cp)
        for k in (1, 2, 3):
            ox, oy = _other_chip(x, y, k)
            for i in range(n):
                half = ins[i].shape[0] // 2
                piece = outs[i].at[2 * ox + oy, _half_rows(half, c)]
                _remote(piece, piece, ici_s.at[3 * i + k - 1], ici_r.at[3 * i + k - 1], (ox, oy, c)).wait_recv()
                cp = _remote(piece, piece, d2d_s.at[3 * i + k - 1], d2d_r.at[3 * i + k - 1], (x, y, 1 - c))
                cp.start()
                sends.append(cp)
        for k in (1, 2, 3):
            ox, oy = _other_chip(x, y, k)
            for i in range(n):
                half = ins[i].shape[0] // 2
                piece = outs[i].at[2 * ox + oy, _half_rows(half, 1 - c)]
                _remote(piece, piece, d2d_s.at[3 * i + k - 1], d2d_r.at[3 * i + k - 1], (x, y, 1 - c)).wait_recv()
        for cp in sends:
            cp.wait_send()

    return pl.pallas_call(
        body, name="gather_weights", in_specs=[HBM_SPEC] * n, out_specs=[HBM_SPEC] * n,
        out_shape=[jax.ShapeDtypeStruct((4,) + s.shape, s.dtype) for s in shards],
        scratch_shapes=[pltpu.SemaphoreType.DMA((3 * n,))] * 4)(*shards)


SEM_SPEC = pl.BlockSpec(memory_space=pltpu.SEMAPHORE)
DATAFLOW = pltpu.SideEffectType.DATAFLOW_SIDE_EFFECTING


def _ici_piece(srcs, lands, send_sems, recv_sems, i, k, x, y, c):
    half = srcs[i].shape[0] // 2
    ox, oy = _other_chip(x, y, k)
    return _remote(srcs[i].at[_half_rows(half, c)], lands[i].at[2 * x + y, _half_rows(half, c)],
                   send_sems.at[3 * i + k - 1], recv_sems.at[3 * i + k - 1], (ox, oy, c)), (ox, oy)


def _gather_start(shards):
    n = len(shards)

    def body(*refs):
        srcs, lands = refs[:n], refs[n:2 * n]
        send_sems, recv_sems = refs[2 * n], refs[2 * n + 1]
        token = refs[-1]
        x, y, c = _coords()
        for i in range(n):
            for k in (1, 2, 3):
                _ici_piece(srcs, lands, send_sems, recv_sems, i, k, x, y, c)[0].start()
        token[...] = jnp.zeros_like(token)

    hbm = lambda a: pltpu.with_memory_space_constraint(a, pltpu.HBM)
    lands = [lax.empty((4,) + s.shape, s.dtype) for s in shards]
    out = pl.pallas_call(
        body, name="gather_rest_start",
        out_shape=(pltpu.SemaphoreType.DMA((3 * n,)), pltpu.SemaphoreType.DMA((3 * n,)),
                   *[pltpu.HBM(s.shape, s.dtype) for s in shards], *[pltpu.HBM(a.shape, a.dtype) for a in lands],
                   jax.ShapeDtypeStruct((SUBLANES, LANES), F32)),
        in_specs=[HBM_SPEC] * (2 * n), out_specs=(SEM_SPEC, SEM_SPEC, *[HBM_SPEC] * (2 * n), VMEM_SPEC),
        input_output_aliases={i: 2 + i for i in range(2 * n)},
        compiler_params=pltpu.CompilerParams(has_side_effects=DATAFLOW))(*[hbm(s) for s in shards], *[hbm(a) for a in lands])
    return out[0], out[1], out[2:2 + n], out[2 + n:2 + 2 * n], out[-1]


def _gather_wait(send_sems, recv_sems, srcs, lands, after):
    n = len(srcs)

    def body(*refs):
        src_refs, land_refs = refs[:n], refs[n:2 * n]
        send_sems, recv_sems = refs[2 * n], refs[2 * n + 1]
        x, y, c = _coords()
        for i in range(n):
            half = src_refs[i].shape[0] // 2
            for k in (1, 2, 3):
                cp, (ox, oy) = _ici_piece(src_refs, land_refs, send_sems, recv_sems, i, k, x, y, c)
                cp.wait_send()
                piece = land_refs[i].at[2 * ox + oy, _half_rows(half, c)]
                _remote(piece, piece, send_sems.at[3 * i + k - 1], recv_sems.at[3 * i + k - 1], (ox, oy, c)).wait_recv()

    out = pl.pallas_call(
        body, name="gather_rest_wait",
        out_shape=(*[pltpu.HBM(s.shape, s.dtype) for s in srcs], *[pltpu.HBM(a.shape, a.dtype) for a in lands]),
        in_specs=[HBM_SPEC] * (2 * n) + [SEM_SPEC, SEM_SPEC, pl.BlockSpec(memory_space=pl.ANY)],
        out_specs=tuple([HBM_SPEC] * (2 * n)), input_output_aliases={i: i for i in range(2 * n)},
        compiler_params=pltpu.CompilerParams(has_side_effects=DATAFLOW))(*srcs, *lands, send_sems, recv_sems, after)
    return out[n:]


def _gather_forward(lands):
    n = len(lands)

    def body(*refs):
        ins, outs = refs[:n], refs[n:2 * n]
        send_sems, recv_sems = refs[2 * n:]
        x, y, c = _coords()
        sends = []
        for i in range(n):
            half = ins[i].shape[1] // 2
            for k in (1, 2, 3):
                ox, oy = _other_chip(x, y, k)
                cp = _remote(ins[i].at[2 * ox + oy, _half_rows(half, c)], outs[i].at[2 * ox + oy, _half_rows(half, c)],
                             send_sems.at[3 * i + k - 1], recv_sems.at[3 * i + k - 1], (x, y, 1 - c))
                cp.start()
                sends.append(cp)
        for i in range(n):
            half = ins[i].shape[1] // 2
            for k in (1, 2, 3):
                ox, oy = _other_chip(x, y, k)
                piece = outs[i].at[2 * ox + oy, _half_rows(half, 1 - c)]
                _remote(piece, piece, send_sems.at[3 * i + k - 1], recv_sems.at[3 * i + k - 1], (x, y, 1 - c)).wait_recv()
        for cp in sends:
            cp.wait_send()

    return pl.pallas_call(
        body, name="gather_rest_forward", in_specs=[HBM_SPEC] * n, out_specs=[HBM_SPEC] * n,
        out_shape=[jax.ShapeDtypeStruct(a.shape, a.dtype) for a in lands],
        input_output_aliases={i: i for i in range(n)},
        scratch_shapes=[pltpu.SemaphoreType.DMA((3 * n,))] * 2)(*lands)


def _rs_sibling_swap(g4s, name):
    n = len(g4s)

    def body(*refs):
        ins, outs = refs[:n], refs[n:2 * n]
        send_sems, recv_sems = refs[2 * n:]
        x, y, c = _coords()
        sends = []
        for i in range(n):
            half = ins[i].shape[1] // 2
            for s in range(4):
                cp = _remote(ins[i].at[s, _half_rows(half, 1 - c)], outs[i].at[s], send_sems.at[4 * i + s],
                             recv_sems.at[4 * i + s], (x, y, 1 - c))
                cp.start()
                sends.append(cp)
        for cp in sends:
            cp.wait_recv()
        for cp in sends:
            cp.wait_send()

    return pl.pallas_call(
        body, name=name, in_specs=[HBM_SPEC] * n, out_specs=[HBM_SPEC] * n,
        out_shape=[jax.ShapeDtypeStruct((4, g.shape[1] // 2, g.shape[2]), g.dtype) for g in g4s],
        scratch_shapes=[pltpu.SemaphoreType.DMA((4 * n,))] * 2)(*g4s)


def _rs_add_sibling(g4, got, c_idx, name):
    _, rows, cols = g4.shape
    half = rows // 2
    tr = min(half, 256)
    nb = half // tr

    def body(c_ref, a_ref, b_ref, o_ref, ob_ref):
        total = a_ref[...] + b_ref[...]
        o_ref[...] = total
        ob_ref[...] = total.astype(BF16)

    blk = (1, tr, cols)
    out = pl.BlockSpec(blk, lambda s, i, c_ref: (s, i, 0))
    grid_spec = pltpu.PrefetchScalarGridSpec(
        num_scalar_prefetch=1, grid=(4, nb),
        in_specs=[pl.BlockSpec(blk, lambda s, i, c_ref: (s, c_ref[0] * nb + i, 0)), out],
        out_specs=[out, out])
    return pl.pallas_call(
        body, name=name, grid_spec=grid_spec,
        out_shape=[jax.ShapeDtypeStruct(got.shape, F32), jax.ShapeDtypeStruct(got.shape, BF16)],
        compiler_params=_params("parallel", "parallel"))(c_idx, g4, got)


def _rs_chip_scatter(p4s):
    n = len(p4s)

    def body(*refs):
        ins, outs = refs[:n], refs[n:2 * n]
        send_sems, recv_sems = refs[2 * n:]
        x, y, c = _coords()
        sends = []
        for i in range(n):
            for k in (1, 2, 3):
                ox, oy = _other_chip(x, y, k)
                cp = _remote(ins[i].at[2 * ox + oy], outs[i].at[k - 1], send_sems.at[3 * i + k - 1],
                             recv_sems.at[3 * i + k - 1], (ox, oy, c))
                cp.start()
                sends.append(cp)
        for cp in sends:
            cp.wait_recv()
        for cp in sends:
            cp.wait_send()

    return pl.pallas_call(
        body, name="rs_chip_scatter", in_specs=[HBM_SPEC] * n, out_specs=[HBM_SPEC] * n,
        out_shape=[jax.ShapeDtypeStruct((3,) + p.shape[1:], p.dtype) for p in p4s],
        scratch_shapes=[pltpu.SemaphoreType.DMA((3 * n,))] * 2)(*p4s)


def _rs_add_chips(p4, got3, idx, name):
    _, half, cols = p4.shape
    tr = min(half, 256)
    nb = half // tr

    def body(idx_ref, p_ref, a_ref, b_ref, c_ref, o_ref):
        o_ref[...] = ((p_ref[0] + a_ref[0].astype(F32)) + b_ref[0].astype(F32)) + c_ref[0].astype(F32)

    blk = (1, tr, cols)
    grid_spec = pltpu.PrefetchScalarGridSpec(
        num_scalar_prefetch=1, grid=(nb,),
        in_specs=[pl.BlockSpec(blk, lambda i, idx_ref: (idx_ref[0], i, 0))]
        + [pl.BlockSpec(blk, functools.partial(lambda k, i, idx_ref: (k, i, 0), k)) for k in range(3)],
        out_specs=pl.BlockSpec((tr, cols), lambda i, idx_ref: (idx_ref[1] * nb + i, 0)))
    return pl.pallas_call(body, name=name, grid_spec=grid_spec, out_shape=jax.ShapeDtypeStruct((2 * half, cols), F32),
                          compiler_params=_params("parallel"))(idx, p4, got3, got3, got3)


def _rs_share_halves(bufs):
    n = len(bufs)

    def body(*refs):
        ins, outs = refs[:n], refs[n:2 * n]
        send_sems, recv_sems = refs[2 * n:]
        x, y, c = _coords()
        sends = []
        for i in range(n):
            half = ins[i].shape[0] // 2
            cp = _remote(ins[i].at[_half_rows(half, c)], outs[i].at[_half_rows(half, c)], send_sems.at[i],
                         recv_sems.at[i], (x, y, 1 - c))
            cp.start()
            sends.append(cp)
        for i in range(n):
            half = ins[i].shape[0] // 2
            _remote(ins[i].at[_half_rows(half, c)], outs[i].at[_half_rows(half, 1 - c)], send_sems.at[i],
                    recv_sems.at[i], (x, y, 1 - c)).wait_recv()
        for cp in sends:
            cp.wait_send()

    return pl.pallas_call(
        body, name="rs_share_halves", in_specs=[HBM_SPEC] * n, out_specs=[HBM_SPEC] * n,
        out_shape=[jax.ShapeDtypeStruct(b.shape, b.dtype) for b in bufs],
        input_output_aliases={i: i for i in range(n)},
        scratch_shapes=[pltpu.SemaphoreType.DMA((n,))] * 2)(*bufs)


def _scatter_piece(srcs, lands, send_sems, recv_sems, i, k, x, y, c):
    ox, oy = _other_chip(x, y, k)
    return _remote(srcs[i].at[2 * ox + oy], lands[i].at[k - 1], send_sems.at[3 * i + k - 1],
                   recv_sems.at[3 * i + k - 1], (ox, oy, c))


def _rs_scatter_start(p4s):
    n = len(p4s)

    def body(*refs):
        srcs, lands = refs[:n], refs[n:2 * n]
        send_sems, recv_sems = refs[2 * n], refs[2 * n + 1]
        token = refs[-1]
        x, y, c = _coords()
        for i in range(n):
            for k in (1, 2, 3):
                _scatter_piece(srcs, lands, send_sems, recv_sems, i, k, x, y, c).start()
        token[...] = jnp.zeros_like(token)

    hbm = lambda a: pltpu.with_memory_space_constraint(a, pltpu.HBM)
    lands = [lax.empty((3,) + p.shape[1:], p.dtype) for p in p4s]
    out = pl.pallas_call(
        body, name="rs_scatter_start",
        out_shape=(pltpu.SemaphoreType.DMA((3 * n,)), pltpu.SemaphoreType.DMA((3 * n,)),
                   *[pltpu.HBM(p.shape, p.dtype) for p in p4s], *[pltpu.HBM(a.shape, a.dtype) for a in lands],
                   jax.ShapeDtypeStruct((SUBLANES, LANES), F32)),
        in_specs=[HBM_SPEC] * (2 * n), out_specs=(SEM_SPEC, SEM_SPEC, *[HBM_SPEC] * (2 * n), VMEM_SPEC),
        input_output_aliases={i: 2 + i for i in range(2 * n)},
        compiler_params=pltpu.CompilerParams(has_side_effects=DATAFLOW))(*[hbm(p) for p in p4s], *[hbm(a) for a in lands])
    return out[0], out[1], out[2:2 + n], out[2 + n:2 + 2 * n], out[-1]


def _rs_scatter_wait(send_sems, recv_sems, srcs, lands, after):
    n = len(srcs)

    def body(*refs):
        src_refs, land_refs = refs[:n], refs[n:2 * n]
        send_sems, recv_sems = refs[2 * n], refs[2 * n + 1]
        x, y, c = _coords()
        for i in range(n):
            for k in (1, 2, 3):
                cp = _scatter_piece(src_refs, land_refs, send_sems, recv_sems, i, k, x, y, c)
                cp.wait_send()
                cp.wait_recv()

    out = pl.pallas_call(
        body, name="rs_scatter_wait",
        out_shape=(*[pltpu.HBM(s.shape, s.dtype) for s in srcs], *[pltpu.HBM(a.shape, a.dtype) for a in lands]),
        in_specs=[HBM_SPEC] * (2 * n) + [SEM_SPEC, SEM_SPEC, pl.BlockSpec(memory_space=pl.ANY)],
        out_specs=tuple([HBM_SPEC] * (2 * n)), input_output_aliases={i: i for i in range(2 * n)},
        compiler_params=pltpu.CompilerParams(has_side_effects=DATAFLOW))(*srcs, *lands, send_sems, recv_sems, after)
    return out[n:]


def _rs_front(g4s, names, tag):
    c_idx = jnp.stack([lax.axis_index("c")]).astype(jnp.int32)
    got = _rs_sibling_swap(g4s, f"rs_sibling_swap_{tag}")
    return [_rs_add_sibling(g, s, c_idx, f"rs_add_sibling_{nm}") for g, s, nm in zip(g4s, got, names)]


def _rs_back(p4s, got3, names):
    x, y, c = _coords()
    idx = jnp.stack([2 * x + y, c]).astype(jnp.int32)
    return [_rs_add_chips(p, t, idx, f"rs_add_chips_{nm}") for (p, _), t, nm in zip(p4s, got3, names)]


def _cols_split(full):
    r, c4 = full.shape
    return full.reshape(r, 4, c4 // 4).transpose(1, 0, 2)


_BIG = {
    "w_in_even": ((1024, 2052), lambda s: s),
    "w_out_even": ((512, 1024), lambda s: s.reshape(2048, 1024)),
    "w_in_odd": ((1024, 2048), lambda s: s),
    "w_out_odd": ((512, 1024), lambda s: s.reshape(2048, 1024)),
    "w_pl": ((512, 256), lambda s: s.reshape(4, 2, 256, 256).transpose(1, 2, 0, 3).reshape(2, 256, 1024)),
    "w_pl_gate": ((512, 1024), lambda s: s.reshape(4, 2, 256, 1024).transpose(1, 0, 2, 3).reshape(2, 1024, 1024)),
}


_RS_ODD = {
    "w_in_odd": lambda f: f,
    "w_out_odd": lambda f: f.reshape(4, 512, 1024),
    "w_pl_gate1": lambda f: f.reshape(4, 256, 1024),
    "w_pl1": _cols_split,
}
_RS_EVEN = {
    "w_in_even": lambda f: f,
    "w_out_even": lambda f: f.reshape(4, 512, 1024),
    "w_pl_gate0": lambda f: f.reshape(4, 256, 1024),
    "w_pl0": _cols_split,
}


def _size(shape):
    n = 1
    for d in shape:
        n *= d
    return n


_SMALL = {"a_log": (1, 8), "dt_bias": (1, 8), "gdn_norm_g": (1, 128), "hgrn_norm_g": (1, 128),
          "lower_bounds": (2, 2048), "ln_g": (2, 1024), "ln_b": (2, 1024), "conv_a_w": (3, 1024), "conv_b_w": (4, 3072)}
_CONV_SHARD = {"conv_a_w": (3, 256), "conv_b_w": (4, 768)}


def _pack_small(parts, shapes, head_rows=0):
    rows = []
    for n, shape in shapes.items():
        v = parts[n].reshape(-1)
        rows.append(jnp.pad(v, (0, -v.shape[0] % LANES)).reshape(-1, LANES))
    buf = jnp.concatenate(rows, axis=0)
    return jnp.pad(buf, ((head_rows, -(buf.shape[0] + head_rows) % SUBLANES), (0, 0)))


def _unpack_small(buf, shapes, head_rows=0):
    out, off = {}, head_rows
    for n, shape in shapes.items():
        nrow = -(-_size(shape) // LANES)
        out[n] = buf[off:off + nrow].reshape(-1)[:_size(shape)].reshape(shape)
        off += nrow
    return out


_WEIGHTS = ["w_in_even", "conv_a_w", "conv_b_w", "a_log", "dt_bias", "gdn_norm_g", "w_out_even", "w_in_odd",
            "lower_bounds", "hgrn_norm_g", "w_out_odd", "ln_g", "ln_b", "w_pl", "w_pl_gate"]


def kernel(x, p, w_in_even, conv_a_w, conv_b_w, a_log, dt_bias, gdn_norm_g, w_out_even, w_in_odd, lower_bounds, hgrn_norm_g, w_out_odd, ln_g, ln_b, w_pl, w_pl_gate, loss_target, m_w_in_even, m_conv_a_w, m_conv_b_w, m_a_log, m_dt_bias, m_gdn_norm_g, m_w_out_even, m_w_in_odd, m_lower_bounds, m_hgrn_norm_g, m_w_out_odd, m_ln_g, m_ln_b, m_w_pl, m_w_pl_gate, v_w_in_even, v_conv_a_w, v_conv_b_w, v_a_log, v_dt_bias, v_gdn_norm_g, v_w_out_even, v_w_in_odd, v_lower_bounds, v_hgrn_norm_g, v_w_out_odd, v_ln_g, v_ln_b, v_w_pl, v_w_pl_gate):
    w = dict(zip(_WEIGHTS, (w_in_even, conv_a_w, conv_b_w, a_log, dt_bias, gdn_norm_g, w_out_even, w_in_odd,
                            lower_bounds, hgrn_norm_g, w_out_odd, ln_g, ln_b, w_pl, w_pl_gate)))
    m = dict(zip(_WEIGHTS, (m_w_in_even, m_conv_a_w, m_conv_b_w, m_a_log, m_dt_bias, m_gdn_norm_g, m_w_out_even,
                            m_w_in_odd, m_lower_bounds, m_hgrn_norm_g, m_w_out_odd, m_ln_g, m_ln_b, m_w_pl, m_w_pl_gate)))
    v = dict(zip(_WEIGHTS, (v_w_in_even, v_conv_a_w, v_conv_b_w, v_a_log, v_dt_bias, v_gdn_norm_g, v_w_out_even,
                            v_w_in_odd, v_lower_bounds, v_hgrn_norm_g, v_w_out_odd, v_ln_g, v_ln_b, v_w_pl, v_w_pl_gate)))
    chip = 2 * lax.axis_index("x") + lax.axis_index("y")

    names = list(_BIG)
    shard_shapes = {n: _BIG[n][0] for n in names}
    shards = {n: w[n].reshape(shard_shapes[n]).astype(BF16) for n in names}
    whole = lambda n, stacked: _BIG[n][1](lax.dynamic_update_slice(stacked, shards[n][None], (chip, 0, 0)))
    early, late = names[:1], names[1:]
    full = {n: whole(n, ga) for n, ga in zip(early, _gather_shards([shards[n] for n in early]))}
    send_sems, recv_sems, srcs, lands, token = _gather_start([shards[n] for n in late])

    def late_weights(after):
        landed = _gather_forward(_gather_wait(send_sems, recv_sems, srcs, lands, after))
        return {n: whole(n, ga) for n, ga in zip(late, landed)}

    conv_mine = _pack_small({n: w[n] for n in _CONV_SHARD}, _CONV_SHARD)
    conv_all = _exchange_small(conv_mine, False, "gather_conv")
    conv_by_chip = [_unpack_small(conv_all[2 * s], _CONV_SHARD) for s in range(4)]
    for n in _CONV_SHARD:
        full[n] = jnp.concatenate([conv_by_chip[s][n] for s in range(4)], axis=1)
    for n in _SMALL:
        if n not in _CONV_SHARD:
            full[n] = w[n]

    odd = {}

    def odd_grads_ready(grads):
        odd["p4s"] = _rs_front([_RS_ODD[n](grads[n]) for n in _RS_ODD], list(_RS_ODD), "odd")
        odd["sems"] = _rs_scatter_start([pb for _, pb in odd["p4s"]])
        return odd["sems"][4][0, 0]

    loss_part, dx, g = _local_step(x[0], p[:, 0], loss_target[0], full, late_weights, odd_grads_ready, token[0, 0])

    bufs = _rs_back(odd["p4s"], _rs_scatter_wait(*odd["sems"][:4], dx), list(_RS_ODD))
    p4s = _rs_front([_RS_EVEN[n](g[n]) for n in _RS_EVEN], list(_RS_EVEN), "even")
    bufs += _rs_back(p4s, _rs_chip_scatter([pb for _, pb in p4s]), list(_RS_EVEN))
    summed = dict(zip(list(_RS_ODD) + list(_RS_EVEN), _rs_share_halves(bufs)))
    g_big = {n: summed[n] for n in names if n in summed}
    g_big["w_pl"] = jnp.stack([summed["w_pl0"], summed["w_pl1"]])
    g_big["w_pl_gate"] = jnp.stack([summed["w_pl_gate0"], summed["w_pl_gate1"]])
    small_sum = _exchange_small(jnp.concatenate([loss_part, _pack_small(g, _SMALL)], axis=0), True, "reduce_small")
    loss = small_sum[0, 0]
    g_small = _unpack_small(small_sum, _SMALL, head_rows=SUBLANES)
    for n, (rows, cols) in _CONV_SHARD.items():
        g_small[n] = lax.dynamic_slice_in_dim(g_small[n], chip * cols, cols, axis=1)

    grads, delta, new_m, new_v = {}, {}, {}, {}
    for n in _BIG:
        grads[n] = g_big[n].reshape(w[n].shape)
        delta[n], new_m[n], new_v[n] = _adamw(w[n], grads[n], m[n], v[n], f"adamw_{n}")
    own = {n: (_CONV_SHARD[n] if n in _CONV_SHARD else _SMALL[n]) for n in _SMALL}
    packs = [_pack_small({n: src[n] for n in _SMALL}, own)[None] for src in (w, g_small, m, v)]
    outs = [_unpack_small(t[0], own) for t in _adamw(*packs, "adamw_small")]
    for n in _SMALL:
        grads[n] = g_small[n].reshape(w[n].shape)
        delta[n], new_m[n], new_v[n] = (t[n].reshape(w[n].shape) for t in outs)
    return (loss, dx[None], *[grads[n] for n in _WEIGHTS], *[delta[n] for n in _WEIGHTS],
            *[new_m[n] for n in _WEIGHTS], *[new_v[n] for n in _WEIGHTS])
```

```python
import functools

import jax
import jax.numpy as jnp
from jax import lax
from jax.experimental import pallas as pl
from jax.experimental.pallas import tpu as pltpu

F32 = jnp.float32
BF16 = jnp.bfloat16
HI = lax.Precision.HIGHEST

D_MODEL = 1024
PL_DIM = 256
GDN_HEADS = 8
HEAD_DIM = 128
GDN_CHUNK = 64
HGRN_HEADS = 16
HGRN_CHUNK = 32
HGRN_WIDTH = 2048
DEEPNORM_ALPHA = 4.0 ** 0.25
NORM_EPS = 1e-5
ADAM_LR, ADAM_B1, ADAM_B2, ADAM_EPS, ADAM_WD, ADAM_STEP = 0.001, 0.9, 0.999, 1e-08, 0.01, 10

VMEM_LIMIT = 56 * 1024 * 1024
SUBLANES = 8
LANES = 128


def _params(*sem):
    return pltpu.CompilerParams(dimension_semantics=sem, vmem_limit_bytes=VMEM_LIMIT)


ONE_PASS, THREE_PASS, FULL_F32, EXACT_LHS, EXACT_RHS = 0, 1, 2, 3, 4


def _split3(v):
    hi = v.astype(BF16)
    r1 = v - hi.astype(F32)
    mid = r1.astype(BF16)
    return hi, mid, (r1 - mid.astype(F32)).astype(BF16)


def _mm_raw(a, b, kind, prec):
    nb = a.ndim - 2
    ca = a.ndim - 1 if kind[0] == "n" else a.ndim - 2
    cb = b.ndim - 2 if kind[1] == "n" else b.ndim - 1
    dims = (((ca,), (cb,)), (tuple(range(nb)),) * 2)
    if prec == FULL_F32:
        return lax.dot_general(a, b, dims, precision=HI, preferred_element_type=F32)
    dot = lambda p, q: lax.dot_general(p, q, dims, preferred_element_type=F32)
    ah, bh = a.astype(BF16), b.astype(BF16)
    if prec == ONE_PASS:
        return dot(ah, bh)
    if prec == EXACT_LHS:
        b1, b2, b3 = _split3(b)
        return dot(ah, b1) + (dot(ah, b2) + dot(ah, b3))
    if prec == EXACT_RHS:
        a1, a2, a3 = _split3(a)
        return dot(a1, bh) + (dot(a2, bh) + dot(a3, bh))
    al = (a - ah.astype(F32)).astype(BF16)
    bl = (b - bh.astype(F32)).astype(BF16)
    return dot(ah, bh) + (dot(ah, bl) + dot(al, bh))


@functools.partial(jax.custom_vjp, nondiff_argnums=(2, 3))
def _mm_vjp(a, b, kind, hi):
    return _mm_raw(a, b, kind, hi)


def _mm_vjp_fwd(a, b, kind, hi):
    return _mm_raw(a, b, kind, hi), (a, b)


def _mm_vjp_bwd(kind, hi, res, dc):
    a, b = res
    if hi in (EXACT_LHS, EXACT_RHS):
        assert kind == "nn"
        if hi == EXACT_LHS:
            return jnp.zeros_like(a), _mm_raw(a, dc, "tn", EXACT_LHS)
        return _mm_raw(dc, b, "nt", EXACT_RHS), jnp.zeros_like(b)
    if kind == "nn":
        return _mm_raw(dc, b, "nt", hi), _mm_raw(a, dc, "tn", hi)
    if kind == "nt":
        return _mm_raw(dc, b, "nn", hi), _mm_raw(dc, a, "tn", hi)
    return _mm_raw(b, dc, "nt", hi), _mm_raw(a, dc, "nn", hi)


_mm_vjp.defvjp(_mm_vjp_fwd, _mm_vjp_bwd)


def _lane_total(v):
    return jnp.broadcast_to(jnp.sum(v, axis=-1, keepdims=True), v.shape)


def _matmul(a, b, *, name, ta=False, tb=False, add=None, add_scale=1.0, tm=1024, tn=1024, tk=2048):
    m, k = (a.shape[1], a.shape[0]) if ta else a.shape
    n = b.shape[0] if tb else b.shape[1]
    tm, tn, tk = min(tm, m), min(tn, n), min(tk, k)
    assert m % tm == 0 and n % tn == 0 and k % tk == 0, (name, m, n, k)
    nk = k // tk
    dims = (((0 if ta else 1,), (1 if tb else 0,)), ((), ()))

    def body(*refs):
        a_ref, b_ref = refs[:2]
        o_ref = refs[-1]
        part = lax.dot_general(a_ref[...].astype(BF16), b_ref[...].astype(BF16), dims, preferred_element_type=F32)
        first = (lambda: part) if add is None else (lambda: part + add_scale * refs[2][...])
        if nk == 1:
            o_ref[...] = first()
        else:
            kk = pl.program_id(2)

            @pl.when(kk == 0)
            def _():
                o_ref[...] = first()

            @pl.when(kk > 0)
            def _():
                o_ref[...] += part

    a_spec = pl.BlockSpec((tk, tm), lambda i, j, kk: (kk, i)) if ta else pl.BlockSpec((tm, tk), lambda i, j, kk: (i, kk))
    b_spec = pl.BlockSpec((tn, tk), lambda i, j, kk: (j, kk)) if tb else pl.BlockSpec((tk, tn), lambda i, j, kk: (kk, j))
    o_spec = pl.BlockSpec((tm, tn), lambda i, j, kk: (i, j))
    in_specs = [a_spec, b_spec] + ([o_spec] if add is not None else [])
    args = (a, b) + ((add,) if add is not None else ())
    return pl.pallas_call(
        body, name=name, grid=(m // tm, n // tn, nk), in_specs=in_specs, out_specs=o_spec,
        out_shape=jax.ShapeDtypeStruct((m, n), F32),
        compiler_params=_params("parallel", "parallel", "arbitrary"))(*args)


HALO = SUBLANES


def _halo_specs(tt, width, col, nt):
    r = tt // HALO
    prev = pl.BlockSpec((HALO, width), lambda i: (jnp.maximum(i * r - 1, 0), col))
    nxt = pl.BlockSpec((HALO, width), lambda i: (jnp.minimum((i + 1) * r, nt * r - 1), col))
    return prev, nxt


def _shift_down(ext, k):
    return ext if k == 0 else pltpu.roll(ext, k, 0)


def _shift_up(ext, k):
    return ext if k == 0 else pltpu.roll(ext, ext.shape[0] - k, 0)


def _causal_conv(ext, w, taps):
    acc = None
    for j in range(taps):
        term = w[j:j + 1, :] * _shift_down(ext, taps - 1 - j)
        acc = term if acc is None else acc + term
    return acc[HALO:, :]


def _conv_a_fwd(proj_a, conv_w):
    t = proj_a.shape[0]
    tt = min(t, 256)
    nt = t // tt
    wdt = 1024

    def body(cur_ref, prev_ref, w_ref, y_ref):
        i = pl.program_id(0)
        cur = cur_ref[...]
        h, c, b, z = (cur[:, k * wdt:(k + 1) * wdt] for k in range(4))
        prev = prev_ref[...]
        u_prev = jnp.where(i > 0, prev[:, wdt:2 * wdt] * prev[:, 0:wdt], 0.0)
        ext = jnp.concatenate([u_prev, c * h], axis=0)
        conv = _causal_conv(ext, w_ref[...], 3)
        y_ref[...] = (b * conv * jax.nn.silu(z)).astype(BF16)

    prev_spec, _ = _halo_specs(tt, 4 * wdt, 0, nt)
    return pl.pallas_call(
        body, name="conv_a_fwd", grid=(nt,),
        in_specs=[pl.BlockSpec((tt, 4 * wdt), lambda i: (i, 0)), prev_spec, pl.BlockSpec((3, wdt), lambda i: (0, 0))],
        out_specs=pl.BlockSpec((tt, wdt), lambda i: (i, 0)),
        out_shape=jax.ShapeDtypeStruct((t, wdt), BF16), compiler_params=_params("parallel"))(proj_a, proj_a, conv_w)


def _conv_a_bwd(proj_a, conv_w, dy):
    t = proj_a.shape[0]
    tt = min(t, 256)
    nt = t // tt
    wdt = 1024

    def body(cur_ref, prev_ref, nxt_ref, w_ref, dy_ref, dyn_ref, d_ref, dw_ref):
        i = pl.program_id(0)
        w = w_ref[...]
        cur, prev, nxt = cur_ref[...], prev_ref[...], nxt_ref[...]
        split = lambda a: tuple(a[:, k * wdt:(k + 1) * wdt] for k in range(4))
        h, c, b, z = split(cur)
        hp, cp, _, _ = split(prev)
        hn, cn, bn, zn = split(nxt)
        u_prev = jnp.where(i > 0, cp * hp, 0.0)
        u_ext = jnp.concatenate([u_prev, c * h, cn * hn], axis=0)
        taps = [_shift_down(u_ext, 2 - j)[HALO:, :] for j in range(3)]
        conv = w[0:1, :] * taps[0] + w[1:2, :] * taps[1] + w[2:3, :] * taps[2]
        b_cn = jnp.concatenate([b, bn], axis=0)
        z_cn = jnp.concatenate([z, zn], axis=0)
        dy_cn = jnp.concatenate([dy_ref[...], jnp.where(i < nt - 1, dyn_ref[...], 0.0)], axis=0)
        sg = jax.nn.sigmoid(z_cn)
        silu = z_cn * sg
        d_conv = dy_cn * b_cn * silu
        db = (dy_cn * conv * silu)[:tt, :]
        dz = (dy_cn * b_cn * conv * (sg * (1.0 + z_cn * (1.0 - sg))))[:tt, :]
        du = None
        for j in range(3):
            term = w[j:j + 1, :] * _shift_up(d_conv, 2 - j)
            du = term if du is None else du + term
        du = du[:tt, :]
        d_ref[...] = jnp.concatenate([du * c, du * h, db, dz], axis=1).astype(BF16)

        @pl.when(i == 0)
        def _():
            dw_ref[...] = jnp.zeros_like(dw_ref)

        d_cur = d_conv[:tt, :]
        rows = [jnp.sum(d_cur * taps[j][:tt, :], axis=0, keepdims=True) for j in range(3)]
        dw_ref[0:3, :] += jnp.concatenate(rows, axis=0)

    prev_spec, nxt_spec = _halo_specs(tt, 4 * wdt, 0, nt)
    _, dyn_spec = _halo_specs(tt, wdt, 0, nt)
    return pl.pallas_call(
        body, name="conv_a_bwd", grid=(nt,),
        in_specs=[pl.BlockSpec((tt, 4 * wdt), lambda i: (i, 0)), prev_spec, nxt_spec,
                  pl.BlockSpec((3, wdt), lambda i: (0, 0)), pl.BlockSpec((tt, wdt), lambda i: (i, 0)), dyn_spec],
        out_specs=[pl.BlockSpec((tt, 4 * wdt), lambda i: (i, 0)), pl.BlockSpec((SUBLANES, wdt), lambda i: (0, 0))],
        out_shape=[jax.ShapeDtypeStruct((t, 4 * wdt), BF16), jax.ShapeDtypeStruct((SUBLANES, wdt), F32)],
        compiler_params=_params("arbitrary"))(proj_a, proj_a, proj_a, conv_w, dy, dy)


def _conv_b_fwd(proj_qkv, conv_w):
    t, width = proj_qkv.shape
    tt = min(t, 256)
    nt = t // tt
    wdt = 1024

    def body(cur_ref, prev_ref, w_ref, y_ref):
        i = pl.program_id(1)
        ext = jnp.concatenate([jnp.where(i > 0, prev_ref[...], 0.0), cur_ref[...]], axis=0)
        y_ref[...] = jax.nn.silu(_causal_conv(ext, w_ref[...], 4))

    r = tt // HALO
    return pl.pallas_call(
        body, name="conv_b_fwd", grid=(width // wdt, nt),
        in_specs=[pl.BlockSpec((tt, wdt), lambda j, i: (i, j)),
                  pl.BlockSpec((HALO, wdt), lambda j, i: (jnp.maximum(i * r - 1, 0), j)),
                  pl.BlockSpec((4, wdt), lambda j, i: (0, j))],
        out_specs=pl.BlockSpec((tt, wdt), lambda j, i: (i, j)),
        out_shape=jax.ShapeDtypeStruct((t, width), F32), compiler_params=_params("parallel", "parallel"))(
            proj_qkv, proj_qkv, conv_w)


def _conv_b_bwd(proj_qkv, conv_w, d_act, col, name):
    t = proj_qkv.shape[0]
    tt = min(t, 256)
    nt = t // tt
    wdt = 1024

    def body(cur_ref, prev_ref, nxt_ref, w_ref, da_ref, dan_ref, d_ref, dw_ref):
        i = pl.program_id(0)
        w = w_ref[...]
        u_ext = jnp.concatenate([jnp.where(i > 0, prev_ref[...], 0.0), cur_ref[...], nxt_ref[...]], axis=0)
        taps = [_shift_down(u_ext, 3 - j)[HALO:, :] for j in range(4)]
        conv = w[0:1, :] * taps[0] + w[1:2, :] * taps[1] + w[2:3, :] * taps[2] + w[3:4, :] * taps[3]
        da_cn = jnp.concatenate([da_ref[...], jnp.where(i < nt - 1, dan_ref[...], 0.0)], axis=0)
        sg = jax.nn.sigmoid(conv)
        d_conv = da_cn * (sg * (1.0 + conv * (1.0 - sg)))
        du = None
        for j in range(4):
            term = w[j:j + 1, :] * _shift_up(d_conv, 3 - j)
            du = term if du is None else du + term
        d_ref[...] = du[:tt, :].astype(BF16)

        @pl.when(i == 0)
        def _():
            dw_ref[...] = jnp.zeros_like(dw_ref)

        d_cur = d_conv[:tt, :]
        rows = [jnp.sum(d_cur * taps[j][:tt, :], axis=0, keepdims=True) for j in range(4)]
        dw_ref[0:4, :] += jnp.concatenate(rows, axis=0)

    prev_spec, nxt_spec = _halo_specs(tt, wdt, col, nt)
    _, dan_spec = _halo_specs(tt, wdt, 0, nt)
    return pl.pallas_call(
        body, name=name, grid=(nt,),
        in_specs=[pl.BlockSpec((tt, wdt), lambda i: (i, col)), prev_spec, nxt_spec,
                  pl.BlockSpec((4, wdt), lambda i: (0, col)), pl.BlockSpec((tt, wdt), lambda i: (i, 0)), dan_spec],
        out_specs=[pl.BlockSpec((tt, wdt), lambda i: (i, 0)), pl.BlockSpec((SUBLANES, wdt), lambda i: (0, 0))],
        out_shape=[jax.ShapeDtypeStruct((t, wdt), BF16), jax.ShapeDtypeStruct((SUBLANES, wdt), F32)],
        compiler_params=_params("arbitrary"))(proj_qkv, proj_qkv, proj_qkv, conv_w, d_act, d_act)


def _rms_gate(o, gn, z):
    on = o * lax.rsqrt(jnp.mean(o * o, axis=-1, keepdims=True) + NORM_EPS) * gn
    return on * jax.nn.silu(z)


GDN_PREP_ROWS = 1024


def _unit_lower_inverse(low):
    c = low.shape[-1]
    eye = lax.broadcasted_iota(jnp.int32, low.shape, low.ndim - 2) == lax.broadcasted_iota(jnp.int32, low.shape, low.ndim - 1)
    x = -low
    inv = eye.astype(F32) + x
    for _ in range(c.bit_length() - 2):
        x = _mm_raw(x, x, "nn", THREE_PASS)
        inv = inv + _mm_raw(inv, x, "nn", THREE_PASS)
    return inv


@jax.custom_vjp
def _known_inverse(low, inv):
    return inv


def _known_inverse_fwd(low, inv):
    return inv, inv


def _known_inverse_bwd(inv, d_inv):
    return -_mm_raw(_mm_raw(inv, d_inv, "tn", THREE_PASS), inv, "nt", THREE_PASS), jnp.zeros_like(inv)


_known_inverse.defvjp(_known_inverse_fwd, _known_inverse_bwd)


def _gdn_prep(mm, qa, ka, va, braw, araw, alog, dtb, inv_kept=None):
    n, c, _ = qa.shape
    q = qa * lax.rsqrt(jnp.sum(qa * qa, axis=-1, keepdims=True) + 1e-6) * (HEAD_DIM ** -0.5)
    k = ka * lax.rsqrt(jnp.sum(ka * ka, axis=-1, keepdims=True) + 1e-6)
    beta = jax.nn.sigmoid(braw)
    g = -jnp.exp(alog) * jax.nn.softplus(araw + dtb)
    ri = lax.broadcasted_iota(jnp.int32, (n, c, c), 1)
    ci = lax.broadcasted_iota(jnp.int32, (n, c, c), 2)
    incl, strict, eye = ri >= ci, ri > ci, ri == ci
    gc = mm(incl.astype(F32), g, "nn", EXACT_LHS)
    gc_i = gc[:, :, :c]
    gc_j = mm(jnp.ones((n, c, c), F32), jnp.where(eye, gc_i, 0.0), "nn", EXACT_LHS)
    decay = jnp.where(incl, jnp.exp(jnp.where(incl, gc_i - gc_j, 0.0)), 0.0)
    kb = k * beta
    low = jnp.where(strict, mm(kb, k, "nt", ONE_PASS) * decay, 0.0)
    inv = _unit_lower_inverse(low) if inv_kept is None else _known_inverse(low, inv_kept)
    egc = jnp.exp(gc)
    u = mm(inv, va * beta, "nn", THREE_PASS)
    w = mm(inv, kb * egc, "nn", THREE_PASS)
    attn = jnp.where(incl, mm(q, k, "nt", ONE_PASS) * decay, 0.0)
    g_last = jnp.sum(g, axis=1, keepdims=True)
    outs = (u, w, q * egc, k * jnp.exp(g_last - gc), attn, jnp.exp(g_last))
    return outs + (inv,) if inv_kept is None else outs


def _gdn_scan(mm, u, w, qd, kd, attn, egl, z, gn, state):
    v_new = u - mm(w, state, "nn", ONE_PASS)
    o = mm(qd, state, "nn", ONE_PASS) + mm(attn, v_new, "nn", ONE_PASS)
    new_state = state * egl + mm(kd, v_new, "tn", ONE_PASS)
    return _rms_gate(o, gn, z), new_state


def _chunks(ref_value, n, c):
    return ref_value.reshape(n, c, ref_value.shape[-1])


def _by_head(ref, rows, heads):
    return jnp.stack([ref[rows, pl.ds(h * HEAD_DIM, HEAD_DIM)] for h in range(heads)])


def _store_heads(ref, rows, value):
    for h in range(value.shape[0]):
        ref[rows, pl.ds(h * HEAD_DIM, HEAD_DIM)] = value[h]


def _gdn_prep_specs(tb, nt_unused=None):
    col = lambda off: pl.BlockSpec((tb, HEAD_DIM), lambda h, i: (i, off + h))
    rep = pl.BlockSpec((1, tb, LANES), lambda h, i: (h, i, 0))
    par = pl.BlockSpec((1, SUBLANES, LANES), lambda h, i: (h, 0, 0))
    att = pl.BlockSpec((1, tb, GDN_CHUNK), lambda h, i: (h, i, 0))
    egl = pl.BlockSpec((1, tb // GDN_CHUNK, SUBLANES, LANES), lambda h, i: (h, i, 0, 0))
    return col, rep, par, att, egl


def _gdn_prep_fwd(qkv_act, braw, araw, alog, dtb):
    t = qkv_act.shape[0]
    tb = min(t, GDN_PREP_ROWS)
    nt, nc = t // tb, tb // GDN_CHUNK
    width = GDN_HEADS * HEAD_DIM

    def body(q_ref, k_ref, v_ref, br_ref, ar_ref, al_ref, dt_ref, u_ref, w_ref, qd_ref, kd_ref, at_ref, eg_ref, inv_ref):
        ch = lambda r: _chunks(r, nc, GDN_CHUNK)
        u, w, qd, kd, attn, egl, inv = _gdn_prep(_mm_raw, ch(q_ref[...]), ch(k_ref[...]), ch(v_ref[...]), ch(br_ref[0]),
                                                 ch(ar_ref[0]), al_ref[0, 0:1, :], dt_ref[0, 0:1, :])
        u_ref[...] = u.reshape(tb, HEAD_DIM)
        w_ref[...] = w.reshape(tb, HEAD_DIM).astype(BF16)
        qd_ref[...] = qd.reshape(tb, HEAD_DIM).astype(BF16)
        kd_ref[...] = kd.reshape(tb, HEAD_DIM).astype(BF16)
        at_ref[0] = attn.reshape(tb, GDN_CHUNK).astype(BF16)
        eg_ref[0] = jnp.broadcast_to(egl, (nc, SUBLANES, LANES))
        inv_ref[0] = inv.reshape(tb, GDN_CHUNK)

    col, rep, par, att, egl = _gdn_prep_specs(tb)
    h = GDN_HEADS
    return pl.pallas_call(
        body, name="gdn_prep_fwd", grid=(h, nt),
        in_specs=[col(0), col(h), col(2 * h), rep, rep, par, par],
        out_specs=[col(0), col(0), col(0), col(0), att, egl, att],
        out_shape=[jax.ShapeDtypeStruct((t, width), F32)] + [jax.ShapeDtypeStruct((t, width), BF16)] * 3
        + [jax.ShapeDtypeStruct((h, t, GDN_CHUNK), BF16), jax.ShapeDtypeStruct((h, t // GDN_CHUNK, SUBLANES, LANES), F32),
           jax.ShapeDtypeStruct((h, t, GDN_CHUNK), F32)],
        compiler_params=_params("parallel", "parallel"))(qkv_act, qkv_act, qkv_act, braw, araw, alog, dtb)


def _gdn_prep_bwd(qkv_act, braw, araw, alog, dtb, inv, du, dw, dqd, dkd, dattn, degl):
    t = qkv_act.shape[0]
    tb = min(t, GDN_PREP_ROWS)
    nt, nc = t // tb, tb // GDN_CHUNK
    width = GDN_HEADS * HEAD_DIM

    def body(q_ref, k_ref, v_ref, br_ref, ar_ref, al_ref, dt_ref, inv_ref, du_ref, dw_ref, dqd_ref, dkd_ref, dat_ref,
             deg_ref, dq_ref, dk_ref, dv_ref, dbr_ref, dar_ref, dal_ref, ddt_ref):
        @pl.when(pl.program_id(1) == 0)
        def _():
            dal_ref[...] = jnp.zeros_like(dal_ref)
            ddt_ref[...] = jnp.zeros_like(ddt_ref)

        ch = lambda r: _chunks(r, nc, GDN_CHUNK)
        _, vjp = jax.vjp(functools.partial(_gdn_prep, _mm_vjp, inv_kept=ch(inv_ref[0])), ch(q_ref[...]), ch(k_ref[...]),
                         ch(v_ref[...]), ch(br_ref[0]), ch(ar_ref[0]), al_ref[0, 0:1, :], dt_ref[0, 0:1, :])
        dq, dk, dv, dbr, dar, dal, ddt = vjp((ch(du_ref[...]), ch(dw_ref[...]), ch(dqd_ref[...]), ch(dkd_ref[...]),
                                              ch(dat_ref[0]), deg_ref[0][:, 0:1, :]))
        dq_ref[...] = dq.reshape(tb, HEAD_DIM)
        dk_ref[...] = dk.reshape(tb, HEAD_DIM)
        dv_ref[...] = dv.reshape(tb, HEAD_DIM)
        dbr_ref[0] = _lane_total(dbr.reshape(tb, LANES))
        dar_ref[0] = _lane_total(dar.reshape(tb, LANES))
        dal_ref[0, 0:1, :] += _lane_total(dal)
        ddt_ref[0, 0:1, :] += _lane_total(ddt)

    col, rep, par, att, egl = _gdn_prep_specs(tb)
    h = GDN_HEADS
    return pl.pallas_call(
        body, name="gdn_prep_bwd", grid=(h, nt),
        in_specs=[col(0), col(h), col(2 * h), rep, rep, par, par, att, col(0), col(0), col(0), col(0), att, egl],
        out_specs=[col(0), col(0), col(0), rep, rep, par, par],
        out_shape=[jax.ShapeDtypeStruct((t, width), F32)] * 3 + [jax.ShapeDtypeStruct((h, t, LANES), F32)] * 2
        + [jax.ShapeDtypeStruct((h, SUBLANES, LANES), F32)] * 2,
        compiler_params=_params("parallel", "arbitrary"))(qkv_act, qkv_act, qkv_act, braw, araw, alog, dtb, inv,
                                                         du, dw, dqd, dkd, dattn, degl)


def _scan_specs(tb, heads, chunk, rev, nt):
    ti = (lambda i: nt - 1 - i) if rev else (lambda i: i)
    row = pl.BlockSpec((tb, heads * HEAD_DIM), lambda i: (ti(i), 0))
    att = pl.BlockSpec((heads, tb, chunk), lambda i: (0, ti(i), 0))
    egl = pl.BlockSpec((heads, tb // chunk, SUBLANES, LANES), lambda i: (0, ti(i), 0, 0))
    hist = pl.BlockSpec((heads, tb // chunk, HEAD_DIM, HEAD_DIM), lambda i: (0, ti(i), 0, 0))
    gn = pl.BlockSpec((SUBLANES, LANES), lambda i: (0, 0))
    return row, att, egl, hist, gn


def _gdn_scan_fwd(u, w, qd, kd, attn, egl, zb, gn):
    t = u.shape[0]
    tb = min(t, 256)
    nt, nc = t // tb, tb // GDN_CHUNK
    nh = GDN_HEADS

    def body(u_ref, w_ref, qd_ref, kd_ref, at_ref, eg_ref, z_ref, gn_ref, y_ref, hist_ref, s_ref):
        @pl.when(pl.program_id(0) == 0)
        def _():
            s_ref[...] = jnp.zeros_like(s_ref)

        g = gn_ref[0:1, :]
        state = s_ref[...]
        for c in range(nc):
            rows = pl.ds(c * GDN_CHUNK, GDN_CHUNK)
            heads = lambda r: _by_head(r, rows, nh)
            hist_ref[:, c] = state
            y, state = _gdn_scan(_mm_raw, heads(u_ref), heads(w_ref), heads(qd_ref), heads(kd_ref), at_ref[:, rows, :],
                                 eg_ref[:, c, 0:1, :], heads(z_ref), g, state)
            _store_heads(y_ref, rows, y.astype(BF16))
        s_ref[...] = state

    row, att, egs, hist, gns = _scan_specs(tb, nh, GDN_CHUNK, False, nt)
    return pl.pallas_call(
        body, name="gdn_scan_fwd", grid=(nt,), in_specs=[row, row, row, row, att, egs, row, gns], out_specs=[row, hist],
        out_shape=[jax.ShapeDtypeStruct((t, nh * HEAD_DIM), BF16),
                   jax.ShapeDtypeStruct((nh, t // GDN_CHUNK, HEAD_DIM, HEAD_DIM), F32)],
        scratch_shapes=[pltpu.VMEM((nh, HEAD_DIM, HEAD_DIM), F32)],
        compiler_params=_params("arbitrary"))(u, w, qd, kd, attn, egl, zb, gn)


def _gdn_scan_bwd(u, w, qd, kd, attn, egl, zb, gn, hist, dy):
    t = u.shape[0]
    tb = min(t, 256)
    nt, nc = t // tb, tb // GDN_CHUNK
    nh = GDN_HEADS

    def body(u_ref, w_ref, qd_ref, kd_ref, at_ref, eg_ref, z_ref, gn_ref, hist_ref, dy_ref,
             du_ref, dw_ref, dqd_ref, dkd_ref, dat_ref, deg_ref, dz_ref, dgn_ref, ds_ref):
        @pl.when(pl.program_id(0) == 0)
        def _():
            ds_ref[...] = jnp.zeros_like(ds_ref)
            dgn_ref[...] = jnp.zeros_like(dgn_ref)

        g = gn_ref[0:1, :]
        d_state = ds_ref[...]
        for c in reversed(range(nc)):
            rows = pl.ds(c * GDN_CHUNK, GDN_CHUNK)
            heads = lambda r: _by_head(r, rows, nh).astype(F32)
            _, vjp = jax.vjp(functools.partial(_gdn_scan, _mm_vjp), heads(u_ref), heads(w_ref), heads(qd_ref),
                             heads(kd_ref), at_ref[:, rows, :].astype(F32), eg_ref[:, c, 0:1, :], heads(z_ref), g,
                             hist_ref[:, c])
            du, dw, dqd, dkd, dat, deg, dz, dgn, d_state = vjp((heads(dy_ref), d_state))
            _store_heads(du_ref, rows, du)
            _store_heads(dw_ref, rows, dw)
            _store_heads(dqd_ref, rows, dqd)
            _store_heads(dkd_ref, rows, dkd)
            dat_ref[:, rows, :] = dat
            deg_ref[:, c] = jnp.broadcast_to(deg, (nh, SUBLANES, LANES))
            _store_heads(dz_ref, rows, dz.astype(BF16))
            dgn_ref[0:1, :] += dgn
        ds_ref[...] = d_state

    row, att, egs, hists, gns = _scan_specs(tb, nh, GDN_CHUNK, True, nt)
    wide = jax.ShapeDtypeStruct((t, nh * HEAD_DIM), F32)
    return pl.pallas_call(
        body, name="gdn_scan_bwd", grid=(nt,),
        in_specs=[row, row, row, row, att, egs, row, gns, hists, row],
        out_specs=[row, row, row, row, att, egs, row, gns],
        out_shape=[wide] * 4 + [jax.ShapeDtypeStruct((nh, t, GDN_CHUNK), F32),
                                jax.ShapeDtypeStruct((nh, t // GDN_CHUNK, SUBLANES, LANES), F32),
                                jax.ShapeDtypeStruct((t, nh * HEAD_DIM), BF16),
                                jax.ShapeDtypeStruct((SUBLANES, LANES), F32)],
        scratch_shapes=[pltpu.VMEM((nh, HEAD_DIM, HEAD_DIM), F32)],
        compiler_params=_params("arbitrary"))(u, w, qd, kd, attn, egl, zb, gn, hist, dy)


def _hgrn_prep(mm, qr, fr, lbl):
    n, c, _ = qr.shape
    lb = jax.nn.sigmoid(lbl[1:2, :] - lbl[0:1, :])
    f = lb + (1.0 - lb) * jax.nn.sigmoid(fr)
    q = jax.nn.silu(qr)
    k = 1.0 - f
    logf = jnp.log(f)
    ri = lax.broadcasted_iota(jnp.int32, (n, c, c), 1)
    ci = lax.broadcasted_iota(jnp.int32, (n, c, c), 2)
    b = mm((ri >= ci).astype(F32), logf, "nn", EXACT_LHS)
    attn = _hgrn_attn(mm, q, k, b)
    b_last = jnp.sum(logf, axis=1, keepdims=True)
    return q * jnp.exp(b), k * jnp.exp(b_last - b), attn, jnp.exp(b_last)


HGRN_SUB = 8


@functools.partial(jax.custom_vjp, nondiff_argnums=(1,))
def _roll_rows(x, shift):
    return pltpu.roll(x, shift, x.ndim - 2)


def _roll_rows_fwd(x, shift):
    return _roll_rows(x, shift), None


def _roll_rows_bwd(shift, _, d):
    return (pltpu.roll(d, d.shape[-2] - shift, d.ndim - 2),)


_roll_rows.defvjp(_roll_rows_fwd, _roll_rows_bwd)


@jax.custom_vjp
def _exp_clamped(v):
    return jnp.exp(jnp.minimum(v, 0.0))


def _exp_clamped_fwd(v):
    out = jnp.exp(jnp.minimum(v, 0.0))
    return out, out


def _exp_clamped_bwd(out, d):
    return (d * out,)


_exp_clamped.defvjp(_exp_clamped_fwd, _exp_clamped_bwd)


def _hgrn_attn(mm, q, k, b):
    n, c, d = q.shape
    sb = HGRN_SUB
    sub = lambda a: a.reshape(n * c // sb, sb, d)
    qs, ks, bs = sub(q), sub(k), sub(b)
    row = lax.broadcasted_iota(jnp.int32, (n, c, c), 1)
    col = lax.broadcasted_iota(jnp.int32, (n, c, c), 2)
    same_block = (row & -sb) == (col & -sb)
    attn = None
    for delta in range(sb):
        if delta == 0:
            prod = qs * ks
        else:
            prod = qs * _roll_rows(ks, delta) * _exp_clamped(bs - _roll_rows(bs, delta))
        sums = jnp.sum(prod, axis=-1, keepdims=True).reshape(n, c, 1)
        term = jnp.where(same_block & (row - col == delta), sums, 0.0)
        attn = term if attn is None else attn + term
    far = [jnp.zeros((n, sb, c), F32)]
    col8 = lax.broadcasted_iota(jnp.int32, (n, sb, c), 2)
    for i in range(1, c // sb):
        r0 = i * sb
        bi = b[:, r0:r0 + sb, :]
        ref = bi[:, 0:1, :]
        part = mm(q[:, r0:r0 + sb, :] * jnp.exp(bi - ref), k * _exp_clamped(ref - b), "nt", ONE_PASS)
        far.append(jnp.where(col8 < r0, part, 0.0))
    return attn + jnp.concatenate(far, axis=1)


def _hgrn_scan(mm, qe, kd, attn, ebl, iv, z, gn, state):
    o = mm(qe, state, "nt", ONE_PASS) + mm(attn, iv, "nn", ONE_PASS)
    new_state = state * ebl + mm(iv, kd, "tn", ONE_PASS)
    return _rms_gate(o, gn, z), new_state


def _hgrn_prep_specs(tb):
    col = pl.BlockSpec((tb, HEAD_DIM), lambda h, i: (i, h))
    lbs = pl.BlockSpec((2, HEAD_DIM), lambda h, i: (0, h))
    att = pl.BlockSpec((1, tb, HGRN_CHUNK), lambda h, i: (h, i, 0))
    ebl = pl.BlockSpec((1, tb // HGRN_CHUNK, SUBLANES, LANES), lambda h, i: (h, i, 0, 0))
    return col, lbs, att, ebl


def _hgrn_prep_fwd(qr, fr, lower_bounds):
    t = qr.shape[0]
    tb = min(t, 256)
    nt, nc = t // tb, tb // HGRN_CHUNK
    hh = HGRN_HEADS

    def body(q_ref, f_ref, lb_ref, qe_ref, kd_ref, at_ref, eb_ref):
        ch = lambda r: _chunks(r, nc, HGRN_CHUNK)
        qe, kd, attn, ebl = _hgrn_prep(_mm_raw, ch(q_ref[...]), ch(f_ref[...]), lb_ref[...])
        qe_ref[...] = qe.reshape(tb, HEAD_DIM).astype(BF16)
        kd_ref[...] = kd.reshape(tb, HEAD_DIM).astype(BF16)
        at_ref[0] = attn.reshape(tb, HGRN_CHUNK).astype(BF16)
        eb_ref[0] = jnp.broadcast_to(ebl, (nc, SUBLANES, LANES))

    col, lbs, att, ebs = _hgrn_prep_specs(tb)
    return pl.pallas_call(
        body, name="hgrn_prep_fwd", grid=(hh, nt), in_specs=[col, col, lbs], out_specs=[col, col, att, ebs],
        out_shape=[jax.ShapeDtypeStruct((t, HGRN_WIDTH), BF16)] * 2
        + [jax.ShapeDtypeStruct((hh, t, HGRN_CHUNK), BF16), jax.ShapeDtypeStruct((hh, t // HGRN_CHUNK, SUBLANES, LANES), F32)],
        compiler_params=_params("parallel", "parallel"))(qr, fr, lower_bounds)


def _hgrn_prep_bwd(qr, fr, lower_bounds, dqe, dkd, dattn, debl):
    t = qr.shape[0]
    tb = min(t, 256)
    nt, nc = t // tb, tb // HGRN_CHUNK
    hh = HGRN_HEADS

    def body(q_ref, f_ref, lb_ref, dqe_ref, dkd_ref, dat_ref, deb_ref, dq_ref, df_ref, dlb_ref):
        @pl.when(pl.program_id(1) == 0)
        def _():
            dlb_ref[...] = jnp.zeros_like(dlb_ref)

        ch = lambda r: _chunks(r, nc, HGRN_CHUNK)
        _, vjp = jax.vjp(functools.partial(_hgrn_prep, _mm_vjp), ch(q_ref[...]), ch(f_ref[...]), lb_ref[...])
        dq, df, dlb = vjp((ch(dqe_ref[...]), ch(dkd_ref[...]), ch(dat_ref[0]), deb_ref[0][:, 0:1, :]))
        dq_ref[...] = dq.reshape(tb, HEAD_DIM).astype(BF16)
        df_ref[...] = df.reshape(tb, HEAD_DIM).astype(BF16)
        dlb_ref[...] += dlb

    col, lbs, att, ebs = _hgrn_prep_specs(tb)
    return pl.pallas_call(
        body, name="hgrn_prep_bwd", grid=(hh, nt), in_specs=[col, col, lbs, col, col, att, ebs],
        out_specs=[col, col, lbs],
        out_shape=[jax.ShapeDtypeStruct((t, HGRN_WIDTH), BF16)] * 2 + [jax.ShapeDtypeStruct((2, HGRN_WIDTH), F32)],
        compiler_params=_params("parallel", "arbitrary"))(qr, fr, lower_bounds, dqe, dkd, dattn, debl)


def _hgrn_scan_fwd(qe, kd, attn, ebl, iv, z, gn):
    t = qe.shape[0]
    tb = min(t, 128)
    nt, nc = t // tb, tb // HGRN_CHUNK
    hh = HGRN_HEADS

    def body(qe_ref, kd_ref, at_ref, eb_ref, i_ref, z_ref, gn_ref, y_ref, hist_ref, s_ref):
        @pl.when(pl.program_id(0) == 0)
        def _():
            s_ref[...] = jnp.zeros_like(s_ref)

        g = gn_ref[0:1, :]
        state = s_ref[...]
        for c in range(nc):
            rows = pl.ds(c * HGRN_CHUNK, HGRN_CHUNK)
            heads = lambda r: _by_head(r, rows, hh)
            hist_ref[:, c] = state
            y, state = _hgrn_scan(_mm_raw, heads(qe_ref), heads(kd_ref), at_ref[:, rows, :], eb_ref[:, c, 0:1, :],
                                  heads(i_ref), heads(z_ref), g, state)
            _store_heads(y_ref, rows, y.astype(BF16))
        s_ref[...] = state

    row, att, ebs, hist, gns = _scan_specs(tb, hh, HGRN_CHUNK, False, nt)
    return pl.pallas_call(
        body, name="hgrn_scan_fwd", grid=(nt,), in_specs=[row, row, att, ebs, row, row, gns], out_specs=[row, hist],
        out_shape=[jax.ShapeDtypeStruct((t, HGRN_WIDTH), BF16),
                   jax.ShapeDtypeStruct((hh, t // HGRN_CHUNK, HEAD_DIM, HEAD_DIM), F32)],
        scratch_shapes=[pltpu.VMEM((hh, HEAD_DIM, HEAD_DIM), F32)],
        compiler_params=_params("arbitrary"))(qe, kd, attn, ebl, iv, z, gn)


def _hgrn_scan_bwd(qe, kd, attn, ebl, iv, z, gn, hist, dy):
    t = qe.shape[0]
    tb = min(t, 128)
    nt, nc = t // tb, tb // HGRN_CHUNK
    hh = HGRN_HEADS

    def body(qe_ref, kd_ref, at_ref, eb_ref, i_ref, z_ref, gn_ref, hist_ref, dy_ref,
             dqe_ref, dkd_ref, dat_ref, deb_ref, di_ref, dz_ref, dgn_ref, ds_ref):
        @pl.when(pl.program_id(0) == 0)
        def _():
            ds_ref[...] = jnp.zeros_like(ds_ref)
            dgn_ref[...] = jnp.zeros_like(dgn_ref)

        g = gn_ref[0:1, :]
        d_state = ds_ref[...]
        for c in reversed(range(nc)):
            rows = pl.ds(c * HGRN_CHUNK, HGRN_CHUNK)
            heads = lambda r: _by_head(r, rows, hh).astype(F32)
            _, vjp = jax.vjp(functools.partial(_hgrn_scan, _mm_vjp), heads(qe_ref), heads(kd_ref),
                             at_ref[:, rows, :].astype(F32), eb_ref[:, c, 0:1, :], heads(i_ref), heads(z_ref), g,
                             hist_ref[:, c])
            dqe, dkd, dat, deb, di, dz, dgn, d_state = vjp((heads(dy_ref), d_state))
            _store_heads(dqe_ref, rows, dqe)
            _store_heads(dkd_ref, rows, dkd)
            dat_ref[:, rows, :] = dat
            deb_ref[:, c] = jnp.broadcast_to(deb, (hh, SUBLANES, LANES))
            _store_heads(di_ref, rows, di.astype(BF16))
            _store_heads(dz_ref, rows, dz.astype(BF16))
            dgn_ref[0:1, :] += dgn
        ds_ref[...] = d_state

    row, att, ebs, hists, gns = _scan_specs(tb, hh, HGRN_CHUNK, True, nt)
    wide = lambda dt: jax.ShapeDtypeStruct((t, HGRN_WIDTH), dt)
    return pl.pallas_call(
        body, name="hgrn_scan_bwd", grid=(nt,),
        in_specs=[row, row, att, ebs, row, row, gns, hists, row],
        out_specs=[row, row, att, ebs, row, row, gns],
        out_shape=[wide(F32), wide(F32), jax.ShapeDtypeStruct((hh, t, HGRN_CHUNK), F32),
                   jax.ShapeDtypeStruct((hh, t // HGRN_CHUNK, SUBLANES, LANES), F32), wide(BF16), wide(BF16),
                   jax.ShapeDtypeStruct((SUBLANES, LANES), F32)],
        scratch_shapes=[pltpu.VMEM((hh, HEAD_DIM, HEAD_DIM), F32)],
        compiler_params=_params("arbitrary"))(qe, kd, attn, ebl, iv, z, gn, hist, dy)


def _layer_norm(pre, g, b):
    mu = jnp.mean(pre, axis=-1, keepdims=True)
    d = pre - mu
    var = jnp.mean(d * d, axis=-1, keepdims=True)
    return d * lax.rsqrt(var + NORM_EPS) * g + b


def _lnpl_fwd(xin, s, p, wg, wpl, ln_g, ln_b):
    t = xin.shape[0]
    tt = min(t, 256)

    def body(x_ref, s_ref, p_ref, wg_ref, wpl_ref, g_ref, b_ref, o_ref, ob_ref):
        xn = _layer_norm(DEEPNORM_ALPHA * x_ref[...] + s_ref[...], g_ref[...], b_ref[...])
        gate = jax.nn.sigmoid(_mm_raw(xn, wg_ref[...], "nn", False))
        out = xn + _mm_raw(p_ref[...], wpl_ref[...], "nn", False) * gate
        o_ref[...] = out
        ob_ref[...] = out.astype(BF16)

    row = lambda w: pl.BlockSpec((tt, w), lambda i: (i, 0))
    full = lambda a: pl.BlockSpec(a.shape, lambda i: (0, 0))
    return pl.pallas_call(
        body, name="lnpl_fwd", grid=(t // tt,),
        in_specs=[row(D_MODEL), row(D_MODEL), row(PL_DIM), full(wg), full(wpl), full(ln_g), full(ln_b)],
        out_specs=[row(D_MODEL), row(D_MODEL)],
        out_shape=[jax.ShapeDtypeStruct((t, D_MODEL), F32), jax.ShapeDtypeStruct((t, D_MODEL), BF16)],
        compiler_params=_params("parallel"))(xin, s, p, wg, wpl, ln_g, ln_b)


def _lnpl_bwd(xin, s, p, wg, wpl, ln_g, ln_b, upstream, last, name):
    t = xin.shape[0]
    tt = min(t, 256)

    def body(x_ref, s_ref, p_ref, wg_ref, wpl_ref, g_ref, b_ref, up_ref,
             dpre_ref, dwg_ref, dwpl_ref, dg_ref, db_ref, loss_ref):
        @pl.when(pl.program_id(0) == 0)
        def _():
            for r in (dwg_ref, dwpl_ref, dg_ref, db_ref, loss_ref):
                r[...] = jnp.zeros_like(r)

        pre = DEEPNORM_ALPHA * x_ref[...] + s_ref[...]
        xn, ln_vjp = jax.vjp(_layer_norm, pre, g_ref[...], b_ref[...])
        gate = jax.nn.sigmoid(_mm_raw(xn, wg_ref[...], "nn", False))
        plv = _mm_raw(p_ref[...], wpl_ref[...], "nn", False)
        if last:
            err = xn + plv * gate - up_ref[...]
            dout = err * (1.0 / D_MODEL)
            tot = jnp.sum(jnp.sum(err * err, axis=1, keepdims=True), axis=0, keepdims=True) * (0.5 / D_MODEL)
            loss_ref[...] += jnp.broadcast_to(tot, loss_ref.shape)
        else:
            dout = up_ref[...]
        dplv = dout * gate
        dlogits = dout * plv * gate * (1.0 - gate)
        dwg_ref[...] += _mm_raw(xn, dlogits, "tn", False)
        dwpl_ref[...] += _mm_raw(p_ref[...], dplv, "tn", False)
        dxn = dout + _mm_raw(dlogits, wg_ref[...], "nt", False)
        dpre, dg, db = ln_vjp(dxn)
        dpre_ref[...] = dpre
        dg_ref[...] += dg
        db_ref[...] += db

    row = lambda w: pl.BlockSpec((tt, w), lambda i: (i, 0))
    full = lambda shape: pl.BlockSpec(shape, lambda i: (0, 0))
    return pl.pallas_call(
        body, name=name, grid=(t // tt,),
        in_specs=[row(D_MODEL), row(D_MODEL), row(PL_DIM), full(wg.shape), full(wpl.shape), full(ln_g.shape),
                  full(ln_b.shape), row(D_MODEL)],
        out_specs=[row(D_MODEL), full(wg.shape), full(wpl.shape), full(ln_g.shape), full(ln_b.shape),
                   full((SUBLANES, LANES))],
        out_shape=[jax.ShapeDtypeStruct((t, D_MODEL), F32), jax.ShapeDtypeStruct(wg.shape, F32),
                   jax.ShapeDtypeStruct(wpl.shape, F32), jax.ShapeDtypeStruct(ln_g.shape, F32),
                   jax.ShapeDtypeStruct(ln_b.shape, F32), jax.ShapeDtypeStruct((SUBLANES, LANES), F32)],
        compiler_params=_params("arbitrary"))(xin, s, p, wg, wpl, ln_g, ln_b, upstream)


def _pack_tail(dbr, dar):
    nh, t, _ = dbr.shape
    tt = min(t, 512)

    def body(b_ref, a_ref, o_ref):
        lane = lax.broadcasted_iota(jnp.int32, (tt, LANES), 1)
        acc = jnp.zeros((tt, LANES), F32)
        for h in range(nh):
            acc = jnp.where(lane == h, b_ref[h], acc)
            acc = jnp.where(lane == nh + h, a_ref[h], acc)
        o_ref[...] = acc.astype(BF16)

    spec = pl.BlockSpec((nh, tt, LANES), lambda i: (0, i, 0))
    return pl.pallas_call(
        body, name="pack_tail", grid=(t // tt,), in_specs=[spec, spec], out_specs=pl.BlockSpec((tt, LANES), lambda i: (i, 0)),
        out_shape=jax.ShapeDtypeStruct((t, LANES), BF16), compiler_params=_params("parallel"))(dbr, dar)


def _rep_rows(v):
    return jnp.broadcast_to(v.reshape(1, LANES), (SUBLANES, LANES))


def _rep_heads(v):
    return jnp.broadcast_to(v.reshape(-1, 1, 1), (v.shape[0], SUBLANES, LANES))


def _col_range(stacked, lo, hi):
    c = stacked.shape[2]
    parts = [stacked[s, :, max(lo, s * c) - s * c:min(hi, (s + 1) * c) - s * c]
             for s in range(4) if max(lo, s * c) < min(hi, (s + 1) * c)]
    return parts[0] if len(parts) == 1 else jnp.concatenate(parts, axis=1)


def _col_shards(pieces, c):
    shards, offs, o = [], [], 0
    for pc in pieces:
        offs.append(o)
        o += pc.shape[1]
    for s in range(4):
        lo, hi = s * c, (s + 1) * c
        parts = [pc[:, max(lo, o) - o:min(hi, o + pc.shape[1]) - o] for pc, o in zip(pieces, offs)
                 if max(lo, o) < min(hi, o + pc.shape[1])]
        shards.append(parts[0] if len(parts) == 1 else jnp.concatenate(parts, axis=1))
    return jnp.stack(shards)


def _local_step(x, p, target, w, late_weights, odd_grads_ready, start_token):
    a = DEEPNORM_ALPHA
    nh = GDN_HEADS
    xb = (x + start_token).astype(BF16)
    wie = w["w_in_even"]
    w_a, w_qkv, w_zb = _col_range(wie, 0, 4096), _col_range(wie, 4096, 7168), _col_range(wie, 7168, 8192)
    w_tail = jnp.pad(_col_range(wie, 8192, 8192 + 2 * nh), ((0, 0), (0, LANES - 2 * nh)))
    conv_a_w, conv_b_w = w["conv_a_w"], w["conv_b_w"]
    ln_g0, ln_b0, ln_g1, ln_b1 = (v.reshape(1, D_MODEL) for v in (w["ln_g"][0], w["ln_b"][0], w["ln_g"][1], w["ln_b"][1]))
    alog, dtb = _rep_heads(w["a_log"].reshape(nh)), _rep_heads(w["dt_bias"].reshape(nh))
    gdn_g, hgrn_g = _rep_rows(w["gdn_norm_g"]), _rep_rows(w["hgrn_norm_g"])

    proj_a = _matmul(xb, w_a, name="fwd_proj_a")
    proj_qkv = _matmul(xb, w_qkv, name="fwd_proj_qkv")
    proj_zb = _matmul(xb, w_zb, name="fwd_proj_zb")
    proj_tail = _matmul(xb, w_tail, name="fwd_proj_tail")
    rep = lambda cols: jnp.broadcast_to(cols.T[:, :, None], (nh, cols.shape[0], LANES))
    braw, araw = rep(proj_tail[:, :nh]), rep(proj_tail[:, nh:2 * nh])
    y_a = _conv_a_fwd(proj_a, conv_a_w)
    qkv_act = _conv_b_fwd(proj_qkv, conv_b_w)
    *gdn_pre, gdn_inv = _gdn_prep_fwd(qkv_act, braw, araw, alog, dtb)
    y_b, gdn_hist = _gdn_scan_fwd(*gdn_pre, proj_zb, gdn_g)
    w = {**w, **late_weights(y_b)}
    woe, wio, woo = w["w_out_even"], w["w_in_odd"], w["w_out_odd"]
    s0 = _matmul(y_b, woe[1024:], name="fwd_out_even_b", add=_matmul(y_a, woe[:1024], name="fwd_out_even_a"))
    x1, x1b = _lnpl_fwd(x, s0, p[0], w["w_pl_gate"][0], w["w_pl"][0], ln_g0, ln_b0)
    proj_o = [_matmul(x1b, wio[j], name=f"fwd_proj_odd{j}") for j in range(4)]
    hgrn_pre = _hgrn_prep_fwd(proj_o[0], proj_o[1], w["lower_bounds"])
    y_o, hgrn_hist = _hgrn_scan_fwd(*hgrn_pre, proj_o[2], proj_o[3], hgrn_g)
    s1 = _matmul(y_o, woo, name="fwd_out_odd")

    g = {}
    dpre1, dwg1, dwpl1, dlng1, dlnb1, loss = _lnpl_bwd(x1, s1, p[1], w["w_pl_gate"][1], w["w_pl"][1], ln_g1, ln_b1,
                                                     target, True, "lnpl_bwd_odd")
    dy_o = _matmul(dpre1, woo, tb=True, name="bwd_out_odd_dx")
    g["w_out_odd"] = _matmul(y_o, dpre1, ta=True, name="bwd_out_odd_dw")
    dqe, dkd, dat, deb, di, dz, dhg = _hgrn_scan_bwd(*hgrn_pre, proj_o[2], proj_o[3], hgrn_g, hgrn_hist, dy_o)
    dq, df, dlb = _hgrn_prep_bwd(proj_o[0], proj_o[1], w["lower_bounds"], dqe, dkd, dat, deb)
    dx1 = dpre1
    scale = a
    dws = []
    for j, dj in enumerate((dq, df, di, dz)):
        dx1 = _matmul(dj, wio[j], tb=True, add=dx1, add_scale=scale, name=f"bwd_proj_odd_dx{j}")
        scale = 1.0
        dws.append(_matmul(x1b, dj, ta=True, name=f"bwd_proj_odd_dw{j}"))
    g["w_in_odd"] = jnp.stack(dws)
    g["hgrn_norm_g"] = dhg[0:1]
    g["lower_bounds"] = dlb
    g["w_pl_gate1"], g["w_pl1"] = dwg1, dwpl1
    token = odd_grads_ready({n: g[n] for n in ("w_in_odd", "w_out_odd", "w_pl_gate1", "w_pl1")})

    dpre0, dwg0, dwpl0, dlng0, dlnb0, _ = _lnpl_bwd(x, s0, p[0], w["w_pl_gate"][0], w["w_pl"][0], ln_g0 + token,
                                                  ln_b0, dx1, False, "lnpl_bwd_even")
    g["w_pl_gate0"], g["w_pl0"] = dwg0, dwpl0
    g["ln_g"] = jnp.concatenate([dlng0, dlng1], axis=0)
    g["ln_b"] = jnp.concatenate([dlnb0, dlnb1], axis=0)
    dy_a = _matmul(dpre0, woe[:1024], tb=True, name="bwd_out_even_dxa")
    dy_b = _matmul(dpre0, woe[1024:], tb=True, name="bwd_out_even_dxb")
    g["w_out_even"] = jnp.concatenate([_matmul(y_a, dpre0, ta=True, name="bwd_out_even_dwa"),
                                       _matmul(y_b, dpre0, ta=True, name="bwd_out_even_dwb")], axis=0)
    du, dw, dqd, dkd, dat, deg, dzb, dgn = _gdn_scan_bwd(*gdn_pre, proj_zb, gdn_g, gdn_hist, dy_b)
    dqa, dka, dva, dbr, dar, dal, ddt = _gdn_prep_bwd(qkv_act, braw, araw, alog, dtb, gdn_inv, du, dw, dqd, dkd, dat, deg)
    g["a_log"] = dal[:, 0, 0].reshape(1, nh)
    g["dt_bias"] = ddt[:, 0, 0].reshape(1, nh)
    g["gdn_norm_g"] = dgn[0:1]
    d_pre_qkv, dwb = [], []
    for j, dj in enumerate((dqa, dka, dva)):
        dpj, dwj = _conv_b_bwd(proj_qkv, conv_b_w, dj, j, f"conv_b_bwd{j}")
        d_pre_qkv.append(dpj)
        dwb.append(dwj[:4])
    g["conv_b_w"] = jnp.concatenate(dwb, axis=1)
    d_a, dwa = _conv_a_bwd(proj_a, conv_a_w, dy_a)
    g["conv_a_w"] = dwa[:3]
    d_tail = _pack_tail(dbr, dar)
    pieces = [(d_a, w_a), (d_pre_qkv[0], w_qkv[:, :1024]), (d_pre_qkv[1], w_qkv[:, 1024:2048]),
              (d_pre_qkv[2], w_qkv[:, 2048:]), (dzb, w_zb), (d_tail, w_tail)]
    dx = dpre0
    scale = a
    dws = []
    for j, (dj, wj) in enumerate(pieces):
        dx = _matmul(dj, wj, tb=True, add=dx, add_scale=scale, name=f"bwd_proj_even_dx{j}")
        scale = 1.0
        dws.append(_matmul(xb, dj, ta=True, name=f"bwd_proj_even_dw{j}"))
    dws[-1] = dws[-1][:, :2 * nh]
    g["w_in_even"] = _col_shards(dws, wie.shape[2])
    return loss, dx, g


def _adamw(w, g, m, v, name):
    lead, rows, cols = w.shape
    tr = rows if rows <= 256 else 256
    assert rows % tr == 0, (name, rows)

    def body(w_ref, g_ref, m_ref, v_ref, d_ref, nm_ref, nv_ref):
        gg = g_ref[...]
        nm = ADAM_B1 * m_ref[...] + (1.0 - ADAM_B1) * gg
        nv = ADAM_B2 * v_ref[...] + (1.0 - ADAM_B2) * jnp.square(gg)
        m_hat = nm / (1.0 - ADAM_B1 ** ADAM_STEP)
        v_hat = nv / (1.0 - ADAM_B2 ** ADAM_STEP)
        d_ref[...] = -ADAM_LR * (m_hat / (jnp.sqrt(v_hat) + ADAM_EPS) + ADAM_WD * w_ref[...])
        nm_ref[...] = nm
        nv_ref[...] = nv

    spec = pl.BlockSpec((1, tr, cols), lambda l, i: (l, i, 0))
    return pl.pallas_call(
        body, name=name, grid=(lead, rows // tr), in_specs=[spec] * 4, out_specs=[spec] * 3,
        out_shape=[jax.ShapeDtypeStruct(w.shape, F32)] * 3, compiler_params=_params("parallel", "parallel"))(w, g, m, v)


MESH = pl.DeviceIdType.MESH
N_DEV = 8
HBM_SPEC = pl.BlockSpec(memory_space=pltpu.HBM)
VMEM_SPEC = pl.BlockSpec(memory_space=pltpu.VMEM)


def _coords():
    return lax.axis_index("x"), lax.axis_index("y"), lax.axis_index("c")


def _flip(v, bit):
    return 1 - v if bit else v


def _remote(src, dst, send_sem, recv_sem, dev):
    return pltpu.make_async_remote_copy(src_ref=src, dst_ref=dst, send_sem=send_sem, recv_sem=recv_sem,
                                        device_id=dev, device_id_type=MESH)


def _exchange_small(buf, reduce, name):
    rows = buf.shape[0]

    def body(in_ref, out_ref, slots, send_sems, recv_sems):
        x, y, c = _coords()
        me = 4 * x + 2 * y + c
        slots[me] = in_ref[...]
        peer = lambda k: (_flip(x, (k >> 2) & 1), _flip(y, (k >> 1) & 1), _flip(c, k & 1))
        sends = []
        for k in range(1, N_DEV):
            cp = _remote(in_ref, slots.at[me], send_sems.at[k - 1], recv_sems.at[k - 1], peer(k))
            cp.start()
            sends.append(cp)
        for k in range(1, N_DEV):
            px, py, pc = peer(k)
            _remote(in_ref, slots.at[4 * px + 2 * py + pc], send_sems.at[k - 1], recv_sems.at[k - 1], peer(k)).wait_recv()
        for cp in sends:
            cp.wait_send()
        if reduce:
            acc = slots[0]
            for d in range(1, N_DEV):
                acc = acc + slots[d]
            out_ref[...] = acc
        else:
            out_ref[...] = slots[...]

    out_shape = (rows, LANES) if reduce else (N_DEV, rows, LANES)
    return pl.pallas_call(
        body, name=name, in_specs=[VMEM_SPEC], out_specs=VMEM_SPEC, out_shape=jax.ShapeDtypeStruct(out_shape, F32),
        scratch_shapes=[pltpu.VMEM((N_DEV, rows, LANES), F32), pltpu.SemaphoreType.DMA((N_DEV - 1,)),
                        pltpu.SemaphoreType.DMA((N_DEV - 1,))])(buf)


def _half_rows(half, which):
    return pl.ds(pl.multiple_of(which * half, 16), half)


def _other_chip(x, y, k):
    return _flip(x, (k >> 1) & 1), _flip(y, k & 1)


def _gather_shards(shards):
    n = len(shards)

    def body(*refs):
        ins, outs = refs[:n], refs[n:2 * n]
        ici_s, ici_r, d2d_s, d2d_r = refs[2 * n:]
        x, y, c = _coords()
        chip = 2 * x + y
        sends = []
        for i in range(n):
            half = ins[i].shape[0] // 2
            for k in (1, 2, 3):
                ox, oy = _other_chip(x, y, k)
                cp = _remote(ins[i].at[_half_rows(half, c)], outs[i].at[chip, _half_rows(half, c)],
                             ici_s.at[3 * i + k - 1], ici_r.at[3 * i + k - 1], (ox, oy, c))
                cp.start()
                sends.append(cp)
        for k in (1, 2, 3):
            ox, oy = _other_chip(x, y, k)
            for i in range(n):
                half = ins[i].shape[0] // 2
                piece = outs[i].at[2 * ox + oy, _half_rows(half, c)]
                _remote(piece, piece, ici_s.at[3 * i + k - 1], ici_r.at[3 * i + k - 1], (ox, oy, c)).wait_recv()
                cp = _remote(piece, piece, d2d_s.at[3 * i + k - 1], d2d_r.at[3 * i + k - 1], (x, y, 1 - c))
                cp.start()
                sends.append(cp)
        for k in (1, 2, 3):
            ox, oy = _other_chip(x, y, k)
            for i in range(n):
                half = ins[i].shape[0] // 2
                piece = outs[i].at[2 * ox + oy, _half_rows(half, 1 - c)]
                _remote(piece, piece, d2d_s.at[3 * i + k - 1], d2d_r.at[3 * i + k - 1], (x, y, 1 - c)).wait_recv()
        for cp in sends:
            cp.wait_send()

    return pl.pallas_call(
        body, name="gather_weights", in_specs=[HBM_SPEC] * n, out_specs=[HBM_SPEC] * n,
        out_shape=[jax.ShapeDtypeStruct((4,) + s.shape, s.dtype) for s in shards],
        scratch_shapes=[pltpu.SemaphoreType.DMA((3 * n,))] * 4)(*shards)


SEM_SPEC = pl.BlockSpec(memory_space=pltpu.SEMAPHORE)
DATAFLOW = pltpu.SideEffectType.DATAFLOW_SIDE_EFFECTING


def _ici_piece(srcs, lands, send_sems, recv_sems, i, k, x, y, c):
    half = srcs[i].shape[0] // 2
    ox, oy = _other_chip(x, y, k)
    return _remote(srcs[i].at[_half_rows(half, c)], lands[i].at[2 * x + y, _half_rows(half, c)],
                   send_sems.at[3 * i + k - 1], recv_sems.at[3 * i + k - 1], (ox, oy, c)), (ox, oy)


def _gather_start(shards):
    n = len(shards)

    def body(*refs):
        srcs, lands = refs[:n], refs[n:2 * n]
        send_sems, recv_sems = refs[2 * n], refs[2 * n + 1]
        token = refs[-1]
        x, y, c = _coords()
        for i in range(n):
            for k in (1, 2, 3):
                _ici_piece(srcs, lands, send_sems, recv_sems, i, k, x, y, c)[0].start()
        token[...] = jnp.zeros_like(token)

    hbm = lambda a: pltpu.with_memory_space_constraint(a, pltpu.HBM)
    lands = [lax.empty((4,) + s.shape, s.dtype) for s in shards]
    out = pl.pallas_call(
        body, name="gather_rest_start",
        out_shape=(pltpu.SemaphoreType.DMA((3 * n,)), pltpu.SemaphoreType.DMA((3 * n,)),
                   *[pltpu.HBM(s.shape, s.dtype) for s in shards], *[pltpu.HBM(a.shape, a.dtype) for a in lands],
                   jax.ShapeDtypeStruct((SUBLANES, LANES), F32)),
        in_specs=[HBM_SPEC] * (2 * n), out_specs=(SEM_SPEC, SEM_SPEC, *[HBM_SPEC] * (2 * n), VMEM_SPEC),
        input_output_aliases={i: 2 + i for i in range(2 * n)},
        compiler_params=pltpu.CompilerParams(has_side_effects=DATAFLOW))(*[hbm(s) for s in shards], *[hbm(a) for a in lands])
    return out[0], out[1], out[2:2 + n], out[2 + n:2 + 2 * n], out[-1]


def _gather_wait(send_sems, recv_sems, srcs, lands, after):
    n = len(srcs)

    def body(*refs):
        src_refs, land_refs = refs[:n], refs[n:2 * n]
        send_sems, recv_sems = refs[2 * n], refs[2 * n + 1]
        x, y, c = _coords()
        for i in range(n):
            half = src_refs[i].shape[0] // 2
            for k in (1, 2, 3):
                cp, (ox, oy) = _ici_piece(src_refs, land_refs, send_sems, recv_sems, i, k, x, y, c)
                cp.wait_send()
                piece = land_refs[i].at[2 * ox + oy, _half_rows(half, c)]
                _remote(piece, piece, send_sems.at[3 * i + k - 1], recv_sems.at[3 * i + k - 1], (ox, oy, c)).wait_recv()

    out = pl.pallas_call(
        body, name="gather_rest_wait",
        out_shape=(*[pltpu.HBM(s.shape, s.dtype) for s in srcs], *[pltpu.HBM(a.shape, a.dtype) for a in lands]),
        in_specs=[HBM_SPEC] * (2 * n) + [SEM_SPEC, SEM_SPEC, pl.BlockSpec(memory_space=pl.ANY)],
        out_specs=tuple([HBM_SPEC] * (2 * n)), input_output_aliases={i: i for i in range(2 * n)},
        compiler_params=pltpu.CompilerParams(has_side_effects=DATAFLOW))(*srcs, *lands, send_sems, recv_sems, after)
    return out[n:]


def _gather_forward(lands):
    n = len(lands)

    def body(*refs):
        ins, outs = refs[:n], refs[n:2 * n]
        send_sems, recv_sems = refs[2 * n:]
        x, y, c = _coords()
        sends = []
        for i in range(n):
            half = ins[i].shape[1] // 2
            for k in (1, 2, 3):
                ox, oy = _other_chip(x, y, k)
                cp = _remote(ins[i].at[2 * ox + oy, _half_rows(half, c)], outs[i].at[2 * ox + oy, _half_rows(half, c)],
                             send_sems.at[3 * i + k - 1], recv_sems.at[3 * i + k - 1], (x, y, 1 - c))
                cp.start()
                sends.append(cp)
        for i in range(n):
            half = ins[i].shape[1] // 2
            for k in (1, 2, 3):
                ox, oy = _other_chip(x, y, k)
                piece = outs[i].at[2 * ox + oy, _half_rows(half, 1 - c)]
                _remote(piece, piece, send_sems.at[3 * i + k - 1], recv_sems.at[3 * i + k - 1], (x, y, 1 - c)).wait_recv()
        for cp in sends:
            cp.wait_send()

    return pl.pallas_call(
        body, name="gather_rest_forward", in_specs=[HBM_SPEC] * n, out_specs=[HBM_SPEC] * n,
        out_shape=[jax.ShapeDtypeStruct(a.shape, a.dtype) for a in lands],
        input_output_aliases={i: i for i in range(n)},
        scratch_shapes=[pltpu.SemaphoreType.DMA((3 * n,))] * 2)(*lands)


def _rs_sibling_swap(g4s, name):
    n = len(g4s)

    def body(*refs):
        ins, outs = refs[:n], refs[n:2 * n]
        send_sems, recv_sems = refs[2 * n:]
        x, y, c = _coords()
        sends = []
        for i in range(n):
            half = ins[i].shape[1] // 2
            for s in range(4):
                cp = _remote(ins[i].at[s, _half_rows(half, 1 - c)], outs[i].at[s], send_sems.at[4 * i + s],
                             recv_sems.at[4 * i + s], (x, y, 1 - c))
                cp.start()
                sends.append(cp)
        for cp in sends:
            cp.wait_recv()
        for cp in sends:
            cp.wait_send()

    return pl.pallas_call(
        body, name=name, in_specs=[HBM_SPEC] * n, out_specs=[HBM_SPEC] * n,
        out_shape=[jax.ShapeDtypeStruct((4, g.shape[1] // 2, g.shape[2]), g.dtype) for g in g4s],
        scratch_shapes=[pltpu.SemaphoreType.DMA((4 * n,))] * 2)(*g4s)


def _rs_add_sibling(g4, got, c_idx, name):
    _, rows, cols = g4.shape
    half = rows // 2
    tr = min(half, 256)
    nb = half // tr

    def body(c_ref, a_ref, b_ref, o_ref, ob_ref):
        total = a_ref[...] + b_ref[...]
        o_ref[...] = total
        ob_ref[...] = total.astype(BF16)

    blk = (1, tr, cols)
    out = pl.BlockSpec(blk, lambda s, i, c_ref: (s, i, 0))
    grid_spec = pltpu.PrefetchScalarGridSpec(
        num_scalar_prefetch=1, grid=(4, nb),
        in_specs=[pl.BlockSpec(blk, lambda s, i, c_ref: (s, c_ref[0] * nb + i, 0)), out],
        out_specs=[out, out])
    return pl.pallas_call(
        body, name=name, grid_spec=grid_spec,
        out_shape=[jax.ShapeDtypeStruct(got.shape, F32), jax.ShapeDtypeStruct(got.shape, BF16)],
        compiler_params=_params("parallel", "parallel"))(c_idx, g4, got)


def _rs_chip_scatter(p4s):
    n = len(p4s)

    def body(*refs):
        ins, outs = refs[:n], refs[n:2 * n]
        send_sems, recv_sems = refs[2 * n:]
        x, y, c = _coords()
        sends = []
        for i in range(n):
            for k in (1, 2, 3):
                ox, oy = _other_chip(x, y, k)
                cp = _remote(ins[i].at[2 * ox + oy], outs[i].at[k - 1], send_sems.at[3 * i + k - 1],
                             recv_sems.at[3 * i + k - 1], (ox, oy, c))
                cp.start()
                sends.append(cp)
        for cp in sends:
            cp.wait_recv()
        for cp in sends:
            cp.wait_send()

    return pl.pallas_call(
        body, name="rs_chip_scatter", in_specs=[HBM_SPEC] * n, out_specs=[HBM_SPEC] * n,
        out_shape=[jax.ShapeDtypeStruct((3,) + p.shape[1:], p.dtype) for p in p4s],
        scratch_shapes=[pltpu.SemaphoreType.DMA((3 * n,))] * 2)(*p4s)


def _rs_add_chips(p4, got3, idx, name):
    _, half, cols = p4.shape
    tr = min(half, 256)
    nb = half // tr

    def body(idx_ref, p_ref, a_ref, b_ref, c_ref, o_ref):
        o_ref[...] = ((p_ref[0] + a_ref[0].astype(F32)) + b_ref[0].astype(F32)) + c_ref[0].astype(F32)

    blk = (1, tr, cols)
    grid_spec = pltpu.PrefetchScalarGridSpec(
        num_scalar_prefetch=1, grid=(nb,),
        in_specs=[pl.BlockSpec(blk, lambda i, idx_ref: (idx_ref[0], i, 0))]
        + [pl.BlockSpec(blk, functools.partial(lambda k, i, idx_ref: (k, i, 0), k)) for k in range(3)],
        out_specs=pl.BlockSpec((tr, cols), lambda i, idx_ref: (idx_ref[1] * nb + i, 0)))
    return pl.pallas_call(body, name=name, grid_spec=grid_spec, out_shape=jax.ShapeDtypeStruct((2 * half, cols), F32),
                          compiler_params=_params("parallel"))(idx, p4, got3, got3, got3)


def _rs_share_halves(bufs):
    n = len(bufs)

    def body(*refs):
        ins, outs = refs[:n], refs[n:2 * n]
        send_sems, recv_sems = refs[2 * n:]
        x, y, c = _coords()
        sends = []
        for i in range(n):
            half = ins[i].shape[0] // 2
            cp = _remote(ins[i].at[_half_rows(half, c)], outs[i].at[_half_rows(half, c)], send_sems.at[i],
                         recv_sems.at[i], (x, y, 1 - c))
            cp.start()
            sends.append(cp)
        for i in range(n):
            half = ins[i].shape[0] // 2
            _remote(ins[i].at[_half_rows(half, c)], outs[i].at[_half_rows(half, 1 - c)], send_sems.at[i],
                    recv_sems.at[i], (x, y, 1 - c)).wait_recv()
        for cp in sends:
            cp.wait_send()

    return pl.pallas_call(
        body, name="rs_share_halves", in_specs=[HBM_SPEC] * n, out_specs=[HBM_SPEC] * n,
        out_shape=[jax.ShapeDtypeStruct(b.shape, b.dtype) for b in bufs],
        input_output_aliases={i: i for i in range(n)},
        scratch_shapes=[pltpu.SemaphoreType.DMA((n,))] * 2)(*bufs)


def _scatter_piece(srcs, lands, send_sems, recv_sems, i, k, x, y, c):
    ox, oy = _other_chip(x, y, k)
    return _remote(srcs[i].at[2 * ox + oy], lands[i].at[k - 1], send_sems.at[3 * i + k - 1],
                   recv_sems.at[3 * i + k - 1], (ox, oy, c))


def _rs_scatter_start(p4s):
    n = len(p4s)

    def body(*refs):
        srcs, lands = refs[:n], refs[n:2 * n]
        send_sems, recv_sems = refs[2 * n], refs[2 * n + 1]
        token = refs[-1]
        x, y, c = _coords()
        for i in range(n):
            for k in (1, 2, 3):
                _scatter_piece(srcs, lands, send_sems, recv_sems, i, k, x, y, c).start()
        token[...] = jnp.zeros_like(token)

    hbm = lambda a: pltpu.with_memory_space_constraint(a, pltpu.HBM)
    lands = [lax.empty((3,) + p.shape[1:], p.dtype) for p in p4s]
    out = pl.pallas_call(
        body, name="rs_scatter_start",
        out_shape=(pltpu.SemaphoreType.DMA((3 * n,)), pltpu.SemaphoreType.DMA((3 * n,)),
                   *[pltpu.HBM(p.shape, p.dtype) for p in p4s], *[pltpu.HBM(a.shape, a.dtype) for a in lands],
                   jax.ShapeDtypeStruct((SUBLANES, LANES), F32)),
        in_specs=[HBM_SPEC] * (2 * n), out_specs=(SEM_SPEC, SEM_SPEC, *[HBM_SPEC] * (2 * n), VMEM_SPEC),
        input_output_aliases={i: 2 + i for i in range(2 * n)},
        compiler_params=pltpu.CompilerParams(has_side_effects=DATAFLOW))(*[hbm(p) for p in p4s], *[hbm(a) for a in lands])
    return out[0], out[1], out[2:2 + n], out[2 + n:2 + 2 * n], out[-1]


def _rs_scatter_wait(send_sems, recv_sems, srcs, lands, after):
    n = len(srcs)

    def body(*refs):
        src_refs, land_refs = refs[:n], refs[n:2 * n]
        send_sems, recv_sems = refs[2 * n], refs[2 * n + 1]
        x, y, c = _coords()
        for i in range(n):
            for k in (1, 2, 3):
                cp = _scatter_piece(src_refs, land_refs, send_sems, recv_sems, i, k, x, y, c)
                cp.wait_send()
                cp.wait_recv()

    out = pl.pallas_call(
        body, name="rs_scatter_wait",
        out_shape=(*[pltpu.HBM(s.shape, s.dtype) for s in srcs], *[pltpu.HBM(a.shape, a.dtype) for a in lands]),
        in_specs=[HBM_SPEC] * (2 * n) + [SEM_SPEC, SEM_SPEC, pl.BlockSpec(memory_space=pl.ANY)],
        out_specs=tuple([HBM_SPEC] * (2 * n)), input_output_aliases={i: i for i in range(2 * n)},
        compiler_params=pltpu.CompilerParams(has_side_effects=DATAFLOW))(*srcs, *lands, send_sems, recv_sems, after)
    return out[n:]


def _rs_front(g4s, names, tag):
    c_idx = jnp.stack([lax.axis_index("c")]).astype(jnp.int32)
    got = _rs_sibling_swap(g4s, f"rs_sibling_swap_{tag}")
    return [_rs_add_sibling(g, s, c_idx, f"rs_add_sibling_{nm}") for g, s, nm in zip(g4s, got, names)]


def _rs_back(p4s, got3, names):
    x, y, c = _coords()
    idx = jnp.stack([2 * x + y, c]).astype(jnp.int32)
    return [_rs_add_chips(p, t, idx, f"rs_add_chips_{nm}") for (p, _), t, nm in zip(p4s, got3, names)]


def _cols_split(full):
    r, c4 = full.shape
    return full.reshape(r, 4, c4 // 4).transpose(1, 0, 2)


_BIG = {
    "w_in_even": ((1024, 2052), lambda s: s),
    "w_out_even": ((512, 1024), lambda s: s.reshape(2048, 1024)),
    "w_in_odd": ((1024, 2048), lambda s: s),
    "w_out_odd": ((512, 1024), lambda s: s.reshape(2048, 1024)),
    "w_pl": ((512, 256), lambda s: s.reshape(4, 2, 256, 256).transpose(1, 2, 0, 3).reshape(2, 256, 1024)),
    "w_pl_gate": ((512, 1024), lambda s: s.reshape(4, 2, 256, 1024).transpose(1, 0, 2, 3).reshape(2, 1024, 1024)),
}


_RS_ODD = {
    "w_in_odd": lambda f: f,
    "w_out_odd": lambda f: f.reshape(4, 512, 1024),
    "w_pl_gate1": lambda f: f.reshape(4, 256, 1024),
    "w_pl1": _cols_split,
}
_RS_EVEN = {
    "w_in_even": lambda f: f,
    "w_out_even": lambda f: f.reshape(4, 512, 1024),
    "w_pl_gate0": lambda f: f.reshape(4, 256, 1024),
    "w_pl0": _cols_split,
}


def _size(shape):
    n = 1
    for d in shape:
        n *= d
    return n


_SMALL = {"a_log": (1, 8), "dt_bias": (1, 8), "gdn_norm_g": (1, 128), "hgrn_norm_g": (1, 128),
          "lower_bounds": (2, 2048), "ln_g": (2, 1024), "ln_b": (2, 1024), "conv_a_w": (3, 1024), "conv_b_w": (4, 3072)}
_CONV_SHARD = {"conv_a_w": (3, 256), "conv_b_w": (4, 768)}


def _pack_small(parts, shapes, head_rows=0):
    rows = []
    for n, shape in shapes.items():
        v = parts[n].reshape(-1)
        rows.append(jnp.pad(v, (0, -v.shape[0] % LANES)).reshape(-1, LANES))
    buf = jnp.concatenate(rows, axis=0)
    return jnp.pad(buf, ((head_rows, -(buf.shape[0] + head_rows) % SUBLANES), (0, 0)))


def _unpack_small(buf, shapes, head_rows=0):
    out, off = {}, head_rows
    for n, shape in shapes.items():
        nrow = -(-_size(shape) // LANES)
        out[n] = buf[off:off + nrow].reshape(-1)[:_size(shape)].reshape(shape)
        off += nrow
    return out


_WEIGHTS = ["w_in_even", "conv_a_w", "conv_b_w", "a_log", "dt_bias", "gdn_norm_g", "w_out_even", "w_in_odd",
            "lower_bounds", "hgrn_norm_g", "w_out_odd", "ln_g", "ln_b", "w_pl", "w_pl_gate"]


def kernel(x, p, w_in_even, conv_a_w, conv_b_w, a_log, dt_bias, gdn_norm_g, w_out_even, w_in_odd, lower_bounds, hgrn_norm_g, w_out_odd, ln_g, ln_b, w_pl, w_pl_gate, loss_target, m_w_in_even, m_conv_a_w, m_conv_b_w, m_a_log, m_dt_bias, m_gdn_norm_g, m_w_out_even, m_w_in_odd, m_lower_bounds, m_hgrn_norm_g, m_w_out_odd, m_ln_g, m_ln_b, m_w_pl, m_w_pl_gate, v_w_in_even, v_conv_a_w, v_conv_b_w, v_a_log, v_dt_bias, v_gdn_norm_g, v_w_out_even, v_w_in_odd, v_lower_bounds, v_hgrn_norm_g, v_w_out_odd, v_ln_g, v_ln_b, v_w_pl, v_w_pl_gate):
    w = dict(zip(_WEIGHTS, (w_in_even, conv_a_w, conv_b_w, a_log, dt_bias, gdn_norm_g, w_out_even, w_in_odd,
                            lower_bounds, hgrn_norm_g, w_out_odd, ln_g, ln_b, w_pl, w_pl_gate)))
    m = dict(zip(_WEIGHTS, (m_w_in_even, m_conv_a_w, m_conv_b_w, m_a_log, m_dt_bias, m_gdn_norm_g, m_w_out_even,
                            m_w_in_odd, m_lower_bounds, m_hgrn_norm_g, m_w_out_odd, m_ln_g, m_ln_b, m_w_pl, m_w_pl_gate)))
    v = dict(zip(_WEIGHTS, (v_w_in_even, v_conv_a_w, v_conv_b_w, v_a_log, v_dt_bias, v_gdn_norm_g, v_w_out_even,
                            v_w_in_odd, v_lower_bounds, v_hgrn_norm_g, v_w_out_odd, v_ln_g, v_ln_b, v_w_pl, v_w_pl_gate)))
    chip = 2 * lax.axis_index("x") + lax.axis_index("y")

    names = list(_BIG)
    shard_shapes = {n: _BIG[n][0] for n in names}
    shards = {n: w[n].reshape(shard_shapes[n]).astype(BF16) for n in names}
    whole = lambda n, stacked: _BIG[n][1](lax.dynamic_update_slice(stacked, shards[n][None], (chip, 0, 0)))
    early, late = names[:1], names[1:]
    full = {n: whole(n, ga) for n, ga in zip(early, _gather_shards([shards[n] for n in early]))}
    send_sems, recv_sems, srcs, lands, token = _gather_start([shards[n] for n in late])

    def late_weights(after):
        landed = _gather_forward(_gather_wait(send_sems, recv_sems, srcs, lands, after))
        return {n: whole(n, ga) for n, ga in zip(late, landed)}

    conv_mine = _pack_small({n: w[n] for n in _CONV_SHARD}, _CONV_SHARD)
    conv_all = _exchange_small(conv_mine, False, "gather_conv")
    conv_by_chip = [_unpack_small(conv_all[2 * s], _CONV_SHARD) for s in range(4)]
    for n in _CONV_SHARD:
        full[n] = jnp.concatenate([conv_by_chip[s][n] for s in range(4)], axis=1)
    for n in _SMALL:
        if n not in _CONV_SHARD:
            full[n] = w[n]

    odd = {}

    def odd_grads_ready(grads):
        odd["p4s"] = _rs_front([_RS_ODD[n](grads[n]) for n in _RS_ODD], list(_RS_ODD), "odd")
        odd["sems"] = _rs_scatter_start([pb for _, pb in odd["p4s"]])
        return odd["sems"][4][0, 0]

    loss_part, dx, g = _local_step(x[0], p[:, 0], loss_target[0], full, late_weights, odd_grads_ready, token[0, 0])

    bufs = _rs_back(odd["p4s"], _rs_scatter_wait(*odd["sems"][:4], dx), list(_RS_ODD))
    p4s = _rs_front([_RS_EVEN[n](g[n]) for n in _RS_EVEN], list(_RS_EVEN), "even")
    bufs += _rs_back(p4s, _rs_chip_scatter([pb for _, pb in p4s]), list(_RS_EVEN))
    summed = dict(zip(list(_RS_ODD) + list(_RS_EVEN), _rs_share_halves(bufs)))
    g_big = {n: summed[n] for n in names if n in summed}
    g_big["w_pl"] = jnp.stack([summed["w_pl0"], summed["w_pl1"]])
    g_big["w_pl_gate"] = jnp.stack([summed["w_pl_gate0"], summed["w_pl_gate1"]])
    small_sum = _exchange_small(jnp.concatenate([loss_part, _pack_small(g, _SMALL)], axis=0), True, "reduce_small")
    loss = small_sum[0, 0]
    g_small = _unpack_small(small_sum, _SMALL, head_rows=SUBLANES)
    for n, (rows, cols) in _CONV_SHARD.items():
        g_small[n] = lax.dynamic_slice_in_dim(g_small[n], chip * cols, cols, axis=1)

    grads, delta, new_m, new_v = {}, {}, {}, {}
    for n in _BIG:
        grads[n] = g_big[n].reshape(w[n].shape)
        delta[n], new_m[n], new_v[n] = _adamw(w[n], grads[n], m[n], v[n], f"adamw_{n}")
    own = {n: (_CONV_SHARD[n] if n in _CONV_SHARD else _SMALL[n]) for n in _SMALL}
    packs = [_pack_small({n: src[n] for n in _SMALL}, own)[None] for src in (w, g_small, m, v)]
    outs = [_unpack_small(t[0], own) for t in _adamw(*packs, "adamw_small")]
    for n in _SMALL:
        grads[n] = g_small[n].reshape(w[n].shape)
        delta[n], new_m[n], new_v[n] = (t[n].reshape(w[n].shape) for t in outs)
    return (loss, dx[None], *[grads[n] for n in _WEIGHTS], *[delta[n] for n in _WEIGHTS],
            *[new_m[n] for n in _WEIGHTS], *[new_v[n] for n in _WEIGHTS])
```

```python
import functools

import jax
import jax.numpy as jnp
from jax import lax
from jax.experimental import pallas as pl
from jax.experimental.pallas import tpu as pltpu

F32 = jnp.float32
BF16 = jnp.bfloat16
HI = lax.Precision.HIGHEST

D_MODEL = 1024
PL_DIM = 256
GDN_HEADS = 8
HEAD_DIM = 128
GDN_CHUNK = 64
HGRN_HEADS = 16
HGRN_CHUNK = 32
HGRN_WIDTH = 2048
DEEPNORM_ALPHA = 4.0 ** 0.25
NORM_EPS = 1e-5
ADAM_LR, ADAM_B1, ADAM_B2, ADAM_EPS, ADAM_WD, ADAM_STEP = 0.001, 0.9, 0.999, 1e-08, 0.01, 10

VMEM_LIMIT = 56 * 1024 * 1024
SUBLANES = 8
LANES = 128


def _params(*sem):
    return pltpu.CompilerParams(dimension_semantics=sem, vmem_limit_bytes=VMEM_LIMIT)


ONE_PASS, THREE_PASS, FULL_F32, EXACT_LHS, EXACT_RHS = 0, 1, 2, 3, 4


def _split3(v):
    hi = v.astype(BF16)
    r1 = v - hi.astype(F32)
    mid = r1.astype(BF16)
    return hi, mid, (r1 - mid.astype(F32)).astype(BF16)


def _mm_raw(a, b, kind, prec):
    nb = a.ndim - 2
    ca = a.ndim - 1 if kind[0] == "n" else a.ndim - 2
    cb = b.ndim - 2 if kind[1] == "n" else b.ndim - 1
    dims = (((ca,), (cb,)), (tuple(range(nb)),) * 2)
    if prec == FULL_F32:
        return lax.dot_general(a, b, dims, precision=HI, preferred_element_type=F32)
    dot = lambda p, q: lax.dot_general(p, q, dims, preferred_element_type=F32)
    ah, bh = a.astype(BF16), b.astype(BF16)
    if prec == ONE_PASS:
        return dot(ah, bh)
    if prec == EXACT_LHS:
        b1, b2, b3 = _split3(b)
        return dot(ah, b1) + (dot(ah, b2) + dot(ah, b3))
    if prec == EXACT_RHS:
        a1, a2, a3 = _split3(a)
        return dot(a1, bh) + (dot(a2, bh) + dot(a3, bh))
    al = (a - ah.astype(F32)).astype(BF16)
    bl = (b - bh.astype(F32)).astype(BF16)
    return dot(ah, bh) + (dot(ah, bl) + dot(al, bh))


@functools.partial(jax.custom_vjp, nondiff_argnums=(2, 3))
def _mm_vjp(a, b, kind, hi):
    return _mm_raw(a, b, kind, hi)


def _mm_vjp_fwd(a, b, kind, hi):
    return _mm_raw(a, b, kind, hi), (a, b)


def _mm_vjp_bwd(kind, hi, res, dc):
    a, b = res
    if hi in (EXACT_LHS, EXACT_RHS):
        assert kind == "nn"
        if hi == EXACT_LHS:
            return jnp.zeros_like(a), _mm_raw(a, dc, "tn", EXACT_LHS)
        return _mm_raw(dc, b, "nt", EXACT_RHS), jnp.zeros_like(b)
    if kind == "nn":
        return _mm_raw(dc, b, "nt", hi), _mm_raw(a, dc, "tn", hi)
    if kind == "nt":
        return _mm_raw(dc, b, "nn", hi), _mm_raw(dc, a, "tn", hi)
    return _mm_raw(b, dc, "nt", hi), _mm_raw(a, dc, "nn", hi)


_mm_vjp.defvjp(_mm_vjp_fwd, _mm_vjp_bwd)


def _lane_total(v):
    return jnp.broadcast_to(jnp.sum(v, axis=-1, keepdims=True), v.shape)


def _matmul(a, b, *, name, ta=False, tb=False, add=None, add_scale=1.0, tm=1024, tn=2048, tk=1024):
    m, k = (a.shape[1], a.shape[0]) if ta else a.shape
    n = b.shape[0] if tb else b.shape[1]
    tm, tn, tk = min(tm, m), min(tn, n), min(tk, k)
    tn = tn if n % tn == 0 else tn // 2
    assert m % tm == 0 and n % tn == 0 and k % tk == 0, (name, m, n, k)
    nk = k // tk
    dims = (((0 if ta else 1,), (1 if tb else 0,)), ((), ()))

    def body(*refs):
        a_ref, b_ref = refs[:2]
        o_ref = refs[-1]
        part = lax.dot_general(a_ref[...].astype(BF16), b_ref[...].astype(BF16), dims, preferred_element_type=F32)
        first = (lambda: part) if add is None else (lambda: part + add_scale * refs[2][...])
        if nk == 1:
            o_ref[...] = first()
        else:
            kk = pl.program_id(2)

            @pl.when(kk == 0)
            def _():
                o_ref[...] = first()

            @pl.when(kk > 0)
            def _():
                o_ref[...] += part

    a_spec = pl.BlockSpec((tk, tm), lambda i, j, kk: (kk, i)) if ta else pl.BlockSpec((tm, tk), lambda i, j, kk: (i, kk))
    b_spec = pl.BlockSpec((tn, tk), lambda i, j, kk: (j, kk)) if tb else pl.BlockSpec((tk, tn), lambda i, j, kk: (kk, j))
    o_spec = pl.BlockSpec((tm, tn), lambda i, j, kk: (i, j))
    in_specs = [a_spec, b_spec] + ([o_spec] if add is not None else [])
    args = (a, b) + ((add,) if add is not None else ())
    return pl.pallas_call(
        body, name=name, grid=(m // tm, n // tn, nk), in_specs=in_specs, out_specs=o_spec,
        out_shape=jax.ShapeDtypeStruct((m, n), F32),
        compiler_params=_params("parallel", "parallel", "arbitrary"))(*args)


HALO = SUBLANES


def _halo_specs(tt, width, col, nt):
    r = tt // HALO
    prev = pl.BlockSpec((HALO, width), lambda i: (jnp.maximum(i * r - 1, 0), col))
    nxt = pl.BlockSpec((HALO, width), lambda i: (jnp.minimum((i + 1) * r, nt * r - 1), col))
    return prev, nxt


def _shift_down(ext, k):
    return ext if k == 0 else pltpu.roll(ext, k, 0)


def _shift_up(ext, k):
    return ext if k == 0 else pltpu.roll(ext, ext.shape[0] - k, 0)


def _causal_conv(ext, w, taps):
    acc = None
    for j in range(taps):
        term = w[j:j + 1, :] * _shift_down(ext, taps - 1 - j)
        acc = term if acc is None else acc + term
    return acc[HALO:, :]


def _conv_a_fwd(proj_a, conv_w):
    t = proj_a.shape[0]
    tt = min(t, 256)
    nt = t // tt
    wdt = 1024

    def body(cur_ref, prev_ref, w_ref, y_ref):
        i = pl.program_id(0)
        cur = cur_ref[...]
        h, c, b, z = (cur[:, k * wdt:(k + 1) * wdt] for k in range(4))
        prev = prev_ref[...]
        u_prev = jnp.where(i > 0, prev[:, wdt:2 * wdt] * prev[:, 0:wdt], 0.0)
        ext = jnp.concatenate([u_prev, c * h], axis=0)
        conv = _causal_conv(ext, w_ref[...], 3)
        y_ref[...] = (b * conv * jax.nn.silu(z)).astype(BF16)

    prev_spec, _ = _halo_specs(tt, 4 * wdt, 0, nt)
    return pl.pallas_call(
        body, name="conv_a_fwd", grid=(nt,),
        in_specs=[pl.BlockSpec((tt, 4 * wdt), lambda i: (i, 0)), prev_spec, pl.BlockSpec((3, wdt), lambda i: (0, 0))],
        out_specs=pl.BlockSpec((tt, wdt), lambda i: (i, 0)),
        out_shape=jax.ShapeDtypeStruct((t, wdt), BF16), compiler_params=_params("parallel"))(proj_a, proj_a, conv_w)


def _conv_a_bwd(proj_a, conv_w, dy):
    t = proj_a.shape[0]
    tt = min(t, 256)
    nt = t // tt
    wdt = 1024

    def body(cur_ref, prev_ref, nxt_ref, w_ref, dy_ref, dyn_ref, d_ref, dw_ref):
        i = pl.program_id(0)
        w = w_ref[...]
        cur, prev, nxt = cur_ref[...], prev_ref[...], nxt_ref[...]
        split = lambda a: tuple(a[:, k * wdt:(k + 1) * wdt] for k in range(4))
        h, c, b, z = split(cur)
        hp, cp, _, _ = split(prev)
        hn, cn, bn, zn = split(nxt)
        u_prev = jnp.where(i > 0, cp * hp, 0.0)
        u_ext = jnp.concatenate([u_prev, c * h, cn * hn], axis=0)
        taps = [_shift_down(u_ext, 2 - j)[HALO:, :] for j in range(3)]
        conv = w[0:1, :] * taps[0] + w[1:2, :] * taps[1] + w[2:3, :] * taps[2]
        b_cn = jnp.concatenate([b, bn], axis=0)
        z_cn = jnp.concatenate([z, zn], axis=0)
        dy_cn = jnp.concatenate([dy_ref[...], jnp.where(i < nt - 1, dyn_ref[...], 0.0)], axis=0)
        sg = jax.nn.sigmoid(z_cn)
        silu = z_cn * sg
        d_conv = dy_cn * b_cn * silu
        db = (dy_cn * conv * silu)[:tt, :]
        dz = (dy_cn * b_cn * conv * (sg * (1.0 + z_cn * (1.0 - sg))))[:tt, :]
        du = None
        for j in range(3):
            term = w[j:j + 1, :] * _shift_up(d_conv, 2 - j)
            du = term if du is None else du + term
        du = du[:tt, :]
        d_ref[...] = jnp.concatenate([du * c, du * h, db, dz], axis=1).astype(BF16)

        @pl.when(i == 0)
        def _():
            dw_ref[...] = jnp.zeros_like(dw_ref)

        d_cur = d_conv[:tt, :]
        rows = [jnp.sum(d_cur * taps[j][:tt, :], axis=0, keepdims=True) for j in range(3)]
        dw_ref[0:3, :] += jnp.concatenate(rows, axis=0)

    prev_spec, nxt_spec = _halo_specs(tt, 4 * wdt, 0, nt)
    _, dyn_spec = _halo_specs(tt, wdt, 0, nt)
    return pl.pallas_call(
        body, name="conv_a_bwd", grid=(nt,),
        in_specs=[pl.BlockSpec((tt, 4 * wdt), lambda i: (i, 0)), prev_spec, nxt_spec,
                  pl.BlockSpec((3, wdt), lambda i: (0, 0)), pl.BlockSpec((tt, wdt), lambda i: (i, 0)), dyn_spec],
        out_specs=[pl.BlockSpec((tt, 4 * wdt), lambda i: (i, 0)), pl.BlockSpec((SUBLANES, wdt), lambda i: (0, 0))],
        out_shape=[jax.ShapeDtypeStruct((t, 4 * wdt), BF16), jax.ShapeDtypeStruct((SUBLANES, wdt), F32)],
        compiler_params=_params("arbitrary"))(proj_a, proj_a, proj_a, conv_w, dy, dy)


def _conv_b_fwd(proj_qkv, conv_w):
    t, width = proj_qkv.shape
    tt = min(t, 256)
    nt = t // tt
    wdt = 1024

    def body(cur_ref, prev_ref, w_ref, y_ref):
        i = pl.program_id(1)
        ext = jnp.concatenate([jnp.where(i > 0, prev_ref[...], 0.0), cur_ref[...]], axis=0)
        y_ref[...] = jax.nn.silu(_causal_conv(ext, w_ref[...], 4))

    r = tt // HALO
    return pl.pallas_call(
        body, name="conv_b_fwd", grid=(width // wdt, nt),
        in_specs=[pl.BlockSpec((tt, wdt), lambda j, i: (i, j)),
                  pl.BlockSpec((HALO, wdt), lambda j, i: (jnp.maximum(i * r - 1, 0), j)),
                  pl.BlockSpec((4, wdt), lambda j, i: (0, j))],
        out_specs=pl.BlockSpec((tt, wdt), lambda j, i: (i, j)),
        out_shape=jax.ShapeDtypeStruct((t, width), F32), compiler_params=_params("parallel", "parallel"))(
            proj_qkv, proj_qkv, conv_w)


def _conv_b_bwd(proj_qkv, conv_w, d_act, col, name):
    t = proj_qkv.shape[0]
    tt = min(t, 256)
    nt = t // tt
    wdt = 1024

    def body(cur_ref, prev_ref, nxt_ref, w_ref, da_ref, dan_ref, d_ref, dw_ref):
        i = pl.program_id(0)
        w = w_ref[...]
        u_ext = jnp.concatenate([jnp.where(i > 0, prev_ref[...], 0.0), cur_ref[...], nxt_ref[...]], axis=0)
        taps = [_shift_down(u_ext, 3 - j)[HALO:, :] for j in range(4)]
        conv = w[0:1, :] * taps[0] + w[1:2, :] * taps[1] + w[2:3, :] * taps[2] + w[3:4, :] * taps[3]
        da_cn = jnp.concatenate([da_ref[...], jnp.where(i < nt - 1, dan_ref[...], 0.0)], axis=0)
        sg = jax.nn.sigmoid(conv)
        d_conv = da_cn * (sg * (1.0 + conv * (1.0 - sg)))
        du = None
        for j in range(4):
            term = w[j:j + 1, :] * _shift_up(d_conv, 3 - j)
            du = term if du is None else du + term
        d_ref[...] = du[:tt, :].astype(BF16)

        @pl.when(i == 0)
        def _():
            dw_ref[...] = jnp.zeros_like(dw_ref)

        d_cur = d_conv[:tt, :]
        rows = [jnp.sum(d_cur * taps[j][:tt, :], axis=0, keepdims=True) for j in range(4)]
        dw_ref[0:4, :] += jnp.concatenate(rows, axis=0)

    prev_spec, nxt_spec = _halo_specs(tt, wdt, col, nt)
    _, dan_spec = _halo_specs(tt, wdt, 0, nt)
    return pl.pallas_call(
        body, name=name, grid=(nt,),
        in_specs=[pl.BlockSpec((tt, wdt), lambda i: (i, col)), prev_spec, nxt_spec,
                  pl.BlockSpec((4, wdt), lambda i: (0, col)), pl.BlockSpec((tt, wdt), lambda i: (i, 0)), dan_spec],
        out_specs=[pl.BlockSpec((tt, wdt), lambda i: (i, 0)), pl.BlockSpec((SUBLANES, wdt), lambda i: (0, 0))],
        out_shape=[jax.ShapeDtypeStruct((t, wdt), BF16), jax.ShapeDtypeStruct((SUBLANES, wdt), F32)],
        compiler_params=_params("arbitrary"))(proj_qkv, proj_qkv, proj_qkv, conv_w, d_act, d_act)


def _rms_gate(o, gn, z):
    on = o * lax.rsqrt(jnp.mean(o * o, axis=-1, keepdims=True) + NORM_EPS) * gn
    return on * jax.nn.silu(z)


GDN_PREP_ROWS = 1024


def _unit_lower_inverse(low):
    c = low.shape[-1]
    eye = lax.broadcasted_iota(jnp.int32, low.shape, low.ndim - 2) == lax.broadcasted_iota(jnp.int32, low.shape, low.ndim - 1)
    x = -low
    inv = eye.astype(F32) + x
    for _ in range(c.bit_length() - 2):
        x = _mm_raw(x, x, "nn", THREE_PASS)
        inv = inv + _mm_raw(inv, x, "nn", THREE_PASS)
    return inv


@jax.custom_vjp
def _known_inverse(low, inv):
    return inv


def _known_inverse_fwd(low, inv):
    return inv, inv


def _known_inverse_bwd(inv, d_inv):
    return -_mm_raw(_mm_raw(inv, d_inv, "tn", THREE_PASS), inv, "nt", THREE_PASS), jnp.zeros_like(inv)


_known_inverse.defvjp(_known_inverse_fwd, _known_inverse_bwd)


def _gdn_prep(mm, qa, ka, va, braw, araw, alog, dtb, inv_kept=None):
    n, c, _ = qa.shape
    q = qa * lax.rsqrt(jnp.sum(qa * qa, axis=-1, keepdims=True) + 1e-6) * (HEAD_DIM ** -0.5)
    k = ka * lax.rsqrt(jnp.sum(ka * ka, axis=-1, keepdims=True) + 1e-6)
    beta = jax.nn.sigmoid(braw)
    g = -jnp.exp(alog) * jax.nn.softplus(araw + dtb)
    ri = lax.broadcasted_iota(jnp.int32, (n, c, c), 1)
    ci = lax.broadcasted_iota(jnp.int32, (n, c, c), 2)
    incl, strict, eye = ri >= ci, ri > ci, ri == ci
    gc = mm(incl.astype(F32), g, "nn", EXACT_LHS)
    gc_i = gc[:, :, :c]
    gc_j = mm(jnp.ones((n, c, c), F32), jnp.where(eye, gc_i, 0.0), "nn", EXACT_LHS)
    decay = jnp.where(incl, jnp.exp(jnp.where(incl, gc_i - gc_j, 0.0)), 0.0)
    kb = k * beta
    low = jnp.where(strict, mm(kb, k, "nt", ONE_PASS) * decay, 0.0)
    inv = _unit_lower_inverse(low) if inv_kept is None else _known_inverse(low, inv_kept)
    egc = jnp.exp(gc)
    u = mm(inv, va * beta, "nn", THREE_PASS)
    w = mm(inv, kb * egc, "nn", THREE_PASS)
    attn = jnp.where(incl, mm(q, k, "nt", ONE_PASS) * decay, 0.0)
    g_last = jnp.sum(g, axis=1, keepdims=True)
    outs = (u, w, q * egc, k * jnp.exp(g_last - gc), attn, jnp.exp(g_last))
    return outs + (inv,) if inv_kept is None else outs


def _gdn_scan(mm, u, w, qd, kd, attn, egl, z, gn, state):
    v_new = u - mm(w, state, "nn", ONE_PASS)
    o = mm(qd, state, "nn", ONE_PASS) + mm(attn, v_new, "nn", ONE_PASS)
    new_state = state * egl + mm(kd, v_new, "tn", ONE_PASS)
    return _rms_gate(o, gn, z), new_state


def _chunks(ref_value, n, c):
    return ref_value.reshape(n, c, ref_value.shape[-1])


def _by_head(ref, rows, heads):
    return jnp.stack([ref[rows, pl.ds(h * HEAD_DIM, HEAD_DIM)] for h in range(heads)])


def _store_heads(ref, rows, value):
    for h in range(value.shape[0]):
        ref[rows, pl.ds(h * HEAD_DIM, HEAD_DIM)] = value[h]


def _gdn_prep_specs(tb, nt_unused=None):
    col = lambda off: pl.BlockSpec((tb, HEAD_DIM), lambda h, i: (i, off + h))
    rep = pl.BlockSpec((1, tb, LANES), lambda h, i: (h, i, 0))
    par = pl.BlockSpec((1, SUBLANES, LANES), lambda h, i: (h, 0, 0))
    att = pl.BlockSpec((1, tb, GDN_CHUNK), lambda h, i: (h, i, 0))
    egl = pl.BlockSpec((1, tb // GDN_CHUNK, SUBLANES, LANES), lambda h, i: (h, i, 0, 0))
    return col, rep, par, att, egl


def _gdn_prep_fwd(qkv_act, braw, araw, alog, dtb):
    t = qkv_act.shape[0]
    tb = min(t, GDN_PREP_ROWS)
    nt, nc = t // tb, tb // GDN_CHUNK
    width = GDN_HEADS * HEAD_DIM

    def body(q_ref, k_ref, v_ref, br_ref, ar_ref, al_ref, dt_ref, u_ref, w_ref, qd_ref, kd_ref, at_ref, eg_ref, inv_ref):
        ch = lambda r: _chunks(r, nc, GDN_CHUNK)
        u, w, qd, kd, attn, egl, inv = _gdn_prep(_mm_raw, ch(q_ref[...]), ch(k_ref[...]), ch(v_ref[...]), ch(br_ref[0]),
                                                 ch(ar_ref[0]), al_ref[0, 0:1, :], dt_ref[0, 0:1, :])
        u_ref[...] = u.reshape(tb, HEAD_DIM)
        w_ref[...] = w.reshape(tb, HEAD_DIM).astype(BF16)
        qd_ref[...] = qd.reshape(tb, HEAD_DIM).astype(BF16)
        kd_ref[...] = kd.reshape(tb, HEAD_DIM).astype(BF16)
        at_ref[0] = attn.reshape(tb, GDN_CHUNK).astype(BF16)
        eg_ref[0] = jnp.broadcast_to(egl, (nc, SUBLANES, LANES))
        inv_ref[0] = inv.reshape(tb, GDN_CHUNK)

    col, rep, par, att, egl = _gdn_prep_specs(tb)
    h = GDN_HEADS
    return pl.pallas_call(
        body, name="gdn_prep_fwd", grid=(h, nt),
        in_specs=[col(0), col(h), col(2 * h), rep, rep, par, par],
        out_specs=[col(0), col(0), col(0), col(0), att, egl, att],
        out_shape=[jax.ShapeDtypeStruct((t, width), F32)] + [jax.ShapeDtypeStruct((t, width), BF16)] * 3
        + [jax.ShapeDtypeStruct((h, t, GDN_CHUNK), BF16), jax.ShapeDtypeStruct((h, t // GDN_CHUNK, SUBLANES, LANES), F32),
           jax.ShapeDtypeStruct((h, t, GDN_CHUNK), F32)],
        compiler_params=_params("parallel", "parallel"))(qkv_act, qkv_act, qkv_act, braw, araw, alog, dtb)


def _gdn_prep_bwd(qkv_act, braw, araw, alog, dtb, inv, du, dw, dqd, dkd, dattn, degl):
    t = qkv_act.shape[0]
    tb = min(t, GDN_PREP_ROWS)
    nt, nc = t // tb, tb // GDN_CHUNK
    width = GDN_HEADS * HEAD_DIM

    def body(q_ref, k_ref, v_ref, br_ref, ar_ref, al_ref, dt_ref, inv_ref, du_ref, dw_ref, dqd_ref, dkd_ref, dat_ref,
             deg_ref, dq_ref, dk_ref, dv_ref, dbr_ref, dar_ref, dal_ref, ddt_ref):
        @pl.when(pl.program_id(1) == 0)
        def _():
            dal_ref[...] = jnp.zeros_like(dal_ref)
            ddt_ref[...] = jnp.zeros_like(ddt_ref)

        ch = lambda r: _chunks(r, nc, GDN_CHUNK)
        _, vjp = jax.vjp(functools.partial(_gdn_prep, _mm_vjp, inv_kept=ch(inv_ref[0])), ch(q_ref[...]), ch(k_ref[...]),
                         ch(v_ref[...]), ch(br_ref[0]), ch(ar_ref[0]), al_ref[0, 0:1, :], dt_ref[0, 0:1, :])
        dq, dk, dv, dbr, dar, dal, ddt = vjp((ch(du_ref[...]), ch(dw_ref[...]), ch(dqd_ref[...]), ch(dkd_ref[...]),
                                              ch(dat_ref[0]), deg_ref[0][:, 0:1, :]))
        dq_ref[...] = dq.reshape(tb, HEAD_DIM)
        dk_ref[...] = dk.reshape(tb, HEAD_DIM)
        dv_ref[...] = dv.reshape(tb, HEAD_DIM)
        dbr_ref[0] = _lane_total(dbr.reshape(tb, LANES))
        dar_ref[0] = _lane_total(dar.reshape(tb, LANES))
        dal_ref[0, 0:1, :] += _lane_total(dal)
        ddt_ref[0, 0:1, :] += _lane_total(ddt)

    col, rep, par, att, egl = _gdn_prep_specs(tb)
    h = GDN_HEADS
    return pl.pallas_call(
        body, name="gdn_prep_bwd", grid=(h, nt),
        in_specs=[col(0), col(h), col(2 * h), rep, rep, par, par, att, col(0), col(0), col(0), col(0), att, egl],
        out_specs=[col(0), col(0), col(0), rep, rep, par, par],
        out_shape=[jax.ShapeDtypeStruct((t, width), F32)] * 3 + [jax.ShapeDtypeStruct((h, t, LANES), F32)] * 2
        + [jax.ShapeDtypeStruct((h, SUBLANES, LANES), F32)] * 2,
        compiler_params=_params("parallel", "arbitrary"))(qkv_act, qkv_act, qkv_act, braw, araw, alog, dtb, inv,
                                                         du, dw, dqd, dkd, dattn, degl)


def _scan_specs(tb, heads, chunk, rev, nt):
    ti = (lambda i: nt - 1 - i) if rev else (lambda i: i)
    row = pl.BlockSpec((tb, heads * HEAD_DIM), lambda i: (ti(i), 0))
    att = pl.BlockSpec((heads, tb, chunk), lambda i: (0, ti(i), 0))
    egl = pl.BlockSpec((heads, tb // chunk, SUBLANES, LANES), lambda i: (0, ti(i), 0, 0))
    hist = pl.BlockSpec((heads, tb // chunk, HEAD_DIM, HEAD_DIM), lambda i: (0, ti(i), 0, 0))
    gn = pl.BlockSpec((SUBLANES, LANES), lambda i: (0, 0))
    return row, att, egl, hist, gn


def _gdn_scan_fwd(u, w, qd, kd, attn, egl, zb, gn):
    t = u.shape[0]
    tb = min(t, 256)
    nt, nc = t // tb, tb // GDN_CHUNK
    nh = GDN_HEADS

    def body(u_ref, w_ref, qd_ref, kd_ref, at_ref, eg_ref, z_ref, gn_ref, y_ref, hist_ref, s_ref):
        @pl.when(pl.program_id(0) == 0)
        def _():
            s_ref[...] = jnp.zeros_like(s_ref)

        g = gn_ref[0:1, :]
        state = s_ref[...]
        for c in range(nc):
            rows = pl.ds(c * GDN_CHUNK, GDN_CHUNK)
            heads = lambda r: _by_head(r, rows, nh)
            hist_ref[:, c] = state
            y, state = _gdn_scan(_mm_raw, heads(u_ref), heads(w_ref), heads(qd_ref), heads(kd_ref), at_ref[:, rows, :],
                                 eg_ref[:, c, 0:1, :], heads(z_ref), g, state)
            _store_heads(y_ref, rows, y.astype(BF16))
        s_ref[...] = state

    row, att, egs, hist, gns = _scan_specs(tb, nh, GDN_CHUNK, False, nt)
    return pl.pallas_call(
        body, name="gdn_scan_fwd", grid=(nt,), in_specs=[row, row, row, row, att, egs, row, gns], out_specs=[row, hist],
        out_shape=[jax.ShapeDtypeStruct((t, nh * HEAD_DIM), BF16),
                   jax.ShapeDtypeStruct((nh, t // GDN_CHUNK, HEAD_DIM, HEAD_DIM), F32)],
        scratch_shapes=[pltpu.VMEM((nh, HEAD_DIM, HEAD_DIM), F32)],
        compiler_params=_params("arbitrary"))(u, w, qd, kd, attn, egl, zb, gn)


def _gdn_scan_bwd(u, w, qd, kd, attn, egl, zb, gn, hist, dy):
    t = u.shape[0]
    tb = min(t, 256)
    nt, nc = t // tb, tb // GDN_CHUNK
    nh = GDN_HEADS

    def body(u_ref, w_ref, qd_ref, kd_ref, at_ref, eg_ref, z_ref, gn_ref, hist_ref, dy_ref,
             du_ref, dw_ref, dqd_ref, dkd_ref, dat_ref, deg_ref, dz_ref, dgn_ref, ds_ref):
        @pl.when(pl.program_id(0) == 0)
        def _():
            ds_ref[...] = jnp.zeros_like(ds_ref)
            dgn_ref[...] = jnp.zeros_like(dgn_ref)

        g = gn_ref[0:1, :]
        d_state = ds_ref[...]
        for c in reversed(range(nc)):
            rows = pl.ds(c * GDN_CHUNK, GDN_CHUNK)
            heads = lambda r: _by_head(r, rows, nh).astype(F32)
            _, vjp = jax.vjp(functools.partial(_gdn_scan, _mm_vjp), heads(u_ref), heads(w_ref), heads(qd_ref),
                             heads(kd_ref), at_ref[:, rows, :].astype(F32), eg_ref[:, c, 0:1, :], heads(z_ref), g,
                             hist_ref[:, c])
            du, dw, dqd, dkd, dat, deg, dz, dgn, d_state = vjp((heads(dy_ref), d_state))
            _store_heads(du_ref, rows, du)
            _store_heads(dw_ref, rows, dw)
            _store_heads(dqd_ref, rows, dqd)
            _store_heads(dkd_ref, rows, dkd)
            dat_ref[:, rows, :] = dat
            deg_ref[:, c] = jnp.broadcast_to(deg, (nh, SUBLANES, LANES))
            _store_heads(dz_ref, rows, dz.astype(BF16))
            dgn_ref[0:1, :] += dgn
        ds_ref[...] = d_state

    row, att, egs, hists, gns = _scan_specs(tb, nh, GDN_CHUNK, True, nt)
    wide = jax.ShapeDtypeStruct((t, nh * HEAD_DIM), F32)
    return pl.pallas_call(
        body, name="gdn_scan_bwd", grid=(nt,),
        in_specs=[row, row, row, row, att, egs, row, gns, hists, row],
        out_specs=[row, row, row, row, att, egs, row, gns],
        out_shape=[wide] * 4 + [jax.ShapeDtypeStruct((nh, t, GDN_CHUNK), F32),
                                jax.ShapeDtypeStruct((nh, t // GDN_CHUNK, SUBLANES, LANES), F32),
                                jax.ShapeDtypeStruct((t, nh * HEAD_DIM), BF16),
                                jax.ShapeDtypeStruct((SUBLANES, LANES), F32)],
        scratch_shapes=[pltpu.VMEM((nh, HEAD_DIM, HEAD_DIM), F32)],
        compiler_params=_params("arbitrary"))(u, w, qd, kd, attn, egl, zb, gn, hist, dy)


def _hgrn_prep(mm, qr, fr, lbl):
    n, c, _ = qr.shape
    lb = jax.nn.sigmoid(lbl[1:2, :] - lbl[0:1, :])
    f = lb + (1.0 - lb) * jax.nn.sigmoid(fr)
    q = jax.nn.silu(qr)
    k = 1.0 - f
    logf = jnp.log(f)
    ri = lax.broadcasted_iota(jnp.int32, (n, c, c), 1)
    ci = lax.broadcasted_iota(jnp.int32, (n, c, c), 2)
    b = mm((ri >= ci).astype(F32), logf, "nn", EXACT_LHS)
    attn = _hgrn_attn(mm, q, k, b)
    b_last = jnp.sum(logf, axis=1, keepdims=True)
    return q * jnp.exp(b), k * jnp.exp(b_last - b), attn, jnp.exp(b_last)


HGRN_SUB = 8


@functools.partial(jax.custom_vjp, nondiff_argnums=(1,))
def _roll_rows(x, shift):
    return pltpu.roll(x, shift, x.ndim - 2)


def _roll_rows_fwd(x, shift):
    return _roll_rows(x, shift), None


def _roll_rows_bwd(shift, _, d):
    return (pltpu.roll(d, d.shape[-2] - shift, d.ndim - 2),)


_roll_rows.defvjp(_roll_rows_fwd, _roll_rows_bwd)


@jax.custom_vjp
def _exp_clamped(v):
    return jnp.exp(jnp.minimum(v, 0.0))


def _exp_clamped_fwd(v):
    out = jnp.exp(jnp.minimum(v, 0.0))
    return out, out


def _exp_clamped_bwd(out, d):
    return (d * out,)


_exp_clamped.defvjp(_exp_clamped_fwd, _exp_clamped_bwd)


def _hgrn_attn(mm, q, k, b):
    n, c, d = q.shape
    sb = HGRN_SUB
    sub = lambda a: a.reshape(n * c // sb, sb, d)
    qs, ks, bs = sub(q), sub(k), sub(b)
    row = lax.broadcasted_iota(jnp.int32, (n, c, c), 1)
    col = lax.broadcasted_iota(jnp.int32, (n, c, c), 2)
    same_block = (row & -sb) == (col & -sb)
    attn = None
    for delta in range(sb):
        if delta == 0:
            prod = qs * ks
        else:
            prod = qs * _roll_rows(ks, delta) * _exp_clamped(bs - _roll_rows(bs, delta))
        sums = jnp.sum(prod, axis=-1, keepdims=True).reshape(n, c, 1)
        term = jnp.where(same_block & (row - col == delta), sums, 0.0)
        attn = term if attn is None else attn + term
    far = [jnp.zeros((n, sb, c), F32)]
    col8 = lax.broadcasted_iota(jnp.int32, (n, sb, c), 2)
    for i in range(1, c // sb):
        r0 = i * sb
        bi = b[:, r0:r0 + sb, :]
        ref = bi[:, 0:1, :]
        part = mm(q[:, r0:r0 + sb, :] * jnp.exp(bi - ref), k * _exp_clamped(ref - b), "nt", ONE_PASS)
        far.append(jnp.where(col8 < r0, part, 0.0))
    return attn + jnp.concatenate(far, axis=1)


def _hgrn_scan(mm, qe, kd, attn, ebl, iv, z, gn, state):
    o = mm(qe, state, "nt", ONE_PASS) + mm(attn, iv, "nn", ONE_PASS)
    new_state = state * ebl + mm(iv, kd, "tn", ONE_PASS)
    return _rms_gate(o, gn, z), new_state


def _hgrn_prep_specs(tb):
    col = pl.BlockSpec((tb, HEAD_DIM), lambda h, i: (i, h))
    lbs = pl.BlockSpec((2, HEAD_DIM), lambda h, i: (0, h))
    att = pl.BlockSpec((1, tb, HGRN_CHUNK), lambda h, i: (h, i, 0))
    ebl = pl.BlockSpec((1, tb // HGRN_CHUNK, SUBLANES, LANES), lambda h, i: (h, i, 0, 0))
    return col, lbs, att, ebl


def _hgrn_prep_fwd(qr, fr, lower_bounds):
    t = qr.shape[0]
    tb = min(t, 256)
    nt, nc = t // tb, tb // HGRN_CHUNK
    hh = HGRN_HEADS

    def body(q_ref, f_ref, lb_ref, qe_ref, kd_ref, at_ref, eb_ref):
        ch = lambda r: _chunks(r, nc, HGRN_CHUNK)
        qe, kd, attn, ebl = _hgrn_prep(_mm_raw, ch(q_ref[...]), ch(f_ref[...]), lb_ref[...])
        qe_ref[...] = qe.reshape(tb, HEAD_DIM).astype(BF16)
        kd_ref[...] = kd.reshape(tb, HEAD_DIM).astype(BF16)
        at_ref[0] = attn.reshape(tb, HGRN_CHUNK).astype(BF16)
        eb_ref[0] = jnp.broadcast_to(ebl, (nc, SUBLANES, LANES))

    col, lbs, att, ebs = _hgrn_prep_specs(tb)
    return pl.pallas_call(
        body, name="hgrn_prep_fwd", grid=(hh, nt), in_specs=[col, col, lbs], out_specs=[col, col, att, ebs],
        out_shape=[jax.ShapeDtypeStruct((t, HGRN_WIDTH), BF16)] * 2
        + [jax.ShapeDtypeStruct((hh, t, HGRN_CHUNK), BF16), jax.ShapeDtypeStruct((hh, t // HGRN_CHUNK, SUBLANES, LANES), F32)],
        compiler_params=_params("parallel", "parallel"))(qr, fr, lower_bounds)


def _hgrn_prep_bwd(qr, fr, lower_bounds, dqe, dkd, dattn, debl):
    t = qr.shape[0]
    tb = min(t, 256)
    nt, nc = t // tb, tb // HGRN_CHUNK
    hh = HGRN_HEADS

    def body(q_ref, f_ref, lb_ref, dqe_ref, dkd_ref, dat_ref, deb_ref, dq_ref, df_ref, dlb_ref):
        @pl.when(pl.program_id(1) == 0)
        def _():
            dlb_ref[...] = jnp.zeros_like(dlb_ref)

        ch = lambda r: _chunks(r, nc, HGRN_CHUNK)
        _, vjp = jax.vjp(functools.partial(_hgrn_prep, _mm_vjp), ch(q_ref[...]), ch(f_ref[...]), lb_ref[...])
        dq, df, dlb = vjp((ch(dqe_ref[...]), ch(dkd_ref[...]), ch(dat_ref[0]), deb_ref[0][:, 0:1, :]))
        dq_ref[...] = dq.reshape(tb, HEAD_DIM).astype(BF16)
        df_ref[...] = df.reshape(tb, HEAD_DIM).astype(BF16)
        dlb_ref[...] += dlb

    col, lbs, att, ebs = _hgrn_prep_specs(tb)
    return pl.pallas_call(
        body, name="hgrn_prep_bwd", grid=(hh, nt), in_specs=[col, col, lbs, col, col, att, ebs],
        out_specs=[col, col, lbs],
        out_shape=[jax.ShapeDtypeStruct((t, HGRN_WIDTH), BF16)] * 2 + [jax.ShapeDtypeStruct((2, HGRN_WIDTH), F32)],
        compiler_params=_params("parallel", "arbitrary"))(qr, fr, lower_bounds, dqe, dkd, dattn, debl)


def _hgrn_scan_fwd(qe, kd, attn, ebl, iv, z, gn):
    t = qe.shape[0]
    tb = min(t, 128)
    nt, nc = t // tb, tb // HGRN_CHUNK
    hh = HGRN_HEADS

    def body(qe_ref, kd_ref, at_ref, eb_ref, i_ref, z_ref, gn_ref, y_ref, hist_ref, s_ref):
        @pl.when(pl.program_id(0) == 0)
        def _():
            s_ref[...] = jnp.zeros_like(s_ref)

        g = gn_ref[0:1, :]
        state = s_ref[...]
        for c in range(nc):
            rows = pl.ds(c * HGRN_CHUNK, HGRN_CHUNK)
            heads = lambda r: _by_head(r, rows, hh)
            hist_ref[:, c] = state
            y, state = _hgrn_scan(_mm_raw, heads(qe_ref), heads(kd_ref), at_ref[:, rows, :], eb_ref[:, c, 0:1, :],
                                  heads(i_ref), heads(z_ref), g, state)
            _store_heads(y_ref, rows, y.astype(BF16))
        s_ref[...] = state

    row, att, ebs, hist, gns = _scan_specs(tb, hh, HGRN_CHUNK, False, nt)
    return pl.pallas_call(
        body, name="hgrn_scan_fwd", grid=(nt,), in_specs=[row, row, att, ebs, row, row, gns], out_specs=[row, hist],
        out_shape=[jax.ShapeDtypeStruct((t, HGRN_WIDTH), BF16),
                   jax.ShapeDtypeStruct((hh, t // HGRN_CHUNK, HEAD_DIM, HEAD_DIM), F32)],
        scratch_shapes=[pltpu.VMEM((hh, HEAD_DIM, HEAD_DIM), F32)],
        compiler_params=_params("arbitrary"))(qe, kd, attn, ebl, iv, z, gn)


def _hgrn_scan_bwd(qe, kd, attn, ebl, iv, z, gn, hist, dy):
    t = qe.shape[0]
    tb = min(t, 128)
    nt, nc = t // tb, tb // HGRN_CHUNK
    hh = HGRN_HEADS

    def body(qe_ref, kd_ref, at_ref, eb_ref, i_ref, z_ref, gn_ref, hist_ref, dy_ref,
             dqe_ref, dkd_ref, dat_ref, deb_ref, di_ref, dz_ref, dgn_ref, ds_ref):
        @pl.when(pl.program_id(0) == 0)
        def _():
            ds_ref[...] = jnp.zeros_like(ds_ref)
            dgn_ref[...] = jnp.zeros_like(dgn_ref)

        g = gn_ref[0:1, :]
        d_state = ds_ref[...]
        for c in reversed(range(nc)):
            rows = pl.ds(c * HGRN_CHUNK, HGRN_CHUNK)
            heads = lambda r: _by_head(r, rows, hh).astype(F32)
            _, vjp = jax.vjp(functools.partial(_hgrn_scan, _mm_vjp), heads(qe_ref), heads(kd_ref),
                             at_ref[:, rows, :].astype(F32), eb_ref[:, c, 0:1, :], heads(i_ref), heads(z_ref), g,
                             hist_ref[:, c])
            dqe, dkd, dat, deb, di, dz, dgn, d_state = vjp((heads(dy_ref), d_state))
            _store_heads(dqe_ref, rows, dqe)
            _store_heads(dkd_ref, rows, dkd)
            dat_ref[:, rows, :] = dat
            deb_ref[:, c] = jnp.broadcast_to(deb, (hh, SUBLANES, LANES))
            _store_heads(di_ref, rows, di.astype(BF16))
            _store_heads(dz_ref, rows, dz.astype(BF16))
            dgn_ref[0:1, :] += dgn
        ds_ref[...] = d_state

    row, att, ebs, hists, gns = _scan_specs(tb, hh, HGRN_CHUNK, True, nt)
    wide = lambda dt: jax.ShapeDtypeStruct((t, HGRN_WIDTH), dt)
    return pl.pallas_call(
        body, name="hgrn_scan_bwd", grid=(nt,),
        in_specs=[row, row, att, ebs, row, row, gns, hists, row],
        out_specs=[row, row, att, ebs, row, row, gns],
        out_shape=[wide(F32), wide(F32), jax.ShapeDtypeStruct((hh, t, HGRN_CHUNK), F32),
                   jax.ShapeDtypeStruct((hh, t // HGRN_CHUNK, SUBLANES, LANES), F32), wide(BF16), wide(BF16),
                   jax.ShapeDtypeStruct((SUBLANES, LANES), F32)],
        scratch_shapes=[pltpu.VMEM((hh, HEAD_DIM, HEAD_DIM), F32)],
        compiler_params=_params("arbitrary"))(qe, kd, attn, ebl, iv, z, gn, hist, dy)


def _layer_norm(pre, g, b):
    mu = jnp.mean(pre, axis=-1, keepdims=True)
    d = pre - mu
    var = jnp.mean(d * d, axis=-1, keepdims=True)
    return d * lax.rsqrt(var + NORM_EPS) * g + b


def _lnpl_fwd(xin, s, p, wg, wpl, ln_g, ln_b):
    t = xin.shape[0]
    tt = min(t, 256)

    def body(x_ref, s_ref, p_ref, wg_ref, wpl_ref, g_ref, b_ref, o_ref, ob_ref):
        xn = _layer_norm(DEEPNORM_ALPHA * x_ref[...] + s_ref[...], g_ref[...], b_ref[...])
        gate = jax.nn.sigmoid(_mm_raw(xn, wg_ref[...], "nn", False))
        out = xn + _mm_raw(p_ref[...], wpl_ref[...], "nn", False) * gate
        o_ref[...] = out
        ob_ref[...] = out.astype(BF16)

    row = lambda w: pl.BlockSpec((tt, w), lambda i: (i, 0))
    full = lambda a: pl.BlockSpec(a.shape, lambda i: (0, 0))
    return pl.pallas_call(
        body, name="lnpl_fwd", grid=(t // tt,),
        in_specs=[row(D_MODEL), row(D_MODEL), row(PL_DIM), full(wg), full(wpl), full(ln_g), full(ln_b)],
        out_specs=[row(D_MODEL), row(D_MODEL)],
        out_shape=[jax.ShapeDtypeStruct((t, D_MODEL), F32), jax.ShapeDtypeStruct((t, D_MODEL), BF16)],
        compiler_params=_params("parallel"))(xin, s, p, wg, wpl, ln_g, ln_b)


def _lnpl_bwd(xin, s, p, wg, wpl, ln_g, ln_b, upstream, last, name):
    t = xin.shape[0]
    tt = min(t, 256)

    def body(x_ref, s_ref, p_ref, wg_ref, wpl_ref, g_ref, b_ref, up_ref,
             dpre_ref, dwg_ref, dwpl_ref, dg_ref, db_ref, loss_ref):
        @pl.when(pl.program_id(0) == 0)
        def _():
            for r in (dwg_ref, dwpl_ref, dg_ref, db_ref, loss_ref):
                r[...] = jnp.zeros_like(r)

        pre = DEEPNORM_ALPHA * x_ref[...] + s_ref[...]
        xn, ln_vjp = jax.vjp(_layer_norm, pre, g_ref[...], b_ref[...])
        gate = jax.nn.sigmoid(_mm_raw(xn, wg_ref[...], "nn", False))
        plv = _mm_raw(p_ref[...], wpl_ref[...], "nn", False)
        if last:
            err = xn + plv * gate - up_ref[...]
            dout = err * (1.0 / D_MODEL)
            tot = jnp.sum(jnp.sum(err * err, axis=1, keepdims=True), axis=0, keepdims=True) * (0.5 / D_MODEL)
            loss_ref[...] += jnp.broadcast_to(tot, loss_ref.shape)
        else:
            dout = up_ref[...]
        dplv = dout * gate
        dlogits = dout * plv * gate * (1.0 - gate)
        dwg_ref[...] += _mm_raw(xn, dlogits, "tn", False)
        dwpl_ref[...] += _mm_raw(p_ref[...], dplv, "tn", False)
        dxn = dout + _mm_raw(dlogits, wg_ref[...], "nt", False)
        dpre, dg, db = ln_vjp(dxn)
        dpre_ref[...] = dpre
        dg_ref[...] += dg
        db_ref[...] += db

    row = lambda w: pl.BlockSpec((tt, w), lambda i: (i, 0))
    full = lambda shape: pl.BlockSpec(shape, lambda i: (0, 0))
    return pl.pallas_call(
        body, name=name, grid=(t // tt,),
        in_specs=[row(D_MODEL), row(D_MODEL), row(PL_DIM), full(wg.shape), full(wpl.shape), full(ln_g.shape),
                  full(ln_b.shape), row(D_MODEL)],
        out_specs=[row(D_MODEL), full(wg.shape), full(wpl.shape), full(ln_g.shape), full(ln_b.shape),
                   full((SUBLANES, LANES))],
        out_shape=[jax.ShapeDtypeStruct((t, D_MODEL), F32), jax.ShapeDtypeStruct(wg.shape, F32),
                   jax.ShapeDtypeStruct(wpl.shape, F32), jax.ShapeDtypeStruct(ln_g.shape, F32),
                   jax.ShapeDtypeStruct(ln_b.shape, F32), jax.ShapeDtypeStruct((SUBLANES, LANES), F32)],
        compiler_params=_params("arbitrary"))(xin, s, p, wg, wpl, ln_g, ln_b, upstream)


def _pack_tail(dbr, dar):
    nh, t, _ = dbr.shape
    tt = min(t, 512)

    def body(b_ref, a_ref, o_ref):
        lane = lax.broadcasted_iota(jnp.int32, (tt, LANES), 1)
        acc = jnp.zeros((tt, LANES), F32)
        for h in range(nh):
            acc = jnp.where(lane == h, b_ref[h], acc)
            acc = jnp.where(lane == nh + h, a_ref[h], acc)
        o_ref[...] = acc.astype(BF16)

    spec = pl.BlockSpec((nh, tt, LANES), lambda i: (0, i, 0))
    return pl.pallas_call(
        body, name="pack_tail", grid=(t // tt,), in_specs=[spec, spec], out_specs=pl.BlockSpec((tt, LANES), lambda i: (i, 0)),
        out_shape=jax.ShapeDtypeStruct((t, LANES), BF16), compiler_params=_params("parallel"))(dbr, dar)


def _rep_rows(v):
    return jnp.broadcast_to(v.reshape(1, LANES), (SUBLANES, LANES))


def _rep_heads(v):
    return jnp.broadcast_to(v.reshape(-1, 1, 1), (v.shape[0], SUBLANES, LANES))


def _col_range(stacked, lo, hi):
    c = stacked.shape[2]
    parts = [stacked[s, :, max(lo, s * c) - s * c:min(hi, (s + 1) * c) - s * c]
             for s in range(4) if max(lo, s * c) < min(hi, (s + 1) * c)]
    return parts[0] if len(parts) == 1 else jnp.concatenate(parts, axis=1)


def _col_shards(pieces, c):
    shards, offs, o = [], [], 0
    for pc in pieces:
        offs.append(o)
        o += pc.shape[1]
    for s in range(4):
        lo, hi = s * c, (s + 1) * c
        parts = [pc[:, max(lo, o) - o:min(hi, o + pc.shape[1]) - o] for pc, o in zip(pieces, offs)
                 if max(lo, o) < min(hi, o + pc.shape[1])]
        shards.append(parts[0] if len(parts) == 1 else jnp.concatenate(parts, axis=1))
    return jnp.stack(shards)


def _local_step(x, p, target, w, late_weights, early_grads_ready, start_token):
    a = DEEPNORM_ALPHA
    nh = GDN_HEADS
    xb = (x + start_token).astype(BF16)
    wie = w["w_in_even"]
    w_a, w_qkv, w_zb = _col_range(wie, 0, 4096), _col_range(wie, 4096, 7168), _col_range(wie, 7168, 8192)
    w_tail = jnp.pad(_col_range(wie, 8192, 8192 + 2 * nh), ((0, 0), (0, LANES - 2 * nh)))
    conv_a_w, conv_b_w = w["conv_a_w"], w["conv_b_w"]
    ln_g0, ln_b0, ln_g1, ln_b1 = (v.reshape(1, D_MODEL) for v in (w["ln_g"][0], w["ln_b"][0], w["ln_g"][1], w["ln_b"][1]))
    alog, dtb = _rep_heads(w["a_log"].reshape(nh)), _rep_heads(w["dt_bias"].reshape(nh))
    gdn_g, hgrn_g = _rep_rows(w["gdn_norm_g"]), _rep_rows(w["hgrn_norm_g"])

    proj_a = _matmul(xb, w_a, name="fwd_proj_a")
    proj_qkv = _matmul(xb, w_qkv, name="fwd_proj_qkv")
    proj_zb = _matmul(xb, w_zb, name="fwd_proj_zb")
    proj_tail = _matmul(xb, w_tail, name="fwd_proj_tail")
    rep = lambda cols: jnp.broadcast_to(cols.T[:, :, None], (nh, cols.shape[0], LANES))
    braw, araw = rep(proj_tail[:, :nh]), rep(proj_tail[:, nh:2 * nh])
    y_a = _conv_a_fwd(proj_a, conv_a_w)
    qkv_act = _conv_b_fwd(proj_qkv, conv_b_w)
    *gdn_pre, gdn_inv = _gdn_prep_fwd(qkv_act, braw, araw, alog, dtb)
    y_b, gdn_hist = _gdn_scan_fwd(*gdn_pre, proj_zb, gdn_g)
    w = {**w, **late_weights(y_b)}
    woe, wio, woo = w["w_out_even"], w["w_in_odd"], w["w_out_odd"]
    s0 = _matmul(y_b, woe[1024:], name="fwd_out_even_b", add=_matmul(y_a, woe[:1024], name="fwd_out_even_a"))
    x1, x1b = _lnpl_fwd(x, s0, p[0], w["w_pl_gate"][0], w["w_pl"][0], ln_g0, ln_b0)
    proj_o = [_matmul(x1b, wio[j], name=f"fwd_proj_odd{j}") for j in range(4)]
    hgrn_pre = _hgrn_prep_fwd(proj_o[0], proj_o[1], w["lower_bounds"])
    y_o, hgrn_hist = _hgrn_scan_fwd(*hgrn_pre, proj_o[2], proj_o[3], hgrn_g)
    s1 = _matmul(y_o, woo, name="fwd_out_odd")

    g = {}
    dpre1, dwg1, dwpl1, dlng1, dlnb1, loss = _lnpl_bwd(x1, s1, p[1], w["w_pl_gate"][1], w["w_pl"][1], ln_g1, ln_b1,
                                                     target, True, "lnpl_bwd_odd")
    dy_o = _matmul(dpre1, woo, tb=True, name="bwd_out_odd_dx")
    g["w_out_odd"] = _matmul(y_o, dpre1, ta=True, name="bwd_out_odd_dw")
    dqe, dkd, dat, deb, di, dz, dhg = _hgrn_scan_bwd(*hgrn_pre, proj_o[2], proj_o[3], hgrn_g, hgrn_hist, dy_o)
    dq, df, dlb = _hgrn_prep_bwd(proj_o[0], proj_o[1], w["lower_bounds"], dqe, dkd, dat, deb)
    dx1 = dpre1
    scale = a
    dws = []
    for j, dj in enumerate((dq, df, di, dz)):
        dx1 = _matmul(dj, wio[j], tb=True, add=dx1, add_scale=scale, name=f"bwd_proj_odd_dx{j}")
        scale = 1.0
        dws.append(_matmul(x1b, dj, ta=True, name=f"bwd_proj_odd_dw{j}"))
    g["w_in_odd"] = jnp.stack(dws)
    g["hgrn_norm_g"] = dhg[0:1]
    g["lower_bounds"] = dlb
    g["w_pl_gate1"], g["w_pl1"] = dwg1, dwpl1

    dpre0, dwg0, dwpl0, dlng0, dlnb0, _ = _lnpl_bwd(x, s0, p[0], w["w_pl_gate"][0], w["w_pl"][0], ln_g0, ln_b0,
                                                  dx1, False, "lnpl_bwd_even")
    g["w_pl_gate0"], g["w_pl0"] = dwg0, dwpl0
    g["ln_g"] = jnp.concatenate([dlng0, dlng1], axis=0)
    g["ln_b"] = jnp.concatenate([dlnb0, dlnb1], axis=0)
    dy_a = _matmul(dpre0, woe[:1024], tb=True, name="bwd_out_even_dxa")
    dy_b = _matmul(dpre0, woe[1024:], tb=True, name="bwd_out_even_dxb")
    g["w_out_even"] = jnp.concatenate([_matmul(y_a, dpre0, ta=True, name="bwd_out_even_dwa"),
                                       _matmul(y_b, dpre0, ta=True, name="bwd_out_even_dwb")], axis=0)
    token = early_grads_ready({n: g[n] for n in _RS_EARLY})
    conv_a_w, gdn_g = conv_a_w + token, gdn_g + token
    du, dw, dqd, dkd, dat, deg, dzb, dgn = _gdn_scan_bwd(*gdn_pre, proj_zb, gdn_g, gdn_hist, dy_b)
    dqa, dka, dva, dbr, dar, dal, ddt = _gdn_prep_bwd(qkv_act, braw, araw, alog, dtb, gdn_inv, du, dw, dqd, dkd, dat, deg)
    g["a_log"] = dal[:, 0, 0].reshape(1, nh)
    g["dt_bias"] = ddt[:, 0, 0].reshape(1, nh)
    g["gdn_norm_g"] = dgn[0:1]
    d_pre_qkv, dwb = [], []
    for j, dj in enumerate((dqa, dka, dva)):
        dpj, dwj = _conv_b_bwd(proj_qkv, conv_b_w, dj, j, f"conv_b_bwd{j}")
        d_pre_qkv.append(dpj)
        dwb.append(dwj[:4])
    g["conv_b_w"] = jnp.concatenate(dwb, axis=1)
    d_a, dwa = _conv_a_bwd(proj_a, conv_a_w, dy_a)
    g["conv_a_w"] = dwa[:3]
    d_tail = _pack_tail(dbr, dar)
    pieces = [(d_a, w_a), (d_pre_qkv[0], w_qkv[:, :1024]), (d_pre_qkv[1], w_qkv[:, 1024:2048]),
              (d_pre_qkv[2], w_qkv[:, 2048:]), (dzb, w_zb), (d_tail, w_tail)]
    dx = dpre0
    scale = a
    dws = []
    for j, (dj, wj) in enumerate(pieces):
        dx = _matmul(dj, wj, tb=True, add=dx, add_scale=scale, name=f"bwd_proj_even_dx{j}")
        scale = 1.0
        dws.append(_matmul(xb, dj, ta=True, name=f"bwd_proj_even_dw{j}"))
    dws[-1] = dws[-1][:, :2 * nh]
    g["w_in_even"] = _col_shards(dws, wie.shape[2])
    return loss, dx, g


def _adamw(w, g, m, v, name):
    lead, rows, cols = w.shape
    tr = rows if rows <= 256 else 256
    assert rows % tr == 0, (name, rows)

    def body(w_ref, g_ref, m_ref, v_ref, d_ref, nm_ref, nv_ref):
        gg = g_ref[...]
        nm = ADAM_B1 * m_ref[...] + (1.0 - ADAM_B1) * gg
        nv = ADAM_B2 * v_ref[...] + (1.0 - ADAM_B2) * jnp.square(gg)
        m_hat = nm / (1.0 - ADAM_B1 ** ADAM_STEP)
        v_hat = nv / (1.0 - ADAM_B2 ** ADAM_STEP)
        d_ref[...] = -ADAM_LR * (m_hat / (jnp.sqrt(v_hat) + ADAM_EPS) + ADAM_WD * w_ref[...])
        nm_ref[...] = nm
        nv_ref[...] = nv

    spec = pl.BlockSpec((1, tr, cols), lambda l, i: (l, i, 0))
    return pl.pallas_call(
        body, name=name, grid=(lead, rows // tr), in_specs=[spec] * 4, out_specs=[spec] * 3,
        out_shape=[jax.ShapeDtypeStruct(w.shape, F32)] * 3, compiler_params=_params("parallel", "parallel"))(w, g, m, v)


MESH = pl.DeviceIdType.MESH
N_DEV = 8
HBM_SPEC = pl.BlockSpec(memory_space=pltpu.HBM)
VMEM_SPEC = pl.BlockSpec(memory_space=pltpu.VMEM)


def _coords():
    return lax.axis_index("x"), lax.axis_index("y"), lax.axis_index("c")


def _flip(v, bit):
    return 1 - v if bit else v


def _remote(src, dst, send_sem, recv_sem, dev):
    return pltpu.make_async_remote_copy(src_ref=src, dst_ref=dst, send_sem=send_sem, recv_sem=recv_sem,
                                        device_id=dev, device_id_type=MESH)


def _exchange_small(buf, reduce, name):
    rows = buf.shape[0]

    def body(in_ref, out_ref, slots, send_sems, recv_sems):
        x, y, c = _coords()
        me = 4 * x + 2 * y + c
        slots[me] = in_ref[...]
        peer = lambda k: (_flip(x, (k >> 2) & 1), _flip(y, (k >> 1) & 1), _flip(c, k & 1))
        sends = []
        for k in range(1, N_DEV):
            cp = _remote(in_ref, slots.at[me], send_sems.at[k - 1], recv_sems.at[k - 1], peer(k))
            cp.start()
            sends.append(cp)
        for k in range(1, N_DEV):
            px, py, pc = peer(k)
            _remote(in_ref, slots.at[4 * px + 2 * py + pc], send_sems.at[k - 1], recv_sems.at[k - 1], peer(k)).wait_recv()
        for cp in sends:
            cp.wait_send()
        if reduce:
            acc = slots[0]
            for d in range(1, N_DEV):
                acc = acc + slots[d]
            out_ref[...] = acc
        else:
            out_ref[...] = slots[...]

    out_shape = (rows, LANES) if reduce else (N_DEV, rows, LANES)
    return pl.pallas_call(
        body, name=name, in_specs=[VMEM_SPEC], out_specs=VMEM_SPEC, out_shape=jax.ShapeDtypeStruct(out_shape, F32),
        scratch_shapes=[pltpu.VMEM((N_DEV, rows, LANES), F32), pltpu.SemaphoreType.DMA((N_DEV - 1,)),
                        pltpu.SemaphoreType.DMA((N_DEV - 1,))])(buf)


def _half_rows(half, which):
    return pl.ds(pl.multiple_of(which * half, 16), half)


def _other_chip(x, y, k):
    return _flip(x, (k >> 1) & 1), _flip(y, k & 1)


def _gather_shards(shards):
    n = len(shards)

    def body(*refs):
        ins, outs = refs[:n], refs[n:2 * n]
        ici_s, ici_r, d2d_s, d2d_r = refs[2 * n:]
        x, y, c = _coords()
        chip = 2 * x + y
        sends = []
        for i in range(n):
            half = ins[i].shape[0] // 2
            for k in (1, 2, 3):
                ox, oy = _other_chip(x, y, k)
                cp = _remote(ins[i].at[_half_rows(half, c)], outs[i].at[chip, _half_rows(half, c)],
                             ici_s.at[3 * i + k - 1], ici_r.at[3 * i + k - 1], (ox, oy, c))
                cp.start()
                sends.append(cp)
        for k in (1, 2, 3):
            ox, oy = _other_chip(x, y, k)
            for i in range(n):
                half = ins[i].shape[0] // 2
                piece = outs[i].at[2 * ox + oy, _half_rows(half, c)]
                _remote(piece, piece, ici_s.at[3 * i + k - 1], ici_r.at[3 * i + k - 1], (ox, oy, c)).wait_recv()
                cp = _remote(piece, piece, d2d_s.at[3 * i + k - 1], d2d_r.at[3 * i + k - 1], (x, y, 1 - c))
                cp.start()
                sends.append(cp)
        for k in (1, 2, 3):
            ox, oy = _other_chip(x, y, k)
            for i in range(n):
                half = ins[i].shape[0] // 2
                piece = outs[i].at[2 * ox + oy, _half_rows(half, 1 - c)]
                _remote(piece, piece, d2d_s.at[3 * i + k - 1], d2d_r.at[3 * i + k - 1], (x, y, 1 - c)).wait_recv()
        for cp in sends:
            cp.wait_send()

    return pl.pallas_call(
        body, name="gather_weights", in_specs=[HBM_SPEC] * n, out_specs=[HBM_SPEC] * n,
        out_shape=[jax.ShapeDtypeStruct((4,) + s.shape, s.dtype) for s in shards],
        scratch_shapes=[pltpu.SemaphoreType.DMA((3 * n,))] * 4)(*shards)


SEM_SPEC = pl.BlockSpec(memory_space=pltpu.SEMAPHORE)
DATAFLOW = pltpu.SideEffectType.DATAFLOW_SIDE_EFFECTING


def _ici_piece(srcs, lands, send_sems, recv_sems, i, k, x, y, c):
    half = srcs[i].shape[0] // 2
    ox, oy = _other_chip(x, y, k)
    return _remote(srcs[i].at[_half_rows(half, c)], lands[i].at[2 * x + y, _half_rows(half, c)],
                   send_sems.at[3 * i + k - 1], recv_sems.at[3 * i + k - 1], (ox, oy, c)), (ox, oy)


def _gather_start(shards):
    n = len(shards)

    def body(*refs):
        srcs, lands = refs[:n], refs[n:2 * n]
        send_sems, recv_sems = refs[2 * n], refs[2 * n + 1]
        token = refs[-1]
        x, y, c = _coords()
        for i in range(n):
            for k in (1, 2, 3):
                _ici_piece(srcs, lands, send_sems, recv_sems, i, k, x, y, c)[0].start()
        token[...] = jnp.zeros_like(token)

    hbm = lambda a: pltpu.with_memory_space_constraint(a, pltpu.HBM)
    lands = [lax.empty((4,) + s.shape, s.dtype) for s in shards]
    out = pl.pallas_call(
        body, name="gather_rest_start",
        out_shape=(pltpu.SemaphoreType.DMA((3 * n,)), pltpu.SemaphoreType.DMA((3 * n,)),
                   *[pltpu.HBM(s.shape, s.dtype) for s in shards], *[pltpu.HBM(a.shape, a.dtype) for a in lands],
                   jax.ShapeDtypeStruct((SUBLANES, LANES), F32)),
        in_specs=[HBM_SPEC] * (2 * n), out_specs=(SEM_SPEC, SEM_SPEC, *[HBM_SPEC] * (2 * n), VMEM_SPEC),
        input_output_aliases={i: 2 + i for i in range(2 * n)},
        compiler_params=pltpu.CompilerParams(has_side_effects=DATAFLOW))(*[hbm(s) for s in shards], *[hbm(a) for a in lands])
    return out[0], out[1], out[2:2 + n], out[2 + n:2 + 2 * n], out[-1]


def _gather_wait(send_sems, recv_sems, srcs, lands, after):
    n = len(srcs)

    def body(*refs):
        src_refs, land_refs = refs[:n], refs[n:2 * n]
        send_sems, recv_sems = refs[2 * n], refs[2 * n + 1]
        x, y, c = _coords()
        for i in range(n):
            half = src_refs[i].shape[0] // 2
            for k in (1, 2, 3):
                cp, (ox, oy) = _ici_piece(src_refs, land_refs, send_sems, recv_sems, i, k, x, y, c)
                cp.wait_send()
                piece = land_refs[i].at[2 * ox + oy, _half_rows(half, c)]
                _remote(piece, piece, send_sems.at[3 * i + k - 1], recv_sems.at[3 * i + k - 1], (ox, oy, c)).wait_recv()

    out = pl.pallas_call(
        body, name="gather_rest_wait",
        out_shape=(*[pltpu.HBM(s.shape, s.dtype) for s in srcs], *[pltpu.HBM(a.shape, a.dtype) for a in lands]),
        in_specs=[HBM_SPEC] * (2 * n) + [SEM_SPEC, SEM_SPEC, pl.BlockSpec(memory_space=pl.ANY)],
        out_specs=tuple([HBM_SPEC] * (2 * n)), input_output_aliases={i: i for i in range(2 * n)},
        compiler_params=pltpu.CompilerParams(has_side_effects=DATAFLOW))(*srcs, *lands, send_sems, recv_sems, after)
    return out[n:]


def _gather_forward(lands):
    n = len(lands)

    def body(*refs):
        ins, outs = refs[:n], refs[n:2 * n]
        send_sems, recv_sems = refs[2 * n:]
        x, y, c = _coords()
        sends = []
        for i in range(n):
            half = ins[i].shape[1] // 2
            for k in (1, 2, 3):
                ox, oy = _other_chip(x, y, k)
                cp = _remote(ins[i].at[2 * ox + oy, _half_rows(half, c)], outs[i].at[2 * ox + oy, _half_rows(half, c)],
                             send_sems.at[3 * i + k - 1], recv_sems.at[3 * i + k - 1], (x, y, 1 - c))
                cp.start()
                sends.append(cp)
        for i in range(n):
            half = ins[i].shape[1] // 2
            for k in (1, 2, 3):
                ox, oy = _other_chip(x, y, k)
                piece = outs[i].at[2 * ox + oy, _half_rows(half, 1 - c)]
                _remote(piece, piece, send_sems.at[3 * i + k - 1], recv_sems.at[3 * i + k - 1], (x, y, 1 - c)).wait_recv()
        for cp in sends:
            cp.wait_send()

    return pl.pallas_call(
        body, name="gather_rest_forward", in_specs=[HBM_SPEC] * n, out_specs=[HBM_SPEC] * n,
        out_shape=[jax.ShapeDtypeStruct(a.shape, a.dtype) for a in lands],
        input_output_aliases={i: i for i in range(n)},
        scratch_shapes=[pltpu.SemaphoreType.DMA((3 * n,))] * 2)(*lands)


def _rs_sibling_swap(g4s, name):
    n = len(g4s)

    def body(*refs):
        ins, outs = refs[:n], refs[n:2 * n]
        send_sems, recv_sems = refs[2 * n:]
        x, y, c = _coords()
        sends = []
        for i in range(n):
            half = ins[i].shape[1] // 2
            for s in range(4):
                cp = _remote(ins[i].at[s, _half_rows(half, 1 - c)], outs[i].at[s], send_sems.at[4 * i + s],
                             recv_sems.at[4 * i + s], (x, y, 1 - c))
                cp.start()
                sends.append(cp)
        for cp in sends:
            cp.wait_recv()
        for cp in sends:
            cp.wait_send()

    return pl.pallas_call(
        body, name=name, in_specs=[HBM_SPEC] * n, out_specs=[HBM_SPEC] * n,
        out_shape=[jax.ShapeDtypeStruct((4, g.shape[1] // 2, g.shape[2]), g.dtype) for g in g4s],
        scratch_shapes=[pltpu.SemaphoreType.DMA((4 * n,))] * 2)(*g4s)


def _rs_add_sibling(g4, got, c_idx, name):
    _, rows, cols = g4.shape
    half = rows // 2
    tr = min(half, 256)
    nb = half // tr

    def body(c_ref, a_ref, b_ref, o_ref, ob_ref):
        total = a_ref[...] + b_ref[...]
        o_ref[...] = total
        ob_ref[...] = total.astype(BF16)

    blk = (1, tr, cols)
    out = pl.BlockSpec(blk, lambda s, i, c_ref: (s, i, 0))
    grid_spec = pltpu.PrefetchScalarGridSpec(
        num_scalar_prefetch=1, grid=(4, nb),
        in_specs=[pl.BlockSpec(blk, lambda s, i, c_ref: (s, c_ref[0] * nb + i, 0)), out],
        out_specs=[out, out])
    return pl.pallas_call(
        body, name=name, grid_spec=grid_spec,
        out_shape=[jax.ShapeDtypeStruct(got.shape, F32), jax.ShapeDtypeStruct(got.shape, BF16)],
        compiler_params=_params("parallel", "parallel"))(c_idx, g4, got)


def _rs_add_chips(p4, got3, idx, name):
    _, half, cols = p4.shape
    tr = min(half, 256)
    nb = half // tr

    def body(idx_ref, p_ref, a_ref, b_ref, c_ref, o_ref):
        o_ref[...] = ((p_ref[0] + a_ref[0].astype(F32)) + b_ref[0].astype(F32)) + c_ref[0].astype(F32)

    blk = (1, tr, cols)
    grid_spec = pltpu.PrefetchScalarGridSpec(
        num_scalar_prefetch=1, grid=(nb,),
        in_specs=[pl.BlockSpec(blk, lambda i, idx_ref: (idx_ref[0], i, 0))]
        + [pl.BlockSpec(blk, functools.partial(lambda k, i, idx_ref: (k, i, 0), k)) for k in range(3)],
        out_specs=pl.BlockSpec((tr, cols), lambda i, idx_ref: (idx_ref[1] * nb + i, 0)))
    return pl.pallas_call(body, name=name, grid_spec=grid_spec, out_shape=jax.ShapeDtypeStruct((2 * half, cols), F32),
                          compiler_params=_params("parallel"))(idx, p4, got3, got3, got3)


def _rs_share_halves(bufs, name):
    n = len(bufs)

    def body(*refs):
        ins, outs = refs[:n], refs[n:2 * n]
        send_sems, recv_sems = refs[2 * n:]
        x, y, c = _coords()
        sends = []
        for i in range(n):
            half = ins[i].shape[0] // 2
            cp = _remote(ins[i].at[_half_rows(half, c)], outs[i].at[_half_rows(half, c)], send_sems.at[i],
                         recv_sems.at[i], (x, y, 1 - c))
            cp.start()
            sends.append(cp)
        for i in range(n):
            half = ins[i].shape[0] // 2
            _remote(ins[i].at[_half_rows(half, c)], outs[i].at[_half_rows(half, 1 - c)], send_sems.at[i],
                    recv_sems.at[i], (x, y, 1 - c)).wait_recv()
        for cp in sends:
            cp.wait_send()

    return pl.pallas_call(
        body, name=name, in_specs=[HBM_SPEC] * n, out_specs=[HBM_SPEC] * n,
        out_shape=[jax.ShapeDtypeStruct(b.shape, b.dtype) for b in bufs],
        input_output_aliases={i: i for i in range(n)},
        scratch_shapes=[pltpu.SemaphoreType.DMA((n,))] * 2)(*bufs)


def _scatter_piece(srcs, lands, send_sems, recv_sems, i, k, x, y, c):
    ox, oy = _other_chip(x, y, k)
    return _remote(srcs[i].at[2 * ox + oy], lands[i].at[k - 1], send_sems.at[3 * i + k - 1],
                   recv_sems.at[3 * i + k - 1], (ox, oy, c))


def _rs_scatter_start(p4s, name):
    n = len(p4s)

    def body(*refs):
        srcs, lands = refs[:n], refs[n:2 * n]
        send_sems, recv_sems = refs[2 * n], refs[2 * n + 1]
        token = refs[-1]
        x, y, c = _coords()
        for i in range(n):
            for k in (1, 2, 3):
                _scatter_piece(srcs, lands, send_sems, recv_sems, i, k, x, y, c).start()
        token[...] = jnp.zeros_like(token)

    hbm = lambda a: pltpu.with_memory_space_constraint(a, pltpu.HBM)
    lands = [lax.empty((3,) + p.shape[1:], p.dtype) for p in p4s]
    out = pl.pallas_call(
        body, name=name,
        out_shape=(pltpu.SemaphoreType.DMA((3 * n,)), pltpu.SemaphoreType.DMA((3 * n,)),
                   *[pltpu.HBM(p.shape, p.dtype) for p in p4s], *[pltpu.HBM(a.shape, a.dtype) for a in lands],
                   jax.ShapeDtypeStruct((SUBLANES, LANES), F32)),
        in_specs=[HBM_SPEC] * (2 * n), out_specs=(SEM_SPEC, SEM_SPEC, *[HBM_SPEC] * (2 * n), VMEM_SPEC),
        input_output_aliases={i: 2 + i for i in range(2 * n)},
        compiler_params=pltpu.CompilerParams(has_side_effects=DATAFLOW))(*[hbm(p) for p in p4s], *[hbm(a) for a in lands])
    return out[0], out[1], out[2:2 + n], out[2 + n:2 + 2 * n], out[-1]


def _rs_scatter_wait(send_sems, recv_sems, srcs, lands, after, name):
    n = len(srcs)

    def body(*refs):
        src_refs, land_refs = refs[:n], refs[n:2 * n]
        send_sems, recv_sems = refs[2 * n], refs[2 * n + 1]
        x, y, c = _coords()
        for i in range(n):
            for k in (1, 2, 3):
                cp = _scatter_piece(src_refs, land_refs, send_sems, recv_sems, i, k, x, y, c)
                cp.wait_send()
                cp.wait_recv()

    out = pl.pallas_call(
        body, name=name,
        out_shape=(*[pltpu.HBM(s.shape, s.dtype) for s in srcs], *[pltpu.HBM(a.shape, a.dtype) for a in lands]),
        in_specs=[HBM_SPEC] * (2 * n) + [SEM_SPEC, SEM_SPEC, pl.BlockSpec(memory_space=pl.ANY)],
        out_specs=tuple([HBM_SPEC] * (2 * n)), input_output_aliases={i: i for i in range(2 * n)},
        compiler_params=pltpu.CompilerParams(has_side_effects=DATAFLOW))(*srcs, *lands, send_sems, recv_sems, after)
    return out[n:]


def _rs_front(g4s, names, tag):
    c_idx = jnp.stack([lax.axis_index("c")]).astype(jnp.int32)
    got = _rs_sibling_swap(g4s, f"rs_sibling_swap_{tag}")
    return [_rs_add_sibling(g, s, c_idx, f"rs_add_sibling_{nm}") for g, s, nm in zip(g4s, got, names)]


def _rs_back(p4s, got3, names):
    x, y, c = _coords()
    idx = jnp.stack([2 * x + y, c]).astype(jnp.int32)
    return [_rs_add_chips(p, t, idx, f"rs_add_chips_{nm}") for (p, _), t, nm in zip(p4s, got3, names)]


def _cols_split(full):
    r, c4 = full.shape
    return full.reshape(r, 4, c4 // 4).transpose(1, 0, 2)


_BIG = {
    "w_in_even": ((1024, 2052), lambda s: s),
    "w_out_even": ((512, 1024), lambda s: s.reshape(2048, 1024)),
    "w_in_odd": ((1024, 2048), lambda s: s),
    "w_out_odd": ((512, 1024), lambda s: s.reshape(2048, 1024)),
    "w_pl": ((512, 256), lambda s: s.reshape(4, 2, 256, 256).transpose(1, 2, 0, 3).reshape(2, 256, 1024)),
    "w_pl_gate": ((512, 1024), lambda s: s.reshape(4, 2, 256, 1024).transpose(1, 0, 2, 3).reshape(2, 1024, 1024)),
}


_RS_EARLY = {
    "w_in_odd": lambda f: f,
    "w_out_odd": lambda f: f.reshape(4, 512, 1024),
    "w_pl_gate1": lambda f: f.reshape(4, 256, 1024),
    "w_pl1": _cols_split,
    "w_out_even": lambda f: f.reshape(4, 512, 1024),
    "w_pl_gate0": lambda f: f.reshape(4, 256, 1024),
    "w_pl0": _cols_split,
}
_RS_LATE = {"w_in_even": lambda f: f}


def _size(shape):
    n = 1
    for d in shape:
        n *= d
    return n


_SMALL = {"a_log": (1, 8), "dt_bias": (1, 8), "gdn_norm_g": (1, 128), "hgrn_norm_g": (1, 128),
          "lower_bounds": (2, 2048), "ln_g": (2, 1024), "ln_b": (2, 1024), "conv_a_w": (3, 1024), "conv_b_w": (4, 3072)}
_CONV_SHARD = {"conv_a_w": (3, 256), "conv_b_w": (4, 768)}


def _pack_small(parts, shapes, head_rows=0):
    rows = []
    for n, shape in shapes.items():
        v = parts[n].reshape(-1)
        rows.append(jnp.pad(v, (0, -v.shape[0] % LANES)).reshape(-1, LANES))
    buf = jnp.concatenate(rows, axis=0)
    return jnp.pad(buf, ((head_rows, -(buf.shape[0] + head_rows) % SUBLANES), (0, 0)))


def _unpack_small(buf, shapes, head_rows=0):
    out, off = {}, head_rows
    for n, shape in shapes.items():
        nrow = -(-_size(shape) // LANES)
        out[n] = buf[off:off + nrow].reshape(-1)[:_size(shape)].reshape(shape)
        off += nrow
    return out


_WEIGHTS = ["w_in_even", "conv_a_w", "conv_b_w", "a_log", "dt_bias", "gdn_norm_g", "w_out_even", "w_in_odd",
            "lower_bounds", "hgrn_norm_g", "w_out_odd", "ln_g", "ln_b", "w_pl", "w_pl_gate"]


def kernel(x, p, w_in_even, conv_a_w, conv_b_w, a_log, dt_bias, gdn_norm_g, w_out_even, w_in_odd, lower_bounds, hgrn_norm_g, w_out_odd, ln_g, ln_b, w_pl, w_pl_gate, loss_target, m_w_in_even, m_conv_a_w, m_conv_b_w, m_a_log, m_dt_bias, m_gdn_norm_g, m_w_out_even, m_w_in_odd, m_lower_bounds, m_hgrn_norm_g, m_w_out_odd, m_ln_g, m_ln_b, m_w_pl, m_w_pl_gate, v_w_in_even, v_conv_a_w, v_conv_b_w, v_a_log, v_dt_bias, v_gdn_norm_g, v_w_out_even, v_w_in_odd, v_lower_bounds, v_hgrn_norm_g, v_w_out_odd, v_ln_g, v_ln_b, v_w_pl, v_w_pl_gate):
    w = dict(zip(_WEIGHTS, (w_in_even, conv_a_w, conv_b_w, a_log, dt_bias, gdn_norm_g, w_out_even, w_in_odd,
                            lower_bounds, hgrn_norm_g, w_out_odd, ln_g, ln_b, w_pl, w_pl_gate)))
    m = dict(zip(_WEIGHTS, (m_w_in_even, m_conv_a_w, m_conv_b_w, m_a_log, m_dt_bias, m_gdn_norm_g, m_w_out_even,
                            m_w_in_odd, m_lower_bounds, m_hgrn_norm_g, m_w_out_odd, m_ln_g, m_ln_b, m_w_pl, m_w_pl_gate)))
    v = dict(zip(_WEIGHTS, (v_w_in_even, v_conv_a_w, v_conv_b_w, v_a_log, v_dt_bias, v_gdn_norm_g, v_w_out_even,
                            v_w_in_odd, v_lower_bounds, v_hgrn_norm_g, v_w_out_odd, v_ln_g, v_ln_b, v_w_pl, v_w_pl_gate)))
    chip = 2 * lax.axis_index("x") + lax.axis_index("y")

    names = list(_BIG)
    shard_shapes = {n: _BIG[n][0] for n in names}
    shards = {n: w[n].reshape(shard_shapes[n]).astype(BF16) for n in names}
    whole = lambda n, stacked: _BIG[n][1](lax.dynamic_update_slice(stacked, shards[n][None], (chip, 0, 0)))
    early, late = names[:1], names[1:]
    full = {n: whole(n, ga) for n, ga in zip(early, _gather_shards([shards[n] for n in early]))}
    send_sems, recv_sems, srcs, lands, token = _gather_start([shards[n] for n in late])

    def late_weights(after):
        landed = _gather_forward(_gather_wait(send_sems, recv_sems, srcs, lands, after))
        return {n: whole(n, ga) for n, ga in zip(late, landed)}

    conv_mine = _pack_small({n: w[n] for n in _CONV_SHARD}, _CONV_SHARD)
    conv_all = _exchange_small(conv_mine, False, "gather_conv")
    conv_by_chip = [_unpack_small(conv_all[2 * s], _CONV_SHARD) for s in range(4)]
    for n in _CONV_SHARD:
        full[n] = jnp.concatenate([conv_by_chip[s][n] for s in range(4)], axis=1)
    for n in _SMALL:
        if n not in _CONV_SHARD:
            full[n] = w[n]

    early_rs = {}

    def early_grads_ready(grads):
        early_rs["p4s"] = _rs_front([_RS_EARLY[n](grads[n]) for n in _RS_EARLY], list(_RS_EARLY), "early")
        early_rs["sems"] = _rs_scatter_start([pb for _, pb in early_rs["p4s"]], "rs_scatter_early_start")
        return early_rs["sems"][4][0, 0]

    loss_part, dx, g = _local_step(x[0], p[:, 0], loss_target[0], full, late_weights, early_grads_ready, token[0, 0])

    late_p4s = _rs_front([_RS_LATE[n](g[n]) for n in _RS_LATE], list(_RS_LATE), "late")
    late_sems = _rs_scatter_start([pb for _, pb in late_p4s], "rs_scatter_late_start")
    got3 = _rs_scatter_wait(*early_rs["sems"][:4], late_sems[4], "rs_scatter_early_wait")
    summed = dict(zip(_RS_EARLY, _rs_share_halves(_rs_back(early_rs["p4s"], got3, list(_RS_EARLY)), "rs_share_early")))
    g_big = {n: summed[n] for n in names if n in summed}
    g_big["w_pl"] = jnp.stack([summed["w_pl0"], summed["w_pl1"]])
    g_big["w_pl_gate"] = jnp.stack([summed["w_pl_gate0"], summed["w_pl_gate1"]])
    small_sum = _exchange_small(jnp.concatenate([loss_part, _pack_small(g, _SMALL)], axis=0), True, "reduce_small")
    loss = small_sum[0, 0]
    g_small = _unpack_small(small_sum, _SMALL, head_rows=SUBLANES)
    for n, (rows, cols) in _CONV_SHARD.items():
        g_small[n] = lax.dynamic_slice_in_dim(g_small[n], chip * cols, cols, axis=1)

    grads, delta, new_m, new_v = {}, {}, {}, {}
    for n in late:
        grads[n] = g_big[n].reshape(w[n].shape)
        delta[n], new_m[n], new_v[n] = _adamw(w[n], grads[n], m[n], v[n], f"adamw_{n}")
    own = {n: (_CONV_SHARD[n] if n in _CONV_SHARD else _SMALL[n]) for n in _SMALL}
    packs = [_pack_small({n: src[n] for n in _SMALL}, own)[None] for src in (w, g_small, m, v)]
    outs = [_unpack_small(t[0], own) for t in _adamw(*packs, "adamw_small")]
    for n in _SMALL:
        grads[n] = g_small[n].reshape(w[n].shape)
        delta[n], new_m[n], new_v[n] = (t[n].reshape(w[n].shape) for t in outs)
    got3 = _rs_scatter_wait(*late_sems[:4], new_v["w_in_odd"], "rs_scatter_late_wait")
    (g_in_even,) = _rs_share_halves(_rs_back(late_p4s, got3, list(_RS_LATE)), "rs_share_late")
    for n in early:
        grads[n] = g_in_even.reshape(w[n].shape)
        delta[n], new_m[n], new_v[n] = _adamw(w[n], grads[n], m[n], v[n], f"adamw_{n}")
    return (loss, dx[None], *[grads[n] for n in _WEIGHTS], *[delta[n] for n in _WEIGHTS],
            *[new_m[n] for n in _WEIGHTS], *[new_v[n] for n in _WEIGHTS])
```

```python
import functools

import jax
import jax.numpy as jnp
from jax import lax
from jax.experimental import pallas as pl
from jax.experimental.pallas import tpu as pltpu

F32 = jnp.float32
BF16 = jnp.bfloat16
HI = lax.Precision.HIGHEST

D_MODEL = 1024
PL_DIM = 256
GDN_HEADS = 8
HEAD_DIM = 128
GDN_CHUNK = 64
HGRN_HEADS = 16
HGRN_CHUNK = 32
HGRN_WIDTH = 2048
DEEPNORM_ALPHA = 4.0 ** 0.25
NORM_EPS = 1e-5
ADAM_LR, ADAM_B1, ADAM_B2, ADAM_EPS, ADAM_WD, ADAM_STEP = 0.001, 0.9, 0.999, 1e-08, 0.01, 10

VMEM_LIMIT = 56 * 1024 * 1024
SUBLANES = 8
LANES = 128


def _params(*sem):
    return pltpu.CompilerParams(dimension_semantics=sem, vmem_limit_bytes=VMEM_LIMIT)


ONE_PASS, THREE_PASS, FULL_F32, EXACT_LHS, EXACT_RHS = 0, 1, 2, 3, 4


def _split3(v):
    hi = v.astype(BF16)
    r1 = v - hi.astype(F32)
    mid = r1.astype(BF16)
    return hi, mid, (r1 - mid.astype(F32)).astype(BF16)


def _mm_raw(a, b, kind, prec):
    nb = a.ndim - 2
    ca = a.ndim - 1 if kind[0] == "n" else a.ndim - 2
    cb = b.ndim - 2 if kind[1] == "n" else b.ndim - 1
    dims = (((ca,), (cb,)), (tuple(range(nb)),) * 2)
    if prec == FULL_F32:
        return lax.dot_general(a, b, dims, precision=HI, preferred_element_type=F32)
    dot = lambda p, q: lax.dot_general(p, q, dims, preferred_element_type=F32)
    ah, bh = a.astype(BF16), b.astype(BF16)
    if prec == ONE_PASS:
        return dot(ah, bh)
    if prec == EXACT_LHS:
        b1, b2, b3 = _split3(b)
        return dot(ah, b1) + (dot(ah, b2) + dot(ah, b3))
    if prec == EXACT_RHS:
        a1, a2, a3 = _split3(a)
        return dot(a1, bh) + (dot(a2, bh) + dot(a3, bh))
    al = (a - ah.astype(F32)).astype(BF16)
    bl = (b - bh.astype(F32)).astype(BF16)
    return dot(ah, bh) + (dot(ah, bl) + dot(al, bh))


@functools.partial(jax.custom_vjp, nondiff_argnums=(2, 3))
def _mm_vjp(a, b, kind, hi):
    return _mm_raw(a, b, kind, hi)


def _mm_vjp_fwd(a, b, kind, hi):
    return _mm_raw(a, b, kind, hi), (a, b)


def _mm_vjp_bwd(kind, hi, res, dc):
    a, b = res
    if hi in (EXACT_LHS, EXACT_RHS):
        assert kind == "nn"
        if hi == EXACT_LHS:
            return jnp.zeros_like(a), _mm_raw(a, dc, "tn", EXACT_LHS)
        return _mm_raw(dc, b, "nt", EXACT_RHS), jnp.zeros_like(b)
    if kind == "nn":
        return _mm_raw(dc, b, "nt", hi), _mm_raw(a, dc, "tn", hi)
    if kind == "nt":
        return _mm_raw(dc, b, "nn", hi), _mm_raw(dc, a, "tn", hi)
    return _mm_raw(b, dc, "nt", hi), _mm_raw(a, dc, "nn", hi)


_mm_vjp.defvjp(_mm_vjp_fwd, _mm_vjp_bwd)


def _lane_total(v):
    return jnp.broadcast_to(jnp.sum(v, axis=-1, keepdims=True), v.shape)


def _matmul(a, b, *, name, ta=False, tb=False, add=None, add_scale=1.0, tm=1024, tn=2048, tk=1024):
    m, k = (a.shape[1], a.shape[0]) if ta else a.shape
    n = b.shape[0] if tb else b.shape[1]
    tm, tn, tk = min(tm, m), min(tn, n), min(tk, k)
    tn = tn if n % tn == 0 else tn // 2
    assert m % tm == 0 and n % tn == 0 and k % tk == 0, (name, m, n, k)
    nk = k // tk
    dims = (((0 if ta else 1,), (1 if tb else 0,)), ((), ()))

    def body(*refs):
        a_ref, b_ref = refs[:2]
        o_ref = refs[-1]
        part = lax.dot_general(a_ref[...].astype(BF16), b_ref[...].astype(BF16), dims, preferred_element_type=F32)
        first = (lambda: part) if add is None else (lambda: part + add_scale * refs[2][...])
        if nk == 1:
            o_ref[...] = first()
        else:
            kk = pl.program_id(2)

            @pl.when(kk == 0)
            def _():
                o_ref[...] = first()

            @pl.when(kk > 0)
            def _():
                o_ref[...] += part

    a_spec = pl.BlockSpec((tk, tm), lambda i, j, kk: (kk, i)) if ta else pl.BlockSpec((tm, tk), lambda i, j, kk: (i, kk))
    b_spec = pl.BlockSpec((tn, tk), lambda i, j, kk: (j, kk)) if tb else pl.BlockSpec((tk, tn), lambda i, j, kk: (kk, j))
    o_spec = pl.BlockSpec((tm, tn), lambda i, j, kk: (i, j))
    in_specs = [a_spec, b_spec] + ([o_spec] if add is not None else [])
    args = (a, b) + ((add,) if add is not None else ())
    return pl.pallas_call(
        body, name=name, grid=(m // tm, n // tn, nk), in_specs=in_specs, out_specs=o_spec,
        out_shape=jax.ShapeDtypeStruct((m, n), F32),
        compiler_params=_params("parallel", "parallel", "arbitrary"))(*args)


HALO = SUBLANES


def _halo_specs(tt, width, col, nt):
    r = tt // HALO
    prev = pl.BlockSpec((HALO, width), lambda i: (jnp.maximum(i * r - 1, 0), col))
    nxt = pl.BlockSpec((HALO, width), lambda i: (jnp.minimum((i + 1) * r, nt * r - 1), col))
    return prev, nxt


def _shift_down(ext, k):
    return ext if k == 0 else pltpu.roll(ext, k, 0)


def _shift_up(ext, k):
    return ext if k == 0 else pltpu.roll(ext, ext.shape[0] - k, 0)


def _causal_conv(ext, w, taps):
    acc = None
    for j in range(taps):
        term = w[j:j + 1, :] * _shift_down(ext, taps - 1 - j)
        acc = term if acc is None else acc + term
    return acc[HALO:, :]


def _conv_a_fwd(proj_a, conv_w):
    t = proj_a.shape[0]
    tt = min(t, 256)
    nt = t // tt
    wdt = 1024

    def body(cur_ref, prev_ref, w_ref, y_ref):
        i = pl.program_id(0)
        cur = cur_ref[...]
        h, c, b, z = (cur[:, k * wdt:(k + 1) * wdt] for k in range(4))
        prev = prev_ref[...]
        u_prev = jnp.where(i > 0, prev[:, wdt:2 * wdt] * prev[:, 0:wdt], 0.0)
        ext = jnp.concatenate([u_prev, c * h], axis=0)
        conv = _causal_conv(ext, w_ref[...], 3)
        y_ref[...] = (b * conv * jax.nn.silu(z)).astype(BF16)

    prev_spec, _ = _halo_specs(tt, 4 * wdt, 0, nt)
    return pl.pallas_call(
        body, name="conv_a_fwd", grid=(nt,),
        in_specs=[pl.BlockSpec((tt, 4 * wdt), lambda i: (i, 0)), prev_spec, pl.BlockSpec((3, wdt), lambda i: (0, 0))],
        out_specs=pl.BlockSpec((tt, wdt), lambda i: (i, 0)),
        out_shape=jax.ShapeDtypeStruct((t, wdt), BF16), compiler_params=_params("parallel"))(proj_a, proj_a, conv_w)


def _conv_a_bwd(proj_a, conv_w, dy):
    t = proj_a.shape[0]
    tt = min(t, 256)
    nt = t // tt
    wdt = 1024

    def body(cur_ref, prev_ref, nxt_ref, w_ref, dy_ref, dyn_ref, d_ref, dw_ref):
        i = pl.program_id(0)
        w = w_ref[...]
        cur, prev, nxt = cur_ref[...], prev_ref[...], nxt_ref[...]
        split = lambda a: tuple(a[:, k * wdt:(k + 1) * wdt] for k in range(4))
        h, c, b, z = split(cur)
        hp, cp, _, _ = split(prev)
        hn, cn, bn, zn = split(nxt)
        u_prev = jnp.where(i > 0, cp * hp, 0.0)
        u_ext = jnp.concatenate([u_prev, c * h, cn * hn], axis=0)
        taps = [_shift_down(u_ext, 2 - j)[HALO:, :] for j in range(3)]
        conv = w[0:1, :] * taps[0] + w[1:2, :] * taps[1] + w[2:3, :] * taps[2]
        b_cn = jnp.concatenate([b, bn], axis=0)
        z_cn = jnp.concatenate([z, zn], axis=0)
        dy_cn = jnp.concatenate([dy_ref[...], jnp.where(i < nt - 1, dyn_ref[...], 0.0)], axis=0)
        sg = jax.nn.sigmoid(z_cn)
        silu = z_cn * sg
        d_conv = dy_cn * b_cn * silu
        db = (dy_cn * conv * silu)[:tt, :]
        dz = (dy_cn * b_cn * conv * (sg * (1.0 + z_cn * (1.0 - sg))))[:tt, :]
        du = None
        for j in range(3):
            term = w[j:j + 1, :] * _shift_up(d_conv, 2 - j)
            du = term if du is None else du + term
        du = du[:tt, :]
        d_ref[...] = jnp.concatenate([du * c, du * h, db, dz], axis=1).astype(BF16)

        @pl.when(i == 0)
        def _():
            dw_ref[...] = jnp.zeros_like(dw_ref)

        d_cur = d_conv[:tt, :]
        rows = [jnp.sum(d_cur * taps[j][:tt, :], axis=0, keepdims=True) for j in range(3)]
        dw_ref[0:3, :] += jnp.concatenate(rows, axis=0)

    prev_spec, nxt_spec = _halo_specs(tt, 4 * wdt, 0, nt)
    _, dyn_spec = _halo_specs(tt, wdt, 0, nt)
    return pl.pallas_call(
        body, name="conv_a_bwd", grid=(nt,),
        in_specs=[pl.BlockSpec((tt, 4 * wdt), lambda i: (i, 0)), prev_spec, nxt_spec,
                  pl.BlockSpec((3, wdt), lambda i: (0, 0)), pl.BlockSpec((tt, wdt), lambda i: (i, 0)), dyn_spec],
        out_specs=[pl.BlockSpec((tt, 4 * wdt), lambda i: (i, 0)), pl.BlockSpec((SUBLANES, wdt), lambda i: (0, 0))],
        out_shape=[jax.ShapeDtypeStruct((t, 4 * wdt), BF16), jax.ShapeDtypeStruct((SUBLANES, wdt), F32)],
        compiler_params=_params("arbitrary"))(proj_a, proj_a, proj_a, conv_w, dy, dy)


def _conv_b_fwd(proj_qkv, conv_w):
    t, width = proj_qkv.shape
    tt = min(t, 256)
    nt = t // tt
    wdt = 1024

    def body(cur_ref, prev_ref, w_ref, y_ref):
        i = pl.program_id(1)
        ext = jnp.concatenate([jnp.where(i > 0, prev_ref[...], 0.0), cur_ref[...]], axis=0)
        y_ref[...] = jax.nn.silu(_causal_conv(ext, w_ref[...], 4))

    r = tt // HALO
    return pl.pallas_call(
        body, name="conv_b_fwd", grid=(width // wdt, nt),
        in_specs=[pl.BlockSpec((tt, wdt), lambda j, i: (i, j)),
                  pl.BlockSpec((HALO, wdt), lambda j, i: (jnp.maximum(i * r - 1, 0), j)),
                  pl.BlockSpec((4, wdt), lambda j, i: (0, j))],
        out_specs=pl.BlockSpec((tt, wdt), lambda j, i: (i, j)),
        out_shape=jax.ShapeDtypeStruct((t, width), F32), compiler_params=_params("parallel", "parallel"))(
            proj_qkv, proj_qkv, conv_w)


def _conv_b_bwd(proj_qkv, conv_w, d_act, col, name):
    t = proj_qkv.shape[0]
    tt = min(t, 256)
    nt = t // tt
    wdt = 1024

    def body(cur_ref, prev_ref, nxt_ref, w_ref, da_ref, dan_ref, d_ref, dw_ref):
        i = pl.program_id(0)
        w = w_ref[...]
        u_ext = jnp.concatenate([jnp.where(i > 0, prev_ref[...], 0.0), cur_ref[...], nxt_ref[...]], axis=0)
        taps = [_shift_down(u_ext, 3 - j)[HALO:, :] for j in range(4)]
        conv = w[0:1, :] * taps[0] + w[1:2, :] * taps[1] + w[2:3, :] * taps[2] + w[3:4, :] * taps[3]
        da_cn = jnp.concatenate([da_ref[...], jnp.where(i < nt - 1, dan_ref[...], 0.0)], axis=0)
        sg = jax.nn.sigmoid(conv)
        d_conv = da_cn * (sg * (1.0 + conv * (1.0 - sg)))
        du = None
        for j in range(4):
            term = w[j:j + 1, :] * _shift_up(d_conv, 3 - j)
            du = term if du is None else du + term
        d_ref[...] = du[:tt, :].astype(BF16)

        @pl.when(i == 0)
        def _():
            dw_ref[...] = jnp.zeros_like(dw_ref)

        d_cur = d_conv[:tt, :]
        rows = [jnp.sum(d_cur * taps[j][:tt, :], axis=0, keepdims=True) for j in range(4)]
        dw_ref[0:4, :] += jnp.concatenate(rows, axis=0)

    prev_spec, nxt_spec = _halo_specs(tt, wdt, col, nt)
    _, dan_spec = _halo_specs(tt, wdt, 0, nt)
    return pl.pallas_call(
        body, name=name, grid=(nt,),
        in_specs=[pl.BlockSpec((tt, wdt), lambda i: (i, col)), prev_spec, nxt_spec,
                  pl.BlockSpec((4, wdt), lambda i: (0, col)), pl.BlockSpec((tt, wdt), lambda i: (i, 0)), dan_spec],
        out_specs=[pl.BlockSpec((tt, wdt), lambda i: (i, 0)), pl.BlockSpec((SUBLANES, wdt), lambda i: (0, 0))],
        out_shape=[jax.ShapeDtypeStruct((t, wdt), BF16), jax.ShapeDtypeStruct((SUBLANES, wdt), F32)],
        compiler_params=_params("arbitrary"))(proj_qkv, proj_qkv, proj_qkv, conv_w, d_act, d_act)


def _rms_gate(o, gn, z):
    on = o * lax.rsqrt(jnp.mean(o * o, axis=-1, keepdims=True) + NORM_EPS) * gn
    return on * jax.nn.silu(z)


GDN_PREP_ROWS = 1024


def _unit_lower_inverse(low):
    c = low.shape[-1]
    eye = lax.broadcasted_iota(jnp.int32, low.shape, low.ndim - 2) == lax.broadcasted_iota(jnp.int32, low.shape, low.ndim - 1)
    x = -low
    inv = eye.astype(F32) + x
    for _ in range(c.bit_length() - 2):
        x = _mm_raw(x, x, "nn", THREE_PASS)
        inv = inv + _mm_raw(inv, x, "nn", THREE_PASS)
    return inv


@jax.custom_vjp
def _known_inverse(low, inv):
    return inv


def _known_inverse_fwd(low, inv):
    return inv, inv


def _known_inverse_bwd(inv, d_inv):
    return -_mm_raw(_mm_raw(inv, d_inv, "tn", THREE_PASS), inv, "nt", THREE_PASS), jnp.zeros_like(inv)


_known_inverse.defvjp(_known_inverse_fwd, _known_inverse_bwd)


def _gdn_prep(mm, qa, ka, va, braw, araw, alog, dtb, inv_kept=None):
    n, c, _ = qa.shape
    q = qa * lax.rsqrt(jnp.sum(qa * qa, axis=-1, keepdims=True) + 1e-6) * (HEAD_DIM ** -0.5)
    k = ka * lax.rsqrt(jnp.sum(ka * ka, axis=-1, keepdims=True) + 1e-6)
    beta = jax.nn.sigmoid(braw)
    g = -jnp.exp(alog) * jax.nn.softplus(araw + dtb)
    ri = lax.broadcasted_iota(jnp.int32, (n, c, c), 1)
    ci = lax.broadcasted_iota(jnp.int32, (n, c, c), 2)
    incl, strict, eye = ri >= ci, ri > ci, ri == ci
    gc = mm(incl.astype(F32), g, "nn", EXACT_LHS)
    gc_i = gc[:, :, :c]
    gc_j = mm(jnp.ones((n, c, c), F32), jnp.where(eye, gc_i, 0.0), "nn", EXACT_LHS)
    decay = jnp.where(incl, jnp.exp(jnp.where(incl, gc_i - gc_j, 0.0)), 0.0)
    kb = k * beta
    low = jnp.where(strict, mm(kb, k, "nt", ONE_PASS) * decay, 0.0)
    inv = _unit_lower_inverse(low) if inv_kept is None else _known_inverse(low, inv_kept)
    egc = jnp.exp(gc)
    u = mm(inv, va * beta, "nn", THREE_PASS)
    w = mm(inv, kb * egc, "nn", THREE_PASS)
    attn = jnp.where(incl, mm(q, k, "nt", ONE_PASS) * decay, 0.0)
    g_last = jnp.sum(g, axis=1, keepdims=True)
    outs = (u, w, q * egc, k * jnp.exp(g_last - gc), attn, jnp.exp(g_last))
    return outs + (inv,) if inv_kept is None else outs


def _gdn_scan(mm, u, w, qd, kd, attn, egl, z, gn, state):
    v_new = u - mm(w, state, "nn", ONE_PASS)
    o = mm(qd, state, "nn", ONE_PASS) + mm(attn, v_new, "nn", ONE_PASS)
    new_state = state * egl + mm(kd, v_new, "tn", ONE_PASS)
    return _rms_gate(o, gn, z), new_state


def _chunks(ref_value, n, c):
    return ref_value.reshape(n, c, ref_value.shape[-1])


def _by_head(ref, rows, heads):
    return jnp.stack([ref[rows, pl.ds(h * HEAD_DIM, HEAD_DIM)] for h in range(heads)])


def _store_heads(ref, rows, value):
    for h in range(value.shape[0]):
        ref[rows, pl.ds(h * HEAD_DIM, HEAD_DIM)] = value[h]


def _gdn_prep_specs(tb, nt_unused=None):
    col = lambda off: pl.BlockSpec((tb, HEAD_DIM), lambda h, i: (i, off + h))
    rep = pl.BlockSpec((1, tb, LANES), lambda h, i: (h, i, 0))
    par = pl.BlockSpec((1, SUBLANES, LANES), lambda h, i: (h, 0, 0))
    att = pl.BlockSpec((1, tb, GDN_CHUNK), lambda h, i: (h, i, 0))
    egl = pl.BlockSpec((1, tb // GDN_CHUNK, SUBLANES, LANES), lambda h, i: (h, i, 0, 0))
    return col, rep, par, att, egl


def _gdn_prep_fwd(qkv_act, braw, araw, alog, dtb):
    t = qkv_act.shape[0]
    tb = min(t, GDN_PREP_ROWS)
    nt, nc = t // tb, tb // GDN_CHUNK
    width = GDN_HEADS * HEAD_DIM

    def body(q_ref, k_ref, v_ref, br_ref, ar_ref, al_ref, dt_ref, u_ref, w_ref, qd_ref, kd_ref, at_ref, eg_ref, inv_ref):
        ch = lambda r: _chunks(r, nc, GDN_CHUNK)
        u, w, qd, kd, attn, egl, inv = _gdn_prep(_mm_raw, ch(q_ref[...]), ch(k_ref[...]), ch(v_ref[...]), ch(br_ref[0]),
                                                 ch(ar_ref[0]), al_ref[0, 0:1, :], dt_ref[0, 0:1, :])
        u_ref[...] = u.reshape(tb, HEAD_DIM)
        w_ref[...] = w.reshape(tb, HEAD_DIM).astype(BF16)
        qd_ref[...] = qd.reshape(tb, HEAD_DIM).astype(BF16)
        kd_ref[...] = kd.reshape(tb, HEAD_DIM).astype(BF16)
        at_ref[0] = attn.reshape(tb, GDN_CHUNK).astype(BF16)
        eg_ref[0] = jnp.broadcast_to(egl, (nc, SUBLANES, LANES))
        inv_ref[0] = inv.reshape(tb, GDN_CHUNK)

    col, rep, par, att, egl = _gdn_prep_specs(tb)
    h = GDN_HEADS
    return pl.pallas_call(
        body, name="gdn_prep_fwd", grid=(h, nt),
        in_specs=[col(0), col(h), col(2 * h), rep, rep, par, par],
        out_specs=[col(0), col(0), col(0), col(0), att, egl, att],
        out_shape=[jax.ShapeDtypeStruct((t, width), F32)] + [jax.ShapeDtypeStruct((t, width), BF16)] * 3
        + [jax.ShapeDtypeStruct((h, t, GDN_CHUNK), BF16), jax.ShapeDtypeStruct((h, t // GDN_CHUNK, SUBLANES, LANES), F32),
           jax.ShapeDtypeStruct((h, t, GDN_CHUNK), F32)],
        compiler_params=_params("parallel", "parallel"))(qkv_act, qkv_act, qkv_act, braw, araw, alog, dtb)


def _gdn_prep_bwd(qkv_act, braw, araw, alog, dtb, inv, du, dw, dqd, dkd, dattn, degl):
    t = qkv_act.shape[0]
    tb = min(t, GDN_PREP_ROWS)
    nt, nc = t // tb, tb // GDN_CHUNK
    width = GDN_HEADS * HEAD_DIM

    def body(q_ref, k_ref, v_ref, br_ref, ar_ref, al_ref, dt_ref, inv_ref, du_ref, dw_ref, dqd_ref, dkd_ref, dat_ref,
             deg_ref, dq_ref, dk_ref, dv_ref, dbr_ref, dar_ref, dal_ref, ddt_ref):
        @pl.when(pl.program_id(1) == 0)
        def _():
            dal_ref[...] = jnp.zeros_like(dal_ref)
            ddt_ref[...] = jnp.zeros_like(ddt_ref)

        ch = lambda r: _chunks(r, nc, GDN_CHUNK)
        _, vjp = jax.vjp(functools.partial(_gdn_prep, _mm_vjp, inv_kept=ch(inv_ref[0])), ch(q_ref[...]), ch(k_ref[...]),
                         ch(v_ref[...]), ch(br_ref[0]), ch(ar_ref[0]), al_ref[0, 0:1, :], dt_ref[0, 0:1, :])
        dq, dk, dv, dbr, dar, dal, ddt = vjp((ch(du_ref[...]), ch(dw_ref[...]), ch(dqd_ref[...]), ch(dkd_ref[...]),
                                              ch(dat_ref[0]), deg_ref[0][:, 0:1, :]))
        dq_ref[...] = dq.reshape(tb, HEAD_DIM)
        dk_ref[...] = dk.reshape(tb, HEAD_DIM)
        dv_ref[...] = dv.reshape(tb, HEAD_DIM)
        dbr_ref[0] = _lane_total(dbr.reshape(tb, LANES))
        dar_ref[0] = _lane_total(dar.reshape(tb, LANES))
        dal_ref[0, 0:1, :] += _lane_total(dal)
        ddt_ref[0, 0:1, :] += _lane_total(ddt)

    col, rep, par, att, egl = _gdn_prep_specs(tb)
    h = GDN_HEADS
    return pl.pallas_call(
        body, name="gdn_prep_bwd", grid=(h, nt),
        in_specs=[col(0), col(h), col(2 * h), rep, rep, par, par, att, col(0), col(0), col(0), col(0), att, egl],
        out_specs=[col(0), col(0), col(0), rep, rep, par, par],
        out_shape=[jax.ShapeDtypeStruct((t, width), F32)] * 3 + [jax.ShapeDtypeStruct((h, t, LANES), F32)] * 2
        + [jax.ShapeDtypeStruct((h, SUBLANES, LANES), F32)] * 2,
        compiler_params=_params("parallel", "arbitrary"))(qkv_act, qkv_act, qkv_act, braw, araw, alog, dtb, inv,
                                                         du, dw, dqd, dkd, dattn, degl)


def _scan_specs(tb, heads, chunk, rev, nt):
    ti = (lambda i: nt - 1 - i) if rev else (lambda i: i)
    row = pl.BlockSpec((tb, heads * HEAD_DIM), lambda i: (ti(i), 0))
    att = pl.BlockSpec((heads, tb, chunk), lambda i: (0, ti(i), 0))
    egl = pl.BlockSpec((heads, tb // chunk, SUBLANES, LANES), lambda i: (0, ti(i), 0, 0))
    hist = pl.BlockSpec((heads, tb // chunk, HEAD_DIM, HEAD_DIM), lambda i: (0, ti(i), 0, 0))
    gn = pl.BlockSpec((SUBLANES, LANES), lambda i: (0, 0))
    return row, att, egl, hist, gn


def _gdn_scan_fwd(u, w, qd, kd, attn, egl, zb, gn):
    t = u.shape[0]
    tb = min(t, 256)
    nt, nc = t // tb, tb // GDN_CHUNK
    nh = GDN_HEADS

    def body(u_ref, w_ref, qd_ref, kd_ref, at_ref, eg_ref, z_ref, gn_ref, y_ref, hist_ref, s_ref):
        @pl.when(pl.program_id(0) == 0)
        def _():
            s_ref[...] = jnp.zeros_like(s_ref)

        g = gn_ref[0:1, :]
        state = s_ref[...]
        for c in range(nc):
            rows = pl.ds(c * GDN_CHUNK, GDN_CHUNK)
            heads = lambda r: _by_head(r, rows, nh)
            hist_ref[:, c] = state
            y, state = _gdn_scan(_mm_raw, heads(u_ref), heads(w_ref), heads(qd_ref), heads(kd_ref), at_ref[:, rows, :],
                                 eg_ref[:, c, 0:1, :], heads(z_ref), g, state)
            _store_heads(y_ref, rows, y.astype(BF16))
        s_ref[...] = state

    row, att, egs, hist, gns = _scan_specs(tb, nh, GDN_CHUNK, False, nt)
    return pl.pallas_call(
        body, name="gdn_scan_fwd", grid=(nt,), in_specs=[row, row, row, row, att, egs, row, gns], out_specs=[row, hist],
        out_shape=[jax.ShapeDtypeStruct((t, nh * HEAD_DIM), BF16),
                   jax.ShapeDtypeStruct((nh, t // GDN_CHUNK, HEAD_DIM, HEAD_DIM), F32)],
        scratch_shapes=[pltpu.VMEM((nh, HEAD_DIM, HEAD_DIM), F32)],
        compiler_params=_params("arbitrary"))(u, w, qd, kd, attn, egl, zb, gn)


def _gdn_scan_bwd(u, w, qd, kd, attn, egl, zb, gn, hist, dy):
    t = u.shape[0]
    tb = min(t, 256)
    nt, nc = t // tb, tb // GDN_CHUNK
    nh = GDN_HEADS

    def body(u_ref, w_ref, qd_ref, kd_ref, at_ref, eg_ref, z_ref, gn_ref, hist_ref, dy_ref,
             du_ref, dw_ref, dqd_ref, dkd_ref, dat_ref, deg_ref, dz_ref, dgn_ref, ds_ref):
        @pl.when(pl.program_id(0) == 0)
        def _():
            ds_ref[...] = jnp.zeros_like(ds_ref)
            dgn_ref[...] = jnp.zeros_like(dgn_ref)

        g = gn_ref[0:1, :]
        d_state = ds_ref[...]
        for c in reversed(range(nc)):
            rows = pl.ds(c * GDN_CHUNK, GDN_CHUNK)
            heads = lambda r: _by_head(r, rows, nh).astype(F32)
            _, vjp = jax.vjp(functools.partial(_gdn_scan, _mm_vjp), heads(u_ref), heads(w_ref), heads(qd_ref),
                             heads(kd_ref), at_ref[:, rows, :].astype(F32), eg_ref[:, c, 0:1, :], heads(z_ref), g,
                             hist_ref[:, c])
            du, dw, dqd, dkd, dat, deg, dz, dgn, d_state = vjp((heads(dy_ref), d_state))
            _store_heads(du_ref, rows, du)
            _store_heads(dw_ref, rows, dw)
            _store_heads(dqd_ref, rows, dqd)
            _store_heads(dkd_ref, rows, dkd)
            dat_ref[:, rows, :] = dat
            deg_ref[:, c] = jnp.broadcast_to(deg, (nh, SUBLANES, LANES))
            _store_heads(dz_ref, rows, dz.astype(BF16))
            dgn_ref[0:1, :] += dgn
        ds_ref[...] = d_state

    row, att, egs, hists, gns = _scan_specs(tb, nh, GDN_CHUNK, True, nt)
    wide = jax.ShapeDtypeStruct((t, nh * HEAD_DIM), F32)
    return pl.pallas_call(
        body, name="gdn_scan_bwd", grid=(nt,),
        in_specs=[row, row, row, row, att, egs, row, gns, hists, row],
        out_specs=[row, row, row, row, att, egs, row, gns],
        out_shape=[wide] * 4 + [jax.ShapeDtypeStruct((nh, t, GDN_CHUNK), F32),
                                jax.ShapeDtypeStruct((nh, t // GDN_CHUNK, SUBLANES, LANES), F32),
                                jax.ShapeDtypeStruct((t, nh * HEAD_DIM), BF16),
                                jax.ShapeDtypeStruct((SUBLANES, LANES), F32)],
        scratch_shapes=[pltpu.VMEM((nh, HEAD_DIM, HEAD_DIM), F32)],
        compiler_params=_params("arbitrary"))(u, w, qd, kd, attn, egl, zb, gn, hist, dy)


def _hgrn_prep(mm, qr, fr, lbl):
    n, c, _ = qr.shape
    lb = jax.nn.sigmoid(lbl[1:2, :] - lbl[0:1, :])
    f = lb + (1.0 - lb) * jax.nn.sigmoid(fr)
    q = jax.nn.silu(qr)
    k = 1.0 - f
    logf = jnp.log(f)
    ri = lax.broadcasted_iota(jnp.int32, (n, c, c), 1)
    ci = lax.broadcasted_iota(jnp.int32, (n, c, c), 2)
    b = mm((ri >= ci).astype(F32), logf, "nn", EXACT_LHS)
    attn = _hgrn_attn(mm, q, k, b)
    b_last = jnp.sum(logf, axis=1, keepdims=True)
    return q * jnp.exp(b), k * jnp.exp(b_last - b), attn, jnp.exp(b_last)


HGRN_SUB = 8


@functools.partial(jax.custom_vjp, nondiff_argnums=(1,))
def _roll_rows(x, shift):
    return pltpu.roll(x, shift, x.ndim - 2)


def _roll_rows_fwd(x, shift):
    return _roll_rows(x, shift), None


def _roll_rows_bwd(shift, _, d):
    return (pltpu.roll(d, d.shape[-2] - shift, d.ndim - 2),)


_roll_rows.defvjp(_roll_rows_fwd, _roll_rows_bwd)


@jax.custom_vjp
def _exp_clamped(v):
    return jnp.exp(jnp.minimum(v, 0.0))


def _exp_clamped_fwd(v):
    out = jnp.exp(jnp.minimum(v, 0.0))
    return out, out


def _exp_clamped_bwd(out, d):
    return (d * out,)


_exp_clamped.defvjp(_exp_clamped_fwd, _exp_clamped_bwd)


def _hgrn_attn(mm, q, k, b):
    n, c, d = q.shape
    sb = HGRN_SUB
    sub = lambda a: a.reshape(n * c // sb, sb, d)
    qs, ks, bs = sub(q), sub(k), sub(b)
    row = lax.broadcasted_iota(jnp.int32, (n, c, c), 1)
    col = lax.broadcasted_iota(jnp.int32, (n, c, c), 2)
    same_block = (row & -sb) == (col & -sb)
    attn = None
    for delta in range(sb):
        if delta == 0:
            prod = qs * ks
        else:
            prod = qs * _roll_rows(ks, delta) * _exp_clamped(bs - _roll_rows(bs, delta))
        sums = jnp.sum(prod, axis=-1, keepdims=True).reshape(n, c, 1)
        term = jnp.where(same_block & (row - col == delta), sums, 0.0)
        attn = term if attn is None else attn + term
    far = [jnp.zeros((n, sb, c), F32)]
    col8 = lax.broadcasted_iota(jnp.int32, (n, sb, c), 2)
    for i in range(1, c // sb):
        r0 = i * sb
        bi = b[:, r0:r0 + sb, :]
        ref = bi[:, 0:1, :]
        part = mm(q[:, r0:r0 + sb, :] * jnp.exp(bi - ref), k * _exp_clamped(ref - b), "nt", ONE_PASS)
        far.append(jnp.where(col8 < r0, part, 0.0))
    return attn + jnp.concatenate(far, axis=1)


def _hgrn_scan(mm, qe, kd, attn, ebl, iv, z, gn, state):
    o = mm(qe, state, "nt", ONE_PASS) + mm(attn, iv, "nn", ONE_PASS)
    new_state = state * ebl + mm(iv, kd, "tn", ONE_PASS)
    return _rms_gate(o, gn, z), new_state


def _hgrn_prep_specs(tb):
    col = pl.BlockSpec((tb, HEAD_DIM), lambda h, i: (i, h))
    lbs = pl.BlockSpec((2, HEAD_DIM), lambda h, i: (0, h))
    att = pl.BlockSpec((1, tb, HGRN_CHUNK), lambda h, i: (h, i, 0))
    ebl = pl.BlockSpec((1, tb // HGRN_CHUNK, SUBLANES, LANES), lambda h, i: (h, i, 0, 0))
    return col, lbs, att, ebl


def _hgrn_prep_fwd(qr, fr, lower_bounds):
    t = qr.shape[0]
    tb = min(t, 256)
    nt, nc = t // tb, tb // HGRN_CHUNK
    hh = HGRN_HEADS

    def body(q_ref, f_ref, lb_ref, qe_ref, kd_ref, at_ref, eb_ref):
        ch = lambda r: _chunks(r, nc, HGRN_CHUNK)
        qe, kd, attn, ebl = _hgrn_prep(_mm_raw, ch(q_ref[...]), ch(f_ref[...]), lb_ref[...])
        qe_ref[...] = qe.reshape(tb, HEAD_DIM).astype(BF16)
        kd_ref[...] = kd.reshape(tb, HEAD_DIM).astype(BF16)
        at_ref[0] = attn.reshape(tb, HGRN_CHUNK).astype(BF16)
        eb_ref[0] = jnp.broadcast_to(ebl, (nc, SUBLANES, LANES))

    col, lbs, att, ebs = _hgrn_prep_specs(tb)
    return pl.pallas_call(
        body, name="hgrn_prep_fwd", grid=(hh, nt), in_specs=[col, col, lbs], out_specs=[col, col, att, ebs],
        out_shape=[jax.ShapeDtypeStruct((t, HGRN_WIDTH), BF16)] * 2
        + [jax.ShapeDtypeStruct((hh, t, HGRN_CHUNK), BF16), jax.ShapeDtypeStruct((hh, t // HGRN_CHUNK, SUBLANES, LANES), F32)],
        compiler_params=_params("parallel", "parallel"))(qr, fr, lower_bounds)


def _hgrn_prep_bwd(qr, fr, lower_bounds, dqe, dkd, dattn, debl):
    t = qr.shape[0]
    tb = min(t, 256)
    nt, nc = t // tb, tb // HGRN_CHUNK
    hh = HGRN_HEADS

    def body(q_ref, f_ref, lb_ref, dqe_ref, dkd_ref, dat_ref, deb_ref, dq_ref, df_ref, dlb_ref):
        @pl.when(pl.program_id(1) == 0)
        def _():
            dlb_ref[...] = jnp.zeros_like(dlb_ref)

        ch = lambda r: _chunks(r, nc, HGRN_CHUNK)
        _, vjp = jax.vjp(functools.partial(_hgrn_prep, _mm_vjp), ch(q_ref[...]), ch(f_ref[...]), lb_ref[...])
        dq, df, dlb = vjp((ch(dqe_ref[...]), ch(dkd_ref[...]), ch(dat_ref[0]), deb_ref[0][:, 0:1, :]))
        dq_ref[...] = dq.reshape(tb, HEAD_DIM).astype(BF16)
        df_ref[...] = df.reshape(tb, HEAD_DIM).astype(BF16)
        dlb_ref[...] += dlb

    col, lbs, att, ebs = _hgrn_prep_specs(tb)
    return pl.pallas_call(
        body, name="hgrn_prep_bwd", grid=(hh, nt), in_specs=[col, col, lbs, col, col, att, ebs],
        out_specs=[col, col, lbs],
        out_shape=[jax.ShapeDtypeStruct((t, HGRN_WIDTH), BF16)] * 2 + [jax.ShapeDtypeStruct((2, HGRN_WIDTH), F32)],
        compiler_params=_params("parallel", "arbitrary"))(qr, fr, lower_bounds, dqe, dkd, dattn, debl)


def _hgrn_scan_fwd(qe, kd, attn, ebl, iv, z, gn):
    t = qe.shape[0]
    tb = min(t, 128)
    nt, nc = t // tb, tb // HGRN_CHUNK
    hh = HGRN_HEADS

    def body(qe_ref, kd_ref, at_ref, eb_ref, i_ref, z_ref, gn_ref, y_ref, hist_ref, s_ref):
        @pl.when(pl.program_id(0) == 0)
        def _():
            s_ref[...] = jnp.zeros_like(s_ref)

        g = gn_ref[0:1, :]
        state = s_ref[...]
        for c in range(nc):
            rows = pl.ds(c * HGRN_CHUNK, HGRN_CHUNK)
            heads = lambda r: _by_head(r, rows, hh)
            hist_ref[:, c] = state
            y, state = _hgrn_scan(_mm_raw, heads(qe_ref), heads(kd_ref), at_ref[:, rows, :], eb_ref[:, c, 0:1, :],
                                  heads(i_ref), heads(z_ref), g, state)
            _store_heads(y_ref, rows, y.astype(BF16))
        s_ref[...] = state

    row, att, ebs, hist, gns = _scan_specs(tb, hh, HGRN_CHUNK, False, nt)
    return pl.pallas_call(
        body, name="hgrn_scan_fwd", grid=(nt,), in_specs=[row, row, att, ebs, row, row, gns], out_specs=[row, hist],
        out_shape=[jax.ShapeDtypeStruct((t, HGRN_WIDTH), BF16),
                   jax.ShapeDtypeStruct((hh, t // HGRN_CHUNK, HEAD_DIM, HEAD_DIM), F32)],
        scratch_shapes=[pltpu.VMEM((hh, HEAD_DIM, HEAD_DIM), F32)],
        compiler_params=_params("arbitrary"))(qe, kd, attn, ebl, iv, z, gn)


def _hgrn_scan_bwd(qe, kd, attn, ebl, iv, z, gn, hist, dy):
    t = qe.shape[0]
    tb = min(t, 128)
    nt, nc = t // tb, tb // HGRN_CHUNK
    hh = HGRN_HEADS

    def body(qe_ref, kd_ref, at_ref, eb_ref, i_ref, z_ref, gn_ref, hist_ref, dy_ref,
             dqe_ref, dkd_ref, dat_ref, deb_ref, di_ref, dz_ref, dgn_ref, ds_ref):
        @pl.when(pl.program_id(0) == 0)
        def _():
            ds_ref[...] = jnp.zeros_like(ds_ref)
            dgn_ref[...] = jnp.zeros_like(dgn_ref)

        g = gn_ref[0:1, :]
        d_state = ds_ref[...]
        for c in reversed(range(nc)):
            rows = pl.ds(c * HGRN_CHUNK, HGRN_CHUNK)
            heads = lambda r: _by_head(r, rows, hh).astype(F32)
            _, vjp = jax.vjp(functools.partial(_hgrn_scan, _mm_vjp), heads(qe_ref), heads(kd_ref),
                             at_ref[:, rows, :].astype(F32), eb_ref[:, c, 0:1, :], heads(i_ref), heads(z_ref), g,
                             hist_ref[:, c])
            dqe, dkd, dat, deb, di, dz, dgn, d_state = vjp((heads(dy_ref), d_state))
            _store_heads(dqe_ref, rows, dqe)
            _store_heads(dkd_ref, rows, dkd)
            dat_ref[:, rows, :] = dat
            deb_ref[:, c] = jnp.broadcast_to(deb, (hh, SUBLANES, LANES))
            _store_heads(di_ref, rows, di.astype(BF16))
            _store_heads(dz_ref, rows, dz.astype(BF16))
            dgn_ref[0:1, :] += dgn
        ds_ref[...] = d_state

    row, att, ebs, hists, gns = _scan_specs(tb, hh, HGRN_CHUNK, True, nt)
    wide = lambda dt: jax.ShapeDtypeStruct((t, HGRN_WIDTH), dt)
    return pl.pallas_call(
        body, name="hgrn_scan_bwd", grid=(nt,),
        in_specs=[row, row, att, ebs, row, row, gns, hists, row],
        out_specs=[row, row, att, ebs, row, row, gns],
        out_shape=[wide(F32), wide(F32), jax.ShapeDtypeStruct((hh, t, HGRN_CHUNK), F32),
                   jax.ShapeDtypeStruct((hh, t // HGRN_CHUNK, SUBLANES, LANES), F32), wide(BF16), wide(BF16),
                   jax.ShapeDtypeStruct((SUBLANES, LANES), F32)],
        scratch_shapes=[pltpu.VMEM((hh, HEAD_DIM, HEAD_DIM), F32)],
        compiler_params=_params("arbitrary"))(qe, kd, attn, ebl, iv, z, gn, hist, dy)


def _layer_norm(pre, g, b):
    mu = jnp.mean(pre, axis=-1, keepdims=True)
    d = pre - mu
    var = jnp.mean(d * d, axis=-1, keepdims=True)
    return d * lax.rsqrt(var + NORM_EPS) * g + b


def _lnpl_fwd(xin, s, p, wg, wpl, ln_g, ln_b):
    t = xin.shape[0]
    tt = min(t, 256)

    def body(x_ref, s_ref, p_ref, wg_ref, wpl_ref, g_ref, b_ref, o_ref, ob_ref):
        xn = _layer_norm(DEEPNORM_ALPHA * x_ref[...] + s_ref[...], g_ref[...], b_ref[...])
        gate = jax.nn.sigmoid(_mm_raw(xn, wg_ref[...], "nn", False))
        out = xn + _mm_raw(p_ref[...], wpl_ref[...], "nn", False) * gate
        o_ref[...] = out
        ob_ref[...] = out.astype(BF16)

    row = lambda w: pl.BlockSpec((tt, w), lambda i: (i, 0))
    full = lambda a: pl.BlockSpec(a.shape, lambda i: (0, 0))
    return pl.pallas_call(
        body, name="lnpl_fwd", grid=(t // tt,),
        in_specs=[row(D_MODEL), row(D_MODEL), row(PL_DIM), full(wg), full(wpl), full(ln_g), full(ln_b)],
        out_specs=[row(D_MODEL), row(D_MODEL)],
        out_shape=[jax.ShapeDtypeStruct((t, D_MODEL), F32), jax.ShapeDtypeStruct((t, D_MODEL), BF16)],
        compiler_params=_params("parallel"))(xin, s, p, wg, wpl, ln_g, ln_b)


def _lnpl_bwd(xin, s, p, wg, wpl, ln_g, ln_b, upstream, last, name):
    t = xin.shape[0]
    tt = min(t, 256)

    def body(x_ref, s_ref, p_ref, wg_ref, wpl_ref, g_ref, b_ref, up_ref,
             dpre_ref, dwg_ref, dwpl_ref, dg_ref, db_ref, loss_ref):
        @pl.when(pl.program_id(0) == 0)
        def _():
            for r in (dwg_ref, dwpl_ref, dg_ref, db_ref, loss_ref):
                r[...] = jnp.zeros_like(r)

        pre = DEEPNORM_ALPHA * x_ref[...] + s_ref[...]
        xn, ln_vjp = jax.vjp(_layer_norm, pre, g_ref[...], b_ref[...])
        gate = jax.nn.sigmoid(_mm_raw(xn, wg_ref[...], "nn", False))
        plv = _mm_raw(p_ref[...], wpl_ref[...], "nn", False)
        if last:
            err = xn + plv * gate - up_ref[...]
            dout = err * (1.0 / D_MODEL)
            tot = jnp.sum(jnp.sum(err * err, axis=1, keepdims=True), axis=0, keepdims=True) * (0.5 / D_MODEL)
            loss_ref[...] += jnp.broadcast_to(tot, loss_ref.shape)
        else:
            dout = up_ref[...]
        dplv = dout * gate
        dlogits = dout * plv * gate * (1.0 - gate)
        dwg_ref[...] += _mm_raw(xn, dlogits, "tn", False)
        dwpl_ref[...] += _mm_raw(p_ref[...], dplv, "tn", False)
        dxn = dout + _mm_raw(dlogits, wg_ref[...], "nt", False)
        dpre, dg, db = ln_vjp(dxn)
        dpre_ref[...] = dpre
        dg_ref[...] += dg
        db_ref[...] += db

    row = lambda w: pl.BlockSpec((tt, w), lambda i: (i, 0))
    full = lambda shape: pl.BlockSpec(shape, lambda i: (0, 0))
    return pl.pallas_call(
        body, name=name, grid=(t // tt,),
        in_specs=[row(D_MODEL), row(D_MODEL), row(PL_DIM), full(wg.shape), full(wpl.shape), full(ln_g.shape),
                  full(ln_b.shape), row(D_MODEL)],
        out_specs=[row(D_MODEL), full(wg.shape), full(wpl.shape), full(ln_g.shape), full(ln_b.shape),
                   full((SUBLANES, LANES))],
        out_shape=[jax.ShapeDtypeStruct((t, D_MODEL), F32), jax.ShapeDtypeStruct(wg.shape, F32),
                   jax.ShapeDtypeStruct(wpl.shape, F32), jax.ShapeDtypeStruct(ln_g.shape, F32),
                   jax.ShapeDtypeStruct(ln_b.shape, F32), jax.ShapeDtypeStruct((SUBLANES, LANES), F32)],
        compiler_params=_params("arbitrary"))(xin, s, p, wg, wpl, ln_g, ln_b, upstream)


def _pack_tail(dbr, dar):
    nh, t, _ = dbr.shape
    tt = min(t, 512)

    def body(b_ref, a_ref, o_ref):
        lane = lax.broadcasted_iota(jnp.int32, (tt, LANES), 1)
        acc = jnp.zeros((tt, LANES), F32)
        for h in range(nh):
            acc = jnp.where(lane == h, b_ref[h], acc)
            acc = jnp.where(lane == nh + h, a_ref[h], acc)
        o_ref[...] = acc.astype(BF16)

    spec = pl.BlockSpec((nh, tt, LANES), lambda i: (0, i, 0))
    return pl.pallas_call(
        body, name="pack_tail", grid=(t // tt,), in_specs=[spec, spec], out_specs=pl.BlockSpec((tt, LANES), lambda i: (i, 0)),
        out_shape=jax.ShapeDtypeStruct((t, LANES), BF16), compiler_params=_params("parallel"))(dbr, dar)


def _rep_rows(v):
    return jnp.broadcast_to(v.reshape(1, LANES), (SUBLANES, LANES))


def _rep_heads(v):
    return jnp.broadcast_to(v.reshape(-1, 1, 1), (v.shape[0], SUBLANES, LANES))


def _col_range(stacked, lo, hi):
    c = stacked.shape[2]
    parts = [stacked[s, :, max(lo, s * c) - s * c:min(hi, (s + 1) * c) - s * c]
             for s in range(4) if max(lo, s * c) < min(hi, (s + 1) * c)]
    return parts[0] if len(parts) == 1 else jnp.concatenate(parts, axis=1)


def _col_shards(pieces, c):
    shards, offs, o = [], [], 0
    for pc in pieces:
        offs.append(o)
        o += pc.shape[1]
    for s in range(4):
        lo, hi = s * c, (s + 1) * c
        parts = [pc[:, max(lo, o) - o:min(hi, o + pc.shape[1]) - o] for pc, o in zip(pieces, offs)
                 if max(lo, o) < min(hi, o + pc.shape[1])]
        shards.append(parts[0] if len(parts) == 1 else jnp.concatenate(parts, axis=1))
    return jnp.stack(shards)


def _local_step(x, p, target, w, late_weights, early_grads_ready, start_token):
    a = DEEPNORM_ALPHA
    nh = GDN_HEADS
    xb = (x + start_token).astype(BF16)
    wie = w["w_in_even"]
    w_a, w_qkv, w_zb = _col_range(wie, 0, 4096), _col_range(wie, 4096, 7168), _col_range(wie, 7168, 8192)
    w_tail = jnp.pad(_col_range(wie, 8192, 8192 + 2 * nh), ((0, 0), (0, LANES - 2 * nh)))
    conv_a_w, conv_b_w = w["conv_a_w"], w["conv_b_w"]
    ln_g0, ln_b0, ln_g1, ln_b1 = (v.reshape(1, D_MODEL) for v in (w["ln_g"][0], w["ln_b"][0], w["ln_g"][1], w["ln_b"][1]))
    alog, dtb = _rep_heads(w["a_log"].reshape(nh)), _rep_heads(w["dt_bias"].reshape(nh))
    gdn_g, hgrn_g = _rep_rows(w["gdn_norm_g"]), _rep_rows(w["hgrn_norm_g"])

    proj_a = _matmul(xb, w_a, name="fwd_proj_a")
    proj_qkv = _matmul(xb, w_qkv, name="fwd_proj_qkv")
    proj_zb = _matmul(xb, w_zb, name="fwd_proj_zb")
    proj_tail = _matmul(xb, w_tail, name="fwd_proj_tail")
    rep = lambda cols: jnp.broadcast_to(cols.T[:, :, None], (nh, cols.shape[0], LANES))
    braw, araw = rep(proj_tail[:, :nh]), rep(proj_tail[:, nh:2 * nh])
    y_a = _conv_a_fwd(proj_a, conv_a_w)
    qkv_act = _conv_b_fwd(proj_qkv, conv_b_w)
    *gdn_pre, gdn_inv = _gdn_prep_fwd(qkv_act, braw, araw, alog, dtb)
    y_b, gdn_hist = _gdn_scan_fwd(*gdn_pre, proj_zb, gdn_g)
    w = {**w, **late_weights(y_b)}
    woe, wio, woo = w["w_out_even"], w["w_in_odd"], w["w_out_odd"]
    s0 = _matmul(y_b, woe[1024:], name="fwd_out_even_b", add=_matmul(y_a, woe[:1024], name="fwd_out_even_a"))
    x1, x1b = _lnpl_fwd(x, s0, p[0], w["w_pl_gate"][0], w["w_pl"][0], ln_g0, ln_b0)
    proj_o = [_matmul(x1b, wio[j], name=f"fwd_proj_odd{j}") for j in range(4)]
    hgrn_pre = _hgrn_prep_fwd(proj_o[0], proj_o[1], w["lower_bounds"])
    y_o, hgrn_hist = _hgrn_scan_fwd(*hgrn_pre, proj_o[2], proj_o[3], hgrn_g)
    s1 = _matmul(y_o, woo, name="fwd_out_odd")

    g = {}
    dpre1, dwg1, dwpl1, dlng1, dlnb1, loss = _lnpl_bwd(x1, s1, p[1], w["w_pl_gate"][1], w["w_pl"][1], ln_g1, ln_b1,
                                                     target, True, "lnpl_bwd_odd")
    dy_o = _matmul(dpre1, woo, tb=True, name="bwd_out_odd_dx")
    g["w_out_odd"] = _matmul(y_o, dpre1, ta=True, name="bwd_out_odd_dw")
    dqe, dkd, dat, deb, di, dz, dhg = _hgrn_scan_bwd(*hgrn_pre, proj_o[2], proj_o[3], hgrn_g, hgrn_hist, dy_o)
    dq, df, dlb = _hgrn_prep_bwd(proj_o[0], proj_o[1], w["lower_bounds"], dqe, dkd, dat, deb)
    dx1 = dpre1
    scale = a
    dws = []
    for j, dj in enumerate((dq, df, di, dz)):
        dx1 = _matmul(dj, wio[j], tb=True, add=dx1, add_scale=scale, name=f"bwd_proj_odd_dx{j}")
        scale = 1.0
        dws.append(_matmul(x1b, dj, ta=True, name=f"bwd_proj_odd_dw{j}"))
    g["w_in_odd"] = jnp.stack(dws)
    g["hgrn_norm_g"] = dhg[0:1]
    g["lower_bounds"] = dlb
    g["w_pl_gate1"], g["w_pl1"] = dwg1, dwpl1

    dpre0, dwg0, dwpl0, dlng0, dlnb0, _ = _lnpl_bwd(x, s0, p[0], w["w_pl_gate"][0], w["w_pl"][0], ln_g0, ln_b0,
                                                  dx1, False, "lnpl_bwd_even")
    g["w_pl_gate0"], g["w_pl0"] = dwg0, dwpl0
    g["ln_g"] = jnp.concatenate([dlng0, dlng1], axis=0)
    g["ln_b"] = jnp.concatenate([dlnb0, dlnb1], axis=0)
    dy_a = _matmul(dpre0, woe[:1024], tb=True, name="bwd_out_even_dxa")
    dy_b = _matmul(dpre0, woe[1024:], tb=True, name="bwd_out_even_dxb")
    g["w_out_even"] = jnp.concatenate([_matmul(y_a, dpre0, ta=True, name="bwd_out_even_dwa"),
                                       _matmul(y_b, dpre0, ta=True, name="bwd_out_even_dwb")], axis=0)
    token = early_grads_ready({n: g[n] for n in _RS_EARLY})
    conv_a_w, gdn_g = conv_a_w + token, gdn_g + token
    du, dw, dqd, dkd, dat, deg, dzb, dgn = _gdn_scan_bwd(*gdn_pre, proj_zb, gdn_g, gdn_hist, dy_b)
    dqa, dka, dva, dbr, dar, dal, ddt = _gdn_prep_bwd(qkv_act, braw, araw, alog, dtb, gdn_inv, du, dw, dqd, dkd, dat, deg)
    g["a_log"] = dal[:, 0, 0].reshape(1, nh)
    g["dt_bias"] = ddt[:, 0, 0].reshape(1, nh)
    g["gdn_norm_g"] = dgn[0:1]
    d_pre_qkv, dwb = [], []
    for j, dj in enumerate((dqa, dka, dva)):
        dpj, dwj = _conv_b_bwd(proj_qkv, conv_b_w, dj, j, f"conv_b_bwd{j}")
        d_pre_qkv.append(dpj)
        dwb.append(dwj[:4])
    g["conv_b_w"] = jnp.concatenate(dwb, axis=1)
    d_a, dwa = _conv_a_bwd(proj_a, conv_a_w, dy_a)
    g["conv_a_w"] = dwa[:3]
    d_tail = _pack_tail(dbr, dar)
    pieces = [(d_a, w_a), (d_pre_qkv[0], w_qkv[:, :1024]), (d_pre_qkv[1], w_qkv[:, 1024:2048]),
              (d_pre_qkv[2], w_qkv[:, 2048:]), (dzb, w_zb), (d_tail, w_tail)]
    dx = dpre0
    scale = a
    dws = []
    for j, (dj, wj) in enumerate(pieces):
        dx = _matmul(dj, wj, tb=True, add=dx, add_scale=scale, name=f"bwd_proj_even_dx{j}")
        scale = 1.0
        dws.append(_matmul(xb, dj, ta=True, name=f"bwd_proj_even_dw{j}"))
    dws[-1] = dws[-1][:, :2 * nh]
    g["w_in_even"] = _col_shards(dws, wie.shape[2])
    return loss, dx, g


def _adamw(w, g, m, v, name):
    lead, rows, cols = w.shape
    tr = rows if rows <= 256 else 256
    assert rows % tr == 0, (name, rows)

    def body(w_ref, g_ref, m_ref, v_ref, d_ref, nm_ref, nv_ref):
        gg = g_ref[...]
        nm = ADAM_B1 * m_ref[...] + (1.0 - ADAM_B1) * gg
        nv = ADAM_B2 * v_ref[...] + (1.0 - ADAM_B2) * jnp.square(gg)
        m_hat = nm / (1.0 - ADAM_B1 ** ADAM_STEP)
        v_hat = nv / (1.0 - ADAM_B2 ** ADAM_STEP)
        d_ref[...] = -ADAM_LR * (m_hat / (jnp.sqrt(v_hat) + ADAM_EPS) + ADAM_WD * w_ref[...])
        nm_ref[...] = nm
        nv_ref[...] = nv

    spec = pl.BlockSpec((1, tr, cols), lambda l, i: (l, i, 0))
    return pl.pallas_call(
        body, name=name, grid=(lead, rows // tr), in_specs=[spec] * 4, out_specs=[spec] * 3,
        out_shape=[jax.ShapeDtypeStruct(w.shape, F32)] * 3, compiler_params=_params("parallel", "parallel"))(w, g, m, v)


MESH = pl.DeviceIdType.MESH
N_DEV = 8
HBM_SPEC = pl.BlockSpec(memory_space=pltpu.HBM)
VMEM_SPEC = pl.BlockSpec(memory_space=pltpu.VMEM)


def _coords():
    return lax.axis_index("x"), lax.axis_index("y"), lax.axis_index("c")


def _flip(v, bit):
    return 1 - v if bit else v


def _remote(src, dst, send_sem, recv_sem, dev):
    return pltpu.make_async_remote_copy(src_ref=src, dst_ref=dst, send_sem=send_sem, recv_sem=recv_sem,
                                        device_id=dev, device_id_type=MESH)


def _exchange_small(buf, reduce, name):
    rows = buf.shape[0]

    def body(in_ref, out_ref, slots, send_sems, recv_sems):
        x, y, c = _coords()
        me = 4 * x + 2 * y + c
        slots[me] = in_ref[...]
        peer = lambda k: (_flip(x, (k >> 2) & 1), _flip(y, (k >> 1) & 1), _flip(c, k & 1))
        sends = []
        for k in range(1, N_DEV):
            cp = _remote(in_ref, slots.at[me], send_sems.at[k - 1], recv_sems.at[k - 1], peer(k))
            cp.start()
            sends.append(cp)
        for k in range(1, N_DEV):
            px, py, pc = peer(k)
            _remote(in_ref, slots.at[4 * px + 2 * py + pc], send_sems.at[k - 1], recv_sems.at[k - 1], peer(k)).wait_recv()
        for cp in sends:
            cp.wait_send()
        if reduce:
            acc = slots[0]
            for d in range(1, N_DEV):
                acc = acc + slots[d]
            out_ref[...] = acc
        else:
            out_ref[...] = slots[...]

    out_shape = (rows, LANES) if reduce else (N_DEV, rows, LANES)
    return pl.pallas_call(
        body, name=name, in_specs=[VMEM_SPEC], out_specs=VMEM_SPEC, out_shape=jax.ShapeDtypeStruct(out_shape, F32),
        scratch_shapes=[pltpu.VMEM((N_DEV, rows, LANES), F32), pltpu.SemaphoreType.DMA((N_DEV - 1,)),
                        pltpu.SemaphoreType.DMA((N_DEV - 1,))])(buf)


def _half_rows(half, which):
    return pl.ds(pl.multiple_of(which * half, 16), half)


def _other_chip(x, y, k):
    return _flip(x, (k >> 1) & 1), _flip(y, k & 1)


SEM_SPEC = pl.BlockSpec(memory_space=pltpu.SEMAPHORE)
DATAFLOW = pltpu.SideEffectType.DATAFLOW_SIDE_EFFECTING


def _ici_piece(srcs, lands, send_sems, recv_sems, i, k, x, y, c):
    half = srcs[i].shape[0] // 2
    ox, oy = _other_chip(x, y, k)
    return _remote(srcs[i].at[_half_rows(half, c)], lands[i].at[2 * x + y, _half_rows(half, c)],
                   send_sems.at[3 * i + k - 1], recv_sems.at[3 * i + k - 1], (ox, oy, c)), (ox, oy)


def _gather_start(shards, name):
    n = len(shards)

    def body(*refs):
        srcs, lands = refs[:n], refs[n:2 * n]
        send_sems, recv_sems = refs[2 * n], refs[2 * n + 1]
        token = refs[-1]
        x, y, c = _coords()
        for i in range(n):
            for k in (1, 2, 3):
                _ici_piece(srcs, lands, send_sems, recv_sems, i, k, x, y, c)[0].start()
        token[...] = jnp.zeros_like(token)

    hbm = lambda a: pltpu.with_memory_space_constraint(a, pltpu.HBM)
    lands = [lax.empty((4,) + s.shape, s.dtype) for s in shards]
    out = pl.pallas_call(
        body, name=name,
        out_shape=(pltpu.SemaphoreType.DMA((3 * n,)), pltpu.SemaphoreType.DMA((3 * n,)),
                   *[pltpu.HBM(s.shape, s.dtype) for s in shards], *[pltpu.HBM(a.shape, a.dtype) for a in lands],
                   jax.ShapeDtypeStruct((SUBLANES, LANES), F32)),
        in_specs=[HBM_SPEC] * (2 * n), out_specs=(SEM_SPEC, SEM_SPEC, *[HBM_SPEC] * (2 * n), VMEM_SPEC),
        input_output_aliases={i: 2 + i for i in range(2 * n)},
        compiler_params=pltpu.CompilerParams(has_side_effects=DATAFLOW))(*[hbm(s) for s in shards], *[hbm(a) for a in lands])
    return out[0], out[1], out[2:2 + n], out[2 + n:2 + 2 * n], out[-1]


def _gather_wait(send_sems, recv_sems, srcs, lands, after, name):
    n = len(srcs)

    def body(*refs):
        src_refs, land_refs = refs[:n], refs[n:2 * n]
        send_sems, recv_sems = refs[2 * n], refs[2 * n + 1]
        x, y, c = _coords()
        for i in range(n):
            half = src_refs[i].shape[0] // 2
            for k in (1, 2, 3):
                cp, (ox, oy) = _ici_piece(src_refs, land_refs, send_sems, recv_sems, i, k, x, y, c)
                cp.wait_send()
                piece = land_refs[i].at[2 * ox + oy, _half_rows(half, c)]
                _remote(piece, piece, send_sems.at[3 * i + k - 1], recv_sems.at[3 * i + k - 1], (ox, oy, c)).wait_recv()

    out = pl.pallas_call(
        body, name=name,
        out_shape=(*[pltpu.HBM(s.shape, s.dtype) for s in srcs], *[pltpu.HBM(a.shape, a.dtype) for a in lands]),
        in_specs=[HBM_SPEC] * (2 * n) + [SEM_SPEC, SEM_SPEC, pl.BlockSpec(memory_space=pl.ANY)],
        out_specs=tuple([HBM_SPEC] * (2 * n)), input_output_aliases={i: i for i in range(2 * n)},
        compiler_params=pltpu.CompilerParams(has_side_effects=DATAFLOW))(*srcs, *lands, send_sems, recv_sems, after)
    return out[n:]


def _gather_forward(lands, name):
    n = len(lands)

    def body(*refs):
        ins, outs = refs[:n], refs[n:2 * n]
        send_sems, recv_sems = refs[2 * n:]
        x, y, c = _coords()
        sends = []
        for i in range(n):
            half = ins[i].shape[1] // 2
            for k in (1, 2, 3):
                ox, oy = _other_chip(x, y, k)
                cp = _remote(ins[i].at[2 * ox + oy, _half_rows(half, c)], outs[i].at[2 * ox + oy, _half_rows(half, c)],
                             send_sems.at[3 * i + k - 1], recv_sems.at[3 * i + k - 1], (x, y, 1 - c))
                cp.start()
                sends.append(cp)
        for i in range(n):
            half = ins[i].shape[1] // 2
            for k in (1, 2, 3):
                ox, oy = _other_chip(x, y, k)
                piece = outs[i].at[2 * ox + oy, _half_rows(half, 1 - c)]
                _remote(piece, piece, send_sems.at[3 * i + k - 1], recv_sems.at[3 * i + k - 1], (x, y, 1 - c)).wait_recv()
        for cp in sends:
            cp.wait_send()

    return pl.pallas_call(
        body, name=name, in_specs=[HBM_SPEC] * n, out_specs=[HBM_SPEC] * n,
        out_shape=[jax.ShapeDtypeStruct(a.shape, a.dtype) for a in lands],
        input_output_aliases={i: i for i in range(n)},
        scratch_shapes=[pltpu.SemaphoreType.DMA((3 * n,))] * 2)(*lands)


def _rs_sibling_swap(g4s, name):
    n = len(g4s)

    def body(*refs):
        ins, outs = refs[:n], refs[n:2 * n]
        send_sems, recv_sems = refs[2 * n:]
        x, y, c = _coords()
        sends = []
        for i in range(n):
            half = ins[i].shape[1] // 2
            for s in range(4):
                cp = _remote(ins[i].at[s, _half_rows(half, 1 - c)], outs[i].at[s], send_sems.at[4 * i + s],
                             recv_sems.at[4 * i + s], (x, y, 1 - c))
                cp.start()
                sends.append(cp)
        for cp in sends:
            cp.wait_recv()
        for cp in sends:
            cp.wait_send()

    return pl.pallas_call(
        body, name=name, in_specs=[HBM_SPEC] * n, out_specs=[HBM_SPEC] * n,
        out_shape=[jax.ShapeDtypeStruct((4, g.shape[1] // 2, g.shape[2]), g.dtype) for g in g4s],
        scratch_shapes=[pltpu.SemaphoreType.DMA((4 * n,))] * 2)(*g4s)


def _rs_add_sibling(g4, got, c_idx, name):
    _, rows, cols = g4.shape
    half = rows // 2
    tr = min(half, 256)
    nb = half // tr

    def body(c_ref, a_ref, b_ref, o_ref, ob_ref):
        total = a_ref[...] + b_ref[...]
        o_ref[...] = total
        ob_ref[...] = total.astype(BF16)

    blk = (1, tr, cols)
    out = pl.BlockSpec(blk, lambda s, i, c_ref: (s, i, 0))
    grid_spec = pltpu.PrefetchScalarGridSpec(
        num_scalar_prefetch=1, grid=(4, nb),
        in_specs=[pl.BlockSpec(blk, lambda s, i, c_ref: (s, c_ref[0] * nb + i, 0)), out],
        out_specs=[out, out])
    return pl.pallas_call(
        body, name=name, grid_spec=grid_spec,
        out_shape=[jax.ShapeDtypeStruct(got.shape, F32), jax.ShapeDtypeStruct(got.shape, BF16)],
        compiler_params=_params("parallel", "parallel"))(c_idx, g4, got)


def _rs_add_chips(p4, got3, idx, name):
    _, half, cols = p4.shape
    tr = min(half, 256)
    nb = half // tr

    def body(idx_ref, p_ref, a_ref, b_ref, c_ref, o_ref):
        o_ref[...] = ((p_ref[0] + a_ref[0].astype(F32)) + b_ref[0].astype(F32)) + c_ref[0].astype(F32)

    blk = (1, tr, cols)
    grid_spec = pltpu.PrefetchScalarGridSpec(
        num_scalar_prefetch=1, grid=(nb,),
        in_specs=[pl.BlockSpec(blk, lambda i, idx_ref: (idx_ref[0], i, 0))]
        + [pl.BlockSpec(blk, functools.partial(lambda k, i, idx_ref: (k, i, 0), k)) for k in range(3)],
        out_specs=pl.BlockSpec((tr, cols), lambda i, idx_ref: (idx_ref[1] * nb + i, 0)))
    return pl.pallas_call(body, name=name, grid_spec=grid_spec, out_shape=jax.ShapeDtypeStruct((2 * half, cols), F32),
                          compiler_params=_params("parallel"))(idx, p4, got3, got3, got3)


def _rs_share_halves(bufs, name):
    n = len(bufs)

    def body(*refs):
        ins, outs = refs[:n], refs[n:2 * n]
        send_sems, recv_sems = refs[2 * n:]
        x, y, c = _coords()
        sends = []
        for i in range(n):
            half = ins[i].shape[0] // 2
            cp = _remote(ins[i].at[_half_rows(half, c)], outs[i].at[_half_rows(half, c)], send_sems.at[i],
                         recv_sems.at[i], (x, y, 1 - c))
            cp.start()
            sends.append(cp)
        for i in range(n):
            half = ins[i].shape[0] // 2
            _remote(ins[i].at[_half_rows(half, c)], outs[i].at[_half_rows(half, 1 - c)], send_sems.at[i],
                    recv_sems.at[i], (x, y, 1 - c)).wait_recv()
        for cp in sends:
            cp.wait_send()

    return pl.pallas_call(
        body, name=name, in_specs=[HBM_SPEC] * n, out_specs=[HBM_SPEC] * n,
        out_shape=[jax.ShapeDtypeStruct(b.shape, b.dtype) for b in bufs],
        input_output_aliases={i: i for i in range(n)},
        scratch_shapes=[pltpu.SemaphoreType.DMA((n,))] * 2)(*bufs)


def _scatter_piece(srcs, lands, send_sems, recv_sems, i, k, x, y, c):
    ox, oy = _other_chip(x, y, k)
    return _remote(srcs[i].at[2 * ox + oy], lands[i].at[k - 1], send_sems.at[3 * i + k - 1],
                   recv_sems.at[3 * i + k - 1], (ox, oy, c))


def _rs_scatter_start(p4s, name):
    n = len(p4s)

    def body(*refs):
        srcs, lands = refs[:n], refs[n:2 * n]
        send_sems, recv_sems = refs[2 * n], refs[2 * n + 1]
        token = refs[-1]
        x, y, c = _coords()
        for i in range(n):
            for k in (1, 2, 3):
                _scatter_piece(srcs, lands, send_sems, recv_sems, i, k, x, y, c).start()
        token[...] = jnp.zeros_like(token)

    hbm = lambda a: pltpu.with_memory_space_constraint(a, pltpu.HBM)
    lands = [lax.empty((3,) + p.shape[1:], p.dtype) for p in p4s]
    out = pl.pallas_call(
        body, name=name,
        out_shape=(pltpu.SemaphoreType.DMA((3 * n,)), pltpu.SemaphoreType.DMA((3 * n,)),
                   *[pltpu.HBM(p.shape, p.dtype) for p in p4s], *[pltpu.HBM(a.shape, a.dtype) for a in lands],
                   jax.ShapeDtypeStruct((SUBLANES, LANES), F32)),
        in_specs=[HBM_SPEC] * (2 * n), out_specs=(SEM_SPEC, SEM_SPEC, *[HBM_SPEC] * (2 * n), VMEM_SPEC),
        input_output_aliases={i: 2 + i for i in range(2 * n)},
        compiler_params=pltpu.CompilerParams(has_side_effects=DATAFLOW))(*[hbm(p) for p in p4s], *[hbm(a) for a in lands])
    return out[0], out[1], out[2:2 + n], out[2 + n:2 + 2 * n], out[-1]


def _rs_scatter_wait(send_sems, recv_sems, srcs, lands, after, name):
    n = len(srcs)

    def body(*refs):
        src_refs, land_refs = refs[:n], refs[n:2 * n]
        send_sems, recv_sems = refs[2 * n], refs[2 * n + 1]
        x, y, c = _coords()
        for i in range(n):
            for k in (1, 2, 3):
                cp = _scatter_piece(src_refs, land_refs, send_sems, recv_sems, i, k, x, y, c)
                cp.wait_send()
                cp.wait_recv()

    out = pl.pallas_call(
        body, name=name,
        out_shape=(*[pltpu.HBM(s.shape, s.dtype) for s in srcs], *[pltpu.HBM(a.shape, a.dtype) for a in lands]),
        in_specs=[HBM_SPEC] * (2 * n) + [SEM_SPEC, SEM_SPEC, pl.BlockSpec(memory_space=pl.ANY)],
        out_specs=tuple([HBM_SPEC] * (2 * n)), input_output_aliases={i: i for i in range(2 * n)},
        compiler_params=pltpu.CompilerParams(has_side_effects=DATAFLOW))(*srcs, *lands, send_sems, recv_sems, after)
    return out[n:]


def _rs_front(g4s, names, tag):
    c_idx = jnp.stack([lax.axis_index("c")]).astype(jnp.int32)
    got = _rs_sibling_swap(g4s, f"rs_sibling_swap_{tag}")
    return [_rs_add_sibling(g, s, c_idx, f"rs_add_sibling_{nm}") for g, s, nm in zip(g4s, got, names)]


def _rs_back(p4s, got3, names):
    x, y, c = _coords()
    idx = jnp.stack([2 * x + y, c]).astype(jnp.int32)
    return [_rs_add_chips(p, t, idx, f"rs_add_chips_{nm}") for (p, _), t, nm in zip(p4s, got3, names)]


def _cols_split(full):
    r, c4 = full.shape
    return full.reshape(r, 4, c4 // 4).transpose(1, 0, 2)


_BIG = {
    "w_in_even": ((1024, 2052), lambda s: s),
    "w_out_even": ((512, 1024), lambda s: s.reshape(2048, 1024)),
    "w_in_odd": ((1024, 2048), lambda s: s),
    "w_out_odd": ((512, 1024), lambda s: s.reshape(2048, 1024)),
    "w_pl": ((512, 256), lambda s: s.reshape(4, 2, 256, 256).transpose(1, 2, 0, 3).reshape(2, 256, 1024)),
    "w_pl_gate": ((512, 1024), lambda s: s.reshape(4, 2, 256, 1024).transpose(1, 0, 2, 3).reshape(2, 1024, 1024)),
}


_RS_EARLY = {
    "w_in_odd": lambda f: f,
    "w_out_odd": lambda f: f.reshape(4, 512, 1024),
    "w_pl_gate1": lambda f: f.reshape(4, 256, 1024),
    "w_pl1": _cols_split,
    "w_out_even": lambda f: f.reshape(4, 512, 1024),
    "w_pl_gate0": lambda f: f.reshape(4, 256, 1024),
    "w_pl0": _cols_split,
}
_RS_LATE = {"w_in_even": lambda f: f}


def _size(shape):
    n = 1
    for d in shape:
        n *= d
    return n


_SMALL = {"a_log": (1, 8), "dt_bias": (1, 8), "gdn_norm_g": (1, 128), "hgrn_norm_g": (1, 128),
          "lower_bounds": (2, 2048), "ln_g": (2, 1024), "ln_b": (2, 1024), "conv_a_w": (3, 1024), "conv_b_w": (4, 3072)}
_CONV_SHARD = {"conv_a_w": (3, 256), "conv_b_w": (4, 768)}


def _pack_small(parts, shapes, head_rows=0):
    rows = []
    for n, shape in shapes.items():
        v = parts[n].reshape(-1)
        rows.append(jnp.pad(v, (0, -v.shape[0] % LANES)).reshape(-1, LANES))
    buf = jnp.concatenate(rows, axis=0)
    return jnp.pad(buf, ((head_rows, -(buf.shape[0] + head_rows) % SUBLANES), (0, 0)))


def _unpack_small(buf, shapes, head_rows=0):
    out, off = {}, head_rows
    for n, shape in shapes.items():
        nrow = -(-_size(shape) // LANES)
        out[n] = buf[off:off + nrow].reshape(-1)[:_size(shape)].reshape(shape)
        off += nrow
    return out


_WEIGHTS = ["w_in_even", "conv_a_w", "conv_b_w", "a_log", "dt_bias", "gdn_norm_g", "w_out_even", "w_in_odd",
            "lower_bounds", "hgrn_norm_g", "w_out_odd", "ln_g", "ln_b", "w_pl", "w_pl_gate"]


def kernel(x, p, w_in_even, conv_a_w, conv_b_w, a_log, dt_bias, gdn_norm_g, w_out_even, w_in_odd, lower_bounds, hgrn_norm_g, w_out_odd, ln_g, ln_b, w_pl, w_pl_gate, loss_target, m_w_in_even, m_conv_a_w, m_conv_b_w, m_a_log, m_dt_bias, m_gdn_norm_g, m_w_out_even, m_w_in_odd, m_lower_bounds, m_hgrn_norm_g, m_w_out_odd, m_ln_g, m_ln_b, m_w_pl, m_w_pl_gate, v_w_in_even, v_conv_a_w, v_conv_b_w, v_a_log, v_dt_bias, v_gdn_norm_g, v_w_out_even, v_w_in_odd, v_lower_bounds, v_hgrn_norm_g, v_w_out_odd, v_ln_g, v_ln_b, v_w_pl, v_w_pl_gate):
    w = dict(zip(_WEIGHTS, (w_in_even, conv_a_w, conv_b_w, a_log, dt_bias, gdn_norm_g, w_out_even, w_in_odd,
                            lower_bounds, hgrn_norm_g, w_out_odd, ln_g, ln_b, w_pl, w_pl_gate)))
    m = dict(zip(_WEIGHTS, (m_w_in_even, m_conv_a_w, m_conv_b_w, m_a_log, m_dt_bias, m_gdn_norm_g, m_w_out_even,
                            m_w_in_odd, m_lower_bounds, m_hgrn_norm_g, m_w_out_odd, m_ln_g, m_ln_b, m_w_pl, m_w_pl_gate)))
    v = dict(zip(_WEIGHTS, (v_w_in_even, v_conv_a_w, v_conv_b_w, v_a_log, v_dt_bias, v_gdn_norm_g, v_w_out_even,
                            v_w_in_odd, v_lower_bounds, v_hgrn_norm_g, v_w_out_odd, v_ln_g, v_ln_b, v_w_pl, v_w_pl_gate)))
    chip = 2 * lax.axis_index("x") + lax.axis_index("y")

    names = list(_BIG)
    shard_shapes = {n: _BIG[n][0] for n in names}
    early, late = names[:1], names[1:]
    shards = {n: w[n].reshape(shard_shapes[n]).astype(BF16) for n in early}
    whole = lambda n, stacked: _BIG[n][1](lax.dynamic_update_slice(stacked, shards[n][None], (chip, 0, 0)))
    first = _gather_start([shards[n] for n in early], "gather_first_start")
    shards.update({n: (w[n].reshape(shard_shapes[n]) + first[4][0, 0]).astype(BF16) for n in late})
    send_sems, recv_sems, srcs, lands, token = _gather_start([shards[n] for n in late], "gather_rest_start")

    def late_weights(after):
        landed = _gather_forward(_gather_wait(send_sems, recv_sems, srcs, lands, after, "gather_rest_wait"),
                                 "gather_rest_forward")
        return {n: whole(n, ga) for n, ga in zip(late, landed)}

    conv_mine = _pack_small({n: w[n] for n in _CONV_SHARD}, _CONV_SHARD) + token[0, 0]
    conv_all = _exchange_small(conv_mine, False, "gather_conv")
    landed = _gather_forward(_gather_wait(*first[:4], conv_all, "gather_first_wait"), "gather_first_forward")
    full = {n: whole(n, ga) for n, ga in zip(early, landed)}
    conv_by_chip = [_unpack_small(conv_all[2 * s], _CONV_SHARD) for s in range(4)]
    for n in _CONV_SHARD:
        full[n] = jnp.concatenate([conv_by_chip[s][n] for s in range(4)], axis=1)
    for n in _SMALL:
        if n not in _CONV_SHARD:
            full[n] = w[n]

    early_rs = {}

    def early_grads_ready(grads):
        early_rs["p4s"] = _rs_front([_RS_EARLY[n](grads[n]) for n in _RS_EARLY], list(_RS_EARLY), "early")
        early_rs["sems"] = _rs_scatter_start([pb for _, pb in early_rs["p4s"]], "rs_scatter_early_start")
        return early_rs["sems"][4][0, 0]

    loss_part, dx, g = _local_step(x[0], p[:, 0], loss_target[0], full, late_weights, early_grads_ready, token[0, 0])

    late_p4s = _rs_front([_RS_LATE[n](g[n]) for n in _RS_LATE], list(_RS_LATE), "late")
    late_sems = _rs_scatter_start([pb for _, pb in late_p4s], "rs_scatter_late_start")
    got3 = _rs_scatter_wait(*early_rs["sems"][:4], late_sems[4], "rs_scatter_early_wait")
    summed = dict(zip(_RS_EARLY, _rs_share_halves(_rs_back(early_rs["p4s"], got3, list(_RS_EARLY)), "rs_share_early")))
    g_big = {n: summed[n] for n in names if n in summed}
    g_big["w_pl"] = jnp.stack([summed["w_pl0"], summed["w_pl1"]])
    g_big["w_pl_gate"] = jnp.stack([summed["w_pl_gate0"], summed["w_pl_gate1"]])
    small_sum = _exchange_small(jnp.concatenate([loss_part, _pack_small(g, _SMALL)], axis=0), True, "reduce_small")
    loss = small_sum[0, 0]
    g_small = _unpack_small(small_sum, _SMALL, head_rows=SUBLANES)
    for n, (rows, cols) in _CONV_SHARD.items():
        g_small[n] = lax.dynamic_slice_in_dim(g_small[n], chip * cols, cols, axis=1)

    grads, delta, new_m, new_v = {}, {}, {}, {}
    for n in late:
        grads[n] = g_big[n].reshape(w[n].shape)
        delta[n], new_m[n], new_v[n] = _adamw(w[n], grads[n], m[n], v[n], f"adamw_{n}")
    own = {n: (_CONV_SHARD[n] if n in _CONV_SHARD else _SMALL[n]) for n in _SMALL}
    packs = [_pack_small({n: src[n] for n in _SMALL}, own)[None] for src in (w, g_small, m, v)]
    outs = [_unpack_small(t[0], own) for t in _adamw(*packs, "adamw_small")]
    for n in _SMALL:
        grads[n] = g_small[n].reshape(w[n].shape)
        delta[n], new_m[n], new_v[n] = (t[n].reshape(w[n].shape) for t in outs)
    got3 = _rs_scatter_wait(*late_sems[:4], new_v["w_in_odd"], "rs_scatter_late_wait")
    (g_in_even,) = _rs_share_halves(_rs_back(late_p4s, got3, list(_RS_LATE)), "rs_share_late")
    for n in early:
        grads[n] = g_in_even.reshape(w[n].shape)
        delta[n], new_m[n], new_v[n] = _adamw(w[n], grads[n], m[n], v[n], f"adamw_{n}")
    return (loss, dx[None], *[grads[n] for n in _WEIGHTS], *[delta[n] for n in _WEIGHTS],
            *[new_m[n] for n in _WEIGHTS], *[new_v[n] for n in _WEIGHTS])
```

```python
import functools

import jax
import jax.numpy as jnp
from jax import lax
from jax.experimental import pallas as pl
from jax.experimental.pallas import tpu as pltpu

F32 = jnp.float32
BF16 = jnp.bfloat16
HI = lax.Precision.HIGHEST

D_MODEL = 1024
PL_DIM = 256
GDN_HEADS = 8
HEAD_DIM = 128
GDN_CHUNK = 64
HGRN_HEADS = 16
HGRN_CHUNK = 32
HGRN_WIDTH = 2048
DEEPNORM_ALPHA = 4.0 ** 0.25
NORM_EPS = 1e-5
ADAM_LR, ADAM_B1, ADAM_B2, ADAM_EPS, ADAM_WD, ADAM_STEP = 0.001, 0.9, 0.999, 1e-08, 0.01, 10

VMEM_LIMIT = 56 * 1024 * 1024
SUBLANES = 8
LANES = 128


def _params(*sem):
    return pltpu.CompilerParams(dimension_semantics=sem, vmem_limit_bytes=VMEM_LIMIT)


ONE_PASS, THREE_PASS, FULL_F32, EXACT_LHS, EXACT_RHS = 0, 1, 2, 3, 4


def _split3(v):
    hi = v.astype(BF16)
    r1 = v - hi.astype(F32)
    mid = r1.astype(BF16)
    return hi, mid, (r1 - mid.astype(F32)).astype(BF16)


def _mm_raw(a, b, kind, prec):
    nb = a.ndim - 2
    ca = a.ndim - 1 if kind[0] == "n" else a.ndim - 2
    cb = b.ndim - 2 if kind[1] == "n" else b.ndim - 1
    dims = (((ca,), (cb,)), (tuple(range(nb)),) * 2)
    if prec == FULL_F32:
        return lax.dot_general(a, b, dims, precision=HI, preferred_element_type=F32)
    dot = lambda p, q: lax.dot_general(p, q, dims, preferred_element_type=F32)
    ah, bh = a.astype(BF16), b.astype(BF16)
    if prec == ONE_PASS:
        return dot(ah, bh)
    if prec == EXACT_LHS:
        b1, b2, b3 = _split3(b)
        return dot(ah, b1) + (dot(ah, b2) + dot(ah, b3))
    if prec == EXACT_RHS:
        a1, a2, a3 = _split3(a)
        return dot(a1, bh) + (dot(a2, bh) + dot(a3, bh))
    al = (a - ah.astype(F32)).astype(BF16)
    bl = (b - bh.astype(F32)).astype(BF16)
    return dot(ah, bh) + (dot(ah, bl) + dot(al, bh))


@functools.partial(jax.custom_vjp, nondiff_argnums=(2, 3))
def _mm_vjp(a, b, kind, hi):
    return _mm_raw(a, b, kind, hi)


def _mm_vjp_fwd(a, b, kind, hi):
    return _mm_raw(a, b, kind, hi), (a, b)


def _mm_vjp_bwd(kind, hi, res, dc):
    a, b = res
    if hi in (EXACT_LHS, EXACT_RHS):
        assert kind == "nn"
        if hi == EXACT_LHS:
            return jnp.zeros_like(a), _mm_raw(a, dc, "tn", EXACT_LHS)
        return _mm_raw(dc, b, "nt", EXACT_RHS), jnp.zeros_like(b)
    if kind == "nn":
        return _mm_raw(dc, b, "nt", hi), _mm_raw(a, dc, "tn", hi)
    if kind == "nt":
        return _mm_raw(dc, b, "nn", hi), _mm_raw(dc, a, "tn", hi)
    return _mm_raw(b, dc, "nt", hi), _mm_raw(a, dc, "nn", hi)


_mm_vjp.defvjp(_mm_vjp_fwd, _mm_vjp_bwd)


def _lane_total(v):
    return jnp.broadcast_to(jnp.sum(v, axis=-1, keepdims=True), v.shape)


def _matmul(a, b, *, name, ta=False, tb=False, add=None, add_scale=1.0, tm=1024, tn=2048, tk=1024):
    m, k = (a.shape[1], a.shape[0]) if ta else a.shape
    n = b.shape[0] if tb else b.shape[1]
    tm, tn, tk = min(tm, m), min(tn, n), min(tk, k)
    tn = tn if n % tn == 0 else tn // 2
    assert m % tm == 0 and n % tn == 0 and k % tk == 0, (name, m, n, k)
    nk = k // tk
    dims = (((0 if ta else 1,), (1 if tb else 0,)), ((), ()))

    def body(*refs):
        a_ref, b_ref = refs[:2]
        o_ref = refs[-1]
        part = lax.dot_general(a_ref[...].astype(BF16), b_ref[...].astype(BF16), dims, preferred_element_type=F32)
        first = (lambda: part) if add is None else (lambda: part + add_scale * refs[2][...])
        if nk == 1:
            o_ref[...] = first()
        else:
            kk = pl.program_id(2)

            @pl.when(kk == 0)
            def _():
                o_ref[...] = first()

            @pl.when(kk > 0)
            def _():
                o_ref[...] += part

    a_spec = pl.BlockSpec((tk, tm), lambda i, j, kk: (kk, i)) if ta else pl.BlockSpec((tm, tk), lambda i, j, kk: (i, kk))
    b_spec = pl.BlockSpec((tn, tk), lambda i, j, kk: (j, kk)) if tb else pl.BlockSpec((tk, tn), lambda i, j, kk: (kk, j))
    o_spec = pl.BlockSpec((tm, tn), lambda i, j, kk: (i, j))
    in_specs = [a_spec, b_spec] + ([o_spec] if add is not None else [])
    args = (a, b) + ((add,) if add is not None else ())
    return pl.pallas_call(
        body, name=name, grid=(m // tm, n // tn, nk), in_specs=in_specs, out_specs=o_spec,
        out_shape=jax.ShapeDtypeStruct((m, n), F32),
        compiler_params=_params("parallel", "parallel", "arbitrary"))(*args)


HALO = SUBLANES


def _halo_specs(tt, width, col, nt):
    r = tt // HALO
    prev = pl.BlockSpec((HALO, width), lambda i: (jnp.maximum(i * r - 1, 0), col))
    nxt = pl.BlockSpec((HALO, width), lambda i: (jnp.minimum((i + 1) * r, nt * r - 1), col))
    return prev, nxt


def _shift_down(ext, k):
    return ext if k == 0 else pltpu.roll(ext, k, 0)


def _shift_up(ext, k):
    return ext if k == 0 else pltpu.roll(ext, ext.shape[0] - k, 0)


def _causal_conv(ext, w, taps):
    acc = None
    for j in range(taps):
        term = w[j:j + 1, :] * _shift_down(ext, taps - 1 - j)
        acc = term if acc is None else acc + term
    return acc[HALO:, :]


def _conv_a_fwd(proj_a, conv_w):
    t = proj_a.shape[0]
    tt = min(t, 256)
    nt = t // tt
    wdt = 1024

    def body(cur_ref, prev_ref, w_ref, y_ref):
        i = pl.program_id(0)
        cur = cur_ref[...]
        h, c, b, z = (cur[:, k * wdt:(k + 1) * wdt] for k in range(4))
        prev = prev_ref[...]
        u_prev = jnp.where(i > 0, prev[:, wdt:2 * wdt] * prev[:, 0:wdt], 0.0)
        ext = jnp.concatenate([u_prev, c * h], axis=0)
        conv = _causal_conv(ext, w_ref[...], 3)
        y_ref[...] = (b * conv * jax.nn.silu(z)).astype(BF16)

    prev_spec, _ = _halo_specs(tt, 4 * wdt, 0, nt)
    return pl.pallas_call(
        body, name="conv_a_fwd", grid=(nt,),
        in_specs=[pl.BlockSpec((tt, 4 * wdt), lambda i: (i, 0)), prev_spec, pl.BlockSpec((3, wdt), lambda i: (0, 0))],
        out_specs=pl.BlockSpec((tt, wdt), lambda i: (i, 0)),
        out_shape=jax.ShapeDtypeStruct((t, wdt), BF16), compiler_params=_params("parallel"))(proj_a, proj_a, conv_w)


def _conv_a_bwd(proj_a, conv_w, dy):
    t = proj_a.shape[0]
    tt = min(t, 256)
    nt = t // tt
    wdt = 1024

    def body(cur_ref, prev_ref, nxt_ref, w_ref, dy_ref, dyn_ref, d_ref, dw_ref):
        i = pl.program_id(0)
        w = w_ref[...]
        cur, prev, nxt = cur_ref[...], prev_ref[...], nxt_ref[...]
        split = lambda a: tuple(a[:, k * wdt:(k + 1) * wdt] for k in range(4))
        h, c, b, z = split(cur)
        hp, cp, _, _ = split(prev)
        hn, cn, bn, zn = split(nxt)
        u_prev = jnp.where(i > 0, cp * hp, 0.0)
        u_ext = jnp.concatenate([u_prev, c * h, cn * hn], axis=0)
        taps = [_shift_down(u_ext, 2 - j)[HALO:, :] for j in range(3)]
        conv = w[0:1, :] * taps[0] + w[1:2, :] * taps[1] + w[2:3, :] * taps[2]
        b_cn = jnp.concatenate([b, bn], axis=0)
        z_cn = jnp.concatenate([z, zn], axis=0)
        dy_cn = jnp.concatenate([dy_ref[...], jnp.where(i < nt - 1, dyn_ref[...], 0.0)], axis=0)
        sg = jax.nn.sigmoid(z_cn)
        silu = z_cn * sg
        d_conv = dy_cn * b_cn * silu
        db = (dy_cn * conv * silu)[:tt, :]
        dz = (dy_cn * b_cn * conv * (sg * (1.0 + z_cn * (1.0 - sg))))[:tt, :]
        du = None
        for j in range(3):
            term = w[j:j + 1, :] * _shift_up(d_conv, 2 - j)
            du = term if du is None else du + term
        du = du[:tt, :]
        d_ref[...] = jnp.concatenate([du * c, du * h, db, dz], axis=1).astype(BF16)

        @pl.when(i == 0)
        def _():
            dw_ref[...] = jnp.zeros_like(dw_ref)

        d_cur = d_conv[:tt, :]
        rows = [jnp.sum(d_cur * taps[j][:tt, :], axis=0, keepdims=True) for j in range(3)]
        dw_ref[0:3, :] += jnp.concatenate(rows, axis=0)

    prev_spec, nxt_spec = _halo_specs(tt, 4 * wdt, 0, nt)
    _, dyn_spec = _halo_specs(tt, wdt, 0, nt)
    return pl.pallas_call(
        body, name="conv_a_bwd", grid=(nt,),
        in_specs=[pl.BlockSpec((tt, 4 * wdt), lambda i: (i, 0)), prev_spec, nxt_spec,
                  pl.BlockSpec((3, wdt), lambda i: (0, 0)), pl.BlockSpec((tt, wdt), lambda i: (i, 0)), dyn_spec],
        out_specs=[pl.BlockSpec((tt, 4 * wdt), lambda i: (i, 0)), pl.BlockSpec((SUBLANES, wdt), lambda i: (0, 0))],
        out_shape=[jax.ShapeDtypeStruct((t, 4 * wdt), BF16), jax.ShapeDtypeStruct((SUBLANES, wdt), F32)],
        compiler_params=_params("arbitrary"))(proj_a, proj_a, proj_a, conv_w, dy, dy)


def _conv_b_fwd(proj_qkv, conv_w):
    t, width = proj_qkv.shape
    tt = min(t, 256)
    nt = t // tt
    wdt = 1024

    def body(cur_ref, prev_ref, w_ref, y_ref):
        i = pl.program_id(1)
        ext = jnp.concatenate([jnp.where(i > 0, prev_ref[...], 0.0), cur_ref[...]], axis=0)
        y_ref[...] = jax.nn.silu(_causal_conv(ext, w_ref[...], 4))

    r = tt // HALO
    return pl.pallas_call(
        body, name="conv_b_fwd", grid=(width // wdt, nt),
        in_specs=[pl.BlockSpec((tt, wdt), lambda j, i: (i, j)),
                  pl.BlockSpec((HALO, wdt), lambda j, i: (jnp.maximum(i * r - 1, 0), j)),
                  pl.BlockSpec((4, wdt), lambda j, i: (0, j))],
        out_specs=pl.BlockSpec((tt, wdt), lambda j, i: (i, j)),
        out_shape=jax.ShapeDtypeStruct((t, width), F32), compiler_params=_params("parallel", "parallel"))(
            proj_qkv, proj_qkv, conv_w)


def _conv_b_bwd(proj_qkv, conv_w, d_act, col, name):
    t = proj_qkv.shape[0]
    tt = min(t, 256)
    nt = t // tt
    wdt = 1024

    def body(cur_ref, prev_ref, nxt_ref, w_ref, da_ref, dan_ref, d_ref, dw_ref):
        i = pl.program_id(0)
        w = w_ref[...]
        u_ext = jnp.concatenate([jnp.where(i > 0, prev_ref[...], 0.0), cur_ref[...], nxt_ref[...]], axis=0)
        taps = [_shift_down(u_ext, 3 - j)[HALO:, :] for j in range(4)]
        conv = w[0:1, :] * taps[0] + w[1:2, :] * taps[1] + w[2:3, :] * taps[2] + w[3:4, :] * taps[3]
        da_cn = jnp.concatenate([da_ref[...], jnp.where(i < nt - 1, dan_ref[...], 0.0)], axis=0)
        sg = jax.nn.sigmoid(conv)
        d_conv = da_cn * (sg * (1.0 + conv * (1.0 - sg)))
        du = None
        for j in range(4):
            term = w[j:j + 1, :] * _shift_up(d_conv, 3 - j)
            du = term if du is None else du + term
        d_ref[...] = du[:tt, :].astype(BF16)

        @pl.when(i == 0)
        def _():
            dw_ref[...] = jnp.zeros_like(dw_ref)

        d_cur = d_conv[:tt, :]
        rows = [jnp.sum(d_cur * taps[j][:tt, :], axis=0, keepdims=True) for j in range(4)]
        dw_ref[0:4, :] += jnp.concatenate(rows, axis=0)

    prev_spec, nxt_spec = _halo_specs(tt, wdt, col, nt)
    _, dan_spec = _halo_specs(tt, wdt, 0, nt)
    return pl.pallas_call(
        body, name=name, grid=(nt,),
        in_specs=[pl.BlockSpec((tt, wdt), lambda i: (i, col)), prev_spec, nxt_spec,
                  pl.BlockSpec((4, wdt), lambda i: (0, col)), pl.BlockSpec((tt, wdt), lambda i: (i, 0)), dan_spec],
        out_specs=[pl.BlockSpec((tt, wdt), lambda i: (i, 0)), pl.BlockSpec((SUBLANES, wdt), lambda i: (0, 0))],
        out_shape=[jax.ShapeDtypeStruct((t, wdt), BF16), jax.ShapeDtypeStruct((SUBLANES, wdt), F32)],
        compiler_params=_params("arbitrary"))(proj_qkv, proj_qkv, proj_qkv, conv_w, d_act, d_act)


def _rms_gate(o, gn, z):
    on = o * lax.rsqrt(jnp.mean(o * o, axis=-1, keepdims=True) + NORM_EPS) * gn
    return on * jax.nn.silu(z)


GDN_PREP_ROWS = 1024


def _unit_lower_inverse(low):
    c = low.shape[-1]
    eye = lax.broadcasted_iota(jnp.int32, low.shape, low.ndim - 2) == lax.broadcasted_iota(jnp.int32, low.shape, low.ndim - 1)
    x = -low
    inv = eye.astype(F32) + x
    for _ in range(c.bit_length() - 2):
        x = _mm_raw(x, x, "nn", THREE_PASS)
        inv = inv + _mm_raw(inv, x, "nn", THREE_PASS)
    return inv


@jax.custom_vjp
def _known_inverse(low, inv):
    return inv


def _known_inverse_fwd(low, inv):
    return inv, inv


def _known_inverse_bwd(inv, d_inv):
    return -_mm_raw(_mm_raw(inv, d_inv, "tn", THREE_PASS), inv, "nt", THREE_PASS), jnp.zeros_like(inv)


_known_inverse.defvjp(_known_inverse_fwd, _known_inverse_bwd)


def _gdn_prep(mm, qa, ka, va, braw, araw, alog, dtb, inv_kept=None):
    n, c, _ = qa.shape
    q = qa * lax.rsqrt(jnp.sum(qa * qa, axis=-1, keepdims=True) + 1e-6) * (HEAD_DIM ** -0.5)
    k = ka * lax.rsqrt(jnp.sum(ka * ka, axis=-1, keepdims=True) + 1e-6)
    beta = jax.nn.sigmoid(braw)
    g = -jnp.exp(alog) * jax.nn.softplus(araw + dtb)
    ri = lax.broadcasted_iota(jnp.int32, (n, c, c), 1)
    ci = lax.broadcasted_iota(jnp.int32, (n, c, c), 2)
    incl, strict, eye = ri >= ci, ri > ci, ri == ci
    gc = mm(incl.astype(F32), g, "nn", EXACT_LHS)
    gc_i = gc[:, :, :c]
    gc_j = mm(jnp.ones((n, c, c), F32), jnp.where(eye, gc_i, 0.0), "nn", EXACT_LHS)
    decay = jnp.where(incl, jnp.exp(jnp.where(incl, gc_i - gc_j, 0.0)), 0.0)
    kb = k * beta
    low = jnp.where(strict, mm(kb, k, "nt", ONE_PASS) * decay, 0.0)
    inv = _unit_lower_inverse(low) if inv_kept is None else _known_inverse(low, inv_kept)
    egc = jnp.exp(gc)
    u = mm(inv, va * beta, "nn", THREE_PASS)
    w = mm(inv, kb * egc, "nn", THREE_PASS)
    attn = jnp.where(incl, mm(q, k, "nt", ONE_PASS) * decay, 0.0)
    g_last = jnp.sum(g, axis=1, keepdims=True)
    outs = (u, w, q * egc, k * jnp.exp(g_last - gc), attn, jnp.exp(g_last))
    return outs + (inv,) if inv_kept is None else outs


def _gdn_scan(mm, u, w, qd, kd, attn, egl, z, gn, state):
    v_new = u - mm(w, state, "nn", ONE_PASS)
    o = mm(qd, state, "nn", ONE_PASS) + mm(attn, v_new, "nn", ONE_PASS)
    new_state = state * egl + mm(kd, v_new, "tn", ONE_PASS)
    return _rms_gate(o, gn, z), new_state


def _chunks(ref_value, n, c):
    return ref_value.reshape(n, c, ref_value.shape[-1])


def _by_head(ref, rows, heads):
    return jnp.stack([ref[rows, pl.ds(h * HEAD_DIM, HEAD_DIM)] for h in range(heads)])


def _store_heads(ref, rows, value):
    for h in range(value.shape[0]):
        ref[rows, pl.ds(h * HEAD_DIM, HEAD_DIM)] = value[h]


def _gdn_prep_specs(tb, nt_unused=None):
    col = lambda off: pl.BlockSpec((tb, HEAD_DIM), lambda h, i: (i, off + h))
    rep = pl.BlockSpec((1, tb, LANES), lambda h, i: (h, i, 0))
    par = pl.BlockSpec((1, SUBLANES, LANES), lambda h, i: (h, 0, 0))
    att = pl.BlockSpec((1, tb, GDN_CHUNK), lambda h, i: (h, i, 0))
    egl = pl.BlockSpec((1, tb // GDN_CHUNK, SUBLANES, LANES), lambda h, i: (h, i, 0, 0))
    return col, rep, par, att, egl


def _gdn_prep_fwd(qkv_act, braw, araw, alog, dtb):
    t = qkv_act.shape[0]
    tb = min(t, GDN_PREP_ROWS)
    nt, nc = t // tb, tb // GDN_CHUNK
    width = GDN_HEADS * HEAD_DIM

    def body(q_ref, k_ref, v_ref, br_ref, ar_ref, al_ref, dt_ref, u_ref, w_ref, qd_ref, kd_ref, at_ref, eg_ref, inv_ref):
        ch = lambda r: _chunks(r, nc, GDN_CHUNK)
        u, w, qd, kd, attn, egl, inv = _gdn_prep(_mm_raw, ch(q_ref[...]), ch(k_ref[...]), ch(v_ref[...]), ch(br_ref[0]),
                                                 ch(ar_ref[0]), al_ref[0, 0:1, :], dt_ref[0, 0:1, :])
        u_ref[...] = u.reshape(tb, HEAD_DIM)
        w_ref[...] = w.reshape(tb, HEAD_DIM).astype(BF16)
        qd_ref[...] = qd.reshape(tb, HEAD_DIM).astype(BF16)
        kd_ref[...] = kd.reshape(tb, HEAD_DIM).astype(BF16)
        at_ref[0] = attn.reshape(tb, GDN_CHUNK).astype(BF16)
        eg_ref[0] = jnp.broadcast_to(egl, (nc, SUBLANES, LANES))
        inv_ref[0] = inv.reshape(tb, GDN_CHUNK)

    col, rep, par, att, egl = _gdn_prep_specs(tb)
    h = GDN_HEADS
    return pl.pallas_call(
        body, name="gdn_prep_fwd", grid=(h, nt),
        in_specs=[col(0), col(h), col(2 * h), rep, rep, par, par],
        out_specs=[col(0), col(0), col(0), col(0), att, egl, att],
        out_shape=[jax.ShapeDtypeStruct((t, width), F32)] + [jax.ShapeDtypeStruct((t, width), BF16)] * 3
        + [jax.ShapeDtypeStruct((h, t, GDN_CHUNK), BF16), jax.ShapeDtypeStruct((h, t // GDN_CHUNK, SUBLANES, LANES), F32),
           jax.ShapeDtypeStruct((h, t, GDN_CHUNK), F32)],
        compiler_params=_params("parallel", "parallel"))(qkv_act, qkv_act, qkv_act, braw, araw, alog, dtb)


def _gdn_prep_bwd(qkv_act, braw, araw, alog, dtb, inv, du, dw, dqd, dkd, dattn, degl):
    t = qkv_act.shape[0]
    tb = min(t, GDN_PREP_ROWS)
    nt, nc = t // tb, tb // GDN_CHUNK
    width = GDN_HEADS * HEAD_DIM

    def body(q_ref, k_ref, v_ref, br_ref, ar_ref, al_ref, dt_ref, inv_ref, du_ref, dw_ref, dqd_ref, dkd_ref, dat_ref,
             deg_ref, dq_ref, dk_ref, dv_ref, dbr_ref, dar_ref, dal_ref, ddt_ref):
        @pl.when(pl.program_id(1) == 0)
        def _():
            dal_ref[...] = jnp.zeros_like(dal_ref)
            ddt_ref[...] = jnp.zeros_like(ddt_ref)

        ch = lambda r: _chunks(r, nc, GDN_CHUNK)
        _, vjp = jax.vjp(functools.partial(_gdn_prep, _mm_vjp, inv_kept=ch(inv_ref[0])), ch(q_ref[...]), ch(k_ref[...]),
                         ch(v_ref[...]), ch(br_ref[0]), ch(ar_ref[0]), al_ref[0, 0:1, :], dt_ref[0, 0:1, :])
        dq, dk, dv, dbr, dar, dal, ddt = vjp((ch(du_ref[...]), ch(dw_ref[...]), ch(dqd_ref[...]), ch(dkd_ref[...]),
                                              ch(dat_ref[0]), deg_ref[0][:, 0:1, :]))
        dq_ref[...] = dq.reshape(tb, HEAD_DIM)
        dk_ref[...] = dk.reshape(tb, HEAD_DIM)
        dv_ref[...] = dv.reshape(tb, HEAD_DIM)
        dbr_ref[0] = _lane_total(dbr.reshape(tb, LANES))
        dar_ref[0] = _lane_total(dar.reshape(tb, LANES))
        dal_ref[0, 0:1, :] += _lane_total(dal)
        ddt_ref[0, 0:1, :] += _lane_total(ddt)

    col, rep, par, att, egl = _gdn_prep_specs(tb)
    h = GDN_HEADS
    return pl.pallas_call(
        body, name="gdn_prep_bwd", grid=(h, nt),
        in_specs=[col(0), col(h), col(2 * h), rep, rep, par, par, att, col(0), col(0), col(0), col(0), att, egl],
        out_specs=[col(0), col(0), col(0), rep, rep, par, par],
        out_shape=[jax.ShapeDtypeStruct((t, width), F32)] * 3 + [jax.ShapeDtypeStruct((h, t, LANES), F32)] * 2
        + [jax.ShapeDtypeStruct((h, SUBLANES, LANES), F32)] * 2,
        compiler_params=_params("parallel", "arbitrary"))(qkv_act, qkv_act, qkv_act, braw, araw, alog, dtb, inv,
                                                         du, dw, dqd, dkd, dattn, degl)


def _scan_specs(tb, heads, chunk, rev, nt):
    ti = (lambda i: nt - 1 - i) if rev else (lambda i: i)
    row = pl.BlockSpec((tb, heads * HEAD_DIM), lambda i: (ti(i), 0))
    att = pl.BlockSpec((heads, tb, chunk), lambda i: (0, ti(i), 0))
    egl = pl.BlockSpec((heads, tb // chunk, SUBLANES, LANES), lambda i: (0, ti(i), 0, 0))
    hist = pl.BlockSpec((heads, tb // chunk, HEAD_DIM, HEAD_DIM), lambda i: (0, ti(i), 0, 0))
    gn = pl.BlockSpec((SUBLANES, LANES), lambda i: (0, 0))
    return row, att, egl, hist, gn


def _gdn_scan_fwd(u, w, qd, kd, attn, egl, zb, gn):
    t = u.shape[0]
    tb = min(t, 256)
    nt, nc = t // tb, tb // GDN_CHUNK
    nh = GDN_HEADS

    def body(u_ref, w_ref, qd_ref, kd_ref, at_ref, eg_ref, z_ref, gn_ref, y_ref, hist_ref, s_ref):
        @pl.when(pl.program_id(0) == 0)
        def _():
            s_ref[...] = jnp.zeros_like(s_ref)

        g = gn_ref[0:1, :]
        state = s_ref[...]
        for c in range(nc):
            rows = pl.ds(c * GDN_CHUNK, GDN_CHUNK)
            heads = lambda r: _by_head(r, rows, nh)
            hist_ref[:, c] = state
            y, state = _gdn_scan(_mm_raw, heads(u_ref), heads(w_ref), heads(qd_ref), heads(kd_ref), at_ref[:, rows, :],
                                 eg_ref[:, c, 0:1, :], heads(z_ref), g, state)
            _store_heads(y_ref, rows, y.astype(BF16))
        s_ref[...] = state

    row, att, egs, hist, gns = _scan_specs(tb, nh, GDN_CHUNK, False, nt)
    return pl.pallas_call(
        body, name="gdn_scan_fwd", grid=(nt,), in_specs=[row, row, row, row, att, egs, row, gns], out_specs=[row, hist],
        out_shape=[jax.ShapeDtypeStruct((t, nh * HEAD_DIM), BF16),
                   jax.ShapeDtypeStruct((nh, t // GDN_CHUNK, HEAD_DIM, HEAD_DIM), F32)],
        scratch_shapes=[pltpu.VMEM((nh, HEAD_DIM, HEAD_DIM), F32)],
        compiler_params=_params("arbitrary"))(u, w, qd, kd, attn, egl, zb, gn)


def _gdn_scan_bwd(u, w, qd, kd, attn, egl, zb, gn, hist, dy):
    t = u.shape[0]
    tb = min(t, 256)
    nt, nc = t // tb, tb // GDN_CHUNK
    nh = GDN_HEADS

    def body(u_ref, w_ref, qd_ref, kd_ref, at_ref, eg_ref, z_ref, gn_ref, hist_ref, dy_ref,
             du_ref, dw_ref, dqd_ref, dkd_ref, dat_ref, deg_ref, dz_ref, dgn_ref, ds_ref):
        @pl.when(pl.program_id(0) == 0)
        def _():
            ds_ref[...] = jnp.zeros_like(ds_ref)
            dgn_ref[...] = jnp.zeros_like(dgn_ref)

        g = gn_ref[0:1, :]
        d_state = ds_ref[...]
        for c in reversed(range(nc)):
            rows = pl.ds(c * GDN_CHUNK, GDN_CHUNK)
            heads = lambda r: _by_head(r, rows, nh).astype(F32)
            _, vjp = jax.vjp(functools.partial(_gdn_scan, _mm_vjp), heads(u_ref), heads(w_ref), heads(qd_ref),
                             heads(kd_ref), at_ref[:, rows, :].astype(F32), eg_ref[:, c, 0:1, :], heads(z_ref), g,
                             hist_ref[:, c])
            du, dw, dqd, dkd, dat, deg, dz, dgn, d_state = vjp((heads(dy_ref), d_state))
            _store_heads(du_ref, rows, du)
            _store_heads(dw_ref, rows, dw)
            _store_heads(dqd_ref, rows, dqd)
            _store_heads(dkd_ref, rows, dkd)
            dat_ref[:, rows, :] = dat
            deg_ref[:, c] = jnp.broadcast_to(deg, (nh, SUBLANES, LANES))
            _store_heads(dz_ref, rows, dz.astype(BF16))
            dgn_ref[0:1, :] += dgn
        ds_ref[...] = d_state

    row, att, egs, hists, gns = _scan_specs(tb, nh, GDN_CHUNK, True, nt)
    wide = jax.ShapeDtypeStruct((t, nh * HEAD_DIM), F32)
    return pl.pallas_call(
        body, name="gdn_scan_bwd", grid=(nt,),
        in_specs=[row, row, row, row, att, egs, row, gns, hists, row],
        out_specs=[row, row, row, row, att, egs, row, gns],
        out_shape=[wide] * 4 + [jax.ShapeDtypeStruct((nh, t, GDN_CHUNK), F32),
                                jax.ShapeDtypeStruct((nh, t // GDN_CHUNK, SUBLANES, LANES), F32),
                                jax.ShapeDtypeStruct((t, nh * HEAD_DIM), BF16),
                                jax.ShapeDtypeStruct((SUBLANES, LANES), F32)],
        scratch_shapes=[pltpu.VMEM((nh, HEAD_DIM, HEAD_DIM), F32)],
        compiler_params=_params("arbitrary"))(u, w, qd, kd, attn, egl, zb, gn, hist, dy)


def _hgrn_prep(mm, qr, fr, lbl):
    n, c, _ = qr.shape
    lb = jax.nn.sigmoid(lbl[1:2, :] - lbl[0:1, :])
    f = lb + (1.0 - lb) * jax.nn.sigmoid(fr)
    q = jax.nn.silu(qr)
    k = 1.0 - f
    logf = jnp.log(f)
    ri = lax.broadcasted_iota(jnp.int32, (n, c, c), 1)
    ci = lax.broadcasted_iota(jnp.int32, (n, c, c), 2)
    b = mm((ri >= ci).astype(F32), logf, "nn", EXACT_LHS)
    attn = _hgrn_attn(mm, q, k, b)
    b_last = jnp.sum(logf, axis=1, keepdims=True)
    return q * jnp.exp(b), k * jnp.exp(b_last - b), attn, jnp.exp(b_last)


HGRN_SUB = 8


@functools.partial(jax.custom_vjp, nondiff_argnums=(1,))
def _roll_rows(x, shift):
    return pltpu.roll(x, shift, x.ndim - 2)


def _roll_rows_fwd(x, shift):
    return _roll_rows(x, shift), None


def _roll_rows_bwd(shift, _, d):
    return (pltpu.roll(d, d.shape[-2] - shift, d.ndim - 2),)


_roll_rows.defvjp(_roll_rows_fwd, _roll_rows_bwd)


@jax.custom_vjp
def _exp_clamped(v):
    return jnp.exp(jnp.minimum(v, 0.0))


def _exp_clamped_fwd(v):
    out = jnp.exp(jnp.minimum(v, 0.0))
    return out, out


def _exp_clamped_bwd(out, d):
    return (d * out,)


_exp_clamped.defvjp(_exp_clamped_fwd, _exp_clamped_bwd)


def _hgrn_attn(mm, q, k, b):
    n, c, d = q.shape
    sb = HGRN_SUB
    sub = lambda a: a.reshape(n * c // sb, sb, d)
    qs, ks, bs = sub(q), sub(k), sub(b)
    row = lax.broadcasted_iota(jnp.int32, (n, c, c), 1)
    col = lax.broadcasted_iota(jnp.int32, (n, c, c), 2)
    same_block = (row & -sb) == (col & -sb)
    attn = None
    for delta in range(sb):
        if delta == 0:
            prod = qs * ks
        else:
            prod = qs * _roll_rows(ks, delta) * _exp_clamped(bs - _roll_rows(bs, delta))
        sums = jnp.sum(prod, axis=-1, keepdims=True).reshape(n, c, 1)
        term = jnp.where(same_block & (row - col == delta), sums, 0.0)
        attn = term if attn is None else attn + term
    far = [jnp.zeros((n, sb, c), F32)]
    col8 = lax.broadcasted_iota(jnp.int32, (n, sb, c), 2)
    for i in range(1, c // sb):
        r0 = i * sb
        bi = b[:, r0:r0 + sb, :]
        ref = bi[:, 0:1, :]
        part = mm(q[:, r0:r0 + sb, :] * jnp.exp(bi - ref), k * _exp_clamped(ref - b), "nt", ONE_PASS)
        far.append(jnp.where(col8 < r0, part, 0.0))
    return attn + jnp.concatenate(far, axis=1)


def _hgrn_scan(mm, qe, kd, attn, ebl, iv, z, gn, state):
    o = mm(qe, state, "nt", ONE_PASS) + mm(attn, iv, "nn", ONE_PASS)
    new_state = state * ebl + mm(iv, kd, "tn", ONE_PASS)
    return _rms_gate(o, gn, z), new_state


def _hgrn_prep_specs(tb):
    col = pl.BlockSpec((tb, HEAD_DIM), lambda h, i: (i, h))
    lbs = pl.BlockSpec((2, HEAD_DIM), lambda h, i: (0, h))
    att = pl.BlockSpec((1, tb, HGRN_CHUNK), lambda h, i: (h, i, 0))
    ebl = pl.BlockSpec((1, tb // HGRN_CHUNK, SUBLANES, LANES), lambda h, i: (h, i, 0, 0))
    return col, lbs, att, ebl


def _hgrn_prep_fwd(qr, fr, lower_bounds):
    t = qr.shape[0]
    tb = min(t, 256)
    nt, nc = t // tb, tb // HGRN_CHUNK
    hh = HGRN_HEADS

    def body(q_ref, f_ref, lb_ref, qe_ref, kd_ref, at_ref, eb_ref):
        ch = lambda r: _chunks(r, nc, HGRN_CHUNK)
        qe, kd, attn, ebl = _hgrn_prep(_mm_raw, ch(q_ref[...]), ch(f_ref[...]), lb_ref[...])
        qe_ref[...] = qe.reshape(tb, HEAD_DIM).astype(BF16)
        kd_ref[...] = kd.reshape(tb, HEAD_DIM).astype(BF16)
        at_ref[0] = attn.reshape(tb, HGRN_CHUNK).astype(BF16)
        eb_ref[0] = jnp.broadcast_to(ebl, (nc, SUBLANES, LANES))

    col, lbs, att, ebs = _hgrn_prep_specs(tb)
    return pl.pallas_call(
        body, name="hgrn_prep_fwd", grid=(hh, nt), in_specs=[col, col, lbs], out_specs=[col, col, att, ebs],
        out_shape=[jax.ShapeDtypeStruct((t, HGRN_WIDTH), BF16)] * 2
        + [jax.ShapeDtypeStruct((hh, t, HGRN_CHUNK), BF16), jax.ShapeDtypeStruct((hh, t // HGRN_CHUNK, SUBLANES, LANES), F32)],
        compiler_params=_params("parallel", "parallel"))(qr, fr, lower_bounds)


def _hgrn_prep_bwd(qr, fr, lower_bounds, dqe, dkd, dattn, debl):
    t = qr.shape[0]
    tb = min(t, 256)
    nt, nc = t // tb, tb // HGRN_CHUNK
    hh = HGRN_HEADS

    def body(q_ref, f_ref, lb_ref, dqe_ref, dkd_ref, dat_ref, deb_ref, dq_ref, df_ref, dlb_ref):
        @pl.when(pl.program_id(1) == 0)
        def _():
            dlb_ref[...] = jnp.zeros_like(dlb_ref)

        ch = lambda r: _chunks(r, nc, HGRN_CHUNK)
        _, vjp = jax.vjp(functools.partial(_hgrn_prep, _mm_vjp), ch(q_ref[...]), ch(f_ref[...]), lb_ref[...])
        dq, df, dlb = vjp((ch(dqe_ref[...]), ch(dkd_ref[...]), ch(dat_ref[0]), deb_ref[0][:, 0:1, :]))
        dq_ref[...] = dq.reshape(tb, HEAD_DIM).astype(BF16)
        df_ref[...] = df.reshape(tb, HEAD_DIM).astype(BF16)
        dlb_ref[...] += dlb

    col, lbs, att, ebs = _hgrn_prep_specs(tb)
    return pl.pallas_call(
        body, name="hgrn_prep_bwd", grid=(hh, nt), in_specs=[col, col, lbs, col, col, att, ebs],
        out_specs=[col, col, lbs],
        out_shape=[jax.ShapeDtypeStruct((t, HGRN_WIDTH), BF16)] * 2 + [jax.ShapeDtypeStruct((2, HGRN_WIDTH), F32)],
        compiler_params=_params("parallel", "arbitrary"))(qr, fr, lower_bounds, dqe, dkd, dattn, debl)


def _hgrn_scan_fwd(qe, kd, attn, ebl, iv, z, gn):
    t = qe.shape[0]
    tb = min(t, 128)
    nt, nc = t // tb, tb // HGRN_CHUNK
    hh = HGRN_HEADS

    def body(qe_ref, kd_ref, at_ref, eb_ref, i_ref, z_ref, gn_ref, y_ref, hist_ref, s_ref):
        @pl.when(pl.program_id(0) == 0)
        def _():
            s_ref[...] = jnp.zeros_like(s_ref)

        g = gn_ref[0:1, :]
        state = s_ref[...]
        for c in range(nc):
            rows = pl.ds(c * HGRN_CHUNK, HGRN_CHUNK)
            heads = lambda r: _by_head(r, rows, hh)
            hist_ref[:, c] = state
            y, state = _hgrn_scan(_mm_raw, heads(qe_ref), heads(kd_ref), at_ref[:, rows, :], eb_ref[:, c, 0:1, :],
                                  heads(i_ref), heads(z_ref), g, state)
            _store_heads(y_ref, rows, y.astype(BF16))
        s_ref[...] = state

    row, att, ebs, hist, gns = _scan_specs(tb, hh, HGRN_CHUNK, False, nt)
    return pl.pallas_call(
        body, name="hgrn_scan_fwd", grid=(nt,), in_specs=[row, row, att, ebs, row, row, gns], out_specs=[row, hist],
        out_shape=[jax.ShapeDtypeStruct((t, HGRN_WIDTH), BF16),
                   jax.ShapeDtypeStruct((hh, t // HGRN_CHUNK, HEAD_DIM, HEAD_DIM), F32)],
        scratch_shapes=[pltpu.VMEM((hh, HEAD_DIM, HEAD_DIM), F32)],
        compiler_params=_params("arbitrary"))(qe, kd, attn, ebl, iv, z, gn)


def _hgrn_scan_bwd(qe, kd, attn, ebl, iv, z, gn, hist, dy):
    t = qe.shape[0]
    tb = min(t, 128)
    nt, nc = t // tb, tb // HGRN_CHUNK
    hh = HGRN_HEADS

    def body(qe_ref, kd_ref, at_ref, eb_ref, i_ref, z_ref, gn_ref, hist_ref, dy_ref,
             dqe_ref, dkd_ref, dat_ref, deb_ref, di_ref, dz_ref, dgn_ref, ds_ref):
        @pl.when(pl.program_id(0) == 0)
        def _():
            ds_ref[...] = jnp.zeros_like(ds_ref)
            dgn_ref[...] = jnp.zeros_like(dgn_ref)

        g = gn_ref[0:1, :]
        d_state = ds_ref[...]
        for c in reversed(range(nc)):
            rows = pl.ds(c * HGRN_CHUNK, HGRN_CHUNK)
            heads = lambda r: _by_head(r, rows, hh).astype(F32)
            _, vjp = jax.vjp(functools.partial(_hgrn_scan, _mm_vjp), heads(qe_ref), heads(kd_ref),
                             at_ref[:, rows, :].astype(F32), eb_ref[:, c, 0:1, :], heads(i_ref), heads(z_ref), g,
                             hist_ref[:, c])
            dqe, dkd, dat, deb, di, dz, dgn, d_state = vjp((heads(dy_ref), d_state))
            _store_heads(dqe_ref, rows, dqe)
            _store_heads(dkd_ref, rows, dkd)
            dat_ref[:, rows, :] = dat
            deb_ref[:, c] = jnp.broadcast_to(deb, (hh, SUBLANES, LANES))
            _store_heads(di_ref, rows, di.astype(BF16))
            _store_heads(dz_ref, rows, dz.astype(BF16))
            dgn_ref[0:1, :] += dgn
        ds_ref[...] = d_state

    row, att, ebs, hists, gns = _scan_specs(tb, hh, HGRN_CHUNK, True, nt)
    wide = lambda dt: jax.ShapeDtypeStruct((t, HGRN_WIDTH), dt)
    return pl.pallas_call(
        body, name="hgrn_scan_bwd", grid=(nt,),
        in_specs=[row, row, att, ebs, row, row, gns, hists, row],
        out_specs=[row, row, att, ebs, row, row, gns],
        out_shape=[wide(F32), wide(F32), jax.ShapeDtypeStruct((hh, t, HGRN_CHUNK), F32),
                   jax.ShapeDtypeStruct((hh, t // HGRN_CHUNK, SUBLANES, LANES), F32), wide(BF16), wide(BF16),
                   jax.ShapeDtypeStruct((SUBLANES, LANES), F32)],
        scratch_shapes=[pltpu.VMEM((hh, HEAD_DIM, HEAD_DIM), F32)],
        compiler_params=_params("arbitrary"))(qe, kd, attn, ebl, iv, z, gn, hist, dy)


def _layer_norm(pre, g, b):
    mu = jnp.mean(pre, axis=-1, keepdims=True)
    d = pre - mu
    var = jnp.mean(d * d, axis=-1, keepdims=True)
    return d * lax.rsqrt(var + NORM_EPS) * g + b


def _lnpl_fwd(xin, s, p, wg, wpl, ln_g, ln_b):
    t = xin.shape[0]
    tt = min(t, 256)

    def body(x_ref, s_ref, p_ref, wg_ref, wpl_ref, g_ref, b_ref, o_ref, ob_ref):
        xn = _layer_norm(DEEPNORM_ALPHA * x_ref[...] + s_ref[...], g_ref[...], b_ref[...])
        gate = jax.nn.sigmoid(_mm_raw(xn, wg_ref[...], "nn", False))
        out = xn + _mm_raw(p_ref[...], wpl_ref[...], "nn", False) * gate
        o_ref[...] = out
        ob_ref[...] = out.astype(BF16)

    row = lambda w: pl.BlockSpec((tt, w), lambda i: (i, 0))
    full = lambda a: pl.BlockSpec(a.shape, lambda i: (0, 0))
    return pl.pallas_call(
        body, name="lnpl_fwd", grid=(t // tt,),
        in_specs=[row(D_MODEL), row(D_MODEL), row(PL_DIM), full(wg), full(wpl), full(ln_g), full(ln_b)],
        out_specs=[row(D_MODEL), row(D_MODEL)],
        out_shape=[jax.ShapeDtypeStruct((t, D_MODEL), F32), jax.ShapeDtypeStruct((t, D_MODEL), BF16)],
        compiler_params=_params("parallel"))(xin, s, p, wg, wpl, ln_g, ln_b)


def _lnpl_bwd(xin, s, p, wg, wpl, ln_g, ln_b, upstream, last, name):
    t = xin.shape[0]
    tt = min(t, 256)

    def body(x_ref, s_ref, p_ref, wg_ref, wpl_ref, g_ref, b_ref, up_ref,
             dpre_ref, dwg_ref, dwpl_ref, dg_ref, db_ref, loss_ref):
        @pl.when(pl.program_id(0) == 0)
        def _():
            for r in (dwg_ref, dwpl_ref, dg_ref, db_ref, loss_ref):
                r[...] = jnp.zeros_like(r)

        pre = DEEPNORM_ALPHA * x_ref[...] + s_ref[...]
        xn, ln_vjp = jax.vjp(_layer_norm, pre, g_ref[...], b_ref[...])
        gate = jax.nn.sigmoid(_mm_raw(xn, wg_ref[...], "nn", False))
        plv = _mm_raw(p_ref[...], wpl_ref[...], "nn", False)
        if last:
            err = xn + plv * gate - up_ref[...]
            dout = err * (1.0 / D_MODEL)
            tot = jnp.sum(jnp.sum(err * err, axis=1, keepdims=True), axis=0, keepdims=True) * (0.5 / D_MODEL)
            loss_ref[...] += jnp.broadcast_to(tot, loss_ref.shape)
        else:
            dout = up_ref[...]
        dplv = dout * gate
        dlogits = dout * plv * gate * (1.0 - gate)
        dwg_ref[...] += _mm_raw(xn, dlogits, "tn", False)
        dwpl_ref[...] += _mm_raw(p_ref[...], dplv, "tn", False)
        dxn = dout + _mm_raw(dlogits, wg_ref[...], "nt", False)
        dpre, dg, db = ln_vjp(dxn)
        dpre_ref[...] = dpre
        dg_ref[...] += dg
        db_ref[...] += db

    row = lambda w: pl.BlockSpec((tt, w), lambda i: (i, 0))
    full = lambda shape: pl.BlockSpec(shape, lambda i: (0, 0))
    return pl.pallas_call(
        body, name=name, grid=(t // tt,),
        in_specs=[row(D_MODEL), row(D_MODEL), row(PL_DIM), full(wg.shape), full(wpl.shape), full(ln_g.shape),
                  full(ln_b.shape), row(D_MODEL)],
        out_specs=[row(D_MODEL), full(wg.shape), full(wpl.shape), full(ln_g.shape), full(ln_b.shape),
                   full((SUBLANES, LANES))],
        out_shape=[jax.ShapeDtypeStruct((t, D_MODEL), F32), jax.ShapeDtypeStruct(wg.shape, F32),
                   jax.ShapeDtypeStruct(wpl.shape, F32), jax.ShapeDtypeStruct(ln_g.shape, F32),
                   jax.ShapeDtypeStruct(ln_b.shape, F32), jax.ShapeDtypeStruct((SUBLANES, LANES), F32)],
        compiler_params=_params("arbitrary"))(xin, s, p, wg, wpl, ln_g, ln_b, upstream)


def _pack_tail(dbr, dar):
    nh, t, _ = dbr.shape
    tt = min(t, 512)

    def body(b_ref, a_ref, o_ref):
        lane = lax.broadcasted_iota(jnp.int32, (tt, LANES), 1)
        acc = jnp.zeros((tt, LANES), F32)
        for h in range(nh):
            acc = jnp.where(lane == h, b_ref[h], acc)
            acc = jnp.where(lane == nh + h, a_ref[h], acc)
        o_ref[...] = acc.astype(BF16)

    spec = pl.BlockSpec((nh, tt, LANES), lambda i: (0, i, 0))
    return pl.pallas_call(
        body, name="pack_tail", grid=(t // tt,), in_specs=[spec, spec], out_specs=pl.BlockSpec((tt, LANES), lambda i: (i, 0)),
        out_shape=jax.ShapeDtypeStruct((t, LANES), BF16), compiler_params=_params("parallel"))(dbr, dar)


def _rep_rows(v):
    return jnp.broadcast_to(v.reshape(1, LANES), (SUBLANES, LANES))


def _rep_heads(v):
    return jnp.broadcast_to(v.reshape(-1, 1, 1), (v.shape[0], SUBLANES, LANES))


def _col_range(stacked, lo, hi):
    c = stacked.shape[2]
    parts = [stacked[s, :, max(lo, s * c) - s * c:min(hi, (s + 1) * c) - s * c]
             for s in range(4) if max(lo, s * c) < min(hi, (s + 1) * c)]
    return parts[0] if len(parts) == 1 else jnp.concatenate(parts, axis=1)


def _col_shards(pieces, c):
    shards, offs, o = [], [], 0
    for pc in pieces:
        offs.append(o)
        o += pc.shape[1]
    for s in range(4):
        lo, hi = s * c, (s + 1) * c
        parts = [pc[:, max(lo, o) - o:min(hi, o + pc.shape[1]) - o] for pc, o in zip(pieces, offs)
                 if max(lo, o) < min(hi, o + pc.shape[1])]
        shards.append(parts[0] if len(parts) == 1 else jnp.concatenate(parts, axis=1))
    return jnp.stack(shards)


def _local_step(x, p, target, w, late_weights, early_grads_ready, start_token):
    a = DEEPNORM_ALPHA
    nh = GDN_HEADS
    xb = (x + start_token).astype(BF16)
    wie = w["w_in_even"]
    w_a, w_qkv, w_zb = _col_range(wie, 0, 4096), _col_range(wie, 4096, 7168), _col_range(wie, 7168, 8192)
    w_tail = jnp.pad(_col_range(wie, 8192, 8192 + 2 * nh), ((0, 0), (0, LANES - 2 * nh)))
    conv_a_w, conv_b_w = w["conv_a_w"], w["conv_b_w"]
    ln_g0, ln_b0, ln_g1, ln_b1 = (v.reshape(1, D_MODEL) for v in (w["ln_g"][0], w["ln_b"][0], w["ln_g"][1], w["ln_b"][1]))
    alog, dtb = _rep_heads(w["a_log"].reshape(nh)), _rep_heads(w["dt_bias"].reshape(nh))
    gdn_g, hgrn_g = _rep_rows(w["gdn_norm_g"]), _rep_rows(w["hgrn_norm_g"])

    proj_a = _matmul(xb, w_a, name="fwd_proj_a")
    proj_qkv = _matmul(xb, w_qkv, name="fwd_proj_qkv")
    proj_zb = _matmul(xb, w_zb, name="fwd_proj_zb")
    proj_tail = _matmul(xb, w_tail, name="fwd_proj_tail")
    rep = lambda cols: jnp.broadcast_to(cols.T[:, :, None], (nh, cols.shape[0], LANES))
    braw, araw = rep(proj_tail[:, :nh]), rep(proj_tail[:, nh:2 * nh])
    y_a = _conv_a_fwd(proj_a, conv_a_w)
    qkv_act = _conv_b_fwd(proj_qkv, conv_b_w)
    *gdn_pre, gdn_inv = _gdn_prep_fwd(qkv_act, braw, araw, alog, dtb)
    y_b, gdn_hist = _gdn_scan_fwd(*gdn_pre, proj_zb, gdn_g)
    w = {**w, **late_weights(y_b)}
    woe, wio, woo = w["w_out_even"], w["w_in_odd"], w["w_out_odd"]
    s0 = _matmul(y_b, woe[1024:], name="fwd_out_even_b", add=_matmul(y_a, woe[:1024], name="fwd_out_even_a"))
    x1, x1b = _lnpl_fwd(x, s0, p[0], w["w_pl_gate"][0], w["w_pl"][0], ln_g0, ln_b0)
    proj_o = [_matmul(x1b, wio[j], name=f"fwd_proj_odd{j}") for j in range(4)]
    hgrn_pre = _hgrn_prep_fwd(proj_o[0], proj_o[1], w["lower_bounds"])
    y_o, hgrn_hist = _hgrn_scan_fwd(*hgrn_pre, proj_o[2], proj_o[3], hgrn_g)
    s1 = _matmul(y_o, woo, name="fwd_out_odd")

    g = {}
    dpre1, dwg1, dwpl1, dlng1, dlnb1, loss = _lnpl_bwd(x1, s1, p[1], w["w_pl_gate"][1], w["w_pl"][1], ln_g1, ln_b1,
                                                     target, True, "lnpl_bwd_odd")
    dy_o = _matmul(dpre1, woo, tb=True, name="bwd_out_odd_dx")
    g["w_out_odd"] = _matmul(y_o, dpre1, ta=True, name="bwd_out_odd_dw")
    dqe, dkd, dat, deb, di, dz, dhg = _hgrn_scan_bwd(*hgrn_pre, proj_o[2], proj_o[3], hgrn_g, hgrn_hist, dy_o)
    dq, df, dlb = _hgrn_prep_bwd(proj_o[0], proj_o[1], w["lower_bounds"], dqe, dkd, dat, deb)
    dx1 = dpre1
    scale = a
    dws = []
    for j, dj in enumerate((dq, df, di, dz)):
        dx1 = _matmul(dj, wio[j], tb=True, add=dx1, add_scale=scale, name=f"bwd_proj_odd_dx{j}")
        scale = 1.0
        dws.append(_matmul(x1b, dj, ta=True, name=f"bwd_proj_odd_dw{j}"))
    g["w_in_odd"] = jnp.stack(dws)
    g["hgrn_norm_g"] = dhg[0:1]
    g["lower_bounds"] = dlb
    g["w_pl_gate1"], g["w_pl1"] = dwg1, dwpl1

    dpre0, dwg0, dwpl0, dlng0, dlnb0, _ = _lnpl_bwd(x, s0, p[0], w["w_pl_gate"][0], w["w_pl"][0], ln_g0, ln_b0,
                                                  dx1, False, "lnpl_bwd_even")
    g["w_pl_gate0"], g["w_pl0"] = dwg0, dwpl0
    g["ln_g"] = jnp.concatenate([dlng0, dlng1], axis=0)
    g["ln_b"] = jnp.concatenate([dlnb0, dlnb1], axis=0)
    dy_a = _matmul(dpre0, woe[:1024], tb=True, name="bwd_out_even_dxa")
    dy_b = _matmul(dpre0, woe[1024:], tb=True, name="bwd_out_even_dxb")
    g["w_out_even"] = jnp.concatenate([_matmul(y_a, dpre0, ta=True, name="bwd_out_even_dwa"),
                                       _matmul(y_b, dpre0, ta=True, name="bwd_out_even_dwb")], axis=0)
    token = early_grads_ready({n: g[n] for n in _RS_EARLY})
    conv_a_w, gdn_g = conv_a_w + token, gdn_g + token
    du, dw, dqd, dkd, dat, deg, dzb, dgn = _gdn_scan_bwd(*gdn_pre, proj_zb, gdn_g, gdn_hist, dy_b)
    dqa, dka, dva, dbr, dar, dal, ddt = _gdn_prep_bwd(qkv_act, braw, araw, alog, dtb, gdn_inv, du, dw, dqd, dkd, dat, deg)
    g["a_log"] = dal[:, 0, 0].reshape(1, nh)
    g["dt_bias"] = ddt[:, 0, 0].reshape(1, nh)
    g["gdn_norm_g"] = dgn[0:1]
    d_pre_qkv, dwb = [], []
    for j, dj in enumerate((dqa, dka, dva)):
        dpj, dwj = _conv_b_bwd(proj_qkv, conv_b_w, dj, j, f"conv_b_bwd{j}")
        d_pre_qkv.append(dpj)
        dwb.append(dwj[:4])
    g["conv_b_w"] = jnp.concatenate(dwb, axis=1)
    d_a, dwa = _conv_a_bwd(proj_a, conv_a_w, dy_a)
    g["conv_a_w"] = dwa[:3]
    d_tail = _pack_tail(dbr, dar)
    pieces = [(d_a, w_a), (d_pre_qkv[0], w_qkv[:, :1024]), (d_pre_qkv[1], w_qkv[:, 1024:2048]),
              (d_pre_qkv[2], w_qkv[:, 2048:]), (dzb, w_zb), (d_tail, w_tail)]
    dx = dpre0
    scale = a
    dws = []
    for j, (dj, wj) in enumerate(pieces):
        dx = _matmul(dj, wj, tb=True, add=dx, add_scale=scale, name=f"bwd_proj_even_dx{j}")
        scale = 1.0
        dws.append(_matmul(xb, dj, ta=True, name=f"bwd_proj_even_dw{j}"))
    dws[-1] = dws[-1][:, :2 * nh]
    g["w_in_even"] = _col_shards(dws, wie.shape[2])
    return loss, dx, g


def _adamw(w, g, m, v, name):
    lead, rows, cols = w.shape
    tr = rows if rows <= 256 else 256
    assert rows % tr == 0, (name, rows)

    def body(w_ref, g_ref, m_ref, v_ref, d_ref, nm_ref, nv_ref):
        gg = g_ref[...]
        nm = ADAM_B1 * m_ref[...] + (1.0 - ADAM_B1) * gg
        nv = ADAM_B2 * v_ref[...] + (1.0 - ADAM_B2) * jnp.square(gg)
        m_hat = nm / (1.0 - ADAM_B1 ** ADAM_STEP)
        v_hat = nv / (1.0 - ADAM_B2 ** ADAM_STEP)
        d_ref[...] = -ADAM_LR * (m_hat / (jnp.sqrt(v_hat) + ADAM_EPS) + ADAM_WD * w_ref[...])
        nm_ref[...] = nm
        nv_ref[...] = nv

    spec = pl.BlockSpec((1, tr, cols), lambda l, i: (l, i, 0))
    return pl.pallas_call(
        body, name=name, grid=(lead, rows // tr), in_specs=[spec] * 4, out_specs=[spec] * 3,
        out_shape=[jax.ShapeDtypeStruct(w.shape, F32)] * 3, compiler_params=_params("parallel", "parallel"))(w, g, m, v)


MESH = pl.DeviceIdType.MESH
N_DEV = 8
HBM_SPEC = pl.BlockSpec(memory_space=pltpu.HBM)
VMEM_SPEC = pl.BlockSpec(memory_space=pltpu.VMEM)


def _coords():
    return lax.axis_index("x"), lax.axis_index("y"), lax.axis_index("c")


def _flip(v, bit):
    return 1 - v if bit else v


def _remote(src, dst, send_sem, recv_sem, dev):
    return pltpu.make_async_remote_copy(src_ref=src, dst_ref=dst, send_sem=send_sem, recv_sem=recv_sem,
                                        device_id=dev, device_id_type=MESH)


def _exchange_small(buf, reduce, name):
    rows = buf.shape[0]

    def body(in_ref, out_ref, slots, send_sems, recv_sems):
        x, y, c = _coords()
        me = 4 * x + 2 * y + c
        slots[me] = in_ref[...]
        peer = lambda k: (_flip(x, (k >> 2) & 1), _flip(y, (k >> 1) & 1), _flip(c, k & 1))
        sends = []
        for k in range(1, N_DEV):
            cp = _remote(in_ref, slots.at[me], send_sems.at[k - 1], recv_sems.at[k - 1], peer(k))
            cp.start()
            sends.append(cp)
        for k in range(1, N_DEV):
            px, py, pc = peer(k)
            _remote(in_ref, slots.at[4 * px + 2 * py + pc], send_sems.at[k - 1], recv_sems.at[k - 1], peer(k)).wait_recv()
        for cp in sends:
            cp.wait_send()
        if reduce:
            acc = slots[0]
            for d in range(1, N_DEV):
                acc = acc + slots[d]
            out_ref[...] = acc
        else:
            out_ref[...] = slots[...]

    out_shape = (rows, LANES) if reduce else (N_DEV, rows, LANES)
    return pl.pallas_call(
        body, name=name, in_specs=[VMEM_SPEC], out_specs=VMEM_SPEC, out_shape=jax.ShapeDtypeStruct(out_shape, F32),
        scratch_shapes=[pltpu.VMEM((N_DEV, rows, LANES), F32), pltpu.SemaphoreType.DMA((N_DEV - 1,)),
                        pltpu.SemaphoreType.DMA((N_DEV - 1,))])(buf)


def _half_rows(half, which):
    return pl.ds(pl.multiple_of(which * half, 16), half)


def _other_chip(x, y, k):
    return _flip(x, (k >> 1) & 1), _flip(y, k & 1)


SEM_SPEC = pl.BlockSpec(memory_space=pltpu.SEMAPHORE)
DATAFLOW = pltpu.SideEffectType.DATAFLOW_SIDE_EFFECTING


def _ici_piece(srcs, lands, send_sems, recv_sems, i, k, x, y, c):
    half = srcs[i].shape[0] // 2
    ox, oy = _other_chip(x, y, k)
    return _remote(srcs[i].at[_half_rows(half, c)], lands[i].at[2 * x + y, _half_rows(half, c)],
                   send_sems.at[3 * i + k - 1], recv_sems.at[3 * i + k - 1], (ox, oy, c)), (ox, oy)


def _gather_start(shards, name):
    n = len(shards)

    def body(*refs):
        srcs, lands = refs[:n], refs[n:2 * n]
        send_sems, recv_sems = refs[2 * n], refs[2 * n + 1]
        token = refs[-1]
        x, y, c = _coords()
        for i in range(n):
            for k in (1, 2, 3):
                _ici_piece(srcs, lands, send_sems, recv_sems, i, k, x, y, c)[0].start()
        token[...] = jnp.zeros_like(token)

    hbm = lambda a: pltpu.with_memory_space_constraint(a, pltpu.HBM)
    lands = [lax.empty((4,) + s.shape, s.dtype) for s in shards]
    out = pl.pallas_call(
        body, name=name,
        out_shape=(pltpu.SemaphoreType.DMA((3 * n,)), pltpu.SemaphoreType.DMA((3 * n,)),
                   *[pltpu.HBM(s.shape, s.dtype) for s in shards], *[pltpu.HBM(a.shape, a.dtype) for a in lands],
                   jax.ShapeDtypeStruct((SUBLANES, LANES), F32)),
        in_specs=[HBM_SPEC] * (2 * n), out_specs=(SEM_SPEC, SEM_SPEC, *[HBM_SPEC] * (2 * n), VMEM_SPEC),
        input_output_aliases={i: 2 + i for i in range(2 * n)},
        compiler_params=pltpu.CompilerParams(has_side_effects=DATAFLOW))(*[hbm(s) for s in shards], *[hbm(a) for a in lands])
    return out[0], out[1], out[2:2 + n], out[2 + n:2 + 2 * n], out[-1]


def _gather_wait(send_sems, recv_sems, srcs, lands, after, name):
    n = len(srcs)

    def body(*refs):
        src_refs, land_refs = refs[:n], refs[n:2 * n]
        send_sems, recv_sems = refs[2 * n], refs[2 * n + 1]
        x, y, c = _coords()
        for i in range(n):
            half = src_refs[i].shape[0] // 2
            for k in (1, 2, 3):
                cp, (ox, oy) = _ici_piece(src_refs, land_refs, send_sems, recv_sems, i, k, x, y, c)
                cp.wait_send()
                piece = land_refs[i].at[2 * ox + oy, _half_rows(half, c)]
                _remote(piece, piece, send_sems.at[3 * i + k - 1], recv_sems.at[3 * i + k - 1], (ox, oy, c)).wait_recv()

    out = pl.pallas_call(
        body, name=name,
        out_shape=(*[pltpu.HBM(s.shape, s.dtype) for s in srcs], *[pltpu.HBM(a.shape, a.dtype) for a in lands]),
        in_specs=[HBM_SPEC] * (2 * n) + [SEM_SPEC, SEM_SPEC, pl.BlockSpec(memory_space=pl.ANY)],
        out_specs=tuple([HBM_SPEC] * (2 * n)), input_output_aliases={i: i for i in range(2 * n)},
        compiler_params=pltpu.CompilerParams(has_side_effects=DATAFLOW))(*srcs, *lands, send_sems, recv_sems, after)
    return out[n:]


def _gather_forward(lands, name):
    n = len(lands)

    def body(*refs):
        ins, outs = refs[:n], refs[n:2 * n]
        send_sems, recv_sems = refs[2 * n:]
        x, y, c = _coords()
        sends = []
        for i in range(n):
            half = ins[i].shape[1] // 2
            for k in (1, 2, 3):
                ox, oy = _other_chip(x, y, k)
                cp = _remote(ins[i].at[2 * ox + oy, _half_rows(half, c)], outs[i].at[2 * ox + oy, _half_rows(half, c)],
                             send_sems.at[3 * i + k - 1], recv_sems.at[3 * i + k - 1], (x, y, 1 - c))
                cp.start()
                sends.append(cp)
        for i in range(n):
            half = ins[i].shape[1] // 2
            for k in (1, 2, 3):
                ox, oy = _other_chip(x, y, k)
                piece = outs[i].at[2 * ox + oy, _half_rows(half, 1 - c)]
                _remote(piece, piece, send_sems.at[3 * i + k - 1], recv_sems.at[3 * i + k - 1], (x, y, 1 - c)).wait_recv()
        for cp in sends:
            cp.wait_send()

    return pl.pallas_call(
        body, name=name, in_specs=[HBM_SPEC] * n, out_specs=[HBM_SPEC] * n,
        out_shape=[jax.ShapeDtypeStruct(a.shape, a.dtype) for a in lands],
        input_output_aliases={i: i for i in range(n)},
        scratch_shapes=[pltpu.SemaphoreType.DMA((3 * n,))] * 2)(*lands)


def _rs_sibling_swap(g4s, name):
    n = len(g4s)

    def body(*refs):
        ins, outs = refs[:n], refs[n:2 * n]
        send_sems, recv_sems = refs[2 * n:]
        x, y, c = _coords()
        sends = []
        for i in range(n):
            half = ins[i].shape[1] // 2
            for s in range(4):
                cp = _remote(ins[i].at[s, _half_rows(half, 1 - c)], outs[i].at[s], send_sems.at[4 * i + s],
                             recv_sems.at[4 * i + s], (x, y, 1 - c))
                cp.start()
                sends.append(cp)
        for cp in sends:
            cp.wait_recv()
        for cp in sends:
            cp.wait_send()

    return pl.pallas_call(
        body, name=name, in_specs=[HBM_SPEC] * n, out_specs=[HBM_SPEC] * n,
        out_shape=[jax.ShapeDtypeStruct((4, g.shape[1] // 2, g.shape[2]), g.dtype) for g in g4s],
        scratch_shapes=[pltpu.SemaphoreType.DMA((4 * n,))] * 2)(*g4s)


def _rs_add_sibling(g4, got, c_idx, name):
    _, rows, cols = g4.shape
    half = rows // 2
    tr = min(half, 256)
    nb = half // tr

    def body(c_ref, a_ref, b_ref, o_ref, ob_ref):
        total = a_ref[...] + b_ref[...]
        o_ref[...] = total
        ob_ref[...] = total.astype(BF16)

    blk = (1, tr, cols)
    out = pl.BlockSpec(blk, lambda s, i, c_ref: (s, i, 0))
    grid_spec = pltpu.PrefetchScalarGridSpec(
        num_scalar_prefetch=1, grid=(4, nb),
        in_specs=[pl.BlockSpec(blk, lambda s, i, c_ref: (s, c_ref[0] * nb + i, 0)), out],
        out_specs=[out, out])
    return pl.pallas_call(
        body, name=name, grid_spec=grid_spec,
        out_shape=[jax.ShapeDtypeStruct(got.shape, F32), jax.ShapeDtypeStruct(got.shape, BF16)],
        compiler_params=_params("parallel", "parallel"))(c_idx, g4, got)


def _rs_add_chips(p4, got3, idx, name):
    _, half, cols = p4.shape
    tr = min(half, 256)
    nb = half // tr

    def body(idx_ref, p_ref, a_ref, b_ref, c_ref, o_ref):
        o_ref[...] = ((p_ref[0] + a_ref[0].astype(F32)) + b_ref[0].astype(F32)) + c_ref[0].astype(F32)

    blk = (1, tr, cols)
    grid_spec = pltpu.PrefetchScalarGridSpec(
        num_scalar_prefetch=1, grid=(nb,),
        in_specs=[pl.BlockSpec(blk, lambda i, idx_ref: (idx_ref[0], i, 0))]
        + [pl.BlockSpec(blk, functools.partial(lambda k, i, idx_ref: (k, i, 0), k)) for k in range(3)],
        out_specs=pl.BlockSpec((tr, cols), lambda i, idx_ref: (idx_ref[1] * nb + i, 0)))
    return pl.pallas_call(body, name=name, grid_spec=grid_spec, out_shape=jax.ShapeDtypeStruct((2 * half, cols), F32),
                          compiler_params=_params("parallel"))(idx, p4, got3, got3, got3)


def _rs_share_halves(bufs, name):
    n = len(bufs)

    def body(*refs):
        ins, outs = refs[:n], refs[n:2 * n]
        send_sems, recv_sems = refs[2 * n:]
        x, y, c = _coords()
        sends = []
        for i in range(n):
            half = ins[i].shape[0] // 2
            cp = _remote(ins[i].at[_half_rows(half, c)], outs[i].at[_half_rows(half, c)], send_sems.at[i],
                         recv_sems.at[i], (x, y, 1 - c))
            cp.start()
            sends.append(cp)
        for i in range(n):
            half = ins[i].shape[0] // 2
            _remote(ins[i].at[_half_rows(half, c)], outs[i].at[_half_rows(half, 1 - c)], send_sems.at[i],
                    recv_sems.at[i], (x, y, 1 - c)).wait_recv()
        for cp in sends:
            cp.wait_send()

    return pl.pallas_call(
        body, name=name, in_specs=[HBM_SPEC] * n, out_specs=[HBM_SPEC] * n,
        out_shape=[jax.ShapeDtypeStruct(b.shape, b.dtype) for b in bufs],
        input_output_aliases={i: i for i in range(n)},
        scratch_shapes=[pltpu.SemaphoreType.DMA((n,))] * 2)(*bufs)


def _scatter_piece(srcs, lands, send_sems, recv_sems, i, k, x, y, c):
    ox, oy = _other_chip(x, y, k)
    return _remote(srcs[i].at[2 * ox + oy], lands[i].at[k - 1], send_sems.at[3 * i + k - 1],
                   recv_sems.at[3 * i + k - 1], (ox, oy, c))


def _rs_scatter_start(p4s, name):
    n = len(p4s)

    def body(*refs):
        srcs, lands = refs[:n], refs[n:2 * n]
        send_sems, recv_sems = refs[2 * n], refs[2 * n + 1]
        token = refs[-1]
        x, y, c = _coords()
        for i in range(n):
            for k in (1, 2, 3):
                _scatter_piece(srcs, lands, send_sems, recv_sems, i, k, x, y, c).start()
        token[...] = jnp.zeros_like(token)

    hbm = lambda a: pltpu.with_memory_space_constraint(a, pltpu.HBM)
    lands = [lax.empty((3,) + p.shape[1:], p.dtype) for p in p4s]
    out = pl.pallas_call(
        body, name=name,
        out_shape=(pltpu.SemaphoreType.DMA((3 * n,)), pltpu.SemaphoreType.DMA((3 * n,)),
                   *[pltpu.HBM(p.shape, p.dtype) for p in p4s], *[pltpu.HBM(a.shape, a.dtype) for a in lands],
                   jax.ShapeDtypeStruct((SUBLANES, LANES), F32)),
        in_specs=[HBM_SPEC] * (2 * n), out_specs=(SEM_SPEC, SEM_SPEC, *[HBM_SPEC] * (2 * n), VMEM_SPEC),
        input_output_aliases={i: 2 + i for i in range(2 * n)},
        compiler_params=pltpu.CompilerParams(has_side_effects=DATAFLOW))(*[hbm(p) for p in p4s], *[hbm(a) for a in lands])
    return out[0], out[1], out[2:2 + n], out[2 + n:2 + 2 * n], out[-1]


def _rs_scatter_wait(send_sems, recv_sems, srcs, lands, after, name):
    n = len(srcs)

    def body(*refs):
        src_refs, land_refs = refs[:n], refs[n:2 * n]
        send_sems, recv_sems = refs[2 * n], refs[2 * n + 1]
        x, y, c = _coords()
        for i in range(n):
            for k in (1, 2, 3):
                cp = _scatter_piece(src_refs, land_refs, send_sems, recv_sems, i, k, x, y, c)
                cp.wait_send()
                cp.wait_recv()

    out = pl.pallas_call(
        body, name=name,
        out_shape=(*[pltpu.HBM(s.shape, s.dtype) for s in srcs], *[pltpu.HBM(a.shape, a.dtype) for a in lands]),
        in_specs=[HBM_SPEC] * (2 * n) + [SEM_SPEC, SEM_SPEC, pl.BlockSpec(memory_space=pl.ANY)],
        out_specs=tuple([HBM_SPEC] * (2 * n)), input_output_aliases={i: i for i in range(2 * n)},
        compiler_params=pltpu.CompilerParams(has_side_effects=DATAFLOW))(*srcs, *lands, send_sems, recv_sems, after)
    return out[n:]


def _rs_front(g4s, names, tag):
    c_idx = jnp.stack([lax.axis_index("c")]).astype(jnp.int32)
    got = _rs_sibling_swap(g4s, f"rs_sibling_swap_{tag}")
    return [_rs_add_sibling(g, s, c_idx, f"rs_add_sibling_{nm}") for g, s, nm in zip(g4s, got, names)]


def _rs_back(p4s, got3, names):
    x, y, c = _coords()
    idx = jnp.stack([2 * x + y, c]).astype(jnp.int32)
    return [_rs_add_chips(p, t, idx, f"rs_add_chips_{nm}") for (p, _), t, nm in zip(p4s, got3, names)]


def _cols_split(full):
    r, c4 = full.shape
    return full.reshape(r, 4, c4 // 4).transpose(1, 0, 2)


_BIG = {
    "w_in_even": ((1024, 2052), lambda s: s),
    "w_out_even": ((512, 1024), lambda s: s.reshape(2048, 1024)),
    "w_in_odd": ((1024, 2048), lambda s: s),
    "w_out_odd": ((512, 1024), lambda s: s.reshape(2048, 1024)),
    "w_pl": ((512, 256), lambda s: s.reshape(4, 2, 256, 256).transpose(1, 2, 0, 3).reshape(2, 256, 1024)),
    "w_pl_gate": ((512, 1024), lambda s: s.reshape(4, 2, 256, 1024).transpose(1, 0, 2, 3).reshape(2, 1024, 1024)),
}


_RS_EARLY = {
    "w_in_odd": lambda f: f,
    "w_out_odd": lambda f: f.reshape(4, 512, 1024),
    "w_pl_gate1": lambda f: f.reshape(4, 256, 1024),
    "w_pl1": _cols_split,
    "w_out_even": lambda f: f.reshape(4, 512, 1024),
    "w_pl_gate0": lambda f: f.reshape(4, 256, 1024),
    "w_pl0": _cols_split,
}
_RS_LATE = {"w_in_even": lambda f: f}


def _size(shape):
    n = 1
    for d in shape:
        n *= d
    return n


_SMALL = {"a_log": (1, 8), "dt_bias": (1, 8), "gdn_norm_g": (1, 128), "hgrn_norm_g": (1, 128),
          "lower_bounds": (2, 2048), "ln_g": (2, 1024), "ln_b": (2, 1024), "conv_a_w": (3, 1024), "conv_b_w": (4, 3072)}
_CONV_SHARD = {"conv_a_w": (3, 256), "conv_b_w": (4, 768)}


def _pack_small(parts, shapes, head_rows=0):
    rows = []
    for n, shape in shapes.items():
        v = parts[n].reshape(-1)
        rows.append(jnp.pad(v, (0, -v.shape[0] % LANES)).reshape(-1, LANES))
    buf = jnp.concatenate(rows, axis=0)
    return jnp.pad(buf, ((head_rows, -(buf.shape[0] + head_rows) % SUBLANES), (0, 0)))


def _unpack_small(buf, shapes, head_rows=0):
    out, off = {}, head_rows
    for n, shape in shapes.items():
        nrow = -(-_size(shape) // LANES)
        out[n] = buf[off:off + nrow].reshape(-1)[:_size(shape)].reshape(shape)
        off += nrow
    return out


_WEIGHTS = ["w_in_even", "conv_a_w", "conv_b_w", "a_log", "dt_bias", "gdn_norm_g", "w_out_even", "w_in_odd",
            "lower_bounds", "hgrn_norm_g", "w_out_odd", "ln_g", "ln_b", "w_pl", "w_pl_gate"]


def kernel(x, p, w_in_even, conv_a_w, conv_b_w, a_log, dt_bias, gdn_norm_g, w_out_even, w_in_odd, lower_bounds, hgrn_norm_g, w_out_odd, ln_g, ln_b, w_pl, w_pl_gate, loss_target, m_w_in_even, m_conv_a_w, m_conv_b_w, m_a_log, m_dt_bias, m_gdn_norm_g, m_w_out_even, m_w_in_odd, m_lower_bounds, m_hgrn_norm_g, m_w_out_odd, m_ln_g, m_ln_b, m_w_pl, m_w_pl_gate, v_w_in_even, v_conv_a_w, v_conv_b_w, v_a_log, v_dt_bias, v_gdn_norm_g, v_w_out_even, v_w_in_odd, v_lower_bounds, v_hgrn_norm_g, v_w_out_odd, v_ln_g, v_ln_b, v_w_pl, v_w_pl_gate):
    w = dict(zip(_WEIGHTS, (w_in_even, conv_a_w, conv_b_w, a_log, dt_bias, gdn_norm_g, w_out_even, w_in_odd,
                            lower_bounds, hgrn_norm_g, w_out_odd, ln_g, ln_b, w_pl, w_pl_gate)))
    m = dict(zip(_WEIGHTS, (m_w_in_even, m_conv_a_w, m_conv_b_w, m_a_log, m_dt_bias, m_gdn_norm_g, m_w_out_even,
                            m_w_in_odd, m_lower_bounds, m_hgrn_norm_g, m_w_out_odd, m_ln_g, m_ln_b, m_w_pl, m_w_pl_gate)))
    v = dict(zip(_WEIGHTS, (v_w_in_even, v_conv_a_w, v_conv_b_w, v_a_log, v_dt_bias, v_gdn_norm_g, v_w_out_even,
                            v_w_in_odd, v_lower_bounds, v_hgrn_norm_g, v_w_out_odd, v_ln_g, v_ln_b, v_w_pl, v_w_pl_gate)))
    chip = 2 * lax.axis_index("x") + lax.axis_index("y")

    names = list(_BIG)
    shard_shapes = {n: _BIG[n][0] for n in names}
    early, late = names[:1], names[1:]
    shards = {n: w[n].reshape(shard_shapes[n]).astype(BF16) for n in early}
    whole = lambda n, stacked: _BIG[n][1](lax.dynamic_update_slice(stacked, shards[n][None], (chip, 0, 0)))
    conv_mine = _pack_small({n: w[n] for n in _CONV_SHARD}, _CONV_SHARD)
    conv_all = _exchange_small(conv_mine, False, "gather_conv")
    shards, conv_all = lax.optimization_barrier((shards, conv_all))
    first = _gather_start([shards[n] for n in early], "gather_first_start")
    shards.update({n: (w[n].reshape(shard_shapes[n]) + first[4][0, 0]).astype(BF16) for n in late})
    send_sems, recv_sems, srcs, lands, token = _gather_start([shards[n] for n in late], "gather_rest_start")

    def late_weights(after):
        landed = _gather_forward(_gather_wait(send_sems, recv_sems, srcs, lands, after, "gather_rest_wait"),
                                 "gather_rest_forward")
        return {n: whole(n, ga) for n, ga in zip(late, landed)}

    landed = _gather_forward(_gather_wait(*first[:4], token, "gather_first_wait"), "gather_first_forward")
    full = {n: whole(n, ga) for n, ga in zip(early, landed)}
    conv_by_chip = [_unpack_small(conv_all[2 * s], _CONV_SHARD) for s in range(4)]
    for n in _CONV_SHARD:
        full[n] = jnp.concatenate([conv_by_chip[s][n] for s in range(4)], axis=1)
    for n in _SMALL:
        if n not in _CONV_SHARD:
            full[n] = w[n]

    early_rs = {}

    def early_grads_ready(grads):
        early_rs["p4s"] = _rs_front([_RS_EARLY[n](grads[n]) for n in _RS_EARLY], list(_RS_EARLY), "early")
        early_rs["sems"] = _rs_scatter_start([pb for _, pb in early_rs["p4s"]], "rs_scatter_early_start")
        return early_rs["sems"][4][0, 0]

    loss_part, dx, g = _local_step(x[0], p[:, 0], loss_target[0], full, late_weights, early_grads_ready, token[0, 0])

    late_p4s = _rs_front([_RS_LATE[n](g[n]) for n in _RS_LATE], list(_RS_LATE), "late")
    late_sems = _rs_scatter_start([pb for _, pb in late_p4s], "rs_scatter_late_start")
    got3 = _rs_scatter_wait(*early_rs["sems"][:4], late_sems[4], "rs_scatter_early_wait")
    summed = dict(zip(_RS_EARLY, _rs_share_halves(_rs_back(early_rs["p4s"], got3, list(_RS_EARLY)), "rs_share_early")))
    g_big = {n: summed[n] for n in names if n in summed}
    g_big["w_pl"] = jnp.stack([summed["w_pl0"], summed["w_pl1"]])
    g_big["w_pl_gate"] = jnp.stack([summed["w_pl_gate0"], summed["w_pl_gate1"]])
    small_sum = _exchange_small(jnp.concatenate([loss_part, _pack_small(g, _SMALL)], axis=0), True, "reduce_small")
    loss = small_sum[0, 0]
    g_small = _unpack_small(small_sum, _SMALL, head_rows=SUBLANES)
    for n, (rows, cols) in _CONV_SHARD.items():
        g_small[n] = lax.dynamic_slice_in_dim(g_small[n], chip * cols, cols, axis=1)

    grads, delta, new_m, new_v = {}, {}, {}, {}
    for n in late:
        grads[n] = g_big[n].reshape(w[n].shape)
        delta[n], new_m[n], new_v[n] = _adamw(w[n], grads[n], m[n], v[n], f"adamw_{n}")
    own = {n: (_CONV_SHARD[n] if n in _CONV_SHARD else _SMALL[n]) for n in _SMALL}
    packs = [_pack_small({n: src[n] for n in _SMALL}, own)[None] for src in (w, g_small, m, v)]
    outs = [_unpack_small(t[0], own) for t in _adamw(*packs, "adamw_small")]
    for n in _SMALL:
        grads[n] = g_small[n].reshape(w[n].shape)
        delta[n], new_m[n], new_v[n] = (t[n].reshape(w[n].shape) for t in outs)
    got3 = _rs_scatter_wait(*late_sems[:4], new_v["w_in_odd"], "rs_scatter_late_wait")
    (g_in_even,) = _rs_share_halves(_rs_back(late_p4s, got3, list(_RS_LATE)), "rs_share_late")
    for n in early:
        grads[n] = g_in_even.reshape(w[n].shape)
        delta[n], new_m[n], new_v[n] = _adamw(w[n], grads[n], m[n], v[n], f"adamw_{n}")
    return (loss, dx[None], *[grads[n] for n in _WEIGHTS], *[delta[n] for n in _WEIGHTS],
            *[new_m[n] for n in _WEIGHTS], *[new_v[n] for n in _WEIGHTS])
```

```python
import functools

import jax
import jax.numpy as jnp
from jax import lax
from jax.experimental import pallas as pl
from jax.experimental.pallas import tpu as pltpu

F32 = jnp.float32
BF16 = jnp.bfloat16
HI = lax.Precision.HIGHEST

D_MODEL = 1024
PL_DIM = 256
GDN_HEADS = 8
HEAD_DIM = 128
GDN_CHUNK = 64
HGRN_HEADS = 16
HGRN_CHUNK = 32
HGRN_WIDTH = 2048
DEEPNORM_ALPHA = 4.0 ** 0.25
NORM_EPS = 1e-5
ADAM_LR, ADAM_B1, ADAM_B2, ADAM_EPS, ADAM_WD, ADAM_STEP = 0.001, 0.9, 0.999, 1e-08, 0.01, 10

VMEM_LIMIT = 56 * 1024 * 1024
SUBLANES = 8
LANES = 128


def _params(*sem):
    return pltpu.CompilerParams(dimension_semantics=sem, vmem_limit_bytes=VMEM_LIMIT)


ONE_PASS, THREE_PASS, FULL_F32, EXACT_LHS, EXACT_RHS = 0, 1, 2, 3, 4


def _split3(v):
    hi = v.astype(BF16)
    r1 = v - hi.astype(F32)
    mid = r1.astype(BF16)
    return hi, mid, (r1 - mid.astype(F32)).astype(BF16)


def _mm_raw(a, b, kind, prec):
    nb = a.ndim - 2
    ca = a.ndim - 1 if kind[0] == "n" else a.ndim - 2
    cb = b.ndim - 2 if kind[1] == "n" else b.ndim - 1
    dims = (((ca,), (cb,)), (tuple(range(nb)),) * 2)
    if prec == FULL_F32:
        return lax.dot_general(a, b, dims, precision=HI, preferred_element_type=F32)
    dot = lambda p, q: lax.dot_general(p, q, dims, preferred_element_type=F32)
    ah, bh = a.astype(BF16), b.astype(BF16)
    if prec == ONE_PASS:
        return dot(ah, bh)
    if prec == EXACT_LHS:
        b1, b2, b3 = _split3(b)
        return dot(ah, b1) + (dot(ah, b2) + dot(ah, b3))
    if prec == EXACT_RHS:
        a1, a2, a3 = _split3(a)
        return dot(a1, bh) + (dot(a2, bh) + dot(a3, bh))
    al = (a - ah.astype(F32)).astype(BF16)
    bl = (b - bh.astype(F32)).astype(BF16)
    return dot(ah, bh) + (dot(ah, bl) + dot(al, bh))


@functools.partial(jax.custom_vjp, nondiff_argnums=(2, 3))
def _mm_vjp(a, b, kind, hi):
    return _mm_raw(a, b, kind, hi)


def _mm_vjp_fwd(a, b, kind, hi):
    return _mm_raw(a, b, kind, hi), (a, b)


def _mm_vjp_bwd(kind, hi, res, dc):
    a, b = res
    if hi in (EXACT_LHS, EXACT_RHS):
        assert kind == "nn"
        if hi == EXACT_LHS:
            return jnp.zeros_like(a), _mm_raw(a, dc, "tn", EXACT_LHS)
        return _mm_raw(dc, b, "nt", EXACT_RHS), jnp.zeros_like(b)
    if kind == "nn":
        return _mm_raw(dc, b, "nt", hi), _mm_raw(a, dc, "tn", hi)
    if kind == "nt":
        return _mm_raw(dc, b, "nn", hi), _mm_raw(dc, a, "tn", hi)
    return _mm_raw(b, dc, "nt", hi), _mm_raw(a, dc, "nn", hi)


_mm_vjp.defvjp(_mm_vjp_fwd, _mm_vjp_bwd)


def _lane_total(v):
    return jnp.broadcast_to(jnp.sum(v, axis=-1, keepdims=True), v.shape)


def _matmul(a, b, *, name, ta=False, tb=False, add=None, add_scale=1.0, tm=1024, tn=2048, tk=1024):
    m, k = (a.shape[1], a.shape[0]) if ta else a.shape
    n = b.shape[0] if tb else b.shape[1]
    tm, tn, tk = min(tm, m), min(tn, n), min(tk, k)
    tn = tn if n % tn == 0 else tn // 2
    assert m % tm == 0 and n % tn == 0 and k % tk == 0, (name, m, n, k)
    nk = k // tk
    dims = (((0 if ta else 1,), (1 if tb else 0,)), ((), ()))

    def body(*refs):
        a_ref, b_ref = refs[:2]
        o_ref = refs[-1]
        part = lax.dot_general(a_ref[...].astype(BF16), b_ref[...].astype(BF16), dims, preferred_element_type=F32)
        first = (lambda: part) if add is None else (lambda: part + add_scale * refs[2][...])
        if nk == 1:
            o_ref[...] = first()
        else:
            kk = pl.program_id(2)

            @pl.when(kk == 0)
            def _():
                o_ref[...] = first()

            @pl.when(kk > 0)
            def _():
                o_ref[...] += part

    a_spec = pl.BlockSpec((tk, tm), lambda i, j, kk: (kk, i)) if ta else pl.BlockSpec((tm, tk), lambda i, j, kk: (i, kk))
    b_spec = pl.BlockSpec((tn, tk), lambda i, j, kk: (j, kk)) if tb else pl.BlockSpec((tk, tn), lambda i, j, kk: (kk, j))
    o_spec = pl.BlockSpec((tm, tn), lambda i, j, kk: (i, j))
    in_specs = [a_spec, b_spec] + ([o_spec] if add is not None else [])
    args = (a, b) + ((add,) if add is not None else ())
    return pl.pallas_call(
        body, name=name, grid=(m // tm, n // tn, nk), in_specs=in_specs, out_specs=o_spec,
        out_shape=jax.ShapeDtypeStruct((m, n), F32),
        compiler_params=_params("parallel", "parallel", "arbitrary"))(*args)


HALO = SUBLANES


def _halo_specs(tt, width, col, nt):
    r = tt // HALO
    prev = pl.BlockSpec((HALO, width), lambda i: (jnp.maximum(i * r - 1, 0), col))
    nxt = pl.BlockSpec((HALO, width), lambda i: (jnp.minimum((i + 1) * r, nt * r - 1), col))
    return prev, nxt


def _shift_down(ext, k):
    return ext if k == 0 else pltpu.roll(ext, k, 0)


def _shift_up(ext, k):
    return ext if k == 0 else pltpu.roll(ext, ext.shape[0] - k, 0)


def _causal_conv(ext, w, taps):
    acc = None
    for j in range(taps):
        term = w[j:j + 1, :] * _shift_down(ext, taps - 1 - j)
        acc = term if acc is None else acc + term
    return acc[HALO:, :]


def _conv_a_fwd(proj_a, conv_w):
    t = proj_a.shape[0]
    tt = min(t, 256)
    nt = t // tt
    wdt = 1024

    def body(cur_ref, prev_ref, w_ref, y_ref):
        i = pl.program_id(0)
        cur = cur_ref[...]
        h, c, b, z = (cur[:, k * wdt:(k + 1) * wdt] for k in range(4))
        prev = prev_ref[...]
        u_prev = jnp.where(i > 0, prev[:, wdt:2 * wdt] * prev[:, 0:wdt], 0.0)
        ext = jnp.concatenate([u_prev, c * h], axis=0)
        conv = _causal_conv(ext, w_ref[...], 3)
        y_ref[...] = (b * conv * jax.nn.silu(z)).astype(BF16)

    prev_spec, _ = _halo_specs(tt, 4 * wdt, 0, nt)
    return pl.pallas_call(
        body, name="conv_a_fwd", grid=(nt,),
        in_specs=[pl.BlockSpec((tt, 4 * wdt), lambda i: (i, 0)), prev_spec, pl.BlockSpec((3, wdt), lambda i: (0, 0))],
        out_specs=pl.BlockSpec((tt, wdt), lambda i: (i, 0)),
        out_shape=jax.ShapeDtypeStruct((t, wdt), BF16), compiler_params=_params("parallel"))(proj_a, proj_a, conv_w)


def _conv_a_bwd(proj_a, conv_w, dy):
    t = proj_a.shape[0]
    tt = min(t, 256)
    nt = t // tt
    wdt = 1024

    def body(cur_ref, prev_ref, nxt_ref, w_ref, dy_ref, dyn_ref, d_ref, dw_ref):
        i = pl.program_id(0)
        w = w_ref[...]
        cur, prev, nxt = cur_ref[...], prev_ref[...], nxt_ref[...]
        split = lambda a: tuple(a[:, k * wdt:(k + 1) * wdt] for k in range(4))
        h, c, b, z = split(cur)
        hp, cp, _, _ = split(prev)
        hn, cn, bn, zn = split(nxt)
        u_prev = jnp.where(i > 0, cp * hp, 0.0)
        u_ext = jnp.concatenate([u_prev, c * h, cn * hn], axis=0)
        taps = [_shift_down(u_ext, 2 - j)[HALO:, :] for j in range(3)]
        conv = w[0:1, :] * taps[0] + w[1:2, :] * taps[1] + w[2:3, :] * taps[2]
        b_cn = jnp.concatenate([b, bn], axis=0)
        z_cn = jnp.concatenate([z, zn], axis=0)
        dy_cn = jnp.concatenate([dy_ref[...], jnp.where(i < nt - 1, dyn_ref[...], 0.0)], axis=0)
        sg = jax.nn.sigmoid(z_cn)
        silu = z_cn * sg
        d_conv = dy_cn * b_cn * silu
        db = (dy_cn * conv * silu)[:tt, :]
        dz = (dy_cn * b_cn * conv * (sg * (1.0 + z_cn * (1.0 - sg))))[:tt, :]
        du = None
        for j in range(3):
            term = w[j:j + 1, :] * _shift_up(d_conv, 2 - j)
            du = term if du is None else du + term
        du = du[:tt, :]
        d_ref[...] = jnp.concatenate([du * c, du * h, db, dz], axis=1).astype(BF16)

        @pl.when(i == 0)
        def _():
            dw_ref[...] = jnp.zeros_like(dw_ref)

        d_cur = d_conv[:tt, :]
        rows = [jnp.sum(d_cur * taps[j][:tt, :], axis=0, keepdims=True) for j in range(3)]
        dw_ref[0:3, :] += jnp.concatenate(rows, axis=0)

    prev_spec, nxt_spec = _halo_specs(tt, 4 * wdt, 0, nt)
    _, dyn_spec = _halo_specs(tt, wdt, 0, nt)
    return pl.pallas_call(
        body, name="conv_a_bwd", grid=(nt,),
        in_specs=[pl.BlockSpec((tt, 4 * wdt), lambda i: (i, 0)), prev_spec, nxt_spec,
                  pl.BlockSpec((3, wdt), lambda i: (0, 0)), pl.BlockSpec((tt, wdt), lambda i: (i, 0)), dyn_spec],
        out_specs=[pl.BlockSpec((tt, 4 * wdt), lambda i: (i, 0)), pl.BlockSpec((SUBLANES, wdt), lambda i: (0, 0))],
        out_shape=[jax.ShapeDtypeStruct((t, 4 * wdt), BF16), jax.ShapeDtypeStruct((SUBLANES, wdt), F32)],
        compiler_params=_params("arbitrary"))(proj_a, proj_a, proj_a, conv_w, dy, dy)


def _conv_b_fwd(proj_qkv, conv_w):
    t, width = proj_qkv.shape
    tt = min(t, 256)
    nt = t // tt
    wdt = 1024

    def body(cur_ref, prev_ref, w_ref, y_ref):
        i = pl.program_id(1)
        ext = jnp.concatenate([jnp.where(i > 0, prev_ref[...], 0.0), cur_ref[...]], axis=0)
        y_ref[...] = jax.nn.silu(_causal_conv(ext, w_ref[...], 4))

    r = tt // HALO
    return pl.pallas_call(
        body, name="conv_b_fwd", grid=(width // wdt, nt),
        in_specs=[pl.BlockSpec((tt, wdt), lambda j, i: (i, j)),
                  pl.BlockSpec((HALO, wdt), lambda j, i: (jnp.maximum(i * r - 1, 0), j)),
                  pl.BlockSpec((4, wdt), lambda j, i: (0, j))],
        out_specs=pl.BlockSpec((tt, wdt), lambda j, i: (i, j)),
        out_shape=jax.ShapeDtypeStruct((t, width), F32), compiler_params=_params("parallel", "parallel"))(
            proj_qkv, proj_qkv, conv_w)


def _conv_b_bwd(proj_qkv, conv_w, d_act, col, name):
    t = proj_qkv.shape[0]
    tt = min(t, 256)
    nt = t // tt
    wdt = 1024

    def body(cur_ref, prev_ref, nxt_ref, w_ref, da_ref, dan_ref, d_ref, dw_ref):
        i = pl.program_id(0)
        w = w_ref[...]
        u_ext = jnp.concatenate([jnp.where(i > 0, prev_ref[...], 0.0), cur_ref[...], nxt_ref[...]], axis=0)
        taps = [_shift_down(u_ext, 3 - j)[HALO:, :] for j in range(4)]
        conv = w[0:1, :] * taps[0] + w[1:2, :] * taps[1] + w[2:3, :] * taps[2] + w[3:4, :] * taps[3]
        da_cn = jnp.concatenate([da_ref[...], jnp.where(i < nt - 1, dan_ref[...], 0.0)], axis=0)
        sg = jax.nn.sigmoid(conv)
        d_conv = da_cn * (sg * (1.0 + conv * (1.0 - sg)))
        du = None
        for j in range(4):
            term = w[j:j + 1, :] * _shift_up(d_conv, 3 - j)
            du = term if du is None else du + term
        d_ref[...] = du[:tt, :].astype(BF16)

        @pl.when(i == 0)
        def _():
            dw_ref[...] = jnp.zeros_like(dw_ref)

        d_cur = d_conv[:tt, :]
        rows = [jnp.sum(d_cur * taps[j][:tt, :], axis=0, keepdims=True) for j in range(4)]
        dw_ref[0:4, :] += jnp.concatenate(rows, axis=0)

    prev_spec, nxt_spec = _halo_specs(tt, wdt, col, nt)
    _, dan_spec = _halo_specs(tt, wdt, 0, nt)
    return pl.pallas_call(
        body, name=name, grid=(nt,),
        in_specs=[pl.BlockSpec((tt, wdt), lambda i: (i, col)), prev_spec, nxt_spec,
                  pl.BlockSpec((4, wdt), lambda i: (0, col)), pl.BlockSpec((tt, wdt), lambda i: (i, 0)), dan_spec],
        out_specs=[pl.BlockSpec((tt, wdt), lambda i: (i, 0)), pl.BlockSpec((SUBLANES, wdt), lambda i: (0, 0))],
        out_shape=[jax.ShapeDtypeStruct((t, wdt), BF16), jax.ShapeDtypeStruct((SUBLANES, wdt), F32)],
        compiler_params=_params("arbitrary"))(proj_qkv, proj_qkv, proj_qkv, conv_w, d_act, d_act)


def _rms_gate(o, gn, z):
    on = o * lax.rsqrt(jnp.mean(o * o, axis=-1, keepdims=True) + NORM_EPS) * gn
    return on * jax.nn.silu(z)


GDN_PREP_ROWS = 1024


def _unit_lower_inverse(low):
    c = low.shape[-1]
    eye = lax.broadcasted_iota(jnp.int32, low.shape, low.ndim - 2) == lax.broadcasted_iota(jnp.int32, low.shape, low.ndim - 1)
    x = -low
    inv = eye.astype(F32) + x
    for _ in range(c.bit_length() - 2):
        x = _mm_raw(x, x, "nn", THREE_PASS)
        inv = inv + _mm_raw(inv, x, "nn", THREE_PASS)
    return inv


@jax.custom_vjp
def _known_inverse(low, inv):
    return inv


def _known_inverse_fwd(low, inv):
    return inv, inv


def _known_inverse_bwd(inv, d_inv):
    return -_mm_raw(_mm_raw(inv, d_inv, "tn", THREE_PASS), inv, "nt", THREE_PASS), jnp.zeros_like(inv)


_known_inverse.defvjp(_known_inverse_fwd, _known_inverse_bwd)


def _gdn_prep(mm, qa, ka, va, braw, araw, alog, dtb, inv_kept=None):
    n, c, _ = qa.shape
    q = qa * lax.rsqrt(jnp.sum(qa * qa, axis=-1, keepdims=True) + 1e-6) * (HEAD_DIM ** -0.5)
    k = ka * lax.rsqrt(jnp.sum(ka * ka, axis=-1, keepdims=True) + 1e-6)
    beta = jax.nn.sigmoid(braw)
    g = -jnp.exp(alog) * jax.nn.softplus(araw + dtb)
    ri = lax.broadcasted_iota(jnp.int32, (n, c, c), 1)
    ci = lax.broadcasted_iota(jnp.int32, (n, c, c), 2)
    incl, strict, eye = ri >= ci, ri > ci, ri == ci
    gc = mm(incl.astype(F32), g, "nn", EXACT_LHS)
    gc_i = gc[:, :, :c]
    gc_j = mm(jnp.ones((n, c, c), F32), jnp.where(eye, gc_i, 0.0), "nn", EXACT_LHS)
    decay = jnp.where(incl, jnp.exp(jnp.where(incl, gc_i - gc_j, 0.0)), 0.0)
    kb = k * beta
    low = jnp.where(strict, mm(kb, k, "nt", ONE_PASS) * decay, 0.0)
    inv = _unit_lower_inverse(low) if inv_kept is None else _known_inverse(low, inv_kept)
    egc = jnp.exp(gc)
    u = mm(inv, va * beta, "nn", THREE_PASS)
    w = mm(inv, kb * egc, "nn", THREE_PASS)
    attn = jnp.where(incl, mm(q, k, "nt", ONE_PASS) * decay, 0.0)
    g_last = jnp.sum(g, axis=1, keepdims=True)
    outs = (u, w, q * egc, k * jnp.exp(g_last - gc), attn, jnp.exp(g_last))
    return outs + (inv,) if inv_kept is None else outs


def _gdn_scan(mm, u, w, qd, kd, attn, egl, z, gn, state):
    v_new = u - mm(w, state, "nn", ONE_PASS)
    o = mm(qd, state, "nn", ONE_PASS) + mm(attn, v_new, "nn", ONE_PASS)
    new_state = state * egl + mm(kd, v_new, "tn", ONE_PASS)
    return _rms_gate(o, gn, z), new_state


def _chunks(ref_value, n, c):
    return ref_value.reshape(n, c, ref_value.shape[-1])


def _by_head(ref, rows, heads):
    return jnp.stack([ref[rows, pl.ds(h * HEAD_DIM, HEAD_DIM)] for h in range(heads)])


def _store_heads(ref, rows, value):
    for h in range(value.shape[0]):
        ref[rows, pl.ds(h * HEAD_DIM, HEAD_DIM)] = value[h]


def _gdn_prep_specs(tb, nt_unused=None):
    col = lambda off: pl.BlockSpec((tb, HEAD_DIM), lambda h, i: (i, off + h))
    rep = pl.BlockSpec((1, tb, LANES), lambda h, i: (h, i, 0))
    par = pl.BlockSpec((1, SUBLANES, LANES), lambda h, i: (h, 0, 0))
    att = pl.BlockSpec((1, tb, GDN_CHUNK), lambda h, i: (h, i, 0))
    egl = pl.BlockSpec((1, tb // GDN_CHUNK, SUBLANES, LANES), lambda h, i: (h, i, 0, 0))
    return col, rep, par, att, egl


def _gdn_prep_fwd(qkv_act, braw, araw, alog, dtb):
    t = qkv_act.shape[0]
    tb = min(t, GDN_PREP_ROWS)
    nt, nc = t // tb, tb // GDN_CHUNK
    width = GDN_HEADS * HEAD_DIM

    def body(q_ref, k_ref, v_ref, br_ref, ar_ref, al_ref, dt_ref, u_ref, w_ref, qd_ref, kd_ref, at_ref, eg_ref, inv_ref):
        ch = lambda r: _chunks(r, nc, GDN_CHUNK)
        u, w, qd, kd, attn, egl, inv = _gdn_prep(_mm_raw, ch(q_ref[...]), ch(k_ref[...]), ch(v_ref[...]), ch(br_ref[0]),
                                                 ch(ar_ref[0]), al_ref[0, 0:1, :], dt_ref[0, 0:1, :])
        u_ref[...] = u.reshape(tb, HEAD_DIM)
        w_ref[...] = w.reshape(tb, HEAD_DIM).astype(BF16)
        qd_ref[...] = qd.reshape(tb, HEAD_DIM).astype(BF16)
        kd_ref[...] = kd.reshape(tb, HEAD_DIM).astype(BF16)
        at_ref[0] = attn.reshape(tb, GDN_CHUNK).astype(BF16)
        eg_ref[0] = jnp.broadcast_to(egl, (nc, SUBLANES, LANES))
        inv_ref[0] = inv.reshape(tb, GDN_CHUNK)

    col, rep, par, att, egl = _gdn_prep_specs(tb)
    h = GDN_HEADS
    return pl.pallas_call(
        body, name="gdn_prep_fwd", grid=(h, nt),
        in_specs=[col(0), col(h), col(2 * h), rep, rep, par, par],
        out_specs=[col(0), col(0), col(0), col(0), att, egl, att],
        out_shape=[jax.ShapeDtypeStruct((t, width), F32)] + [jax.ShapeDtypeStruct((t, width), BF16)] * 3
        + [jax.ShapeDtypeStruct((h, t, GDN_CHUNK), BF16), jax.ShapeDtypeStruct((h, t // GDN_CHUNK, SUBLANES, LANES), F32),
           jax.ShapeDtypeStruct((h, t, GDN_CHUNK), F32)],
        compiler_params=_params("parallel", "parallel"))(qkv_act, qkv_act, qkv_act, braw, araw, alog, dtb)


def _gdn_prep_bwd(qkv_act, braw, araw, alog, dtb, inv, du, dw, dqd, dkd, dattn, degl):
    t = qkv_act.shape[0]
    tb = min(t, GDN_PREP_ROWS)
    nt, nc = t // tb, tb // GDN_CHUNK
    width = GDN_HEADS * HEAD_DIM

    def body(q_ref, k_ref, v_ref, br_ref, ar_ref, al_ref, dt_ref, inv_ref, du_ref, dw_ref, dqd_ref, dkd_ref, dat_ref,
             deg_ref, dq_ref, dk_ref, dv_ref, dbr_ref, dar_ref, dal_ref, ddt_ref):
        @pl.when(pl.program_id(1) == 0)
        def _():
            dal_ref[...] = jnp.zeros_like(dal_ref)
            ddt_ref[...] = jnp.zeros_like(ddt_ref)

        ch = lambda r: _chunks(r, nc, GDN_CHUNK)
        _, vjp = jax.vjp(functools.partial(_gdn_prep, _mm_vjp, inv_kept=ch(inv_ref[0])), ch(q_ref[...]), ch(k_ref[...]),
                         ch(v_ref[...]), ch(br_ref[0]), ch(ar_ref[0]), al_ref[0, 0:1, :], dt_ref[0, 0:1, :])
        dq, dk, dv, dbr, dar, dal, ddt = vjp((ch(du_ref[...]), ch(dw_ref[...]), ch(dqd_ref[...]), ch(dkd_ref[...]),
                                              ch(dat_ref[0]), deg_ref[0][:, 0:1, :]))
        dq_ref[...] = dq.reshape(tb, HEAD_DIM)
        dk_ref[...] = dk.reshape(tb, HEAD_DIM)
        dv_ref[...] = dv.reshape(tb, HEAD_DIM)
        dbr_ref[0] = _lane_total(dbr.reshape(tb, LANES))
        dar_ref[0] = _lane_total(dar.reshape(tb, LANES))
        dal_ref[0, 0:1, :] += _lane_total(dal)
        ddt_ref[0, 0:1, :] += _lane_total(ddt)

    col, rep, par, att, egl = _gdn_prep_specs(tb)
    h = GDN_HEADS
    return pl.pallas_call(
        body, name="gdn_prep_bwd", grid=(h, nt),
        in_specs=[col(0), col(h), col(2 * h), rep, rep, par, par, att, col(0), col(0), col(0), col(0), att, egl],
        out_specs=[col(0), col(0), col(0), rep, rep, par, par],
        out_shape=[jax.ShapeDtypeStruct((t, width), F32)] * 3 + [jax.ShapeDtypeStruct((h, t, LANES), F32)] * 2
        + [jax.ShapeDtypeStruct((h, SUBLANES, LANES), F32)] * 2,
        compiler_params=_params("parallel", "arbitrary"))(qkv_act, qkv_act, qkv_act, braw, araw, alog, dtb, inv,
                                                         du, dw, dqd, dkd, dattn, degl)


def _scan_specs(tb, heads, chunk, rev, nt):
    ti = (lambda i: nt - 1 - i) if rev else (lambda i: i)
    row = pl.BlockSpec((tb, heads * HEAD_DIM), lambda i: (ti(i), 0))
    att = pl.BlockSpec((heads, tb, chunk), lambda i: (0, ti(i), 0))
    egl = pl.BlockSpec((heads, tb // chunk, SUBLANES, LANES), lambda i: (0, ti(i), 0, 0))
    hist = pl.BlockSpec((heads, tb // chunk, HEAD_DIM, HEAD_DIM), lambda i: (0, ti(i), 0, 0))
    gn = pl.BlockSpec((SUBLANES, LANES), lambda i: (0, 0))
    return row, att, egl, hist, gn


def _gdn_scan_fwd(u, w, qd, kd, attn, egl, zb, gn):
    t = u.shape[0]
    tb = min(t, 256)
    nt, nc = t // tb, tb // GDN_CHUNK
    nh = GDN_HEADS

    def body(u_ref, w_ref, qd_ref, kd_ref, at_ref, eg_ref, z_ref, gn_ref, y_ref, hist_ref, s_ref):
        @pl.when(pl.program_id(0) == 0)
        def _():
            s_ref[...] = jnp.zeros_like(s_ref)

        g = gn_ref[0:1, :]
        state = s_ref[...]
        for c in range(nc):
            rows = pl.ds(c * GDN_CHUNK, GDN_CHUNK)
            heads = lambda r: _by_head(r, rows, nh)
            hist_ref[:, c] = state
            y, state = _gdn_scan(_mm_raw, heads(u_ref), heads(w_ref), heads(qd_ref), heads(kd_ref), at_ref[:, rows, :],
                                 eg_ref[:, c, 0:1, :], heads(z_ref), g, state)
            _store_heads(y_ref, rows, y.astype(BF16))
        s_ref[...] = state

    row, att, egs, hist, gns = _scan_specs(tb, nh, GDN_CHUNK, False, nt)
    return pl.pallas_call(
        body, name="gdn_scan_fwd", grid=(nt,), in_specs=[row, row, row, row, att, egs, row, gns], out_specs=[row, hist],
        out_shape=[jax.ShapeDtypeStruct((t, nh * HEAD_DIM), BF16),
                   jax.ShapeDtypeStruct((nh, t // GDN_CHUNK, HEAD_DIM, HEAD_DIM), F32)],
        scratch_shapes=[pltpu.VMEM((nh, HEAD_DIM, HEAD_DIM), F32)],
        compiler_params=_params("arbitrary"))(u, w, qd, kd, attn, egl, zb, gn)


def _gdn_scan_bwd(u, w, qd, kd, attn, egl, zb, gn, hist, dy):
    t = u.shape[0]
    tb = min(t, 256)
    nt, nc = t // tb, tb // GDN_CHUNK
    nh = GDN_HEADS

    def body(u_ref, w_ref, qd_ref, kd_ref, at_ref, eg_ref, z_ref, gn_ref, hist_ref, dy_ref,
             du_ref, dw_ref, dqd_ref, dkd_ref, dat_ref, deg_ref, dz_ref, dgn_ref, ds_ref):
        @pl.when(pl.program_id(0) == 0)
        def _():
            ds_ref[...] = jnp.zeros_like(ds_ref)
            dgn_ref[...] = jnp.zeros_like(dgn_ref)

        g = gn_ref[0:1, :]
        d_state = ds_ref[...]
        for c in reversed(range(nc)):
            rows = pl.ds(c * GDN_CHUNK, GDN_CHUNK)
            heads = lambda r: _by_head(r, rows, nh).astype(F32)
            _, vjp = jax.vjp(functools.partial(_gdn_scan, _mm_vjp), heads(u_ref), heads(w_ref), heads(qd_ref),
                             heads(kd_ref), at_ref[:, rows, :].astype(F32), eg_ref[:, c, 0:1, :], heads(z_ref), g,
                             hist_ref[:, c])
            du, dw, dqd, dkd, dat, deg, dz, dgn, d_state = vjp((heads(dy_ref), d_state))
            _store_heads(du_ref, rows, du)
            _store_heads(dw_ref, rows, dw)
            _store_heads(dqd_ref, rows, dqd)
            _store_heads(dkd_ref, rows, dkd)
            dat_ref[:, rows, :] = dat
            deg_ref[:, c] = jnp.broadcast_to(deg, (nh, SUBLANES, LANES))
            _store_heads(dz_ref, rows, dz.astype(BF16))
            dgn_ref[0:1, :] += dgn
        ds_ref[...] = d_state

    row, att, egs, hists, gns = _scan_specs(tb, nh, GDN_CHUNK, True, nt)
    wide = jax.ShapeDtypeStruct((t, nh * HEAD_DIM), F32)
    return pl.pallas_call(
        body, name="gdn_scan_bwd", grid=(nt,),
        in_specs=[row, row, row, row, att, egs, row, gns, hists, row],
        out_specs=[row, row, row, row, att, egs, row, gns],
        out_shape=[wide] * 4 + [jax.ShapeDtypeStruct((nh, t, GDN_CHUNK), F32),
                                jax.ShapeDtypeStruct((nh, t // GDN_CHUNK, SUBLANES, LANES), F32),
                                jax.ShapeDtypeStruct((t, nh * HEAD_DIM), BF16),
                                jax.ShapeDtypeStruct((SUBLANES, LANES), F32)],
        scratch_shapes=[pltpu.VMEM((nh, HEAD_DIM, HEAD_DIM), F32)],
        compiler_params=_params("arbitrary"))(u, w, qd, kd, attn, egl, zb, gn, hist, dy)


def _hgrn_prep(mm, qr, fr, lbl):
    n, c, _ = qr.shape
    lb = jax.nn.sigmoid(lbl[1:2, :] - lbl[0:1, :])
    f = lb + (1.0 - lb) * jax.nn.sigmoid(fr)
    q = jax.nn.silu(qr)
    k = 1.0 - f
    logf = jnp.log(f)
    ri = lax.broadcasted_iota(jnp.int32, (n, c, c), 1)
    ci = lax.broadcasted_iota(jnp.int32, (n, c, c), 2)
    b = mm((ri >= ci).astype(F32), logf, "nn", EXACT_LHS)
    attn = _hgrn_attn(mm, q, k, b)
    b_last = jnp.sum(logf, axis=1, keepdims=True)
    return q * jnp.exp(b), k * jnp.exp(b_last - b), attn, jnp.exp(b_last)


HGRN_SUB = 8


@functools.partial(jax.custom_vjp, nondiff_argnums=(1,))
def _roll_rows(x, shift):
    return pltpu.roll(x, shift, x.ndim - 2)


def _roll_rows_fwd(x, shift):
    return _roll_rows(x, shift), None


def _roll_rows_bwd(shift, _, d):
    return (pltpu.roll(d, d.shape[-2] - shift, d.ndim - 2),)


_roll_rows.defvjp(_roll_rows_fwd, _roll_rows_bwd)


@jax.custom_vjp
def _exp_clamped(v):
    return jnp.exp(jnp.minimum(v, 0.0))


def _exp_clamped_fwd(v):
    out = jnp.exp(jnp.minimum(v, 0.0))
    return out, out


def _exp_clamped_bwd(out, d):
    return (d * out,)


_exp_clamped.defvjp(_exp_clamped_fwd, _exp_clamped_bwd)


def _hgrn_attn(mm, q, k, b):
    n, c, d = q.shape
    sb = HGRN_SUB
    sub = lambda a: a.reshape(n * c // sb, sb, d)
    qs, ks, bs = sub(q), sub(k), sub(b)
    row = lax.broadcasted_iota(jnp.int32, (n, c, c), 1)
    col = lax.broadcasted_iota(jnp.int32, (n, c, c), 2)
    same_block = (row & -sb) == (col & -sb)
    attn = None
    for delta in range(sb):
        if delta == 0:
            prod = qs * ks
        else:
            prod = qs * _roll_rows(ks, delta) * _exp_clamped(bs - _roll_rows(bs, delta))
        sums = jnp.sum(prod, axis=-1, keepdims=True).reshape(n, c, 1)
        term = jnp.where(same_block & (row - col == delta), sums, 0.0)
        attn = term if attn is None else attn + term
    far = [jnp.zeros((n, sb, c), F32)]
    col8 = lax.broadcasted_iota(jnp.int32, (n, sb, c), 2)
    for i in range(1, c // sb):
        r0 = i * sb
        bi = b[:, r0:r0 + sb, :]
        ref = bi[:, 0:1, :]
        part = mm(q[:, r0:r0 + sb, :] * jnp.exp(bi - ref), k * _exp_clamped(ref - b), "nt", ONE_PASS)
        far.append(jnp.where(col8 < r0, part, 0.0))
    return attn + jnp.concatenate(far, axis=1)


def _hgrn_scan(mm, qe, kd, attn, ebl, iv, z, gn, state):
    o = mm(qe, state, "nt", ONE_PASS) + mm(attn, iv, "nn", ONE_PASS)
    new_state = state * ebl + mm(iv, kd, "tn", ONE_PASS)
    return _rms_gate(o, gn, z), new_state


def _hgrn_prep_specs(tb):
    col = pl.BlockSpec((tb, HEAD_DIM), lambda h, i: (i, h))
    lbs = pl.BlockSpec((2, HEAD_DIM), lambda h, i: (0, h))
    att = pl.BlockSpec((1, tb, HGRN_CHUNK), lambda h, i: (h, i, 0))
    ebl = pl.BlockSpec((1, tb // HGRN_CHUNK, SUBLANES, LANES), lambda h, i: (h, i, 0, 0))
    return col, lbs, att, ebl


def _hgrn_prep_fwd(qr, fr, lower_bounds):
    t = qr.shape[0]
    tb = min(t, 256)
    nt, nc = t // tb, tb // HGRN_CHUNK
    hh = HGRN_HEADS

    def body(q_ref, f_ref, lb_ref, qe_ref, kd_ref, at_ref, eb_ref):
        ch = lambda r: _chunks(r, nc, HGRN_CHUNK)
        qe, kd, attn, ebl = _hgrn_prep(_mm_raw, ch(q_ref[...]), ch(f_ref[...]), lb_ref[...])
        qe_ref[...] = qe.reshape(tb, HEAD_DIM).astype(BF16)
        kd_ref[...] = kd.reshape(tb, HEAD_DIM).astype(BF16)
        at_ref[0] = attn.reshape(tb, HGRN_CHUNK).astype(BF16)
        eb_ref[0] = jnp.broadcast_to(ebl, (nc, SUBLANES, LANES))

    col, lbs, att, ebs = _hgrn_prep_specs(tb)
    return pl.pallas_call(
        body, name="hgrn_prep_fwd", grid=(hh, nt), in_specs=[col, col, lbs], out_specs=[col, col, att, ebs],
        out_shape=[jax.ShapeDtypeStruct((t, HGRN_WIDTH), BF16)] * 2
        + [jax.ShapeDtypeStruct((hh, t, HGRN_CHUNK), BF16), jax.ShapeDtypeStruct((hh, t // HGRN_CHUNK, SUBLANES, LANES), F32)],
        compiler_params=_params("parallel", "parallel"))(qr, fr, lower_bounds)


def _hgrn_prep_bwd(qr, fr, lower_bounds, dqe, dkd, dattn, debl):
    t = qr.shape[0]
    tb = min(t, 256)
    nt, nc = t // tb, tb // HGRN_CHUNK
    hh = HGRN_HEADS

    def body(q_ref, f_ref, lb_ref, dqe_ref, dkd_ref, dat_ref, deb_ref, dq_ref, df_ref, dlb_ref):
        @pl.when(pl.program_id(1) == 0)
        def _():
            dlb_ref[...] = jnp.zeros_like(dlb_ref)

        ch = lambda r: _chunks(r, nc, HGRN_CHUNK)
        _, vjp = jax.vjp(functools.partial(_hgrn_prep, _mm_vjp), ch(q_ref[...]), ch(f_ref[...]), lb_ref[...])
        dq, df, dlb = vjp((ch(dqe_ref[...]), ch(dkd_ref[...]), ch(dat_ref[0]), deb_ref[0][:, 0:1, :]))
        dq_ref[...] = dq.reshape(tb, HEAD_DIM).astype(BF16)
        df_ref[...] = df.reshape(tb, HEAD_DIM).astype(BF16)
        dlb_ref[...] += dlb

    col, lbs, att, ebs = _hgrn_prep_specs(tb)
    return pl.pallas_call(
        body, name="hgrn_prep_bwd", grid=(hh, nt), in_specs=[col, col, lbs, col, col, att, ebs],
        out_specs=[col, col, lbs],
        out_shape=[jax.ShapeDtypeStruct((t, HGRN_WIDTH), BF16)] * 2 + [jax.ShapeDtypeStruct((2, HGRN_WIDTH), F32)],
        compiler_params=_params("parallel", "arbitrary"))(qr, fr, lower_bounds, dqe, dkd, dattn, debl)


def _hgrn_scan_fwd(qe, kd, attn, ebl, iv, z, gn):
    t = qe.shape[0]
    tb = min(t, 128)
    nt, nc = t // tb, tb // HGRN_CHUNK
    hh = HGRN_HEADS

    def body(qe_ref, kd_ref, at_ref, eb_ref, i_ref, z_ref, gn_ref, y_ref, hist_ref, s_ref):
        @pl.when(pl.program_id(0) == 0)
        def _():
            s_ref[...] = jnp.zeros_like(s_ref)

        g = gn_ref[0:1, :]
        state = s_ref[...]
        for c in range(nc):
            rows = pl.ds(c * HGRN_CHUNK, HGRN_CHUNK)
            heads = lambda r: _by_head(r, rows, hh)
            hist_ref[:, c] = state
            y, state = _hgrn_scan(_mm_raw, heads(qe_ref), heads(kd_ref), at_ref[:, rows, :], eb_ref[:, c, 0:1, :],
                                  heads(i_ref), heads(z_ref), g, state)
            _store_heads(y_ref, rows, y.astype(BF16))
        s_ref[...] = state

    row, att, ebs, hist, gns = _scan_specs(tb, hh, HGRN_CHUNK, False, nt)
    return pl.pallas_call(
        body, name="hgrn_scan_fwd", grid=(nt,), in_specs=[row, row, att, ebs, row, row, gns], out_specs=[row, hist],
        out_shape=[jax.ShapeDtypeStruct((t, HGRN_WIDTH), BF16),
                   jax.ShapeDtypeStruct((hh, t // HGRN_CHUNK, HEAD_DIM, HEAD_DIM), F32)],
        scratch_shapes=[pltpu.VMEM((hh, HEAD_DIM, HEAD_DIM), F32)],
        compiler_params=_params("arbitrary"))(qe, kd, attn, ebl, iv, z, gn)


def _hgrn_scan_bwd(qe, kd, attn, ebl, iv, z, gn, hist, dy):
    t = qe.shape[0]
    tb = min(t, 128)
    nt, nc = t // tb, tb // HGRN_CHUNK
    hh = HGRN_HEADS

    def body(qe_ref, kd_ref, at_ref, eb_ref, i_ref, z_ref, gn_ref, hist_ref, dy_ref,
             dqe_ref, dkd_ref, dat_ref, deb_ref, di_ref, dz_ref, dgn_ref, ds_ref):
        @pl.when(pl.program_id(0) == 0)
        def _():
            ds_ref[...] = jnp.zeros_like(ds_ref)
            dgn_ref[...] = jnp.zeros_like(dgn_ref)

        g = gn_ref[0:1, :]
        d_state = ds_ref[...]
        for c in reversed(range(nc)):
            rows = pl.ds(c * HGRN_CHUNK, HGRN_CHUNK)
            heads = lambda r: _by_head(r, rows, hh).astype(F32)
            _, vjp = jax.vjp(functools.partial(_hgrn_scan, _mm_vjp), heads(qe_ref), heads(kd_ref),
                             at_ref[:, rows, :].astype(F32), eb_ref[:, c, 0:1, :], heads(i_ref), heads(z_ref), g,
                             hist_ref[:, c])
            dqe, dkd, dat, deb, di, dz, dgn, d_state = vjp((heads(dy_ref), d_state))
            _store_heads(dqe_ref, rows, dqe)
            _store_heads(dkd_ref, rows, dkd)
            dat_ref[:, rows, :] = dat
            deb_ref[:, c] = jnp.broadcast_to(deb, (hh, SUBLANES, LANES))
            _store_heads(di_ref, rows, di.astype(BF16))
            _store_heads(dz_ref, rows, dz.astype(BF16))
            dgn_ref[0:1, :] += dgn
        ds_ref[...] = d_state

    row, att, ebs, hists, gns = _scan_specs(tb, hh, HGRN_CHUNK, True, nt)
    wide = lambda dt: jax.ShapeDtypeStruct((t, HGRN_WIDTH), dt)
    return pl.pallas_call(
        body, name="hgrn_scan_bwd", grid=(nt,),
        in_specs=[row, row, att, ebs, row, row, gns, hists, row],
        out_specs=[row, row, att, ebs, row, row, gns],
        out_shape=[wide(F32), wide(F32), jax.ShapeDtypeStruct((hh, t, HGRN_CHUNK), F32),
                   jax.ShapeDtypeStruct((hh, t // HGRN_CHUNK, SUBLANES, LANES), F32), wide(BF16), wide(BF16),
                   jax.ShapeDtypeStruct((SUBLANES, LANES), F32)],
        scratch_shapes=[pltpu.VMEM((hh, HEAD_DIM, HEAD_DIM), F32)],
        compiler_params=_params("arbitrary"))(qe, kd, attn, ebl, iv, z, gn, hist, dy)


def _layer_norm(pre, g, b):
    mu = jnp.mean(pre, axis=-1, keepdims=True)
    d = pre - mu
    var = jnp.mean(d * d, axis=-1, keepdims=True)
    return d * lax.rsqrt(var + NORM_EPS) * g + b


def _lnpl_fwd(xin, s, p, wg, wpl, ln_g, ln_b):
    t = xin.shape[0]
    tt = min(t, 256)

    def body(x_ref, s_ref, p_ref, wg_ref, wpl_ref, g_ref, b_ref, o_ref, ob_ref):
        xn = _layer_norm(DEEPNORM_ALPHA * x_ref[...] + s_ref[...], g_ref[...], b_ref[...])
        gate = jax.nn.sigmoid(_mm_raw(xn, wg_ref[...], "nn", False))
        out = xn + _mm_raw(p_ref[...], wpl_ref[...], "nn", False) * gate
        o_ref[...] = out
        ob_ref[...] = out.astype(BF16)

    row = lambda w: pl.BlockSpec((tt, w), lambda i: (i, 0))
    full = lambda a: pl.BlockSpec(a.shape, lambda i: (0, 0))
    return pl.pallas_call(
        body, name="lnpl_fwd", grid=(t // tt,),
        in_specs=[row(D_MODEL), row(D_MODEL), row(PL_DIM), full(wg), full(wpl), full(ln_g), full(ln_b)],
        out_specs=[row(D_MODEL), row(D_MODEL)],
        out_shape=[jax.ShapeDtypeStruct((t, D_MODEL), F32), jax.ShapeDtypeStruct((t, D_MODEL), BF16)],
        compiler_params=_params("parallel"))(xin, s, p, wg, wpl, ln_g, ln_b)


def _lnpl_bwd(xin, s, p, wg, wpl, ln_g, ln_b, upstream, last, name):
    t = xin.shape[0]
    tt = min(t, 256)

    def body(x_ref, s_ref, p_ref, wg_ref, wpl_ref, g_ref, b_ref, up_ref,
             dpre_ref, dwg_ref, dwpl_ref, dg_ref, db_ref, loss_ref):
        @pl.when(pl.program_id(0) == 0)
        def _():
            for r in (dwg_ref, dwpl_ref, dg_ref, db_ref, loss_ref):
                r[...] = jnp.zeros_like(r)

        pre = DEEPNORM_ALPHA * x_ref[...] + s_ref[...]
        xn, ln_vjp = jax.vjp(_layer_norm, pre, g_ref[...], b_ref[...])
        gate = jax.nn.sigmoid(_mm_raw(xn, wg_ref[...], "nn", False))
        plv = _mm_raw(p_ref[...], wpl_ref[...], "nn", False)
        if last:
            err = xn + plv * gate - up_ref[...]
            dout = err * (1.0 / D_MODEL)
            tot = jnp.sum(jnp.sum(err * err, axis=1, keepdims=True), axis=0, keepdims=True) * (0.5 / D_MODEL)
            loss_ref[...] += jnp.broadcast_to(tot, loss_ref.shape)
        else:
            dout = up_ref[...]
        dplv = dout * gate
        dlogits = dout * plv * gate * (1.0 - gate)
        dwg_ref[...] += _mm_raw(xn, dlogits, "tn", False)
        dwpl_ref[...] += _mm_raw(p_ref[...], dplv, "tn", False)
        dxn = dout + _mm_raw(dlogits, wg_ref[...], "nt", False)
        dpre, dg, db = ln_vjp(dxn)
        dpre_ref[...] = dpre
        dg_ref[...] += dg
        db_ref[...] += db

    row = lambda w: pl.BlockSpec((tt, w), lambda i: (i, 0))
    full = lambda shape: pl.BlockSpec(shape, lambda i: (0, 0))
    return pl.pallas_call(
        body, name=name, grid=(t // tt,),
        in_specs=[row(D_MODEL), row(D_MODEL), row(PL_DIM), full(wg.shape), full(wpl.shape), full(ln_g.shape),
                  full(ln_b.shape), row(D_MODEL)],
        out_specs=[row(D_MODEL), full(wg.shape), full(wpl.shape), full(ln_g.shape), full(ln_b.shape),
                   full((SUBLANES, LANES))],
        out_shape=[jax.ShapeDtypeStruct((t, D_MODEL), F32), jax.ShapeDtypeStruct(wg.shape, F32),
                   jax.ShapeDtypeStruct(wpl.shape, F32), jax.ShapeDtypeStruct(ln_g.shape, F32),
                   jax.ShapeDtypeStruct(ln_b.shape, F32), jax.ShapeDtypeStruct((SUBLANES, LANES), F32)],
        compiler_params=_params("arbitrary"))(xin, s, p, wg, wpl, ln_g, ln_b, upstream)


def _pack_tail(dbr, dar):
    nh, t, _ = dbr.shape
    tt = min(t, 512)

    def body(b_ref, a_ref, o_ref):
        lane = lax.broadcasted_iota(jnp.int32, (tt, LANES), 1)
        acc = jnp.zeros((tt, LANES), F32)
        for h in range(nh):
            acc = jnp.where(lane == h, b_ref[h], acc)
            acc = jnp.where(lane == nh + h, a_ref[h], acc)
        o_ref[...] = acc.astype(BF16)

    spec = pl.BlockSpec((nh, tt, LANES), lambda i: (0, i, 0))
    return pl.pallas_call(
        body, name="pack_tail", grid=(t // tt,), in_specs=[spec, spec], out_specs=pl.BlockSpec((tt, LANES), lambda i: (i, 0)),
        out_shape=jax.ShapeDtypeStruct((t, LANES), BF16), compiler_params=_params("parallel"))(dbr, dar)


def _rep_rows(v):
    return jnp.broadcast_to(v.reshape(1, LANES), (SUBLANES, LANES))


def _rep_heads(v):
    return jnp.broadcast_to(v.reshape(-1, 1, 1), (v.shape[0], SUBLANES, LANES))


def _col_range(stacked, lo, hi):
    c = stacked.shape[2]
    parts = [stacked[s, :, max(lo, s * c) - s * c:min(hi, (s + 1) * c) - s * c]
             for s in range(4) if max(lo, s * c) < min(hi, (s + 1) * c)]
    return parts[0] if len(parts) == 1 else jnp.concatenate(parts, axis=1)


def _col_shards(pieces, c):
    shards, offs, o = [], [], 0
    for pc in pieces:
        offs.append(o)
        o += pc.shape[1]
    for s in range(4):
        lo, hi = s * c, (s + 1) * c
        parts = [pc[:, max(lo, o) - o:min(hi, o + pc.shape[1]) - o] for pc, o in zip(pieces, offs)
                 if max(lo, o) < min(hi, o + pc.shape[1])]
        shards.append(parts[0] if len(parts) == 1 else jnp.concatenate(parts, axis=1))
    return jnp.stack(shards)


def _local_step(x, p, target, w, late_weights, early_grads_ready, last_grad_ready, start_token):
    a = DEEPNORM_ALPHA
    nh = GDN_HEADS
    xb = (x + start_token).astype(BF16)
    wie = w["w_in_even"]
    w_a, w_qkv, w_zb = _col_range(wie, 0, 4096), _col_range(wie, 4096, 7168), _col_range(wie, 7168, 8192)
    w_tail = jnp.pad(_col_range(wie, 8192, 8192 + 2 * nh), ((0, 0), (0, LANES - 2 * nh)))
    conv_a_w, conv_b_w = w["conv_a_w"], w["conv_b_w"]
    ln_g0, ln_b0, ln_g1, ln_b1 = (v.reshape(1, D_MODEL) for v in (w["ln_g"][0], w["ln_b"][0], w["ln_g"][1], w["ln_b"][1]))
    alog, dtb = _rep_heads(w["a_log"].reshape(nh)), _rep_heads(w["dt_bias"].reshape(nh))
    gdn_g, hgrn_g = _rep_rows(w["gdn_norm_g"]), _rep_rows(w["hgrn_norm_g"])

    proj_a = _matmul(xb, w_a, name="fwd_proj_a")
    proj_qkv = _matmul(xb, w_qkv, name="fwd_proj_qkv")
    proj_zb = _matmul(xb, w_zb, name="fwd_proj_zb")
    proj_tail = _matmul(xb, w_tail, name="fwd_proj_tail")
    rep = lambda cols: jnp.broadcast_to(cols.T[:, :, None], (nh, cols.shape[0], LANES))
    braw, araw = rep(proj_tail[:, :nh]), rep(proj_tail[:, nh:2 * nh])
    y_a = _conv_a_fwd(proj_a, conv_a_w)
    qkv_act = _conv_b_fwd(proj_qkv, conv_b_w)
    *gdn_pre, gdn_inv = _gdn_prep_fwd(qkv_act, braw, araw, alog, dtb)
    y_b, gdn_hist = _gdn_scan_fwd(*gdn_pre, proj_zb, gdn_g)
    w = {**w, **late_weights(y_b)}
    woe, wio, woo = w["w_out_even"], w["w_in_odd"], w["w_out_odd"]
    s0 = _matmul(y_b, woe[1024:], name="fwd_out_even_b", add=_matmul(y_a, woe[:1024], name="fwd_out_even_a"))
    x1, x1b = _lnpl_fwd(x, s0, p[0], w["w_pl_gate"][0], w["w_pl"][0], ln_g0, ln_b0)
    proj_o = [_matmul(x1b, wio[j], name=f"fwd_proj_odd{j}") for j in range(4)]
    hgrn_pre = _hgrn_prep_fwd(proj_o[0], proj_o[1], w["lower_bounds"])
    y_o, hgrn_hist = _hgrn_scan_fwd(*hgrn_pre, proj_o[2], proj_o[3], hgrn_g)
    s1 = _matmul(y_o, woo, name="fwd_out_odd")

    g = {}
    dpre1, dwg1, dwpl1, dlng1, dlnb1, loss = _lnpl_bwd(x1, s1, p[1], w["w_pl_gate"][1], w["w_pl"][1], ln_g1, ln_b1,
                                                     target, True, "lnpl_bwd_odd")
    dy_o = _matmul(dpre1, woo, tb=True, name="bwd_out_odd_dx")
    g["w_out_odd"] = _matmul(y_o, dpre1, ta=True, name="bwd_out_odd_dw")
    dqe, dkd, dat, deb, di, dz, dhg = _hgrn_scan_bwd(*hgrn_pre, proj_o[2], proj_o[3], hgrn_g, hgrn_hist, dy_o)
    dq, df, dlb = _hgrn_prep_bwd(proj_o[0], proj_o[1], w["lower_bounds"], dqe, dkd, dat, deb)
    dx1 = dpre1
    scale = a
    dws = []
    for j, dj in enumerate((dq, df, di, dz)):
        dx1 = _matmul(dj, wio[j], tb=True, add=dx1, add_scale=scale, name=f"bwd_proj_odd_dx{j}")
        scale = 1.0
        dws.append(_matmul(x1b, dj, ta=True, name=f"bwd_proj_odd_dw{j}"))
    g["w_in_odd"] = jnp.stack(dws)
    g["hgrn_norm_g"] = dhg[0:1]
    g["lower_bounds"] = dlb
    g["w_pl_gate1"], g["w_pl1"] = dwg1, dwpl1

    dpre0, dwg0, dwpl0, dlng0, dlnb0, _ = _lnpl_bwd(x, s0, p[0], w["w_pl_gate"][0], w["w_pl"][0], ln_g0, ln_b0,
                                                  dx1, False, "lnpl_bwd_even")
    g["w_pl_gate0"], g["w_pl0"] = dwg0, dwpl0
    g["ln_g"] = jnp.concatenate([dlng0, dlng1], axis=0)
    g["ln_b"] = jnp.concatenate([dlnb0, dlnb1], axis=0)
    dy_a = _matmul(dpre0, woe[:1024], tb=True, name="bwd_out_even_dxa")
    dy_b = _matmul(dpre0, woe[1024:], tb=True, name="bwd_out_even_dxb")
    g["w_out_even"] = jnp.concatenate([_matmul(y_a, dpre0, ta=True, name="bwd_out_even_dwa"),
                                       _matmul(y_b, dpre0, ta=True, name="bwd_out_even_dwb")], axis=0)
    token = early_grads_ready({n: g[n] for n in _RS_EARLY})
    conv_a_w, gdn_g = conv_a_w + token, gdn_g + token
    du, dw, dqd, dkd, dat, deg, dzb, dgn = _gdn_scan_bwd(*gdn_pre, proj_zb, gdn_g, gdn_hist, dy_b)
    dqa, dka, dva, dbr, dar, dal, ddt = _gdn_prep_bwd(qkv_act, braw, araw, alog, dtb, gdn_inv, du, dw, dqd, dkd, dat, deg)
    g["a_log"] = dal[:, 0, 0].reshape(1, nh)
    g["dt_bias"] = ddt[:, 0, 0].reshape(1, nh)
    g["gdn_norm_g"] = dgn[0:1]
    d_pre_qkv, dwb = [], []
    for j, dj in enumerate((dqa, dka, dva)):
        dpj, dwj = _conv_b_bwd(proj_qkv, conv_b_w, dj, j, f"conv_b_bwd{j}")
        d_pre_qkv.append(dpj)
        dwb.append(dwj[:4])
    g["conv_b_w"] = jnp.concatenate(dwb, axis=1)
    d_a, dwa = _conv_a_bwd(proj_a, conv_a_w, dy_a)
    g["conv_a_w"] = dwa[:3]
    d_tail = _pack_tail(dbr, dar)
    pieces = [(d_a, w_a), (d_pre_qkv[0], w_qkv[:, :1024]), (d_pre_qkv[1], w_qkv[:, 1024:2048]),
              (d_pre_qkv[2], w_qkv[:, 2048:]), (dzb, w_zb), (d_tail, w_tail)]
    dws = [_matmul(xb, dj, ta=True, name=f"bwd_proj_even_dw{j}") for j, (dj, _) in enumerate(pieces)]
    dws[-1] = dws[-1][:, :2 * nh]
    g["w_in_even"] = _col_shards(dws, wie.shape[2])
    token = last_grad_ready(g["w_in_even"])
    dx = dpre0
    scale = a
    for j, (dj, wj) in enumerate(pieces):
        dx = _matmul(dj, wj + jnp.asarray(token).astype(BF16) if j == 0 else wj, tb=True, add=dx, add_scale=scale,
                     name=f"bwd_proj_even_dx{j}")
        scale = 1.0
    return loss, dx, g


def _adamw(w, g, m, v, name):
    lead, rows, cols = w.shape
    tr = rows if rows <= 256 else 256
    assert rows % tr == 0, (name, rows)

    def body(w_ref, g_ref, m_ref, v_ref, d_ref, nm_ref, nv_ref):
        gg = g_ref[...]
        nm = ADAM_B1 * m_ref[...] + (1.0 - ADAM_B1) * gg
        nv = ADAM_B2 * v_ref[...] + (1.0 - ADAM_B2) * jnp.square(gg)
        m_hat = nm / (1.0 - ADAM_B1 ** ADAM_STEP)
        v_hat = nv / (1.0 - ADAM_B2 ** ADAM_STEP)
        d_ref[...] = -ADAM_LR * (m_hat / (jnp.sqrt(v_hat) + ADAM_EPS) + ADAM_WD * w_ref[...])
        nm_ref[...] = nm
        nv_ref[...] = nv

    spec = pl.BlockSpec((1, tr, cols), lambda l, i: (l, i, 0))
    return pl.pallas_call(
        body, name=name, grid=(lead, rows // tr), in_specs=[spec] * 4, out_specs=[spec] * 3,
        out_shape=[jax.ShapeDtypeStruct(w.shape, F32)] * 3, compiler_params=_params("parallel", "parallel"))(w, g, m, v)


MESH = pl.DeviceIdType.MESH
N_DEV = 8
HBM_SPEC = pl.BlockSpec(memory_space=pltpu.HBM)
VMEM_SPEC = pl.BlockSpec(memory_space=pltpu.VMEM)


def _coords():
    return lax.axis_index("x"), lax.axis_index("y"), lax.axis_index("c")


def _flip(v, bit):
    return 1 - v if bit else v


def _remote(src, dst, send_sem, recv_sem, dev):
    return pltpu.make_async_remote_copy(src_ref=src, dst_ref=dst, send_sem=send_sem, recv_sem=recv_sem,
                                        device_id=dev, device_id_type=MESH)


def _exchange_small(buf, reduce, name):
    rows = buf.shape[0]

    def body(in_ref, out_ref, slots, send_sems, recv_sems):
        x, y, c = _coords()
        me = 4 * x + 2 * y + c
        slots[me] = in_ref[...]
        peer = lambda k: (_flip(x, (k >> 2) & 1), _flip(y, (k >> 1) & 1), _flip(c, k & 1))
        sends = []
        for k in range(1, N_DEV):
            cp = _remote(in_ref, slots.at[me], send_sems.at[k - 1], recv_sems.at[k - 1], peer(k))
            cp.start()
            sends.append(cp)
        for k in range(1, N_DEV):
            px, py, pc = peer(k)
            _remote(in_ref, slots.at[4 * px + 2 * py + pc], send_sems.at[k - 1], recv_sems.at[k - 1], peer(k)).wait_recv()
        for cp in sends:
            cp.wait_send()
        if reduce:
            acc = slots[0]
            for d in range(1, N_DEV):
                acc = acc + slots[d]
            out_ref[...] = acc
        else:
            out_ref[...] = slots[...]

    out_shape = (rows, LANES) if reduce else (N_DEV, rows, LANES)
    return pl.pallas_call(
        body, name=name, in_specs=[VMEM_SPEC], out_specs=VMEM_SPEC, out_shape=jax.ShapeDtypeStruct(out_shape, F32),
        scratch_shapes=[pltpu.VMEM((N_DEV, rows, LANES), F32), pltpu.SemaphoreType.DMA((N_DEV - 1,)),
                        pltpu.SemaphoreType.DMA((N_DEV - 1,))])(buf)


def _half_rows(half, which):
    return pl.ds(pl.multiple_of(which * half, 16), half)


def _other_chip(x, y, k):
    return _flip(x, (k >> 1) & 1), _flip(y, k & 1)


SEM_SPEC = pl.BlockSpec(memory_space=pltpu.SEMAPHORE)
DATAFLOW = pltpu.SideEffectType.DATAFLOW_SIDE_EFFECTING


def _ici_piece(srcs, lands, send_sems, recv_sems, i, k, x, y, c):
    half = srcs[i].shape[0] // 2
    ox, oy = _other_chip(x, y, k)
    return _remote(srcs[i].at[_half_rows(half, c)], lands[i].at[2 * x + y, _half_rows(half, c)],
                   send_sems.at[3 * i + k - 1], recv_sems.at[3 * i + k - 1], (ox, oy, c)), (ox, oy)


def _gather_start(shards, name):
    n = len(shards)

    def body(*refs):
        srcs, lands = refs[:n], refs[n:2 * n]
        send_sems, recv_sems = refs[2 * n], refs[2 * n + 1]
        token = refs[-1]
        x, y, c = _coords()
        for i in range(n):
            for k in (1, 2, 3):
                _ici_piece(srcs, lands, send_sems, recv_sems, i, k, x, y, c)[0].start()
        token[...] = jnp.zeros_like(token)

    hbm = lambda a: pltpu.with_memory_space_constraint(a, pltpu.HBM)
    lands = [lax.empty((4,) + s.shape, s.dtype) for s in shards]
    out = pl.pallas_call(
        body, name=name,
        out_shape=(pltpu.SemaphoreType.DMA((3 * n,)), pltpu.SemaphoreType.DMA((3 * n,)),
                   *[pltpu.HBM(s.shape, s.dtype) for s in shards], *[pltpu.HBM(a.shape, a.dtype) for a in lands],
                   jax.ShapeDtypeStruct((SUBLANES, LANES), F32)),
        in_specs=[HBM_SPEC] * (2 * n), out_specs=(SEM_SPEC, SEM_SPEC, *[HBM_SPEC] * (2 * n), VMEM_SPEC),
        input_output_aliases={i: 2 + i for i in range(2 * n)},
        compiler_params=pltpu.CompilerParams(has_side_effects=DATAFLOW))(*[hbm(s) for s in shards], *[hbm(a) for a in lands])
    return out[0], out[1], out[2:2 + n], out[2 + n:2 + 2 * n], out[-1]


def _gather_wait(send_sems, recv_sems, srcs, lands, after, name):
    n = len(srcs)

    def body(*refs):
        src_refs, land_refs = refs[:n], refs[n:2 * n]
        send_sems, recv_sems = refs[2 * n], refs[2 * n + 1]
        x, y, c = _coords()
        for i in range(n):
            half = src_refs[i].shape[0] // 2
            for k in (1, 2, 3):
                cp, (ox, oy) = _ici_piece(src_refs, land_refs, send_sems, recv_sems, i, k, x, y, c)
                cp.wait_send()
                piece = land_refs[i].at[2 * ox + oy, _half_rows(half, c)]
                _remote(piece, piece, send_sems.at[3 * i + k - 1], recv_sems.at[3 * i + k - 1], (ox, oy, c)).wait_recv()

    out = pl.pallas_call(
        body, name=name,
        out_shape=(*[pltpu.HBM(s.shape, s.dtype) for s in srcs], *[pltpu.HBM(a.shape, a.dtype) for a in lands]),
        in_specs=[HBM_SPEC] * (2 * n) + [SEM_SPEC, SEM_SPEC, pl.BlockSpec(memory_space=pl.ANY)],
        out_specs=tuple([HBM_SPEC] * (2 * n)), input_output_aliases={i: i for i in range(2 * n)},
        compiler_params=pltpu.CompilerParams(has_side_effects=DATAFLOW))(*srcs, *lands, send_sems, recv_sems, after)
    return out[n:]


def _gather_forward(lands, name):
    n = len(lands)

    def body(*refs):
        ins, outs = refs[:n], refs[n:2 * n]
        send_sems, recv_sems = refs[2 * n:]
        x, y, c = _coords()
        sends = []
        for i in range(n):
            half = ins[i].shape[1] // 2
            for k in (1, 2, 3):
                ox, oy = _other_chip(x, y, k)
                cp = _remote(ins[i].at[2 * ox + oy, _half_rows(half, c)], outs[i].at[2 * ox + oy, _half_rows(half, c)],
                             send_sems.at[3 * i + k - 1], recv_sems.at[3 * i + k - 1], (x, y, 1 - c))
                cp.start()
                sends.append(cp)
        for i in range(n):
            half = ins[i].shape[1] // 2
            for k in (1, 2, 3):
                ox, oy = _other_chip(x, y, k)
                piece = outs[i].at[2 * ox + oy, _half_rows(half, 1 - c)]
                _remote(piece, piece, send_sems.at[3 * i + k - 1], recv_sems.at[3 * i + k - 1], (x, y, 1 - c)).wait_recv()
        for cp in sends:
            cp.wait_send()

    return pl.pallas_call(
        body, name=name, in_specs=[HBM_SPEC] * n, out_specs=[HBM_SPEC] * n,
        out_shape=[jax.ShapeDtypeStruct(a.shape, a.dtype) for a in lands],
        input_output_aliases={i: i for i in range(n)},
        scratch_shapes=[pltpu.SemaphoreType.DMA((3 * n,))] * 2)(*lands)


def _rs_sibling_swap(g4s, name):
    n = len(g4s)

    def body(*refs):
        ins, outs = refs[:n], refs[n:2 * n]
        send_sems, recv_sems = refs[2 * n:]
        x, y, c = _coords()
        sends = []
        for i in range(n):
            half = ins[i].shape[1] // 2
            for s in range(4):
                cp = _remote(ins[i].at[s, _half_rows(half, 1 - c)], outs[i].at[s], send_sems.at[4 * i + s],
                             recv_sems.at[4 * i + s], (x, y, 1 - c))
                cp.start()
                sends.append(cp)
        for cp in sends:
            cp.wait_recv()
        for cp in sends:
            cp.wait_send()

    return pl.pallas_call(
        body, name=name, in_specs=[HBM_SPEC] * n, out_specs=[HBM_SPEC] * n,
        out_shape=[jax.ShapeDtypeStruct((4, g.shape[1] // 2, g.shape[2]), g.dtype) for g in g4s],
        scratch_shapes=[pltpu.SemaphoreType.DMA((4 * n,))] * 2)(*g4s)


def _rs_add_sibling(g4, got, c_idx, name):
    _, rows, cols = g4.shape
    half = rows // 2
    tr = min(half, 256)
    nb = half // tr

    def body(c_ref, a_ref, b_ref, o_ref, ob_ref):
        total = a_ref[...] + b_ref[...]
        o_ref[...] = total
        ob_ref[...] = total.astype(BF16)

    blk = (1, tr, cols)
    out = pl.BlockSpec(blk, lambda s, i, c_ref: (s, i, 0))
    grid_spec = pltpu.PrefetchScalarGridSpec(
        num_scalar_prefetch=1, grid=(4, nb),
        in_specs=[pl.BlockSpec(blk, lambda s, i, c_ref: (s, c_ref[0] * nb + i, 0)), out],
        out_specs=[out, out])
    return pl.pallas_call(
        body, name=name, grid_spec=grid_spec,
        out_shape=[jax.ShapeDtypeStruct(got.shape, F32), jax.ShapeDtypeStruct(got.shape, BF16)],
        compiler_params=_params("parallel", "parallel"))(c_idx, g4, got)


def _rs_add_chips(p4, got3, idx, name):
    _, half, cols = p4.shape
    tr = min(half, 256)
    nb = half // tr

    def body(idx_ref, p_ref, a_ref, b_ref, c_ref, o_ref):
        o_ref[...] = ((p_ref[0] + a_ref[0].astype(F32)) + b_ref[0].astype(F32)) + c_ref[0].astype(F32)

    blk = (1, tr, cols)
    grid_spec = pltpu.PrefetchScalarGridSpec(
        num_scalar_prefetch=1, grid=(nb,),
        in_specs=[pl.BlockSpec(blk, lambda i, idx_ref: (idx_ref[0], i, 0))]
        + [pl.BlockSpec(blk, functools.partial(lambda k, i, idx_ref: (k, i, 0), k)) for k in range(3)],
        out_specs=pl.BlockSpec((tr, cols), lambda i, idx_ref: (idx_ref[1] * nb + i, 0)))
    return pl.pallas_call(body, name=name, grid_spec=grid_spec, out_shape=jax.ShapeDtypeStruct((2 * half, cols), F32),
                          compiler_params=_params("parallel"))(idx, p4, got3, got3, got3)


def _rs_share_halves(bufs, name):
    n = len(bufs)

    def body(*refs):
        ins, outs = refs[:n], refs[n:2 * n]
        send_sems, recv_sems = refs[2 * n:]
        x, y, c = _coords()
        sends = []
        for i in range(n):
            half = ins[i].shape[0] // 2
            cp = _remote(ins[i].at[_half_rows(half, c)], outs[i].at[_half_rows(half, c)], send_sems.at[i],
                         recv_sems.at[i], (x, y, 1 - c))
            cp.start()
            sends.append(cp)
        for i in range(n):
            half = ins[i].shape[0] // 2
            _remote(ins[i].at[_half_rows(half, c)], outs[i].at[_half_rows(half, 1 - c)], send_sems.at[i],
                    recv_sems.at[i], (x, y, 1 - c)).wait_recv()
        for cp in sends:
            cp.wait_send()

    return pl.pallas_call(
        body, name=name, in_specs=[HBM_SPEC] * n, out_specs=[HBM_SPEC] * n,
        out_shape=[jax.ShapeDtypeStruct(b.shape, b.dtype) for b in bufs],
        input_output_aliases={i: i for i in range(n)},
        scratch_shapes=[pltpu.SemaphoreType.DMA((n,))] * 2)(*bufs)


def _scatter_piece(srcs, lands, send_sems, recv_sems, i, k, x, y, c):
    ox, oy = _other_chip(x, y, k)
    return _remote(srcs[i].at[2 * ox + oy], lands[i].at[k - 1], send_sems.at[3 * i + k - 1],
                   recv_sems.at[3 * i + k - 1], (ox, oy, c))


def _rs_scatter_start(p4s, name):
    n = len(p4s)

    def body(*refs):
        srcs, lands = refs[:n], refs[n:2 * n]
        send_sems, recv_sems = refs[2 * n], refs[2 * n + 1]
        token = refs[-1]
        x, y, c = _coords()
        for i in range(n):
            for k in (1, 2, 3):
                _scatter_piece(srcs, lands, send_sems, recv_sems, i, k, x, y, c).start()
        token[...] = jnp.zeros_like(token)

    hbm = lambda a: pltpu.with_memory_space_constraint(a, pltpu.HBM)
    lands = [lax.empty((3,) + p.shape[1:], p.dtype) for p in p4s]
    out = pl.pallas_call(
        body, name=name,
        out_shape=(pltpu.SemaphoreType.DMA((3 * n,)), pltpu.SemaphoreType.DMA((3 * n,)),
                   *[pltpu.HBM(p.shape, p.dtype) for p in p4s], *[pltpu.HBM(a.shape, a.dtype) for a in lands],
                   jax.ShapeDtypeStruct((SUBLANES, LANES), F32)),
        in_specs=[HBM_SPEC] * (2 * n), out_specs=(SEM_SPEC, SEM_SPEC, *[HBM_SPEC] * (2 * n), VMEM_SPEC),
        input_output_aliases={i: 2 + i for i in range(2 * n)},
        compiler_params=pltpu.CompilerParams(has_side_effects=DATAFLOW))(*[hbm(p) for p in p4s], *[hbm(a) for a in lands])
    return out[0], out[1], out[2:2 + n], out[2 + n:2 + 2 * n], out[-1]


def _rs_scatter_wait(send_sems, recv_sems, srcs, lands, after, name):
    n = len(srcs)

    def body(*refs):
        src_refs, land_refs = refs[:n], refs[n:2 * n]
        send_sems, recv_sems = refs[2 * n], refs[2 * n + 1]
        x, y, c = _coords()
        for i in range(n):
            for k in (1, 2, 3):
                cp = _scatter_piece(src_refs, land_refs, send_sems, recv_sems, i, k, x, y, c)
                cp.wait_send()
                cp.wait_recv()

    out = pl.pallas_call(
        body, name=name,
        out_shape=(*[pltpu.HBM(s.shape, s.dtype) for s in srcs], *[pltpu.HBM(a.shape, a.dtype) for a in lands]),
        in_specs=[HBM_SPEC] * (2 * n) + [SEM_SPEC, SEM_SPEC, pl.BlockSpec(memory_space=pl.ANY)],
        out_specs=tuple([HBM_SPEC] * (2 * n)), input_output_aliases={i: i for i in range(2 * n)},
        compiler_params=pltpu.CompilerParams(has_side_effects=DATAFLOW))(*srcs, *lands, send_sems, recv_sems, after)
    return out[n:]


def _rs_front(g4s, names, tag):
    c_idx = jnp.stack([lax.axis_index("c")]).astype(jnp.int32)
    got = _rs_sibling_swap(g4s, f"rs_sibling_swap_{tag}")
    return [_rs_add_sibling(g, s, c_idx, f"rs_add_sibling_{nm}") for g, s, nm in zip(g4s, got, names)]


def _rs_back(p4s, got3, names):
    x, y, c = _coords()
    idx = jnp.stack([2 * x + y, c]).astype(jnp.int32)
    return [_rs_add_chips(p, t, idx, f"rs_add_chips_{nm}") for (p, _), t, nm in zip(p4s, got3, names)]


def _cols_split(full):
    r, c4 = full.shape
    return full.reshape(r, 4, c4 // 4).transpose(1, 0, 2)


_BIG = {
    "w_in_even": ((1024, 2052), lambda s: s),
    "w_out_even": ((512, 1024), lambda s: s.reshape(2048, 1024)),
    "w_in_odd": ((1024, 2048), lambda s: s),
    "w_out_odd": ((512, 1024), lambda s: s.reshape(2048, 1024)),
    "w_pl": ((512, 256), lambda s: s.reshape(4, 2, 256, 256).transpose(1, 2, 0, 3).reshape(2, 256, 1024)),
    "w_pl_gate": ((512, 1024), lambda s: s.reshape(4, 2, 256, 1024).transpose(1, 0, 2, 3).reshape(2, 1024, 1024)),
}


_RS_EARLY = {
    "w_in_odd": lambda f: f,
    "w_out_odd": lambda f: f.reshape(4, 512, 1024),
    "w_pl_gate1": lambda f: f.reshape(4, 256, 1024),
    "w_pl1": _cols_split,
    "w_out_even": lambda f: f.reshape(4, 512, 1024),
    "w_pl_gate0": lambda f: f.reshape(4, 256, 1024),
    "w_pl0": _cols_split,
}
_RS_LATE = {"w_in_even": lambda f: f}


def _size(shape):
    n = 1
    for d in shape:
        n *= d
    return n


_SMALL = {"a_log": (1, 8), "dt_bias": (1, 8), "gdn_norm_g": (1, 128), "hgrn_norm_g": (1, 128),
          "lower_bounds": (2, 2048), "ln_g": (2, 1024), "ln_b": (2, 1024), "conv_a_w": (3, 1024), "conv_b_w": (4, 3072)}
_CONV_SHARD = {"conv_a_w": (3, 256), "conv_b_w": (4, 768)}


def _pack_small(parts, shapes, head_rows=0):
    rows = []
    for n, shape in shapes.items():
        v = parts[n].reshape(-1)
        rows.append(jnp.pad(v, (0, -v.shape[0] % LANES)).reshape(-1, LANES))
    buf = jnp.concatenate(rows, axis=0)
    return jnp.pad(buf, ((head_rows, -(buf.shape[0] + head_rows) % SUBLANES), (0, 0)))


def _unpack_small(buf, shapes, head_rows=0):
    out, off = {}, head_rows
    for n, shape in shapes.items():
        nrow = -(-_size(shape) // LANES)
        out[n] = buf[off:off + nrow].reshape(-1)[:_size(shape)].reshape(shape)
        off += nrow
    return out


_WEIGHTS = ["w_in_even", "conv_a_w", "conv_b_w", "a_log", "dt_bias", "gdn_norm_g", "w_out_even", "w_in_odd",
            "lower_bounds", "hgrn_norm_g", "w_out_odd", "ln_g", "ln_b", "w_pl", "w_pl_gate"]


def kernel(x, p, w_in_even, conv_a_w, conv_b_w, a_log, dt_bias, gdn_norm_g, w_out_even, w_in_odd, lower_bounds, hgrn_norm_g, w_out_odd, ln_g, ln_b, w_pl, w_pl_gate, loss_target, m_w_in_even, m_conv_a_w, m_conv_b_w, m_a_log, m_dt_bias, m_gdn_norm_g, m_w_out_even, m_w_in_odd, m_lower_bounds, m_hgrn_norm_g, m_w_out_odd, m_ln_g, m_ln_b, m_w_pl, m_w_pl_gate, v_w_in_even, v_conv_a_w, v_conv_b_w, v_a_log, v_dt_bias, v_gdn_norm_g, v_w_out_even, v_w_in_odd, v_lower_bounds, v_hgrn_norm_g, v_w_out_odd, v_ln_g, v_ln_b, v_w_pl, v_w_pl_gate):
    w = dict(zip(_WEIGHTS, (w_in_even, conv_a_w, conv_b_w, a_log, dt_bias, gdn_norm_g, w_out_even, w_in_odd,
                            lower_bounds, hgrn_norm_g, w_out_odd, ln_g, ln_b, w_pl, w_pl_gate)))
    m = dict(zip(_WEIGHTS, (m_w_in_even, m_conv_a_w, m_conv_b_w, m_a_log, m_dt_bias, m_gdn_norm_g, m_w_out_even,
                            m_w_in_odd, m_lower_bounds, m_hgrn_norm_g, m_w_out_odd, m_ln_g, m_ln_b, m_w_pl, m_w_pl_gate)))
    v = dict(zip(_WEIGHTS, (v_w_in_even, v_conv_a_w, v_conv_b_w, v_a_log, v_dt_bias, v_gdn_norm_g, v_w_out_even,
                            v_w_in_odd, v_lower_bounds, v_hgrn_norm_g, v_w_out_odd, v_ln_g, v_ln_b, v_w_pl, v_w_pl_gate)))
    chip = 2 * lax.axis_index("x") + lax.axis_index("y")

    names = list(_BIG)
    shard_shapes = {n: _BIG[n][0] for n in names}
    early, late = names[:1], names[1:]
    shards = {n: w[n].reshape(shard_shapes[n]).astype(BF16) for n in early}
    whole = lambda n, stacked: _BIG[n][1](lax.dynamic_update_slice(stacked, shards[n][None], (chip, 0, 0)))
    conv_mine = _pack_small({n: w[n] for n in _CONV_SHARD}, _CONV_SHARD)
    conv_all = _exchange_small(conv_mine, False, "gather_conv")
    shards, conv_all = lax.optimization_barrier((shards, conv_all))
    first = _gather_start([shards[n] for n in early], "gather_first_start")
    shards.update({n: (w[n].reshape(shard_shapes[n]) + first[4][0, 0]).astype(BF16) for n in late})
    send_sems, recv_sems, srcs, lands, token = _gather_start([shards[n] for n in late], "gather_rest_start")

    def late_weights(after):
        landed = _gather_forward(_gather_wait(send_sems, recv_sems, srcs, lands, after, "gather_rest_wait"),
                                 "gather_rest_forward")
        return {n: whole(n, ga) for n, ga in zip(late, landed)}

    landed = _gather_forward(_gather_wait(*first[:4], token, "gather_first_wait"), "gather_first_forward")
    full = {n: whole(n, ga) for n, ga in zip(early, landed)}
    conv_by_chip = [_unpack_small(conv_all[2 * s], _CONV_SHARD) for s in range(4)]
    for n in _CONV_SHARD:
        full[n] = jnp.concatenate([conv_by_chip[s][n] for s in range(4)], axis=1)
    for n in _SMALL:
        if n not in _CONV_SHARD:
            full[n] = w[n]

    early_rs = {}

    def early_grads_ready(grads):
        early_rs["p4s"] = _rs_front([_RS_EARLY[n](grads[n]) for n in _RS_EARLY], list(_RS_EARLY), "early")
        early_rs["sems"] = _rs_scatter_start([pb for _, pb in early_rs["p4s"]], "rs_scatter_early_start")
        return early_rs["sems"][4][0, 0]

    late_rs = {}

    def last_grad_ready(grad):
        late_rs["p4s"] = _rs_front([_RS_LATE[n](grad) for n in _RS_LATE], list(_RS_LATE), "late")
        late_rs["sems"] = _rs_scatter_start([pb for _, pb in late_rs["p4s"]], "rs_scatter_late_start")
        return late_rs["sems"][4][0, 0]

    loss_part, dx, g = _local_step(x[0], p[:, 0], loss_target[0], full, late_weights, early_grads_ready,
                                   last_grad_ready, token[0, 0])

    late_p4s, late_sems = late_rs["p4s"], late_rs["sems"]
    got3 = _rs_scatter_wait(*early_rs["sems"][:4], dx, "rs_scatter_early_wait")
    summed = dict(zip(_RS_EARLY, _rs_share_halves(_rs_back(early_rs["p4s"], got3, list(_RS_EARLY)), "rs_share_early")))
    g_big = {n: summed[n] for n in names if n in summed}
    g_big["w_pl"] = jnp.stack([summed["w_pl0"], summed["w_pl1"]])
    g_big["w_pl_gate"] = jnp.stack([summed["w_pl_gate0"], summed["w_pl_gate1"]])
    small_sum = _exchange_small(jnp.concatenate([loss_part, _pack_small(g, _SMALL)], axis=0), True, "reduce_small")
    loss = small_sum[0, 0]
    g_small = _unpack_small(small_sum, _SMALL, head_rows=SUBLANES)
    for n, (rows, cols) in _CONV_SHARD.items():
        g_small[n] = lax.dynamic_slice_in_dim(g_small[n], chip * cols, cols, axis=1)

    grads, delta, new_m, new_v = {}, {}, {}, {}
    for n in late:
        grads[n] = g_big[n].reshape(w[n].shape)
        delta[n], new_m[n], new_v[n] = _adamw(w[n], grads[n], m[n], v[n], f"adamw_{n}")
    own = {n: (_CONV_SHARD[n] if n in _CONV_SHARD else _SMALL[n]) for n in _SMALL}
    packs = [_pack_small({n: src[n] for n in _SMALL}, own)[None] for src in (w, g_small, m, v)]
    outs = [_unpack_small(t[0], own) for t in _adamw(*packs, "adamw_small")]
    for n in _SMALL:
        grads[n] = g_small[n].reshape(w[n].shape)
        delta[n], new_m[n], new_v[n] = (t[n].reshape(w[n].shape) for t in outs)
    got3 = _rs_scatter_wait(*late_sems[:4], new_v["w_in_odd"], "rs_scatter_late_wait")
    (g_in_even,) = _rs_share_halves(_rs_back(late_p4s, got3, list(_RS_LATE)), "rs_share_late")
    for n in early:
        grads[n] = g_in_even.reshape(w[n].shape)
        delta[n], new_m[n], new_v[n] = _adamw(w[n], grads[n], m[n], v[n], f"adamw_{n}")
    return (loss, dx[None], *[grads[n] for n in _WEIGHTS], *[delta[n] for n in _WEIGHTS],
            *[new_m[n] for n in _WEIGHTS], *[new_v[n] for n in _WEIGHTS])
```

```python
import functools

import jax
import jax.numpy as jnp
from jax import lax
from jax.experimental import pallas as pl
from jax.experimental.pallas import tpu as pltpu

F32 = jnp.float32
BF16 = jnp.bfloat16
HI = lax.Precision.HIGHEST

D_MODEL = 1024
PL_DIM = 256
GDN_HEADS = 8
HEAD_DIM = 128
GDN_CHUNK = 64
HGRN_HEADS = 16
HGRN_CHUNK = 32
HGRN_WIDTH = 2048
DEEPNORM_ALPHA = 4.0 ** 0.25
NORM_EPS = 1e-5
ADAM_LR, ADAM_B1, ADAM_B2, ADAM_EPS, ADAM_WD, ADAM_STEP = 0.001, 0.9, 0.999, 1e-08, 0.01, 10

VMEM_LIMIT = 56 * 1024 * 1024
SUBLANES = 8
LANES = 128


def _params(*sem):
    return pltpu.CompilerParams(dimension_semantics=sem, vmem_limit_bytes=VMEM_LIMIT)


ONE_PASS, THREE_PASS, FULL_F32, EXACT_LHS, EXACT_RHS = 0, 1, 2, 3, 4


def _split3(v):
    hi = v.astype(BF16)
    r1 = v - hi.astype(F32)
    mid = r1.astype(BF16)
    return hi, mid, (r1 - mid.astype(F32)).astype(BF16)


def _mm_raw(a, b, kind, prec):
    nb = a.ndim - 2
    ca = a.ndim - 1 if kind[0] == "n" else a.ndim - 2
    cb = b.ndim - 2 if kind[1] == "n" else b.ndim - 1
    dims = (((ca,), (cb,)), (tuple(range(nb)),) * 2)
    if prec == FULL_F32:
        return lax.dot_general(a, b, dims, precision=HI, preferred_element_type=F32)
    dot = lambda p, q: lax.dot_general(p, q, dims, preferred_element_type=F32)
    ah, bh = a.astype(BF16), b.astype(BF16)
    if prec == ONE_PASS:
        return dot(ah, bh)
    if prec == EXACT_LHS:
        b1, b2, b3 = _split3(b)
        return dot(ah, b1) + (dot(ah, b2) + dot(ah, b3))
    if prec == EXACT_RHS:
        a1, a2, a3 = _split3(a)
        return dot(a1, bh) + (dot(a2, bh) + dot(a3, bh))
    al = (a - ah.astype(F32)).astype(BF16)
    bl = (b - bh.astype(F32)).astype(BF16)
    return dot(ah, bh) + (dot(ah, bl) + dot(al, bh))


@functools.partial(jax.custom_vjp, nondiff_argnums=(2, 3))
def _mm_vjp(a, b, kind, hi):
    return _mm_raw(a, b, kind, hi)


def _mm_vjp_fwd(a, b, kind, hi):
    return _mm_raw(a, b, kind, hi), (a, b)


def _mm_vjp_bwd(kind, hi, res, dc):
    a, b = res
    if hi in (EXACT_LHS, EXACT_RHS):
        assert kind == "nn"
        if hi == EXACT_LHS:
            return jnp.zeros_like(a), _mm_raw(a, dc, "tn", EXACT_LHS)
        return _mm_raw(dc, b, "nt", EXACT_RHS), jnp.zeros_like(b)
    if kind == "nn":
        return _mm_raw(dc, b, "nt", hi), _mm_raw(a, dc, "tn", hi)
    if kind == "nt":
        return _mm_raw(dc, b, "nn", hi), _mm_raw(dc, a, "tn", hi)
    return _mm_raw(b, dc, "nt", hi), _mm_raw(a, dc, "nn", hi)


_mm_vjp.defvjp(_mm_vjp_fwd, _mm_vjp_bwd)


def _lane_total(v):
    return jnp.broadcast_to(jnp.sum(v, axis=-1, keepdims=True), v.shape)


def _matmul(a, b, *, name, ta=False, tb=False, add=None, add_scale=1.0, tm=1024, tn=2048, tk=1024):
    m, k = (a.shape[1], a.shape[0]) if ta else a.shape
    n = b.shape[0] if tb else b.shape[1]
    tm, tn, tk = min(tm, m), min(tn, n), min(tk, k)
    tn = tn if n % tn == 0 else tn // 2
    assert m % tm == 0 and n % tn == 0 and k % tk == 0, (name, m, n, k)
    nk = k // tk
    dims = (((0 if ta else 1,), (1 if tb else 0,)), ((), ()))

    def body(*refs):
        a_ref, b_ref = refs[:2]
        o_ref = refs[-1]
        part = lax.dot_general(a_ref[...].astype(BF16), b_ref[...].astype(BF16), dims, preferred_element_type=F32)
        first = (lambda: part) if add is None else (lambda: part + add_scale * refs[2][...])
        if nk == 1:
            o_ref[...] = first()
        else:
            kk = pl.program_id(2)

            @pl.when(kk == 0)
            def _():
                o_ref[...] = first()

            @pl.when(kk > 0)
            def _():
                o_ref[...] += part

    a_spec = pl.BlockSpec((tk, tm), lambda i, j, kk: (kk, i)) if ta else pl.BlockSpec((tm, tk), lambda i, j, kk: (i, kk))
    b_spec = pl.BlockSpec((tn, tk), lambda i, j, kk: (j, kk)) if tb else pl.BlockSpec((tk, tn), lambda i, j, kk: (kk, j))
    o_spec = pl.BlockSpec((tm, tn), lambda i, j, kk: (i, j))
    in_specs = [a_spec, b_spec] + ([o_spec] if add is not None else [])
    args = (a, b) + ((add,) if add is not None else ())
    return pl.pallas_call(
        body, name=name, grid=(m // tm, n // tn, nk), in_specs=in_specs, out_specs=o_spec,
        out_shape=jax.ShapeDtypeStruct((m, n), F32),
        compiler_params=_params("parallel", "parallel", "arbitrary"))(*args)


HALO = SUBLANES


def _halo_specs(tt, width, col, nt):
    r = tt // HALO
    prev = pl.BlockSpec((HALO, width), lambda i: (jnp.maximum(i * r - 1, 0), col))
    nxt = pl.BlockSpec((HALO, width), lambda i: (jnp.minimum((i + 1) * r, nt * r - 1), col))
    return prev, nxt


def _shift_down(ext, k):
    return ext if k == 0 else pltpu.roll(ext, k, 0)


def _shift_up(ext, k):
    return ext if k == 0 else pltpu.roll(ext, ext.shape[0] - k, 0)


def _causal_conv(ext, w, taps):
    acc = None
    for j in range(taps):
        term = w[j:j + 1, :] * _shift_down(ext, taps - 1 - j)
        acc = term if acc is None else acc + term
    return acc[HALO:, :]


def _conv_a_fwd(proj_a, conv_w):
    t = proj_a.shape[0]
    tt = min(t, 256)
    nt = t // tt
    wdt = 1024

    def body(cur_ref, prev_ref, w_ref, y_ref):
        i = pl.program_id(0)
        cur = cur_ref[...]
        h, c, b, z = (cur[:, k * wdt:(k + 1) * wdt] for k in range(4))
        prev = prev_ref[...]
        u_prev = jnp.where(i > 0, prev[:, wdt:2 * wdt] * prev[:, 0:wdt], 0.0)
        ext = jnp.concatenate([u_prev, c * h], axis=0)
        conv = _causal_conv(ext, w_ref[...], 3)
        y_ref[...] = (b * conv * jax.nn.silu(z)).astype(BF16)

    prev_spec, _ = _halo_specs(tt, 4 * wdt, 0, nt)
    return pl.pallas_call(
        body, name="conv_a_fwd", grid=(nt,),
        in_specs=[pl.BlockSpec((tt, 4 * wdt), lambda i: (i, 0)), prev_spec, pl.BlockSpec((3, wdt), lambda i: (0, 0))],
        out_specs=pl.BlockSpec((tt, wdt), lambda i: (i, 0)),
        out_shape=jax.ShapeDtypeStruct((t, wdt), BF16), compiler_params=_params("parallel"))(proj_a, proj_a, conv_w)


def _conv_a_bwd(proj_a, conv_w, dy):
    t = proj_a.shape[0]
    tt = min(t, 256)
    nt = t // tt
    wdt = 1024

    def body(cur_ref, prev_ref, nxt_ref, w_ref, dy_ref, dyn_ref, d_ref, dw_ref):
        i = pl.program_id(0)
        w = w_ref[...]
        cur, prev, nxt = cur_ref[...], prev_ref[...], nxt_ref[...]
        split = lambda a: tuple(a[:, k * wdt:(k + 1) * wdt] for k in range(4))
        h, c, b, z = split(cur)
        hp, cp, _, _ = split(prev)
        hn, cn, bn, zn = split(nxt)
        u_prev = jnp.where(i > 0, cp * hp, 0.0)
        u_ext = jnp.concatenate([u_prev, c * h, cn * hn], axis=0)
        taps = [_shift_down(u_ext, 2 - j)[HALO:, :] for j in range(3)]
        conv = w[0:1, :] * taps[0] + w[1:2, :] * taps[1] + w[2:3, :] * taps[2]
        b_cn = jnp.concatenate([b, bn], axis=0)
        z_cn = jnp.concatenate([z, zn], axis=0)
        dy_cn = jnp.concatenate([dy_ref[...], jnp.where(i < nt - 1, dyn_ref[...], 0.0)], axis=0)
        sg = jax.nn.sigmoid(z_cn)
        silu = z_cn * sg
        d_conv = dy_cn * b_cn * silu
        db = (dy_cn * conv * silu)[:tt, :]
        dz = (dy_cn * b_cn * conv * (sg * (1.0 + z_cn * (1.0 - sg))))[:tt, :]
        du = None
        for j in range(3):
            term = w[j:j + 1, :] * _shift_up(d_conv, 2 - j)
            du = term if du is None else du + term
        du = du[:tt, :]
        d_ref[...] = jnp.concatenate([du * c, du * h, db, dz], axis=1).astype(BF16)

        @pl.when(i == 0)
        def _():
            dw_ref[...] = jnp.zeros_like(dw_ref)

        d_cur = d_conv[:tt, :]
        rows = [jnp.sum(d_cur * taps[j][:tt, :], axis=0, keepdims=True) for j in range(3)]
        dw_ref[0:3, :] += jnp.concatenate(rows, axis=0)

    prev_spec, nxt_spec = _halo_specs(tt, 4 * wdt, 0, nt)
    _, dyn_spec = _halo_specs(tt, wdt, 0, nt)
    return pl.pallas_call(
        body, name="conv_a_bwd", grid=(nt,),
        in_specs=[pl.BlockSpec((tt, 4 * wdt), lambda i: (i, 0)), prev_spec, nxt_spec,
                  pl.BlockSpec((3, wdt), lambda i: (0, 0)), pl.BlockSpec((tt, wdt), lambda i: (i, 0)), dyn_spec],
        out_specs=[pl.BlockSpec((tt, 4 * wdt), lambda i: (i, 0)), pl.BlockSpec((SUBLANES, wdt), lambda i: (0, 0))],
        out_shape=[jax.ShapeDtypeStruct((t, 4 * wdt), BF16), jax.ShapeDtypeStruct((SUBLANES, wdt), F32)],
        compiler_params=_params("arbitrary"))(proj_a, proj_a, proj_a, conv_w, dy, dy)


def _conv_b_fwd(proj_qkv, conv_w):
    t, width = proj_qkv.shape
    tt = min(t, 256)
    nt = t // tt
    wdt = 1024

    def body(cur_ref, prev_ref, w_ref, y_ref):
        i = pl.program_id(1)
        ext = jnp.concatenate([jnp.where(i > 0, prev_ref[...], 0.0), cur_ref[...]], axis=0)
        y_ref[...] = jax.nn.silu(_causal_conv(ext, w_ref[...], 4))

    r = tt // HALO
    return pl.pallas_call(
        body, name="conv_b_fwd", grid=(width // wdt, nt),
        in_specs=[pl.BlockSpec((tt, wdt), lambda j, i: (i, j)),
                  pl.BlockSpec((HALO, wdt), lambda j, i: (jnp.maximum(i * r - 1, 0), j)),
                  pl.BlockSpec((4, wdt), lambda j, i: (0, j))],
        out_specs=pl.BlockSpec((tt, wdt), lambda j, i: (i, j)),
        out_shape=jax.ShapeDtypeStruct((t, width), F32), compiler_params=_params("parallel", "parallel"))(
            proj_qkv, proj_qkv, conv_w)


def _conv_b_bwd(proj_qkv, conv_w, d_act, col, name):
    t = proj_qkv.shape[0]
    tt = min(t, 256)
    nt = t // tt
    wdt = 1024

    def body(cur_ref, prev_ref, nxt_ref, w_ref, da_ref, dan_ref, d_ref, dw_ref):
        i = pl.program_id(0)
        w = w_ref[...]
        u_ext = jnp.concatenate([jnp.where(i > 0, prev_ref[...], 0.0), cur_ref[...], nxt_ref[...]], axis=0)
        taps = [_shift_down(u_ext, 3 - j)[HALO:, :] for j in range(4)]
        conv = w[0:1, :] * taps[0] + w[1:2, :] * taps[1] + w[2:3, :] * taps[2] + w[3:4, :] * taps[3]
        da_cn = jnp.concatenate([da_ref[...], jnp.where(i < nt - 1, dan_ref[...], 0.0)], axis=0)
        sg = jax.nn.sigmoid(conv)
        d_conv = da_cn * (sg * (1.0 + conv * (1.0 - sg)))
        du = None
        for j in range(4):
            term = w[j:j + 1, :] * _shift_up(d_conv, 3 - j)
            du = term if du is None else du + term
        d_ref[...] = du[:tt, :].astype(BF16)

        @pl.when(i == 0)
        def _():
            dw_ref[...] = jnp.zeros_like(dw_ref)

        d_cur = d_conv[:tt, :]
        rows = [jnp.sum(d_cur * taps[j][:tt, :], axis=0, keepdims=True) for j in range(4)]
        dw_ref[0:4, :] += jnp.concatenate(rows, axis=0)

    prev_spec, nxt_spec = _halo_specs(tt, wdt, col, nt)
    _, dan_spec = _halo_specs(tt, wdt, 0, nt)
    return pl.pallas_call(
        body, name=name, grid=(nt,),
        in_specs=[pl.BlockSpec((tt, wdt), lambda i: (i, col)), prev_spec, nxt_spec,
                  pl.BlockSpec((4, wdt), lambda i: (0, col)), pl.BlockSpec((tt, wdt), lambda i: (i, 0)), dan_spec],
        out_specs=[pl.BlockSpec((tt, wdt), lambda i: (i, 0)), pl.BlockSpec((SUBLANES, wdt), lambda i: (0, 0))],
        out_shape=[jax.ShapeDtypeStruct((t, wdt), BF16), jax.ShapeDtypeStruct((SUBLANES, wdt), F32)],
        compiler_params=_params("arbitrary"))(proj_qkv, proj_qkv, proj_qkv, conv_w, d_act, d_act)


def _rms_gate(o, gn, z):
    on = o * lax.rsqrt(jnp.mean(o * o, axis=-1, keepdims=True) + NORM_EPS) * gn
    return on * jax.nn.silu(z)


GDN_PREP_ROWS = 1024


def _unit_lower_inverse(low):
    c = low.shape[-1]
    eye = lax.broadcasted_iota(jnp.int32, low.shape, low.ndim - 2) == lax.broadcasted_iota(jnp.int32, low.shape, low.ndim - 1)
    x = -low
    inv = eye.astype(F32) + x
    for _ in range(c.bit_length() - 2):
        x = _mm_raw(x, x, "nn", THREE_PASS)
        inv = inv + _mm_raw(inv, x, "nn", THREE_PASS)
    return inv


@jax.custom_vjp
def _known_inverse(low, inv):
    return inv


def _known_inverse_fwd(low, inv):
    return inv, inv


def _known_inverse_bwd(inv, d_inv):
    return -_mm_raw(_mm_raw(inv, d_inv, "tn", THREE_PASS), inv, "nt", THREE_PASS), jnp.zeros_like(inv)


_known_inverse.defvjp(_known_inverse_fwd, _known_inverse_bwd)


def _gdn_prep(mm, qa, ka, va, braw, araw, alog, dtb, inv_kept=None):
    n, c, _ = qa.shape
    q = qa * lax.rsqrt(jnp.sum(qa * qa, axis=-1, keepdims=True) + 1e-6) * (HEAD_DIM ** -0.5)
    k = ka * lax.rsqrt(jnp.sum(ka * ka, axis=-1, keepdims=True) + 1e-6)
    beta = jax.nn.sigmoid(braw)
    g = -jnp.exp(alog) * jax.nn.softplus(araw + dtb)
    ri = lax.broadcasted_iota(jnp.int32, (n, c, c), 1)
    ci = lax.broadcasted_iota(jnp.int32, (n, c, c), 2)
    incl, strict, eye = ri >= ci, ri > ci, ri == ci
    gc = mm(incl.astype(F32), g, "nn", EXACT_LHS)
    gc_i = gc[:, :, :c]
    gc_j = mm(jnp.ones((n, c, c), F32), jnp.where(eye, gc_i, 0.0), "nn", EXACT_LHS)
    decay = jnp.where(incl, jnp.exp(jnp.where(incl, gc_i - gc_j, 0.0)), 0.0)
    kb = k * beta
    low = jnp.where(strict, mm(kb, k, "nt", ONE_PASS) * decay, 0.0)
    inv = _unit_lower_inverse(low) if inv_kept is None else _known_inverse(low, inv_kept)
    egc = jnp.exp(gc)
    u = mm(inv, va * beta, "nn", THREE_PASS)
    w = mm(inv, kb * egc, "nn", THREE_PASS)
    attn = jnp.where(incl, mm(q, k, "nt", ONE_PASS) * decay, 0.0)
    g_last = jnp.sum(g, axis=1, keepdims=True)
    outs = (u, w, q * egc, k * jnp.exp(g_last - gc), attn, jnp.exp(g_last))
    return outs + (inv,) if inv_kept is None else outs


def _gdn_scan(mm, u, w, qd, kd, attn, egl, z, gn, state):
    v_new = u - mm(w, state, "nn", ONE_PASS)
    o = mm(qd, state, "nn", ONE_PASS) + mm(attn, v_new, "nn", ONE_PASS)
    new_state = state * egl + mm(kd, v_new, "tn", ONE_PASS)
    return _rms_gate(o, gn, z), new_state


def _chunks(ref_value, n, c):
    return ref_value.reshape(n, c, ref_value.shape[-1])


def _by_head(ref, rows, heads):
    return jnp.stack([ref[rows, pl.ds(h * HEAD_DIM, HEAD_DIM)] for h in range(heads)])


def _store_heads(ref, rows, value):
    for h in range(value.shape[0]):
        ref[rows, pl.ds(h * HEAD_DIM, HEAD_DIM)] = value[h]


def _gdn_prep_specs(tb, nt_unused=None):
    col = lambda off: pl.BlockSpec((tb, HEAD_DIM), lambda h, i: (i, off + h))
    rep = pl.BlockSpec((1, tb, LANES), lambda h, i: (h, i, 0))
    par = pl.BlockSpec((1, SUBLANES, LANES), lambda h, i: (h, 0, 0))
    att = pl.BlockSpec((1, tb, GDN_CHUNK), lambda h, i: (h, i, 0))
    egl = pl.BlockSpec((1, tb // GDN_CHUNK, SUBLANES, LANES), lambda h, i: (h, i, 0, 0))
    return col, rep, par, att, egl


def _gdn_prep_fwd(qkv_act, braw, araw, alog, dtb):
    t = qkv_act.shape[0]
    tb = min(t, GDN_PREP_ROWS)
    nt, nc = t // tb, tb // GDN_CHUNK
    width = GDN_HEADS * HEAD_DIM

    def body(q_ref, k_ref, v_ref, br_ref, ar_ref, al_ref, dt_ref, u_ref, w_ref, qd_ref, kd_ref, at_ref, eg_ref, inv_ref):
        ch = lambda r: _chunks(r, nc, GDN_CHUNK)
        u, w, qd, kd, attn, egl, inv = _gdn_prep(_mm_raw, ch(q_ref[...]), ch(k_ref[...]), ch(v_ref[...]), ch(br_ref[0]),
                                                 ch(ar_ref[0]), al_ref[0, 0:1, :], dt_ref[0, 0:1, :])
        u_ref[...] = u.reshape(tb, HEAD_DIM)
        w_ref[...] = w.reshape(tb, HEAD_DIM).astype(BF16)
        qd_ref[...] = qd.reshape(tb, HEAD_DIM).astype(BF16)
        kd_ref[...] = kd.reshape(tb, HEAD_DIM).astype(BF16)
        at_ref[0] = attn.reshape(tb, GDN_CHUNK).astype(BF16)
        eg_ref[0] = jnp.broadcast_to(egl, (nc, SUBLANES, LANES))
        inv_ref[0] = inv.reshape(tb, GDN_CHUNK)

    col, rep, par, att, egl = _gdn_prep_specs(tb)
    h = GDN_HEADS
    return pl.pallas_call(
        body, name="gdn_prep_fwd", grid=(h, nt),
        in_specs=[col(0), col(h), col(2 * h), rep, rep, par, par],
        out_specs=[col(0), col(0), col(0), col(0), att, egl, att],
        out_shape=[jax.ShapeDtypeStruct((t, width), F32)] + [jax.ShapeDtypeStruct((t, width), BF16)] * 3
        + [jax.ShapeDtypeStruct((h, t, GDN_CHUNK), BF16), jax.ShapeDtypeStruct((h, t // GDN_CHUNK, SUBLANES, LANES), F32),
           jax.ShapeDtypeStruct((h, t, GDN_CHUNK), F32)],
        compiler_params=_params("parallel", "parallel"))(qkv_act, qkv_act, qkv_act, braw, araw, alog, dtb)


def _gdn_prep_bwd(qkv_act, braw, araw, alog, dtb, inv, du, dw, dqd, dkd, dattn, degl):
    t = qkv_act.shape[0]
    tb = min(t, GDN_PREP_ROWS)
    nt, nc = t // tb, tb // GDN_CHUNK
    width = GDN_HEADS * HEAD_DIM

    def body(q_ref, k_ref, v_ref, br_ref, ar_ref, al_ref, dt_ref, inv_ref, du_ref, dw_ref, dqd_ref, dkd_ref, dat_ref,
             deg_ref, dq_ref, dk_ref, dv_ref, dbr_ref, dar_ref, dal_ref, ddt_ref):
        @pl.when(pl.program_id(1) == 0)
        def _():
            dal_ref[...] = jnp.zeros_like(dal_ref)
            ddt_ref[...] = jnp.zeros_like(ddt_ref)

        ch = lambda r: _chunks(r, nc, GDN_CHUNK)
        _, vjp = jax.vjp(functools.partial(_gdn_prep, _mm_vjp, inv_kept=ch(inv_ref[0])), ch(q_ref[...]), ch(k_ref[...]),
                         ch(v_ref[...]), ch(br_ref[0]), ch(ar_ref[0]), al_ref[0, 0:1, :], dt_ref[0, 0:1, :])
        dq, dk, dv, dbr, dar, dal, ddt = vjp((ch(du_ref[...]), ch(dw_ref[...]), ch(dqd_ref[...]), ch(dkd_ref[...]),
                                              ch(dat_ref[0]), deg_ref[0][:, 0:1, :]))
        dq_ref[...] = dq.reshape(tb, HEAD_DIM)
        dk_ref[...] = dk.reshape(tb, HEAD_DIM)
        dv_ref[...] = dv.reshape(tb, HEAD_DIM)
        dbr_ref[0] = _lane_total(dbr.reshape(tb, LANES))
        dar_ref[0] = _lane_total(dar.reshape(tb, LANES))
        dal_ref[0, 0:1, :] += _lane_total(dal)
        ddt_ref[0, 0:1, :] += _lane_total(ddt)

    col, rep, par, att, egl = _gdn_prep_specs(tb)
    h = GDN_HEADS
    return pl.pallas_call(
        body, name="gdn_prep_bwd", grid=(h, nt),
        in_specs=[col(0), col(h), col(2 * h), rep, rep, par, par, att, col(0), col(0), col(0), col(0), att, egl],
        out_specs=[col(0), col(0), col(0), rep, rep, par, par],
        out_shape=[jax.ShapeDtypeStruct((t, width), F32)] * 3 + [jax.ShapeDtypeStruct((h, t, LANES), F32)] * 2
        + [jax.ShapeDtypeStruct((h, SUBLANES, LANES), F32)] * 2,
        compiler_params=_params("parallel", "arbitrary"))(qkv_act, qkv_act, qkv_act, braw, araw, alog, dtb, inv,
                                                         du, dw, dqd, dkd, dattn, degl)


def _scan_specs(tb, heads, chunk, rev, nt):
    ti = (lambda i: nt - 1 - i) if rev else (lambda i: i)
    row = pl.BlockSpec((tb, heads * HEAD_DIM), lambda i: (ti(i), 0))
    att = pl.BlockSpec((heads, tb, chunk), lambda i: (0, ti(i), 0))
    egl = pl.BlockSpec((heads, tb // chunk, SUBLANES, LANES), lambda i: (0, ti(i), 0, 0))
    hist = pl.BlockSpec((heads, tb // chunk, HEAD_DIM, HEAD_DIM), lambda i: (0, ti(i), 0, 0))
    gn = pl.BlockSpec((SUBLANES, LANES), lambda i: (0, 0))
    return row, att, egl, hist, gn


def _gdn_scan_fwd(u, w, qd, kd, attn, egl, zb, gn):
    t = u.shape[0]
    tb = min(t, 256)
    nt, nc = t // tb, tb // GDN_CHUNK
    nh = GDN_HEADS

    def body(u_ref, w_ref, qd_ref, kd_ref, at_ref, eg_ref, z_ref, gn_ref, y_ref, hist_ref, s_ref):
        @pl.when(pl.program_id(0) == 0)
        def _():
            s_ref[...] = jnp.zeros_like(s_ref)

        g = gn_ref[0:1, :]
        state = s_ref[...]
        for c in range(nc):
            rows = pl.ds(c * GDN_CHUNK, GDN_CHUNK)
            heads = lambda r: _by_head(r, rows, nh)
            hist_ref[:, c] = state
            y, state = _gdn_scan(_mm_raw, heads(u_ref), heads(w_ref), heads(qd_ref), heads(kd_ref), at_ref[:, rows, :],
                                 eg_ref[:, c, 0:1, :], heads(z_ref), g, state)
            _store_heads(y_ref, rows, y.astype(BF16))
        s_ref[...] = state

    row, att, egs, hist, gns = _scan_specs(tb, nh, GDN_CHUNK, False, nt)
    return pl.pallas_call(
        body, name="gdn_scan_fwd", grid=(nt,), in_specs=[row, row, row, row, att, egs, row, gns], out_specs=[row, hist],
        out_shape=[jax.ShapeDtypeStruct((t, nh * HEAD_DIM), BF16),
                   jax.ShapeDtypeStruct((nh, t // GDN_CHUNK, HEAD_DIM, HEAD_DIM), F32)],
        scratch_shapes=[pltpu.VMEM((nh, HEAD_DIM, HEAD_DIM), F32)],
        compiler_params=_params("arbitrary"))(u, w, qd, kd, attn, egl, zb, gn)


def _gdn_scan_bwd(u, w, qd, kd, attn, egl, zb, gn, hist, dy):
    t = u.shape[0]
    tb = min(t, 256)
    nt, nc = t // tb, tb // GDN_CHUNK
    nh = GDN_HEADS

    def body(u_ref, w_ref, qd_ref, kd_ref, at_ref, eg_ref, z_ref, gn_ref, hist_ref, dy_ref,
             du_ref, dw_ref, dqd_ref, dkd_ref, dat_ref, deg_ref, dz_ref, dgn_ref, ds_ref):
        @pl.when(pl.program_id(0) == 0)
        def _():
            ds_ref[...] = jnp.zeros_like(ds_ref)
            dgn_ref[...] = jnp.zeros_like(dgn_ref)

        g = gn_ref[0:1, :]
        d_state = ds_ref[...]
        for c in reversed(range(nc)):
            rows = pl.ds(c * GDN_CHUNK, GDN_CHUNK)
            heads = lambda r: _by_head(r, rows, nh).astype(F32)
            _, vjp = jax.vjp(functools.partial(_gdn_scan, _mm_vjp), heads(u_ref), heads(w_ref), heads(qd_ref),
                             heads(kd_ref), at_ref[:, rows, :].astype(F32), eg_ref[:, c, 0:1, :], heads(z_ref), g,
                             hist_ref[:, c])
            du, dw, dqd, dkd, dat, deg, dz, dgn, d_state = vjp((heads(dy_ref), d_state))
            _store_heads(du_ref, rows, du)
            _store_heads(dw_ref, rows, dw)
            _store_heads(dqd_ref, rows, dqd)
            _store_heads(dkd_ref, rows, dkd)
            dat_ref[:, rows, :] = dat
            deg_ref[:, c] = jnp.broadcast_to(deg, (nh, SUBLANES, LANES))
            _store_heads(dz_ref, rows, dz.astype(BF16))
            dgn_ref[0:1, :] += dgn
        ds_ref[...] = d_state

    row, att, egs, hists, gns = _scan_specs(tb, nh, GDN_CHUNK, True, nt)
    wide = jax.ShapeDtypeStruct((t, nh * HEAD_DIM), F32)
    return pl.pallas_call(
        body, name="gdn_scan_bwd", grid=(nt,),
        in_specs=[row, row, row, row, att, egs, row, gns, hists, row],
        out_specs=[row, row, row, row, att, egs, row, gns],
        out_shape=[wide] * 4 + [jax.ShapeDtypeStruct((nh, t, GDN_CHUNK), F32),
                                jax.ShapeDtypeStruct((nh, t // GDN_CHUNK, SUBLANES, LANES), F32),
                                jax.ShapeDtypeStruct((t, nh * HEAD_DIM), BF16),
                                jax.ShapeDtypeStruct((SUBLANES, LANES), F32)],
        scratch_shapes=[pltpu.VMEM((nh, HEAD_DIM, HEAD_DIM), F32)],
        compiler_params=_params("arbitrary"))(u, w, qd, kd, attn, egl, zb, gn, hist, dy)


def _hgrn_prep(mm, qr, fr, lbl):
    n, c, _ = qr.shape
    lb = jax.nn.sigmoid(lbl[1:2, :] - lbl[0:1, :])
    f = lb + (1.0 - lb) * jax.nn.sigmoid(fr)
    q = jax.nn.silu(qr)
    k = 1.0 - f
    logf = jnp.log(f)
    ri = lax.broadcasted_iota(jnp.int32, (n, c, c), 1)
    ci = lax.broadcasted_iota(jnp.int32, (n, c, c), 2)
    b = mm((ri >= ci).astype(F32), logf, "nn", EXACT_LHS)
    attn = _hgrn_attn(mm, q, k, b)
    b_last = jnp.sum(logf, axis=1, keepdims=True)
    return q * jnp.exp(b), k * jnp.exp(b_last - b), attn, jnp.exp(b_last)


HGRN_SUB = 8
HGRN_PREP_ROWS = 1024


@functools.partial(jax.custom_vjp, nondiff_argnums=(1,))
def _roll_rows(x, shift):
    return pltpu.roll(x, shift, x.ndim - 2)


def _roll_rows_fwd(x, shift):
    return _roll_rows(x, shift), None


def _roll_rows_bwd(shift, _, d):
    return (pltpu.roll(d, d.shape[-2] - shift, d.ndim - 2),)


_roll_rows.defvjp(_roll_rows_fwd, _roll_rows_bwd)


@jax.custom_vjp
def _exp_clamped(v):
    return jnp.exp(jnp.minimum(v, 0.0))


def _exp_clamped_fwd(v):
    out = jnp.exp(jnp.minimum(v, 0.0))
    return out, out


def _exp_clamped_bwd(out, d):
    return (d * out,)


_exp_clamped.defvjp(_exp_clamped_fwd, _exp_clamped_bwd)


def _hgrn_attn(mm, q, k, b):
    n, c, d = q.shape
    sb = HGRN_SUB
    sub = lambda a: a.reshape(n * c // sb, sb, d)
    qs, ks, bs = sub(q), sub(k), sub(b)
    row = lax.broadcasted_iota(jnp.int32, (n, c, c), 1)
    col = lax.broadcasted_iota(jnp.int32, (n, c, c), 2)
    same_block = (row & -sb) == (col & -sb)
    attn = None
    for delta in range(sb):
        if delta == 0:
            prod = qs * ks
        else:
            prod = qs * _roll_rows(ks, delta) * _exp_clamped(bs - _roll_rows(bs, delta))
        sums = jnp.sum(prod, axis=-1, keepdims=True).reshape(n, c, 1)
        term = jnp.where(same_block & (row - col == delta), sums, 0.0)
        attn = term if attn is None else attn + term
    far = [jnp.zeros((n, sb, c), F32)]
    col8 = lax.broadcasted_iota(jnp.int32, (n, sb, c), 2)
    for i in range(1, c // sb):
        r0 = i * sb
        bi = b[:, r0:r0 + sb, :]
        ref = bi[:, 0:1, :]
        part = mm(q[:, r0:r0 + sb, :] * jnp.exp(bi - ref), k * _exp_clamped(ref - b), "nt", ONE_PASS)
        far.append(jnp.where(col8 < r0, part, 0.0))
    return attn + jnp.concatenate(far, axis=1)


def _hgrn_scan(mm, qe, kd, attn, ebl, iv, z, gn, state):
    o = mm(qe, state, "nt", ONE_PASS) + mm(attn, iv, "nn", ONE_PASS)
    new_state = state * ebl + mm(iv, kd, "tn", ONE_PASS)
    return _rms_gate(o, gn, z), new_state


def _hgrn_prep_specs(tb):
    col = pl.BlockSpec((tb, HEAD_DIM), lambda h, i: (i, h))
    lbs = pl.BlockSpec((2, HEAD_DIM), lambda h, i: (0, h))
    att = pl.BlockSpec((1, tb, HGRN_CHUNK), lambda h, i: (h, i, 0))
    ebl = pl.BlockSpec((1, tb // HGRN_CHUNK, SUBLANES, LANES), lambda h, i: (h, i, 0, 0))
    return col, lbs, att, ebl


def _hgrn_prep_fwd(qr, fr, lower_bounds):
    t = qr.shape[0]
    tb = min(t, HGRN_PREP_ROWS)
    nt, nc = t // tb, tb // HGRN_CHUNK
    hh = HGRN_HEADS

    def body(q_ref, f_ref, lb_ref, qe_ref, kd_ref, at_ref, eb_ref):
        ch = lambda r: _chunks(r, nc, HGRN_CHUNK)
        qe, kd, attn, ebl = _hgrn_prep(_mm_raw, ch(q_ref[...]), ch(f_ref[...]), lb_ref[...])
        qe_ref[...] = qe.reshape(tb, HEAD_DIM).astype(BF16)
        kd_ref[...] = kd.reshape(tb, HEAD_DIM).astype(BF16)
        at_ref[0] = attn.reshape(tb, HGRN_CHUNK).astype(BF16)
        eb_ref[0] = jnp.broadcast_to(ebl, (nc, SUBLANES, LANES))

    col, lbs, att, ebs = _hgrn_prep_specs(tb)
    return pl.pallas_call(
        body, name="hgrn_prep_fwd", grid=(hh, nt), in_specs=[col, col, lbs], out_specs=[col, col, att, ebs],
        out_shape=[jax.ShapeDtypeStruct((t, HGRN_WIDTH), BF16)] * 2
        + [jax.ShapeDtypeStruct((hh, t, HGRN_CHUNK), BF16), jax.ShapeDtypeStruct((hh, t // HGRN_CHUNK, SUBLANES, LANES), F32)],
        compiler_params=_params("parallel", "parallel"))(qr, fr, lower_bounds)


def _hgrn_prep_bwd(qr, fr, lower_bounds, dqe, dkd, dattn, debl):
    t = qr.shape[0]
    tb = min(t, HGRN_PREP_ROWS)
    nt, nc = t // tb, tb // HGRN_CHUNK
    hh = HGRN_HEADS

    def body(q_ref, f_ref, lb_ref, dqe_ref, dkd_ref, dat_ref, deb_ref, dq_ref, df_ref, dlb_ref):
        @pl.when(pl.program_id(1) == 0)
        def _():
            dlb_ref[...] = jnp.zeros_like(dlb_ref)

        ch = lambda r: _chunks(r, nc, HGRN_CHUNK)
        _, vjp = jax.vjp(functools.partial(_hgrn_prep, _mm_vjp), ch(q_ref[...]), ch(f_ref[...]), lb_ref[...])
        dq, df, dlb = vjp((ch(dqe_ref[...]), ch(dkd_ref[...]), ch(dat_ref[0]), deb_ref[0][:, 0:1, :]))
        dq_ref[...] = dq.reshape(tb, HEAD_DIM).astype(BF16)
        df_ref[...] = df.reshape(tb, HEAD_DIM).astype(BF16)
        dlb_ref[...] += dlb

    col, lbs, att, ebs = _hgrn_prep_specs(tb)
    return pl.pallas_call(
        body, name="hgrn_prep_bwd", grid=(hh, nt), in_specs=[col, col, lbs, col, col, att, ebs],
        out_specs=[col, col, lbs],
        out_shape=[jax.ShapeDtypeStruct((t, HGRN_WIDTH), BF16)] * 2 + [jax.ShapeDtypeStruct((2, HGRN_WIDTH), F32)],
        compiler_params=_params("parallel", "arbitrary"))(qr, fr, lower_bounds, dqe, dkd, dattn, debl)


def _hgrn_scan_fwd(qe, kd, attn, ebl, iv, z, gn):
    t = qe.shape[0]
    tb = min(t, 128)
    nt, nc = t // tb, tb // HGRN_CHUNK
    hh = HGRN_HEADS

    def body(qe_ref, kd_ref, at_ref, eb_ref, i_ref, z_ref, gn_ref, y_ref, hist_ref, s_ref):
        @pl.when(pl.program_id(0) == 0)
        def _():
            s_ref[...] = jnp.zeros_like(s_ref)

        g = gn_ref[0:1, :]
        state = s_ref[...]
        for c in range(nc):
            rows = pl.ds(c * HGRN_CHUNK, HGRN_CHUNK)
            heads = lambda r: _by_head(r, rows, hh)
            hist_ref[:, c] = state
            y, state = _hgrn_scan(_mm_raw, heads(qe_ref), heads(kd_ref), at_ref[:, rows, :], eb_ref[:, c, 0:1, :],
                                  heads(i_ref), heads(z_ref), g, state)
            _store_heads(y_ref, rows, y.astype(BF16))
        s_ref[...] = state

    row, att, ebs, hist, gns = _scan_specs(tb, hh, HGRN_CHUNK, False, nt)
    return pl.pallas_call(
        body, name="hgrn_scan_fwd", grid=(nt,), in_specs=[row, row, att, ebs, row, row, gns], out_specs=[row, hist],
        out_shape=[jax.ShapeDtypeStruct((t, HGRN_WIDTH), BF16),
                   jax.ShapeDtypeStruct((hh, t // HGRN_CHUNK, HEAD_DIM, HEAD_DIM), F32)],
        scratch_shapes=[pltpu.VMEM((hh, HEAD_DIM, HEAD_DIM), F32)],
        compiler_params=_params("arbitrary"))(qe, kd, attn, ebl, iv, z, gn)


def _hgrn_scan_bwd(qe, kd, attn, ebl, iv, z, gn, hist, dy):
    t = qe.shape[0]
    tb = min(t, 128)
    nt, nc = t // tb, tb // HGRN_CHUNK
    hh = HGRN_HEADS

    def body(qe_ref, kd_ref, at_ref, eb_ref, i_ref, z_ref, gn_ref, hist_ref, dy_ref,
             dqe_ref, dkd_ref, dat_ref, deb_ref, di_ref, dz_ref, dgn_ref, ds_ref):
        @pl.when(pl.program_id(0) == 0)
        def _():
            ds_ref[...] = jnp.zeros_like(ds_ref)
            dgn_ref[...] = jnp.zeros_like(dgn_ref)

        g = gn_ref[0:1, :]
        d_state = ds_ref[...]
        for c in reversed(range(nc)):
            rows = pl.ds(c * HGRN_CHUNK, HGRN_CHUNK)
            heads = lambda r: _by_head(r, rows, hh).astype(F32)
            _, vjp = jax.vjp(functools.partial(_hgrn_scan, _mm_vjp), heads(qe_ref), heads(kd_ref),
                             at_ref[:, rows, :].astype(F32), eb_ref[:, c, 0:1, :], heads(i_ref), heads(z_ref), g,
                             hist_ref[:, c])
            dqe, dkd, dat, deb, di, dz, dgn, d_state = vjp((heads(dy_ref), d_state))
            _store_heads(dqe_ref, rows, dqe)
            _store_heads(dkd_ref, rows, dkd)
            dat_ref[:, rows, :] = dat
            deb_ref[:, c] = jnp.broadcast_to(deb, (hh, SUBLANES, LANES))
            _store_heads(di_ref, rows, di.astype(BF16))
            _store_heads(dz_ref, rows, dz.astype(BF16))
            dgn_ref[0:1, :] += dgn
        ds_ref[...] = d_state

    row, att, ebs, hists, gns = _scan_specs(tb, hh, HGRN_CHUNK, True, nt)
    wide = lambda dt: jax.ShapeDtypeStruct((t, HGRN_WIDTH), dt)
    return pl.pallas_call(
        body, name="hgrn_scan_bwd", grid=(nt,),
        in_specs=[row, row, att, ebs, row, row, gns, hists, row],
        out_specs=[row, row, att, ebs, row, row, gns],
        out_shape=[wide(F32), wide(F32), jax.ShapeDtypeStruct((hh, t, HGRN_CHUNK), F32),
                   jax.ShapeDtypeStruct((hh, t // HGRN_CHUNK, SUBLANES, LANES), F32), wide(BF16), wide(BF16),
                   jax.ShapeDtypeStruct((SUBLANES, LANES), F32)],
        scratch_shapes=[pltpu.VMEM((hh, HEAD_DIM, HEAD_DIM), F32)],
        compiler_params=_params("arbitrary"))(qe, kd, attn, ebl, iv, z, gn, hist, dy)


def _layer_norm(pre, g, b):
    mu = jnp.mean(pre, axis=-1, keepdims=True)
    d = pre - mu
    var = jnp.mean(d * d, axis=-1, keepdims=True)
    return d * lax.rsqrt(var + NORM_EPS) * g + b


def _lnpl_fwd(xin, s, p, wg, wpl, ln_g, ln_b):
    t = xin.shape[0]
    tt = min(t, 256)

    def body(x_ref, s_ref, p_ref, wg_ref, wpl_ref, g_ref, b_ref, o_ref, ob_ref):
        xn = _layer_norm(DEEPNORM_ALPHA * x_ref[...] + s_ref[...], g_ref[...], b_ref[...])
        gate = jax.nn.sigmoid(_mm_raw(xn, wg_ref[...], "nn", False))
        out = xn + _mm_raw(p_ref[...], wpl_ref[...], "nn", False) * gate
        o_ref[...] = out
        ob_ref[...] = out.astype(BF16)

    row = lambda w: pl.BlockSpec((tt, w), lambda i: (i, 0))
    full = lambda a: pl.BlockSpec(a.shape, lambda i: (0, 0))
    return pl.pallas_call(
        body, name="lnpl_fwd", grid=(t // tt,),
        in_specs=[row(D_MODEL), row(D_MODEL), row(PL_DIM), full(wg), full(wpl), full(ln_g), full(ln_b)],
        out_specs=[row(D_MODEL), row(D_MODEL)],
        out_shape=[jax.ShapeDtypeStruct((t, D_MODEL), F32), jax.ShapeDtypeStruct((t, D_MODEL), BF16)],
        compiler_params=_params("parallel"))(xin, s, p, wg, wpl, ln_g, ln_b)


def _lnpl_bwd(xin, s, p, wg, wpl, ln_g, ln_b, upstream, last, name):
    t = xin.shape[0]
    tt = min(t, 256)

    def body(x_ref, s_ref, p_ref, wg_ref, wpl_ref, g_ref, b_ref, up_ref,
             dpre_ref, dwg_ref, dwpl_ref, dg_ref, db_ref, loss_ref):
        @pl.when(pl.program_id(0) == 0)
        def _():
            for r in (dwg_ref, dwpl_ref, dg_ref, db_ref, loss_ref):
                r[...] = jnp.zeros_like(r)

        pre = DEEPNORM_ALPHA * x_ref[...] + s_ref[...]
        xn, ln_vjp = jax.vjp(_layer_norm, pre, g_ref[...], b_ref[...])
        gate = jax.nn.sigmoid(_mm_raw(xn, wg_ref[...], "nn", False))
        plv = _mm_raw(p_ref[...], wpl_ref[...], "nn", False)
        if last:
            err = xn + plv * gate - up_ref[...]
            dout = err * (1.0 / D_MODEL)
            tot = jnp.sum(jnp.sum(err * err, axis=1, keepdims=True), axis=0, keepdims=True) * (0.5 / D_MODEL)
            loss_ref[...] += jnp.broadcast_to(tot, loss_ref.shape)
        else:
            dout = up_ref[...]
        dplv = dout * gate
        dlogits = dout * plv * gate * (1.0 - gate)
        dwg_ref[...] += _mm_raw(xn, dlogits, "tn", False)
        dwpl_ref[...] += _mm_raw(p_ref[...], dplv, "tn", False)
        dxn = dout + _mm_raw(dlogits, wg_ref[...], "nt", False)
        dpre, dg, db = ln_vjp(dxn)
        dpre_ref[...] = dpre
        dg_ref[...] += dg
        db_ref[...] += db

    row = lambda w: pl.BlockSpec((tt, w), lambda i: (i, 0))
    full = lambda shape: pl.BlockSpec(shape, lambda i: (0, 0))
    return pl.pallas_call(
        body, name=name, grid=(t // tt,),
        in_specs=[row(D_MODEL), row(D_MODEL), row(PL_DIM), full(wg.shape), full(wpl.shape), full(ln_g.shape),
                  full(ln_b.shape), row(D_MODEL)],
        out_specs=[row(D_MODEL), full(wg.shape), full(wpl.shape), full(ln_g.shape), full(ln_b.shape),
                   full((SUBLANES, LANES))],
        out_shape=[jax.ShapeDtypeStruct((t, D_MODEL), F32), jax.ShapeDtypeStruct(wg.shape, F32),
                   jax.ShapeDtypeStruct(wpl.shape, F32), jax.ShapeDtypeStruct(ln_g.shape, F32),
                   jax.ShapeDtypeStruct(ln_b.shape, F32), jax.ShapeDtypeStruct((SUBLANES, LANES), F32)],
        compiler_params=_params("arbitrary"))(xin, s, p, wg, wpl, ln_g, ln_b, upstream)


def _pack_tail(dbr, dar):
    nh, t, _ = dbr.shape
    tt = min(t, 512)

    def body(b_ref, a_ref, o_ref):
        lane = lax.broadcasted_iota(jnp.int32, (tt, LANES), 1)
        acc = jnp.zeros((tt, LANES), F32)
        for h in range(nh):
            acc = jnp.where(lane == h, b_ref[h], acc)
            acc = jnp.where(lane == nh + h, a_ref[h], acc)
        o_ref[...] = acc.astype(BF16)

    spec = pl.BlockSpec((nh, tt, LANES), lambda i: (0, i, 0))
    return pl.pallas_call(
        body, name="pack_tail", grid=(t // tt,), in_specs=[spec, spec], out_specs=pl.BlockSpec((tt, LANES), lambda i: (i, 0)),
        out_shape=jax.ShapeDtypeStruct((t, LANES), BF16), compiler_params=_params("parallel"))(dbr, dar)


def _rep_rows(v):
    return jnp.broadcast_to(v.reshape(1, LANES), (SUBLANES, LANES))


def _rep_heads(v):
    return jnp.broadcast_to(v.reshape(-1, 1, 1), (v.shape[0], SUBLANES, LANES))


def _col_range(stacked, lo, hi):
    c = stacked.shape[2]
    parts = [stacked[s, :, max(lo, s * c) - s * c:min(hi, (s + 1) * c) - s * c]
             for s in range(4) if max(lo, s * c) < min(hi, (s + 1) * c)]
    return parts[0] if len(parts) == 1 else jnp.concatenate(parts, axis=1)


def _col_shards(pieces, c):
    shards, offs, o = [], [], 0
    for pc in pieces:
        offs.append(o)
        o += pc.shape[1]
    for s in range(4):
        lo, hi = s * c, (s + 1) * c
        parts = [pc[:, max(lo, o) - o:min(hi, o + pc.shape[1]) - o] for pc, o in zip(pieces, offs)
                 if max(lo, o) < min(hi, o + pc.shape[1])]
        shards.append(parts[0] if len(parts) == 1 else jnp.concatenate(parts, axis=1))
    return jnp.stack(shards)


def _local_step(x, p, target, w, late_weights, early_grads_ready, last_grad_ready, start_token):
    a = DEEPNORM_ALPHA
    nh = GDN_HEADS
    xb = (x + start_token).astype(BF16)
    wie = w["w_in_even"]
    w_a, w_qkv, w_zb = _col_range(wie, 0, 4096), _col_range(wie, 4096, 7168), _col_range(wie, 7168, 8192)
    w_tail = jnp.pad(_col_range(wie, 8192, 8192 + 2 * nh), ((0, 0), (0, LANES - 2 * nh)))
    conv_a_w, conv_b_w = w["conv_a_w"], w["conv_b_w"]
    ln_g0, ln_b0, ln_g1, ln_b1 = (v.reshape(1, D_MODEL) for v in (w["ln_g"][0], w["ln_b"][0], w["ln_g"][1], w["ln_b"][1]))
    alog, dtb = _rep_heads(w["a_log"].reshape(nh)), _rep_heads(w["dt_bias"].reshape(nh))
    gdn_g, hgrn_g = _rep_rows(w["gdn_norm_g"]), _rep_rows(w["hgrn_norm_g"])

    proj_a = _matmul(xb, w_a, name="fwd_proj_a")
    proj_qkv = _matmul(xb, w_qkv, name="fwd_proj_qkv")
    proj_zb = _matmul(xb, w_zb, name="fwd_proj_zb")
    proj_tail = _matmul(xb, w_tail, name="fwd_proj_tail")
    rep = lambda cols: jnp.broadcast_to(cols.T[:, :, None], (nh, cols.shape[0], LANES))
    braw, araw = rep(proj_tail[:, :nh]), rep(proj_tail[:, nh:2 * nh])
    y_a = _conv_a_fwd(proj_a, conv_a_w)
    qkv_act = _conv_b_fwd(proj_qkv, conv_b_w)
    *gdn_pre, gdn_inv = _gdn_prep_fwd(qkv_act, braw, araw, alog, dtb)
    y_b, gdn_hist = _gdn_scan_fwd(*gdn_pre, proj_zb, gdn_g)
    w = {**w, **late_weights(y_b)}
    woe, wio, woo = w["w_out_even"], w["w_in_odd"], w["w_out_odd"]
    s0 = _matmul(y_b, woe[1024:], name="fwd_out_even_b", add=_matmul(y_a, woe[:1024], name="fwd_out_even_a"))
    x1, x1b = _lnpl_fwd(x, s0, p[0], w["w_pl_gate"][0], w["w_pl"][0], ln_g0, ln_b0)
    proj_o = [_matmul(x1b, wio[j], name=f"fwd_proj_odd{j}") for j in range(4)]
    hgrn_pre = _hgrn_prep_fwd(proj_o[0], proj_o[1], w["lower_bounds"])
    y_o, hgrn_hist = _hgrn_scan_fwd(*hgrn_pre, proj_o[2], proj_o[3], hgrn_g)
    s1 = _matmul(y_o, woo, name="fwd_out_odd")

    g = {}
    dpre1, dwg1, dwpl1, dlng1, dlnb1, loss = _lnpl_bwd(x1, s1, p[1], w["w_pl_gate"][1], w["w_pl"][1], ln_g1, ln_b1,
                                                     target, True, "lnpl_bwd_odd")
    dy_o = _matmul(dpre1, woo, tb=True, name="bwd_out_odd_dx")
    g["w_out_odd"] = _matmul(y_o, dpre1, ta=True, name="bwd_out_odd_dw")
    dqe, dkd, dat, deb, di, dz, dhg = _hgrn_scan_bwd(*hgrn_pre, proj_o[2], proj_o[3], hgrn_g, hgrn_hist, dy_o)
    dq, df, dlb = _hgrn_prep_bwd(proj_o[0], proj_o[1], w["lower_bounds"], dqe, dkd, dat, deb)
    dx1 = dpre1
    scale = a
    dws = []
    for j, dj in enumerate((dq, df, di, dz)):
        dx1 = _matmul(dj, wio[j], tb=True, add=dx1, add_scale=scale, name=f"bwd_proj_odd_dx{j}")
        scale = 1.0
        dws.append(_matmul(x1b, dj, ta=True, name=f"bwd_proj_odd_dw{j}"))
    g["w_in_odd"] = jnp.stack(dws)
    g["hgrn_norm_g"] = dhg[0:1]
    g["lower_bounds"] = dlb
    g["w_pl_gate1"], g["w_pl1"] = dwg1, dwpl1

    dpre0, dwg0, dwpl0, dlng0, dlnb0, _ = _lnpl_bwd(x, s0, p[0], w["w_pl_gate"][0], w["w_pl"][0], ln_g0, ln_b0,
                                                  dx1, False, "lnpl_bwd_even")
    g["w_pl_gate0"], g["w_pl0"] = dwg0, dwpl0
    g["ln_g"] = jnp.concatenate([dlng0, dlng1], axis=0)
    g["ln_b"] = jnp.concatenate([dlnb0, dlnb1], axis=0)
    dy_a = _matmul(dpre0, woe[:1024], tb=True, name="bwd_out_even_dxa")
    dy_b = _matmul(dpre0, woe[1024:], tb=True, name="bwd_out_even_dxb")
    g["w_out_even"] = jnp.concatenate([_matmul(y_a, dpre0, ta=True, name="bwd_out_even_dwa"),
                                       _matmul(y_b, dpre0, ta=True, name="bwd_out_even_dwb")], axis=0)
    token = early_grads_ready({n: g[n] for n in _RS_EARLY})
    conv_a_w, gdn_g = conv_a_w + token, gdn_g + token
    du, dw, dqd, dkd, dat, deg, dzb, dgn = _gdn_scan_bwd(*gdn_pre, proj_zb, gdn_g, gdn_hist, dy_b)
    dqa, dka, dva, dbr, dar, dal, ddt = _gdn_prep_bwd(qkv_act, braw, araw, alog, dtb, gdn_inv, du, dw, dqd, dkd, dat, deg)
    g["a_log"] = dal[:, 0, 0].reshape(1, nh)
    g["dt_bias"] = ddt[:, 0, 0].reshape(1, nh)
    g["gdn_norm_g"] = dgn[0:1]
    d_pre_qkv, dwb = [], []
    for j, dj in enumerate((dqa, dka, dva)):
        dpj, dwj = _conv_b_bwd(proj_qkv, conv_b_w, dj, j, f"conv_b_bwd{j}")
        d_pre_qkv.append(dpj)
        dwb.append(dwj[:4])
    g["conv_b_w"] = jnp.concatenate(dwb, axis=1)
    d_a, dwa = _conv_a_bwd(proj_a, conv_a_w, dy_a)
    g["conv_a_w"] = dwa[:3]
    d_tail = _pack_tail(dbr, dar)
    pieces = [(d_a, w_a), (d_pre_qkv[0], w_qkv[:, :1024]), (d_pre_qkv[1], w_qkv[:, 1024:2048]),
              (d_pre_qkv[2], w_qkv[:, 2048:]), (dzb, w_zb), (d_tail, w_tail)]
    dws = [_matmul(xb, dj, ta=True, name=f"bwd_proj_even_dw{j}") for j, (dj, _) in enumerate(pieces)]
    dws[-1] = dws[-1][:, :2 * nh]
    g["w_in_even"] = _col_shards(dws, wie.shape[2])
    token = last_grad_ready(g["w_in_even"])
    dx = dpre0
    scale = a
    for j, (dj, wj) in enumerate(pieces):
        dx = _matmul(dj, wj + jnp.asarray(token).astype(BF16) if j == 0 else wj, tb=True, add=dx, add_scale=scale,
                     name=f"bwd_proj_even_dx{j}")
        scale = 1.0
    return loss, dx, g


def _adamw(w, g, m, v, name):
    lead, rows, cols = w.shape
    tr = rows if rows <= 256 else 256
    assert rows % tr == 0, (name, rows)

    def body(w_ref, g_ref, m_ref, v_ref, d_ref, nm_ref, nv_ref):
        gg = g_ref[...]
        nm = ADAM_B1 * m_ref[...] + (1.0 - ADAM_B1) * gg
        nv = ADAM_B2 * v_ref[...] + (1.0 - ADAM_B2) * jnp.square(gg)
        m_hat = nm / (1.0 - ADAM_B1 ** ADAM_STEP)
        v_hat = nv / (1.0 - ADAM_B2 ** ADAM_STEP)
        d_ref[...] = -ADAM_LR * (m_hat / (jnp.sqrt(v_hat) + ADAM_EPS) + ADAM_WD * w_ref[...])
        nm_ref[...] = nm
        nv_ref[...] = nv

    spec = pl.BlockSpec((1, tr, cols), lambda l, i: (l, i, 0))
    return pl.pallas_call(
        body, name=name, grid=(lead, rows // tr), in_specs=[spec] * 4, out_specs=[spec] * 3,
        out_shape=[jax.ShapeDtypeStruct(w.shape, F32)] * 3, compiler_params=_params("parallel", "parallel"))(w, g, m, v)


MESH = pl.DeviceIdType.MESH
N_DEV = 8
HBM_SPEC = pl.BlockSpec(memory_space=pltpu.HBM)
VMEM_SPEC = pl.BlockSpec(memory_space=pltpu.VMEM)


def _coords():
    return lax.axis_index("x"), lax.axis_index("y"), lax.axis_index("c")


def _flip(v, bit):
    return 1 - v if bit else v


def _remote(src, dst, send_sem, recv_sem, dev):
    return pltpu.make_async_remote_copy(src_ref=src, dst_ref=dst, send_sem=send_sem, recv_sem=recv_sem,
                                        device_id=dev, device_id_type=MESH)


def _exchange_small(buf, reduce, name):
    rows = buf.shape[0]

    def body(in_ref, out_ref, slots, send_sems, recv_sems):
        x, y, c = _coords()
        me = 4 * x + 2 * y + c
        slots[me] = in_ref[...]
        peer = lambda k: (_flip(x, (k >> 2) & 1), _flip(y, (k >> 1) & 1), _flip(c, k & 1))
        sends = []
        for k in range(1, N_DEV):
            cp = _remote(in_ref, slots.at[me], send_sems.at[k - 1], recv_sems.at[k - 1], peer(k))
            cp.start()
            sends.append(cp)
        for k in range(1, N_DEV):
            px, py, pc = peer(k)
            _remote(in_ref, slots.at[4 * px + 2 * py + pc], send_sems.at[k - 1], recv_sems.at[k - 1], peer(k)).wait_recv()
        for cp in sends:
            cp.wait_send()
        if reduce:
            acc = slots[0]
            for d in range(1, N_DEV):
                acc = acc + slots[d]
            out_ref[...] = acc
        else:
            out_ref[...] = slots[...]

    out_shape = (rows, LANES) if reduce else (N_DEV, rows, LANES)
    return pl.pallas_call(
        body, name=name, in_specs=[VMEM_SPEC], out_specs=VMEM_SPEC, out_shape=jax.ShapeDtypeStruct(out_shape, F32),
        scratch_shapes=[pltpu.VMEM((N_DEV, rows, LANES), F32), pltpu.SemaphoreType.DMA((N_DEV - 1,)),
                        pltpu.SemaphoreType.DMA((N_DEV - 1,))])(buf)


def _half_rows(half, which):
    return pl.ds(pl.multiple_of(which * half, 16), half)


def _other_chip(x, y, k):
    return _flip(x, (k >> 1) & 1), _flip(y, k & 1)


SEM_SPEC = pl.BlockSpec(memory_space=pltpu.SEMAPHORE)
DATAFLOW = pltpu.SideEffectType.DATAFLOW_SIDE_EFFECTING


def _ici_piece(srcs, lands, send_sems, recv_sems, i, k, x, y, c):
    half = srcs[i].shape[0] // 2
    ox, oy = _other_chip(x, y, k)
    return _remote(srcs[i].at[_half_rows(half, c)], lands[i].at[2 * x + y, _half_rows(half, c)],
                   send_sems.at[3 * i + k - 1], recv_sems.at[3 * i + k - 1], (ox, oy, c)), (ox, oy)


def _gather_start(shards, name):
    n = len(shards)

    def body(*refs):
        srcs, lands = refs[:n], refs[n:2 * n]
        send_sems, recv_sems = refs[2 * n], refs[2 * n + 1]
        token = refs[-1]
        x, y, c = _coords()
        for i in range(n):
            for k in (1, 2, 3):
                _ici_piece(srcs, lands, send_sems, recv_sems, i, k, x, y, c)[0].start()
        token[...] = jnp.zeros_like(token)

    hbm = lambda a: pltpu.with_memory_space_constraint(a, pltpu.HBM)
    lands = [lax.empty((4,) + s.shape, s.dtype) for s in shards]
    out = pl.pallas_call(
        body, name=name,
        out_shape=(pltpu.SemaphoreType.DMA((3 * n,)), pltpu.SemaphoreType.DMA((3 * n,)),
                   *[pltpu.HBM(s.shape, s.dtype) for s in shards], *[pltpu.HBM(a.shape, a.dtype) for a in lands],
                   jax.ShapeDtypeStruct((SUBLANES, LANES), F32)),
        in_specs=[HBM_SPEC] * (2 * n), out_specs=(SEM_SPEC, SEM_SPEC, *[HBM_SPEC] * (2 * n), VMEM_SPEC),
        input_output_aliases={i: 2 + i for i in range(2 * n)},
        compiler_params=pltpu.CompilerParams(has_side_effects=DATAFLOW))(*[hbm(s) for s in shards], *[hbm(a) for a in lands])
    return out[0], out[1], out[2:2 + n], out[2 + n:2 + 2 * n], out[-1]


def _gather_wait(send_sems, recv_sems, srcs, lands, after, name):
    n = len(srcs)

    def body(*refs):
        src_refs, land_refs = refs[:n], refs[n:2 * n]
        send_sems, recv_sems = refs[2 * n], refs[2 * n + 1]
        x, y, c = _coords()
        for i in range(n):
            half = src_refs[i].shape[0] // 2
            for k in (1, 2, 3):
                cp, (ox, oy) = _ici_piece(src_refs, land_refs, send_sems, recv_sems, i, k, x, y, c)
                cp.wait_send()
                piece = land_refs[i].at[2 * ox + oy, _half_rows(half, c)]
                _remote(piece, piece, send_sems.at[3 * i + k - 1], recv_sems.at[3 * i + k - 1], (ox, oy, c)).wait_recv()

    out = pl.pallas_call(
        body, name=name,
        out_shape=(*[pltpu.HBM(s.shape, s.dtype) for s in srcs], *[pltpu.HBM(a.shape, a.dtype) for a in lands]),
        in_specs=[HBM_SPEC] * (2 * n) + [SEM_SPEC, SEM_SPEC, pl.BlockSpec(memory_space=pl.ANY)],
        out_specs=tuple([HBM_SPEC] * (2 * n)), input_output_aliases={i: i for i in range(2 * n)},
        compiler_params=pltpu.CompilerParams(has_side_effects=DATAFLOW))(*srcs, *lands, send_sems, recv_sems, after)
    return out[n:]


def _gather_forward(lands, name):
    n = len(lands)

    def body(*refs):
        ins, outs = refs[:n], refs[n:2 * n]
        send_sems, recv_sems = refs[2 * n:]
        x, y, c = _coords()
        sends = []
        for i in range(n):
            half = ins[i].shape[1] // 2
            for k in (1, 2, 3):
                ox, oy = _other_chip(x, y, k)
                cp = _remote(ins[i].at[2 * ox + oy, _half_rows(half, c)], outs[i].at[2 * ox + oy, _half_rows(half, c)],
                             send_sems.at[3 * i + k - 1], recv_sems.at[3 * i + k - 1], (x, y, 1 - c))
                cp.start()
                sends.append(cp)
        for i in range(n):
            half = ins[i].shape[1] // 2
            for k in (1, 2, 3):
                ox, oy = _other_chip(x, y, k)
                piece = outs[i].at[2 * ox + oy, _half_rows(half, 1 - c)]
                _remote(piece, piece, send_sems.at[3 * i + k - 1], recv_sems.at[3 * i + k - 1], (x, y, 1 - c)).wait_recv()
        for cp in sends:
            cp.wait_send()

    return pl.pallas_call(
        body, name=name, in_specs=[HBM_SPEC] * n, out_specs=[HBM_SPEC] * n,
        out_shape=[jax.ShapeDtypeStruct(a.shape, a.dtype) for a in lands],
        input_output_aliases={i: i for i in range(n)},
        scratch_shapes=[pltpu.SemaphoreType.DMA((3 * n,))] * 2)(*lands)


def _rs_sibling_swap(g4s, name):
    n = len(g4s)

    def body(*refs):
        ins, outs = refs[:n], refs[n:2 * n]
        send_sems, recv_sems = refs[2 * n:]
        x, y, c = _coords()
        sends = []
        for i in range(n):
            half = ins[i].shape[1] // 2
            for s in range(4):
                cp = _remote(ins[i].at[s, _half_rows(half, 1 - c)], outs[i].at[s], send_sems.at[4 * i + s],
                             recv_sems.at[4 * i + s], (x, y, 1 - c))
                cp.start()
                sends.append(cp)
        for cp in sends:
            cp.wait_recv()
        for cp in sends:
            cp.wait_send()

    return pl.pallas_call(
        body, name=name, in_specs=[HBM_SPEC] * n, out_specs=[HBM_SPEC] * n,
        out_shape=[jax.ShapeDtypeStruct((4, g.shape[1] // 2, g.shape[2]), g.dtype) for g in g4s],
        scratch_shapes=[pltpu.SemaphoreType.DMA((4 * n,))] * 2)(*g4s)


def _rs_add_sibling(g4, got, c_idx, name):
    _, rows, cols = g4.shape
    half = rows // 2
    tr = min(half, 256)
    nb = half // tr

    def body(c_ref, a_ref, b_ref, o_ref, ob_ref):
        total = a_ref[...] + b_ref[...]
        o_ref[...] = total
        ob_ref[...] = total.astype(BF16)

    blk = (1, tr, cols)
    out = pl.BlockSpec(blk, lambda s, i, c_ref: (s, i, 0))
    grid_spec = pltpu.PrefetchScalarGridSpec(
        num_scalar_prefetch=1, grid=(4, nb),
        in_specs=[pl.BlockSpec(blk, lambda s, i, c_ref: (s, c_ref[0] * nb + i, 0)), out],
        out_specs=[out, out])
    return pl.pallas_call(
        body, name=name, grid_spec=grid_spec,
        out_shape=[jax.ShapeDtypeStruct(got.shape, F32), jax.ShapeDtypeStruct(got.shape, BF16)],
        compiler_params=_params("parallel", "parallel"))(c_idx, g4, got)


def _rs_add_chips(p4, got3, idx, name):
    _, half, cols = p4.shape
    tr = min(half, 256)
    nb = half // tr

    def body(idx_ref, p_ref, a_ref, b_ref, c_ref, o_ref):
        o_ref[...] = ((p_ref[0] + a_ref[0].astype(F32)) + b_ref[0].astype(F32)) + c_ref[0].astype(F32)

    blk = (1, tr, cols)
    grid_spec = pltpu.PrefetchScalarGridSpec(
        num_scalar_prefetch=1, grid=(nb,),
        in_specs=[pl.BlockSpec(blk, lambda i, idx_ref: (idx_ref[0], i, 0))]
        + [pl.BlockSpec(blk, functools.partial(lambda k, i, idx_ref: (k, i, 0), k)) for k in range(3)],
        out_specs=pl.BlockSpec((tr, cols), lambda i, idx_ref: (idx_ref[1] * nb + i, 0)))
    return pl.pallas_call(body, name=name, grid_spec=grid_spec, out_shape=jax.ShapeDtypeStruct((2 * half, cols), F32),
                          compiler_params=_params("parallel"))(idx, p4, got3, got3, got3)


def _rs_share_halves(bufs, name):
    n = len(bufs)

    def body(*refs):
        ins, outs = refs[:n], refs[n:2 * n]
        send_sems, recv_sems = refs[2 * n:]
        x, y, c = _coords()
        sends = []
        for i in range(n):
            half = ins[i].shape[0] // 2
            cp = _remote(ins[i].at[_half_rows(half, c)], outs[i].at[_half_rows(half, c)], send_sems.at[i],
                         recv_sems.at[i], (x, y, 1 - c))
            cp.start()
            sends.append(cp)
        for i in range(n):
            half = ins[i].shape[0] // 2
            _remote(ins[i].at[_half_rows(half, c)], outs[i].at[_half_rows(half, 1 - c)], send_sems.at[i],
                    recv_sems.at[i], (x, y, 1 - c)).wait_recv()
        for cp in sends:
            cp.wait_send()

    return pl.pallas_call(
        body, name=name, in_specs=[HBM_SPEC] * n, out_specs=[HBM_SPEC] * n,
        out_shape=[jax.ShapeDtypeStruct(b.shape, b.dtype) for b in bufs],
        input_output_aliases={i: i for i in range(n)},
        scratch_shapes=[pltpu.SemaphoreType.DMA((n,))] * 2)(*bufs)


def _scatter_piece(srcs, lands, send_sems, recv_sems, i, k, x, y, c):
    ox, oy = _other_chip(x, y, k)
    return _remote(srcs[i].at[2 * ox + oy], lands[i].at[k - 1], send_sems.at[3 * i + k - 1],
                   recv_sems.at[3 * i + k - 1], (ox, oy, c))


def _rs_scatter_start(p4s, name):
    n = len(p4s)

    def body(*refs):
        srcs, lands = refs[:n], refs[n:2 * n]
        send_sems, recv_sems = refs[2 * n], refs[2 * n + 1]
        token = refs[-1]
        x, y, c = _coords()
        for i in range(n):
            for k in (1, 2, 3):
                _scatter_piece(srcs, lands, send_sems, recv_sems, i, k, x, y, c).start()
        token[...] = jnp.zeros_like(token)

    hbm = lambda a: pltpu.with_memory_space_constraint(a, pltpu.HBM)
    lands = [lax.empty((3,) + p.shape[1:], p.dtype) for p in p4s]
    out = pl.pallas_call(
        body, name=name,
        out_shape=(pltpu.SemaphoreType.DMA((3 * n,)), pltpu.SemaphoreType.DMA((3 * n,)),
                   *[pltpu.HBM(p.shape, p.dtype) for p in p4s], *[pltpu.HBM(a.shape, a.dtype) for a in lands],
                   jax.ShapeDtypeStruct((SUBLANES, LANES), F32)),
        in_specs=[HBM_SPEC] * (2 * n), out_specs=(SEM_SPEC, SEM_SPEC, *[HBM_SPEC] * (2 * n), VMEM_SPEC),
        input_output_aliases={i: 2 + i for i in range(2 * n)},
        compiler_params=pltpu.CompilerParams(has_side_effects=DATAFLOW))(*[hbm(p) for p in p4s], *[hbm(a) for a in lands])
    return out[0], out[1], out[2:2 + n], out[2 + n:2 + 2 * n], out[-1]


def _rs_scatter_wait(send_sems, recv_sems, srcs, lands, after, name):
    n = len(srcs)

    def body(*refs):
        src_refs, land_refs = refs[:n], refs[n:2 * n]
        send_sems, recv_sems = refs[2 * n], refs[2 * n + 1]
        x, y, c = _coords()
        for i in range(n):
            for k in (1, 2, 3):
                cp = _scatter_piece(src_refs, land_refs, send_sems, recv_sems, i, k, x, y, c)
                cp.wait_send()
                cp.wait_recv()

    out = pl.pallas_call(
        body, name=name,
        out_shape=(*[pltpu.HBM(s.shape, s.dtype) for s in srcs], *[pltpu.HBM(a.shape, a.dtype) for a in lands]),
        in_specs=[HBM_SPEC] * (2 * n) + [SEM_SPEC, SEM_SPEC, pl.BlockSpec(memory_space=pl.ANY)],
        out_specs=tuple([HBM_SPEC] * (2 * n)), input_output_aliases={i: i for i in range(2 * n)},
        compiler_params=pltpu.CompilerParams(has_side_effects=DATAFLOW))(*srcs, *lands, send_sems, recv_sems, after)
    return out[n:]


def _rs_front(g4s, names, tag):
    c_idx = jnp.stack([lax.axis_index("c")]).astype(jnp.int32)
    got = _rs_sibling_swap(g4s, f"rs_sibling_swap_{tag}")
    return [_rs_add_sibling(g, s, c_idx, f"rs_add_sibling_{nm}") for g, s, nm in zip(g4s, got, names)]


def _rs_back(p4s, got3, names):
    x, y, c = _coords()
    idx = jnp.stack([2 * x + y, c]).astype(jnp.int32)
    return [_rs_add_chips(p, t, idx, f"rs_add_chips_{nm}") for (p, _), t, nm in zip(p4s, got3, names)]


def _cols_split(full):
    r, c4 = full.shape
    return full.reshape(r, 4, c4 // 4).transpose(1, 0, 2)


_BIG = {
    "w_in_even": ((1024, 2052), lambda s: s),
    "w_out_even": ((512, 1024), lambda s: s.reshape(2048, 1024)),
    "w_in_odd": ((1024, 2048), lambda s: s),
    "w_out_odd": ((512, 1024), lambda s: s.reshape(2048, 1024)),
    "w_pl": ((512, 256), lambda s: s.reshape(4, 2, 256, 256).transpose(1, 2, 0, 3).reshape(2, 256, 1024)),
    "w_pl_gate": ((512, 1024), lambda s: s.reshape(4, 2, 256, 1024).transpose(1, 0, 2, 3).reshape(2, 1024, 1024)),
}


_RS_EARLY = {
    "w_in_odd": lambda f: f,
    "w_out_odd": lambda f: f.reshape(4, 512, 1024),
    "w_pl_gate1": lambda f: f.reshape(4, 256, 1024),
    "w_pl1": _cols_split,
    "w_out_even": lambda f: f.reshape(4, 512, 1024),
    "w_pl_gate0": lambda f: f.reshape(4, 256, 1024),
    "w_pl0": _cols_split,
}
_RS_LATE = {"w_in_even": lambda f: f}


def _size(shape):
    n = 1
    for d in shape:
        n *= d
    return n


_SMALL = {"a_log": (1, 8), "dt_bias": (1, 8), "gdn_norm_g": (1, 128), "hgrn_norm_g": (1, 128),
          "lower_bounds": (2, 2048), "ln_g": (2, 1024), "ln_b": (2, 1024), "conv_a_w": (3, 1024), "conv_b_w": (4, 3072)}
_CONV_SHARD = {"conv_a_w": (3, 256), "conv_b_w": (4, 768)}


def _pack_small(parts, shapes, head_rows=0):
    rows = []
    for n, shape in shapes.items():
        v = parts[n].reshape(-1)
        rows.append(jnp.pad(v, (0, -v.shape[0] % LANES)).reshape(-1, LANES))
    buf = jnp.concatenate(rows, axis=0)
    return jnp.pad(buf, ((head_rows, -(buf.shape[0] + head_rows) % SUBLANES), (0, 0)))


def _unpack_small(buf, shapes, head_rows=0):
    out, off = {}, head_rows
    for n, shape in shapes.items():
        nrow = -(-_size(shape) // LANES)
        out[n] = buf[off:off + nrow].reshape(-1)[:_size(shape)].reshape(shape)
        off += nrow
    return out


_WEIGHTS = ["w_in_even", "conv_a_w", "conv_b_w", "a_log", "dt_bias", "gdn_norm_g", "w_out_even", "w_in_odd",
            "lower_bounds", "hgrn_norm_g", "w_out_odd", "ln_g", "ln_b", "w_pl", "w_pl_gate"]


def kernel(x, p, w_in_even, conv_a_w, conv_b_w, a_log, dt_bias, gdn_norm_g, w_out_even, w_in_odd, lower_bounds, hgrn_norm_g, w_out_odd, ln_g, ln_b, w_pl, w_pl_gate, loss_target, m_w_in_even, m_conv_a_w, m_conv_b_w, m_a_log, m_dt_bias, m_gdn_norm_g, m_w_out_even, m_w_in_odd, m_lower_bounds, m_hgrn_norm_g, m_w_out_odd, m_ln_g, m_ln_b, m_w_pl, m_w_pl_gate, v_w_in_even, v_conv_a_w, v_conv_b_w, v_a_log, v_dt_bias, v_gdn_norm_g, v_w_out_even, v_w_in_odd, v_lower_bounds, v_hgrn_norm_g, v_w_out_odd, v_ln_g, v_ln_b, v_w_pl, v_w_pl_gate):
    w = dict(zip(_WEIGHTS, (w_in_even, conv_a_w, conv_b_w, a_log, dt_bias, gdn_norm_g, w_out_even, w_in_odd,
                            lower_bounds, hgrn_norm_g, w_out_odd, ln_g, ln_b, w_pl, w_pl_gate)))
    m = dict(zip(_WEIGHTS, (m_w_in_even, m_conv_a_w, m_conv_b_w, m_a_log, m_dt_bias, m_gdn_norm_g, m_w_out_even,
                            m_w_in_odd, m_lower_bounds, m_hgrn_norm_g, m_w_out_odd, m_ln_g, m_ln_b, m_w_pl, m_w_pl_gate)))
    v = dict(zip(_WEIGHTS, (v_w_in_even, v_conv_a_w, v_conv_b_w, v_a_log, v_dt_bias, v_gdn_norm_g, v_w_out_even,
                            v_w_in_odd, v_lower_bounds, v_hgrn_norm_g, v_w_out_odd, v_ln_g, v_ln_b, v_w_pl, v_w_pl_gate)))
    chip = 2 * lax.axis_index("x") + lax.axis_index("y")

    names = list(_BIG)
    shard_shapes = {n: _BIG[n][0] for n in names}
    early, late = names[:1], names[1:]
    shards = {n: w[n].reshape(shard_shapes[n]).astype(BF16) for n in early}
    whole = lambda n, stacked: _BIG[n][1](lax.dynamic_update_slice(stacked, shards[n][None], (chip, 0, 0)))
    conv_mine = _pack_small({n: w[n] for n in _CONV_SHARD}, _CONV_SHARD)
    conv_all = _exchange_small(conv_mine, False, "gather_conv")
    shards, conv_all = lax.optimization_barrier((shards, conv_all))
    first = _gather_start([shards[n] for n in early], "gather_first_start")
    shards.update({n: (w[n].reshape(shard_shapes[n]) + first[4][0, 0]).astype(BF16) for n in late})
    send_sems, recv_sems, srcs, lands, token = _gather_start([shards[n] for n in late], "gather_rest_start")

    def late_weights(after):
        landed = _gather_forward(_gather_wait(send_sems, recv_sems, srcs, lands, after, "gather_rest_wait"),
                                 "gather_rest_forward")
        return {n: whole(n, ga) for n, ga in zip(late, landed)}

    landed = _gather_forward(_gather_wait(*first[:4], token, "gather_first_wait"), "gather_first_forward")
    full = {n: whole(n, ga) for n, ga in zip(early, landed)}
    conv_by_chip = [_unpack_small(conv_all[2 * s], _CONV_SHARD) for s in range(4)]
    for n in _CONV_SHARD:
        full[n] = jnp.concatenate([conv_by_chip[s][n] for s in range(4)], axis=1)
    for n in _SMALL:
        if n not in _CONV_SHARD:
            full[n] = w[n]

    early_rs = {}

    def early_grads_ready(grads):
        early_rs["p4s"] = _rs_front([_RS_EARLY[n](grads[n]) for n in _RS_EARLY], list(_RS_EARLY), "early")
        early_rs["sems"] = _rs_scatter_start([pb for _, pb in early_rs["p4s"]], "rs_scatter_early_start")
        return early_rs["sems"][4][0, 0]

    late_rs = {}

    def last_grad_ready(grad):
        late_rs["p4s"] = _rs_front([_RS_LATE[n](grad) for n in _RS_LATE], list(_RS_LATE), "late")
        late_rs["sems"] = _rs_scatter_start([pb for _, pb in late_rs["p4s"]], "rs_scatter_late_start")
        return late_rs["sems"][4][0, 0]

    loss_part, dx, g = _local_step(x[0], p[:, 0], loss_target[0], full, late_weights, early_grads_ready,
                                   last_grad_ready, token[0, 0])

    late_p4s, late_sems = late_rs["p4s"], late_rs["sems"]
    got3 = _rs_scatter_wait(*early_rs["sems"][:4], dx, "rs_scatter_early_wait")
    summed = dict(zip(_RS_EARLY, _rs_share_halves(_rs_back(early_rs["p4s"], got3, list(_RS_EARLY)), "rs_share_early")))
    g_big = {n: summed[n] for n in names if n in summed}
    g_big["w_pl"] = jnp.stack([summed["w_pl0"], summed["w_pl1"]])
    g_big["w_pl_gate"] = jnp.stack([summed["w_pl_gate0"], summed["w_pl_gate1"]])
    small_sum = _exchange_small(jnp.concatenate([loss_part, _pack_small(g, _SMALL)], axis=0), True, "reduce_small")
    loss = small_sum[0, 0]
    g_small = _unpack_small(small_sum, _SMALL, head_rows=SUBLANES)
    for n, (rows, cols) in _CONV_SHARD.items():
        g_small[n] = lax.dynamic_slice_in_dim(g_small[n], chip * cols, cols, axis=1)

    grads, delta, new_m, new_v = {}, {}, {}, {}
    for n in late:
        grads[n] = g_big[n].reshape(w[n].shape)
        delta[n], new_m[n], new_v[n] = _adamw(w[n], grads[n], m[n], v[n], f"adamw_{n}")
    own = {n: (_CONV_SHARD[n] if n in _CONV_SHARD else _SMALL[n]) for n in _SMALL}
    packs = [_pack_small({n: src[n] for n in _SMALL}, own)[None] for src in (w, g_small, m, v)]
    outs = [_unpack_small(t[0], own) for t in _adamw(*packs, "adamw_small")]
    for n in _SMALL:
        grads[n] = g_small[n].reshape(w[n].shape)
        delta[n], new_m[n], new_v[n] = (t[n].reshape(w[n].shape) for t in outs)
    got3 = _rs_scatter_wait(*late_sems[:4], new_v["w_in_odd"], "rs_scatter_late_wait")
    (g_in_even,) = _rs_share_halves(_rs_back(late_p4s, got3, list(_RS_LATE)), "rs_share_late")
    for n in early:
        grads[n] = g_in_even.reshape(w[n].shape)
        delta[n], new_m[n], new_v[n] = _adamw(w[n], grads[n], m[n], v[n], f"adamw_{n}")
    return (loss, dx[None], *[grads[n] for n in _WEIGHTS], *[delta[n] for n in _WEIGHTS],
            *[new_m[n] for n in _WEIGHTS], *[new_v[n] for n in _WEIGHTS])
```

```python
import functools

import jax
import jax.numpy as jnp
from jax import lax
from jax.experimental import pallas as pl
from jax.experimental.pallas import tpu as pltpu

F32 = jnp.float32
BF16 = jnp.bfloat16
HI = lax.Precision.HIGHEST

D_MODEL = 1024
PL_DIM = 256
GDN_HEADS = 8
HEAD_DIM = 128
GDN_CHUNK = 64
HGRN_HEADS = 16
HGRN_CHUNK = 32
HGRN_WIDTH = 2048
DEEPNORM_ALPHA = 4.0 ** 0.25
NORM_EPS = 1e-5
ADAM_LR, ADAM_B1, ADAM_B2, ADAM_EPS, ADAM_WD, ADAM_STEP = 0.001, 0.9, 0.999, 1e-08, 0.01, 10

VMEM_LIMIT = 56 * 1024 * 1024
SUBLANES = 8
LANES = 128


def _params(*sem):
    return pltpu.CompilerParams(dimension_semantics=sem, vmem_limit_bytes=VMEM_LIMIT)


ONE_PASS, THREE_PASS, FULL_F32, EXACT_LHS, EXACT_RHS = 0, 1, 2, 3, 4


def _split3(v):
    hi = v.astype(BF16)
    r1 = v - hi.astype(F32)
    mid = r1.astype(BF16)
    return hi, mid, (r1 - mid.astype(F32)).astype(BF16)


def _mm_raw(a, b, kind, prec):
    nb = a.ndim - 2
    ca = a.ndim - 1 if kind[0] == "n" else a.ndim - 2
    cb = b.ndim - 2 if kind[1] == "n" else b.ndim - 1
    dims = (((ca,), (cb,)), (tuple(range(nb)),) * 2)
    if prec == FULL_F32:
        return lax.dot_general(a, b, dims, precision=HI, preferred_element_type=F32)
    dot = lambda p, q: lax.dot_general(p, q, dims, preferred_element_type=F32)
    ah, bh = a.astype(BF16), b.astype(BF16)
    if prec == ONE_PASS:
        return dot(ah, bh)
    if prec == EXACT_LHS:
        b1, b2, b3 = _split3(b)
        return dot(ah, b1) + (dot(ah, b2) + dot(ah, b3))
    if prec == EXACT_RHS:
        a1, a2, a3 = _split3(a)
        return dot(a1, bh) + (dot(a2, bh) + dot(a3, bh))
    al = (a - ah.astype(F32)).astype(BF16)
    bl = (b - bh.astype(F32)).astype(BF16)
    return dot(ah, bh) + (dot(ah, bl) + dot(al, bh))


@functools.partial(jax.custom_vjp, nondiff_argnums=(2, 3))
def _mm_vjp(a, b, kind, hi):
    return _mm_raw(a, b, kind, hi)


def _mm_vjp_fwd(a, b, kind, hi):
    return _mm_raw(a, b, kind, hi), (a, b)


def _mm_vjp_bwd(kind, hi, res, dc):
    a, b = res
    if hi in (EXACT_LHS, EXACT_RHS):
        assert kind == "nn"
        if hi == EXACT_LHS:
            return jnp.zeros_like(a), _mm_raw(a, dc, "tn", EXACT_LHS)
        return _mm_raw(dc, b, "nt", EXACT_RHS), jnp.zeros_like(b)
    if kind == "nn":
        return _mm_raw(dc, b, "nt", hi), _mm_raw(a, dc, "tn", hi)
    if kind == "nt":
        return _mm_raw(dc, b, "nn", hi), _mm_raw(dc, a, "tn", hi)
    return _mm_raw(b, dc, "nt", hi), _mm_raw(a, dc, "nn", hi)


_mm_vjp.defvjp(_mm_vjp_fwd, _mm_vjp_bwd)


def _lane_total(v):
    return jnp.broadcast_to(jnp.sum(v, axis=-1, keepdims=True), v.shape)


def _matmul(a, b, *, name, ta=False, tb=False, add=None, add_scale=1.0, tm=1024, tn=2048, tk=1024):
    m, k = (a.shape[1], a.shape[0]) if ta else a.shape
    n = b.shape[0] if tb else b.shape[1]
    tm, tn, tk = min(tm, m), min(tn, n), min(tk, k)
    tn = tn if n % tn == 0 else tn // 2
    assert m % tm == 0 and n % tn == 0 and k % tk == 0, (name, m, n, k)
    nk = k // tk
    dims = (((0 if ta else 1,), (1 if tb else 0,)), ((), ()))

    def body(*refs):
        a_ref, b_ref = refs[:2]
        o_ref = refs[-1]
        part = lax.dot_general(a_ref[...].astype(BF16), b_ref[...].astype(BF16), dims, preferred_element_type=F32)
        first = (lambda: part) if add is None else (lambda: part + add_scale * refs[2][...])
        if nk == 1:
            o_ref[...] = first()
        else:
            kk = pl.program_id(2)

            @pl.when(kk == 0)
            def _():
                o_ref[...] = first()

            @pl.when(kk > 0)
            def _():
                o_ref[...] += part

    a_spec = pl.BlockSpec((tk, tm), lambda i, j, kk: (kk, i)) if ta else pl.BlockSpec((tm, tk), lambda i, j, kk: (i, kk))
    b_spec = pl.BlockSpec((tn, tk), lambda i, j, kk: (j, kk)) if tb else pl.BlockSpec((tk, tn), lambda i, j, kk: (kk, j))
    o_spec = pl.BlockSpec((tm, tn), lambda i, j, kk: (i, j))
    in_specs = [a_spec, b_spec] + ([o_spec] if add is not None else [])
    args = (a, b) + ((add,) if add is not None else ())
    return pl.pallas_call(
        body, name=name, grid=(m // tm, n // tn, nk), in_specs=in_specs, out_specs=o_spec,
        out_shape=jax.ShapeDtypeStruct((m, n), F32),
        compiler_params=_params("parallel", "parallel", "arbitrary"))(*args)


HALO = SUBLANES


def _halo_specs(tt, width, col, nt):
    r = tt // HALO
    prev = pl.BlockSpec((HALO, width), lambda i: (jnp.maximum(i * r - 1, 0), col))
    nxt = pl.BlockSpec((HALO, width), lambda i: (jnp.minimum((i + 1) * r, nt * r - 1), col))
    return prev, nxt


def _shift_down(ext, k):
    return ext if k == 0 else pltpu.roll(ext, k, 0)


def _shift_up(ext, k):
    return ext if k == 0 else pltpu.roll(ext, ext.shape[0] - k, 0)


def _causal_conv(ext, w, taps):
    acc = None
    for j in range(taps):
        term = w[j:j + 1, :] * _shift_down(ext, taps - 1 - j)
        acc = term if acc is None else acc + term
    return acc[HALO:, :]


def _conv_a_fwd(proj_a, conv_w):
    t = proj_a.shape[0]
    tt = min(t, 256)
    nt = t // tt
    wdt = 1024

    def body(cur_ref, prev_ref, w_ref, y_ref):
        i = pl.program_id(0)
        cur = cur_ref[...]
        h, c, b, z = (cur[:, k * wdt:(k + 1) * wdt] for k in range(4))
        prev = prev_ref[...]
        u_prev = jnp.where(i > 0, prev[:, wdt:2 * wdt] * prev[:, 0:wdt], 0.0)
        ext = jnp.concatenate([u_prev, c * h], axis=0)
        conv = _causal_conv(ext, w_ref[...], 3)
        y_ref[...] = (b * conv * jax.nn.silu(z)).astype(BF16)

    prev_spec, _ = _halo_specs(tt, 4 * wdt, 0, nt)
    return pl.pallas_call(
        body, name="conv_a_fwd", grid=(nt,),
        in_specs=[pl.BlockSpec((tt, 4 * wdt), lambda i: (i, 0)), prev_spec, pl.BlockSpec((3, wdt), lambda i: (0, 0))],
        out_specs=pl.BlockSpec((tt, wdt), lambda i: (i, 0)),
        out_shape=jax.ShapeDtypeStruct((t, wdt), BF16), compiler_params=_params("parallel"))(proj_a, proj_a, conv_w)


def _conv_a_bwd(proj_a, conv_w, dy):
    t = proj_a.shape[0]
    tt = min(t, 256)
    nt = t // tt
    wdt = 1024

    def body(cur_ref, prev_ref, nxt_ref, w_ref, dy_ref, dyn_ref, d_ref, dw_ref):
        i = pl.program_id(0)
        w = w_ref[...]
        cur, prev, nxt = cur_ref[...], prev_ref[...], nxt_ref[...]
        split = lambda a: tuple(a[:, k * wdt:(k + 1) * wdt] for k in range(4))
        h, c, b, z = split(cur)
        hp, cp, _, _ = split(prev)
        hn, cn, bn, zn = split(nxt)
        u_prev = jnp.where(i > 0, cp * hp, 0.0)
        u_ext = jnp.concatenate([u_prev, c * h, cn * hn], axis=0)
        taps = [_shift_down(u_ext, 2 - j)[HALO:, :] for j in range(3)]
        conv = w[0:1, :] * taps[0] + w[1:2, :] * taps[1] + w[2:3, :] * taps[2]
        b_cn = jnp.concatenate([b, bn], axis=0)
        z_cn = jnp.concatenate([z, zn], axis=0)
        dy_cn = jnp.concatenate([dy_ref[...], jnp.where(i < nt - 1, dyn_ref[...], 0.0)], axis=0)
        sg = jax.nn.sigmoid(z_cn)
        silu = z_cn * sg
        d_conv = dy_cn * b_cn * silu
        db = (dy_cn * conv * silu)[:tt, :]
        dz = (dy_cn * b_cn * conv * (sg * (1.0 + z_cn * (1.0 - sg))))[:tt, :]
        du = None
        for j in range(3):
            term = w[j:j + 1, :] * _shift_up(d_conv, 2 - j)
            du = term if du is None else du + term
        du = du[:tt, :]
        d_ref[...] = jnp.concatenate([du * c, du * h, db, dz], axis=1).astype(BF16)

        @pl.when(i == 0)
        def _():
            dw_ref[...] = jnp.zeros_like(dw_ref)

        d_cur = d_conv[:tt, :]
        rows = [jnp.sum(d_cur * taps[j][:tt, :], axis=0, keepdims=True) for j in range(3)]
        dw_ref[0:3, :] += jnp.concatenate(rows, axis=0)

    prev_spec, nxt_spec = _halo_specs(tt, 4 * wdt, 0, nt)
    _, dyn_spec = _halo_specs(tt, wdt, 0, nt)
    return pl.pallas_call(
        body, name="conv_a_bwd", grid=(nt,),
        in_specs=[pl.BlockSpec((tt, 4 * wdt), lambda i: (i, 0)), prev_spec, nxt_spec,
                  pl.BlockSpec((3, wdt), lambda i: (0, 0)), pl.BlockSpec((tt, wdt), lambda i: (i, 0)), dyn_spec],
        out_specs=[pl.BlockSpec((tt, 4 * wdt), lambda i: (i, 0)), pl.BlockSpec((SUBLANES, wdt), lambda i: (0, 0))],
        out_shape=[jax.ShapeDtypeStruct((t, 4 * wdt), BF16), jax.ShapeDtypeStruct((SUBLANES, wdt), F32)],
        compiler_params=_params("arbitrary"))(proj_a, proj_a, proj_a, conv_w, dy, dy)


def _conv_b_fwd(proj_qkv, conv_w):
    t, width = proj_qkv.shape
    tt = min(t, 256)
    nt = t // tt
    wdt = 1024

    def body(cur_ref, prev_ref, w_ref, y_ref):
        i = pl.program_id(1)
        ext = jnp.concatenate([jnp.where(i > 0, prev_ref[...], 0.0), cur_ref[...]], axis=0)
        y_ref[...] = jax.nn.silu(_causal_conv(ext, w_ref[...], 4))

    r = tt // HALO
    return pl.pallas_call(
        body, name="conv_b_fwd", grid=(width // wdt, nt),
        in_specs=[pl.BlockSpec((tt, wdt), lambda j, i: (i, j)),
                  pl.BlockSpec((HALO, wdt), lambda j, i: (jnp.maximum(i * r - 1, 0), j)),
                  pl.BlockSpec((4, wdt), lambda j, i: (0, j))],
        out_specs=pl.BlockSpec((tt, wdt), lambda j, i: (i, j)),
        out_shape=jax.ShapeDtypeStruct((t, width), F32), compiler_params=_params("parallel", "parallel"))(
            proj_qkv, proj_qkv, conv_w)


def _conv_b_bwd(proj_qkv, conv_w, d_act, col, name):
    t = proj_qkv.shape[0]
    tt = min(t, 256)
    nt = t // tt
    wdt = 1024

    def body(cur_ref, prev_ref, nxt_ref, w_ref, da_ref, dan_ref, d_ref, dw_ref):
        i = pl.program_id(0)
        w = w_ref[...]
        u_ext = jnp.concatenate([jnp.where(i > 0, prev_ref[...], 0.0), cur_ref[...], nxt_ref[...]], axis=0)
        taps = [_shift_down(u_ext, 3 - j)[HALO:, :] for j in range(4)]
        conv = w[0:1, :] * taps[0] + w[1:2, :] * taps[1] + w[2:3, :] * taps[2] + w[3:4, :] * taps[3]
        da_cn = jnp.concatenate([da_ref[...], jnp.where(i < nt - 1, dan_ref[...], 0.0)], axis=0)
        sg = jax.nn.sigmoid(conv)
        d_conv = da_cn * (sg * (1.0 + conv * (1.0 - sg)))
        du = None
        for j in range(4):
            term = w[j:j + 1, :] * _shift_up(d_conv, 3 - j)
            du = term if du is None else du + term
        d_ref[...] = du[:tt, :].astype(BF16)

        @pl.when(i == 0)
        def _():
            dw_ref[...] = jnp.zeros_like(dw_ref)

        d_cur = d_conv[:tt, :]
        rows = [jnp.sum(d_cur * taps[j][:tt, :], axis=0, keepdims=True) for j in range(4)]
        dw_ref[0:4, :] += jnp.concatenate(rows, axis=0)

    prev_spec, nxt_spec = _halo_specs(tt, wdt, col, nt)
    _, dan_spec = _halo_specs(tt, wdt, 0, nt)
    return pl.pallas_call(
        body, name=name, grid=(nt,),
        in_specs=[pl.BlockSpec((tt, wdt), lambda i: (i, col)), prev_spec, nxt_spec,
                  pl.BlockSpec((4, wdt), lambda i: (0, col)), pl.BlockSpec((tt, wdt), lambda i: (i, 0)), dan_spec],
        out_specs=[pl.BlockSpec((tt, wdt), lambda i: (i, 0)), pl.BlockSpec((SUBLANES, wdt), lambda i: (0, 0))],
        out_shape=[jax.ShapeDtypeStruct((t, wdt), BF16), jax.ShapeDtypeStruct((SUBLANES, wdt), F32)],
        compiler_params=_params("arbitrary"))(proj_qkv, proj_qkv, proj_qkv, conv_w, d_act, d_act)


def _rms_gate(o, gn, z):
    on = o * lax.rsqrt(jnp.mean(o * o, axis=-1, keepdims=True) + NORM_EPS) * gn
    return on * jax.nn.silu(z)


GDN_PREP_ROWS = 1024


def _unit_lower_inverse(low):
    c = low.shape[-1]
    eye = lax.broadcasted_iota(jnp.int32, low.shape, low.ndim - 2) == lax.broadcasted_iota(jnp.int32, low.shape, low.ndim - 1)
    x = -low
    inv = eye.astype(F32) + x
    for _ in range(c.bit_length() - 2):
        x = _mm_raw(x, x, "nn", THREE_PASS)
        inv = inv + _mm_raw(inv, x, "nn", THREE_PASS)
    return inv


@jax.custom_vjp
def _known_inverse(low, inv):
    return inv


def _known_inverse_fwd(low, inv):
    return inv, inv


def _known_inverse_bwd(inv, d_inv):
    return -_mm_raw(_mm_raw(inv, d_inv, "tn", THREE_PASS), inv, "nt", THREE_PASS), jnp.zeros_like(inv)


_known_inverse.defvjp(_known_inverse_fwd, _known_inverse_bwd)


def _gdn_prep(mm, qa, ka, va, braw, araw, alog, dtb, inv_kept=None):
    n, c, _ = qa.shape
    q = qa * lax.rsqrt(jnp.sum(qa * qa, axis=-1, keepdims=True) + 1e-6) * (HEAD_DIM ** -0.5)
    k = ka * lax.rsqrt(jnp.sum(ka * ka, axis=-1, keepdims=True) + 1e-6)
    beta = jax.nn.sigmoid(braw)
    g = -jnp.exp(alog) * jax.nn.softplus(araw + dtb)
    ri = lax.broadcasted_iota(jnp.int32, (n, c, c), 1)
    ci = lax.broadcasted_iota(jnp.int32, (n, c, c), 2)
    incl, strict, eye = ri >= ci, ri > ci, ri == ci
    gc = mm(incl.astype(F32), g, "nn", EXACT_LHS)
    gc_i = gc[:, :, :c]
    gc_j = mm(jnp.ones((n, c, c), F32), jnp.where(eye, gc_i, 0.0), "nn", EXACT_LHS)
    decay = jnp.where(incl, jnp.exp(jnp.where(incl, gc_i - gc_j, 0.0)), 0.0)
    kb = k * beta
    low = jnp.where(strict, mm(kb, k, "nt", ONE_PASS) * decay, 0.0)
    inv = _unit_lower_inverse(low) if inv_kept is None else _known_inverse(low, inv_kept)
    egc = jnp.exp(gc)
    u = mm(inv, va * beta, "nn", THREE_PASS)
    w = mm(inv, kb * egc, "nn", THREE_PASS)
    attn = jnp.where(incl, mm(q, k, "nt", ONE_PASS) * decay, 0.0)
    g_last = jnp.sum(g, axis=1, keepdims=True)
    outs = (u, w, q * egc, k * jnp.exp(g_last - gc), attn, jnp.exp(g_last))
    return outs + (inv,) if inv_kept is None else outs


def _gdn_scan(mm, u, w, qd, kd, attn, egl, z, gn, state):
    v_new = u - mm(w, state, "nn", ONE_PASS)
    o = mm(qd, state, "nn", ONE_PASS) + mm(attn, v_new, "nn", ONE_PASS)
    new_state = state * egl + mm(kd, v_new, "tn", ONE_PASS)
    return _rms_gate(o, gn, z), new_state


def _chunks(ref_value, n, c):
    return ref_value.reshape(n, c, ref_value.shape[-1])


def _by_head(ref, rows, heads):
    return jnp.stack([ref[rows, pl.ds(h * HEAD_DIM, HEAD_DIM)] for h in range(heads)])


def _store_heads(ref, rows, value):
    for h in range(value.shape[0]):
        ref[rows, pl.ds(h * HEAD_DIM, HEAD_DIM)] = value[h]


def _gdn_prep_specs(tb, nt_unused=None):
    col = lambda off: pl.BlockSpec((tb, HEAD_DIM), lambda h, i: (i, off + h))
    rep = pl.BlockSpec((1, tb, LANES), lambda h, i: (h, i, 0))
    par = pl.BlockSpec((1, SUBLANES, LANES), lambda h, i: (h, 0, 0))
    att = pl.BlockSpec((1, tb, GDN_CHUNK), lambda h, i: (h, i, 0))
    egl = pl.BlockSpec((1, tb // GDN_CHUNK, SUBLANES, LANES), lambda h, i: (h, i, 0, 0))
    return col, rep, par, att, egl


def _gdn_prep_fwd(qkv_act, braw, araw, alog, dtb):
    t = qkv_act.shape[0]
    tb = min(t, 2 * GDN_PREP_ROWS)
    nt, nc = t // tb, tb // GDN_CHUNK
    width = GDN_HEADS * HEAD_DIM

    def body(q_ref, k_ref, v_ref, br_ref, ar_ref, al_ref, dt_ref, u_ref, w_ref, qd_ref, kd_ref, at_ref, eg_ref, inv_ref):
        ch = lambda r: _chunks(r, nc, GDN_CHUNK)
        u, w, qd, kd, attn, egl, inv = _gdn_prep(_mm_raw, ch(q_ref[...]), ch(k_ref[...]), ch(v_ref[...]), ch(br_ref[0]),
                                                 ch(ar_ref[0]), al_ref[0, 0:1, :], dt_ref[0, 0:1, :])
        u_ref[...] = u.reshape(tb, HEAD_DIM)
        w_ref[...] = w.reshape(tb, HEAD_DIM).astype(BF16)
        qd_ref[...] = qd.reshape(tb, HEAD_DIM).astype(BF16)
        kd_ref[...] = kd.reshape(tb, HEAD_DIM).astype(BF16)
        at_ref[0] = attn.reshape(tb, GDN_CHUNK).astype(BF16)
        eg_ref[0] = jnp.broadcast_to(egl, (nc, SUBLANES, LANES))
        inv_ref[0] = inv.reshape(tb, GDN_CHUNK)

    col, rep, par, att, egl = _gdn_prep_specs(tb)
    h = GDN_HEADS
    return pl.pallas_call(
        body, name="gdn_prep_fwd", grid=(h, nt),
        in_specs=[col(0), col(h), col(2 * h), rep, rep, par, par],
        out_specs=[col(0), col(0), col(0), col(0), att, egl, att],
        out_shape=[jax.ShapeDtypeStruct((t, width), F32)] + [jax.ShapeDtypeStruct((t, width), BF16)] * 3
        + [jax.ShapeDtypeStruct((h, t, GDN_CHUNK), BF16), jax.ShapeDtypeStruct((h, t // GDN_CHUNK, SUBLANES, LANES), F32),
           jax.ShapeDtypeStruct((h, t, GDN_CHUNK), F32)],
        compiler_params=_params("parallel", "parallel"))(qkv_act, qkv_act, qkv_act, braw, araw, alog, dtb)


def _gdn_prep_bwd(qkv_act, braw, araw, alog, dtb, inv, du, dw, dqd, dkd, dattn, degl):
    t = qkv_act.shape[0]
    tb = min(t, GDN_PREP_ROWS)
    nt, nc = t // tb, tb // GDN_CHUNK
    width = GDN_HEADS * HEAD_DIM

    def body(q_ref, k_ref, v_ref, br_ref, ar_ref, al_ref, dt_ref, inv_ref, du_ref, dw_ref, dqd_ref, dkd_ref, dat_ref,
             deg_ref, dq_ref, dk_ref, dv_ref, dbr_ref, dar_ref, dal_ref, ddt_ref):
        @pl.when(pl.program_id(1) == 0)
        def _():
            dal_ref[...] = jnp.zeros_like(dal_ref)
            ddt_ref[...] = jnp.zeros_like(ddt_ref)

        ch = lambda r: _chunks(r, nc, GDN_CHUNK)
        _, vjp = jax.vjp(functools.partial(_gdn_prep, _mm_vjp, inv_kept=ch(inv_ref[0])), ch(q_ref[...]), ch(k_ref[...]),
                         ch(v_ref[...]), ch(br_ref[0]), ch(ar_ref[0]), al_ref[0, 0:1, :], dt_ref[0, 0:1, :])
        dq, dk, dv, dbr, dar, dal, ddt = vjp((ch(du_ref[...]), ch(dw_ref[...]), ch(dqd_ref[...]), ch(dkd_ref[...]),
                                              ch(dat_ref[0]), deg_ref[0][:, 0:1, :]))
        dq_ref[...] = dq.reshape(tb, HEAD_DIM)
        dk_ref[...] = dk.reshape(tb, HEAD_DIM)
        dv_ref[...] = dv.reshape(tb, HEAD_DIM)
        dbr_ref[0] = _lane_total(dbr.reshape(tb, LANES))
        dar_ref[0] = _lane_total(dar.reshape(tb, LANES))
        dal_ref[0, 0:1, :] += _lane_total(dal)
        ddt_ref[0, 0:1, :] += _lane_total(ddt)

    col, rep, par, att, egl = _gdn_prep_specs(tb)
    h = GDN_HEADS
    return pl.pallas_call(
        body, name="gdn_prep_bwd", grid=(h, nt),
        in_specs=[col(0), col(h), col(2 * h), rep, rep, par, par, att, col(0), col(0), col(0), col(0), att, egl],
        out_specs=[col(0), col(0), col(0), rep, rep, par, par],
        out_shape=[jax.ShapeDtypeStruct((t, width), F32)] * 3 + [jax.ShapeDtypeStruct((h, t, LANES), F32)] * 2
        + [jax.ShapeDtypeStruct((h, SUBLANES, LANES), F32)] * 2,
        compiler_params=_params("parallel", "arbitrary"))(qkv_act, qkv_act, qkv_act, braw, araw, alog, dtb, inv,
                                                         du, dw, dqd, dkd, dattn, degl)


def _scan_specs(tb, heads, chunk, rev, nt):
    ti = (lambda i: nt - 1 - i) if rev else (lambda i: i)
    row = pl.BlockSpec((tb, heads * HEAD_DIM), lambda i: (ti(i), 0))
    att = pl.BlockSpec((heads, tb, chunk), lambda i: (0, ti(i), 0))
    egl = pl.BlockSpec((heads, tb // chunk, SUBLANES, LANES), lambda i: (0, ti(i), 0, 0))
    hist = pl.BlockSpec((heads, tb // chunk, HEAD_DIM, HEAD_DIM), lambda i: (0, ti(i), 0, 0))
    gn = pl.BlockSpec((SUBLANES, LANES), lambda i: (0, 0))
    return row, att, egl, hist, gn


def _gdn_scan_fwd(u, w, qd, kd, attn, egl, zb, gn):
    t = u.shape[0]
    tb = min(t, 256)
    nt, nc = t // tb, tb // GDN_CHUNK
    nh = GDN_HEADS

    def body(u_ref, w_ref, qd_ref, kd_ref, at_ref, eg_ref, z_ref, gn_ref, y_ref, hist_ref, s_ref):
        @pl.when(pl.program_id(0) == 0)
        def _():
            s_ref[...] = jnp.zeros_like(s_ref)

        g = gn_ref[0:1, :]
        state = s_ref[...]
        for c in range(nc):
            rows = pl.ds(c * GDN_CHUNK, GDN_CHUNK)
            heads = lambda r: _by_head(r, rows, nh)
            hist_ref[:, c] = state
            y, state = _gdn_scan(_mm_raw, heads(u_ref), heads(w_ref), heads(qd_ref), heads(kd_ref), at_ref[:, rows, :],
                                 eg_ref[:, c, 0:1, :], heads(z_ref), g, state)
            _store_heads(y_ref, rows, y.astype(BF16))
        s_ref[...] = state

    row, att, egs, hist, gns = _scan_specs(tb, nh, GDN_CHUNK, False, nt)
    return pl.pallas_call(
        body, name="gdn_scan_fwd", grid=(nt,), in_specs=[row, row, row, row, att, egs, row, gns], out_specs=[row, hist],
        out_shape=[jax.ShapeDtypeStruct((t, nh * HEAD_DIM), BF16),
                   jax.ShapeDtypeStruct((nh, t // GDN_CHUNK, HEAD_DIM, HEAD_DIM), F32)],
        scratch_shapes=[pltpu.VMEM((nh, HEAD_DIM, HEAD_DIM), F32)],
        compiler_params=_params("arbitrary"))(u, w, qd, kd, attn, egl, zb, gn)


def _gdn_scan_bwd(u, w, qd, kd, attn, egl, zb, gn, hist, dy):
    t = u.shape[0]
    tb = min(t, 256)
    nt, nc = t // tb, tb // GDN_CHUNK
    nh = GDN_HEADS

    def body(u_ref, w_ref, qd_ref, kd_ref, at_ref, eg_ref, z_ref, gn_ref, hist_ref, dy_ref,
             du_ref, dw_ref, dqd_ref, dkd_ref, dat_ref, deg_ref, dz_ref, dgn_ref, ds_ref):
        @pl.when(pl.program_id(0) == 0)
        def _():
            ds_ref[...] = jnp.zeros_like(ds_ref)
            dgn_ref[...] = jnp.zeros_like(dgn_ref)

        g = gn_ref[0:1, :]
        d_state = ds_ref[...]
        for c in reversed(range(nc)):
            rows = pl.ds(c * GDN_CHUNK, GDN_CHUNK)
            heads = lambda r: _by_head(r, rows, nh).astype(F32)
            _, vjp = jax.vjp(functools.partial(_gdn_scan, _mm_vjp), heads(u_ref), heads(w_ref), heads(qd_ref),
                             heads(kd_ref), at_ref[:, rows, :].astype(F32), eg_ref[:, c, 0:1, :], heads(z_ref), g,
                             hist_ref[:, c])
            du, dw, dqd, dkd, dat, deg, dz, dgn, d_state = vjp((heads(dy_ref), d_state))
            _store_heads(du_ref, rows, du)
            _store_heads(dw_ref, rows, dw)
            _store_heads(dqd_ref, rows, dqd)
            _store_heads(dkd_ref, rows, dkd)
            dat_ref[:, rows, :] = dat
            deg_ref[:, c] = jnp.broadcast_to(deg, (nh, SUBLANES, LANES))
            _store_heads(dz_ref, rows, dz.astype(BF16))
            dgn_ref[0:1, :] += dgn
        ds_ref[...] = d_state

    row, att, egs, hists, gns = _scan_specs(tb, nh, GDN_CHUNK, True, nt)
    wide = jax.ShapeDtypeStruct((t, nh * HEAD_DIM), F32)
    return pl.pallas_call(
        body, name="gdn_scan_bwd", grid=(nt,),
        in_specs=[row, row, row, row, att, egs, row, gns, hists, row],
        out_specs=[row, row, row, row, att, egs, row, gns],
        out_shape=[wide] * 4 + [jax.ShapeDtypeStruct((nh, t, GDN_CHUNK), F32),
                                jax.ShapeDtypeStruct((nh, t // GDN_CHUNK, SUBLANES, LANES), F32),
                                jax.ShapeDtypeStruct((t, nh * HEAD_DIM), BF16),
                                jax.ShapeDtypeStruct((SUBLANES, LANES), F32)],
        scratch_shapes=[pltpu.VMEM((nh, HEAD_DIM, HEAD_DIM), F32)],
        compiler_params=_params("arbitrary"))(u, w, qd, kd, attn, egl, zb, gn, hist, dy)


def _hgrn_prep(mm, qr, fr, lbl):
    n, c, _ = qr.shape
    lb = jax.nn.sigmoid(lbl[1:2, :] - lbl[0:1, :])
    f = lb + (1.0 - lb) * jax.nn.sigmoid(fr)
    q = jax.nn.silu(qr)
    k = 1.0 - f
    logf = jnp.log(f)
    ri = lax.broadcasted_iota(jnp.int32, (n, c, c), 1)
    ci = lax.broadcasted_iota(jnp.int32, (n, c, c), 2)
    b = mm((ri >= ci).astype(F32), logf, "nn", EXACT_LHS)
    attn = _hgrn_attn(mm, q, k, b)
    b_last = jnp.sum(logf, axis=1, keepdims=True)
    return q * jnp.exp(b), k * jnp.exp(b_last - b), attn, jnp.exp(b_last)


HGRN_SUB = 8
HGRN_PREP_ROWS = 2048


@functools.partial(jax.custom_vjp, nondiff_argnums=(1,))
def _roll_rows(x, shift):
    return pltpu.roll(x, shift, x.ndim - 2)


def _roll_rows_fwd(x, shift):
    return _roll_rows(x, shift), None


def _roll_rows_bwd(shift, _, d):
    return (pltpu.roll(d, d.shape[-2] - shift, d.ndim - 2),)


_roll_rows.defvjp(_roll_rows_fwd, _roll_rows_bwd)


@jax.custom_vjp
def _exp_clamped(v):
    return jnp.exp(jnp.minimum(v, 0.0))


def _exp_clamped_fwd(v):
    out = jnp.exp(jnp.minimum(v, 0.0))
    return out, out


def _exp_clamped_bwd(out, d):
    return (d * out,)


_exp_clamped.defvjp(_exp_clamped_fwd, _exp_clamped_bwd)


def _hgrn_attn(mm, q, k, b):
    n, c, d = q.shape
    sb = HGRN_SUB
    sub = lambda a: a.reshape(n * c // sb, sb, d)
    qs, ks, bs = sub(q), sub(k), sub(b)
    row = lax.broadcasted_iota(jnp.int32, (n, c, c), 1)
    col = lax.broadcasted_iota(jnp.int32, (n, c, c), 2)
    same_block = (row & -sb) == (col & -sb)
    attn = None
    for delta in range(sb):
        if delta == 0:
            prod = qs * ks
        else:
            prod = qs * _roll_rows(ks, delta) * _exp_clamped(bs - _roll_rows(bs, delta))
        sums = jnp.sum(prod, axis=-1, keepdims=True).reshape(n, c, 1)
        term = jnp.where(same_block & (row - col == delta), sums, 0.0)
        attn = term if attn is None else attn + term
    far = [jnp.zeros((n, sb, c), F32)]
    col8 = lax.broadcasted_iota(jnp.int32, (n, sb, c), 2)
    for i in range(1, c // sb):
        r0 = i * sb
        bi = b[:, r0:r0 + sb, :]
        ref = bi[:, 0:1, :]
        part = mm(q[:, r0:r0 + sb, :] * jnp.exp(bi - ref), k * _exp_clamped(ref - b), "nt", ONE_PASS)
        far.append(jnp.where(col8 < r0, part, 0.0))
    return attn + jnp.concatenate(far, axis=1)


def _hgrn_scan(mm, qe, kd, attn, ebl, iv, z, gn, state):
    o = mm(qe, state, "nt", ONE_PASS) + mm(attn, iv, "nn", ONE_PASS)
    new_state = state * ebl + mm(iv, kd, "tn", ONE_PASS)
    return _rms_gate(o, gn, z), new_state


def _hgrn_prep_specs(tb):
    col = pl.BlockSpec((tb, HEAD_DIM), lambda h, i: (i, h))
    lbs = pl.BlockSpec((2, HEAD_DIM), lambda h, i: (0, h))
    att = pl.BlockSpec((1, tb, HGRN_CHUNK), lambda h, i: (h, i, 0))
    ebl = pl.BlockSpec((1, tb // HGRN_CHUNK, SUBLANES, LANES), lambda h, i: (h, i, 0, 0))
    return col, lbs, att, ebl


def _hgrn_prep_fwd(qr, fr, lower_bounds):
    t = qr.shape[0]
    tb = min(t, HGRN_PREP_ROWS)
    nt, nc = t // tb, tb // HGRN_CHUNK
    hh = HGRN_HEADS

    def body(q_ref, f_ref, lb_ref, qe_ref, kd_ref, at_ref, eb_ref):
        ch = lambda r: _chunks(r, nc, HGRN_CHUNK)
        qe, kd, attn, ebl = _hgrn_prep(_mm_raw, ch(q_ref[...]), ch(f_ref[...]), lb_ref[...])
        qe_ref[...] = qe.reshape(tb, HEAD_DIM).astype(BF16)
        kd_ref[...] = kd.reshape(tb, HEAD_DIM).astype(BF16)
        at_ref[0] = attn.reshape(tb, HGRN_CHUNK).astype(BF16)
        eb_ref[0] = jnp.broadcast_to(ebl, (nc, SUBLANES, LANES))

    col, lbs, att, ebs = _hgrn_prep_specs(tb)
    return pl.pallas_call(
        body, name="hgrn_prep_fwd", grid=(hh, nt), in_specs=[col, col, lbs], out_specs=[col, col, att, ebs],
        out_shape=[jax.ShapeDtypeStruct((t, HGRN_WIDTH), BF16)] * 2
        + [jax.ShapeDtypeStruct((hh, t, HGRN_CHUNK), BF16), jax.ShapeDtypeStruct((hh, t // HGRN_CHUNK, SUBLANES, LANES), F32)],
        compiler_params=_params("parallel", "parallel"))(qr, fr, lower_bounds)


def _hgrn_prep_bwd(qr, fr, lower_bounds, dqe, dkd, dattn, debl):
    t = qr.shape[0]
    tb = min(t, HGRN_PREP_ROWS)
    nt, nc = t // tb, tb // HGRN_CHUNK
    hh = HGRN_HEADS

    def body(q_ref, f_ref, lb_ref, dqe_ref, dkd_ref, dat_ref, deb_ref, dq_ref, df_ref, dlb_ref):
        @pl.when(pl.program_id(1) == 0)
        def _():
            dlb_ref[...] = jnp.zeros_like(dlb_ref)

        ch = lambda r: _chunks(r, nc, HGRN_CHUNK)
        _, vjp = jax.vjp(functools.partial(_hgrn_prep, _mm_vjp), ch(q_ref[...]), ch(f_ref[...]), lb_ref[...])
        dq, df, dlb = vjp((ch(dqe_ref[...]), ch(dkd_ref[...]), ch(dat_ref[0]), deb_ref[0][:, 0:1, :]))
        dq_ref[...] = dq.reshape(tb, HEAD_DIM).astype(BF16)
        df_ref[...] = df.reshape(tb, HEAD_DIM).astype(BF16)
        dlb_ref[...] += dlb

    col, lbs, att, ebs = _hgrn_prep_specs(tb)
    return pl.pallas_call(
        body, name="hgrn_prep_bwd", grid=(hh, nt), in_specs=[col, col, lbs, col, col, att, ebs],
        out_specs=[col, col, lbs],
        out_shape=[jax.ShapeDtypeStruct((t, HGRN_WIDTH), BF16)] * 2 + [jax.ShapeDtypeStruct((2, HGRN_WIDTH), F32)],
        compiler_params=_params("parallel", "arbitrary"))(qr, fr, lower_bounds, dqe, dkd, dattn, debl)


def _hgrn_scan_fwd(qe, kd, attn, ebl, iv, z, gn):
    t = qe.shape[0]
    tb = min(t, 128)
    nt, nc = t // tb, tb // HGRN_CHUNK
    hh = HGRN_HEADS

    def body(qe_ref, kd_ref, at_ref, eb_ref, i_ref, z_ref, gn_ref, y_ref, hist_ref, s_ref):
        @pl.when(pl.program_id(0) == 0)
        def _():
            s_ref[...] = jnp.zeros_like(s_ref)

        g = gn_ref[0:1, :]
        state = s_ref[...]
        for c in range(nc):
            rows = pl.ds(c * HGRN_CHUNK, HGRN_CHUNK)
            heads = lambda r: _by_head(r, rows, hh)
            hist_ref[:, c] = state
            y, state = _hgrn_scan(_mm_raw, heads(qe_ref), heads(kd_ref), at_ref[:, rows, :], eb_ref[:, c, 0:1, :],
                                  heads(i_ref), heads(z_ref), g, state)
            _store_heads(y_ref, rows, y.astype(BF16))
        s_ref[...] = state

    row, att, ebs, hist, gns = _scan_specs(tb, hh, HGRN_CHUNK, False, nt)
    return pl.pallas_call(
        body, name="hgrn_scan_fwd", grid=(nt,), in_specs=[row, row, att, ebs, row, row, gns], out_specs=[row, hist],
        out_shape=[jax.ShapeDtypeStruct((t, HGRN_WIDTH), BF16),
                   jax.ShapeDtypeStruct((hh, t // HGRN_CHUNK, HEAD_DIM, HEAD_DIM), F32)],
        scratch_shapes=[pltpu.VMEM((hh, HEAD_DIM, HEAD_DIM), F32)],
        compiler_params=_params("arbitrary"))(qe, kd, attn, ebl, iv, z, gn)


def _hgrn_scan_bwd(qe, kd, attn, ebl, iv, z, gn, hist, dy):
    t = qe.shape[0]
    tb = min(t, 128)
    nt, nc = t // tb, tb // HGRN_CHUNK
    hh = HGRN_HEADS

    def body(qe_ref, kd_ref, at_ref, eb_ref, i_ref, z_ref, gn_ref, hist_ref, dy_ref,
             dqe_ref, dkd_ref, dat_ref, deb_ref, di_ref, dz_ref, dgn_ref, ds_ref):
        @pl.when(pl.program_id(0) == 0)
        def _():
            ds_ref[...] = jnp.zeros_like(ds_ref)
            dgn_ref[...] = jnp.zeros_like(dgn_ref)

        g = gn_ref[0:1, :]
        d_state = ds_ref[...]
        for c in reversed(range(nc)):
            rows = pl.ds(c * HGRN_CHUNK, HGRN_CHUNK)
            heads = lambda r: _by_head(r, rows, hh).astype(F32)
            _, vjp = jax.vjp(functools.partial(_hgrn_scan, _mm_vjp), heads(qe_ref), heads(kd_ref),
                             at_ref[:, rows, :].astype(F32), eb_ref[:, c, 0:1, :], heads(i_ref), heads(z_ref), g,
                             hist_ref[:, c])
            dqe, dkd, dat, deb, di, dz, dgn, d_state = vjp((heads(dy_ref), d_state))
            _store_heads(dqe_ref, rows, dqe)
            _store_heads(dkd_ref, rows, dkd)
            dat_ref[:, rows, :] = dat
            deb_ref[:, c] = jnp.broadcast_to(deb, (hh, SUBLANES, LANES))
            _store_heads(di_ref, rows, di.astype(BF16))
            _store_heads(dz_ref, rows, dz.astype(BF16))
            dgn_ref[0:1, :] += dgn
        ds_ref[...] = d_state

    row, att, ebs, hists, gns = _scan_specs(tb, hh, HGRN_CHUNK, True, nt)
    wide = lambda dt: jax.ShapeDtypeStruct((t, HGRN_WIDTH), dt)
    return pl.pallas_call(
        body, name="hgrn_scan_bwd", grid=(nt,),
        in_specs=[row, row, att, ebs, row, row, gns, hists, row],
        out_specs=[row, row, att, ebs, row, row, gns],
        out_shape=[wide(F32), wide(F32), jax.ShapeDtypeStruct((hh, t, HGRN_CHUNK), F32),
                   jax.ShapeDtypeStruct((hh, t // HGRN_CHUNK, SUBLANES, LANES), F32), wide(BF16), wide(BF16),
                   jax.ShapeDtypeStruct((SUBLANES, LANES), F32)],
        scratch_shapes=[pltpu.VMEM((hh, HEAD_DIM, HEAD_DIM), F32)],
        compiler_params=_params("arbitrary"))(qe, kd, attn, ebl, iv, z, gn, hist, dy)


def _layer_norm(pre, g, b):
    mu = jnp.mean(pre, axis=-1, keepdims=True)
    d = pre - mu
    var = jnp.mean(d * d, axis=-1, keepdims=True)
    return d * lax.rsqrt(var + NORM_EPS) * g + b


def _lnpl_fwd(xin, s, p, wg, wpl, ln_g, ln_b):
    t = xin.shape[0]
    tt = min(t, 256)

    def body(x_ref, s_ref, p_ref, wg_ref, wpl_ref, g_ref, b_ref, o_ref, ob_ref):
        xn = _layer_norm(DEEPNORM_ALPHA * x_ref[...] + s_ref[...], g_ref[...], b_ref[...])
        gate = jax.nn.sigmoid(_mm_raw(xn, wg_ref[...], "nn", False))
        out = xn + _mm_raw(p_ref[...], wpl_ref[...], "nn", False) * gate
        o_ref[...] = out
        ob_ref[...] = out.astype(BF16)

    row = lambda w: pl.BlockSpec((tt, w), lambda i: (i, 0))
    full = lambda a: pl.BlockSpec(a.shape, lambda i: (0, 0))
    return pl.pallas_call(
        body, name="lnpl_fwd", grid=(t // tt,),
        in_specs=[row(D_MODEL), row(D_MODEL), row(PL_DIM), full(wg), full(wpl), full(ln_g), full(ln_b)],
        out_specs=[row(D_MODEL), row(D_MODEL)],
        out_shape=[jax.ShapeDtypeStruct((t, D_MODEL), F32), jax.ShapeDtypeStruct((t, D_MODEL), BF16)],
        compiler_params=_params("parallel"))(xin, s, p, wg, wpl, ln_g, ln_b)


def _lnpl_bwd(xin, s, p, wg, wpl, ln_g, ln_b, upstream, last, name):
    t = xin.shape[0]
    tt = min(t, 256)

    def body(x_ref, s_ref, p_ref, wg_ref, wpl_ref, g_ref, b_ref, up_ref,
             dpre_ref, dwg_ref, dwpl_ref, dg_ref, db_ref, loss_ref):
        @pl.when(pl.program_id(0) == 0)
        def _():
            for r in (dwg_ref, dwpl_ref, dg_ref, db_ref, loss_ref):
                r[...] = jnp.zeros_like(r)

        pre = DEEPNORM_ALPHA * x_ref[...] + s_ref[...]
        xn, ln_vjp = jax.vjp(_layer_norm, pre, g_ref[...], b_ref[...])
        gate = jax.nn.sigmoid(_mm_raw(xn, wg_ref[...], "nn", False))
        plv = _mm_raw(p_ref[...], wpl_ref[...], "nn", False)
        if last:
            err = xn + plv * gate - up_ref[...]
            dout = err * (1.0 / D_MODEL)
            tot = jnp.sum(jnp.sum(err * err, axis=1, keepdims=True), axis=0, keepdims=True) * (0.5 / D_MODEL)
            loss_ref[...] += jnp.broadcast_to(tot, loss_ref.shape)
        else:
            dout = up_ref[...]
        dplv = dout * gate
        dlogits = dout * plv * gate * (1.0 - gate)
        dwg_ref[...] += _mm_raw(xn, dlogits, "tn", False)
        dwpl_ref[...] += _mm_raw(p_ref[...], dplv, "tn", False)
        dxn = dout + _mm_raw(dlogits, wg_ref[...], "nt", False)
        dpre, dg, db = ln_vjp(dxn)
        dpre_ref[...] = dpre
        dg_ref[...] += dg
        db_ref[...] += db

    row = lambda w: pl.BlockSpec((tt, w), lambda i: (i, 0))
    full = lambda shape: pl.BlockSpec(shape, lambda i: (0, 0))
    return pl.pallas_call(
        body, name=name, grid=(t // tt,),
        in_specs=[row(D_MODEL), row(D_MODEL), row(PL_DIM), full(wg.shape), full(wpl.shape), full(ln_g.shape),
                  full(ln_b.shape), row(D_MODEL)],
        out_specs=[row(D_MODEL), full(wg.shape), full(wpl.shape), full(ln_g.shape), full(ln_b.shape),
                   full((SUBLANES, LANES))],
        out_shape=[jax.ShapeDtypeStruct((t, D_MODEL), F32), jax.ShapeDtypeStruct(wg.shape, F32),
                   jax.ShapeDtypeStruct(wpl.shape, F32), jax.ShapeDtypeStruct(ln_g.shape, F32),
                   jax.ShapeDtypeStruct(ln_b.shape, F32), jax.ShapeDtypeStruct((SUBLANES, LANES), F32)],
        compiler_params=_params("arbitrary"))(xin, s, p, wg, wpl, ln_g, ln_b, upstream)


def _pack_tail(dbr, dar):
    nh, t, _ = dbr.shape
    tt = min(t, 512)

    def body(b_ref, a_ref, o_ref):
        lane = lax.broadcasted_iota(jnp.int32, (tt, LANES), 1)
        acc = jnp.zeros((tt, LANES), F32)
        for h in range(nh):
            acc = jnp.where(lane == h, b_ref[h], acc)
            acc = jnp.where(lane == nh + h, a_ref[h], acc)
        o_ref[...] = acc.astype(BF16)

    spec = pl.BlockSpec((nh, tt, LANES), lambda i: (0, i, 0))
    return pl.pallas_call(
        body, name="pack_tail", grid=(t // tt,), in_specs=[spec, spec], out_specs=pl.BlockSpec((tt, LANES), lambda i: (i, 0)),
        out_shape=jax.ShapeDtypeStruct((t, LANES), BF16), compiler_params=_params("parallel"))(dbr, dar)


def _rep_rows(v):
    return jnp.broadcast_to(v.reshape(1, LANES), (SUBLANES, LANES))


def _rep_heads(v):
    return jnp.broadcast_to(v.reshape(-1, 1, 1), (v.shape[0], SUBLANES, LANES))


def _col_range(stacked, lo, hi):
    c = stacked.shape[2]
    parts = [stacked[s, :, max(lo, s * c) - s * c:min(hi, (s + 1) * c) - s * c]
             for s in range(4) if max(lo, s * c) < min(hi, (s + 1) * c)]
    return parts[0] if len(parts) == 1 else jnp.concatenate(parts, axis=1)


def _col_shards(pieces, c):
    shards, offs, o = [], [], 0
    for pc in pieces:
        offs.append(o)
        o += pc.shape[1]
    for s in range(4):
        lo, hi = s * c, (s + 1) * c
        parts = [pc[:, max(lo, o) - o:min(hi, o + pc.shape[1]) - o] for pc, o in zip(pieces, offs)
                 if max(lo, o) < min(hi, o + pc.shape[1])]
        shards.append(parts[0] if len(parts) == 1 else jnp.concatenate(parts, axis=1))
    return jnp.stack(shards)


def _local_step(x, p, target, w, late_weights, early_grads_ready, last_grad_ready, start_token):
    a = DEEPNORM_ALPHA
    nh = GDN_HEADS
    xb = (x + start_token).astype(BF16)
    wie = w["w_in_even"]
    w_a, w_qkv, w_zb = _col_range(wie, 0, 4096), _col_range(wie, 4096, 7168), _col_range(wie, 7168, 8192)
    w_tail = jnp.pad(_col_range(wie, 8192, 8192 + 2 * nh), ((0, 0), (0, LANES - 2 * nh)))
    conv_a_w, conv_b_w = w["conv_a_w"], w["conv_b_w"]
    ln_g0, ln_b0, ln_g1, ln_b1 = (v.reshape(1, D_MODEL) for v in (w["ln_g"][0], w["ln_b"][0], w["ln_g"][1], w["ln_b"][1]))
    alog, dtb = _rep_heads(w["a_log"].reshape(nh)), _rep_heads(w["dt_bias"].reshape(nh))
    gdn_g, hgrn_g = _rep_rows(w["gdn_norm_g"]), _rep_rows(w["hgrn_norm_g"])

    proj_a = _matmul(xb, w_a, name="fwd_proj_a")
    proj_qkv = _matmul(xb, w_qkv, name="fwd_proj_qkv")
    proj_zb = _matmul(xb, w_zb, name="fwd_proj_zb")
    proj_tail = _matmul(xb, w_tail, name="fwd_proj_tail")
    rep = lambda cols: jnp.broadcast_to(cols.T[:, :, None], (nh, cols.shape[0], LANES))
    braw, araw = rep(proj_tail[:, :nh]), rep(proj_tail[:, nh:2 * nh])
    y_a = _conv_a_fwd(proj_a, conv_a_w)
    qkv_act = _conv_b_fwd(proj_qkv, conv_b_w)
    *gdn_pre, gdn_inv = _gdn_prep_fwd(qkv_act, braw, araw, alog, dtb)
    y_b, gdn_hist = _gdn_scan_fwd(*gdn_pre, proj_zb, gdn_g)
    w = {**w, **late_weights(y_b)}
    woe, wio, woo = w["w_out_even"], w["w_in_odd"], w["w_out_odd"]
    s0 = _matmul(y_b, woe[1024:], name="fwd_out_even_b", add=_matmul(y_a, woe[:1024], name="fwd_out_even_a"))
    x1, x1b = _lnpl_fwd(x, s0, p[0], w["w_pl_gate"][0], w["w_pl"][0], ln_g0, ln_b0)
    proj_o = [_matmul(x1b, wio[j], name=f"fwd_proj_odd{j}") for j in range(4)]
    hgrn_pre = _hgrn_prep_fwd(proj_o[0], proj_o[1], w["lower_bounds"])
    y_o, hgrn_hist = _hgrn_scan_fwd(*hgrn_pre, proj_o[2], proj_o[3], hgrn_g)
    s1 = _matmul(y_o, woo, name="fwd_out_odd")

    g = {}
    dpre1, dwg1, dwpl1, dlng1, dlnb1, loss = _lnpl_bwd(x1, s1, p[1], w["w_pl_gate"][1], w["w_pl"][1], ln_g1, ln_b1,
                                                     target, True, "lnpl_bwd_odd")
    dy_o = _matmul(dpre1, woo, tb=True, name="bwd_out_odd_dx")
    g["w_out_odd"] = _matmul(y_o, dpre1, ta=True, name="bwd_out_odd_dw")
    dqe, dkd, dat, deb, di, dz, dhg = _hgrn_scan_bwd(*hgrn_pre, proj_o[2], proj_o[3], hgrn_g, hgrn_hist, dy_o)
    dq, df, dlb = _hgrn_prep_bwd(proj_o[0], proj_o[1], w["lower_bounds"], dqe, dkd, dat, deb)
    dx1 = dpre1
    scale = a
    dws = []
    for j, dj in enumerate((dq, df, di, dz)):
        dx1 = _matmul(dj, wio[j], tb=True, add=dx1, add_scale=scale, name=f"bwd_proj_odd_dx{j}")
        scale = 1.0
        dws.append(_matmul(x1b, dj, ta=True, name=f"bwd_proj_odd_dw{j}"))
    g["w_in_odd"] = jnp.stack(dws)
    g["hgrn_norm_g"] = dhg[0:1]
    g["lower_bounds"] = dlb
    g["w_pl_gate1"], g["w_pl1"] = dwg1, dwpl1

    dpre0, dwg0, dwpl0, dlng0, dlnb0, _ = _lnpl_bwd(x, s0, p[0], w["w_pl_gate"][0], w["w_pl"][0], ln_g0, ln_b0,
                                                  dx1, False, "lnpl_bwd_even")
    g["w_pl_gate0"], g["w_pl0"] = dwg0, dwpl0
    g["ln_g"] = jnp.concatenate([dlng0, dlng1], axis=0)
    g["ln_b"] = jnp.concatenate([dlnb0, dlnb1], axis=0)
    dy_a = _matmul(dpre0, woe[:1024], tb=True, name="bwd_out_even_dxa")
    dy_b = _matmul(dpre0, woe[1024:], tb=True, name="bwd_out_even_dxb")
    g["w_out_even"] = jnp.concatenate([_matmul(y_a, dpre0, ta=True, name="bwd_out_even_dwa"),
                                       _matmul(y_b, dpre0, ta=True, name="bwd_out_even_dwb")], axis=0)
    token = early_grads_ready({n: g[n] for n in _RS_EARLY})
    conv_a_w, gdn_g = conv_a_w + token, gdn_g + token
    du, dw, dqd, dkd, dat, deg, dzb, dgn = _gdn_scan_bwd(*gdn_pre, proj_zb, gdn_g, gdn_hist, dy_b)
    dqa, dka, dva, dbr, dar, dal, ddt = _gdn_prep_bwd(qkv_act, braw, araw, alog, dtb, gdn_inv, du, dw, dqd, dkd, dat, deg)
    g["a_log"] = dal[:, 0, 0].reshape(1, nh)
    g["dt_bias"] = ddt[:, 0, 0].reshape(1, nh)
    g["gdn_norm_g"] = dgn[0:1]
    d_pre_qkv, dwb = [], []
    for j, dj in enumerate((dqa, dka, dva)):
        dpj, dwj = _conv_b_bwd(proj_qkv, conv_b_w, dj, j, f"conv_b_bwd{j}")
        d_pre_qkv.append(dpj)
        dwb.append(dwj[:4])
    g["conv_b_w"] = jnp.concatenate(dwb, axis=1)
    d_a, dwa = _conv_a_bwd(proj_a, conv_a_w, dy_a)
    g["conv_a_w"] = dwa[:3]
    d_tail = _pack_tail(dbr, dar)
    pieces = [(d_a, w_a), (d_pre_qkv[0], w_qkv[:, :1024]), (d_pre_qkv[1], w_qkv[:, 1024:2048]),
              (d_pre_qkv[2], w_qkv[:, 2048:]), (dzb, w_zb), (d_tail, w_tail)]
    dws = [_matmul(xb, dj, ta=True, name=f"bwd_proj_even_dw{j}") for j, (dj, _) in enumerate(pieces)]
    dws[-1] = dws[-1][:, :2 * nh]
    g["w_in_even"] = _col_shards(dws, wie.shape[2])
    token = last_grad_ready(g["w_in_even"])
    dx = dpre0
    scale = a
    for j, (dj, wj) in enumerate(pieces):
        dx = _matmul(dj, wj + jnp.asarray(token).astype(BF16) if j == 0 else wj, tb=True, add=dx, add_scale=scale,
                     name=f"bwd_proj_even_dx{j}")
        scale = 1.0
    return loss, dx, g


def _adamw(w, g, m, v, name):
    lead, rows, cols = w.shape
    if rows % SUBLANES == 0 or rows <= 256:
        tr, tc = (rows if rows <= 256 else 256), cols
    else:
        tr, tc = rows, 256
    assert rows % tr == 0 and cols % tc == 0, (name, rows, cols)

    def body(w_ref, g_ref, m_ref, v_ref, d_ref, nm_ref, nv_ref):
        gg = g_ref[...]
        nm = ADAM_B1 * m_ref[...] + (1.0 - ADAM_B1) * gg
        nv = ADAM_B2 * v_ref[...] + (1.0 - ADAM_B2) * jnp.square(gg)
        m_hat = nm / (1.0 - ADAM_B1 ** ADAM_STEP)
        v_hat = nv / (1.0 - ADAM_B2 ** ADAM_STEP)
        d_ref[...] = -ADAM_LR * (m_hat / (jnp.sqrt(v_hat) + ADAM_EPS) + ADAM_WD * w_ref[...])
        nm_ref[...] = nm
        nv_ref[...] = nv

    spec = pl.BlockSpec((1, tr, tc), lambda l, i, j: (l, i, j))
    return pl.pallas_call(
        body, name=name, grid=(lead, rows // tr, cols // tc), in_specs=[spec] * 4, out_specs=[spec] * 3,
        out_shape=[jax.ShapeDtypeStruct(w.shape, F32)] * 3,
        compiler_params=_params("parallel", "parallel", "parallel"))(w, g, m, v)


MESH = pl.DeviceIdType.MESH
N_DEV = 8
HBM_SPEC = pl.BlockSpec(memory_space=pltpu.HBM)
VMEM_SPEC = pl.BlockSpec(memory_space=pltpu.VMEM)


def _coords():
    return lax.axis_index("x"), lax.axis_index("y"), lax.axis_index("c")


def _flip(v, bit):
    return 1 - v if bit else v


def _remote(src, dst, send_sem, recv_sem, dev):
    return pltpu.make_async_remote_copy(src_ref=src, dst_ref=dst, send_sem=send_sem, recv_sem=recv_sem,
                                        device_id=dev, device_id_type=MESH)


def _exchange_small(buf, reduce, name):
    rows = buf.shape[0]

    def body(in_ref, out_ref, slots, send_sems, recv_sems):
        x, y, c = _coords()
        me = 4 * x + 2 * y + c
        slots[me] = in_ref[...]
        peer = lambda k: (_flip(x, (k >> 2) & 1), _flip(y, (k >> 1) & 1), _flip(c, k & 1))
        sends = []
        for k in range(1, N_DEV):
            cp = _remote(in_ref, slots.at[me], send_sems.at[k - 1], recv_sems.at[k - 1], peer(k))
            cp.start()
            sends.append(cp)
        for k in range(1, N_DEV):
            px, py, pc = peer(k)
            _remote(in_ref, slots.at[4 * px + 2 * py + pc], send_sems.at[k - 1], recv_sems.at[k - 1], peer(k)).wait_recv()
        for cp in sends:
            cp.wait_send()
        if reduce:
            acc = slots[0]
            for d in range(1, N_DEV):
                acc = acc + slots[d]
            out_ref[...] = acc
        else:
            out_ref[...] = slots[...]

    out_shape = (rows, LANES) if reduce else (N_DEV, rows, LANES)
    return pl.pallas_call(
        body, name=name, in_specs=[VMEM_SPEC], out_specs=VMEM_SPEC, out_shape=jax.ShapeDtypeStruct(out_shape, F32),
        scratch_shapes=[pltpu.VMEM((N_DEV, rows, LANES), F32), pltpu.SemaphoreType.DMA((N_DEV - 1,)),
                        pltpu.SemaphoreType.DMA((N_DEV - 1,))])(buf)


def _half_rows(half, which):
    return pl.ds(pl.multiple_of(which * half, 16), half)


def _other_chip(x, y, k):
    return _flip(x, (k >> 1) & 1), _flip(y, k & 1)


SEM_SPEC = pl.BlockSpec(memory_space=pltpu.SEMAPHORE)
DATAFLOW = pltpu.SideEffectType.DATAFLOW_SIDE_EFFECTING


def _ici_piece(srcs, lands, send_sems, recv_sems, i, k, x, y, c):
    half = srcs[i].shape[0] // 2
    ox, oy = _other_chip(x, y, k)
    return _remote(srcs[i].at[_half_rows(half, c)], lands[i].at[2 * x + y, _half_rows(half, c)],
                   send_sems.at[3 * i + k - 1], recv_sems.at[3 * i + k - 1], (ox, oy, c)), (ox, oy)


def _gather_start(shards, name):
    n = len(shards)

    def body(*refs):
        srcs, lands = refs[:n], refs[n:2 * n]
        send_sems, recv_sems = refs[2 * n], refs[2 * n + 1]
        token = refs[-1]
        x, y, c = _coords()
        for i in range(n):
            for k in (1, 2, 3):
                _ici_piece(srcs, lands, send_sems, recv_sems, i, k, x, y, c)[0].start()
        token[...] = jnp.zeros_like(token)

    hbm = lambda a: pltpu.with_memory_space_constraint(a, pltpu.HBM)
    lands = [lax.empty((4,) + s.shape, s.dtype) for s in shards]
    out = pl.pallas_call(
        body, name=name,
        out_shape=(pltpu.SemaphoreType.DMA((3 * n,)), pltpu.SemaphoreType.DMA((3 * n,)),
                   *[pltpu.HBM(s.shape, s.dtype) for s in shards], *[pltpu.HBM(a.shape, a.dtype) for a in lands],
                   jax.ShapeDtypeStruct((SUBLANES, LANES), F32)),
        in_specs=[HBM_SPEC] * (2 * n), out_specs=(SEM_SPEC, SEM_SPEC, *[HBM_SPEC] * (2 * n), VMEM_SPEC),
        input_output_aliases={i: 2 + i for i in range(2 * n)},
        compiler_params=pltpu.CompilerParams(has_side_effects=DATAFLOW))(*[hbm(s) for s in shards], *[hbm(a) for a in lands])
    return out[0], out[1], out[2:2 + n], out[2 + n:2 + 2 * n], out[-1]


def _gather_wait(send_sems, recv_sems, srcs, lands, after, name):
    n = len(srcs)

    def body(*refs):
        src_refs, land_refs = refs[:n], refs[n:2 * n]
        send_sems, recv_sems = refs[2 * n], refs[2 * n + 1]
        x, y, c = _coords()
        for i in range(n):
            half = src_refs[i].shape[0] // 2
            for k in (1, 2, 3):
                cp, (ox, oy) = _ici_piece(src_refs, land_refs, send_sems, recv_sems, i, k, x, y, c)
                cp.wait_send()
                piece = land_refs[i].at[2 * ox + oy, _half_rows(half, c)]
                _remote(piece, piece, send_sems.at[3 * i + k - 1], recv_sems.at[3 * i + k - 1], (ox, oy, c)).wait_recv()

    out = pl.pallas_call(
        body, name=name,
        out_shape=(*[pltpu.HBM(s.shape, s.dtype) for s in srcs], *[pltpu.HBM(a.shape, a.dtype) for a in lands]),
        in_specs=[HBM_SPEC] * (2 * n) + [SEM_SPEC, SEM_SPEC, pl.BlockSpec(memory_space=pl.ANY)],
        out_specs=tuple([HBM_SPEC] * (2 * n)), input_output_aliases={i: i for i in range(2 * n)},
        compiler_params=pltpu.CompilerParams(has_side_effects=DATAFLOW))(*srcs, *lands, send_sems, recv_sems, after)
    return out[n:]


def _gather_forward(lands, name):
    n = len(lands)

    def body(*refs):
        ins, outs = refs[:n], refs[n:2 * n]
        send_sems, recv_sems = refs[2 * n:]
        x, y, c = _coords()
        sends = []
        for i in range(n):
            half = ins[i].shape[1] // 2
            for k in (1, 2, 3):
                ox, oy = _other_chip(x, y, k)
                cp = _remote(ins[i].at[2 * ox + oy, _half_rows(half, c)], outs[i].at[2 * ox + oy, _half_rows(half, c)],
                             send_sems.at[3 * i + k - 1], recv_sems.at[3 * i + k - 1], (x, y, 1 - c))
                cp.start()
                sends.append(cp)
        for i in range(n):
            half = ins[i].shape[1] // 2
            for k in (1, 2, 3):
                ox, oy = _other_chip(x, y, k)
                piece = outs[i].at[2 * ox + oy, _half_rows(half, 1 - c)]
                _remote(piece, piece, send_sems.at[3 * i + k - 1], recv_sems.at[3 * i + k - 1], (x, y, 1 - c)).wait_recv()
        for cp in sends:
            cp.wait_send()

    return pl.pallas_call(
        body, name=name, in_specs=[HBM_SPEC] * n, out_specs=[HBM_SPEC] * n,
        out_shape=[jax.ShapeDtypeStruct(a.shape, a.dtype) for a in lands],
        input_output_aliases={i: i for i in range(n)},
        scratch_shapes=[pltpu.SemaphoreType.DMA((3 * n,))] * 2)(*lands)


def _rs_sibling_swap(g4s, name):
    n = len(g4s)

    def body(*refs):
        ins, outs = refs[:n], refs[n:2 * n]
        send_sems, recv_sems = refs[2 * n:]
        x, y, c = _coords()
        sends = []
        for i in range(n):
            half = ins[i].shape[1] // 2
            for s in range(4):
                cp = _remote(ins[i].at[s, _half_rows(half, 1 - c)], outs[i].at[s], send_sems.at[4 * i + s],
                             recv_sems.at[4 * i + s], (x, y, 1 - c))
                cp.start()
                sends.append(cp)
        for cp in sends:
            cp.wait_recv()
        for cp in sends:
            cp.wait_send()

    return pl.pallas_call(
        body, name=name, in_specs=[HBM_SPEC] * n, out_specs=[HBM_SPEC] * n,
        out_shape=[jax.ShapeDtypeStruct((4, g.shape[1] // 2, g.shape[2]), g.dtype) for g in g4s],
        scratch_shapes=[pltpu.SemaphoreType.DMA((4 * n,))] * 2)(*g4s)


def _rs_add_sibling(g4, got, c_idx, name):
    _, rows, cols = g4.shape
    half = rows // 2
    tr = min(half, 256)
    nb = half // tr

    def body(c_ref, a_ref, b_ref, o_ref, ob_ref):
        total = a_ref[...] + b_ref[...]
        o_ref[...] = total
        ob_ref[...] = total.astype(BF16)

    blk = (1, tr, cols)
    out = pl.BlockSpec(blk, lambda s, i, c_ref: (s, i, 0))
    grid_spec = pltpu.PrefetchScalarGridSpec(
        num_scalar_prefetch=1, grid=(4, nb),
        in_specs=[pl.BlockSpec(blk, lambda s, i, c_ref: (s, c_ref[0] * nb + i, 0)), out],
        out_specs=[out, out])
    return pl.pallas_call(
        body, name=name, grid_spec=grid_spec,
        out_shape=[jax.ShapeDtypeStruct(got.shape, F32), jax.ShapeDtypeStruct(got.shape, BF16)],
        compiler_params=_params("parallel", "parallel"))(c_idx, g4, got)


def _rs_add_chips(p4, got3, idx, name):
    _, half, cols = p4.shape
    tr = min(half, 256)
    nb = half // tr

    def body(idx_ref, p_ref, a_ref, b_ref, c_ref, o_ref):
        o_ref[...] = ((p_ref[0] + a_ref[0].astype(F32)) + b_ref[0].astype(F32)) + c_ref[0].astype(F32)

    blk = (1, tr, cols)
    grid_spec = pltpu.PrefetchScalarGridSpec(
        num_scalar_prefetch=1, grid=(nb,),
        in_specs=[pl.BlockSpec(blk, lambda i, idx_ref: (idx_ref[0], i, 0))]
        + [pl.BlockSpec(blk, functools.partial(lambda k, i, idx_ref: (k, i, 0), k)) for k in range(3)],
        out_specs=pl.BlockSpec((tr, cols), lambda i, idx_ref: (idx_ref[1] * nb + i, 0)))
    return pl.pallas_call(body, name=name, grid_spec=grid_spec, out_shape=jax.ShapeDtypeStruct((2 * half, cols), F32),
                          compiler_params=_params("parallel"))(idx, p4, got3, got3, got3)


def _rs_share_halves(bufs, name):
    n = len(bufs)

    def body(*refs):
        ins, outs = refs[:n], refs[n:2 * n]
        send_sems, recv_sems = refs[2 * n:]
        x, y, c = _coords()
        sends = []
        for i in range(n):
            half = ins[i].shape[0] // 2
            cp = _remote(ins[i].at[_half_rows(half, c)], outs[i].at[_half_rows(half, c)], send_sems.at[i],
                         recv_sems.at[i], (x, y, 1 - c))
            cp.start()
            sends.append(cp)
        for i in range(n):
            half = ins[i].shape[0] // 2
            _remote(ins[i].at[_half_rows(half, c)], outs[i].at[_half_rows(half, 1 - c)], send_sems.at[i],
                    recv_sems.at[i], (x, y, 1 - c)).wait_recv()
        for cp in sends:
            cp.wait_send()

    return pl.pallas_call(
        body, name=name, in_specs=[HBM_SPEC] * n, out_specs=[HBM_SPEC] * n,
        out_shape=[jax.ShapeDtypeStruct(b.shape, b.dtype) for b in bufs],
        input_output_aliases={i: i for i in range(n)},
        scratch_shapes=[pltpu.SemaphoreType.DMA((n,))] * 2)(*bufs)


def _scatter_piece(srcs, lands, send_sems, recv_sems, i, k, x, y, c):
    ox, oy = _other_chip(x, y, k)
    return _remote(srcs[i].at[2 * ox + oy], lands[i].at[k - 1], send_sems.at[3 * i + k - 1],
                   recv_sems.at[3 * i + k - 1], (ox, oy, c))


def _rs_scatter_start(p4s, name):
    n = len(p4s)

    def body(*refs):
        srcs, lands = refs[:n], refs[n:2 * n]
        send_sems, recv_sems = refs[2 * n], refs[2 * n + 1]
        token = refs[-1]
        x, y, c = _coords()
        for i in range(n):
            for k in (1, 2, 3):
                _scatter_piece(srcs, lands, send_sems, recv_sems, i, k, x, y, c).start()
        token[...] = jnp.zeros_like(token)

    hbm = lambda a: pltpu.with_memory_space_constraint(a, pltpu.HBM)
    lands = [lax.empty((3,) + p.shape[1:], p.dtype) for p in p4s]
    out = pl.pallas_call(
        body, name=name,
        out_shape=(pltpu.SemaphoreType.DMA((3 * n,)), pltpu.SemaphoreType.DMA((3 * n,)),
                   *[pltpu.HBM(p.shape, p.dtype) for p in p4s], *[pltpu.HBM(a.shape, a.dtype) for a in lands],
                   jax.ShapeDtypeStruct((SUBLANES, LANES), F32)),
        in_specs=[HBM_SPEC] * (2 * n), out_specs=(SEM_SPEC, SEM_SPEC, *[HBM_SPEC] * (2 * n), VMEM_SPEC),
        input_output_aliases={i: 2 + i for i in range(2 * n)},
        compiler_params=pltpu.CompilerParams(has_side_effects=DATAFLOW))(*[hbm(p) for p in p4s], *[hbm(a) for a in lands])
    return out[0], out[1], out[2:2 + n], out[2 + n:2 + 2 * n], out[-1]


def _rs_scatter_wait(send_sems, recv_sems, srcs, lands, after, name):
    n = len(srcs)

    def body(*refs):
        src_refs, land_refs = refs[:n], refs[n:2 * n]
        send_sems, recv_sems = refs[2 * n], refs[2 * n + 1]
        x, y, c = _coords()
        for i in range(n):
            for k in (1, 2, 3):
                cp = _scatter_piece(src_refs, land_refs, send_sems, recv_sems, i, k, x, y, c)
                cp.wait_send()
                cp.wait_recv()

    out = pl.pallas_call(
        body, name=name,
        out_shape=(*[pltpu.HBM(s.shape, s.dtype) for s in srcs], *[pltpu.HBM(a.shape, a.dtype) for a in lands]),
        in_specs=[HBM_SPEC] * (2 * n) + [SEM_SPEC, SEM_SPEC, pl.BlockSpec(memory_space=pl.ANY)],
        out_specs=tuple([HBM_SPEC] * (2 * n)), input_output_aliases={i: i for i in range(2 * n)},
        compiler_params=pltpu.CompilerParams(has_side_effects=DATAFLOW))(*srcs, *lands, send_sems, recv_sems, after)
    return out[n:]


def _rs_front(g4s, names, tag):
    c_idx = jnp.stack([lax.axis_index("c")]).astype(jnp.int32)
    got = _rs_sibling_swap(g4s, f"rs_sibling_swap_{tag}")
    return [_rs_add_sibling(g, s, c_idx, f"rs_add_sibling_{nm}") for g, s, nm in zip(g4s, got, names)]


def _rs_back(p4s, got3, names):
    x, y, c = _coords()
    idx = jnp.stack([2 * x + y, c]).astype(jnp.int32)
    return [_rs_add_chips(p, t, idx, f"rs_add_chips_{nm}") for (p, _), t, nm in zip(p4s, got3, names)]


def _cols_split(full):
    r, c4 = full.shape
    return full.reshape(r, 4, c4 // 4).transpose(1, 0, 2)


_BIG = {
    "w_in_even": ((1024, 2052), lambda s: s),
    "w_out_even": ((512, 1024), lambda s: s.reshape(2048, 1024)),
    "w_in_odd": ((1024, 2048), lambda s: s),
    "w_out_odd": ((512, 1024), lambda s: s.reshape(2048, 1024)),
    "w_pl": ((512, 256), lambda s: s.reshape(4, 2, 256, 256).transpose(1, 2, 0, 3).reshape(2, 256, 1024)),
    "w_pl_gate": ((512, 1024), lambda s: s.reshape(4, 2, 256, 1024).transpose(1, 0, 2, 3).reshape(2, 1024, 1024)),
}


_RS_EARLY = {
    "w_in_odd": lambda f: f,
    "w_out_odd": lambda f: f.reshape(4, 512, 1024),
    "w_pl_gate1": lambda f: f.reshape(4, 256, 1024),
    "w_pl1": _cols_split,
    "w_out_even": lambda f: f.reshape(4, 512, 1024),
    "w_pl_gate0": lambda f: f.reshape(4, 256, 1024),
    "w_pl0": _cols_split,
}
_RS_LATE = {"w_in_even": lambda f: f}


def _size(shape):
    n = 1
    for d in shape:
        n *= d
    return n


_SMALL = {"a_log": (1, 8), "dt_bias": (1, 8), "gdn_norm_g": (1, 128), "hgrn_norm_g": (1, 128),
          "lower_bounds": (2, 2048), "ln_g": (2, 1024), "ln_b": (2, 1024), "conv_a_w": (3, 1024), "conv_b_w": (4, 3072)}
_CONV_SHARD = {"conv_a_w": (3, 256), "conv_b_w": (4, 768)}


def _pack_small(parts, shapes, head_rows=0):
    rows = []
    for n, shape in shapes.items():
        v = parts[n].reshape(-1)
        rows.append(jnp.pad(v, (0, -v.shape[0] % LANES)).reshape(-1, LANES))
    buf = jnp.concatenate(rows, axis=0)
    return jnp.pad(buf, ((head_rows, -(buf.shape[0] + head_rows) % SUBLANES), (0, 0)))


def _unpack_small(buf, shapes, head_rows=0):
    out, off = {}, head_rows
    for n, shape in shapes.items():
        nrow = -(-_size(shape) // LANES)
        out[n] = buf[off:off + nrow].reshape(-1)[:_size(shape)].reshape(shape)
        off += nrow
    return out


_WEIGHTS = ["w_in_even", "conv_a_w", "conv_b_w", "a_log", "dt_bias", "gdn_norm_g", "w_out_even", "w_in_odd",
            "lower_bounds", "hgrn_norm_g", "w_out_odd", "ln_g", "ln_b", "w_pl", "w_pl_gate"]


def kernel(x, p, w_in_even, conv_a_w, conv_b_w, a_log, dt_bias, gdn_norm_g, w_out_even, w_in_odd, lower_bounds, hgrn_norm_g, w_out_odd, ln_g, ln_b, w_pl, w_pl_gate, loss_target, m_w_in_even, m_conv_a_w, m_conv_b_w, m_a_log, m_dt_bias, m_gdn_norm_g, m_w_out_even, m_w_in_odd, m_lower_bounds, m_hgrn_norm_g, m_w_out_odd, m_ln_g, m_ln_b, m_w_pl, m_w_pl_gate, v_w_in_even, v_conv_a_w, v_conv_b_w, v_a_log, v_dt_bias, v_gdn_norm_g, v_w_out_even, v_w_in_odd, v_lower_bounds, v_hgrn_norm_g, v_w_out_odd, v_ln_g, v_ln_b, v_w_pl, v_w_pl_gate):
    w = dict(zip(_WEIGHTS, (w_in_even, conv_a_w, conv_b_w, a_log, dt_bias, gdn_norm_g, w_out_even, w_in_odd,
                            lower_bounds, hgrn_norm_g, w_out_odd, ln_g, ln_b, w_pl, w_pl_gate)))
    m = dict(zip(_WEIGHTS, (m_w_in_even, m_conv_a_w, m_conv_b_w, m_a_log, m_dt_bias, m_gdn_norm_g, m_w_out_even,
                            m_w_in_odd, m_lower_bounds, m_hgrn_norm_g, m_w_out_odd, m_ln_g, m_ln_b, m_w_pl, m_w_pl_gate)))
    v = dict(zip(_WEIGHTS, (v_w_in_even, v_conv_a_w, v_conv_b_w, v_a_log, v_dt_bias, v_gdn_norm_g, v_w_out_even,
                            v_w_in_odd, v_lower_bounds, v_hgrn_norm_g, v_w_out_odd, v_ln_g, v_ln_b, v_w_pl, v_w_pl_gate)))
    chip = 2 * lax.axis_index("x") + lax.axis_index("y")

    names = list(_BIG)
    shard_shapes = {n: _BIG[n][0] for n in names}
    early, late = names[:1], names[1:]
    shards = {n: w[n].reshape(shard_shapes[n]).astype(BF16) for n in early}
    whole = lambda n, stacked: _BIG[n][1](lax.dynamic_update_slice(stacked, shards[n][None], (chip, 0, 0)))
    conv_mine = _pack_small({n: w[n] for n in _CONV_SHARD}, _CONV_SHARD)
    conv_all = _exchange_small(conv_mine, False, "gather_conv")
    shards, conv_all = lax.optimization_barrier((shards, conv_all))
    first = _gather_start([shards[n] for n in early], "gather_first_start")
    shards.update({n: (w[n].reshape(shard_shapes[n]) + first[4][0, 0]).astype(BF16) for n in late})
    send_sems, recv_sems, srcs, lands, token = _gather_start([shards[n] for n in late], "gather_rest_start")

    def late_weights(after):
        landed = _gather_forward(_gather_wait(send_sems, recv_sems, srcs, lands, after, "gather_rest_wait"),
                                 "gather_rest_forward")
        return {n: whole(n, ga) for n, ga in zip(late, landed)}

    landed = _gather_forward(_gather_wait(*first[:4], token, "gather_first_wait"), "gather_first_forward")
    full = {n: whole(n, ga) for n, ga in zip(early, landed)}
    conv_by_chip = [_unpack_small(conv_all[2 * s], _CONV_SHARD) for s in range(4)]
    for n in _CONV_SHARD:
        full[n] = jnp.concatenate([conv_by_chip[s][n] for s in range(4)], axis=1)
    for n in _SMALL:
        if n not in _CONV_SHARD:
            full[n] = w[n]

    early_rs = {}

    def early_grads_ready(grads):
        early_rs["p4s"] = _rs_front([_RS_EARLY[n](grads[n]) for n in _RS_EARLY], list(_RS_EARLY), "early")
        early_rs["sems"] = _rs_scatter_start([pb for _, pb in early_rs["p4s"]], "rs_scatter_early_start")
        return early_rs["sems"][4][0, 0]

    late_rs = {}

    def last_grad_ready(grad):
        late_rs["p4s"] = _rs_front([_RS_LATE[n](grad) for n in _RS_LATE], list(_RS_LATE), "late")
        late_rs["sems"] = _rs_scatter_start([pb for _, pb in late_rs["p4s"]], "rs_scatter_late_start")
        return late_rs["sems"][4][0, 0]

    loss_part, dx, g = _local_step(x[0], p[:, 0], loss_target[0], full, late_weights, early_grads_ready,
                                   last_grad_ready, token[0, 0])

    late_p4s, late_sems = late_rs["p4s"], late_rs["sems"]
    got3 = _rs_scatter_wait(*early_rs["sems"][:4], dx, "rs_scatter_early_wait")
    summed = dict(zip(_RS_EARLY, _rs_share_halves(_rs_back(early_rs["p4s"], got3, list(_RS_EARLY)), "rs_share_early")))
    g_big = {n: summed[n] for n in names if n in summed}
    g_big["w_pl"] = jnp.stack([summed["w_pl0"], summed["w_pl1"]])
    g_big["w_pl_gate"] = jnp.stack([summed["w_pl_gate0"], summed["w_pl_gate1"]])
    small_sum = _exchange_small(jnp.concatenate([loss_part, _pack_small(g, _SMALL)], axis=0), True, "reduce_small")
    loss = small_sum[0, 0]
    g_small = _unpack_small(small_sum, _SMALL, head_rows=SUBLANES)
    for n, (rows, cols) in _CONV_SHARD.items():
        g_small[n] = lax.dynamic_slice_in_dim(g_small[n], chip * cols, cols, axis=1)

    grads, delta, new_m, new_v = {}, {}, {}, {}
    for n in late:
        grads[n] = g_big[n].reshape(w[n].shape)
        delta[n], new_m[n], new_v[n] = _adamw(w[n], grads[n], m[n], v[n], f"adamw_{n}")
    own = {n: (_CONV_SHARD[n] if n in _CONV_SHARD else _SMALL[n]) for n in _SMALL}
    packs = [_pack_small({n: src[n] for n in _SMALL}, own)[None] for src in (w, g_small, m, v)]
    outs = [_unpack_small(t[0], own) for t in _adamw(*packs, "adamw_small")]
    for n in _SMALL:
        grads[n] = g_small[n].reshape(w[n].shape)
        delta[n], new_m[n], new_v[n] = (t[n].reshape(w[n].shape) for t in outs)
    got3 = _rs_scatter_wait(*late_sems[:4], new_v["w_in_odd"], "rs_scatter_late_wait")
    (g_in_even,) = _rs_share_halves(_rs_back(late_p4s, got3, list(_RS_LATE)), "rs_share_late")
    for n in early:
        t_ = lambda a: jnp.swapaxes(a, 1, 2)
        g_t = t_(g_in_even.reshape(w[n].shape))
        grads[n] = t_(g_t)
        delta[n], new_m[n], new_v[n] = (t_(o) for o in _adamw(t_(w[n]), g_t, t_(m[n]), t_(v[n]), f"adamw_{n}"))
    return (loss, dx[None], *[grads[n] for n in _WEIGHTS], *[delta[n] for n in _WEIGHTS],
            *[new_m[n] for n in _WEIGHTS], *[new_v[n] for n in _WEIGHTS])
```

```python
import functools

import jax
import jax.numpy as jnp
from jax import lax
from jax.experimental import pallas as pl
from jax.experimental.pallas import tpu as pltpu

F32 = jnp.float32
BF16 = jnp.bfloat16
HI = lax.Precision.HIGHEST

D_MODEL = 1024
PL_DIM = 256
GDN_HEADS = 8
HEAD_DIM = 128
GDN_CHUNK = 64
HGRN_HEADS = 16
HGRN_CHUNK = 32
HGRN_WIDTH = 2048
DEEPNORM_ALPHA = 4.0 ** 0.25
NORM_EPS = 1e-5
ADAM_LR, ADAM_B1, ADAM_B2, ADAM_EPS, ADAM_WD, ADAM_STEP = 0.001, 0.9, 0.999, 1e-08, 0.01, 10

VMEM_LIMIT = 56 * 1024 * 1024
SUBLANES = 8
LANES = 128


def _params(*sem):
    return pltpu.CompilerParams(dimension_semantics=sem, vmem_limit_bytes=VMEM_LIMIT)


ONE_PASS, THREE_PASS, FULL_F32, EXACT_LHS, EXACT_RHS = 0, 1, 2, 3, 4


def _split3(v):
    hi = v.astype(BF16)
    r1 = v - hi.astype(F32)
    mid = r1.astype(BF16)
    return hi, mid, (r1 - mid.astype(F32)).astype(BF16)


def _mm_raw(a, b, kind, prec):
    nb = a.ndim - 2
    ca = a.ndim - 1 if kind[0] == "n" else a.ndim - 2
    cb = b.ndim - 2 if kind[1] == "n" else b.ndim - 1
    dims = (((ca,), (cb,)), (tuple(range(nb)),) * 2)
    if prec == FULL_F32:
        return lax.dot_general(a, b, dims, precision=HI, preferred_element_type=F32)
    dot = lambda p, q: lax.dot_general(p, q, dims, preferred_element_type=F32)
    ah, bh = a.astype(BF16), b.astype(BF16)
    if prec == ONE_PASS:
        return dot(ah, bh)
    if prec == EXACT_LHS:
        b1, b2, b3 = _split3(b)
        return dot(ah, b1) + (dot(ah, b2) + dot(ah, b3))
    if prec == EXACT_RHS:
        a1, a2, a3 = _split3(a)
        return dot(a1, bh) + (dot(a2, bh) + dot(a3, bh))
    al = (a - ah.astype(F32)).astype(BF16)
    bl = (b - bh.astype(F32)).astype(BF16)
    return dot(ah, bh) + (dot(ah, bl) + dot(al, bh))


@functools.partial(jax.custom_vjp, nondiff_argnums=(2, 3))
def _mm_vjp(a, b, kind, hi):
    return _mm_raw(a, b, kind, hi)


def _mm_vjp_fwd(a, b, kind, hi):
    return _mm_raw(a, b, kind, hi), (a, b)


def _mm_vjp_bwd(kind, hi, res, dc):
    a, b = res
    if hi in (EXACT_LHS, EXACT_RHS):
        assert kind == "nn"
        if hi == EXACT_LHS:
            return jnp.zeros_like(a), _mm_raw(a, dc, "tn", EXACT_LHS)
        return _mm_raw(dc, b, "nt", EXACT_RHS), jnp.zeros_like(b)
    if kind == "nn":
        return _mm_raw(dc, b, "nt", hi), _mm_raw(a, dc, "tn", hi)
    if kind == "nt":
        return _mm_raw(dc, b, "nn", hi), _mm_raw(dc, a, "tn", hi)
    return _mm_raw(b, dc, "nt", hi), _mm_raw(a, dc, "nn", hi)


_mm_vjp.defvjp(_mm_vjp_fwd, _mm_vjp_bwd)


def _lane_total(v):
    return jnp.broadcast_to(jnp.sum(v, axis=-1, keepdims=True), v.shape)


def _matmul(a, b, *, name, ta=False, tb=False, add=None, add_scale=1.0, tm=1024, tn=2048, tk=1024):
    m, k = (a.shape[1], a.shape[0]) if ta else a.shape
    n = b.shape[0] if tb else b.shape[1]
    tm, tn, tk = min(tm, m), min(tn, n), min(tk, k)
    tn = tn if n % tn == 0 else tn // 2
    assert m % tm == 0 and n % tn == 0 and k % tk == 0, (name, m, n, k)
    nk = k // tk
    dims = (((0 if ta else 1,), (1 if tb else 0,)), ((), ()))

    def body(*refs):
        a_ref, b_ref = refs[:2]
        o_ref = refs[-1]
        part = lax.dot_general(a_ref[...].astype(BF16), b_ref[...].astype(BF16), dims, preferred_element_type=F32)
        first = (lambda: part) if add is None else (lambda: part + add_scale * refs[2][...])
        if nk == 1:
            o_ref[...] = first()
        else:
            kk = pl.program_id(2)

            @pl.when(kk == 0)
            def _():
                o_ref[...] = first()

            @pl.when(kk > 0)
            def _():
                o_ref[...] += part

    a_spec = pl.BlockSpec((tk, tm), lambda i, j, kk: (kk, i)) if ta else pl.BlockSpec((tm, tk), lambda i, j, kk: (i, kk))
    b_spec = pl.BlockSpec((tn, tk), lambda i, j, kk: (j, kk)) if tb else pl.BlockSpec((tk, tn), lambda i, j, kk: (kk, j))
    o_spec = pl.BlockSpec((tm, tn), lambda i, j, kk: (i, j))
    in_specs = [a_spec, b_spec] + ([o_spec] if add is not None else [])
    args = (a, b) + ((add,) if add is not None else ())
    return pl.pallas_call(
        body, name=name, grid=(m // tm, n // tn, nk), in_specs=in_specs, out_specs=o_spec,
        out_shape=jax.ShapeDtypeStruct((m, n), F32),
        compiler_params=_params("parallel", "parallel", "arbitrary"))(*args)


HALO = SUBLANES


def _halo_specs(tt, width, col, nt):
    r = tt // HALO
    prev = pl.BlockSpec((HALO, width), lambda i: (jnp.maximum(i * r - 1, 0), col))
    nxt = pl.BlockSpec((HALO, width), lambda i: (jnp.minimum((i + 1) * r, nt * r - 1), col))
    return prev, nxt


def _shift_down(ext, k):
    return ext if k == 0 else pltpu.roll(ext, k, 0)


def _shift_up(ext, k):
    return ext if k == 0 else pltpu.roll(ext, ext.shape[0] - k, 0)


def _causal_conv(ext, w, taps):
    acc = None
    for j in range(taps):
        term = w[j:j + 1, :] * _shift_down(ext, taps - 1 - j)
        acc = term if acc is None else acc + term
    return acc[HALO:, :]


def _conv_a_fwd(proj_a, conv_w):
    t = proj_a.shape[0]
    tt = min(t, 256)
    nt = t // tt
    wdt = 1024

    def body(cur_ref, prev_ref, w_ref, y_ref):
        i = pl.program_id(0)
        cur = cur_ref[...]
        h, c, b, z = (cur[:, k * wdt:(k + 1) * wdt] for k in range(4))
        prev = prev_ref[...]
        u_prev = jnp.where(i > 0, prev[:, wdt:2 * wdt] * prev[:, 0:wdt], 0.0)
        ext = jnp.concatenate([u_prev, c * h], axis=0)
        conv = _causal_conv(ext, w_ref[...], 3)
        y_ref[...] = (b * conv * jax.nn.silu(z)).astype(BF16)

    prev_spec, _ = _halo_specs(tt, 4 * wdt, 0, nt)
    return pl.pallas_call(
        body, name="conv_a_fwd", grid=(nt,),
        in_specs=[pl.BlockSpec((tt, 4 * wdt), lambda i: (i, 0)), prev_spec, pl.BlockSpec((3, wdt), lambda i: (0, 0))],
        out_specs=pl.BlockSpec((tt, wdt), lambda i: (i, 0)),
        out_shape=jax.ShapeDtypeStruct((t, wdt), BF16), compiler_params=_params("parallel"))(proj_a, proj_a, conv_w)


def _conv_a_bwd(proj_a, conv_w, dy):
    t = proj_a.shape[0]
    tt = min(t, 256)
    nt = t // tt
    wdt = 1024

    def body(cur_ref, prev_ref, nxt_ref, w_ref, dy_ref, dyn_ref, d_ref, dw_ref):
        i = pl.program_id(0)
        w = w_ref[...]
        cur, prev, nxt = cur_ref[...], prev_ref[...], nxt_ref[...]
        split = lambda a: tuple(a[:, k * wdt:(k + 1) * wdt] for k in range(4))
        h, c, b, z = split(cur)
        hp, cp, _, _ = split(prev)
        hn, cn, bn, zn = split(nxt)
        u_prev = jnp.where(i > 0, cp * hp, 0.0)
        u_ext = jnp.concatenate([u_prev, c * h, cn * hn], axis=0)
        taps = [_shift_down(u_ext, 2 - j)[HALO:, :] for j in range(3)]
        conv = w[0:1, :] * taps[0] + w[1:2, :] * taps[1] + w[2:3, :] * taps[2]
        b_cn = jnp.concatenate([b, bn], axis=0)
        z_cn = jnp.concatenate([z, zn], axis=0)
        dy_cn = jnp.concatenate([dy_ref[...], jnp.where(i < nt - 1, dyn_ref[...], 0.0)], axis=0)
        sg = jax.nn.sigmoid(z_cn)
        silu = z_cn * sg
        d_conv = dy_cn * b_cn * silu
        db = (dy_cn * conv * silu)[:tt, :]
        dz = (dy_cn * b_cn * conv * (sg * (1.0 + z_cn * (1.0 - sg))))[:tt, :]
        du = None
        for j in range(3):
            term = w[j:j + 1, :] * _shift_up(d_conv, 2 - j)
            du = term if du is None else du + term
        du = du[:tt, :]
        d_ref[...] = jnp.concatenate([du * c, du * h, db, dz], axis=1).astype(BF16)

        @pl.when(i == 0)
        def _():
            dw_ref[...] = jnp.zeros_like(dw_ref)

        d_cur = d_conv[:tt, :]
        rows = [jnp.sum(d_cur * taps[j][:tt, :], axis=0, keepdims=True) for j in range(3)]
        dw_ref[0:3, :] += jnp.concatenate(rows, axis=0)

    prev_spec, nxt_spec = _halo_specs(tt, 4 * wdt, 0, nt)
    _, dyn_spec = _halo_specs(tt, wdt, 0, nt)
    return pl.pallas_call(
        body, name="conv_a_bwd", grid=(nt,),
        in_specs=[pl.BlockSpec((tt, 4 * wdt), lambda i: (i, 0)), prev_spec, nxt_spec,
                  pl.BlockSpec((3, wdt), lambda i: (0, 0)), pl.BlockSpec((tt, wdt), lambda i: (i, 0)), dyn_spec],
        out_specs=[pl.BlockSpec((tt, 4 * wdt), lambda i: (i, 0)), pl.BlockSpec((SUBLANES, wdt), lambda i: (0, 0))],
        out_shape=[jax.ShapeDtypeStruct((t, 4 * wdt), BF16), jax.ShapeDtypeStruct((SUBLANES, wdt), F32)],
        compiler_params=_params("arbitrary"))(proj_a, proj_a, proj_a, conv_w, dy, dy)


def _conv_b_fwd(proj_qkv, conv_w):
    t, width = proj_qkv.shape
    tt = min(t, 256)
    nt = t // tt
    wdt = 1024

    def body(cur_ref, prev_ref, w_ref, y_ref):
        i = pl.program_id(1)
        ext = jnp.concatenate([jnp.where(i > 0, prev_ref[...], 0.0), cur_ref[...]], axis=0)
        y_ref[...] = jax.nn.silu(_causal_conv(ext, w_ref[...], 4))

    r = tt // HALO
    return pl.pallas_call(
        body, name="conv_b_fwd", grid=(width // wdt, nt),
        in_specs=[pl.BlockSpec((tt, wdt), lambda j, i: (i, j)),
                  pl.BlockSpec((HALO, wdt), lambda j, i: (jnp.maximum(i * r - 1, 0), j)),
                  pl.BlockSpec((4, wdt), lambda j, i: (0, j))],
        out_specs=pl.BlockSpec((tt, wdt), lambda j, i: (i, j)),
        out_shape=jax.ShapeDtypeStruct((t, width), F32), compiler_params=_params("parallel", "parallel"))(
            proj_qkv, proj_qkv, conv_w)


def _conv_b_bwd(proj_qkv, conv_w, d_act, col, name):
    t = proj_qkv.shape[0]
    tt = min(t, 256)
    nt = t // tt
    wdt = 1024

    def body(cur_ref, prev_ref, nxt_ref, w_ref, da_ref, dan_ref, d_ref, dw_ref):
        i = pl.program_id(0)
        w = w_ref[...]
        u_ext = jnp.concatenate([jnp.where(i > 0, prev_ref[...], 0.0), cur_ref[...], nxt_ref[...]], axis=0)
        taps = [_shift_down(u_ext, 3 - j)[HALO:, :] for j in range(4)]
        conv = w[0:1, :] * taps[0] + w[1:2, :] * taps[1] + w[2:3, :] * taps[2] + w[3:4, :] * taps[3]
        da_cn = jnp.concatenate([da_ref[...], jnp.where(i < nt - 1, dan_ref[...], 0.0)], axis=0)
        sg = jax.nn.sigmoid(conv)
        d_conv = da_cn * (sg * (1.0 + conv * (1.0 - sg)))
        du = None
        for j in range(4):
            term = w[j:j + 1, :] * _shift_up(d_conv, 3 - j)
            du = term if du is None else du + term
        d_ref[...] = du[:tt, :].astype(BF16)

        @pl.when(i == 0)
        def _():
            dw_ref[...] = jnp.zeros_like(dw_ref)

        d_cur = d_conv[:tt, :]
        rows = [jnp.sum(d_cur * taps[j][:tt, :], axis=0, keepdims=True) for j in range(4)]
        dw_ref[0:4, :] += jnp.concatenate(rows, axis=0)

    prev_spec, nxt_spec = _halo_specs(tt, wdt, col, nt)
    _, dan_spec = _halo_specs(tt, wdt, 0, nt)
    return pl.pallas_call(
        body, name=name, grid=(nt,),
        in_specs=[pl.BlockSpec((tt, wdt), lambda i: (i, col)), prev_spec, nxt_spec,
                  pl.BlockSpec((4, wdt), lambda i: (0, col)), pl.BlockSpec((tt, wdt), lambda i: (i, 0)), dan_spec],
        out_specs=[pl.BlockSpec((tt, wdt), lambda i: (i, 0)), pl.BlockSpec((SUBLANES, wdt), lambda i: (0, 0))],
        out_shape=[jax.ShapeDtypeStruct((t, wdt), BF16), jax.ShapeDtypeStruct((SUBLANES, wdt), F32)],
        compiler_params=_params("arbitrary"))(proj_qkv, proj_qkv, proj_qkv, conv_w, d_act, d_act)


def _rms_gate(o, gn, z):
    on = o * lax.rsqrt(jnp.mean(o * o, axis=-1, keepdims=True) + NORM_EPS) * gn
    return on * jax.nn.silu(z)


GDN_PREP_ROWS = 1024


def _unit_lower_inverse(low):
    c = low.shape[-1]
    eye = lax.broadcasted_iota(jnp.int32, low.shape, low.ndim - 2) == lax.broadcasted_iota(jnp.int32, low.shape, low.ndim - 1)
    x = -low
    inv = eye.astype(F32) + x
    for _ in range(c.bit_length() - 2):
        x = _mm_raw(x, x, "nn", THREE_PASS)
        inv = inv + _mm_raw(inv, x, "nn", THREE_PASS)
    return inv


@jax.custom_vjp
def _known_inverse(low, inv):
    return inv


def _known_inverse_fwd(low, inv):
    return inv, inv


def _known_inverse_bwd(inv, d_inv):
    return -_mm_raw(_mm_raw(inv, d_inv, "tn", THREE_PASS), inv, "nt", THREE_PASS), jnp.zeros_like(inv)


_known_inverse.defvjp(_known_inverse_fwd, _known_inverse_bwd)


def _gdn_prep(mm, qa, ka, va, braw, araw, alog, dtb, inv_kept=None):
    n, c, _ = qa.shape
    q = qa * lax.rsqrt(jnp.sum(qa * qa, axis=-1, keepdims=True) + 1e-6) * (HEAD_DIM ** -0.5)
    k = ka * lax.rsqrt(jnp.sum(ka * ka, axis=-1, keepdims=True) + 1e-6)
    beta = jax.nn.sigmoid(braw)
    g = -jnp.exp(alog) * jax.nn.softplus(araw + dtb)
    ri = lax.broadcasted_iota(jnp.int32, (n, c, c), 1)
    ci = lax.broadcasted_iota(jnp.int32, (n, c, c), 2)
    incl, strict, eye = ri >= ci, ri > ci, ri == ci
    gc = mm(incl.astype(F32), g, "nn", EXACT_LHS)
    gc_i = gc[:, :, :c]
    gc_j = mm(jnp.ones((n, c, c), F32), jnp.where(eye, gc_i, 0.0), "nn", EXACT_LHS)
    decay = jnp.where(incl, jnp.exp(jnp.where(incl, gc_i - gc_j, 0.0)), 0.0)
    kb = k * beta
    low = jnp.where(strict, mm(kb, k, "nt", ONE_PASS) * decay, 0.0)
    inv = _unit_lower_inverse(low) if inv_kept is None else _known_inverse(low, inv_kept)
    egc = jnp.exp(gc)
    u = mm(inv, va * beta, "nn", THREE_PASS)
    w = mm(inv, kb * egc, "nn", THREE_PASS)
    attn = jnp.where(incl, mm(q, k, "nt", ONE_PASS) * decay, 0.0)
    g_last = jnp.sum(g, axis=1, keepdims=True)
    outs = (u, w, q * egc, k * jnp.exp(g_last - gc), attn, jnp.exp(g_last))
    return outs + (inv,) if inv_kept is None else outs


def _gdn_scan(mm, u, w, qd, kd, attn, egl, z, gn, state):
    v_new = u - mm(w, state, "nn", ONE_PASS)
    o = mm(qd, state, "nn", ONE_PASS) + mm(attn, v_new, "nn", ONE_PASS)
    new_state = state * egl + mm(kd, v_new, "tn", ONE_PASS)
    return _rms_gate(o, gn, z), new_state


def _chunks(ref_value, n, c):
    return ref_value.reshape(n, c, ref_value.shape[-1])


def _by_head(ref, rows, heads):
    return jnp.stack([ref[rows, pl.ds(h * HEAD_DIM, HEAD_DIM)] for h in range(heads)])


def _store_heads(ref, rows, value):
    for h in range(value.shape[0]):
        ref[rows, pl.ds(h * HEAD_DIM, HEAD_DIM)] = value[h]


def _gdn_prep_specs(tb, nt_unused=None):
    col = lambda off: pl.BlockSpec((tb, HEAD_DIM), lambda h, i: (i, off + h))
    rep = pl.BlockSpec((1, tb, LANES), lambda h, i: (h, i, 0))
    par = pl.BlockSpec((1, SUBLANES, LANES), lambda h, i: (h, 0, 0))
    att = pl.BlockSpec((1, tb, GDN_CHUNK), lambda h, i: (h, i, 0))
    egl = pl.BlockSpec((1, tb // GDN_CHUNK, SUBLANES, LANES), lambda h, i: (h, i, 0, 0))
    return col, rep, par, att, egl


def _gdn_prep_fwd(qkv_act, braw, araw, alog, dtb):
    t = qkv_act.shape[0]
    tb = min(t, 2 * GDN_PREP_ROWS)
    nt, nc = t // tb, tb // GDN_CHUNK
    width = GDN_HEADS * HEAD_DIM

    def body(q_ref, k_ref, v_ref, br_ref, ar_ref, al_ref, dt_ref, u_ref, w_ref, qd_ref, kd_ref, at_ref, eg_ref, inv_ref):
        ch = lambda r: _chunks(r, nc, GDN_CHUNK)
        u, w, qd, kd, attn, egl, inv = _gdn_prep(_mm_raw, ch(q_ref[...]), ch(k_ref[...]), ch(v_ref[...]), ch(br_ref[0]),
                                                 ch(ar_ref[0]), al_ref[0, 0:1, :], dt_ref[0, 0:1, :])
        u_ref[...] = u.reshape(tb, HEAD_DIM)
        w_ref[...] = w.reshape(tb, HEAD_DIM).astype(BF16)
        qd_ref[...] = qd.reshape(tb, HEAD_DIM).astype(BF16)
        kd_ref[...] = kd.reshape(tb, HEAD_DIM).astype(BF16)
        at_ref[0] = attn.reshape(tb, GDN_CHUNK).astype(BF16)
        eg_ref[0] = jnp.broadcast_to(egl, (nc, SUBLANES, LANES))
        inv_ref[0] = inv.reshape(tb, GDN_CHUNK)

    col, rep, par, att, egl = _gdn_prep_specs(tb)
    h = GDN_HEADS
    return pl.pallas_call(
        body, name="gdn_prep_fwd", grid=(h, nt),
        in_specs=[col(0), col(h), col(2 * h), rep, rep, par, par],
        out_specs=[col(0), col(0), col(0), col(0), att, egl, att],
        out_shape=[jax.ShapeDtypeStruct((t, width), F32)] + [jax.ShapeDtypeStruct((t, width), BF16)] * 3
        + [jax.ShapeDtypeStruct((h, t, GDN_CHUNK), BF16), jax.ShapeDtypeStruct((h, t // GDN_CHUNK, SUBLANES, LANES), F32),
           jax.ShapeDtypeStruct((h, t, GDN_CHUNK), F32)],
        compiler_params=_params("parallel", "parallel"))(qkv_act, qkv_act, qkv_act, braw, araw, alog, dtb)


def _gdn_prep_bwd(qkv_act, braw, araw, alog, dtb, inv, du, dw, dqd, dkd, dattn, degl):
    t = qkv_act.shape[0]
    tb = min(t, GDN_PREP_ROWS)
    nt, nc = t // tb, tb // GDN_CHUNK
    width = GDN_HEADS * HEAD_DIM

    def body(q_ref, k_ref, v_ref, br_ref, ar_ref, al_ref, dt_ref, inv_ref, du_ref, dw_ref, dqd_ref, dkd_ref, dat_ref,
             deg_ref, dq_ref, dk_ref, dv_ref, dbr_ref, dar_ref, dal_ref, ddt_ref):
        @pl.when(pl.program_id(1) == 0)
        def _():
            dal_ref[...] = jnp.zeros_like(dal_ref)
            ddt_ref[...] = jnp.zeros_like(ddt_ref)

        ch = lambda r: _chunks(r, nc, GDN_CHUNK)
        _, vjp = jax.vjp(functools.partial(_gdn_prep, _mm_vjp, inv_kept=ch(inv_ref[0])), ch(q_ref[...]), ch(k_ref[...]),
                         ch(v_ref[...]), ch(br_ref[0]), ch(ar_ref[0]), al_ref[0, 0:1, :], dt_ref[0, 0:1, :])
        dq, dk, dv, dbr, dar, dal, ddt = vjp((ch(du_ref[...]), ch(dw_ref[...]), ch(dqd_ref[...]), ch(dkd_ref[...]),
                                              ch(dat_ref[0]), deg_ref[0][:, 0:1, :]))
        dq_ref[...] = dq.reshape(tb, HEAD_DIM)
        dk_ref[...] = dk.reshape(tb, HEAD_DIM)
        dv_ref[...] = dv.reshape(tb, HEAD_DIM)
        dbr_ref[0] = _lane_total(dbr.reshape(tb, LANES))
        dar_ref[0] = _lane_total(dar.reshape(tb, LANES))
        dal_ref[0, 0:1, :] += _lane_total(dal)
        ddt_ref[0, 0:1, :] += _lane_total(ddt)

    col, rep, par, att, egl = _gdn_prep_specs(tb)
    h = GDN_HEADS
    return pl.pallas_call(
        body, name="gdn_prep_bwd", grid=(h, nt),
        in_specs=[col(0), col(h), col(2 * h), rep, rep, par, par, att, col(0), col(0), col(0), col(0), att, egl],
        out_specs=[col(0), col(0), col(0), rep, rep, par, par],
        out_shape=[jax.ShapeDtypeStruct((t, width), F32)] * 3 + [jax.ShapeDtypeStruct((h, t, LANES), F32)] * 2
        + [jax.ShapeDtypeStruct((h, SUBLANES, LANES), F32)] * 2,
        compiler_params=_params("parallel", "arbitrary"))(qkv_act, qkv_act, qkv_act, braw, araw, alog, dtb, inv,
                                                         du, dw, dqd, dkd, dattn, degl)


def _scan_specs(tb, heads, chunk, rev, nt):
    ti = (lambda i: nt - 1 - i) if rev else (lambda i: i)
    row = pl.BlockSpec((tb, heads * HEAD_DIM), lambda i: (ti(i), 0))
    att = pl.BlockSpec((heads, tb, chunk), lambda i: (0, ti(i), 0))
    egl = pl.BlockSpec((heads, tb // chunk, SUBLANES, LANES), lambda i: (0, ti(i), 0, 0))
    hist = pl.BlockSpec((heads, tb // chunk, HEAD_DIM, HEAD_DIM), lambda i: (0, ti(i), 0, 0))
    gn = pl.BlockSpec((SUBLANES, LANES), lambda i: (0, 0))
    return row, att, egl, hist, gn


def _gdn_scan_fwd(u, w, qd, kd, attn, egl, zb, gn):
    t = u.shape[0]
    tb = min(t, 256)
    nt, nc = t // tb, tb // GDN_CHUNK
    nh = GDN_HEADS

    def body(u_ref, w_ref, qd_ref, kd_ref, at_ref, eg_ref, z_ref, gn_ref, y_ref, hist_ref, s_ref):
        @pl.when(pl.program_id(0) == 0)
        def _():
            s_ref[...] = jnp.zeros_like(s_ref)

        g = gn_ref[0:1, :]
        state = s_ref[...]
        for c in range(nc):
            rows = pl.ds(c * GDN_CHUNK, GDN_CHUNK)
            heads = lambda r: _by_head(r, rows, nh)
            hist_ref[:, c] = state
            y, state = _gdn_scan(_mm_raw, heads(u_ref), heads(w_ref), heads(qd_ref), heads(kd_ref), at_ref[:, rows, :],
                                 eg_ref[:, c, 0:1, :], heads(z_ref), g, state)
            _store_heads(y_ref, rows, y.astype(BF16))
        s_ref[...] = state

    row, att, egs, hist, gns = _scan_specs(tb, nh, GDN_CHUNK, False, nt)
    return pl.pallas_call(
        body, name="gdn_scan_fwd", grid=(nt,), in_specs=[row, row, row, row, att, egs, row, gns], out_specs=[row, hist],
        out_shape=[jax.ShapeDtypeStruct((t, nh * HEAD_DIM), BF16),
                   jax.ShapeDtypeStruct((nh, t // GDN_CHUNK, HEAD_DIM, HEAD_DIM), F32)],
        scratch_shapes=[pltpu.VMEM((nh, HEAD_DIM, HEAD_DIM), F32)],
        compiler_params=_params("arbitrary"))(u, w, qd, kd, attn, egl, zb, gn)


def _gdn_scan_bwd(u, w, qd, kd, attn, egl, zb, gn, hist, dy):
    t = u.shape[0]
    tb = min(t, 256)
    nt, nc = t // tb, tb // GDN_CHUNK
    nh = GDN_HEADS

    def body(u_ref, w_ref, qd_ref, kd_ref, at_ref, eg_ref, z_ref, gn_ref, hist_ref, dy_ref,
             du_ref, dw_ref, dqd_ref, dkd_ref, dat_ref, deg_ref, dz_ref, dgn_ref, ds_ref):
        @pl.when(pl.program_id(0) == 0)
        def _():
            ds_ref[...] = jnp.zeros_like(ds_ref)
            dgn_ref[...] = jnp.zeros_like(dgn_ref)

        g = gn_ref[0:1, :]
        d_state = ds_ref[...]
        for c in reversed(range(nc)):
            rows = pl.ds(c * GDN_CHUNK, GDN_CHUNK)
            heads = lambda r: _by_head(r, rows, nh).astype(F32)
            _, vjp = jax.vjp(functools.partial(_gdn_scan, _mm_vjp), heads(u_ref), heads(w_ref), heads(qd_ref),
                             heads(kd_ref), at_ref[:, rows, :].astype(F32), eg_ref[:, c, 0:1, :], heads(z_ref), g,
                             hist_ref[:, c])
            du, dw, dqd, dkd, dat, deg, dz, dgn, d_state = vjp((heads(dy_ref), d_state))
            _store_heads(du_ref, rows, du)
            _store_heads(dw_ref, rows, dw)
            _store_heads(dqd_ref, rows, dqd)
            _store_heads(dkd_ref, rows, dkd)
            dat_ref[:, rows, :] = dat
            deg_ref[:, c] = jnp.broadcast_to(deg, (nh, SUBLANES, LANES))
            _store_heads(dz_ref, rows, dz.astype(BF16))
            dgn_ref[0:1, :] += dgn
        ds_ref[...] = d_state

    row, att, egs, hists, gns = _scan_specs(tb, nh, GDN_CHUNK, True, nt)
    wide = jax.ShapeDtypeStruct((t, nh * HEAD_DIM), F32)
    return pl.pallas_call(
        body, name="gdn_scan_bwd", grid=(nt,),
        in_specs=[row, row, row, row, att, egs, row, gns, hists, row],
        out_specs=[row, row, row, row, att, egs, row, gns],
        out_shape=[wide] * 4 + [jax.ShapeDtypeStruct((nh, t, GDN_CHUNK), F32),
                                jax.ShapeDtypeStruct((nh, t // GDN_CHUNK, SUBLANES, LANES), F32),
                                jax.ShapeDtypeStruct((t, nh * HEAD_DIM), BF16),
                                jax.ShapeDtypeStruct((SUBLANES, LANES), F32)],
        scratch_shapes=[pltpu.VMEM((nh, HEAD_DIM, HEAD_DIM), F32)],
        compiler_params=_params("arbitrary"))(u, w, qd, kd, attn, egl, zb, gn, hist, dy)


def _hgrn_prep(mm, qr, fr, lbl):
    n, c, _ = qr.shape
    lb = jax.nn.sigmoid(lbl[1:2, :] - lbl[0:1, :])
    f = lb + (1.0 - lb) * jax.nn.sigmoid(fr)
    q = jax.nn.silu(qr)
    k = 1.0 - f
    logf = jnp.log(f)
    ri = lax.broadcasted_iota(jnp.int32, (n, c, c), 1)
    ci = lax.broadcasted_iota(jnp.int32, (n, c, c), 2)
    b = mm((ri >= ci).astype(F32), logf, "nn", EXACT_LHS)
    attn = _hgrn_attn(mm, q, k, b)
    b_last = jnp.sum(logf, axis=1, keepdims=True)
    return q * jnp.exp(b), k * jnp.exp(b_last - b), attn, jnp.exp(b_last)


HGRN_SUB = 8
HGRN_PREP_ROWS = 2048


@functools.partial(jax.custom_vjp, nondiff_argnums=(1,))
def _roll_rows(x, shift):
    return pltpu.roll(x, shift, x.ndim - 2)


def _roll_rows_fwd(x, shift):
    return _roll_rows(x, shift), None


def _roll_rows_bwd(shift, _, d):
    return (pltpu.roll(d, d.shape[-2] - shift, d.ndim - 2),)


_roll_rows.defvjp(_roll_rows_fwd, _roll_rows_bwd)


@jax.custom_vjp
def _exp_clamped(v):
    return jnp.exp(jnp.minimum(v, 0.0))


def _exp_clamped_fwd(v):
    out = jnp.exp(jnp.minimum(v, 0.0))
    return out, out


def _exp_clamped_bwd(out, d):
    return (d * out,)


_exp_clamped.defvjp(_exp_clamped_fwd, _exp_clamped_bwd)


def _hgrn_attn(mm, q, k, b):
    n, c, d = q.shape
    sb = HGRN_SUB
    sub = lambda a: a.reshape(n * c // sb, sb, d)
    qs, ks, bs = sub(q), sub(k), sub(b)
    row = lax.broadcasted_iota(jnp.int32, (n, c, c), 1)
    col = lax.broadcasted_iota(jnp.int32, (n, c, c), 2)
    same_block = (row & -sb) == (col & -sb)
    attn = None
    for delta in range(sb):
        if delta == 0:
            prod = qs * ks
        else:
            prod = qs * _roll_rows(ks, delta) * _exp_clamped(bs - _roll_rows(bs, delta))
        sums = jnp.sum(prod, axis=-1, keepdims=True).reshape(n, c, 1)
        term = jnp.where(same_block & (row - col == delta), sums, 0.0)
        attn = term if attn is None else attn + term
    far = [jnp.zeros((n, sb, c), F32)]
    col8 = lax.broadcasted_iota(jnp.int32, (n, sb, c), 2)
    for i in range(1, c // sb):
        r0 = i * sb
        bi = b[:, r0:r0 + sb, :]
        ref = bi[:, 0:1, :]
        part = mm(q[:, r0:r0 + sb, :] * jnp.exp(bi - ref), k * _exp_clamped(ref - b), "nt", ONE_PASS)
        far.append(jnp.where(col8 < r0, part, 0.0))
    return attn + jnp.concatenate(far, axis=1)


def _hgrn_scan(mm, qe, kd, attn, ebl, iv, z, gn, state):
    o = mm(qe, state, "nt", ONE_PASS) + mm(attn, iv, "nn", ONE_PASS)
    new_state = state * ebl + mm(iv, kd, "tn", ONE_PASS)
    return _rms_gate(o, gn, z), new_state


def _hgrn_prep_specs(tb):
    col = pl.BlockSpec((tb, HEAD_DIM), lambda h, i: (i, h))
    lbs = pl.BlockSpec((2, HEAD_DIM), lambda h, i: (0, h))
    att = pl.BlockSpec((1, tb, HGRN_CHUNK), lambda h, i: (h, i, 0))
    ebl = pl.BlockSpec((1, tb // HGRN_CHUNK, SUBLANES, LANES), lambda h, i: (h, i, 0, 0))
    return col, lbs, att, ebl


def _hgrn_prep_fwd(qr, fr, lower_bounds):
    t = qr.shape[0]
    tb = min(t, HGRN_PREP_ROWS)
    nt, nc = t // tb, tb // HGRN_CHUNK
    hh = HGRN_HEADS

    def body(q_ref, f_ref, lb_ref, qe_ref, kd_ref, at_ref, eb_ref):
        ch = lambda r: _chunks(r, nc, HGRN_CHUNK)
        qe, kd, attn, ebl = _hgrn_prep(_mm_raw, ch(q_ref[...]), ch(f_ref[...]), lb_ref[...])
        qe_ref[...] = qe.reshape(tb, HEAD_DIM).astype(BF16)
        kd_ref[...] = kd.reshape(tb, HEAD_DIM).astype(BF16)
        at_ref[0] = attn.reshape(tb, HGRN_CHUNK).astype(BF16)
        eb_ref[0] = jnp.broadcast_to(ebl, (nc, SUBLANES, LANES))

    col, lbs, att, ebs = _hgrn_prep_specs(tb)
    return pl.pallas_call(
        body, name="hgrn_prep_fwd", grid=(hh, nt), in_specs=[col, col, lbs], out_specs=[col, col, att, ebs],
        out_shape=[jax.ShapeDtypeStruct((t, HGRN_WIDTH), BF16)] * 2
        + [jax.ShapeDtypeStruct((hh, t, HGRN_CHUNK), BF16), jax.ShapeDtypeStruct((hh, t // HGRN_CHUNK, SUBLANES, LANES), F32)],
        compiler_params=_params("parallel", "parallel"))(qr, fr, lower_bounds)


def _hgrn_prep_bwd(qr, fr, lower_bounds, dqe, dkd, dattn, debl):
    t = qr.shape[0]
    tb = min(t, HGRN_PREP_ROWS)
    nt, nc = t // tb, tb // HGRN_CHUNK
    hh = HGRN_HEADS

    def body(q_ref, f_ref, lb_ref, dqe_ref, dkd_ref, dat_ref, deb_ref, dq_ref, df_ref, dlb_ref):
        @pl.when(pl.program_id(1) == 0)
        def _():
            dlb_ref[...] = jnp.zeros_like(dlb_ref)

        ch = lambda r: _chunks(r, nc, HGRN_CHUNK)
        _, vjp = jax.vjp(functools.partial(_hgrn_prep, _mm_vjp), ch(q_ref[...]), ch(f_ref[...]), lb_ref[...])
        dq, df, dlb = vjp((ch(dqe_ref[...]), ch(dkd_ref[...]), ch(dat_ref[0]), deb_ref[0][:, 0:1, :]))
        dq_ref[...] = dq.reshape(tb, HEAD_DIM).astype(BF16)
        df_ref[...] = df.reshape(tb, HEAD_DIM).astype(BF16)
        dlb_ref[...] += dlb

    col, lbs, att, ebs = _hgrn_prep_specs(tb)
    return pl.pallas_call(
        body, name="hgrn_prep_bwd", grid=(hh, nt), in_specs=[col, col, lbs, col, col, att, ebs],
        out_specs=[col, col, lbs],
        out_shape=[jax.ShapeDtypeStruct((t, HGRN_WIDTH), BF16)] * 2 + [jax.ShapeDtypeStruct((2, HGRN_WIDTH), F32)],
        compiler_params=_params("parallel", "arbitrary"))(qr, fr, lower_bounds, dqe, dkd, dattn, debl)


def _hgrn_scan_fwd(qe, kd, attn, ebl, iv, z, gn):
    t = qe.shape[0]
    tb = min(t, 128)
    nt, nc = t // tb, tb // HGRN_CHUNK
    hh = HGRN_HEADS

    def body(qe_ref, kd_ref, at_ref, eb_ref, i_ref, z_ref, gn_ref, y_ref, hist_ref, s_ref):
        @pl.when(pl.program_id(0) == 0)
        def _():
            s_ref[...] = jnp.zeros_like(s_ref)

        g = gn_ref[0:1, :]
        state = s_ref[...]
        for c in range(nc):
            rows = pl.ds(c * HGRN_CHUNK, HGRN_CHUNK)
            heads = lambda r: _by_head(r, rows, hh)
            hist_ref[:, c] = state
            y, state = _hgrn_scan(_mm_raw, heads(qe_ref), heads(kd_ref), at_ref[:, rows, :], eb_ref[:, c, 0:1, :],
                                  heads(i_ref), heads(z_ref), g, state)
            _store_heads(y_ref, rows, y.astype(BF16))
        s_ref[...] = state

    row, att, ebs, hist, gns = _scan_specs(tb, hh, HGRN_CHUNK, False, nt)
    return pl.pallas_call(
        body, name="hgrn_scan_fwd", grid=(nt,), in_specs=[row, row, att, ebs, row, row, gns], out_specs=[row, hist],
        out_shape=[jax.ShapeDtypeStruct((t, HGRN_WIDTH), BF16),
                   jax.ShapeDtypeStruct((hh, t // HGRN_CHUNK, HEAD_DIM, HEAD_DIM), F32)],
        scratch_shapes=[pltpu.VMEM((hh, HEAD_DIM, HEAD_DIM), F32)],
        compiler_params=_params("arbitrary"))(qe, kd, attn, ebl, iv, z, gn)


def _hgrn_scan_bwd(qe, kd, attn, ebl, iv, z, gn, hist, dy):
    t = qe.shape[0]
    tb = min(t, 128)
    nt, nc = t // tb, tb // HGRN_CHUNK
    hh = HGRN_HEADS

    def body(qe_ref, kd_ref, at_ref, eb_ref, i_ref, z_ref, gn_ref, hist_ref, dy_ref,
             dqe_ref, dkd_ref, dat_ref, deb_ref, di_ref, dz_ref, dgn_ref, ds_ref):
        @pl.when(pl.program_id(0) == 0)
        def _():
            ds_ref[...] = jnp.zeros_like(ds_ref)
            dgn_ref[...] = jnp.zeros_like(dgn_ref)

        g = gn_ref[0:1, :]
        d_state = ds_ref[...]
        for c in reversed(range(nc)):
            rows = pl.ds(c * HGRN_CHUNK, HGRN_CHUNK)
            heads = lambda r: _by_head(r, rows, hh).astype(F32)
            _, vjp = jax.vjp(functools.partial(_hgrn_scan, _mm_vjp), heads(qe_ref), heads(kd_ref),
                             at_ref[:, rows, :].astype(F32), eb_ref[:, c, 0:1, :], heads(i_ref), heads(z_ref), g,
                             hist_ref[:, c])
            dqe, dkd, dat, deb, di, dz, dgn, d_state = vjp((heads(dy_ref), d_state))
            _store_heads(dqe_ref, rows, dqe)
            _store_heads(dkd_ref, rows, dkd)
            dat_ref[:, rows, :] = dat
            deb_ref[:, c] = jnp.broadcast_to(deb, (hh, SUBLANES, LANES))
            _store_heads(di_ref, rows, di.astype(BF16))
            _store_heads(dz_ref, rows, dz.astype(BF16))
            dgn_ref[0:1, :] += dgn
        ds_ref[...] = d_state

    row, att, ebs, hists, gns = _scan_specs(tb, hh, HGRN_CHUNK, True, nt)
    wide = lambda dt: jax.ShapeDtypeStruct((t, HGRN_WIDTH), dt)
    return pl.pallas_call(
        body, name="hgrn_scan_bwd", grid=(nt,),
        in_specs=[row, row, att, ebs, row, row, gns, hists, row],
        out_specs=[row, row, att, ebs, row, row, gns],
        out_shape=[wide(F32), wide(F32), jax.ShapeDtypeStruct((hh, t, HGRN_CHUNK), F32),
                   jax.ShapeDtypeStruct((hh, t // HGRN_CHUNK, SUBLANES, LANES), F32), wide(BF16), wide(BF16),
                   jax.ShapeDtypeStruct((SUBLANES, LANES), F32)],
        scratch_shapes=[pltpu.VMEM((hh, HEAD_DIM, HEAD_DIM), F32)],
        compiler_params=_params("arbitrary"))(qe, kd, attn, ebl, iv, z, gn, hist, dy)


def _layer_norm(pre, g, b):
    mu = jnp.mean(pre, axis=-1, keepdims=True)
    d = pre - mu
    var = jnp.mean(d * d, axis=-1, keepdims=True)
    return d * lax.rsqrt(var + NORM_EPS) * g + b


def _lnpl_fwd(xin, s, p, wg, wpl, ln_g, ln_b):
    t = xin.shape[0]
    tt = min(t, 256)

    def body(x_ref, s_ref, p_ref, wg_ref, wpl_ref, g_ref, b_ref, o_ref, ob_ref):
        xn = _layer_norm(DEEPNORM_ALPHA * x_ref[...] + s_ref[...], g_ref[...], b_ref[...])
        gate = jax.nn.sigmoid(_mm_raw(xn, wg_ref[...], "nn", False))
        out = xn + _mm_raw(p_ref[...], wpl_ref[...], "nn", False) * gate
        o_ref[...] = out
        ob_ref[...] = out.astype(BF16)

    row = lambda w: pl.BlockSpec((tt, w), lambda i: (i, 0))
    full = lambda a: pl.BlockSpec(a.shape, lambda i: (0, 0))
    return pl.pallas_call(
        body, name="lnpl_fwd", grid=(t // tt,),
        in_specs=[row(D_MODEL), row(D_MODEL), row(PL_DIM), full(wg), full(wpl), full(ln_g), full(ln_b)],
        out_specs=[row(D_MODEL), row(D_MODEL)],
        out_shape=[jax.ShapeDtypeStruct((t, D_MODEL), F32), jax.ShapeDtypeStruct((t, D_MODEL), BF16)],
        compiler_params=_params("parallel"))(xin, s, p, wg, wpl, ln_g, ln_b)


def _lnpl_bwd(xin, s, p, wg, wpl, ln_g, ln_b, upstream, last, name):
    t = xin.shape[0]
    tt = min(t, 256)

    def body(x_ref, s_ref, p_ref, wg_ref, wpl_ref, g_ref, b_ref, up_ref,
             dpre_ref, dwg_ref, dwpl_ref, dg_ref, db_ref, loss_ref):
        @pl.when(pl.program_id(0) == 0)
        def _():
            for r in (dwg_ref, dwpl_ref, dg_ref, db_ref, loss_ref):
                r[...] = jnp.zeros_like(r)

        pre = DEEPNORM_ALPHA * x_ref[...] + s_ref[...]
        xn, ln_vjp = jax.vjp(_layer_norm, pre, g_ref[...], b_ref[...])
        gate = jax.nn.sigmoid(_mm_raw(xn, wg_ref[...], "nn", False))
        plv = _mm_raw(p_ref[...], wpl_ref[...], "nn", False)
        if last:
            err = xn + plv * gate - up_ref[...]
            dout = err * (1.0 / D_MODEL)
            tot = jnp.sum(jnp.sum(err * err, axis=1, keepdims=True), axis=0, keepdims=True) * (0.5 / D_MODEL)
            loss_ref[...] += jnp.broadcast_to(tot, loss_ref.shape)
        else:
            dout = up_ref[...]
        dplv = dout * gate
        dlogits = dout * plv * gate * (1.0 - gate)
        dwg_ref[...] += _mm_raw(xn, dlogits, "tn", False)
        dwpl_ref[...] += _mm_raw(p_ref[...], dplv, "tn", False)
        dxn = dout + _mm_raw(dlogits, wg_ref[...], "nt", False)
        dpre, dg, db = ln_vjp(dxn)
        dpre_ref[...] = dpre
        dg_ref[...] += dg
        db_ref[...] += db

    row = lambda w: pl.BlockSpec((tt, w), lambda i: (i, 0))
    full = lambda shape: pl.BlockSpec(shape, lambda i: (0, 0))
    return pl.pallas_call(
        body, name=name, grid=(t // tt,),
        in_specs=[row(D_MODEL), row(D_MODEL), row(PL_DIM), full(wg.shape), full(wpl.shape), full(ln_g.shape),
                  full(ln_b.shape), row(D_MODEL)],
        out_specs=[row(D_MODEL), full(wg.shape), full(wpl.shape), full(ln_g.shape), full(ln_b.shape),
                   full((SUBLANES, LANES))],
        out_shape=[jax.ShapeDtypeStruct((t, D_MODEL), F32), jax.ShapeDtypeStruct(wg.shape, F32),
                   jax.ShapeDtypeStruct(wpl.shape, F32), jax.ShapeDtypeStruct(ln_g.shape, F32),
                   jax.ShapeDtypeStruct(ln_b.shape, F32), jax.ShapeDtypeStruct((SUBLANES, LANES), F32)],
        compiler_params=_params("arbitrary"))(xin, s, p, wg, wpl, ln_g, ln_b, upstream)


def _pack_tail(dbr, dar):
    nh, t, _ = dbr.shape
    tt = min(t, 512)

    def body(b_ref, a_ref, o_ref):
        lane = lax.broadcasted_iota(jnp.int32, (tt, LANES), 1)
        acc = jnp.zeros((tt, LANES), F32)
        for h in range(nh):
            acc = jnp.where(lane == h, b_ref[h], acc)
            acc = jnp.where(lane == nh + h, a_ref[h], acc)
        o_ref[...] = acc.astype(BF16)

    spec = pl.BlockSpec((nh, tt, LANES), lambda i: (0, i, 0))
    return pl.pallas_call(
        body, name="pack_tail", grid=(t // tt,), in_specs=[spec, spec], out_specs=pl.BlockSpec((tt, LANES), lambda i: (i, 0)),
        out_shape=jax.ShapeDtypeStruct((t, LANES), BF16), compiler_params=_params("parallel"))(dbr, dar)


def _rep_rows(v):
    return jnp.broadcast_to(v.reshape(1, LANES), (SUBLANES, LANES))


def _rep_heads(v):
    return jnp.broadcast_to(v.reshape(-1, 1, 1), (v.shape[0], SUBLANES, LANES))


def _col_range(stacked, lo, hi):
    c = stacked.shape[2]
    parts = [stacked[s, :, max(lo, s * c) - s * c:min(hi, (s + 1) * c) - s * c]
             for s in range(4) if max(lo, s * c) < min(hi, (s + 1) * c)]
    return parts[0] if len(parts) == 1 else jnp.concatenate(parts, axis=1)


def _col_shards(pieces, c):
    shards, offs, o = [], [], 0
    for pc in pieces:
        offs.append(o)
        o += pc.shape[1]
    for s in range(4):
        lo, hi = s * c, (s + 1) * c
        parts = [pc[:, max(lo, o) - o:min(hi, o + pc.shape[1]) - o] for pc, o in zip(pieces, offs)
                 if max(lo, o) < min(hi, o + pc.shape[1])]
        shards.append(parts[0] if len(parts) == 1 else jnp.concatenate(parts, axis=1))
    return jnp.stack(shards)


def _local_step(x, p, target, w, late_weights, early_grads_ready, early_grads_swapped, last_grad_ready, start_token):
    a = DEEPNORM_ALPHA
    nh = GDN_HEADS
    xb = (x + start_token).astype(BF16)
    wie = w["w_in_even"]
    w_a, w_qkv, w_zb = _col_range(wie, 0, 4096), _col_range(wie, 4096, 7168), _col_range(wie, 7168, 8192)
    w_tail = jnp.pad(_col_range(wie, 8192, 8192 + 2 * nh), ((0, 0), (0, LANES - 2 * nh)))
    conv_a_w, conv_b_w = w["conv_a_w"], w["conv_b_w"]
    ln_g0, ln_b0, ln_g1, ln_b1 = (v.reshape(1, D_MODEL) for v in (w["ln_g"][0], w["ln_b"][0], w["ln_g"][1], w["ln_b"][1]))
    alog, dtb = _rep_heads(w["a_log"].reshape(nh)), _rep_heads(w["dt_bias"].reshape(nh))
    gdn_g, hgrn_g = _rep_rows(w["gdn_norm_g"]), _rep_rows(w["hgrn_norm_g"])

    proj_a = _matmul(xb, w_a, name="fwd_proj_a")
    proj_qkv = _matmul(xb, w_qkv, name="fwd_proj_qkv")
    proj_zb = _matmul(xb, w_zb, name="fwd_proj_zb")
    proj_tail = _matmul(xb, w_tail, name="fwd_proj_tail")
    rep = lambda cols: jnp.broadcast_to(cols.T[:, :, None], (nh, cols.shape[0], LANES))
    braw, araw = rep(proj_tail[:, :nh]), rep(proj_tail[:, nh:2 * nh])
    y_a = _conv_a_fwd(proj_a, conv_a_w)
    qkv_act = _conv_b_fwd(proj_qkv, conv_b_w)
    *gdn_pre, gdn_inv = _gdn_prep_fwd(qkv_act, braw, araw, alog, dtb)
    y_b, gdn_hist = _gdn_scan_fwd(*gdn_pre, proj_zb, gdn_g)
    w = {**w, **late_weights(y_b)}
    woe, wio, woo = w["w_out_even"], w["w_in_odd"], w["w_out_odd"]
    s0 = _matmul(y_b, woe[1024:], name="fwd_out_even_b", add=_matmul(y_a, woe[:1024], name="fwd_out_even_a"))
    x1, x1b = _lnpl_fwd(x, s0, p[0], w["w_pl_gate"][0], w["w_pl"][0], ln_g0, ln_b0)
    proj_o = [_matmul(x1b, wio[j], name=f"fwd_proj_odd{j}") for j in range(4)]
    hgrn_pre = _hgrn_prep_fwd(proj_o[0], proj_o[1], w["lower_bounds"])
    y_o, hgrn_hist = _hgrn_scan_fwd(*hgrn_pre, proj_o[2], proj_o[3], hgrn_g)
    s1 = _matmul(y_o, woo, name="fwd_out_odd")

    g = {}
    dpre1, dwg1, dwpl1, dlng1, dlnb1, loss = _lnpl_bwd(x1, s1, p[1], w["w_pl_gate"][1], w["w_pl"][1], ln_g1, ln_b1,
                                                     target, True, "lnpl_bwd_odd")
    dy_o = _matmul(dpre1, woo, tb=True, name="bwd_out_odd_dx")
    g["w_out_odd"] = _matmul(y_o, dpre1, ta=True, name="bwd_out_odd_dw")
    dqe, dkd, dat, deb, di, dz, dhg = _hgrn_scan_bwd(*hgrn_pre, proj_o[2], proj_o[3], hgrn_g, hgrn_hist, dy_o)
    dq, df, dlb = _hgrn_prep_bwd(proj_o[0], proj_o[1], w["lower_bounds"], dqe, dkd, dat, deb)
    dx1 = dpre1
    scale = a
    dws = []
    for j, dj in enumerate((dq, df, di, dz)):
        dx1 = _matmul(dj, wio[j], tb=True, add=dx1, add_scale=scale, name=f"bwd_proj_odd_dx{j}")
        scale = 1.0
        dws.append(_matmul(x1b, dj, ta=True, name=f"bwd_proj_odd_dw{j}"))
    g["w_in_odd"] = jnp.stack(dws)
    g["hgrn_norm_g"] = dhg[0:1]
    g["lower_bounds"] = dlb
    g["w_pl_gate1"], g["w_pl1"] = dwg1, dwpl1

    dpre0, dwg0, dwpl0, dlng0, dlnb0, _ = _lnpl_bwd(x, s0, p[0], w["w_pl_gate"][0], w["w_pl"][0], ln_g0, ln_b0,
                                                  dx1, False, "lnpl_bwd_even")
    g["w_pl_gate0"], g["w_pl0"] = dwg0, dwpl0
    g["ln_g"] = jnp.concatenate([dlng0, dlng1], axis=0)
    g["ln_b"] = jnp.concatenate([dlnb0, dlnb1], axis=0)
    dy_a = _matmul(dpre0, woe[:1024], tb=True, name="bwd_out_even_dxa")
    dy_b = _matmul(dpre0, woe[1024:], tb=True, name="bwd_out_even_dxb")
    g["w_out_even"] = jnp.concatenate([_matmul(y_a, dpre0, ta=True, name="bwd_out_even_dwa"),
                                       _matmul(y_b, dpre0, ta=True, name="bwd_out_even_dwb")], axis=0)
    token = early_grads_ready({n: g[n] for n in _RS_EARLY})
    d_a, dwa = _conv_a_bwd(proj_a, conv_a_w + token, dy_a)
    g["conv_a_w"] = dwa[:3]
    gdn_g = gdn_g + early_grads_swapped(d_a)
    du, dw, dqd, dkd, dat, deg, dzb, dgn = _gdn_scan_bwd(*gdn_pre, proj_zb, gdn_g, gdn_hist, dy_b)
    dqa, dka, dva, dbr, dar, dal, ddt = _gdn_prep_bwd(qkv_act, braw, araw, alog, dtb, gdn_inv, du, dw, dqd, dkd, dat, deg)
    g["a_log"] = dal[:, 0, 0].reshape(1, nh)
    g["dt_bias"] = ddt[:, 0, 0].reshape(1, nh)
    g["gdn_norm_g"] = dgn[0:1]
    d_pre_qkv, dwb = [], []
    for j, dj in enumerate((dqa, dka, dva)):
        dpj, dwj = _conv_b_bwd(proj_qkv, conv_b_w, dj, j, f"conv_b_bwd{j}")
        d_pre_qkv.append(dpj)
        dwb.append(dwj[:4])
    g["conv_b_w"] = jnp.concatenate(dwb, axis=1)
    d_tail = _pack_tail(dbr, dar)
    pieces = [(d_a, w_a), (d_pre_qkv[0], w_qkv[:, :1024]), (d_pre_qkv[1], w_qkv[:, 1024:2048]),
              (d_pre_qkv[2], w_qkv[:, 2048:]), (dzb, w_zb), (d_tail, w_tail)]
    dws = [_matmul(xb, dj, ta=True, name=f"bwd_proj_even_dw{j}") for j, (dj, _) in enumerate(pieces)]
    dws[-1] = dws[-1][:, :2 * nh]
    g["w_in_even"] = _col_shards(dws, wie.shape[2])
    token = last_grad_ready(g["w_in_even"])
    dx = dpre0
    scale = a
    for j, (dj, wj) in enumerate(pieces):
        dx = _matmul(dj, wj + jnp.asarray(token).astype(BF16) if j == 0 else wj, tb=True, add=dx, add_scale=scale,
                     name=f"bwd_proj_even_dx{j}")
        scale = 1.0
    return loss, dx, g


def _adamw(w, g, m, v, name):
    lead, rows, cols = w.shape
    if rows % SUBLANES == 0 or rows <= 256:
        tr, tc = (rows if rows <= 256 else 256), cols
    else:
        tr, tc = rows, 256
    assert rows % tr == 0 and cols % tc == 0, (name, rows, cols)

    def body(w_ref, g_ref, m_ref, v_ref, d_ref, nm_ref, nv_ref):
        gg = g_ref[...]
        nm = ADAM_B1 * m_ref[...] + (1.0 - ADAM_B1) * gg
        nv = ADAM_B2 * v_ref[...] + (1.0 - ADAM_B2) * jnp.square(gg)
        m_hat = nm / (1.0 - ADAM_B1 ** ADAM_STEP)
        v_hat = nv / (1.0 - ADAM_B2 ** ADAM_STEP)
        d_ref[...] = -ADAM_LR * (m_hat / (jnp.sqrt(v_hat) + ADAM_EPS) + ADAM_WD * w_ref[...])
        nm_ref[...] = nm
        nv_ref[...] = nv

    spec = pl.BlockSpec((1, tr, tc), lambda l, i, j: (l, i, j))
    return pl.pallas_call(
        body, name=name, grid=(lead, rows // tr, cols // tc), in_specs=[spec] * 4, out_specs=[spec] * 3,
        out_shape=[jax.ShapeDtypeStruct(w.shape, F32)] * 3,
        compiler_params=_params("parallel", "parallel", "parallel"))(w, g, m, v)


MESH = pl.DeviceIdType.MESH
N_DEV = 8
HBM_SPEC = pl.BlockSpec(memory_space=pltpu.HBM)
VMEM_SPEC = pl.BlockSpec(memory_space=pltpu.VMEM)


def _coords():
    return lax.axis_index("x"), lax.axis_index("y"), lax.axis_index("c")


def _flip(v, bit):
    return 1 - v if bit else v


def _remote(src, dst, send_sem, recv_sem, dev):
    return pltpu.make_async_remote_copy(src_ref=src, dst_ref=dst, send_sem=send_sem, recv_sem=recv_sem,
                                        device_id=dev, device_id_type=MESH)


def _exchange_small(buf, reduce, name):
    rows = buf.shape[0]

    def body(in_ref, out_ref, slots, send_sems, recv_sems):
        x, y, c = _coords()
        me = 4 * x + 2 * y + c
        slots[me] = in_ref[...]
        peer = lambda k: (_flip(x, (k >> 2) & 1), _flip(y, (k >> 1) & 1), _flip(c, k & 1))
        sends = []
        for k in range(1, N_DEV):
            cp = _remote(in_ref, slots.at[me], send_sems.at[k - 1], recv_sems.at[k - 1], peer(k))
            cp.start()
            sends.append(cp)
        for k in range(1, N_DEV):
            px, py, pc = peer(k)
            _remote(in_ref, slots.at[4 * px + 2 * py + pc], send_sems.at[k - 1], recv_sems.at[k - 1], peer(k)).wait_recv()
        for cp in sends:
            cp.wait_send()
        if reduce:
            acc = slots[0]
            for d in range(1, N_DEV):
                acc = acc + slots[d]
            out_ref[...] = acc
        else:
            out_ref[...] = slots[...]

    out_shape = (rows, LANES) if reduce else (N_DEV, rows, LANES)
    return pl.pallas_call(
        body, name=name, in_specs=[VMEM_SPEC], out_specs=VMEM_SPEC, out_shape=jax.ShapeDtypeStruct(out_shape, F32),
        scratch_shapes=[pltpu.VMEM((N_DEV, rows, LANES), F32), pltpu.SemaphoreType.DMA((N_DEV - 1,)),
                        pltpu.SemaphoreType.DMA((N_DEV - 1,))])(buf)


def _half_rows(half, which):
    return pl.ds(pl.multiple_of(which * half, 16), half)


def _other_chip(x, y, k):
    return _flip(x, (k >> 1) & 1), _flip(y, k & 1)


SEM_SPEC = pl.BlockSpec(memory_space=pltpu.SEMAPHORE)
DATAFLOW = pltpu.SideEffectType.DATAFLOW_SIDE_EFFECTING


def _ici_piece(srcs, lands, send_sems, recv_sems, i, k, x, y, c):
    half = srcs[i].shape[0] // 2
    ox, oy = _other_chip(x, y, k)
    return _remote(srcs[i].at[_half_rows(half, c)], lands[i].at[2 * x + y, _half_rows(half, c)],
                   send_sems.at[3 * i + k - 1], recv_sems.at[3 * i + k - 1], (ox, oy, c)), (ox, oy)


def _gather_start(shards, name):
    n = len(shards)

    def body(*refs):
        srcs, lands = refs[:n], refs[n:2 * n]
        send_sems, recv_sems = refs[2 * n], refs[2 * n + 1]
        token = refs[-1]
        x, y, c = _coords()
        for i in range(n):
            for k in (1, 2, 3):
                _ici_piece(srcs, lands, send_sems, recv_sems, i, k, x, y, c)[0].start()
        token[...] = jnp.zeros_like(token)

    hbm = lambda a: pltpu.with_memory_space_constraint(a, pltpu.HBM)
    lands = [lax.empty((4,) + s.shape, s.dtype) for s in shards]
    out = pl.pallas_call(
        body, name=name,
        out_shape=(pltpu.SemaphoreType.DMA((3 * n,)), pltpu.SemaphoreType.DMA((3 * n,)),
                   *[pltpu.HBM(s.shape, s.dtype) for s in shards], *[pltpu.HBM(a.shape, a.dtype) for a in lands],
                   jax.ShapeDtypeStruct((SUBLANES, LANES), F32)),
        in_specs=[HBM_SPEC] * (2 * n), out_specs=(SEM_SPEC, SEM_SPEC, *[HBM_SPEC] * (2 * n), VMEM_SPEC),
        input_output_aliases={i: 2 + i for i in range(2 * n)},
        compiler_params=pltpu.CompilerParams(has_side_effects=DATAFLOW))(*[hbm(s) for s in shards], *[hbm(a) for a in lands])
    return out[0], out[1], out[2:2 + n], out[2 + n:2 + 2 * n], out[-1]


def _gather_wait(send_sems, recv_sems, srcs, lands, after, name):
    n = len(srcs)

    def body(*refs):
        src_refs, land_refs = refs[:n], refs[n:2 * n]
        send_sems, recv_sems = refs[2 * n], refs[2 * n + 1]
        x, y, c = _coords()
        for i in range(n):
            half = src_refs[i].shape[0] // 2
            for k in (1, 2, 3):
                cp, (ox, oy) = _ici_piece(src_refs, land_refs, send_sems, recv_sems, i, k, x, y, c)
                cp.wait_send()
                piece = land_refs[i].at[2 * ox + oy, _half_rows(half, c)]
                _remote(piece, piece, send_sems.at[3 * i + k - 1], recv_sems.at[3 * i + k - 1], (ox, oy, c)).wait_recv()

    out = pl.pallas_call(
        body, name=name,
        out_shape=(*[pltpu.HBM(s.shape, s.dtype) for s in srcs], *[pltpu.HBM(a.shape, a.dtype) for a in lands]),
        in_specs=[HBM_SPEC] * (2 * n) + [SEM_SPEC, SEM_SPEC, pl.BlockSpec(memory_space=pl.ANY)],
        out_specs=tuple([HBM_SPEC] * (2 * n)), input_output_aliases={i: i for i in range(2 * n)},
        compiler_params=pltpu.CompilerParams(has_side_effects=DATAFLOW))(*srcs, *lands, send_sems, recv_sems, after)
    return out[n:]


def _gather_forward(lands, name):
    n = len(lands)

    def body(*refs):
        ins, outs = refs[:n], refs[n:2 * n]
        send_sems, recv_sems = refs[2 * n:]
        x, y, c = _coords()
        sends = []
        for i in range(n):
            half = ins[i].shape[1] // 2
            for k in (1, 2, 3):
                ox, oy = _other_chip(x, y, k)
                cp = _remote(ins[i].at[2 * ox + oy, _half_rows(half, c)], outs[i].at[2 * ox + oy, _half_rows(half, c)],
                             send_sems.at[3 * i + k - 1], recv_sems.at[3 * i + k - 1], (x, y, 1 - c))
                cp.start()
                sends.append(cp)
        for i in range(n):
            half = ins[i].shape[1] // 2
            for k in (1, 2, 3):
                ox, oy = _other_chip(x, y, k)
                piece = outs[i].at[2 * ox + oy, _half_rows(half, 1 - c)]
                _remote(piece, piece, send_sems.at[3 * i + k - 1], recv_sems.at[3 * i + k - 1], (x, y, 1 - c)).wait_recv()
        for cp in sends:
            cp.wait_send()

    return pl.pallas_call(
        body, name=name, in_specs=[HBM_SPEC] * n, out_specs=[HBM_SPEC] * n,
        out_shape=[jax.ShapeDtypeStruct(a.shape, a.dtype) for a in lands],
        input_output_aliases={i: i for i in range(n)},
        scratch_shapes=[pltpu.SemaphoreType.DMA((3 * n,))] * 2)(*lands)


def _rs_sibling_swap(g4s, name):
    n = len(g4s)

    def body(*refs):
        ins, outs = refs[:n], refs[n:2 * n]
        send_sems, recv_sems = refs[2 * n:]
        x, y, c = _coords()
        sends = []
        for i in range(n):
            half = ins[i].shape[1] // 2
            for s in range(4):
                cp = _remote(ins[i].at[s, _half_rows(half, 1 - c)], outs[i].at[s], send_sems.at[4 * i + s],
                             recv_sems.at[4 * i + s], (x, y, 1 - c))
                cp.start()
                sends.append(cp)
        for cp in sends:
            cp.wait_recv()
        for cp in sends:
            cp.wait_send()

    return pl.pallas_call(
        body, name=name, in_specs=[HBM_SPEC] * n, out_specs=[HBM_SPEC] * n,
        out_shape=[jax.ShapeDtypeStruct((4, g.shape[1] // 2, g.shape[2]), g.dtype) for g in g4s],
        scratch_shapes=[pltpu.SemaphoreType.DMA((4 * n,))] * 2)(*g4s)


def _rs_add_sibling(g4, got, c_idx, name):
    _, rows, cols = g4.shape
    half = rows // 2
    tr = min(half, 256)
    nb = half // tr

    def body(c_ref, a_ref, b_ref, o_ref, ob_ref):
        total = a_ref[...] + b_ref[...]
        o_ref[...] = total
        ob_ref[...] = total.astype(BF16)

    blk = (1, tr, cols)
    out = pl.BlockSpec(blk, lambda s, i, c_ref: (s, i, 0))
    grid_spec = pltpu.PrefetchScalarGridSpec(
        num_scalar_prefetch=1, grid=(4, nb),
        in_specs=[pl.BlockSpec(blk, lambda s, i, c_ref: (s, c_ref[0] * nb + i, 0)), out],
        out_specs=[out, out])
    return pl.pallas_call(
        body, name=name, grid_spec=grid_spec,
        out_shape=[jax.ShapeDtypeStruct(got.shape, F32), jax.ShapeDtypeStruct(got.shape, BF16)],
        compiler_params=_params("parallel", "parallel"))(c_idx, g4, got)


def _rs_add_chips(p4, got3, idx, name):
    _, half, cols = p4.shape
    tr = min(half, 256)
    nb = half // tr

    def body(idx_ref, p_ref, a_ref, b_ref, c_ref, o_ref):
        o_ref[...] = ((p_ref[0] + a_ref[0].astype(F32)) + b_ref[0].astype(F32)) + c_ref[0].astype(F32)

    blk = (1, tr, cols)
    grid_spec = pltpu.PrefetchScalarGridSpec(
        num_scalar_prefetch=1, grid=(nb,),
        in_specs=[pl.BlockSpec(blk, lambda i, idx_ref: (idx_ref[0], i, 0))]
        + [pl.BlockSpec(blk, functools.partial(lambda k, i, idx_ref: (k, i, 0), k)) for k in range(3)],
        out_specs=pl.BlockSpec((tr, cols), lambda i, idx_ref: (idx_ref[1] * nb + i, 0)))
    return pl.pallas_call(body, name=name, grid_spec=grid_spec, out_shape=jax.ShapeDtypeStruct((2 * half, cols), F32),
                          compiler_params=_params("parallel"))(idx, p4, got3, got3, got3)


def _rs_share_halves(bufs, name):
    n = len(bufs)

    def body(*refs):
        ins, outs = refs[:n], refs[n:2 * n]
        send_sems, recv_sems = refs[2 * n:]
        x, y, c = _coords()
        sends = []
        for i in range(n):
            half = ins[i].shape[0] // 2
            cp = _remote(ins[i].at[_half_rows(half, c)], outs[i].at[_half_rows(half, c)], send_sems.at[i],
                         recv_sems.at[i], (x, y, 1 - c))
            cp.start()
            sends.append(cp)
        for i in range(n):
            half = ins[i].shape[0] // 2
            _remote(ins[i].at[_half_rows(half, c)], outs[i].at[_half_rows(half, 1 - c)], send_sems.at[i],
                    recv_sems.at[i], (x, y, 1 - c)).wait_recv()
        for cp in sends:
            cp.wait_send()

    return pl.pallas_call(
        body, name=name, in_specs=[HBM_SPEC] * n, out_specs=[HBM_SPEC] * n,
        out_shape=[jax.ShapeDtypeStruct(b.shape, b.dtype) for b in bufs],
        input_output_aliases={i: i for i in range(n)},
        scratch_shapes=[pltpu.SemaphoreType.DMA((n,))] * 2)(*bufs)


def _scatter_piece(srcs, lands, send_sems, recv_sems, i, k, x, y, c):
    ox, oy = _other_chip(x, y, k)
    return _remote(srcs[i].at[2 * ox + oy], lands[i].at[k - 1], send_sems.at[3 * i + k - 1],
                   recv_sems.at[3 * i + k - 1], (ox, oy, c))


def _rs_scatter_start(p4s, name):
    n = len(p4s)

    def body(*refs):
        srcs, lands = refs[:n], refs[n:2 * n]
        send_sems, recv_sems = refs[2 * n], refs[2 * n + 1]
        token = refs[-1]
        x, y, c = _coords()
        for i in range(n):
            for k in (1, 2, 3):
                _scatter_piece(srcs, lands, send_sems, recv_sems, i, k, x, y, c).start()
        token[...] = jnp.zeros_like(token)

    hbm = lambda a: pltpu.with_memory_space_constraint(a, pltpu.HBM)
    lands = [lax.empty((3,) + p.shape[1:], p.dtype) for p in p4s]
    out = pl.pallas_call(
        body, name=name,
        out_shape=(pltpu.SemaphoreType.DMA((3 * n,)), pltpu.SemaphoreType.DMA((3 * n,)),
                   *[pltpu.HBM(p.shape, p.dtype) for p in p4s], *[pltpu.HBM(a.shape, a.dtype) for a in lands],
                   jax.ShapeDtypeStruct((SUBLANES, LANES), F32)),
        in_specs=[HBM_SPEC] * (2 * n), out_specs=(SEM_SPEC, SEM_SPEC, *[HBM_SPEC] * (2 * n), VMEM_SPEC),
        input_output_aliases={i: 2 + i for i in range(2 * n)},
        compiler_params=pltpu.CompilerParams(has_side_effects=DATAFLOW))(*[hbm(p) for p in p4s], *[hbm(a) for a in lands])
    return out[0], out[1], out[2:2 + n], out[2 + n:2 + 2 * n], out[-1]


def _rs_scatter_wait(send_sems, recv_sems, srcs, lands, after, name):
    n = len(srcs)

    def body(*refs):
        src_refs, land_refs = refs[:n], refs[n:2 * n]
        send_sems, recv_sems = refs[2 * n], refs[2 * n + 1]
        x, y, c = _coords()
        for i in range(n):
            for k in (1, 2, 3):
                cp = _scatter_piece(src_refs, land_refs, send_sems, recv_sems, i, k, x, y, c)
                cp.wait_send()
                cp.wait_recv()

    out = pl.pallas_call(
        body, name=name,
        out_shape=(*[pltpu.HBM(s.shape, s.dtype) for s in srcs], *[pltpu.HBM(a.shape, a.dtype) for a in lands]),
        in_specs=[HBM_SPEC] * (2 * n) + [SEM_SPEC, SEM_SPEC, pl.BlockSpec(memory_space=pl.ANY)],
        out_specs=tuple([HBM_SPEC] * (2 * n)), input_output_aliases={i: i for i in range(2 * n)},
        compiler_params=pltpu.CompilerParams(has_side_effects=DATAFLOW))(*srcs, *lands, send_sems, recv_sems, after)
    return out[n:]


def _swap_piece(srcs, lands, send_sems, recv_sems, i, s, x, y, c):
    half = srcs[i].shape[1] // 2
    return _remote(srcs[i].at[s, _half_rows(half, 1 - c)], lands[i].at[s], send_sems.at[4 * i + s],
                   recv_sems.at[4 * i + s], (x, y, 1 - c))


def _rs_swap_start(g4s, name):
    n = len(g4s)

    def body(*refs):
        srcs, lands = refs[:n], refs[n:2 * n]
        send_sems, recv_sems = refs[2 * n], refs[2 * n + 1]
        token = refs[-1]
        x, y, c = _coords()
        for i in range(n):
            for s in range(4):
                _swap_piece(srcs, lands, send_sems, recv_sems, i, s, x, y, c).start()
        token[...] = jnp.zeros_like(token)

    hbm = lambda a: pltpu.with_memory_space_constraint(a, pltpu.HBM)
    lands = [lax.empty((4, g.shape[1] // 2, g.shape[2]), g.dtype) for g in g4s]
    out = pl.pallas_call(
        body, name=name,
        out_shape=(pltpu.SemaphoreType.DMA((4 * n,)), pltpu.SemaphoreType.DMA((4 * n,)),
                   *[pltpu.HBM(g.shape, g.dtype) for g in g4s], *[pltpu.HBM(a.shape, a.dtype) for a in lands],
                   jax.ShapeDtypeStruct((SUBLANES, LANES), F32)),
        in_specs=[HBM_SPEC] * (2 * n), out_specs=(SEM_SPEC, SEM_SPEC, *[HBM_SPEC] * (2 * n), VMEM_SPEC),
        input_output_aliases={i: 2 + i for i in range(2 * n)},
        compiler_params=pltpu.CompilerParams(has_side_effects=DATAFLOW))(*[hbm(g) for g in g4s], *[hbm(a) for a in lands])
    return out[0], out[1], out[2:2 + n], out[2 + n:2 + 2 * n], out[-1]


def _rs_swap_wait(send_sems, recv_sems, srcs, lands, after, name):
    n = len(srcs)

    def body(*refs):
        src_refs, land_refs = refs[:n], refs[n:2 * n]
        send_sems, recv_sems = refs[2 * n], refs[2 * n + 1]
        x, y, c = _coords()
        for i in range(n):
            for s in range(4):
                cp = _swap_piece(src_refs, land_refs, send_sems, recv_sems, i, s, x, y, c)
                cp.wait_send()
                cp.wait_recv()

    out = pl.pallas_call(
        body, name=name,
        out_shape=(*[pltpu.HBM(s.shape, s.dtype) for s in srcs], *[pltpu.HBM(a.shape, a.dtype) for a in lands]),
        in_specs=[HBM_SPEC] * (2 * n) + [SEM_SPEC, SEM_SPEC, pl.BlockSpec(memory_space=pl.ANY)],
        out_specs=tuple([HBM_SPEC] * (2 * n)), input_output_aliases={i: i for i in range(2 * n)},
        compiler_params=pltpu.CompilerParams(has_side_effects=DATAFLOW))(*srcs, *lands, send_sems, recv_sems, after)
    return out[:n], out[n:]


def _rs_front(g4s, names, tag):
    c_idx = jnp.stack([lax.axis_index("c")]).astype(jnp.int32)
    got = _rs_sibling_swap(g4s, f"rs_sibling_swap_{tag}")
    return [_rs_add_sibling(g, s, c_idx, f"rs_add_sibling_{nm}") for g, s, nm in zip(g4s, got, names)]


def _rs_back(p4s, got3, names):
    x, y, c = _coords()
    idx = jnp.stack([2 * x + y, c]).astype(jnp.int32)
    return [_rs_add_chips(p, t, idx, f"rs_add_chips_{nm}") for (p, _), t, nm in zip(p4s, got3, names)]


def _cols_split(full):
    r, c4 = full.shape
    return full.reshape(r, 4, c4 // 4).transpose(1, 0, 2)


_BIG = {
    "w_in_even": ((1024, 2052), lambda s: s),
    "w_out_even": ((512, 1024), lambda s: s.reshape(2048, 1024)),
    "w_in_odd": ((1024, 2048), lambda s: s),
    "w_out_odd": ((512, 1024), lambda s: s.reshape(2048, 1024)),
    "w_pl": ((512, 256), lambda s: s.reshape(4, 2, 256, 256).transpose(1, 2, 0, 3).reshape(2, 256, 1024)),
    "w_pl_gate": ((512, 1024), lambda s: s.reshape(4, 2, 256, 1024).transpose(1, 0, 2, 3).reshape(2, 1024, 1024)),
}


_RS_EARLY = {
    "w_in_odd": lambda f: f,
    "w_out_odd": lambda f: f.reshape(4, 512, 1024),
    "w_pl_gate1": lambda f: f.reshape(4, 256, 1024),
    "w_pl1": _cols_split,
    "w_out_even": lambda f: f.reshape(4, 512, 1024),
    "w_pl_gate0": lambda f: f.reshape(4, 256, 1024),
    "w_pl0": _cols_split,
}
_RS_LATE = {"w_in_even": lambda f: f}


def _size(shape):
    n = 1
    for d in shape:
        n *= d
    return n


_SMALL = {"a_log": (1, 8), "dt_bias": (1, 8), "gdn_norm_g": (1, 128), "hgrn_norm_g": (1, 128),
          "lower_bounds": (2, 2048), "ln_g": (2, 1024), "ln_b": (2, 1024), "conv_a_w": (3, 1024), "conv_b_w": (4, 3072)}
_CONV_SHARD = {"conv_a_w": (3, 256), "conv_b_w": (4, 768)}


def _pack_small(parts, shapes, head_rows=0):
    rows = []
    for n, shape in shapes.items():
        v = parts[n].reshape(-1)
        rows.append(jnp.pad(v, (0, -v.shape[0] % LANES)).reshape(-1, LANES))
    buf = jnp.concatenate(rows, axis=0)
    return jnp.pad(buf, ((head_rows, -(buf.shape[0] + head_rows) % SUBLANES), (0, 0)))


def _unpack_small(buf, shapes, head_rows=0):
    out, off = {}, head_rows
    for n, shape in shapes.items():
        nrow = -(-_size(shape) // LANES)
        out[n] = buf[off:off + nrow].reshape(-1)[:_size(shape)].reshape(shape)
        off += nrow
    return out


_WEIGHTS = ["w_in_even", "conv_a_w", "conv_b_w", "a_log", "dt_bias", "gdn_norm_g", "w_out_even", "w_in_odd",
            "lower_bounds", "hgrn_norm_g", "w_out_odd", "ln_g", "ln_b", "w_pl", "w_pl_gate"]


def kernel(x, p, w_in_even, conv_a_w, conv_b_w, a_log, dt_bias, gdn_norm_g, w_out_even, w_in_odd, lower_bounds, hgrn_norm_g, w_out_odd, ln_g, ln_b, w_pl, w_pl_gate, loss_target, m_w_in_even, m_conv_a_w, m_conv_b_w, m_a_log, m_dt_bias, m_gdn_norm_g, m_w_out_even, m_w_in_odd, m_lower_bounds, m_hgrn_norm_g, m_w_out_odd, m_ln_g, m_ln_b, m_w_pl, m_w_pl_gate, v_w_in_even, v_conv_a_w, v_conv_b_w, v_a_log, v_dt_bias, v_gdn_norm_g, v_w_out_even, v_w_in_odd, v_lower_bounds, v_hgrn_norm_g, v_w_out_odd, v_ln_g, v_ln_b, v_w_pl, v_w_pl_gate):
    w = dict(zip(_WEIGHTS, (w_in_even, conv_a_w, conv_b_w, a_log, dt_bias, gdn_norm_g, w_out_even, w_in_odd,
                            lower_bounds, hgrn_norm_g, w_out_odd, ln_g, ln_b, w_pl, w_pl_gate)))
    m = dict(zip(_WEIGHTS, (m_w_in_even, m_conv_a_w, m_conv_b_w, m_a_log, m_dt_bias, m_gdn_norm_g, m_w_out_even,
                            m_w_in_odd, m_lower_bounds, m_hgrn_norm_g, m_w_out_odd, m_ln_g, m_ln_b, m_w_pl, m_w_pl_gate)))
    v = dict(zip(_WEIGHTS, (v_w_in_even, v_conv_a_w, v_conv_b_w, v_a_log, v_dt_bias, v_gdn_norm_g, v_w_out_even,
                            v_w_in_odd, v_lower_bounds, v_hgrn_norm_g, v_w_out_odd, v_ln_g, v_ln_b, v_w_pl, v_w_pl_gate)))
    chip = 2 * lax.axis_index("x") + lax.axis_index("y")

    names = list(_BIG)
    shard_shapes = {n: _BIG[n][0] for n in names}
    early, late = names[:1], names[1:]
    shards = {n: w[n].reshape(shard_shapes[n]).astype(BF16) for n in early}
    whole = lambda n, stacked: _BIG[n][1](lax.dynamic_update_slice(stacked, shards[n][None], (chip, 0, 0)))
    conv_mine = _pack_small({n: w[n] for n in _CONV_SHARD}, _CONV_SHARD)
    conv_all = _exchange_small(conv_mine, False, "gather_conv")
    shards, conv_all = lax.optimization_barrier((shards, conv_all))
    first = _gather_start([shards[n] for n in early], "gather_first_start")
    shards.update({n: (w[n].reshape(shard_shapes[n]) + first[4][0, 0]).astype(BF16) for n in late})
    send_sems, recv_sems, srcs, lands, token = _gather_start([shards[n] for n in late], "gather_rest_start")

    def late_weights(after):
        landed = _gather_forward(_gather_wait(send_sems, recv_sems, srcs, lands, after, "gather_rest_wait"),
                                 "gather_rest_forward")
        return {n: whole(n, ga) for n, ga in zip(late, landed)}

    landed = _gather_forward(_gather_wait(*first[:4], token, "gather_first_wait"), "gather_first_forward")
    full = {n: whole(n, ga) for n, ga in zip(early, landed)}
    conv_by_chip = [_unpack_small(conv_all[2 * s], _CONV_SHARD) for s in range(4)]
    for n in _CONV_SHARD:
        full[n] = jnp.concatenate([conv_by_chip[s][n] for s in range(4)], axis=1)
    for n in _SMALL:
        if n not in _CONV_SHARD:
            full[n] = w[n]

    early_rs = {}

    def early_grads_ready(grads):
        early_rs["swap"] = _rs_swap_start([_RS_EARLY[n](grads[n]) for n in _RS_EARLY], "rs_swap_early_start")
        return early_rs["swap"][4][0, 0]

    def early_grads_swapped(after):
        g4s, got = _rs_swap_wait(*early_rs["swap"][:4], after, "rs_swap_early_wait")
        c_idx = jnp.stack([lax.axis_index("c")]).astype(jnp.int32)
        early_rs["p4s"] = [_rs_add_sibling(g_, s_, c_idx, f"rs_add_sibling_{nm}") for g_, s_, nm in zip(g4s, got, _RS_EARLY)]
        early_rs["sems"] = _rs_scatter_start([pb for _, pb in early_rs["p4s"]], "rs_scatter_early_start")
        return early_rs["sems"][4][0, 0]

    late_rs = {}

    def last_grad_ready(grad):
        late_rs["p4s"] = _rs_front([_RS_LATE[n](grad) for n in _RS_LATE], list(_RS_LATE), "late")
        late_rs["sems"] = _rs_scatter_start([pb for _, pb in late_rs["p4s"]], "rs_scatter_late_start")
        return late_rs["sems"][4][0, 0]

    loss_part, dx, g = _local_step(x[0], p[:, 0], loss_target[0], full, late_weights, early_grads_ready,
                                   early_grads_swapped, last_grad_ready, token[0, 0])

    late_p4s, late_sems = late_rs["p4s"], late_rs["sems"]
    got3 = _rs_scatter_wait(*early_rs["sems"][:4], dx, "rs_scatter_early_wait")
    summed = dict(zip(_RS_EARLY, _rs_share_halves(_rs_back(early_rs["p4s"], got3, list(_RS_EARLY)), "rs_share_early")))
    g_big = {n: summed[n] for n in names if n in summed}
    g_big["w_pl"] = jnp.stack([summed["w_pl0"], summed["w_pl1"]])
    g_big["w_pl_gate"] = jnp.stack([summed["w_pl_gate0"], summed["w_pl_gate1"]])
    small_sum = _exchange_small(jnp.concatenate([loss_part, _pack_small(g, _SMALL)], axis=0), True, "reduce_small")
    loss = small_sum[0, 0]
    g_small = _unpack_small(small_sum, _SMALL, head_rows=SUBLANES)
    for n, (rows, cols) in _CONV_SHARD.items():
        g_small[n] = lax.dynamic_slice_in_dim(g_small[n], chip * cols, cols, axis=1)

    grads, delta, new_m, new_v = {}, {}, {}, {}
    for n in late:
        grads[n] = g_big[n].reshape(w[n].shape)
        delta[n], new_m[n], new_v[n] = _adamw(w[n], grads[n], m[n], v[n], f"adamw_{n}")
    own = {n: (_CONV_SHARD[n] if n in _CONV_SHARD else _SMALL[n]) for n in _SMALL}
    packs = [_pack_small({n: src[n] for n in _SMALL}, own)[None] for src in (w, g_small, m, v)]
    outs = [_unpack_small(t[0], own) for t in _adamw(*packs, "adamw_small")]
    for n in _SMALL:
        grads[n] = g_small[n].reshape(w[n].shape)
        delta[n], new_m[n], new_v[n] = (t[n].reshape(w[n].shape) for t in outs)
    got3 = _rs_scatter_wait(*late_sems[:4], new_v["w_in_odd"], "rs_scatter_late_wait")
    (g_in_even,) = _rs_share_halves(_rs_back(late_p4s, got3, list(_RS_LATE)), "rs_share_late")
    for n in early:
        t_ = lambda a: jnp.swapaxes(a, 1, 2)
        g_t = t_(g_in_even.reshape(w[n].shape))
        grads[n] = t_(g_t)
        delta[n], new_m[n], new_v[n] = (t_(o) for o in _adamw(t_(w[n]), g_t, t_(m[n]), t_(v[n]), f"adamw_{n}"))
    return (loss, dx[None], *[grads[n] for n in _WEIGHTS], *[delta[n] for n in _WEIGHTS],
            *[new_m[n] for n in _WEIGHTS], *[new_v[n] for n in _WEIGHTS])
```

```python
import functools

import jax
import jax.numpy as jnp
from jax import lax
from jax.experimental import pallas as pl
from jax.experimental.pallas import tpu as pltpu

F32 = jnp.float32
BF16 = jnp.bfloat16
HI = lax.Precision.HIGHEST

D_MODEL = 1024
PL_DIM = 256
GDN_HEADS = 8
HEAD_DIM = 128
GDN_CHUNK = 64
HGRN_HEADS = 16
HGRN_CHUNK = 32
HGRN_WIDTH = 2048
DEEPNORM_ALPHA = 4.0 ** 0.25
NORM_EPS = 1e-5
ADAM_LR, ADAM_B1, ADAM_B2, ADAM_EPS, ADAM_WD, ADAM_STEP = 0.001, 0.9, 0.999, 1e-08, 0.01, 10

VMEM_LIMIT = 56 * 1024 * 1024
SUBLANES = 8
LANES = 128


def _params(*sem):
    return pltpu.CompilerParams(dimension_semantics=sem, vmem_limit_bytes=VMEM_LIMIT)


ONE_PASS, THREE_PASS, FULL_F32, EXACT_LHS, EXACT_RHS = 0, 1, 2, 3, 4


def _split3(v):
    hi = v.astype(BF16)
    r1 = v - hi.astype(F32)
    mid = r1.astype(BF16)
    return hi, mid, (r1 - mid.astype(F32)).astype(BF16)


def _mm_raw(a, b, kind, prec):
    nb = a.ndim - 2
    ca = a.ndim - 1 if kind[0] == "n" else a.ndim - 2
    cb = b.ndim - 2 if kind[1] == "n" else b.ndim - 1
    dims = (((ca,), (cb,)), (tuple(range(nb)),) * 2)
    if prec == FULL_F32:
        return lax.dot_general(a, b, dims, precision=HI, preferred_element_type=F32)
    dot = lambda p, q: lax.dot_general(p, q, dims, preferred_element_type=F32)
    ah, bh = a.astype(BF16), b.astype(BF16)
    if prec == ONE_PASS:
        return dot(ah, bh)
    if prec == EXACT_LHS:
        b1, b2, b3 = _split3(b)
        return dot(ah, b1) + (dot(ah, b2) + dot(ah, b3))
    if prec == EXACT_RHS:
        a1, a2, a3 = _split3(a)
        return dot(a1, bh) + (dot(a2, bh) + dot(a3, bh))
    al = (a - ah.astype(F32)).astype(BF16)
    bl = (b - bh.astype(F32)).astype(BF16)
    return dot(ah, bh) + (dot(ah, bl) + dot(al, bh))


@functools.partial(jax.custom_vjp, nondiff_argnums=(2, 3))
def _mm_vjp(a, b, kind, hi):
    return _mm_raw(a, b, kind, hi)


def _mm_vjp_fwd(a, b, kind, hi):
    return _mm_raw(a, b, kind, hi), (a, b)


def _mm_vjp_bwd(kind, hi, res, dc):
    a, b = res
    if hi in (EXACT_LHS, EXACT_RHS):
        assert kind == "nn"
        if hi == EXACT_LHS:
            return jnp.zeros_like(a), _mm_raw(a, dc, "tn", EXACT_LHS)
        return _mm_raw(dc, b, "nt", EXACT_RHS), jnp.zeros_like(b)
    if kind == "nn":
        return _mm_raw(dc, b, "nt", hi), _mm_raw(a, dc, "tn", hi)
    if kind == "nt":
        return _mm_raw(dc, b, "nn", hi), _mm_raw(dc, a, "tn", hi)
    return _mm_raw(b, dc, "nt", hi), _mm_raw(a, dc, "nn", hi)


_mm_vjp.defvjp(_mm_vjp_fwd, _mm_vjp_bwd)


def _lane_total(v):
    return jnp.broadcast_to(jnp.sum(v, axis=-1, keepdims=True), v.shape)


def _matmul(a, b, *, name, ta=False, tb=False, add=None, add_scale=1.0, tm=1024, tn=2048, tk=1024):
    m, k = (a.shape[1], a.shape[0]) if ta else a.shape
    n = b.shape[0] if tb else b.shape[1]
    tm, tn, tk = min(tm, m), min(tn, n), min(tk, k)
    tn = tn if n % tn == 0 else tn // 2
    assert m % tm == 0 and n % tn == 0 and k % tk == 0, (name, m, n, k)
    nk = k // tk
    dims = (((0 if ta else 1,), (1 if tb else 0,)), ((), ()))

    def body(*refs):
        a_ref, b_ref = refs[:2]
        o_ref = refs[-1]
        part = lax.dot_general(a_ref[...].astype(BF16), b_ref[...].astype(BF16), dims, preferred_element_type=F32)
        first = (lambda: part) if add is None else (lambda: part + add_scale * refs[2][...])
        if nk == 1:
            o_ref[...] = first()
        else:
            kk = pl.program_id(2)

            @pl.when(kk == 0)
            def _():
                o_ref[...] = first()

            @pl.when(kk > 0)
            def _():
                o_ref[...] += part

    a_spec = pl.BlockSpec((tk, tm), lambda i, j, kk: (kk, i)) if ta else pl.BlockSpec((tm, tk), lambda i, j, kk: (i, kk))
    b_spec = pl.BlockSpec((tn, tk), lambda i, j, kk: (j, kk)) if tb else pl.BlockSpec((tk, tn), lambda i, j, kk: (kk, j))
    o_spec = pl.BlockSpec((tm, tn), lambda i, j, kk: (i, j))
    in_specs = [a_spec, b_spec] + ([o_spec] if add is not None else [])
    args = (a, b) + ((add,) if add is not None else ())
    return pl.pallas_call(
        body, name=name, grid=(m // tm, n // tn, nk), in_specs=in_specs, out_specs=o_spec,
        out_shape=jax.ShapeDtypeStruct((m, n), F32),
        compiler_params=_params("parallel", "parallel", "arbitrary"))(*args)


HALO = SUBLANES


def _halo_specs(tt, width, col, nt):
    r = tt // HALO
    prev = pl.BlockSpec((HALO, width), lambda i: (jnp.maximum(i * r - 1, 0), col))
    nxt = pl.BlockSpec((HALO, width), lambda i: (jnp.minimum((i + 1) * r, nt * r - 1), col))
    return prev, nxt


def _shift_down(ext, k):
    return ext if k == 0 else pltpu.roll(ext, k, 0)


def _shift_up(ext, k):
    return ext if k == 0 else pltpu.roll(ext, ext.shape[0] - k, 0)


def _causal_conv(ext, w, taps):
    acc = None
    for j in range(taps):
        term = w[j:j + 1, :] * _shift_down(ext, taps - 1 - j)
        acc = term if acc is None else acc + term
    return acc[HALO:, :]


def _conv_a_fwd(proj_a, conv_w):
    t = proj_a.shape[0]
    tt = min(t, 256)
    nt = t // tt
    wdt = 1024

    def body(cur_ref, prev_ref, w_ref, y_ref):
        i = pl.program_id(0)
        cur = cur_ref[...]
        h, c, b, z = (cur[:, k * wdt:(k + 1) * wdt] for k in range(4))
        prev = prev_ref[...]
        u_prev = jnp.where(i > 0, prev[:, wdt:2 * wdt] * prev[:, 0:wdt], 0.0)
        ext = jnp.concatenate([u_prev, c * h], axis=0)
        conv = _causal_conv(ext, w_ref[...], 3)
        y_ref[...] = (b * conv * jax.nn.silu(z)).astype(BF16)

    prev_spec, _ = _halo_specs(tt, 4 * wdt, 0, nt)
    return pl.pallas_call(
        body, name="conv_a_fwd", grid=(nt,),
        in_specs=[pl.BlockSpec((tt, 4 * wdt), lambda i: (i, 0)), prev_spec, pl.BlockSpec((3, wdt), lambda i: (0, 0))],
        out_specs=pl.BlockSpec((tt, wdt), lambda i: (i, 0)),
        out_shape=jax.ShapeDtypeStruct((t, wdt), BF16), compiler_params=_params("parallel"))(proj_a, proj_a, conv_w)


def _conv_a_bwd(proj_a, conv_w, dy):
    t = proj_a.shape[0]
    tt = min(t, 256)
    nt = t // tt
    wdt = 1024

    def body(cur_ref, prev_ref, nxt_ref, w_ref, dy_ref, dyn_ref, d_ref, dw_ref):
        i = pl.program_id(0)
        w = w_ref[...]
        cur, prev, nxt = cur_ref[...], prev_ref[...], nxt_ref[...]
        split = lambda a: tuple(a[:, k * wdt:(k + 1) * wdt] for k in range(4))
        h, c, b, z = split(cur)
        hp, cp, _, _ = split(prev)
        hn, cn, bn, zn = split(nxt)
        u_prev = jnp.where(i > 0, cp * hp, 0.0)
        u_ext = jnp.concatenate([u_prev, c * h, cn * hn], axis=0)
        taps = [_shift_down(u_ext, 2 - j)[HALO:, :] for j in range(3)]
        conv = w[0:1, :] * taps[0] + w[1:2, :] * taps[1] + w[2:3, :] * taps[2]
        b_cn = jnp.concatenate([b, bn], axis=0)
        z_cn = jnp.concatenate([z, zn], axis=0)
        dy_cn = jnp.concatenate([dy_ref[...], jnp.where(i < nt - 1, dyn_ref[...], 0.0)], axis=0)
        sg = jax.nn.sigmoid(z_cn)
        silu = z_cn * sg
        d_conv = dy_cn * b_cn * silu
        db = (dy_cn * conv * silu)[:tt, :]
        dz = (dy_cn * b_cn * conv * (sg * (1.0 + z_cn * (1.0 - sg))))[:tt, :]
        du = None
        for j in range(3):
            term = w[j:j + 1, :] * _shift_up(d_conv, 2 - j)
            du = term if du is None else du + term
        du = du[:tt, :]
        d_ref[...] = jnp.concatenate([du * c, du * h, db, dz], axis=1).astype(BF16)

        @pl.when(i == 0)
        def _():
            dw_ref[...] = jnp.zeros_like(dw_ref)

        d_cur = d_conv[:tt, :]
        rows = [jnp.sum(d_cur * taps[j][:tt, :], axis=0, keepdims=True) for j in range(3)]
        dw_ref[0:3, :] += jnp.concatenate(rows, axis=0)

    prev_spec, nxt_spec = _halo_specs(tt, 4 * wdt, 0, nt)
    _, dyn_spec = _halo_specs(tt, wdt, 0, nt)
    return pl.pallas_call(
        body, name="conv_a_bwd", grid=(nt,),
        in_specs=[pl.BlockSpec((tt, 4 * wdt), lambda i: (i, 0)), prev_spec, nxt_spec,
                  pl.BlockSpec((3, wdt), lambda i: (0, 0)), pl.BlockSpec((tt, wdt), lambda i: (i, 0)), dyn_spec],
        out_specs=[pl.BlockSpec((tt, 4 * wdt), lambda i: (i, 0)), pl.BlockSpec((SUBLANES, wdt), lambda i: (0, 0))],
        out_shape=[jax.ShapeDtypeStruct((t, 4 * wdt), BF16), jax.ShapeDtypeStruct((SUBLANES, wdt), F32)],
        compiler_params=_params("arbitrary"))(proj_a, proj_a, proj_a, conv_w, dy, dy)


def _conv_b_fwd(proj_qkv, conv_w):
    t, width = proj_qkv.shape
    tt = min(t, 256)
    nt = t // tt
    wdt = 1024

    def body(cur_ref, prev_ref, w_ref, y_ref):
        i = pl.program_id(1)
        ext = jnp.concatenate([jnp.where(i > 0, prev_ref[...], 0.0), cur_ref[...]], axis=0)
        y_ref[...] = jax.nn.silu(_causal_conv(ext, w_ref[...], 4))

    r = tt // HALO
    return pl.pallas_call(
        body, name="conv_b_fwd", grid=(width // wdt, nt),
        in_specs=[pl.BlockSpec((tt, wdt), lambda j, i: (i, j)),
                  pl.BlockSpec((HALO, wdt), lambda j, i: (jnp.maximum(i * r - 1, 0), j)),
                  pl.BlockSpec((4, wdt), lambda j, i: (0, j))],
        out_specs=pl.BlockSpec((tt, wdt), lambda j, i: (i, j)),
        out_shape=jax.ShapeDtypeStruct((t, width), F32), compiler_params=_params("parallel", "parallel"))(
            proj_qkv, proj_qkv, conv_w)


def _conv_b_bwd(proj_qkv, conv_w, d_act, col, name):
    t = proj_qkv.shape[0]
    tt = min(t, 256)
    nt = t // tt
    wdt = 1024

    def body(cur_ref, prev_ref, nxt_ref, w_ref, da_ref, dan_ref, d_ref, dw_ref):
        i = pl.program_id(0)
        w = w_ref[...]
        u_ext = jnp.concatenate([jnp.where(i > 0, prev_ref[...], 0.0), cur_ref[...], nxt_ref[...]], axis=0)
        taps = [_shift_down(u_ext, 3 - j)[HALO:, :] for j in range(4)]
        conv = w[0:1, :] * taps[0] + w[1:2, :] * taps[1] + w[2:3, :] * taps[2] + w[3:4, :] * taps[3]
        da_cn = jnp.concatenate([da_ref[...], jnp.where(i < nt - 1, dan_ref[...], 0.0)], axis=0)
        sg = jax.nn.sigmoid(conv)
        d_conv = da_cn * (sg * (1.0 + conv * (1.0 - sg)))
        du = None
        for j in range(4):
            term = w[j:j + 1, :] * _shift_up(d_conv, 3 - j)
            du = term if du is None else du + term
        d_ref[...] = du[:tt, :].astype(BF16)

        @pl.when(i == 0)
        def _():
            dw_ref[...] = jnp.zeros_like(dw_ref)

        d_cur = d_conv[:tt, :]
        rows = [jnp.sum(d_cur * taps[j][:tt, :], axis=0, keepdims=True) for j in range(4)]
        dw_ref[0:4, :] += jnp.concatenate(rows, axis=0)

    prev_spec, nxt_spec = _halo_specs(tt, wdt, col, nt)
    _, dan_spec = _halo_specs(tt, wdt, 0, nt)
    return pl.pallas_call(
        body, name=name, grid=(nt,),
        in_specs=[pl.BlockSpec((tt, wdt), lambda i: (i, col)), prev_spec, nxt_spec,
                  pl.BlockSpec((4, wdt), lambda i: (0, col)), pl.BlockSpec((tt, wdt), lambda i: (i, 0)), dan_spec],
        out_specs=[pl.BlockSpec((tt, wdt), lambda i: (i, 0)), pl.BlockSpec((SUBLANES, wdt), lambda i: (0, 0))],
        out_shape=[jax.ShapeDtypeStruct((t, wdt), BF16), jax.ShapeDtypeStruct((SUBLANES, wdt), F32)],
        compiler_params=_params("arbitrary"))(proj_qkv, proj_qkv, proj_qkv, conv_w, d_act, d_act)


def _rms_gate(o, gn, z):
    on = o * lax.rsqrt(jnp.mean(o * o, axis=-1, keepdims=True) + NORM_EPS) * gn
    return on * jax.nn.silu(z)


GDN_PREP_ROWS = 1024


def _unit_lower_inverse(low):
    c = low.shape[-1]
    eye = lax.broadcasted_iota(jnp.int32, low.shape, low.ndim - 2) == lax.broadcasted_iota(jnp.int32, low.shape, low.ndim - 1)
    x = -low
    inv = eye.astype(F32) + x
    for _ in range(c.bit_length() - 2):
        x = _mm_raw(x, x, "nn", THREE_PASS)
        inv = inv + _mm_raw(inv, x, "nn", THREE_PASS)
    return inv


@jax.custom_vjp
def _known_inverse(low, inv):
    return inv


def _known_inverse_fwd(low, inv):
    return inv, inv


def _known_inverse_bwd(inv, d_inv):
    return -_mm_raw(_mm_raw(inv, d_inv, "tn", THREE_PASS), inv, "nt", THREE_PASS), jnp.zeros_like(inv)


_known_inverse.defvjp(_known_inverse_fwd, _known_inverse_bwd)


def _gdn_prep(mm, qa, ka, va, braw, araw, alog, dtb, inv_kept=None):
    n, c, _ = qa.shape
    q = qa * lax.rsqrt(jnp.sum(qa * qa, axis=-1, keepdims=True) + 1e-6) * (HEAD_DIM ** -0.5)
    k = ka * lax.rsqrt(jnp.sum(ka * ka, axis=-1, keepdims=True) + 1e-6)
    beta = jax.nn.sigmoid(braw)
    g = -jnp.exp(alog) * jax.nn.softplus(araw + dtb)
    ri = lax.broadcasted_iota(jnp.int32, (n, c, c), 1)
    ci = lax.broadcasted_iota(jnp.int32, (n, c, c), 2)
    incl, strict, eye = ri >= ci, ri > ci, ri == ci
    gc = mm(incl.astype(F32), g, "nn", EXACT_LHS)
    gc_i = gc[:, :, :c]
    gc_j = mm(jnp.ones((n, c, c), F32), jnp.where(eye, gc_i, 0.0), "nn", EXACT_LHS)
    decay = jnp.where(incl, jnp.exp(jnp.where(incl, gc_i - gc_j, 0.0)), 0.0)
    kb = k * beta
    low = jnp.where(strict, mm(kb, k, "nt", ONE_PASS) * decay, 0.0)
    inv = _unit_lower_inverse(low) if inv_kept is None else _known_inverse(low, inv_kept)
    egc = jnp.exp(gc)
    u = mm(inv, va * beta, "nn", THREE_PASS)
    w = mm(inv, kb * egc, "nn", THREE_PASS)
    attn = jnp.where(incl, mm(q, k, "nt", ONE_PASS) * decay, 0.0)
    g_last = jnp.sum(g, axis=1, keepdims=True)
    outs = (u, w, q * egc, k * jnp.exp(g_last - gc), attn, jnp.exp(g_last))
    return outs + (inv,) if inv_kept is None else outs


def _gdn_scan(mm, u, w, qd, kd, attn, egl, z, gn, state):
    v_new = u - mm(w, state, "nn", ONE_PASS)
    o = mm(qd, state, "nn", ONE_PASS) + mm(attn, v_new, "nn", ONE_PASS)
    new_state = state * egl + mm(kd, v_new, "tn", ONE_PASS)
    return _rms_gate(o, gn, z), new_state


def _chunks(ref_value, n, c):
    return ref_value.reshape(n, c, ref_value.shape[-1])


def _by_head(ref, rows, heads):
    return jnp.stack([ref[rows, pl.ds(h * HEAD_DIM, HEAD_DIM)] for h in range(heads)])


def _store_heads(ref, rows, value):
    for h in range(value.shape[0]):
        ref[rows, pl.ds(h * HEAD_DIM, HEAD_DIM)] = value[h]


def _gdn_prep_specs(tb, nt_unused=None):
    col = lambda off: pl.BlockSpec((tb, HEAD_DIM), lambda h, i: (i, off + h))
    rep = pl.BlockSpec((1, tb, LANES), lambda h, i: (h, i, 0))
    par = pl.BlockSpec((1, SUBLANES, LANES), lambda h, i: (h, 0, 0))
    att = pl.BlockSpec((1, tb, GDN_CHUNK), lambda h, i: (h, i, 0))
    egl = pl.BlockSpec((1, tb // GDN_CHUNK, SUBLANES, LANES), lambda h, i: (h, i, 0, 0))
    return col, rep, par, att, egl


def _gdn_prep_fwd(qkv_act, braw, araw, alog, dtb):
    t = qkv_act.shape[0]
    tb = min(t, 2 * GDN_PREP_ROWS)
    nt, nc = t // tb, tb // GDN_CHUNK
    width = GDN_HEADS * HEAD_DIM

    def body(q_ref, k_ref, v_ref, br_ref, ar_ref, al_ref, dt_ref, u_ref, w_ref, qd_ref, kd_ref, at_ref, eg_ref, inv_ref):
        ch = lambda r: _chunks(r, nc, GDN_CHUNK)
        u, w, qd, kd, attn, egl, inv = _gdn_prep(_mm_raw, ch(q_ref[...]), ch(k_ref[...]), ch(v_ref[...]), ch(br_ref[0]),
                                                 ch(ar_ref[0]), al_ref[0, 0:1, :], dt_ref[0, 0:1, :])
        u_ref[...] = u.reshape(tb, HEAD_DIM)
        w_ref[...] = w.reshape(tb, HEAD_DIM).astype(BF16)
        qd_ref[...] = qd.reshape(tb, HEAD_DIM).astype(BF16)
        kd_ref[...] = kd.reshape(tb, HEAD_DIM).astype(BF16)
        at_ref[0] = attn.reshape(tb, GDN_CHUNK).astype(BF16)
        eg_ref[0] = jnp.broadcast_to(egl, (nc, SUBLANES, LANES))
        inv_ref[0] = inv.reshape(tb, GDN_CHUNK)

    col, rep, par, att, egl = _gdn_prep_specs(tb)
    h = GDN_HEADS
    return pl.pallas_call(
        body, name="gdn_prep_fwd", grid=(h, nt),
        in_specs=[col(0), col(h), col(2 * h), rep, rep, par, par],
        out_specs=[col(0), col(0), col(0), col(0), att, egl, att],
        out_shape=[jax.ShapeDtypeStruct((t, width), F32)] + [jax.ShapeDtypeStruct((t, width), BF16)] * 3
        + [jax.ShapeDtypeStruct((h, t, GDN_CHUNK), BF16), jax.ShapeDtypeStruct((h, t // GDN_CHUNK, SUBLANES, LANES), F32),
           jax.ShapeDtypeStruct((h, t, GDN_CHUNK), F32)],
        compiler_params=_params("parallel", "parallel"))(qkv_act, qkv_act, qkv_act, braw, araw, alog, dtb)


def _gdn_prep_bwd(qkv_act, braw, araw, alog, dtb, inv, du, dw, dqd, dkd, dattn, degl):
    t = qkv_act.shape[0]
    tb = min(t, GDN_PREP_ROWS)
    nt, nc = t // tb, tb // GDN_CHUNK
    width = GDN_HEADS * HEAD_DIM

    def body(q_ref, k_ref, v_ref, br_ref, ar_ref, al_ref, dt_ref, inv_ref, du_ref, dw_ref, dqd_ref, dkd_ref, dat_ref,
             deg_ref, dq_ref, dk_ref, dv_ref, dbr_ref, dar_ref, dal_ref, ddt_ref):
        @pl.when(pl.program_id(1) == 0)
        def _():
            dal_ref[...] = jnp.zeros_like(dal_ref)
            ddt_ref[...] = jnp.zeros_like(ddt_ref)

        ch = lambda r: _chunks(r, nc, GDN_CHUNK)
        _, vjp = jax.vjp(functools.partial(_gdn_prep, _mm_vjp, inv_kept=ch(inv_ref[0])), ch(q_ref[...]), ch(k_ref[...]),
                         ch(v_ref[...]), ch(br_ref[0]), ch(ar_ref[0]), al_ref[0, 0:1, :], dt_ref[0, 0:1, :])
        dq, dk, dv, dbr, dar, dal, ddt = vjp((ch(du_ref[...]), ch(dw_ref[...]), ch(dqd_ref[...]), ch(dkd_ref[...]),
                                              ch(dat_ref[0]), deg_ref[0][:, 0:1, :]))
        dq_ref[...] = dq.reshape(tb, HEAD_DIM)
        dk_ref[...] = dk.reshape(tb, HEAD_DIM)
        dv_ref[...] = dv.reshape(tb, HEAD_DIM)
        dbr_ref[0] = _lane_total(dbr.reshape(tb, LANES))
        dar_ref[0] = _lane_total(dar.reshape(tb, LANES))
        dal_ref[0, 0:1, :] += _lane_total(dal)
        ddt_ref[0, 0:1, :] += _lane_total(ddt)

    col, rep, par, att, egl = _gdn_prep_specs(tb)
    h = GDN_HEADS
    return pl.pallas_call(
        body, name="gdn_prep_bwd", grid=(h, nt),
        in_specs=[col(0), col(h), col(2 * h), rep, rep, par, par, att, col(0), col(0), col(0), col(0), att, egl],
        out_specs=[col(0), col(0), col(0), rep, rep, par, par],
        out_shape=[jax.ShapeDtypeStruct((t, width), F32)] * 3 + [jax.ShapeDtypeStruct((h, t, LANES), F32)] * 2
        + [jax.ShapeDtypeStruct((h, SUBLANES, LANES), F32)] * 2,
        compiler_params=_params("parallel", "arbitrary"))(qkv_act, qkv_act, qkv_act, braw, araw, alog, dtb, inv,
                                                         du, dw, dqd, dkd, dattn, degl)


def _scan_specs(tb, heads, chunk, rev, nt):
    ti = (lambda i: nt - 1 - i) if rev else (lambda i: i)
    row = pl.BlockSpec((tb, heads * HEAD_DIM), lambda i: (ti(i), 0))
    att = pl.BlockSpec((heads, tb, chunk), lambda i: (0, ti(i), 0))
    egl = pl.BlockSpec((heads, tb // chunk, SUBLANES, LANES), lambda i: (0, ti(i), 0, 0))
    hist = pl.BlockSpec((heads, tb // chunk, HEAD_DIM, HEAD_DIM), lambda i: (0, ti(i), 0, 0))
    gn = pl.BlockSpec((SUBLANES, LANES), lambda i: (0, 0))
    return row, att, egl, hist, gn


def _gdn_scan_fwd(u, w, qd, kd, attn, egl, zb, gn):
    t = u.shape[0]
    tb = min(t, 256)
    nt, nc = t // tb, tb // GDN_CHUNK
    nh = GDN_HEADS

    def body(u_ref, w_ref, qd_ref, kd_ref, at_ref, eg_ref, z_ref, gn_ref, y_ref, hist_ref, s_ref):
        @pl.when(pl.program_id(0) == 0)
        def _():
            s_ref[...] = jnp.zeros_like(s_ref)

        g = gn_ref[0:1, :]
        state = s_ref[...]
        for c in range(nc):
            rows = pl.ds(c * GDN_CHUNK, GDN_CHUNK)
            heads = lambda r: _by_head(r, rows, nh)
            hist_ref[:, c] = state.astype(BF16)
            y, state = _gdn_scan(_mm_raw, heads(u_ref), heads(w_ref), heads(qd_ref), heads(kd_ref), at_ref[:, rows, :],
                                 eg_ref[:, c, 0:1, :], heads(z_ref), g, state)
            _store_heads(y_ref, rows, y.astype(BF16))
        s_ref[...] = state

    row, att, egs, hist, gns = _scan_specs(tb, nh, GDN_CHUNK, False, nt)
    return pl.pallas_call(
        body, name="gdn_scan_fwd", grid=(nt,), in_specs=[row, row, row, row, att, egs, row, gns], out_specs=[row, hist],
        out_shape=[jax.ShapeDtypeStruct((t, nh * HEAD_DIM), BF16),
                   jax.ShapeDtypeStruct((nh, t // GDN_CHUNK, HEAD_DIM, HEAD_DIM), BF16)],
        scratch_shapes=[pltpu.VMEM((nh, HEAD_DIM, HEAD_DIM), F32)],
        compiler_params=_params("arbitrary"))(u, w, qd, kd, attn, egl, zb, gn)


def _gdn_scan_bwd(u, w, qd, kd, attn, egl, zb, gn, hist, dy):
    t = u.shape[0]
    tb = min(t, 256)
    nt, nc = t // tb, tb // GDN_CHUNK
    nh = GDN_HEADS

    def body(u_ref, w_ref, qd_ref, kd_ref, at_ref, eg_ref, z_ref, gn_ref, hist_ref, dy_ref,
             du_ref, dw_ref, dqd_ref, dkd_ref, dat_ref, deg_ref, dz_ref, dgn_ref, ds_ref):
        @pl.when(pl.program_id(0) == 0)
        def _():
            ds_ref[...] = jnp.zeros_like(ds_ref)
            dgn_ref[...] = jnp.zeros_like(dgn_ref)

        g = gn_ref[0:1, :]
        d_state = ds_ref[...]
        for c in reversed(range(nc)):
            rows = pl.ds(c * GDN_CHUNK, GDN_CHUNK)
            heads = lambda r: _by_head(r, rows, nh).astype(F32)
            _, vjp = jax.vjp(functools.partial(_gdn_scan, _mm_vjp), heads(u_ref), heads(w_ref), heads(qd_ref),
                             heads(kd_ref), at_ref[:, rows, :].astype(F32), eg_ref[:, c, 0:1, :], heads(z_ref), g,
                             hist_ref[:, c].astype(F32))
            du, dw, dqd, dkd, dat, deg, dz, dgn, d_state = vjp((heads(dy_ref), d_state))
            _store_heads(du_ref, rows, du)
            _store_heads(dw_ref, rows, dw)
            _store_heads(dqd_ref, rows, dqd)
            _store_heads(dkd_ref, rows, dkd)
            dat_ref[:, rows, :] = dat
            deg_ref[:, c] = jnp.broadcast_to(deg, (nh, SUBLANES, LANES))
            _store_heads(dz_ref, rows, dz.astype(BF16))
            dgn_ref[0:1, :] += dgn
        ds_ref[...] = d_state

    row, att, egs, hists, gns = _scan_specs(tb, nh, GDN_CHUNK, True, nt)
    wide = jax.ShapeDtypeStruct((t, nh * HEAD_DIM), F32)
    return pl.pallas_call(
        body, name="gdn_scan_bwd", grid=(nt,),
        in_specs=[row, row, row, row, att, egs, row, gns, hists, row],
        out_specs=[row, row, row, row, att, egs, row, gns],
        out_shape=[wide] * 4 + [jax.ShapeDtypeStruct((nh, t, GDN_CHUNK), F32),
                                jax.ShapeDtypeStruct((nh, t // GDN_CHUNK, SUBLANES, LANES), F32),
                                jax.ShapeDtypeStruct((t, nh * HEAD_DIM), BF16),
                                jax.ShapeDtypeStruct((SUBLANES, LANES), F32)],
        scratch_shapes=[pltpu.VMEM((nh, HEAD_DIM, HEAD_DIM), F32)],
        compiler_params=_params("arbitrary"))(u, w, qd, kd, attn, egl, zb, gn, hist, dy)


def _hgrn_prep(mm, qr, fr, lbl):
    n, c, _ = qr.shape
    lb = jax.nn.sigmoid(lbl[1:2, :] - lbl[0:1, :])
    f = lb + (1.0 - lb) * jax.nn.sigmoid(fr)
    q = jax.nn.silu(qr)
    k = 1.0 - f
    logf = jnp.log(f)
    ri = lax.broadcasted_iota(jnp.int32, (n, c, c), 1)
    ci = lax.broadcasted_iota(jnp.int32, (n, c, c), 2)
    b = mm((ri >= ci).astype(F32), logf, "nn", EXACT_LHS)
    attn = _hgrn_attn(mm, q, k, b)
    b_last = jnp.sum(logf, axis=1, keepdims=True)
    return q * jnp.exp(b), k * jnp.exp(b_last - b), attn, jnp.exp(b_last)


HGRN_SUB = 8
HGRN_PREP_ROWS = 2048


@functools.partial(jax.custom_vjp, nondiff_argnums=(1,))
def _roll_rows(x, shift):
    return pltpu.roll(x, shift, x.ndim - 2)


def _roll_rows_fwd(x, shift):
    return _roll_rows(x, shift), None


def _roll_rows_bwd(shift, _, d):
    return (pltpu.roll(d, d.shape[-2] - shift, d.ndim - 2),)


_roll_rows.defvjp(_roll_rows_fwd, _roll_rows_bwd)


@jax.custom_vjp
def _exp_clamped(v):
    return jnp.exp(jnp.minimum(v, 0.0))


def _exp_clamped_fwd(v):
    out = jnp.exp(jnp.minimum(v, 0.0))
    return out, out


def _exp_clamped_bwd(out, d):
    return (d * out,)


_exp_clamped.defvjp(_exp_clamped_fwd, _exp_clamped_bwd)


def _hgrn_attn(mm, q, k, b):
    n, c, d = q.shape
    sb = HGRN_SUB
    sub = lambda a: a.reshape(n * c // sb, sb, d)
    qs, ks, bs = sub(q), sub(k), sub(b)
    row = lax.broadcasted_iota(jnp.int32, (n, c, c), 1)
    col = lax.broadcasted_iota(jnp.int32, (n, c, c), 2)
    same_block = (row & -sb) == (col & -sb)
    attn = None
    for delta in range(sb):
        if delta == 0:
            prod = qs * ks
        else:
            prod = qs * _roll_rows(ks, delta) * _exp_clamped(bs - _roll_rows(bs, delta))
        sums = jnp.sum(prod, axis=-1, keepdims=True).reshape(n, c, 1)
        term = jnp.where(same_block & (row - col == delta), sums, 0.0)
        attn = term if attn is None else attn + term
    far = [jnp.zeros((n, sb, c), F32)]
    col8 = lax.broadcasted_iota(jnp.int32, (n, sb, c), 2)
    for i in range(1, c // sb):
        r0 = i * sb
        bi = b[:, r0:r0 + sb, :]
        ref = bi[:, 0:1, :]
        part = mm(q[:, r0:r0 + sb, :] * jnp.exp(bi - ref), k * _exp_clamped(ref - b), "nt", ONE_PASS)
        far.append(jnp.where(col8 < r0, part, 0.0))
    return attn + jnp.concatenate(far, axis=1)


def _hgrn_scan(mm, qe, kd, attn, ebl, iv, z, gn, state):
    o = mm(qe, state, "nt", ONE_PASS) + mm(attn, iv, "nn", ONE_PASS)
    new_state = state * ebl + mm(iv, kd, "tn", ONE_PASS)
    return _rms_gate(o, gn, z), new_state


def _hgrn_prep_specs(tb):
    col = pl.BlockSpec((tb, HEAD_DIM), lambda h, i: (i, h))
    lbs = pl.BlockSpec((2, HEAD_DIM), lambda h, i: (0, h))
    att = pl.BlockSpec((1, tb, HGRN_CHUNK), lambda h, i: (h, i, 0))
    ebl = pl.BlockSpec((1, tb // HGRN_CHUNK, SUBLANES, LANES), lambda h, i: (h, i, 0, 0))
    return col, lbs, att, ebl


def _hgrn_prep_fwd(qr, fr, lower_bounds):
    t = qr.shape[0]
    tb = min(t, HGRN_PREP_ROWS)
    nt, nc = t // tb, tb // HGRN_CHUNK
    hh = HGRN_HEADS

    def body(q_ref, f_ref, lb_ref, qe_ref, kd_ref, at_ref, eb_ref):
        ch = lambda r: _chunks(r, nc, HGRN_CHUNK)
        qe, kd, attn, ebl = _hgrn_prep(_mm_raw, ch(q_ref[...]), ch(f_ref[...]), lb_ref[...])
        qe_ref[...] = qe.reshape(tb, HEAD_DIM).astype(BF16)
        kd_ref[...] = kd.reshape(tb, HEAD_DIM).astype(BF16)
        at_ref[0] = attn.reshape(tb, HGRN_CHUNK).astype(BF16)
        eb_ref[0] = jnp.broadcast_to(ebl, (nc, SUBLANES, LANES))

    col, lbs, att, ebs = _hgrn_prep_specs(tb)
    return pl.pallas_call(
        body, name="hgrn_prep_fwd", grid=(hh, nt), in_specs=[col, col, lbs], out_specs=[col, col, att, ebs],
        out_shape=[jax.ShapeDtypeStruct((t, HGRN_WIDTH), BF16)] * 2
        + [jax.ShapeDtypeStruct((hh, t, HGRN_CHUNK), BF16), jax.ShapeDtypeStruct((hh, t // HGRN_CHUNK, SUBLANES, LANES), F32)],
        compiler_params=_params("parallel", "parallel"))(qr, fr, lower_bounds)


def _hgrn_prep_bwd(qr, fr, lower_bounds, dqe, dkd, dattn, debl):
    t = qr.shape[0]
    tb = min(t, HGRN_PREP_ROWS)
    nt, nc = t // tb, tb // HGRN_CHUNK
    hh = HGRN_HEADS

    def body(q_ref, f_ref, lb_ref, dqe_ref, dkd_ref, dat_ref, deb_ref, dq_ref, df_ref, dlb_ref):
        @pl.when(pl.program_id(1) == 0)
        def _():
            dlb_ref[...] = jnp.zeros_like(dlb_ref)

        ch = lambda r: _chunks(r, nc, HGRN_CHUNK)
        _, vjp = jax.vjp(functools.partial(_hgrn_prep, _mm_vjp), ch(q_ref[...]), ch(f_ref[...]), lb_ref[...])
        dq, df, dlb = vjp((ch(dqe_ref[...]), ch(dkd_ref[...]), ch(dat_ref[0]), deb_ref[0][:, 0:1, :]))
        dq_ref[...] = dq.reshape(tb, HEAD_DIM).astype(BF16)
        df_ref[...] = df.reshape(tb, HEAD_DIM).astype(BF16)
        dlb_ref[...] += dlb

    col, lbs, att, ebs = _hgrn_prep_specs(tb)
    return pl.pallas_call(
        body, name="hgrn_prep_bwd", grid=(hh, nt), in_specs=[col, col, lbs, col, col, att, ebs],
        out_specs=[col, col, lbs],
        out_shape=[jax.ShapeDtypeStruct((t, HGRN_WIDTH), BF16)] * 2 + [jax.ShapeDtypeStruct((2, HGRN_WIDTH), F32)],
        compiler_params=_params("parallel", "arbitrary"))(qr, fr, lower_bounds, dqe, dkd, dattn, debl)


def _hgrn_scan_fwd(qe, kd, attn, ebl, iv, z, gn):
    t = qe.shape[0]
    tb = min(t, 128)
    nt, nc = t // tb, tb // HGRN_CHUNK
    hh = HGRN_HEADS

    def body(qe_ref, kd_ref, at_ref, eb_ref, i_ref, z_ref, gn_ref, y_ref, hist_ref, s_ref):
        @pl.when(pl.program_id(0) == 0)
        def _():
            s_ref[...] = jnp.zeros_like(s_ref)

        g = gn_ref[0:1, :]
        state = s_ref[...]
        for c in range(nc):
            rows = pl.ds(c * HGRN_CHUNK, HGRN_CHUNK)
            heads = lambda r: _by_head(r, rows, hh)
            hist_ref[:, c] = state.astype(BF16)
            y, state = _hgrn_scan(_mm_raw, heads(qe_ref), heads(kd_ref), at_ref[:, rows, :], eb_ref[:, c, 0:1, :],
                                  heads(i_ref), heads(z_ref), g, state)
            _store_heads(y_ref, rows, y.astype(BF16))
        s_ref[...] = state

    row, att, ebs, hist, gns = _scan_specs(tb, hh, HGRN_CHUNK, False, nt)
    return pl.pallas_call(
        body, name="hgrn_scan_fwd", grid=(nt,), in_specs=[row, row, att, ebs, row, row, gns], out_specs=[row, hist],
        out_shape=[jax.ShapeDtypeStruct((t, HGRN_WIDTH), BF16),
                   jax.ShapeDtypeStruct((hh, t // HGRN_CHUNK, HEAD_DIM, HEAD_DIM), BF16)],
        scratch_shapes=[pltpu.VMEM((hh, HEAD_DIM, HEAD_DIM), F32)],
        compiler_params=_params("arbitrary"))(qe, kd, attn, ebl, iv, z, gn)


def _hgrn_scan_bwd(qe, kd, attn, ebl, iv, z, gn, hist, dy):
    t = qe.shape[0]
    tb = min(t, 128)
    nt, nc = t // tb, tb // HGRN_CHUNK
    hh = HGRN_HEADS

    def body(qe_ref, kd_ref, at_ref, eb_ref, i_ref, z_ref, gn_ref, hist_ref, dy_ref,
             dqe_ref, dkd_ref, dat_ref, deb_ref, di_ref, dz_ref, dgn_ref, ds_ref):
        @pl.when(pl.program_id(0) == 0)
        def _():
            ds_ref[...] = jnp.zeros_like(ds_ref)
            dgn_ref[...] = jnp.zeros_like(dgn_ref)

        g = gn_ref[0:1, :]
        d_state = ds_ref[...]
        for c in reversed(range(nc)):
            rows = pl.ds(c * HGRN_CHUNK, HGRN_CHUNK)
            heads = lambda r: _by_head(r, rows, hh).astype(F32)
            _, vjp = jax.vjp(functools.partial(_hgrn_scan, _mm_vjp), heads(qe_ref), heads(kd_ref),
                             at_ref[:, rows, :].astype(F32), eb_ref[:, c, 0:1, :], heads(i_ref), heads(z_ref), g,
                             hist_ref[:, c].astype(F32))
            dqe, dkd, dat, deb, di, dz, dgn, d_state = vjp((heads(dy_ref), d_state))
            _store_heads(dqe_ref, rows, dqe)
            _store_heads(dkd_ref, rows, dkd)
            dat_ref[:, rows, :] = dat
            deb_ref[:, c] = jnp.broadcast_to(deb, (hh, SUBLANES, LANES))
            _store_heads(di_ref, rows, di.astype(BF16))
            _store_heads(dz_ref, rows, dz.astype(BF16))
            dgn_ref[0:1, :] += dgn
        ds_ref[...] = d_state

    row, att, ebs, hists, gns = _scan_specs(tb, hh, HGRN_CHUNK, True, nt)
    wide = lambda dt: jax.ShapeDtypeStruct((t, HGRN_WIDTH), dt)
    return pl.pallas_call(
        body, name="hgrn_scan_bwd", grid=(nt,),
        in_specs=[row, row, att, ebs, row, row, gns, hists, row],
        out_specs=[row, row, att, ebs, row, row, gns],
        out_shape=[wide(F32), wide(F32), jax.ShapeDtypeStruct((hh, t, HGRN_CHUNK), F32),
                   jax.ShapeDtypeStruct((hh, t // HGRN_CHUNK, SUBLANES, LANES), F32), wide(BF16), wide(BF16),
                   jax.ShapeDtypeStruct((SUBLANES, LANES), F32)],
        scratch_shapes=[pltpu.VMEM((hh, HEAD_DIM, HEAD_DIM), F32)],
        compiler_params=_params("arbitrary"))(qe, kd, attn, ebl, iv, z, gn, hist, dy)


def _layer_norm(pre, g, b):
    mu = jnp.mean(pre, axis=-1, keepdims=True)
    d = pre - mu
    var = jnp.mean(d * d, axis=-1, keepdims=True)
    return d * lax.rsqrt(var + NORM_EPS) * g + b


def _lnpl_fwd(xin, s, p, wg, wpl, ln_g, ln_b):
    t = xin.shape[0]
    tt = min(t, 256)

    def body(x_ref, s_ref, p_ref, wg_ref, wpl_ref, g_ref, b_ref, o_ref, ob_ref):
        xn = _layer_norm(DEEPNORM_ALPHA * x_ref[...] + s_ref[...], g_ref[...], b_ref[...])
        gate = jax.nn.sigmoid(_mm_raw(xn, wg_ref[...], "nn", False))
        out = xn + _mm_raw(p_ref[...], wpl_ref[...], "nn", False) * gate
        o_ref[...] = out
        ob_ref[...] = out.astype(BF16)

    row = lambda w: pl.BlockSpec((tt, w), lambda i: (i, 0))
    full = lambda a: pl.BlockSpec(a.shape, lambda i: (0, 0))
    return pl.pallas_call(
        body, name="lnpl_fwd", grid=(t // tt,),
        in_specs=[row(D_MODEL), row(D_MODEL), row(PL_DIM), full(wg), full(wpl), full(ln_g), full(ln_b)],
        out_specs=[row(D_MODEL), row(D_MODEL)],
        out_shape=[jax.ShapeDtypeStruct((t, D_MODEL), F32), jax.ShapeDtypeStruct((t, D_MODEL), BF16)],
        compiler_params=_params("parallel"))(xin, s, p, wg, wpl, ln_g, ln_b)


def _lnpl_bwd(xin, s, p, wg, wpl, ln_g, ln_b, upstream, last, name):
    t = xin.shape[0]
    tt = min(t, 256)

    def body(x_ref, s_ref, p_ref, wg_ref, wpl_ref, g_ref, b_ref, up_ref,
             dpre_ref, dwg_ref, dwpl_ref, dg_ref, db_ref, loss_ref):
        @pl.when(pl.program_id(0) == 0)
        def _():
            for r in (dwg_ref, dwpl_ref, dg_ref, db_ref, loss_ref):
                r[...] = jnp.zeros_like(r)

        pre = DEEPNORM_ALPHA * x_ref[...] + s_ref[...]
        xn, ln_vjp = jax.vjp(_layer_norm, pre, g_ref[...], b_ref[...])
        gate = jax.nn.sigmoid(_mm_raw(xn, wg_ref[...], "nn", False))
        plv = _mm_raw(p_ref[...], wpl_ref[...], "nn", False)
        if last:
            err = xn + plv * gate - up_ref[...]
            dout = err * (1.0 / D_MODEL)
            tot = jnp.sum(jnp.sum(err * err, axis=1, keepdims=True), axis=0, keepdims=True) * (0.5 / D_MODEL)
            loss_ref[...] += jnp.broadcast_to(tot, loss_ref.shape)
        else:
            dout = up_ref[...]
        dplv = dout * gate
        dlogits = dout * plv * gate * (1.0 - gate)
        dwg_ref[...] += _mm_raw(xn, dlogits, "tn", False)
        dwpl_ref[...] += _mm_raw(p_ref[...], dplv, "tn", False)
        dxn = dout + _mm_raw(dlogits, wg_ref[...], "nt", False)
        dpre, dg, db = ln_vjp(dxn)
        dpre_ref[...] = dpre
        dg_ref[...] += dg
        db_ref[...] += db

    row = lambda w: pl.BlockSpec((tt, w), lambda i: (i, 0))
    full = lambda shape: pl.BlockSpec(shape, lambda i: (0, 0))
    return pl.pallas_call(
        body, name=name, grid=(t // tt,),
        in_specs=[row(D_MODEL), row(D_MODEL), row(PL_DIM), full(wg.shape), full(wpl.shape), full(ln_g.shape),
                  full(ln_b.shape), row(D_MODEL)],
        out_specs=[row(D_MODEL), full(wg.shape), full(wpl.shape), full(ln_g.shape), full(ln_b.shape),
                   full((SUBLANES, LANES))],
        out_shape=[jax.ShapeDtypeStruct((t, D_MODEL), F32), jax.ShapeDtypeStruct(wg.shape, F32),
                   jax.ShapeDtypeStruct(wpl.shape, F32), jax.ShapeDtypeStruct(ln_g.shape, F32),
                   jax.ShapeDtypeStruct(ln_b.shape, F32), jax.ShapeDtypeStruct((SUBLANES, LANES), F32)],
        compiler_params=_params("arbitrary"))(xin, s, p, wg, wpl, ln_g, ln_b, upstream)


def _pack_tail(dbr, dar):
    nh, t, _ = dbr.shape
    tt = min(t, 512)

    def body(b_ref, a_ref, o_ref):
        lane = lax.broadcasted_iota(jnp.int32, (tt, LANES), 1)
        acc = jnp.zeros((tt, LANES), F32)
        for h in range(nh):
            acc = jnp.where(lane == h, b_ref[h], acc)
            acc = jnp.where(lane == nh + h, a_ref[h], acc)
        o_ref[...] = acc.astype(BF16)

    spec = pl.BlockSpec((nh, tt, LANES), lambda i: (0, i, 0))
    return pl.pallas_call(
        body, name="pack_tail", grid=(t // tt,), in_specs=[spec, spec], out_specs=pl.BlockSpec((tt, LANES), lambda i: (i, 0)),
        out_shape=jax.ShapeDtypeStruct((t, LANES), BF16), compiler_params=_params("parallel"))(dbr, dar)


def _rep_rows(v):
    return jnp.broadcast_to(v.reshape(1, LANES), (SUBLANES, LANES))


def _rep_heads(v):
    return jnp.broadcast_to(v.reshape(-1, 1, 1), (v.shape[0], SUBLANES, LANES))


def _col_range(stacked, lo, hi):
    c = stacked.shape[2]
    parts = [stacked[s, :, max(lo, s * c) - s * c:min(hi, (s + 1) * c) - s * c]
             for s in range(4) if max(lo, s * c) < min(hi, (s + 1) * c)]
    return parts[0] if len(parts) == 1 else jnp.concatenate(parts, axis=1)


def _col_shards(pieces, c):
    shards, offs, o = [], [], 0
    for pc in pieces:
        offs.append(o)
        o += pc.shape[1]
    for s in range(4):
        lo, hi = s * c, (s + 1) * c
        parts = [pc[:, max(lo, o) - o:min(hi, o + pc.shape[1]) - o] for pc, o in zip(pieces, offs)
                 if max(lo, o) < min(hi, o + pc.shape[1])]
        shards.append(parts[0] if len(parts) == 1 else jnp.concatenate(parts, axis=1))
    return jnp.stack(shards)


def _local_step(x, p, target, w, late_weights, early_grads_ready, early_grads_swapped, last_grad_ready, start_token):
    a = DEEPNORM_ALPHA
    nh = GDN_HEADS
    xb = (x + start_token).astype(BF16)
    wie = w["w_in_even"]
    w_a, w_qkv, w_zb = _col_range(wie, 0, 4096), _col_range(wie, 4096, 7168), _col_range(wie, 7168, 8192)
    w_tail = jnp.pad(_col_range(wie, 8192, 8192 + 2 * nh), ((0, 0), (0, LANES - 2 * nh)))
    conv_a_w, conv_b_w = w["conv_a_w"], w["conv_b_w"]
    ln_g0, ln_b0, ln_g1, ln_b1 = (v.reshape(1, D_MODEL) for v in (w["ln_g"][0], w["ln_b"][0], w["ln_g"][1], w["ln_b"][1]))
    alog, dtb = _rep_heads(w["a_log"].reshape(nh)), _rep_heads(w["dt_bias"].reshape(nh))
    gdn_g, hgrn_g = _rep_rows(w["gdn_norm_g"]), _rep_rows(w["hgrn_norm_g"])

    proj_a = _matmul(xb, w_a, name="fwd_proj_a")
    proj_qkv = _matmul(xb, w_qkv, name="fwd_proj_qkv")
    proj_zb = _matmul(xb, w_zb, name="fwd_proj_zb")
    proj_tail = _matmul(xb, w_tail, name="fwd_proj_tail")
    rep = lambda cols: jnp.broadcast_to(cols.T[:, :, None], (nh, cols.shape[0], LANES))
    braw, araw = rep(proj_tail[:, :nh]), rep(proj_tail[:, nh:2 * nh])
    y_a = _conv_a_fwd(proj_a, conv_a_w)
    qkv_act = _conv_b_fwd(proj_qkv, conv_b_w)
    *gdn_pre, gdn_inv = _gdn_prep_fwd(qkv_act, braw, araw, alog, dtb)
    y_b, gdn_hist = _gdn_scan_fwd(*gdn_pre, proj_zb, gdn_g)
    w = {**w, **late_weights(y_b)}
    woe, wio, woo = w["w_out_even"], w["w_in_odd"], w["w_out_odd"]
    s0 = _matmul(y_b, woe[1024:], name="fwd_out_even_b", add=_matmul(y_a, woe[:1024], name="fwd_out_even_a"))
    x1, x1b = _lnpl_fwd(x, s0, p[0], w["w_pl_gate"][0], w["w_pl"][0], ln_g0, ln_b0)
    proj_o = [_matmul(x1b, wio[j], name=f"fwd_proj_odd{j}") for j in range(4)]
    hgrn_pre = _hgrn_prep_fwd(proj_o[0], proj_o[1], w["lower_bounds"])
    y_o, hgrn_hist = _hgrn_scan_fwd(*hgrn_pre, proj_o[2], proj_o[3], hgrn_g)
    s1 = _matmul(y_o, woo, name="fwd_out_odd")

    g = {}
    dpre1, dwg1, dwpl1, dlng1, dlnb1, loss = _lnpl_bwd(x1, s1, p[1], w["w_pl_gate"][1], w["w_pl"][1], ln_g1, ln_b1,
                                                     target, True, "lnpl_bwd_odd")
    dy_o = _matmul(dpre1, woo, tb=True, name="bwd_out_odd_dx")
    g["w_out_odd"] = _matmul(y_o, dpre1, ta=True, name="bwd_out_odd_dw")
    dqe, dkd, dat, deb, di, dz, dhg = _hgrn_scan_bwd(*hgrn_pre, proj_o[2], proj_o[3], hgrn_g, hgrn_hist, dy_o)
    dq, df, dlb = _hgrn_prep_bwd(proj_o[0], proj_o[1], w["lower_bounds"], dqe, dkd, dat, deb)
    dx1 = dpre1
    scale = a
    dws = []
    for j, dj in enumerate((dq, df, di, dz)):
        dx1 = _matmul(dj, wio[j], tb=True, add=dx1, add_scale=scale, name=f"bwd_proj_odd_dx{j}")
        scale = 1.0
        dws.append(_matmul(x1b, dj, ta=True, name=f"bwd_proj_odd_dw{j}"))
    g["w_in_odd"] = jnp.stack(dws)
    g["hgrn_norm_g"] = dhg[0:1]
    g["lower_bounds"] = dlb
    g["w_pl_gate1"], g["w_pl1"] = dwg1, dwpl1

    dpre0, dwg0, dwpl0, dlng0, dlnb0, _ = _lnpl_bwd(x, s0, p[0], w["w_pl_gate"][0], w["w_pl"][0], ln_g0, ln_b0,
                                                  dx1, False, "lnpl_bwd_even")
    g["w_pl_gate0"], g["w_pl0"] = dwg0, dwpl0
    g["ln_g"] = jnp.concatenate([dlng0, dlng1], axis=0)
    g["ln_b"] = jnp.concatenate([dlnb0, dlnb1], axis=0)
    dy_a = _matmul(dpre0, woe[:1024], tb=True, name="bwd_out_even_dxa")
    dy_b = _matmul(dpre0, woe[1024:], tb=True, name="bwd_out_even_dxb")
    g["w_out_even"] = jnp.concatenate([_matmul(y_a, dpre0, ta=True, name="bwd_out_even_dwa"),
                                       _matmul(y_b, dpre0, ta=True, name="bwd_out_even_dwb")], axis=0)
    token = early_grads_ready({n: g[n] for n in _RS_EARLY})
    d_a, dwa = _conv_a_bwd(proj_a, conv_a_w + token, dy_a)
    g["conv_a_w"] = dwa[:3]
    gdn_g = gdn_g + early_grads_swapped(d_a)
    du, dw, dqd, dkd, dat, deg, dzb, dgn = _gdn_scan_bwd(*gdn_pre, proj_zb, gdn_g, gdn_hist, dy_b)
    dqa, dka, dva, dbr, dar, dal, ddt = _gdn_prep_bwd(qkv_act, braw, araw, alog, dtb, gdn_inv, du, dw, dqd, dkd, dat, deg)
    g["a_log"] = dal[:, 0, 0].reshape(1, nh)
    g["dt_bias"] = ddt[:, 0, 0].reshape(1, nh)
    g["gdn_norm_g"] = dgn[0:1]
    d_pre_qkv, dwb = [], []
    for j, dj in enumerate((dqa, dka, dva)):
        dpj, dwj = _conv_b_bwd(proj_qkv, conv_b_w, dj, j, f"conv_b_bwd{j}")
        d_pre_qkv.append(dpj)
        dwb.append(dwj[:4])
    g["conv_b_w"] = jnp.concatenate(dwb, axis=1)
    d_tail = _pack_tail(dbr, dar)
    pieces = [(d_a, w_a), (d_pre_qkv[0], w_qkv[:, :1024]), (d_pre_qkv[1], w_qkv[:, 1024:2048]),
              (d_pre_qkv[2], w_qkv[:, 2048:]), (dzb, w_zb), (d_tail, w_tail)]
    dws = [_matmul(xb, dj, ta=True, name=f"bwd_proj_even_dw{j}") for j, (dj, _) in enumerate(pieces)]
    dws[-1] = dws[-1][:, :2 * nh]
    g["w_in_even"] = _col_shards(dws, wie.shape[2])
    token = last_grad_ready(g["w_in_even"])
    dx = dpre0
    scale = a
    for j, (dj, wj) in enumerate(pieces):
        dx = _matmul(dj, wj + jnp.asarray(token).astype(BF16) if j == 0 else wj, tb=True, add=dx, add_scale=scale,
                     name=f"bwd_proj_even_dx{j}")
        scale = 1.0
    return loss, dx, g


def _adamw(w, g, m, v, name):
    lead, rows, cols = w.shape
    if rows % SUBLANES == 0 or rows <= 256:
        tr, tc = (rows if rows <= 256 else 256), cols
    else:
        tr, tc = rows, 256
    assert rows % tr == 0 and cols % tc == 0, (name, rows, cols)

    def body(w_ref, g_ref, m_ref, v_ref, d_ref, nm_ref, nv_ref):
        gg = g_ref[...]
        nm = ADAM_B1 * m_ref[...] + (1.0 - ADAM_B1) * gg
        nv = ADAM_B2 * v_ref[...] + (1.0 - ADAM_B2) * jnp.square(gg)
        m_hat = nm / (1.0 - ADAM_B1 ** ADAM_STEP)
        v_hat = nv / (1.0 - ADAM_B2 ** ADAM_STEP)
        d_ref[...] = -ADAM_LR * (m_hat / (jnp.sqrt(v_hat) + ADAM_EPS) + ADAM_WD * w_ref[...])
        nm_ref[...] = nm
        nv_ref[...] = nv

    spec = pl.BlockSpec((1, tr, tc), lambda l, i, j: (l, i, j))
    return pl.pallas_call(
        body, name=name, grid=(lead, rows // tr, cols // tc), in_specs=[spec] * 4, out_specs=[spec] * 3,
        out_shape=[jax.ShapeDtypeStruct(w.shape, F32)] * 3,
        compiler_params=_params("parallel", "parallel", "parallel"))(w, g, m, v)


MESH = pl.DeviceIdType.MESH
N_DEV = 8
HBM_SPEC = pl.BlockSpec(memory_space=pltpu.HBM)
VMEM_SPEC = pl.BlockSpec(memory_space=pltpu.VMEM)


def _coords():
    return lax.axis_index("x"), lax.axis_index("y"), lax.axis_index("c")


def _flip(v, bit):
    return 1 - v if bit else v


def _remote(src, dst, send_sem, recv_sem, dev):
    return pltpu.make_async_remote_copy(src_ref=src, dst_ref=dst, send_sem=send_sem, recv_sem=recv_sem,
                                        device_id=dev, device_id_type=MESH)


def _exchange_small(buf, reduce, name):
    rows = buf.shape[0]

    def body(in_ref, out_ref, slots, send_sems, recv_sems):
        x, y, c = _coords()
        me = 4 * x + 2 * y + c
        slots[me] = in_ref[...]
        peer = lambda k: (_flip(x, (k >> 2) & 1), _flip(y, (k >> 1) & 1), _flip(c, k & 1))
        sends = []
        for k in range(1, N_DEV):
            cp = _remote(in_ref, slots.at[me], send_sems.at[k - 1], recv_sems.at[k - 1], peer(k))
            cp.start()
            sends.append(cp)
        for k in range(1, N_DEV):
            px, py, pc = peer(k)
            _remote(in_ref, slots.at[4 * px + 2 * py + pc], send_sems.at[k - 1], recv_sems.at[k - 1], peer(k)).wait_recv()
        for cp in sends:
            cp.wait_send()
        if reduce:
            acc = slots[0]
            for d in range(1, N_DEV):
                acc = acc + slots[d]
            out_ref[...] = acc
        else:
            out_ref[...] = slots[...]

    out_shape = (rows, LANES) if reduce else (N_DEV, rows, LANES)
    return pl.pallas_call(
        body, name=name, in_specs=[VMEM_SPEC], out_specs=VMEM_SPEC, out_shape=jax.ShapeDtypeStruct(out_shape, F32),
        scratch_shapes=[pltpu.VMEM((N_DEV, rows, LANES), F32), pltpu.SemaphoreType.DMA((N_DEV - 1,)),
                        pltpu.SemaphoreType.DMA((N_DEV - 1,))])(buf)


def _half_rows(half, which):
    return pl.ds(pl.multiple_of(which * half, 16), half)


def _other_chip(x, y, k):
    return _flip(x, (k >> 1) & 1), _flip(y, k & 1)


SEM_SPEC = pl.BlockSpec(memory_space=pltpu.SEMAPHORE)
DATAFLOW = pltpu.SideEffectType.DATAFLOW_SIDE_EFFECTING


def _ici_piece(srcs, lands, send_sems, recv_sems, i, k, x, y, c):
    half = srcs[i].shape[0] // 2
    ox, oy = _other_chip(x, y, k)
    return _remote(srcs[i].at[_half_rows(half, c)], lands[i].at[2 * x + y, _half_rows(half, c)],
                   send_sems.at[3 * i + k - 1], recv_sems.at[3 * i + k - 1], (ox, oy, c)), (ox, oy)


def _gather_start(shards, name):
    n = len(shards)

    def body(*refs):
        srcs, lands = refs[:n], refs[n:2 * n]
        send_sems, recv_sems = refs[2 * n], refs[2 * n + 1]
        token = refs[-1]
        x, y, c = _coords()
        for i in range(n):
            for k in (1, 2, 3):
                _ici_piece(srcs, lands, send_sems, recv_sems, i, k, x, y, c)[0].start()
        token[...] = jnp.zeros_like(token)

    hbm = lambda a: pltpu.with_memory_space_constraint(a, pltpu.HBM)
    lands = [lax.empty((4,) + s.shape, s.dtype) for s in shards]
    out = pl.pallas_call(
        body, name=name,
        out_shape=(pltpu.SemaphoreType.DMA((3 * n,)), pltpu.SemaphoreType.DMA((3 * n,)),
                   *[pltpu.HBM(s.shape, s.dtype) for s in shards], *[pltpu.HBM(a.shape, a.dtype) for a in lands],
                   jax.ShapeDtypeStruct((SUBLANES, LANES), F32)),
        in_specs=[HBM_SPEC] * (2 * n), out_specs=(SEM_SPEC, SEM_SPEC, *[HBM_SPEC] * (2 * n), VMEM_SPEC),
        input_output_aliases={i: 2 + i for i in range(2 * n)},
        compiler_params=pltpu.CompilerParams(has_side_effects=DATAFLOW))(*[hbm(s) for s in shards], *[hbm(a) for a in lands])
    return out[0], out[1], out[2:2 + n], out[2 + n:2 + 2 * n], out[-1]


def _gather_wait(send_sems, recv_sems, srcs, lands, after, name):
    n = len(srcs)

    def body(*refs):
        src_refs, land_refs = refs[:n], refs[n:2 * n]
        send_sems, recv_sems = refs[2 * n], refs[2 * n + 1]
        x, y, c = _coords()
        for i in range(n):
            half = src_refs[i].shape[0] // 2
            for k in (1, 2, 3):
                cp, (ox, oy) = _ici_piece(src_refs, land_refs, send_sems, recv_sems, i, k, x, y, c)
                cp.wait_send()
                piece = land_refs[i].at[2 * ox + oy, _half_rows(half, c)]
                _remote(piece, piece, send_sems.at[3 * i + k - 1], recv_sems.at[3 * i + k - 1], (ox, oy, c)).wait_recv()

    out = pl.pallas_call(
        body, name=name,
        out_shape=(*[pltpu.HBM(s.shape, s.dtype) for s in srcs], *[pltpu.HBM(a.shape, a.dtype) for a in lands]),
        in_specs=[HBM_SPEC] * (2 * n) + [SEM_SPEC, SEM_SPEC, pl.BlockSpec(memory_space=pl.ANY)],
        out_specs=tuple([HBM_SPEC] * (2 * n)), input_output_aliases={i: i for i in range(2 * n)},
        compiler_params=pltpu.CompilerParams(has_side_effects=DATAFLOW))(*srcs, *lands, send_sems, recv_sems, after)
    return out[n:]


def _gather_forward(lands, name):
    n = len(lands)

    def body(*refs):
        ins, outs = refs[:n], refs[n:2 * n]
        send_sems, recv_sems = refs[2 * n:]
        x, y, c = _coords()
        sends = []
        for i in range(n):
            half = ins[i].shape[1] // 2
            for k in (1, 2, 3):
                ox, oy = _other_chip(x, y, k)
                cp = _remote(ins[i].at[2 * ox + oy, _half_rows(half, c)], outs[i].at[2 * ox + oy, _half_rows(half, c)],
                             send_sems.at[3 * i + k - 1], recv_sems.at[3 * i + k - 1], (x, y, 1 - c))
                cp.start()
                sends.append(cp)
        for i in range(n):
            half = ins[i].shape[1] // 2
            for k in (1, 2, 3):
                ox, oy = _other_chip(x, y, k)
                piece = outs[i].at[2 * ox + oy, _half_rows(half, 1 - c)]
                _remote(piece, piece, send_sems.at[3 * i + k - 1], recv_sems.at[3 * i + k - 1], (x, y, 1 - c)).wait_recv()
        for cp in sends:
            cp.wait_send()

    return pl.pallas_call(
        body, name=name, in_specs=[HBM_SPEC] * n, out_specs=[HBM_SPEC] * n,
        out_shape=[jax.ShapeDtypeStruct(a.shape, a.dtype) for a in lands],
        input_output_aliases={i: i for i in range(n)},
        scratch_shapes=[pltpu.SemaphoreType.DMA((3 * n,))] * 2)(*lands)


def _rs_sibling_swap(g4s, name):
    n = len(g4s)

    def body(*refs):
        ins, outs = refs[:n], refs[n:2 * n]
        send_sems, recv_sems = refs[2 * n:]
        x, y, c = _coords()
        sends = []
        for i in range(n):
            half = ins[i].shape[1] // 2
            for s in range(4):
                cp = _remote(ins[i].at[s, _half_rows(half, 1 - c)], outs[i].at[s], send_sems.at[4 * i + s],
                             recv_sems.at[4 * i + s], (x, y, 1 - c))
                cp.start()
                sends.append(cp)
        for cp in sends:
            cp.wait_recv()
        for cp in sends:
            cp.wait_send()

    return pl.pallas_call(
        body, name=name, in_specs=[HBM_SPEC] * n, out_specs=[HBM_SPEC] * n,
        out_shape=[jax.ShapeDtypeStruct((4, g.shape[1] // 2, g.shape[2]), g.dtype) for g in g4s],
        scratch_shapes=[pltpu.SemaphoreType.DMA((4 * n,))] * 2)(*g4s)


def _rs_add_sibling(g4, got, c_idx, name):
    _, rows, cols = g4.shape
    half = rows // 2
    tr = min(half, 256)
    nb = half // tr

    def body(c_ref, a_ref, b_ref, o_ref, ob_ref):
        total = a_ref[...] + b_ref[...]
        o_ref[...] = total
        ob_ref[...] = total.astype(BF16)

    blk = (1, tr, cols)
    out = pl.BlockSpec(blk, lambda s, i, c_ref: (s, i, 0))
    grid_spec = pltpu.PrefetchScalarGridSpec(
        num_scalar_prefetch=1, grid=(4, nb),
        in_specs=[pl.BlockSpec(blk, lambda s, i, c_ref: (s, c_ref[0] * nb + i, 0)), out],
        out_specs=[out, out])
    return pl.pallas_call(
        body, name=name, grid_spec=grid_spec,
        out_shape=[jax.ShapeDtypeStruct(got.shape, F32), jax.ShapeDtypeStruct(got.shape, BF16)],
        compiler_params=_params("parallel", "parallel"))(c_idx, g4, got)


def _rs_add_chips(p4, got3, idx, name):
    _, half, cols = p4.shape
    tr = min(half, 256)
    nb = half // tr

    def body(idx_ref, p_ref, a_ref, b_ref, c_ref, o_ref):
        o_ref[...] = ((p_ref[0] + a_ref[0].astype(F32)) + b_ref[0].astype(F32)) + c_ref[0].astype(F32)

    blk = (1, tr, cols)
    grid_spec = pltpu.PrefetchScalarGridSpec(
        num_scalar_prefetch=1, grid=(nb,),
        in_specs=[pl.BlockSpec(blk, lambda i, idx_ref: (idx_ref[0], i, 0))]
        + [pl.BlockSpec(blk, functools.partial(lambda k, i, idx_ref: (k, i, 0), k)) for k in range(3)],
        out_specs=pl.BlockSpec((tr, cols), lambda i, idx_ref: (idx_ref[1] * nb + i, 0)))
    return pl.pallas_call(body, name=name, grid_spec=grid_spec, out_shape=jax.ShapeDtypeStruct((2 * half, cols), F32),
                          compiler_params=_params("parallel"))(idx, p4, got3, got3, got3)


def _rs_share_halves(bufs, name):
    n = len(bufs)

    def body(*refs):
        ins, outs = refs[:n], refs[n:2 * n]
        send_sems, recv_sems = refs[2 * n:]
        x, y, c = _coords()
        sends = []
        for i in range(n):
            half = ins[i].shape[0] // 2
            cp = _remote(ins[i].at[_half_rows(half, c)], outs[i].at[_half_rows(half, c)], send_sems.at[i],
                         recv_sems.at[i], (x, y, 1 - c))
            cp.start()
            sends.append(cp)
        for i in range(n):
            half = ins[i].shape[0] // 2
            _remote(ins[i].at[_half_rows(half, c)], outs[i].at[_half_rows(half, 1 - c)], send_sems.at[i],
                    recv_sems.at[i], (x, y, 1 - c)).wait_recv()
        for cp in sends:
            cp.wait_send()

    return pl.pallas_call(
        body, name=name, in_specs=[HBM_SPEC] * n, out_specs=[HBM_SPEC] * n,
        out_shape=[jax.ShapeDtypeStruct(b.shape, b.dtype) for b in bufs],
        input_output_aliases={i: i for i in range(n)},
        scratch_shapes=[pltpu.SemaphoreType.DMA((n,))] * 2)(*bufs)


def _scatter_piece(srcs, lands, send_sems, recv_sems, i, k, x, y, c):
    ox, oy = _other_chip(x, y, k)
    return _remote(srcs[i].at[2 * ox + oy], lands[i].at[k - 1], send_sems.at[3 * i + k - 1],
                   recv_sems.at[3 * i + k - 1], (ox, oy, c))


def _rs_scatter_start(p4s, name):
    n = len(p4s)

    def body(*refs):
        srcs, lands = refs[:n], refs[n:2 * n]
        send_sems, recv_sems = refs[2 * n], refs[2 * n + 1]
        token = refs[-1]
        x, y, c = _coords()
        for i in range(n):
            for k in (1, 2, 3):
                _scatter_piece(srcs, lands, send_sems, recv_sems, i, k, x, y, c).start()
        token[...] = jnp.zeros_like(token)

    hbm = lambda a: pltpu.with_memory_space_constraint(a, pltpu.HBM)
    lands = [lax.empty((3,) + p.shape[1:], p.dtype) for p in p4s]
    out = pl.pallas_call(
        body, name=name,
        out_shape=(pltpu.SemaphoreType.DMA((3 * n,)), pltpu.SemaphoreType.DMA((3 * n,)),
                   *[pltpu.HBM(p.shape, p.dtype) for p in p4s], *[pltpu.HBM(a.shape, a.dtype) for a in lands],
                   jax.ShapeDtypeStruct((SUBLANES, LANES), F32)),
        in_specs=[HBM_SPEC] * (2 * n), out_specs=(SEM_SPEC, SEM_SPEC, *[HBM_SPEC] * (2 * n), VMEM_SPEC),
        input_output_aliases={i: 2 + i for i in range(2 * n)},
        compiler_params=pltpu.CompilerParams(has_side_effects=DATAFLOW))(*[hbm(p) for p in p4s], *[hbm(a) for a in lands])
    return out[0], out[1], out[2:2 + n], out[2 + n:2 + 2 * n], out[-1]


def _rs_scatter_wait(send_sems, recv_sems, srcs, lands, after, name):
    n = len(srcs)

    def body(*refs):
        src_refs, land_refs = refs[:n], refs[n:2 * n]
        send_sems, recv_sems = refs[2 * n], refs[2 * n + 1]
        x, y, c = _coords()
        for i in range(n):
            for k in (1, 2, 3):
                cp = _scatter_piece(src_refs, land_refs, send_sems, recv_sems, i, k, x, y, c)
                cp.wait_send()
                cp.wait_recv()

    out = pl.pallas_call(
        body, name=name,
        out_shape=(*[pltpu.HBM(s.shape, s.dtype) for s in srcs], *[pltpu.HBM(a.shape, a.dtype) for a in lands]),
        in_specs=[HBM_SPEC] * (2 * n) + [SEM_SPEC, SEM_SPEC, pl.BlockSpec(memory_space=pl.ANY)],
        out_specs=tuple([HBM_SPEC] * (2 * n)), input_output_aliases={i: i for i in range(2 * n)},
        compiler_params=pltpu.CompilerParams(has_side_effects=DATAFLOW))(*srcs, *lands, send_sems, recv_sems, after)
    return out[n:]


def _swap_piece(srcs, lands, send_sems, recv_sems, i, s, x, y, c):
    half = srcs[i].shape[1] // 2
    return _remote(srcs[i].at[s, _half_rows(half, 1 - c)], lands[i].at[s], send_sems.at[4 * i + s],
                   recv_sems.at[4 * i + s], (x, y, 1 - c))


def _rs_swap_start(g4s, name):
    n = len(g4s)

    def body(*refs):
        srcs, lands = refs[:n], refs[n:2 * n]
        send_sems, recv_sems = refs[2 * n], refs[2 * n + 1]
        token = refs[-1]
        x, y, c = _coords()
        for i in range(n):
            for s in range(4):
                _swap_piece(srcs, lands, send_sems, recv_sems, i, s, x, y, c).start()
        token[...] = jnp.zeros_like(token)

    hbm = lambda a: pltpu.with_memory_space_constraint(a, pltpu.HBM)
    lands = [lax.empty((4, g.shape[1] // 2, g.shape[2]), g.dtype) for g in g4s]
    out = pl.pallas_call(
        body, name=name,
        out_shape=(pltpu.SemaphoreType.DMA((4 * n,)), pltpu.SemaphoreType.DMA((4 * n,)),
                   *[pltpu.HBM(g.shape, g.dtype) for g in g4s], *[pltpu.HBM(a.shape, a.dtype) for a in lands],
                   jax.ShapeDtypeStruct((SUBLANES, LANES), F32)),
        in_specs=[HBM_SPEC] * (2 * n), out_specs=(SEM_SPEC, SEM_SPEC, *[HBM_SPEC] * (2 * n), VMEM_SPEC),
        input_output_aliases={i: 2 + i for i in range(2 * n)},
        compiler_params=pltpu.CompilerParams(has_side_effects=DATAFLOW))(*[hbm(g) for g in g4s], *[hbm(a) for a in lands])
    return out[0], out[1], out[2:2 + n], out[2 + n:2 + 2 * n], out[-1]


def _rs_swap_wait(send_sems, recv_sems, srcs, lands, after, name):
    n = len(srcs)

    def body(*refs):
        src_refs, land_refs = refs[:n], refs[n:2 * n]
        send_sems, recv_sems = refs[2 * n], refs[2 * n + 1]
        x, y, c = _coords()
        for i in range(n):
            for s in range(4):
                cp = _swap_piece(src_refs, land_refs, send_sems, recv_sems, i, s, x, y, c)
                cp.wait_send()
                cp.wait_recv()

    out = pl.pallas_call(
        body, name=name,
        out_shape=(*[pltpu.HBM(s.shape, s.dtype) for s in srcs], *[pltpu.HBM(a.shape, a.dtype) for a in lands]),
        in_specs=[HBM_SPEC] * (2 * n) + [SEM_SPEC, SEM_SPEC, pl.BlockSpec(memory_space=pl.ANY)],
        out_specs=tuple([HBM_SPEC] * (2 * n)), input_output_aliases={i: i for i in range(2 * n)},
        compiler_params=pltpu.CompilerParams(has_side_effects=DATAFLOW))(*srcs, *lands, send_sems, recv_sems, after)
    return out[:n], out[n:]


def _rs_front(g4s, names, tag):
    c_idx = jnp.stack([lax.axis_index("c")]).astype(jnp.int32)
    got = _rs_sibling_swap(g4s, f"rs_sibling_swap_{tag}")
    return [_rs_add_sibling(g, s, c_idx, f"rs_add_sibling_{nm}") for g, s, nm in zip(g4s, got, names)]


def _rs_back(p4s, got3, names):
    x, y, c = _coords()
    idx = jnp.stack([2 * x + y, c]).astype(jnp.int32)
    return [_rs_add_chips(p, t, idx, f"rs_add_chips_{nm}") for (p, _), t, nm in zip(p4s, got3, names)]


def _cols_split(full):
    r, c4 = full.shape
    return full.reshape(r, 4, c4 // 4).transpose(1, 0, 2)


_BIG = {
    "w_in_even": ((1024, 2052), lambda s: s),
    "w_out_even": ((512, 1024), lambda s: s.reshape(2048, 1024)),
    "w_in_odd": ((1024, 2048), lambda s: s),
    "w_out_odd": ((512, 1024), lambda s: s.reshape(2048, 1024)),
    "w_pl": ((512, 256), lambda s: s.reshape(4, 2, 256, 256).transpose(1, 2, 0, 3).reshape(2, 256, 1024)),
    "w_pl_gate": ((512, 1024), lambda s: s.reshape(4, 2, 256, 1024).transpose(1, 0, 2, 3).reshape(2, 1024, 1024)),
}


_RS_EARLY = {
    "w_in_odd": lambda f: f,
    "w_out_odd": lambda f: f.reshape(4, 512, 1024),
    "w_pl_gate1": lambda f: f.reshape(4, 256, 1024),
    "w_pl1": _cols_split,
    "w_out_even": lambda f: f.reshape(4, 512, 1024),
    "w_pl_gate0": lambda f: f.reshape(4, 256, 1024),
    "w_pl0": _cols_split,
}
_RS_LATE = {"w_in_even": lambda f: f}


def _size(shape):
    n = 1
    for d in shape:
        n *= d
    return n


_SMALL = {"a_log": (1, 8), "dt_bias": (1, 8), "gdn_norm_g": (1, 128), "hgrn_norm_g": (1, 128),
          "lower_bounds": (2, 2048), "ln_g": (2, 1024), "ln_b": (2, 1024), "conv_a_w": (3, 1024), "conv_b_w": (4, 3072)}
_CONV_SHARD = {"conv_a_w": (3, 256), "conv_b_w": (4, 768)}


def _pack_small(parts, shapes, head_rows=0):
    rows = []
    for n, shape in shapes.items():
        v = parts[n].reshape(-1)
        rows.append(jnp.pad(v, (0, -v.shape[0] % LANES)).reshape(-1, LANES))
    buf = jnp.concatenate(rows, axis=0)
    return jnp.pad(buf, ((head_rows, -(buf.shape[0] + head_rows) % SUBLANES), (0, 0)))


def _unpack_small(buf, shapes, head_rows=0):
    out, off = {}, head_rows
    for n, shape in shapes.items():
        nrow = -(-_size(shape) // LANES)
        out[n] = buf[off:off + nrow].reshape(-1)[:_size(shape)].reshape(shape)
        off += nrow
    return out


_WEIGHTS = ["w_in_even", "conv_a_w", "conv_b_w", "a_log", "dt_bias", "gdn_norm_g", "w_out_even", "w_in_odd",
            "lower_bounds", "hgrn_norm_g", "w_out_odd", "ln_g", "ln_b", "w_pl", "w_pl_gate"]


def kernel(x, p, w_in_even, conv_a_w, conv_b_w, a_log, dt_bias, gdn_norm_g, w_out_even, w_in_odd, lower_bounds, hgrn_norm_g, w_out_odd, ln_g, ln_b, w_pl, w_pl_gate, loss_target, m_w_in_even, m_conv_a_w, m_conv_b_w, m_a_log, m_dt_bias, m_gdn_norm_g, m_w_out_even, m_w_in_odd, m_lower_bounds, m_hgrn_norm_g, m_w_out_odd, m_ln_g, m_ln_b, m_w_pl, m_w_pl_gate, v_w_in_even, v_conv_a_w, v_conv_b_w, v_a_log, v_dt_bias, v_gdn_norm_g, v_w_out_even, v_w_in_odd, v_lower_bounds, v_hgrn_norm_g, v_w_out_odd, v_ln_g, v_ln_b, v_w_pl, v_w_pl_gate):
    w = dict(zip(_WEIGHTS, (w_in_even, conv_a_w, conv_b_w, a_log, dt_bias, gdn_norm_g, w_out_even, w_in_odd,
                            lower_bounds, hgrn_norm_g, w_out_odd, ln_g, ln_b, w_pl, w_pl_gate)))
    m = dict(zip(_WEIGHTS, (m_w_in_even, m_conv_a_w, m_conv_b_w, m_a_log, m_dt_bias, m_gdn_norm_g, m_w_out_even,
                            m_w_in_odd, m_lower_bounds, m_hgrn_norm_g, m_w_out_odd, m_ln_g, m_ln_b, m_w_pl, m_w_pl_gate)))
    v = dict(zip(_WEIGHTS, (v_w_in_even, v_conv_a_w, v_conv_b_w, v_a_log, v_dt_bias, v_gdn_norm_g, v_w_out_even,
                            v_w_in_odd, v_lower_bounds, v_hgrn_norm_g, v_w_out_odd, v_ln_g, v_ln_b, v_w_pl, v_w_pl_gate)))
    chip = 2 * lax.axis_index("x") + lax.axis_index("y")

    names = list(_BIG)
    shard_shapes = {n: _BIG[n][0] for n in names}
    early, late = names[:1], names[1:]
    shards = {n: w[n].reshape(shard_shapes[n]).astype(BF16) for n in early}
    whole = lambda n, stacked: _BIG[n][1](lax.dynamic_update_slice(stacked, shards[n][None], (chip, 0, 0)))
    conv_mine = _pack_small({n: w[n] for n in _CONV_SHARD}, _CONV_SHARD)
    conv_all = _exchange_small(conv_mine, False, "gather_conv")
    shards, conv_all = lax.optimization_barrier((shards, conv_all))
    first = _gather_start([shards[n] for n in early], "gather_first_start")
    shards.update({n: (w[n].reshape(shard_shapes[n]) + first[4][0, 0]).astype(BF16) for n in late})
    send_sems, recv_sems, srcs, lands, token = _gather_start([shards[n] for n in late], "gather_rest_start")

    def late_weights(after):
        landed = _gather_forward(_gather_wait(send_sems, recv_sems, srcs, lands, after, "gather_rest_wait"),
                                 "gather_rest_forward")
        return {n: whole(n, ga) for n, ga in zip(late, landed)}

    landed = _gather_forward(_gather_wait(*first[:4], token, "gather_first_wait"), "gather_first_forward")
    full = {n: whole(n, ga) for n, ga in zip(early, landed)}
    conv_by_chip = [_unpack_small(conv_all[2 * s], _CONV_SHARD) for s in range(4)]
    for n in _CONV_SHARD:
        full[n] = jnp.concatenate([conv_by_chip[s][n] for s in range(4)], axis=1)
    for n in _SMALL:
        if n not in _CONV_SHARD:
            full[n] = w[n]

    early_rs = {}

    def early_grads_ready(grads):
        early_rs["swap"] = _rs_swap_start([_RS_EARLY[n](grads[n]) for n in _RS_EARLY], "rs_swap_early_start")
        return early_rs["swap"][4][0, 0]

    def early_grads_swapped(after):
        g4s, got = _rs_swap_wait(*early_rs["swap"][:4], after, "rs_swap_early_wait")
        c_idx = jnp.stack([lax.axis_index("c")]).astype(jnp.int32)
        early_rs["p4s"] = [_rs_add_sibling(g_, s_, c_idx, f"rs_add_sibling_{nm}") for g_, s_, nm in zip(g4s, got, _RS_EARLY)]
        early_rs["sems"] = _rs_scatter_start([pb for _, pb in early_rs["p4s"]], "rs_scatter_early_start")
        return early_rs["sems"][4][0, 0]

    late_rs = {}

    def last_grad_ready(grad):
        late_rs["p4s"] = _rs_front([_RS_LATE[n](grad) for n in _RS_LATE], list(_RS_LATE), "late")
        late_rs["sems"] = _rs_scatter_start([pb for _, pb in late_rs["p4s"]], "rs_scatter_late_start")
        return late_rs["sems"][4][0, 0]

    loss_part, dx, g = _local_step(x[0], p[:, 0], loss_target[0], full, late_weights, early_grads_ready,
                                   early_grads_swapped, last_grad_ready, token[0, 0])

    late_p4s, late_sems = late_rs["p4s"], late_rs["sems"]
    got3 = _rs_scatter_wait(*early_rs["sems"][:4], dx, "rs_scatter_early_wait")
    summed = dict(zip(_RS_EARLY, _rs_share_halves(_rs_back(early_rs["p4s"], got3, list(_RS_EARLY)), "rs_share_early")))
    g_big = {n: summed[n] for n in names if n in summed}
    g_big["w_pl"] = jnp.stack([summed["w_pl0"], summed["w_pl1"]])
    g_big["w_pl_gate"] = jnp.stack([summed["w_pl_gate0"], summed["w_pl_gate1"]])
    small_sum = _exchange_small(jnp.concatenate([loss_part, _pack_small(g, _SMALL)], axis=0), True, "reduce_small")
    loss = small_sum[0, 0]
    g_small = _unpack_small(small_sum, _SMALL, head_rows=SUBLANES)
    for n, (rows, cols) in _CONV_SHARD.items():
        g_small[n] = lax.dynamic_slice_in_dim(g_small[n], chip * cols, cols, axis=1)

    grads, delta, new_m, new_v = {}, {}, {}, {}
    for n in late:
        grads[n] = g_big[n].reshape(w[n].shape)
        delta[n], new_m[n], new_v[n] = _adamw(w[n], grads[n], m[n], v[n], f"adamw_{n}")
    own = {n: (_CONV_SHARD[n] if n in _CONV_SHARD else _SMALL[n]) for n in _SMALL}
    packs = [_pack_small({n: src[n] for n in _SMALL}, own)[None] for src in (w, g_small, m, v)]
    outs = [_unpack_small(t[0], own) for t in _adamw(*packs, "adamw_small")]
    for n in _SMALL:
        grads[n] = g_small[n].reshape(w[n].shape)
        delta[n], new_m[n], new_v[n] = (t[n].reshape(w[n].shape) for t in outs)
    got3 = _rs_scatter_wait(*late_sems[:4], new_v["w_in_odd"], "rs_scatter_late_wait")
    (g_in_even,) = _rs_share_halves(_rs_back(late_p4s, got3, list(_RS_LATE)), "rs_share_late")
    for n in early:
        t_ = lambda a: jnp.swapaxes(a, 1, 2)
        g_t = t_(g_in_even.reshape(w[n].shape))
        grads[n] = t_(g_t)
        delta[n], new_m[n], new_v[n] = (t_(o) for o in _adamw(t_(w[n]), g_t, t_(m[n]), t_(v[n]), f"adamw_{n}"))
    return (loss, dx[None], *[grads[n] for n in _WEIGHTS], *[delta[n] for n in _WEIGHTS],
            *[new_m[n] for n in _WEIGHTS], *[new_v[n] for n in _WEIGHTS])
```

```python
import functools

import jax
import jax.numpy as jnp
from jax import lax
from jax.experimental import pallas as pl
from jax.experimental.pallas import tpu as pltpu

F32 = jnp.float32
BF16 = jnp.bfloat16

D_MODEL = 1024
PL_DIM = 256
GDN_HEADS = 8
HEAD_DIM = 128
GDN_CHUNK = 64
HGRN_HEADS = 16
HGRN_CHUNK = 32
HGRN_WIDTH = 2048
DEEPNORM_ALPHA = 4.0 ** 0.25
NORM_EPS = 1e-5
ADAM_LR, ADAM_B1, ADAM_B2, ADAM_EPS, ADAM_WD, ADAM_STEP = 0.001, 0.9, 0.999, 1e-08, 0.01, 10

VMEM_LIMIT = 56 * 1024 * 1024
SUBLANES = 8
LANES = 128


def _params(*sem):
    return pltpu.CompilerParams(dimension_semantics=sem, vmem_limit_bytes=VMEM_LIMIT)


ONE_PASS, THREE_PASS, EXACT_LHS, EXACT_RHS = 0, 1, 2, 3


def _split3(v):
    hi = v.astype(BF16)
    r1 = v - hi.astype(F32)
    mid = r1.astype(BF16)
    return hi, mid, (r1 - mid.astype(F32)).astype(BF16)


def _mm_raw(a, b, kind, prec):
    nb = a.ndim - 2
    ca = a.ndim - 1 if kind[0] == "n" else a.ndim - 2
    cb = b.ndim - 2 if kind[1] == "n" else b.ndim - 1
    dims = (((ca,), (cb,)), (tuple(range(nb)),) * 2)
    dot = lambda p, q: lax.dot_general(p, q, dims, preferred_element_type=F32)
    ah, bh = a.astype(BF16), b.astype(BF16)
    if prec == ONE_PASS:
        return dot(ah, bh)
    if prec == EXACT_LHS:
        b1, b2, b3 = _split3(b)
        return dot(ah, b1) + (dot(ah, b2) + dot(ah, b3))
    if prec == EXACT_RHS:
        a1, a2, a3 = _split3(a)
        return dot(a1, bh) + (dot(a2, bh) + dot(a3, bh))
    al = (a - ah.astype(F32)).astype(BF16)
    bl = (b - bh.astype(F32)).astype(BF16)
    return dot(ah, bh) + (dot(ah, bl) + dot(al, bh))


@functools.partial(jax.custom_vjp, nondiff_argnums=(2, 3))
def _mm_vjp(a, b, kind, hi):
    return _mm_raw(a, b, kind, hi)


def _mm_vjp_fwd(a, b, kind, hi):
    return _mm_raw(a, b, kind, hi), (a, b)


def _mm_vjp_bwd(kind, hi, res, dc):
    a, b = res
    if hi in (EXACT_LHS, EXACT_RHS):
        assert kind == "nn"
        if hi == EXACT_LHS:
            return jnp.zeros_like(a), _mm_raw(a, dc, "tn", EXACT_LHS)
        return _mm_raw(dc, b, "nt", EXACT_RHS), jnp.zeros_like(b)
    if kind == "nn":
        return _mm_raw(dc, b, "nt", hi), _mm_raw(a, dc, "tn", hi)
    if kind == "nt":
        return _mm_raw(dc, b, "nn", hi), _mm_raw(dc, a, "tn", hi)
    return _mm_raw(b, dc, "nt", hi), _mm_raw(a, dc, "nn", hi)


_mm_vjp.defvjp(_mm_vjp_fwd, _mm_vjp_bwd)


def _lane_total(v):
    return jnp.broadcast_to(jnp.sum(v, axis=-1, keepdims=True), v.shape)


def _matmul(a, b, *, name, ta=False, tb=False, add=None, add_scale=1.0, tm=1024, tn=2048, tk=1024):
    m, k = (a.shape[1], a.shape[0]) if ta else a.shape
    n = b.shape[0] if tb else b.shape[1]
    tm, tn, tk = min(tm, m), min(tn, n), min(tk, k)
    tn = tn if n % tn == 0 else tn // 2
    assert m % tm == 0 and n % tn == 0 and k % tk == 0, (name, m, n, k)
    nk = k // tk
    dims = (((0 if ta else 1,), (1 if tb else 0,)), ((), ()))

    def body(*refs):
        a_ref, b_ref = refs[:2]
        o_ref = refs[-1]
        part = lax.dot_general(a_ref[...].astype(BF16), b_ref[...].astype(BF16), dims, preferred_element_type=F32)
        first = (lambda: part) if add is None else (lambda: part + add_scale * refs[2][...])
        if nk == 1:
            o_ref[...] = first()
        else:
            kk = pl.program_id(2)

            @pl.when(kk == 0)
            def _():
                o_ref[...] = first()

            @pl.when(kk > 0)
            def _():
                o_ref[...] += part

    a_spec = pl.BlockSpec((tk, tm), lambda i, j, kk: (kk, i)) if ta else pl.BlockSpec((tm, tk), lambda i, j, kk: (i, kk))
    b_spec = pl.BlockSpec((tn, tk), lambda i, j, kk: (j, kk)) if tb else pl.BlockSpec((tk, tn), lambda i, j, kk: (kk, j))
    o_spec = pl.BlockSpec((tm, tn), lambda i, j, kk: (i, j))
    in_specs = [a_spec, b_spec] + ([o_spec] if add is not None else [])
    args = (a, b) + ((add,) if add is not None else ())
    return pl.pallas_call(
        body, name=name, grid=(m // tm, n // tn, nk), in_specs=in_specs, out_specs=o_spec,
        out_shape=jax.ShapeDtypeStruct((m, n), F32),
        compiler_params=_params("parallel", "parallel", "arbitrary"))(*args)


HALO = SUBLANES


def _halo_specs(tt, width, col, nt):
    r = tt // HALO
    prev = pl.BlockSpec((HALO, width), lambda i: (jnp.maximum(i * r - 1, 0), col))
    nxt = pl.BlockSpec((HALO, width), lambda i: (jnp.minimum((i + 1) * r, nt * r - 1), col))
    return prev, nxt


def _shift_down(ext, k):
    return ext if k == 0 else pltpu.roll(ext, k, 0)


def _shift_up(ext, k):
    return ext if k == 0 else pltpu.roll(ext, ext.shape[0] - k, 0)


def _causal_conv(ext, w, taps):
    acc = None
    for j in range(taps):
        term = w[j:j + 1, :] * _shift_down(ext, taps - 1 - j)
        acc = term if acc is None else acc + term
    return acc[HALO:, :]


def _conv_a_fwd(proj_a, conv_w):
    t = proj_a.shape[0]
    tt = min(t, 256)
    nt = t // tt
    wdt = 1024

    def body(cur_ref, prev_ref, w_ref, y_ref):
        i = pl.program_id(0)
        cur = cur_ref[...]
        h, c, b, z = (cur[:, k * wdt:(k + 1) * wdt] for k in range(4))
        prev = prev_ref[...]
        u_prev = jnp.where(i > 0, prev[:, wdt:2 * wdt] * prev[:, 0:wdt], 0.0)
        ext = jnp.concatenate([u_prev, c * h], axis=0)
        conv = _causal_conv(ext, w_ref[...], 3)
        y_ref[...] = (b * conv * jax.nn.silu(z)).astype(BF16)

    prev_spec, _ = _halo_specs(tt, 4 * wdt, 0, nt)
    return pl.pallas_call(
        body, name="conv_a_fwd", grid=(nt,),
        in_specs=[pl.BlockSpec((tt, 4 * wdt), lambda i: (i, 0)), prev_spec, pl.BlockSpec((3, wdt), lambda i: (0, 0))],
        out_specs=pl.BlockSpec((tt, wdt), lambda i: (i, 0)),
        out_shape=jax.ShapeDtypeStruct((t, wdt), BF16), compiler_params=_params("parallel"))(proj_a, proj_a, conv_w)


def _conv_a_bwd(proj_a, conv_w, dy):
    t = proj_a.shape[0]
    tt = min(t, 256)
    nt = t // tt
    wdt = 1024

    def body(cur_ref, prev_ref, nxt_ref, w_ref, dy_ref, dyn_ref, d_ref, dw_ref):
        i = pl.program_id(0)
        w = w_ref[...]
        cur, prev, nxt = cur_ref[...], prev_ref[...], nxt_ref[...]
        split = lambda a: tuple(a[:, k * wdt:(k + 1) * wdt] for k in range(4))
        h, c, b, z = split(cur)
        hp, cp, _, _ = split(prev)
        hn, cn, bn, zn = split(nxt)
        u_prev = jnp.where(i > 0, cp * hp, 0.0)
        u_ext = jnp.concatenate([u_prev, c * h, cn * hn], axis=0)
        taps = [_shift_down(u_ext, 2 - j)[HALO:, :] for j in range(3)]
        conv = w[0:1, :] * taps[0] + w[1:2, :] * taps[1] + w[2:3, :] * taps[2]
        b_cn = jnp.concatenate([b, bn], axis=0)
        z_cn = jnp.concatenate([z, zn], axis=0)
        dy_cn = jnp.concatenate([dy_ref[...], jnp.where(i < nt - 1, dyn_ref[...], 0.0)], axis=0)
        sg = jax.nn.sigmoid(z_cn)
        silu = z_cn * sg
        d_conv = dy_cn * b_cn * silu
        db = (dy_cn * conv * silu)[:tt, :]
        dz = (dy_cn * b_cn * conv * (sg * (1.0 + z_cn * (1.0 - sg))))[:tt, :]
        du = None
        for j in range(3):
            term = w[j:j + 1, :] * _shift_up(d_conv, 2 - j)
            du = term if du is None else du + term
        du = du[:tt, :]
        d_ref[...] = jnp.concatenate([du * c, du * h, db, dz], axis=1).astype(BF16)

        @pl.when(i == 0)
        def _():
            dw_ref[...] = jnp.zeros_like(dw_ref)

        d_cur = d_conv[:tt, :]
        rows = [jnp.sum(d_cur * taps[j][:tt, :], axis=0, keepdims=True) for j in range(3)]
        dw_ref[0:3, :] += jnp.concatenate(rows, axis=0)

    prev_spec, nxt_spec = _halo_specs(tt, 4 * wdt, 0, nt)
    _, dyn_spec = _halo_specs(tt, wdt, 0, nt)
    return pl.pallas_call(
        body, name="conv_a_bwd", grid=(nt,),
        in_specs=[pl.BlockSpec((tt, 4 * wdt), lambda i: (i, 0)), prev_spec, nxt_spec,
                  pl.BlockSpec((3, wdt), lambda i: (0, 0)), pl.BlockSpec((tt, wdt), lambda i: (i, 0)), dyn_spec],
        out_specs=[pl.BlockSpec((tt, 4 * wdt), lambda i: (i, 0)), pl.BlockSpec((SUBLANES, wdt), lambda i: (0, 0))],
        out_shape=[jax.ShapeDtypeStruct((t, 4 * wdt), BF16), jax.ShapeDtypeStruct((SUBLANES, wdt), F32)],
        compiler_params=_params("arbitrary"))(proj_a, proj_a, proj_a, conv_w, dy, dy)


def _conv_b_fwd(proj_qkv, conv_w):
    t, width = proj_qkv.shape
    tt = min(t, 256)
    nt = t // tt
    wdt = 1024

    def body(cur_ref, prev_ref, w_ref, y_ref):
        i = pl.program_id(1)
        ext = jnp.concatenate([jnp.where(i > 0, prev_ref[...], 0.0), cur_ref[...]], axis=0)
        y_ref[...] = jax.nn.silu(_causal_conv(ext, w_ref[...], 4))

    r = tt // HALO
    return pl.pallas_call(
        body, name="conv_b_fwd", grid=(width // wdt, nt),
        in_specs=[pl.BlockSpec((tt, wdt), lambda j, i: (i, j)),
                  pl.BlockSpec((HALO, wdt), lambda j, i: (jnp.maximum(i * r - 1, 0), j)),
                  pl.BlockSpec((4, wdt), lambda j, i: (0, j))],
        out_specs=pl.BlockSpec((tt, wdt), lambda j, i: (i, j)),
        out_shape=jax.ShapeDtypeStruct((t, width), F32), compiler_params=_params("parallel", "parallel"))(
            proj_qkv, proj_qkv, conv_w)


def _conv_b_bwd(proj_qkv, conv_w, d_act, col, name):
    t = proj_qkv.shape[0]
    tt = min(t, 256)
    nt = t // tt
    wdt = 1024

    def body(cur_ref, prev_ref, nxt_ref, w_ref, da_ref, dan_ref, d_ref, dw_ref):
        i = pl.program_id(0)
        w = w_ref[...]
        u_ext = jnp.concatenate([jnp.where(i > 0, prev_ref[...], 0.0), cur_ref[...], nxt_ref[...]], axis=0)
        taps = [_shift_down(u_ext, 3 - j)[HALO:, :] for j in range(4)]
        conv = w[0:1, :] * taps[0] + w[1:2, :] * taps[1] + w[2:3, :] * taps[2] + w[3:4, :] * taps[3]
        da_cn = jnp.concatenate([da_ref[...], jnp.where(i < nt - 1, dan_ref[...], 0.0)], axis=0)
        sg = jax.nn.sigmoid(conv)
        d_conv = da_cn * (sg * (1.0 + conv * (1.0 - sg)))
        du = None
        for j in range(4):
            term = w[j:j + 1, :] * _shift_up(d_conv, 3 - j)
            du = term if du is None else du + term
        d_ref[...] = du[:tt, :].astype(BF16)

        @pl.when(i == 0)
        def _():
            dw_ref[...] = jnp.zeros_like(dw_ref)

        d_cur = d_conv[:tt, :]
        rows = [jnp.sum(d_cur * taps[j][:tt, :], axis=0, keepdims=True) for j in range(4)]
        dw_ref[0:4, :] += jnp.concatenate(rows, axis=0)

    prev_spec, nxt_spec = _halo_specs(tt, wdt, col, nt)
    _, dan_spec = _halo_specs(tt, wdt, 0, nt)
    return pl.pallas_call(
        body, name=name, grid=(nt,),
        in_specs=[pl.BlockSpec((tt, wdt), lambda i: (i, col)), prev_spec, nxt_spec,
                  pl.BlockSpec((4, wdt), lambda i: (0, col)), pl.BlockSpec((tt, wdt), lambda i: (i, 0)), dan_spec],
        out_specs=[pl.BlockSpec((tt, wdt), lambda i: (i, 0)), pl.BlockSpec((SUBLANES, wdt), lambda i: (0, 0))],
        out_shape=[jax.ShapeDtypeStruct((t, wdt), BF16), jax.ShapeDtypeStruct((SUBLANES, wdt), F32)],
        compiler_params=_params("arbitrary"))(proj_qkv, proj_qkv, proj_qkv, conv_w, d_act, d_act)


def _rms_gate(o, gn, z):
    on = o * lax.rsqrt(jnp.mean(o * o, axis=-1, keepdims=True) + NORM_EPS) * gn
    return on * jax.nn.silu(z)


GDN_PREP_ROWS = 1024


def _unit_lower_inverse(low):
    c = low.shape[-1]
    eye = lax.broadcasted_iota(jnp.int32, low.shape, low.ndim - 2) == lax.broadcasted_iota(jnp.int32, low.shape, low.ndim - 1)
    x = -low
    inv = eye.astype(F32) + x
    for _ in range(c.bit_length() - 2):
        x = _mm_raw(x, x, "nn", THREE_PASS)
        inv = inv + _mm_raw(inv, x, "nn", THREE_PASS)
    return inv


@jax.custom_vjp
def _known_inverse(low, inv):
    return inv


def _known_inverse_fwd(low, inv):
    return inv, inv


def _known_inverse_bwd(inv, d_inv):
    return -_mm_raw(_mm_raw(inv, d_inv, "tn", THREE_PASS), inv, "nt", THREE_PASS), jnp.zeros_like(inv)


_known_inverse.defvjp(_known_inverse_fwd, _known_inverse_bwd)


def _gdn_prep(mm, qa, ka, va, braw, araw, alog, dtb, inv_kept=None):
    n, c, _ = qa.shape
    q = qa * lax.rsqrt(jnp.sum(qa * qa, axis=-1, keepdims=True) + 1e-6) * (HEAD_DIM ** -0.5)
    k = ka * lax.rsqrt(jnp.sum(ka * ka, axis=-1, keepdims=True) + 1e-6)
    beta = jax.nn.sigmoid(braw)
    g = -jnp.exp(alog) * jax.nn.softplus(araw + dtb)
    ri = lax.broadcasted_iota(jnp.int32, (n, c, c), 1)
    ci = lax.broadcasted_iota(jnp.int32, (n, c, c), 2)
    incl, strict, eye = ri >= ci, ri > ci, ri == ci
    gc = mm(incl.astype(F32), g, "nn", EXACT_LHS)
    gc_i = gc[:, :, :c]
    gc_j = mm(jnp.ones((n, c, c), F32), jnp.where(eye, gc_i, 0.0), "nn", EXACT_LHS)
    decay = jnp.where(incl, jnp.exp(jnp.where(incl, gc_i - gc_j, 0.0)), 0.0)
    kb = k * beta
    low = jnp.where(strict, mm(kb, k, "nt", ONE_PASS) * decay, 0.0)
    inv = _unit_lower_inverse(low) if inv_kept is None else _known_inverse(low, inv_kept)
    egc = jnp.exp(gc)
    u = mm(inv, va * beta, "nn", THREE_PASS)
    w = mm(inv, kb * egc, "nn", THREE_PASS)
    attn = jnp.where(incl, mm(q, k, "nt", ONE_PASS) * decay, 0.0)
    g_last = jnp.sum(g, axis=1, keepdims=True)
    outs = (u, w, q * egc, k * jnp.exp(g_last - gc), attn, jnp.exp(g_last))
    return outs + (inv,) if inv_kept is None else outs


def _gdn_scan(mm, u, w, qd, kd, attn, egl, z, gn, state):
    v_new = u - mm(w, state, "nn", ONE_PASS)
    o = mm(qd, state, "nn", ONE_PASS) + mm(attn, v_new, "nn", ONE_PASS)
    new_state = state * egl + mm(kd, v_new, "tn", ONE_PASS)
    return _rms_gate(o, gn, z), new_state


def _chunks(ref_value, n, c):
    return ref_value.reshape(n, c, ref_value.shape[-1])


def _by_head(ref, rows, heads):
    return jnp.stack([ref[rows, pl.ds(h * HEAD_DIM, HEAD_DIM)] for h in range(heads)])


def _store_heads(ref, rows, value):
    for h in range(value.shape[0]):
        ref[rows, pl.ds(h * HEAD_DIM, HEAD_DIM)] = value[h]


def _gdn_prep_specs(tb):
    col = lambda off: pl.BlockSpec((tb, HEAD_DIM), lambda h, i: (i, off + h))
    rep = pl.BlockSpec((1, tb, LANES), lambda h, i: (h, i, 0))
    par = pl.BlockSpec((1, SUBLANES, LANES), lambda h, i: (h, 0, 0))
    att = pl.BlockSpec((1, tb, GDN_CHUNK), lambda h, i: (h, i, 0))
    egl = pl.BlockSpec((1, tb // GDN_CHUNK, SUBLANES, LANES), lambda h, i: (h, i, 0, 0))
    return col, rep, par, att, egl


def _gdn_prep_fwd(qkv_act, braw, araw, alog, dtb):
    t = qkv_act.shape[0]
    tb = min(t, 2 * GDN_PREP_ROWS)
    nt, nc = t // tb, tb // GDN_CHUNK
    width = GDN_HEADS * HEAD_DIM

    def body(q_ref, k_ref, v_ref, br_ref, ar_ref, al_ref, dt_ref, u_ref, w_ref, qd_ref, kd_ref, at_ref, eg_ref, inv_ref):
        ch = lambda r: _chunks(r, nc, GDN_CHUNK)
        u, w, qd, kd, attn, egl, inv = _gdn_prep(_mm_raw, ch(q_ref[...]), ch(k_ref[...]), ch(v_ref[...]), ch(br_ref[0]),
                                                 ch(ar_ref[0]), al_ref[0, 0:1, :], dt_ref[0, 0:1, :])
        u_ref[...] = u.reshape(tb, HEAD_DIM)
        w_ref[...] = w.reshape(tb, HEAD_DIM).astype(BF16)
        qd_ref[...] = qd.reshape(tb, HEAD_DIM).astype(BF16)
        kd_ref[...] = kd.reshape(tb, HEAD_DIM).astype(BF16)
        at_ref[0] = attn.reshape(tb, GDN_CHUNK).astype(BF16)
        eg_ref[0] = jnp.broadcast_to(egl, (nc, SUBLANES, LANES))
        inv_ref[0] = inv.reshape(tb, GDN_CHUNK)

    col, rep, par, att, egl = _gdn_prep_specs(tb)
    h = GDN_HEADS
    return pl.pallas_call(
        body, name="gdn_prep_fwd", grid=(h, nt),
        in_specs=[col(0), col(h), col(2 * h), rep, rep, par, par],
        out_specs=[col(0), col(0), col(0), col(0), att, egl, att],
        out_shape=[jax.ShapeDtypeStruct((t, width), F32)] + [jax.ShapeDtypeStruct((t, width), BF16)] * 3
        + [jax.ShapeDtypeStruct((h, t, GDN_CHUNK), BF16), jax.ShapeDtypeStruct((h, t // GDN_CHUNK, SUBLANES, LANES), F32),
           jax.ShapeDtypeStruct((h, t, GDN_CHUNK), F32)],
        compiler_params=_params("parallel", "parallel"))(qkv_act, qkv_act, qkv_act, braw, araw, alog, dtb)


def _gdn_prep_bwd(qkv_act, braw, araw, alog, dtb, inv, du, dw, dqd, dkd, dattn, degl):
    t = qkv_act.shape[0]
    tb = min(t, GDN_PREP_ROWS)
    nt, nc = t // tb, tb // GDN_CHUNK
    width = GDN_HEADS * HEAD_DIM

    def body(q_ref, k_ref, v_ref, br_ref, ar_ref, al_ref, dt_ref, inv_ref, du_ref, dw_ref, dqd_ref, dkd_ref, dat_ref,
             deg_ref, dq_ref, dk_ref, dv_ref, dbr_ref, dar_ref, dal_ref, ddt_ref):
        @pl.when(pl.program_id(1) == 0)
        def _():
            dal_ref[...] = jnp.zeros_like(dal_ref)
            ddt_ref[...] = jnp.zeros_like(ddt_ref)

        ch = lambda r: _chunks(r, nc, GDN_CHUNK)
        _, vjp = jax.vjp(functools.partial(_gdn_prep, _mm_vjp, inv_kept=ch(inv_ref[0])), ch(q_ref[...]), ch(k_ref[...]),
                         ch(v_ref[...]), ch(br_ref[0]), ch(ar_ref[0]), al_ref[0, 0:1, :], dt_ref[0, 0:1, :])
        dq, dk, dv, dbr, dar, dal, ddt = vjp((ch(du_ref[...]), ch(dw_ref[...]), ch(dqd_ref[...]), ch(dkd_ref[...]),
                                              ch(dat_ref[0]), deg_ref[0][:, 0:1, :]))
        dq_ref[...] = dq.reshape(tb, HEAD_DIM)
        dk_ref[...] = dk.reshape(tb, HEAD_DIM)
        dv_ref[...] = dv.reshape(tb, HEAD_DIM)
        dbr_ref[0] = _lane_total(dbr.reshape(tb, LANES))
        dar_ref[0] = _lane_total(dar.reshape(tb, LANES))
        dal_ref[0, 0:1, :] += _lane_total(dal)
        ddt_ref[0, 0:1, :] += _lane_total(ddt)

    col, rep, par, att, egl = _gdn_prep_specs(tb)
    h = GDN_HEADS
    return pl.pallas_call(
        body, name="gdn_prep_bwd", grid=(h, nt),
        in_specs=[col(0), col(h), col(2 * h), rep, rep, par, par, att, col(0), col(0), col(0), col(0), att, egl],
        out_specs=[col(0), col(0), col(0), rep, rep, par, par],
        out_shape=[jax.ShapeDtypeStruct((t, width), F32)] * 3 + [jax.ShapeDtypeStruct((h, t, LANES), F32)] * 2
        + [jax.ShapeDtypeStruct((h, SUBLANES, LANES), F32)] * 2,
        compiler_params=_params("parallel", "arbitrary"))(qkv_act, qkv_act, qkv_act, braw, araw, alog, dtb, inv,
                                                         du, dw, dqd, dkd, dattn, degl)


def _scan_specs(tb, heads, chunk, rev, nt):
    ti = (lambda i: nt - 1 - i) if rev else (lambda i: i)
    row = pl.BlockSpec((tb, heads * HEAD_DIM), lambda i: (ti(i), 0))
    att = pl.BlockSpec((heads, tb, chunk), lambda i: (0, ti(i), 0))
    egl = pl.BlockSpec((heads, tb // chunk, SUBLANES, LANES), lambda i: (0, ti(i), 0, 0))
    hist = pl.BlockSpec((heads, tb // chunk, HEAD_DIM, HEAD_DIM), lambda i: (0, ti(i), 0, 0))
    gn = pl.BlockSpec((SUBLANES, LANES), lambda i: (0, 0))
    return row, att, egl, hist, gn


def _gdn_scan_fwd(u, w, qd, kd, attn, egl, zb, gn):
    t = u.shape[0]
    tb = min(t, 256)
    nt, nc = t // tb, tb // GDN_CHUNK
    nh = GDN_HEADS

    def body(u_ref, w_ref, qd_ref, kd_ref, at_ref, eg_ref, z_ref, gn_ref, y_ref, hist_ref, s_ref):
        @pl.when(pl.program_id(0) == 0)
        def _():
            s_ref[...] = jnp.zeros_like(s_ref)

        g = gn_ref[0:1, :]
        state = s_ref[...]
        for c in range(nc):
            rows = pl.ds(c * GDN_CHUNK, GDN_CHUNK)
            heads = lambda r: _by_head(r, rows, nh)
            hist_ref[:, c] = state
            y, state = _gdn_scan(_mm_raw, heads(u_ref), heads(w_ref), heads(qd_ref), heads(kd_ref), at_ref[:, rows, :],
                                 eg_ref[:, c, 0:1, :], heads(z_ref), g, state)
            _store_heads(y_ref, rows, y.astype(BF16))
        s_ref[...] = state

    row, att, egs, hist, gns = _scan_specs(tb, nh, GDN_CHUNK, False, nt)
    return pl.pallas_call(
        body, name="gdn_scan_fwd", grid=(nt,), in_specs=[row, row, row, row, att, egs, row, gns], out_specs=[row, hist],
        out_shape=[jax.ShapeDtypeStruct((t, nh * HEAD_DIM), BF16),
                   jax.ShapeDtypeStruct((nh, t // GDN_CHUNK, HEAD_DIM, HEAD_DIM), F32)],
        scratch_shapes=[pltpu.VMEM((nh, HEAD_DIM, HEAD_DIM), F32)],
        compiler_params=_params("arbitrary"))(u, w, qd, kd, attn, egl, zb, gn)


def _gdn_scan_bwd(u, w, qd, kd, attn, egl, zb, gn, hist, dy):
    t = u.shape[0]
    tb = min(t, 256)
    nt, nc = t // tb, tb // GDN_CHUNK
    nh = GDN_HEADS

    def body(u_ref, w_ref, qd_ref, kd_ref, at_ref, eg_ref, z_ref, gn_ref, hist_ref, dy_ref,
             du_ref, dw_ref, dqd_ref, dkd_ref, dat_ref, deg_ref, dz_ref, dgn_ref, ds_ref):
        @pl.when(pl.program_id(0) == 0)
        def _():
            ds_ref[...] = jnp.zeros_like(ds_ref)
            dgn_ref[...] = jnp.zeros_like(dgn_ref)

        g = gn_ref[0:1, :]
        d_state = ds_ref[...]
        for c in reversed(range(nc)):
            rows = pl.ds(c * GDN_CHUNK, GDN_CHUNK)
            heads = lambda r: _by_head(r, rows, nh).astype(F32)
            _, vjp = jax.vjp(functools.partial(_gdn_scan, _mm_vjp), heads(u_ref), heads(w_ref), heads(qd_ref),
                             heads(kd_ref), at_ref[:, rows, :].astype(F32), eg_ref[:, c, 0:1, :], heads(z_ref), g,
                             hist_ref[:, c])
            du, dw, dqd, dkd, dat, deg, dz, dgn, d_state = vjp((heads(dy_ref), d_state))
            _store_heads(du_ref, rows, du)
            _store_heads(dw_ref, rows, dw)
            _store_heads(dqd_ref, rows, dqd)
            _store_heads(dkd_ref, rows, dkd)
            dat_ref[:, rows, :] = dat
            deg_ref[:, c] = jnp.broadcast_to(deg, (nh, SUBLANES, LANES))
            _store_heads(dz_ref, rows, dz.astype(BF16))
            dgn_ref[0:1, :] += dgn
        ds_ref[...] = d_state

    row, att, egs, hists, gns = _scan_specs(tb, nh, GDN_CHUNK, True, nt)
    wide = jax.ShapeDtypeStruct((t, nh * HEAD_DIM), F32)
    return pl.pallas_call(
        body, name="gdn_scan_bwd", grid=(nt,),
        in_specs=[row, row, row, row, att, egs, row, gns, hists, row],
        out_specs=[row, row, row, row, att, egs, row, gns],
        out_shape=[wide] * 4 + [jax.ShapeDtypeStruct((nh, t, GDN_CHUNK), F32),
                                jax.ShapeDtypeStruct((nh, t // GDN_CHUNK, SUBLANES, LANES), F32),
                                jax.ShapeDtypeStruct((t, nh * HEAD_DIM), BF16),
                                jax.ShapeDtypeStruct((SUBLANES, LANES), F32)],
        scratch_shapes=[pltpu.VMEM((nh, HEAD_DIM, HEAD_DIM), F32)],
        compiler_params=_params("arbitrary"))(u, w, qd, kd, attn, egl, zb, gn, hist, dy)


def _hgrn_prep(mm, qr, fr, lbl):
    n, c, _ = qr.shape
    lb = jax.nn.sigmoid(lbl[1:2, :] - lbl[0:1, :])
    f = lb + (1.0 - lb) * jax.nn.sigmoid(fr)
    q = jax.nn.silu(qr)
    k = 1.0 - f
    logf = jnp.log(f)
    ri = lax.broadcasted_iota(jnp.int32, (n, c, c), 1)
    ci = lax.broadcasted_iota(jnp.int32, (n, c, c), 2)
    b = mm((ri >= ci).astype(F32), logf, "nn", EXACT_LHS)
    attn = _hgrn_attn(mm, q, k, b)
    b_last = jnp.sum(logf, axis=1, keepdims=True)
    return q * jnp.exp(b), k * jnp.exp(b_last - b), attn, jnp.exp(b_last)


HGRN_SUB = 8
HGRN_PREP_ROWS = 2048


@functools.partial(jax.custom_vjp, nondiff_argnums=(1,))
def _roll_rows(x, shift):
    return pltpu.roll(x, shift, x.ndim - 2)


def _roll_rows_fwd(x, shift):
    return _roll_rows(x, shift), None


def _roll_rows_bwd(shift, _, d):
    return (pltpu.roll(d, d.shape[-2] - shift, d.ndim - 2),)


_roll_rows.defvjp(_roll_rows_fwd, _roll_rows_bwd)


@jax.custom_vjp
def _exp_clamped(v):
    return jnp.exp(jnp.minimum(v, 0.0))


def _exp_clamped_fwd(v):
    out = jnp.exp(jnp.minimum(v, 0.0))
    return out, out


def _exp_clamped_bwd(out, d):
    return (d * out,)


_exp_clamped.defvjp(_exp_clamped_fwd, _exp_clamped_bwd)


def _hgrn_attn(mm, q, k, b):
    n, c, d = q.shape
    sb = HGRN_SUB
    sub = lambda a: a.reshape(n * c // sb, sb, d)
    qs, ks, bs = sub(q), sub(k), sub(b)
    row = lax.broadcasted_iota(jnp.int32, (n, c, c), 1)
    col = lax.broadcasted_iota(jnp.int32, (n, c, c), 2)
    same_block = (row & -sb) == (col & -sb)
    attn = None
    for delta in range(sb):
        if delta == 0:
            prod = qs * ks
        else:
            prod = qs * _roll_rows(ks, delta) * _exp_clamped(bs - _roll_rows(bs, delta))
        sums = jnp.sum(prod, axis=-1, keepdims=True).reshape(n, c, 1)
        term = jnp.where(same_block & (row - col == delta), sums, 0.0)
        attn = term if attn is None else attn + term
    far = [jnp.zeros((n, sb, c), F32)]
    col8 = lax.broadcasted_iota(jnp.int32, (n, sb, c), 2)
    for i in range(1, c // sb):
        r0 = i * sb
        bi = b[:, r0:r0 + sb, :]
        ref = bi[:, 0:1, :]
        part = mm(q[:, r0:r0 + sb, :] * jnp.exp(bi - ref), k * _exp_clamped(ref - b), "nt", ONE_PASS)
        far.append(jnp.where(col8 < r0, part, 0.0))
    return attn + jnp.concatenate(far, axis=1)


def _hgrn_scan(mm, qe, kd, attn, ebl, iv, z, gn, state):
    o = mm(qe, state, "nt", ONE_PASS) + mm(attn, iv, "nn", ONE_PASS)
    new_state = state * ebl + mm(iv, kd, "tn", ONE_PASS)
    return _rms_gate(o, gn, z), new_state


def _hgrn_prep_specs(tb):
    col = pl.BlockSpec((tb, HEAD_DIM), lambda h, i: (i, h))
    lbs = pl.BlockSpec((2, HEAD_DIM), lambda h, i: (0, h))
    att = pl.BlockSpec((1, tb, HGRN_CHUNK), lambda h, i: (h, i, 0))
    ebl = pl.BlockSpec((1, tb // HGRN_CHUNK, SUBLANES, LANES), lambda h, i: (h, i, 0, 0))
    return col, lbs, att, ebl


def _hgrn_prep_fwd(qr, fr, lower_bounds):
    t = qr.shape[0]
    tb = min(t, HGRN_PREP_ROWS)
    nt, nc = t // tb, tb // HGRN_CHUNK
    hh = HGRN_HEADS

    def body(q_ref, f_ref, lb_ref, qe_ref, kd_ref, at_ref, eb_ref):
        ch = lambda r: _chunks(r, nc, HGRN_CHUNK)
        qe, kd, attn, ebl = _hgrn_prep(_mm_raw, ch(q_ref[...]), ch(f_ref[...]), lb_ref[...])
        qe_ref[...] = qe.reshape(tb, HEAD_DIM).astype(BF16)
        kd_ref[...] = kd.reshape(tb, HEAD_DIM).astype(BF16)
        at_ref[0] = attn.reshape(tb, HGRN_CHUNK).astype(BF16)
        eb_ref[0] = jnp.broadcast_to(ebl, (nc, SUBLANES, LANES))

    col, lbs, att, ebs = _hgrn_prep_specs(tb)
    return pl.pallas_call(
        body, name="hgrn_prep_fwd", grid=(hh, nt), in_specs=[col, col, lbs], out_specs=[col, col, att, ebs],
        out_shape=[jax.ShapeDtypeStruct((t, HGRN_WIDTH), BF16)] * 2
        + [jax.ShapeDtypeStruct((hh, t, HGRN_CHUNK), BF16), jax.ShapeDtypeStruct((hh, t // HGRN_CHUNK, SUBLANES, LANES), F32)],
        compiler_params=_params("parallel", "parallel"))(qr, fr, lower_bounds)


def _hgrn_prep_bwd(qr, fr, lower_bounds, dqe, dkd, dattn, debl):
    t = qr.shape[0]
    tb = min(t, HGRN_PREP_ROWS)
    nt, nc = t // tb, tb // HGRN_CHUNK
    hh = HGRN_HEADS

    def body(q_ref, f_ref, lb_ref, dqe_ref, dkd_ref, dat_ref, deb_ref, dq_ref, df_ref, dlb_ref):
        @pl.when(pl.program_id(1) == 0)
        def _():
            dlb_ref[...] = jnp.zeros_like(dlb_ref)

        ch = lambda r: _chunks(r, nc, HGRN_CHUNK)
        _, vjp = jax.vjp(functools.partial(_hgrn_prep, _mm_vjp), ch(q_ref[...]), ch(f_ref[...]), lb_ref[...])
        dq, df, dlb = vjp((ch(dqe_ref[...]), ch(dkd_ref[...]), ch(dat_ref[0]), deb_ref[0][:, 0:1, :]))
        dq_ref[...] = dq.reshape(tb, HEAD_DIM).astype(BF16)
        df_ref[...] = df.reshape(tb, HEAD_DIM).astype(BF16)
        dlb_ref[...] += dlb

    col, lbs, att, ebs = _hgrn_prep_specs(tb)
    return pl.pallas_call(
        body, name="hgrn_prep_bwd", grid=(hh, nt), in_specs=[col, col, lbs, col, col, att, ebs],
        out_specs=[col, col, lbs],
        out_shape=[jax.ShapeDtypeStruct((t, HGRN_WIDTH), BF16)] * 2 + [jax.ShapeDtypeStruct((2, HGRN_WIDTH), F32)],
        compiler_params=_params("parallel", "arbitrary"))(qr, fr, lower_bounds, dqe, dkd, dattn, debl)


def _hgrn_scan_fwd(qe, kd, attn, ebl, iv, z, gn):
    t = qe.shape[0]
    tb = min(t, 128)
    nt, nc = t // tb, tb // HGRN_CHUNK
    hh = HGRN_HEADS

    def body(qe_ref, kd_ref, at_ref, eb_ref, i_ref, z_ref, gn_ref, y_ref, hist_ref, s_ref):
        @pl.when(pl.program_id(0) == 0)
        def _():
            s_ref[...] = jnp.zeros_like(s_ref)

        g = gn_ref[0:1, :]
        state = s_ref[...]
        for c in range(nc):
            rows = pl.ds(c * HGRN_CHUNK, HGRN_CHUNK)
            heads = lambda r: _by_head(r, rows, hh)
            hist_ref[:, c] = state
            y, state = _hgrn_scan(_mm_raw, heads(qe_ref), heads(kd_ref), at_ref[:, rows, :], eb_ref[:, c, 0:1, :],
                                  heads(i_ref), heads(z_ref), g, state)
            _store_heads(y_ref, rows, y.astype(BF16))
        s_ref[...] = state

    row, att, ebs, hist, gns = _scan_specs(tb, hh, HGRN_CHUNK, False, nt)
    return pl.pallas_call(
        body, name="hgrn_scan_fwd", grid=(nt,), in_specs=[row, row, att, ebs, row, row, gns], out_specs=[row, hist],
        out_shape=[jax.ShapeDtypeStruct((t, HGRN_WIDTH), BF16),
                   jax.ShapeDtypeStruct((hh, t // HGRN_CHUNK, HEAD_DIM, HEAD_DIM), F32)],
        scratch_shapes=[pltpu.VMEM((hh, HEAD_DIM, HEAD_DIM), F32)],
        compiler_params=_params("arbitrary"))(qe, kd, attn, ebl, iv, z, gn)


def _hgrn_scan_bwd(qe, kd, attn, ebl, iv, z, gn, hist, dy):
    t = qe.shape[0]
    tb = min(t, 128)
    nt, nc = t // tb, tb // HGRN_CHUNK
    hh = HGRN_HEADS

    def body(qe_ref, kd_ref, at_ref, eb_ref, i_ref, z_ref, gn_ref, hist_ref, dy_ref,
             dqe_ref, dkd_ref, dat_ref, deb_ref, di_ref, dz_ref, dgn_ref, ds_ref):
        @pl.when(pl.program_id(0) == 0)
        def _():
            ds_ref[...] = jnp.zeros_like(ds_ref)
            dgn_ref[...] = jnp.zeros_like(dgn_ref)

        g = gn_ref[0:1, :]
        d_state = ds_ref[...]
        for c in reversed(range(nc)):
            rows = pl.ds(c * HGRN_CHUNK, HGRN_CHUNK)
            heads = lambda r: _by_head(r, rows, hh).astype(F32)
            _, vjp = jax.vjp(functools.partial(_hgrn_scan, _mm_vjp), heads(qe_ref), heads(kd_ref),
                             at_ref[:, rows, :].astype(F32), eb_ref[:, c, 0:1, :], heads(i_ref), heads(z_ref), g,
                             hist_ref[:, c])
            dqe, dkd, dat, deb, di, dz, dgn, d_state = vjp((heads(dy_ref), d_state))
            _store_heads(dqe_ref, rows, dqe)
            _store_heads(dkd_ref, rows, dkd)
            dat_ref[:, rows, :] = dat
            deb_ref[:, c] = jnp.broadcast_to(deb, (hh, SUBLANES, LANES))
            _store_heads(di_ref, rows, di.astype(BF16))
            _store_heads(dz_ref, rows, dz.astype(BF16))
            dgn_ref[0:1, :] += dgn
        ds_ref[...] = d_state

    row, att, ebs, hists, gns = _scan_specs(tb, hh, HGRN_CHUNK, True, nt)
    wide = lambda dt: jax.ShapeDtypeStruct((t, HGRN_WIDTH), dt)
    return pl.pallas_call(
        body, name="hgrn_scan_bwd", grid=(nt,),
        in_specs=[row, row, att, ebs, row, row, gns, hists, row],
        out_specs=[row, row, att, ebs, row, row, gns],
        out_shape=[wide(F32), wide(F32), jax.ShapeDtypeStruct((hh, t, HGRN_CHUNK), F32),
                   jax.ShapeDtypeStruct((hh, t // HGRN_CHUNK, SUBLANES, LANES), F32), wide(BF16), wide(BF16),
                   jax.ShapeDtypeStruct((SUBLANES, LANES), F32)],
        scratch_shapes=[pltpu.VMEM((hh, HEAD_DIM, HEAD_DIM), F32)],
        compiler_params=_params("arbitrary"))(qe, kd, attn, ebl, iv, z, gn, hist, dy)


def _layer_norm(pre, g, b):
    mu = jnp.mean(pre, axis=-1, keepdims=True)
    d = pre - mu
    var = jnp.mean(d * d, axis=-1, keepdims=True)
    return d * lax.rsqrt(var + NORM_EPS) * g + b


def _lnpl_fwd(xin, s, p, wg, wpl, ln_g, ln_b):
    t = xin.shape[0]
    tt = min(t, 512)

    def body(x_ref, s_ref, p_ref, wg_ref, wpl_ref, g_ref, b_ref, o_ref, ob_ref):
        xn = _layer_norm(DEEPNORM_ALPHA * x_ref[...] + s_ref[...], g_ref[...], b_ref[...])
        gate = jax.nn.sigmoid(_mm_raw(xn, wg_ref[...], "nn", False))
        out = xn + _mm_raw(p_ref[...], wpl_ref[...], "nn", False) * gate
        o_ref[...] = out
        ob_ref[...] = out.astype(BF16)

    row = lambda w: pl.BlockSpec((tt, w), lambda i: (i, 0))
    full = lambda a: pl.BlockSpec(a.shape, lambda i: (0, 0))
    return pl.pallas_call(
        body, name="lnpl_fwd", grid=(t // tt,),
        in_specs=[row(D_MODEL), row(D_MODEL), row(PL_DIM), full(wg), full(wpl), full(ln_g), full(ln_b)],
        out_specs=[row(D_MODEL), row(D_MODEL)],
        out_shape=[jax.ShapeDtypeStruct((t, D_MODEL), F32), jax.ShapeDtypeStruct((t, D_MODEL), BF16)],
        compiler_params=_params("parallel"))(xin, s, p, wg, wpl, ln_g, ln_b)


def _lnpl_bwd(xin, s, p, wg, wpl, ln_g, ln_b, upstream, last, name):
    t = xin.shape[0]
    tt = min(t, 512)

    def body(x_ref, s_ref, p_ref, wg_ref, wpl_ref, g_ref, b_ref, up_ref,
             dpre_ref, dwg_ref, dwpl_ref, dg_ref, db_ref, loss_ref):
        @pl.when(pl.program_id(0) == 0)
        def _():
            for r in (dwg_ref, dwpl_ref, dg_ref, db_ref, loss_ref):
                r[...] = jnp.zeros_like(r)

        pre = DEEPNORM_ALPHA * x_ref[...] + s_ref[...]
        xn, ln_vjp = jax.vjp(_layer_norm, pre, g_ref[...], b_ref[...])
        gate = jax.nn.sigmoid(_mm_raw(xn, wg_ref[...], "nn", False))
        plv = _mm_raw(p_ref[...], wpl_ref[...], "nn", False)
        if last:
            err = xn + plv * gate - up_ref[...]
            dout = err * (1.0 / D_MODEL)
            tot = jnp.sum(jnp.sum(err * err, axis=1, keepdims=True), axis=0, keepdims=True) * (0.5 / D_MODEL)
            loss_ref[...] += jnp.broadcast_to(tot, loss_ref.shape)
        else:
            dout = up_ref[...]
        dplv = dout * gate
        dlogits = dout * plv * gate * (1.0 - gate)
        dwg_ref[...] += _mm_raw(xn, dlogits, "tn", False)
        dwpl_ref[...] += _mm_raw(p_ref[...], dplv, "tn", False)
        dxn = dout + _mm_raw(dlogits, wg_ref[...], "nt", False)
        dpre, dg, db = ln_vjp(dxn)
        dpre_ref[...] = dpre
        dg_ref[...] += dg
        db_ref[...] += db

    row = lambda w: pl.BlockSpec((tt, w), lambda i: (i, 0))
    full = lambda shape: pl.BlockSpec(shape, lambda i: (0, 0))
    return pl.pallas_call(
        body, name=name, grid=(t // tt,),
        in_specs=[row(D_MODEL), row(D_MODEL), row(PL_DIM), full(wg.shape), full(wpl.shape), full(ln_g.shape),
                  full(ln_b.shape), row(D_MODEL)],
        out_specs=[row(D_MODEL), full(wg.shape), full(wpl.shape), full(ln_g.shape), full(ln_b.shape),
                   full((SUBLANES, LANES))],
        out_shape=[jax.ShapeDtypeStruct((t, D_MODEL), F32), jax.ShapeDtypeStruct(wg.shape, F32),
                   jax.ShapeDtypeStruct(wpl.shape, F32), jax.ShapeDtypeStruct(ln_g.shape, F32),
                   jax.ShapeDtypeStruct(ln_b.shape, F32), jax.ShapeDtypeStruct((SUBLANES, LANES), F32)],
        compiler_params=_params("arbitrary"))(xin, s, p, wg, wpl, ln_g, ln_b, upstream)


def _pack_tail(dbr, dar):
    nh, t, _ = dbr.shape
    tt = min(t, 512)

    def body(b_ref, a_ref, o_ref):
        lane = lax.broadcasted_iota(jnp.int32, (tt, LANES), 1)
        acc = jnp.zeros((tt, LANES), F32)
        for h in range(nh):
            acc = jnp.where(lane == h, b_ref[h], acc)
            acc = jnp.where(lane == nh + h, a_ref[h], acc)
        o_ref[...] = acc.astype(BF16)

    spec = pl.BlockSpec((nh, tt, LANES), lambda i: (0, i, 0))
    return pl.pallas_call(
        body, name="pack_tail", grid=(t // tt,), in_specs=[spec, spec], out_specs=pl.BlockSpec((tt, LANES), lambda i: (i, 0)),
        out_shape=jax.ShapeDtypeStruct((t, LANES), BF16), compiler_params=_params("parallel"))(dbr, dar)


def _rep_rows(v):
    return jnp.broadcast_to(v.reshape(1, LANES), (SUBLANES, LANES))


def _rep_heads(v):
    return jnp.broadcast_to(v.reshape(-1, 1, 1), (v.shape[0], SUBLANES, LANES))


def _col_range(stacked, lo, hi):
    c = stacked.shape[2]
    parts = [stacked[s, :, max(lo, s * c) - s * c:min(hi, (s + 1) * c) - s * c]
             for s in range(4) if max(lo, s * c) < min(hi, (s + 1) * c)]
    return parts[0] if len(parts) == 1 else jnp.concatenate(parts, axis=1)


def _col_shards(pieces, c):
    shards, offs, o = [], [], 0
    for pc in pieces:
        offs.append(o)
        o += pc.shape[1]
    for s in range(4):
        lo, hi = s * c, (s + 1) * c
        parts = [pc[:, max(lo, o) - o:min(hi, o + pc.shape[1]) - o] for pc, o in zip(pieces, offs)
                 if max(lo, o) < min(hi, o + pc.shape[1])]
        shards.append(parts[0] if len(parts) == 1 else jnp.concatenate(parts, axis=1))
    return jnp.stack(shards)


def _local_step(x, p, target, w, late_weights, early_grads_ready, early_grads_swapped, last_grad_ready, start_token):
    a = DEEPNORM_ALPHA
    nh = GDN_HEADS
    xb = (x + start_token).astype(BF16)
    wie = w["w_in_even"]
    w_a, w_qkv, w_zb = _col_range(wie, 0, 4096), _col_range(wie, 4096, 7168), _col_range(wie, 7168, 8192)
    w_tail = jnp.pad(_col_range(wie, 8192, 8192 + 2 * nh), ((0, 0), (0, LANES - 2 * nh)))
    conv_a_w, conv_b_w = w["conv_a_w"], w["conv_b_w"]
    ln_g0, ln_b0, ln_g1, ln_b1 = (v.reshape(1, D_MODEL) for v in (w["ln_g"][0], w["ln_b"][0], w["ln_g"][1], w["ln_b"][1]))
    alog, dtb = _rep_heads(w["a_log"].reshape(nh)), _rep_heads(w["dt_bias"].reshape(nh))
    gdn_g, hgrn_g = _rep_rows(w["gdn_norm_g"]), _rep_rows(w["hgrn_norm_g"])

    proj_a = _matmul(xb, w_a, name="fwd_proj_a")
    proj_qkv = _matmul(xb, w_qkv, name="fwd_proj_qkv")
    proj_zb = _matmul(xb, w_zb, name="fwd_proj_zb")
    proj_tail = _matmul(xb, w_tail, name="fwd_proj_tail")
    rep = lambda cols: jnp.broadcast_to(cols.T[:, :, None], (nh, cols.shape[0], LANES))
    braw, araw = rep(proj_tail[:, :nh]), rep(proj_tail[:, nh:2 * nh])
    y_a = _conv_a_fwd(proj_a, conv_a_w)
    qkv_act = _conv_b_fwd(proj_qkv, conv_b_w)
    *gdn_pre, gdn_inv = _gdn_prep_fwd(qkv_act, braw, araw, alog, dtb)
    y_b, gdn_hist = _gdn_scan_fwd(*gdn_pre, proj_zb, gdn_g)
    w = {**w, **late_weights(y_b)}
    woe, wio, woo = w["w_out_even"], w["w_in_odd"], w["w_out_odd"]
    s0 = _matmul(y_b, woe[1024:], name="fwd_out_even_b", add=_matmul(y_a, woe[:1024], name="fwd_out_even_a"))
    x1, x1b = _lnpl_fwd(x, s0, p[0], w["w_pl_gate"][0], w["w_pl"][0], ln_g0, ln_b0)
    proj_o = [_matmul(x1b, wio[j], name=f"fwd_proj_odd{j}") for j in range(4)]
    hgrn_pre = _hgrn_prep_fwd(proj_o[0], proj_o[1], w["lower_bounds"])
    y_o, hgrn_hist = _hgrn_scan_fwd(*hgrn_pre, proj_o[2], proj_o[3], hgrn_g)
    s1 = _matmul(y_o, woo, name="fwd_out_odd")

    g = {}
    dpre1, dwg1, dwpl1, dlng1, dlnb1, loss = _lnpl_bwd(x1, s1, p[1], w["w_pl_gate"][1], w["w_pl"][1], ln_g1, ln_b1,
                                                     target, True, "lnpl_bwd_odd")
    dy_o = _matmul(dpre1, woo, tb=True, name="bwd_out_odd_dx")
    g["w_out_odd"] = _matmul(y_o, dpre1, ta=True, name="bwd_out_odd_dw")
    dqe, dkd, dat, deb, di, dz, dhg = _hgrn_scan_bwd(*hgrn_pre, proj_o[2], proj_o[3], hgrn_g, hgrn_hist, dy_o)
    dq, df, dlb = _hgrn_prep_bwd(proj_o[0], proj_o[1], w["lower_bounds"], dqe, dkd, dat, deb)
    dx1 = dpre1
    scale = a
    dws = []
    for j, dj in enumerate((dq, df, di, dz)):
        dx1 = _matmul(dj, wio[j], tb=True, add=dx1, add_scale=scale, name=f"bwd_proj_odd_dx{j}")
        scale = 1.0
        dws.append(_matmul(x1b, dj, ta=True, name=f"bwd_proj_odd_dw{j}"))
    g["w_in_odd"] = jnp.stack(dws)
    g["hgrn_norm_g"] = dhg[0:1]
    g["lower_bounds"] = dlb
    g["w_pl_gate1"], g["w_pl1"] = dwg1, dwpl1

    dpre0, dwg0, dwpl0, dlng0, dlnb0, _ = _lnpl_bwd(x, s0, p[0], w["w_pl_gate"][0], w["w_pl"][0], ln_g0, ln_b0,
                                                  dx1, False, "lnpl_bwd_even")
    g["w_pl_gate0"], g["w_pl0"] = dwg0, dwpl0
    g["ln_g"] = jnp.concatenate([dlng0, dlng1], axis=0)
    g["ln_b"] = jnp.concatenate([dlnb0, dlnb1], axis=0)
    dy_a = _matmul(dpre0, woe[:1024], tb=True, name="bwd_out_even_dxa")
    dy_b = _matmul(dpre0, woe[1024:], tb=True, name="bwd_out_even_dxb")
    g["w_out_even"] = jnp.concatenate([_matmul(y_a, dpre0, ta=True, name="bwd_out_even_dwa"),
                                       _matmul(y_b, dpre0, ta=True, name="bwd_out_even_dwb")], axis=0)
    token = early_grads_ready({n: g[n] for n in _RS_EARLY})
    d_a, dwa = _conv_a_bwd(proj_a, conv_a_w + token, dy_a)
    g["conv_a_w"] = dwa[:3]
    gdn_g = gdn_g + early_grads_swapped(d_a)
    du, dw, dqd, dkd, dat, deg, dzb, dgn = _gdn_scan_bwd(*gdn_pre, proj_zb, gdn_g, gdn_hist, dy_b)
    dqa, dka, dva, dbr, dar, dal, ddt = _gdn_prep_bwd(qkv_act, braw, araw, alog, dtb, gdn_inv, du, dw, dqd, dkd, dat, deg)
    g["a_log"] = dal[:, 0, 0].reshape(1, nh)
    g["dt_bias"] = ddt[:, 0, 0].reshape(1, nh)
    g["gdn_norm_g"] = dgn[0:1]
    d_pre_qkv, dwb = [], []
    for j, dj in enumerate((dqa, dka, dva)):
        dpj, dwj = _conv_b_bwd(proj_qkv, conv_b_w, dj, j, f"conv_b_bwd{j}")
        d_pre_qkv.append(dpj)
        dwb.append(dwj[:4])
    g["conv_b_w"] = jnp.concatenate(dwb, axis=1)
    d_tail = _pack_tail(dbr, dar)
    pieces = [(d_a, w_a), (d_pre_qkv[0], w_qkv[:, :1024]), (d_pre_qkv[1], w_qkv[:, 1024:2048]),
              (d_pre_qkv[2], w_qkv[:, 2048:]), (dzb, w_zb), (d_tail, w_tail)]
    dws = [_matmul(xb, dj, ta=True, name=f"bwd_proj_even_dw{j}") for j, (dj, _) in enumerate(pieces)]
    dws[-1] = dws[-1][:, :2 * nh]
    g["w_in_even"] = _col_shards(dws, wie.shape[2])
    token = last_grad_ready(g["w_in_even"])
    dx = dpre0
    scale = a
    for j, (dj, wj) in enumerate(pieces):
        dx = _matmul(dj, wj + jnp.asarray(token).astype(BF16) if j == 0 else wj, tb=True, add=dx, add_scale=scale,
                     name=f"bwd_proj_even_dx{j}")
        scale = 1.0
    return loss, dx, g


def _adamw(w, g, m, v, name):
    lead, rows, cols = w.shape
    if rows % SUBLANES == 0 or rows <= 256:
        tr, tc = (rows if rows <= 256 else 256), cols
    else:
        tr, tc = rows, 256
    assert rows % tr == 0 and cols % tc == 0, (name, rows, cols)

    def body(w_ref, g_ref, m_ref, v_ref, d_ref, nm_ref, nv_ref):
        gg = g_ref[...]
        nm = ADAM_B1 * m_ref[...] + (1.0 - ADAM_B1) * gg
        nv = ADAM_B2 * v_ref[...] + (1.0 - ADAM_B2) * jnp.square(gg)
        m_hat = nm / (1.0 - ADAM_B1 ** ADAM_STEP)
        v_hat = nv / (1.0 - ADAM_B2 ** ADAM_STEP)
        d_ref[...] = -ADAM_LR * (m_hat / (jnp.sqrt(v_hat) + ADAM_EPS) + ADAM_WD * w_ref[...])
        nm_ref[...] = nm
        nv_ref[...] = nv

    spec = pl.BlockSpec((1, tr, tc), lambda l, i, j: (l, i, j))
    return pl.pallas_call(
        body, name=name, grid=(lead, rows // tr, cols // tc), in_specs=[spec] * 4, out_specs=[spec] * 3,
        out_shape=[jax.ShapeDtypeStruct(w.shape, F32)] * 3,
        compiler_params=_params("parallel", "parallel", "parallel"))(w, g, m, v)


MESH = pl.DeviceIdType.MESH
N_DEV = 8
HBM_SPEC = pl.BlockSpec(memory_space=pltpu.HBM)
VMEM_SPEC = pl.BlockSpec(memory_space=pltpu.VMEM)


def _coords():
    return lax.axis_index("x"), lax.axis_index("y"), lax.axis_index("c")


def _flip(v, bit):
    return 1 - v if bit else v


def _remote(src, dst, send_sem, recv_sem, dev):
    return pltpu.make_async_remote_copy(src_ref=src, dst_ref=dst, send_sem=send_sem, recv_sem=recv_sem,
                                        device_id=dev, device_id_type=MESH)


def _exchange_small(buf, reduce, name):
    rows = buf.shape[0]

    def body(in_ref, out_ref, slots, send_sems, recv_sems):
        x, y, c = _coords()
        me = 4 * x + 2 * y + c
        slots[me] = in_ref[...]
        peer = lambda k: (_flip(x, (k >> 2) & 1), _flip(y, (k >> 1) & 1), _flip(c, k & 1))
        sends = []
        for k in range(1, N_DEV):
            cp = _remote(in_ref, slots.at[me], send_sems.at[k - 1], recv_sems.at[k - 1], peer(k))
            cp.start()
            sends.append(cp)
        for k in range(1, N_DEV):
            px, py, pc = peer(k)
            _remote(in_ref, slots.at[4 * px + 2 * py + pc], send_sems.at[k - 1], recv_sems.at[k - 1], peer(k)).wait_recv()
        for cp in sends:
            cp.wait_send()
        if reduce:
            acc = slots[0]
            for d in range(1, N_DEV):
                acc = acc + slots[d]
            out_ref[...] = acc
        else:
            out_ref[...] = slots[...]

    out_shape = (rows, LANES) if reduce else (N_DEV, rows, LANES)
    return pl.pallas_call(
        body, name=name, in_specs=[VMEM_SPEC], out_specs=VMEM_SPEC, out_shape=jax.ShapeDtypeStruct(out_shape, F32),
        scratch_shapes=[pltpu.VMEM((N_DEV, rows, LANES), F32), pltpu.SemaphoreType.DMA((N_DEV - 1,)),
                        pltpu.SemaphoreType.DMA((N_DEV - 1,))])(buf)


def _half_rows(half, which):
    return pl.ds(pl.multiple_of(which * half, 16), half)


def _other_chip(x, y, k):
    return _flip(x, (k >> 1) & 1), _flip(y, k & 1)


SEM_SPEC = pl.BlockSpec(memory_space=pltpu.SEMAPHORE)
DATAFLOW = pltpu.SideEffectType.DATAFLOW_SIDE_EFFECTING


def _ici_piece(srcs, lands, send_sems, recv_sems, i, k, x, y, c):
    half = srcs[i].shape[0] // 2
    ox, oy = _other_chip(x, y, k)
    return _remote(srcs[i].at[_half_rows(half, c)], lands[i].at[2 * x + y, _half_rows(half, c)],
                   send_sems.at[3 * i + k - 1], recv_sems.at[3 * i + k - 1], (ox, oy, c)), (ox, oy)


def _gather_start(shards, name):
    n = len(shards)

    def body(*refs):
        srcs, lands = refs[:n], refs[n:2 * n]
        send_sems, recv_sems = refs[2 * n], refs[2 * n + 1]
        token = refs[-1]
        x, y, c = _coords()
        for i in range(n):
            for k in (1, 2, 3):
                _ici_piece(srcs, lands, send_sems, recv_sems, i, k, x, y, c)[0].start()
        token[...] = jnp.zeros_like(token)

    hbm = lambda a: pltpu.with_memory_space_constraint(a, pltpu.HBM)
    lands = [lax.empty((4,) + s.shape, s.dtype) for s in shards]
    out = pl.pallas_call(
        body, name=name,
        out_shape=(pltpu.SemaphoreType.DMA((3 * n,)), pltpu.SemaphoreType.DMA((3 * n,)),
                   *[pltpu.HBM(s.shape, s.dtype) for s in shards], *[pltpu.HBM(a.shape, a.dtype) for a in lands],
                   jax.ShapeDtypeStruct((SUBLANES, LANES), F32)),
        in_specs=[HBM_SPEC] * (2 * n), out_specs=(SEM_SPEC, SEM_SPEC, *[HBM_SPEC] * (2 * n), VMEM_SPEC),
        input_output_aliases={i: 2 + i for i in range(2 * n)},
        compiler_params=pltpu.CompilerParams(has_side_effects=DATAFLOW))(*[hbm(s) for s in shards], *[hbm(a) for a in lands])
    return out[0], out[1], out[2:2 + n], out[2 + n:2 + 2 * n], out[-1]


def _gather_wait(send_sems, recv_sems, srcs, lands, after, name):
    n = len(srcs)

    def body(*refs):
        src_refs, land_refs = refs[:n], refs[n:2 * n]
        send_sems, recv_sems = refs[2 * n], refs[2 * n + 1]
        x, y, c = _coords()
        for i in range(n):
            half = src_refs[i].shape[0] // 2
            for k in (1, 2, 3):
                cp, (ox, oy) = _ici_piece(src_refs, land_refs, send_sems, recv_sems, i, k, x, y, c)
                cp.wait_send()
                piece = land_refs[i].at[2 * ox + oy, _half_rows(half, c)]
                _remote(piece, piece, send_sems.at[3 * i + k - 1], recv_sems.at[3 * i + k - 1], (ox, oy, c)).wait_recv()

    out = pl.pallas_call(
        body, name=name,
        out_shape=(*[pltpu.HBM(s.shape, s.dtype) for s in srcs], *[pltpu.HBM(a.shape, a.dtype) for a in lands]),
        in_specs=[HBM_SPEC] * (2 * n) + [SEM_SPEC, SEM_SPEC, pl.BlockSpec(memory_space=pl.ANY)],
        out_specs=tuple([HBM_SPEC] * (2 * n)), input_output_aliases={i: i for i in range(2 * n)},
        compiler_params=pltpu.CompilerParams(has_side_effects=DATAFLOW))(*srcs, *lands, send_sems, recv_sems, after)
    return out[n:]


def _gather_forward(lands, name):
    n = len(lands)

    def body(*refs):
        ins, outs = refs[:n], refs[n:2 * n]
        send_sems, recv_sems = refs[2 * n:]
        x, y, c = _coords()
        sends = []
        for i in range(n):
            half = ins[i].shape[1] // 2
            for k in (1, 2, 3):
                ox, oy = _other_chip(x, y, k)
                cp = _remote(ins[i].at[2 * ox + oy, _half_rows(half, c)], outs[i].at[2 * ox + oy, _half_rows(half, c)],
                             send_sems.at[3 * i + k - 1], recv_sems.at[3 * i + k - 1], (x, y, 1 - c))
                cp.start()
                sends.append(cp)
        for i in range(n):
            half = ins[i].shape[1] // 2
            for k in (1, 2, 3):
                ox, oy = _other_chip(x, y, k)
                piece = outs[i].at[2 * ox + oy, _half_rows(half, 1 - c)]
                _remote(piece, piece, send_sems.at[3 * i + k - 1], recv_sems.at[3 * i + k - 1], (x, y, 1 - c)).wait_recv()
        for cp in sends:
            cp.wait_send()

    return pl.pallas_call(
        body, name=name, in_specs=[HBM_SPEC] * n, out_specs=[HBM_SPEC] * n,
        out_shape=[jax.ShapeDtypeStruct(a.shape, a.dtype) for a in lands],
        input_output_aliases={i: i for i in range(n)},
        scratch_shapes=[pltpu.SemaphoreType.DMA((3 * n,))] * 2)(*lands)


def _rs_sibling_swap(g4s, name):
    n = len(g4s)

    def body(*refs):
        ins, outs = refs[:n], refs[n:2 * n]
        send_sems, recv_sems = refs[2 * n:]
        x, y, c = _coords()
        sends = []
        for i in range(n):
            half = ins[i].shape[1] // 2
            for s in range(4):
                cp = _remote(ins[i].at[s, _half_rows(half, 1 - c)], outs[i].at[s], send_sems.at[4 * i + s],
                             recv_sems.at[4 * i + s], (x, y, 1 - c))
                cp.start()
                sends.append(cp)
        for cp in sends:
            cp.wait_recv()
        for cp in sends:
            cp.wait_send()

    return pl.pallas_call(
        body, name=name, in_specs=[HBM_SPEC] * n, out_specs=[HBM_SPEC] * n,
        out_shape=[jax.ShapeDtypeStruct((4, g.shape[1] // 2, g.shape[2]), g.dtype) for g in g4s],
        scratch_shapes=[pltpu.SemaphoreType.DMA((4 * n,))] * 2)(*g4s)


def _rs_add_sibling(g4, got, c_idx, name):
    _, rows, cols = g4.shape
    half = rows // 2
    tr = min(half, 256)
    nb = half // tr

    def body(c_ref, a_ref, b_ref, o_ref, ob_ref):
        total = a_ref[...] + b_ref[...]
        o_ref[...] = total
        ob_ref[...] = total.astype(BF16)

    blk = (1, tr, cols)
    out = pl.BlockSpec(blk, lambda s, i, c_ref: (s, i, 0))
    grid_spec = pltpu.PrefetchScalarGridSpec(
        num_scalar_prefetch=1, grid=(4, nb),
        in_specs=[pl.BlockSpec(blk, lambda s, i, c_ref: (s, c_ref[0] * nb + i, 0)), out],
        out_specs=[out, out])
    return pl.pallas_call(
        body, name=name, grid_spec=grid_spec,
        out_shape=[jax.ShapeDtypeStruct(got.shape, F32), jax.ShapeDtypeStruct(got.shape, BF16)],
        compiler_params=_params("parallel", "parallel"))(c_idx, g4, got)


def _rs_add_chips(p4, got3, idx, name):
    _, half, cols = p4.shape
    tr = min(half, 256)
    nb = half // tr

    def body(idx_ref, p_ref, a_ref, b_ref, c_ref, o_ref):
        o_ref[...] = ((p_ref[0] + a_ref[0].astype(F32)) + b_ref[0].astype(F32)) + c_ref[0].astype(F32)

    blk = (1, tr, cols)
    grid_spec = pltpu.PrefetchScalarGridSpec(
        num_scalar_prefetch=1, grid=(nb,),
        in_specs=[pl.BlockSpec(blk, lambda i, idx_ref: (idx_ref[0], i, 0))]
        + [pl.BlockSpec(blk, functools.partial(lambda k, i, idx_ref: (k, i, 0), k)) for k in range(3)],
        out_specs=pl.BlockSpec((tr, cols), lambda i, idx_ref: (idx_ref[1] * nb + i, 0)))
    return pl.pallas_call(body, name=name, grid_spec=grid_spec, out_shape=jax.ShapeDtypeStruct((2 * half, cols), F32),
                          compiler_params=_params("parallel"))(idx, p4, got3, got3, got3)


def _rs_share_halves(bufs, name):
    n = len(bufs)

    def body(*refs):
        ins, outs = refs[:n], refs[n:2 * n]
        send_sems, recv_sems = refs[2 * n:]
        x, y, c = _coords()
        sends = []
        for i in range(n):
            half = ins[i].shape[0] // 2
            cp = _remote(ins[i].at[_half_rows(half, c)], outs[i].at[_half_rows(half, c)], send_sems.at[i],
                         recv_sems.at[i], (x, y, 1 - c))
            cp.start()
            sends.append(cp)
        for i in range(n):
            half = ins[i].shape[0] // 2
            _remote(ins[i].at[_half_rows(half, c)], outs[i].at[_half_rows(half, 1 - c)], send_sems.at[i],
                    recv_sems.at[i], (x, y, 1 - c)).wait_recv()
        for cp in sends:
            cp.wait_send()

    return pl.pallas_call(
        body, name=name, in_specs=[HBM_SPEC] * n, out_specs=[HBM_SPEC] * n,
        out_shape=[jax.ShapeDtypeStruct(b.shape, b.dtype) for b in bufs],
        input_output_aliases={i: i for i in range(n)},
        scratch_shapes=[pltpu.SemaphoreType.DMA((n,))] * 2)(*bufs)


def _scatter_piece(srcs, lands, send_sems, recv_sems, i, k, x, y, c):
    ox, oy = _other_chip(x, y, k)
    return _remote(srcs[i].at[2 * ox + oy], lands[i].at[k - 1], send_sems.at[3 * i + k - 1],
                   recv_sems.at[3 * i + k - 1], (ox, oy, c))


def _rs_scatter_start(p4s, name):
    n = len(p4s)

    def body(*refs):
        srcs, lands = refs[:n], refs[n:2 * n]
        send_sems, recv_sems = refs[2 * n], refs[2 * n + 1]
        token = refs[-1]
        x, y, c = _coords()
        for i in range(n):
            for k in (1, 2, 3):
                _scatter_piece(srcs, lands, send_sems, recv_sems, i, k, x, y, c).start()
        token[...] = jnp.zeros_like(token)

    hbm = lambda a: pltpu.with_memory_space_constraint(a, pltpu.HBM)
    lands = [lax.empty((3,) + p.shape[1:], p.dtype) for p in p4s]
    out = pl.pallas_call(
        body, name=name,
        out_shape=(pltpu.SemaphoreType.DMA((3 * n,)), pltpu.SemaphoreType.DMA((3 * n,)),
                   *[pltpu.HBM(p.shape, p.dtype) for p in p4s], *[pltpu.HBM(a.shape, a.dtype) for a in lands],
                   jax.ShapeDtypeStruct((SUBLANES, LANES), F32)),
        in_specs=[HBM_SPEC] * (2 * n), out_specs=(SEM_SPEC, SEM_SPEC, *[HBM_SPEC] * (2 * n), VMEM_SPEC),
        input_output_aliases={i: 2 + i for i in range(2 * n)},
        compiler_params=pltpu.CompilerParams(has_side_effects=DATAFLOW))(*[hbm(p) for p in p4s], *[hbm(a) for a in lands])
    return out[0], out[1], out[2:2 + n], out[2 + n:2 + 2 * n], out[-1]


def _rs_scatter_wait(send_sems, recv_sems, srcs, lands, after, name):
    n = len(srcs)

    def body(*refs):
        src_refs, land_refs = refs[:n], refs[n:2 * n]
        send_sems, recv_sems = refs[2 * n], refs[2 * n + 1]
        x, y, c = _coords()
        for i in range(n):
            for k in (1, 2, 3):
                cp = _scatter_piece(src_refs, land_refs, send_sems, recv_sems, i, k, x, y, c)
                cp.wait_send()
                cp.wait_recv()

    out = pl.pallas_call(
        body, name=name,
        out_shape=(*[pltpu.HBM(s.shape, s.dtype) for s in srcs], *[pltpu.HBM(a.shape, a.dtype) for a in lands]),
        in_specs=[HBM_SPEC] * (2 * n) + [SEM_SPEC, SEM_SPEC, pl.BlockSpec(memory_space=pl.ANY)],
        out_specs=tuple([HBM_SPEC] * (2 * n)), input_output_aliases={i: i for i in range(2 * n)},
        compiler_params=pltpu.CompilerParams(has_side_effects=DATAFLOW))(*srcs, *lands, send_sems, recv_sems, after)
    return out[n:]


def _swap_piece(srcs, lands, send_sems, recv_sems, i, s, x, y, c):
    half = srcs[i].shape[1] // 2
    return _remote(srcs[i].at[s, _half_rows(half, 1 - c)], lands[i].at[s], send_sems.at[4 * i + s],
                   recv_sems.at[4 * i + s], (x, y, 1 - c))


def _rs_swap_start(g4s, name):
    n = len(g4s)

    def body(*refs):
        srcs, lands = refs[:n], refs[n:2 * n]
        send_sems, recv_sems = refs[2 * n], refs[2 * n + 1]
        token = refs[-1]
        x, y, c = _coords()
        for i in range(n):
            for s in range(4):
                _swap_piece(srcs, lands, send_sems, recv_sems, i, s, x, y, c).start()
        token[...] = jnp.zeros_like(token)

    hbm = lambda a: pltpu.with_memory_space_constraint(a, pltpu.HBM)
    lands = [lax.empty((4, g.shape[1] // 2, g.shape[2]), g.dtype) for g in g4s]
    out = pl.pallas_call(
        body, name=name,
        out_shape=(pltpu.SemaphoreType.DMA((4 * n,)), pltpu.SemaphoreType.DMA((4 * n,)),
                   *[pltpu.HBM(g.shape, g.dtype) for g in g4s], *[pltpu.HBM(a.shape, a.dtype) for a in lands],
                   jax.ShapeDtypeStruct((SUBLANES, LANES), F32)),
        in_specs=[HBM_SPEC] * (2 * n), out_specs=(SEM_SPEC, SEM_SPEC, *[HBM_SPEC] * (2 * n), VMEM_SPEC),
        input_output_aliases={i: 2 + i for i in range(2 * n)},
        compiler_params=pltpu.CompilerParams(has_side_effects=DATAFLOW))(*[hbm(g) for g in g4s], *[hbm(a) for a in lands])
    return out[0], out[1], out[2:2 + n], out[2 + n:2 + 2 * n], out[-1]


def _rs_swap_wait(send_sems, recv_sems, srcs, lands, after, name):
    n = len(srcs)

    def body(*refs):
        src_refs, land_refs = refs[:n], refs[n:2 * n]
        send_sems, recv_sems = refs[2 * n], refs[2 * n + 1]
        x, y, c = _coords()
        for i in range(n):
            for s in range(4):
                cp = _swap_piece(src_refs, land_refs, send_sems, recv_sems, i, s, x, y, c)
                cp.wait_send()
                cp.wait_recv()

    out = pl.pallas_call(
        body, name=name,
        out_shape=(*[pltpu.HBM(s.shape, s.dtype) for s in srcs], *[pltpu.HBM(a.shape, a.dtype) for a in lands]),
        in_specs=[HBM_SPEC] * (2 * n) + [SEM_SPEC, SEM_SPEC, pl.BlockSpec(memory_space=pl.ANY)],
        out_specs=tuple([HBM_SPEC] * (2 * n)), input_output_aliases={i: i for i in range(2 * n)},
        compiler_params=pltpu.CompilerParams(has_side_effects=DATAFLOW))(*srcs, *lands, send_sems, recv_sems, after)
    return out[:n], out[n:]


def _rs_front(g4s, names, tag):
    c_idx = jnp.stack([lax.axis_index("c")]).astype(jnp.int32)
    got = _rs_sibling_swap(g4s, f"rs_sibling_swap_{tag}")
    return [_rs_add_sibling(g, s, c_idx, f"rs_add_sibling_{nm}") for g, s, nm in zip(g4s, got, names)]


def _rs_back(p4s, got3, names):
    x, y, c = _coords()
    idx = jnp.stack([2 * x + y, c]).astype(jnp.int32)
    return [_rs_add_chips(p, t, idx, f"rs_add_chips_{nm}") for (p, _), t, nm in zip(p4s, got3, names)]


def _cols_split(full):
    r, c4 = full.shape
    return full.reshape(r, 4, c4 // 4).transpose(1, 0, 2)


_BIG = {
    "w_in_even": ((1024, 2052), lambda s: s),
    "w_out_even": ((512, 1024), lambda s: s.reshape(2048, 1024)),
    "w_in_odd": ((1024, 2048), lambda s: s),
    "w_out_odd": ((512, 1024), lambda s: s.reshape(2048, 1024)),
    "w_pl": ((512, 256), lambda s: s.reshape(4, 2, 256, 256).transpose(1, 2, 0, 3).reshape(2, 256, 1024)),
    "w_pl_gate": ((512, 1024), lambda s: s.reshape(4, 2, 256, 1024).transpose(1, 0, 2, 3).reshape(2, 1024, 1024)),
}


_RS_EARLY = {
    "w_in_odd": lambda f: f,
    "w_out_odd": lambda f: f.reshape(4, 512, 1024),
    "w_pl_gate1": lambda f: f.reshape(4, 256, 1024),
    "w_pl1": _cols_split,
    "w_out_even": lambda f: f.reshape(4, 512, 1024),
    "w_pl_gate0": lambda f: f.reshape(4, 256, 1024),
    "w_pl0": _cols_split,
}
_RS_LATE = {"w_in_even": lambda f: f}


def _size(shape):
    n = 1
    for d in shape:
        n *= d
    return n


_SMALL = {"a_log": (1, 8), "dt_bias": (1, 8), "gdn_norm_g": (1, 128), "hgrn_norm_g": (1, 128),
          "lower_bounds": (2, 2048), "ln_g": (2, 1024), "ln_b": (2, 1024), "conv_a_w": (3, 1024), "conv_b_w": (4, 3072)}
_CONV_SHARD = {"conv_a_w": (3, 256), "conv_b_w": (4, 768)}


def _pack_small(parts, shapes, head_rows=0):
    rows = []
    for n, shape in shapes.items():
        v = parts[n].reshape(-1)
        rows.append(jnp.pad(v, (0, -v.shape[0] % LANES)).reshape(-1, LANES))
    buf = jnp.concatenate(rows, axis=0)
    return jnp.pad(buf, ((head_rows, -(buf.shape[0] + head_rows) % SUBLANES), (0, 0)))


def _unpack_small(buf, shapes, head_rows=0):
    out, off = {}, head_rows
    for n, shape in shapes.items():
        nrow = -(-_size(shape) // LANES)
        out[n] = buf[off:off + nrow].reshape(-1)[:_size(shape)].reshape(shape)
        off += nrow
    return out


_WEIGHTS = ["w_in_even", "conv_a_w", "conv_b_w", "a_log", "dt_bias", "gdn_norm_g", "w_out_even", "w_in_odd",
            "lower_bounds", "hgrn_norm_g", "w_out_odd", "ln_g", "ln_b", "w_pl", "w_pl_gate"]


def kernel(x, p, w_in_even, conv_a_w, conv_b_w, a_log, dt_bias, gdn_norm_g, w_out_even, w_in_odd, lower_bounds, hgrn_norm_g, w_out_odd, ln_g, ln_b, w_pl, w_pl_gate, loss_target, m_w_in_even, m_conv_a_w, m_conv_b_w, m_a_log, m_dt_bias, m_gdn_norm_g, m_w_out_even, m_w_in_odd, m_lower_bounds, m_hgrn_norm_g, m_w_out_odd, m_ln_g, m_ln_b, m_w_pl, m_w_pl_gate, v_w_in_even, v_conv_a_w, v_conv_b_w, v_a_log, v_dt_bias, v_gdn_norm_g, v_w_out_even, v_w_in_odd, v_lower_bounds, v_hgrn_norm_g, v_w_out_odd, v_ln_g, v_ln_b, v_w_pl, v_w_pl_gate):
    w = dict(zip(_WEIGHTS, (w_in_even, conv_a_w, conv_b_w, a_log, dt_bias, gdn_norm_g, w_out_even, w_in_odd,
                            lower_bounds, hgrn_norm_g, w_out_odd, ln_g, ln_b, w_pl, w_pl_gate)))
    m = dict(zip(_WEIGHTS, (m_w_in_even, m_conv_a_w, m_conv_b_w, m_a_log, m_dt_bias, m_gdn_norm_g, m_w_out_even,
                            m_w_in_odd, m_lower_bounds, m_hgrn_norm_g, m_w_out_odd, m_ln_g, m_ln_b, m_w_pl, m_w_pl_gate)))
    v = dict(zip(_WEIGHTS, (v_w_in_even, v_conv_a_w, v_conv_b_w, v_a_log, v_dt_bias, v_gdn_norm_g, v_w_out_even,
                            v_w_in_odd, v_lower_bounds, v_hgrn_norm_g, v_w_out_odd, v_ln_g, v_ln_b, v_w_pl, v_w_pl_gate)))
    chip = 2 * lax.axis_index("x") + lax.axis_index("y")

    names = list(_BIG)
    shard_shapes = {n: _BIG[n][0] for n in names}
    early, late = names[:1], names[1:]
    shards = {n: w[n].reshape(shard_shapes[n]).astype(BF16) for n in early}
    whole = lambda n, stacked: _BIG[n][1](lax.dynamic_update_slice(stacked, shards[n][None], (chip, 0, 0)))
    conv_mine = _pack_small({n: w[n] for n in _CONV_SHARD}, _CONV_SHARD)
    conv_all = _exchange_small(conv_mine, False, "gather_conv")
    shards, conv_all = lax.optimization_barrier((shards, conv_all))
    first = _gather_start([shards[n] for n in early], "gather_first_start")
    shards.update({n: (w[n].reshape(shard_shapes[n]) + first[4][0, 0]).astype(BF16) for n in late})
    send_sems, recv_sems, srcs, lands, token = _gather_start([shards[n] for n in late], "gather_rest_start")

    def late_weights(after):
        landed = _gather_forward(_gather_wait(send_sems, recv_sems, srcs, lands, after, "gather_rest_wait"),
                                 "gather_rest_forward")
        return {n: whole(n, ga) for n, ga in zip(late, landed)}

    landed = _gather_forward(_gather_wait(*first[:4], token, "gather_first_wait"), "gather_first_forward")
    full = {n: whole(n, ga) for n, ga in zip(early, landed)}
    conv_by_chip = [_unpack_small(conv_all[2 * s], _CONV_SHARD) for s in range(4)]
    for n in _CONV_SHARD:
        full[n] = jnp.concatenate([conv_by_chip[s][n] for s in range(4)], axis=1)
    for n in _SMALL:
        if n not in _CONV_SHARD:
            full[n] = w[n]

    early_rs = {}

    def early_grads_ready(grads):
        early_rs["swap"] = _rs_swap_start([_RS_EARLY[n](grads[n]) for n in _RS_EARLY], "rs_swap_early_start")
        return early_rs["swap"][4][0, 0]

    def early_grads_swapped(after):
        g4s, got = _rs_swap_wait(*early_rs["swap"][:4], after, "rs_swap_early_wait")
        c_idx = jnp.stack([lax.axis_index("c")]).astype(jnp.int32)
        early_rs["p4s"] = [_rs_add_sibling(g_, s_, c_idx, f"rs_add_sibling_{nm}") for g_, s_, nm in zip(g4s, got, _RS_EARLY)]
        early_rs["sems"] = _rs_scatter_start([pb for _, pb in early_rs["p4s"]], "rs_scatter_early_start")
        return early_rs["sems"][4][0, 0]

    late_rs = {}

    def last_grad_ready(grad):
        late_rs["p4s"] = _rs_front([_RS_LATE[n](grad) for n in _RS_LATE], list(_RS_LATE), "late")
        late_rs["sems"] = _rs_scatter_start([pb for _, pb in late_rs["p4s"]], "rs_scatter_late_start")
        return late_rs["sems"][4][0, 0]

    loss_part, dx, g = _local_step(x[0], p[:, 0], loss_target[0], full, late_weights, early_grads_ready,
                                   early_grads_swapped, last_grad_ready, token[0, 0])

    late_p4s, late_sems = late_rs["p4s"], late_rs["sems"]
    got3 = _rs_scatter_wait(*early_rs["sems"][:4], dx, "rs_scatter_early_wait")
    summed = dict(zip(_RS_EARLY, _rs_share_halves(_rs_back(early_rs["p4s"], got3, list(_RS_EARLY)), "rs_share_early")))
    g_big = {n: summed[n] for n in names if n in summed}
    g_big["w_pl"] = jnp.stack([summed["w_pl0"], summed["w_pl1"]])
    g_big["w_pl_gate"] = jnp.stack([summed["w_pl_gate0"], summed["w_pl_gate1"]])
    small_sum = _exchange_small(jnp.concatenate([loss_part, _pack_small(g, _SMALL)], axis=0), True, "reduce_small")
    loss = small_sum[0, 0]
    g_small = _unpack_small(small_sum, _SMALL, head_rows=SUBLANES)
    for n, (rows, cols) in _CONV_SHARD.items():
        g_small[n] = lax.dynamic_slice_in_dim(g_small[n], chip * cols, cols, axis=1)

    grads, delta, new_m, new_v = {}, {}, {}, {}
    for n in late:
        grads[n] = g_big[n].reshape(w[n].shape)
        delta[n], new_m[n], new_v[n] = _adamw(w[n], grads[n], m[n], v[n], f"adamw_{n}")
    own = {n: (_CONV_SHARD[n] if n in _CONV_SHARD else _SMALL[n]) for n in _SMALL}
    packs = [_pack_small({n: src[n] for n in _SMALL}, own)[None] for src in (w, g_small, m, v)]
    outs = [_unpack_small(t[0], own) for t in _adamw(*packs, "adamw_small")]
    for n in _SMALL:
        grads[n] = g_small[n].reshape(w[n].shape)
        delta[n], new_m[n], new_v[n] = (t[n].reshape(w[n].shape) for t in outs)
    got3 = _rs_scatter_wait(*late_sems[:4], new_v["w_in_odd"], "rs_scatter_late_wait")
    (g_in_even,) = _rs_share_halves(_rs_back(late_p4s, got3, list(_RS_LATE)), "rs_share_late")
    for n in early:
        t_ = lambda a: jnp.swapaxes(a, 1, 2)
        g_t = t_(g_in_even.reshape(w[n].shape))
        grads[n] = t_(g_t)
        delta[n], new_m[n], new_v[n] = (t_(o) for o in _adamw(t_(w[n]), g_t, t_(m[n]), t_(v[n]), f"adamw_{n}"))
    return (loss, dx[None], *[grads[n] for n in _WEIGHTS], *[delta[n] for n in _WEIGHTS],
            *[new_m[n] for n in _WEIGHTS], *[new_v[n] for n in _WEIGHTS])
```

```python
import functools

import jax
import jax.numpy as jnp
from jax import lax
from jax.experimental import pallas as pl
from jax.experimental.pallas import tpu as pltpu

F32 = jnp.float32
BF16 = jnp.bfloat16

D_MODEL = 1024
PL_DIM = 256
GDN_HEADS = 8
HEAD_DIM = 128
GDN_CHUNK = 64
HGRN_HEADS = 16
HGRN_CHUNK = 32
HGRN_WIDTH = 2048
DEEPNORM_ALPHA = 4.0 ** 0.25
NORM_EPS = 1e-5
ADAM_LR, ADAM_B1, ADAM_B2, ADAM_EPS, ADAM_WD, ADAM_STEP = 0.001, 0.9, 0.999, 1e-08, 0.01, 10

VMEM_LIMIT = 56 * 1024 * 1024
SUBLANES = 8
LANES = 128


def _params(*sem):
    return pltpu.CompilerParams(dimension_semantics=sem, vmem_limit_bytes=VMEM_LIMIT)


ONE_PASS, THREE_PASS, EXACT_LHS, EXACT_RHS = 0, 1, 2, 3


def _split3(v):
    hi = v.astype(BF16)
    r1 = v - hi.astype(F32)
    mid = r1.astype(BF16)
    return hi, mid, (r1 - mid.astype(F32)).astype(BF16)


def _mm_raw(a, b, kind, prec):
    nb = a.ndim - 2
    ca = a.ndim - 1 if kind[0] == "n" else a.ndim - 2
    cb = b.ndim - 2 if kind[1] == "n" else b.ndim - 1
    dims = (((ca,), (cb,)), (tuple(range(nb)),) * 2)
    dot = lambda p, q: lax.dot_general(p, q, dims, preferred_element_type=F32)
    ah, bh = a.astype(BF16), b.astype(BF16)
    if prec == ONE_PASS:
        return dot(ah, bh)
    if prec == EXACT_LHS:
        b1, b2, b3 = _split3(b)
        return dot(ah, b1) + (dot(ah, b2) + dot(ah, b3))
    if prec == EXACT_RHS:
        a1, a2, a3 = _split3(a)
        return dot(a1, bh) + (dot(a2, bh) + dot(a3, bh))
    al = (a - ah.astype(F32)).astype(BF16)
    bl = (b - bh.astype(F32)).astype(BF16)
    return dot(ah, bh) + (dot(ah, bl) + dot(al, bh))


@functools.partial(jax.custom_vjp, nondiff_argnums=(2, 3))
def _mm_vjp(a, b, kind, hi):
    return _mm_raw(a, b, kind, hi)


def _mm_vjp_fwd(a, b, kind, hi):
    return _mm_raw(a, b, kind, hi), (a, b)


def _mm_vjp_bwd(kind, hi, res, dc):
    a, b = res
    if hi in (EXACT_LHS, EXACT_RHS):
        assert kind == "nn"
        if hi == EXACT_LHS:
            return jnp.zeros_like(a), _mm_raw(a, dc, "tn", EXACT_LHS)
        return _mm_raw(dc, b, "nt", EXACT_RHS), jnp.zeros_like(b)
    if kind == "nn":
        return _mm_raw(dc, b, "nt", hi), _mm_raw(a, dc, "tn", hi)
    if kind == "nt":
        return _mm_raw(dc, b, "nn", hi), _mm_raw(dc, a, "tn", hi)
    return _mm_raw(b, dc, "nt", hi), _mm_raw(a, dc, "nn", hi)


_mm_vjp.defvjp(_mm_vjp_fwd, _mm_vjp_bwd)


def _lane_total(v):
    return jnp.broadcast_to(jnp.sum(v, axis=-1, keepdims=True), v.shape)


def _matmul(a, b, *, name, ta=False, tb=False, add=None, add_scale=1.0, tm=1024, tn=2048, tk=1024):
    m, k = (a.shape[1], a.shape[0]) if ta else a.shape
    n = b.shape[0] if tb else b.shape[1]
    tm, tn, tk = min(tm, m), min(tn, n), min(tk, k)
    tn = tn if n % tn == 0 else tn // 2
    assert m % tm == 0 and n % tn == 0 and k % tk == 0, (name, m, n, k)
    nk = k // tk
    dims = (((0 if ta else 1,), (1 if tb else 0,)), ((), ()))

    def body(*refs):
        a_ref, b_ref = refs[:2]
        o_ref = refs[-1]
        part = lax.dot_general(a_ref[...].astype(BF16), b_ref[...].astype(BF16), dims, preferred_element_type=F32)
        first = (lambda: part) if add is None else (lambda: part + add_scale * refs[2][...])
        if nk == 1:
            o_ref[...] = first()
        else:
            kk = pl.program_id(2)

            @pl.when(kk == 0)
            def _():
                o_ref[...] = first()

            @pl.when(kk > 0)
            def _():
                o_ref[...] += part

    a_spec = pl.BlockSpec((tk, tm), lambda i, j, kk: (kk, i)) if ta else pl.BlockSpec((tm, tk), lambda i, j, kk: (i, kk))
    b_spec = pl.BlockSpec((tn, tk), lambda i, j, kk: (j, kk)) if tb else pl.BlockSpec((tk, tn), lambda i, j, kk: (kk, j))
    o_spec = pl.BlockSpec((tm, tn), lambda i, j, kk: (i, j))
    in_specs = [a_spec, b_spec] + ([o_spec] if add is not None else [])
    args = (a, b) + ((add,) if add is not None else ())
    return pl.pallas_call(
        body, name=name, grid=(m // tm, n // tn, nk), in_specs=in_specs, out_specs=o_spec,
        out_shape=jax.ShapeDtypeStruct((m, n), F32),
        compiler_params=_params("parallel", "parallel", "arbitrary"))(*args)


HALO = SUBLANES


def _halo_specs(tt, width, col, nt):
    r = tt // HALO
    prev = pl.BlockSpec((HALO, width), lambda i: (jnp.maximum(i * r - 1, 0), col))
    nxt = pl.BlockSpec((HALO, width), lambda i: (jnp.minimum((i + 1) * r, nt * r - 1), col))
    return prev, nxt


def _shift_down(ext, k):
    return ext if k == 0 else pltpu.roll(ext, k, 0)


def _shift_up(ext, k):
    return ext if k == 0 else pltpu.roll(ext, ext.shape[0] - k, 0)


def _causal_conv(ext, w, taps):
    acc = None
    for j in range(taps):
        term = w[j:j + 1, :] * _shift_down(ext, taps - 1 - j)
        acc = term if acc is None else acc + term
    return acc[HALO:, :]


def _conv_a_fwd(proj_a, conv_w):
    t = proj_a.shape[0]
    tt = min(t, 256)
    nt = t // tt
    wdt = 1024

    def body(cur_ref, prev_ref, w_ref, y_ref):
        i = pl.program_id(0)
        cur = cur_ref[...]
        h, c, b, z = (cur[:, k * wdt:(k + 1) * wdt] for k in range(4))
        prev = prev_ref[...]
        u_prev = jnp.where(i > 0, prev[:, wdt:2 * wdt] * prev[:, 0:wdt], 0.0)
        ext = jnp.concatenate([u_prev, c * h], axis=0)
        conv = _causal_conv(ext, w_ref[...], 3)
        y_ref[...] = (b * conv * jax.nn.silu(z)).astype(BF16)

    prev_spec, _ = _halo_specs(tt, 4 * wdt, 0, nt)
    return pl.pallas_call(
        body, name="conv_a_fwd", grid=(nt,),
        in_specs=[pl.BlockSpec((tt, 4 * wdt), lambda i: (i, 0)), prev_spec, pl.BlockSpec((3, wdt), lambda i: (0, 0))],
        out_specs=pl.BlockSpec((tt, wdt), lambda i: (i, 0)),
        out_shape=jax.ShapeDtypeStruct((t, wdt), BF16), compiler_params=_params("parallel"))(proj_a, proj_a, conv_w)


def _conv_a_bwd(proj_a, conv_w, dy):
    t = proj_a.shape[0]
    tt = min(t, 256)
    nt = t // tt
    wdt = 1024

    def body(cur_ref, prev_ref, nxt_ref, w_ref, dy_ref, dyn_ref, d_ref, dw_ref):
        i = pl.program_id(0)
        w = w_ref[...]
        cur, prev, nxt = cur_ref[...], prev_ref[...], nxt_ref[...]
        split = lambda a: tuple(a[:, k * wdt:(k + 1) * wdt] for k in range(4))
        h, c, b, z = split(cur)
        hp, cp, _, _ = split(prev)
        hn, cn, bn, zn = split(nxt)
        u_prev = jnp.where(i > 0, cp * hp, 0.0)
        u_ext = jnp.concatenate([u_prev, c * h, cn * hn], axis=0)
        taps = [_shift_down(u_ext, 2 - j)[HALO:, :] for j in range(3)]
        conv = w[0:1, :] * taps[0] + w[1:2, :] * taps[1] + w[2:3, :] * taps[2]
        b_cn = jnp.concatenate([b, bn], axis=0)
        z_cn = jnp.concatenate([z, zn], axis=0)
        dy_cn = jnp.concatenate([dy_ref[...], jnp.where(i < nt - 1, dyn_ref[...], 0.0)], axis=0)
        sg = jax.nn.sigmoid(z_cn)
        silu = z_cn * sg
        d_conv = dy_cn * b_cn * silu
        db = (dy_cn * conv * silu)[:tt, :]
        dz = (dy_cn * b_cn * conv * (sg * (1.0 + z_cn * (1.0 - sg))))[:tt, :]
        du = None
        for j in range(3):
            term = w[j:j + 1, :] * _shift_up(d_conv, 2 - j)
            du = term if du is None else du + term
        du = du[:tt, :]
        d_ref[...] = jnp.concatenate([du * c, du * h, db, dz], axis=1).astype(BF16)

        @pl.when(i == 0)
        def _():
            dw_ref[...] = jnp.zeros_like(dw_ref)

        d_cur = d_conv[:tt, :]
        rows = [jnp.sum(d_cur * taps[j][:tt, :], axis=0, keepdims=True) for j in range(3)]
        dw_ref[0:3, :] += jnp.concatenate(rows, axis=0)

    prev_spec, nxt_spec = _halo_specs(tt, 4 * wdt, 0, nt)
    _, dyn_spec = _halo_specs(tt, wdt, 0, nt)
    return pl.pallas_call(
        body, name="conv_a_bwd", grid=(nt,),
        in_specs=[pl.BlockSpec((tt, 4 * wdt), lambda i: (i, 0)), prev_spec, nxt_spec,
                  pl.BlockSpec((3, wdt), lambda i: (0, 0)), pl.BlockSpec((tt, wdt), lambda i: (i, 0)), dyn_spec],
        out_specs=[pl.BlockSpec((tt, 4 * wdt), lambda i: (i, 0)), pl.BlockSpec((SUBLANES, wdt), lambda i: (0, 0))],
        out_shape=[jax.ShapeDtypeStruct((t, 4 * wdt), BF16), jax.ShapeDtypeStruct((SUBLANES, wdt), F32)],
        compiler_params=_params("arbitrary"))(proj_a, proj_a, proj_a, conv_w, dy, dy)


def _conv_b_fwd(proj_qkv, conv_w):
    t, width = proj_qkv.shape
    tt = min(t, 256)
    nt = t // tt
    wdt = 1024

    def body(cur_ref, prev_ref, w_ref, y_ref):
        i = pl.program_id(1)
        ext = jnp.concatenate([jnp.where(i > 0, prev_ref[...], 0.0), cur_ref[...]], axis=0)
        y_ref[...] = jax.nn.silu(_causal_conv(ext, w_ref[...], 4))

    r = tt // HALO
    return pl.pallas_call(
        body, name="conv_b_fwd", grid=(width // wdt, nt),
        in_specs=[pl.BlockSpec((tt, wdt), lambda j, i: (i, j)),
                  pl.BlockSpec((HALO, wdt), lambda j, i: (jnp.maximum(i * r - 1, 0), j)),
                  pl.BlockSpec((4, wdt), lambda j, i: (0, j))],
        out_specs=pl.BlockSpec((tt, wdt), lambda j, i: (i, j)),
        out_shape=jax.ShapeDtypeStruct((t, width), F32), compiler_params=_params("parallel", "parallel"))(
            proj_qkv, proj_qkv, conv_w)


def _conv_b_bwd(proj_qkv, conv_w, d_act, col, name):
    t = proj_qkv.shape[0]
    tt = min(t, 256)
    nt = t // tt
    wdt = 1024

    def body(cur_ref, prev_ref, nxt_ref, w_ref, da_ref, dan_ref, d_ref, dw_ref):
        i = pl.program_id(0)
        w = w_ref[...]
        u_ext = jnp.concatenate([jnp.where(i > 0, prev_ref[...], 0.0), cur_ref[...], nxt_ref[...]], axis=0)
        taps = [_shift_down(u_ext, 3 - j)[HALO:, :] for j in range(4)]
        conv = w[0:1, :] * taps[0] + w[1:2, :] * taps[1] + w[2:3, :] * taps[2] + w[3:4, :] * taps[3]
        da_cn = jnp.concatenate([da_ref[...], jnp.where(i < nt - 1, dan_ref[...], 0.0)], axis=0)
        sg = jax.nn.sigmoid(conv)
        d_conv = da_cn * (sg * (1.0 + conv * (1.0 - sg)))
        du = None
        for j in range(4):
            term = w[j:j + 1, :] * _shift_up(d_conv, 3 - j)
            du = term if du is None else du + term
        d_ref[...] = du[:tt, :].astype(BF16)

        @pl.when(i == 0)
        def _():
            dw_ref[...] = jnp.zeros_like(dw_ref)

        d_cur = d_conv[:tt, :]
        rows = [jnp.sum(d_cur * taps[j][:tt, :], axis=0, keepdims=True) for j in range(4)]
        dw_ref[0:4, :] += jnp.concatenate(rows, axis=0)

    prev_spec, nxt_spec = _halo_specs(tt, wdt, col, nt)
    _, dan_spec = _halo_specs(tt, wdt, 0, nt)
    return pl.pallas_call(
        body, name=name, grid=(nt,),
        in_specs=[pl.BlockSpec((tt, wdt), lambda i: (i, col)), prev_spec, nxt_spec,
                  pl.BlockSpec((4, wdt), lambda i: (0, col)), pl.BlockSpec((tt, wdt), lambda i: (i, 0)), dan_spec],
        out_specs=[pl.BlockSpec((tt, wdt), lambda i: (i, 0)), pl.BlockSpec((SUBLANES, wdt), lambda i: (0, 0))],
        out_shape=[jax.ShapeDtypeStruct((t, wdt), BF16), jax.ShapeDtypeStruct((SUBLANES, wdt), F32)],
        compiler_params=_params("arbitrary"))(proj_qkv, proj_qkv, proj_qkv, conv_w, d_act, d_act)


def _rms_gate(o, gn, z):
    on = o * lax.rsqrt(jnp.mean(o * o, axis=-1, keepdims=True) + NORM_EPS) * gn
    return on * jax.nn.silu(z)


GDN_PREP_ROWS = 1024


def _unit_lower_inverse(low):
    c = low.shape[-1]
    eye = lax.broadcasted_iota(jnp.int32, low.shape, low.ndim - 2) == lax.broadcasted_iota(jnp.int32, low.shape, low.ndim - 1)
    x = -low
    inv = eye.astype(F32) + x
    for _ in range(c.bit_length() - 2):
        x = _mm_raw(x, x, "nn", THREE_PASS)
        inv = inv + _mm_raw(inv, x, "nn", THREE_PASS)
    return inv


@jax.custom_vjp
def _known_inverse(low, inv):
    return inv


def _known_inverse_fwd(low, inv):
    return inv, inv


def _known_inverse_bwd(inv, d_inv):
    return -_mm_raw(_mm_raw(inv, d_inv, "tn", THREE_PASS), inv, "nt", THREE_PASS), jnp.zeros_like(inv)


_known_inverse.defvjp(_known_inverse_fwd, _known_inverse_bwd)


def _gdn_prep(mm, qa, ka, va, braw, araw, alog, dtb, inv_kept=None):
    n, c, _ = qa.shape
    q = qa * lax.rsqrt(jnp.sum(qa * qa, axis=-1, keepdims=True) + 1e-6) * (HEAD_DIM ** -0.5)
    k = ka * lax.rsqrt(jnp.sum(ka * ka, axis=-1, keepdims=True) + 1e-6)
    beta = jax.nn.sigmoid(braw)
    g = -jnp.exp(alog) * jax.nn.softplus(araw + dtb)
    ri = lax.broadcasted_iota(jnp.int32, (n, c, c), 1)
    ci = lax.broadcasted_iota(jnp.int32, (n, c, c), 2)
    incl, strict, eye = ri >= ci, ri > ci, ri == ci
    gc = mm(incl.astype(F32), g, "nn", EXACT_LHS)
    gc_i = gc[:, :, :c]
    gc_j = mm(jnp.ones((n, c, c), F32), jnp.where(eye, gc_i, 0.0), "nn", EXACT_LHS)
    decay = jnp.where(incl, jnp.exp(jnp.where(incl, gc_i - gc_j, 0.0)), 0.0)
    kb = k * beta
    low = jnp.where(strict, mm(kb, k, "nt", ONE_PASS) * decay, 0.0)
    inv = _unit_lower_inverse(low) if inv_kept is None else _known_inverse(low, inv_kept)
    egc = jnp.exp(gc)
    u = mm(inv, va * beta, "nn", THREE_PASS)
    w = mm(inv, kb * egc, "nn", THREE_PASS)
    attn = jnp.where(incl, mm(q, k, "nt", ONE_PASS) * decay, 0.0)
    g_last = jnp.sum(g, axis=1, keepdims=True)
    outs = (u, w, q * egc, k * jnp.exp(g_last - gc), attn, jnp.exp(g_last))
    return outs + (inv,) if inv_kept is None else outs


def _gdn_scan(mm, u, w, qd, kd, attn, egl, z, gn, state):
    v_new = u - mm(w, state, "nn", ONE_PASS)
    o = mm(qd, state, "nn", ONE_PASS) + mm(attn, v_new, "nn", ONE_PASS)
    new_state = state * egl + mm(kd, v_new, "tn", ONE_PASS)
    return _rms_gate(o, gn, z), new_state


def _chunks(ref_value, n, c):
    return ref_value.reshape(n, c, ref_value.shape[-1])


def _by_head(ref, rows, heads):
    return jnp.stack([ref[rows, pl.ds(h * HEAD_DIM, HEAD_DIM)] for h in range(heads)])


def _store_heads(ref, rows, value):
    for h in range(value.shape[0]):
        ref[rows, pl.ds(h * HEAD_DIM, HEAD_DIM)] = value[h]


def _gdn_prep_specs(tb):
    col = lambda off: pl.BlockSpec((tb, HEAD_DIM), lambda h, i: (i, off + h))
    rep = pl.BlockSpec((1, tb, LANES), lambda h, i: (h, i, 0))
    par = pl.BlockSpec((1, SUBLANES, LANES), lambda h, i: (h, 0, 0))
    att = pl.BlockSpec((1, tb, GDN_CHUNK), lambda h, i: (h, i, 0))
    egl = pl.BlockSpec((1, tb // GDN_CHUNK, SUBLANES, LANES), lambda h, i: (h, i, 0, 0))
    return col, rep, par, att, egl


def _gdn_prep_fwd(qkv_act, braw, araw, alog, dtb):
    t = qkv_act.shape[0]
    tb = min(t, 2 * GDN_PREP_ROWS)
    nt, nc = t // tb, tb // GDN_CHUNK
    width = GDN_HEADS * HEAD_DIM

    def body(q_ref, k_ref, v_ref, br_ref, ar_ref, al_ref, dt_ref, u_ref, w_ref, qd_ref, kd_ref, at_ref, eg_ref, inv_ref):
        ch = lambda r: _chunks(r, nc, GDN_CHUNK)
        u, w, qd, kd, attn, egl, inv = _gdn_prep(_mm_raw, ch(q_ref[...]), ch(k_ref[...]), ch(v_ref[...]), ch(br_ref[0]),
                                                 ch(ar_ref[0]), al_ref[0, 0:1, :], dt_ref[0, 0:1, :])
        u_ref[...] = u.reshape(tb, HEAD_DIM)
        w_ref[...] = w.reshape(tb, HEAD_DIM).astype(BF16)
        qd_ref[...] = qd.reshape(tb, HEAD_DIM).astype(BF16)
        kd_ref[...] = kd.reshape(tb, HEAD_DIM).astype(BF16)
        at_ref[0] = attn.reshape(tb, GDN_CHUNK).astype(BF16)
        eg_ref[0] = jnp.broadcast_to(egl, (nc, SUBLANES, LANES))
        inv_ref[0] = inv.reshape(tb, GDN_CHUNK)

    col, rep, par, att, egl = _gdn_prep_specs(tb)
    h = GDN_HEADS
    return pl.pallas_call(
        body, name="gdn_prep_fwd", grid=(h, nt),
        in_specs=[col(0), col(h), col(2 * h), rep, rep, par, par],
        out_specs=[col(0), col(0), col(0), col(0), att, egl, att],
        out_shape=[jax.ShapeDtypeStruct((t, width), F32)] + [jax.ShapeDtypeStruct((t, width), BF16)] * 3
        + [jax.ShapeDtypeStruct((h, t, GDN_CHUNK), BF16), jax.ShapeDtypeStruct((h, t // GDN_CHUNK, SUBLANES, LANES), F32),
           jax.ShapeDtypeStruct((h, t, GDN_CHUNK), F32)],
        compiler_params=_params("parallel", "parallel"))(qkv_act, qkv_act, qkv_act, braw, araw, alog, dtb)


def _gdn_prep_bwd(qkv_act, braw, araw, alog, dtb, inv, du, dw, dqd, dkd, dattn, degl):
    t = qkv_act.shape[0]
    tb = min(t, GDN_PREP_ROWS)
    nt, nc = t // tb, tb // GDN_CHUNK
    width = GDN_HEADS * HEAD_DIM

    def body(q_ref, k_ref, v_ref, br_ref, ar_ref, al_ref, dt_ref, inv_ref, du_ref, dw_ref, dqd_ref, dkd_ref, dat_ref,
             deg_ref, dq_ref, dk_ref, dv_ref, dbr_ref, dar_ref, dal_ref, ddt_ref):
        @pl.when(pl.program_id(1) == 0)
        def _():
            dal_ref[...] = jnp.zeros_like(dal_ref)
            ddt_ref[...] = jnp.zeros_like(ddt_ref)

        ch = lambda r: _chunks(r, nc, GDN_CHUNK)
        _, vjp = jax.vjp(functools.partial(_gdn_prep, _mm_vjp, inv_kept=ch(inv_ref[0])), ch(q_ref[...]), ch(k_ref[...]),
                         ch(v_ref[...]), ch(br_ref[0]), ch(ar_ref[0]), al_ref[0, 0:1, :], dt_ref[0, 0:1, :])
        dq, dk, dv, dbr, dar, dal, ddt = vjp((ch(du_ref[...]), ch(dw_ref[...]), ch(dqd_ref[...]), ch(dkd_ref[...]),
                                              ch(dat_ref[0]), deg_ref[0][:, 0:1, :]))
        dq_ref[...] = dq.reshape(tb, HEAD_DIM)
        dk_ref[...] = dk.reshape(tb, HEAD_DIM)
        dv_ref[...] = dv.reshape(tb, HEAD_DIM)
        dbr_ref[0] = _lane_total(dbr.reshape(tb, LANES))
        dar_ref[0] = _lane_total(dar.reshape(tb, LANES))
        dal_ref[0, 0:1, :] += _lane_total(dal)
        ddt_ref[0, 0:1, :] += _lane_total(ddt)

    col, rep, par, att, egl = _gdn_prep_specs(tb)
    h = GDN_HEADS
    return pl.pallas_call(
        body, name="gdn_prep_bwd", grid=(h, nt),
        in_specs=[col(0), col(h), col(2 * h), rep, rep, par, par, att, col(0), col(0), col(0), col(0), att, egl],
        out_specs=[col(0), col(0), col(0), rep, rep, par, par],
        out_shape=[jax.ShapeDtypeStruct((t, width), F32)] * 3 + [jax.ShapeDtypeStruct((h, t, LANES), F32)] * 2
        + [jax.ShapeDtypeStruct((h, SUBLANES, LANES), F32)] * 2,
        compiler_params=_params("parallel", "arbitrary"))(qkv_act, qkv_act, qkv_act, braw, araw, alog, dtb, inv,
                                                         du, dw, dqd, dkd, dattn, degl)


def _scan_specs(tb, heads, chunk, rev, nt):
    ti = (lambda i: nt - 1 - i) if rev else (lambda i: i)
    row = pl.BlockSpec((tb, heads * HEAD_DIM), lambda i: (ti(i), 0))
    att = pl.BlockSpec((heads, tb, chunk), lambda i: (0, ti(i), 0))
    egl = pl.BlockSpec((heads, tb // chunk, SUBLANES, LANES), lambda i: (0, ti(i), 0, 0))
    hist = pl.BlockSpec((heads, tb // chunk, HEAD_DIM, HEAD_DIM), lambda i: (0, ti(i), 0, 0))
    gn = pl.BlockSpec((SUBLANES, LANES), lambda i: (0, 0))
    return row, att, egl, hist, gn


def _gdn_scan_fwd(u, w, qd, kd, attn, egl, zb, gn):
    t = u.shape[0]
    tb = min(t, 256)
    nt, nc = t // tb, tb // GDN_CHUNK
    nh = GDN_HEADS

    def body(u_ref, w_ref, qd_ref, kd_ref, at_ref, eg_ref, z_ref, gn_ref, y_ref, hist_ref, s_ref):
        @pl.when(pl.program_id(0) == 0)
        def _():
            s_ref[...] = jnp.zeros_like(s_ref)

        g = gn_ref[0:1, :]
        state = s_ref[...]
        for c in range(nc):
            rows = pl.ds(c * GDN_CHUNK, GDN_CHUNK)
            heads = lambda r: _by_head(r, rows, nh)
            hist_ref[:, c] = state
            y, state = _gdn_scan(_mm_raw, heads(u_ref), heads(w_ref), heads(qd_ref), heads(kd_ref), at_ref[:, rows, :],
                                 eg_ref[:, c, 0:1, :], heads(z_ref), g, state)
            _store_heads(y_ref, rows, y.astype(BF16))
        s_ref[...] = state

    row, att, egs, hist, gns = _scan_specs(tb, nh, GDN_CHUNK, False, nt)
    return pl.pallas_call(
        body, name="gdn_scan_fwd", grid=(nt,), in_specs=[row, row, row, row, att, egs, row, gns], out_specs=[row, hist],
        out_shape=[jax.ShapeDtypeStruct((t, nh * HEAD_DIM), BF16),
                   jax.ShapeDtypeStruct((nh, t // GDN_CHUNK, HEAD_DIM, HEAD_DIM), F32)],
        scratch_shapes=[pltpu.VMEM((nh, HEAD_DIM, HEAD_DIM), F32)],
        compiler_params=_params("arbitrary"))(u, w, qd, kd, attn, egl, zb, gn)


def _gdn_scan_bwd(u, w, qd, kd, attn, egl, zb, gn, hist, dy):
    t = u.shape[0]
    tb = min(t, 256)
    nt, nc = t // tb, tb // GDN_CHUNK
    nh = GDN_HEADS

    def body(u_ref, w_ref, qd_ref, kd_ref, at_ref, eg_ref, z_ref, gn_ref, hist_ref, dy_ref,
             du_ref, dw_ref, dqd_ref, dkd_ref, dat_ref, deg_ref, dz_ref, dgn_ref, ds_ref):
        @pl.when(pl.program_id(0) == 0)
        def _():
            ds_ref[...] = jnp.zeros_like(ds_ref)
            dgn_ref[...] = jnp.zeros_like(dgn_ref)

        g = gn_ref[0:1, :]
        d_state = ds_ref[...]
        for c in reversed(range(nc)):
            rows = pl.ds(c * GDN_CHUNK, GDN_CHUNK)
            heads = lambda r: _by_head(r, rows, nh).astype(F32)
            _, vjp = jax.vjp(functools.partial(_gdn_scan, _mm_vjp), heads(u_ref), heads(w_ref), heads(qd_ref),
                             heads(kd_ref), at_ref[:, rows, :].astype(F32), eg_ref[:, c, 0:1, :], heads(z_ref), g,
                             hist_ref[:, c])
            du, dw, dqd, dkd, dat, deg, dz, dgn, d_state = vjp((heads(dy_ref), d_state))
            _store_heads(du_ref, rows, du)
            _store_heads(dw_ref, rows, dw)
            _store_heads(dqd_ref, rows, dqd)
            _store_heads(dkd_ref, rows, dkd)
            dat_ref[:, rows, :] = dat
            deg_ref[:, c] = jnp.broadcast_to(deg, (nh, SUBLANES, LANES))
            _store_heads(dz_ref, rows, dz.astype(BF16))
            dgn_ref[0:1, :] += dgn
        ds_ref[...] = d_state

    row, att, egs, hists, gns = _scan_specs(tb, nh, GDN_CHUNK, True, nt)
    wide = jax.ShapeDtypeStruct((t, nh * HEAD_DIM), F32)
    return pl.pallas_call(
        body, name="gdn_scan_bwd", grid=(nt,),
        in_specs=[row, row, row, row, att, egs, row, gns, hists, row],
        out_specs=[row, row, row, row, att, egs, row, gns],
        out_shape=[wide] * 4 + [jax.ShapeDtypeStruct((nh, t, GDN_CHUNK), F32),
                                jax.ShapeDtypeStruct((nh, t // GDN_CHUNK, SUBLANES, LANES), F32),
                                jax.ShapeDtypeStruct((t, nh * HEAD_DIM), BF16),
                                jax.ShapeDtypeStruct((SUBLANES, LANES), F32)],
        scratch_shapes=[pltpu.VMEM((nh, HEAD_DIM, HEAD_DIM), F32)],
        compiler_params=_params("arbitrary"))(u, w, qd, kd, attn, egl, zb, gn, hist, dy)


def _hgrn_prep(mm, qr, fr, lbl):
    n, c, _ = qr.shape
    lb = jax.nn.sigmoid(lbl[1:2, :] - lbl[0:1, :])
    f = lb + (1.0 - lb) * jax.nn.sigmoid(fr)
    q = jax.nn.silu(qr)
    k = 1.0 - f
    logf = jnp.log(f)
    ri = lax.broadcasted_iota(jnp.int32, (n, c, c), 1)
    ci = lax.broadcasted_iota(jnp.int32, (n, c, c), 2)
    b = mm((ri >= ci).astype(F32), logf, "nn", EXACT_LHS)
    attn = _hgrn_attn(mm, q, k, b)
    b_last = jnp.sum(logf, axis=1, keepdims=True)
    return q * jnp.exp(b), k * jnp.exp(b_last - b), attn, jnp.exp(b_last)


HGRN_SUB = 8
HGRN_PREP_ROWS = 2048


@functools.partial(jax.custom_vjp, nondiff_argnums=(1,))
def _roll_rows(x, shift):
    return pltpu.roll(x, shift, x.ndim - 2)


def _roll_rows_fwd(x, shift):
    return _roll_rows(x, shift), None


def _roll_rows_bwd(shift, _, d):
    return (pltpu.roll(d, d.shape[-2] - shift, d.ndim - 2),)


_roll_rows.defvjp(_roll_rows_fwd, _roll_rows_bwd)


@jax.custom_vjp
def _exp_clamped(v):
    return jnp.exp(jnp.minimum(v, 0.0))


def _exp_clamped_fwd(v):
    out = jnp.exp(jnp.minimum(v, 0.0))
    return out, out


def _exp_clamped_bwd(out, d):
    return (d * out,)


_exp_clamped.defvjp(_exp_clamped_fwd, _exp_clamped_bwd)


def _hgrn_attn(mm, q, k, b):
    n, c, d = q.shape
    sb = HGRN_SUB
    sub = lambda a: a.reshape(n * c // sb, sb, d)
    qs, ks, bs = sub(q), sub(k), sub(b)
    row = lax.broadcasted_iota(jnp.int32, (n, c, c), 1)
    col = lax.broadcasted_iota(jnp.int32, (n, c, c), 2)
    same_block = (row & -sb) == (col & -sb)
    attn = None
    for delta in range(sb):
        if delta == 0:
            prod = qs * ks
        else:
            prod = qs * _roll_rows(ks, delta) * _exp_clamped(bs - _roll_rows(bs, delta))
        sums = jnp.sum(prod, axis=-1, keepdims=True).reshape(n, c, 1)
        term = jnp.where(same_block & (row - col == delta), sums, 0.0)
        attn = term if attn is None else attn + term
    far = [jnp.zeros((n, sb, c), F32)]
    for i in range(1, c // sb):
        r0 = i * sb
        bi = b[:, r0:r0 + sb, :]
        ref = bi[:, 0:1, :]
        before = jnp.concatenate([k[:, :r0, :] * jnp.exp(ref - b[:, :r0, :]), jnp.zeros((n, c - r0, d), F32)], axis=1)
        far.append(mm(q[:, r0:r0 + sb, :] * jnp.exp(bi - ref), before, "nt", ONE_PASS))
    return attn + jnp.concatenate(far, axis=1)


def _hgrn_scan(mm, qe, kd, attn, ebl, iv, z, gn, state):
    o = mm(qe, state, "nt", ONE_PASS) + mm(attn, iv, "nn", ONE_PASS)
    new_state = state * ebl + mm(iv, kd, "tn", ONE_PASS)
    return _rms_gate(o, gn, z), new_state


def _hgrn_prep_specs(tb):
    col = pl.BlockSpec((tb, HEAD_DIM), lambda h, i: (i, h))
    lbs = pl.BlockSpec((2, HEAD_DIM), lambda h, i: (0, h))
    att = pl.BlockSpec((1, tb, HGRN_CHUNK), lambda h, i: (h, i, 0))
    ebl = pl.BlockSpec((1, tb // HGRN_CHUNK, SUBLANES, LANES), lambda h, i: (h, i, 0, 0))
    return col, lbs, att, ebl


def _hgrn_prep_fwd(qr, fr, lower_bounds):
    t = qr.shape[0]
    tb = min(t, HGRN_PREP_ROWS)
    nt, nc = t // tb, tb // HGRN_CHUNK
    hh = HGRN_HEADS

    def body(q_ref, f_ref, lb_ref, qe_ref, kd_ref, at_ref, eb_ref):
        ch = lambda r: _chunks(r, nc, HGRN_CHUNK)
        qe, kd, attn, ebl = _hgrn_prep(_mm_raw, ch(q_ref[...]), ch(f_ref[...]), lb_ref[...])
        qe_ref[...] = qe.reshape(tb, HEAD_DIM).astype(BF16)
        kd_ref[...] = kd.reshape(tb, HEAD_DIM).astype(BF16)
        at_ref[0] = attn.reshape(tb, HGRN_CHUNK).astype(BF16)
        eb_ref[0] = jnp.broadcast_to(ebl, (nc, SUBLANES, LANES))

    col, lbs, att, ebs = _hgrn_prep_specs(tb)
    return pl.pallas_call(
        body, name="hgrn_prep_fwd", grid=(hh, nt), in_specs=[col, col, lbs], out_specs=[col, col, att, ebs],
        out_shape=[jax.ShapeDtypeStruct((t, HGRN_WIDTH), BF16)] * 2
        + [jax.ShapeDtypeStruct((hh, t, HGRN_CHUNK), BF16), jax.ShapeDtypeStruct((hh, t // HGRN_CHUNK, SUBLANES, LANES), F32)],
        compiler_params=_params("parallel", "parallel"))(qr, fr, lower_bounds)


def _hgrn_prep_bwd(qr, fr, lower_bounds, dqe, dkd, dattn, debl):
    t = qr.shape[0]
    tb = min(t, HGRN_PREP_ROWS)
    nt, nc = t // tb, tb // HGRN_CHUNK
    hh = HGRN_HEADS

    def body(q_ref, f_ref, lb_ref, dqe_ref, dkd_ref, dat_ref, deb_ref, dq_ref, df_ref, dlb_ref):
        @pl.when(pl.program_id(1) == 0)
        def _():
            dlb_ref[...] = jnp.zeros_like(dlb_ref)

        ch = lambda r: _chunks(r, nc, HGRN_CHUNK)
        _, vjp = jax.vjp(functools.partial(_hgrn_prep, _mm_vjp), ch(q_ref[...]), ch(f_ref[...]), lb_ref[...])
        dq, df, dlb = vjp((ch(dqe_ref[...]), ch(dkd_ref[...]), ch(dat_ref[0]), deb_ref[0][:, 0:1, :]))
        dq_ref[...] = dq.reshape(tb, HEAD_DIM).astype(BF16)
        df_ref[...] = df.reshape(tb, HEAD_DIM).astype(BF16)
        dlb_ref[...] += dlb

    col, lbs, att, ebs = _hgrn_prep_specs(tb)
    return pl.pallas_call(
        body, name="hgrn_prep_bwd", grid=(hh, nt), in_specs=[col, col, lbs, col, col, att, ebs],
        out_specs=[col, col, lbs],
        out_shape=[jax.ShapeDtypeStruct((t, HGRN_WIDTH), BF16)] * 2 + [jax.ShapeDtypeStruct((2, HGRN_WIDTH), F32)],
        compiler_params=_params("parallel", "arbitrary"))(qr, fr, lower_bounds, dqe, dkd, dattn, debl)


def _hgrn_scan_fwd(qe, kd, attn, ebl, iv, z, gn):
    t = qe.shape[0]
    tb = min(t, 128)
    nt, nc = t // tb, tb // HGRN_CHUNK
    hh = HGRN_HEADS

    def body(qe_ref, kd_ref, at_ref, eb_ref, i_ref, z_ref, gn_ref, y_ref, hist_ref, s_ref):
        @pl.when(pl.program_id(0) == 0)
        def _():
            s_ref[...] = jnp.zeros_like(s_ref)

        g = gn_ref[0:1, :]
        state = s_ref[...]
        for c in range(nc):
            rows = pl.ds(c * HGRN_CHUNK, HGRN_CHUNK)
            heads = lambda r: _by_head(r, rows, hh)
            hist_ref[:, c] = state
            y, state = _hgrn_scan(_mm_raw, heads(qe_ref), heads(kd_ref), at_ref[:, rows, :], eb_ref[:, c, 0:1, :],
                                  heads(i_ref), heads(z_ref), g, state)
            _store_heads(y_ref, rows, y.astype(BF16))
        s_ref[...] = state

    row, att, ebs, hist, gns = _scan_specs(tb, hh, HGRN_CHUNK, False, nt)
    return pl.pallas_call(
        body, name="hgrn_scan_fwd", grid=(nt,), in_specs=[row, row, att, ebs, row, row, gns], out_specs=[row, hist],
        out_shape=[jax.ShapeDtypeStruct((t, HGRN_WIDTH), BF16),
                   jax.ShapeDtypeStruct((hh, t // HGRN_CHUNK, HEAD_DIM, HEAD_DIM), F32)],
        scratch_shapes=[pltpu.VMEM((hh, HEAD_DIM, HEAD_DIM), F32)],
        compiler_params=_params("arbitrary"))(qe, kd, attn, ebl, iv, z, gn)


def _hgrn_scan_bwd(qe, kd, attn, ebl, iv, z, gn, hist, dy):
    t = qe.shape[0]
    tb = min(t, 128)
    nt, nc = t // tb, tb // HGRN_CHUNK
    hh = HGRN_HEADS

    def body(qe_ref, kd_ref, at_ref, eb_ref, i_ref, z_ref, gn_ref, hist_ref, dy_ref,
             dqe_ref, dkd_ref, dat_ref, deb_ref, di_ref, dz_ref, dgn_ref, ds_ref):
        @pl.when(pl.program_id(0) == 0)
        def _():
            ds_ref[...] = jnp.zeros_like(ds_ref)
            dgn_ref[...] = jnp.zeros_like(dgn_ref)

        g = gn_ref[0:1, :]
        d_state = ds_ref[...]
        for c in reversed(range(nc)):
            rows = pl.ds(c * HGRN_CHUNK, HGRN_CHUNK)
            heads = lambda r: _by_head(r, rows, hh).astype(F32)
            _, vjp = jax.vjp(functools.partial(_hgrn_scan, _mm_vjp), heads(qe_ref), heads(kd_ref),
                             at_ref[:, rows, :].astype(F32), eb_ref[:, c, 0:1, :], heads(i_ref), heads(z_ref), g,
                             hist_ref[:, c])
            dqe, dkd, dat, deb, di, dz, dgn, d_state = vjp((heads(dy_ref), d_state))
            _store_heads(dqe_ref, rows, dqe)
            _store_heads(dkd_ref, rows, dkd)
            dat_ref[:, rows, :] = dat
            deb_ref[:, c] = jnp.broadcast_to(deb, (hh, SUBLANES, LANES))
            _store_heads(di_ref, rows, di.astype(BF16))
            _store_heads(dz_ref, rows, dz.astype(BF16))
            dgn_ref[0:1, :] += dgn
        ds_ref[...] = d_state

    row, att, ebs, hists, gns = _scan_specs(tb, hh, HGRN_CHUNK, True, nt)
    wide = lambda dt: jax.ShapeDtypeStruct((t, HGRN_WIDTH), dt)
    return pl.pallas_call(
        body, name="hgrn_scan_bwd", grid=(nt,),
        in_specs=[row, row, att, ebs, row, row, gns, hists, row],
        out_specs=[row, row, att, ebs, row, row, gns],
        out_shape=[wide(F32), wide(F32), jax.ShapeDtypeStruct((hh, t, HGRN_CHUNK), F32),
                   jax.ShapeDtypeStruct((hh, t // HGRN_CHUNK, SUBLANES, LANES), F32), wide(BF16), wide(BF16),
                   jax.ShapeDtypeStruct((SUBLANES, LANES), F32)],
        scratch_shapes=[pltpu.VMEM((hh, HEAD_DIM, HEAD_DIM), F32)],
        compiler_params=_params("arbitrary"))(qe, kd, attn, ebl, iv, z, gn, hist, dy)


def _layer_norm(pre, g, b):
    mu = jnp.mean(pre, axis=-1, keepdims=True)
    d = pre - mu
    var = jnp.mean(d * d, axis=-1, keepdims=True)
    return d * lax.rsqrt(var + NORM_EPS) * g + b


def _lnpl_fwd(xin, s, p, wg, wpl, ln_g, ln_b):
    t = xin.shape[0]
    tt = min(t, 512)

    def body(x_ref, s_ref, p_ref, wg_ref, wpl_ref, g_ref, b_ref, o_ref, ob_ref):
        xn = _layer_norm(DEEPNORM_ALPHA * x_ref[...] + s_ref[...], g_ref[...], b_ref[...])
        gate = jax.nn.sigmoid(_mm_raw(xn, wg_ref[...], "nn", False))
        out = xn + _mm_raw(p_ref[...], wpl_ref[...], "nn", False) * gate
        o_ref[...] = out
        ob_ref[...] = out.astype(BF16)

    row = lambda w: pl.BlockSpec((tt, w), lambda i: (i, 0))
    full = lambda a: pl.BlockSpec(a.shape, lambda i: (0, 0))
    return pl.pallas_call(
        body, name="lnpl_fwd", grid=(t // tt,),
        in_specs=[row(D_MODEL), row(D_MODEL), row(PL_DIM), full(wg), full(wpl), full(ln_g), full(ln_b)],
        out_specs=[row(D_MODEL), row(D_MODEL)],
        out_shape=[jax.ShapeDtypeStruct((t, D_MODEL), F32), jax.ShapeDtypeStruct((t, D_MODEL), BF16)],
        compiler_params=_params("parallel"))(xin, s, p, wg, wpl, ln_g, ln_b)


def _lnpl_bwd(xin, s, p, wg, wpl, ln_g, ln_b, upstream, last, name):
    t = xin.shape[0]
    tt = min(t, 512)

    def body(x_ref, s_ref, p_ref, wg_ref, wpl_ref, g_ref, b_ref, up_ref,
             dpre_ref, dwg_ref, dwpl_ref, dg_ref, db_ref, loss_ref):
        @pl.when(pl.program_id(0) == 0)
        def _():
            for r in (dwg_ref, dwpl_ref, dg_ref, db_ref, loss_ref):
                r[...] = jnp.zeros_like(r)

        pre = DEEPNORM_ALPHA * x_ref[...] + s_ref[...]
        xn, ln_vjp = jax.vjp(_layer_norm, pre, g_ref[...], b_ref[...])
        gate = jax.nn.sigmoid(_mm_raw(xn, wg_ref[...], "nn", False))
        plv = _mm_raw(p_ref[...], wpl_ref[...], "nn", False)
        if last:
            err = xn + plv * gate - up_ref[...]
            dout = err * (1.0 / D_MODEL)
            tot = jnp.sum(jnp.sum(err * err, axis=1, keepdims=True), axis=0, keepdims=True) * (0.5 / D_MODEL)
            loss_ref[...] += jnp.broadcast_to(tot, loss_ref.shape)
        else:
            dout = up_ref[...]
        dplv = dout * gate
        dlogits = dout * plv * gate * (1.0 - gate)
        dwg_ref[...] += _mm_raw(xn, dlogits, "tn", False)
        dwpl_ref[...] += _mm_raw(p_ref[...], dplv, "tn", False)
        dxn = dout + _mm_raw(dlogits, wg_ref[...], "nt", False)
        dpre, dg, db = ln_vjp(dxn)
        dpre_ref[...] = dpre
        dg_ref[...] += dg
        db_ref[...] += db

    row = lambda w: pl.BlockSpec((tt, w), lambda i: (i, 0))
    full = lambda shape: pl.BlockSpec(shape, lambda i: (0, 0))
    return pl.pallas_call(
        body, name=name, grid=(t // tt,),
        in_specs=[row(D_MODEL), row(D_MODEL), row(PL_DIM), full(wg.shape), full(wpl.shape), full(ln_g.shape),
                  full(ln_b.shape), row(D_MODEL)],
        out_specs=[row(D_MODEL), full(wg.shape), full(wpl.shape), full(ln_g.shape), full(ln_b.shape),
                   full((SUBLANES, LANES))],
        out_shape=[jax.ShapeDtypeStruct((t, D_MODEL), F32), jax.ShapeDtypeStruct(wg.shape, F32),
                   jax.ShapeDtypeStruct(wpl.shape, F32), jax.ShapeDtypeStruct(ln_g.shape, F32),
                   jax.ShapeDtypeStruct(ln_b.shape, F32), jax.ShapeDtypeStruct((SUBLANES, LANES), F32)],
        compiler_params=_params("arbitrary"))(xin, s, p, wg, wpl, ln_g, ln_b, upstream)


def _pack_tail(dbr, dar):
    nh, t, _ = dbr.shape
    tt = min(t, 512)

    def body(b_ref, a_ref, o_ref):
        lane = lax.broadcasted_iota(jnp.int32, (tt, LANES), 1)
        acc = jnp.zeros((tt, LANES), F32)
        for h in range(nh):
            acc = jnp.where(lane == h, b_ref[h], acc)
            acc = jnp.where(lane == nh + h, a_ref[h], acc)
        o_ref[...] = acc.astype(BF16)

    spec = pl.BlockSpec((nh, tt, LANES), lambda i: (0, i, 0))
    return pl.pallas_call(
        body, name="pack_tail", grid=(t // tt,), in_specs=[spec, spec], out_specs=pl.BlockSpec((tt, LANES), lambda i: (i, 0)),
        out_shape=jax.ShapeDtypeStruct((t, LANES), BF16), compiler_params=_params("parallel"))(dbr, dar)


def _rep_rows(v):
    return jnp.broadcast_to(v.reshape(1, LANES), (SUBLANES, LANES))


def _rep_heads(v):
    return jnp.broadcast_to(v.reshape(-1, 1, 1), (v.shape[0], SUBLANES, LANES))


def _col_range(stacked, lo, hi):
    c = stacked.shape[2]
    parts = [stacked[s, :, max(lo, s * c) - s * c:min(hi, (s + 1) * c) - s * c]
             for s in range(4) if max(lo, s * c) < min(hi, (s + 1) * c)]
    return parts[0] if len(parts) == 1 else jnp.concatenate(parts, axis=1)


def _col_shards(pieces, c):
    shards, offs, o = [], [], 0
    for pc in pieces:
        offs.append(o)
        o += pc.shape[1]
    for s in range(4):
        lo, hi = s * c, (s + 1) * c
        parts = [pc[:, max(lo, o) - o:min(hi, o + pc.shape[1]) - o] for pc, o in zip(pieces, offs)
                 if max(lo, o) < min(hi, o + pc.shape[1])]
        shards.append(parts[0] if len(parts) == 1 else jnp.concatenate(parts, axis=1))
    return jnp.stack(shards)


def _local_step(x, p, target, w, late_weights, early_grads_ready, early_grads_swapped, last_grad_ready, start_token):
    a = DEEPNORM_ALPHA
    nh = GDN_HEADS
    xb = (x + start_token).astype(BF16)
    wie = w["w_in_even"]
    w_a, w_qkv, w_zb = _col_range(wie, 0, 4096), _col_range(wie, 4096, 7168), _col_range(wie, 7168, 8192)
    w_tail = jnp.pad(_col_range(wie, 8192, 8192 + 2 * nh), ((0, 0), (0, LANES - 2 * nh)))
    conv_a_w, conv_b_w = w["conv_a_w"], w["conv_b_w"]
    ln_g0, ln_b0, ln_g1, ln_b1 = (v.reshape(1, D_MODEL) for v in (w["ln_g"][0], w["ln_b"][0], w["ln_g"][1], w["ln_b"][1]))
    alog, dtb = _rep_heads(w["a_log"].reshape(nh)), _rep_heads(w["dt_bias"].reshape(nh))
    gdn_g, hgrn_g = _rep_rows(w["gdn_norm_g"]), _rep_rows(w["hgrn_norm_g"])

    proj_a = _matmul(xb, w_a, name="fwd_proj_a")
    proj_qkv = _matmul(xb, w_qkv, name="fwd_proj_qkv")
    proj_zb = _matmul(xb, w_zb, name="fwd_proj_zb")
    proj_tail = _matmul(xb, w_tail, name="fwd_proj_tail")
    rep = lambda cols: jnp.broadcast_to(cols.T[:, :, None], (nh, cols.shape[0], LANES))
    braw, araw = rep(proj_tail[:, :nh]), rep(proj_tail[:, nh:2 * nh])
    y_a = _conv_a_fwd(proj_a, conv_a_w)
    qkv_act = _conv_b_fwd(proj_qkv, conv_b_w)
    *gdn_pre, gdn_inv = _gdn_prep_fwd(qkv_act, braw, araw, alog, dtb)
    y_b, gdn_hist = _gdn_scan_fwd(*gdn_pre, proj_zb, gdn_g)
    w = {**w, **late_weights(y_b)}
    woe, wio, woo = w["w_out_even"], w["w_in_odd"], w["w_out_odd"]
    s0 = _matmul(y_b, woe[1024:], name="fwd_out_even_b", add=_matmul(y_a, woe[:1024], name="fwd_out_even_a"))
    x1, x1b = _lnpl_fwd(x, s0, p[0], w["w_pl_gate"][0], w["w_pl"][0], ln_g0, ln_b0)
    proj_o = [_matmul(x1b, wio[j], name=f"fwd_proj_odd{j}") for j in range(4)]
    hgrn_pre = _hgrn_prep_fwd(proj_o[0], proj_o[1], w["lower_bounds"])
    y_o, hgrn_hist = _hgrn_scan_fwd(*hgrn_pre, proj_o[2], proj_o[3], hgrn_g)
    s1 = _matmul(y_o, woo, name="fwd_out_odd")

    g = {}
    dpre1, dwg1, dwpl1, dlng1, dlnb1, loss = _lnpl_bwd(x1, s1, p[1], w["w_pl_gate"][1], w["w_pl"][1], ln_g1, ln_b1,
                                                     target, True, "lnpl_bwd_odd")
    dy_o = _matmul(dpre1, woo, tb=True, name="bwd_out_odd_dx")
    g["w_out_odd"] = _matmul(y_o, dpre1, ta=True, name="bwd_out_odd_dw")
    dqe, dkd, dat, deb, di, dz, dhg = _hgrn_scan_bwd(*hgrn_pre, proj_o[2], proj_o[3], hgrn_g, hgrn_hist, dy_o)
    dq, df, dlb = _hgrn_prep_bwd(proj_o[0], proj_o[1], w["lower_bounds"], dqe, dkd, dat, deb)
    dx1 = dpre1
    scale = a
    dws = []
    for j, dj in enumerate((dq, df, di, dz)):
        dx1 = _matmul(dj, wio[j], tb=True, add=dx1, add_scale=scale, name=f"bwd_proj_odd_dx{j}")
        scale = 1.0
        dws.append(_matmul(x1b, dj, ta=True, name=f"bwd_proj_odd_dw{j}"))
    g["w_in_odd"] = jnp.stack(dws)
    g["hgrn_norm_g"] = dhg[0:1]
    g["lower_bounds"] = dlb
    g["w_pl_gate1"], g["w_pl1"] = dwg1, dwpl1

    dpre0, dwg0, dwpl0, dlng0, dlnb0, _ = _lnpl_bwd(x, s0, p[0], w["w_pl_gate"][0], w["w_pl"][0], ln_g0, ln_b0,
                                                  dx1, False, "lnpl_bwd_even")
    g["w_pl_gate0"], g["w_pl0"] = dwg0, dwpl0
    g["ln_g"] = jnp.concatenate([dlng0, dlng1], axis=0)
    g["ln_b"] = jnp.concatenate([dlnb0, dlnb1], axis=0)
    dy_a = _matmul(dpre0, woe[:1024], tb=True, name="bwd_out_even_dxa")
    dy_b = _matmul(dpre0, woe[1024:], tb=True, name="bwd_out_even_dxb")
    g["w_out_even"] = jnp.concatenate([_matmul(y_a, dpre0, ta=True, name="bwd_out_even_dwa"),
                                       _matmul(y_b, dpre0, ta=True, name="bwd_out_even_dwb")], axis=0)
    token = early_grads_ready({n: g[n] for n in _RS_EARLY})
    d_a, dwa = _conv_a_bwd(proj_a, conv_a_w + token, dy_a)
    g["conv_a_w"] = dwa[:3]
    gdn_g = gdn_g + early_grads_swapped(d_a)
    du, dw, dqd, dkd, dat, deg, dzb, dgn = _gdn_scan_bwd(*gdn_pre, proj_zb, gdn_g, gdn_hist, dy_b)
    dqa, dka, dva, dbr, dar, dal, ddt = _gdn_prep_bwd(qkv_act, braw, araw, alog, dtb, gdn_inv, du, dw, dqd, dkd, dat, deg)
    g["a_log"] = dal[:, 0, 0].reshape(1, nh)
    g["dt_bias"] = ddt[:, 0, 0].reshape(1, nh)
    g["gdn_norm_g"] = dgn[0:1]
    d_pre_qkv, dwb = [], []
    for j, dj in enumerate((dqa, dka, dva)):
        dpj, dwj = _conv_b_bwd(proj_qkv, conv_b_w, dj, j, f"conv_b_bwd{j}")
        d_pre_qkv.append(dpj)
        dwb.append(dwj[:4])
    g["conv_b_w"] = jnp.concatenate(dwb, axis=1)
    d_tail = _pack_tail(dbr, dar)
    pieces = [(d_a, w_a), (d_pre_qkv[0], w_qkv[:, :1024]), (d_pre_qkv[1], w_qkv[:, 1024:2048]),
              (d_pre_qkv[2], w_qkv[:, 2048:]), (dzb, w_zb), (d_tail, w_tail)]
    dws = [_matmul(xb, dj, ta=True, name=f"bwd_proj_even_dw{j}") for j, (dj, _) in enumerate(pieces)]
    dws[-1] = dws[-1][:, :2 * nh]
    g["w_in_even"] = _col_shards(dws, wie.shape[2])
    token = last_grad_ready(g["w_in_even"])
    dx = dpre0
    scale = a
    for j, (dj, wj) in enumerate(pieces):
        dx = _matmul(dj, wj + jnp.asarray(token).astype(BF16) if j == 0 else wj, tb=True, add=dx, add_scale=scale,
                     name=f"bwd_proj_even_dx{j}")
        scale = 1.0
    return loss, dx, g


def _adamw(w, g, m, v, name):
    lead, rows, cols = w.shape
    if rows % SUBLANES == 0 or rows <= 256:
        tr, tc = (rows if rows <= 256 else 256), cols
    else:
        tr, tc = rows, 256
    assert rows % tr == 0 and cols % tc == 0, (name, rows, cols)

    def body(w_ref, g_ref, m_ref, v_ref, d_ref, nm_ref, nv_ref):
        gg = g_ref[...]
        nm = ADAM_B1 * m_ref[...] + (1.0 - ADAM_B1) * gg
        nv = ADAM_B2 * v_ref[...] + (1.0 - ADAM_B2) * jnp.square(gg)
        m_hat = nm / (1.0 - ADAM_B1 ** ADAM_STEP)
        v_hat = nv / (1.0 - ADAM_B2 ** ADAM_STEP)
        d_ref[...] = -ADAM_LR * (m_hat / (jnp.sqrt(v_hat) + ADAM_EPS) + ADAM_WD * w_ref[...])
        nm_ref[...] = nm
        nv_ref[...] = nv

    spec = pl.BlockSpec((1, tr, tc), lambda l, i, j: (l, i, j))
    return pl.pallas_call(
        body, name=name, grid=(lead, rows // tr, cols // tc), in_specs=[spec] * 4, out_specs=[spec] * 3,
        out_shape=[jax.ShapeDtypeStruct(w.shape, F32)] * 3,
        compiler_params=_params("parallel", "parallel", "parallel"))(w, g, m, v)


MESH = pl.DeviceIdType.MESH
N_DEV = 8
HBM_SPEC = pl.BlockSpec(memory_space=pltpu.HBM)
VMEM_SPEC = pl.BlockSpec(memory_space=pltpu.VMEM)


def _coords():
    return lax.axis_index("x"), lax.axis_index("y"), lax.axis_index("c")


def _flip(v, bit):
    return 1 - v if bit else v


def _remote(src, dst, send_sem, recv_sem, dev):
    return pltpu.make_async_remote_copy(src_ref=src, dst_ref=dst, send_sem=send_sem, recv_sem=recv_sem,
                                        device_id=dev, device_id_type=MESH)


def _exchange_small(buf, reduce, name):
    rows = buf.shape[0]

    def body(in_ref, out_ref, slots, send_sems, recv_sems):
        x, y, c = _coords()
        me = 4 * x + 2 * y + c
        slots[me] = in_ref[...]
        peer = lambda k: (_flip(x, (k >> 2) & 1), _flip(y, (k >> 1) & 1), _flip(c, k & 1))
        sends = []
        for k in range(1, N_DEV):
            cp = _remote(in_ref, slots.at[me], send_sems.at[k - 1], recv_sems.at[k - 1], peer(k))
            cp.start()
            sends.append(cp)
        for k in range(1, N_DEV):
            px, py, pc = peer(k)
            _remote(in_ref, slots.at[4 * px + 2 * py + pc], send_sems.at[k - 1], recv_sems.at[k - 1], peer(k)).wait_recv()
        for cp in sends:
            cp.wait_send()
        if reduce:
            acc = slots[0]
            for d in range(1, N_DEV):
                acc = acc + slots[d]
            out_ref[...] = acc
        else:
            out_ref[...] = slots[...]

    out_shape = (rows, LANES) if reduce else (N_DEV, rows, LANES)
    return pl.pallas_call(
        body, name=name, in_specs=[VMEM_SPEC], out_specs=VMEM_SPEC, out_shape=jax.ShapeDtypeStruct(out_shape, F32),
        scratch_shapes=[pltpu.VMEM((N_DEV, rows, LANES), F32), pltpu.SemaphoreType.DMA((N_DEV - 1,)),
                        pltpu.SemaphoreType.DMA((N_DEV - 1,))])(buf)


def _half_rows(half, which):
    return pl.ds(pl.multiple_of(which * half, 16), half)


def _other_chip(x, y, k):
    return _flip(x, (k >> 1) & 1), _flip(y, k & 1)


SEM_SPEC = pl.BlockSpec(memory_space=pltpu.SEMAPHORE)
DATAFLOW = pltpu.SideEffectType.DATAFLOW_SIDE_EFFECTING


def _ici_piece(srcs, lands, send_sems, recv_sems, i, k, x, y, c):
    half = srcs[i].shape[0] // 2
    ox, oy = _other_chip(x, y, k)
    return _remote(srcs[i].at[_half_rows(half, c)], lands[i].at[2 * x + y, _half_rows(half, c)],
                   send_sems.at[3 * i + k - 1], recv_sems.at[3 * i + k - 1], (ox, oy, c)), (ox, oy)


def _gather_start(shards, name):
    n = len(shards)

    def body(*refs):
        srcs, lands = refs[:n], refs[n:2 * n]
        send_sems, recv_sems = refs[2 * n], refs[2 * n + 1]
        token = refs[-1]
        x, y, c = _coords()
        for i in range(n):
            for k in (1, 2, 3):
                _ici_piece(srcs, lands, send_sems, recv_sems, i, k, x, y, c)[0].start()
        token[...] = jnp.zeros_like(token)

    hbm = lambda a: pltpu.with_memory_space_constraint(a, pltpu.HBM)
    lands = [lax.empty((4,) + s.shape, s.dtype) for s in shards]
    out = pl.pallas_call(
        body, name=name,
        out_shape=(pltpu.SemaphoreType.DMA((3 * n,)), pltpu.SemaphoreType.DMA((3 * n,)),
                   *[pltpu.HBM(s.shape, s.dtype) for s in shards], *[pltpu.HBM(a.shape, a.dtype) for a in lands],
                   jax.ShapeDtypeStruct((SUBLANES, LANES), F32)),
        in_specs=[HBM_SPEC] * (2 * n), out_specs=(SEM_SPEC, SEM_SPEC, *[HBM_SPEC] * (2 * n), VMEM_SPEC),
        input_output_aliases={i: 2 + i for i in range(2 * n)},
        compiler_params=pltpu.CompilerParams(has_side_effects=DATAFLOW))(*[hbm(s) for s in shards], *[hbm(a) for a in lands])
    return out[0], out[1], out[2:2 + n], out[2 + n:2 + 2 * n], out[-1]


def _gather_wait(send_sems, recv_sems, srcs, lands, after, name):
    n = len(srcs)

    def body(*refs):
        src_refs, land_refs = refs[:n], refs[n:2 * n]
        send_sems, recv_sems = refs[2 * n], refs[2 * n + 1]
        x, y, c = _coords()
        for i in range(n):
            half = src_refs[i].shape[0] // 2
            for k in (1, 2, 3):
                cp, (ox, oy) = _ici_piece(src_refs, land_refs, send_sems, recv_sems, i, k, x, y, c)
                cp.wait_send()
                piece = land_refs[i].at[2 * ox + oy, _half_rows(half, c)]
                _remote(piece, piece, send_sems.at[3 * i + k - 1], recv_sems.at[3 * i + k - 1], (ox, oy, c)).wait_recv()

    out = pl.pallas_call(
        body, name=name,
        out_shape=(*[pltpu.HBM(s.shape, s.dtype) for s in srcs], *[pltpu.HBM(a.shape, a.dtype) for a in lands]),
        in_specs=[HBM_SPEC] * (2 * n) + [SEM_SPEC, SEM_SPEC, pl.BlockSpec(memory_space=pl.ANY)],
        out_specs=tuple([HBM_SPEC] * (2 * n)), input_output_aliases={i: i for i in range(2 * n)},
        compiler_params=pltpu.CompilerParams(has_side_effects=DATAFLOW))(*srcs, *lands, send_sems, recv_sems, after)
    return out[n:]


def _gather_forward(lands, name):
    n = len(lands)

    def body(*refs):
        ins, outs = refs[:n], refs[n:2 * n]
        send_sems, recv_sems = refs[2 * n:]
        x, y, c = _coords()
        sends = []
        for i in range(n):
            half = ins[i].shape[1] // 2
            for k in (1, 2, 3):
                ox, oy = _other_chip(x, y, k)
                cp = _remote(ins[i].at[2 * ox + oy, _half_rows(half, c)], outs[i].at[2 * ox + oy, _half_rows(half, c)],
                             send_sems.at[3 * i + k - 1], recv_sems.at[3 * i + k - 1], (x, y, 1 - c))
                cp.start()
                sends.append(cp)
        for i in range(n):
            half = ins[i].shape[1] // 2
            for k in (1, 2, 3):
                ox, oy = _other_chip(x, y, k)
                piece = outs[i].at[2 * ox + oy, _half_rows(half, 1 - c)]
                _remote(piece, piece, send_sems.at[3 * i + k - 1], recv_sems.at[3 * i + k - 1], (x, y, 1 - c)).wait_recv()
        for cp in sends:
            cp.wait_send()

    return pl.pallas_call(
        body, name=name, in_specs=[HBM_SPEC] * n, out_specs=[HBM_SPEC] * n,
        out_shape=[jax.ShapeDtypeStruct(a.shape, a.dtype) for a in lands],
        input_output_aliases={i: i for i in range(n)},
        scratch_shapes=[pltpu.SemaphoreType.DMA((3 * n,))] * 2)(*lands)


def _rs_sibling_swap(g4s, name):
    n = len(g4s)

    def body(*refs):
        ins, outs = refs[:n], refs[n:2 * n]
        send_sems, recv_sems = refs[2 * n:]
        x, y, c = _coords()
        sends = []
        for i in range(n):
            half = ins[i].shape[1] // 2
            for s in range(4):
                cp = _remote(ins[i].at[s, _half_rows(half, 1 - c)], outs[i].at[s], send_sems.at[4 * i + s],
                             recv_sems.at[4 * i + s], (x, y, 1 - c))
                cp.start()
                sends.append(cp)
        for cp in sends:
            cp.wait_recv()
        for cp in sends:
            cp.wait_send()

    return pl.pallas_call(
        body, name=name, in_specs=[HBM_SPEC] * n, out_specs=[HBM_SPEC] * n,
        out_shape=[jax.ShapeDtypeStruct((4, g.shape[1] // 2, g.shape[2]), g.dtype) for g in g4s],
        scratch_shapes=[pltpu.SemaphoreType.DMA((4 * n,))] * 2)(*g4s)


def _rs_add_sibling(g4, got, c_idx, name):
    _, rows, cols = g4.shape
    half = rows // 2
    tr = min(half, 256)
    nb = half // tr

    def body(c_ref, a_ref, b_ref, o_ref, ob_ref):
        total = a_ref[...] + b_ref[...]
        o_ref[...] = total
        ob_ref[...] = total.astype(BF16)

    blk = (1, tr, cols)
    out = pl.BlockSpec(blk, lambda s, i, c_ref: (s, i, 0))
    grid_spec = pltpu.PrefetchScalarGridSpec(
        num_scalar_prefetch=1, grid=(4, nb),
        in_specs=[pl.BlockSpec(blk, lambda s, i, c_ref: (s, c_ref[0] * nb + i, 0)), out],
        out_specs=[out, out])
    return pl.pallas_call(
        body, name=name, grid_spec=grid_spec,
        out_shape=[jax.ShapeDtypeStruct(got.shape, F32), jax.ShapeDtypeStruct(got.shape, BF16)],
        compiler_params=_params("parallel", "parallel"))(c_idx, g4, got)


def _rs_add_chips(p4, got3, idx, name):
    _, half, cols = p4.shape
    tr = min(half, 256)
    nb = half // tr

    def body(idx_ref, p_ref, a_ref, b_ref, c_ref, o_ref):
        o_ref[...] = ((p_ref[0] + a_ref[0].astype(F32)) + b_ref[0].astype(F32)) + c_ref[0].astype(F32)

    blk = (1, tr, cols)
    grid_spec = pltpu.PrefetchScalarGridSpec(
        num_scalar_prefetch=1, grid=(nb,),
        in_specs=[pl.BlockSpec(blk, lambda i, idx_ref: (idx_ref[0], i, 0))]
        + [pl.BlockSpec(blk, functools.partial(lambda k, i, idx_ref: (k, i, 0), k)) for k in range(3)],
        out_specs=pl.BlockSpec((tr, cols), lambda i, idx_ref: (idx_ref[1] * nb + i, 0)))
    return pl.pallas_call(body, name=name, grid_spec=grid_spec, out_shape=jax.ShapeDtypeStruct((2 * half, cols), F32),
                          compiler_params=_params("parallel"))(idx, p4, got3, got3, got3)


def _rs_share_halves(bufs, name):
    n = len(bufs)

    def body(*refs):
        ins, outs = refs[:n], refs[n:2 * n]
        send_sems, recv_sems = refs[2 * n:]
        x, y, c = _coords()
        sends = []
        for i in range(n):
            half = ins[i].shape[0] // 2
            cp = _remote(ins[i].at[_half_rows(half, c)], outs[i].at[_half_rows(half, c)], send_sems.at[i],
                         recv_sems.at[i], (x, y, 1 - c))
            cp.start()
            sends.append(cp)
        for i in range(n):
            half = ins[i].shape[0] // 2
            _remote(ins[i].at[_half_rows(half, c)], outs[i].at[_half_rows(half, 1 - c)], send_sems.at[i],
                    recv_sems.at[i], (x, y, 1 - c)).wait_recv()
        for cp in sends:
            cp.wait_send()

    return pl.pallas_call(
        body, name=name, in_specs=[HBM_SPEC] * n, out_specs=[HBM_SPEC] * n,
        out_shape=[jax.ShapeDtypeStruct(b.shape, b.dtype) for b in bufs],
        input_output_aliases={i: i for i in range(n)},
        scratch_shapes=[pltpu.SemaphoreType.DMA((n,))] * 2)(*bufs)


def _scatter_piece(srcs, lands, send_sems, recv_sems, i, k, x, y, c):
    ox, oy = _other_chip(x, y, k)
    return _remote(srcs[i].at[2 * ox + oy], lands[i].at[k - 1], send_sems.at[3 * i + k - 1],
                   recv_sems.at[3 * i + k - 1], (ox, oy, c))


def _rs_scatter_start(p4s, name):
    n = len(p4s)

    def body(*refs):
        srcs, lands = refs[:n], refs[n:2 * n]
        send_sems, recv_sems = refs[2 * n], refs[2 * n + 1]
        token = refs[-1]
        x, y, c = _coords()
        for i in range(n):
            for k in (1, 2, 3):
                _scatter_piece(srcs, lands, send_sems, recv_sems, i, k, x, y, c).start()
        token[...] = jnp.zeros_like(token)

    hbm = lambda a: pltpu.with_memory_space_constraint(a, pltpu.HBM)
    lands = [lax.empty((3,) + p.shape[1:], p.dtype) for p in p4s]
    out = pl.pallas_call(
        body, name=name,
        out_shape=(pltpu.SemaphoreType.DMA((3 * n,)), pltpu.SemaphoreType.DMA((3 * n,)),
                   *[pltpu.HBM(p.shape, p.dtype) for p in p4s], *[pltpu.HBM(a.shape, a.dtype) for a in lands],
                   jax.ShapeDtypeStruct((SUBLANES, LANES), F32)),
        in_specs=[HBM_SPEC] * (2 * n), out_specs=(SEM_SPEC, SEM_SPEC, *[HBM_SPEC] * (2 * n), VMEM_SPEC),
        input_output_aliases={i: 2 + i for i in range(2 * n)},
        compiler_params=pltpu.CompilerParams(has_side_effects=DATAFLOW))(*[hbm(p) for p in p4s], *[hbm(a) for a in lands])
    return out[0], out[1], out[2:2 + n], out[2 + n:2 + 2 * n], out[-1]


def _rs_scatter_wait(send_sems, recv_sems, srcs, lands, after, name):
    n = len(srcs)

    def body(*refs):
        src_refs, land_refs = refs[:n], refs[n:2 * n]
        send_sems, recv_sems = refs[2 * n], refs[2 * n + 1]
        x, y, c = _coords()
        for i in range(n):
            for k in (1, 2, 3):
                cp = _scatter_piece(src_refs, land_refs, send_sems, recv_sems, i, k, x, y, c)
                cp.wait_send()
                cp.wait_recv()

    out = pl.pallas_call(
        body, name=name,
        out_shape=(*[pltpu.HBM(s.shape, s.dtype) for s in srcs], *[pltpu.HBM(a.shape, a.dtype) for a in lands]),
        in_specs=[HBM_SPEC] * (2 * n) + [SEM_SPEC, SEM_SPEC, pl.BlockSpec(memory_space=pl.ANY)],
        out_specs=tuple([HBM_SPEC] * (2 * n)), input_output_aliases={i: i for i in range(2 * n)},
        compiler_params=pltpu.CompilerParams(has_side_effects=DATAFLOW))(*srcs, *lands, send_sems, recv_sems, after)
    return out[n:]


def _swap_piece(srcs, lands, send_sems, recv_sems, i, s, x, y, c):
    half = srcs[i].shape[1] // 2
    return _remote(srcs[i].at[s, _half_rows(half, 1 - c)], lands[i].at[s], send_sems.at[4 * i + s],
                   recv_sems.at[4 * i + s], (x, y, 1 - c))


def _rs_swap_start(g4s, name):
    n = len(g4s)

    def body(*refs):
        srcs, lands = refs[:n], refs[n:2 * n]
        send_sems, recv_sems = refs[2 * n], refs[2 * n + 1]
        token = refs[-1]
        x, y, c = _coords()
        for i in range(n):
            for s in range(4):
                _swap_piece(srcs, lands, send_sems, recv_sems, i, s, x, y, c).start()
        token[...] = jnp.zeros_like(token)

    hbm = lambda a: pltpu.with_memory_space_constraint(a, pltpu.HBM)
    lands = [lax.empty((4, g.shape[1] // 2, g.shape[2]), g.dtype) for g in g4s]
    out = pl.pallas_call(
        body, name=name,
        out_shape=(pltpu.SemaphoreType.DMA((4 * n,)), pltpu.SemaphoreType.DMA((4 * n,)),
                   *[pltpu.HBM(g.shape, g.dtype) for g in g4s], *[pltpu.HBM(a.shape, a.dtype) for a in lands],
                   jax.ShapeDtypeStruct((SUBLANES, LANES), F32)),
        in_specs=[HBM_SPEC] * (2 * n), out_specs=(SEM_SPEC, SEM_SPEC, *[HBM_SPEC] * (2 * n), VMEM_SPEC),
        input_output_aliases={i: 2 + i for i in range(2 * n)},
        compiler_params=pltpu.CompilerParams(has_side_effects=DATAFLOW))(*[hbm(g) for g in g4s], *[hbm(a) for a in lands])
    return out[0], out[1], out[2:2 + n], out[2 + n:2 + 2 * n], out[-1]


def _rs_swap_wait(send_sems, recv_sems, srcs, lands, after, name):
    n = len(srcs)

    def body(*refs):
        src_refs, land_refs = refs[:n], refs[n:2 * n]
        send_sems, recv_sems = refs[2 * n], refs[2 * n + 1]
        x, y, c = _coords()
        for i in range(n):
            for s in range(4):
                cp = _swap_piece(src_refs, land_refs, send_sems, recv_sems, i, s, x, y, c)
                cp.wait_send()
                cp.wait_recv()

    out = pl.pallas_call(
        body, name=name,
        out_shape=(*[pltpu.HBM(s.shape, s.dtype) for s in srcs], *[pltpu.HBM(a.shape, a.dtype) for a in lands]),
        in_specs=[HBM_SPEC] * (2 * n) + [SEM_SPEC, SEM_SPEC, pl.BlockSpec(memory_space=pl.ANY)],
        out_specs=tuple([HBM_SPEC] * (2 * n)), input_output_aliases={i: i for i in range(2 * n)},
        compiler_params=pltpu.CompilerParams(has_side_effects=DATAFLOW))(*srcs, *lands, send_sems, recv_sems, after)
    return out[:n], out[n:]


def _rs_front(g4s, names, tag):
    c_idx = jnp.stack([lax.axis_index("c")]).astype(jnp.int32)
    got = _rs_sibling_swap(g4s, f"rs_sibling_swap_{tag}")
    return [_rs_add_sibling(g, s, c_idx, f"rs_add_sibling_{nm}") for g, s, nm in zip(g4s, got, names)]


def _rs_back(p4s, got3, names):
    x, y, c = _coords()
    idx = jnp.stack([2 * x + y, c]).astype(jnp.int32)
    return [_rs_add_chips(p, t, idx, f"rs_add_chips_{nm}") for (p, _), t, nm in zip(p4s, got3, names)]


def _cols_split(full):
    r, c4 = full.shape
    return full.reshape(r, 4, c4 // 4).transpose(1, 0, 2)


_BIG = {
    "w_in_even": ((1024, 2052), lambda s: s),
    "w_out_even": ((512, 1024), lambda s: s.reshape(2048, 1024)),
    "w_in_odd": ((1024, 2048), lambda s: s),
    "w_out_odd": ((512, 1024), lambda s: s.reshape(2048, 1024)),
    "w_pl": ((512, 256), lambda s: s.reshape(4, 2, 256, 256).transpose(1, 2, 0, 3).reshape(2, 256, 1024)),
    "w_pl_gate": ((512, 1024), lambda s: s.reshape(4, 2, 256, 1024).transpose(1, 0, 2, 3).reshape(2, 1024, 1024)),
}


_RS_EARLY = {
    "w_in_odd": lambda f: f,
    "w_out_odd": lambda f: f.reshape(4, 512, 1024),
    "w_pl_gate1": lambda f: f.reshape(4, 256, 1024),
    "w_pl1": _cols_split,
    "w_out_even": lambda f: f.reshape(4, 512, 1024),
    "w_pl_gate0": lambda f: f.reshape(4, 256, 1024),
    "w_pl0": _cols_split,
}
_RS_LATE = {"w_in_even": lambda f: f}


def _size(shape):
    n = 1
    for d in shape:
        n *= d
    return n


_SMALL = {"a_log": (1, 8), "dt_bias": (1, 8), "gdn_norm_g": (1, 128), "hgrn_norm_g": (1, 128),
          "lower_bounds": (2, 2048), "ln_g": (2, 1024), "ln_b": (2, 1024), "conv_a_w": (3, 1024), "conv_b_w": (4, 3072)}
_CONV_SHARD = {"conv_a_w": (3, 256), "conv_b_w": (4, 768)}


def _pack_small(parts, shapes, head_rows=0):
    rows = []
    for n, shape in shapes.items():
        v = parts[n].reshape(-1)
        rows.append(jnp.pad(v, (0, -v.shape[0] % LANES)).reshape(-1, LANES))
    buf = jnp.concatenate(rows, axis=0)
    return jnp.pad(buf, ((head_rows, -(buf.shape[0] + head_rows) % SUBLANES), (0, 0)))


def _unpack_small(buf, shapes, head_rows=0):
    out, off = {}, head_rows
    for n, shape in shapes.items():
        nrow = -(-_size(shape) // LANES)
        out[n] = buf[off:off + nrow].reshape(-1)[:_size(shape)].reshape(shape)
        off += nrow
    return out


_WEIGHTS = ["w_in_even", "conv_a_w", "conv_b_w", "a_log", "dt_bias", "gdn_norm_g", "w_out_even", "w_in_odd",
            "lower_bounds", "hgrn_norm_g", "w_out_odd", "ln_g", "ln_b", "w_pl", "w_pl_gate"]


def kernel(x, p, w_in_even, conv_a_w, conv_b_w, a_log, dt_bias, gdn_norm_g, w_out_even, w_in_odd, lower_bounds, hgrn_norm_g, w_out_odd, ln_g, ln_b, w_pl, w_pl_gate, loss_target, m_w_in_even, m_conv_a_w, m_conv_b_w, m_a_log, m_dt_bias, m_gdn_norm_g, m_w_out_even, m_w_in_odd, m_lower_bounds, m_hgrn_norm_g, m_w_out_odd, m_ln_g, m_ln_b, m_w_pl, m_w_pl_gate, v_w_in_even, v_conv_a_w, v_conv_b_w, v_a_log, v_dt_bias, v_gdn_norm_g, v_w_out_even, v_w_in_odd, v_lower_bounds, v_hgrn_norm_g, v_w_out_odd, v_ln_g, v_ln_b, v_w_pl, v_w_pl_gate):
    w = dict(zip(_WEIGHTS, (w_in_even, conv_a_w, conv_b_w, a_log, dt_bias, gdn_norm_g, w_out_even, w_in_odd,
                            lower_bounds, hgrn_norm_g, w_out_odd, ln_g, ln_b, w_pl, w_pl_gate)))
    m = dict(zip(_WEIGHTS, (m_w_in_even, m_conv_a_w, m_conv_b_w, m_a_log, m_dt_bias, m_gdn_norm_g, m_w_out_even,
                            m_w_in_odd, m_lower_bounds, m_hgrn_norm_g, m_w_out_odd, m_ln_g, m_ln_b, m_w_pl, m_w_pl_gate)))
    v = dict(zip(_WEIGHTS, (v_w_in_even, v_conv_a_w, v_conv_b_w, v_a_log, v_dt_bias, v_gdn_norm_g, v_w_out_even,
                            v_w_in_odd, v_lower_bounds, v_hgrn_norm_g, v_w_out_odd, v_ln_g, v_ln_b, v_w_pl, v_w_pl_gate)))
    chip = 2 * lax.axis_index("x") + lax.axis_index("y")

    names = list(_BIG)
    shard_shapes = {n: _BIG[n][0] for n in names}
    early, late = names[:1], names[1:]
    shards = {n: w[n].reshape(shard_shapes[n]).astype(BF16) for n in early}
    whole = lambda n, stacked: _BIG[n][1](lax.dynamic_update_slice(stacked, shards[n][None], (chip, 0, 0)))
    conv_mine = _pack_small({n: w[n] for n in _CONV_SHARD}, _CONV_SHARD)
    conv_all = _exchange_small(conv_mine, False, "gather_conv")
    shards, conv_all = lax.optimization_barrier((shards, conv_all))
    first = _gather_start([shards[n] for n in early], "gather_first_start")
    shards.update({n: (w[n].reshape(shard_shapes[n]) + first[4][0, 0]).astype(BF16) for n in late})
    send_sems, recv_sems, srcs, lands, token = _gather_start([shards[n] for n in late], "gather_rest_start")

    def late_weights(after):
        landed = _gather_forward(_gather_wait(send_sems, recv_sems, srcs, lands, after, "gather_rest_wait"),
                                 "gather_rest_forward")
        return {n: whole(n, ga) for n, ga in zip(late, landed)}

    landed = _gather_forward(_gather_wait(*first[:4], token, "gather_first_wait"), "gather_first_forward")
    full = {n: whole(n, ga) for n, ga in zip(early, landed)}
    conv_by_chip = [_unpack_small(conv_all[2 * s], _CONV_SHARD) for s in range(4)]
    for n in _CONV_SHARD:
        full[n] = jnp.concatenate([conv_by_chip[s][n] for s in range(4)], axis=1)
    for n in _SMALL:
        if n not in _CONV_SHARD:
            full[n] = w[n]

    early_rs = {}

    def early_grads_ready(grads):
        early_rs["swap"] = _rs_swap_start([_RS_EARLY[n](grads[n]) for n in _RS_EARLY], "rs_swap_early_start")
        return early_rs["swap"][4][0, 0]

    def early_grads_swapped(after):
        g4s, got = _rs_swap_wait(*early_rs["swap"][:4], after, "rs_swap_early_wait")
        c_idx = jnp.stack([lax.axis_index("c")]).astype(jnp.int32)
        early_rs["p4s"] = [_rs_add_sibling(g_, s_, c_idx, f"rs_add_sibling_{nm}") for g_, s_, nm in zip(g4s, got, _RS_EARLY)]
        early_rs["sems"] = _rs_scatter_start([pb for _, pb in early_rs["p4s"]], "rs_scatter_early_start")
        return early_rs["sems"][4][0, 0]

    late_rs = {}

    def last_grad_ready(grad):
        late_rs["p4s"] = _rs_front([_RS_LATE[n](grad) for n in _RS_LATE], list(_RS_LATE), "late")
        late_rs["sems"] = _rs_scatter_start([pb for _, pb in late_rs["p4s"]], "rs_scatter_late_start")
        return late_rs["sems"][4][0, 0]

    loss_part, dx, g = _local_step(x[0], p[:, 0], loss_target[0], full, late_weights, early_grads_ready,
                                   early_grads_swapped, last_grad_ready, token[0, 0])

    late_p4s, late_sems = late_rs["p4s"], late_rs["sems"]
    got3 = _rs_scatter_wait(*early_rs["sems"][:4], dx, "rs_scatter_early_wait")
    summed = dict(zip(_RS_EARLY, _rs_share_halves(_rs_back(early_rs["p4s"], got3, list(_RS_EARLY)), "rs_share_early")))
    g_big = {n: summed[n] for n in names if n in summed}
    g_big["w_pl"] = jnp.stack([summed["w_pl0"], summed["w_pl1"]])
    g_big["w_pl_gate"] = jnp.stack([summed["w_pl_gate0"], summed["w_pl_gate1"]])
    small_sum = _exchange_small(jnp.concatenate([loss_part, _pack_small(g, _SMALL)], axis=0), True, "reduce_small")
    loss = small_sum[0, 0]
    g_small = _unpack_small(small_sum, _SMALL, head_rows=SUBLANES)
    for n, (rows, cols) in _CONV_SHARD.items():
        g_small[n] = lax.dynamic_slice_in_dim(g_small[n], chip * cols, cols, axis=1)

    grads, delta, new_m, new_v = {}, {}, {}, {}
    for n in late:
        grads[n] = g_big[n].reshape(w[n].shape)
        delta[n], new_m[n], new_v[n] = _adamw(w[n], grads[n], m[n], v[n], f"adamw_{n}")
    own = {n: (_CONV_SHARD[n] if n in _CONV_SHARD else _SMALL[n]) for n in _SMALL}
    packs = [_pack_small({n: src[n] for n in _SMALL}, own)[None] for src in (w, g_small, m, v)]
    outs = [_unpack_small(t[0], own) for t in _adamw(*packs, "adamw_small")]
    for n in _SMALL:
        grads[n] = g_small[n].reshape(w[n].shape)
        delta[n], new_m[n], new_v[n] = (t[n].reshape(w[n].shape) for t in outs)
    got3 = _rs_scatter_wait(*late_sems[:4], new_v["w_in_odd"], "rs_scatter_late_wait")
    (g_in_even,) = _rs_share_halves(_rs_back(late_p4s, got3, list(_RS_LATE)), "rs_share_late")
    for n in early:
        t_ = lambda a: jnp.swapaxes(a, 1, 2)
        g_t = t_(g_in_even.reshape(w[n].shape))
        grads[n] = t_(g_t)
        delta[n], new_m[n], new_v[n] = (t_(o) for o in _adamw(t_(w[n]), g_t, t_(m[n]), t_(v[n]), f"adamw_{n}"))
    return (loss, dx[None], *[grads[n] for n in _WEIGHTS], *[delta[n] for n in _WEIGHTS],
            *[new_m[n] for n in _WEIGHTS], *[new_v[n] for n in _WEIGHTS])
```

```python
import functools

import jax
import jax.numpy as jnp
from jax import lax
from jax.experimental import pallas as pl
from jax.experimental.pallas import tpu as pltpu

F32 = jnp.float32
BF16 = jnp.bfloat16

D_MODEL = 1024
PL_DIM = 256
GDN_HEADS = 8
HEAD_DIM = 128
GDN_CHUNK = 64
HGRN_HEADS = 16
HGRN_CHUNK = 32
HGRN_WIDTH = 2048
DEEPNORM_ALPHA = 4.0 ** 0.25
NORM_EPS = 1e-5
ADAM_LR, ADAM_B1, ADAM_B2, ADAM_EPS, ADAM_WD, ADAM_STEP = 0.001, 0.9, 0.999, 1e-08, 0.01, 10

VMEM_LIMIT = 56 * 1024 * 1024
SUBLANES = 8
LANES = 128


def _params(*sem):
    return pltpu.CompilerParams(dimension_semantics=sem, vmem_limit_bytes=VMEM_LIMIT)


ONE_PASS, THREE_PASS, EXACT_LHS, EXACT_RHS = 0, 1, 2, 3


def _split3(v):
    hi = v.astype(BF16)
    r1 = v - hi.astype(F32)
    mid = r1.astype(BF16)
    return hi, mid, (r1 - mid.astype(F32)).astype(BF16)


def _mm_raw(a, b, kind, prec):
    nb = a.ndim - 2
    ca = a.ndim - 1 if kind[0] == "n" else a.ndim - 2
    cb = b.ndim - 2 if kind[1] == "n" else b.ndim - 1
    dims = (((ca,), (cb,)), (tuple(range(nb)),) * 2)
    dot = lambda p, q: lax.dot_general(p, q, dims, preferred_element_type=F32)
    ah, bh = a.astype(BF16), b.astype(BF16)
    if prec == ONE_PASS:
        return dot(ah, bh)
    if prec == EXACT_LHS:
        b1, b2, b3 = _split3(b)
        return dot(ah, b1) + (dot(ah, b2) + dot(ah, b3))
    if prec == EXACT_RHS:
        a1, a2, a3 = _split3(a)
        return dot(a1, bh) + (dot(a2, bh) + dot(a3, bh))
    al = (a - ah.astype(F32)).astype(BF16)
    bl = (b - bh.astype(F32)).astype(BF16)
    return dot(ah, bh) + (dot(ah, bl) + dot(al, bh))


@functools.partial(jax.custom_vjp, nondiff_argnums=(2, 3))
def _mm_vjp(a, b, kind, hi):
    return _mm_raw(a, b, kind, hi)


def _mm_vjp_fwd(a, b, kind, hi):
    return _mm_raw(a, b, kind, hi), (a, b)


def _mm_vjp_bwd(kind, hi, res, dc):
    a, b = res
    if hi in (EXACT_LHS, EXACT_RHS):
        assert kind == "nn"
        if hi == EXACT_LHS:
            return jnp.zeros_like(a), _mm_raw(a, dc, "tn", EXACT_LHS)
        return _mm_raw(dc, b, "nt", EXACT_RHS), jnp.zeros_like(b)
    if kind == "nn":
        return _mm_raw(dc, b, "nt", hi), _mm_raw(a, dc, "tn", hi)
    if kind == "nt":
        return _mm_raw(dc, b, "nn", hi), _mm_raw(dc, a, "tn", hi)
    return _mm_raw(b, dc, "nt", hi), _mm_raw(a, dc, "nn", hi)


_mm_vjp.defvjp(_mm_vjp_fwd, _mm_vjp_bwd)


def _lane_total(v):
    return jnp.broadcast_to(jnp.sum(v, axis=-1, keepdims=True), v.shape)


def _matmul(a, b, *, name, ta=False, tb=False, add=None, add_scale=1.0, tm=1024, tn=2048, tk=1024):
    m, k = (a.shape[1], a.shape[0]) if ta else a.shape
    n = b.shape[0] if tb else b.shape[1]
    tm, tn, tk = min(tm, m), min(tn, n), min(tk, k)
    tn = tn if n % tn == 0 else tn // 2
    assert m % tm == 0 and n % tn == 0 and k % tk == 0, (name, m, n, k)
    nk = k // tk
    dims = (((0 if ta else 1,), (1 if tb else 0,)), ((), ()))

    def body(*refs):
        a_ref, b_ref = refs[:2]
        o_ref = refs[-1]
        part = lax.dot_general(a_ref[...].astype(BF16), b_ref[...].astype(BF16), dims, preferred_element_type=F32)
        first = (lambda: part) if add is None else (lambda: part + add_scale * refs[2][...])
        if nk == 1:
            o_ref[...] = first()
        else:
            kk = pl.program_id(2)

            @pl.when(kk == 0)
            def _():
                o_ref[...] = first()

            @pl.when(kk > 0)
            def _():
                o_ref[...] += part

    a_spec = pl.BlockSpec((tk, tm), lambda i, j, kk: (kk, i)) if ta else pl.BlockSpec((tm, tk), lambda i, j, kk: (i, kk))
    b_spec = pl.BlockSpec((tn, tk), lambda i, j, kk: (j, kk)) if tb else pl.BlockSpec((tk, tn), lambda i, j, kk: (kk, j))
    o_spec = pl.BlockSpec((tm, tn), lambda i, j, kk: (i, j))
    in_specs = [a_spec, b_spec] + ([o_spec] if add is not None else [])
    args = (a, b) + ((add,) if add is not None else ())
    return pl.pallas_call(
        body, name=name, grid=(m // tm, n // tn, nk), in_specs=in_specs, out_specs=o_spec,
        out_shape=jax.ShapeDtypeStruct((m, n), F32),
        compiler_params=_params("parallel", "parallel", "arbitrary"))(*args)


HALO = SUBLANES


def _halo_specs(tt, width, col, nt):
    r = tt // HALO
    prev = pl.BlockSpec((HALO, width), lambda i: (jnp.maximum(i * r - 1, 0), col))
    nxt = pl.BlockSpec((HALO, width), lambda i: (jnp.minimum((i + 1) * r, nt * r - 1), col))
    return prev, nxt


def _shift_down(ext, k):
    return ext if k == 0 else pltpu.roll(ext, k, 0)


def _shift_up(ext, k):
    return ext if k == 0 else pltpu.roll(ext, ext.shape[0] - k, 0)


def _causal_conv(ext, w, taps):
    acc = None
    for j in range(taps):
        term = w[j:j + 1, :] * _shift_down(ext, taps - 1 - j)
        acc = term if acc is None else acc + term
    return acc[HALO:, :]


def _conv_a_fwd(proj_a, conv_w):
    t = proj_a.shape[0]
    tt = min(t, 256)
    nt = t // tt
    wdt = 1024

    def body(cur_ref, prev_ref, w_ref, y_ref):
        i = pl.program_id(0)
        cur = cur_ref[...]
        h, c, b, z = (cur[:, k * wdt:(k + 1) * wdt] for k in range(4))
        prev = prev_ref[...]
        u_prev = jnp.where(i > 0, prev[:, wdt:2 * wdt] * prev[:, 0:wdt], 0.0)
        ext = jnp.concatenate([u_prev, c * h], axis=0)
        conv = _causal_conv(ext, w_ref[...], 3)
        y_ref[...] = (b * conv * jax.nn.silu(z)).astype(BF16)

    prev_spec, _ = _halo_specs(tt, 4 * wdt, 0, nt)
    return pl.pallas_call(
        body, name="conv_a_fwd", grid=(nt,),
        in_specs=[pl.BlockSpec((tt, 4 * wdt), lambda i: (i, 0)), prev_spec, pl.BlockSpec((3, wdt), lambda i: (0, 0))],
        out_specs=pl.BlockSpec((tt, wdt), lambda i: (i, 0)),
        out_shape=jax.ShapeDtypeStruct((t, wdt), BF16), compiler_params=_params("parallel"))(proj_a, proj_a, conv_w)


def _conv_a_bwd(proj_a, conv_w, dy):
    t = proj_a.shape[0]
    tt = min(t, 256)
    nt = t // tt
    wdt = 1024

    def body(cur_ref, prev_ref, nxt_ref, w_ref, dy_ref, dyn_ref, d_ref, dw_ref):
        i = pl.program_id(0)
        w = w_ref[...]
        cur, prev, nxt = cur_ref[...], prev_ref[...], nxt_ref[...]
        split = lambda a: tuple(a[:, k * wdt:(k + 1) * wdt] for k in range(4))
        h, c, b, z = split(cur)
        hp, cp, _, _ = split(prev)
        hn, cn, bn, zn = split(nxt)
        u_prev = jnp.where(i > 0, cp * hp, 0.0)
        u_ext = jnp.concatenate([u_prev, c * h, cn * hn], axis=0)
        taps = [_shift_down(u_ext, 2 - j)[HALO:, :] for j in range(3)]
        conv = w[0:1, :] * taps[0] + w[1:2, :] * taps[1] + w[2:3, :] * taps[2]
        b_cn = jnp.concatenate([b, bn], axis=0)
        z_cn = jnp.concatenate([z, zn], axis=0)
        dy_cn = jnp.concatenate([dy_ref[...], jnp.where(i < nt - 1, dyn_ref[...], 0.0)], axis=0)
        sg = jax.nn.sigmoid(z_cn)
        silu = z_cn * sg
        d_conv = dy_cn * b_cn * silu
        db = (dy_cn * conv * silu)[:tt, :]
        dz = (dy_cn * b_cn * conv * (sg * (1.0 + z_cn * (1.0 - sg))))[:tt, :]
        du = None
        for j in range(3):
            term = w[j:j + 1, :] * _shift_up(d_conv, 2 - j)
            du = term if du is None else du + term
        du = du[:tt, :]
        d_ref[...] = jnp.concatenate([du * c, du * h, db, dz], axis=1).astype(BF16)

        @pl.when(i == 0)
        def _():
            dw_ref[...] = jnp.zeros_like(dw_ref)

        d_cur = d_conv[:tt, :]
        rows = [jnp.sum(d_cur * taps[j][:tt, :], axis=0, keepdims=True) for j in range(3)]
        dw_ref[0:3, :] += jnp.concatenate(rows, axis=0)

    prev_spec, nxt_spec = _halo_specs(tt, 4 * wdt, 0, nt)
    _, dyn_spec = _halo_specs(tt, wdt, 0, nt)
    return pl.pallas_call(
        body, name="conv_a_bwd", grid=(nt,),
        in_specs=[pl.BlockSpec((tt, 4 * wdt), lambda i: (i, 0)), prev_spec, nxt_spec,
                  pl.BlockSpec((3, wdt), lambda i: (0, 0)), pl.BlockSpec((tt, wdt), lambda i: (i, 0)), dyn_spec],
        out_specs=[pl.BlockSpec((tt, 4 * wdt), lambda i: (i, 0)), pl.BlockSpec((SUBLANES, wdt), lambda i: (0, 0))],
        out_shape=[jax.ShapeDtypeStruct((t, 4 * wdt), BF16), jax.ShapeDtypeStruct((SUBLANES, wdt), F32)],
        compiler_params=_params("arbitrary"))(proj_a, proj_a, proj_a, conv_w, dy, dy)


def _conv_b_fwd(proj_qkv, conv_w):
    t, width = proj_qkv.shape
    tt = min(t, 256)
    nt = t // tt
    wdt = 1024

    def body(cur_ref, prev_ref, w_ref, y_ref):
        i = pl.program_id(1)
        ext = jnp.concatenate([jnp.where(i > 0, prev_ref[...], 0.0), cur_ref[...]], axis=0)
        y_ref[...] = jax.nn.silu(_causal_conv(ext, w_ref[...], 4))

    r = tt // HALO
    return pl.pallas_call(
        body, name="conv_b_fwd", grid=(width // wdt, nt),
        in_specs=[pl.BlockSpec((tt, wdt), lambda j, i: (i, j)),
                  pl.BlockSpec((HALO, wdt), lambda j, i: (jnp.maximum(i * r - 1, 0), j)),
                  pl.BlockSpec((4, wdt), lambda j, i: (0, j))],
        out_specs=pl.BlockSpec((tt, wdt), lambda j, i: (i, j)),
        out_shape=jax.ShapeDtypeStruct((t, width), F32), compiler_params=_params("parallel", "parallel"))(
            proj_qkv, proj_qkv, conv_w)


def _conv_b_bwd(proj_qkv, conv_w, d_act, col, name):
    t = proj_qkv.shape[0]
    tt = min(t, 256)
    nt = t // tt
    wdt = 1024

    def body(cur_ref, prev_ref, nxt_ref, w_ref, da_ref, dan_ref, d_ref, dw_ref):
        i = pl.program_id(0)
        w = w_ref[...]
        u_ext = jnp.concatenate([jnp.where(i > 0, prev_ref[...], 0.0), cur_ref[...], nxt_ref[...]], axis=0)
        taps = [_shift_down(u_ext, 3 - j)[HALO:, :] for j in range(4)]
        conv = w[0:1, :] * taps[0] + w[1:2, :] * taps[1] + w[2:3, :] * taps[2] + w[3:4, :] * taps[3]
        da_cn = jnp.concatenate([da_ref[...], jnp.where(i < nt - 1, dan_ref[...], 0.0)], axis=0)
        sg = jax.nn.sigmoid(conv)
        d_conv = da_cn * (sg * (1.0 + conv * (1.0 - sg)))
        du = None
        for j in range(4):
            term = w[j:j + 1, :] * _shift_up(d_conv, 3 - j)
            du = term if du is None else du + term
        d_ref[...] = du[:tt, :].astype(BF16)

        @pl.when(i == 0)
        def _():
            dw_ref[...] = jnp.zeros_like(dw_ref)

        d_cur = d_conv[:tt, :]
        rows = [jnp.sum(d_cur * taps[j][:tt, :], axis=0, keepdims=True) for j in range(4)]
        dw_ref[0:4, :] += jnp.concatenate(rows, axis=0)

    prev_spec, nxt_spec = _halo_specs(tt, wdt, col, nt)
    _, dan_spec = _halo_specs(tt, wdt, 0, nt)
    return pl.pallas_call(
        body, name=name, grid=(nt,),
        in_specs=[pl.BlockSpec((tt, wdt), lambda i: (i, col)), prev_spec, nxt_spec,
                  pl.BlockSpec((4, wdt), lambda i: (0, col)), pl.BlockSpec((tt, wdt), lambda i: (i, 0)), dan_spec],
        out_specs=[pl.BlockSpec((tt, wdt), lambda i: (i, 0)), pl.BlockSpec((SUBLANES, wdt), lambda i: (0, 0))],
        out_shape=[jax.ShapeDtypeStruct((t, wdt), BF16), jax.ShapeDtypeStruct((SUBLANES, wdt), F32)],
        compiler_params=_params("arbitrary"))(proj_qkv, proj_qkv, proj_qkv, conv_w, d_act, d_act)


def _rms_gate(o, gn, z):
    on = o * lax.rsqrt(jnp.mean(o * o, axis=-1, keepdims=True) + NORM_EPS) * gn
    return on * jax.nn.silu(z)


GDN_PREP_ROWS = 1024


def _unit_lower_inverse(low):
    c = low.shape[-1]
    eye = lax.broadcasted_iota(jnp.int32, low.shape, low.ndim - 2) == lax.broadcasted_iota(jnp.int32, low.shape, low.ndim - 1)
    x = -low
    inv = eye.astype(F32) + x
    for _ in range(c.bit_length() - 2):
        x = _mm_raw(x, x, "nn", THREE_PASS)
        inv = inv + _mm_raw(inv, x, "nn", THREE_PASS)
    return inv


@jax.custom_vjp
def _known_inverse(low, inv):
    return inv


def _known_inverse_fwd(low, inv):
    return inv, inv


def _known_inverse_bwd(inv, d_inv):
    return -_mm_raw(_mm_raw(inv, d_inv, "tn", THREE_PASS), inv, "nt", THREE_PASS), jnp.zeros_like(inv)


_known_inverse.defvjp(_known_inverse_fwd, _known_inverse_bwd)


def _gdn_prep(mm, qa, ka, va, braw, araw, alog, dtb, inv_kept=None):
    n, c, _ = qa.shape
    q = qa * lax.rsqrt(jnp.sum(qa * qa, axis=-1, keepdims=True) + 1e-6) * (HEAD_DIM ** -0.5)
    k = ka * lax.rsqrt(jnp.sum(ka * ka, axis=-1, keepdims=True) + 1e-6)
    beta = jax.nn.sigmoid(braw)
    g = -jnp.exp(alog) * jax.nn.softplus(araw + dtb)
    ri = lax.broadcasted_iota(jnp.int32, (n, c, c), 1)
    ci = lax.broadcasted_iota(jnp.int32, (n, c, c), 2)
    incl, strict, eye = ri >= ci, ri > ci, ri == ci
    gc = mm(incl.astype(F32), g, "nn", EXACT_LHS)
    gc_i = gc[:, :, :c]
    gc_j = mm(jnp.ones((n, c, c), F32), jnp.where(eye, gc_i, 0.0), "nn", EXACT_LHS)
    decay = jnp.where(incl, jnp.exp(jnp.where(incl, gc_i - gc_j, 0.0)), 0.0)
    kb = k * beta
    low = jnp.where(strict, mm(kb, k, "nt", ONE_PASS) * decay, 0.0)
    inv = _unit_lower_inverse(low) if inv_kept is None else _known_inverse(low, inv_kept)
    egc = jnp.exp(gc)
    u = mm(inv, va * beta, "nn", THREE_PASS)
    w = mm(inv, kb * egc, "nn", THREE_PASS)
    attn = jnp.where(incl, mm(q, k, "nt", ONE_PASS) * decay, 0.0)
    g_last = jnp.sum(g, axis=1, keepdims=True)
    outs = (u, w, q * egc, k * jnp.exp(g_last - gc), attn, jnp.exp(g_last))
    return outs + (inv,) if inv_kept is None else outs


def _gdn_scan(mm, u, w, qd, kd, attn, egl, z, gn, state):
    v_new = u - mm(w, state, "nn", ONE_PASS)
    o = mm(qd, state, "nn", ONE_PASS) + mm(attn, v_new, "nn", ONE_PASS)
    new_state = state * egl + mm(kd, v_new, "tn", ONE_PASS)
    return _rms_gate(o, gn, z), new_state


def _chunks(ref_value, n, c):
    return ref_value.reshape(n, c, ref_value.shape[-1])


def _by_head(ref, rows, heads):
    return jnp.stack([ref[rows, pl.ds(h * HEAD_DIM, HEAD_DIM)] for h in range(heads)])


def _store_heads(ref, rows, value):
    for h in range(value.shape[0]):
        ref[rows, pl.ds(h * HEAD_DIM, HEAD_DIM)] = value[h]


def _gdn_prep_specs(tb):
    col = lambda off: pl.BlockSpec((tb, HEAD_DIM), lambda h, i: (i, off + h))
    rep = pl.BlockSpec((1, tb, LANES), lambda h, i: (h, i, 0))
    par = pl.BlockSpec((1, SUBLANES, LANES), lambda h, i: (h, 0, 0))
    att = pl.BlockSpec((1, tb, GDN_CHUNK), lambda h, i: (h, i, 0))
    egl = pl.BlockSpec((1, tb // GDN_CHUNK, SUBLANES, LANES), lambda h, i: (h, i, 0, 0))
    return col, rep, par, att, egl


def _gdn_prep_fwd(qkv_act, braw, araw, alog, dtb):
    t = qkv_act.shape[0]
    tb = min(t, 2 * GDN_PREP_ROWS)
    nt, nc = t // tb, tb // GDN_CHUNK
    width = GDN_HEADS * HEAD_DIM

    def body(q_ref, k_ref, v_ref, br_ref, ar_ref, al_ref, dt_ref, u_ref, w_ref, qd_ref, kd_ref, at_ref, eg_ref, inv_ref):
        ch = lambda r: _chunks(r, nc, GDN_CHUNK)
        u, w, qd, kd, attn, egl, inv = _gdn_prep(_mm_raw, ch(q_ref[...]), ch(k_ref[...]), ch(v_ref[...]), ch(br_ref[0]),
                                                 ch(ar_ref[0]), al_ref[0, 0:1, :], dt_ref[0, 0:1, :])
        u_ref[...] = u.reshape(tb, HEAD_DIM)
        w_ref[...] = w.reshape(tb, HEAD_DIM).astype(BF16)
        qd_ref[...] = qd.reshape(tb, HEAD_DIM).astype(BF16)
        kd_ref[...] = kd.reshape(tb, HEAD_DIM).astype(BF16)
        at_ref[0] = attn.reshape(tb, GDN_CHUNK).astype(BF16)
        eg_ref[0] = jnp.broadcast_to(egl, (nc, SUBLANES, LANES))
        inv_ref[0] = inv.reshape(tb, GDN_CHUNK)

    col, rep, par, att, egl = _gdn_prep_specs(tb)
    h = GDN_HEADS
    return pl.pallas_call(
        body, name="gdn_prep_fwd", grid=(h, nt),
        in_specs=[col(0), col(h), col(2 * h), rep, rep, par, par],
        out_specs=[col(0), col(0), col(0), col(0), att, egl, att],
        out_shape=[jax.ShapeDtypeStruct((t, width), F32)] + [jax.ShapeDtypeStruct((t, width), BF16)] * 3
        + [jax.ShapeDtypeStruct((h, t, GDN_CHUNK), BF16), jax.ShapeDtypeStruct((h, t // GDN_CHUNK, SUBLANES, LANES), F32),
           jax.ShapeDtypeStruct((h, t, GDN_CHUNK), F32)],
        compiler_params=_params("parallel", "parallel"))(qkv_act, qkv_act, qkv_act, braw, araw, alog, dtb)


def _gdn_prep_bwd(qkv_act, braw, araw, alog, dtb, inv, du, dw, dqd, dkd, dattn, degl):
    t = qkv_act.shape[0]
    tb = min(t, GDN_PREP_ROWS)
    nt, nc = t // tb, tb // GDN_CHUNK
    width = GDN_HEADS * HEAD_DIM

    def body(q_ref, k_ref, v_ref, br_ref, ar_ref, al_ref, dt_ref, inv_ref, du_ref, dw_ref, dqd_ref, dkd_ref, dat_ref,
             deg_ref, dq_ref, dk_ref, dv_ref, dbr_ref, dar_ref, dal_ref, ddt_ref):
        @pl.when(pl.program_id(1) == 0)
        def _():
            dal_ref[...] = jnp.zeros_like(dal_ref)
            ddt_ref[...] = jnp.zeros_like(ddt_ref)

        ch = lambda r: _chunks(r, nc, GDN_CHUNK)
        _, vjp = jax.vjp(functools.partial(_gdn_prep, _mm_vjp, inv_kept=ch(inv_ref[0])), ch(q_ref[...]), ch(k_ref[...]),
                         ch(v_ref[...]), ch(br_ref[0]), ch(ar_ref[0]), al_ref[0, 0:1, :], dt_ref[0, 0:1, :])
        dq, dk, dv, dbr, dar, dal, ddt = vjp((ch(du_ref[...]), ch(dw_ref[...]), ch(dqd_ref[...]), ch(dkd_ref[...]),
                                              ch(dat_ref[0]), deg_ref[0][:, 0:1, :]))
        dq_ref[...] = dq.reshape(tb, HEAD_DIM)
        dk_ref[...] = dk.reshape(tb, HEAD_DIM)
        dv_ref[...] = dv.reshape(tb, HEAD_DIM)
        dbr_ref[0] = _lane_total(dbr.reshape(tb, LANES))
        dar_ref[0] = _lane_total(dar.reshape(tb, LANES))
        dal_ref[0, 0:1, :] += _lane_total(dal)
        ddt_ref[0, 0:1, :] += _lane_total(ddt)

    col, rep, par, att, egl = _gdn_prep_specs(tb)
    h = GDN_HEADS
    return pl.pallas_call(
        body, name="gdn_prep_bwd", grid=(h, nt),
        in_specs=[col(0), col(h), col(2 * h), rep, rep, par, par, att, col(0), col(0), col(0), col(0), att, egl],
        out_specs=[col(0), col(0), col(0), rep, rep, par, par],
        out_shape=[jax.ShapeDtypeStruct((t, width), F32)] * 3 + [jax.ShapeDtypeStruct((h, t, LANES), F32)] * 2
        + [jax.ShapeDtypeStruct((h, SUBLANES, LANES), F32)] * 2,
        compiler_params=_params("parallel", "arbitrary"))(qkv_act, qkv_act, qkv_act, braw, araw, alog, dtb, inv,
                                                         du, dw, dqd, dkd, dattn, degl)


def _scan_specs(tb, heads, chunk, rev, nt):
    ti = (lambda i: nt - 1 - i) if rev else (lambda i: i)
    row = pl.BlockSpec((tb, heads * HEAD_DIM), lambda i: (ti(i), 0))
    att = pl.BlockSpec((heads, tb, chunk), lambda i: (0, ti(i), 0))
    egl = pl.BlockSpec((heads, tb // chunk, SUBLANES, LANES), lambda i: (0, ti(i), 0, 0))
    hist = pl.BlockSpec((heads, tb // chunk, HEAD_DIM, HEAD_DIM), lambda i: (0, ti(i), 0, 0))
    gn = pl.BlockSpec((SUBLANES, LANES), lambda i: (0, 0))
    return row, att, egl, hist, gn


def _gdn_scan_fwd(u, w, qd, kd, attn, egl, zb, gn):
    t = u.shape[0]
    tb = min(t, 256)
    nt, nc = t // tb, tb // GDN_CHUNK
    nh = GDN_HEADS

    def body(u_ref, w_ref, qd_ref, kd_ref, at_ref, eg_ref, z_ref, gn_ref, y_ref, hist_ref, s_ref):
        @pl.when(pl.program_id(0) == 0)
        def _():
            s_ref[...] = jnp.zeros_like(s_ref)

        g = gn_ref[0:1, :]
        state = s_ref[...]
        for c in range(nc):
            rows = pl.ds(c * GDN_CHUNK, GDN_CHUNK)
            heads = lambda r: _by_head(r, rows, nh)
            hist_ref[:, c] = state
            y, state = _gdn_scan(_mm_raw, heads(u_ref), heads(w_ref), heads(qd_ref), heads(kd_ref), at_ref[:, rows, :],
                                 eg_ref[:, c, 0:1, :], heads(z_ref), g, state)
            _store_heads(y_ref, rows, y.astype(BF16))
        s_ref[...] = state

    row, att, egs, hist, gns = _scan_specs(tb, nh, GDN_CHUNK, False, nt)
    return pl.pallas_call(
        body, name="gdn_scan_fwd", grid=(nt,), in_specs=[row, row, row, row, att, egs, row, gns], out_specs=[row, hist],
        out_shape=[jax.ShapeDtypeStruct((t, nh * HEAD_DIM), BF16),
                   jax.ShapeDtypeStruct((nh, t // GDN_CHUNK, HEAD_DIM, HEAD_DIM), F32)],
        scratch_shapes=[pltpu.VMEM((nh, HEAD_DIM, HEAD_DIM), F32)],
        compiler_params=_params("arbitrary"))(u, w, qd, kd, attn, egl, zb, gn)


def _gdn_scan_bwd(u, w, qd, kd, attn, egl, zb, gn, hist, dy):
    t = u.shape[0]
    tb = min(t, 256)
    nt, nc = t // tb, tb // GDN_CHUNK
    nh = GDN_HEADS

    def body(u_ref, w_ref, qd_ref, kd_ref, at_ref, eg_ref, z_ref, gn_ref, hist_ref, dy_ref,
             du_ref, dw_ref, dqd_ref, dkd_ref, dat_ref, deg_ref, dz_ref, dgn_ref, ds_ref):
        @pl.when(pl.program_id(0) == 0)
        def _():
            ds_ref[...] = jnp.zeros_like(ds_ref)
            dgn_ref[...] = jnp.zeros_like(dgn_ref)

        g = gn_ref[0:1, :]
        d_state = ds_ref[...]
        for c in reversed(range(nc)):
            rows = pl.ds(c * GDN_CHUNK, GDN_CHUNK)
            heads = lambda r: _by_head(r, rows, nh).astype(F32)
            _, vjp = jax.vjp(functools.partial(_gdn_scan, _mm_vjp), heads(u_ref), heads(w_ref), heads(qd_ref),
                             heads(kd_ref), at_ref[:, rows, :].astype(F32), eg_ref[:, c, 0:1, :], heads(z_ref), g,
                             hist_ref[:, c])
            du, dw, dqd, dkd, dat, deg, dz, dgn, d_state = vjp((heads(dy_ref), d_state))
            _store_heads(du_ref, rows, du)
            _store_heads(dw_ref, rows, dw)
            _store_heads(dqd_ref, rows, dqd)
            _store_heads(dkd_ref, rows, dkd)
            dat_ref[:, rows, :] = dat
            deg_ref[:, c] = jnp.broadcast_to(deg, (nh, SUBLANES, LANES))
            _store_heads(dz_ref, rows, dz.astype(BF16))
            dgn_ref[0:1, :] += dgn
        ds_ref[...] = d_state

    row, att, egs, hists, gns = _scan_specs(tb, nh, GDN_CHUNK, True, nt)
    wide = jax.ShapeDtypeStruct((t, nh * HEAD_DIM), F32)
    return pl.pallas_call(
        body, name="gdn_scan_bwd", grid=(nt,),
        in_specs=[row, row, row, row, att, egs, row, gns, hists, row],
        out_specs=[row, row, row, row, att, egs, row, gns],
        out_shape=[wide] * 4 + [jax.ShapeDtypeStruct((nh, t, GDN_CHUNK), F32),
                                jax.ShapeDtypeStruct((nh, t // GDN_CHUNK, SUBLANES, LANES), F32),
                                jax.ShapeDtypeStruct((t, nh * HEAD_DIM), BF16),
                                jax.ShapeDtypeStruct((SUBLANES, LANES), F32)],
        scratch_shapes=[pltpu.VMEM((nh, HEAD_DIM, HEAD_DIM), F32)],
        compiler_params=_params("arbitrary"))(u, w, qd, kd, attn, egl, zb, gn, hist, dy)


def _hgrn_prep(mm, qr, fr, lbl):
    n, c, _ = qr.shape
    lb = jax.nn.sigmoid(lbl[1:2, :] - lbl[0:1, :])
    f = lb + (1.0 - lb) * jax.nn.sigmoid(fr)
    q = jax.nn.silu(qr)
    k = 1.0 - f
    logf = jnp.log(f)
    ri = lax.broadcasted_iota(jnp.int32, (n, c, c), 1)
    ci = lax.broadcasted_iota(jnp.int32, (n, c, c), 2)
    b = mm((ri >= ci).astype(F32), logf, "nn", EXACT_LHS)
    attn = _hgrn_attn(mm, q, k, b)
    b_last = jnp.sum(logf, axis=1, keepdims=True)
    return q * jnp.exp(b), k * jnp.exp(b_last - b), attn, jnp.exp(b_last)


HGRN_SUB = 8
HGRN_PREP_ROWS = 2048


@functools.partial(jax.custom_vjp, nondiff_argnums=(1,))
def _roll_rows(x, shift):
    return pltpu.roll(x, shift, x.ndim - 2)


def _roll_rows_fwd(x, shift):
    return _roll_rows(x, shift), None


def _roll_rows_bwd(shift, _, d):
    return (pltpu.roll(d, d.shape[-2] - shift, d.ndim - 2),)


_roll_rows.defvjp(_roll_rows_fwd, _roll_rows_bwd)


@jax.custom_vjp
def _exp_clamped(v):
    return jnp.exp(jnp.minimum(v, 0.0))


def _exp_clamped_fwd(v):
    out = jnp.exp(jnp.minimum(v, 0.0))
    return out, out


def _exp_clamped_bwd(out, d):
    return (d * out,)


_exp_clamped.defvjp(_exp_clamped_fwd, _exp_clamped_bwd)


def _hgrn_attn(mm, q, k, b):
    n, c, d = q.shape
    sb = HGRN_SUB
    sub = lambda a: a.reshape(n * c // sb, sb, d)
    qs, ks, bs = sub(q), sub(k), sub(b)
    row = lax.broadcasted_iota(jnp.int32, (n, c, c), 1)
    col = lax.broadcasted_iota(jnp.int32, (n, c, c), 2)
    same_block = (row & -sb) == (col & -sb)
    attn = None
    for delta in range(sb):
        if delta == 0:
            prod = qs * ks
        else:
            prod = qs * _roll_rows(ks, delta) * _exp_clamped(bs - _roll_rows(bs, delta))
        sums = jnp.sum(prod, axis=-1, keepdims=True).reshape(n, c, 1)
        term = jnp.where(same_block & (row - col == delta), sums, 0.0)
        attn = term if attn is None else attn + term
    far = [jnp.zeros((n, sb, c), F32)]
    for i in range(1, c // sb):
        r0 = i * sb
        bi = b[:, r0:r0 + sb, :]
        ref = bi[:, 0:1, :]
        before = jnp.concatenate([k[:, :r0, :] * jnp.exp(ref - b[:, :r0, :]), jnp.zeros((n, c - r0, d), F32)], axis=1)
        far.append(mm(q[:, r0:r0 + sb, :] * jnp.exp(bi - ref), before, "nt", ONE_PASS))
    return attn + jnp.concatenate(far, axis=1)


def _hgrn_scan(mm, qe, kd, attn, ebl, iv, state):
    o = mm(qe, state, "nt", ONE_PASS) + mm(attn, iv, "nn", ONE_PASS)
    new_state = state * ebl + mm(iv, kd, "tn", ONE_PASS)
    return o, new_state


def _hgrn_prep_specs(tb):
    col = pl.BlockSpec((tb, HEAD_DIM), lambda h, i: (i, h))
    lbs = pl.BlockSpec((2, HEAD_DIM), lambda h, i: (0, h))
    att = pl.BlockSpec((1, tb, HGRN_CHUNK), lambda h, i: (h, i, 0))
    ebl = pl.BlockSpec((1, tb // HGRN_CHUNK, SUBLANES, LANES), lambda h, i: (h, i, 0, 0))
    return col, lbs, att, ebl


def _hgrn_prep_fwd(qr, fr, lower_bounds):
    t = qr.shape[0]
    tb = min(t, HGRN_PREP_ROWS)
    nt, nc = t // tb, tb // HGRN_CHUNK
    hh = HGRN_HEADS

    def body(q_ref, f_ref, lb_ref, qe_ref, kd_ref, at_ref, eb_ref):
        ch = lambda r: _chunks(r, nc, HGRN_CHUNK)
        qe, kd, attn, ebl = _hgrn_prep(_mm_raw, ch(q_ref[...]), ch(f_ref[...]), lb_ref[...])
        qe_ref[...] = qe.reshape(tb, HEAD_DIM).astype(BF16)
        kd_ref[...] = kd.reshape(tb, HEAD_DIM).astype(BF16)
        at_ref[0] = attn.reshape(tb, HGRN_CHUNK).astype(BF16)
        eb_ref[0] = jnp.broadcast_to(ebl, (nc, SUBLANES, LANES))

    col, lbs, att, ebs = _hgrn_prep_specs(tb)
    return pl.pallas_call(
        body, name="hgrn_prep_fwd", grid=(hh, nt), in_specs=[col, col, lbs], out_specs=[col, col, att, ebs],
        out_shape=[jax.ShapeDtypeStruct((t, HGRN_WIDTH), BF16)] * 2
        + [jax.ShapeDtypeStruct((hh, t, HGRN_CHUNK), BF16), jax.ShapeDtypeStruct((hh, t // HGRN_CHUNK, SUBLANES, LANES), F32)],
        compiler_params=_params("parallel", "parallel"))(qr, fr, lower_bounds)


def _hgrn_prep_bwd(qr, fr, lower_bounds, dqe, dkd, dattn, debl):
    t = qr.shape[0]
    tb = min(t, HGRN_PREP_ROWS)
    nt, nc = t // tb, tb // HGRN_CHUNK
    hh = HGRN_HEADS

    def body(q_ref, f_ref, lb_ref, dqe_ref, dkd_ref, dat_ref, deb_ref, dq_ref, df_ref, dlb_ref):
        @pl.when(pl.program_id(1) == 0)
        def _():
            dlb_ref[...] = jnp.zeros_like(dlb_ref)

        ch = lambda r: _chunks(r, nc, HGRN_CHUNK)
        _, vjp = jax.vjp(functools.partial(_hgrn_prep, _mm_vjp), ch(q_ref[...]), ch(f_ref[...]), lb_ref[...])
        dq, df, dlb = vjp((ch(dqe_ref[...]), ch(dkd_ref[...]), ch(dat_ref[0]), deb_ref[0][:, 0:1, :]))
        dq_ref[...] = dq.reshape(tb, HEAD_DIM).astype(BF16)
        df_ref[...] = df.reshape(tb, HEAD_DIM).astype(BF16)
        dlb_ref[...] += dlb

    col, lbs, att, ebs = _hgrn_prep_specs(tb)
    return pl.pallas_call(
        body, name="hgrn_prep_bwd", grid=(hh, nt), in_specs=[col, col, lbs, col, col, att, ebs],
        out_specs=[col, col, lbs],
        out_shape=[jax.ShapeDtypeStruct((t, HGRN_WIDTH), BF16)] * 2 + [jax.ShapeDtypeStruct((2, HGRN_WIDTH), F32)],
        compiler_params=_params("parallel", "arbitrary"))(qr, fr, lower_bounds, dqe, dkd, dattn, debl)


def _hgrn_scan_fwd(qe, kd, attn, ebl, iv, z, gn):
    t = qe.shape[0]
    tb = min(t, 128)
    nt, nc = t // tb, tb // HGRN_CHUNK
    hh = HGRN_HEADS

    def body(qe_ref, kd_ref, at_ref, eb_ref, i_ref, z_ref, gn_ref, y_ref, o_ref, hist_ref, s_ref):
        @pl.when(pl.program_id(0) == 0)
        def _():
            s_ref[...] = jnp.zeros_like(s_ref)

        g = gn_ref[0:1, :]
        state = s_ref[...]
        for c in range(nc):
            rows = pl.ds(c * HGRN_CHUNK, HGRN_CHUNK)
            heads = lambda r: _by_head(r, rows, hh)
            hist_ref[:, c] = state
            o, state = _hgrn_scan(_mm_raw, heads(qe_ref), heads(kd_ref), at_ref[:, rows, :], eb_ref[:, c, 0:1, :],
                                  heads(i_ref), state)
            _store_heads(o_ref, rows, o)
            _store_heads(y_ref, rows, _rms_gate(o, g, heads(z_ref)).astype(BF16))
        s_ref[...] = state

    row, att, ebs, hist, gns = _scan_specs(tb, hh, HGRN_CHUNK, False, nt)
    return pl.pallas_call(
        body, name="hgrn_scan_fwd", grid=(nt,), in_specs=[row, row, att, ebs, row, row, gns], out_specs=[row, row, hist],
        out_shape=[jax.ShapeDtypeStruct((t, HGRN_WIDTH), BF16), jax.ShapeDtypeStruct((t, HGRN_WIDTH), F32),
                   jax.ShapeDtypeStruct((hh, t // HGRN_CHUNK, HEAD_DIM, HEAD_DIM), F32)],
        scratch_shapes=[pltpu.VMEM((hh, HEAD_DIM, HEAD_DIM), F32)],
        compiler_params=_params("arbitrary"))(qe, kd, attn, ebl, iv, z, gn)


def _hgrn_scan_bwd(qe, kd, attn, ebl, iv, z, gn, o, hist, dy):
    t = qe.shape[0]
    tb = min(t, 128)
    nt, nc = t // tb, tb // HGRN_CHUNK
    hh = HGRN_HEADS

    def body(qe_ref, kd_ref, at_ref, eb_ref, i_ref, z_ref, gn_ref, o_ref, hist_ref, dy_ref,
             dqe_ref, dkd_ref, dat_ref, deb_ref, di_ref, dz_ref, dgn_ref, ds_ref):
        @pl.when(pl.program_id(0) == 0)
        def _():
            ds_ref[...] = jnp.zeros_like(ds_ref)
            dgn_ref[...] = jnp.zeros_like(dgn_ref)

        g = gn_ref[0:1, :]
        d_state = ds_ref[...]
        for c in reversed(range(nc)):
            rows = pl.ds(c * HGRN_CHUNK, HGRN_CHUNK)
            heads = lambda r: _by_head(r, rows, hh).astype(F32)
            _, gate_vjp = jax.vjp(_rms_gate, heads(o_ref), g, heads(z_ref))
            d_o, dgn, dz = gate_vjp(heads(dy_ref))
            _, vjp = jax.vjp(functools.partial(_hgrn_scan, _mm_vjp), heads(qe_ref), heads(kd_ref),
                             at_ref[:, rows, :].astype(F32), eb_ref[:, c, 0:1, :], heads(i_ref), hist_ref[:, c])
            dqe, dkd, dat, deb, di, d_state = vjp((d_o, d_state))
            _store_heads(dqe_ref, rows, dqe)
            _store_heads(dkd_ref, rows, dkd)
            dat_ref[:, rows, :] = dat
            deb_ref[:, c] = jnp.broadcast_to(deb, (hh, SUBLANES, LANES))
            _store_heads(di_ref, rows, di.astype(BF16))
            _store_heads(dz_ref, rows, dz.astype(BF16))
            dgn_ref[0:1, :] += dgn
        ds_ref[...] = d_state

    row, att, ebs, hists, gns = _scan_specs(tb, hh, HGRN_CHUNK, True, nt)
    wide = lambda dt: jax.ShapeDtypeStruct((t, HGRN_WIDTH), dt)
    return pl.pallas_call(
        body, name="hgrn_scan_bwd", grid=(nt,),
        in_specs=[row, row, att, ebs, row, row, gns, row, hists, row],
        out_specs=[row, row, att, ebs, row, row, gns],
        out_shape=[wide(F32), wide(F32), jax.ShapeDtypeStruct((hh, t, HGRN_CHUNK), F32),
                   jax.ShapeDtypeStruct((hh, t // HGRN_CHUNK, SUBLANES, LANES), F32), wide(BF16), wide(BF16),
                   jax.ShapeDtypeStruct((SUBLANES, LANES), F32)],
        scratch_shapes=[pltpu.VMEM((hh, HEAD_DIM, HEAD_DIM), F32)],
        compiler_params=_params("arbitrary"))(qe, kd, attn, ebl, iv, z, gn, o, hist, dy)


def _layer_norm(pre, g, b):
    mu = jnp.mean(pre, axis=-1, keepdims=True)
    d = pre - mu
    var = jnp.mean(d * d, axis=-1, keepdims=True)
    return d * lax.rsqrt(var + NORM_EPS) * g + b


def _lnpl_fwd(xin, s, p, wg, wpl, ln_g, ln_b):
    t = xin.shape[0]
    tt = min(t, 512)

    def body(x_ref, s_ref, p_ref, wg_ref, wpl_ref, g_ref, b_ref, o_ref, ob_ref):
        xn = _layer_norm(DEEPNORM_ALPHA * x_ref[...] + s_ref[...], g_ref[...], b_ref[...])
        gate = jax.nn.sigmoid(_mm_raw(xn, wg_ref[...], "nn", False))
        out = xn + _mm_raw(p_ref[...], wpl_ref[...], "nn", False) * gate
        o_ref[...] = out
        ob_ref[...] = out.astype(BF16)

    row = lambda w: pl.BlockSpec((tt, w), lambda i: (i, 0))
    full = lambda a: pl.BlockSpec(a.shape, lambda i: (0, 0))
    return pl.pallas_call(
        body, name="lnpl_fwd", grid=(t // tt,),
        in_specs=[row(D_MODEL), row(D_MODEL), row(PL_DIM), full(wg), full(wpl), full(ln_g), full(ln_b)],
        out_specs=[row(D_MODEL), row(D_MODEL)],
        out_shape=[jax.ShapeDtypeStruct((t, D_MODEL), F32), jax.ShapeDtypeStruct((t, D_MODEL), BF16)],
        compiler_params=_params("parallel"))(xin, s, p, wg, wpl, ln_g, ln_b)


def _lnpl_bwd(xin, s, p, wg, wpl, ln_g, ln_b, upstream, last, name):
    t = xin.shape[0]
    tt = min(t, 512)

    def body(x_ref, s_ref, p_ref, wg_ref, wpl_ref, g_ref, b_ref, up_ref,
             dpre_ref, dwg_ref, dwpl_ref, dg_ref, db_ref, loss_ref):
        @pl.when(pl.program_id(0) == 0)
        def _():
            for r in (dwg_ref, dwpl_ref, dg_ref, db_ref, loss_ref):
                r[...] = jnp.zeros_like(r)

        pre = DEEPNORM_ALPHA * x_ref[...] + s_ref[...]
        xn, ln_vjp = jax.vjp(_layer_norm, pre, g_ref[...], b_ref[...])
        gate = jax.nn.sigmoid(_mm_raw(xn, wg_ref[...], "nn", False))
        plv = _mm_raw(p_ref[...], wpl_ref[...], "nn", False)
        if last:
            err = xn + plv * gate - up_ref[...]
            dout = err * (1.0 / D_MODEL)
            tot = jnp.sum(jnp.sum(err * err, axis=1, keepdims=True), axis=0, keepdims=True) * (0.5 / D_MODEL)
            loss_ref[...] += jnp.broadcast_to(tot, loss_ref.shape)
        else:
            dout = up_ref[...]
        dplv = dout * gate
        dlogits = dout * plv * gate * (1.0 - gate)
        dwg_ref[...] += _mm_raw(xn, dlogits, "tn", False)
        dwpl_ref[...] += _mm_raw(p_ref[...], dplv, "tn", False)
        dxn = dout + _mm_raw(dlogits, wg_ref[...], "nt", False)
        dpre, dg, db = ln_vjp(dxn)
        dpre_ref[...] = dpre
        dg_ref[...] += dg
        db_ref[...] += db

    row = lambda w: pl.BlockSpec((tt, w), lambda i: (i, 0))
    full = lambda shape: pl.BlockSpec(shape, lambda i: (0, 0))
    return pl.pallas_call(
        body, name=name, grid=(t // tt,),
        in_specs=[row(D_MODEL), row(D_MODEL), row(PL_DIM), full(wg.shape), full(wpl.shape), full(ln_g.shape),
                  full(ln_b.shape), row(D_MODEL)],
        out_specs=[row(D_MODEL), full(wg.shape), full(wpl.shape), full(ln_g.shape), full(ln_b.shape),
                   full((SUBLANES, LANES))],
        out_shape=[jax.ShapeDtypeStruct((t, D_MODEL), F32), jax.ShapeDtypeStruct(wg.shape, F32),
                   jax.ShapeDtypeStruct(wpl.shape, F32), jax.ShapeDtypeStruct(ln_g.shape, F32),
                   jax.ShapeDtypeStruct(ln_b.shape, F32), jax.ShapeDtypeStruct((SUBLANES, LANES), F32)],
        compiler_params=_params("arbitrary"))(xin, s, p, wg, wpl, ln_g, ln_b, upstream)


def _pack_tail(dbr, dar):
    nh, t, _ = dbr.shape
    tt = min(t, 512)

    def body(b_ref, a_ref, o_ref):
        lane = lax.broadcasted_iota(jnp.int32, (tt, LANES), 1)
        acc = jnp.zeros((tt, LANES), F32)
        for h in range(nh):
            acc = jnp.where(lane == h, b_ref[h], acc)
            acc = jnp.where(lane == nh + h, a_ref[h], acc)
        o_ref[...] = acc.astype(BF16)

    spec = pl.BlockSpec((nh, tt, LANES), lambda i: (0, i, 0))
    return pl.pallas_call(
        body, name="pack_tail", grid=(t // tt,), in_specs=[spec, spec], out_specs=pl.BlockSpec((tt, LANES), lambda i: (i, 0)),
        out_shape=jax.ShapeDtypeStruct((t, LANES), BF16), compiler_params=_params("parallel"))(dbr, dar)


def _rep_rows(v):
    return jnp.broadcast_to(v.reshape(1, LANES), (SUBLANES, LANES))


def _rep_heads(v):
    return jnp.broadcast_to(v.reshape(-1, 1, 1), (v.shape[0], SUBLANES, LANES))


def _col_range(stacked, lo, hi):
    c = stacked.shape[2]
    parts = [stacked[s, :, max(lo, s * c) - s * c:min(hi, (s + 1) * c) - s * c]
             for s in range(4) if max(lo, s * c) < min(hi, (s + 1) * c)]
    return parts[0] if len(parts) == 1 else jnp.concatenate(parts, axis=1)


def _col_shards(pieces, c):
    shards, offs, o = [], [], 0
    for pc in pieces:
        offs.append(o)
        o += pc.shape[1]
    for s in range(4):
        lo, hi = s * c, (s + 1) * c
        parts = [pc[:, max(lo, o) - o:min(hi, o + pc.shape[1]) - o] for pc, o in zip(pieces, offs)
                 if max(lo, o) < min(hi, o + pc.shape[1])]
        shards.append(parts[0] if len(parts) == 1 else jnp.concatenate(parts, axis=1))
    return jnp.stack(shards)


def _local_step(x, p, target, w, late_weights, early_grads_ready, early_grads_swapped, last_grad_ready, start_token):
    a = DEEPNORM_ALPHA
    nh = GDN_HEADS
    xb = (x + start_token).astype(BF16)
    wie = w["w_in_even"]
    w_a, w_qkv, w_zb = _col_range(wie, 0, 4096), _col_range(wie, 4096, 7168), _col_range(wie, 7168, 8192)
    w_tail = jnp.pad(_col_range(wie, 8192, 8192 + 2 * nh), ((0, 0), (0, LANES - 2 * nh)))
    conv_a_w, conv_b_w = w["conv_a_w"], w["conv_b_w"]
    ln_g0, ln_b0, ln_g1, ln_b1 = (v.reshape(1, D_MODEL) for v in (w["ln_g"][0], w["ln_b"][0], w["ln_g"][1], w["ln_b"][1]))
    alog, dtb = _rep_heads(w["a_log"].reshape(nh)), _rep_heads(w["dt_bias"].reshape(nh))
    gdn_g, hgrn_g = _rep_rows(w["gdn_norm_g"]), _rep_rows(w["hgrn_norm_g"])

    proj_a = _matmul(xb, w_a, name="fwd_proj_a")
    proj_qkv = _matmul(xb, w_qkv, name="fwd_proj_qkv")
    proj_zb = _matmul(xb, w_zb, name="fwd_proj_zb")
    proj_tail = _matmul(xb, w_tail, name="fwd_proj_tail")
    rep = lambda cols: jnp.broadcast_to(cols.T[:, :, None], (nh, cols.shape[0], LANES))
    braw, araw = rep(proj_tail[:, :nh]), rep(proj_tail[:, nh:2 * nh])
    y_a = _conv_a_fwd(proj_a, conv_a_w)
    qkv_act = _conv_b_fwd(proj_qkv, conv_b_w)
    *gdn_pre, gdn_inv = _gdn_prep_fwd(qkv_act, braw, araw, alog, dtb)
    y_b, gdn_hist = _gdn_scan_fwd(*gdn_pre, proj_zb, gdn_g)
    w = {**w, **late_weights(y_b)}
    woe, wio, woo = w["w_out_even"], w["w_in_odd"], w["w_out_odd"]
    s0 = _matmul(y_b, woe[1024:], name="fwd_out_even_b", add=_matmul(y_a, woe[:1024], name="fwd_out_even_a"))
    x1, x1b = _lnpl_fwd(x, s0, p[0], w["w_pl_gate"][0], w["w_pl"][0], ln_g0, ln_b0)
    proj_o = [_matmul(x1b, wio[j], name=f"fwd_proj_odd{j}") for j in range(4)]
    hgrn_pre = _hgrn_prep_fwd(proj_o[0], proj_o[1], w["lower_bounds"])
    y_o, hgrn_o, hgrn_hist = _hgrn_scan_fwd(*hgrn_pre, proj_o[2], proj_o[3], hgrn_g)
    s1 = _matmul(y_o, woo, name="fwd_out_odd")

    g = {}
    dpre1, dwg1, dwpl1, dlng1, dlnb1, loss = _lnpl_bwd(x1, s1, p[1], w["w_pl_gate"][1], w["w_pl"][1], ln_g1, ln_b1,
                                                     target, True, "lnpl_bwd_odd")
    dy_o = _matmul(dpre1, woo, tb=True, name="bwd_out_odd_dx")
    g["w_out_odd"] = _matmul(y_o, dpre1, ta=True, name="bwd_out_odd_dw")
    dqe, dkd, dat, deb, di, dz, dhg = _hgrn_scan_bwd(*hgrn_pre, proj_o[2], proj_o[3], hgrn_g, hgrn_o, hgrn_hist, dy_o)
    dq, df, dlb = _hgrn_prep_bwd(proj_o[0], proj_o[1], w["lower_bounds"], dqe, dkd, dat, deb)
    dx1 = dpre1
    scale = a
    dws = []
    for j, dj in enumerate((dq, df, di, dz)):
        dx1 = _matmul(dj, wio[j], tb=True, add=dx1, add_scale=scale, name=f"bwd_proj_odd_dx{j}")
        scale = 1.0
        dws.append(_matmul(x1b, dj, ta=True, name=f"bwd_proj_odd_dw{j}"))
    g["w_in_odd"] = jnp.stack(dws)
    g["hgrn_norm_g"] = dhg[0:1]
    g["lower_bounds"] = dlb
    g["w_pl_gate1"], g["w_pl1"] = dwg1, dwpl1

    dpre0, dwg0, dwpl0, dlng0, dlnb0, _ = _lnpl_bwd(x, s0, p[0], w["w_pl_gate"][0], w["w_pl"][0], ln_g0, ln_b0,
                                                  dx1, False, "lnpl_bwd_even")
    g["w_pl_gate0"], g["w_pl0"] = dwg0, dwpl0
    g["ln_g"] = jnp.concatenate([dlng0, dlng1], axis=0)
    g["ln_b"] = jnp.concatenate([dlnb0, dlnb1], axis=0)
    dy_a = _matmul(dpre0, woe[:1024], tb=True, name="bwd_out_even_dxa")
    dy_b = _matmul(dpre0, woe[1024:], tb=True, name="bwd_out_even_dxb")
    g["w_out_even"] = jnp.concatenate([_matmul(y_a, dpre0, ta=True, name="bwd_out_even_dwa"),
                                       _matmul(y_b, dpre0, ta=True, name="bwd_out_even_dwb")], axis=0)
    token = early_grads_ready({n: g[n] for n in _RS_EARLY})
    d_a, dwa = _conv_a_bwd(proj_a, conv_a_w + token, dy_a)
    g["conv_a_w"] = dwa[:3]
    gdn_g = gdn_g + early_grads_swapped(d_a)
    du, dw, dqd, dkd, dat, deg, dzb, dgn = _gdn_scan_bwd(*gdn_pre, proj_zb, gdn_g, gdn_hist, dy_b)
    dqa, dka, dva, dbr, dar, dal, ddt = _gdn_prep_bwd(qkv_act, braw, araw, alog, dtb, gdn_inv, du, dw, dqd, dkd, dat, deg)
    g["a_log"] = dal[:, 0, 0].reshape(1, nh)
    g["dt_bias"] = ddt[:, 0, 0].reshape(1, nh)
    g["gdn_norm_g"] = dgn[0:1]
    d_pre_qkv, dwb = [], []
    for j, dj in enumerate((dqa, dka, dva)):
        dpj, dwj = _conv_b_bwd(proj_qkv, conv_b_w, dj, j, f"conv_b_bwd{j}")
        d_pre_qkv.append(dpj)
        dwb.append(dwj[:4])
    g["conv_b_w"] = jnp.concatenate(dwb, axis=1)
    d_tail = _pack_tail(dbr, dar)
    pieces = [(d_a, w_a), (d_pre_qkv[0], w_qkv[:, :1024]), (d_pre_qkv[1], w_qkv[:, 1024:2048]),
              (d_pre_qkv[2], w_qkv[:, 2048:]), (dzb, w_zb), (d_tail, w_tail)]
    dws = [_matmul(xb, dj, ta=True, name=f"bwd_proj_even_dw{j}") for j, (dj, _) in enumerate(pieces)]
    dws[-1] = dws[-1][:, :2 * nh]
    g["w_in_even"] = _col_shards(dws, wie.shape[2])
    token = last_grad_ready(g["w_in_even"])
    dx = dpre0
    scale = a
    for j, (dj, wj) in enumerate(pieces):
        dx = _matmul(dj, wj + jnp.asarray(token).astype(BF16) if j == 0 else wj, tb=True, add=dx, add_scale=scale,
                     name=f"bwd_proj_even_dx{j}")
        scale = 1.0
    return loss, dx, g


def _adamw(w, g, m, v, name):
    lead, rows, cols = w.shape
    if rows % SUBLANES == 0 or rows <= 256:
        tr, tc = (rows if rows <= 256 else 256), cols
    else:
        tr, tc = rows, 256
    assert rows % tr == 0 and cols % tc == 0, (name, rows, cols)

    def body(w_ref, g_ref, m_ref, v_ref, d_ref, nm_ref, nv_ref):
        gg = g_ref[...]
        nm = ADAM_B1 * m_ref[...] + (1.0 - ADAM_B1) * gg
        nv = ADAM_B2 * v_ref[...] + (1.0 - ADAM_B2) * jnp.square(gg)
        m_hat = nm / (1.0 - ADAM_B1 ** ADAM_STEP)
        v_hat = nv / (1.0 - ADAM_B2 ** ADAM_STEP)
        d_ref[...] = -ADAM_LR * (m_hat / (jnp.sqrt(v_hat) + ADAM_EPS) + ADAM_WD * w_ref[...])
        nm_ref[...] = nm
        nv_ref[...] = nv

    spec = pl.BlockSpec((1, tr, tc), lambda l, i, j: (l, i, j))
    return pl.pallas_call(
        body, name=name, grid=(lead, rows // tr, cols // tc), in_specs=[spec] * 4, out_specs=[spec] * 3,
        out_shape=[jax.ShapeDtypeStruct(w.shape, F32)] * 3,
        compiler_params=_params("parallel", "parallel", "parallel"))(w, g, m, v)


MESH = pl.DeviceIdType.MESH
N_DEV = 8
HBM_SPEC = pl.BlockSpec(memory_space=pltpu.HBM)
VMEM_SPEC = pl.BlockSpec(memory_space=pltpu.VMEM)


def _coords():
    return lax.axis_index("x"), lax.axis_index("y"), lax.axis_index("c")


def _flip(v, bit):
    return 1 - v if bit else v


def _remote(src, dst, send_sem, recv_sem, dev):
    return pltpu.make_async_remote_copy(src_ref=src, dst_ref=dst, send_sem=send_sem, recv_sem=recv_sem,
                                        device_id=dev, device_id_type=MESH)


def _exchange_small(buf, reduce, name):
    rows = buf.shape[0]

    def body(in_ref, out_ref, slots, send_sems, recv_sems):
        x, y, c = _coords()
        me = 4 * x + 2 * y + c
        slots[me] = in_ref[...]
        peer = lambda k: (_flip(x, (k >> 2) & 1), _flip(y, (k >> 1) & 1), _flip(c, k & 1))
        sends = []
        for k in range(1, N_DEV):
            cp = _remote(in_ref, slots.at[me], send_sems.at[k - 1], recv_sems.at[k - 1], peer(k))
            cp.start()
            sends.append(cp)
        for k in range(1, N_DEV):
            px, py, pc = peer(k)
            _remote(in_ref, slots.at[4 * px + 2 * py + pc], send_sems.at[k - 1], recv_sems.at[k - 1], peer(k)).wait_recv()
        for cp in sends:
            cp.wait_send()
        if reduce:
            acc = slots[0]
            for d in range(1, N_DEV):
                acc = acc + slots[d]
            out_ref[...] = acc
        else:
            out_ref[...] = slots[...]

    out_shape = (rows, LANES) if reduce else (N_DEV, rows, LANES)
    return pl.pallas_call(
        body, name=name, in_specs=[VMEM_SPEC], out_specs=VMEM_SPEC, out_shape=jax.ShapeDtypeStruct(out_shape, F32),
        scratch_shapes=[pltpu.VMEM((N_DEV, rows, LANES), F32), pltpu.SemaphoreType.DMA((N_DEV - 1,)),
                        pltpu.SemaphoreType.DMA((N_DEV - 1,))])(buf)


def _half_rows(half, which):
    return pl.ds(pl.multiple_of(which * half, 16), half)


def _other_chip(x, y, k):
    return _flip(x, (k >> 1) & 1), _flip(y, k & 1)


SEM_SPEC = pl.BlockSpec(memory_space=pltpu.SEMAPHORE)
DATAFLOW = pltpu.SideEffectType.DATAFLOW_SIDE_EFFECTING


def _ici_piece(srcs, lands, send_sems, recv_sems, i, k, x, y, c):
    half = srcs[i].shape[0] // 2
    ox, oy = _other_chip(x, y, k)
    return _remote(srcs[i].at[_half_rows(half, c)], lands[i].at[2 * x + y, _half_rows(half, c)],
                   send_sems.at[3 * i + k - 1], recv_sems.at[3 * i + k - 1], (ox, oy, c)), (ox, oy)


def _gather_start(shards, name):
    n = len(shards)

    def body(*refs):
        srcs, lands = refs[:n], refs[n:2 * n]
        send_sems, recv_sems = refs[2 * n], refs[2 * n + 1]
        token = refs[-1]
        x, y, c = _coords()
        for i in range(n):
            for k in (1, 2, 3):
                _ici_piece(srcs, lands, send_sems, recv_sems, i, k, x, y, c)[0].start()
        token[...] = jnp.zeros_like(token)

    hbm = lambda a: pltpu.with_memory_space_constraint(a, pltpu.HBM)
    lands = [lax.empty((4,) + s.shape, s.dtype) for s in shards]
    out = pl.pallas_call(
        body, name=name,
        out_shape=(pltpu.SemaphoreType.DMA((3 * n,)), pltpu.SemaphoreType.DMA((3 * n,)),
                   *[pltpu.HBM(s.shape, s.dtype) for s in shards], *[pltpu.HBM(a.shape, a.dtype) for a in lands],
                   jax.ShapeDtypeStruct((SUBLANES, LANES), F32)),
        in_specs=[HBM_SPEC] * (2 * n), out_specs=(SEM_SPEC, SEM_SPEC, *[HBM_SPEC] * (2 * n), VMEM_SPEC),
        input_output_aliases={i: 2 + i for i in range(2 * n)},
        compiler_params=pltpu.CompilerParams(has_side_effects=DATAFLOW))(*[hbm(s) for s in shards], *[hbm(a) for a in lands])
    return out[0], out[1], out[2:2 + n], out[2 + n:2 + 2 * n], out[-1]


def _gather_wait(send_sems, recv_sems, srcs, lands, after, name):
    n = len(srcs)

    def body(*refs):
        src_refs, land_refs = refs[:n], refs[n:2 * n]
        send_sems, recv_sems = refs[2 * n], refs[2 * n + 1]
        x, y, c = _coords()
        for i in range(n):
            half = src_refs[i].shape[0] // 2
            for k in (1, 2, 3):
                cp, (ox, oy) = _ici_piece(src_refs, land_refs, send_sems, recv_sems, i, k, x, y, c)
                cp.wait_send()
                piece = land_refs[i].at[2 * ox + oy, _half_rows(half, c)]
                _remote(piece, piece, send_sems.at[3 * i + k - 1], recv_sems.at[3 * i + k - 1], (ox, oy, c)).wait_recv()

    out = pl.pallas_call(
        body, name=name,
        out_shape=(*[pltpu.HBM(s.shape, s.dtype) for s in srcs], *[pltpu.HBM(a.shape, a.dtype) for a in lands]),
        in_specs=[HBM_SPEC] * (2 * n) + [SEM_SPEC, SEM_SPEC, pl.BlockSpec(memory_space=pl.ANY)],
        out_specs=tuple([HBM_SPEC] * (2 * n)), input_output_aliases={i: i for i in range(2 * n)},
        compiler_params=pltpu.CompilerParams(has_side_effects=DATAFLOW))(*srcs, *lands, send_sems, recv_sems, after)
    return out[n:]


def _gather_forward(lands, name):
    n = len(lands)

    def body(*refs):
        ins, outs = refs[:n], refs[n:2 * n]
        send_sems, recv_sems = refs[2 * n:]
        x, y, c = _coords()
        sends = []
        for i in range(n):
            half = ins[i].shape[1] // 2
            for k in (1, 2, 3):
                ox, oy = _other_chip(x, y, k)
                cp = _remote(ins[i].at[2 * ox + oy, _half_rows(half, c)], outs[i].at[2 * ox + oy, _half_rows(half, c)],
                             send_sems.at[3 * i + k - 1], recv_sems.at[3 * i + k - 1], (x, y, 1 - c))
                cp.start()
                sends.append(cp)
        for i in range(n):
            half = ins[i].shape[1] // 2
            for k in (1, 2, 3):
                ox, oy = _other_chip(x, y, k)
                piece = outs[i].at[2 * ox + oy, _half_rows(half, 1 - c)]
                _remote(piece, piece, send_sems.at[3 * i + k - 1], recv_sems.at[3 * i + k - 1], (x, y, 1 - c)).wait_recv()
        for cp in sends:
            cp.wait_send()

    return pl.pallas_call(
        body, name=name, in_specs=[HBM_SPEC] * n, out_specs=[HBM_SPEC] * n,
        out_shape=[jax.ShapeDtypeStruct(a.shape, a.dtype) for a in lands],
        input_output_aliases={i: i for i in range(n)},
        scratch_shapes=[pltpu.SemaphoreType.DMA((3 * n,))] * 2)(*lands)


def _rs_sibling_swap(g4s, name):
    n = len(g4s)

    def body(*refs):
        ins, outs = refs[:n], refs[n:2 * n]
        send_sems, recv_sems = refs[2 * n:]
        x, y, c = _coords()
        sends = []
        for i in range(n):
            half = ins[i].shape[1] // 2
            for s in range(4):
                cp = _remote(ins[i].at[s, _half_rows(half, 1 - c)], outs[i].at[s], send_sems.at[4 * i + s],
                             recv_sems.at[4 * i + s], (x, y, 1 - c))
                cp.start()
                sends.append(cp)
        for cp in sends:
            cp.wait_recv()
        for cp in sends:
            cp.wait_send()

    return pl.pallas_call(
        body, name=name, in_specs=[HBM_SPEC] * n, out_specs=[HBM_SPEC] * n,
        out_shape=[jax.ShapeDtypeStruct((4, g.shape[1] // 2, g.shape[2]), g.dtype) for g in g4s],
        scratch_shapes=[pltpu.SemaphoreType.DMA((4 * n,))] * 2)(*g4s)


def _rs_add_sibling(g4, got, c_idx, name):
    _, rows, cols = g4.shape
    half = rows // 2
    tr = min(half, 256)
    nb = half // tr

    def body(c_ref, a_ref, b_ref, o_ref, ob_ref):
        total = a_ref[...] + b_ref[...]
        o_ref[...] = total
        ob_ref[...] = total.astype(BF16)

    blk = (1, tr, cols)
    out = pl.BlockSpec(blk, lambda s, i, c_ref: (s, i, 0))
    grid_spec = pltpu.PrefetchScalarGridSpec(
        num_scalar_prefetch=1, grid=(4, nb),
        in_specs=[pl.BlockSpec(blk, lambda s, i, c_ref: (s, c_ref[0] * nb + i, 0)), out],
        out_specs=[out, out])
    return pl.pallas_call(
        body, name=name, grid_spec=grid_spec,
        out_shape=[jax.ShapeDtypeStruct(got.shape, F32), jax.ShapeDtypeStruct(got.shape, BF16)],
        compiler_params=_params("parallel", "parallel"))(c_idx, g4, got)


def _rs_add_chips(p4, got3, idx, name):
    _, half, cols = p4.shape
    tr = min(half, 256)
    nb = half // tr

    def body(idx_ref, p_ref, a_ref, b_ref, c_ref, o_ref):
        o_ref[...] = ((p_ref[0] + a_ref[0].astype(F32)) + b_ref[0].astype(F32)) + c_ref[0].astype(F32)

    blk = (1, tr, cols)
    grid_spec = pltpu.PrefetchScalarGridSpec(
        num_scalar_prefetch=1, grid=(nb,),
        in_specs=[pl.BlockSpec(blk, lambda i, idx_ref: (idx_ref[0], i, 0))]
        + [pl.BlockSpec(blk, functools.partial(lambda k, i, idx_ref: (k, i, 0), k)) for k in range(3)],
        out_specs=pl.BlockSpec((tr, cols), lambda i, idx_ref: (idx_ref[1] * nb + i, 0)))
    return pl.pallas_call(body, name=name, grid_spec=grid_spec, out_shape=jax.ShapeDtypeStruct((2 * half, cols), F32),
                          compiler_params=_params("parallel"))(idx, p4, got3, got3, got3)


def _rs_share_halves(bufs, name):
    n = len(bufs)

    def body(*refs):
        ins, outs = refs[:n], refs[n:2 * n]
        send_sems, recv_sems = refs[2 * n:]
        x, y, c = _coords()
        sends = []
        for i in range(n):
            half = ins[i].shape[0] // 2
            cp = _remote(ins[i].at[_half_rows(half, c)], outs[i].at[_half_rows(half, c)], send_sems.at[i],
                         recv_sems.at[i], (x, y, 1 - c))
            cp.start()
            sends.append(cp)
        for i in range(n):
            half = ins[i].shape[0] // 2
            _remote(ins[i].at[_half_rows(half, c)], outs[i].at[_half_rows(half, 1 - c)], send_sems.at[i],
                    recv_sems.at[i], (x, y, 1 - c)).wait_recv()
        for cp in sends:
            cp.wait_send()

    return pl.pallas_call(
        body, name=name, in_specs=[HBM_SPEC] * n, out_specs=[HBM_SPEC] * n,
        out_shape=[jax.ShapeDtypeStruct(b.shape, b.dtype) for b in bufs],
        input_output_aliases={i: i for i in range(n)},
        scratch_shapes=[pltpu.SemaphoreType.DMA((n,))] * 2)(*bufs)


def _scatter_piece(srcs, lands, send_sems, recv_sems, i, k, x, y, c):
    ox, oy = _other_chip(x, y, k)
    return _remote(srcs[i].at[2 * ox + oy], lands[i].at[k - 1], send_sems.at[3 * i + k - 1],
                   recv_sems.at[3 * i + k - 1], (ox, oy, c))


def _rs_scatter_start(p4s, name):
    n = len(p4s)

    def body(*refs):
        srcs, lands = refs[:n], refs[n:2 * n]
        send_sems, recv_sems = refs[2 * n], refs[2 * n + 1]
        token = refs[-1]
        x, y, c = _coords()
        for i in range(n):
            for k in (1, 2, 3):
                _scatter_piece(srcs, lands, send_sems, recv_sems, i, k, x, y, c).start()
        token[...] = jnp.zeros_like(token)

    hbm = lambda a: pltpu.with_memory_space_constraint(a, pltpu.HBM)
    lands = [lax.empty((3,) + p.shape[1:], p.dtype) for p in p4s]
    out = pl.pallas_call(
        body, name=name,
        out_shape=(pltpu.SemaphoreType.DMA((3 * n,)), pltpu.SemaphoreType.DMA((3 * n,)),
                   *[pltpu.HBM(p.shape, p.dtype) for p in p4s], *[pltpu.HBM(a.shape, a.dtype) for a in lands],
                   jax.ShapeDtypeStruct((SUBLANES, LANES), F32)),
        in_specs=[HBM_SPEC] * (2 * n), out_specs=(SEM_SPEC, SEM_SPEC, *[HBM_SPEC] * (2 * n), VMEM_SPEC),
        input_output_aliases={i: 2 + i for i in range(2 * n)},
        compiler_params=pltpu.CompilerParams(has_side_effects=DATAFLOW))(*[hbm(p) for p in p4s], *[hbm(a) for a in lands])
    return out[0], out[1], out[2:2 + n], out[2 + n:2 + 2 * n], out[-1]


def _rs_scatter_wait(send_sems, recv_sems, srcs, lands, after, name):
    n = len(srcs)

    def body(*refs):
        src_refs, land_refs = refs[:n], refs[n:2 * n]
        send_sems, recv_sems = refs[2 * n], refs[2 * n + 1]
        x, y, c = _coords()
        for i in range(n):
            for k in (1, 2, 3):
                cp = _scatter_piece(src_refs, land_refs, send_sems, recv_sems, i, k, x, y, c)
                cp.wait_send()
                cp.wait_recv()

    out = pl.pallas_call(
        body, name=name,
        out_shape=(*[pltpu.HBM(s.shape, s.dtype) for s in srcs], *[pltpu.HBM(a.shape, a.dtype) for a in lands]),
        in_specs=[HBM_SPEC] * (2 * n) + [SEM_SPEC, SEM_SPEC, pl.BlockSpec(memory_space=pl.ANY)],
        out_specs=tuple([HBM_SPEC] * (2 * n)), input_output_aliases={i: i for i in range(2 * n)},
        compiler_params=pltpu.CompilerParams(has_side_effects=DATAFLOW))(*srcs, *lands, send_sems, recv_sems, after)
    return out[n:]


def _swap_piece(srcs, lands, send_sems, recv_sems, i, s, x, y, c):
    half = srcs[i].shape[1] // 2
    return _remote(srcs[i].at[s, _half_rows(half, 1 - c)], lands[i].at[s], send_sems.at[4 * i + s],
                   recv_sems.at[4 * i + s], (x, y, 1 - c))


def _rs_swap_start(g4s, name):
    n = len(g4s)

    def body(*refs):
        srcs, lands = refs[:n], refs[n:2 * n]
        send_sems, recv_sems = refs[2 * n], refs[2 * n + 1]
        token = refs[-1]
        x, y, c = _coords()
        for i in range(n):
            for s in range(4):
                _swap_piece(srcs, lands, send_sems, recv_sems, i, s, x, y, c).start()
        token[...] = jnp.zeros_like(token)

    hbm = lambda a: pltpu.with_memory_space_constraint(a, pltpu.HBM)
    lands = [lax.empty((4, g.shape[1] // 2, g.shape[2]), g.dtype) for g in g4s]
    out = pl.pallas_call(
        body, name=name,
        out_shape=(pltpu.SemaphoreType.DMA((4 * n,)), pltpu.SemaphoreType.DMA((4 * n,)),
                   *[pltpu.HBM(g.shape, g.dtype) for g in g4s], *[pltpu.HBM(a.shape, a.dtype) for a in lands],
                   jax.ShapeDtypeStruct((SUBLANES, LANES), F32)),
        in_specs=[HBM_SPEC] * (2 * n), out_specs=(SEM_SPEC, SEM_SPEC, *[HBM_SPEC] * (2 * n), VMEM_SPEC),
        input_output_aliases={i: 2 + i for i in range(2 * n)},
        compiler_params=pltpu.CompilerParams(has_side_effects=DATAFLOW))(*[hbm(g) for g in g4s], *[hbm(a) for a in lands])
    return out[0], out[1], out[2:2 + n], out[2 + n:2 + 2 * n], out[-1]


def _rs_swap_wait(send_sems, recv_sems, srcs, lands, after, name):
    n = len(srcs)

    def body(*refs):
        src_refs, land_refs = refs[:n], refs[n:2 * n]
        send_sems, recv_sems = refs[2 * n], refs[2 * n + 1]
        x, y, c = _coords()
        for i in range(n):
            for s in range(4):
                cp = _swap_piece(src_refs, land_refs, send_sems, recv_sems, i, s, x, y, c)
                cp.wait_send()
                cp.wait_recv()

    out = pl.pallas_call(
        body, name=name,
        out_shape=(*[pltpu.HBM(s.shape, s.dtype) for s in srcs], *[pltpu.HBM(a.shape, a.dtype) for a in lands]),
        in_specs=[HBM_SPEC] * (2 * n) + [SEM_SPEC, SEM_SPEC, pl.BlockSpec(memory_space=pl.ANY)],
        out_specs=tuple([HBM_SPEC] * (2 * n)), input_output_aliases={i: i for i in range(2 * n)},
        compiler_params=pltpu.CompilerParams(has_side_effects=DATAFLOW))(*srcs, *lands, send_sems, recv_sems, after)
    return out[:n], out[n:]


def _rs_front(g4s, names, tag):
    c_idx = jnp.stack([lax.axis_index("c")]).astype(jnp.int32)
    got = _rs_sibling_swap(g4s, f"rs_sibling_swap_{tag}")
    return [_rs_add_sibling(g, s, c_idx, f"rs_add_sibling_{nm}") for g, s, nm in zip(g4s, got, names)]


def _rs_back(p4s, got3, names):
    x, y, c = _coords()
    idx = jnp.stack([2 * x + y, c]).astype(jnp.int32)
    return [_rs_add_chips(p, t, idx, f"rs_add_chips_{nm}") for (p, _), t, nm in zip(p4s, got3, names)]


def _cols_split(full):
    r, c4 = full.shape
    return full.reshape(r, 4, c4 // 4).transpose(1, 0, 2)


_BIG = {
    "w_in_even": ((1024, 2052), lambda s: s),
    "w_out_even": ((512, 1024), lambda s: s.reshape(2048, 1024)),
    "w_in_odd": ((1024, 2048), lambda s: s),
    "w_out_odd": ((512, 1024), lambda s: s.reshape(2048, 1024)),
    "w_pl": ((512, 256), lambda s: s.reshape(4, 2, 256, 256).transpose(1, 2, 0, 3).reshape(2, 256, 1024)),
    "w_pl_gate": ((512, 1024), lambda s: s.reshape(4, 2, 256, 1024).transpose(1, 0, 2, 3).reshape(2, 1024, 1024)),
}


_RS_EARLY = {
    "w_in_odd": lambda f: f,
    "w_out_odd": lambda f: f.reshape(4, 512, 1024),
    "w_pl_gate1": lambda f: f.reshape(4, 256, 1024),
    "w_pl1": _cols_split,
    "w_out_even": lambda f: f.reshape(4, 512, 1024),
    "w_pl_gate0": lambda f: f.reshape(4, 256, 1024),
    "w_pl0": _cols_split,
}
_RS_LATE = {"w_in_even": lambda f: f}


def _size(shape):
    n = 1
    for d in shape:
        n *= d
    return n


_SMALL = {"a_log": (1, 8), "dt_bias": (1, 8), "gdn_norm_g": (1, 128), "hgrn_norm_g": (1, 128),
          "lower_bounds": (2, 2048), "ln_g": (2, 1024), "ln_b": (2, 1024), "conv_a_w": (3, 1024), "conv_b_w": (4, 3072)}
_CONV_SHARD = {"conv_a_w": (3, 256), "conv_b_w": (4, 768)}


def _pack_small(parts, shapes, head_rows=0):
    rows = []
    for n, shape in shapes.items():
        v = parts[n].reshape(-1)
        rows.append(jnp.pad(v, (0, -v.shape[0] % LANES)).reshape(-1, LANES))
    buf = jnp.concatenate(rows, axis=0)
    return jnp.pad(buf, ((head_rows, -(buf.shape[0] + head_rows) % SUBLANES), (0, 0)))


def _unpack_small(buf, shapes, head_rows=0):
    out, off = {}, head_rows
    for n, shape in shapes.items():
        nrow = -(-_size(shape) // LANES)
        out[n] = buf[off:off + nrow].reshape(-1)[:_size(shape)].reshape(shape)
        off += nrow
    return out


_WEIGHTS = ["w_in_even", "conv_a_w", "conv_b_w", "a_log", "dt_bias", "gdn_norm_g", "w_out_even", "w_in_odd",
            "lower_bounds", "hgrn_norm_g", "w_out_odd", "ln_g", "ln_b", "w_pl", "w_pl_gate"]


def kernel(x, p, w_in_even, conv_a_w, conv_b_w, a_log, dt_bias, gdn_norm_g, w_out_even, w_in_odd, lower_bounds, hgrn_norm_g, w_out_odd, ln_g, ln_b, w_pl, w_pl_gate, loss_target, m_w_in_even, m_conv_a_w, m_conv_b_w, m_a_log, m_dt_bias, m_gdn_norm_g, m_w_out_even, m_w_in_odd, m_lower_bounds, m_hgrn_norm_g, m_w_out_odd, m_ln_g, m_ln_b, m_w_pl, m_w_pl_gate, v_w_in_even, v_conv_a_w, v_conv_b_w, v_a_log, v_dt_bias, v_gdn_norm_g, v_w_out_even, v_w_in_odd, v_lower_bounds, v_hgrn_norm_g, v_w_out_odd, v_ln_g, v_ln_b, v_w_pl, v_w_pl_gate):
    w = dict(zip(_WEIGHTS, (w_in_even, conv_a_w, conv_b_w, a_log, dt_bias, gdn_norm_g, w_out_even, w_in_odd,
                            lower_bounds, hgrn_norm_g, w_out_odd, ln_g, ln_b, w_pl, w_pl_gate)))
    m = dict(zip(_WEIGHTS, (m_w_in_even, m_conv_a_w, m_conv_b_w, m_a_log, m_dt_bias, m_gdn_norm_g, m_w_out_even,
                            m_w_in_odd, m_lower_bounds, m_hgrn_norm_g, m_w_out_odd, m_ln_g, m_ln_b, m_w_pl, m_w_pl_gate)))
    v = dict(zip(_WEIGHTS, (v_w_in_even, v_conv_a_w, v_conv_b_w, v_a_log, v_dt_bias, v_gdn_norm_g, v_w_out_even,
                            v_w_in_odd, v_lower_bounds, v_hgrn_norm_g, v_w_out_odd, v_ln_g, v_ln_b, v_w_pl, v_w_pl_gate)))
    chip = 2 * lax.axis_index("x") + lax.axis_index("y")

    names = list(_BIG)
    shard_shapes = {n: _BIG[n][0] for n in names}
    early, late = names[:1], names[1:]
    shards = {n: w[n].reshape(shard_shapes[n]).astype(BF16) for n in early}
    whole = lambda n, stacked: _BIG[n][1](lax.dynamic_update_slice(stacked, shards[n][None], (chip, 0, 0)))
    conv_mine = _pack_small({n: w[n] for n in _CONV_SHARD}, _CONV_SHARD)
    conv_all = _exchange_small(conv_mine, False, "gather_conv")
    shards, conv_all = lax.optimization_barrier((shards, conv_all))
    first = _gather_start([shards[n] for n in early], "gather_first_start")
    shards.update({n: (w[n].reshape(shard_shapes[n]) + first[4][0, 0]).astype(BF16) for n in late})
    send_sems, recv_sems, srcs, lands, token = _gather_start([shards[n] for n in late], "gather_rest_start")

    def late_weights(after):
        landed = _gather_forward(_gather_wait(send_sems, recv_sems, srcs, lands, after, "gather_rest_wait"),
                                 "gather_rest_forward")
        return {n: whole(n, ga) for n, ga in zip(late, landed)}

    landed = _gather_forward(_gather_wait(*first[:4], token, "gather_first_wait"), "gather_first_forward")
    full = {n: whole(n, ga) for n, ga in zip(early, landed)}
    conv_by_chip = [_unpack_small(conv_all[2 * s], _CONV_SHARD) for s in range(4)]
    for n in _CONV_SHARD:
        full[n] = jnp.concatenate([conv_by_chip[s][n] for s in range(4)], axis=1)
    for n in _SMALL:
        if n not in _CONV_SHARD:
            full[n] = w[n]

    early_rs = {}

    def early_grads_ready(grads):
        early_rs["swap"] = _rs_swap_start([_RS_EARLY[n](grads[n]) for n in _RS_EARLY], "rs_swap_early_start")
        return early_rs["swap"][4][0, 0]

    def early_grads_swapped(after):
        g4s, got = _rs_swap_wait(*early_rs["swap"][:4], after, "rs_swap_early_wait")
        c_idx = jnp.stack([lax.axis_index("c")]).astype(jnp.int32)
        early_rs["p4s"] = [_rs_add_sibling(g_, s_, c_idx, f"rs_add_sibling_{nm}") for g_, s_, nm in zip(g4s, got, _RS_EARLY)]
        early_rs["sems"] = _rs_scatter_start([pb for _, pb in early_rs["p4s"]], "rs_scatter_early_start")
        return early_rs["sems"][4][0, 0]

    late_rs = {}

    def last_grad_ready(grad):
        late_rs["p4s"] = _rs_front([_RS_LATE[n](grad) for n in _RS_LATE], list(_RS_LATE), "late")
        late_rs["sems"] = _rs_scatter_start([pb for _, pb in late_rs["p4s"]], "rs_scatter_late_start")
        return late_rs["sems"][4][0, 0]

    loss_part, dx, g = _local_step(x[0], p[:, 0], loss_target[0], full, late_weights, early_grads_ready,
                                   early_grads_swapped, last_grad_ready, token[0, 0])

    late_p4s, late_sems = late_rs["p4s"], late_rs["sems"]
    got3 = _rs_scatter_wait(*early_rs["sems"][:4], dx, "rs_scatter_early_wait")
    summed = dict(zip(_RS_EARLY, _rs_share_halves(_rs_back(early_rs["p4s"], got3, list(_RS_EARLY)), "rs_share_early")))
    g_big = {n: summed[n] for n in names if n in summed}
    g_big["w_pl"] = jnp.stack([summed["w_pl0"], summed["w_pl1"]])
    g_big["w_pl_gate"] = jnp.stack([summed["w_pl_gate0"], summed["w_pl_gate1"]])
    small_sum = _exchange_small(jnp.concatenate([loss_part, _pack_small(g, _SMALL)], axis=0), True, "reduce_small")
    loss = small_sum[0, 0]
    g_small = _unpack_small(small_sum, _SMALL, head_rows=SUBLANES)
    for n, (rows, cols) in _CONV_SHARD.items():
        g_small[n] = lax.dynamic_slice_in_dim(g_small[n], chip * cols, cols, axis=1)

    grads, delta, new_m, new_v = {}, {}, {}, {}
    for n in late:
        grads[n] = g_big[n].reshape(w[n].shape)
        delta[n], new_m[n], new_v[n] = _adamw(w[n], grads[n], m[n], v[n], f"adamw_{n}")
    own = {n: (_CONV_SHARD[n] if n in _CONV_SHARD else _SMALL[n]) for n in _SMALL}
    packs = [_pack_small({n: src[n] for n in _SMALL}, own)[None] for src in (w, g_small, m, v)]
    outs = [_unpack_small(t[0], own) for t in _adamw(*packs, "adamw_small")]
    for n in _SMALL:
        grads[n] = g_small[n].reshape(w[n].shape)
        delta[n], new_m[n], new_v[n] = (t[n].reshape(w[n].shape) for t in outs)
    got3 = _rs_scatter_wait(*late_sems[:4], new_v["w_in_odd"], "rs_scatter_late_wait")
    (g_in_even,) = _rs_share_halves(_rs_back(late_p4s, got3, list(_RS_LATE)), "rs_share_late")
    for n in early:
        t_ = lambda a: jnp.swapaxes(a, 1, 2)
        g_t = t_(g_in_even.reshape(w[n].shape))
        grads[n] = t_(g_t)
        delta[n], new_m[n], new_v[n] = (t_(o) for o in _adamw(t_(w[n]), g_t, t_(m[n]), t_(v[n]), f"adamw_{n}"))
    return (loss, dx[None], *[grads[n] for n in _WEIGHTS], *[delta[n] for n in _WEIGHTS],
            *[new_m[n] for n in _WEIGHTS], *[new_v[n] for n in _WEIGHTS])
```

```python
import functools

import jax
import jax.numpy as jnp
from jax import lax
from jax.experimental import pallas as pl
from jax.experimental.pallas import tpu as pltpu

F32 = jnp.float32
BF16 = jnp.bfloat16

D_MODEL = 1024
PL_DIM = 256
GDN_HEADS = 8
HEAD_DIM = 128
GDN_CHUNK = 64
HGRN_HEADS = 16
HGRN_CHUNK = 32
HGRN_WIDTH = 2048
DEEPNORM_ALPHA = 4.0 ** 0.25
NORM_EPS = 1e-5
ADAM_LR, ADAM_B1, ADAM_B2, ADAM_EPS, ADAM_WD, ADAM_STEP = 0.001, 0.9, 0.999, 1e-08, 0.01, 10

VMEM_LIMIT = 56 * 1024 * 1024
SUBLANES = 8
LANES = 128


def _params(*sem):
    return pltpu.CompilerParams(dimension_semantics=sem, vmem_limit_bytes=VMEM_LIMIT)


ONE_PASS, THREE_PASS, EXACT_LHS, EXACT_RHS = 0, 1, 2, 3


def _split3(v):
    hi = v.astype(BF16)
    r1 = v - hi.astype(F32)
    mid = r1.astype(BF16)
    return hi, mid, (r1 - mid.astype(F32)).astype(BF16)


def _mm_raw(a, b, kind, prec):
    nb = a.ndim - 2
    ca = a.ndim - 1 if kind[0] == "n" else a.ndim - 2
    cb = b.ndim - 2 if kind[1] == "n" else b.ndim - 1
    dims = (((ca,), (cb,)), (tuple(range(nb)),) * 2)
    dot = lambda p, q: lax.dot_general(p, q, dims, preferred_element_type=F32)
    ah, bh = a.astype(BF16), b.astype(BF16)
    if prec == ONE_PASS:
        return dot(ah, bh)
    if prec == EXACT_LHS:
        b1, b2, b3 = _split3(b)
        return dot(ah, b1) + (dot(ah, b2) + dot(ah, b3))
    if prec == EXACT_RHS:
        a1, a2, a3 = _split3(a)
        return dot(a1, bh) + (dot(a2, bh) + dot(a3, bh))
    al = (a - ah.astype(F32)).astype(BF16)
    bl = (b - bh.astype(F32)).astype(BF16)
    return dot(ah, bh) + (dot(ah, bl) + dot(al, bh))


@functools.partial(jax.custom_vjp, nondiff_argnums=(2, 3))
def _mm_vjp(a, b, kind, hi):
    return _mm_raw(a, b, kind, hi)


def _mm_vjp_fwd(a, b, kind, hi):
    return _mm_raw(a, b, kind, hi), (a, b)


def _mm_vjp_bwd(kind, hi, res, dc):
    a, b = res
    if hi in (EXACT_LHS, EXACT_RHS):
        assert kind == "nn"
        if hi == EXACT_LHS:
            return jnp.zeros_like(a), _mm_raw(a, dc, "tn", EXACT_LHS)
        return _mm_raw(dc, b, "nt", EXACT_RHS), jnp.zeros_like(b)
    if kind == "nn":
        return _mm_raw(dc, b, "nt", hi), _mm_raw(a, dc, "tn", hi)
    if kind == "nt":
        return _mm_raw(dc, b, "nn", hi), _mm_raw(dc, a, "tn", hi)
    return _mm_raw(b, dc, "nt", hi), _mm_raw(a, dc, "nn", hi)


_mm_vjp.defvjp(_mm_vjp_fwd, _mm_vjp_bwd)


def _lane_total(v):
    return jnp.broadcast_to(jnp.sum(v, axis=-1, keepdims=True), v.shape)


def _matmul(a, b, *, name, ta=False, tb=False, add=None, add_scale=1.0, tm=1024, tn=2048, tk=1024):
    m, k = (a.shape[1], a.shape[0]) if ta else a.shape
    n = b.shape[0] if tb else b.shape[1]
    tm, tn, tk = min(tm, m), min(tn, n), min(tk, k)
    tn = tn if n % tn == 0 else tn // 2
    assert m % tm == 0 and n % tn == 0 and k % tk == 0, (name, m, n, k)
    nk = k // tk
    dims = (((0 if ta else 1,), (1 if tb else 0,)), ((), ()))

    def body(*refs):
        a_ref, b_ref = refs[:2]
        o_ref = refs[-1]
        part = lax.dot_general(a_ref[...].astype(BF16), b_ref[...].astype(BF16), dims, preferred_element_type=F32)
        first = (lambda: part) if add is None else (lambda: part + add_scale * refs[2][...])
        if nk == 1:
            o_ref[...] = first()
        else:
            kk = pl.program_id(2)

            @pl.when(kk == 0)
            def _():
                o_ref[...] = first()

            @pl.when(kk > 0)
            def _():
                o_ref[...] += part

    a_spec = pl.BlockSpec((tk, tm), lambda i, j, kk: (kk, i)) if ta else pl.BlockSpec((tm, tk), lambda i, j, kk: (i, kk))
    b_spec = pl.BlockSpec((tn, tk), lambda i, j, kk: (j, kk)) if tb else pl.BlockSpec((tk, tn), lambda i, j, kk: (kk, j))
    o_spec = pl.BlockSpec((tm, tn), lambda i, j, kk: (i, j))
    in_specs = [a_spec, b_spec] + ([o_spec] if add is not None else [])
    args = (a, b) + ((add,) if add is not None else ())
    return pl.pallas_call(
        body, name=name, grid=(m // tm, n // tn, nk), in_specs=in_specs, out_specs=o_spec,
        out_shape=jax.ShapeDtypeStruct((m, n), F32),
        compiler_params=_params("parallel", "parallel", "arbitrary"))(*args)


HALO = SUBLANES


def _halo_specs(tt, width, col, nt):
    r = tt // HALO
    prev = pl.BlockSpec((HALO, width), lambda i: (jnp.maximum(i * r - 1, 0), col))
    nxt = pl.BlockSpec((HALO, width), lambda i: (jnp.minimum((i + 1) * r, nt * r - 1), col))
    return prev, nxt


def _shift_down(ext, k):
    return ext if k == 0 else pltpu.roll(ext, k, 0)


def _shift_up(ext, k):
    return ext if k == 0 else pltpu.roll(ext, ext.shape[0] - k, 0)


def _causal_conv(ext, w, taps):
    acc = None
    for j in range(taps):
        term = w[j:j + 1, :] * _shift_down(ext, taps - 1 - j)
        acc = term if acc is None else acc + term
    return acc[HALO:, :]


def _conv_a_fwd(proj_a, conv_w):
    t = proj_a.shape[0]
    tt = min(t, 256)
    nt = t // tt
    wdt = 1024

    def body(cur_ref, prev_ref, w_ref, y_ref):
        i = pl.program_id(0)
        cur = cur_ref[...]
        h, c, b, z = (cur[:, k * wdt:(k + 1) * wdt] for k in range(4))
        prev = prev_ref[...]
        u_prev = jnp.where(i > 0, prev[:, wdt:2 * wdt] * prev[:, 0:wdt], 0.0)
        ext = jnp.concatenate([u_prev, c * h], axis=0)
        conv = _causal_conv(ext, w_ref[...], 3)
        y_ref[...] = (b * conv * jax.nn.silu(z)).astype(BF16)

    prev_spec, _ = _halo_specs(tt, 4 * wdt, 0, nt)
    return pl.pallas_call(
        body, name="conv_a_fwd", grid=(nt,),
        in_specs=[pl.BlockSpec((tt, 4 * wdt), lambda i: (i, 0)), prev_spec, pl.BlockSpec((3, wdt), lambda i: (0, 0))],
        out_specs=pl.BlockSpec((tt, wdt), lambda i: (i, 0)),
        out_shape=jax.ShapeDtypeStruct((t, wdt), BF16), compiler_params=_params("parallel"))(proj_a, proj_a, conv_w)


def _conv_a_bwd(proj_a, conv_w, dy):
    t = proj_a.shape[0]
    tt = min(t, 256)
    nt = t // tt
    wdt = 1024

    def body(cur_ref, prev_ref, nxt_ref, w_ref, dy_ref, dyn_ref, d_ref, dw_ref):
        i = pl.program_id(0)
        w = w_ref[...]
        cur, prev, nxt = cur_ref[...], prev_ref[...], nxt_ref[...]
        split = lambda a: tuple(a[:, k * wdt:(k + 1) * wdt] for k in range(4))
        h, c, b, z = split(cur)
        hp, cp, _, _ = split(prev)
        hn, cn, bn, zn = split(nxt)
        u_prev = jnp.where(i > 0, cp * hp, 0.0)
        u_ext = jnp.concatenate([u_prev, c * h, cn * hn], axis=0)
        taps = [_shift_down(u_ext, 2 - j)[HALO:, :] for j in range(3)]
        conv = w[0:1, :] * taps[0] + w[1:2, :] * taps[1] + w[2:3, :] * taps[2]
        b_cn = jnp.concatenate([b, bn], axis=0)
        z_cn = jnp.concatenate([z, zn], axis=0)
        dy_cn = jnp.concatenate([dy_ref[...], jnp.where(i < nt - 1, dyn_ref[...], 0.0)], axis=0)
        sg = jax.nn.sigmoid(z_cn)
        silu = z_cn * sg
        d_conv = dy_cn * b_cn * silu
        db = (dy_cn * conv * silu)[:tt, :]
        dz = (dy_cn * b_cn * conv * (sg * (1.0 + z_cn * (1.0 - sg))))[:tt, :]
        du = None
        for j in range(3):
            term = w[j:j + 1, :] * _shift_up(d_conv, 2 - j)
            du = term if du is None else du + term
        du = du[:tt, :]
        d_ref[...] = jnp.concatenate([du * c, du * h, db, dz], axis=1).astype(BF16)

        @pl.when(i == 0)
        def _():
            dw_ref[...] = jnp.zeros_like(dw_ref)

        d_cur = d_conv[:tt, :]
        rows = [jnp.sum(d_cur * taps[j][:tt, :], axis=0, keepdims=True) for j in range(3)]
        dw_ref[0:3, :] += jnp.concatenate(rows, axis=0)

    prev_spec, nxt_spec = _halo_specs(tt, 4 * wdt, 0, nt)
    _, dyn_spec = _halo_specs(tt, wdt, 0, nt)
    return pl.pallas_call(
        body, name="conv_a_bwd", grid=(nt,),
        in_specs=[pl.BlockSpec((tt, 4 * wdt), lambda i: (i, 0)), prev_spec, nxt_spec,
                  pl.BlockSpec((3, wdt), lambda i: (0, 0)), pl.BlockSpec((tt, wdt), lambda i: (i, 0)), dyn_spec],
        out_specs=[pl.BlockSpec((tt, 4 * wdt), lambda i: (i, 0)), pl.BlockSpec((SUBLANES, wdt), lambda i: (0, 0))],
        out_shape=[jax.ShapeDtypeStruct((t, 4 * wdt), BF16), jax.ShapeDtypeStruct((SUBLANES, wdt), F32)],
        compiler_params=_params("arbitrary"))(proj_a, proj_a, proj_a, conv_w, dy, dy)


def _conv_b_fwd(proj_qkv, conv_w):
    t, width = proj_qkv.shape
    tt = min(t, 256)
    nt = t // tt
    wdt = 1024

    def body(cur_ref, prev_ref, w_ref, y_ref):
        i = pl.program_id(1)
        ext = jnp.concatenate([jnp.where(i > 0, prev_ref[...], 0.0), cur_ref[...]], axis=0)
        y_ref[...] = jax.nn.silu(_causal_conv(ext, w_ref[...], 4))

    r = tt // HALO
    return pl.pallas_call(
        body, name="conv_b_fwd", grid=(width // wdt, nt),
        in_specs=[pl.BlockSpec((tt, wdt), lambda j, i: (i, j)),
                  pl.BlockSpec((HALO, wdt), lambda j, i: (jnp.maximum(i * r - 1, 0), j)),
                  pl.BlockSpec((4, wdt), lambda j, i: (0, j))],
        out_specs=pl.BlockSpec((tt, wdt), lambda j, i: (i, j)),
        out_shape=jax.ShapeDtypeStruct((t, width), F32), compiler_params=_params("parallel", "parallel"))(
            proj_qkv, proj_qkv, conv_w)


def _conv_b_bwd(proj_qkv, conv_w, d_act, col, name):
    t = proj_qkv.shape[0]
    tt = min(t, 256)
    nt = t // tt
    wdt = 1024

    def body(cur_ref, prev_ref, nxt_ref, w_ref, da_ref, dan_ref, d_ref, dw_ref):
        i = pl.program_id(0)
        w = w_ref[...]
        u_ext = jnp.concatenate([jnp.where(i > 0, prev_ref[...], 0.0), cur_ref[...], nxt_ref[...]], axis=0)
        taps = [_shift_down(u_ext, 3 - j)[HALO:, :] for j in range(4)]
        conv = w[0:1, :] * taps[0] + w[1:2, :] * taps[1] + w[2:3, :] * taps[2] + w[3:4, :] * taps[3]
        da_cn = jnp.concatenate([da_ref[...], jnp.where(i < nt - 1, dan_ref[...], 0.0)], axis=0)
        sg = jax.nn.sigmoid(conv)
        d_conv = da_cn * (sg * (1.0 + conv * (1.0 - sg)))
        du = None
        for j in range(4):
            term = w[j:j + 1, :] * _shift_up(d_conv, 3 - j)
            du = term if du is None else du + term
        d_ref[...] = du[:tt, :].astype(BF16)

        @pl.when(i == 0)
        def _():
            dw_ref[...] = jnp.zeros_like(dw_ref)

        d_cur = d_conv[:tt, :]
        rows = [jnp.sum(d_cur * taps[j][:tt, :], axis=0, keepdims=True) for j in range(4)]
        dw_ref[0:4, :] += jnp.concatenate(rows, axis=0)

    prev_spec, nxt_spec = _halo_specs(tt, wdt, col, nt)
    _, dan_spec = _halo_specs(tt, wdt, 0, nt)
    return pl.pallas_call(
        body, name=name, grid=(nt,),
        in_specs=[pl.BlockSpec((tt, wdt), lambda i: (i, col)), prev_spec, nxt_spec,
                  pl.BlockSpec((4, wdt), lambda i: (0, col)), pl.BlockSpec((tt, wdt), lambda i: (i, 0)), dan_spec],
        out_specs=[pl.BlockSpec((tt, wdt), lambda i: (i, 0)), pl.BlockSpec((SUBLANES, wdt), lambda i: (0, 0))],
        out_shape=[jax.ShapeDtypeStruct((t, wdt), BF16), jax.ShapeDtypeStruct((SUBLANES, wdt), F32)],
        compiler_params=_params("arbitrary"))(proj_qkv, proj_qkv, proj_qkv, conv_w, d_act, d_act)


def _rms_gate(o, gn, z):
    on = o * lax.rsqrt(jnp.mean(o * o, axis=-1, keepdims=True) + NORM_EPS) * gn
    return on * jax.nn.silu(z)


GDN_PREP_ROWS = 1024


def _unit_lower_inverse(low):
    c = low.shape[-1]
    eye = lax.broadcasted_iota(jnp.int32, low.shape, low.ndim - 2) == lax.broadcasted_iota(jnp.int32, low.shape, low.ndim - 1)
    x = -low
    inv = eye.astype(F32) + x
    for _ in range(c.bit_length() - 2):
        x = _mm_raw(x, x, "nn", THREE_PASS)
        inv = inv + _mm_raw(inv, x, "nn", THREE_PASS)
    return inv


@jax.custom_vjp
def _known_inverse(low, inv):
    return inv


def _known_inverse_fwd(low, inv):
    return inv, inv


def _known_inverse_bwd(inv, d_inv):
    return -_mm_raw(_mm_raw(inv, d_inv, "tn", THREE_PASS), inv, "nt", THREE_PASS), jnp.zeros_like(inv)


_known_inverse.defvjp(_known_inverse_fwd, _known_inverse_bwd)


def _gdn_prep(mm, qa, ka, va, braw, araw, alog, dtb, inv_kept=None):
    n, c, _ = qa.shape
    q = qa * lax.rsqrt(jnp.sum(qa * qa, axis=-1, keepdims=True) + 1e-6) * (HEAD_DIM ** -0.5)
    k = ka * lax.rsqrt(jnp.sum(ka * ka, axis=-1, keepdims=True) + 1e-6)
    beta = jax.nn.sigmoid(braw)
    g = -jnp.exp(alog) * jax.nn.softplus(araw + dtb)
    ri = lax.broadcasted_iota(jnp.int32, (n, c, c), 1)
    ci = lax.broadcasted_iota(jnp.int32, (n, c, c), 2)
    incl, strict, eye = ri >= ci, ri > ci, ri == ci
    gc = mm(incl.astype(F32), g, "nn", EXACT_LHS)
    gc_i = gc[:, :, :c]
    gc_j = mm(jnp.ones((n, c, c), F32), jnp.where(eye, gc_i, 0.0), "nn", EXACT_LHS)
    decay = jnp.where(incl, jnp.exp(jnp.where(incl, gc_i - gc_j, 0.0)), 0.0)
    kb = k * beta
    low = jnp.where(strict, mm(kb, k, "nt", ONE_PASS) * decay, 0.0)
    inv = _unit_lower_inverse(low) if inv_kept is None else _known_inverse(low, inv_kept)
    egc = jnp.exp(gc)
    u = mm(inv, va * beta, "nn", THREE_PASS)
    w = mm(inv, kb * egc, "nn", THREE_PASS)
    attn = jnp.where(incl, mm(q, k, "nt", ONE_PASS) * decay, 0.0)
    g_last = jnp.sum(g, axis=1, keepdims=True)
    outs = (u, w, q * egc, k * jnp.exp(g_last - gc), attn, jnp.exp(g_last))
    return outs + (inv,) if inv_kept is None else outs


def _gdn_scan(mm, u, w, qd, kd, attn, egl, z, gn, state):
    v_new = u - mm(w, state, "nn", ONE_PASS)
    o = mm(qd, state, "nn", ONE_PASS) + mm(attn, v_new, "nn", ONE_PASS)
    new_state = state * egl + mm(kd, v_new, "tn", ONE_PASS)
    return _rms_gate(o, gn, z), new_state


def _chunks(ref_value, n, c):
    return ref_value.reshape(n, c, ref_value.shape[-1])


def _by_head(ref, rows, heads):
    return jnp.stack([ref[rows, pl.ds(h * HEAD_DIM, HEAD_DIM)] for h in range(heads)])


def _store_heads(ref, rows, value):
    for h in range(value.shape[0]):
        ref[rows, pl.ds(h * HEAD_DIM, HEAD_DIM)] = value[h]


def _gdn_prep_specs(tb):
    col = lambda off: pl.BlockSpec((tb, HEAD_DIM), lambda h, i: (i, off + h))
    rep = pl.BlockSpec((1, tb, LANES), lambda h, i: (h, i, 0))
    par = pl.BlockSpec((1, SUBLANES, LANES), lambda h, i: (h, 0, 0))
    att = pl.BlockSpec((1, tb, GDN_CHUNK), lambda h, i: (h, i, 0))
    egl = pl.BlockSpec((1, tb // GDN_CHUNK, SUBLANES, LANES), lambda h, i: (h, i, 0, 0))
    return col, rep, par, att, egl


def _gdn_prep_fwd(qkv_act, braw, araw, alog, dtb):
    t = qkv_act.shape[0]
    tb = min(t, 2 * GDN_PREP_ROWS)
    nt, nc = t // tb, tb // GDN_CHUNK
    width = GDN_HEADS * HEAD_DIM

    def body(q_ref, k_ref, v_ref, br_ref, ar_ref, al_ref, dt_ref, u_ref, w_ref, qd_ref, kd_ref, at_ref, eg_ref, inv_ref):
        ch = lambda r: _chunks(r, nc, GDN_CHUNK)
        u, w, qd, kd, attn, egl, inv = _gdn_prep(_mm_raw, ch(q_ref[...]), ch(k_ref[...]), ch(v_ref[...]), ch(br_ref[0]),
                                                 ch(ar_ref[0]), al_ref[0, 0:1, :], dt_ref[0, 0:1, :])
        u_ref[...] = u.reshape(tb, HEAD_DIM)
        w_ref[...] = w.reshape(tb, HEAD_DIM).astype(BF16)
        qd_ref[...] = qd.reshape(tb, HEAD_DIM).astype(BF16)
        kd_ref[...] = kd.reshape(tb, HEAD_DIM).astype(BF16)
        at_ref[0] = attn.reshape(tb, GDN_CHUNK).astype(BF16)
        eg_ref[0] = jnp.broadcast_to(egl, (nc, SUBLANES, LANES))
        inv_ref[0] = inv.reshape(tb, GDN_CHUNK)

    col, rep, par, att, egl = _gdn_prep_specs(tb)
    h = GDN_HEADS
    return pl.pallas_call(
        body, name="gdn_prep_fwd", grid=(h, nt),
        in_specs=[col(0), col(h), col(2 * h), rep, rep, par, par],
        out_specs=[col(0), col(0), col(0), col(0), att, egl, att],
        out_shape=[jax.ShapeDtypeStruct((t, width), F32)] + [jax.ShapeDtypeStruct((t, width), BF16)] * 3
        + [jax.ShapeDtypeStruct((h, t, GDN_CHUNK), BF16), jax.ShapeDtypeStruct((h, t // GDN_CHUNK, SUBLANES, LANES), F32),
           jax.ShapeDtypeStruct((h, t, GDN_CHUNK), F32)],
        compiler_params=_params("parallel", "parallel"))(qkv_act, qkv_act, qkv_act, braw, araw, alog, dtb)


def _gdn_prep_bwd(qkv_act, braw, araw, alog, dtb, inv, du, dw, dqd, dkd, dattn, degl):
    t = qkv_act.shape[0]
    tb = min(t, GDN_PREP_ROWS)
    nt, nc = t // tb, tb // GDN_CHUNK
    width = GDN_HEADS * HEAD_DIM

    def body(q_ref, k_ref, v_ref, br_ref, ar_ref, al_ref, dt_ref, inv_ref, du_ref, dw_ref, dqd_ref, dkd_ref, dat_ref,
             deg_ref, dq_ref, dk_ref, dv_ref, dbr_ref, dar_ref, dal_ref, ddt_ref):
        @pl.when(pl.program_id(1) == 0)
        def _():
            dal_ref[...] = jnp.zeros_like(dal_ref)
            ddt_ref[...] = jnp.zeros_like(ddt_ref)

        ch = lambda r: _chunks(r, nc, GDN_CHUNK)
        _, vjp = jax.vjp(functools.partial(_gdn_prep, _mm_vjp, inv_kept=ch(inv_ref[0])), ch(q_ref[...]), ch(k_ref[...]),
                         ch(v_ref[...]), ch(br_ref[0]), ch(ar_ref[0]), al_ref[0, 0:1, :], dt_ref[0, 0:1, :])
        dq, dk, dv, dbr, dar, dal, ddt = vjp((ch(du_ref[...]), ch(dw_ref[...]), ch(dqd_ref[...]), ch(dkd_ref[...]),
                                              ch(dat_ref[0]), deg_ref[0][:, 0:1, :]))
        dq_ref[...] = dq.reshape(tb, HEAD_DIM)
        dk_ref[...] = dk.reshape(tb, HEAD_DIM)
        dv_ref[...] = dv.reshape(tb, HEAD_DIM)
        dbr_ref[0] = _lane_total(dbr.reshape(tb, LANES))
        dar_ref[0] = _lane_total(dar.reshape(tb, LANES))
        dal_ref[0, 0:1, :] += _lane_total(dal)
        ddt_ref[0, 0:1, :] += _lane_total(ddt)

    col, rep, par, att, egl = _gdn_prep_specs(tb)
    h = GDN_HEADS
    return pl.pallas_call(
        body, name="gdn_prep_bwd", grid=(h, nt),
        in_specs=[col(0), col(h), col(2 * h), rep, rep, par, par, att, col(0), col(0), col(0), col(0), att, egl],
        out_specs=[col(0), col(0), col(0), rep, rep, par, par],
        out_shape=[jax.ShapeDtypeStruct((t, width), F32)] * 3 + [jax.ShapeDtypeStruct((h, t, LANES), F32)] * 2
        + [jax.ShapeDtypeStruct((h, SUBLANES, LANES), F32)] * 2,
        compiler_params=_params("parallel", "arbitrary"))(qkv_act, qkv_act, qkv_act, braw, araw, alog, dtb, inv,
                                                         du, dw, dqd, dkd, dattn, degl)


def _scan_specs(tb, heads, chunk, rev, nt):
    ti = (lambda i: nt - 1 - i) if rev else (lambda i: i)
    row = pl.BlockSpec((tb, heads * HEAD_DIM), lambda i: (ti(i), 0))
    att = pl.BlockSpec((heads, tb, chunk), lambda i: (0, ti(i), 0))
    egl = pl.BlockSpec((heads, tb // chunk, SUBLANES, LANES), lambda i: (0, ti(i), 0, 0))
    hist = pl.BlockSpec((heads, tb // chunk, HEAD_DIM, HEAD_DIM), lambda i: (0, ti(i), 0, 0))
    gn = pl.BlockSpec((SUBLANES, LANES), lambda i: (0, 0))
    return row, att, egl, hist, gn


def _gdn_scan_fwd(u, w, qd, kd, attn, egl, zb, gn):
    t = u.shape[0]
    tb = min(t, 512)
    nt, nc = t // tb, tb // GDN_CHUNK
    nh = GDN_HEADS

    def body(u_ref, w_ref, qd_ref, kd_ref, at_ref, eg_ref, z_ref, gn_ref, y_ref, hist_ref, s_ref):
        @pl.when(pl.program_id(0) == 0)
        def _():
            s_ref[...] = jnp.zeros_like(s_ref)

        g = gn_ref[0:1, :]
        state = s_ref[...]
        for c in range(nc):
            rows = pl.ds(c * GDN_CHUNK, GDN_CHUNK)
            heads = lambda r: _by_head(r, rows, nh)
            hist_ref[:, c] = state
            y, state = _gdn_scan(_mm_raw, heads(u_ref), heads(w_ref), heads(qd_ref), heads(kd_ref), at_ref[:, rows, :],
                                 eg_ref[:, c, 0:1, :], heads(z_ref), g, state)
            _store_heads(y_ref, rows, y.astype(BF16))
        s_ref[...] = state

    row, att, egs, hist, gns = _scan_specs(tb, nh, GDN_CHUNK, False, nt)
    return pl.pallas_call(
        body, name="gdn_scan_fwd", grid=(nt,), in_specs=[row, row, row, row, att, egs, row, gns], out_specs=[row, hist],
        out_shape=[jax.ShapeDtypeStruct((t, nh * HEAD_DIM), BF16),
                   jax.ShapeDtypeStruct((nh, t // GDN_CHUNK, HEAD_DIM, HEAD_DIM), F32)],
        scratch_shapes=[pltpu.VMEM((nh, HEAD_DIM, HEAD_DIM), F32)],
        compiler_params=_params("arbitrary"))(u, w, qd, kd, attn, egl, zb, gn)


def _gdn_scan_bwd(u, w, qd, kd, attn, egl, zb, gn, hist, dy):
    t = u.shape[0]
    tb = min(t, 512)
    nt, nc = t // tb, tb // GDN_CHUNK
    nh = GDN_HEADS

    def body(u_ref, w_ref, qd_ref, kd_ref, at_ref, eg_ref, z_ref, gn_ref, hist_ref, dy_ref,
             du_ref, dw_ref, dqd_ref, dkd_ref, dat_ref, deg_ref, dz_ref, dgn_ref, ds_ref):
        @pl.when(pl.program_id(0) == 0)
        def _():
            ds_ref[...] = jnp.zeros_like(ds_ref)
            dgn_ref[...] = jnp.zeros_like(dgn_ref)

        g = gn_ref[0:1, :]
        d_state = ds_ref[...]
        for c in reversed(range(nc)):
            rows = pl.ds(c * GDN_CHUNK, GDN_CHUNK)
            heads = lambda r: _by_head(r, rows, nh).astype(F32)
            _, vjp = jax.vjp(functools.partial(_gdn_scan, _mm_vjp), heads(u_ref), heads(w_ref), heads(qd_ref),
                             heads(kd_ref), at_ref[:, rows, :].astype(F32), eg_ref[:, c, 0:1, :], heads(z_ref), g,
                             hist_ref[:, c])
            du, dw, dqd, dkd, dat, deg, dz, dgn, d_state = vjp((heads(dy_ref), d_state))
            _store_heads(du_ref, rows, du)
            _store_heads(dw_ref, rows, dw)
            _store_heads(dqd_ref, rows, dqd)
            _store_heads(dkd_ref, rows, dkd)
            dat_ref[:, rows, :] = dat
            deg_ref[:, c] = jnp.broadcast_to(deg, (nh, SUBLANES, LANES))
            _store_heads(dz_ref, rows, dz.astype(BF16))
            dgn_ref[0:1, :] += dgn
        ds_ref[...] = d_state

    row, att, egs, hists, gns = _scan_specs(tb, nh, GDN_CHUNK, True, nt)
    wide = jax.ShapeDtypeStruct((t, nh * HEAD_DIM), F32)
    return pl.pallas_call(
        body, name="gdn_scan_bwd", grid=(nt,),
        in_specs=[row, row, row, row, att, egs, row, gns, hists, row],
        out_specs=[row, row, row, row, att, egs, row, gns],
        out_shape=[wide] * 4 + [jax.ShapeDtypeStruct((nh, t, GDN_CHUNK), F32),
                                jax.ShapeDtypeStruct((nh, t // GDN_CHUNK, SUBLANES, LANES), F32),
                                jax.ShapeDtypeStruct((t, nh * HEAD_DIM), BF16),
                                jax.ShapeDtypeStruct((SUBLANES, LANES), F32)],
        scratch_shapes=[pltpu.VMEM((nh, HEAD_DIM, HEAD_DIM), F32)],
        compiler_params=_params("arbitrary"))(u, w, qd, kd, attn, egl, zb, gn, hist, dy)


def _hgrn_prep(mm, qr, fr, lbl):
    n, c, _ = qr.shape
    lb = jax.nn.sigmoid(lbl[1:2, :] - lbl[0:1, :])
    f = lb + (1.0 - lb) * jax.nn.sigmoid(fr)
    q = jax.nn.silu(qr)
    k = 1.0 - f
    logf = jnp.log(f)
    ri = lax.broadcasted_iota(jnp.int32, (n, c, c), 1)
    ci = lax.broadcasted_iota(jnp.int32, (n, c, c), 2)
    b = mm((ri >= ci).astype(F32), logf, "nn", EXACT_LHS)
    attn = _hgrn_attn(mm, q, k, b)
    b_last = jnp.sum(logf, axis=1, keepdims=True)
    return q * jnp.exp(b), k * jnp.exp(b_last - b), attn, jnp.exp(b_last)


HGRN_SUB = 8
HGRN_PREP_ROWS = 2048


@functools.partial(jax.custom_vjp, nondiff_argnums=(1,))
def _roll_rows(x, shift):
    return pltpu.roll(x, shift, x.ndim - 2)


def _roll_rows_fwd(x, shift):
    return _roll_rows(x, shift), None


def _roll_rows_bwd(shift, _, d):
    return (pltpu.roll(d, d.shape[-2] - shift, d.ndim - 2),)


_roll_rows.defvjp(_roll_rows_fwd, _roll_rows_bwd)


@jax.custom_vjp
def _exp_clamped(v):
    return jnp.exp(jnp.minimum(v, 0.0))


def _exp_clamped_fwd(v):
    out = jnp.exp(jnp.minimum(v, 0.0))
    return out, out


def _exp_clamped_bwd(out, d):
    return (d * out,)


_exp_clamped.defvjp(_exp_clamped_fwd, _exp_clamped_bwd)


def _hgrn_attn(mm, q, k, b):
    n, c, d = q.shape
    sb = HGRN_SUB
    sub = lambda a: a.reshape(n * c // sb, sb, d)
    qs, ks, bs = sub(q), sub(k), sub(b)
    row = lax.broadcasted_iota(jnp.int32, (n, c, c), 1)
    col = lax.broadcasted_iota(jnp.int32, (n, c, c), 2)
    same_block = (row & -sb) == (col & -sb)
    attn = None
    for delta in range(sb):
        if delta == 0:
            prod = qs * ks
        else:
            prod = qs * _roll_rows(ks, delta) * _exp_clamped(bs - _roll_rows(bs, delta))
        sums = jnp.sum(prod, axis=-1, keepdims=True).reshape(n, c, 1)
        term = jnp.where(same_block & (row - col == delta), sums, 0.0)
        attn = term if attn is None else attn + term
    far = [jnp.zeros((n, sb, c), F32)]
    for i in range(1, c // sb):
        r0 = i * sb
        bi = b[:, r0:r0 + sb, :]
        ref = bi[:, 0:1, :]
        before = jnp.concatenate([k[:, :r0, :] * jnp.exp(ref - b[:, :r0, :]), jnp.zeros((n, c - r0, d), F32)], axis=1)
        far.append(mm(q[:, r0:r0 + sb, :] * jnp.exp(bi - ref), before, "nt", ONE_PASS))
    return attn + jnp.concatenate(far, axis=1)


def _hgrn_scan(mm, qe, kd, attn, ebl, iv, z, gn, state):
    o = mm(qe, state, "nt", ONE_PASS) + mm(attn, iv, "nn", ONE_PASS)
    new_state = state * ebl + mm(iv, kd, "tn", ONE_PASS)
    return _rms_gate(o, gn, z), new_state


def _hgrn_prep_specs(tb):
    col = pl.BlockSpec((tb, HEAD_DIM), lambda h, i: (i, h))
    lbs = pl.BlockSpec((2, HEAD_DIM), lambda h, i: (0, h))
    att = pl.BlockSpec((1, tb, HGRN_CHUNK), lambda h, i: (h, i, 0))
    ebl = pl.BlockSpec((1, tb // HGRN_CHUNK, SUBLANES, LANES), lambda h, i: (h, i, 0, 0))
    return col, lbs, att, ebl


def _hgrn_prep_fwd(qr, fr, lower_bounds):
    t = qr.shape[0]
    tb = min(t, HGRN_PREP_ROWS)
    nt, nc = t // tb, tb // HGRN_CHUNK
    hh = HGRN_HEADS

    def body(q_ref, f_ref, lb_ref, qe_ref, kd_ref, at_ref, eb_ref):
        ch = lambda r: _chunks(r, nc, HGRN_CHUNK)
        qe, kd, attn, ebl = _hgrn_prep(_mm_raw, ch(q_ref[...]), ch(f_ref[...]), lb_ref[...])
        qe_ref[...] = qe.reshape(tb, HEAD_DIM).astype(BF16)
        kd_ref[...] = kd.reshape(tb, HEAD_DIM).astype(BF16)
        at_ref[0] = attn.reshape(tb, HGRN_CHUNK).astype(BF16)
        eb_ref[0] = jnp.broadcast_to(ebl, (nc, SUBLANES, LANES))

    col, lbs, att, ebs = _hgrn_prep_specs(tb)
    return pl.pallas_call(
        body, name="hgrn_prep_fwd", grid=(hh, nt), in_specs=[col, col, lbs], out_specs=[col, col, att, ebs],
        out_shape=[jax.ShapeDtypeStruct((t, HGRN_WIDTH), BF16)] * 2
        + [jax.ShapeDtypeStruct((hh, t, HGRN_CHUNK), BF16), jax.ShapeDtypeStruct((hh, t // HGRN_CHUNK, SUBLANES, LANES), F32)],
        compiler_params=_params("parallel", "parallel"))(qr, fr, lower_bounds)


def _hgrn_prep_bwd(qr, fr, lower_bounds, dqe, dkd, dattn, debl):
    t = qr.shape[0]
    tb = min(t, HGRN_PREP_ROWS)
    nt, nc = t // tb, tb // HGRN_CHUNK
    hh = HGRN_HEADS

    def body(q_ref, f_ref, lb_ref, dqe_ref, dkd_ref, dat_ref, deb_ref, dq_ref, df_ref, dlb_ref):
        @pl.when(pl.program_id(1) == 0)
        def _():
            dlb_ref[...] = jnp.zeros_like(dlb_ref)

        ch = lambda r: _chunks(r, nc, HGRN_CHUNK)
        _, vjp = jax.vjp(functools.partial(_hgrn_prep, _mm_vjp), ch(q_ref[...]), ch(f_ref[...]), lb_ref[...])
        dq, df, dlb = vjp((ch(dqe_ref[...]), ch(dkd_ref[...]), ch(dat_ref[0]), deb_ref[0][:, 0:1, :]))
        dq_ref[...] = dq.reshape(tb, HEAD_DIM).astype(BF16)
        df_ref[...] = df.reshape(tb, HEAD_DIM).astype(BF16)
        dlb_ref[...] += dlb

    col, lbs, att, ebs = _hgrn_prep_specs(tb)
    return pl.pallas_call(
        body, name="hgrn_prep_bwd", grid=(hh, nt), in_specs=[col, col, lbs, col, col, att, ebs],
        out_specs=[col, col, lbs],
        out_shape=[jax.ShapeDtypeStruct((t, HGRN_WIDTH), BF16)] * 2 + [jax.ShapeDtypeStruct((2, HGRN_WIDTH), F32)],
        compiler_params=_params("parallel", "arbitrary"))(qr, fr, lower_bounds, dqe, dkd, dattn, debl)


def _hgrn_scan_fwd(qe, kd, attn, ebl, iv, z, gn):
    t = qe.shape[0]
    tb = min(t, 256)
    nt, nc = t // tb, tb // HGRN_CHUNK
    hh = HGRN_HEADS

    def body(qe_ref, kd_ref, at_ref, eb_ref, i_ref, z_ref, gn_ref, y_ref, hist_ref, s_ref):
        @pl.when(pl.program_id(0) == 0)
        def _():
            s_ref[...] = jnp.zeros_like(s_ref)

        g = gn_ref[0:1, :]
        state = s_ref[...]
        for c in range(nc):
            rows = pl.ds(c * HGRN_CHUNK, HGRN_CHUNK)
            heads = lambda r: _by_head(r, rows, hh)
            hist_ref[:, c] = state
            y, state = _hgrn_scan(_mm_raw, heads(qe_ref), heads(kd_ref), at_ref[:, rows, :], eb_ref[:, c, 0:1, :],
                                  heads(i_ref), heads(z_ref), g, state)
            _store_heads(y_ref, rows, y.astype(BF16))
        s_ref[...] = state

    row, att, ebs, hist, gns = _scan_specs(tb, hh, HGRN_CHUNK, False, nt)
    return pl.pallas_call(
        body, name="hgrn_scan_fwd", grid=(nt,), in_specs=[row, row, att, ebs, row, row, gns], out_specs=[row, hist],
        out_shape=[jax.ShapeDtypeStruct((t, HGRN_WIDTH), BF16),
                   jax.ShapeDtypeStruct((hh, t // HGRN_CHUNK, HEAD_DIM, HEAD_DIM), F32)],
        scratch_shapes=[pltpu.VMEM((hh, HEAD_DIM, HEAD_DIM), F32)],
        compiler_params=_params("arbitrary"))(qe, kd, attn, ebl, iv, z, gn)


def _hgrn_scan_bwd(qe, kd, attn, ebl, iv, z, gn, hist, dy):
    t = qe.shape[0]
    tb = min(t, 256)
    nt, nc = t // tb, tb // HGRN_CHUNK
    hh = HGRN_HEADS

    def body(qe_ref, kd_ref, at_ref, eb_ref, i_ref, z_ref, gn_ref, hist_ref, dy_ref,
             dqe_ref, dkd_ref, dat_ref, deb_ref, di_ref, dz_ref, dgn_ref, ds_ref):
        @pl.when(pl.program_id(0) == 0)
        def _():
            ds_ref[...] = jnp.zeros_like(ds_ref)
            dgn_ref[...] = jnp.zeros_like(dgn_ref)

        g = gn_ref[0:1, :]
        d_state = ds_ref[...]
        for c in reversed(range(nc)):
            rows = pl.ds(c * HGRN_CHUNK, HGRN_CHUNK)
            heads = lambda r: _by_head(r, rows, hh).astype(F32)
            _, vjp = jax.vjp(functools.partial(_hgrn_scan, _mm_vjp), heads(qe_ref), heads(kd_ref),
                             at_ref[:, rows, :].astype(F32), eb_ref[:, c, 0:1, :], heads(i_ref), heads(z_ref), g,
                             hist_ref[:, c])
            dqe, dkd, dat, deb, di, dz, dgn, d_state = vjp((heads(dy_ref), d_state))
            _store_heads(dqe_ref, rows, dqe)
            _store_heads(dkd_ref, rows, dkd)
            dat_ref[:, rows, :] = dat
            deb_ref[:, c] = jnp.broadcast_to(deb, (hh, SUBLANES, LANES))
            _store_heads(di_ref, rows, di.astype(BF16))
            _store_heads(dz_ref, rows, dz.astype(BF16))
            dgn_ref[0:1, :] += dgn
        ds_ref[...] = d_state

    row, att, ebs, hists, gns = _scan_specs(tb, hh, HGRN_CHUNK, True, nt)
    wide = lambda dt: jax.ShapeDtypeStruct((t, HGRN_WIDTH), dt)
    return pl.pallas_call(
        body, name="hgrn_scan_bwd", grid=(nt,),
        in_specs=[row, row, att, ebs, row, row, gns, hists, row],
        out_specs=[row, row, att, ebs, row, row, gns],
        out_shape=[wide(F32), wide(F32), jax.ShapeDtypeStruct((hh, t, HGRN_CHUNK), F32),
                   jax.ShapeDtypeStruct((hh, t // HGRN_CHUNK, SUBLANES, LANES), F32), wide(BF16), wide(BF16),
                   jax.ShapeDtypeStruct((SUBLANES, LANES), F32)],
        scratch_shapes=[pltpu.VMEM((hh, HEAD_DIM, HEAD_DIM), F32)],
        compiler_params=_params("arbitrary"))(qe, kd, attn, ebl, iv, z, gn, hist, dy)


def _layer_norm(pre, g, b):
    mu = jnp.mean(pre, axis=-1, keepdims=True)
    d = pre - mu
    var = jnp.mean(d * d, axis=-1, keepdims=True)
    return d * lax.rsqrt(var + NORM_EPS) * g + b


def _lnpl_fwd(xin, s, p, wg, wpl, ln_g, ln_b):
    t = xin.shape[0]
    tt = min(t, 512)

    def body(x_ref, s_ref, p_ref, wg_ref, wpl_ref, g_ref, b_ref, o_ref, ob_ref):
        xn = _layer_norm(DEEPNORM_ALPHA * x_ref[...] + s_ref[...], g_ref[...], b_ref[...])
        gate = jax.nn.sigmoid(_mm_raw(xn, wg_ref[...], "nn", False))
        out = xn + _mm_raw(p_ref[...], wpl_ref[...], "nn", False) * gate
        o_ref[...] = out
        ob_ref[...] = out.astype(BF16)

    row = lambda w: pl.BlockSpec((tt, w), lambda i: (i, 0))
    full = lambda a: pl.BlockSpec(a.shape, lambda i: (0, 0))
    return pl.pallas_call(
        body, name="lnpl_fwd", grid=(t // tt,),
        in_specs=[row(D_MODEL), row(D_MODEL), row(PL_DIM), full(wg), full(wpl), full(ln_g), full(ln_b)],
        out_specs=[row(D_MODEL), row(D_MODEL)],
        out_shape=[jax.ShapeDtypeStruct((t, D_MODEL), F32), jax.ShapeDtypeStruct((t, D_MODEL), BF16)],
        compiler_params=_params("parallel"))(xin, s, p, wg, wpl, ln_g, ln_b)


def _lnpl_bwd(xin, s, p, wg, wpl, ln_g, ln_b, upstream, last, name):
    t = xin.shape[0]
    tt = min(t, 512)

    def body(x_ref, s_ref, p_ref, wg_ref, wpl_ref, g_ref, b_ref, up_ref,
             dpre_ref, dwg_ref, dwpl_ref, dg_ref, db_ref, loss_ref):
        @pl.when(pl.program_id(0) == 0)
        def _():
            for r in (dwg_ref, dwpl_ref, dg_ref, db_ref, loss_ref):
                r[...] = jnp.zeros_like(r)

        pre = DEEPNORM_ALPHA * x_ref[...] + s_ref[...]
        xn, ln_vjp = jax.vjp(_layer_norm, pre, g_ref[...], b_ref[...])
        gate = jax.nn.sigmoid(_mm_raw(xn, wg_ref[...], "nn", False))
        plv = _mm_raw(p_ref[...], wpl_ref[...], "nn", False)
        if last:
            err = xn + plv * gate - up_ref[...]
            dout = err * (1.0 / D_MODEL)
            tot = jnp.sum(jnp.sum(err * err, axis=1, keepdims=True), axis=0, keepdims=True) * (0.5 / D_MODEL)
            loss_ref[...] += jnp.broadcast_to(tot, loss_ref.shape)
        else:
            dout = up_ref[...]
        dplv = dout * gate
        dlogits = dout * plv * gate * (1.0 - gate)
        dwg_ref[...] += _mm_raw(xn, dlogits, "tn", False)
        dwpl_ref[...] += _mm_raw(p_ref[...], dplv, "tn", False)
        dxn = dout + _mm_raw(dlogits, wg_ref[...], "nt", False)
        dpre, dg, db = ln_vjp(dxn)
        dpre_ref[...] = dpre
        dg_ref[...] += dg
        db_ref[...] += db

    row = lambda w: pl.BlockSpec((tt, w), lambda i: (i, 0))
    full = lambda shape: pl.BlockSpec(shape, lambda i: (0, 0))
    return pl.pallas_call(
        body, name=name, grid=(t // tt,),
        in_specs=[row(D_MODEL), row(D_MODEL), row(PL_DIM), full(wg.shape), full(wpl.shape), full(ln_g.shape),
                  full(ln_b.shape), row(D_MODEL)],
        out_specs=[row(D_MODEL), full(wg.shape), full(wpl.shape), full(ln_g.shape), full(ln_b.shape),
                   full((SUBLANES, LANES))],
        out_shape=[jax.ShapeDtypeStruct((t, D_MODEL), F32), jax.ShapeDtypeStruct(wg.shape, F32),
                   jax.ShapeDtypeStruct(wpl.shape, F32), jax.ShapeDtypeStruct(ln_g.shape, F32),
                   jax.ShapeDtypeStruct(ln_b.shape, F32), jax.ShapeDtypeStruct((SUBLANES, LANES), F32)],
        compiler_params=_params("arbitrary"))(xin, s, p, wg, wpl, ln_g, ln_b, upstream)


def _pack_tail(dbr, dar):
    nh, t, _ = dbr.shape
    tt = min(t, 512)

    def body(b_ref, a_ref, o_ref):
        lane = lax.broadcasted_iota(jnp.int32, (tt, LANES), 1)
        acc = jnp.zeros((tt, LANES), F32)
        for h in range(nh):
            acc = jnp.where(lane == h, b_ref[h], acc)
            acc = jnp.where(lane == nh + h, a_ref[h], acc)
        o_ref[...] = acc.astype(BF16)

    spec = pl.BlockSpec((nh, tt, LANES), lambda i: (0, i, 0))
    return pl.pallas_call(
        body, name="pack_tail", grid=(t // tt,), in_specs=[spec, spec], out_specs=pl.BlockSpec((tt, LANES), lambda i: (i, 0)),
        out_shape=jax.ShapeDtypeStruct((t, LANES), BF16), compiler_params=_params("parallel"))(dbr, dar)


def _rep_rows(v):
    return jnp.broadcast_to(v.reshape(1, LANES), (SUBLANES, LANES))


def _rep_heads(v):
    return jnp.broadcast_to(v.reshape(-1, 1, 1), (v.shape[0], SUBLANES, LANES))


def _col_range(stacked, lo, hi):
    c = stacked.shape[2]
    parts = [stacked[s, :, max(lo, s * c) - s * c:min(hi, (s + 1) * c) - s * c]
             for s in range(4) if max(lo, s * c) < min(hi, (s + 1) * c)]
    return parts[0] if len(parts) == 1 else jnp.concatenate(parts, axis=1)


def _col_shards(pieces, c):
    shards, offs, o = [], [], 0
    for pc in pieces:
        offs.append(o)
        o += pc.shape[1]
    for s in range(4):
        lo, hi = s * c, (s + 1) * c
        parts = [pc[:, max(lo, o) - o:min(hi, o + pc.shape[1]) - o] for pc, o in zip(pieces, offs)
                 if max(lo, o) < min(hi, o + pc.shape[1])]
        shards.append(parts[0] if len(parts) == 1 else jnp.concatenate(parts, axis=1))
    return jnp.stack(shards)


def _local_step(x, p, target, w, late_weights, early_grads_ready, early_grads_swapped, last_grad_ready, start_token):
    a = DEEPNORM_ALPHA
    nh = GDN_HEADS
    xb = (x + start_token).astype(BF16)
    wie = w["w_in_even"]
    w_a, w_qkv, w_zb = _col_range(wie, 0, 4096), _col_range(wie, 4096, 7168), _col_range(wie, 7168, 8192)
    w_tail = jnp.pad(_col_range(wie, 8192, 8192 + 2 * nh), ((0, 0), (0, LANES - 2 * nh)))
    conv_a_w, conv_b_w = w["conv_a_w"], w["conv_b_w"]
    ln_g0, ln_b0, ln_g1, ln_b1 = (v.reshape(1, D_MODEL) for v in (w["ln_g"][0], w["ln_b"][0], w["ln_g"][1], w["ln_b"][1]))
    alog, dtb = _rep_heads(w["a_log"].reshape(nh)), _rep_heads(w["dt_bias"].reshape(nh))
    gdn_g, hgrn_g = _rep_rows(w["gdn_norm_g"]), _rep_rows(w["hgrn_norm_g"])

    proj_a = _matmul(xb, w_a, name="fwd_proj_a")
    proj_qkv = _matmul(xb, w_qkv, name="fwd_proj_qkv")
    proj_zb = _matmul(xb, w_zb, name="fwd_proj_zb")
    proj_tail = _matmul(xb, w_tail, name="fwd_proj_tail")
    rep = lambda cols: jnp.broadcast_to(cols.T[:, :, None], (nh, cols.shape[0], LANES))
    braw, araw = rep(proj_tail[:, :nh]), rep(proj_tail[:, nh:2 * nh])
    y_a = _conv_a_fwd(proj_a, conv_a_w)
    qkv_act = _conv_b_fwd(proj_qkv, conv_b_w)
    *gdn_pre, gdn_inv = _gdn_prep_fwd(qkv_act, braw, araw, alog, dtb)
    y_b, gdn_hist = _gdn_scan_fwd(*gdn_pre, proj_zb, gdn_g)
    w = {**w, **late_weights(y_b)}
    woe, wio, woo = w["w_out_even"], w["w_in_odd"], w["w_out_odd"]
    s0 = _matmul(y_b, woe[1024:], name="fwd_out_even_b", add=_matmul(y_a, woe[:1024], name="fwd_out_even_a"))
    x1, x1b = _lnpl_fwd(x, s0, p[0], w["w_pl_gate"][0], w["w_pl"][0], ln_g0, ln_b0)
    proj_o = [_matmul(x1b, wio[j], name=f"fwd_proj_odd{j}") for j in range(4)]
    hgrn_pre = _hgrn_prep_fwd(proj_o[0], proj_o[1], w["lower_bounds"])
    y_o, hgrn_hist = _hgrn_scan_fwd(*hgrn_pre, proj_o[2], proj_o[3], hgrn_g)
    s1 = _matmul(y_o, woo, name="fwd_out_odd")

    g = {}
    dpre1, dwg1, dwpl1, dlng1, dlnb1, loss = _lnpl_bwd(x1, s1, p[1], w["w_pl_gate"][1], w["w_pl"][1], ln_g1, ln_b1,
                                                     target, True, "lnpl_bwd_odd")
    dy_o = _matmul(dpre1, woo, tb=True, name="bwd_out_odd_dx")
    g["w_out_odd"] = _matmul(y_o, dpre1, ta=True, name="bwd_out_odd_dw")
    dqe, dkd, dat, deb, di, dz, dhg = _hgrn_scan_bwd(*hgrn_pre, proj_o[2], proj_o[3], hgrn_g, hgrn_hist, dy_o)
    dq, df, dlb = _hgrn_prep_bwd(proj_o[0], proj_o[1], w["lower_bounds"], dqe, dkd, dat, deb)
    dx1 = dpre1
    scale = a
    dws = []
    for j, dj in enumerate((dq, df, di, dz)):
        dx1 = _matmul(dj, wio[j], tb=True, add=dx1, add_scale=scale, name=f"bwd_proj_odd_dx{j}")
        scale = 1.0
        dws.append(_matmul(x1b, dj, ta=True, name=f"bwd_proj_odd_dw{j}"))
    g["w_in_odd"] = jnp.stack(dws)
    g["hgrn_norm_g"] = dhg[0:1]
    g["lower_bounds"] = dlb
    g["w_pl_gate1"], g["w_pl1"] = dwg1, dwpl1

    dpre0, dwg0, dwpl0, dlng0, dlnb0, _ = _lnpl_bwd(x, s0, p[0], w["w_pl_gate"][0], w["w_pl"][0], ln_g0, ln_b0,
                                                  dx1, False, "lnpl_bwd_even")
    g["w_pl_gate0"], g["w_pl0"] = dwg0, dwpl0
    g["ln_g"] = jnp.concatenate([dlng0, dlng1], axis=0)
    g["ln_b"] = jnp.concatenate([dlnb0, dlnb1], axis=0)
    dy_a = _matmul(dpre0, woe[:1024], tb=True, name="bwd_out_even_dxa")
    dy_b = _matmul(dpre0, woe[1024:], tb=True, name="bwd_out_even_dxb")
    g["w_out_even"] = jnp.concatenate([_matmul(y_a, dpre0, ta=True, name="bwd_out_even_dwa"),
                                       _matmul(y_b, dpre0, ta=True, name="bwd_out_even_dwb")], axis=0)
    token = early_grads_ready({n: g[n] for n in _RS_EARLY})
    d_a, dwa = _conv_a_bwd(proj_a, conv_a_w + token, dy_a)
    g["conv_a_w"] = dwa[:3]
    gdn_g = gdn_g + early_grads_swapped(d_a)
    du, dw, dqd, dkd, dat, deg, dzb, dgn = _gdn_scan_bwd(*gdn_pre, proj_zb, gdn_g, gdn_hist, dy_b)
    dqa, dka, dva, dbr, dar, dal, ddt = _gdn_prep_bwd(qkv_act, braw, araw, alog, dtb, gdn_inv, du, dw, dqd, dkd, dat, deg)
    g["a_log"] = dal[:, 0, 0].reshape(1, nh)
    g["dt_bias"] = ddt[:, 0, 0].reshape(1, nh)
    g["gdn_norm_g"] = dgn[0:1]
    d_pre_qkv, dwb = [], []
    for j, dj in enumerate((dqa, dka, dva)):
        dpj, dwj = _conv_b_bwd(proj_qkv, conv_b_w, dj, j, f"conv_b_bwd{j}")
        d_pre_qkv.append(dpj)
        dwb.append(dwj[:4])
    g["conv_b_w"] = jnp.concatenate(dwb, axis=1)
    d_tail = _pack_tail(dbr, dar)
    pieces = [(d_a, w_a), (d_pre_qkv[0], w_qkv[:, :1024]), (d_pre_qkv[1], w_qkv[:, 1024:2048]),
              (d_pre_qkv[2], w_qkv[:, 2048:]), (dzb, w_zb), (d_tail, w_tail)]
    dws = [_matmul(xb, dj, ta=True, name=f"bwd_proj_even_dw{j}") for j, (dj, _) in enumerate(pieces)]
    dws[-1] = dws[-1][:, :2 * nh]
    g["w_in_even"] = _col_shards(dws, wie.shape[2])
    token = last_grad_ready(g["w_in_even"])
    dx = dpre0
    scale = a
    for j, (dj, wj) in enumerate(pieces):
        dx = _matmul(dj, wj + jnp.asarray(token).astype(BF16) if j == 0 else wj, tb=True, add=dx, add_scale=scale,
                     name=f"bwd_proj_even_dx{j}")
        scale = 1.0
    return loss, dx, g


def _adamw(w, g, m, v, name):
    lead, rows, cols = w.shape
    if rows % SUBLANES == 0 or rows <= 256:
        tr, tc = (rows if rows <= 256 else 256), cols
    else:
        tr, tc = rows, 256
    assert rows % tr == 0 and cols % tc == 0, (name, rows, cols)

    def body(w_ref, g_ref, m_ref, v_ref, d_ref, nm_ref, nv_ref):
        gg = g_ref[...]
        nm = ADAM_B1 * m_ref[...] + (1.0 - ADAM_B1) * gg
        nv = ADAM_B2 * v_ref[...] + (1.0 - ADAM_B2) * jnp.square(gg)
        m_hat = nm / (1.0 - ADAM_B1 ** ADAM_STEP)
        v_hat = nv / (1.0 - ADAM_B2 ** ADAM_STEP)
        d_ref[...] = -ADAM_LR * (m_hat / (jnp.sqrt(v_hat) + ADAM_EPS) + ADAM_WD * w_ref[...])
        nm_ref[...] = nm
        nv_ref[...] = nv

    spec = pl.BlockSpec((1, tr, tc), lambda l, i, j: (l, i, j))
    return pl.pallas_call(
        body, name=name, grid=(lead, rows // tr, cols // tc), in_specs=[spec] * 4, out_specs=[spec] * 3,
        out_shape=[jax.ShapeDtypeStruct(w.shape, F32)] * 3,
        compiler_params=_params("parallel", "parallel", "parallel"))(w, g, m, v)


MESH = pl.DeviceIdType.MESH
N_DEV = 8
HBM_SPEC = pl.BlockSpec(memory_space=pltpu.HBM)
VMEM_SPEC = pl.BlockSpec(memory_space=pltpu.VMEM)


def _coords():
    return lax.axis_index("x"), lax.axis_index("y"), lax.axis_index("c")


def _flip(v, bit):
    return 1 - v if bit else v


def _remote(src, dst, send_sem, recv_sem, dev):
    return pltpu.make_async_remote_copy(src_ref=src, dst_ref=dst, send_sem=send_sem, recv_sem=recv_sem,
                                        device_id=dev, device_id_type=MESH)


def _exchange_small(buf, reduce, name):
    rows = buf.shape[0]

    def body(in_ref, out_ref, slots, send_sems, recv_sems):
        x, y, c = _coords()
        me = 4 * x + 2 * y + c
        slots[me] = in_ref[...]
        peer = lambda k: (_flip(x, (k >> 2) & 1), _flip(y, (k >> 1) & 1), _flip(c, k & 1))
        sends = []
        for k in range(1, N_DEV):
            cp = _remote(in_ref, slots.at[me], send_sems.at[k - 1], recv_sems.at[k - 1], peer(k))
            cp.start()
            sends.append(cp)
        for k in range(1, N_DEV):
            px, py, pc = peer(k)
            _remote(in_ref, slots.at[4 * px + 2 * py + pc], send_sems.at[k - 1], recv_sems.at[k - 1], peer(k)).wait_recv()
        for cp in sends:
            cp.wait_send()
        if reduce:
            acc = slots[0]
            for d in range(1, N_DEV):
                acc = acc + slots[d]
            out_ref[...] = acc
        else:
            out_ref[...] = slots[...]

    out_shape = (rows, LANES) if reduce else (N_DEV, rows, LANES)
    return pl.pallas_call(
        body, name=name, in_specs=[VMEM_SPEC], out_specs=VMEM_SPEC, out_shape=jax.ShapeDtypeStruct(out_shape, F32),
        scratch_shapes=[pltpu.VMEM((N_DEV, rows, LANES), F32), pltpu.SemaphoreType.DMA((N_DEV - 1,)),
                        pltpu.SemaphoreType.DMA((N_DEV - 1,))])(buf)


def _half_rows(half, which):
    return pl.ds(pl.multiple_of(which * half, 16), half)


def _other_chip(x, y, k):
    return _flip(x, (k >> 1) & 1), _flip(y, k & 1)


SEM_SPEC = pl.BlockSpec(memory_space=pltpu.SEMAPHORE)
DATAFLOW = pltpu.SideEffectType.DATAFLOW_SIDE_EFFECTING


def _ici_piece(srcs, lands, send_sems, recv_sems, i, k, x, y, c):
    half = srcs[i].shape[0] // 2
    ox, oy = _other_chip(x, y, k)
    return _remote(srcs[i].at[_half_rows(half, c)], lands[i].at[2 * x + y, _half_rows(half, c)],
                   send_sems.at[3 * i + k - 1], recv_sems.at[3 * i + k - 1], (ox, oy, c)), (ox, oy)


def _gather_start(shards, name):
    n = len(shards)

    def body(*refs):
        srcs, lands = refs[:n], refs[n:2 * n]
        send_sems, recv_sems = refs[2 * n], refs[2 * n + 1]
        token = refs[-1]
        x, y, c = _coords()
        for i in range(n):
            for k in (1, 2, 3):
                _ici_piece(srcs, lands, send_sems, recv_sems, i, k, x, y, c)[0].start()
        token[...] = jnp.zeros_like(token)

    hbm = lambda a: pltpu.with_memory_space_constraint(a, pltpu.HBM)
    lands = [lax.empty((4,) + s.shape, s.dtype) for s in shards]
    out = pl.pallas_call(
        body, name=name,
        out_shape=(pltpu.SemaphoreType.DMA((3 * n,)), pltpu.SemaphoreType.DMA((3 * n,)),
                   *[pltpu.HBM(s.shape, s.dtype) for s in shards], *[pltpu.HBM(a.shape, a.dtype) for a in lands],
                   jax.ShapeDtypeStruct((SUBLANES, LANES), F32)),
        in_specs=[HBM_SPEC] * (2 * n), out_specs=(SEM_SPEC, SEM_SPEC, *[HBM_SPEC] * (2 * n), VMEM_SPEC),
        input_output_aliases={i: 2 + i for i in range(2 * n)},
        compiler_params=pltpu.CompilerParams(has_side_effects=DATAFLOW))(*[hbm(s) for s in shards], *[hbm(a) for a in lands])
    return out[0], out[1], out[2:2 + n], out[2 + n:2 + 2 * n], out[-1]


def _gather_wait(send_sems, recv_sems, srcs, lands, after, name):
    n = len(srcs)

    def body(*refs):
        src_refs, land_refs = refs[:n], refs[n:2 * n]
        send_sems, recv_sems = refs[2 * n], refs[2 * n + 1]
        x, y, c = _coords()
        for i in range(n):
            half = src_refs[i].shape[0] // 2
            for k in (1, 2, 3):
                cp, (ox, oy) = _ici_piece(src_refs, land_refs, send_sems, recv_sems, i, k, x, y, c)
                cp.wait_send()
                piece = land_refs[i].at[2 * ox + oy, _half_rows(half, c)]
                _remote(piece, piece, send_sems.at[3 * i + k - 1], recv_sems.at[3 * i + k - 1], (ox, oy, c)).wait_recv()

    out = pl.pallas_call(
        body, name=name,
        out_shape=(*[pltpu.HBM(s.shape, s.dtype) for s in srcs], *[pltpu.HBM(a.shape, a.dtype) for a in lands]),
        in_specs=[HBM_SPEC] * (2 * n) + [SEM_SPEC, SEM_SPEC, pl.BlockSpec(memory_space=pl.ANY)],
        out_specs=tuple([HBM_SPEC] * (2 * n)), input_output_aliases={i: i for i in range(2 * n)},
        compiler_params=pltpu.CompilerParams(has_side_effects=DATAFLOW))(*srcs, *lands, send_sems, recv_sems, after)
    return out[n:]


def _gather_forward(lands, name):
    n = len(lands)

    def body(*refs):
        ins, outs = refs[:n], refs[n:2 * n]
        send_sems, recv_sems = refs[2 * n:]
        x, y, c = _coords()
        sends = []
        for i in range(n):
            half = ins[i].shape[1] // 2
            for k in (1, 2, 3):
                ox, oy = _other_chip(x, y, k)
                cp = _remote(ins[i].at[2 * ox + oy, _half_rows(half, c)], outs[i].at[2 * ox + oy, _half_rows(half, c)],
                             send_sems.at[3 * i + k - 1], recv_sems.at[3 * i + k - 1], (x, y, 1 - c))
                cp.start()
                sends.append(cp)
        for i in range(n):
            half = ins[i].shape[1] // 2
            for k in (1, 2, 3):
                ox, oy = _other_chip(x, y, k)
                piece = outs[i].at[2 * ox + oy, _half_rows(half, 1 - c)]
                _remote(piece, piece, send_sems.at[3 * i + k - 1], recv_sems.at[3 * i + k - 1], (x, y, 1 - c)).wait_recv()
        for cp in sends:
            cp.wait_send()

    return pl.pallas_call(
        body, name=name, in_specs=[HBM_SPEC] * n, out_specs=[HBM_SPEC] * n,
        out_shape=[jax.ShapeDtypeStruct(a.shape, a.dtype) for a in lands],
        input_output_aliases={i: i for i in range(n)},
        scratch_shapes=[pltpu.SemaphoreType.DMA((3 * n,))] * 2)(*lands)


def _rs_sibling_swap(g4s, name):
    n = len(g4s)

    def body(*refs):
        ins, outs = refs[:n], refs[n:2 * n]
        send_sems, recv_sems = refs[2 * n:]
        x, y, c = _coords()
        sends = []
        for i in range(n):
            half = ins[i].shape[1] // 2
            for s in range(4):
                cp = _remote(ins[i].at[s, _half_rows(half, 1 - c)], outs[i].at[s], send_sems.at[4 * i + s],
                             recv_sems.at[4 * i + s], (x, y, 1 - c))
                cp.start()
                sends.append(cp)
        for cp in sends:
            cp.wait_recv()
        for cp in sends:
            cp.wait_send()

    return pl.pallas_call(
        body, name=name, in_specs=[HBM_SPEC] * n, out_specs=[HBM_SPEC] * n,
        out_shape=[jax.ShapeDtypeStruct((4, g.shape[1] // 2, g.shape[2]), g.dtype) for g in g4s],
        scratch_shapes=[pltpu.SemaphoreType.DMA((4 * n,))] * 2)(*g4s)


def _rs_add_sibling(g4, got, c_idx, name):
    _, rows, cols = g4.shape
    half = rows // 2
    tr = min(half, 256)
    nb = half // tr

    def body(c_ref, a_ref, b_ref, o_ref, ob_ref):
        total = a_ref[...] + b_ref[...]
        o_ref[...] = total
        ob_ref[...] = total.astype(BF16)

    blk = (1, tr, cols)
    out = pl.BlockSpec(blk, lambda s, i, c_ref: (s, i, 0))
    grid_spec = pltpu.PrefetchScalarGridSpec(
        num_scalar_prefetch=1, grid=(4, nb),
        in_specs=[pl.BlockSpec(blk, lambda s, i, c_ref: (s, c_ref[0] * nb + i, 0)), out],
        out_specs=[out, out])
    return pl.pallas_call(
        body, name=name, grid_spec=grid_spec,
        out_shape=[jax.ShapeDtypeStruct(got.shape, F32), jax.ShapeDtypeStruct(got.shape, BF16)],
        compiler_params=_params("parallel", "parallel"))(c_idx, g4, got)


def _rs_add_chips(p4, got3, idx, name):
    _, half, cols = p4.shape
    tr = min(half, 256)
    nb = half // tr

    def body(idx_ref, p_ref, a_ref, b_ref, c_ref, o_ref):
        o_ref[...] = ((p_ref[0] + a_ref[0].astype(F32)) + b_ref[0].astype(F32)) + c_ref[0].astype(F32)

    blk = (1, tr, cols)
    grid_spec = pltpu.PrefetchScalarGridSpec(
        num_scalar_prefetch=1, grid=(nb,),
        in_specs=[pl.BlockSpec(blk, lambda i, idx_ref: (idx_ref[0], i, 0))]
        + [pl.BlockSpec(blk, functools.partial(lambda k, i, idx_ref: (k, i, 0), k)) for k in range(3)],
        out_specs=pl.BlockSpec((tr, cols), lambda i, idx_ref: (idx_ref[1] * nb + i, 0)))
    return pl.pallas_call(body, name=name, grid_spec=grid_spec, out_shape=jax.ShapeDtypeStruct((2 * half, cols), F32),
                          compiler_params=_params("parallel"))(idx, p4, got3, got3, got3)


def _rs_share_halves(bufs, name):
    n = len(bufs)

    def body(*refs):
        ins, outs = refs[:n], refs[n:2 * n]
        send_sems, recv_sems = refs[2 * n:]
        x, y, c = _coords()
        sends = []
        for i in range(n):
            half = ins[i].shape[0] // 2
            cp = _remote(ins[i].at[_half_rows(half, c)], outs[i].at[_half_rows(half, c)], send_sems.at[i],
                         recv_sems.at[i], (x, y, 1 - c))
            cp.start()
            sends.append(cp)
        for i in range(n):
            half = ins[i].shape[0] // 2
            _remote(ins[i].at[_half_rows(half, c)], outs[i].at[_half_rows(half, 1 - c)], send_sems.at[i],
                    recv_sems.at[i], (x, y, 1 - c)).wait_recv()
        for cp in sends:
            cp.wait_send()

    return pl.pallas_call(
        body, name=name, in_specs=[HBM_SPEC] * n, out_specs=[HBM_SPEC] * n,
        out_shape=[jax.ShapeDtypeStruct(b.shape, b.dtype) for b in bufs],
        input_output_aliases={i: i for i in range(n)},
        scratch_shapes=[pltpu.SemaphoreType.DMA((n,))] * 2)(*bufs)


def _scatter_piece(srcs, lands, send_sems, recv_sems, i, k, x, y, c):
    ox, oy = _other_chip(x, y, k)
    return _remote(srcs[i].at[2 * ox + oy], lands[i].at[k - 1], send_sems.at[3 * i + k - 1],
                   recv_sems.at[3 * i + k - 1], (ox, oy, c))


def _rs_scatter_start(p4s, name):
    n = len(p4s)

    def body(*refs):
        srcs, lands = refs[:n], refs[n:2 * n]
        send_sems, recv_sems = refs[2 * n], refs[2 * n + 1]
        token = refs[-1]
        x, y, c = _coords()
        for i in range(n):
            for k in (1, 2, 3):
                _scatter_piece(srcs, lands, send_sems, recv_sems, i, k, x, y, c).start()
        token[...] = jnp.zeros_like(token)

    hbm = lambda a: pltpu.with_memory_space_constraint(a, pltpu.HBM)
    lands = [lax.empty((3,) + p.shape[1:], p.dtype) for p in p4s]
    out = pl.pallas_call(
        body, name=name,
        out_shape=(pltpu.SemaphoreType.DMA((3 * n,)), pltpu.SemaphoreType.DMA((3 * n,)),
                   *[pltpu.HBM(p.shape, p.dtype) for p in p4s], *[pltpu.HBM(a.shape, a.dtype) for a in lands],
                   jax.ShapeDtypeStruct((SUBLANES, LANES), F32)),
        in_specs=[HBM_SPEC] * (2 * n), out_specs=(SEM_SPEC, SEM_SPEC, *[HBM_SPEC] * (2 * n), VMEM_SPEC),
        input_output_aliases={i: 2 + i for i in range(2 * n)},
        compiler_params=pltpu.CompilerParams(has_side_effects=DATAFLOW))(*[hbm(p) for p in p4s], *[hbm(a) for a in lands])
    return out[0], out[1], out[2:2 + n], out[2 + n:2 + 2 * n], out[-1]


def _rs_scatter_wait(send_sems, recv_sems, srcs, lands, after, name):
    n = len(srcs)

    def body(*refs):
        src_refs, land_refs = refs[:n], refs[n:2 * n]
        send_sems, recv_sems = refs[2 * n], refs[2 * n + 1]
        x, y, c = _coords()
        for i in range(n):
            for k in (1, 2, 3):
                cp = _scatter_piece(src_refs, land_refs, send_sems, recv_sems, i, k, x, y, c)
                cp.wait_send()
                cp.wait_recv()

    out = pl.pallas_call(
        body, name=name,
        out_shape=(*[pltpu.HBM(s.shape, s.dtype) for s in srcs], *[pltpu.HBM(a.shape, a.dtype) for a in lands]),
        in_specs=[HBM_SPEC] * (2 * n) + [SEM_SPEC, SEM_SPEC, pl.BlockSpec(memory_space=pl.ANY)],
        out_specs=tuple([HBM_SPEC] * (2 * n)), input_output_aliases={i: i for i in range(2 * n)},
        compiler_params=pltpu.CompilerParams(has_side_effects=DATAFLOW))(*srcs, *lands, send_sems, recv_sems, after)
    return out[n:]


def _swap_piece(srcs, lands, send_sems, recv_sems, i, s, x, y, c):
    half = srcs[i].shape[1] // 2
    return _remote(srcs[i].at[s, _half_rows(half, 1 - c)], lands[i].at[s], send_sems.at[4 * i + s],
                   recv_sems.at[4 * i + s], (x, y, 1 - c))


def _rs_swap_start(g4s, name):
    n = len(g4s)

    def body(*refs):
        srcs, lands = refs[:n], refs[n:2 * n]
        send_sems, recv_sems = refs[2 * n], refs[2 * n + 1]
        token = refs[-1]
        x, y, c = _coords()
        for i in range(n):
            for s in range(4):
                _swap_piece(srcs, lands, send_sems, recv_sems, i, s, x, y, c).start()
        token[...] = jnp.zeros_like(token)

    hbm = lambda a: pltpu.with_memory_space_constraint(a, pltpu.HBM)
    lands = [lax.empty((4, g.shape[1] // 2, g.shape[2]), g.dtype) for g in g4s]
    out = pl.pallas_call(
        body, name=name,
        out_shape=(pltpu.SemaphoreType.DMA((4 * n,)), pltpu.SemaphoreType.DMA((4 * n,)),
                   *[pltpu.HBM(g.shape, g.dtype) for g in g4s], *[pltpu.HBM(a.shape, a.dtype) for a in lands],
                   jax.ShapeDtypeStruct((SUBLANES, LANES), F32)),
        in_specs=[HBM_SPEC] * (2 * n), out_specs=(SEM_SPEC, SEM_SPEC, *[HBM_SPEC] * (2 * n), VMEM_SPEC),
        input_output_aliases={i: 2 + i for i in range(2 * n)},
        compiler_params=pltpu.CompilerParams(has_side_effects=DATAFLOW))(*[hbm(g) for g in g4s], *[hbm(a) for a in lands])
    return out[0], out[1], out[2:2 + n], out[2 + n:2 + 2 * n], out[-1]


def _rs_swap_wait(send_sems, recv_sems, srcs, lands, after, name):
    n = len(srcs)

    def body(*refs):
        src_refs, land_refs = refs[:n], refs[n:2 * n]
        send_sems, recv_sems = refs[2 * n], refs[2 * n + 1]
        x, y, c = _coords()
        for i in range(n):
            for s in range(4):
                cp = _swap_piece(src_refs, land_refs, send_sems, recv_sems, i, s, x, y, c)
                cp.wait_send()
                cp.wait_recv()

    out = pl.pallas_call(
        body, name=name,
        out_shape=(*[pltpu.HBM(s.shape, s.dtype) for s in srcs], *[pltpu.HBM(a.shape, a.dtype) for a in lands]),
        in_specs=[HBM_SPEC] * (2 * n) + [SEM_SPEC, SEM_SPEC, pl.BlockSpec(memory_space=pl.ANY)],
        out_specs=tuple([HBM_SPEC] * (2 * n)), input_output_aliases={i: i for i in range(2 * n)},
        compiler_params=pltpu.CompilerParams(has_side_effects=DATAFLOW))(*srcs, *lands, send_sems, recv_sems, after)
    return out[:n], out[n:]


def _rs_front(g4s, names, tag):
    c_idx = jnp.stack([lax.axis_index("c")]).astype(jnp.int32)
    got = _rs_sibling_swap(g4s, f"rs_sibling_swap_{tag}")
    return [_rs_add_sibling(g, s, c_idx, f"rs_add_sibling_{nm}") for g, s, nm in zip(g4s, got, names)]


def _rs_back(p4s, got3, names):
    x, y, c = _coords()
    idx = jnp.stack([2 * x + y, c]).astype(jnp.int32)
    return [_rs_add_chips(p, t, idx, f"rs_add_chips_{nm}") for (p, _), t, nm in zip(p4s, got3, names)]


def _cols_split(full):
    r, c4 = full.shape
    return full.reshape(r, 4, c4 // 4).transpose(1, 0, 2)


_BIG = {
    "w_in_even": ((1024, 2052), lambda s: s),
    "w_out_even": ((512, 1024), lambda s: s.reshape(2048, 1024)),
    "w_in_odd": ((1024, 2048), lambda s: s),
    "w_out_odd": ((512, 1024), lambda s: s.reshape(2048, 1024)),
    "w_pl": ((512, 256), lambda s: s.reshape(4, 2, 256, 256).transpose(1, 2, 0, 3).reshape(2, 256, 1024)),
    "w_pl_gate": ((512, 1024), lambda s: s.reshape(4, 2, 256, 1024).transpose(1, 0, 2, 3).reshape(2, 1024, 1024)),
}


_RS_EARLY = {
    "w_in_odd": lambda f: f,
    "w_out_odd": lambda f: f.reshape(4, 512, 1024),
    "w_pl_gate1": lambda f: f.reshape(4, 256, 1024),
    "w_pl1": _cols_split,
    "w_out_even": lambda f: f.reshape(4, 512, 1024),
    "w_pl_gate0": lambda f: f.reshape(4, 256, 1024),
    "w_pl0": _cols_split,
}
_RS_LATE = {"w_in_even": lambda f: f}


def _size(shape):
    n = 1
    for d in shape:
        n *= d
    return n


_SMALL = {"a_log": (1, 8), "dt_bias": (1, 8), "gdn_norm_g": (1, 128), "hgrn_norm_g": (1, 128),
          "lower_bounds": (2, 2048), "ln_g": (2, 1024), "ln_b": (2, 1024), "conv_a_w": (3, 1024), "conv_b_w": (4, 3072)}
_CONV_SHARD = {"conv_a_w": (3, 256), "conv_b_w": (4, 768)}


def _pack_small(parts, shapes, head_rows=0):
    rows = []
    for n, shape in shapes.items():
        v = parts[n].reshape(-1)
        rows.append(jnp.pad(v, (0, -v.shape[0] % LANES)).reshape(-1, LANES))
    buf = jnp.concatenate(rows, axis=0)
    return jnp.pad(buf, ((head_rows, -(buf.shape[0] + head_rows) % SUBLANES), (0, 0)))


def _unpack_small(buf, shapes, head_rows=0):
    out, off = {}, head_rows
    for n, shape in shapes.items():
        nrow = -(-_size(shape) // LANES)
        out[n] = buf[off:off + nrow].reshape(-1)[:_size(shape)].reshape(shape)
        off += nrow
    return out


_WEIGHTS = ["w_in_even", "conv_a_w", "conv_b_w", "a_log", "dt_bias", "gdn_norm_g", "w_out_even", "w_in_odd",
            "lower_bounds", "hgrn_norm_g", "w_out_odd", "ln_g", "ln_b", "w_pl", "w_pl_gate"]


def kernel(x, p, w_in_even, conv_a_w, conv_b_w, a_log, dt_bias, gdn_norm_g, w_out_even, w_in_odd, lower_bounds, hgrn_norm_g, w_out_odd, ln_g, ln_b, w_pl, w_pl_gate, loss_target, m_w_in_even, m_conv_a_w, m_conv_b_w, m_a_log, m_dt_bias, m_gdn_norm_g, m_w_out_even, m_w_in_odd, m_lower_bounds, m_hgrn_norm_g, m_w_out_odd, m_ln_g, m_ln_b, m_w_pl, m_w_pl_gate, v_w_in_even, v_conv_a_w, v_conv_b_w, v_a_log, v_dt_bias, v_gdn_norm_g, v_w_out_even, v_w_in_odd, v_lower_bounds, v_hgrn_norm_g, v_w_out_odd, v_ln_g, v_ln_b, v_w_pl, v_w_pl_gate):
    w = dict(zip(_WEIGHTS, (w_in_even, conv_a_w, conv_b_w, a_log, dt_bias, gdn_norm_g, w_out_even, w_in_odd,
                            lower_bounds, hgrn_norm_g, w_out_odd, ln_g, ln_b, w_pl, w_pl_gate)))
    m = dict(zip(_WEIGHTS, (m_w_in_even, m_conv_a_w, m_conv_b_w, m_a_log, m_dt_bias, m_gdn_norm_g, m_w_out_even,
                            m_w_in_odd, m_lower_bounds, m_hgrn_norm_g, m_w_out_odd, m_ln_g, m_ln_b, m_w_pl, m_w_pl_gate)))
    v = dict(zip(_WEIGHTS, (v_w_in_even, v_conv_a_w, v_conv_b_w, v_a_log, v_dt_bias, v_gdn_norm_g, v_w_out_even,
                            v_w_in_odd, v_lower_bounds, v_hgrn_norm_g, v_w_out_odd, v_ln_g, v_ln_b, v_w_pl, v_w_pl_gate)))
    chip = 2 * lax.axis_index("x") + lax.axis_index("y")

    names = list(_BIG)
    shard_shapes = {n: _BIG[n][0] for n in names}
    early, late = names[:1], names[1:]
    shards = {n: w[n].reshape(shard_shapes[n]).astype(BF16) for n in early}
    whole = lambda n, stacked: _BIG[n][1](lax.dynamic_update_slice(stacked, shards[n][None], (chip, 0, 0)))
    conv_mine = _pack_small({n: w[n] for n in _CONV_SHARD}, _CONV_SHARD)
    conv_all = _exchange_small(conv_mine, False, "gather_conv")
    shards, conv_all = lax.optimization_barrier((shards, conv_all))
    first = _gather_start([shards[n] for n in early], "gather_first_start")
    shards.update({n: (w[n].reshape(shard_shapes[n]) + first[4][0, 0]).astype(BF16) for n in late})
    send_sems, recv_sems, srcs, lands, token = _gather_start([shards[n] for n in late], "gather_rest_start")

    def late_weights(after):
        landed = _gather_forward(_gather_wait(send_sems, recv_sems, srcs, lands, after, "gather_rest_wait"),
                                 "gather_rest_forward")
        return {n: whole(n, ga) for n, ga in zip(late, landed)}

    landed = _gather_forward(_gather_wait(*first[:4], token, "gather_first_wait"), "gather_first_forward")
    full = {n: whole(n, ga) for n, ga in zip(early, landed)}
    conv_by_chip = [_unpack_small(conv_all[2 * s], _CONV_SHARD) for s in range(4)]
    for n in _CONV_SHARD:
        full[n] = jnp.concatenate([conv_by_chip[s][n] for s in range(4)], axis=1)
    for n in _SMALL:
        if n not in _CONV_SHARD:
            full[n] = w[n]

    early_rs = {}

    def early_grads_ready(grads):
        early_rs["swap"] = _rs_swap_start([_RS_EARLY[n](grads[n]) for n in _RS_EARLY], "rs_swap_early_start")
        return early_rs["swap"][4][0, 0]

    def early_grads_swapped(after):
        g4s, got = _rs_swap_wait(*early_rs["swap"][:4], after, "rs_swap_early_wait")
        c_idx = jnp.stack([lax.axis_index("c")]).astype(jnp.int32)
        early_rs["p4s"] = [_rs_add_sibling(g_, s_, c_idx, f"rs_add_sibling_{nm}") for g_, s_, nm in zip(g4s, got, _RS_EARLY)]
        early_rs["sems"] = _rs_scatter_start([pb for _, pb in early_rs["p4s"]], "rs_scatter_early_start")
        return early_rs["sems"][4][0, 0]

    late_rs = {}

    def last_grad_ready(grad):
        late_rs["p4s"] = _rs_front([_RS_LATE[n](grad) for n in _RS_LATE], list(_RS_LATE), "late")
        late_rs["sems"] = _rs_scatter_start([pb for _, pb in late_rs["p4s"]], "rs_scatter_late_start")
        return late_rs["sems"][4][0, 0]

    loss_part, dx, g = _local_step(x[0], p[:, 0], loss_target[0], full, late_weights, early_grads_ready,
                                   early_grads_swapped, last_grad_ready, token[0, 0])

    late_p4s, late_sems = late_rs["p4s"], late_rs["sems"]
    got3 = _rs_scatter_wait(*early_rs["sems"][:4], dx, "rs_scatter_early_wait")
    summed = dict(zip(_RS_EARLY, _rs_share_halves(_rs_back(early_rs["p4s"], got3, list(_RS_EARLY)), "rs_share_early")))
    g_big = {n: summed[n] for n in names if n in summed}
    g_big["w_pl"] = jnp.stack([summed["w_pl0"], summed["w_pl1"]])
    g_big["w_pl_gate"] = jnp.stack([summed["w_pl_gate0"], summed["w_pl_gate1"]])
    small_sum = _exchange_small(jnp.concatenate([loss_part, _pack_small(g, _SMALL)], axis=0), True, "reduce_small")
    loss = small_sum[0, 0]
    g_small = _unpack_small(small_sum, _SMALL, head_rows=SUBLANES)
    for n, (rows, cols) in _CONV_SHARD.items():
        g_small[n] = lax.dynamic_slice_in_dim(g_small[n], chip * cols, cols, axis=1)

    grads, delta, new_m, new_v = {}, {}, {}, {}
    for n in late:
        grads[n] = g_big[n].reshape(w[n].shape)
        delta[n], new_m[n], new_v[n] = _adamw(w[n], grads[n], m[n], v[n], f"adamw_{n}")
    own = {n: (_CONV_SHARD[n] if n in _CONV_SHARD else _SMALL[n]) for n in _SMALL}
    packs = [_pack_small({n: src[n] for n in _SMALL}, own)[None] for src in (w, g_small, m, v)]
    outs = [_unpack_small(t[0], own) for t in _adamw(*packs, "adamw_small")]
    for n in _SMALL:
        grads[n] = g_small[n].reshape(w[n].shape)
        delta[n], new_m[n], new_v[n] = (t[n].reshape(w[n].shape) for t in outs)
    got3 = _rs_scatter_wait(*late_sems[:4], new_v["w_in_odd"], "rs_scatter_late_wait")
    (g_in_even,) = _rs_share_halves(_rs_back(late_p4s, got3, list(_RS_LATE)), "rs_share_late")
    for n in early:
        t_ = lambda a: jnp.swapaxes(a, 1, 2)
        g_t = t_(g_in_even.reshape(w[n].shape))
        grads[n] = t_(g_t)
        delta[n], new_m[n], new_v[n] = (t_(o) for o in _adamw(t_(w[n]), g_t, t_(m[n]), t_(v[n]), f"adamw_{n}"))
    return (loss, dx[None], *[grads[n] for n in _WEIGHTS], *[delta[n] for n in _WEIGHTS],
            *[new_m[n] for n in _WEIGHTS], *[new_v[n] for n in _WEIGHTS])
```

```python
import functools

import jax
import jax.numpy as jnp
from jax import lax
from jax.experimental import pallas as pl
from jax.experimental.pallas import tpu as pltpu

F32 = jnp.float32
BF16 = jnp.bfloat16

D_MODEL = 1024
PL_DIM = 256
GDN_HEADS = 8
HEAD_DIM = 128
GDN_CHUNK = 64
HGRN_HEADS = 16
HGRN_CHUNK = 32
HGRN_WIDTH = 2048
DEEPNORM_ALPHA = 4.0 ** 0.25
NORM_EPS = 1e-5
ADAM_LR, ADAM_B1, ADAM_B2, ADAM_EPS, ADAM_WD, ADAM_STEP = 0.001, 0.9, 0.999, 1e-08, 0.01, 10

VMEM_LIMIT = 60 * 1024 * 1024
SUBLANES = 8
LANES = 128


def _params(*sem):
    return pltpu.CompilerParams(dimension_semantics=sem, vmem_limit_bytes=VMEM_LIMIT)


ONE_PASS, THREE_PASS, EXACT_LHS, EXACT_RHS = 0, 1, 2, 3


def _split3(v):
    hi = v.astype(BF16)
    r1 = v - hi.astype(F32)
    mid = r1.astype(BF16)
    return hi, mid, (r1 - mid.astype(F32)).astype(BF16)


def _mm_raw(a, b, kind, prec):
    nb = a.ndim - 2
    ca = a.ndim - 1 if kind[0] == "n" else a.ndim - 2
    cb = b.ndim - 2 if kind[1] == "n" else b.ndim - 1
    dims = (((ca,), (cb,)), (tuple(range(nb)),) * 2)
    dot = lambda p, q: lax.dot_general(p, q, dims, preferred_element_type=F32)
    ah, bh = a.astype(BF16), b.astype(BF16)
    if prec == ONE_PASS:
        return dot(ah, bh)
    if prec == EXACT_LHS:
        b1, b2, b3 = _split3(b)
        return dot(ah, b1) + (dot(ah, b2) + dot(ah, b3))
    if prec == EXACT_RHS:
        a1, a2, a3 = _split3(a)
        return dot(a1, bh) + (dot(a2, bh) + dot(a3, bh))
    al = (a - ah.astype(F32)).astype(BF16)
    bl = (b - bh.astype(F32)).astype(BF16)
    return dot(ah, bh) + (dot(ah, bl) + dot(al, bh))


@functools.partial(jax.custom_vjp, nondiff_argnums=(2, 3))
def _mm_vjp(a, b, kind, hi):
    return _mm_raw(a, b, kind, hi)


def _mm_vjp_fwd(a, b, kind, hi):
    return _mm_raw(a, b, kind, hi), (a, b)


def _mm_vjp_bwd(kind, hi, res, dc):
    a, b = res
    if hi in (EXACT_LHS, EXACT_RHS):
        assert kind == "nn"
        if hi == EXACT_LHS:
            return jnp.zeros_like(a), _mm_raw(a, dc, "tn", EXACT_LHS)
        return _mm_raw(dc, b, "nt", EXACT_RHS), jnp.zeros_like(b)
    if kind == "nn":
        return _mm_raw(dc, b, "nt", hi), _mm_raw(a, dc, "tn", hi)
    if kind == "nt":
        return _mm_raw(dc, b, "nn", hi), _mm_raw(dc, a, "tn", hi)
    return _mm_raw(b, dc, "nt", hi), _mm_raw(a, dc, "nn", hi)


_mm_vjp.defvjp(_mm_vjp_fwd, _mm_vjp_bwd)


def _lane_total(v):
    return jnp.broadcast_to(jnp.sum(v, axis=-1, keepdims=True), v.shape)


def _matmul(a, b, *, name, ta=False, tb=False, add=None, add_scale=1.0, tm=1024, tn=2048, tk=1024):
    m, k = (a.shape[1], a.shape[0]) if ta else a.shape
    n = b.shape[0] if tb else b.shape[1]
    tm, tn, tk = min(tm, m), min(tn, n), min(tk, k)
    tn = tn if n % tn == 0 else tn // 2
    assert m % tm == 0 and n % tn == 0 and k % tk == 0, (name, m, n, k)
    nk = k // tk
    dims = (((0 if ta else 1,), (1 if tb else 0,)), ((), ()))

    def body(*refs):
        a_ref, b_ref = refs[:2]
        o_ref = refs[-1]
        part = lax.dot_general(a_ref[...].astype(BF16), b_ref[...].astype(BF16), dims, preferred_element_type=F32)
        first = (lambda: part) if add is None else (lambda: part + add_scale * refs[2][...])
        if nk == 1:
            o_ref[...] = first()
        else:
            kk = pl.program_id(2)

            @pl.when(kk == 0)
            def _():
                o_ref[...] = first()

            @pl.when(kk > 0)
            def _():
                o_ref[...] += part

    a_spec = pl.BlockSpec((tk, tm), lambda i, j, kk: (kk, i)) if ta else pl.BlockSpec((tm, tk), lambda i, j, kk: (i, kk))
    b_spec = pl.BlockSpec((tn, tk), lambda i, j, kk: (j, kk)) if tb else pl.BlockSpec((tk, tn), lambda i, j, kk: (kk, j))
    o_spec = pl.BlockSpec((tm, tn), lambda i, j, kk: (i, j))
    in_specs = [a_spec, b_spec] + ([o_spec] if add is not None else [])
    args = (a, b) + ((add,) if add is not None else ())
    return pl.pallas_call(
        body, name=name, grid=(m // tm, n // tn, nk), in_specs=in_specs, out_specs=o_spec,
        out_shape=jax.ShapeDtypeStruct((m, n), F32),
        compiler_params=_params("parallel", "parallel", "arbitrary"))(*args)


HALO = SUBLANES


def _halo_specs(tt, width, col, nt):
    r = tt // HALO
    prev = pl.BlockSpec((HALO, width), lambda i: (jnp.maximum(i * r - 1, 0), col))
    nxt = pl.BlockSpec((HALO, width), lambda i: (jnp.minimum((i + 1) * r, nt * r - 1), col))
    return prev, nxt


def _shift_down(ext, k):
    return ext if k == 0 else pltpu.roll(ext, k, 0)


def _shift_up(ext, k):
    return ext if k == 0 else pltpu.roll(ext, ext.shape[0] - k, 0)


def _causal_conv(ext, w, taps):
    acc = None
    for j in range(taps):
        term = w[j:j + 1, :] * _shift_down(ext, taps - 1 - j)
        acc = term if acc is None else acc + term
    return acc[HALO:, :]


def _conv_a_fwd(proj_a, conv_w):
    t = proj_a.shape[0]
    tt = min(t, 256)
    nt = t // tt
    wdt = 1024

    def body(cur_ref, prev_ref, w_ref, y_ref):
        i = pl.program_id(0)
        cur = cur_ref[...]
        h, c, b, z = (cur[:, k * wdt:(k + 1) * wdt] for k in range(4))
        prev = prev_ref[...]
        u_prev = jnp.where(i > 0, prev[:, wdt:2 * wdt] * prev[:, 0:wdt], 0.0)
        ext = jnp.concatenate([u_prev, c * h], axis=0)
        conv = _causal_conv(ext, w_ref[...], 3)
        y_ref[...] = (b * conv * jax.nn.silu(z)).astype(BF16)

    prev_spec, _ = _halo_specs(tt, 4 * wdt, 0, nt)
    return pl.pallas_call(
        body, name="conv_a_fwd", grid=(nt,),
        in_specs=[pl.BlockSpec((tt, 4 * wdt), lambda i: (i, 0)), prev_spec, pl.BlockSpec((3, wdt), lambda i: (0, 0))],
        out_specs=pl.BlockSpec((tt, wdt), lambda i: (i, 0)),
        out_shape=jax.ShapeDtypeStruct((t, wdt), BF16), compiler_params=_params("parallel"))(proj_a, proj_a, conv_w)


def _conv_a_bwd(proj_a, conv_w, dy):
    t = proj_a.shape[0]
    tt = min(t, 256)
    nt = t // tt
    wdt = 1024

    def body(cur_ref, prev_ref, nxt_ref, w_ref, dy_ref, dyn_ref, d_ref, dw_ref):
        i = pl.program_id(0)
        w = w_ref[...]
        cur, prev, nxt = cur_ref[...], prev_ref[...], nxt_ref[...]
        split = lambda a: tuple(a[:, k * wdt:(k + 1) * wdt] for k in range(4))
        h, c, b, z = split(cur)
        hp, cp, _, _ = split(prev)
        hn, cn, bn, zn = split(nxt)
        u_prev = jnp.where(i > 0, cp * hp, 0.0)
        u_ext = jnp.concatenate([u_prev, c * h, cn * hn], axis=0)
        taps = [_shift_down(u_ext, 2 - j)[HALO:, :] for j in range(3)]
        conv = w[0:1, :] * taps[0] + w[1:2, :] * taps[1] + w[2:3, :] * taps[2]
        b_cn = jnp.concatenate([b, bn], axis=0)
        z_cn = jnp.concatenate([z, zn], axis=0)
        dy_cn = jnp.concatenate([dy_ref[...], jnp.where(i < nt - 1, dyn_ref[...], 0.0)], axis=0)
        sg = jax.nn.sigmoid(z_cn)
        silu = z_cn * sg
        d_conv = dy_cn * b_cn * silu
        db = (dy_cn * conv * silu)[:tt, :]
        dz = (dy_cn * b_cn * conv * (sg * (1.0 + z_cn * (1.0 - sg))))[:tt, :]
        du = None
        for j in range(3):
            term = w[j:j + 1, :] * _shift_up(d_conv, 2 - j)
            du = term if du is None else du + term
        du = du[:tt, :]
        d_ref[...] = jnp.concatenate([du * c, du * h, db, dz], axis=1).astype(BF16)

        @pl.when(i == 0)
        def _():
            dw_ref[...] = jnp.zeros_like(dw_ref)

        d_cur = d_conv[:tt, :]
        rows = [jnp.sum(d_cur * taps[j][:tt, :], axis=0, keepdims=True) for j in range(3)]
        dw_ref[0:3, :] += jnp.concatenate(rows, axis=0)

    prev_spec, nxt_spec = _halo_specs(tt, 4 * wdt, 0, nt)
    _, dyn_spec = _halo_specs(tt, wdt, 0, nt)
    return pl.pallas_call(
        body, name="conv_a_bwd", grid=(nt,),
        in_specs=[pl.BlockSpec((tt, 4 * wdt), lambda i: (i, 0)), prev_spec, nxt_spec,
                  pl.BlockSpec((3, wdt), lambda i: (0, 0)), pl.BlockSpec((tt, wdt), lambda i: (i, 0)), dyn_spec],
        out_specs=[pl.BlockSpec((tt, 4 * wdt), lambda i: (i, 0)), pl.BlockSpec((SUBLANES, wdt), lambda i: (0, 0))],
        out_shape=[jax.ShapeDtypeStruct((t, 4 * wdt), BF16), jax.ShapeDtypeStruct((SUBLANES, wdt), F32)],
        compiler_params=_params("arbitrary"))(proj_a, proj_a, proj_a, conv_w, dy, dy)


def _conv_b_fwd(proj_qkv, conv_w):
    t, width = proj_qkv.shape
    tt = min(t, 256)
    nt = t // tt
    wdt = 1024

    def body(cur_ref, prev_ref, w_ref, y_ref):
        i = pl.program_id(1)
        ext = jnp.concatenate([jnp.where(i > 0, prev_ref[...], 0.0), cur_ref[...]], axis=0)
        y_ref[...] = jax.nn.silu(_causal_conv(ext, w_ref[...], 4))

    r = tt // HALO
    return pl.pallas_call(
        body, name="conv_b_fwd", grid=(width // wdt, nt),
        in_specs=[pl.BlockSpec((tt, wdt), lambda j, i: (i, j)),
                  pl.BlockSpec((HALO, wdt), lambda j, i: (jnp.maximum(i * r - 1, 0), j)),
                  pl.BlockSpec((4, wdt), lambda j, i: (0, j))],
        out_specs=pl.BlockSpec((tt, wdt), lambda j, i: (i, j)),
        out_shape=jax.ShapeDtypeStruct((t, width), F32), compiler_params=_params("parallel", "parallel"))(
            proj_qkv, proj_qkv, conv_w)


def _conv_b_bwd(proj_qkv, conv_w, d_act, col, name):
    t = proj_qkv.shape[0]
    tt = min(t, 256)
    nt = t // tt
    wdt = 1024

    def body(cur_ref, prev_ref, nxt_ref, w_ref, da_ref, dan_ref, d_ref, dw_ref):
        i = pl.program_id(0)
        w = w_ref[...]
        u_ext = jnp.concatenate([jnp.where(i > 0, prev_ref[...], 0.0), cur_ref[...], nxt_ref[...]], axis=0)
        taps = [_shift_down(u_ext, 3 - j)[HALO:, :] for j in range(4)]
        conv = w[0:1, :] * taps[0] + w[1:2, :] * taps[1] + w[2:3, :] * taps[2] + w[3:4, :] * taps[3]
        da_cn = jnp.concatenate([da_ref[...], jnp.where(i < nt - 1, dan_ref[...], 0.0)], axis=0)
        sg = jax.nn.sigmoid(conv)
        d_conv = da_cn * (sg * (1.0 + conv * (1.0 - sg)))
        du = None
        for j in range(4):
            term = w[j:j + 1, :] * _shift_up(d_conv, 3 - j)
            du = term if du is None else du + term
        d_ref[...] = du[:tt, :].astype(BF16)

        @pl.when(i == 0)
        def _():
            dw_ref[...] = jnp.zeros_like(dw_ref)

        d_cur = d_conv[:tt, :]
        rows = [jnp.sum(d_cur * taps[j][:tt, :], axis=0, keepdims=True) for j in range(4)]
        dw_ref[0:4, :] += jnp.concatenate(rows, axis=0)

    prev_spec, nxt_spec = _halo_specs(tt, wdt, col, nt)
    _, dan_spec = _halo_specs(tt, wdt, 0, nt)
    return pl.pallas_call(
        body, name=name, grid=(nt,),
        in_specs=[pl.BlockSpec((tt, wdt), lambda i: (i, col)), prev_spec, nxt_spec,
                  pl.BlockSpec((4, wdt), lambda i: (0, col)), pl.BlockSpec((tt, wdt), lambda i: (i, 0)), dan_spec],
        out_specs=[pl.BlockSpec((tt, wdt), lambda i: (i, 0)), pl.BlockSpec((SUBLANES, wdt), lambda i: (0, 0))],
        out_shape=[jax.ShapeDtypeStruct((t, wdt), BF16), jax.ShapeDtypeStruct((SUBLANES, wdt), F32)],
        compiler_params=_params("arbitrary"))(proj_qkv, proj_qkv, proj_qkv, conv_w, d_act, d_act)


def _rms_gate(o, gn, z):
    on = o * lax.rsqrt(jnp.mean(o * o, axis=-1, keepdims=True) + NORM_EPS) * gn
    return on * jax.nn.silu(z)


GDN_PREP_ROWS = 1024


def _unit_lower_inverse(low):
    c = low.shape[-1]
    eye = lax.broadcasted_iota(jnp.int32, low.shape, low.ndim - 2) == lax.broadcasted_iota(jnp.int32, low.shape, low.ndim - 1)
    x = -low
    inv = eye.astype(F32) + x
    for _ in range(c.bit_length() - 2):
        x = _mm_raw(x, x, "nn", THREE_PASS)
        inv = inv + _mm_raw(inv, x, "nn", THREE_PASS)
    return inv


@jax.custom_vjp
def _known_inverse(low, inv):
    return inv


def _known_inverse_fwd(low, inv):
    return inv, inv


def _known_inverse_bwd(inv, d_inv):
    return -_mm_raw(_mm_raw(inv, d_inv, "tn", THREE_PASS), inv, "nt", THREE_PASS), jnp.zeros_like(inv)


_known_inverse.defvjp(_known_inverse_fwd, _known_inverse_bwd)


def _gdn_prep(mm, qa, ka, va, braw, araw, alog, dtb, inv_kept=None):
    n, c, _ = qa.shape
    q = qa * lax.rsqrt(jnp.sum(qa * qa, axis=-1, keepdims=True) + 1e-6) * (HEAD_DIM ** -0.5)
    k = ka * lax.rsqrt(jnp.sum(ka * ka, axis=-1, keepdims=True) + 1e-6)
    beta = jax.nn.sigmoid(braw)
    g = -jnp.exp(alog) * jax.nn.softplus(araw + dtb)
    ri = lax.broadcasted_iota(jnp.int32, (n, c, c), 1)
    ci = lax.broadcasted_iota(jnp.int32, (n, c, c), 2)
    incl, strict, eye = ri >= ci, ri > ci, ri == ci
    gc = mm(incl.astype(F32), g, "nn", EXACT_LHS)
    gc_i = gc[:, :, :c]
    gc_j = mm(jnp.ones((n, c, c), F32), jnp.where(eye, gc_i, 0.0), "nn", EXACT_LHS)
    decay = jnp.where(incl, jnp.exp(jnp.where(incl, gc_i - gc_j, 0.0)), 0.0)
    kb = k * beta
    low = jnp.where(strict, mm(kb, k, "nt", ONE_PASS) * decay, 0.0)
    inv = _unit_lower_inverse(low) if inv_kept is None else _known_inverse(low, inv_kept)
    egc = jnp.exp(gc)
    u = mm(inv, va * beta, "nn", THREE_PASS)
    w = mm(inv, kb * egc, "nn", THREE_PASS)
    attn = jnp.where(incl, mm(q, k, "nt", ONE_PASS) * decay, 0.0)
    g_last = jnp.sum(g, axis=1, keepdims=True)
    outs = (u, w, q * egc, k * jnp.exp(g_last - gc), attn, jnp.exp(g_last))
    return outs + (inv,) if inv_kept is None else outs


def _gdn_scan(mm, u, w, qd, kd, attn, egl, z, gn, state):
    v_new = u - mm(w, state, "nn", ONE_PASS)
    o = mm(qd, state, "nn", ONE_PASS) + mm(attn, v_new, "nn", ONE_PASS)
    new_state = state * egl + mm(kd, v_new, "tn", ONE_PASS)
    return _rms_gate(o, gn, z), new_state


def _chunks(ref_value, n, c):
    return ref_value.reshape(n, c, ref_value.shape[-1])


def _by_head(ref, rows, heads):
    return jnp.stack([ref[rows, pl.ds(h * HEAD_DIM, HEAD_DIM)] for h in range(heads)])


def _store_heads(ref, rows, value):
    for h in range(value.shape[0]):
        ref[rows, pl.ds(h * HEAD_DIM, HEAD_DIM)] = value[h]


def _gdn_prep_specs(tb):
    col = lambda off: pl.BlockSpec((tb, HEAD_DIM), lambda h, i: (i, off + h))
    rep = pl.BlockSpec((1, tb, LANES), lambda h, i: (h, i, 0))
    par = pl.BlockSpec((1, SUBLANES, LANES), lambda h, i: (h, 0, 0))
    att = pl.BlockSpec((1, tb, GDN_CHUNK), lambda h, i: (h, i, 0))
    egl = pl.BlockSpec((1, tb // GDN_CHUNK, SUBLANES, LANES), lambda h, i: (h, i, 0, 0))
    return col, rep, par, att, egl


def _gdn_prep_fwd(qkv_act, braw, araw, alog, dtb):
    t = qkv_act.shape[0]
    tb = min(t, 2 * GDN_PREP_ROWS)
    nt, nc = t // tb, tb // GDN_CHUNK
    width = GDN_HEADS * HEAD_DIM

    def body(q_ref, k_ref, v_ref, br_ref, ar_ref, al_ref, dt_ref, u_ref, w_ref, qd_ref, kd_ref, at_ref, eg_ref, inv_ref):
        ch = lambda r: _chunks(r, nc, GDN_CHUNK)
        u, w, qd, kd, attn, egl, inv = _gdn_prep(_mm_raw, ch(q_ref[...]), ch(k_ref[...]), ch(v_ref[...]), ch(br_ref[0]),
                                                 ch(ar_ref[0]), al_ref[0, 0:1, :], dt_ref[0, 0:1, :])
        u_ref[...] = u.reshape(tb, HEAD_DIM)
        w_ref[...] = w.reshape(tb, HEAD_DIM).astype(BF16)
        qd_ref[...] = qd.reshape(tb, HEAD_DIM).astype(BF16)
        kd_ref[...] = kd.reshape(tb, HEAD_DIM).astype(BF16)
        at_ref[0] = attn.reshape(tb, GDN_CHUNK).astype(BF16)
        eg_ref[0] = jnp.broadcast_to(egl, (nc, SUBLANES, LANES))
        inv_ref[0] = inv.reshape(tb, GDN_CHUNK)

    col, rep, par, att, egl = _gdn_prep_specs(tb)
    h = GDN_HEADS
    return pl.pallas_call(
        body, name="gdn_prep_fwd", grid=(h, nt),
        in_specs=[col(0), col(h), col(2 * h), rep, rep, par, par],
        out_specs=[col(0), col(0), col(0), col(0), att, egl, att],
        out_shape=[jax.ShapeDtypeStruct((t, width), F32)] + [jax.ShapeDtypeStruct((t, width), BF16)] * 3
        + [jax.ShapeDtypeStruct((h, t, GDN_CHUNK), BF16), jax.ShapeDtypeStruct((h, t // GDN_CHUNK, SUBLANES, LANES), F32),
           jax.ShapeDtypeStruct((h, t, GDN_CHUNK), F32)],
        compiler_params=_params("parallel", "parallel"))(qkv_act, qkv_act, qkv_act, braw, araw, alog, dtb)


def _gdn_prep_bwd(qkv_act, braw, araw, alog, dtb, inv, du, dw, dqd, dkd, dattn, degl):
    t = qkv_act.shape[0]
    tb = min(t, 2 * GDN_PREP_ROWS)
    nt, nc = t // tb, tb // GDN_CHUNK
    width = GDN_HEADS * HEAD_DIM

    def body(q_ref, k_ref, v_ref, br_ref, ar_ref, al_ref, dt_ref, inv_ref, du_ref, dw_ref, dqd_ref, dkd_ref, dat_ref,
             deg_ref, dq_ref, dk_ref, dv_ref, dbr_ref, dar_ref, dal_ref, ddt_ref):
        @pl.when(pl.program_id(1) == 0)
        def _():
            dal_ref[...] = jnp.zeros_like(dal_ref)
            ddt_ref[...] = jnp.zeros_like(ddt_ref)

        ch = lambda r: _chunks(r, nc, GDN_CHUNK)
        _, vjp = jax.vjp(functools.partial(_gdn_prep, _mm_vjp, inv_kept=ch(inv_ref[0])), ch(q_ref[...]), ch(k_ref[...]),
                         ch(v_ref[...]), ch(br_ref[0]), ch(ar_ref[0]), al_ref[0, 0:1, :], dt_ref[0, 0:1, :])
        dq, dk, dv, dbr, dar, dal, ddt = vjp((ch(du_ref[...]), ch(dw_ref[...]), ch(dqd_ref[...]), ch(dkd_ref[...]),
                                              ch(dat_ref[0]), deg_ref[0][:, 0:1, :]))
        dq_ref[...] = dq.reshape(tb, HEAD_DIM)
        dk_ref[...] = dk.reshape(tb, HEAD_DIM)
        dv_ref[...] = dv.reshape(tb, HEAD_DIM)
        dbr_ref[0] = _lane_total(dbr.reshape(tb, LANES))
        dar_ref[0] = _lane_total(dar.reshape(tb, LANES))
        dal_ref[0, 0:1, :] += _lane_total(dal)
        ddt_ref[0, 0:1, :] += _lane_total(ddt)

    col, rep, par, att, egl = _gdn_prep_specs(tb)
    h = GDN_HEADS
    return pl.pallas_call(
        body, name="gdn_prep_bwd", grid=(h, nt),
        in_specs=[col(0), col(h), col(2 * h), rep, rep, par, par, att, col(0), col(0), col(0), col(0), att, egl],
        out_specs=[col(0), col(0), col(0), rep, rep, par, par],
        out_shape=[jax.ShapeDtypeStruct((t, width), F32)] * 3 + [jax.ShapeDtypeStruct((h, t, LANES), F32)] * 2
        + [jax.ShapeDtypeStruct((h, SUBLANES, LANES), F32)] * 2,
        compiler_params=_params("parallel", "arbitrary"))(qkv_act, qkv_act, qkv_act, braw, araw, alog, dtb, inv,
                                                         du, dw, dqd, dkd, dattn, degl)


def _scan_specs(tb, heads, chunk, rev, nt):
    ti = (lambda i: nt - 1 - i) if rev else (lambda i: i)
    row = pl.BlockSpec((tb, heads * HEAD_DIM), lambda i: (ti(i), 0))
    att = pl.BlockSpec((heads, tb, chunk), lambda i: (0, ti(i), 0))
    egl = pl.BlockSpec((heads, tb // chunk, SUBLANES, LANES), lambda i: (0, ti(i), 0, 0))
    hist = pl.BlockSpec((heads, tb // chunk, HEAD_DIM, HEAD_DIM), lambda i: (0, ti(i), 0, 0))
    gn = pl.BlockSpec((SUBLANES, LANES), lambda i: (0, 0))
    return row, att, egl, hist, gn


def _gdn_scan_fwd(u, w, qd, kd, attn, egl, zb, gn):
    t = u.shape[0]
    tb = min(t, 256)
    nt, nc = t // tb, tb // GDN_CHUNK
    nh = GDN_HEADS

    def body(u_ref, w_ref, qd_ref, kd_ref, at_ref, eg_ref, z_ref, gn_ref, y_ref, hist_ref, s_ref):
        @pl.when(pl.program_id(0) == 0)
        def _():
            s_ref[...] = jnp.zeros_like(s_ref)

        g = gn_ref[0:1, :]
        state = s_ref[...]
        for c in range(nc):
            rows = pl.ds(c * GDN_CHUNK, GDN_CHUNK)
            heads = lambda r: _by_head(r, rows, nh)
            hist_ref[:, c] = state
            y, state = _gdn_scan(_mm_raw, heads(u_ref), heads(w_ref), heads(qd_ref), heads(kd_ref), at_ref[:, rows, :],
                                 eg_ref[:, c, 0:1, :], heads(z_ref), g, state)
            _store_heads(y_ref, rows, y.astype(BF16))
        s_ref[...] = state

    row, att, egs, hist, gns = _scan_specs(tb, nh, GDN_CHUNK, False, nt)
    return pl.pallas_call(
        body, name="gdn_scan_fwd", grid=(nt,), in_specs=[row, row, row, row, att, egs, row, gns], out_specs=[row, hist],
        out_shape=[jax.ShapeDtypeStruct((t, nh * HEAD_DIM), BF16),
                   jax.ShapeDtypeStruct((nh, t // GDN_CHUNK, HEAD_DIM, HEAD_DIM), F32)],
        scratch_shapes=[pltpu.VMEM((nh, HEAD_DIM, HEAD_DIM), F32)],
        compiler_params=_params("arbitrary"))(u, w, qd, kd, attn, egl, zb, gn)


def _gdn_scan_bwd(u, w, qd, kd, attn, egl, zb, gn, hist, dy):
    t = u.shape[0]
    tb = min(t, 256)
    nt, nc = t // tb, tb // GDN_CHUNK
    nh = GDN_HEADS

    def body(u_ref, w_ref, qd_ref, kd_ref, at_ref, eg_ref, z_ref, gn_ref, hist_ref, dy_ref,
             du_ref, dw_ref, dqd_ref, dkd_ref, dat_ref, deg_ref, dz_ref, dgn_ref, ds_ref):
        @pl.when(pl.program_id(0) == 0)
        def _():
            ds_ref[...] = jnp.zeros_like(ds_ref)
            dgn_ref[...] = jnp.zeros_like(dgn_ref)

        g = gn_ref[0:1, :]
        d_state = ds_ref[...]
        for c in reversed(range(nc)):
            rows = pl.ds(c * GDN_CHUNK, GDN_CHUNK)
            heads = lambda r: _by_head(r, rows, nh).astype(F32)
            _, vjp = jax.vjp(functools.partial(_gdn_scan, _mm_vjp), heads(u_ref), heads(w_ref), heads(qd_ref),
                             heads(kd_ref), at_ref[:, rows, :].astype(F32), eg_ref[:, c, 0:1, :], heads(z_ref), g,
                             hist_ref[:, c])
            du, dw, dqd, dkd, dat, deg, dz, dgn, d_state = vjp((heads(dy_ref), d_state))
            _store_heads(du_ref, rows, du)
            _store_heads(dw_ref, rows, dw)
            _store_heads(dqd_ref, rows, dqd)
            _store_heads(dkd_ref, rows, dkd)
            dat_ref[:, rows, :] = dat
            deg_ref[:, c] = jnp.broadcast_to(deg, (nh, SUBLANES, LANES))
            _store_heads(dz_ref, rows, dz.astype(BF16))
            dgn_ref[0:1, :] += dgn
        ds_ref[...] = d_state

    row, att, egs, hists, gns = _scan_specs(tb, nh, GDN_CHUNK, True, nt)
    wide = jax.ShapeDtypeStruct((t, nh * HEAD_DIM), F32)
    return pl.pallas_call(
        body, name="gdn_scan_bwd", grid=(nt,),
        in_specs=[row, row, row, row, att, egs, row, gns, hists, row],
        out_specs=[row, row, row, row, att, egs, row, gns],
        out_shape=[wide] * 4 + [jax.ShapeDtypeStruct((nh, t, GDN_CHUNK), F32),
                                jax.ShapeDtypeStruct((nh, t // GDN_CHUNK, SUBLANES, LANES), F32),
                                jax.ShapeDtypeStruct((t, nh * HEAD_DIM), BF16),
                                jax.ShapeDtypeStruct((SUBLANES, LANES), F32)],
        scratch_shapes=[pltpu.VMEM((nh, HEAD_DIM, HEAD_DIM), F32)],
        compiler_params=_params("arbitrary"))(u, w, qd, kd, attn, egl, zb, gn, hist, dy)


def _hgrn_prep(mm, qr, fr, lbl):
    n, c, _ = qr.shape
    lb = jax.nn.sigmoid(lbl[1:2, :] - lbl[0:1, :])
    f = lb + (1.0 - lb) * jax.nn.sigmoid(fr)
    q = jax.nn.silu(qr)
    k = 1.0 - f
    logf = jnp.log(f)
    ri = lax.broadcasted_iota(jnp.int32, (n, c, c), 1)
    ci = lax.broadcasted_iota(jnp.int32, (n, c, c), 2)
    b = mm((ri >= ci).astype(F32), logf, "nn", EXACT_LHS)
    attn = _hgrn_attn(mm, q, k, b)
    b_last = jnp.sum(logf, axis=1, keepdims=True)
    return q * jnp.exp(b), k * jnp.exp(b_last - b), attn, jnp.exp(b_last)


HGRN_SUB = 8
HGRN_PREP_ROWS = 2048


@functools.partial(jax.custom_vjp, nondiff_argnums=(1,))
def _roll_rows(x, shift):
    return pltpu.roll(x, shift, x.ndim - 2)


def _roll_rows_fwd(x, shift):
    return _roll_rows(x, shift), None


def _roll_rows_bwd(shift, _, d):
    return (pltpu.roll(d, d.shape[-2] - shift, d.ndim - 2),)


_roll_rows.defvjp(_roll_rows_fwd, _roll_rows_bwd)


@jax.custom_vjp
def _exp_clamped(v):
    return jnp.exp(jnp.minimum(v, 0.0))


def _exp_clamped_fwd(v):
    out = jnp.exp(jnp.minimum(v, 0.0))
    return out, out


def _exp_clamped_bwd(out, d):
    return (d * out,)


_exp_clamped.defvjp(_exp_clamped_fwd, _exp_clamped_bwd)


def _hgrn_attn(mm, q, k, b):
    n, c, d = q.shape
    sb = HGRN_SUB
    sub = lambda a: a.reshape(n * c // sb, sb, d)
    qs, ks, bs = sub(q), sub(k), sub(b)
    row = lax.broadcasted_iota(jnp.int32, (n, c, c), 1)
    col = lax.broadcasted_iota(jnp.int32, (n, c, c), 2)
    same_block = (row & -sb) == (col & -sb)
    attn = None
    for delta in range(sb):
        if delta == 0:
            prod = qs * ks
        else:
            prod = qs * _roll_rows(ks, delta) * _exp_clamped(bs - _roll_rows(bs, delta))
        sums = jnp.sum(prod, axis=-1, keepdims=True).reshape(n, c, 1)
        term = jnp.where(same_block & (row - col == delta), sums, 0.0)
        attn = term if attn is None else attn + term
    far = [jnp.zeros((n, sb, c), F32)]
    for i in range(1, c // sb):
        r0 = i * sb
        bi = b[:, r0:r0 + sb, :]
        ref = bi[:, 0:1, :]
        before = jnp.concatenate([k[:, :r0, :] * jnp.exp(ref - b[:, :r0, :]), jnp.zeros((n, c - r0, d), F32)], axis=1)
        far.append(mm(q[:, r0:r0 + sb, :] * jnp.exp(bi - ref), before, "nt", ONE_PASS))
    return attn + jnp.concatenate(far, axis=1)


def _hgrn_scan(mm, qe, kd, attn, ebl, iv, z, gn, state):
    o = mm(qe, state, "nt", ONE_PASS) + mm(attn, iv, "nn", ONE_PASS)
    new_state = state * ebl + mm(iv, kd, "tn", ONE_PASS)
    return _rms_gate(o, gn, z), new_state


def _hgrn_prep_specs(tb):
    col = pl.BlockSpec((tb, HEAD_DIM), lambda h, i: (i, h))
    lbs = pl.BlockSpec((2, HEAD_DIM), lambda h, i: (0, h))
    att = pl.BlockSpec((1, tb, HGRN_CHUNK), lambda h, i: (h, i, 0))
    ebl = pl.BlockSpec((1, tb // HGRN_CHUNK, SUBLANES, LANES), lambda h, i: (h, i, 0, 0))
    return col, lbs, att, ebl


def _hgrn_prep_fwd(qr, fr, lower_bounds):
    t = qr.shape[0]
    tb = min(t, HGRN_PREP_ROWS)
    nt, nc = t // tb, tb // HGRN_CHUNK
    hh = HGRN_HEADS

    def body(q_ref, f_ref, lb_ref, qe_ref, kd_ref, at_ref, eb_ref):
        ch = lambda r: _chunks(r, nc, HGRN_CHUNK)
        qe, kd, attn, ebl = _hgrn_prep(_mm_raw, ch(q_ref[...]), ch(f_ref[...]), lb_ref[...])
        qe_ref[...] = qe.reshape(tb, HEAD_DIM).astype(BF16)
        kd_ref[...] = kd.reshape(tb, HEAD_DIM).astype(BF16)
        at_ref[0] = attn.reshape(tb, HGRN_CHUNK).astype(BF16)
        eb_ref[0] = jnp.broadcast_to(ebl, (nc, SUBLANES, LANES))

    col, lbs, att, ebs = _hgrn_prep_specs(tb)
    return pl.pallas_call(
        body, name="hgrn_prep_fwd", grid=(hh, nt), in_specs=[col, col, lbs], out_specs=[col, col, att, ebs],
        out_shape=[jax.ShapeDtypeStruct((t, HGRN_WIDTH), BF16)] * 2
        + [jax.ShapeDtypeStruct((hh, t, HGRN_CHUNK), BF16), jax.ShapeDtypeStruct((hh, t // HGRN_CHUNK, SUBLANES, LANES), F32)],
        compiler_params=_params("parallel", "parallel"))(qr, fr, lower_bounds)


def _hgrn_prep_bwd(qr, fr, lower_bounds, dqe, dkd, dattn, debl):
    t = qr.shape[0]
    tb = min(t, HGRN_PREP_ROWS)
    nt, nc = t // tb, tb // HGRN_CHUNK
    hh = HGRN_HEADS

    def body(q_ref, f_ref, lb_ref, dqe_ref, dkd_ref, dat_ref, deb_ref, dq_ref, df_ref, dlb_ref):
        @pl.when(pl.program_id(1) == 0)
        def _():
            dlb_ref[...] = jnp.zeros_like(dlb_ref)

        ch = lambda r: _chunks(r, nc, HGRN_CHUNK)
        _, vjp = jax.vjp(functools.partial(_hgrn_prep, _mm_vjp), ch(q_ref[...]), ch(f_ref[...]), lb_ref[...])
        dq, df, dlb = vjp((ch(dqe_ref[...]), ch(dkd_ref[...]), ch(dat_ref[0]), deb_ref[0][:, 0:1, :]))
        dq_ref[...] = dq.reshape(tb, HEAD_DIM).astype(BF16)
        df_ref[...] = df.reshape(tb, HEAD_DIM).astype(BF16)
        dlb_ref[...] += dlb

    col, lbs, att, ebs = _hgrn_prep_specs(tb)
    return pl.pallas_call(
        body, name="hgrn_prep_bwd", grid=(hh, nt), in_specs=[col, col, lbs, col, col, att, ebs],
        out_specs=[col, col, lbs],
        out_shape=[jax.ShapeDtypeStruct((t, HGRN_WIDTH), BF16)] * 2 + [jax.ShapeDtypeStruct((2, HGRN_WIDTH), F32)],
        compiler_params=_params("parallel", "arbitrary"))(qr, fr, lower_bounds, dqe, dkd, dattn, debl)


def _hgrn_scan_fwd(qe, kd, attn, ebl, iv, z, gn):
    t = qe.shape[0]
    tb = min(t, 128)
    nt, nc = t // tb, tb // HGRN_CHUNK
    hh = HGRN_HEADS

    def body(qe_ref, kd_ref, at_ref, eb_ref, i_ref, z_ref, gn_ref, y_ref, hist_ref, s_ref):
        @pl.when(pl.program_id(0) == 0)
        def _():
            s_ref[...] = jnp.zeros_like(s_ref)

        g = gn_ref[0:1, :]
        state = s_ref[...]
        for c in range(nc):
            rows = pl.ds(c * HGRN_CHUNK, HGRN_CHUNK)
            heads = lambda r: _by_head(r, rows, hh)
            hist_ref[:, c] = state
            y, state = _hgrn_scan(_mm_raw, heads(qe_ref), heads(kd_ref), at_ref[:, rows, :], eb_ref[:, c, 0:1, :],
                                  heads(i_ref), heads(z_ref), g, state)
            _store_heads(y_ref, rows, y.astype(BF16))
        s_ref[...] = state

    row, att, ebs, hist, gns = _scan_specs(tb, hh, HGRN_CHUNK, False, nt)
    return pl.pallas_call(
        body, name="hgrn_scan_fwd", grid=(nt,), in_specs=[row, row, att, ebs, row, row, gns], out_specs=[row, hist],
        out_shape=[jax.ShapeDtypeStruct((t, HGRN_WIDTH), BF16),
                   jax.ShapeDtypeStruct((hh, t // HGRN_CHUNK, HEAD_DIM, HEAD_DIM), F32)],
        scratch_shapes=[pltpu.VMEM((hh, HEAD_DIM, HEAD_DIM), F32)],
        compiler_params=_params("arbitrary"))(qe, kd, attn, ebl, iv, z, gn)


def _hgrn_scan_bwd(qe, kd, attn, ebl, iv, z, gn, hist, dy):
    t = qe.shape[0]
    tb = min(t, 128)
    nt, nc = t // tb, tb // HGRN_CHUNK
    hh = HGRN_HEADS

    def body(qe_ref, kd_ref, at_ref, eb_ref, i_ref, z_ref, gn_ref, hist_ref, dy_ref,
             dqe_ref, dkd_ref, dat_ref, deb_ref, di_ref, dz_ref, dgn_ref, ds_ref):
        @pl.when(pl.program_id(0) == 0)
        def _():
            ds_ref[...] = jnp.zeros_like(ds_ref)
            dgn_ref[...] = jnp.zeros_like(dgn_ref)

        g = gn_ref[0:1, :]
        d_state = ds_ref[...]
        for c in reversed(range(nc)):
            rows = pl.ds(c * HGRN_CHUNK, HGRN_CHUNK)
            heads = lambda r: _by_head(r, rows, hh).astype(F32)
            _, vjp = jax.vjp(functools.partial(_hgrn_scan, _mm_vjp), heads(qe_ref), heads(kd_ref),
                             at_ref[:, rows, :].astype(F32), eb_ref[:, c, 0:1, :], heads(i_ref), heads(z_ref), g,
                             hist_ref[:, c])
            dqe, dkd, dat, deb, di, dz, dgn, d_state = vjp((heads(dy_ref), d_state))
            _store_heads(dqe_ref, rows, dqe)
            _store_heads(dkd_ref, rows, dkd)
            dat_ref[:, rows, :] = dat
            deb_ref[:, c] = jnp.broadcast_to(deb, (hh, SUBLANES, LANES))
            _store_heads(di_ref, rows, di.astype(BF16))
            _store_heads(dz_ref, rows, dz.astype(BF16))
            dgn_ref[0:1, :] += dgn
        ds_ref[...] = d_state

    row, att, ebs, hists, gns = _scan_specs(tb, hh, HGRN_CHUNK, True, nt)
    wide = lambda dt: jax.ShapeDtypeStruct((t, HGRN_WIDTH), dt)
    return pl.pallas_call(
        body, name="hgrn_scan_bwd", grid=(nt,),
        in_specs=[row, row, att, ebs, row, row, gns, hists, row],
        out_specs=[row, row, att, ebs, row, row, gns],
        out_shape=[wide(F32), wide(F32), jax.ShapeDtypeStruct((hh, t, HGRN_CHUNK), F32),
                   jax.ShapeDtypeStruct((hh, t // HGRN_CHUNK, SUBLANES, LANES), F32), wide(BF16), wide(BF16),
                   jax.ShapeDtypeStruct((SUBLANES, LANES), F32)],
        scratch_shapes=[pltpu.VMEM((hh, HEAD_DIM, HEAD_DIM), F32)],
        compiler_params=_params("arbitrary"))(qe, kd, attn, ebl, iv, z, gn, hist, dy)


def _layer_norm(pre, g, b):
    mu = jnp.mean(pre, axis=-1, keepdims=True)
    d = pre - mu
    var = jnp.mean(d * d, axis=-1, keepdims=True)
    return d * lax.rsqrt(var + NORM_EPS) * g + b


def _lnpl_fwd(xin, s, p, wg, wpl, ln_g, ln_b):
    t = xin.shape[0]
    tt = min(t, 512)

    def body(x_ref, s_ref, p_ref, wg_ref, wpl_ref, g_ref, b_ref, o_ref, ob_ref):
        xn = _layer_norm(DEEPNORM_ALPHA * x_ref[...] + s_ref[...], g_ref[...], b_ref[...])
        gate = jax.nn.sigmoid(_mm_raw(xn, wg_ref[...], "nn", ONE_PASS))
        out = xn + _mm_raw(p_ref[...], wpl_ref[...], "nn", ONE_PASS) * gate
        o_ref[...] = out
        ob_ref[...] = out.astype(BF16)

    row = lambda w: pl.BlockSpec((tt, w), lambda i: (i, 0))
    full = lambda a: pl.BlockSpec(a.shape, lambda i: (0, 0))
    return pl.pallas_call(
        body, name="lnpl_fwd", grid=(t // tt,),
        in_specs=[row(D_MODEL), row(D_MODEL), row(PL_DIM), full(wg), full(wpl), full(ln_g), full(ln_b)],
        out_specs=[row(D_MODEL), row(D_MODEL)],
        out_shape=[jax.ShapeDtypeStruct((t, D_MODEL), F32), jax.ShapeDtypeStruct((t, D_MODEL), BF16)],
        compiler_params=_params("parallel"))(xin, s, p, wg, wpl, ln_g, ln_b)


def _lnpl_bwd(xin, s, p, wg, wpl, ln_g, ln_b, upstream, last, name):
    t = xin.shape[0]
    tt = min(t, 512)

    def body(x_ref, s_ref, p_ref, wg_ref, wpl_ref, g_ref, b_ref, up_ref,
             dpre_ref, dwg_ref, dwpl_ref, dg_ref, db_ref, loss_ref):
        @pl.when(pl.program_id(0) == 0)
        def _():
            for r in (dwg_ref, dwpl_ref, dg_ref, db_ref, loss_ref):
                r[...] = jnp.zeros_like(r)

        pre = DEEPNORM_ALPHA * x_ref[...] + s_ref[...]
        xn, ln_vjp = jax.vjp(_layer_norm, pre, g_ref[...], b_ref[...])
        gate = jax.nn.sigmoid(_mm_raw(xn, wg_ref[...], "nn", ONE_PASS))
        plv = _mm_raw(p_ref[...], wpl_ref[...], "nn", ONE_PASS)
        if last:
            err = xn + plv * gate - up_ref[...]
            dout = err * (1.0 / D_MODEL)
            tot = jnp.sum(jnp.sum(err * err, axis=1, keepdims=True), axis=0, keepdims=True) * (0.5 / D_MODEL)
            loss_ref[...] += jnp.broadcast_to(tot, loss_ref.shape)
        else:
            dout = up_ref[...]
        dplv = dout * gate
        dlogits = dout * plv * gate * (1.0 - gate)
        dwg_ref[...] += _mm_raw(xn, dlogits, "tn", ONE_PASS)
        dwpl_ref[...] += _mm_raw(p_ref[...], dplv, "tn", ONE_PASS)
        dxn = dout + _mm_raw(dlogits, wg_ref[...], "nt", ONE_PASS)
        dpre, dg, db = ln_vjp(dxn)
        dpre_ref[...] = dpre
        dg_ref[...] += dg
        db_ref[...] += db

    row = lambda w: pl.BlockSpec((tt, w), lambda i: (i, 0))
    full = lambda shape: pl.BlockSpec(shape, lambda i: (0, 0))
    return pl.pallas_call(
        body, name=name, grid=(t // tt,),
        in_specs=[row(D_MODEL), row(D_MODEL), row(PL_DIM), full(wg.shape), full(wpl.shape), full(ln_g.shape),
                  full(ln_b.shape), row(D_MODEL)],
        out_specs=[row(D_MODEL), full(wg.shape), full(wpl.shape), full(ln_g.shape), full(ln_b.shape),
                   full((SUBLANES, LANES))],
        out_shape=[jax.ShapeDtypeStruct((t, D_MODEL), F32), jax.ShapeDtypeStruct(wg.shape, F32),
                   jax.ShapeDtypeStruct(wpl.shape, F32), jax.ShapeDtypeStruct(ln_g.shape, F32),
                   jax.ShapeDtypeStruct(ln_b.shape, F32), jax.ShapeDtypeStruct((SUBLANES, LANES), F32)],
        compiler_params=_params("arbitrary"))(xin, s, p, wg, wpl, ln_g, ln_b, upstream)


def _pack_tail(dbr, dar):
    nh, t, _ = dbr.shape
    tt = min(t, 512)

    def body(b_ref, a_ref, o_ref):
        lane = lax.broadcasted_iota(jnp.int32, (tt, LANES), 1)
        acc = jnp.zeros((tt, LANES), F32)
        for h in range(nh):
            acc = jnp.where(lane == h, b_ref[h], acc)
            acc = jnp.where(lane == nh + h, a_ref[h], acc)
        o_ref[...] = acc.astype(BF16)

    spec = pl.BlockSpec((nh, tt, LANES), lambda i: (0, i, 0))
    return pl.pallas_call(
        body, name="pack_tail", grid=(t // tt,), in_specs=[spec, spec], out_specs=pl.BlockSpec((tt, LANES), lambda i: (i, 0)),
        out_shape=jax.ShapeDtypeStruct((t, LANES), BF16), compiler_params=_params("parallel"))(dbr, dar)


def _rep_rows(v):
    return jnp.broadcast_to(v.reshape(1, LANES), (SUBLANES, LANES))


def _rep_heads(v):
    return jnp.broadcast_to(v.reshape(-1, 1, 1), (v.shape[0], SUBLANES, LANES))


def _col_range(stacked, lo, hi):
    c = stacked.shape[2]
    parts = [stacked[s, :, max(lo, s * c) - s * c:min(hi, (s + 1) * c) - s * c]
             for s in range(4) if max(lo, s * c) < min(hi, (s + 1) * c)]
    return parts[0] if len(parts) == 1 else jnp.concatenate(parts, axis=1)


def _col_shards(pieces, c):
    shards, offs, o = [], [], 0
    for pc in pieces:
        offs.append(o)
        o += pc.shape[1]
    for s in range(4):
        lo, hi = s * c, (s + 1) * c
        parts = [pc[:, max(lo, o) - o:min(hi, o + pc.shape[1]) - o] for pc, o in zip(pieces, offs)
                 if max(lo, o) < min(hi, o + pc.shape[1])]
        shards.append(parts[0] if len(parts) == 1 else jnp.concatenate(parts, axis=1))
    return jnp.stack(shards)


def _local_step(x, p, target, w, late_weights, early_grads_ready, early_grads_swapped, last_grad_ready, start_token):
    a = DEEPNORM_ALPHA
    nh = GDN_HEADS
    xb = (x + start_token).astype(BF16)
    wie = w["w_in_even"]
    w_a, w_qkv, w_zb = _col_range(wie, 0, 4096), _col_range(wie, 4096, 7168), _col_range(wie, 7168, 8192)
    w_tail = jnp.pad(_col_range(wie, 8192, 8192 + 2 * nh), ((0, 0), (0, LANES - 2 * nh)))
    conv_a_w, conv_b_w = w["conv_a_w"], w["conv_b_w"]
    ln_g0, ln_b0, ln_g1, ln_b1 = (v.reshape(1, D_MODEL) for v in (w["ln_g"][0], w["ln_b"][0], w["ln_g"][1], w["ln_b"][1]))
    alog, dtb = _rep_heads(w["a_log"].reshape(nh)), _rep_heads(w["dt_bias"].reshape(nh))
    gdn_g, hgrn_g = _rep_rows(w["gdn_norm_g"]), _rep_rows(w["hgrn_norm_g"])

    proj_a = _matmul(xb, w_a, name="fwd_proj_a")
    proj_qkv = _matmul(xb, w_qkv, name="fwd_proj_qkv")
    proj_zb = _matmul(xb, w_zb, name="fwd_proj_zb")
    proj_tail = _matmul(xb, w_tail, name="fwd_proj_tail")
    rep = lambda cols: jnp.broadcast_to(cols.T[:, :, None], (nh, cols.shape[0], LANES))
    braw, araw = rep(proj_tail[:, :nh]), rep(proj_tail[:, nh:2 * nh])
    y_a = _conv_a_fwd(proj_a, conv_a_w)
    qkv_act = _conv_b_fwd(proj_qkv, conv_b_w)
    *gdn_pre, gdn_inv = _gdn_prep_fwd(qkv_act, braw, araw, alog, dtb)
    y_b, gdn_hist = _gdn_scan_fwd(*gdn_pre, proj_zb, gdn_g)
    w = {**w, **late_weights(y_b)}
    woe, wio, woo = w["w_out_even"], w["w_in_odd"], w["w_out_odd"]
    s0 = _matmul(y_b, woe[1024:], name="fwd_out_even_b", add=_matmul(y_a, woe[:1024], name="fwd_out_even_a"))
    x1, x1b = _lnpl_fwd(x, s0, p[0], w["w_pl_gate"][0], w["w_pl"][0], ln_g0, ln_b0)
    proj_o = [_matmul(x1b, wio[j], name=f"fwd_proj_odd{j}") for j in range(4)]
    hgrn_pre = _hgrn_prep_fwd(proj_o[0], proj_o[1], w["lower_bounds"])
    y_o, hgrn_hist = _hgrn_scan_fwd(*hgrn_pre, proj_o[2], proj_o[3], hgrn_g)
    s1 = _matmul(y_o, woo, name="fwd_out_odd")

    g = {}
    dpre1, dwg1, dwpl1, dlng1, dlnb1, loss = _lnpl_bwd(x1, s1, p[1], w["w_pl_gate"][1], w["w_pl"][1], ln_g1, ln_b1,
                                                     target, True, "lnpl_bwd_odd")
    dy_o = _matmul(dpre1, woo, tb=True, name="bwd_out_odd_dx")
    g["w_out_odd"] = _matmul(y_o, dpre1, ta=True, name="bwd_out_odd_dw")
    dqe, dkd, dat, deb, di, dz, dhg = _hgrn_scan_bwd(*hgrn_pre, proj_o[2], proj_o[3], hgrn_g, hgrn_hist, dy_o)
    dq, df, dlb = _hgrn_prep_bwd(proj_o[0], proj_o[1], w["lower_bounds"], dqe, dkd, dat, deb)
    dx1 = dpre1
    scale = a
    dws = []
    for j, dj in enumerate((dq, df, di, dz)):
        dx1 = _matmul(dj, wio[j], tb=True, add=dx1, add_scale=scale, name=f"bwd_proj_odd_dx{j}")
        scale = 1.0
        dws.append(_matmul(x1b, dj, ta=True, name=f"bwd_proj_odd_dw{j}"))
    g["w_in_odd"] = jnp.stack(dws)
    g["hgrn_norm_g"] = dhg[0:1]
    g["lower_bounds"] = dlb
    g["w_pl_gate1"], g["w_pl1"] = dwg1, dwpl1

    dpre0, dwg0, dwpl0, dlng0, dlnb0, _ = _lnpl_bwd(x, s0, p[0], w["w_pl_gate"][0], w["w_pl"][0], ln_g0, ln_b0,
                                                  dx1, False, "lnpl_bwd_even")
    g["w_pl_gate0"], g["w_pl0"] = dwg0, dwpl0
    g["ln_g"] = jnp.concatenate([dlng0, dlng1], axis=0)
    g["ln_b"] = jnp.concatenate([dlnb0, dlnb1], axis=0)
    dy_a = _matmul(dpre0, woe[:1024], tb=True, name="bwd_out_even_dxa")
    dy_b = _matmul(dpre0, woe[1024:], tb=True, name="bwd_out_even_dxb")
    g["w_out_even"] = jnp.concatenate([_matmul(y_a, dpre0, ta=True, name="bwd_out_even_dwa"),
                                       _matmul(y_b, dpre0, ta=True, name="bwd_out_even_dwb")], axis=0)
    token = early_grads_ready({n: g[n] for n in _RS_EARLY})
    d_a, dwa = _conv_a_bwd(proj_a, conv_a_w + token, dy_a)
    g["conv_a_w"] = dwa[:3]
    gdn_g = gdn_g + early_grads_swapped(d_a)
    du, dw, dqd, dkd, dat, deg, dzb, dgn = _gdn_scan_bwd(*gdn_pre, proj_zb, gdn_g, gdn_hist, dy_b)
    dqa, dka, dva, dbr, dar, dal, ddt = _gdn_prep_bwd(qkv_act, braw, araw, alog, dtb, gdn_inv, du, dw, dqd, dkd, dat, deg)
    g["a_log"] = dal[:, 0, 0].reshape(1, nh)
    g["dt_bias"] = ddt[:, 0, 0].reshape(1, nh)
    g["gdn_norm_g"] = dgn[0:1]
    d_pre_qkv, dwb = [], []
    for j, dj in enumerate((dqa, dka, dva)):
        dpj, dwj = _conv_b_bwd(proj_qkv, conv_b_w, dj, j, f"conv_b_bwd{j}")
        d_pre_qkv.append(dpj)
        dwb.append(dwj[:4])
    g["conv_b_w"] = jnp.concatenate(dwb, axis=1)
    d_tail = _pack_tail(dbr, dar)
    pieces = [(d_a, w_a), (d_pre_qkv[0], w_qkv[:, :1024]), (d_pre_qkv[1], w_qkv[:, 1024:2048]),
              (d_pre_qkv[2], w_qkv[:, 2048:]), (dzb, w_zb), (d_tail, w_tail)]
    dws = [_matmul(xb, dj, ta=True, name=f"bwd_proj_even_dw{j}") for j, (dj, _) in enumerate(pieces)]
    dws[-1] = dws[-1][:, :2 * nh]
    g["w_in_even"] = _col_shards(dws, wie.shape[2])
    token = last_grad_ready(g["w_in_even"])
    dx = dpre0
    scale = a
    for j, (dj, wj) in enumerate(pieces):
        dx = _matmul(dj, wj + jnp.asarray(token).astype(BF16) if j == 0 else wj, tb=True, add=dx, add_scale=scale,
                     name=f"bwd_proj_even_dx{j}")
        scale = 1.0
    return loss, dx, g


def _adamw(w, g, m, v, name):
    lead, rows, cols = w.shape
    if rows % SUBLANES == 0 or rows <= 256:
        tr, tc = (rows if rows <= 256 else 256), cols
    else:
        tr, tc = rows, 256
    assert rows % tr == 0 and cols % tc == 0, (name, rows, cols)

    def body(w_ref, g_ref, m_ref, v_ref, d_ref, nm_ref, nv_ref):
        gg = g_ref[...]
        nm = ADAM_B1 * m_ref[...] + (1.0 - ADAM_B1) * gg
        nv = ADAM_B2 * v_ref[...] + (1.0 - ADAM_B2) * jnp.square(gg)
        m_hat = nm / (1.0 - ADAM_B1 ** ADAM_STEP)
        v_hat = nv / (1.0 - ADAM_B2 ** ADAM_STEP)
        d_ref[...] = -ADAM_LR * (m_hat / (jnp.sqrt(v_hat) + ADAM_EPS) + ADAM_WD * w_ref[...])
        nm_ref[...] = nm
        nv_ref[...] = nv

    spec = pl.BlockSpec((1, tr, tc), lambda l, i, j: (l, i, j))
    return pl.pallas_call(
        body, name=name, grid=(lead, rows // tr, cols // tc), in_specs=[spec] * 4, out_specs=[spec] * 3,
        out_shape=[jax.ShapeDtypeStruct(w.shape, F32)] * 3,
        compiler_params=_params("parallel", "parallel", "parallel"))(w, g, m, v)


MESH = pl.DeviceIdType.MESH
N_DEV = 8
HBM_SPEC = pl.BlockSpec(memory_space=pltpu.HBM)
VMEM_SPEC = pl.BlockSpec(memory_space=pltpu.VMEM)


def _coords():
    return lax.axis_index("x"), lax.axis_index("y"), lax.axis_index("c")


def _flip(v, bit):
    return 1 - v if bit else v


def _remote(src, dst, send_sem, recv_sem, dev):
    return pltpu.make_async_remote_copy(src_ref=src, dst_ref=dst, send_sem=send_sem, recv_sem=recv_sem,
                                        device_id=dev, device_id_type=MESH)


def _exchange_small(buf, reduce, name):
    rows = buf.shape[0]

    def body(in_ref, out_ref, slots, send_sems, recv_sems):
        x, y, c = _coords()
        me = 4 * x + 2 * y + c
        slots[me] = in_ref[...]
        peer = lambda k: (_flip(x, (k >> 2) & 1), _flip(y, (k >> 1) & 1), _flip(c, k & 1))
        sends = []
        for k in range(1, N_DEV):
            cp = _remote(in_ref, slots.at[me], send_sems.at[k - 1], recv_sems.at[k - 1], peer(k))
            cp.start()
            sends.append(cp)
        for k in range(1, N_DEV):
            px, py, pc = peer(k)
            _remote(in_ref, slots.at[4 * px + 2 * py + pc], send_sems.at[k - 1], recv_sems.at[k - 1], peer(k)).wait_recv()
        for cp in sends:
            cp.wait_send()
        if reduce:
            acc = slots[0]
            for d in range(1, N_DEV):
                acc = acc + slots[d]
            out_ref[...] = acc
        else:
            out_ref[...] = slots[...]

    out_shape = (rows, LANES) if reduce else (N_DEV, rows, LANES)
    return pl.pallas_call(
        body, name=name, in_specs=[VMEM_SPEC], out_specs=VMEM_SPEC, out_shape=jax.ShapeDtypeStruct(out_shape, F32),
        scratch_shapes=[pltpu.VMEM((N_DEV, rows, LANES), F32), pltpu.SemaphoreType.DMA((N_DEV - 1,)),
                        pltpu.SemaphoreType.DMA((N_DEV - 1,))])(buf)


def _half_rows(half, which):
    return pl.ds(pl.multiple_of(which * half, 16), half)


def _other_chip(x, y, k):
    return _flip(x, (k >> 1) & 1), _flip(y, k & 1)


SEM_SPEC = pl.BlockSpec(memory_space=pltpu.SEMAPHORE)
DATAFLOW = pltpu.SideEffectType.DATAFLOW_SIDE_EFFECTING


def _ici_piece(srcs, lands, send_sems, recv_sems, i, k, x, y, c):
    half = srcs[i].shape[0] // 2
    ox, oy = _other_chip(x, y, k)
    return _remote(srcs[i].at[_half_rows(half, c)], lands[i].at[2 * x + y, _half_rows(half, c)],
                   send_sems.at[3 * i + k - 1], recv_sems.at[3 * i + k - 1], (ox, oy, c)), (ox, oy)


def _gather_start(shards, name):
    n = len(shards)

    def body(*refs):
        srcs, lands = refs[:n], refs[n:2 * n]
        send_sems, recv_sems = refs[2 * n], refs[2 * n + 1]
        token = refs[-1]
        x, y, c = _coords()
        for i in range(n):
            for k in (1, 2, 3):
                _ici_piece(srcs, lands, send_sems, recv_sems, i, k, x, y, c)[0].start()
        token[...] = jnp.zeros_like(token)

    hbm = lambda a: pltpu.with_memory_space_constraint(a, pltpu.HBM)
    lands = [lax.empty((4,) + s.shape, s.dtype) for s in shards]
    out = pl.pallas_call(
        body, name=name,
        out_shape=(pltpu.SemaphoreType.DMA((3 * n,)), pltpu.SemaphoreType.DMA((3 * n,)),
                   *[pltpu.HBM(s.shape, s.dtype) for s in shards], *[pltpu.HBM(a.shape, a.dtype) for a in lands],
                   jax.ShapeDtypeStruct((SUBLANES, LANES), F32)),
        in_specs=[HBM_SPEC] * (2 * n), out_specs=(SEM_SPEC, SEM_SPEC, *[HBM_SPEC] * (2 * n), VMEM_SPEC),
        input_output_aliases={i: 2 + i for i in range(2 * n)},
        compiler_params=pltpu.CompilerParams(has_side_effects=DATAFLOW))(*[hbm(s) for s in shards], *[hbm(a) for a in lands])
    return out[0], out[1], out[2:2 + n], out[2 + n:2 + 2 * n], out[-1]


def _gather_wait(send_sems, recv_sems, srcs, lands, after, name):
    n = len(srcs)

    def body(*refs):
        src_refs, land_refs = refs[:n], refs[n:2 * n]
        send_sems, recv_sems = refs[2 * n], refs[2 * n + 1]
        x, y, c = _coords()
        for i in range(n):
            half = src_refs[i].shape[0] // 2
            for k in (1, 2, 3):
                cp, (ox, oy) = _ici_piece(src_refs, land_refs, send_sems, recv_sems, i, k, x, y, c)
                cp.wait_send()
                piece = land_refs[i].at[2 * ox + oy, _half_rows(half, c)]
                _remote(piece, piece, send_sems.at[3 * i + k - 1], recv_sems.at[3 * i + k - 1], (ox, oy, c)).wait_recv()

    out = pl.pallas_call(
        body, name=name,
        out_shape=(*[pltpu.HBM(s.shape, s.dtype) for s in srcs], *[pltpu.HBM(a.shape, a.dtype) for a in lands]),
        in_specs=[HBM_SPEC] * (2 * n) + [SEM_SPEC, SEM_SPEC, pl.BlockSpec(memory_space=pl.ANY)],
        out_specs=tuple([HBM_SPEC] * (2 * n)), input_output_aliases={i: i for i in range(2 * n)},
        compiler_params=pltpu.CompilerParams(has_side_effects=DATAFLOW))(*srcs, *lands, send_sems, recv_sems, after)
    return out[n:]


def _gather_forward(lands, name):
    n = len(lands)

    def body(*refs):
        ins, outs = refs[:n], refs[n:2 * n]
        send_sems, recv_sems = refs[2 * n:]
        x, y, c = _coords()
        sends = []
        for i in range(n):
            half = ins[i].shape[1] // 2
            for k in (1, 2, 3):
                ox, oy = _other_chip(x, y, k)
                cp = _remote(ins[i].at[2 * ox + oy, _half_rows(half, c)], outs[i].at[2 * ox + oy, _half_rows(half, c)],
                             send_sems.at[3 * i + k - 1], recv_sems.at[3 * i + k - 1], (x, y, 1 - c))
                cp.start()
                sends.append(cp)
        for i in range(n):
            half = ins[i].shape[1] // 2
            for k in (1, 2, 3):
                ox, oy = _other_chip(x, y, k)
                piece = outs[i].at[2 * ox + oy, _half_rows(half, 1 - c)]
                _remote(piece, piece, send_sems.at[3 * i + k - 1], recv_sems.at[3 * i + k - 1], (x, y, 1 - c)).wait_recv()
        for cp in sends:
            cp.wait_send()

    return pl.pallas_call(
        body, name=name, in_specs=[HBM_SPEC] * n, out_specs=[HBM_SPEC] * n,
        out_shape=[jax.ShapeDtypeStruct(a.shape, a.dtype) for a in lands],
        input_output_aliases={i: i for i in range(n)},
        scratch_shapes=[pltpu.SemaphoreType.DMA((3 * n,))] * 2)(*lands)


def _rs_sibling_swap(g4s, name):
    n = len(g4s)

    def body(*refs):
        ins, outs = refs[:n], refs[n:2 * n]
        send_sems, recv_sems = refs[2 * n:]
        x, y, c = _coords()
        sends = []
        for i in range(n):
            half = ins[i].shape[1] // 2
            for s in range(4):
                cp = _remote(ins[i].at[s, _half_rows(half, 1 - c)], outs[i].at[s], send_sems.at[4 * i + s],
                             recv_sems.at[4 * i + s], (x, y, 1 - c))
                cp.start()
                sends.append(cp)
        for cp in sends:
            cp.wait_recv()
        for cp in sends:
            cp.wait_send()

    return pl.pallas_call(
        body, name=name, in_specs=[HBM_SPEC] * n, out_specs=[HBM_SPEC] * n,
        out_shape=[jax.ShapeDtypeStruct((4, g.shape[1] // 2, g.shape[2]), g.dtype) for g in g4s],
        scratch_shapes=[pltpu.SemaphoreType.DMA((4 * n,))] * 2)(*g4s)


def _rs_add_sibling(g4, got, c_idx, name):
    _, rows, cols = g4.shape
    half = rows // 2
    tr = min(half, 256)
    nb = half // tr

    def body(c_ref, a_ref, b_ref, o_ref, ob_ref):
        total = a_ref[...] + b_ref[...]
        o_ref[...] = total
        ob_ref[...] = total.astype(BF16)

    blk = (1, tr, cols)
    out = pl.BlockSpec(blk, lambda s, i, c_ref: (s, i, 0))
    grid_spec = pltpu.PrefetchScalarGridSpec(
        num_scalar_prefetch=1, grid=(4, nb),
        in_specs=[pl.BlockSpec(blk, lambda s, i, c_ref: (s, c_ref[0] * nb + i, 0)), out],
        out_specs=[out, out])
    return pl.pallas_call(
        body, name=name, grid_spec=grid_spec,
        out_shape=[jax.ShapeDtypeStruct(got.shape, F32), jax.ShapeDtypeStruct(got.shape, BF16)],
        compiler_params=_params("parallel", "parallel"))(c_idx, g4, got)


def _rs_add_chips(p4, got3, idx, name):
    _, half, cols = p4.shape
    tr = min(half, 256)
    nb = half // tr

    def body(idx_ref, p_ref, a_ref, b_ref, c_ref, o_ref):
        o_ref[...] = ((p_ref[0] + a_ref[0].astype(F32)) + b_ref[0].astype(F32)) + c_ref[0].astype(F32)

    blk = (1, tr, cols)
    grid_spec = pltpu.PrefetchScalarGridSpec(
        num_scalar_prefetch=1, grid=(nb,),
        in_specs=[pl.BlockSpec(blk, lambda i, idx_ref: (idx_ref[0], i, 0))]
        + [pl.BlockSpec(blk, functools.partial(lambda k, i, idx_ref: (k, i, 0), k)) for k in range(3)],
        out_specs=pl.BlockSpec((tr, cols), lambda i, idx_ref: (idx_ref[1] * nb + i, 0)))
    return pl.pallas_call(body, name=name, grid_spec=grid_spec, out_shape=jax.ShapeDtypeStruct((2 * half, cols), F32),
                          compiler_params=_params("parallel"))(idx, p4, got3, got3, got3)


def _rs_share_halves(bufs, name):
    n = len(bufs)

    def body(*refs):
        ins, outs = refs[:n], refs[n:2 * n]
        send_sems, recv_sems = refs[2 * n:]
        x, y, c = _coords()
        sends = []
        for i in range(n):
            half = ins[i].shape[0] // 2
            cp = _remote(ins[i].at[_half_rows(half, c)], outs[i].at[_half_rows(half, c)], send_sems.at[i],
                         recv_sems.at[i], (x, y, 1 - c))
            cp.start()
            sends.append(cp)
        for i in range(n):
            half = ins[i].shape[0] // 2
            _remote(ins[i].at[_half_rows(half, c)], outs[i].at[_half_rows(half, 1 - c)], send_sems.at[i],
                    recv_sems.at[i], (x, y, 1 - c)).wait_recv()
        for cp in sends:
            cp.wait_send()

    return pl.pallas_call(
        body, name=name, in_specs=[HBM_SPEC] * n, out_specs=[HBM_SPEC] * n,
        out_shape=[jax.ShapeDtypeStruct(b.shape, b.dtype) for b in bufs],
        input_output_aliases={i: i for i in range(n)},
        scratch_shapes=[pltpu.SemaphoreType.DMA((n,))] * 2)(*bufs)


def _scatter_piece(srcs, lands, send_sems, recv_sems, i, k, x, y, c):
    ox, oy = _other_chip(x, y, k)
    return _remote(srcs[i].at[2 * ox + oy], lands[i].at[k - 1], send_sems.at[3 * i + k - 1],
                   recv_sems.at[3 * i + k - 1], (ox, oy, c))


def _rs_scatter_start(p4s, name):
    n = len(p4s)

    def body(*refs):
        srcs, lands = refs[:n], refs[n:2 * n]
        send_sems, recv_sems = refs[2 * n], refs[2 * n + 1]
        token = refs[-1]
        x, y, c = _coords()
        for i in range(n):
            for k in (1, 2, 3):
                _scatter_piece(srcs, lands, send_sems, recv_sems, i, k, x, y, c).start()
        token[...] = jnp.zeros_like(token)

    hbm = lambda a: pltpu.with_memory_space_constraint(a, pltpu.HBM)
    lands = [lax.empty((3,) + p.shape[1:], p.dtype) for p in p4s]
    out = pl.pallas_call(
        body, name=name,
        out_shape=(pltpu.SemaphoreType.DMA((3 * n,)), pltpu.SemaphoreType.DMA((3 * n,)),
                   *[pltpu.HBM(p.shape, p.dtype) for p in p4s], *[pltpu.HBM(a.shape, a.dtype) for a in lands],
                   jax.ShapeDtypeStruct((SUBLANES, LANES), F32)),
        in_specs=[HBM_SPEC] * (2 * n), out_specs=(SEM_SPEC, SEM_SPEC, *[HBM_SPEC] * (2 * n), VMEM_SPEC),
        input_output_aliases={i: 2 + i for i in range(2 * n)},
        compiler_params=pltpu.CompilerParams(has_side_effects=DATAFLOW))(*[hbm(p) for p in p4s], *[hbm(a) for a in lands])
    return out[0], out[1], out[2:2 + n], out[2 + n:2 + 2 * n], out[-1]


def _rs_scatter_wait(send_sems, recv_sems, srcs, lands, after, name):
    n = len(srcs)

    def body(*refs):
        src_refs, land_refs = refs[:n], refs[n:2 * n]
        send_sems, recv_sems = refs[2 * n], refs[2 * n + 1]
        x, y, c = _coords()
        for i in range(n):
            for k in (1, 2, 3):
                cp = _scatter_piece(src_refs, land_refs, send_sems, recv_sems, i, k, x, y, c)
                cp.wait_send()
                cp.wait_recv()

    out = pl.pallas_call(
        body, name=name,
        out_shape=(*[pltpu.HBM(s.shape, s.dtype) for s in srcs], *[pltpu.HBM(a.shape, a.dtype) for a in lands]),
        in_specs=[HBM_SPEC] * (2 * n) + [SEM_SPEC, SEM_SPEC, pl.BlockSpec(memory_space=pl.ANY)],
        out_specs=tuple([HBM_SPEC] * (2 * n)), input_output_aliases={i: i for i in range(2 * n)},
        compiler_params=pltpu.CompilerParams(has_side_effects=DATAFLOW))(*srcs, *lands, send_sems, recv_sems, after)
    return out[n:]


def _swap_piece(srcs, lands, send_sems, recv_sems, i, s, x, y, c):
    half = srcs[i].shape[1] // 2
    return _remote(srcs[i].at[s, _half_rows(half, 1 - c)], lands[i].at[s], send_sems.at[4 * i + s],
                   recv_sems.at[4 * i + s], (x, y, 1 - c))


def _rs_swap_start(g4s, name):
    n = len(g4s)

    def body(*refs):
        srcs, lands = refs[:n], refs[n:2 * n]
        send_sems, recv_sems = refs[2 * n], refs[2 * n + 1]
        token = refs[-1]
        x, y, c = _coords()
        for i in range(n):
            for s in range(4):
                _swap_piece(srcs, lands, send_sems, recv_sems, i, s, x, y, c).start()
        token[...] = jnp.zeros_like(token)

    hbm = lambda a: pltpu.with_memory_space_constraint(a, pltpu.HBM)
    lands = [lax.empty((4, g.shape[1] // 2, g.shape[2]), g.dtype) for g in g4s]
    out = pl.pallas_call(
        body, name=name,
        out_shape=(pltpu.SemaphoreType.DMA((4 * n,)), pltpu.SemaphoreType.DMA((4 * n,)),
                   *[pltpu.HBM(g.shape, g.dtype) for g in g4s], *[pltpu.HBM(a.shape, a.dtype) for a in lands],
                   jax.ShapeDtypeStruct((SUBLANES, LANES), F32)),
        in_specs=[HBM_SPEC] * (2 * n), out_specs=(SEM_SPEC, SEM_SPEC, *[HBM_SPEC] * (2 * n), VMEM_SPEC),
        input_output_aliases={i: 2 + i for i in range(2 * n)},
        compiler_params=pltpu.CompilerParams(has_side_effects=DATAFLOW))(*[hbm(g) for g in g4s], *[hbm(a) for a in lands])
    return out[0], out[1], out[2:2 + n], out[2 + n:2 + 2 * n], out[-1]


def _rs_swap_wait(send_sems, recv_sems, srcs, lands, after, name):
    n = len(srcs)

    def body(*refs):
        src_refs, land_refs = refs[:n], refs[n:2 * n]
        send_sems, recv_sems = refs[2 * n], refs[2 * n + 1]
        x, y, c = _coords()
        for i in range(n):
            for s in range(4):
                cp = _swap_piece(src_refs, land_refs, send_sems, recv_sems, i, s, x, y, c)
                cp.wait_send()
                cp.wait_recv()

    out = pl.pallas_call(
        body, name=name,
        out_shape=(*[pltpu.HBM(s.shape, s.dtype) for s in srcs], *[pltpu.HBM(a.shape, a.dtype) for a in lands]),
        in_specs=[HBM_SPEC] * (2 * n) + [SEM_SPEC, SEM_SPEC, pl.BlockSpec(memory_space=pl.ANY)],
        out_specs=tuple([HBM_SPEC] * (2 * n)), input_output_aliases={i: i for i in range(2 * n)},
        compiler_params=pltpu.CompilerParams(has_side_effects=DATAFLOW))(*srcs, *lands, send_sems, recv_sems, after)
    return out[:n], out[n:]


def _rs_front(g4s, names, tag):
    c_idx = jnp.stack([lax.axis_index("c")]).astype(jnp.int32)
    got = _rs_sibling_swap(g4s, f"rs_sibling_swap_{tag}")
    return [_rs_add_sibling(g, s, c_idx, f"rs_add_sibling_{nm}") for g, s, nm in zip(g4s, got, names)]


def _rs_back(p4s, got3, names):
    x, y, c = _coords()
    idx = jnp.stack([2 * x + y, c]).astype(jnp.int32)
    return [_rs_add_chips(p, t, idx, f"rs_add_chips_{nm}") for (p, _), t, nm in zip(p4s, got3, names)]


def _cols_split(full):
    r, c4 = full.shape
    return full.reshape(r, 4, c4 // 4).transpose(1, 0, 2)


_BIG = {
    "w_in_even": ((1024, 2052), lambda s: s),
    "w_out_even": ((512, 1024), lambda s: s.reshape(2048, 1024)),
    "w_in_odd": ((1024, 2048), lambda s: s),
    "w_out_odd": ((512, 1024), lambda s: s.reshape(2048, 1024)),
    "w_pl": ((512, 256), lambda s: s.reshape(4, 2, 256, 256).transpose(1, 2, 0, 3).reshape(2, 256, 1024)),
    "w_pl_gate": ((512, 1024), lambda s: s.reshape(4, 2, 256, 1024).transpose(1, 0, 2, 3).reshape(2, 1024, 1024)),
}


_RS_EARLY = {
    "w_in_odd": lambda f: f,
    "w_out_odd": lambda f: f.reshape(4, 512, 1024),
    "w_pl_gate1": lambda f: f.reshape(4, 256, 1024),
    "w_pl1": _cols_split,
    "w_out_even": lambda f: f.reshape(4, 512, 1024),
    "w_pl_gate0": lambda f: f.reshape(4, 256, 1024),
    "w_pl0": _cols_split,
}
_RS_LATE = {"w_in_even": lambda f: f}


def _size(shape):
    n = 1
    for d in shape:
        n *= d
    return n


_SMALL = {"a_log": (1, 8), "dt_bias": (1, 8), "gdn_norm_g": (1, 128), "hgrn_norm_g": (1, 128),
          "lower_bounds": (2, 2048), "ln_g": (2, 1024), "ln_b": (2, 1024), "conv_a_w": (3, 1024), "conv_b_w": (4, 3072)}
_CONV_SHARD = {"conv_a_w": (3, 256), "conv_b_w": (4, 768)}


def _pack_small(parts, shapes, head_rows=0):
    rows = []
    for n, shape in shapes.items():
        v = parts[n].reshape(-1)
        rows.append(jnp.pad(v, (0, -v.shape[0] % LANES)).reshape(-1, LANES))
    buf = jnp.concatenate(rows, axis=0)
    return jnp.pad(buf, ((head_rows, -(buf.shape[0] + head_rows) % SUBLANES), (0, 0)))


def _unpack_small(buf, shapes, head_rows=0):
    out, off = {}, head_rows
    for n, shape in shapes.items():
        nrow = -(-_size(shape) // LANES)
        out[n] = buf[off:off + nrow].reshape(-1)[:_size(shape)].reshape(shape)
        off += nrow
    return out


_WEIGHTS = ["w_in_even", "conv_a_w", "conv_b_w", "a_log", "dt_bias", "gdn_norm_g", "w_out_even", "w_in_odd",
            "lower_bounds", "hgrn_norm_g", "w_out_odd", "ln_g", "ln_b", "w_pl", "w_pl_gate"]


def kernel(x, p, w_in_even, conv_a_w, conv_b_w, a_log, dt_bias, gdn_norm_g, w_out_even, w_in_odd, lower_bounds, hgrn_norm_g, w_out_odd, ln_g, ln_b, w_pl, w_pl_gate, loss_target, m_w_in_even, m_conv_a_w, m_conv_b_w, m_a_log, m_dt_bias, m_gdn_norm_g, m_w_out_even, m_w_in_odd, m_lower_bounds, m_hgrn_norm_g, m_w_out_odd, m_ln_g, m_ln_b, m_w_pl, m_w_pl_gate, v_w_in_even, v_conv_a_w, v_conv_b_w, v_a_log, v_dt_bias, v_gdn_norm_g, v_w_out_even, v_w_in_odd, v_lower_bounds, v_hgrn_norm_g, v_w_out_odd, v_ln_g, v_ln_b, v_w_pl, v_w_pl_gate):
    w = dict(zip(_WEIGHTS, (w_in_even, conv_a_w, conv_b_w, a_log, dt_bias, gdn_norm_g, w_out_even, w_in_odd,
                            lower_bounds, hgrn_norm_g, w_out_odd, ln_g, ln_b, w_pl, w_pl_gate)))
    m = dict(zip(_WEIGHTS, (m_w_in_even, m_conv_a_w, m_conv_b_w, m_a_log, m_dt_bias, m_gdn_norm_g, m_w_out_even,
                            m_w_in_odd, m_lower_bounds, m_hgrn_norm_g, m_w_out_odd, m_ln_g, m_ln_b, m_w_pl, m_w_pl_gate)))
    v = dict(zip(_WEIGHTS, (v_w_in_even, v_conv_a_w, v_conv_b_w, v_a_log, v_dt_bias, v_gdn_norm_g, v_w_out_even,
                            v_w_in_odd, v_lower_bounds, v_hgrn_norm_g, v_w_out_odd, v_ln_g, v_ln_b, v_w_pl, v_w_pl_gate)))
    chip = 2 * lax.axis_index("x") + lax.axis_index("y")

    names = list(_BIG)
    shard_shapes = {n: _BIG[n][0] for n in names}
    early, late = names[:1], names[1:]
    shards = {n: w[n].reshape(shard_shapes[n]).astype(BF16) for n in early}
    whole = lambda n, stacked: _BIG[n][1](lax.dynamic_update_slice(stacked, shards[n][None], (chip, 0, 0)))
    conv_mine = _pack_small({n: w[n] for n in _CONV_SHARD}, _CONV_SHARD)
    conv_all = _exchange_small(conv_mine, False, "gather_conv")
    shards, conv_all = lax.optimization_barrier((shards, conv_all))
    first = _gather_start([shards[n] for n in early], "gather_first_start")
    shards.update({n: (w[n].reshape(shard_shapes[n]) + first[4][0, 0]).astype(BF16) for n in late})
    send_sems, recv_sems, srcs, lands, token = _gather_start([shards[n] for n in late], "gather_rest_start")

    def late_weights(after):
        landed = _gather_forward(_gather_wait(send_sems, recv_sems, srcs, lands, after, "gather_rest_wait"),
                                 "gather_rest_forward")
        return {n: whole(n, ga) for n, ga in zip(late, landed)}

    landed = _gather_forward(_gather_wait(*first[:4], token, "gather_first_wait"), "gather_first_forward")
    full = {n: whole(n, ga) for n, ga in zip(early, landed)}
    conv_by_chip = [_unpack_small(conv_all[2 * s], _CONV_SHARD) for s in range(4)]
    for n in _CONV_SHARD:
        full[n] = jnp.concatenate([conv_by_chip[s][n] for s in range(4)], axis=1)
    for n in _SMALL:
        if n not in _CONV_SHARD:
            full[n] = w[n]

    early_rs = {}

    def early_grads_ready(grads):
        early_rs["swap"] = _rs_swap_start([_RS_EARLY[n](grads[n]) for n in _RS_EARLY], "rs_swap_early_start")
        return early_rs["swap"][4][0, 0]

    def early_grads_swapped(after):
        g4s, got = _rs_swap_wait(*early_rs["swap"][:4], after, "rs_swap_early_wait")
        c_idx = jnp.stack([lax.axis_index("c")]).astype(jnp.int32)
        early_rs["p4s"] = [_rs_add_sibling(g_, s_, c_idx, f"rs_add_sibling_{nm}") for g_, s_, nm in zip(g4s, got, _RS_EARLY)]
        early_rs["sems"] = _rs_scatter_start([pb for _, pb in early_rs["p4s"]], "rs_scatter_early_start")
        return early_rs["sems"][4][0, 0]

    late_rs = {}

    def last_grad_ready(grad):
        late_rs["p4s"] = _rs_front([_RS_LATE[n](grad) for n in _RS_LATE], list(_RS_LATE), "late")
        late_rs["sems"] = _rs_scatter_start([pb for _, pb in late_rs["p4s"]], "rs_scatter_late_start")
        return late_rs["sems"][4][0, 0]

    loss_part, dx, g = _local_step(x[0], p[:, 0], loss_target[0], full, late_weights, early_grads_ready,
                                   early_grads_swapped, last_grad_ready, token[0, 0])

    late_p4s, late_sems = late_rs["p4s"], late_rs["sems"]
    got3 = _rs_scatter_wait(*early_rs["sems"][:4], dx, "rs_scatter_early_wait")
    summed = dict(zip(_RS_EARLY, _rs_share_halves(_rs_back(early_rs["p4s"], got3, list(_RS_EARLY)), "rs_share_early")))
    g_big = {n: summed[n] for n in names if n in summed}
    g_big["w_pl"] = jnp.stack([summed["w_pl0"], summed["w_pl1"]])
    g_big["w_pl_gate"] = jnp.stack([summed["w_pl_gate0"], summed["w_pl_gate1"]])
    small_sum = _exchange_small(jnp.concatenate([loss_part, _pack_small(g, _SMALL)], axis=0), True, "reduce_small")
    loss = small_sum[0, 0]
    g_small = _unpack_small(small_sum, _SMALL, head_rows=SUBLANES)
    for n, (rows, cols) in _CONV_SHARD.items():
        g_small[n] = lax.dynamic_slice_in_dim(g_small[n], chip * cols, cols, axis=1)

    grads, delta, new_m, new_v = {}, {}, {}, {}
    for n in late:
        grads[n] = g_big[n].reshape(w[n].shape)
        delta[n], new_m[n], new_v[n] = _adamw(w[n], grads[n], m[n], v[n], f"adamw_{n}")
    own = {n: (_CONV_SHARD[n] if n in _CONV_SHARD else _SMALL[n]) for n in _SMALL}
    packs = [_pack_small({n: src[n] for n in _SMALL}, own)[None] for src in (w, g_small, m, v)]
    outs = [_unpack_small(t[0], own) for t in _adamw(*packs, "adamw_small")]
    for n in _SMALL:
        grads[n] = g_small[n].reshape(w[n].shape)
        delta[n], new_m[n], new_v[n] = (t[n].reshape(w[n].shape) for t in outs)
    got3 = _rs_scatter_wait(*late_sems[:4], new_v["w_in_odd"], "rs_scatter_late_wait")
    (g_in_even,) = _rs_share_halves(_rs_back(late_p4s, got3, list(_RS_LATE)), "rs_share_late")
    for n in early:
        t_ = lambda a: jnp.swapaxes(a, 1, 2)
        g_t = t_(g_in_even.reshape(w[n].shape))
        grads[n] = t_(g_t)
        delta[n], new_m[n], new_v[n] = (t_(o) for o in _adamw(t_(w[n]), g_t, t_(m[n]), t_(v[n]), f"adamw_{n}"))
    return (loss, dx[None], *[grads[n] for n in _WEIGHTS], *[delta[n] for n in _WEIGHTS],
            *[new_m[n] for n in _WEIGHTS], *[new_v[n] for n in _WEIGHTS])
```

```python
import functools

import jax
import jax.numpy as jnp
from jax import lax
from jax.experimental import pallas as pl
from jax.experimental.pallas import tpu as pltpu

F32 = jnp.float32
BF16 = jnp.bfloat16

D_MODEL = 1024
PL_DIM = 256
GDN_HEADS = 8
HEAD_DIM = 128
GDN_CHUNK = 64
HGRN_HEADS = 16
HGRN_CHUNK = 32
HGRN_WIDTH = 2048
DEEPNORM_ALPHA = 4.0 ** 0.25
NORM_EPS = 1e-5
ADAM_LR, ADAM_B1, ADAM_B2, ADAM_EPS, ADAM_WD, ADAM_STEP = 0.001, 0.9, 0.999, 1e-08, 0.01, 10

VMEM_LIMIT = 60 * 1024 * 1024
SUBLANES = 8
LANES = 128


def _params(*sem):
    return pltpu.CompilerParams(dimension_semantics=sem, vmem_limit_bytes=VMEM_LIMIT)


ONE_PASS, THREE_PASS, EXACT_LHS, EXACT_RHS = 0, 1, 2, 3


def _split3(v):
    hi = v.astype(BF16)
    r1 = v - hi.astype(F32)
    mid = r1.astype(BF16)
    return hi, mid, (r1 - mid.astype(F32)).astype(BF16)


def _mm_raw(a, b, kind, prec):
    nb = a.ndim - 2
    ca = a.ndim - 1 if kind[0] == "n" else a.ndim - 2
    cb = b.ndim - 2 if kind[1] == "n" else b.ndim - 1
    dims = (((ca,), (cb,)), (tuple(range(nb)),) * 2)
    dot = lambda p, q: lax.dot_general(p, q, dims, preferred_element_type=F32)
    ah, bh = a.astype(BF16), b.astype(BF16)
    if prec == ONE_PASS:
        return dot(ah, bh)
    if prec == EXACT_LHS:
        b1, b2, b3 = _split3(b)
        return dot(ah, b1) + (dot(ah, b2) + dot(ah, b3))
    if prec == EXACT_RHS:
        a1, a2, a3 = _split3(a)
        return dot(a1, bh) + (dot(a2, bh) + dot(a3, bh))
    al = (a - ah.astype(F32)).astype(BF16)
    bl = (b - bh.astype(F32)).astype(BF16)
    return dot(ah, bh) + (dot(ah, bl) + dot(al, bh))


@functools.partial(jax.custom_vjp, nondiff_argnums=(2, 3))
def _mm_vjp(a, b, kind, hi):
    return _mm_raw(a, b, kind, hi)


def _mm_vjp_fwd(a, b, kind, hi):
    return _mm_raw(a, b, kind, hi), (a, b)


def _mm_vjp_bwd(kind, hi, res, dc):
    a, b = res
    if hi in (EXACT_LHS, EXACT_RHS):
        assert kind == "nn"
        if hi == EXACT_LHS:
            return jnp.zeros_like(a), _mm_raw(a, dc, "tn", EXACT_LHS)
        return _mm_raw(dc, b, "nt", EXACT_RHS), jnp.zeros_like(b)
    if kind == "nn":
        return _mm_raw(dc, b, "nt", hi), _mm_raw(a, dc, "tn", hi)
    if kind == "nt":
        return _mm_raw(dc, b, "nn", hi), _mm_raw(dc, a, "tn", hi)
    return _mm_raw(b, dc, "nt", hi), _mm_raw(a, dc, "nn", hi)


_mm_vjp.defvjp(_mm_vjp_fwd, _mm_vjp_bwd)


def _lane_total(v):
    return jnp.broadcast_to(jnp.sum(v, axis=-1, keepdims=True), v.shape)


def _matmul(a, b, *, name, ta=False, tb=False, add=None, add_scale=1.0, tm=1024, tn=2048, tk=1024):
    m, k = (a.shape[1], a.shape[0]) if ta else a.shape
    n = b.shape[0] if tb else b.shape[1]
    tm, tn, tk = min(tm, m), min(tn, n), min(tk, k)
    tn = tn if n % tn == 0 else tn // 2
    assert m % tm == 0 and n % tn == 0 and k % tk == 0, (name, m, n, k)
    nk = k // tk
    dims = (((0 if ta else 1,), (1 if tb else 0,)), ((), ()))

    def body(*refs):
        a_ref, b_ref = refs[:2]
        o_ref = refs[-1]
        part = lax.dot_general(a_ref[...].astype(BF16), b_ref[...].astype(BF16), dims, preferred_element_type=F32)
        first = (lambda: part) if add is None else (lambda: part + add_scale * refs[2][...])
        if nk == 1:
            o_ref[...] = first()
        else:
            kk = pl.program_id(2)

            @pl.when(kk == 0)
            def _():
                o_ref[...] = first()

            @pl.when(kk > 0)
            def _():
                o_ref[...] += part

    a_spec = pl.BlockSpec((tk, tm), lambda i, j, kk: (kk, i)) if ta else pl.BlockSpec((tm, tk), lambda i, j, kk: (i, kk))
    b_spec = pl.BlockSpec((tn, tk), lambda i, j, kk: (j, kk)) if tb else pl.BlockSpec((tk, tn), lambda i, j, kk: (kk, j))
    o_spec = pl.BlockSpec((tm, tn), lambda i, j, kk: (i, j))
    in_specs = [a_spec, b_spec] + ([o_spec] if add is not None else [])
    args = (a, b) + ((add,) if add is not None else ())
    return pl.pallas_call(
        body, name=name, grid=(m // tm, n // tn, nk), in_specs=in_specs, out_specs=o_spec,
        out_shape=jax.ShapeDtypeStruct((m, n), F32),
        compiler_params=_params("parallel", "parallel", "arbitrary"))(*args)


HALO = SUBLANES


def _halo_specs(tt, width, col, nt):
    r = tt // HALO
    prev = pl.BlockSpec((HALO, width), lambda i: (jnp.maximum(i * r - 1, 0), col))
    nxt = pl.BlockSpec((HALO, width), lambda i: (jnp.minimum((i + 1) * r, nt * r - 1), col))
    return prev, nxt


def _shift_down(ext, k):
    return ext if k == 0 else pltpu.roll(ext, k, 0)


def _shift_up(ext, k):
    return ext if k == 0 else pltpu.roll(ext, ext.shape[0] - k, 0)


def _causal_conv(ext, w, taps):
    acc = None
    for j in range(taps):
        term = w[j:j + 1, :] * _shift_down(ext, taps - 1 - j)
        acc = term if acc is None else acc + term
    return acc[HALO:, :]


def _conv_a_fwd(proj_a, conv_w):
    t = proj_a.shape[0]
    tt = min(t, 256)
    nt = t // tt
    wdt = 1024

    def body(cur_ref, prev_ref, w_ref, y_ref):
        i = pl.program_id(0)
        cur = cur_ref[...]
        h, c, b, z = (cur[:, k * wdt:(k + 1) * wdt] for k in range(4))
        prev = prev_ref[...]
        u_prev = jnp.where(i > 0, prev[:, wdt:2 * wdt] * prev[:, 0:wdt], 0.0)
        ext = jnp.concatenate([u_prev, c * h], axis=0)
        conv = _causal_conv(ext, w_ref[...], 3)
        y_ref[...] = (b * conv * jax.nn.silu(z)).astype(BF16)

    prev_spec, _ = _halo_specs(tt, 4 * wdt, 0, nt)
    return pl.pallas_call(
        body, name="conv_a_fwd", grid=(nt,),
        in_specs=[pl.BlockSpec((tt, 4 * wdt), lambda i: (i, 0)), prev_spec, pl.BlockSpec((3, wdt), lambda i: (0, 0))],
        out_specs=pl.BlockSpec((tt, wdt), lambda i: (i, 0)),
        out_shape=jax.ShapeDtypeStruct((t, wdt), BF16), compiler_params=_params("parallel"))(proj_a, proj_a, conv_w)


def _conv_a_bwd(proj_a, conv_w, dy):
    t = proj_a.shape[0]
    tt = min(t, 256)
    nt = t // tt
    wdt = 1024

    def body(cur_ref, prev_ref, nxt_ref, w_ref, dy_ref, dyn_ref, d_ref, dw_ref):
        i = pl.program_id(0)
        w = w_ref[...]
        cur, prev, nxt = cur_ref[...], prev_ref[...], nxt_ref[...]
        split = lambda a: tuple(a[:, k * wdt:(k + 1) * wdt] for k in range(4))
        h, c, b, z = split(cur)
        hp, cp, _, _ = split(prev)
        hn, cn, bn, zn = split(nxt)
        u_prev = jnp.where(i > 0, cp * hp, 0.0)
        u_ext = jnp.concatenate([u_prev, c * h, cn * hn], axis=0)
        taps = [_shift_down(u_ext, 2 - j)[HALO:, :] for j in range(3)]
        conv = w[0:1, :] * taps[0] + w[1:2, :] * taps[1] + w[2:3, :] * taps[2]
        b_cn = jnp.concatenate([b, bn], axis=0)
        z_cn = jnp.concatenate([z, zn], axis=0)
        dy_cn = jnp.concatenate([dy_ref[...], jnp.where(i < nt - 1, dyn_ref[...], 0.0)], axis=0)
        sg = jax.nn.sigmoid(z_cn)
        silu = z_cn * sg
        d_conv = dy_cn * b_cn * silu
        db = (dy_cn * conv * silu)[:tt, :]
        dz = (dy_cn * b_cn * conv * (sg * (1.0 + z_cn * (1.0 - sg))))[:tt, :]
        du = None
        for j in range(3):
            term = w[j:j + 1, :] * _shift_up(d_conv, 2 - j)
            du = term if du is None else du + term
        du = du[:tt, :]
        d_ref[...] = jnp.concatenate([du * c, du * h, db, dz], axis=1).astype(BF16)

        @pl.when(i == 0)
        def _():
            dw_ref[...] = jnp.zeros_like(dw_ref)

        d_cur = d_conv[:tt, :]
        rows = [jnp.sum(d_cur * taps[j][:tt, :], axis=0, keepdims=True) for j in range(3)]
        dw_ref[0:3, :] += jnp.concatenate(rows, axis=0)

    prev_spec, nxt_spec = _halo_specs(tt, 4 * wdt, 0, nt)
    _, dyn_spec = _halo_specs(tt, wdt, 0, nt)
    return pl.pallas_call(
        body, name="conv_a_bwd", grid=(nt,),
        in_specs=[pl.BlockSpec((tt, 4 * wdt), lambda i: (i, 0)), prev_spec, nxt_spec,
                  pl.BlockSpec((3, wdt), lambda i: (0, 0)), pl.BlockSpec((tt, wdt), lambda i: (i, 0)), dyn_spec],
        out_specs=[pl.BlockSpec((tt, 4 * wdt), lambda i: (i, 0)), pl.BlockSpec((SUBLANES, wdt), lambda i: (0, 0))],
        out_shape=[jax.ShapeDtypeStruct((t, 4 * wdt), BF16), jax.ShapeDtypeStruct((SUBLANES, wdt), F32)],
        compiler_params=_params("arbitrary"))(proj_a, proj_a, proj_a, conv_w, dy, dy)


def _conv_b_fwd(proj_qkv, conv_w):
    t, width = proj_qkv.shape
    tt = min(t, 256)
    nt = t // tt
    wdt = 1024

    def body(cur_ref, prev_ref, w_ref, y_ref):
        i = pl.program_id(1)
        ext = jnp.concatenate([jnp.where(i > 0, prev_ref[...], 0.0), cur_ref[...]], axis=0)
        y_ref[...] = jax.nn.silu(_causal_conv(ext, w_ref[...], 4))

    r = tt // HALO
    return pl.pallas_call(
        body, name="conv_b_fwd", grid=(width // wdt, nt),
        in_specs=[pl.BlockSpec((tt, wdt), lambda j, i: (i, j)),
                  pl.BlockSpec((HALO, wdt), lambda j, i: (jnp.maximum(i * r - 1, 0), j)),
                  pl.BlockSpec((4, wdt), lambda j, i: (0, j))],
        out_specs=pl.BlockSpec((tt, wdt), lambda j, i: (i, j)),
        out_shape=jax.ShapeDtypeStruct((t, width), F32), compiler_params=_params("parallel", "parallel"))(
            proj_qkv, proj_qkv, conv_w)


def _conv_b_bwd(proj_qkv, conv_w, d_act, col, name):
    t = proj_qkv.shape[0]
    tt = min(t, 256)
    nt = t // tt
    wdt = 1024

    def body(cur_ref, prev_ref, nxt_ref, w_ref, da_ref, dan_ref, d_ref, dw_ref):
        i = pl.program_id(0)
        w = w_ref[...]
        u_ext = jnp.concatenate([jnp.where(i > 0, prev_ref[...], 0.0), cur_ref[...], nxt_ref[...]], axis=0)
        taps = [_shift_down(u_ext, 3 - j)[HALO:, :] for j in range(4)]
        conv = w[0:1, :] * taps[0] + w[1:2, :] * taps[1] + w[2:3, :] * taps[2] + w[3:4, :] * taps[3]
        da_cn = jnp.concatenate([da_ref[...], jnp.where(i < nt - 1, dan_ref[...], 0.0)], axis=0)
        sg = jax.nn.sigmoid(conv)
        d_conv = da_cn * (sg * (1.0 + conv * (1.0 - sg)))
        du = None
        for j in range(4):
            term = w[j:j + 1, :] * _shift_up(d_conv, 3 - j)
            du = term if du is None else du + term
        d_ref[...] = du[:tt, :].astype(BF16)

        @pl.when(i == 0)
        def _():
            dw_ref[...] = jnp.zeros_like(dw_ref)

        d_cur = d_conv[:tt, :]
        rows = [jnp.sum(d_cur * taps[j][:tt, :], axis=0, keepdims=True) for j in range(4)]
        dw_ref[0:4, :] += jnp.concatenate(rows, axis=0)

    prev_spec, nxt_spec = _halo_specs(tt, wdt, col, nt)
    _, dan_spec = _halo_specs(tt, wdt, 0, nt)
    return pl.pallas_call(
        body, name=name, grid=(nt,),
        in_specs=[pl.BlockSpec((tt, wdt), lambda i: (i, col)), prev_spec, nxt_spec,
                  pl.BlockSpec((4, wdt), lambda i: (0, col)), pl.BlockSpec((tt, wdt), lambda i: (i, 0)), dan_spec],
        out_specs=[pl.BlockSpec((tt, wdt), lambda i: (i, 0)), pl.BlockSpec((SUBLANES, wdt), lambda i: (0, 0))],
        out_shape=[jax.ShapeDtypeStruct((t, wdt), BF16), jax.ShapeDtypeStruct((SUBLANES, wdt), F32)],
        compiler_params=_params("arbitrary"))(proj_qkv, proj_qkv, proj_qkv, conv_w, d_act, d_act)


def _rms_gate(o, gn, z):
    on = o * lax.rsqrt(jnp.mean(o * o, axis=-1, keepdims=True) + NORM_EPS) * gn
    return on * jax.nn.silu(z)


GDN_PREP_ROWS = 1024


def _unit_lower_inverse(low):
    c = low.shape[-1]
    eye = lax.broadcasted_iota(jnp.int32, low.shape, low.ndim - 2) == lax.broadcasted_iota(jnp.int32, low.shape, low.ndim - 1)
    x = -low
    inv = eye.astype(F32) + x
    for _ in range(c.bit_length() - 2):
        x = _mm_raw(x, x, "nn", THREE_PASS)
        inv = inv + _mm_raw(inv, x, "nn", THREE_PASS)
    return inv


@jax.custom_vjp
def _known_inverse(low, inv):
    return inv


def _known_inverse_fwd(low, inv):
    return inv, inv


def _known_inverse_bwd(inv, d_inv):
    return -_mm_raw(_mm_raw(inv, d_inv, "tn", THREE_PASS), inv, "nt", THREE_PASS), jnp.zeros_like(inv)


_known_inverse.defvjp(_known_inverse_fwd, _known_inverse_bwd)


def _gdn_prep(mm, qa, ka, va, braw, araw, alog, dtb, inv_kept=None):
    n, c, _ = qa.shape
    q = qa * lax.rsqrt(jnp.sum(qa * qa, axis=-1, keepdims=True) + 1e-6) * (HEAD_DIM ** -0.5)
    k = ka * lax.rsqrt(jnp.sum(ka * ka, axis=-1, keepdims=True) + 1e-6)
    beta = jax.nn.sigmoid(braw)
    g = -jnp.exp(alog) * jax.nn.softplus(araw + dtb)
    ri = lax.broadcasted_iota(jnp.int32, (n, c, c), 1)
    ci = lax.broadcasted_iota(jnp.int32, (n, c, c), 2)
    incl, strict, eye = ri >= ci, ri > ci, ri == ci
    gc = mm(incl.astype(F32), g, "nn", EXACT_LHS)
    gc_i = gc[:, :, :c]
    gc_j = mm(jnp.ones((n, c, c), F32), jnp.where(eye, gc_i, 0.0), "nn", EXACT_LHS)
    decay = jnp.where(incl, jnp.exp(jnp.where(incl, gc_i - gc_j, 0.0)), 0.0)
    kb = k * beta
    low = jnp.where(strict, mm(kb, k, "nt", ONE_PASS) * decay, 0.0)
    inv = _unit_lower_inverse(low) if inv_kept is None else _known_inverse(low, inv_kept)
    egc = jnp.exp(gc)
    u = mm(inv, va * beta, "nn", THREE_PASS)
    w = mm(inv, kb * egc, "nn", THREE_PASS)
    attn = jnp.where(incl, mm(q, k, "nt", ONE_PASS) * decay, 0.0)
    g_last = jnp.sum(g, axis=1, keepdims=True)
    outs = (u, w, q * egc, k * jnp.exp(g_last - gc), attn, jnp.exp(g_last))
    return outs + (inv,) if inv_kept is None else outs


def _gdn_scan(mm, u, w, qd, kd, attn, egl, z, gn, state):
    v_new = u - mm(w, state, "nn", ONE_PASS)
    o = mm(qd, state, "nn", ONE_PASS) + mm(attn, v_new, "nn", ONE_PASS)
    new_state = state * egl + mm(kd, v_new, "tn", ONE_PASS)
    return _rms_gate(o, gn, z), new_state


def _chunks(ref_value, n, c):
    return ref_value.reshape(n, c, ref_value.shape[-1])


def _by_head(ref, rows, heads):
    return jnp.stack([ref[rows, pl.ds(h * HEAD_DIM, HEAD_DIM)] for h in range(heads)])


def _store_heads(ref, rows, value):
    for h in range(value.shape[0]):
        ref[rows, pl.ds(h * HEAD_DIM, HEAD_DIM)] = value[h]


def _gdn_prep_specs(tb):
    col = lambda off: pl.BlockSpec((tb, HEAD_DIM), lambda h, i: (i, off + h))
    rep = pl.BlockSpec((1, tb, LANES), lambda h, i: (h, i, 0))
    par = pl.BlockSpec((1, SUBLANES, LANES), lambda h, i: (h, 0, 0))
    att = pl.BlockSpec((1, tb, GDN_CHUNK), lambda h, i: (h, i, 0))
    egl = pl.BlockSpec((1, tb // GDN_CHUNK, SUBLANES, LANES), lambda h, i: (h, i, 0, 0))
    return col, rep, par, att, egl


def _head_columns(tail):
    lane = lax.broadcasted_iota(jnp.int32, tail.shape, 1)
    h = pl.program_id(0)
    return _lane_total(jnp.where(lane == h, tail, 0.0)), _lane_total(jnp.where(lane == h + GDN_HEADS, tail, 0.0))


def _gdn_prep_fwd(qkv_act, tail, alog, dtb):
    t = qkv_act.shape[0]
    tb = min(t, 2 * GDN_PREP_ROWS)
    nt, nc = t // tb, tb // GDN_CHUNK
    width = GDN_HEADS * HEAD_DIM

    def body(q_ref, k_ref, v_ref, tail_ref, al_ref, dt_ref, u_ref, w_ref, qd_ref, kd_ref, at_ref, eg_ref, inv_ref):
        ch = lambda r: _chunks(r, nc, GDN_CHUNK)
        braw, araw = _head_columns(tail_ref[...])
        u, w, qd, kd, attn, egl, inv = _gdn_prep(_mm_raw, ch(q_ref[...]), ch(k_ref[...]), ch(v_ref[...]), ch(braw),
                                                 ch(araw), al_ref[0, 0:1, :], dt_ref[0, 0:1, :])
        u_ref[...] = u.reshape(tb, HEAD_DIM)
        w_ref[...] = w.reshape(tb, HEAD_DIM).astype(BF16)
        qd_ref[...] = qd.reshape(tb, HEAD_DIM).astype(BF16)
        kd_ref[...] = kd.reshape(tb, HEAD_DIM).astype(BF16)
        at_ref[0] = attn.reshape(tb, GDN_CHUNK).astype(BF16)
        eg_ref[0] = jnp.broadcast_to(egl, (nc, SUBLANES, LANES))
        inv_ref[0] = inv.reshape(tb, GDN_CHUNK)

    col, rep, par, att, egl = _gdn_prep_specs(tb)
    h = GDN_HEADS
    return pl.pallas_call(
        body, name="gdn_prep_fwd", grid=(h, nt),
        in_specs=[col(0), col(h), col(2 * h), pl.BlockSpec((tb, LANES), lambda h_, i: (i, 0)), par, par],
        out_specs=[col(0), col(0), col(0), col(0), att, egl, att],
        out_shape=[jax.ShapeDtypeStruct((t, width), F32)] + [jax.ShapeDtypeStruct((t, width), BF16)] * 3
        + [jax.ShapeDtypeStruct((h, t, GDN_CHUNK), BF16), jax.ShapeDtypeStruct((h, t // GDN_CHUNK, SUBLANES, LANES), F32),
           jax.ShapeDtypeStruct((h, t, GDN_CHUNK), F32)],
        compiler_params=_params("parallel", "parallel"))(qkv_act, qkv_act, qkv_act, tail, alog, dtb)


def _gdn_prep_bwd(qkv_act, tail, alog, dtb, inv, du, dw, dqd, dkd, dattn, degl):
    t = qkv_act.shape[0]
    tb = min(t, 2 * GDN_PREP_ROWS)
    nt, nc = t // tb, tb // GDN_CHUNK
    width = GDN_HEADS * HEAD_DIM

    def body(q_ref, k_ref, v_ref, tail_ref, al_ref, dt_ref, inv_ref, du_ref, dw_ref, dqd_ref, dkd_ref, dat_ref,
             deg_ref, dq_ref, dk_ref, dv_ref, dbr_ref, dar_ref, dal_ref, ddt_ref):
        @pl.when(pl.program_id(1) == 0)
        def _():
            dal_ref[...] = jnp.zeros_like(dal_ref)
            ddt_ref[...] = jnp.zeros_like(ddt_ref)

        ch = lambda r: _chunks(r, nc, GDN_CHUNK)
        braw, araw = _head_columns(tail_ref[...])
        _, vjp = jax.vjp(functools.partial(_gdn_prep, _mm_vjp, inv_kept=ch(inv_ref[0])), ch(q_ref[...]), ch(k_ref[...]),
                         ch(v_ref[...]), ch(braw), ch(araw), al_ref[0, 0:1, :], dt_ref[0, 0:1, :])
        dq, dk, dv, dbr, dar, dal, ddt = vjp((ch(du_ref[...]), ch(dw_ref[...]), ch(dqd_ref[...]), ch(dkd_ref[...]),
                                              ch(dat_ref[0]), deg_ref[0][:, 0:1, :]))
        dq_ref[...] = dq.reshape(tb, HEAD_DIM)
        dk_ref[...] = dk.reshape(tb, HEAD_DIM)
        dv_ref[...] = dv.reshape(tb, HEAD_DIM)
        dbr_ref[0] = _lane_total(dbr.reshape(tb, LANES))
        dar_ref[0] = _lane_total(dar.reshape(tb, LANES))
        dal_ref[0, 0:1, :] += _lane_total(dal)
        ddt_ref[0, 0:1, :] += _lane_total(ddt)

    col, rep, par, att, egl = _gdn_prep_specs(tb)
    h = GDN_HEADS
    return pl.pallas_call(
        body, name="gdn_prep_bwd", grid=(h, nt),
        in_specs=[col(0), col(h), col(2 * h), pl.BlockSpec((tb, LANES), lambda h_, i: (i, 0)), par, par, att,
                  col(0), col(0), col(0), col(0), att, egl],
        out_specs=[col(0), col(0), col(0), rep, rep, par, par],
        out_shape=[jax.ShapeDtypeStruct((t, width), F32)] * 3 + [jax.ShapeDtypeStruct((h, t, LANES), F32)] * 2
        + [jax.ShapeDtypeStruct((h, SUBLANES, LANES), F32)] * 2,
        compiler_params=_params("parallel", "arbitrary"))(qkv_act, qkv_act, qkv_act, tail, alog, dtb, inv,
                                                         du, dw, dqd, dkd, dattn, degl)


def _scan_specs(tb, heads, chunk, rev, nt):
    ti = (lambda i: nt - 1 - i) if rev else (lambda i: i)
    row = pl.BlockSpec((tb, heads * HEAD_DIM), lambda i: (ti(i), 0))
    att = pl.BlockSpec((heads, tb, chunk), lambda i: (0, ti(i), 0))
    egl = pl.BlockSpec((heads, tb // chunk, SUBLANES, LANES), lambda i: (0, ti(i), 0, 0))
    hist = pl.BlockSpec((heads, tb // chunk, HEAD_DIM, HEAD_DIM), lambda i: (0, ti(i), 0, 0))
    gn = pl.BlockSpec((SUBLANES, LANES), lambda i: (0, 0))
    return row, att, egl, hist, gn


def _gdn_scan_fwd(u, w, qd, kd, attn, egl, zb, gn):
    t = u.shape[0]
    tb = min(t, 256)
    nt, nc = t // tb, tb // GDN_CHUNK
    nh = GDN_HEADS

    def body(u_ref, w_ref, qd_ref, kd_ref, at_ref, eg_ref, z_ref, gn_ref, y_ref, hist_ref, s_ref):
        @pl.when(pl.program_id(0) == 0)
        def _():
            s_ref[...] = jnp.zeros_like(s_ref)

        g = gn_ref[0:1, :]
        state = s_ref[...]
        for c in range(nc):
            rows = pl.ds(c * GDN_CHUNK, GDN_CHUNK)
            heads = lambda r: _by_head(r, rows, nh)
            hist_ref[:, c] = state
            y, state = _gdn_scan(_mm_raw, heads(u_ref), heads(w_ref), heads(qd_ref), heads(kd_ref), at_ref[:, rows, :],
                                 eg_ref[:, c, 0:1, :], heads(z_ref), g, state)
            _store_heads(y_ref, rows, y.astype(BF16))
        s_ref[...] = state

    row, att, egs, hist, gns = _scan_specs(tb, nh, GDN_CHUNK, False, nt)
    return pl.pallas_call(
        body, name="gdn_scan_fwd", grid=(nt,), in_specs=[row, row, row, row, att, egs, row, gns], out_specs=[row, hist],
        out_shape=[jax.ShapeDtypeStruct((t, nh * HEAD_DIM), BF16),
                   jax.ShapeDtypeStruct((nh, t // GDN_CHUNK, HEAD_DIM, HEAD_DIM), F32)],
        scratch_shapes=[pltpu.VMEM((nh, HEAD_DIM, HEAD_DIM), F32)],
        compiler_params=_params("arbitrary"))(u, w, qd, kd, attn, egl, zb, gn)


def _gdn_scan_bwd(u, w, qd, kd, attn, egl, zb, gn, hist, dy):
    t = u.shape[0]
    tb = min(t, 256)
    nt, nc = t // tb, tb // GDN_CHUNK
    nh = GDN_HEADS

    def body(u_ref, w_ref, qd_ref, kd_ref, at_ref, eg_ref, z_ref, gn_ref, hist_ref, dy_ref,
             du_ref, dw_ref, dqd_ref, dkd_ref, dat_ref, deg_ref, dz_ref, dgn_ref, ds_ref):
        @pl.when(pl.program_id(0) == 0)
        def _():
            ds_ref[...] = jnp.zeros_like(ds_ref)
            dgn_ref[...] = jnp.zeros_like(dgn_ref)

        g = gn_ref[0:1, :]
        d_state = ds_ref[...]
        for c in reversed(range(nc)):
            rows = pl.ds(c * GDN_CHUNK, GDN_CHUNK)
            heads = lambda r: _by_head(r, rows, nh).astype(F32)
            _, vjp = jax.vjp(functools.partial(_gdn_scan, _mm_vjp), heads(u_ref), heads(w_ref), heads(qd_ref),
                             heads(kd_ref), at_ref[:, rows, :].astype(F32), eg_ref[:, c, 0:1, :], heads(z_ref), g,
                             hist_ref[:, c])
            du, dw, dqd, dkd, dat, deg, dz, dgn, d_state = vjp((heads(dy_ref), d_state))
            _store_heads(du_ref, rows, du)
            _store_heads(dw_ref, rows, dw)
            _store_heads(dqd_ref, rows, dqd)
            _store_heads(dkd_ref, rows, dkd)
            dat_ref[:, rows, :] = dat
            deg_ref[:, c] = jnp.broadcast_to(deg, (nh, SUBLANES, LANES))
            _store_heads(dz_ref, rows, dz.astype(BF16))
            dgn_ref[0:1, :] += dgn
        ds_ref[...] = d_state

    row, att, egs, hists, gns = _scan_specs(tb, nh, GDN_CHUNK, True, nt)
    wide = jax.ShapeDtypeStruct((t, nh * HEAD_DIM), F32)
    return pl.pallas_call(
        body, name="gdn_scan_bwd", grid=(nt,),
        in_specs=[row, row, row, row, att, egs, row, gns, hists, row],
        out_specs=[row, row, row, row, att, egs, row, gns],
        out_shape=[wide] * 4 + [jax.ShapeDtypeStruct((nh, t, GDN_CHUNK), F32),
                                jax.ShapeDtypeStruct((nh, t // GDN_CHUNK, SUBLANES, LANES), F32),
                                jax.ShapeDtypeStruct((t, nh * HEAD_DIM), BF16),
                                jax.ShapeDtypeStruct((SUBLANES, LANES), F32)],
        scratch_shapes=[pltpu.VMEM((nh, HEAD_DIM, HEAD_DIM), F32)],
        compiler_params=_params("arbitrary"))(u, w, qd, kd, attn, egl, zb, gn, hist, dy)


def _hgrn_prep(mm, qr, fr, lbl):
    n, c, _ = qr.shape
    lb = jax.nn.sigmoid(lbl[1:2, :] - lbl[0:1, :])
    f = lb + (1.0 - lb) * jax.nn.sigmoid(fr)
    q = jax.nn.silu(qr)
    k = 1.0 - f
    logf = jnp.log(f)
    ri = lax.broadcasted_iota(jnp.int32, (n, c, c), 1)
    ci = lax.broadcasted_iota(jnp.int32, (n, c, c), 2)
    b = mm((ri >= ci).astype(F32), logf, "nn", EXACT_LHS)
    attn = _hgrn_attn(mm, q, k, b)
    b_last = jnp.sum(logf, axis=1, keepdims=True)
    return q * jnp.exp(b), k * jnp.exp(b_last - b), attn, jnp.exp(b_last)


HGRN_SUB = 8
HGRN_PREP_ROWS = 2048


@functools.partial(jax.custom_vjp, nondiff_argnums=(1,))
def _roll_rows(x, shift):
    return pltpu.roll(x, shift, x.ndim - 2)


def _roll_rows_fwd(x, shift):
    return _roll_rows(x, shift), None


def _roll_rows_bwd(shift, _, d):
    return (pltpu.roll(d, d.shape[-2] - shift, d.ndim - 2),)


_roll_rows.defvjp(_roll_rows_fwd, _roll_rows_bwd)


@jax.custom_vjp
def _exp_clamped(v):
    return jnp.exp(jnp.minimum(v, 0.0))


def _exp_clamped_fwd(v):
    out = jnp.exp(jnp.minimum(v, 0.0))
    return out, out


def _exp_clamped_bwd(out, d):
    return (d * out,)


_exp_clamped.defvjp(_exp_clamped_fwd, _exp_clamped_bwd)


def _hgrn_attn(mm, q, k, b):
    n, c, d = q.shape
    sb = HGRN_SUB
    sub = lambda a: a.reshape(n * c // sb, sb, d)
    qs, ks, bs = sub(q), sub(k), sub(b)
    row = lax.broadcasted_iota(jnp.int32, (n, c, c), 1)
    col = lax.broadcasted_iota(jnp.int32, (n, c, c), 2)
    same_block = (row & -sb) == (col & -sb)
    attn = None
    for delta in range(sb):
        if delta == 0:
            prod = qs * ks
        else:
            prod = qs * _roll_rows(ks, delta) * _exp_clamped(bs - _roll_rows(bs, delta))
        sums = jnp.sum(prod, axis=-1, keepdims=True).reshape(n, c, 1)
        term = jnp.where(same_block & (row - col == delta), sums, 0.0)
        attn = term if attn is None else attn + term
    far = [jnp.zeros((n, sb, c), F32)]
    for i in range(1, c // sb):
        r0 = i * sb
        bi = b[:, r0:r0 + sb, :]
        ref = bi[:, 0:1, :]
        before = jnp.concatenate([k[:, :r0, :] * jnp.exp(ref - b[:, :r0, :]), jnp.zeros((n, c - r0, d), F32)], axis=1)
        far.append(mm(q[:, r0:r0 + sb, :] * jnp.exp(bi - ref), before, "nt", ONE_PASS))
    return attn + jnp.concatenate(far, axis=1)


def _hgrn_scan(mm, qe, kd, attn, ebl, iv, z, gn, state):
    o = mm(qe, state, "nt", ONE_PASS) + mm(attn, iv, "nn", ONE_PASS)
    new_state = state * ebl + mm(iv, kd, "tn", ONE_PASS)
    return _rms_gate(o, gn, z), new_state


def _hgrn_prep_specs(tb):
    col = pl.BlockSpec((tb, HEAD_DIM), lambda h, i: (i, h))
    lbs = pl.BlockSpec((2, HEAD_DIM), lambda h, i: (0, h))
    att = pl.BlockSpec((1, tb, HGRN_CHUNK), lambda h, i: (h, i, 0))
    ebl = pl.BlockSpec((1, tb // HGRN_CHUNK, SUBLANES, LANES), lambda h, i: (h, i, 0, 0))
    return col, lbs, att, ebl


def _hgrn_prep_fwd(qr, fr, lower_bounds):
    t = qr.shape[0]
    tb = min(t, HGRN_PREP_ROWS)
    nt, nc = t // tb, tb // HGRN_CHUNK
    hh = HGRN_HEADS

    def body(q_ref, f_ref, lb_ref, qe_ref, kd_ref, at_ref, eb_ref):
        ch = lambda r: _chunks(r, nc, HGRN_CHUNK)
        qe, kd, attn, ebl = _hgrn_prep(_mm_raw, ch(q_ref[...]), ch(f_ref[...]), lb_ref[...])
        qe_ref[...] = qe.reshape(tb, HEAD_DIM).astype(BF16)
        kd_ref[...] = kd.reshape(tb, HEAD_DIM).astype(BF16)
        at_ref[0] = attn.reshape(tb, HGRN_CHUNK).astype(BF16)
        eb_ref[0] = jnp.broadcast_to(ebl, (nc, SUBLANES, LANES))

    col, lbs, att, ebs = _hgrn_prep_specs(tb)
    return pl.pallas_call(
        body, name="hgrn_prep_fwd", grid=(hh, nt), in_specs=[col, col, lbs], out_specs=[col, col, att, ebs],
        out_shape=[jax.ShapeDtypeStruct((t, HGRN_WIDTH), BF16)] * 2
        + [jax.ShapeDtypeStruct((hh, t, HGRN_CHUNK), BF16), jax.ShapeDtypeStruct((hh, t // HGRN_CHUNK, SUBLANES, LANES), F32)],
        compiler_params=_params("parallel", "parallel"))(qr, fr, lower_bounds)


def _hgrn_prep_bwd(qr, fr, lower_bounds, dqe, dkd, dattn, debl):
    t = qr.shape[0]
    tb = min(t, HGRN_PREP_ROWS)
    nt, nc = t // tb, tb // HGRN_CHUNK
    hh = HGRN_HEADS

    def body(q_ref, f_ref, lb_ref, dqe_ref, dkd_ref, dat_ref, deb_ref, dq_ref, df_ref, dlb_ref):
        @pl.when(pl.program_id(1) == 0)
        def _():
            dlb_ref[...] = jnp.zeros_like(dlb_ref)

        ch = lambda r: _chunks(r, nc, HGRN_CHUNK)
        _, vjp = jax.vjp(functools.partial(_hgrn_prep, _mm_vjp), ch(q_ref[...]), ch(f_ref[...]), lb_ref[...])
        dq, df, dlb = vjp((ch(dqe_ref[...]), ch(dkd_ref[...]), ch(dat_ref[0]), deb_ref[0][:, 0:1, :]))
        dq_ref[...] = dq.reshape(tb, HEAD_DIM).astype(BF16)
        df_ref[...] = df.reshape(tb, HEAD_DIM).astype(BF16)
        dlb_ref[...] += dlb

    col, lbs, att, ebs = _hgrn_prep_specs(tb)
    return pl.pallas_call(
        body, name="hgrn_prep_bwd", grid=(hh, nt), in_specs=[col, col, lbs, col, col, att, ebs],
        out_specs=[col, col, lbs],
        out_shape=[jax.ShapeDtypeStruct((t, HGRN_WIDTH), BF16)] * 2 + [jax.ShapeDtypeStruct((2, HGRN_WIDTH), F32)],
        compiler_params=_params("parallel", "arbitrary"))(qr, fr, lower_bounds, dqe, dkd, dattn, debl)


def _hgrn_scan_fwd(qe, kd, attn, ebl, iv, z, gn):
    t = qe.shape[0]
    tb = min(t, 128)
    nt, nc = t // tb, tb // HGRN_CHUNK
    hh = HGRN_HEADS

    def body(qe_ref, kd_ref, at_ref, eb_ref, i_ref, z_ref, gn_ref, y_ref, hist_ref, s_ref):
        @pl.when(pl.program_id(0) == 0)
        def _():
            s_ref[...] = jnp.zeros_like(s_ref)

        g = gn_ref[0:1, :]
        state = s_ref[...]
        for c in range(nc):
            rows = pl.ds(c * HGRN_CHUNK, HGRN_CHUNK)
            heads = lambda r: _by_head(r, rows, hh)
            hist_ref[:, c] = state
            y, state = _hgrn_scan(_mm_raw, heads(qe_ref), heads(kd_ref), at_ref[:, rows, :], eb_ref[:, c, 0:1, :],
                                  heads(i_ref), heads(z_ref), g, state)
            _store_heads(y_ref, rows, y.astype(BF16))
        s_ref[...] = state

    row, att, ebs, hist, gns = _scan_specs(tb, hh, HGRN_CHUNK, False, nt)
    return pl.pallas_call(
        body, name="hgrn_scan_fwd", grid=(nt,), in_specs=[row, row, att, ebs, row, row, gns], out_specs=[row, hist],
        out_shape=[jax.ShapeDtypeStruct((t, HGRN_WIDTH), BF16),
                   jax.ShapeDtypeStruct((hh, t // HGRN_CHUNK, HEAD_DIM, HEAD_DIM), F32)],
        scratch_shapes=[pltpu.VMEM((hh, HEAD_DIM, HEAD_DIM), F32)],
        compiler_params=_params("arbitrary"))(qe, kd, attn, ebl, iv, z, gn)


def _hgrn_scan_bwd(qe, kd, attn, ebl, iv, z, gn, hist, dy):
    t = qe.shape[0]
    tb = min(t, 128)
    nt, nc = t // tb, tb // HGRN_CHUNK
    hh = HGRN_HEADS

    def body(qe_ref, kd_ref, at_ref, eb_ref, i_ref, z_ref, gn_ref, hist_ref, dy_ref,
             dqe_ref, dkd_ref, dat_ref, deb_ref, di_ref, dz_ref, dgn_ref, ds_ref):
        @pl.when(pl.program_id(0) == 0)
        def _():
            ds_ref[...] = jnp.zeros_like(ds_ref)
            dgn_ref[...] = jnp.zeros_like(dgn_ref)

        g = gn_ref[0:1, :]
        d_state = ds_ref[...]
        for c in reversed(range(nc)):
            rows = pl.ds(c * HGRN_CHUNK, HGRN_CHUNK)
            heads = lambda r: _by_head(r, rows, hh).astype(F32)
            _, vjp = jax.vjp(functools.partial(_hgrn_scan, _mm_vjp), heads(qe_ref), heads(kd_ref),
                             at_ref[:, rows, :].astype(F32), eb_ref[:, c, 0:1, :], heads(i_ref), heads(z_ref), g,
                             hist_ref[:, c])
            dqe, dkd, dat, deb, di, dz, dgn, d_state = vjp((heads(dy_ref), d_state))
            _store_heads(dqe_ref, rows, dqe)
            _store_heads(dkd_ref, rows, dkd)
            dat_ref[:, rows, :] = dat
            deb_ref[:, c] = jnp.broadcast_to(deb, (hh, SUBLANES, LANES))
            _store_heads(di_ref, rows, di.astype(BF16))
            _store_heads(dz_ref, rows, dz.astype(BF16))
            dgn_ref[0:1, :] += dgn
        ds_ref[...] = d_state

    row, att, ebs, hists, gns = _scan_specs(tb, hh, HGRN_CHUNK, True, nt)
    wide = lambda dt: jax.ShapeDtypeStruct((t, HGRN_WIDTH), dt)
    return pl.pallas_call(
        body, name="hgrn_scan_bwd", grid=(nt,),
        in_specs=[row, row, att, ebs, row, row, gns, hists, row],
        out_specs=[row, row, att, ebs, row, row, gns],
        out_shape=[wide(F32), wide(F32), jax.ShapeDtypeStruct((hh, t, HGRN_CHUNK), F32),
                   jax.ShapeDtypeStruct((hh, t // HGRN_CHUNK, SUBLANES, LANES), F32), wide(BF16), wide(BF16),
                   jax.ShapeDtypeStruct((SUBLANES, LANES), F32)],
        scratch_shapes=[pltpu.VMEM((hh, HEAD_DIM, HEAD_DIM), F32)],
        compiler_params=_params("arbitrary"))(qe, kd, attn, ebl, iv, z, gn, hist, dy)


def _layer_norm(pre, g, b):
    mu = jnp.mean(pre, axis=-1, keepdims=True)
    d = pre - mu
    var = jnp.mean(d * d, axis=-1, keepdims=True)
    return d * lax.rsqrt(var + NORM_EPS) * g + b


def _lnpl_fwd(xin, s, p, wg, wpl, ln_g, ln_b):
    t = xin.shape[0]
    tt = min(t, 512)

    def body(x_ref, s_ref, p_ref, wg_ref, wpl_ref, g_ref, b_ref, o_ref, ob_ref):
        xn = _layer_norm(DEEPNORM_ALPHA * x_ref[...] + s_ref[...], g_ref[...], b_ref[...])
        gate = jax.nn.sigmoid(_mm_raw(xn, wg_ref[...], "nn", ONE_PASS))
        out = xn + _mm_raw(p_ref[...], wpl_ref[...], "nn", ONE_PASS) * gate
        o_ref[...] = out
        ob_ref[...] = out.astype(BF16)

    row = lambda w: pl.BlockSpec((tt, w), lambda i: (i, 0))
    full = lambda a: pl.BlockSpec(a.shape, lambda i: (0, 0))
    return pl.pallas_call(
        body, name="lnpl_fwd", grid=(t // tt,),
        in_specs=[row(D_MODEL), row(D_MODEL), row(PL_DIM), full(wg), full(wpl), full(ln_g), full(ln_b)],
        out_specs=[row(D_MODEL), row(D_MODEL)],
        out_shape=[jax.ShapeDtypeStruct((t, D_MODEL), F32), jax.ShapeDtypeStruct((t, D_MODEL), BF16)],
        compiler_params=_params("parallel"))(xin, s, p, wg, wpl, ln_g, ln_b)


def _lnpl_bwd(xin, s, p, wg, wpl, ln_g, ln_b, upstream, last, name):
    t = xin.shape[0]
    tt = min(t, 512)

    def body(x_ref, s_ref, p_ref, wg_ref, wpl_ref, g_ref, b_ref, up_ref,
             dpre_ref, dwg_ref, dwpl_ref, dg_ref, db_ref, loss_ref):
        @pl.when(pl.program_id(0) == 0)
        def _():
            for r in (dwg_ref, dwpl_ref, dg_ref, db_ref, loss_ref):
                r[...] = jnp.zeros_like(r)

        pre = DEEPNORM_ALPHA * x_ref[...] + s_ref[...]
        xn, ln_vjp = jax.vjp(_layer_norm, pre, g_ref[...], b_ref[...])
        gate = jax.nn.sigmoid(_mm_raw(xn, wg_ref[...], "nn", ONE_PASS))
        plv = _mm_raw(p_ref[...], wpl_ref[...], "nn", ONE_PASS)
        if last:
            err = xn + plv * gate - up_ref[...]
            dout = err * (1.0 / D_MODEL)
            tot = jnp.sum(jnp.sum(err * err, axis=1, keepdims=True), axis=0, keepdims=True) * (0.5 / D_MODEL)
            loss_ref[...] += jnp.broadcast_to(tot, loss_ref.shape)
        else:
            dout = up_ref[...]
        dplv = dout * gate
        dlogits = dout * plv * gate * (1.0 - gate)
        dwg_ref[...] += _mm_raw(xn, dlogits, "tn", ONE_PASS)
        dwpl_ref[...] += _mm_raw(p_ref[...], dplv, "tn", ONE_PASS)
        dxn = dout + _mm_raw(dlogits, wg_ref[...], "nt", ONE_PASS)
        dpre, dg, db = ln_vjp(dxn)
        dpre_ref[...] = dpre
        dg_ref[...] += dg
        db_ref[...] += db

    row = lambda w: pl.BlockSpec((tt, w), lambda i: (i, 0))
    full = lambda shape: pl.BlockSpec(shape, lambda i: (0, 0))
    return pl.pallas_call(
        body, name=name, grid=(t // tt,),
        in_specs=[row(D_MODEL), row(D_MODEL), row(PL_DIM), full(wg.shape), full(wpl.shape), full(ln_g.shape),
                  full(ln_b.shape), row(D_MODEL)],
        out_specs=[row(D_MODEL), full(wg.shape), full(wpl.shape), full(ln_g.shape), full(ln_b.shape),
                   full((SUBLANES, LANES))],
        out_shape=[jax.ShapeDtypeStruct((t, D_MODEL), F32), jax.ShapeDtypeStruct(wg.shape, F32),
                   jax.ShapeDtypeStruct(wpl.shape, F32), jax.ShapeDtypeStruct(ln_g.shape, F32),
                   jax.ShapeDtypeStruct(ln_b.shape, F32), jax.ShapeDtypeStruct((SUBLANES, LANES), F32)],
        compiler_params=_params("arbitrary"))(xin, s, p, wg, wpl, ln_g, ln_b, upstream)


def _pack_tail(dbr, dar):
    nh, t, _ = dbr.shape
    tt = min(t, 512)

    def body(b_ref, a_ref, o_ref):
        lane = lax.broadcasted_iota(jnp.int32, (tt, LANES), 1)
        acc = jnp.zeros((tt, LANES), F32)
        for h in range(nh):
            acc = jnp.where(lane == h, b_ref[h], acc)
            acc = jnp.where(lane == nh + h, a_ref[h], acc)
        o_ref[...] = acc.astype(BF16)

    spec = pl.BlockSpec((nh, tt, LANES), lambda i: (0, i, 0))
    return pl.pallas_call(
        body, name="pack_tail", grid=(t // tt,), in_specs=[spec, spec], out_specs=pl.BlockSpec((tt, LANES), lambda i: (i, 0)),
        out_shape=jax.ShapeDtypeStruct((t, LANES), BF16), compiler_params=_params("parallel"))(dbr, dar)


def _rep_rows(v):
    return jnp.broadcast_to(v.reshape(1, LANES), (SUBLANES, LANES))


def _rep_heads(v):
    return jnp.broadcast_to(v.reshape(-1, 1, 1), (v.shape[0], SUBLANES, LANES))


def _col_range(stacked, lo, hi):
    c = stacked.shape[2]
    parts = [stacked[s, :, max(lo, s * c) - s * c:min(hi, (s + 1) * c) - s * c]
             for s in range(4) if max(lo, s * c) < min(hi, (s + 1) * c)]
    return parts[0] if len(parts) == 1 else jnp.concatenate(parts, axis=1)


def _col_shards(pieces, c):
    shards, offs, o = [], [], 0
    for pc in pieces:
        offs.append(o)
        o += pc.shape[1]
    for s in range(4):
        lo, hi = s * c, (s + 1) * c
        parts = [pc[:, max(lo, o) - o:min(hi, o + pc.shape[1]) - o] for pc, o in zip(pieces, offs)
                 if max(lo, o) < min(hi, o + pc.shape[1])]
        shards.append(parts[0] if len(parts) == 1 else jnp.concatenate(parts, axis=1))
    return jnp.stack(shards)


def _local_step(x, p, target, w, late_weights, early_grads_ready, early_grads_swapped, last_grad_ready, start_token):
    a = DEEPNORM_ALPHA
    nh = GDN_HEADS
    xb = (x + start_token).astype(BF16)
    wie = w["w_in_even"]
    w_a, w_qkv, w_zb = _col_range(wie, 0, 4096), _col_range(wie, 4096, 7168), _col_range(wie, 7168, 8192)
    w_tail = jnp.pad(_col_range(wie, 8192, 8192 + 2 * nh), ((0, 0), (0, LANES - 2 * nh)))
    conv_a_w, conv_b_w = w["conv_a_w"], w["conv_b_w"]
    ln_g0, ln_b0, ln_g1, ln_b1 = (v.reshape(1, D_MODEL) for v in (w["ln_g"][0], w["ln_b"][0], w["ln_g"][1], w["ln_b"][1]))
    alog, dtb = _rep_heads(w["a_log"].reshape(nh)), _rep_heads(w["dt_bias"].reshape(nh))
    gdn_g, hgrn_g = _rep_rows(w["gdn_norm_g"]), _rep_rows(w["hgrn_norm_g"])

    proj_a = _matmul(xb, w_a, name="fwd_proj_a")
    proj_qkv = _matmul(xb, w_qkv, name="fwd_proj_qkv")
    proj_zb = _matmul(xb, w_zb, name="fwd_proj_zb")
    proj_tail = _matmul(xb, w_tail, name="fwd_proj_tail")
    y_a = _conv_a_fwd(proj_a, conv_a_w)
    qkv_act = _conv_b_fwd(proj_qkv, conv_b_w)
    *gdn_pre, gdn_inv = _gdn_prep_fwd(qkv_act, proj_tail, alog, dtb)
    y_b, gdn_hist = _gdn_scan_fwd(*gdn_pre, proj_zb, gdn_g)
    w = {**w, **late_weights(y_b)}
    woe, wio, woo = w["w_out_even"], w["w_in_odd"], w["w_out_odd"]
    s0 = _matmul(y_b, woe[1024:], name="fwd_out_even_b", add=_matmul(y_a, woe[:1024], name="fwd_out_even_a"))
    x1, x1b = _lnpl_fwd(x, s0, p[0], w["w_pl_gate"][0], w["w_pl"][0], ln_g0, ln_b0)
    proj_o = [_matmul(x1b, wio[j], name=f"fwd_proj_odd{j}") for j in range(4)]
    hgrn_pre = _hgrn_prep_fwd(proj_o[0], proj_o[1], w["lower_bounds"])
    y_o, hgrn_hist = _hgrn_scan_fwd(*hgrn_pre, proj_o[2], proj_o[3], hgrn_g)
    s1 = _matmul(y_o, woo, name="fwd_out_odd")

    g = {}
    dpre1, dwg1, dwpl1, dlng1, dlnb1, loss = _lnpl_bwd(x1, s1, p[1], w["w_pl_gate"][1], w["w_pl"][1], ln_g1, ln_b1,
                                                     target, True, "lnpl_bwd_odd")
    dy_o = _matmul(dpre1, woo, tb=True, name="bwd_out_odd_dx")
    g["w_out_odd"] = _matmul(y_o, dpre1, ta=True, name="bwd_out_odd_dw")
    dqe, dkd, dat, deb, di, dz, dhg = _hgrn_scan_bwd(*hgrn_pre, proj_o[2], proj_o[3], hgrn_g, hgrn_hist, dy_o)
    dq, df, dlb = _hgrn_prep_bwd(proj_o[0], proj_o[1], w["lower_bounds"], dqe, dkd, dat, deb)
    dx1 = dpre1
    scale = a
    dws = []
    for j, dj in enumerate((dq, df, di, dz)):
        dx1 = _matmul(dj, wio[j], tb=True, add=dx1, add_scale=scale, name=f"bwd_proj_odd_dx{j}")
        scale = 1.0
        dws.append(_matmul(x1b, dj, ta=True, name=f"bwd_proj_odd_dw{j}"))
    g["w_in_odd"] = jnp.stack(dws)
    g["hgrn_norm_g"] = dhg[0:1]
    g["lower_bounds"] = dlb
    g["w_pl_gate1"], g["w_pl1"] = dwg1, dwpl1

    dpre0, dwg0, dwpl0, dlng0, dlnb0, _ = _lnpl_bwd(x, s0, p[0], w["w_pl_gate"][0], w["w_pl"][0], ln_g0, ln_b0,
                                                  dx1, False, "lnpl_bwd_even")
    g["w_pl_gate0"], g["w_pl0"] = dwg0, dwpl0
    g["ln_g"] = jnp.concatenate([dlng0, dlng1], axis=0)
    g["ln_b"] = jnp.concatenate([dlnb0, dlnb1], axis=0)
    dy_a = _matmul(dpre0, woe[:1024], tb=True, name="bwd_out_even_dxa")
    dy_b = _matmul(dpre0, woe[1024:], tb=True, name="bwd_out_even_dxb")
    g["w_out_even"] = jnp.concatenate([_matmul(y_a, dpre0, ta=True, name="bwd_out_even_dwa"),
                                       _matmul(y_b, dpre0, ta=True, name="bwd_out_even_dwb")], axis=0)
    token = early_grads_ready({n: g[n] for n in _RS_EARLY})
    d_a, dwa = _conv_a_bwd(proj_a, conv_a_w + token, dy_a)
    g["conv_a_w"] = dwa[:3]
    gdn_g = gdn_g + early_grads_swapped(d_a)
    du, dw, dqd, dkd, dat, deg, dzb, dgn = _gdn_scan_bwd(*gdn_pre, proj_zb, gdn_g, gdn_hist, dy_b)
    dqa, dka, dva, dbr, dar, dal, ddt = _gdn_prep_bwd(qkv_act, proj_tail, alog, dtb, gdn_inv, du, dw, dqd, dkd, dat, deg)
    g["a_log"] = dal[:, 0, 0].reshape(1, nh)
    g["dt_bias"] = ddt[:, 0, 0].reshape(1, nh)
    g["gdn_norm_g"] = dgn[0:1]
    d_pre_qkv, dwb = [], []
    for j, dj in enumerate((dqa, dka, dva)):
        dpj, dwj = _conv_b_bwd(proj_qkv, conv_b_w, dj, j, f"conv_b_bwd{j}")
        d_pre_qkv.append(dpj)
        dwb.append(dwj[:4])
    g["conv_b_w"] = jnp.concatenate(dwb, axis=1)
    d_tail = _pack_tail(dbr, dar)
    pieces = [(d_a, w_a), (d_pre_qkv[0], w_qkv[:, :1024]), (d_pre_qkv[1], w_qkv[:, 1024:2048]),
              (d_pre_qkv[2], w_qkv[:, 2048:]), (dzb, w_zb), (d_tail, w_tail)]
    dws = [_matmul(xb, dj, ta=True, name=f"bwd_proj_even_dw{j}") for j, (dj, _) in enumerate(pieces)]
    dws[-1] = dws[-1][:, :2 * nh]
    g["w_in_even"] = _col_shards(dws, wie.shape[2])
    token = last_grad_ready(g["w_in_even"])
    dx = dpre0
    scale = a
    for j, (dj, wj) in enumerate(pieces):
        dx = _matmul(dj, wj + jnp.asarray(token).astype(BF16) if j == 0 else wj, tb=True, add=dx, add_scale=scale,
                     name=f"bwd_proj_even_dx{j}")
        scale = 1.0
    return loss, dx, g


def _adamw(w, g, m, v, name):
    lead, rows, cols = w.shape
    if rows % SUBLANES == 0 or rows <= 256:
        tr, tc = (rows if rows <= 256 else 256), cols
    else:
        tr, tc = rows, 256
    assert rows % tr == 0 and cols % tc == 0, (name, rows, cols)

    def body(w_ref, g_ref, m_ref, v_ref, d_ref, nm_ref, nv_ref):
        gg = g_ref[...]
        nm = ADAM_B1 * m_ref[...] + (1.0 - ADAM_B1) * gg
        nv = ADAM_B2 * v_ref[...] + (1.0 - ADAM_B2) * jnp.square(gg)
        m_hat = nm / (1.0 - ADAM_B1 ** ADAM_STEP)
        v_hat = nv / (1.0 - ADAM_B2 ** ADAM_STEP)
        d_ref[...] = -ADAM_LR * (m_hat / (jnp.sqrt(v_hat) + ADAM_EPS) + ADAM_WD * w_ref[...])
        nm_ref[...] = nm
        nv_ref[...] = nv

    spec = pl.BlockSpec((1, tr, tc), lambda l, i, j: (l, i, j))
    return pl.pallas_call(
        body, name=name, grid=(lead, rows // tr, cols // tc), in_specs=[spec] * 4, out_specs=[spec] * 3,
        out_shape=[jax.ShapeDtypeStruct(w.shape, F32)] * 3,
        compiler_params=_params("parallel", "parallel", "parallel"))(w, g, m, v)


MESH = pl.DeviceIdType.MESH
N_DEV = 8
HBM_SPEC = pl.BlockSpec(memory_space=pltpu.HBM)
VMEM_SPEC = pl.BlockSpec(memory_space=pltpu.VMEM)


def _coords():
    return lax.axis_index("x"), lax.axis_index("y"), lax.axis_index("c")


def _flip(v, bit):
    return 1 - v if bit else v


def _remote(src, dst, send_sem, recv_sem, dev):
    return pltpu.make_async_remote_copy(src_ref=src, dst_ref=dst, send_sem=send_sem, recv_sem=recv_sem,
                                        device_id=dev, device_id_type=MESH)


def _exchange_small(buf, reduce, name):
    rows = buf.shape[0]

    def body(in_ref, out_ref, slots, send_sems, recv_sems):
        x, y, c = _coords()
        me = 4 * x + 2 * y + c
        slots[me] = in_ref[...]
        peer = lambda k: (_flip(x, (k >> 2) & 1), _flip(y, (k >> 1) & 1), _flip(c, k & 1))
        sends = []
        for k in range(1, N_DEV):
            cp = _remote(in_ref, slots.at[me], send_sems.at[k - 1], recv_sems.at[k - 1], peer(k))
            cp.start()
            sends.append(cp)
        for k in range(1, N_DEV):
            px, py, pc = peer(k)
            _remote(in_ref, slots.at[4 * px + 2 * py + pc], send_sems.at[k - 1], recv_sems.at[k - 1], peer(k)).wait_recv()
        for cp in sends:
            cp.wait_send()
        if reduce:
            acc = slots[0]
            for d in range(1, N_DEV):
                acc = acc + slots[d]
            out_ref[...] = acc
        else:
            out_ref[...] = slots[...]

    out_shape = (rows, LANES) if reduce else (N_DEV, rows, LANES)
    return pl.pallas_call(
        body, name=name, in_specs=[VMEM_SPEC], out_specs=VMEM_SPEC, out_shape=jax.ShapeDtypeStruct(out_shape, F32),
        scratch_shapes=[pltpu.VMEM((N_DEV, rows, LANES), F32), pltpu.SemaphoreType.DMA((N_DEV - 1,)),
                        pltpu.SemaphoreType.DMA((N_DEV - 1,))])(buf)


def _half_rows(half, which):
    return pl.ds(pl.multiple_of(which * half, 16), half)


def _other_chip(x, y, k):
    return _flip(x, (k >> 1) & 1), _flip(y, k & 1)


SEM_SPEC = pl.BlockSpec(memory_space=pltpu.SEMAPHORE)
DATAFLOW = pltpu.SideEffectType.DATAFLOW_SIDE_EFFECTING


def _ici_piece(srcs, lands, send_sems, recv_sems, i, k, x, y, c):
    half = srcs[i].shape[0] // 2
    ox, oy = _other_chip(x, y, k)
    return _remote(srcs[i].at[_half_rows(half, c)], lands[i].at[2 * x + y, _half_rows(half, c)],
                   send_sems.at[3 * i + k - 1], recv_sems.at[3 * i + k - 1], (ox, oy, c)), (ox, oy)


def _gather_start(shards, name):
    n = len(shards)

    def body(*refs):
        srcs, lands = refs[:n], refs[n:2 * n]
        send_sems, recv_sems = refs[2 * n], refs[2 * n + 1]
        token = refs[-1]
        x, y, c = _coords()
        for i in range(n):
            for k in (1, 2, 3):
                _ici_piece(srcs, lands, send_sems, recv_sems, i, k, x, y, c)[0].start()
        token[...] = jnp.zeros_like(token)

    hbm = lambda a: pltpu.with_memory_space_constraint(a, pltpu.HBM)
    lands = [lax.empty((4,) + s.shape, s.dtype) for s in shards]
    out = pl.pallas_call(
        body, name=name,
        out_shape=(pltpu.SemaphoreType.DMA((3 * n,)), pltpu.SemaphoreType.DMA((3 * n,)),
                   *[pltpu.HBM(s.shape, s.dtype) for s in shards], *[pltpu.HBM(a.shape, a.dtype) for a in lands],
                   jax.ShapeDtypeStruct((SUBLANES, LANES), F32)),
        in_specs=[HBM_SPEC] * (2 * n), out_specs=(SEM_SPEC, SEM_SPEC, *[HBM_SPEC] * (2 * n), VMEM_SPEC),
        input_output_aliases={i: 2 + i for i in range(2 * n)},
        compiler_params=pltpu.CompilerParams(has_side_effects=DATAFLOW))(*[hbm(s) for s in shards], *[hbm(a) for a in lands])
    return out[0], out[1], out[2:2 + n], out[2 + n:2 + 2 * n], out[-1]


def _gather_wait(send_sems, recv_sems, srcs, lands, after, name):
    n = len(srcs)

    def body(*refs):
        src_refs, land_refs = refs[:n], refs[n:2 * n]
        send_sems, recv_sems = refs[2 * n], refs[2 * n + 1]
        x, y, c = _coords()
        for i in range(n):
            half = src_refs[i].shape[0] // 2
            for k in (1, 2, 3):
                cp, (ox, oy) = _ici_piece(src_refs, land_refs, send_sems, recv_sems, i, k, x, y, c)
                cp.wait_send()
                piece = land_refs[i].at[2 * ox + oy, _half_rows(half, c)]
                _remote(piece, piece, send_sems.at[3 * i + k - 1], recv_sems.at[3 * i + k - 1], (ox, oy, c)).wait_recv()

    out = pl.pallas_call(
        body, name=name,
        out_shape=(*[pltpu.HBM(s.shape, s.dtype) for s in srcs], *[pltpu.HBM(a.shape, a.dtype) for a in lands]),
        in_specs=[HBM_SPEC] * (2 * n) + [SEM_SPEC, SEM_SPEC, pl.BlockSpec(memory_space=pl.ANY)],
        out_specs=tuple([HBM_SPEC] * (2 * n)), input_output_aliases={i: i for i in range(2 * n)},
        compiler_params=pltpu.CompilerParams(has_side_effects=DATAFLOW))(*srcs, *lands, send_sems, recv_sems, after)
    return out[n:]


def _gather_forward(lands, name):
    n = len(lands)

    def body(*refs):
        ins, outs = refs[:n], refs[n:2 * n]
        send_sems, recv_sems = refs[2 * n:]
        x, y, c = _coords()
        sends = []
        for i in range(n):
            half = ins[i].shape[1] // 2
            for k in (1, 2, 3):
                ox, oy = _other_chip(x, y, k)
                cp = _remote(ins[i].at[2 * ox + oy, _half_rows(half, c)], outs[i].at[2 * ox + oy, _half_rows(half, c)],
                             send_sems.at[3 * i + k - 1], recv_sems.at[3 * i + k - 1], (x, y, 1 - c))
                cp.start()
                sends.append(cp)
        for i in range(n):
            half = ins[i].shape[1] // 2
            for k in (1, 2, 3):
                ox, oy = _other_chip(x, y, k)
                piece = outs[i].at[2 * ox + oy, _half_rows(half, 1 - c)]
                _remote(piece, piece, send_sems.at[3 * i + k - 1], recv_sems.at[3 * i + k - 1], (x, y, 1 - c)).wait_recv()
        for cp in sends:
            cp.wait_send()

    return pl.pallas_call(
        body, name=name, in_specs=[HBM_SPEC] * n, out_specs=[HBM_SPEC] * n,
        out_shape=[jax.ShapeDtypeStruct(a.shape, a.dtype) for a in lands],
        input_output_aliases={i: i for i in range(n)},
        scratch_shapes=[pltpu.SemaphoreType.DMA((3 * n,))] * 2)(*lands)


def _rs_sibling_swap(g4s, name):
    n = len(g4s)

    def body(*refs):
        ins, outs = refs[:n], refs[n:2 * n]
        send_sems, recv_sems = refs[2 * n:]
        x, y, c = _coords()
        sends = []
        for i in range(n):
            half = ins[i].shape[1] // 2
            for s in range(4):
                cp = _remote(ins[i].at[s, _half_rows(half, 1 - c)], outs[i].at[s], send_sems.at[4 * i + s],
                             recv_sems.at[4 * i + s], (x, y, 1 - c))
                cp.start()
                sends.append(cp)
        for cp in sends:
            cp.wait_recv()
        for cp in sends:
            cp.wait_send()

    return pl.pallas_call(
        body, name=name, in_specs=[HBM_SPEC] * n, out_specs=[HBM_SPEC] * n,
        out_shape=[jax.ShapeDtypeStruct((4, g.shape[1] // 2, g.shape[2]), g.dtype) for g in g4s],
        scratch_shapes=[pltpu.SemaphoreType.DMA((4 * n,))] * 2)(*g4s)


def _rs_add_sibling(g4, got, c_idx, name):
    _, rows, cols = g4.shape
    half = rows // 2
    tr = min(half, 256)
    nb = half // tr

    def body(c_ref, a_ref, b_ref, o_ref, ob_ref):
        total = a_ref[...] + b_ref[...]
        o_ref[...] = total
        ob_ref[...] = total.astype(BF16)

    blk = (1, tr, cols)
    out = pl.BlockSpec(blk, lambda s, i, c_ref: (s, i, 0))
    grid_spec = pltpu.PrefetchScalarGridSpec(
        num_scalar_prefetch=1, grid=(4, nb),
        in_specs=[pl.BlockSpec(blk, lambda s, i, c_ref: (s, c_ref[0] * nb + i, 0)), out],
        out_specs=[out, out])
    return pl.pallas_call(
        body, name=name, grid_spec=grid_spec,
        out_shape=[jax.ShapeDtypeStruct(got.shape, F32), jax.ShapeDtypeStruct(got.shape, BF16)],
        compiler_params=_params("parallel", "parallel"))(c_idx, g4, got)


def _rs_add_chips(p4, got3, idx, name):
    _, half, cols = p4.shape
    tr = min(half, 256)
    nb = half // tr

    def body(idx_ref, p_ref, a_ref, b_ref, c_ref, o_ref):
        o_ref[...] = ((p_ref[0] + a_ref[0].astype(F32)) + b_ref[0].astype(F32)) + c_ref[0].astype(F32)

    blk = (1, tr, cols)
    grid_spec = pltpu.PrefetchScalarGridSpec(
        num_scalar_prefetch=1, grid=(nb,),
        in_specs=[pl.BlockSpec(blk, lambda i, idx_ref: (idx_ref[0], i, 0))]
        + [pl.BlockSpec(blk, functools.partial(lambda k, i, idx_ref: (k, i, 0), k)) for k in range(3)],
        out_specs=pl.BlockSpec((tr, cols), lambda i, idx_ref: (idx_ref[1] * nb + i, 0)))
    return pl.pallas_call(body, name=name, grid_spec=grid_spec, out_shape=jax.ShapeDtypeStruct((2 * half, cols), F32),
                          compiler_params=_params("parallel"))(idx, p4, got3, got3, got3)


def _rs_share_halves(bufs, name):
    n = len(bufs)

    def body(*refs):
        ins, outs = refs[:n], refs[n:2 * n]
        send_sems, recv_sems = refs[2 * n:]
        x, y, c = _coords()
        sends = []
        for i in range(n):
            half = ins[i].shape[0] // 2
            cp = _remote(ins[i].at[_half_rows(half, c)], outs[i].at[_half_rows(half, c)], send_sems.at[i],
                         recv_sems.at[i], (x, y, 1 - c))
            cp.start()
            sends.append(cp)
        for i in range(n):
            half = ins[i].shape[0] // 2
            _remote(ins[i].at[_half_rows(half, c)], outs[i].at[_half_rows(half, 1 - c)], send_sems.at[i],
                    recv_sems.at[i], (x, y, 1 - c)).wait_recv()
        for cp in sends:
            cp.wait_send()

    return pl.pallas_call(
        body, name=name, in_specs=[HBM_SPEC] * n, out_specs=[HBM_SPEC] * n,
        out_shape=[jax.ShapeDtypeStruct(b.shape, b.dtype) for b in bufs],
        input_output_aliases={i: i for i in range(n)},
        scratch_shapes=[pltpu.SemaphoreType.DMA((n,))] * 2)(*bufs)


def _scatter_piece(srcs, lands, send_sems, recv_sems, i, k, x, y, c):
    ox, oy = _other_chip(x, y, k)
    return _remote(srcs[i].at[2 * ox + oy], lands[i].at[k - 1], send_sems.at[3 * i + k - 1],
                   recv_sems.at[3 * i + k - 1], (ox, oy, c))


def _rs_scatter_start(p4s, name):
    n = len(p4s)

    def body(*refs):
        srcs, lands = refs[:n], refs[n:2 * n]
        send_sems, recv_sems = refs[2 * n], refs[2 * n + 1]
        token = refs[-1]
        x, y, c = _coords()
        for i in range(n):
            for k in (1, 2, 3):
                _scatter_piece(srcs, lands, send_sems, recv_sems, i, k, x, y, c).start()
        token[...] = jnp.zeros_like(token)

    hbm = lambda a: pltpu.with_memory_space_constraint(a, pltpu.HBM)
    lands = [lax.empty((3,) + p.shape[1:], p.dtype) for p in p4s]
    out = pl.pallas_call(
        body, name=name,
        out_shape=(pltpu.SemaphoreType.DMA((3 * n,)), pltpu.SemaphoreType.DMA((3 * n,)),
                   *[pltpu.HBM(p.shape, p.dtype) for p in p4s], *[pltpu.HBM(a.shape, a.dtype) for a in lands],
                   jax.ShapeDtypeStruct((SUBLANES, LANES), F32)),
        in_specs=[HBM_SPEC] * (2 * n), out_specs=(SEM_SPEC, SEM_SPEC, *[HBM_SPEC] * (2 * n), VMEM_SPEC),
        input_output_aliases={i: 2 + i for i in range(2 * n)},
        compiler_params=pltpu.CompilerParams(has_side_effects=DATAFLOW))(*[hbm(p) for p in p4s], *[hbm(a) for a in lands])
    return out[0], out[1], out[2:2 + n], out[2 + n:2 + 2 * n], out[-1]


def _rs_scatter_wait(send_sems, recv_sems, srcs, lands, after, name):
    n = len(srcs)

    def body(*refs):
        src_refs, land_refs = refs[:n], refs[n:2 * n]
        send_sems, recv_sems = refs[2 * n], refs[2 * n + 1]
        x, y, c = _coords()
        for i in range(n):
            for k in (1, 2, 3):
                cp = _scatter_piece(src_refs, land_refs, send_sems, recv_sems, i, k, x, y, c)
                cp.wait_send()
                cp.wait_recv()

    out = pl.pallas_call(
        body, name=name,
        out_shape=(*[pltpu.HBM(s.shape, s.dtype) for s in srcs], *[pltpu.HBM(a.shape, a.dtype) for a in lands]),
        in_specs=[HBM_SPEC] * (2 * n) + [SEM_SPEC, SEM_SPEC, pl.BlockSpec(memory_space=pl.ANY)],
        out_specs=tuple([HBM_SPEC] * (2 * n)), input_output_aliases={i: i for i in range(2 * n)},
        compiler_params=pltpu.CompilerParams(has_side_effects=DATAFLOW))(*srcs, *lands, send_sems, recv_sems, after)
    return out[n:]


def _swap_piece(srcs, lands, send_sems, recv_sems, i, s, x, y, c):
    half = srcs[i].shape[1] // 2
    return _remote(srcs[i].at[s, _half_rows(half, 1 - c)], lands[i].at[s], send_sems.at[4 * i + s],
                   recv_sems.at[4 * i + s], (x, y, 1 - c))


def _rs_swap_start(g4s, name):
    n = len(g4s)

    def body(*refs):
        srcs, lands = refs[:n], refs[n:2 * n]
        send_sems, recv_sems = refs[2 * n], refs[2 * n + 1]
        token = refs[-1]
        x, y, c = _coords()
        for i in range(n):
            for s in range(4):
                _swap_piece(srcs, lands, send_sems, recv_sems, i, s, x, y, c).start()
        token[...] = jnp.zeros_like(token)

    hbm = lambda a: pltpu.with_memory_space_constraint(a, pltpu.HBM)
    lands = [lax.empty((4, g.shape[1] // 2, g.shape[2]), g.dtype) for g in g4s]
    out = pl.pallas_call(
        body, name=name,
        out_shape=(pltpu.SemaphoreType.DMA((4 * n,)), pltpu.SemaphoreType.DMA((4 * n,)),
                   *[pltpu.HBM(g.shape, g.dtype) for g in g4s], *[pltpu.HBM(a.shape, a.dtype) for a in lands],
                   jax.ShapeDtypeStruct((SUBLANES, LANES), F32)),
        in_specs=[HBM_SPEC] * (2 * n), out_specs=(SEM_SPEC, SEM_SPEC, *[HBM_SPEC] * (2 * n), VMEM_SPEC),
        input_output_aliases={i: 2 + i for i in range(2 * n)},
        compiler_params=pltpu.CompilerParams(has_side_effects=DATAFLOW))(*[hbm(g) for g in g4s], *[hbm(a) for a in lands])
    return out[0], out[1], out[2:2 + n], out[2 + n:2 + 2 * n], out[-1]


def _rs_swap_wait(send_sems, recv_sems, srcs, lands, after, name):
    n = len(srcs)

    def body(*refs):
        src_refs, land_refs = refs[:n], refs[n:2 * n]
        send_sems, recv_sems = refs[2 * n], refs[2 * n + 1]
        x, y, c = _coords()
        for i in range(n):
            for s in range(4):
                cp = _swap_piece(src_refs, land_refs, send_sems, recv_sems, i, s, x, y, c)
                cp.wait_send()
                cp.wait_recv()

    out = pl.pallas_call(
        body, name=name,
        out_shape=(*[pltpu.HBM(s.shape, s.dtype) for s in srcs], *[pltpu.HBM(a.shape, a.dtype) for a in lands]),
        in_specs=[HBM_SPEC] * (2 * n) + [SEM_SPEC, SEM_SPEC, pl.BlockSpec(memory_space=pl.ANY)],
        out_specs=tuple([HBM_SPEC] * (2 * n)), input_output_aliases={i: i for i in range(2 * n)},
        compiler_params=pltpu.CompilerParams(has_side_effects=DATAFLOW))(*srcs, *lands, send_sems, recv_sems, after)
    return out[:n], out[n:]


def _rs_front(g4s, names, tag):
    c_idx = jnp.stack([lax.axis_index("c")]).astype(jnp.int32)
    got = _rs_sibling_swap(g4s, f"rs_sibling_swap_{tag}")
    return [_rs_add_sibling(g, s, c_idx, f"rs_add_sibling_{nm}") for g, s, nm in zip(g4s, got, names)]


def _rs_back(p4s, got3, names):
    x, y, c = _coords()
    idx = jnp.stack([2 * x + y, c]).astype(jnp.int32)
    return [_rs_add_chips(p, t, idx, f"rs_add_chips_{nm}") for (p, _), t, nm in zip(p4s, got3, names)]


def _cols_split(full):
    r, c4 = full.shape
    return full.reshape(r, 4, c4 // 4).transpose(1, 0, 2)


_BIG = {
    "w_in_even": ((1024, 2052), lambda s: s),
    "w_out_even": ((512, 1024), lambda s: s.reshape(2048, 1024)),
    "w_in_odd": ((1024, 2048), lambda s: s),
    "w_out_odd": ((512, 1024), lambda s: s.reshape(2048, 1024)),
    "w_pl": ((512, 256), lambda s: s.reshape(4, 2, 256, 256).transpose(1, 2, 0, 3).reshape(2, 256, 1024)),
    "w_pl_gate": ((512, 1024), lambda s: s.reshape(4, 2, 256, 1024).transpose(1, 0, 2, 3).reshape(2, 1024, 1024)),
}


_RS_EARLY = {
    "w_in_odd": lambda f: f,
    "w_out_odd": lambda f: f.reshape(4, 512, 1024),
    "w_pl_gate1": lambda f: f.reshape(4, 256, 1024),
    "w_pl1": _cols_split,
    "w_out_even": lambda f: f.reshape(4, 512, 1024),
    "w_pl_gate0": lambda f: f.reshape(4, 256, 1024),
    "w_pl0": _cols_split,
}
_RS_LATE = {"w_in_even": lambda f: f}


def _size(shape):
    n = 1
    for d in shape:
        n *= d
    return n


_SMALL = {"a_log": (1, 8), "dt_bias": (1, 8), "gdn_norm_g": (1, 128), "hgrn_norm_g": (1, 128),
          "lower_bounds": (2, 2048), "ln_g": (2, 1024), "ln_b": (2, 1024), "conv_a_w": (3, 1024), "conv_b_w": (4, 3072)}
_CONV_SHARD = {"conv_a_w": (3, 256), "conv_b_w": (4, 768)}


def _pack_small(parts, shapes, head_rows=0):
    rows = []
    for n, shape in shapes.items():
        v = parts[n].reshape(-1)
        rows.append(jnp.pad(v, (0, -v.shape[0] % LANES)).reshape(-1, LANES))
    buf = jnp.concatenate(rows, axis=0)
    return jnp.pad(buf, ((head_rows, -(buf.shape[0] + head_rows) % SUBLANES), (0, 0)))


def _unpack_small(buf, shapes, head_rows=0):
    out, off = {}, head_rows
    for n, shape in shapes.items():
        nrow = -(-_size(shape) // LANES)
        out[n] = buf[off:off + nrow].reshape(-1)[:_size(shape)].reshape(shape)
        off += nrow
    return out


_WEIGHTS = ["w_in_even", "conv_a_w", "conv_b_w", "a_log", "dt_bias", "gdn_norm_g", "w_out_even", "w_in_odd",
            "lower_bounds", "hgrn_norm_g", "w_out_odd", "ln_g", "ln_b", "w_pl", "w_pl_gate"]


def kernel(x, p, w_in_even, conv_a_w, conv_b_w, a_log, dt_bias, gdn_norm_g, w_out_even, w_in_odd, lower_bounds, hgrn_norm_g, w_out_odd, ln_g, ln_b, w_pl, w_pl_gate, loss_target, m_w_in_even, m_conv_a_w, m_conv_b_w, m_a_log, m_dt_bias, m_gdn_norm_g, m_w_out_even, m_w_in_odd, m_lower_bounds, m_hgrn_norm_g, m_w_out_odd, m_ln_g, m_ln_b, m_w_pl, m_w_pl_gate, v_w_in_even, v_conv_a_w, v_conv_b_w, v_a_log, v_dt_bias, v_gdn_norm_g, v_w_out_even, v_w_in_odd, v_lower_bounds, v_hgrn_norm_g, v_w_out_odd, v_ln_g, v_ln_b, v_w_pl, v_w_pl_gate):
    w = dict(zip(_WEIGHTS, (w_in_even, conv_a_w, conv_b_w, a_log, dt_bias, gdn_norm_g, w_out_even, w_in_odd,
                            lower_bounds, hgrn_norm_g, w_out_odd, ln_g, ln_b, w_pl, w_pl_gate)))
    m = dict(zip(_WEIGHTS, (m_w_in_even, m_conv_a_w, m_conv_b_w, m_a_log, m_dt_bias, m_gdn_norm_g, m_w_out_even,
                            m_w_in_odd, m_lower_bounds, m_hgrn_norm_g, m_w_out_odd, m_ln_g, m_ln_b, m_w_pl, m_w_pl_gate)))
    v = dict(zip(_WEIGHTS, (v_w_in_even, v_conv_a_w, v_conv_b_w, v_a_log, v_dt_bias, v_gdn_norm_g, v_w_out_even,
                            v_w_in_odd, v_lower_bounds, v_hgrn_norm_g, v_w_out_odd, v_ln_g, v_ln_b, v_w_pl, v_w_pl_gate)))
    chip = 2 * lax.axis_index("x") + lax.axis_index("y")

    names = list(_BIG)
    shard_shapes = {n: _BIG[n][0] for n in names}
    early, late = names[:1], names[1:]
    shards = {n: w[n].reshape(shard_shapes[n]).astype(BF16) for n in early}
    whole = lambda n, stacked: _BIG[n][1](lax.dynamic_update_slice(stacked, shards[n][None], (chip, 0, 0)))
    conv_mine = _pack_small({n: w[n] for n in _CONV_SHARD}, _CONV_SHARD)
    conv_all = _exchange_small(conv_mine, False, "gather_conv")
    shards, conv_all = lax.optimization_barrier((shards, conv_all))
    first = _gather_start([shards[n] for n in early], "gather_first_start")
    shards.update({n: (w[n].reshape(shard_shapes[n]) + first[4][0, 0]).astype(BF16) for n in late})
    send_sems, recv_sems, srcs, lands, token = _gather_start([shards[n] for n in late], "gather_rest_start")

    def late_weights(after):
        landed = _gather_forward(_gather_wait(send_sems, recv_sems, srcs, lands, after, "gather_rest_wait"),
                                 "gather_rest_forward")
        return {n: whole(n, ga) for n, ga in zip(late, landed)}

    landed = _gather_forward(_gather_wait(*first[:4], token, "gather_first_wait"), "gather_first_forward")
    full = {n: whole(n, ga) for n, ga in zip(early, landed)}
    conv_by_chip = [_unpack_small(conv_all[2 * s], _CONV_SHARD) for s in range(4)]
    for n in _CONV_SHARD:
        full[n] = jnp.concatenate([conv_by_chip[s][n] for s in range(4)], axis=1)
    for n in _SMALL:
        if n not in _CONV_SHARD:
            full[n] = w[n]

    early_rs = {}

    def early_grads_ready(grads):
        early_rs["swap"] = _rs_swap_start([_RS_EARLY[n](grads[n]) for n in _RS_EARLY], "rs_swap_early_start")
        return early_rs["swap"][4][0, 0]

    def early_grads_swapped(after):
        g4s, got = _rs_swap_wait(*early_rs["swap"][:4], after, "rs_swap_early_wait")
        c_idx = jnp.stack([lax.axis_index("c")]).astype(jnp.int32)
        early_rs["p4s"] = [_rs_add_sibling(g_, s_, c_idx, f"rs_add_sibling_{nm}") for g_, s_, nm in zip(g4s, got, _RS_EARLY)]
        early_rs["sems"] = _rs_scatter_start([pb for _, pb in early_rs["p4s"]], "rs_scatter_early_start")
        return early_rs["sems"][4][0, 0]

    late_rs = {}

    def last_grad_ready(grad):
        late_rs["p4s"] = _rs_front([_RS_LATE[n](grad) for n in _RS_LATE], list(_RS_LATE), "late")
        late_rs["sems"] = _rs_scatter_start([pb for _, pb in late_rs["p4s"]], "rs_scatter_late_start")
        return late_rs["sems"][4][0, 0]

    loss_part, dx, g = _local_step(x[0], p[:, 0], loss_target[0], full, late_weights, early_grads_ready,
                                   early_grads_swapped, last_grad_ready, token[0, 0])

    late_p4s, late_sems = late_rs["p4s"], late_rs["sems"]
    got3 = _rs_scatter_wait(*early_rs["sems"][:4], dx, "rs_scatter_early_wait")
    summed = dict(zip(_RS_EARLY, _rs_share_halves(_rs_back(early_rs["p4s"], got3, list(_RS_EARLY)), "rs_share_early")))
    g_big = {n: summed[n] for n in names if n in summed}
    g_big["w_pl"] = jnp.stack([summed["w_pl0"], summed["w_pl1"]])
    g_big["w_pl_gate"] = jnp.stack([summed["w_pl_gate0"], summed["w_pl_gate1"]])
    small_sum = _exchange_small(jnp.concatenate([loss_part, _pack_small(g, _SMALL)], axis=0), True, "reduce_small")
    loss = small_sum[0, 0]
    g_small = _unpack_small(small_sum, _SMALL, head_rows=SUBLANES)
    for n, (rows, cols) in _CONV_SHARD.items():
        g_small[n] = lax.dynamic_slice_in_dim(g_small[n], chip * cols, cols, axis=1)

    grads, delta, new_m, new_v = {}, {}, {}, {}
    for n in late:
        grads[n] = g_big[n].reshape(w[n].shape)
        delta[n], new_m[n], new_v[n] = _adamw(w[n], grads[n], m[n], v[n], f"adamw_{n}")
    own = {n: (_CONV_SHARD[n] if n in _CONV_SHARD else _SMALL[n]) for n in _SMALL}
    packs = [_pack_small({n: src[n] for n in _SMALL}, own)[None] for src in (w, g_small, m, v)]
    outs = [_unpack_small(t[0], own) for t in _adamw(*packs, "adamw_small")]
    for n in _SMALL:
        grads[n] = g_small[n].reshape(w[n].shape)
        delta[n], new_m[n], new_v[n] = (t[n].reshape(w[n].shape) for t in outs)
    got3 = _rs_scatter_wait(*late_sems[:4], new_v["w_in_odd"], "rs_scatter_late_wait")
    (g_in_even,) = _rs_share_halves(_rs_back(late_p4s, got3, list(_RS_LATE)), "rs_share_late")
    for n in early:
        t_ = lambda a: jnp.swapaxes(a, 1, 2)
        g_t = t_(g_in_even.reshape(w[n].shape))
        grads[n] = t_(g_t)
        delta[n], new_m[n], new_v[n] = (t_(o) for o in _adamw(t_(w[n]), g_t, t_(m[n]), t_(v[n]), f"adamw_{n}"))
    return (loss, dx[None], *[grads[n] for n in _WEIGHTS], *[delta[n] for n in _WEIGHTS],
            *[new_m[n] for n in _WEIGHTS], *[new_v[n] for n in _WEIGHTS])
```

```python
import functools

import jax
import jax.numpy as jnp
from jax import lax
from jax.experimental import pallas as pl
from jax.experimental.pallas import tpu as pltpu

F32 = jnp.float32
BF16 = jnp.bfloat16

D_MODEL = 1024
PL_DIM = 256
GDN_HEADS = 8
HEAD_DIM = 128
GDN_CHUNK = 64
HGRN_HEADS = 16
HGRN_CHUNK = 32
HGRN_WIDTH = 2048
DEEPNORM_ALPHA = 4.0 ** 0.25
NORM_EPS = 1e-5
ADAM_LR, ADAM_B1, ADAM_B2, ADAM_EPS, ADAM_WD, ADAM_STEP = 0.001, 0.9, 0.999, 1e-08, 0.01, 10

VMEM_LIMIT = 60 * 1024 * 1024
SUBLANES = 8
LANES = 128


def _params(*sem):
    return pltpu.CompilerParams(dimension_semantics=sem, vmem_limit_bytes=VMEM_LIMIT)


ONE_PASS, THREE_PASS, EXACT_LHS, EXACT_RHS = 0, 1, 2, 3


def _split3(v):
    hi = v.astype(BF16)
    r1 = v - hi.astype(F32)
    mid = r1.astype(BF16)
    return hi, mid, (r1 - mid.astype(F32)).astype(BF16)


def _mm_raw(a, b, kind, prec):
    nb = a.ndim - 2
    ca = a.ndim - 1 if kind[0] == "n" else a.ndim - 2
    cb = b.ndim - 2 if kind[1] == "n" else b.ndim - 1
    dims = (((ca,), (cb,)), (tuple(range(nb)),) * 2)
    dot = lambda p, q: lax.dot_general(p, q, dims, preferred_element_type=F32)
    ah, bh = a.astype(BF16), b.astype(BF16)
    if prec == ONE_PASS:
        return dot(ah, bh)
    if prec == EXACT_LHS:
        b1, b2, b3 = _split3(b)
        return dot(ah, b1) + (dot(ah, b2) + dot(ah, b3))
    if prec == EXACT_RHS:
        a1, a2, a3 = _split3(a)
        return dot(a1, bh) + (dot(a2, bh) + dot(a3, bh))
    al = (a - ah.astype(F32)).astype(BF16)
    bl = (b - bh.astype(F32)).astype(BF16)
    return dot(ah, bh) + (dot(ah, bl) + dot(al, bh))


@functools.partial(jax.custom_vjp, nondiff_argnums=(2, 3))
def _mm_vjp(a, b, kind, hi):
    return _mm_raw(a, b, kind, hi)


def _mm_vjp_fwd(a, b, kind, hi):
    return _mm_raw(a, b, kind, hi), (a, b)


def _mm_vjp_bwd(kind, hi, res, dc):
    a, b = res
    if hi in (EXACT_LHS, EXACT_RHS):
        assert kind == "nn"
        if hi == EXACT_LHS:
            return jnp.zeros_like(a), _mm_raw(a, dc, "tn", EXACT_LHS)
        return _mm_raw(dc, b, "nt", EXACT_RHS), jnp.zeros_like(b)
    if kind == "nn":
        return _mm_raw(dc, b, "nt", hi), _mm_raw(a, dc, "tn", hi)
    if kind == "nt":
        return _mm_raw(dc, b, "nn", hi), _mm_raw(dc, a, "tn", hi)
    return _mm_raw(b, dc, "nt", hi), _mm_raw(a, dc, "nn", hi)


_mm_vjp.defvjp(_mm_vjp_fwd, _mm_vjp_bwd)


def _lane_total(v):
    return jnp.broadcast_to(jnp.sum(v, axis=-1, keepdims=True), v.shape)


def _matmul(a, b, *, name, ta=False, tb=False, add=None, add_scale=1.0, tm=1024, tn=2048, tk=1024):
    m, k = (a.shape[1], a.shape[0]) if ta else a.shape
    n = b.shape[0] if tb else b.shape[1]
    tm, tn, tk = min(tm, m), min(tn, n), min(tk, k)
    tn = tn if n % tn == 0 else tn // 2
    assert m % tm == 0 and n % tn == 0 and k % tk == 0, (name, m, n, k)
    nk = k // tk
    dims = (((0 if ta else 1,), (1 if tb else 0,)), ((), ()))

    def body(*refs):
        a_ref, b_ref = refs[:2]
        o_ref = refs[-1]
        part = lax.dot_general(a_ref[...].astype(BF16), b_ref[...].astype(BF16), dims, preferred_element_type=F32)
        first = (lambda: part) if add is None else (lambda: part + add_scale * refs[2][...])
        if nk == 1:
            o_ref[...] = first()
        else:
            kk = pl.program_id(2)

            @pl.when(kk == 0)
            def _():
                o_ref[...] = first()

            @pl.when(kk > 0)
            def _():
                o_ref[...] += part

    a_spec = pl.BlockSpec((tk, tm), lambda i, j, kk: (kk, i)) if ta else pl.BlockSpec((tm, tk), lambda i, j, kk: (i, kk))
    b_spec = pl.BlockSpec((tn, tk), lambda i, j, kk: (j, kk)) if tb else pl.BlockSpec((tk, tn), lambda i, j, kk: (kk, j))
    o_spec = pl.BlockSpec((tm, tn), lambda i, j, kk: (i, j))
    in_specs = [a_spec, b_spec] + ([o_spec] if add is not None else [])
    args = (a, b) + ((add,) if add is not None else ())
    return pl.pallas_call(
        body, name=name, grid=(m // tm, n // tn, nk), in_specs=in_specs, out_specs=o_spec,
        out_shape=jax.ShapeDtypeStruct((m, n), F32),
        compiler_params=_params("parallel", "parallel", "arbitrary"))(*args)


HALO = SUBLANES


def _halo_specs(tt, width, col, nt):
    r = tt // HALO
    prev = pl.BlockSpec((HALO, width), lambda i: (jnp.maximum(i * r - 1, 0), col))
    nxt = pl.BlockSpec((HALO, width), lambda i: (jnp.minimum((i + 1) * r, nt * r - 1), col))
    return prev, nxt


def _shift_down(ext, k):
    return ext if k == 0 else pltpu.roll(ext, k, 0)


def _shift_up(ext, k):
    return ext if k == 0 else pltpu.roll(ext, ext.shape[0] - k, 0)


def _causal_conv(ext, w, taps):
    acc = None
    for j in range(taps):
        term = w[j:j + 1, :] * _shift_down(ext, taps - 1 - j)
        acc = term if acc is None else acc + term
    return acc[HALO:, :]


def _conv_a_fwd(proj_a, conv_w):
    t = proj_a.shape[0]
    tt = min(t, 256)
    nt = t // tt
    wdt = 1024

    def body(cur_ref, prev_ref, w_ref, y_ref):
        i = pl.program_id(0)
        cur = cur_ref[...]
        h, c, b, z = (cur[:, k * wdt:(k + 1) * wdt] for k in range(4))
        prev = prev_ref[...]
        u_prev = jnp.where(i > 0, prev[:, wdt:2 * wdt] * prev[:, 0:wdt], 0.0)
        ext = jnp.concatenate([u_prev, c * h], axis=0)
        conv = _causal_conv(ext, w_ref[...], 3)
        y_ref[...] = (b * conv * jax.nn.silu(z)).astype(BF16)

    prev_spec, _ = _halo_specs(tt, 4 * wdt, 0, nt)
    return pl.pallas_call(
        body, name="conv_a_fwd", grid=(nt,),
        in_specs=[pl.BlockSpec((tt, 4 * wdt), lambda i: (i, 0)), prev_spec, pl.BlockSpec((3, wdt), lambda i: (0, 0))],
        out_specs=pl.BlockSpec((tt, wdt), lambda i: (i, 0)),
        out_shape=jax.ShapeDtypeStruct((t, wdt), BF16), compiler_params=_params("parallel"))(proj_a, proj_a, conv_w)


def _conv_a_bwd(proj_a, conv_w, dy):
    t = proj_a.shape[0]
    tt = min(t, 256)
    nt = t // tt
    wdt = 1024

    def body(cur_ref, prev_ref, nxt_ref, w_ref, dy_ref, dyn_ref, d_ref, dw_ref):
        i = pl.program_id(0)
        w = w_ref[...]
        cur, prev, nxt = cur_ref[...], prev_ref[...], nxt_ref[...]
        split = lambda a: tuple(a[:, k * wdt:(k + 1) * wdt] for k in range(4))
        h, c, b, z = split(cur)
        hp, cp, _, _ = split(prev)
        hn, cn, bn, zn = split(nxt)
        u_prev = jnp.where(i > 0, cp * hp, 0.0)
        u_ext = jnp.concatenate([u_prev, c * h, cn * hn], axis=0)
        taps = [_shift_down(u_ext, 2 - j)[HALO:, :] for j in range(3)]
        conv = w[0:1, :] * taps[0] + w[1:2, :] * taps[1] + w[2:3, :] * taps[2]
        b_cn = jnp.concatenate([b, bn], axis=0)
        z_cn = jnp.concatenate([z, zn], axis=0)
        dy_cn = jnp.concatenate([dy_ref[...], jnp.where(i < nt - 1, dyn_ref[...], 0.0)], axis=0)
        sg = jax.nn.sigmoid(z_cn)
        silu = z_cn * sg
        d_conv = dy_cn * b_cn * silu
        db = (dy_cn * conv * silu)[:tt, :]
        dz = (dy_cn * b_cn * conv * (sg * (1.0 + z_cn * (1.0 - sg))))[:tt, :]
        du = None
        for j in range(3):
            term = w[j:j + 1, :] * _shift_up(d_conv, 2 - j)
            du = term if du is None else du + term
        du = du[:tt, :]
        d_ref[...] = jnp.concatenate([du * c, du * h, db, dz], axis=1).astype(BF16)

        @pl.when(i == 0)
        def _():
            dw_ref[...] = jnp.zeros_like(dw_ref)

        d_cur = d_conv[:tt, :]
        rows = [jnp.sum(d_cur * taps[j][:tt, :], axis=0, keepdims=True) for j in range(3)]
        dw_ref[0:3, :] += jnp.concatenate(rows, axis=0)

    prev_spec, nxt_spec = _halo_specs(tt, 4 * wdt, 0, nt)
    _, dyn_spec = _halo_specs(tt, wdt, 0, nt)
    return pl.pallas_call(
        body, name="conv_a_bwd", grid=(nt,),
        in_specs=[pl.BlockSpec((tt, 4 * wdt), lambda i: (i, 0)), prev_spec, nxt_spec,
                  pl.BlockSpec((3, wdt), lambda i: (0, 0)), pl.BlockSpec((tt, wdt), lambda i: (i, 0)), dyn_spec],
        out_specs=[pl.BlockSpec((tt, 4 * wdt), lambda i: (i, 0)), pl.BlockSpec((SUBLANES, wdt), lambda i: (0, 0))],
        out_shape=[jax.ShapeDtypeStruct((t, 4 * wdt), BF16), jax.ShapeDtypeStruct((SUBLANES, wdt), F32)],
        compiler_params=_params("arbitrary"))(proj_a, proj_a, proj_a, conv_w, dy, dy)


def _conv_b_fwd(proj_qkv, conv_w):
    t, width = proj_qkv.shape
    tt = min(t, 256)
    nt = t // tt
    wdt = 1024

    def body(cur_ref, prev_ref, w_ref, y_ref):
        i = pl.program_id(1)
        ext = jnp.concatenate([jnp.where(i > 0, prev_ref[...], 0.0), cur_ref[...]], axis=0)
        y_ref[...] = jax.nn.silu(_causal_conv(ext, w_ref[...], 4))

    r = tt // HALO
    return pl.pallas_call(
        body, name="conv_b_fwd", grid=(width // wdt, nt),
        in_specs=[pl.BlockSpec((tt, wdt), lambda j, i: (i, j)),
                  pl.BlockSpec((HALO, wdt), lambda j, i: (jnp.maximum(i * r - 1, 0), j)),
                  pl.BlockSpec((4, wdt), lambda j, i: (0, j))],
        out_specs=pl.BlockSpec((tt, wdt), lambda j, i: (i, j)),
        out_shape=jax.ShapeDtypeStruct((t, width), F32), compiler_params=_params("parallel", "parallel"))(
            proj_qkv, proj_qkv, conv_w)


def _conv_b_bwd(proj_qkv, conv_w, d_act, col, name):
    t = proj_qkv.shape[0]
    tt = min(t, 256)
    nt = t // tt
    wdt = 1024

    def body(cur_ref, prev_ref, nxt_ref, w_ref, da_ref, dan_ref, d_ref, dw_ref):
        i = pl.program_id(0)
        w = w_ref[...]
        u_ext = jnp.concatenate([jnp.where(i > 0, prev_ref[...], 0.0), cur_ref[...], nxt_ref[...]], axis=0)
        taps = [_shift_down(u_ext, 3 - j)[HALO:, :] for j in range(4)]
        conv = w[0:1, :] * taps[0] + w[1:2, :] * taps[1] + w[2:3, :] * taps[2] + w[3:4, :] * taps[3]
        da_cn = jnp.concatenate([da_ref[...], jnp.where(i < nt - 1, dan_ref[...], 0.0)], axis=0)
        sg = jax.nn.sigmoid(conv)
        d_conv = da_cn * (sg * (1.0 + conv * (1.0 - sg)))
        du = None
        for j in range(4):
            term = w[j:j + 1, :] * _shift_up(d_conv, 3 - j)
            du = term if du is None else du + term
        d_ref[...] = du[:tt, :].astype(BF16)

        @pl.when(i == 0)
        def _():
            dw_ref[...] = jnp.zeros_like(dw_ref)

        d_cur = d_conv[:tt, :]
        rows = [jnp.sum(d_cur * taps[j][:tt, :], axis=0, keepdims=True) for j in range(4)]
        dw_ref[0:4, :] += jnp.concatenate(rows, axis=0)

    prev_spec, nxt_spec = _halo_specs(tt, wdt, col, nt)
    _, dan_spec = _halo_specs(tt, wdt, 0, nt)
    return pl.pallas_call(
        body, name=name, grid=(nt,),
        in_specs=[pl.BlockSpec((tt, wdt), lambda i: (i, col)), prev_spec, nxt_spec,
                  pl.BlockSpec((4, wdt), lambda i: (0, col)), pl.BlockSpec((tt, wdt), lambda i: (i, 0)), dan_spec],
        out_specs=[pl.BlockSpec((tt, wdt), lambda i: (i, 0)), pl.BlockSpec((SUBLANES, wdt), lambda i: (0, 0))],
        out_shape=[jax.ShapeDtypeStruct((t, wdt), BF16), jax.ShapeDtypeStruct((SUBLANES, wdt), F32)],
        compiler_params=_params("arbitrary"))(proj_qkv, proj_qkv, proj_qkv, conv_w, d_act, d_act)


def _rms_gate(o, gn, z):
    on = o * lax.rsqrt(jnp.mean(o * o, axis=-1, keepdims=True) + NORM_EPS) * gn
    return on * jax.nn.silu(z)


GDN_PREP_ROWS = 1024


def _unit_lower_inverse(low):
    c = low.shape[-1]
    eye = lax.broadcasted_iota(jnp.int32, low.shape, low.ndim - 2) == lax.broadcasted_iota(jnp.int32, low.shape, low.ndim - 1)
    x = -low
    inv = eye.astype(F32) + x
    for _ in range(c.bit_length() - 2):
        x = _mm_raw(x, x, "nn", THREE_PASS)
        inv = inv + _mm_raw(inv, x, "nn", THREE_PASS)
    return inv


@jax.custom_vjp
def _known_inverse(low, inv):
    return inv


def _known_inverse_fwd(low, inv):
    return inv, inv


def _known_inverse_bwd(inv, d_inv):
    return -_mm_raw(_mm_raw(inv, d_inv, "tn", THREE_PASS), inv, "nt", THREE_PASS), jnp.zeros_like(inv)


_known_inverse.defvjp(_known_inverse_fwd, _known_inverse_bwd)


def _gdn_prep(mm, qa, ka, va, braw, araw, alog, dtb, inv_kept=None):
    n, c, _ = qa.shape
    q = qa * lax.rsqrt(jnp.sum(qa * qa, axis=-1, keepdims=True) + 1e-6) * (HEAD_DIM ** -0.5)
    k = ka * lax.rsqrt(jnp.sum(ka * ka, axis=-1, keepdims=True) + 1e-6)
    beta = jax.nn.sigmoid(braw)
    g = -jnp.exp(alog) * jax.nn.softplus(araw + dtb)
    ri = lax.broadcasted_iota(jnp.int32, (n, c, c), 1)
    ci = lax.broadcasted_iota(jnp.int32, (n, c, c), 2)
    incl, strict, eye = ri >= ci, ri > ci, ri == ci
    gc = mm(incl.astype(F32), g, "nn", EXACT_LHS)
    gc_i = gc[:, :, :c]
    gc_j = mm(jnp.ones((n, c, c), F32), jnp.where(eye, gc_i, 0.0), "nn", EXACT_LHS)
    decay = jnp.where(incl, jnp.exp(jnp.where(incl, gc_i - gc_j, 0.0)), 0.0)
    kb = k * beta
    low = jnp.where(strict, mm(kb, k, "nt", ONE_PASS) * decay, 0.0)
    inv = _unit_lower_inverse(low) if inv_kept is None else _known_inverse(low, inv_kept)
    egc = jnp.exp(gc)
    u = mm(inv, va * beta, "nn", THREE_PASS)
    w = mm(inv, kb * egc, "nn", THREE_PASS)
    attn = jnp.where(incl, mm(q, k, "nt", ONE_PASS) * decay, 0.0)
    g_last = jnp.sum(g, axis=1, keepdims=True)
    outs = (u, w, q * egc, k * jnp.exp(g_last - gc), attn, jnp.exp(g_last))
    return outs + (inv,) if inv_kept is None else outs


def _gdn_scan(mm, u, w, qd, kd, attn, egl, z, gn, state):
    v_new = u - mm(w, state, "nn", ONE_PASS)
    o = mm(qd, state, "nn", ONE_PASS) + mm(attn, v_new, "nn", ONE_PASS)
    new_state = state * egl + mm(kd, v_new, "tn", ONE_PASS)
    return _rms_gate(o, gn, z), new_state


def _chunks(ref_value, n, c):
    return ref_value.reshape(n, c, ref_value.shape[-1])


def _by_head(ref, rows, heads):
    return jnp.stack([ref[rows, pl.ds(h * HEAD_DIM, HEAD_DIM)] for h in range(heads)])


def _store_heads(ref, rows, value):
    for h in range(value.shape[0]):
        ref[rows, pl.ds(h * HEAD_DIM, HEAD_DIM)] = value[h]


def _gdn_prep_specs(tb):
    col = lambda off: pl.BlockSpec((tb, HEAD_DIM), lambda h, i: (i, off + h))
    rep = pl.BlockSpec((1, tb, LANES), lambda h, i: (h, i, 0))
    par = pl.BlockSpec((1, SUBLANES, LANES), lambda h, i: (h, 0, 0))
    att = pl.BlockSpec((1, tb, GDN_CHUNK), lambda h, i: (h, i, 0))
    egl = pl.BlockSpec((1, tb // GDN_CHUNK, SUBLANES, LANES), lambda h, i: (h, i, 0, 0))
    return col, rep, par, att, egl


def _head_columns(tail):
    lane = lax.broadcasted_iota(jnp.int32, tail.shape, 1)
    h = pl.program_id(0)
    return _lane_total(jnp.where(lane == h, tail, 0.0)), _lane_total(jnp.where(lane == h + GDN_HEADS, tail, 0.0))


def _gdn_prep_fwd(qkv_act, tail, alog, dtb):
    t = qkv_act.shape[0]
    tb = min(t, 2 * GDN_PREP_ROWS)
    nt, nc = t // tb, tb // GDN_CHUNK
    width = GDN_HEADS * HEAD_DIM

    def body(q_ref, k_ref, v_ref, tail_ref, al_ref, dt_ref, u_ref, w_ref, qd_ref, kd_ref, at_ref, eg_ref, inv_ref):
        ch = lambda r: _chunks(r, nc, GDN_CHUNK)
        braw, araw = _head_columns(tail_ref[...])
        u, w, qd, kd, attn, egl, inv = _gdn_prep(_mm_raw, ch(q_ref[...]), ch(k_ref[...]), ch(v_ref[...]), ch(braw),
                                                 ch(araw), al_ref[0, 0:1, :], dt_ref[0, 0:1, :])
        u_ref[...] = u.reshape(tb, HEAD_DIM)
        w_ref[...] = w.reshape(tb, HEAD_DIM).astype(BF16)
        qd_ref[...] = qd.reshape(tb, HEAD_DIM).astype(BF16)
        kd_ref[...] = kd.reshape(tb, HEAD_DIM).astype(BF16)
        at_ref[0] = attn.reshape(tb, GDN_CHUNK).astype(BF16)
        eg_ref[0] = jnp.broadcast_to(egl, (nc, SUBLANES, LANES))
        inv_ref[0] = inv.reshape(tb, GDN_CHUNK)

    col, rep, par, att, egl = _gdn_prep_specs(tb)
    h = GDN_HEADS
    return pl.pallas_call(
        body, name="gdn_prep_fwd", grid=(h, nt),
        in_specs=[col(0), col(h), col(2 * h), pl.BlockSpec((tb, LANES), lambda h_, i: (i, 0)), par, par],
        out_specs=[col(0), col(0), col(0), col(0), att, egl, att],
        out_shape=[jax.ShapeDtypeStruct((t, width), F32)] + [jax.ShapeDtypeStruct((t, width), BF16)] * 3
        + [jax.ShapeDtypeStruct((h, t, GDN_CHUNK), BF16), jax.ShapeDtypeStruct((h, t // GDN_CHUNK, SUBLANES, LANES), F32),
           jax.ShapeDtypeStruct((h, t, GDN_CHUNK), F32)],
        compiler_params=_params("parallel", "parallel"))(qkv_act, qkv_act, qkv_act, tail, alog, dtb)


def _gdn_prep_bwd(qkv_act, tail, alog, dtb, inv, du, dw, dqd, dkd, dattn, degl):
    t = qkv_act.shape[0]
    tb = min(t, 2 * GDN_PREP_ROWS)
    nt, nc = t // tb, tb // GDN_CHUNK
    width = GDN_HEADS * HEAD_DIM

    def body(q_ref, k_ref, v_ref, tail_ref, al_ref, dt_ref, inv_ref, du_ref, dw_ref, dqd_ref, dkd_ref, dat_ref,
             deg_ref, dq_ref, dk_ref, dv_ref, dbr_ref, dar_ref, dal_ref, ddt_ref):
        @pl.when(pl.program_id(1) == 0)
        def _():
            dal_ref[...] = jnp.zeros_like(dal_ref)
            ddt_ref[...] = jnp.zeros_like(ddt_ref)

        ch = lambda r: _chunks(r, nc, GDN_CHUNK)
        braw, araw = _head_columns(tail_ref[...])
        _, vjp = jax.vjp(functools.partial(_gdn_prep, _mm_vjp, inv_kept=ch(inv_ref[0])), ch(q_ref[...]), ch(k_ref[...]),
                         ch(v_ref[...]), ch(braw), ch(araw), al_ref[0, 0:1, :], dt_ref[0, 0:1, :])
        dq, dk, dv, dbr, dar, dal, ddt = vjp((ch(du_ref[...]), ch(dw_ref[...]), ch(dqd_ref[...]), ch(dkd_ref[...]),
                                              ch(dat_ref[0]), deg_ref[0][:, 0:1, :]))
        dq_ref[...] = dq.reshape(tb, HEAD_DIM)
        dk_ref[...] = dk.reshape(tb, HEAD_DIM)
        dv_ref[...] = dv.reshape(tb, HEAD_DIM)
        dbr_ref[0] = _lane_total(dbr.reshape(tb, LANES))
        dar_ref[0] = _lane_total(dar.reshape(tb, LANES))
        dal_ref[0, 0:1, :] += _lane_total(dal)
        ddt_ref[0, 0:1, :] += _lane_total(ddt)

    col, rep, par, att, egl = _gdn_prep_specs(tb)
    h = GDN_HEADS
    return pl.pallas_call(
        body, name="gdn_prep_bwd", grid=(h, nt),
        in_specs=[col(0), col(h), col(2 * h), pl.BlockSpec((tb, LANES), lambda h_, i: (i, 0)), par, par, att,
                  col(0), col(0), col(0), col(0), att, egl],
        out_specs=[col(0), col(0), col(0), rep, rep, par, par],
        out_shape=[jax.ShapeDtypeStruct((t, width), F32)] * 3 + [jax.ShapeDtypeStruct((h, t, LANES), F32)] * 2
        + [jax.ShapeDtypeStruct((h, SUBLANES, LANES), F32)] * 2,
        compiler_params=_params("parallel", "arbitrary"))(qkv_act, qkv_act, qkv_act, tail, alog, dtb, inv,
                                                         du, dw, dqd, dkd, dattn, degl)


def _scan_specs(tb, heads, chunk, rev, nt):
    ti = (lambda i: nt - 1 - i) if rev else (lambda i: i)
    row = pl.BlockSpec((tb, heads * HEAD_DIM), lambda i: (ti(i), 0))
    att = pl.BlockSpec((heads, tb, chunk), lambda i: (0, ti(i), 0))
    egl = pl.BlockSpec((heads, tb // chunk, SUBLANES, LANES), lambda i: (0, ti(i), 0, 0))
    hist = pl.BlockSpec((heads, tb // chunk, HEAD_DIM, HEAD_DIM), lambda i: (0, ti(i), 0, 0))
    gn = pl.BlockSpec((SUBLANES, LANES), lambda i: (0, 0))
    return row, att, egl, hist, gn


def _gdn_scan_fwd(u, w, qd, kd, attn, egl, zb, gn):
    t = u.shape[0]
    tb = min(t, 256)
    nt, nc = t // tb, tb // GDN_CHUNK
    nh = GDN_HEADS

    def body(u_ref, w_ref, qd_ref, kd_ref, at_ref, eg_ref, z_ref, gn_ref, y_ref, hist_ref, s_ref):
        @pl.when(pl.program_id(0) == 0)
        def _():
            s_ref[...] = jnp.zeros_like(s_ref)

        g = gn_ref[0:1, :]
        state = s_ref[...]
        for c in range(nc):
            rows = pl.ds(c * GDN_CHUNK, GDN_CHUNK)
            heads = lambda r: _by_head(r, rows, nh)
            hist_ref[:, c] = state
            y, state = _gdn_scan(_mm_raw, heads(u_ref), heads(w_ref), heads(qd_ref), heads(kd_ref), at_ref[:, rows, :],
                                 eg_ref[:, c, 0:1, :], heads(z_ref), g, state)
            _store_heads(y_ref, rows, y.astype(BF16))
        s_ref[...] = state

    row, att, egs, hist, gns = _scan_specs(tb, nh, GDN_CHUNK, False, nt)
    return pl.pallas_call(
        body, name="gdn_scan_fwd", grid=(nt,), in_specs=[row, row, row, row, att, egs, row, gns], out_specs=[row, hist],
        out_shape=[jax.ShapeDtypeStruct((t, nh * HEAD_DIM), BF16),
                   jax.ShapeDtypeStruct((nh, t // GDN_CHUNK, HEAD_DIM, HEAD_DIM), F32)],
        scratch_shapes=[pltpu.VMEM((nh, HEAD_DIM, HEAD_DIM), F32)],
        compiler_params=_params("arbitrary"))(u, w, qd, kd, attn, egl, zb, gn)


def _gdn_scan_bwd(u, w, qd, kd, attn, egl, zb, gn, hist, dy):
    t = u.shape[0]
    tb = min(t, 256)
    nt, nc = t // tb, tb // GDN_CHUNK
    nh = GDN_HEADS

    def body(u_ref, w_ref, qd_ref, kd_ref, at_ref, eg_ref, z_ref, gn_ref, hist_ref, dy_ref,
             du_ref, dw_ref, dqd_ref, dkd_ref, dat_ref, deg_ref, dz_ref, dgn_ref, ds_ref):
        @pl.when(pl.program_id(0) == 0)
        def _():
            ds_ref[...] = jnp.zeros_like(ds_ref)
            dgn_ref[...] = jnp.zeros_like(dgn_ref)

        g = gn_ref[0:1, :]
        d_state = ds_ref[...]
        for c in reversed(range(nc)):
            rows = pl.ds(c * GDN_CHUNK, GDN_CHUNK)
            heads = lambda r: _by_head(r, rows, nh).astype(F32)
            _, vjp = jax.vjp(functools.partial(_gdn_scan, _mm_vjp), heads(u_ref), heads(w_ref), heads(qd_ref),
                             heads(kd_ref), at_ref[:, rows, :].astype(F32), eg_ref[:, c, 0:1, :], heads(z_ref), g,
                             hist_ref[:, c])
            du, dw, dqd, dkd, dat, deg, dz, dgn, d_state = vjp((heads(dy_ref), d_state))
            _store_heads(du_ref, rows, du)
            _store_heads(dw_ref, rows, dw)
            _store_heads(dqd_ref, rows, dqd)
            _store_heads(dkd_ref, rows, dkd)
            dat_ref[:, rows, :] = dat
            deg_ref[:, c] = jnp.broadcast_to(deg, (nh, SUBLANES, LANES))
            _store_heads(dz_ref, rows, dz.astype(BF16))
            dgn_ref[0:1, :] += dgn
        ds_ref[...] = d_state

    row, att, egs, hists, gns = _scan_specs(tb, nh, GDN_CHUNK, True, nt)
    wide = jax.ShapeDtypeStruct((t, nh * HEAD_DIM), F32)
    return pl.pallas_call(
        body, name="gdn_scan_bwd", grid=(nt,),
        in_specs=[row, row, row, row, att, egs, row, gns, hists, row],
        out_specs=[row, row, row, row, att, egs, row, gns],
        out_shape=[wide] * 4 + [jax.ShapeDtypeStruct((nh, t, GDN_CHUNK), F32),
                                jax.ShapeDtypeStruct((nh, t // GDN_CHUNK, SUBLANES, LANES), F32),
                                jax.ShapeDtypeStruct((t, nh * HEAD_DIM), BF16),
                                jax.ShapeDtypeStruct((SUBLANES, LANES), F32)],
        scratch_shapes=[pltpu.VMEM((nh, HEAD_DIM, HEAD_DIM), F32)],
        compiler_params=_params("arbitrary"))(u, w, qd, kd, attn, egl, zb, gn, hist, dy)


def _hgrn_prep(mm, qr, fr, lbl):
    n, c, _ = qr.shape
    lb = jax.nn.sigmoid(lbl[1:2, :] - lbl[0:1, :])
    f = lb + (1.0 - lb) * jax.nn.sigmoid(fr)
    q = jax.nn.silu(qr)
    k = 1.0 - f
    logf = jnp.log(f)
    ri = lax.broadcasted_iota(jnp.int32, (n, c, c), 1)
    ci = lax.broadcasted_iota(jnp.int32, (n, c, c), 2)
    b = mm((ri >= ci).astype(F32), logf, "nn", EXACT_LHS)
    attn = _hgrn_attn(mm, q, k, b)
    b_last = jnp.sum(logf, axis=1, keepdims=True)
    return q * jnp.exp(b), k * jnp.exp(b_last - b), attn, jnp.exp(b_last)


HGRN_SUB = 8
HGRN_PREP_ROWS = 2048


@functools.partial(jax.custom_vjp, nondiff_argnums=(1,))
def _roll_rows(x, shift):
    return pltpu.roll(x, shift, x.ndim - 2)


def _roll_rows_fwd(x, shift):
    return _roll_rows(x, shift), None


def _roll_rows_bwd(shift, _, d):
    return (pltpu.roll(d, d.shape[-2] - shift, d.ndim - 2),)


_roll_rows.defvjp(_roll_rows_fwd, _roll_rows_bwd)


@jax.custom_vjp
def _exp_clamped(v):
    return jnp.exp(jnp.minimum(v, 0.0))


def _exp_clamped_fwd(v):
    out = jnp.exp(jnp.minimum(v, 0.0))
    return out, out


def _exp_clamped_bwd(out, d):
    return (d * out,)


_exp_clamped.defvjp(_exp_clamped_fwd, _exp_clamped_bwd)


def _hgrn_attn(mm, q, k, b):
    n, c, d = q.shape
    sb = HGRN_SUB
    sub = lambda a: a.reshape(n * c // sb, sb, d)
    qs, ks, bs = sub(q), sub(k), sub(b)
    row = lax.broadcasted_iota(jnp.int32, (n, c, c), 1)
    col = lax.broadcasted_iota(jnp.int32, (n, c, c), 2)
    same_block = (row & -sb) == (col & -sb)
    attn = None
    for delta in range(sb):
        if delta == 0:
            prod = qs * ks
        else:
            prod = qs * _roll_rows(ks, delta) * _exp_clamped(bs - _roll_rows(bs, delta))
        sums = jnp.sum(prod, axis=-1, keepdims=True).reshape(n, c, 1)
        term = jnp.where(same_block & (row - col == delta), sums, 0.0)
        attn = term if attn is None else attn + term
    far = [jnp.zeros((n, sb, c), F32)]
    for i in range(1, c // sb):
        r0 = i * sb
        bi = b[:, r0:r0 + sb, :]
        ref = bi[:, 0:1, :]
        before = jnp.concatenate([k[:, :r0, :] * jnp.exp(ref - b[:, :r0, :]), jnp.zeros((n, c - r0, d), F32)], axis=1)
        far.append(mm(q[:, r0:r0 + sb, :] * jnp.exp(bi - ref), before, "nt", ONE_PASS))
    return attn + jnp.concatenate(far, axis=1)


def _hgrn_scan(mm, qe, kd, attn, ebl, iv, z, gn, state):
    o = mm(qe, state, "nt", ONE_PASS) + mm(attn, iv, "nn", ONE_PASS)
    new_state = state * ebl + mm(iv, kd, "tn", ONE_PASS)
    return _rms_gate(o, gn, z), new_state


def _hgrn_prep_specs(tb):
    col = pl.BlockSpec((tb, HEAD_DIM), lambda h, i: (i, h))
    lbs = pl.BlockSpec((2, HEAD_DIM), lambda h, i: (0, h))
    att = pl.BlockSpec((1, tb, HGRN_CHUNK), lambda h, i: (h, i, 0))
    ebl = pl.BlockSpec((1, tb // HGRN_CHUNK, SUBLANES, LANES), lambda h, i: (h, i, 0, 0))
    return col, lbs, att, ebl


def _hgrn_prep_fwd(qr, fr, lower_bounds):
    t = qr.shape[0]
    tb = min(t, HGRN_PREP_ROWS)
    nt, nc = t // tb, tb // HGRN_CHUNK
    hh = HGRN_HEADS

    def body(q_ref, f_ref, lb_ref, qe_ref, kd_ref, at_ref, eb_ref):
        ch = lambda r: _chunks(r, nc, HGRN_CHUNK)
        qe, kd, attn, ebl = _hgrn_prep(_mm_raw, ch(q_ref[...]), ch(f_ref[...]), lb_ref[...])
        qe_ref[...] = qe.reshape(tb, HEAD_DIM).astype(BF16)
        kd_ref[...] = kd.reshape(tb, HEAD_DIM).astype(BF16)
        at_ref[0] = attn.reshape(tb, HGRN_CHUNK).astype(BF16)
        eb_ref[0] = jnp.broadcast_to(ebl, (nc, SUBLANES, LANES))

    col, lbs, att, ebs = _hgrn_prep_specs(tb)
    return pl.pallas_call(
        body, name="hgrn_prep_fwd", grid=(hh, nt), in_specs=[col, col, lbs], out_specs=[col, col, att, ebs],
        out_shape=[jax.ShapeDtypeStruct((t, HGRN_WIDTH), BF16)] * 2
        + [jax.ShapeDtypeStruct((hh, t, HGRN_CHUNK), BF16), jax.ShapeDtypeStruct((hh, t // HGRN_CHUNK, SUBLANES, LANES), F32)],
        compiler_params=_params("parallel", "parallel"))(qr, fr, lower_bounds)


def _hgrn_prep_bwd(qr, fr, lower_bounds, dqe, dkd, dattn, debl):
    t = qr.shape[0]
    tb = min(t, HGRN_PREP_ROWS)
    nt, nc = t // tb, tb // HGRN_CHUNK
    hh = HGRN_HEADS

    def body(q_ref, f_ref, lb_ref, dqe_ref, dkd_ref, dat_ref, deb_ref, dq_ref, df_ref, dlb_ref):
        @pl.when(pl.program_id(1) == 0)
        def _():
            dlb_ref[...] = jnp.zeros_like(dlb_ref)

        ch = lambda r: _chunks(r, nc, HGRN_CHUNK)
        _, vjp = jax.vjp(functools.partial(_hgrn_prep, _mm_vjp), ch(q_ref[...]), ch(f_ref[...]), lb_ref[...])
        dq, df, dlb = vjp((ch(dqe_ref[...]), ch(dkd_ref[...]), ch(dat_ref[0]), deb_ref[0][:, 0:1, :]))
        dq_ref[...] = dq.reshape(tb, HEAD_DIM).astype(BF16)
        df_ref[...] = df.reshape(tb, HEAD_DIM).astype(BF16)
        dlb_ref[...] += dlb

    col, lbs, att, ebs = _hgrn_prep_specs(tb)
    return pl.pallas_call(
        body, name="hgrn_prep_bwd", grid=(hh, nt), in_specs=[col, col, lbs, col, col, att, ebs],
        out_specs=[col, col, lbs],
        out_shape=[jax.ShapeDtypeStruct((t, HGRN_WIDTH), BF16)] * 2 + [jax.ShapeDtypeStruct((2, HGRN_WIDTH), F32)],
        compiler_params=_params("parallel", "arbitrary"))(qr, fr, lower_bounds, dqe, dkd, dattn, debl)


def _hgrn_scan_fwd(qe, kd, attn, ebl, iv, z, gn):
    t = qe.shape[0]
    tb = min(t, 128)
    nt, nc = t // tb, tb // HGRN_CHUNK
    hh = HGRN_HEADS

    def body(qe_ref, kd_ref, at_ref, eb_ref, i_ref, z_ref, gn_ref, y_ref, hist_ref, s_ref):
        @pl.when(pl.program_id(0) == 0)
        def _():
            s_ref[...] = jnp.zeros_like(s_ref)

        g = gn_ref[0:1, :]
        state = s_ref[...]
        for c in range(nc):
            rows = pl.ds(c * HGRN_CHUNK, HGRN_CHUNK)
            heads = lambda r: _by_head(r, rows, hh)
            hist_ref[:, c] = state
            y, state = _hgrn_scan(_mm_raw, heads(qe_ref), heads(kd_ref), at_ref[:, rows, :], eb_ref[:, c, 0:1, :],
                                  heads(i_ref), heads(z_ref), g, state)
            _store_heads(y_ref, rows, y.astype(BF16))
        s_ref[...] = state

    row, att, ebs, hist, gns = _scan_specs(tb, hh, HGRN_CHUNK, False, nt)
    return pl.pallas_call(
        body, name="hgrn_scan_fwd", grid=(nt,), in_specs=[row, row, att, ebs, row, row, gns], out_specs=[row, hist],
        out_shape=[jax.ShapeDtypeStruct((t, HGRN_WIDTH), BF16),
                   jax.ShapeDtypeStruct((hh, t // HGRN_CHUNK, HEAD_DIM, HEAD_DIM), F32)],
        scratch_shapes=[pltpu.VMEM((hh, HEAD_DIM, HEAD_DIM), F32)],
        compiler_params=_params("arbitrary"))(qe, kd, attn, ebl, iv, z, gn)


def _hgrn_scan_bwd(qe, kd, attn, ebl, iv, z, gn, hist, dy):
    t = qe.shape[0]
    tb = min(t, 128)
    nt, nc = t // tb, tb // HGRN_CHUNK
    hh = HGRN_HEADS

    def body(qe_ref, kd_ref, at_ref, eb_ref, i_ref, z_ref, gn_ref, hist_ref, dy_ref,
             dqe_ref, dkd_ref, dat_ref, deb_ref, di_ref, dz_ref, dgn_ref, ds_ref):
        @pl.when(pl.program_id(0) == 0)
        def _():
            ds_ref[...] = jnp.zeros_like(ds_ref)
            dgn_ref[...] = jnp.zeros_like(dgn_ref)

        g = gn_ref[0:1, :]
        d_state = ds_ref[...]
        for c in reversed(range(nc)):
            rows = pl.ds(c * HGRN_CHUNK, HGRN_CHUNK)
            heads = lambda r: _by_head(r, rows, hh).astype(F32)
            _, vjp = jax.vjp(functools.partial(_hgrn_scan, _mm_vjp), heads(qe_ref), heads(kd_ref),
                             at_ref[:, rows, :].astype(F32), eb_ref[:, c, 0:1, :], heads(i_ref), heads(z_ref), g,
                             hist_ref[:, c])
            dqe, dkd, dat, deb, di, dz, dgn, d_state = vjp((heads(dy_ref), d_state))
            _store_heads(dqe_ref, rows, dqe)
            _store_heads(dkd_ref, rows, dkd)
            dat_ref[:, rows, :] = dat
            deb_ref[:, c] = jnp.broadcast_to(deb, (hh, SUBLANES, LANES))
            _store_heads(di_ref, rows, di.astype(BF16))
            _store_heads(dz_ref, rows, dz.astype(BF16))
            dgn_ref[0:1, :] += dgn
        ds_ref[...] = d_state

    row, att, ebs, hists, gns = _scan_specs(tb, hh, HGRN_CHUNK, True, nt)
    wide = lambda dt: jax.ShapeDtypeStruct((t, HGRN_WIDTH), dt)
    return pl.pallas_call(
        body, name="hgrn_scan_bwd", grid=(nt,),
        in_specs=[row, row, att, ebs, row, row, gns, hists, row],
        out_specs=[row, row, att, ebs, row, row, gns],
        out_shape=[wide(F32), wide(F32), jax.ShapeDtypeStruct((hh, t, HGRN_CHUNK), F32),
                   jax.ShapeDtypeStruct((hh, t // HGRN_CHUNK, SUBLANES, LANES), F32), wide(BF16), wide(BF16),
                   jax.ShapeDtypeStruct((SUBLANES, LANES), F32)],
        scratch_shapes=[pltpu.VMEM((hh, HEAD_DIM, HEAD_DIM), F32)],
        compiler_params=_params("arbitrary"))(qe, kd, attn, ebl, iv, z, gn, hist, dy)


def _layer_norm(pre, g, b):
    mu = jnp.mean(pre, axis=-1, keepdims=True)
    d = pre - mu
    var = jnp.mean(d * d, axis=-1, keepdims=True)
    return d * lax.rsqrt(var + NORM_EPS) * g + b


def _lnpl_fwd(xin, s, p, wg, wpl, ln_g, ln_b):
    t = xin.shape[0]
    tt = min(t, 512)

    def body(x_ref, s_ref, p_ref, wg_ref, wpl_ref, g_ref, b_ref, o_ref, ob_ref):
        xn = _layer_norm(DEEPNORM_ALPHA * x_ref[...] + s_ref[...], g_ref[...], b_ref[...])
        gate = jax.nn.sigmoid(_mm_raw(xn, wg_ref[...], "nn", ONE_PASS))
        out = xn + _mm_raw(p_ref[...], wpl_ref[...], "nn", ONE_PASS) * gate
        o_ref[...] = out
        ob_ref[...] = out.astype(BF16)

    row = lambda w: pl.BlockSpec((tt, w), lambda i: (i, 0))
    full = lambda a: pl.BlockSpec(a.shape, lambda i: (0, 0))
    return pl.pallas_call(
        body, name="lnpl_fwd", grid=(t // tt,),
        in_specs=[row(D_MODEL), row(D_MODEL), row(PL_DIM), full(wg), full(wpl), full(ln_g), full(ln_b)],
        out_specs=[row(D_MODEL), row(D_MODEL)],
        out_shape=[jax.ShapeDtypeStruct((t, D_MODEL), F32), jax.ShapeDtypeStruct((t, D_MODEL), BF16)],
        compiler_params=_params("parallel"))(xin, s, p, wg, wpl, ln_g, ln_b)


def _lnpl_bwd(xin, s, p, wg, wpl, ln_g, ln_b, upstream, last, name):
    t = xin.shape[0]
    tt = min(t, 512)

    def body(x_ref, s_ref, p_ref, wg_ref, wpl_ref, g_ref, b_ref, up_ref,
             dpre_ref, dwg_ref, dwpl_ref, dg_ref, db_ref, loss_ref, dpre_b_ref):
        @pl.when(pl.program_id(0) == 0)
        def _():
            for r in (dwg_ref, dwpl_ref, dg_ref, db_ref, loss_ref):
                r[...] = jnp.zeros_like(r)

        pre = DEEPNORM_ALPHA * x_ref[...] + s_ref[...]
        xn, ln_vjp = jax.vjp(_layer_norm, pre, g_ref[...], b_ref[...])
        gate = jax.nn.sigmoid(_mm_raw(xn, wg_ref[...], "nn", ONE_PASS))
        plv = _mm_raw(p_ref[...], wpl_ref[...], "nn", ONE_PASS)
        if last:
            err = xn + plv * gate - up_ref[...]
            dout = err * (1.0 / D_MODEL)
            tot = jnp.sum(jnp.sum(err * err, axis=1, keepdims=True), axis=0, keepdims=True) * (0.5 / D_MODEL)
            loss_ref[...] += jnp.broadcast_to(tot, loss_ref.shape)
        else:
            dout = up_ref[...]
        dplv = dout * gate
        dlogits = dout * plv * gate * (1.0 - gate)
        dwg_ref[...] += _mm_raw(xn, dlogits, "tn", ONE_PASS)
        dwpl_ref[...] += _mm_raw(p_ref[...], dplv, "tn", ONE_PASS)
        dxn = dout + _mm_raw(dlogits, wg_ref[...], "nt", ONE_PASS)
        dpre, dg, db = ln_vjp(dxn)
        dpre_ref[...] = dpre
        dpre_b_ref[...] = dpre.astype(BF16)
        dg_ref[...] += dg
        db_ref[...] += db

    row = lambda w: pl.BlockSpec((tt, w), lambda i: (i, 0))
    full = lambda shape: pl.BlockSpec(shape, lambda i: (0, 0))
    return pl.pallas_call(
        body, name=name, grid=(t // tt,),
        in_specs=[row(D_MODEL), row(D_MODEL), row(PL_DIM), full(wg.shape), full(wpl.shape), full(ln_g.shape),
                  full(ln_b.shape), row(D_MODEL)],
        out_specs=[row(D_MODEL), full(wg.shape), full(wpl.shape), full(ln_g.shape), full(ln_b.shape),
                   full((SUBLANES, LANES)), row(D_MODEL)],
        out_shape=[jax.ShapeDtypeStruct((t, D_MODEL), F32), jax.ShapeDtypeStruct(wg.shape, F32),
                   jax.ShapeDtypeStruct(wpl.shape, F32), jax.ShapeDtypeStruct(ln_g.shape, F32),
                   jax.ShapeDtypeStruct(ln_b.shape, F32), jax.ShapeDtypeStruct((SUBLANES, LANES), F32),
                   jax.ShapeDtypeStruct((t, D_MODEL), BF16)],
        compiler_params=_params("arbitrary"))(xin, s, p, wg, wpl, ln_g, ln_b, upstream)


def _pack_tail(dbr, dar):
    nh, t, _ = dbr.shape
    tt = min(t, 512)

    def body(b_ref, a_ref, o_ref):
        lane = lax.broadcasted_iota(jnp.int32, (tt, LANES), 1)
        acc = jnp.zeros((tt, LANES), F32)
        for h in range(nh):
            acc = jnp.where(lane == h, b_ref[h], acc)
            acc = jnp.where(lane == nh + h, a_ref[h], acc)
        o_ref[...] = acc.astype(BF16)

    spec = pl.BlockSpec((nh, tt, LANES), lambda i: (0, i, 0))
    return pl.pallas_call(
        body, name="pack_tail", grid=(t // tt,), in_specs=[spec, spec], out_specs=pl.BlockSpec((tt, LANES), lambda i: (i, 0)),
        out_shape=jax.ShapeDtypeStruct((t, LANES), BF16), compiler_params=_params("parallel"))(dbr, dar)


def _rep_rows(v):
    return jnp.broadcast_to(v.reshape(1, LANES), (SUBLANES, LANES))


def _rep_heads(v):
    return jnp.broadcast_to(v.reshape(-1, 1, 1), (v.shape[0], SUBLANES, LANES))


def _col_range(stacked, lo, hi):
    c = stacked.shape[2]
    parts = [stacked[s, :, max(lo, s * c) - s * c:min(hi, (s + 1) * c) - s * c]
             for s in range(4) if max(lo, s * c) < min(hi, (s + 1) * c)]
    return parts[0] if len(parts) == 1 else jnp.concatenate(parts, axis=1)


def _col_shards(pieces, c):
    shards, offs, o = [], [], 0
    for pc in pieces:
        offs.append(o)
        o += pc.shape[1]
    for s in range(4):
        lo, hi = s * c, (s + 1) * c
        parts = [pc[:, max(lo, o) - o:min(hi, o + pc.shape[1]) - o] for pc, o in zip(pieces, offs)
                 if max(lo, o) < min(hi, o + pc.shape[1])]
        shards.append(parts[0] if len(parts) == 1 else jnp.concatenate(parts, axis=1))
    return jnp.stack(shards)


def _local_step(x, p, target, w, late_weights, early_grads_ready, early_grads_swapped, last_grad_ready, start_token):
    a = DEEPNORM_ALPHA
    nh = GDN_HEADS
    xb = (x + start_token).astype(BF16)
    wie = w["w_in_even"]
    w_a, w_qkv, w_zb = _col_range(wie, 0, 4096), _col_range(wie, 4096, 7168), _col_range(wie, 7168, 8192)
    w_tail = jnp.pad(_col_range(wie, 8192, 8192 + 2 * nh), ((0, 0), (0, LANES - 2 * nh)))
    conv_a_w, conv_b_w = w["conv_a_w"], w["conv_b_w"]
    ln_g0, ln_b0, ln_g1, ln_b1 = (v.reshape(1, D_MODEL) for v in (w["ln_g"][0], w["ln_b"][0], w["ln_g"][1], w["ln_b"][1]))
    alog, dtb = _rep_heads(w["a_log"].reshape(nh)), _rep_heads(w["dt_bias"].reshape(nh))
    gdn_g, hgrn_g = _rep_rows(w["gdn_norm_g"]), _rep_rows(w["hgrn_norm_g"])

    proj_a = _matmul(xb, w_a, name="fwd_proj_a")
    proj_qkv = _matmul(xb, w_qkv, name="fwd_proj_qkv")
    proj_zb = _matmul(xb, w_zb, name="fwd_proj_zb")
    proj_tail = _matmul(xb, w_tail, name="fwd_proj_tail")
    y_a = _conv_a_fwd(proj_a, conv_a_w)
    qkv_act = _conv_b_fwd(proj_qkv, conv_b_w)
    *gdn_pre, gdn_inv = _gdn_prep_fwd(qkv_act, proj_tail, alog, dtb)
    y_b, gdn_hist = _gdn_scan_fwd(*gdn_pre, proj_zb, gdn_g)
    w = {**w, **late_weights(y_b)}
    woe, wio, woo = w["w_out_even"], w["w_in_odd"], w["w_out_odd"]
    s0 = _matmul(y_b, woe[1024:], name="fwd_out_even_b", add=_matmul(y_a, woe[:1024], name="fwd_out_even_a"))
    x1, x1b = _lnpl_fwd(x, s0, p[0], w["w_pl_gate"][0], w["w_pl"][0], ln_g0, ln_b0)
    proj_o = [_matmul(x1b, wio[j], name=f"fwd_proj_odd{j}") for j in range(4)]
    hgrn_pre = _hgrn_prep_fwd(proj_o[0], proj_o[1], w["lower_bounds"])
    y_o, hgrn_hist = _hgrn_scan_fwd(*hgrn_pre, proj_o[2], proj_o[3], hgrn_g)
    s1 = _matmul(y_o, woo, name="fwd_out_odd")

    g = {}
    dpre1, dwg1, dwpl1, dlng1, dlnb1, loss, dpre1b = _lnpl_bwd(x1, s1, p[1], w["w_pl_gate"][1], w["w_pl"][1], ln_g1,
                                                             ln_b1, target, True, "lnpl_bwd_odd")
    dy_o = _matmul(dpre1b, woo, tb=True, name="bwd_out_odd_dx")
    g["w_out_odd"] = _matmul(y_o, dpre1b, ta=True, name="bwd_out_odd_dw")
    dqe, dkd, dat, deb, di, dz, dhg = _hgrn_scan_bwd(*hgrn_pre, proj_o[2], proj_o[3], hgrn_g, hgrn_hist, dy_o)
    dq, df, dlb = _hgrn_prep_bwd(proj_o[0], proj_o[1], w["lower_bounds"], dqe, dkd, dat, deb)
    dx1 = dpre1
    scale = a
    dws = []
    for j, dj in enumerate((dq, df, di, dz)):
        dx1 = _matmul(dj, wio[j], tb=True, add=dx1, add_scale=scale, name=f"bwd_proj_odd_dx{j}")
        scale = 1.0
        dws.append(_matmul(x1b, dj, ta=True, name=f"bwd_proj_odd_dw{j}"))
    g["w_in_odd"] = jnp.stack(dws)
    g["hgrn_norm_g"] = dhg[0:1]
    g["lower_bounds"] = dlb
    g["w_pl_gate1"], g["w_pl1"] = dwg1, dwpl1

    dpre0, dwg0, dwpl0, dlng0, dlnb0, _, dpre0b = _lnpl_bwd(x, s0, p[0], w["w_pl_gate"][0], w["w_pl"][0], ln_g0, ln_b0,
                                                          dx1, False, "lnpl_bwd_even")
    g["w_pl_gate0"], g["w_pl0"] = dwg0, dwpl0
    g["ln_g"] = jnp.concatenate([dlng0, dlng1], axis=0)
    g["ln_b"] = jnp.concatenate([dlnb0, dlnb1], axis=0)
    dy_a = _matmul(dpre0b, woe[:1024], tb=True, name="bwd_out_even_dxa")
    dy_b = _matmul(dpre0b, woe[1024:], tb=True, name="bwd_out_even_dxb")
    g["w_out_even"] = jnp.concatenate([_matmul(y_a, dpre0b, ta=True, name="bwd_out_even_dwa"),
                                       _matmul(y_b, dpre0b, ta=True, name="bwd_out_even_dwb")], axis=0)
    token = early_grads_ready({n: g[n] for n in _RS_EARLY})
    d_a, dwa = _conv_a_bwd(proj_a, conv_a_w + token, dy_a)
    g["conv_a_w"] = dwa[:3]
    gdn_g = gdn_g + early_grads_swapped(d_a)
    du, dw, dqd, dkd, dat, deg, dzb, dgn = _gdn_scan_bwd(*gdn_pre, proj_zb, gdn_g, gdn_hist, dy_b)
    dqa, dka, dva, dbr, dar, dal, ddt = _gdn_prep_bwd(qkv_act, proj_tail, alog, dtb, gdn_inv, du, dw, dqd, dkd, dat, deg)
    g["a_log"] = dal[:, 0, 0].reshape(1, nh)
    g["dt_bias"] = ddt[:, 0, 0].reshape(1, nh)
    g["gdn_norm_g"] = dgn[0:1]
    d_pre_qkv, dwb = [], []
    for j, dj in enumerate((dqa, dka, dva)):
        dpj, dwj = _conv_b_bwd(proj_qkv, conv_b_w, dj, j, f"conv_b_bwd{j}")
        d_pre_qkv.append(dpj)
        dwb.append(dwj[:4])
    g["conv_b_w"] = jnp.concatenate(dwb, axis=1)
    d_tail = _pack_tail(dbr, dar)
    pieces = [(d_a, w_a), (d_pre_qkv[0], w_qkv[:, :1024]), (d_pre_qkv[1], w_qkv[:, 1024:2048]),
              (d_pre_qkv[2], w_qkv[:, 2048:]), (dzb, w_zb), (d_tail, w_tail)]
    dws = [_matmul(xb, dj, ta=True, name=f"bwd_proj_even_dw{j}") for j, (dj, _) in enumerate(pieces)]
    dws[-1] = dws[-1][:, :2 * nh]
    g["w_in_even"] = _col_shards(dws, wie.shape[2])
    token = last_grad_ready(g["w_in_even"])
    dx = dpre0
    scale = a
    for j, (dj, wj) in enumerate(pieces):
        dx = _matmul(dj, wj + jnp.asarray(token).astype(BF16) if j == 0 else wj, tb=True, add=dx, add_scale=scale,
                     name=f"bwd_proj_even_dx{j}")
        scale = 1.0
    return loss, dx, g


def _adamw(w, g, m, v, name):
    lead, rows, cols = w.shape
    if rows % SUBLANES == 0 or rows <= 256:
        tr, tc = (rows if rows <= 256 else 256), cols
    else:
        tr, tc = rows, 256
    assert rows % tr == 0 and cols % tc == 0, (name, rows, cols)

    def body(w_ref, g_ref, m_ref, v_ref, d_ref, nm_ref, nv_ref):
        gg = g_ref[...]
        nm = ADAM_B1 * m_ref[...] + (1.0 - ADAM_B1) * gg
        nv = ADAM_B2 * v_ref[...] + (1.0 - ADAM_B2) * jnp.square(gg)
        m_hat = nm / (1.0 - ADAM_B1 ** ADAM_STEP)
        v_hat = nv / (1.0 - ADAM_B2 ** ADAM_STEP)
        d_ref[...] = -ADAM_LR * (m_hat / (jnp.sqrt(v_hat) + ADAM_EPS) + ADAM_WD * w_ref[...])
        nm_ref[...] = nm
        nv_ref[...] = nv

    spec = pl.BlockSpec((1, tr, tc), lambda l, i, j: (l, i, j))
    return pl.pallas_call(
        body, name=name, grid=(lead, rows // tr, cols // tc), in_specs=[spec] * 4, out_specs=[spec] * 3,
        out_shape=[jax.ShapeDtypeStruct(w.shape, F32)] * 3,
        compiler_params=_params("parallel", "parallel", "parallel"))(w, g, m, v)


MESH = pl.DeviceIdType.MESH
N_DEV = 8
HBM_SPEC = pl.BlockSpec(memory_space=pltpu.HBM)
VMEM_SPEC = pl.BlockSpec(memory_space=pltpu.VMEM)


def _coords():
    return lax.axis_index("x"), lax.axis_index("y"), lax.axis_index("c")


def _flip(v, bit):
    return 1 - v if bit else v


def _remote(src, dst, send_sem, recv_sem, dev):
    return pltpu.make_async_remote_copy(src_ref=src, dst_ref=dst, send_sem=send_sem, recv_sem=recv_sem,
                                        device_id=dev, device_id_type=MESH)


def _exchange_small(buf, reduce, name):
    rows = buf.shape[0]

    def body(in_ref, out_ref, slots, send_sems, recv_sems):
        x, y, c = _coords()
        me = 4 * x + 2 * y + c
        slots[me] = in_ref[...]
        peer = lambda k: (_flip(x, (k >> 2) & 1), _flip(y, (k >> 1) & 1), _flip(c, k & 1))
        sends = []
        for k in range(1, N_DEV):
            cp = _remote(in_ref, slots.at[me], send_sems.at[k - 1], recv_sems.at[k - 1], peer(k))
            cp.start()
            sends.append(cp)
        for k in range(1, N_DEV):
            px, py, pc = peer(k)
            _remote(in_ref, slots.at[4 * px + 2 * py + pc], send_sems.at[k - 1], recv_sems.at[k - 1], peer(k)).wait_recv()
        for cp in sends:
            cp.wait_send()
        if reduce:
            acc = slots[0]
            for d in range(1, N_DEV):
                acc = acc + slots[d]
            out_ref[...] = acc
        else:
            out_ref[...] = slots[...]

    out_shape = (rows, LANES) if reduce else (N_DEV, rows, LANES)
    return pl.pallas_call(
        body, name=name, in_specs=[VMEM_SPEC], out_specs=VMEM_SPEC, out_shape=jax.ShapeDtypeStruct(out_shape, F32),
        scratch_shapes=[pltpu.VMEM((N_DEV, rows, LANES), F32), pltpu.SemaphoreType.DMA((N_DEV - 1,)),
                        pltpu.SemaphoreType.DMA((N_DEV - 1,))])(buf)


def _half_rows(half, which):
    return pl.ds(pl.multiple_of(which * half, 16), half)


def _other_chip(x, y, k):
    return _flip(x, (k >> 1) & 1), _flip(y, k & 1)


SEM_SPEC = pl.BlockSpec(memory_space=pltpu.SEMAPHORE)
DATAFLOW = pltpu.SideEffectType.DATAFLOW_SIDE_EFFECTING


def _ici_piece(srcs, lands, send_sems, recv_sems, i, k, x, y, c):
    half = srcs[i].shape[0] // 2
    ox, oy = _other_chip(x, y, k)
    return _remote(srcs[i].at[_half_rows(half, c)], lands[i].at[2 * x + y, _half_rows(half, c)],
                   send_sems.at[3 * i + k - 1], recv_sems.at[3 * i + k - 1], (ox, oy, c)), (ox, oy)


def _gather_start(shards, name):
    n = len(shards)

    def body(*refs):
        srcs, lands = refs[:n], refs[n:2 * n]
        send_sems, recv_sems = refs[2 * n], refs[2 * n + 1]
        token = refs[-1]
        x, y, c = _coords()
        for i in range(n):
            for k in (1, 2, 3):
                _ici_piece(srcs, lands, send_sems, recv_sems, i, k, x, y, c)[0].start()
        token[...] = jnp.zeros_like(token)

    hbm = lambda a: pltpu.with_memory_space_constraint(a, pltpu.HBM)
    lands = [lax.empty((4,) + s.shape, s.dtype) for s in shards]
    out = pl.pallas_call(
        body, name=name,
        out_shape=(pltpu.SemaphoreType.DMA((3 * n,)), pltpu.SemaphoreType.DMA((3 * n,)),
                   *[pltpu.HBM(s.shape, s.dtype) for s in shards], *[pltpu.HBM(a.shape, a.dtype) for a in lands],
                   jax.ShapeDtypeStruct((SUBLANES, LANES), F32)),
        in_specs=[HBM_SPEC] * (2 * n), out_specs=(SEM_SPEC, SEM_SPEC, *[HBM_SPEC] * (2 * n), VMEM_SPEC),
        input_output_aliases={i: 2 + i for i in range(2 * n)},
        compiler_params=pltpu.CompilerParams(has_side_effects=DATAFLOW))(*[hbm(s) for s in shards], *[hbm(a) for a in lands])
    return out[0], out[1], out[2:2 + n], out[2 + n:2 + 2 * n], out[-1]


def _gather_wait(send_sems, recv_sems, srcs, lands, after, name):
    n = len(srcs)

    def body(*refs):
        src_refs, land_refs = refs[:n], refs[n:2 * n]
        send_sems, recv_sems = refs[2 * n], refs[2 * n + 1]
        x, y, c = _coords()
        for i in range(n):
            half = src_refs[i].shape[0] // 2
            for k in (1, 2, 3):
                cp, (ox, oy) = _ici_piece(src_refs, land_refs, send_sems, recv_sems, i, k, x, y, c)
                cp.wait_send()
                piece = land_refs[i].at[2 * ox + oy, _half_rows(half, c)]
                _remote(piece, piece, send_sems.at[3 * i + k - 1], recv_sems.at[3 * i + k - 1], (ox, oy, c)).wait_recv()

    out = pl.pallas_call(
        body, name=name,
        out_shape=(*[pltpu.HBM(s.shape, s.dtype) for s in srcs], *[pltpu.HBM(a.shape, a.dtype) for a in lands]),
        in_specs=[HBM_SPEC] * (2 * n) + [SEM_SPEC, SEM_SPEC, pl.BlockSpec(memory_space=pl.ANY)],
        out_specs=tuple([HBM_SPEC] * (2 * n)), input_output_aliases={i: i for i in range(2 * n)},
        compiler_params=pltpu.CompilerParams(has_side_effects=DATAFLOW))(*srcs, *lands, send_sems, recv_sems, after)
    return out[n:]


def _gather_forward(lands, name):
    n = len(lands)

    def body(*refs):
        ins, outs = refs[:n], refs[n:2 * n]
        send_sems, recv_sems = refs[2 * n:]
        x, y, c = _coords()
        sends = []
        for i in range(n):
            half = ins[i].shape[1] // 2
            for k in (1, 2, 3):
                ox, oy = _other_chip(x, y, k)
                cp = _remote(ins[i].at[2 * ox + oy, _half_rows(half, c)], outs[i].at[2 * ox + oy, _half_rows(half, c)],
                             send_sems.at[3 * i + k - 1], recv_sems.at[3 * i + k - 1], (x, y, 1 - c))
                cp.start()
                sends.append(cp)
        for i in range(n):
            half = ins[i].shape[1] // 2
            for k in (1, 2, 3):
                ox, oy = _other_chip(x, y, k)
                piece = outs[i].at[2 * ox + oy, _half_rows(half, 1 - c)]
                _remote(piece, piece, send_sems.at[3 * i + k - 1], recv_sems.at[3 * i + k - 1], (x, y, 1 - c)).wait_recv()
        for cp in sends:
            cp.wait_send()

    return pl.pallas_call(
        body, name=name, in_specs=[HBM_SPEC] * n, out_specs=[HBM_SPEC] * n,
        out_shape=[jax.ShapeDtypeStruct(a.shape, a.dtype) for a in lands],
        input_output_aliases={i: i for i in range(n)},
        scratch_shapes=[pltpu.SemaphoreType.DMA((3 * n,))] * 2)(*lands)


def _rs_sibling_swap(g4s, name):
    n = len(g4s)

    def body(*refs):
        ins, outs = refs[:n], refs[n:2 * n]
        send_sems, recv_sems = refs[2 * n:]
        x, y, c = _coords()
        sends = []
        for i in range(n):
            half = ins[i].shape[1] // 2
            for s in range(4):
                cp = _remote(ins[i].at[s, _half_rows(half, 1 - c)], outs[i].at[s], send_sems.at[4 * i + s],
                             recv_sems.at[4 * i + s], (x, y, 1 - c))
                cp.start()
                sends.append(cp)
        for cp in sends:
            cp.wait_recv()
        for cp in sends:
            cp.wait_send()

    return pl.pallas_call(
        body, name=name, in_specs=[HBM_SPEC] * n, out_specs=[HBM_SPEC] * n,
        out_shape=[jax.ShapeDtypeStruct((4, g.shape[1] // 2, g.shape[2]), g.dtype) for g in g4s],
        scratch_shapes=[pltpu.SemaphoreType.DMA((4 * n,))] * 2)(*g4s)


def _rs_add_sibling(g4, got, c_idx, name):
    _, rows, cols = g4.shape
    half = rows // 2
    tr = min(half, 256)
    nb = half // tr

    def body(c_ref, a_ref, b_ref, o_ref, ob_ref):
        total = a_ref[...] + b_ref[...]
        o_ref[...] = total
        ob_ref[...] = total.astype(BF16)

    blk = (1, tr, cols)
    out = pl.BlockSpec(blk, lambda s, i, c_ref: (s, i, 0))
    grid_spec = pltpu.PrefetchScalarGridSpec(
        num_scalar_prefetch=1, grid=(4, nb),
        in_specs=[pl.BlockSpec(blk, lambda s, i, c_ref: (s, c_ref[0] * nb + i, 0)), out],
        out_specs=[out, out])
    return pl.pallas_call(
        body, name=name, grid_spec=grid_spec,
        out_shape=[jax.ShapeDtypeStruct(got.shape, F32), jax.ShapeDtypeStruct(got.shape, BF16)],
        compiler_params=_params("parallel", "parallel"))(c_idx, g4, got)


def _rs_add_chips(p4, got3, idx, name):
    _, half, cols = p4.shape
    tr = min(half, 256)
    nb = half // tr

    def body(idx_ref, p_ref, a_ref, b_ref, c_ref, o_ref):
        o_ref[...] = ((p_ref[0] + a_ref[0].astype(F32)) + b_ref[0].astype(F32)) + c_ref[0].astype(F32)

    blk = (1, tr, cols)
    grid_spec = pltpu.PrefetchScalarGridSpec(
        num_scalar_prefetch=1, grid=(nb,),
        in_specs=[pl.BlockSpec(blk, lambda i, idx_ref: (idx_ref[0], i, 0))]
        + [pl.BlockSpec(blk, functools.partial(lambda k, i, idx_ref: (k, i, 0), k)) for k in range(3)],
        out_specs=pl.BlockSpec((tr, cols), lambda i, idx_ref: (idx_ref[1] * nb + i, 0)))
    return pl.pallas_call(body, name=name, grid_spec=grid_spec, out_shape=jax.ShapeDtypeStruct((2 * half, cols), F32),
                          compiler_params=_params("parallel"))(idx, p4, got3, got3, got3)


def _rs_share_halves(bufs, name):
    n = len(bufs)

    def body(*refs):
        ins, outs = refs[:n], refs[n:2 * n]
        send_sems, recv_sems = refs[2 * n:]
        x, y, c = _coords()
        sends = []
        for i in range(n):
            half = ins[i].shape[0] // 2
            cp = _remote(ins[i].at[_half_rows(half, c)], outs[i].at[_half_rows(half, c)], send_sems.at[i],
                         recv_sems.at[i], (x, y, 1 - c))
            cp.start()
            sends.append(cp)
        for i in range(n):
            half = ins[i].shape[0] // 2
            _remote(ins[i].at[_half_rows(half, c)], outs[i].at[_half_rows(half, 1 - c)], send_sems.at[i],
                    recv_sems.at[i], (x, y, 1 - c)).wait_recv()
        for cp in sends:
            cp.wait_send()

    return pl.pallas_call(
        body, name=name, in_specs=[HBM_SPEC] * n, out_specs=[HBM_SPEC] * n,
        out_shape=[jax.ShapeDtypeStruct(b.shape, b.dtype) for b in bufs],
        input_output_aliases={i: i for i in range(n)},
        scratch_shapes=[pltpu.SemaphoreType.DMA((n,))] * 2)(*bufs)


def _scatter_piece(srcs, lands, send_sems, recv_sems, i, k, x, y, c):
    ox, oy = _other_chip(x, y, k)
    return _remote(srcs[i].at[2 * ox + oy], lands[i].at[k - 1], send_sems.at[3 * i + k - 1],
                   recv_sems.at[3 * i + k - 1], (ox, oy, c))


def _rs_scatter_start(p4s, name):
    n = len(p4s)

    def body(*refs):
        srcs, lands = refs[:n], refs[n:2 * n]
        send_sems, recv_sems = refs[2 * n], refs[2 * n + 1]
        token = refs[-1]
        x, y, c = _coords()
        for i in range(n):
            for k in (1, 2, 3):
                _scatter_piece(srcs, lands, send_sems, recv_sems, i, k, x, y, c).start()
        token[...] = jnp.zeros_like(token)

    hbm = lambda a: pltpu.with_memory_space_constraint(a, pltpu.HBM)
    lands = [lax.empty((3,) + p.shape[1:], p.dtype) for p in p4s]
    out = pl.pallas_call(
        body, name=name,
        out_shape=(pltpu.SemaphoreType.DMA((3 * n,)), pltpu.SemaphoreType.DMA((3 * n,)),
                   *[pltpu.HBM(p.shape, p.dtype) for p in p4s], *[pltpu.HBM(a.shape, a.dtype) for a in lands],
                   jax.ShapeDtypeStruct((SUBLANES, LANES), F32)),
        in_specs=[HBM_SPEC] * (2 * n), out_specs=(SEM_SPEC, SEM_SPEC, *[HBM_SPEC] * (2 * n), VMEM_SPEC),
        input_output_aliases={i: 2 + i for i in range(2 * n)},
        compiler_params=pltpu.CompilerParams(has_side_effects=DATAFLOW))(*[hbm(p) for p in p4s], *[hbm(a) for a in lands])
    return out[0], out[1], out[2:2 + n], out[2 + n:2 + 2 * n], out[-1]


def _rs_scatter_wait(send_sems, recv_sems, srcs, lands, after, name):
    n = len(srcs)

    def body(*refs):
        src_refs, land_refs = refs[:n], refs[n:2 * n]
        send_sems, recv_sems = refs[2 * n], refs[2 * n + 1]
        x, y, c = _coords()
        for i in range(n):
            for k in (1, 2, 3):
                cp = _scatter_piece(src_refs, land_refs, send_sems, recv_sems, i, k, x, y, c)
                cp.wait_send()
                cp.wait_recv()

    out = pl.pallas_call(
        body, name=name,
        out_shape=(*[pltpu.HBM(s.shape, s.dtype) for s in srcs], *[pltpu.HBM(a.shape, a.dtype) for a in lands]),
        in_specs=[HBM_SPEC] * (2 * n) + [SEM_SPEC, SEM_SPEC, pl.BlockSpec(memory_space=pl.ANY)],
        out_specs=tuple([HBM_SPEC] * (2 * n)), input_output_aliases={i: i for i in range(2 * n)},
        compiler_params=pltpu.CompilerParams(has_side_effects=DATAFLOW))(*srcs, *lands, send_sems, recv_sems, after)
    return out[n:]


def _swap_piece(srcs, lands, send_sems, recv_sems, i, s, x, y, c):
    half = srcs[i].shape[1] // 2
    return _remote(srcs[i].at[s, _half_rows(half, 1 - c)], lands[i].at[s], send_sems.at[4 * i + s],
                   recv_sems.at[4 * i + s], (x, y, 1 - c))


def _rs_swap_start(g4s, name):
    n = len(g4s)

    def body(*refs):
        srcs, lands = refs[:n], refs[n:2 * n]
        send_sems, recv_sems = refs[2 * n], refs[2 * n + 1]
        token = refs[-1]
        x, y, c = _coords()
        for i in range(n):
            for s in range(4):
                _swap_piece(srcs, lands, send_sems, recv_sems, i, s, x, y, c).start()
        token[...] = jnp.zeros_like(token)

    hbm = lambda a: pltpu.with_memory_space_constraint(a, pltpu.HBM)
    lands = [lax.empty((4, g.shape[1] // 2, g.shape[2]), g.dtype) for g in g4s]
    out = pl.pallas_call(
        body, name=name,
        out_shape=(pltpu.SemaphoreType.DMA((4 * n,)), pltpu.SemaphoreType.DMA((4 * n,)),
                   *[pltpu.HBM(g.shape, g.dtype) for g in g4s], *[pltpu.HBM(a.shape, a.dtype) for a in lands],
                   jax.ShapeDtypeStruct((SUBLANES, LANES), F32)),
        in_specs=[HBM_SPEC] * (2 * n), out_specs=(SEM_SPEC, SEM_SPEC, *[HBM_SPEC] * (2 * n), VMEM_SPEC),
        input_output_aliases={i: 2 + i for i in range(2 * n)},
        compiler_params=pltpu.CompilerParams(has_side_effects=DATAFLOW))(*[hbm(g) for g in g4s], *[hbm(a) for a in lands])
    return out[0], out[1], out[2:2 + n], out[2 + n:2 + 2 * n], out[-1]


def _rs_swap_wait(send_sems, recv_sems, srcs, lands, after, name):
    n = len(srcs)

    def body(*refs):
        src_refs, land_refs = refs[:n], refs[n:2 * n]
        send_sems, recv_sems = refs[2 * n], refs[2 * n + 1]
        x, y, c = _coords()
        for i in range(n):
            for s in range(4):
                cp = _swap_piece(src_refs, land_refs, send_sems, recv_sems, i, s, x, y, c)
                cp.wait_send()
                cp.wait_recv()

    out = pl.pallas_call(
        body, name=name,
        out_shape=(*[pltpu.HBM(s.shape, s.dtype) for s in srcs], *[pltpu.HBM(a.shape, a.dtype) for a in lands]),
        in_specs=[HBM_SPEC] * (2 * n) + [SEM_SPEC, SEM_SPEC, pl.BlockSpec(memory_space=pl.ANY)],
        out_specs=tuple([HBM_SPEC] * (2 * n)), input_output_aliases={i: i for i in range(2 * n)},
        compiler_params=pltpu.CompilerParams(has_side_effects=DATAFLOW))(*srcs, *lands, send_sems, recv_sems, after)
    return out[:n], out[n:]


def _rs_front(g4s, names, tag):
    c_idx = jnp.stack([lax.axis_index("c")]).astype(jnp.int32)
    got = _rs_sibling_swap(g4s, f"rs_sibling_swap_{tag}")
    return [_rs_add_sibling(g, s, c_idx, f"rs_add_sibling_{nm}") for g, s, nm in zip(g4s, got, names)]


def _rs_back(p4s, got3, names):
    x, y, c = _coords()
    idx = jnp.stack([2 * x + y, c]).astype(jnp.int32)
    return [_rs_add_chips(p, t, idx, f"rs_add_chips_{nm}") for (p, _), t, nm in zip(p4s, got3, names)]


def _cols_split(full):
    r, c4 = full.shape
    return full.reshape(r, 4, c4 // 4).transpose(1, 0, 2)


_BIG = {
    "w_in_even": ((1024, 2052), lambda s: s),
    "w_out_even": ((512, 1024), lambda s: s.reshape(2048, 1024)),
    "w_in_odd": ((1024, 2048), lambda s: s),
    "w_out_odd": ((512, 1024), lambda s: s.reshape(2048, 1024)),
    "w_pl": ((512, 256), lambda s: s.reshape(4, 2, 256, 256).transpose(1, 2, 0, 3).reshape(2, 256, 1024)),
    "w_pl_gate": ((512, 1024), lambda s: s.reshape(4, 2, 256, 1024).transpose(1, 0, 2, 3).reshape(2, 1024, 1024)),
}


_RS_EARLY = {
    "w_in_odd": lambda f: f,
    "w_out_odd": lambda f: f.reshape(4, 512, 1024),
    "w_pl_gate1": lambda f: f.reshape(4, 256, 1024),
    "w_pl1": _cols_split,
    "w_out_even": lambda f: f.reshape(4, 512, 1024),
    "w_pl_gate0": lambda f: f.reshape(4, 256, 1024),
    "w_pl0": _cols_split,
}
_RS_LATE = {"w_in_even": lambda f: f}


def _size(shape):
    n = 1
    for d in shape:
        n *= d
    return n


_SMALL = {"a_log": (1, 8), "dt_bias": (1, 8), "gdn_norm_g": (1, 128), "hgrn_norm_g": (1, 128),
          "lower_bounds": (2, 2048), "ln_g": (2, 1024), "ln_b": (2, 1024), "conv_a_w": (3, 1024), "conv_b_w": (4, 3072)}
_CONV_SHARD = {"conv_a_w": (3, 256), "conv_b_w": (4, 768)}


def _pack_small(parts, shapes, head_rows=0):
    rows = []
    for n, shape in shapes.items():
        v = parts[n].reshape(-1)
        rows.append(jnp.pad(v, (0, -v.shape[0] % LANES)).reshape(-1, LANES))
    buf = jnp.concatenate(rows, axis=0)
    return jnp.pad(buf, ((head_rows, -(buf.shape[0] + head_rows) % SUBLANES), (0, 0)))


def _unpack_small(buf, shapes, head_rows=0):
    out, off = {}, head_rows
    for n, shape in shapes.items():
        nrow = -(-_size(shape) // LANES)
        out[n] = buf[off:off + nrow].reshape(-1)[:_size(shape)].reshape(shape)
        off += nrow
    return out


_WEIGHTS = ["w_in_even", "conv_a_w", "conv_b_w", "a_log", "dt_bias", "gdn_norm_g", "w_out_even", "w_in_odd",
            "lower_bounds", "hgrn_norm_g", "w_out_odd", "ln_g", "ln_b", "w_pl", "w_pl_gate"]


def kernel(x, p, w_in_even, conv_a_w, conv_b_w, a_log, dt_bias, gdn_norm_g, w_out_even, w_in_odd, lower_bounds, hgrn_norm_g, w_out_odd, ln_g, ln_b, w_pl, w_pl_gate, loss_target, m_w_in_even, m_conv_a_w, m_conv_b_w, m_a_log, m_dt_bias, m_gdn_norm_g, m_w_out_even, m_w_in_odd, m_lower_bounds, m_hgrn_norm_g, m_w_out_odd, m_ln_g, m_ln_b, m_w_pl, m_w_pl_gate, v_w_in_even, v_conv_a_w, v_conv_b_w, v_a_log, v_dt_bias, v_gdn_norm_g, v_w_out_even, v_w_in_odd, v_lower_bounds, v_hgrn_norm_g, v_w_out_odd, v_ln_g, v_ln_b, v_w_pl, v_w_pl_gate):
    w = dict(zip(_WEIGHTS, (w_in_even, conv_a_w, conv_b_w, a_log, dt_bias, gdn_norm_g, w_out_even, w_in_odd,
                            lower_bounds, hgrn_norm_g, w_out_odd, ln_g, ln_b, w_pl, w_pl_gate)))
    m = dict(zip(_WEIGHTS, (m_w_in_even, m_conv_a_w, m_conv_b_w, m_a_log, m_dt_bias, m_gdn_norm_g, m_w_out_even,
                            m_w_in_odd, m_lower_bounds, m_hgrn_norm_g, m_w_out_odd, m_ln_g, m_ln_b, m_w_pl, m_w_pl_gate)))
    v = dict(zip(_WEIGHTS, (v_w_in_even, v_conv_a_w, v_conv_b_w, v_a_log, v_dt_bias, v_gdn_norm_g, v_w_out_even,
                            v_w_in_odd, v_lower_bounds, v_hgrn_norm_g, v_w_out_odd, v_ln_g, v_ln_b, v_w_pl, v_w_pl_gate)))
    chip = 2 * lax.axis_index("x") + lax.axis_index("y")

    names = list(_BIG)
    shard_shapes = {n: _BIG[n][0] for n in names}
    early, late = names[:1], names[1:]
    shards = {n: w[n].reshape(shard_shapes[n]).astype(BF16) for n in early}
    whole = lambda n, stacked: _BIG[n][1](lax.dynamic_update_slice(stacked, shards[n][None], (chip, 0, 0)))
    conv_mine = _pack_small({n: w[n] for n in _CONV_SHARD}, _CONV_SHARD)
    conv_all = _exchange_small(conv_mine, False, "gather_conv")
    shards, conv_all = lax.optimization_barrier((shards, conv_all))
    first = _gather_start([shards[n] for n in early], "gather_first_start")
    shards.update({n: (w[n].reshape(shard_shapes[n]) + first[4][0, 0]).astype(BF16) for n in late})
    send_sems, recv_sems, srcs, lands, token = _gather_start([shards[n] for n in late], "gather_rest_start")

    def late_weights(after):
        landed = _gather_forward(_gather_wait(send_sems, recv_sems, srcs, lands, after, "gather_rest_wait"),
                                 "gather_rest_forward")
        return {n: whole(n, ga) for n, ga in zip(late, landed)}

    landed = _gather_forward(_gather_wait(*first[:4], token, "gather_first_wait"), "gather_first_forward")
    full = {n: whole(n, ga) for n, ga in zip(early, landed)}
    conv_by_chip = [_unpack_small(conv_all[2 * s], _CONV_SHARD) for s in range(4)]
    for n in _CONV_SHARD:
        full[n] = jnp.concatenate([conv_by_chip[s][n] for s in range(4)], axis=1)
    for n in _SMALL:
        if n not in _CONV_SHARD:
            full[n] = w[n]

    early_rs = {}

    def early_grads_ready(grads):
        early_rs["swap"] = _rs_swap_start([_RS_EARLY[n](grads[n]) for n in _RS_EARLY], "rs_swap_early_start")
        return early_rs["swap"][4][0, 0]

    def early_grads_swapped(after):
        g4s, got = _rs_swap_wait(*early_rs["swap"][:4], after, "rs_swap_early_wait")
        c_idx = jnp.stack([lax.axis_index("c")]).astype(jnp.int32)
        early_rs["p4s"] = [_rs_add_sibling(g_, s_, c_idx, f"rs_add_sibling_{nm}") for g_, s_, nm in zip(g4s, got, _RS_EARLY)]
        early_rs["sems"] = _rs_scatter_start([pb for _, pb in early_rs["p4s"]], "rs_scatter_early_start")
        return early_rs["sems"][4][0, 0]

    late_rs = {}

    def last_grad_ready(grad):
        late_rs["p4s"] = _rs_front([_RS_LATE[n](grad) for n in _RS_LATE], list(_RS_LATE), "late")
        late_rs["sems"] = _rs_scatter_start([pb for _, pb in late_rs["p4s"]], "rs_scatter_late_start")
        return late_rs["sems"][4][0, 0]

    loss_part, dx, g = _local_step(x[0], p[:, 0], loss_target[0], full, late_weights, early_grads_ready,
                                   early_grads_swapped, last_grad_ready, token[0, 0])

    late_p4s, late_sems = late_rs["p4s"], late_rs["sems"]
    got3 = _rs_scatter_wait(*early_rs["sems"][:4], dx, "rs_scatter_early_wait")
    summed = dict(zip(_RS_EARLY, _rs_share_halves(_rs_back(early_rs["p4s"], got3, list(_RS_EARLY)), "rs_share_early")))
    g_big = {n: summed[n] for n in names if n in summed}
    g_big["w_pl"] = jnp.stack([summed["w_pl0"], summed["w_pl1"]])
    g_big["w_pl_gate"] = jnp.stack([summed["w_pl_gate0"], summed["w_pl_gate1"]])
    small_sum = _exchange_small(jnp.concatenate([loss_part, _pack_small(g, _SMALL)], axis=0), True, "reduce_small")
    loss = small_sum[0, 0]
    g_small = _unpack_small(small_sum, _SMALL, head_rows=SUBLANES)
    for n, (rows, cols) in _CONV_SHARD.items():
        g_small[n] = lax.dynamic_slice_in_dim(g_small[n], chip * cols, cols, axis=1)

    grads, delta, new_m, new_v = {}, {}, {}, {}
    for n in late:
        grads[n] = g_big[n].reshape(w[n].shape)
        delta[n], new_m[n], new_v[n] = _adamw(w[n], grads[n], m[n], v[n], f"adamw_{n}")
    own = {n: (_CONV_SHARD[n] if n in _CONV_SHARD else _SMALL[n]) for n in _SMALL}
    packs = [_pack_small({n: src[n] for n in _SMALL}, own)[None] for src in (w, g_small, m, v)]
    outs = [_unpack_small(t[0], own) for t in _adamw(*packs, "adamw_small")]
    for n in _SMALL:
        grads[n] = g_small[n].reshape(w[n].shape)
        delta[n], new_m[n], new_v[n] = (t[n].reshape(w[n].shape) for t in outs)
    got3 = _rs_scatter_wait(*late_sems[:4], new_v["w_in_odd"], "rs_scatter_late_wait")
    (g_in_even,) = _rs_share_halves(_rs_back(late_p4s, got3, list(_RS_LATE)), "rs_share_late")
    for n in early:
        t_ = lambda a: jnp.swapaxes(a, 1, 2)
        g_t = t_(g_in_even.reshape(w[n].shape))
        grads[n] = t_(g_t)
        delta[n], new_m[n], new_v[n] = (t_(o) for o in _adamw(t_(w[n]), g_t, t_(m[n]), t_(v[n]), f"adamw_{n}"))
    return (loss, dx[None], *[grads[n] for n in _WEIGHTS], *[delta[n] for n in _WEIGHTS],
            *[new_m[n] for n in _WEIGHTS], *[new_v[n] for n in _WEIGHTS])
```

```python
import functools

import jax
import jax.numpy as jnp
from jax import lax
from jax.experimental import pallas as pl
from jax.experimental.pallas import tpu as pltpu

F32 = jnp.float32
BF16 = jnp.bfloat16

D_MODEL = 1024
PL_DIM = 256
GDN_HEADS = 8
HEAD_DIM = 128
GDN_CHUNK = 64
HGRN_HEADS = 16
HGRN_CHUNK = 32
HGRN_WIDTH = 2048
DEEPNORM_ALPHA = 4.0 ** 0.25
NORM_EPS = 1e-5
ADAM_LR, ADAM_B1, ADAM_B2, ADAM_EPS, ADAM_WD, ADAM_STEP = 0.001, 0.9, 0.999, 1e-08, 0.01, 10

VMEM_LIMIT = 60 * 1024 * 1024
SUBLANES = 8
LANES = 128


def _params(*sem):
    return pltpu.CompilerParams(dimension_semantics=sem, vmem_limit_bytes=VMEM_LIMIT)


ONE_PASS, THREE_PASS, EXACT_LHS, EXACT_RHS = 0, 1, 2, 3


def _split3(v):
    hi = v.astype(BF16)
    r1 = v - hi.astype(F32)
    mid = r1.astype(BF16)
    return hi, mid, (r1 - mid.astype(F32)).astype(BF16)


def _mm_raw(a, b, kind, prec):
    nb = a.ndim - 2
    ca = a.ndim - 1 if kind[0] == "n" else a.ndim - 2
    cb = b.ndim - 2 if kind[1] == "n" else b.ndim - 1
    dims = (((ca,), (cb,)), (tuple(range(nb)),) * 2)
    dot = lambda p, q: lax.dot_general(p, q, dims, preferred_element_type=F32)
    ah, bh = a.astype(BF16), b.astype(BF16)
    if prec == ONE_PASS:
        return dot(ah, bh)
    if prec == EXACT_LHS:
        b1, b2, b3 = _split3(b)
        return dot(ah, b1) + (dot(ah, b2) + dot(ah, b3))
    if prec == EXACT_RHS:
        a1, a2, a3 = _split3(a)
        return dot(a1, bh) + (dot(a2, bh) + dot(a3, bh))
    al = (a - ah.astype(F32)).astype(BF16)
    bl = (b - bh.astype(F32)).astype(BF16)
    return dot(ah, bh) + (dot(ah, bl) + dot(al, bh))


@functools.partial(jax.custom_vjp, nondiff_argnums=(2, 3))
def _mm_vjp(a, b, kind, hi):
    return _mm_raw(a, b, kind, hi)


def _mm_vjp_fwd(a, b, kind, hi):
    return _mm_raw(a, b, kind, hi), (a, b)


def _mm_vjp_bwd(kind, hi, res, dc):
    a, b = res
    if hi in (EXACT_LHS, EXACT_RHS):
        assert kind == "nn"
        if hi == EXACT_LHS:
            return jnp.zeros_like(a), _mm_raw(a, dc, "tn", EXACT_LHS)
        return _mm_raw(dc, b, "nt", EXACT_RHS), jnp.zeros_like(b)
    if kind == "nn":
        return _mm_raw(dc, b, "nt", hi), _mm_raw(a, dc, "tn", hi)
    if kind == "nt":
        return _mm_raw(dc, b, "nn", hi), _mm_raw(dc, a, "tn", hi)
    return _mm_raw(b, dc, "nt", hi), _mm_raw(a, dc, "nn", hi)


_mm_vjp.defvjp(_mm_vjp_fwd, _mm_vjp_bwd)


def _lane_total(v):
    return jnp.broadcast_to(jnp.sum(v, axis=-1, keepdims=True), v.shape)


def _matmul(a, b, *, name, ta=False, tb=False, add=None, add_scale=1.0, tm=1024, tn=2048, tk=1024):
    m, k = (a.shape[1], a.shape[0]) if ta else a.shape
    n = b.shape[0] if tb else b.shape[1]
    tm, tn, tk = min(tm, m), min(tn, n), min(tk, k)
    tn = tn if n % tn == 0 else tn // 2
    assert m % tm == 0 and n % tn == 0 and k % tk == 0, (name, m, n, k)
    nk = k // tk
    dims = (((0 if ta else 1,), (1 if tb else 0,)), ((), ()))

    def body(*refs):
        a_ref, b_ref = refs[:2]
        o_ref = refs[-1]
        part = lax.dot_general(a_ref[...].astype(BF16), b_ref[...].astype(BF16), dims, preferred_element_type=F32)
        first = (lambda: part) if add is None else (lambda: part + add_scale * refs[2][...])
        if nk == 1:
            o_ref[...] = first()
        else:
            kk = pl.program_id(2)

            @pl.when(kk == 0)
            def _():
                o_ref[...] = first()

            @pl.when(kk > 0)
            def _():
                o_ref[...] += part

    a_spec = pl.BlockSpec((tk, tm), lambda i, j, kk: (kk, i)) if ta else pl.BlockSpec((tm, tk), lambda i, j, kk: (i, kk))
    b_spec = pl.BlockSpec((tn, tk), lambda i, j, kk: (j, kk)) if tb else pl.BlockSpec((tk, tn), lambda i, j, kk: (kk, j))
    o_spec = pl.BlockSpec((tm, tn), lambda i, j, kk: (i, j))
    in_specs = [a_spec, b_spec] + ([o_spec] if add is not None else [])
    args = (a, b) + ((add,) if add is not None else ())
    return pl.pallas_call(
        body, name=name, grid=(m // tm, n // tn, nk), in_specs=in_specs, out_specs=o_spec,
        out_shape=jax.ShapeDtypeStruct((m, n), F32),
        compiler_params=_params("parallel", "parallel", "arbitrary"))(*args)


HALO = SUBLANES


def _halo_specs(tt, width, col, nt):
    r = tt // HALO
    prev = pl.BlockSpec((HALO, width), lambda i: (jnp.maximum(i * r - 1, 0), col))
    nxt = pl.BlockSpec((HALO, width), lambda i: (jnp.minimum((i + 1) * r, nt * r - 1), col))
    return prev, nxt


def _shift_down(ext, k):
    return ext if k == 0 else pltpu.roll(ext, k, 0)


def _shift_up(ext, k):
    return ext if k == 0 else pltpu.roll(ext, ext.shape[0] - k, 0)


def _causal_conv(ext, w, taps):
    acc = None
    for j in range(taps):
        term = w[j:j + 1, :] * _shift_down(ext, taps - 1 - j)
        acc = term if acc is None else acc + term
    return acc[HALO:, :]


def _conv_a_fwd(proj_a, conv_w):
    t = proj_a.shape[0]
    tt = min(t, 256)
    nt = t // tt
    wdt = 1024

    def body(cur_ref, prev_ref, w_ref, y_ref):
        i = pl.program_id(0)
        cur = cur_ref[...]
        h, c, b, z = (cur[:, k * wdt:(k + 1) * wdt] for k in range(4))
        prev = prev_ref[...]
        u_prev = jnp.where(i > 0, prev[:, wdt:2 * wdt] * prev[:, 0:wdt], 0.0)
        ext = jnp.concatenate([u_prev, c * h], axis=0)
        conv = _causal_conv(ext, w_ref[...], 3)
        y_ref[...] = (b * conv * jax.nn.silu(z)).astype(BF16)

    prev_spec, _ = _halo_specs(tt, 4 * wdt, 0, nt)
    return pl.pallas_call(
        body, name="conv_a_fwd", grid=(nt,),
        in_specs=[pl.BlockSpec((tt, 4 * wdt), lambda i: (i, 0)), prev_spec, pl.BlockSpec((3, wdt), lambda i: (0, 0))],
        out_specs=pl.BlockSpec((tt, wdt), lambda i: (i, 0)),
        out_shape=jax.ShapeDtypeStruct((t, wdt), BF16), compiler_params=_params("parallel"))(proj_a, proj_a, conv_w)


def _conv_a_bwd(proj_a, conv_w, dy):
    t = proj_a.shape[0]
    tt = min(t, 256)
    nt = t // tt
    wdt = 1024

    def body(cur_ref, prev_ref, nxt_ref, w_ref, dy_ref, dyn_ref, d_ref, dw_ref):
        i = pl.program_id(0)
        w = w_ref[...]
        cur, prev, nxt = cur_ref[...], prev_ref[...], nxt_ref[...]
        split = lambda a: tuple(a[:, k * wdt:(k + 1) * wdt] for k in range(4))
        h, c, b, z = split(cur)
        hp, cp, _, _ = split(prev)
        hn, cn, bn, zn = split(nxt)
        u_prev = jnp.where(i > 0, cp * hp, 0.0)
        u_ext = jnp.concatenate([u_prev, c * h, cn * hn], axis=0)
        taps = [_shift_down(u_ext, 2 - j)[HALO:, :] for j in range(3)]
        conv = w[0:1, :] * taps[0] + w[1:2, :] * taps[1] + w[2:3, :] * taps[2]
        b_cn = jnp.concatenate([b, bn], axis=0)
        z_cn = jnp.concatenate([z, zn], axis=0)
        dy_cn = jnp.concatenate([dy_ref[...], jnp.where(i < nt - 1, dyn_ref[...], 0.0)], axis=0)
        sg = jax.nn.sigmoid(z_cn)
        silu = z_cn * sg
        d_conv = dy_cn * b_cn * silu
        db = (dy_cn * conv * silu)[:tt, :]
        dz = (dy_cn * b_cn * conv * (sg * (1.0 + z_cn * (1.0 - sg))))[:tt, :]
        du = None
        for j in range(3):
            term = w[j:j + 1, :] * _shift_up(d_conv, 2 - j)
            du = term if du is None else du + term
        du = du[:tt, :]
        d_ref[...] = jnp.concatenate([du * c, du * h, db, dz], axis=1).astype(BF16)

        @pl.when(i == 0)
        def _():
            dw_ref[...] = jnp.zeros_like(dw_ref)

        d_cur = d_conv[:tt, :]
        rows = [jnp.sum(d_cur * taps[j][:tt, :], axis=0, keepdims=True) for j in range(3)]
        dw_ref[0:3, :] += jnp.concatenate(rows, axis=0)

    prev_spec, nxt_spec = _halo_specs(tt, 4 * wdt, 0, nt)
    _, dyn_spec = _halo_specs(tt, wdt, 0, nt)
    return pl.pallas_call(
        body, name="conv_a_bwd", grid=(nt,),
        in_specs=[pl.BlockSpec((tt, 4 * wdt), lambda i: (i, 0)), prev_spec, nxt_spec,
                  pl.BlockSpec((3, wdt), lambda i: (0, 0)), pl.BlockSpec((tt, wdt), lambda i: (i, 0)), dyn_spec],
        out_specs=[pl.BlockSpec((tt, 4 * wdt), lambda i: (i, 0)), pl.BlockSpec((SUBLANES, wdt), lambda i: (0, 0))],
        out_shape=[jax.ShapeDtypeStruct((t, 4 * wdt), BF16), jax.ShapeDtypeStruct((SUBLANES, wdt), F32)],
        compiler_params=_params("arbitrary"))(proj_a, proj_a, proj_a, conv_w, dy, dy)


def _conv_b_fwd(proj_qkv, conv_w):
    t, width = proj_qkv.shape
    tt = min(t, 256)
    nt = t // tt
    wdt = 1024

    def body(cur_ref, prev_ref, w_ref, y_ref):
        i = pl.program_id(1)
        ext = jnp.concatenate([jnp.where(i > 0, prev_ref[...], 0.0), cur_ref[...]], axis=0)
        y_ref[...] = jax.nn.silu(_causal_conv(ext, w_ref[...], 4))

    r = tt // HALO
    return pl.pallas_call(
        body, name="conv_b_fwd", grid=(width // wdt, nt),
        in_specs=[pl.BlockSpec((tt, wdt), lambda j, i: (i, j)),
                  pl.BlockSpec((HALO, wdt), lambda j, i: (jnp.maximum(i * r - 1, 0), j)),
                  pl.BlockSpec((4, wdt), lambda j, i: (0, j))],
        out_specs=pl.BlockSpec((tt, wdt), lambda j, i: (i, j)),
        out_shape=jax.ShapeDtypeStruct((t, width), F32), compiler_params=_params("parallel", "parallel"))(
            proj_qkv, proj_qkv, conv_w)


def _conv_b_bwd(proj_qkv, conv_w, d_act, col, name):
    t = proj_qkv.shape[0]
    tt = min(t, 256)
    nt = t // tt
    wdt = 1024

    def body(cur_ref, prev_ref, nxt_ref, w_ref, da_ref, dan_ref, d_ref, dw_ref):
        i = pl.program_id(0)
        w = w_ref[...]
        u_ext = jnp.concatenate([jnp.where(i > 0, prev_ref[...], 0.0), cur_ref[...], nxt_ref[...]], axis=0)
        taps = [_shift_down(u_ext, 3 - j)[HALO:, :] for j in range(4)]
        conv = w[0:1, :] * taps[0] + w[1:2, :] * taps[1] + w[2:3, :] * taps[2] + w[3:4, :] * taps[3]
        da_cn = jnp.concatenate([da_ref[...], jnp.where(i < nt - 1, dan_ref[...], 0.0)], axis=0)
        sg = jax.nn.sigmoid(conv)
        d_conv = da_cn * (sg * (1.0 + conv * (1.0 - sg)))
        du = None
        for j in range(4):
            term = w[j:j + 1, :] * _shift_up(d_conv, 3 - j)
            du = term if du is None else du + term
        d_ref[...] = du[:tt, :].astype(BF16)

        @pl.when(i == 0)
        def _():
            dw_ref[...] = jnp.zeros_like(dw_ref)

        d_cur = d_conv[:tt, :]
        rows = [jnp.sum(d_cur * taps[j][:tt, :], axis=0, keepdims=True) for j in range(4)]
        dw_ref[0:4, :] += jnp.concatenate(rows, axis=0)

    prev_spec, nxt_spec = _halo_specs(tt, wdt, col, nt)
    _, dan_spec = _halo_specs(tt, wdt, 0, nt)
    return pl.pallas_call(
        body, name=name, grid=(nt,),
        in_specs=[pl.BlockSpec((tt, wdt), lambda i: (i, col)), prev_spec, nxt_spec,
                  pl.BlockSpec((4, wdt), lambda i: (0, col)), pl.BlockSpec((tt, wdt), lambda i: (i, 0)), dan_spec],
        out_specs=[pl.BlockSpec((tt, wdt), lambda i: (i, 0)), pl.BlockSpec((SUBLANES, wdt), lambda i: (0, 0))],
        out_shape=[jax.ShapeDtypeStruct((t, wdt), BF16), jax.ShapeDtypeStruct((SUBLANES, wdt), F32)],
        compiler_params=_params("arbitrary"))(proj_qkv, proj_qkv, proj_qkv, conv_w, d_act, d_act)


def _rms_gate(o, gn, z):
    on = o * lax.rsqrt(jnp.mean(o * o, axis=-1, keepdims=True) + NORM_EPS) * gn
    return on * jax.nn.silu(z)


GDN_PREP_ROWS = 1024


def _unit_lower_inverse(low):
    c = low.shape[-1]
    eye = lax.broadcasted_iota(jnp.int32, low.shape, low.ndim - 2) == lax.broadcasted_iota(jnp.int32, low.shape, low.ndim - 1)
    x = -low
    inv = eye.astype(F32) + x
    for _ in range(c.bit_length() - 2):
        x = _mm_raw(x, x, "nn", THREE_PASS)
        inv = inv + _mm_raw(inv, x, "nn", THREE_PASS)
    return inv


@jax.custom_vjp
def _known_inverse(low, inv):
    return inv


def _known_inverse_fwd(low, inv):
    return inv, inv


def _known_inverse_bwd(inv, d_inv):
    return -_mm_raw(_mm_raw(inv, d_inv, "tn", THREE_PASS), inv, "nt", THREE_PASS), jnp.zeros_like(inv)


_known_inverse.defvjp(_known_inverse_fwd, _known_inverse_bwd)


def _gdn_prep(mm, qa, ka, va, braw, araw, alog, dtb, inv_kept=None):
    n, c, _ = qa.shape
    q = qa * lax.rsqrt(jnp.sum(qa * qa, axis=-1, keepdims=True) + 1e-6) * (HEAD_DIM ** -0.5)
    k = ka * lax.rsqrt(jnp.sum(ka * ka, axis=-1, keepdims=True) + 1e-6)
    beta = jax.nn.sigmoid(braw)
    g = -jnp.exp(alog) * jax.nn.softplus(araw + dtb)
    ri = lax.broadcasted_iota(jnp.int32, (n, c, c), 1)
    ci = lax.broadcasted_iota(jnp.int32, (n, c, c), 2)
    incl, strict, eye = ri >= ci, ri > ci, ri == ci
    gc = mm(incl.astype(F32), g, "nn", EXACT_LHS)
    gc_i = gc[:, :, :c]
    gc_j = mm(jnp.ones((n, c, c), F32), jnp.where(eye, gc_i, 0.0), "nn", EXACT_LHS)
    decay = jnp.where(incl, jnp.exp(jnp.where(incl, gc_i - gc_j, 0.0)), 0.0)
    kb = k * beta
    low = jnp.where(strict, mm(kb, k, "nt", ONE_PASS) * decay, 0.0)
    inv = _unit_lower_inverse(low) if inv_kept is None else _known_inverse(low, inv_kept)
    egc = jnp.exp(gc)
    u = mm(inv, va * beta, "nn", THREE_PASS)
    w = mm(inv, kb * egc, "nn", THREE_PASS)
    attn = jnp.where(incl, mm(q, k, "nt", ONE_PASS) * decay, 0.0)
    g_last = jnp.sum(g, axis=1, keepdims=True)
    outs = (u, w, q * egc, k * jnp.exp(g_last - gc), attn, jnp.exp(g_last))
    return outs + (inv,) if inv_kept is None else outs


def _gdn_scan(mm, u, w, qd, kd, attn, egl, z, gn, state):
    v_new = u - mm(w, state, "nn", ONE_PASS)
    o = mm(qd, state, "nn", ONE_PASS) + mm(attn, v_new, "nn", ONE_PASS)
    new_state = state * egl + mm(kd, v_new, "tn", ONE_PASS)
    return _rms_gate(o, gn, z), new_state


def _chunks(ref_value, n, c):
    return ref_value.reshape(n, c, ref_value.shape[-1])


def _by_head(ref, rows, heads):
    return jnp.stack([ref[rows, pl.ds(h * HEAD_DIM, HEAD_DIM)] for h in range(heads)])


def _store_heads(ref, rows, value):
    for h in range(value.shape[0]):
        ref[rows, pl.ds(h * HEAD_DIM, HEAD_DIM)] = value[h]


def _gdn_prep_specs(tb):
    col = lambda off: pl.BlockSpec((tb, HEAD_DIM), lambda h, i: (i, off + h))
    rep = pl.BlockSpec((1, tb, LANES), lambda h, i: (h, i, 0))
    par = pl.BlockSpec((1, SUBLANES, LANES), lambda h, i: (h, 0, 0))
    att = pl.BlockSpec((1, tb, GDN_CHUNK), lambda h, i: (h, i, 0))
    egl = pl.BlockSpec((1, tb // GDN_CHUNK, SUBLANES, LANES), lambda h, i: (h, i, 0, 0))
    return col, rep, par, att, egl


def _head_columns(tail):
    lane = lax.broadcasted_iota(jnp.int32, tail.shape, 1)
    h = pl.program_id(0)
    return _lane_total(jnp.where(lane == h, tail, 0.0)), _lane_total(jnp.where(lane == h + GDN_HEADS, tail, 0.0))


def _gdn_prep_fwd(qkv_act, tail, alog, dtb):
    t = qkv_act.shape[0]
    tb = min(t, 2 * GDN_PREP_ROWS)
    nt, nc = t // tb, tb // GDN_CHUNK
    width = GDN_HEADS * HEAD_DIM

    def body(q_ref, k_ref, v_ref, tail_ref, al_ref, dt_ref, u_ref, w_ref, qd_ref, kd_ref, at_ref, eg_ref, inv_ref):
        ch = lambda r: _chunks(r, nc, GDN_CHUNK)
        braw, araw = _head_columns(tail_ref[...])
        u, w, qd, kd, attn, egl, inv = _gdn_prep(_mm_raw, ch(q_ref[...]), ch(k_ref[...]), ch(v_ref[...]), ch(braw),
                                                 ch(araw), al_ref[0, 0:1, :], dt_ref[0, 0:1, :])
        u_ref[...] = u.reshape(tb, HEAD_DIM)
        w_ref[...] = w.reshape(tb, HEAD_DIM).astype(BF16)
        qd_ref[...] = qd.reshape(tb, HEAD_DIM).astype(BF16)
        kd_ref[...] = kd.reshape(tb, HEAD_DIM).astype(BF16)
        at_ref[0] = attn.reshape(tb, GDN_CHUNK).astype(BF16)
        eg_ref[0] = jnp.broadcast_to(egl, (nc, SUBLANES, LANES))
        inv_ref[0] = inv.reshape(tb, GDN_CHUNK)

    col, rep, par, att, egl = _gdn_prep_specs(tb)
    h = GDN_HEADS
    return pl.pallas_call(
        body, name="gdn_prep_fwd", grid=(h, nt),
        in_specs=[col(0), col(h), col(2 * h), pl.BlockSpec((tb, LANES), lambda h_, i: (i, 0)), par, par],
        out_specs=[col(0), col(0), col(0), col(0), att, egl, att],
        out_shape=[jax.ShapeDtypeStruct((t, width), F32)] + [jax.ShapeDtypeStruct((t, width), BF16)] * 3
        + [jax.ShapeDtypeStruct((h, t, GDN_CHUNK), BF16), jax.ShapeDtypeStruct((h, t // GDN_CHUNK, SUBLANES, LANES), F32),
           jax.ShapeDtypeStruct((h, t, GDN_CHUNK), F32)],
        compiler_params=_params("parallel", "parallel"))(qkv_act, qkv_act, qkv_act, tail, alog, dtb)


def _gdn_prep_bwd(qkv_act, tail, alog, dtb, inv, du, dw, dqd, dkd, dattn, degl):
    t = qkv_act.shape[0]
    tb = min(t, 2 * GDN_PREP_ROWS)
    nt, nc = t // tb, tb // GDN_CHUNK
    width = GDN_HEADS * HEAD_DIM

    def body(q_ref, k_ref, v_ref, tail_ref, al_ref, dt_ref, inv_ref, du_ref, dw_ref, dqd_ref, dkd_ref, dat_ref,
             deg_ref, dq_ref, dk_ref, dv_ref, dbr_ref, dar_ref, dal_ref, ddt_ref):
        @pl.when(pl.program_id(1) == 0)
        def _():
            dal_ref[...] = jnp.zeros_like(dal_ref)
            ddt_ref[...] = jnp.zeros_like(ddt_ref)

        ch = lambda r: _chunks(r, nc, GDN_CHUNK)
        braw, araw = _head_columns(tail_ref[...])
        _, vjp = jax.vjp(functools.partial(_gdn_prep, _mm_vjp, inv_kept=ch(inv_ref[0])), ch(q_ref[...]), ch(k_ref[...]),
                         ch(v_ref[...]), ch(braw), ch(araw), al_ref[0, 0:1, :], dt_ref[0, 0:1, :])
        dq, dk, dv, dbr, dar, dal, ddt = vjp((ch(du_ref[...]), ch(dw_ref[...]), ch(dqd_ref[...]), ch(dkd_ref[...]),
                                              ch(dat_ref[0]), deg_ref[0][:, 0:1, :]))
        dq_ref[...] = dq.reshape(tb, HEAD_DIM)
        dk_ref[...] = dk.reshape(tb, HEAD_DIM)
        dv_ref[...] = dv.reshape(tb, HEAD_DIM)
        dbr_ref[0] = _lane_total(dbr.reshape(tb, LANES))
        dar_ref[0] = _lane_total(dar.reshape(tb, LANES))
        dal_ref[0, 0:1, :] += _lane_total(dal)
        ddt_ref[0, 0:1, :] += _lane_total(ddt)

    col, rep, par, att, egl = _gdn_prep_specs(tb)
    h = GDN_HEADS
    return pl.pallas_call(
        body, name="gdn_prep_bwd", grid=(h, nt),
        in_specs=[col(0), col(h), col(2 * h), pl.BlockSpec((tb, LANES), lambda h_, i: (i, 0)), par, par, att,
                  col(0), col(0), col(0), col(0), att, egl],
        out_specs=[col(0), col(0), col(0), rep, rep, par, par],
        out_shape=[jax.ShapeDtypeStruct((t, width), F32)] * 3 + [jax.ShapeDtypeStruct((h, t, LANES), F32)] * 2
        + [jax.ShapeDtypeStruct((h, SUBLANES, LANES), F32)] * 2,
        compiler_params=_params("parallel", "arbitrary"))(qkv_act, qkv_act, qkv_act, tail, alog, dtb, inv,
                                                         du, dw, dqd, dkd, dattn, degl)


def _scan_specs(tb, heads, chunk, rev, nt):
    ti = (lambda i: nt - 1 - i) if rev else (lambda i: i)
    row = pl.BlockSpec((tb, heads * HEAD_DIM), lambda i: (ti(i), 0))
    att = pl.BlockSpec((heads, tb, chunk), lambda i: (0, ti(i), 0))
    egl = pl.BlockSpec((heads, tb // chunk, SUBLANES, LANES), lambda i: (0, ti(i), 0, 0))
    hist = pl.BlockSpec((heads, tb // chunk, HEAD_DIM, HEAD_DIM), lambda i: (0, ti(i), 0, 0))
    gn = pl.BlockSpec((SUBLANES, LANES), lambda i: (0, 0))
    return row, att, egl, hist, gn


def _gdn_scan_fwd(u, w, qd, kd, attn, egl, zb, gn):
    t = u.shape[0]
    tb = min(t, 256)
    nt, nc = t // tb, tb // GDN_CHUNK
    nh = GDN_HEADS

    def body(u_ref, w_ref, qd_ref, kd_ref, at_ref, eg_ref, z_ref, gn_ref, y_ref, hist_ref, s_ref):
        @pl.when(pl.program_id(0) == 0)
        def _():
            s_ref[...] = jnp.zeros_like(s_ref)

        g = gn_ref[0:1, :]
        state = s_ref[...]
        for c in range(nc):
            rows = pl.ds(c * GDN_CHUNK, GDN_CHUNK)
            heads = lambda r: _by_head(r, rows, nh)
            hist_ref[:, c] = state
            y, state = _gdn_scan(_mm_raw, heads(u_ref), heads(w_ref), heads(qd_ref), heads(kd_ref), at_ref[:, rows, :],
                                 eg_ref[:, c, 0:1, :], heads(z_ref), g, state)
            _store_heads(y_ref, rows, y.astype(BF16))
        s_ref[...] = state

    row, att, egs, hist, gns = _scan_specs(tb, nh, GDN_CHUNK, False, nt)
    return pl.pallas_call(
        body, name="gdn_scan_fwd", grid=(nt,), in_specs=[row, row, row, row, att, egs, row, gns], out_specs=[row, hist],
        out_shape=[jax.ShapeDtypeStruct((t, nh * HEAD_DIM), BF16),
                   jax.ShapeDtypeStruct((nh, t // GDN_CHUNK, HEAD_DIM, HEAD_DIM), F32)],
        scratch_shapes=[pltpu.VMEM((nh, HEAD_DIM, HEAD_DIM), F32)],
        compiler_params=_params("arbitrary"))(u, w, qd, kd, attn, egl, zb, gn)


def _gdn_scan_bwd(u, w, qd, kd, attn, egl, zb, gn, hist, dy):
    t = u.shape[0]
    tb = min(t, 256)
    nt, nc = t // tb, tb // GDN_CHUNK
    nh = GDN_HEADS

    def body(u_ref, w_ref, qd_ref, kd_ref, at_ref, eg_ref, z_ref, gn_ref, hist_ref, dy_ref,
             du_ref, dw_ref, dqd_ref, dkd_ref, dat_ref, deg_ref, dz_ref, dgn_ref, ds_ref):
        @pl.when(pl.program_id(0) == 0)
        def _():
            ds_ref[...] = jnp.zeros_like(ds_ref)
            dgn_ref[...] = jnp.zeros_like(dgn_ref)

        g = gn_ref[0:1, :]
        d_state = ds_ref[...]
        for c in reversed(range(nc)):
            rows = pl.ds(c * GDN_CHUNK, GDN_CHUNK)
            heads = lambda r: _by_head(r, rows, nh).astype(F32)
            _, vjp = jax.vjp(functools.partial(_gdn_scan, _mm_vjp), heads(u_ref), heads(w_ref), heads(qd_ref),
                             heads(kd_ref), at_ref[:, rows, :].astype(F32), eg_ref[:, c, 0:1, :], heads(z_ref), g,
                             hist_ref[:, c])
            du, dw, dqd, dkd, dat, deg, dz, dgn, d_state = vjp((heads(dy_ref), d_state))
            _store_heads(du_ref, rows, du)
            _store_heads(dw_ref, rows, dw)
            _store_heads(dqd_ref, rows, dqd)
            _store_heads(dkd_ref, rows, dkd)
            dat_ref[:, rows, :] = dat
            deg_ref[:, c] = jnp.broadcast_to(deg, (nh, SUBLANES, LANES))
            _store_heads(dz_ref, rows, dz.astype(BF16))
            dgn_ref[0:1, :] += dgn
        ds_ref[...] = d_state

    row, att, egs, hists, gns = _scan_specs(tb, nh, GDN_CHUNK, True, nt)
    wide = jax.ShapeDtypeStruct((t, nh * HEAD_DIM), F32)
    return pl.pallas_call(
        body, name="gdn_scan_bwd", grid=(nt,),
        in_specs=[row, row, row, row, att, egs, row, gns, hists, row],
        out_specs=[row, row, row, row, att, egs, row, gns],
        out_shape=[wide] * 4 + [jax.ShapeDtypeStruct((nh, t, GDN_CHUNK), F32),
                                jax.ShapeDtypeStruct((nh, t // GDN_CHUNK, SUBLANES, LANES), F32),
                                jax.ShapeDtypeStruct((t, nh * HEAD_DIM), BF16),
                                jax.ShapeDtypeStruct((SUBLANES, LANES), F32)],
        scratch_shapes=[pltpu.VMEM((nh, HEAD_DIM, HEAD_DIM), F32)],
        compiler_params=_params("arbitrary"))(u, w, qd, kd, attn, egl, zb, gn, hist, dy)


def _hgrn_prep(mm, qr, fr, lbl):
    n, c, _ = qr.shape
    lb = jax.nn.sigmoid(lbl[1:2, :] - lbl[0:1, :])
    f = lb + (1.0 - lb) * jax.nn.sigmoid(fr)
    q = jax.nn.silu(qr)
    k = 1.0 - f
    logf = jnp.log(f)
    ri = lax.broadcasted_iota(jnp.int32, (n, c, c), 1)
    ci = lax.broadcasted_iota(jnp.int32, (n, c, c), 2)
    b = mm((ri >= ci).astype(F32), logf, "nn", EXACT_LHS)
    attn = _hgrn_attn(mm, q, k, b)
    b_last = jnp.sum(logf, axis=1, keepdims=True)
    return q * jnp.exp(b), k * jnp.exp(b_last - b), attn, jnp.exp(b_last)


HGRN_SUB = 8
HGRN_PREP_ROWS = 4096


@functools.partial(jax.custom_vjp, nondiff_argnums=(1,))
def _roll_rows(x, shift):
    return pltpu.roll(x, shift, x.ndim - 2)


def _roll_rows_fwd(x, shift):
    return _roll_rows(x, shift), None


def _roll_rows_bwd(shift, _, d):
    return (pltpu.roll(d, d.shape[-2] - shift, d.ndim - 2),)


_roll_rows.defvjp(_roll_rows_fwd, _roll_rows_bwd)


@jax.custom_vjp
def _exp_clamped(v):
    return jnp.exp(jnp.minimum(v, 0.0))


def _exp_clamped_fwd(v):
    out = jnp.exp(jnp.minimum(v, 0.0))
    return out, out


def _exp_clamped_bwd(out, d):
    return (d * out,)


_exp_clamped.defvjp(_exp_clamped_fwd, _exp_clamped_bwd)


def _hgrn_attn(mm, q, k, b):
    n, c, d = q.shape
    sb = HGRN_SUB
    sub = lambda a: a.reshape(n * c // sb, sb, d)
    qs, ks, bs = sub(q), sub(k), sub(b)
    row = lax.broadcasted_iota(jnp.int32, (n, c, c), 1)
    col = lax.broadcasted_iota(jnp.int32, (n, c, c), 2)
    same_block = (row & -sb) == (col & -sb)
    attn = None
    for delta in range(sb):
        if delta == 0:
            prod = qs * ks
        else:
            prod = qs * _roll_rows(ks, delta) * _exp_clamped(bs - _roll_rows(bs, delta))
        sums = jnp.sum(prod, axis=-1, keepdims=True).reshape(n, c, 1)
        term = jnp.where(same_block & (row - col == delta), sums, 0.0)
        attn = term if attn is None else attn + term
    far = [jnp.zeros((n, sb, c), F32)]
    for i in range(1, c // sb):
        r0 = i * sb
        bi = b[:, r0:r0 + sb, :]
        ref = bi[:, 0:1, :]
        before = jnp.concatenate([k[:, :r0, :] * jnp.exp(ref - b[:, :r0, :]), jnp.zeros((n, c - r0, d), F32)], axis=1)
        far.append(mm(q[:, r0:r0 + sb, :] * jnp.exp(bi - ref), before, "nt", ONE_PASS))
    return attn + jnp.concatenate(far, axis=1)


def _hgrn_scan(mm, qe, kd, attn, ebl, iv, z, gn, state):
    o = mm(qe, state, "nt", ONE_PASS) + mm(attn, iv, "nn", ONE_PASS)
    new_state = state * ebl + mm(iv, kd, "tn", ONE_PASS)
    return _rms_gate(o, gn, z), new_state


def _hgrn_prep_specs(tb):
    col = pl.BlockSpec((tb, HEAD_DIM), lambda h, i: (i, h))
    lbs = pl.BlockSpec((2, HEAD_DIM), lambda h, i: (0, h))
    att = pl.BlockSpec((1, tb, HGRN_CHUNK), lambda h, i: (h, i, 0))
    ebl = pl.BlockSpec((1, tb // HGRN_CHUNK, SUBLANES, LANES), lambda h, i: (h, i, 0, 0))
    return col, lbs, att, ebl


def _hgrn_prep_fwd(qr, fr, lower_bounds):
    t = qr.shape[0]
    tb = min(t, HGRN_PREP_ROWS)
    nt, nc = t // tb, tb // HGRN_CHUNK
    hh = HGRN_HEADS

    def body(q_ref, f_ref, lb_ref, qe_ref, kd_ref, at_ref, eb_ref):
        ch = lambda r: _chunks(r, nc, HGRN_CHUNK)
        qe, kd, attn, ebl = _hgrn_prep(_mm_raw, ch(q_ref[...]), ch(f_ref[...]), lb_ref[...])
        qe_ref[...] = qe.reshape(tb, HEAD_DIM).astype(BF16)
        kd_ref[...] = kd.reshape(tb, HEAD_DIM).astype(BF16)
        at_ref[0] = attn.reshape(tb, HGRN_CHUNK).astype(BF16)
        eb_ref[0] = jnp.broadcast_to(ebl, (nc, SUBLANES, LANES))

    col, lbs, att, ebs = _hgrn_prep_specs(tb)
    return pl.pallas_call(
        body, name="hgrn_prep_fwd", grid=(hh, nt), in_specs=[col, col, lbs], out_specs=[col, col, att, ebs],
        out_shape=[jax.ShapeDtypeStruct((t, HGRN_WIDTH), BF16)] * 2
        + [jax.ShapeDtypeStruct((hh, t, HGRN_CHUNK), BF16), jax.ShapeDtypeStruct((hh, t // HGRN_CHUNK, SUBLANES, LANES), F32)],
        compiler_params=_params("parallel", "parallel"))(qr, fr, lower_bounds)


def _hgrn_prep_bwd(qr, fr, lower_bounds, dqe, dkd, dattn, debl):
    t = qr.shape[0]
    tb = min(t, HGRN_PREP_ROWS)
    nt, nc = t // tb, tb // HGRN_CHUNK
    hh = HGRN_HEADS

    def body(q_ref, f_ref, lb_ref, dqe_ref, dkd_ref, dat_ref, deb_ref, dq_ref, df_ref, dlb_ref):
        @pl.when(pl.program_id(1) == 0)
        def _():
            dlb_ref[...] = jnp.zeros_like(dlb_ref)

        ch = lambda r: _chunks(r, nc, HGRN_CHUNK)
        _, vjp = jax.vjp(functools.partial(_hgrn_prep, _mm_vjp), ch(q_ref[...]), ch(f_ref[...]), lb_ref[...])
        dq, df, dlb = vjp((ch(dqe_ref[...]), ch(dkd_ref[...]), ch(dat_ref[0]), deb_ref[0][:, 0:1, :]))
        dq_ref[...] = dq.reshape(tb, HEAD_DIM).astype(BF16)
        df_ref[...] = df.reshape(tb, HEAD_DIM).astype(BF16)
        dlb_ref[...] += dlb

    col, lbs, att, ebs = _hgrn_prep_specs(tb)
    return pl.pallas_call(
        body, name="hgrn_prep_bwd", grid=(hh, nt), in_specs=[col, col, lbs, col, col, att, ebs],
        out_specs=[col, col, lbs],
        out_shape=[jax.ShapeDtypeStruct((t, HGRN_WIDTH), BF16)] * 2 + [jax.ShapeDtypeStruct((2, HGRN_WIDTH), F32)],
        compiler_params=_params("parallel", "arbitrary"))(qr, fr, lower_bounds, dqe, dkd, dattn, debl)


def _hgrn_scan_fwd(qe, kd, attn, ebl, iv, z, gn):
    t = qe.shape[0]
    tb = min(t, 128)
    nt, nc = t // tb, tb // HGRN_CHUNK
    hh = HGRN_HEADS

    def body(qe_ref, kd_ref, at_ref, eb_ref, i_ref, z_ref, gn_ref, y_ref, hist_ref, s_ref):
        @pl.when(pl.program_id(0) == 0)
        def _():
            s_ref[...] = jnp.zeros_like(s_ref)

        g = gn_ref[0:1, :]
        state = s_ref[...]
        for c in range(nc):
            rows = pl.ds(c * HGRN_CHUNK, HGRN_CHUNK)
            heads = lambda r: _by_head(r, rows, hh)
            hist_ref[:, c] = state
            y, state = _hgrn_scan(_mm_raw, heads(qe_ref), heads(kd_ref), at_ref[:, rows, :], eb_ref[:, c, 0:1, :],
                                  heads(i_ref), heads(z_ref), g, state)
            _store_heads(y_ref, rows, y.astype(BF16))
        s_ref[...] = state

    row, att, ebs, hist, gns = _scan_specs(tb, hh, HGRN_CHUNK, False, nt)
    return pl.pallas_call(
        body, name="hgrn_scan_fwd", grid=(nt,), in_specs=[row, row, att, ebs, row, row, gns], out_specs=[row, hist],
        out_shape=[jax.ShapeDtypeStruct((t, HGRN_WIDTH), BF16),
                   jax.ShapeDtypeStruct((hh, t // HGRN_CHUNK, HEAD_DIM, HEAD_DIM), F32)],
        scratch_shapes=[pltpu.VMEM((hh, HEAD_DIM, HEAD_DIM), F32)],
        compiler_params=_params("arbitrary"))(qe, kd, attn, ebl, iv, z, gn)


def _hgrn_scan_bwd(qe, kd, attn, ebl, iv, z, gn, hist, dy):
    t = qe.shape[0]
    tb = min(t, 128)
    nt, nc = t // tb, tb // HGRN_CHUNK
    hh = HGRN_HEADS

    def body(qe_ref, kd_ref, at_ref, eb_ref, i_ref, z_ref, gn_ref, hist_ref, dy_ref,
             dqe_ref, dkd_ref, dat_ref, deb_ref, di_ref, dz_ref, dgn_ref, ds_ref):
        @pl.when(pl.program_id(0) == 0)
        def _():
            ds_ref[...] = jnp.zeros_like(ds_ref)
            dgn_ref[...] = jnp.zeros_like(dgn_ref)

        g = gn_ref[0:1, :]
        d_state = ds_ref[...]
        for c in reversed(range(nc)):
            rows = pl.ds(c * HGRN_CHUNK, HGRN_CHUNK)
            heads = lambda r: _by_head(r, rows, hh).astype(F32)
            _, vjp = jax.vjp(functools.partial(_hgrn_scan, _mm_vjp), heads(qe_ref), heads(kd_ref),
                             at_ref[:, rows, :].astype(F32), eb_ref[:, c, 0:1, :], heads(i_ref), heads(z_ref), g,
                             hist_ref[:, c])
            dqe, dkd, dat, deb, di, dz, dgn, d_state = vjp((heads(dy_ref), d_state))
            _store_heads(dqe_ref, rows, dqe)
            _store_heads(dkd_ref, rows, dkd)
            dat_ref[:, rows, :] = dat
            deb_ref[:, c] = jnp.broadcast_to(deb, (hh, SUBLANES, LANES))
            _store_heads(di_ref, rows, di.astype(BF16))
            _store_heads(dz_ref, rows, dz.astype(BF16))
            dgn_ref[0:1, :] += dgn
        ds_ref[...] = d_state

    row, att, ebs, hists, gns = _scan_specs(tb, hh, HGRN_CHUNK, True, nt)
    wide = lambda dt: jax.ShapeDtypeStruct((t, HGRN_WIDTH), dt)
    return pl.pallas_call(
        body, name="hgrn_scan_bwd", grid=(nt,),
        in_specs=[row, row, att, ebs, row, row, gns, hists, row],
        out_specs=[row, row, att, ebs, row, row, gns],
        out_shape=[wide(F32), wide(F32), jax.ShapeDtypeStruct((hh, t, HGRN_CHUNK), F32),
                   jax.ShapeDtypeStruct((hh, t // HGRN_CHUNK, SUBLANES, LANES), F32), wide(BF16), wide(BF16),
                   jax.ShapeDtypeStruct((SUBLANES, LANES), F32)],
        scratch_shapes=[pltpu.VMEM((hh, HEAD_DIM, HEAD_DIM), F32)],
        compiler_params=_params("arbitrary"))(qe, kd, attn, ebl, iv, z, gn, hist, dy)


def _layer_norm(pre, g, b):
    mu = jnp.mean(pre, axis=-1, keepdims=True)
    d = pre - mu
    var = jnp.mean(d * d, axis=-1, keepdims=True)
    return d * lax.rsqrt(var + NORM_EPS) * g + b


def _lnpl_fwd(xin, s, p, wg, wpl, ln_g, ln_b):
    t = xin.shape[0]
    tt = min(t, 512)

    def body(x_ref, s_ref, p_ref, wg_ref, wpl_ref, g_ref, b_ref, o_ref, ob_ref):
        xn = _layer_norm(DEEPNORM_ALPHA * x_ref[...] + s_ref[...], g_ref[...], b_ref[...])
        gate = jax.nn.sigmoid(_mm_raw(xn, wg_ref[...], "nn", ONE_PASS))
        out = xn + _mm_raw(p_ref[...], wpl_ref[...], "nn", ONE_PASS) * gate
        o_ref[...] = out
        ob_ref[...] = out.astype(BF16)

    row = lambda w: pl.BlockSpec((tt, w), lambda i: (i, 0))
    full = lambda a: pl.BlockSpec(a.shape, lambda i: (0, 0))
    return pl.pallas_call(
        body, name="lnpl_fwd", grid=(t // tt,),
        in_specs=[row(D_MODEL), row(D_MODEL), row(PL_DIM), full(wg), full(wpl), full(ln_g), full(ln_b)],
        out_specs=[row(D_MODEL), row(D_MODEL)],
        out_shape=[jax.ShapeDtypeStruct((t, D_MODEL), F32), jax.ShapeDtypeStruct((t, D_MODEL), BF16)],
        compiler_params=_params("parallel"))(xin, s, p, wg, wpl, ln_g, ln_b)


def _lnpl_bwd(xin, s, p, wg, wpl, ln_g, ln_b, upstream, last, name):
    t = xin.shape[0]
    tt = min(t, 512)

    def body(x_ref, s_ref, p_ref, wg_ref, wpl_ref, g_ref, b_ref, up_ref,
             dpre_ref, dwg_ref, dwpl_ref, dg_ref, db_ref, loss_ref, dpre_b_ref):
        @pl.when(pl.program_id(0) == 0)
        def _():
            for r in (dwg_ref, dwpl_ref, dg_ref, db_ref, loss_ref):
                r[...] = jnp.zeros_like(r)

        pre = DEEPNORM_ALPHA * x_ref[...] + s_ref[...]
        xn, ln_vjp = jax.vjp(_layer_norm, pre, g_ref[...], b_ref[...])
        gate = jax.nn.sigmoid(_mm_raw(xn, wg_ref[...], "nn", ONE_PASS))
        plv = _mm_raw(p_ref[...], wpl_ref[...], "nn", ONE_PASS)
        if last:
            err = xn + plv * gate - up_ref[...]
            dout = err * (1.0 / D_MODEL)
            tot = jnp.sum(jnp.sum(err * err, axis=1, keepdims=True), axis=0, keepdims=True) * (0.5 / D_MODEL)
            loss_ref[...] += jnp.broadcast_to(tot, loss_ref.shape)
        else:
            dout = up_ref[...]
        dplv = dout * gate
        dlogits = dout * plv * gate * (1.0 - gate)
        dwg_ref[...] += _mm_raw(xn, dlogits, "tn", ONE_PASS)
        dwpl_ref[...] += _mm_raw(p_ref[...], dplv, "tn", ONE_PASS)
        dxn = dout + _mm_raw(dlogits, wg_ref[...], "nt", ONE_PASS)
        dpre, dg, db = ln_vjp(dxn)
        dpre_ref[...] = dpre
        dpre_b_ref[...] = dpre.astype(BF16)
        dg_ref[...] += dg
        db_ref[...] += db

    row = lambda w: pl.BlockSpec((tt, w), lambda i: (i, 0))
    full = lambda shape: pl.BlockSpec(shape, lambda i: (0, 0))
    return pl.pallas_call(
        body, name=name, grid=(t // tt,),
        in_specs=[row(D_MODEL), row(D_MODEL), row(PL_DIM), full(wg.shape), full(wpl.shape), full(ln_g.shape),
                  full(ln_b.shape), row(D_MODEL)],
        out_specs=[row(D_MODEL), full(wg.shape), full(wpl.shape), full(ln_g.shape), full(ln_b.shape),
                   full((SUBLANES, LANES)), row(D_MODEL)],
        out_shape=[jax.ShapeDtypeStruct((t, D_MODEL), F32), jax.ShapeDtypeStruct(wg.shape, F32),
                   jax.ShapeDtypeStruct(wpl.shape, F32), jax.ShapeDtypeStruct(ln_g.shape, F32),
                   jax.ShapeDtypeStruct(ln_b.shape, F32), jax.ShapeDtypeStruct((SUBLANES, LANES), F32),
                   jax.ShapeDtypeStruct((t, D_MODEL), BF16)],
        compiler_params=_params("arbitrary"))(xin, s, p, wg, wpl, ln_g, ln_b, upstream)


def _pack_tail(dbr, dar):
    nh, t, _ = dbr.shape
    tt = min(t, 512)

    def body(b_ref, a_ref, o_ref):
        lane = lax.broadcasted_iota(jnp.int32, (tt, LANES), 1)
        acc = jnp.zeros((tt, LANES), F32)
        for h in range(nh):
            acc = jnp.where(lane == h, b_ref[h], acc)
            acc = jnp.where(lane == nh + h, a_ref[h], acc)
        o_ref[...] = acc.astype(BF16)

    spec = pl.BlockSpec((nh, tt, LANES), lambda i: (0, i, 0))
    return pl.pallas_call(
        body, name="pack_tail", grid=(t // tt,), in_specs=[spec, spec], out_specs=pl.BlockSpec((tt, LANES), lambda i: (i, 0)),
        out_shape=jax.ShapeDtypeStruct((t, LANES), BF16), compiler_params=_params("parallel"))(dbr, dar)


def _rep_rows(v):
    return jnp.broadcast_to(v.reshape(1, LANES), (SUBLANES, LANES))


def _rep_heads(v):
    return jnp.broadcast_to(v.reshape(-1, 1, 1), (v.shape[0], SUBLANES, LANES))


def _col_range(stacked, lo, hi):
    c = stacked.shape[2]
    parts = [stacked[s, :, max(lo, s * c) - s * c:min(hi, (s + 1) * c) - s * c]
             for s in range(4) if max(lo, s * c) < min(hi, (s + 1) * c)]
    return parts[0] if len(parts) == 1 else jnp.concatenate(parts, axis=1)


def _col_shards(pieces, c):
    shards, offs, o = [], [], 0
    for pc in pieces:
        offs.append(o)
        o += pc.shape[1]
    for s in range(4):
        lo, hi = s * c, (s + 1) * c
        parts = [pc[:, max(lo, o) - o:min(hi, o + pc.shape[1]) - o] for pc, o in zip(pieces, offs)
                 if max(lo, o) < min(hi, o + pc.shape[1])]
        shards.append(parts[0] if len(parts) == 1 else jnp.concatenate(parts, axis=1))
    return jnp.stack(shards)


def _local_step(x, p, target, w, late_weights, early_grads_ready, early_grads_swapped, last_grad_ready, start_token):
    a = DEEPNORM_ALPHA
    nh = GDN_HEADS
    xb = (x + start_token).astype(BF16)
    wie = w["w_in_even"]
    w_a, w_qkv, w_zb = _col_range(wie, 0, 4096), _col_range(wie, 4096, 7168), _col_range(wie, 7168, 8192)
    w_tail = jnp.pad(_col_range(wie, 8192, 8192 + 2 * nh), ((0, 0), (0, LANES - 2 * nh)))
    conv_a_w, conv_b_w = w["conv_a_w"], w["conv_b_w"]
    ln_g0, ln_b0, ln_g1, ln_b1 = (v.reshape(1, D_MODEL) for v in (w["ln_g"][0], w["ln_b"][0], w["ln_g"][1], w["ln_b"][1]))
    alog, dtb = _rep_heads(w["a_log"].reshape(nh)), _rep_heads(w["dt_bias"].reshape(nh))
    gdn_g, hgrn_g = _rep_rows(w["gdn_norm_g"]), _rep_rows(w["hgrn_norm_g"])

    proj_a = _matmul(xb, w_a, name="fwd_proj_a")
    proj_qkv = _matmul(xb, w_qkv, name="fwd_proj_qkv")
    proj_zb = _matmul(xb, w_zb, name="fwd_proj_zb")
    proj_tail = _matmul(xb, w_tail, name="fwd_proj_tail")
    y_a = _conv_a_fwd(proj_a, conv_a_w)
    qkv_act = _conv_b_fwd(proj_qkv, conv_b_w)
    *gdn_pre, gdn_inv = _gdn_prep_fwd(qkv_act, proj_tail, alog, dtb)
    y_b, gdn_hist = _gdn_scan_fwd(*gdn_pre, proj_zb, gdn_g)
    w = {**w, **late_weights(y_b)}
    woe, wio, woo = w["w_out_even"], w["w_in_odd"], w["w_out_odd"]
    s0 = _matmul(y_b, woe[1024:], name="fwd_out_even_b", add=_matmul(y_a, woe[:1024], name="fwd_out_even_a"))
    x1, x1b = _lnpl_fwd(x, s0, p[0], w["w_pl_gate"][0], w["w_pl"][0], ln_g0, ln_b0)
    proj_o = [_matmul(x1b, wio[j], name=f"fwd_proj_odd{j}") for j in range(4)]
    hgrn_pre = _hgrn_prep_fwd(proj_o[0], proj_o[1], w["lower_bounds"])
    y_o, hgrn_hist = _hgrn_scan_fwd(*hgrn_pre, proj_o[2], proj_o[3], hgrn_g)
    s1 = _matmul(y_o, woo, name="fwd_out_odd")

    g = {}
    dpre1, dwg1, dwpl1, dlng1, dlnb1, loss, dpre1b = _lnpl_bwd(x1, s1, p[1], w["w_pl_gate"][1], w["w_pl"][1], ln_g1,
                                                             ln_b1, target, True, "lnpl_bwd_odd")
    dy_o = _matmul(dpre1b, woo, tb=True, name="bwd_out_odd_dx")
    g["w_out_odd"] = _matmul(y_o, dpre1b, ta=True, name="bwd_out_odd_dw")
    dqe, dkd, dat, deb, di, dz, dhg = _hgrn_scan_bwd(*hgrn_pre, proj_o[2], proj_o[3], hgrn_g, hgrn_hist, dy_o)
    dq, df, dlb = _hgrn_prep_bwd(proj_o[0], proj_o[1], w["lower_bounds"], dqe, dkd, dat, deb)
    dx1 = dpre1
    scale = a
    dws = []
    for j, dj in enumerate((dq, df, di, dz)):
        dx1 = _matmul(dj, wio[j], tb=True, add=dx1, add_scale=scale, name=f"bwd_proj_odd_dx{j}")
        scale = 1.0
        dws.append(_matmul(x1b, dj, ta=True, name=f"bwd_proj_odd_dw{j}"))
    g["w_in_odd"] = jnp.stack(dws)
    g["hgrn_norm_g"] = dhg[0:1]
    g["lower_bounds"] = dlb
    g["w_pl_gate1"], g["w_pl1"] = dwg1, dwpl1

    dpre0, dwg0, dwpl0, dlng0, dlnb0, _, dpre0b = _lnpl_bwd(x, s0, p[0], w["w_pl_gate"][0], w["w_pl"][0], ln_g0, ln_b0,
                                                          dx1, False, "lnpl_bwd_even")
    g["w_pl_gate0"], g["w_pl0"] = dwg0, dwpl0
    g["ln_g"] = jnp.concatenate([dlng0, dlng1], axis=0)
    g["ln_b"] = jnp.concatenate([dlnb0, dlnb1], axis=0)
    dy_a = _matmul(dpre0b, woe[:1024], tb=True, name="bwd_out_even_dxa")
    dy_b = _matmul(dpre0b, woe[1024:], tb=True, name="bwd_out_even_dxb")
    g["w_out_even"] = jnp.concatenate([_matmul(y_a, dpre0b, ta=True, name="bwd_out_even_dwa"),
                                       _matmul(y_b, dpre0b, ta=True, name="bwd_out_even_dwb")], axis=0)
    token = early_grads_ready({n: g[n] for n in _RS_EARLY})
    d_a, dwa = _conv_a_bwd(proj_a, conv_a_w + token, dy_a)
    g["conv_a_w"] = dwa[:3]
    gdn_g = gdn_g + early_grads_swapped(d_a)
    du, dw, dqd, dkd, dat, deg, dzb, dgn = _gdn_scan_bwd(*gdn_pre, proj_zb, gdn_g, gdn_hist, dy_b)
    dqa, dka, dva, dbr, dar, dal, ddt = _gdn_prep_bwd(qkv_act, proj_tail, alog, dtb, gdn_inv, du, dw, dqd, dkd, dat, deg)
    g["a_log"] = dal[:, 0, 0].reshape(1, nh)
    g["dt_bias"] = ddt[:, 0, 0].reshape(1, nh)
    g["gdn_norm_g"] = dgn[0:1]
    d_pre_qkv, dwb = [], []
    for j, dj in enumerate((dqa, dka, dva)):
        dpj, dwj = _conv_b_bwd(proj_qkv, conv_b_w, dj, j, f"conv_b_bwd{j}")
        d_pre_qkv.append(dpj)
        dwb.append(dwj[:4])
    g["conv_b_w"] = jnp.concatenate(dwb, axis=1)
    d_tail = _pack_tail(dbr, dar)
    pieces = [(d_a, w_a), (d_pre_qkv[0], w_qkv[:, :1024]), (d_pre_qkv[1], w_qkv[:, 1024:2048]),
              (d_pre_qkv[2], w_qkv[:, 2048:]), (dzb, w_zb), (d_tail, w_tail)]
    dws = [_matmul(xb, dj, ta=True, name=f"bwd_proj_even_dw{j}") for j, (dj, _) in enumerate(pieces)]
    dws[-1] = dws[-1][:, :2 * nh]
    g["w_in_even"] = _col_shards(dws, wie.shape[2])
    token = last_grad_ready(g["w_in_even"])
    dx = dpre0
    scale = a
    for j, (dj, wj) in enumerate(pieces):
        dx = _matmul(dj, wj + jnp.asarray(token).astype(BF16) if j == 0 else wj, tb=True, add=dx, add_scale=scale,
                     name=f"bwd_proj_even_dx{j}")
        scale = 1.0
    return loss, dx, g


def _adamw(w, g, m, v, name):
    lead, rows, cols = w.shape
    if rows % SUBLANES == 0 or rows <= 256:
        tr, tc = (rows if rows <= 256 else 256), cols
    else:
        tr, tc = rows, 256
    assert rows % tr == 0 and cols % tc == 0, (name, rows, cols)

    def body(w_ref, g_ref, m_ref, v_ref, d_ref, nm_ref, nv_ref):
        gg = g_ref[...]
        nm = ADAM_B1 * m_ref[...] + (1.0 - ADAM_B1) * gg
        nv = ADAM_B2 * v_ref[...] + (1.0 - ADAM_B2) * jnp.square(gg)
        m_hat = nm / (1.0 - ADAM_B1 ** ADAM_STEP)
        v_hat = nv / (1.0 - ADAM_B2 ** ADAM_STEP)
        d_ref[...] = -ADAM_LR * (m_hat / (jnp.sqrt(v_hat) + ADAM_EPS) + ADAM_WD * w_ref[...])
        nm_ref[...] = nm
        nv_ref[...] = nv

    spec = pl.BlockSpec((1, tr, tc), lambda l, i, j: (l, i, j))
    return pl.pallas_call(
        body, name=name, grid=(lead, rows // tr, cols // tc), in_specs=[spec] * 4, out_specs=[spec] * 3,
        out_shape=[jax.ShapeDtypeStruct(w.shape, F32)] * 3,
        compiler_params=_params("parallel", "parallel", "parallel"))(w, g, m, v)


MESH = pl.DeviceIdType.MESH
N_DEV = 8
HBM_SPEC = pl.BlockSpec(memory_space=pltpu.HBM)
VMEM_SPEC = pl.BlockSpec(memory_space=pltpu.VMEM)


def _coords():
    return lax.axis_index("x"), lax.axis_index("y"), lax.axis_index("c")


def _flip(v, bit):
    return 1 - v if bit else v


def _remote(src, dst, send_sem, recv_sem, dev):
    return pltpu.make_async_remote_copy(src_ref=src, dst_ref=dst, send_sem=send_sem, recv_sem=recv_sem,
                                        device_id=dev, device_id_type=MESH)


def _exchange_small(buf, reduce, name):
    rows = buf.shape[0]

    def body(in_ref, out_ref, slots, send_sems, recv_sems):
        x, y, c = _coords()
        me = 4 * x + 2 * y + c
        slots[me] = in_ref[...]
        peer = lambda k: (_flip(x, (k >> 2) & 1), _flip(y, (k >> 1) & 1), _flip(c, k & 1))
        sends = []
        for k in range(1, N_DEV):
            cp = _remote(in_ref, slots.at[me], send_sems.at[k - 1], recv_sems.at[k - 1], peer(k))
            cp.start()
            sends.append(cp)
        for k in range(1, N_DEV):
            px, py, pc = peer(k)
            _remote(in_ref, slots.at[4 * px + 2 * py + pc], send_sems.at[k - 1], recv_sems.at[k - 1], peer(k)).wait_recv()
        for cp in sends:
            cp.wait_send()
        if reduce:
            acc = slots[0]
            for d in range(1, N_DEV):
                acc = acc + slots[d]
            out_ref[...] = acc
        else:
            out_ref[...] = slots[...]

    out_shape = (rows, LANES) if reduce else (N_DEV, rows, LANES)
    return pl.pallas_call(
        body, name=name, in_specs=[VMEM_SPEC], out_specs=VMEM_SPEC, out_shape=jax.ShapeDtypeStruct(out_shape, F32),
        scratch_shapes=[pltpu.VMEM((N_DEV, rows, LANES), F32), pltpu.SemaphoreType.DMA((N_DEV - 1,)),
                        pltpu.SemaphoreType.DMA((N_DEV - 1,))])(buf)


def _half_rows(half, which):
    return pl.ds(pl.multiple_of(which * half, 16), half)


def _other_chip(x, y, k):
    return _flip(x, (k >> 1) & 1), _flip(y, k & 1)


SEM_SPEC = pl.BlockSpec(memory_space=pltpu.SEMAPHORE)
DATAFLOW = pltpu.SideEffectType.DATAFLOW_SIDE_EFFECTING


def _ici_piece(srcs, lands, send_sems, recv_sems, i, k, x, y, c):
    half = srcs[i].shape[0] // 2
    ox, oy = _other_chip(x, y, k)
    return _remote(srcs[i].at[_half_rows(half, c)], lands[i].at[2 * x + y, _half_rows(half, c)],
                   send_sems.at[3 * i + k - 1], recv_sems.at[3 * i + k - 1], (ox, oy, c)), (ox, oy)


def _gather_start(shards, name):
    n = len(shards)

    def body(*refs):
        srcs, lands = refs[:n], refs[n:2 * n]
        send_sems, recv_sems = refs[2 * n], refs[2 * n + 1]
        token = refs[-1]
        x, y, c = _coords()
        for i in range(n):
            for k in (1, 2, 3):
                _ici_piece(srcs, lands, send_sems, recv_sems, i, k, x, y, c)[0].start()
        token[...] = jnp.zeros_like(token)

    hbm = lambda a: pltpu.with_memory_space_constraint(a, pltpu.HBM)
    lands = [lax.empty((4,) + s.shape, s.dtype) for s in shards]
    out = pl.pallas_call(
        body, name=name,
        out_shape=(pltpu.SemaphoreType.DMA((3 * n,)), pltpu.SemaphoreType.DMA((3 * n,)),
                   *[pltpu.HBM(s.shape, s.dtype) for s in shards], *[pltpu.HBM(a.shape, a.dtype) for a in lands],
                   jax.ShapeDtypeStruct((SUBLANES, LANES), F32)),
        in_specs=[HBM_SPEC] * (2 * n), out_specs=(SEM_SPEC, SEM_SPEC, *[HBM_SPEC] * (2 * n), VMEM_SPEC),
        input_output_aliases={i: 2 + i for i in range(2 * n)},
        compiler_params=pltpu.CompilerParams(has_side_effects=DATAFLOW))(*[hbm(s) for s in shards], *[hbm(a) for a in lands])
    return out[0], out[1], out[2:2 + n], out[2 + n:2 + 2 * n], out[-1]


def _gather_wait(send_sems, recv_sems, srcs, lands, after, name):
    n = len(srcs)

    def body(*refs):
        src_refs, land_refs = refs[:n], refs[n:2 * n]
        send_sems, recv_sems = refs[2 * n], refs[2 * n + 1]
        x, y, c = _coords()
        for i in range(n):
            half = src_refs[i].shape[0] // 2
            for k in (1, 2, 3):
                cp, (ox, oy) = _ici_piece(src_refs, land_refs, send_sems, recv_sems, i, k, x, y, c)
                cp.wait_send()
                piece = land_refs[i].at[2 * ox + oy, _half_rows(half, c)]
                _remote(piece, piece, send_sems.at[3 * i + k - 1], recv_sems.at[3 * i + k - 1], (ox, oy, c)).wait_recv()

    out = pl.pallas_call(
        body, name=name,
        out_shape=(*[pltpu.HBM(s.shape, s.dtype) for s in srcs], *[pltpu.HBM(a.shape, a.dtype) for a in lands]),
        in_specs=[HBM_SPEC] * (2 * n) + [SEM_SPEC, SEM_SPEC, pl.BlockSpec(memory_space=pl.ANY)],
        out_specs=tuple([HBM_SPEC] * (2 * n)), input_output_aliases={i: i for i in range(2 * n)},
        compiler_params=pltpu.CompilerParams(has_side_effects=DATAFLOW))(*srcs, *lands, send_sems, recv_sems, after)
    return out[n:]


def _gather_forward(lands, name):
    n = len(lands)

    def body(*refs):
        ins, outs = refs[:n], refs[n:2 * n]
        send_sems, recv_sems = refs[2 * n:]
        x, y, c = _coords()
        sends = []
        for i in range(n):
            half = ins[i].shape[1] // 2
            for k in (1, 2, 3):
                ox, oy = _other_chip(x, y, k)
                cp = _remote(ins[i].at[2 * ox + oy, _half_rows(half, c)], outs[i].at[2 * ox + oy, _half_rows(half, c)],
                             send_sems.at[3 * i + k - 1], recv_sems.at[3 * i + k - 1], (x, y, 1 - c))
                cp.start()
                sends.append(cp)
        for i in range(n):
            half = ins[i].shape[1] // 2
            for k in (1, 2, 3):
                ox, oy = _other_chip(x, y, k)
                piece = outs[i].at[2 * ox + oy, _half_rows(half, 1 - c)]
                _remote(piece, piece, send_sems.at[3 * i + k - 1], recv_sems.at[3 * i + k - 1], (x, y, 1 - c)).wait_recv()
        for cp in sends:
            cp.wait_send()

    return pl.pallas_call(
        body, name=name, in_specs=[HBM_SPEC] * n, out_specs=[HBM_SPEC] * n,
        out_shape=[jax.ShapeDtypeStruct(a.shape, a.dtype) for a in lands],
        input_output_aliases={i: i for i in range(n)},
        scratch_shapes=[pltpu.SemaphoreType.DMA((3 * n,))] * 2)(*lands)


def _rs_sibling_swap(g4s, name):
    n = len(g4s)

    def body(*refs):
        ins, outs = refs[:n], refs[n:2 * n]
        send_sems, recv_sems = refs[2 * n:]
        x, y, c = _coords()
        sends = []
        for i in range(n):
            half = ins[i].shape[1] // 2
            for s in range(4):
                cp = _remote(ins[i].at[s, _half_rows(half, 1 - c)], outs[i].at[s], send_sems.at[4 * i + s],
                             recv_sems.at[4 * i + s], (x, y, 1 - c))
                cp.start()
                sends.append(cp)
        for cp in sends:
            cp.wait_recv()
        for cp in sends:
            cp.wait_send()

    return pl.pallas_call(
        body, name=name, in_specs=[HBM_SPEC] * n, out_specs=[HBM_SPEC] * n,
        out_shape=[jax.ShapeDtypeStruct((4, g.shape[1] // 2, g.shape[2]), g.dtype) for g in g4s],
        scratch_shapes=[pltpu.SemaphoreType.DMA((4 * n,))] * 2)(*g4s)


def _rs_add_sibling(g4, got, c_idx, name):
    _, rows, cols = g4.shape
    half = rows // 2
    tr = min(half, 256)
    nb = half // tr

    def body(c_ref, a_ref, b_ref, o_ref, ob_ref):
        total = a_ref[...] + b_ref[...]
        o_ref[...] = total
        ob_ref[...] = total.astype(BF16)

    blk = (1, tr, cols)
    out = pl.BlockSpec(blk, lambda s, i, c_ref: (s, i, 0))
    grid_spec = pltpu.PrefetchScalarGridSpec(
        num_scalar_prefetch=1, grid=(4, nb),
        in_specs=[pl.BlockSpec(blk, lambda s, i, c_ref: (s, c_ref[0] * nb + i, 0)), out],
        out_specs=[out, out])
    return pl.pallas_call(
        body, name=name, grid_spec=grid_spec,
        out_shape=[jax.ShapeDtypeStruct(got.shape, F32), jax.ShapeDtypeStruct(got.shape, BF16)],
        compiler_params=_params("parallel", "parallel"))(c_idx, g4, got)


def _rs_add_chips(p4, got3, idx, name):
    _, half, cols = p4.shape
    tr = min(half, 256)
    nb = half // tr

    def body(idx_ref, p_ref, a_ref, b_ref, c_ref, o_ref):
        o_ref[...] = ((p_ref[0] + a_ref[0].astype(F32)) + b_ref[0].astype(F32)) + c_ref[0].astype(F32)

    blk = (1, tr, cols)
    grid_spec = pltpu.PrefetchScalarGridSpec(
        num_scalar_prefetch=1, grid=(nb,),
        in_specs=[pl.BlockSpec(blk, lambda i, idx_ref: (idx_ref[0], i, 0))]
        + [pl.BlockSpec(blk, functools.partial(lambda k, i, idx_ref: (k, i, 0), k)) for k in range(3)],
        out_specs=pl.BlockSpec((tr, cols), lambda i, idx_ref: (idx_ref[1] * nb + i, 0)))
    return pl.pallas_call(body, name=name, grid_spec=grid_spec, out_shape=jax.ShapeDtypeStruct((2 * half, cols), F32),
                          compiler_params=_params("parallel"))(idx, p4, got3, got3, got3)


def _rs_share_halves(bufs, name):
    n = len(bufs)

    def body(*refs):
        ins, outs = refs[:n], refs[n:2 * n]
        send_sems, recv_sems = refs[2 * n:]
        x, y, c = _coords()
        sends = []
        for i in range(n):
            half = ins[i].shape[0] // 2
            cp = _remote(ins[i].at[_half_rows(half, c)], outs[i].at[_half_rows(half, c)], send_sems.at[i],
                         recv_sems.at[i], (x, y, 1 - c))
            cp.start()
            sends.append(cp)
        for i in range(n):
            half = ins[i].shape[0] // 2
            _remote(ins[i].at[_half_rows(half, c)], outs[i].at[_half_rows(half, 1 - c)], send_sems.at[i],
                    recv_sems.at[i], (x, y, 1 - c)).wait_recv()
        for cp in sends:
            cp.wait_send()

    return pl.pallas_call(
        body, name=name, in_specs=[HBM_SPEC] * n, out_specs=[HBM_SPEC] * n,
        out_shape=[jax.ShapeDtypeStruct(b.shape, b.dtype) for b in bufs],
        input_output_aliases={i: i for i in range(n)},
        scratch_shapes=[pltpu.SemaphoreType.DMA((n,))] * 2)(*bufs)


def _scatter_piece(srcs, lands, send_sems, recv_sems, i, k, x, y, c):
    ox, oy = _other_chip(x, y, k)
    return _remote(srcs[i].at[2 * ox + oy], lands[i].at[k - 1], send_sems.at[3 * i + k - 1],
                   recv_sems.at[3 * i + k - 1], (ox, oy, c))


def _rs_scatter_start(p4s, name):
    n = len(p4s)

    def body(*refs):
        srcs, lands = refs[:n], refs[n:2 * n]
        send_sems, recv_sems = refs[2 * n], refs[2 * n + 1]
        token = refs[-1]
        x, y, c = _coords()
        for i in range(n):
            for k in (1, 2, 3):
                _scatter_piece(srcs, lands, send_sems, recv_sems, i, k, x, y, c).start()
        token[...] = jnp.zeros_like(token)

    hbm = lambda a: pltpu.with_memory_space_constraint(a, pltpu.HBM)
    lands = [lax.empty((3,) + p.shape[1:], p.dtype) for p in p4s]
    out = pl.pallas_call(
        body, name=name,
        out_shape=(pltpu.SemaphoreType.DMA((3 * n,)), pltpu.SemaphoreType.DMA((3 * n,)),
                   *[pltpu.HBM(p.shape, p.dtype) for p in p4s], *[pltpu.HBM(a.shape, a.dtype) for a in lands],
                   jax.ShapeDtypeStruct((SUBLANES, LANES), F32)),
        in_specs=[HBM_SPEC] * (2 * n), out_specs=(SEM_SPEC, SEM_SPEC, *[HBM_SPEC] * (2 * n), VMEM_SPEC),
        input_output_aliases={i: 2 + i for i in range(2 * n)},
        compiler_params=pltpu.CompilerParams(has_side_effects=DATAFLOW))(*[hbm(p) for p in p4s], *[hbm(a) for a in lands])
    return out[0], out[1], out[2:2 + n], out[2 + n:2 + 2 * n], out[-1]


def _rs_scatter_wait(send_sems, recv_sems, srcs, lands, after, name):
    n = len(srcs)

    def body(*refs):
        src_refs, land_refs = refs[:n], refs[n:2 * n]
        send_sems, recv_sems = refs[2 * n], refs[2 * n + 1]
        x, y, c = _coords()
        for i in range(n):
            for k in (1, 2, 3):
                cp = _scatter_piece(src_refs, land_refs, send_sems, recv_sems, i, k, x, y, c)
                cp.wait_send()
                cp.wait_recv()

    out = pl.pallas_call(
        body, name=name,
        out_shape=(*[pltpu.HBM(s.shape, s.dtype) for s in srcs], *[pltpu.HBM(a.shape, a.dtype) for a in lands]),
        in_specs=[HBM_SPEC] * (2 * n) + [SEM_SPEC, SEM_SPEC, pl.BlockSpec(memory_space=pl.ANY)],
        out_specs=tuple([HBM_SPEC] * (2 * n)), input_output_aliases={i: i for i in range(2 * n)},
        compiler_params=pltpu.CompilerParams(has_side_effects=DATAFLOW))(*srcs, *lands, send_sems, recv_sems, after)
    return out[n:]


def _swap_piece(srcs, lands, send_sems, recv_sems, i, s, x, y, c):
    half = srcs[i].shape[1] // 2
    return _remote(srcs[i].at[s, _half_rows(half, 1 - c)], lands[i].at[s], send_sems.at[4 * i + s],
                   recv_sems.at[4 * i + s], (x, y, 1 - c))


def _rs_swap_start(g4s, name):
    n = len(g4s)

    def body(*refs):
        srcs, lands = refs[:n], refs[n:2 * n]
        send_sems, recv_sems = refs[2 * n], refs[2 * n + 1]
        token = refs[-1]
        x, y, c = _coords()
        for i in range(n):
            for s in range(4):
                _swap_piece(srcs, lands, send_sems, recv_sems, i, s, x, y, c).start()
        token[...] = jnp.zeros_like(token)

    hbm = lambda a: pltpu.with_memory_space_constraint(a, pltpu.HBM)
    lands = [lax.empty((4, g.shape[1] // 2, g.shape[2]), g.dtype) for g in g4s]
    out = pl.pallas_call(
        body, name=name,
        out_shape=(pltpu.SemaphoreType.DMA((4 * n,)), pltpu.SemaphoreType.DMA((4 * n,)),
                   *[pltpu.HBM(g.shape, g.dtype) for g in g4s], *[pltpu.HBM(a.shape, a.dtype) for a in lands],
                   jax.ShapeDtypeStruct((SUBLANES, LANES), F32)),
        in_specs=[HBM_SPEC] * (2 * n), out_specs=(SEM_SPEC, SEM_SPEC, *[HBM_SPEC] * (2 * n), VMEM_SPEC),
        input_output_aliases={i: 2 + i for i in range(2 * n)},
        compiler_params=pltpu.CompilerParams(has_side_effects=DATAFLOW))(*[hbm(g) for g in g4s], *[hbm(a) for a in lands])
    return out[0], out[1], out[2:2 + n], out[2 + n:2 + 2 * n], out[-1]


def _rs_swap_wait(send_sems, recv_sems, srcs, lands, after, name):
    n = len(srcs)

    def body(*refs):
        src_refs, land_refs = refs[:n], refs[n:2 * n]
        send_sems, recv_sems = refs[2 * n], refs[2 * n + 1]
        x, y, c = _coords()
        for i in range(n):
            for s in range(4):
                cp = _swap_piece(src_refs, land_refs, send_sems, recv_sems, i, s, x, y, c)
                cp.wait_send()
                cp.wait_recv()

    out = pl.pallas_call(
        body, name=name,
        out_shape=(*[pltpu.HBM(s.shape, s.dtype) for s in srcs], *[pltpu.HBM(a.shape, a.dtype) for a in lands]),
        in_specs=[HBM_SPEC] * (2 * n) + [SEM_SPEC, SEM_SPEC, pl.BlockSpec(memory_space=pl.ANY)],
        out_specs=tuple([HBM_SPEC] * (2 * n)), input_output_aliases={i: i for i in range(2 * n)},
        compiler_params=pltpu.CompilerParams(has_side_effects=DATAFLOW))(*srcs, *lands, send_sems, recv_sems, after)
    return out[:n], out[n:]


def _rs_front(g4s, names, tag):
    c_idx = jnp.stack([lax.axis_index("c")]).astype(jnp.int32)
    got = _rs_sibling_swap(g4s, f"rs_sibling_swap_{tag}")
    return [_rs_add_sibling(g, s, c_idx, f"rs_add_sibling_{nm}") for g, s, nm in zip(g4s, got, names)]


def _rs_back(p4s, got3, names):
    x, y, c = _coords()
    idx = jnp.stack([2 * x + y, c]).astype(jnp.int32)
    return [_rs_add_chips(p, t, idx, f"rs_add_chips_{nm}") for (p, _), t, nm in zip(p4s, got3, names)]


def _cols_split(full):
    r, c4 = full.shape
    return full.reshape(r, 4, c4 // 4).transpose(1, 0, 2)


_BIG = {
    "w_in_even": ((1024, 2052), lambda s: s),
    "w_out_even": ((512, 1024), lambda s: s.reshape(2048, 1024)),
    "w_in_odd": ((1024, 2048), lambda s: s),
    "w_out_odd": ((512, 1024), lambda s: s.reshape(2048, 1024)),
    "w_pl": ((512, 256), lambda s: s.reshape(4, 2, 256, 256).transpose(1, 2, 0, 3).reshape(2, 256, 1024)),
    "w_pl_gate": ((512, 1024), lambda s: s.reshape(4, 2, 256, 1024).transpose(1, 0, 2, 3).reshape(2, 1024, 1024)),
}


_RS_EARLY = {
    "w_in_odd": lambda f: f,
    "w_out_odd": lambda f: f.reshape(4, 512, 1024),
    "w_pl_gate1": lambda f: f.reshape(4, 256, 1024),
    "w_pl1": _cols_split,
    "w_out_even": lambda f: f.reshape(4, 512, 1024),
    "w_pl_gate0": lambda f: f.reshape(4, 256, 1024),
    "w_pl0": _cols_split,
}
_RS_LATE = {"w_in_even": lambda f: f}


def _size(shape):
    n = 1
    for d in shape:
        n *= d
    return n


_SMALL = {"a_log": (1, 8), "dt_bias": (1, 8), "gdn_norm_g": (1, 128), "hgrn_norm_g": (1, 128),
          "lower_bounds": (2, 2048), "ln_g": (2, 1024), "ln_b": (2, 1024), "conv_a_w": (3, 1024), "conv_b_w": (4, 3072)}
_CONV_SHARD = {"conv_a_w": (3, 256), "conv_b_w": (4, 768)}


def _pack_small(parts, shapes, head_rows=0):
    rows = []
    for n, shape in shapes.items():
        v = parts[n].reshape(-1)
        rows.append(jnp.pad(v, (0, -v.shape[0] % LANES)).reshape(-1, LANES))
    buf = jnp.concatenate(rows, axis=0)
    return jnp.pad(buf, ((head_rows, -(buf.shape[0] + head_rows) % SUBLANES), (0, 0)))


def _unpack_small(buf, shapes, head_rows=0):
    out, off = {}, head_rows
    for n, shape in shapes.items():
        nrow = -(-_size(shape) // LANES)
        out[n] = buf[off:off + nrow].reshape(-1)[:_size(shape)].reshape(shape)
        off += nrow
    return out


_WEIGHTS = ["w_in_even", "conv_a_w", "conv_b_w", "a_log", "dt_bias", "gdn_norm_g", "w_out_even", "w_in_odd",
            "lower_bounds", "hgrn_norm_g", "w_out_odd", "ln_g", "ln_b", "w_pl", "w_pl_gate"]


def kernel(x, p, w_in_even, conv_a_w, conv_b_w, a_log, dt_bias, gdn_norm_g, w_out_even, w_in_odd, lower_bounds, hgrn_norm_g, w_out_odd, ln_g, ln_b, w_pl, w_pl_gate, loss_target, m_w_in_even, m_conv_a_w, m_conv_b_w, m_a_log, m_dt_bias, m_gdn_norm_g, m_w_out_even, m_w_in_odd, m_lower_bounds, m_hgrn_norm_g, m_w_out_odd, m_ln_g, m_ln_b, m_w_pl, m_w_pl_gate, v_w_in_even, v_conv_a_w, v_conv_b_w, v_a_log, v_dt_bias, v_gdn_norm_g, v_w_out_even, v_w_in_odd, v_lower_bounds, v_hgrn_norm_g, v_w_out_odd, v_ln_g, v_ln_b, v_w_pl, v_w_pl_gate):
    w = dict(zip(_WEIGHTS, (w_in_even, conv_a_w, conv_b_w, a_log, dt_bias, gdn_norm_g, w_out_even, w_in_odd,
                            lower_bounds, hgrn_norm_g, w_out_odd, ln_g, ln_b, w_pl, w_pl_gate)))
    m = dict(zip(_WEIGHTS, (m_w_in_even, m_conv_a_w, m_conv_b_w, m_a_log, m_dt_bias, m_gdn_norm_g, m_w_out_even,
                            m_w_in_odd, m_lower_bounds, m_hgrn_norm_g, m_w_out_odd, m_ln_g, m_ln_b, m_w_pl, m_w_pl_gate)))
    v = dict(zip(_WEIGHTS, (v_w_in_even, v_conv_a_w, v_conv_b_w, v_a_log, v_dt_bias, v_gdn_norm_g, v_w_out_even,
                            v_w_in_odd, v_lower_bounds, v_hgrn_norm_g, v_w_out_odd, v_ln_g, v_ln_b, v_w_pl, v_w_pl_gate)))
    chip = 2 * lax.axis_index("x") + lax.axis_index("y")

    names = list(_BIG)
    shard_shapes = {n: _BIG[n][0] for n in names}
    early, late = names[:1], names[1:]
    shards = {n: w[n].reshape(shard_shapes[n]).astype(BF16) for n in early}
    whole = lambda n, stacked: _BIG[n][1](lax.dynamic_update_slice(stacked, shards[n][None], (chip, 0, 0)))
    conv_mine = _pack_small({n: w[n] for n in _CONV_SHARD}, _CONV_SHARD)
    conv_all = _exchange_small(conv_mine, False, "gather_conv")
    shards, conv_all = lax.optimization_barrier((shards, conv_all))
    first = _gather_start([shards[n] for n in early], "gather_first_start")
    shards.update({n: (w[n].reshape(shard_shapes[n]) + first[4][0, 0]).astype(BF16) for n in late})
    send_sems, recv_sems, srcs, lands, token = _gather_start([shards[n] for n in late], "gather_rest_start")

    def late_weights(after):
        landed = _gather_forward(_gather_wait(send_sems, recv_sems, srcs, lands, after, "gather_rest_wait"),
                                 "gather_rest_forward")
        return {n: whole(n, ga) for n, ga in zip(late, landed)}

    landed = _gather_forward(_gather_wait(*first[:4], token, "gather_first_wait"), "gather_first_forward")
    full = {n: whole(n, ga) for n, ga in zip(early, landed)}
    conv_by_chip = [_unpack_small(conv_all[2 * s], _CONV_SHARD) for s in range(4)]
    for n in _CONV_SHARD:
        full[n] = jnp.concatenate([conv_by_chip[s][n] for s in range(4)], axis=1)
    for n in _SMALL:
        if n not in _CONV_SHARD:
            full[n] = w[n]

    early_rs = {}

    def early_grads_ready(grads):
        early_rs["swap"] = _rs_swap_start([_RS_EARLY[n](grads[n]) for n in _RS_EARLY], "rs_swap_early_start")
        return early_rs["swap"][4][0, 0]

    def early_grads_swapped(after):
        g4s, got = _rs_swap_wait(*early_rs["swap"][:4], after, "rs_swap_early_wait")
        c_idx = jnp.stack([lax.axis_index("c")]).astype(jnp.int32)
        early_rs["p4s"] = [_rs_add_sibling(g_, s_, c_idx, f"rs_add_sibling_{nm}") for g_, s_, nm in zip(g4s, got, _RS_EARLY)]
        early_rs["sems"] = _rs_scatter_start([pb for _, pb in early_rs["p4s"]], "rs_scatter_early_start")
        return early_rs["sems"][4][0, 0]

    late_rs = {}

    def last_grad_ready(grad):
        late_rs["p4s"] = _rs_front([_RS_LATE[n](grad) for n in _RS_LATE], list(_RS_LATE), "late")
        late_rs["sems"] = _rs_scatter_start([pb for _, pb in late_rs["p4s"]], "rs_scatter_late_start")
        return late_rs["sems"][4][0, 0]

    loss_part, dx, g = _local_step(x[0], p[:, 0], loss_target[0], full, late_weights, early_grads_ready,
                                   early_grads_swapped, last_grad_ready, token[0, 0])

    late_p4s, late_sems = late_rs["p4s"], late_rs["sems"]
    got3 = _rs_scatter_wait(*early_rs["sems"][:4], dx, "rs_scatter_early_wait")
    summed = dict(zip(_RS_EARLY, _rs_share_halves(_rs_back(early_rs["p4s"], got3, list(_RS_EARLY)), "rs_share_early")))
    g_big = {n: summed[n] for n in names if n in summed}
    g_big["w_pl"] = jnp.stack([summed["w_pl0"], summed["w_pl1"]])
    g_big["w_pl_gate"] = jnp.stack([summed["w_pl_gate0"], summed["w_pl_gate1"]])
    small_sum = _exchange_small(jnp.concatenate([loss_part, _pack_small(g, _SMALL)], axis=0), True, "reduce_small")
    loss = small_sum[0, 0]
    g_small = _unpack_small(small_sum, _SMALL, head_rows=SUBLANES)
    for n, (rows, cols) in _CONV_SHARD.items():
        g_small[n] = lax.dynamic_slice_in_dim(g_small[n], chip * cols, cols, axis=1)

    grads, delta, new_m, new_v = {}, {}, {}, {}
    for n in late:
        grads[n] = g_big[n].reshape(w[n].shape)
        delta[n], new_m[n], new_v[n] = _adamw(w[n], grads[n], m[n], v[n], f"adamw_{n}")
    own = {n: (_CONV_SHARD[n] if n in _CONV_SHARD else _SMALL[n]) for n in _SMALL}
    packs = [_pack_small({n: src[n] for n in _SMALL}, own)[None] for src in (w, g_small, m, v)]
    outs = [_unpack_small(t[0], own) for t in _adamw(*packs, "adamw_small")]
    for n in _SMALL:
        grads[n] = g_small[n].reshape(w[n].shape)
        delta[n], new_m[n], new_v[n] = (t[n].reshape(w[n].shape) for t in outs)
    got3 = _rs_scatter_wait(*late_sems[:4], new_v["w_in_odd"], "rs_scatter_late_wait")
    (g_in_even,) = _rs_share_halves(_rs_back(late_p4s, got3, list(_RS_LATE)), "rs_share_late")
    for n in early:
        t_ = lambda a: jnp.swapaxes(a, 1, 2)
        g_t = t_(g_in_even.reshape(w[n].shape))
        grads[n] = t_(g_t)
        delta[n], new_m[n], new_v[n] = (t_(o) for o in _adamw(t_(w[n]), g_t, t_(m[n]), t_(v[n]), f"adamw_{n}"))
    return (loss, dx[None], *[grads[n] for n in _WEIGHTS], *[delta[n] for n in _WEIGHTS],
            *[new_m[n] for n in _WEIGHTS], *[new_v[n] for n in _WEIGHTS])
```
